```python
import math
import jax, jax.numpy as jnp
from jax import lax
import numpy as np

D_MODEL = 1024
BATCH = 8
SEQ = 4096
DEPTH = 2

EPS = 1e-6
MIX_WIDTH = D_MODEL
A_WIDTH = MIX_WIDTH // 2
B_WIDTH = MIX_WIDTH - A_WIDTH
A_GROUPS = 4
A_GROUP_DIM = A_WIDTH // A_GROUPS
CHUNK = 128
B_GROUPS = 8
CONV_W = 3
EVEN_IN_WIDTH = 2 * A_WIDTH + 3 * B_WIDTH
N_HEADS = 16
N_KV_HEADS = 4
Q_PER_KV = N_HEADS // N_KV_HEADS
HEAD_DIM = D_MODEL // N_HEADS
QKV_WIDTH = (N_HEADS + 2 * N_KV_HEADS) * HEAD_DIM
WINDOW = 128
BLOCK = 128
N_BUCKETS = 32
MAX_DISTANCE = 128
D_FF = ((8 * D_MODEL // 3 + 255) // 256) * 256
N_EVEN = (DEPTH + 1) // 2
N_ODD = DEPTH // 2

kernel_name = "hybrid_gmlp_shortconv_swa_encoder"


def rmsnorm(x, g):
    xf = x.astype(jnp.float32)
    y = xf * lax.rsqrt(jnp.mean(xf * xf, axis=-1, keepdims=True) + EPS)
    return (y * g.astype(jnp.float32)).astype(x.dtype)


def layernorm(x, g, b):
    xf = x.astype(jnp.float32)
    mu = jnp.mean(xf, axis=-1, keepdims=True)
    xc = xf - mu
    y = xc * lax.rsqrt(jnp.mean(xc * xc, axis=-1, keepdims=True) + EPS)
    return (y * g.astype(jnp.float32) + b.astype(jnp.float32)).astype(x.dtype)


def t5_buckets(rel):
    nb = N_BUCKETS // 2
    ret = jnp.where(rel > 0, nb, 0)
    n = jnp.abs(rel)
    max_exact = nb // 2
    nf = jnp.maximum(n, 1).astype(jnp.float32)
    large = max_exact + (jnp.log(nf / max_exact) / math.log(MAX_DISTANCE / max_exact)
                         * (nb - max_exact)).astype(jnp.int32)
    large = jnp.minimum(large, nb - 1)
    return ret + jnp.where(n < max_exact, n, large)


def even_mixer(h, w_in, v_ln_g, v_ln_b, w_spatial, b_spatial, conv_w, w_out):
    bsz, s, _ = h.shape
    proj = jnp.einsum('bsd,de->bse', h, w_in)
    a_u, a_v, b_b, b_c, b_h = jnp.split(
        proj, [A_WIDTH, 2 * A_WIDTH, 2 * A_WIDTH + B_WIDTH, 2 * A_WIDTH + 2 * B_WIDTH], axis=-1)
    a_u = jax.nn.gelu(a_u, approximate=False)
    a_v = layernorm(jax.nn.gelu(a_v, approximate=False), v_ln_g, v_ln_b)
    n_chunks = s // CHUNK
    v = a_v.reshape(bsz, n_chunks, CHUNK, A_GROUPS, A_GROUP_DIM)
    mixed = jnp.einsum('gpq,bcqgd->bcpgd', w_spatial, v) + b_spatial.T[None, None, :, :, None]
    a_out = a_u * mixed.reshape(bsz, s, A_WIDTH)
    z = b_c * b_h
    pad = CONV_W // 2
    zp = jnp.pad(z, ((0, 0), (pad, pad), (0, 0)))
    conv = zp[:, 0:s] * conv_w[0]
    for tap in range(1, CONV_W):
        conv = conv + zp[:, tap:tap + s] * conv_w[tap]
    b_out = b_b * conv
    y = jnp.concatenate([a_out, b_out], axis=-1)
    return jnp.einsum('bse,ed->bsd', y, w_out)


def window_attention(h, w_qkv, sink, rel_bias, w_out):
    bsz, s, _ = h.shape
    n_blocks = s // BLOCK
    qkv = jnp.einsum('bsd,de->bse', h, w_qkv)
    q, k, v = jnp.split(qkv, [N_HEADS * HEAD_DIM, (N_HEADS + N_KV_HEADS) * HEAD_DIM], axis=-1)
    q = q.reshape(bsz, n_blocks, BLOCK, N_KV_HEADS, Q_PER_KV, HEAD_DIM).transpose(1, 0, 2, 3, 4, 5)
    k = k.reshape(bsz, s, N_KV_HEADS, HEAD_DIM)
    v = v.reshape(bsz, s, N_KV_HEADS, HEAD_DIM)
    kp = jnp.pad(k, ((0, 0), (BLOCK, BLOCK), (0, 0), (0, 0)))
    vp = jnp.pad(v, ((0, 0), (BLOCK, BLOCK), (0, 0), (0, 0)))
    qi = jnp.arange(BLOCK, dtype=jnp.int32)[:, None]
    kj = jnp.arange(3 * BLOCK, dtype=jnp.int32)[None, :]
    rel = kj - BLOCK - qi
    band = jnp.abs(rel) <= WINDOW
    bias = rel_bias[t5_buckets(rel)].astype(jnp.float32)
    bias = bias.transpose(2, 0, 1).reshape(N_KV_HEADS, Q_PER_KV, BLOCK, 3 * BLOCK)
    sink_l = sink.astype(jnp.float32).reshape(N_KV_HEADS, Q_PER_KV, 1, 1)
    scale = HEAD_DIM ** -0.5

    def attend_block(args):
        n, qb = args
        start = n * BLOCK
        kb = lax.dynamic_slice_in_dim(kp, start, 3 * BLOCK, axis=1)
        vb = lax.dynamic_slice_in_dim(vp, start, 3 * BLOCK, axis=1)
        key_pos = start - BLOCK + kj
        mask = band & (key_pos >= 0) & (key_pos < s)
        sc = jnp.einsum('bqhgd,bkhd->bhgqk', qb, kb).astype(jnp.float32) * scale + bias
        sc = jnp.where(mask, sc, -jnp.inf)
        m = jnp.maximum(jnp.max(sc, axis=-1, keepdims=True), sink_l)
        p = jnp.exp(sc - m)
        p = p / (jnp.sum(p, axis=-1, keepdims=True) + jnp.exp(sink_l - m))
        return jnp.einsum('bhgqk,bkhd->bqhgd', p.astype(vb.dtype), vb)

    out = lax.map(attend_block, (jnp.arange(n_blocks, dtype=jnp.int32), q))
    out = out.transpose(1, 0, 2, 3, 4, 5).reshape(bsz, s, N_HEADS * HEAD_DIM)
    return jnp.einsum('bse,ed->bsd', out, w_out)


def swiglu(h, w_gate, w_up, w_down):
    g = jnp.einsum('bsd,df->bsf', h, w_gate)
    u = jnp.einsum('bsd,df->bsf', h, w_up)
    return jnp.einsum('bsf,fd->bsd', jax.nn.silu(g) * u, w_down)


def _fwd_setup_inputs(seed: int = 0) -> dict:
    key = jax.random.key(seed)
    ks = jax.random.split(key, 20)
    f32 = jnp.float32
    nrm = lambda k, shape, scale: jax.random.normal(k, shape, f32) * scale
    return {
        "x": nrm(ks[0], (BATCH, SEQ, D_MODEL), 1.0),
        "norm_mix": 1.0 + nrm(ks[1], (DEPTH, D_MODEL), 0.02),
        "norm_ffn": 1.0 + nrm(ks[2], (DEPTH, D_MODEL), 0.02),
        "even_w_in": nrm(ks[3], (N_EVEN, D_MODEL, EVEN_IN_WIDTH), D_MODEL ** -0.5),
        "even_v_ln_g": 1.0 + nrm(ks[4], (N_EVEN, A_WIDTH), 0.02),
        "even_v_ln_b": nrm(ks[5], (N_EVEN, A_WIDTH), 0.02),
        "even_w_spatial": nrm(ks[6], (N_EVEN, A_GROUPS, CHUNK, CHUNK), CHUNK ** -0.5),
        "even_b_spatial": 1.0 + nrm(ks[7], (N_EVEN, A_GROUPS, CHUNK), 0.1),
        "even_conv_w": nrm(ks[8], (N_EVEN, CONV_W, B_WIDTH), CONV_W ** -0.5),
        "even_w_out": nrm(ks[9], (N_EVEN, MIX_WIDTH, D_MODEL), MIX_WIDTH ** -0.5),
        "attn_w_qkv": nrm(ks[10], (N_ODD, D_MODEL, QKV_WIDTH), D_MODEL ** -0.5),
        "attn_sink": nrm(ks[11], (N_ODD, N_HEADS), 0.5),
        "rel_bias": nrm(ks[12], (N_BUCKETS, N_HEADS), 0.5),
        "attn_w_out": nrm(ks[13], (N_ODD, N_HEADS * HEAD_DIM, D_MODEL), (N_HEADS * HEAD_DIM) ** -0.5),
        "ffn_w_gate": nrm(ks[14], (DEPTH, D_MODEL, D_FF), D_MODEL ** -0.5),
        "ffn_w_up": nrm(ks[15], (DEPTH, D_MODEL, D_FF), D_MODEL ** -0.5),
        "ffn_w_down": nrm(ks[16], (DEPTH, D_FF, D_MODEL), D_FF ** -0.5),
        "final_norm": 1.0 + nrm(ks[17], (D_MODEL,), 0.02),
    }


def _fwd_reference(x, norm_mix, norm_ffn, even_w_in, even_v_ln_g, even_v_ln_b, even_w_spatial,
              even_b_spatial, even_conv_w, even_w_out, attn_w_qkv, attn_sink, rel_bias,
              attn_w_out, ffn_w_gate, ffn_w_up, ffn_w_down, final_norm):
    for layer in range(DEPTH):
        i = layer // 2
        h = rmsnorm(x, norm_mix[layer])
        if layer % 2 == 0:
            x = x + even_mixer(h, even_w_in[i], even_v_ln_g[i], even_v_ln_b[i], even_w_spatial[i],
                               even_b_spatial[i], even_conv_w[i], even_w_out[i])
        else:
            x = x + window_attention(h, attn_w_qkv[i], attn_sink[i], rel_bias, attn_w_out[i])
        h = rmsnorm(x, norm_ffn[layer])
        x = x + swiglu(h, ffn_w_gate[layer], ffn_w_up[layer], ffn_w_down[layer])
    return rmsnorm(x, final_norm)


import jax as _jax
import jax.numpy as _jnp

TWIN_FORMAT = 'train_step'
FWD_PARAMS = ['x', 'norm_mix', 'norm_ffn', 'even_w_in', 'even_v_ln_g', 'even_v_ln_b', 'even_w_spatial', 'even_b_spatial', 'even_conv_w', 'even_w_out', 'attn_w_qkv', 'attn_sink', 'rel_bias', 'attn_w_out', 'ffn_w_gate', 'ffn_w_up', 'ffn_w_down', 'final_norm']
TWIN_WEIGHTS = ['norm_mix', 'norm_ffn', 'even_w_in', 'even_v_ln_g', 'even_v_ln_b', 'even_w_spatial', 'even_b_spatial', 'even_conv_w', 'even_w_out', 'attn_w_qkv', 'attn_sink', 'rel_bias', 'attn_w_out', 'ffn_w_gate', 'ffn_w_up', 'ffn_w_down', 'final_norm']
TWIN_DIFF_INPUT = 'x'
TWIN_INPUTS = ['x', 'norm_mix', 'norm_ffn', 'even_w_in', 'even_v_ln_g', 'even_v_ln_b', 'even_w_spatial', 'even_b_spatial', 'even_conv_w', 'even_w_out', 'attn_w_qkv', 'attn_sink', 'rel_bias', 'attn_w_out', 'ffn_w_gate', 'ffn_w_up', 'ffn_w_down', 'final_norm', 'loss_target', 'm_norm_mix', 'm_norm_ffn', 'm_even_w_in', 'm_even_v_ln_g', 'm_even_v_ln_b', 'm_even_w_spatial', 'm_even_b_spatial', 'm_even_conv_w', 'm_even_w_out', 'm_attn_w_qkv', 'm_attn_sink', 'm_rel_bias', 'm_attn_w_out', 'm_ffn_w_gate', 'm_ffn_w_up', 'm_ffn_w_down', 'm_final_norm', 'v_norm_mix', 'v_norm_ffn', 'v_even_w_in', 'v_even_v_ln_g', 'v_even_v_ln_b', 'v_even_w_spatial', 'v_even_b_spatial', 'v_even_conv_w', 'v_even_w_out', 'v_attn_w_qkv', 'v_attn_sink', 'v_rel_bias', 'v_attn_w_out', 'v_ffn_w_gate', 'v_ffn_w_up', 'v_ffn_w_down', 'v_final_norm']
TWIN_OUTPUTS = ['loss', 'grad_x', 'grad_norm_mix', 'grad_norm_ffn', 'grad_even_w_in', 'grad_even_v_ln_g', 'grad_even_v_ln_b', 'grad_even_w_spatial', 'grad_even_b_spatial', 'grad_even_conv_w', 'grad_even_w_out', 'grad_attn_w_qkv', 'grad_attn_sink', 'grad_rel_bias', 'grad_attn_w_out', 'grad_ffn_w_gate', 'grad_ffn_w_up', 'grad_ffn_w_down', 'grad_final_norm', 'delta_norm_mix', 'delta_norm_ffn', 'delta_even_w_in', 'delta_even_v_ln_g', 'delta_even_v_ln_b', 'delta_even_w_spatial', 'delta_even_b_spatial', 'delta_even_conv_w', 'delta_even_w_out', 'delta_attn_w_qkv', 'delta_attn_sink', 'delta_rel_bias', 'delta_attn_w_out', 'delta_ffn_w_gate', 'delta_ffn_w_up', 'delta_ffn_w_down', 'delta_final_norm', 'new_m_norm_mix', 'new_m_norm_ffn', 'new_m_even_w_in', 'new_m_even_v_ln_g', 'new_m_even_v_ln_b', 'new_m_even_w_spatial', 'new_m_even_b_spatial', 'new_m_even_conv_w', 'new_m_even_w_out', 'new_m_attn_w_qkv', 'new_m_attn_sink', 'new_m_rel_bias', 'new_m_attn_w_out', 'new_m_ffn_w_gate', 'new_m_ffn_w_up', 'new_m_ffn_w_down', 'new_m_final_norm', 'new_v_norm_mix', 'new_v_norm_ffn', 'new_v_even_w_in', 'new_v_even_v_ln_g', 'new_v_even_v_ln_b', 'new_v_even_w_spatial', 'new_v_even_b_spatial', 'new_v_even_conv_w', 'new_v_even_w_out', 'new_v_attn_w_qkv', 'new_v_attn_sink', 'new_v_rel_bias', 'new_v_attn_w_out', 'new_v_ffn_w_gate', 'new_v_ffn_w_up', 'new_v_ffn_w_down', 'new_v_final_norm']
TWIN_LEAF_KINDS = {'loss': 'loss', 'grad_x': 'grad_x', 'grad_norm_mix': 'grad_w', 'grad_norm_ffn': 'grad_w', 'grad_even_w_in': 'grad_w', 'grad_even_v_ln_g': 'grad_w', 'grad_even_v_ln_b': 'grad_w', 'grad_even_w_spatial': 'grad_w', 'grad_even_b_spatial': 'grad_w', 'grad_even_conv_w': 'grad_w', 'grad_even_w_out': 'grad_w', 'grad_attn_w_qkv': 'grad_w', 'grad_attn_sink': 'grad_w', 'grad_rel_bias': 'grad_w', 'grad_attn_w_out': 'grad_w', 'grad_ffn_w_gate': 'grad_w', 'grad_ffn_w_up': 'grad_w', 'grad_ffn_w_down': 'grad_w', 'grad_final_norm': 'grad_w', 'delta_norm_mix': 'delta_w', 'delta_norm_ffn': 'delta_w', 'delta_even_w_in': 'delta_w', 'delta_even_v_ln_g': 'delta_w', 'delta_even_v_ln_b': 'delta_w', 'delta_even_w_spatial': 'delta_w', 'delta_even_b_spatial': 'delta_w', 'delta_even_conv_w': 'delta_w', 'delta_even_w_out': 'delta_w', 'delta_attn_w_qkv': 'delta_w', 'delta_attn_sink': 'delta_w', 'delta_rel_bias': 'delta_w', 'delta_attn_w_out': 'delta_w', 'delta_ffn_w_gate': 'delta_w', 'delta_ffn_w_up': 'delta_w', 'delta_ffn_w_down': 'delta_w', 'delta_final_norm': 'delta_w', 'new_m_norm_mix': 'new_m', 'new_m_norm_ffn': 'new_m', 'new_m_even_w_in': 'new_m', 'new_m_even_v_ln_g': 'new_m', 'new_m_even_v_ln_b': 'new_m', 'new_m_even_w_spatial': 'new_m', 'new_m_even_b_spatial': 'new_m', 'new_m_even_conv_w': 'new_m', 'new_m_even_w_out': 'new_m', 'new_m_attn_w_qkv': 'new_m', 'new_m_attn_sink': 'new_m', 'new_m_rel_bias': 'new_m', 'new_m_attn_w_out': 'new_m', 'new_m_ffn_w_gate': 'new_m', 'new_m_ffn_w_up': 'new_m', 'new_m_ffn_w_down': 'new_m', 'new_m_final_norm': 'new_m', 'new_v_norm_mix': 'new_v', 'new_v_norm_ffn': 'new_v', 'new_v_even_w_in': 'new_v', 'new_v_even_v_ln_g': 'new_v', 'new_v_even_v_ln_b': 'new_v', 'new_v_even_w_spatial': 'new_v', 'new_v_even_b_spatial': 'new_v', 'new_v_even_conv_w': 'new_v', 'new_v_even_w_out': 'new_v', 'new_v_attn_w_qkv': 'new_v', 'new_v_attn_sink': 'new_v', 'new_v_rel_bias': 'new_v', 'new_v_attn_w_out': 'new_v', 'new_v_ffn_w_gate': 'new_v', 'new_v_ffn_w_up': 'new_v', 'new_v_ffn_w_down': 'new_v', 'new_v_final_norm': 'new_v'}


def _forward(args):
    return _fwd_reference(*[args[k] for k in FWD_PARAMS])


def _output_shape():
    out = _jax.eval_shape(lambda: _forward(_fwd_setup_inputs(0)))
    return out.shape, out.dtype

N_MICROBATCH = 1
ADAM_LR = 0.001
ADAM_B1 = 0.9
ADAM_B2 = 0.999
ADAM_EPS = 1e-08
ADAM_WD = 0.01
ADAM_STEP = 10
PER_EXAMPLE_BATCH_AXIS = {'x': 0, 'loss_target': 0}
SHARED_INPUTS = []
_WEIGHT_DTYPES = {'norm_mix': _jnp.float32, 'norm_ffn': _jnp.float32, 'even_w_in': _jnp.float32, 'even_v_ln_g': _jnp.float32, 'even_v_ln_b': _jnp.float32, 'even_w_spatial': _jnp.float32, 'even_b_spatial': _jnp.float32, 'even_conv_w': _jnp.float32, 'even_w_out': _jnp.float32, 'attn_w_qkv': _jnp.float32, 'attn_sink': _jnp.float32, 'rel_bias': _jnp.float32, 'attn_w_out': _jnp.float32, 'ffn_w_gate': _jnp.float32, 'ffn_w_up': _jnp.float32, 'ffn_w_down': _jnp.float32, 'final_norm': _jnp.float32}
MOMENT_SCALE = {'norm_mix': 1.549507e-01, 'norm_ffn': 1.026394e-01, 'even_w_in': 1.391296e-01, 'even_v_ln_g': 9.848919e-02, 'even_v_ln_b': 1.037882e-01, 'even_w_spatial': 9.798794e-02, 'even_b_spatial': 1.059488e-01, 'even_conv_w': 1.497972e-01, 'even_w_out': 1.427627e-01, 'attn_w_qkv': 2.704067e-02, 'attn_sink': 1.057572e-03, 'rel_bias': 1.901439e-02, 'attn_w_out': 2.398406e-02, 'ffn_w_gate': 4.410483e-02, 'ffn_w_up': 4.271265e-02, 'ffn_w_down': 7.099974e-02, 'final_norm': 3.205659e+01}


def _to_microbatches(a, axis):
    t = _jnp.moveaxis(a, axis, 0)
    t = t.reshape((N_MICROBATCH, t.shape[0] // N_MICROBATCH) + t.shape[1:])
    return _jnp.moveaxis(t, 1, axis + 1)


def setup_inputs(seed: int = 0) -> dict:
    inp = _fwd_setup_inputs(seed)
    key = _jax.random.fold_in(_jax.random.key(seed), 7919)
    shape, _ = _output_shape()
    out = dict(inp)
    out["loss_target"] = _jax.random.normal(_jax.random.fold_in(key, 0), shape, _jnp.float32)
    for i, name in enumerate(TWIN_WEIGHTS):
        w = inp[name].astype(_jnp.float32)
        if MOMENT_SCALE is None:
            s = _jnp.sqrt(_jnp.mean(_jnp.square(w)) + 1e-30)
        else:
            s = MOMENT_SCALE[name]
        km, kv = _jax.random.split(_jax.random.fold_in(key, i + 1))
        out[name] = w
        out["m_" + name] = s * _jax.random.normal(km, w.shape, _jnp.float32)
        out["v_" + name] = (s * s) * _jax.random.uniform(kv, w.shape, _jnp.float32, 0.5, 1.5)
    if N_MICROBATCH > 1:
        for name, axis in PER_EXAMPLE_BATCH_AXIS.items():
            out[name] = _to_microbatches(out[name], axis)
    return {'x': out['x'], 'norm_mix': out['norm_mix'], 'norm_ffn': out['norm_ffn'], 'even_w_in': out['even_w_in'], 'even_v_ln_g': out['even_v_ln_g'], 'even_v_ln_b': out['even_v_ln_b'], 'even_w_spatial': out['even_w_spatial'], 'even_b_spatial': out['even_b_spatial'], 'even_conv_w': out['even_conv_w'], 'even_w_out': out['even_w_out'], 'attn_w_qkv': out['attn_w_qkv'], 'attn_sink': out['attn_sink'], 'rel_bias': out['rel_bias'], 'attn_w_out': out['attn_w_out'], 'ffn_w_gate': out['ffn_w_gate'], 'ffn_w_up': out['ffn_w_up'], 'ffn_w_down': out['ffn_w_down'], 'final_norm': out['final_norm'], 'loss_target': out['loss_target'], 'm_norm_mix': out['m_norm_mix'], 'm_norm_ffn': out['m_norm_ffn'], 'm_even_w_in': out['m_even_w_in'], 'm_even_v_ln_g': out['m_even_v_ln_g'], 'm_even_v_ln_b': out['m_even_v_ln_b'], 'm_even_w_spatial': out['m_even_w_spatial'], 'm_even_b_spatial': out['m_even_b_spatial'], 'm_even_conv_w': out['m_even_conv_w'], 'm_even_w_out': out['m_even_w_out'], 'm_attn_w_qkv': out['m_attn_w_qkv'], 'm_attn_sink': out['m_attn_sink'], 'm_rel_bias': out['m_rel_bias'], 'm_attn_w_out': out['m_attn_w_out'], 'm_ffn_w_gate': out['m_ffn_w_gate'], 'm_ffn_w_up': out['m_ffn_w_up'], 'm_ffn_w_down': out['m_ffn_w_down'], 'm_final_norm': out['m_final_norm'], 'v_norm_mix': out['v_norm_mix'], 'v_norm_ffn': out['v_norm_ffn'], 'v_even_w_in': out['v_even_w_in'], 'v_even_v_ln_g': out['v_even_v_ln_g'], 'v_even_v_ln_b': out['v_even_v_ln_b'], 'v_even_w_spatial': out['v_even_w_spatial'], 'v_even_b_spatial': out['v_even_b_spatial'], 'v_even_conv_w': out['v_even_conv_w'], 'v_even_w_out': out['v_even_w_out'], 'v_attn_w_qkv': out['v_attn_w_qkv'], 'v_attn_sink': out['v_attn_sink'], 'v_rel_bias': out['v_rel_bias'], 'v_attn_w_out': out['v_attn_w_out'], 'v_ffn_w_gate': out['v_ffn_w_gate'], 'v_ffn_w_up': out['v_ffn_w_up'], 'v_ffn_w_down': out['v_ffn_w_down'], 'v_final_norm': out['v_final_norm']}


def _loss(weights, diff, rest, loss_target):
    with _jax.named_scope("forward"):
        args = {**rest, TWIN_DIFF_INPUT: diff, **{k: w.astype(_WEIGHT_DTYPES[k]) for k, w in weights.items()}}
        y = _forward(args)
    with _jax.named_scope("loss_head"):
        err = _jnp.square(y.astype(_jnp.float32) - loss_target)
        return 0.5 * _jnp.sum(_jnp.mean(err, axis=-1)) if err.ndim else 0.5 * err


def _adamw(w, g, m, v):
    m = ADAM_B1 * m + (1.0 - ADAM_B1) * g
    v = ADAM_B2 * v + (1.0 - ADAM_B2) * _jnp.square(g)
    m_hat = m / (1.0 - ADAM_B1 ** ADAM_STEP)
    v_hat = v / (1.0 - ADAM_B2 ** ADAM_STEP)
    delta = -ADAM_LR * (m_hat / (_jnp.sqrt(v_hat) + ADAM_EPS) + ADAM_WD * w)
    return delta, m, v


def reference(x, norm_mix, norm_ffn, even_w_in, even_v_ln_g, even_v_ln_b, even_w_spatial, even_b_spatial, even_conv_w, even_w_out, attn_w_qkv, attn_sink, rel_bias, attn_w_out, ffn_w_gate, ffn_w_up, ffn_w_down, final_norm, loss_target, m_norm_mix, m_norm_ffn, m_even_w_in, m_even_v_ln_g, m_even_v_ln_b, m_even_w_spatial, m_even_b_spatial, m_even_conv_w, m_even_w_out, m_attn_w_qkv, m_attn_sink, m_rel_bias, m_attn_w_out, m_ffn_w_gate, m_ffn_w_up, m_ffn_w_down, m_final_norm, v_norm_mix, v_norm_ffn, v_even_w_in, v_even_v_ln_g, v_even_v_ln_b, v_even_w_spatial, v_even_b_spatial, v_even_conv_w, v_even_w_out, v_attn_w_qkv, v_attn_sink, v_rel_bias, v_attn_w_out, v_ffn_w_gate, v_ffn_w_up, v_ffn_w_down, v_final_norm):
    given = dict(x=x, norm_mix=norm_mix, norm_ffn=norm_ffn, even_w_in=even_w_in, even_v_ln_g=even_v_ln_g, even_v_ln_b=even_v_ln_b, even_w_spatial=even_w_spatial, even_b_spatial=even_b_spatial, even_conv_w=even_conv_w, even_w_out=even_w_out, attn_w_qkv=attn_w_qkv, attn_sink=attn_sink, rel_bias=rel_bias, attn_w_out=attn_w_out, ffn_w_gate=ffn_w_gate, ffn_w_up=ffn_w_up, ffn_w_down=ffn_w_down, final_norm=final_norm, loss_target=loss_target, m_norm_mix=m_norm_mix, m_norm_ffn=m_norm_ffn, m_even_w_in=m_even_w_in, m_even_v_ln_g=m_even_v_ln_g, m_even_v_ln_b=m_even_v_ln_b, m_even_w_spatial=m_even_w_spatial, m_even_b_spatial=m_even_b_spatial, m_even_conv_w=m_even_conv_w, m_even_w_out=m_even_w_out, m_attn_w_qkv=m_attn_w_qkv, m_attn_sink=m_attn_sink, m_rel_bias=m_rel_bias, m_attn_w_out=m_attn_w_out, m_ffn_w_gate=m_ffn_w_gate, m_ffn_w_up=m_ffn_w_up, m_ffn_w_down=m_ffn_w_down, m_final_norm=m_final_norm, v_norm_mix=v_norm_mix, v_norm_ffn=v_norm_ffn, v_even_w_in=v_even_w_in, v_even_v_ln_g=v_even_v_ln_g, v_even_v_ln_b=v_even_v_ln_b, v_even_w_spatial=v_even_w_spatial, v_even_b_spatial=v_even_b_spatial, v_even_conv_w=v_even_conv_w, v_even_w_out=v_even_w_out, v_attn_w_qkv=v_attn_w_qkv, v_attn_sink=v_attn_sink, v_rel_bias=v_rel_bias, v_attn_w_out=v_attn_w_out, v_ffn_w_gate=v_ffn_w_gate, v_ffn_w_up=v_ffn_w_up, v_ffn_w_down=v_ffn_w_down, v_final_norm=v_final_norm)
    weights = {n: given[n] for n in TWIN_WEIGHTS}
    shared = {n: given[n] for n in SHARED_INPUTS}
    per_example = {n: given[n] for n in ['x']}
    grad_fn = _jax.value_and_grad(_loss, argnums=(0, 1))

    def one_microbatch(ex, loss_target):
        ex = dict(ex)
        diff = ex.pop(TWIN_DIFF_INPUT)
        return grad_fn(weights, diff, {**shared, **ex}, loss_target)

    if N_MICROBATCH == 1:
        loss, (grad_w, grad_x) = one_microbatch(per_example, given["loss_target"])
    else:
        def body(carry, xs):
            loss_sum, grad_sum = carry
            l_k, (gw_k, gx_k) = one_microbatch(xs[0], xs[1])
            with _jax.named_scope("update"):
                return (loss_sum + l_k, _jax.tree.map(_jnp.add, grad_sum, gw_k)), gx_k

        init = (_jnp.zeros((), _jnp.float32), _jax.tree.map(_jnp.zeros_like, weights))
        (loss, grad_w), grad_x = _jax.lax.scan(body, init, (per_example, given["loss_target"]))
    with _jax.named_scope("update"):
        delta_w, new_m, new_v = {}, {}, {}
        for n in TWIN_WEIGHTS:
            delta_w[n], new_m[n], new_v[n] = _adamw(weights[n], grad_w[n], given["m_" + n], given["v_" + n])
    return (loss, grad_x, *[grad_w[n] for n in TWIN_WEIGHTS], *[delta_w[n] for n in TWIN_WEIGHTS],
            *[new_m[n] for n in TWIN_WEIGHTS], *[new_v[n] for n in TWIN_WEIGHTS])
```

```python
import math

import jax
import jax.numpy as jnp
import numpy as np
from jax import lax
from jax.experimental import pallas as pl
from jax.experimental.pallas import tpu as pltpu

F32, BF16 = jnp.float32, jnp.bfloat16
NDEV = 8
EPS = 1e-6
CHUNK = 128
A_GROUPS = 4
N_HEADS, N_KV, HEAD_DIM = 16, 4, 64
N_BUCKETS, MAX_DISTANCE = 32, 128
NEG = -1e30
ADAM_LR, ADAM_B1, ADAM_B2, ADAM_EPS, ADAM_WD, ADAM_STEP = 0.001, 0.9, 0.999, 1e-08, 0.01, 10
VMEM_LIMIT = 56 * 1024 * 1024
MESH = pl.DeviceIdType.MESH
NT = (((1,), (1,)), ((), ()))
NN = (((1,), (0,)), ((), ()))
TN = (((0,), (0,)), ((), ()))
ANY = pl.BlockSpec(memory_space=pl.ANY)


def _cp(n_grid=1):
    return pltpu.CompilerParams(dimension_semantics=("arbitrary",) * n_grid, vmem_limit_bytes=VMEM_LIMIT)


def _dot(a, b, dims):
    return lax.dot_general(a, b, dims, preferred_element_type=F32)


def _my_index():
    return 4 * lax.axis_index("x") + 2 * lax.axis_index("y") + lax.axis_index("c")


def _peer(k):
    x, y, c = lax.axis_index("x"), lax.axis_index("y"), lax.axis_index("c")
    px = 1 - x if k & 4 else x
    py = 1 - y if k & 2 else y
    pc = 1 - c if k & 1 else c
    return (px, py, pc)


class _Layout:
    def __init__(self, d_model, in_w, qkv_w, d_ff):
        names = ["winT", "wqkvT", "wgT0", "wuT0", "wgT1", "wuT1", "woe", "woa", "wd0", "wd1"]
        rows = [in_w // NDEV, qkv_w // NDEV, d_ff // NDEV, d_ff // NDEV, d_ff // NDEV, d_ff // NDEV,
                d_model // NDEV, d_model // NDEV, d_ff // NDEV, d_ff // NDEV]
        self.names = names
        self.rows = dict(zip(names, rows))
        self.off = {}
        o = 0
        for n, r in zip(names, rows):
            assert r % 16 == 0, (n, r)
            self.off[n] = o
            o += r
        self.total = o


def _load_weight(gath_ref, wbuf, sems, off, rows):
    cps = [pltpu.make_async_copy(gath_ref.at[d, pl.ds(off, rows), :], wbuf.at[pl.ds(d * rows, rows), :], sems.at[d])
           for d in range(NDEV)]
    for c in cps:
        c.start()
    for c in cps:
        c.wait()


def _all_gather_weights(shard):
    rs, dm = shard.shape

    def body(x_ref, out_ref, send_sems, recv_sems, local_sem):
        x, y, c = lax.axis_index("x"), lax.axis_index("y"), lax.axis_index("c")
        me, sibling = (x, y, c), (x, y, 1 - c)
        chips = [(1 - x, y), (x, 1 - y), (1 - x, 1 - y)]

        def slot(px, py, pc):
            return out_ref.at[4 * px + 2 * py + pc]

        def copy(k, block, to, src=None):
            return pltpu.make_async_remote_copy(
                src_ref=slot(*block) if src is None else src, dst_ref=slot(*block),
                send_sem=send_sems.at[k], recv_sem=recv_sems.at[k], device_id=to, device_id_type=MESH)

        mine = pltpu.make_async_copy(x_ref, slot(*me), local_sem)
        mine.start()
        first = [copy(0, me, sibling, src=x_ref)]
        first += [copy(1 + j, me, (*chip, c), src=x_ref) for j, chip in enumerate(chips)]
        for cp in first:
            cp.start()
        passed = [copy(4 + j, (*chip, c), sibling) for j, chip in enumerate(chips)]
        for j, chip in enumerate(chips):
            copy(1 + j, (*chip, c), me).wait_recv()
            passed[j].start()
        copy(0, sibling, me).wait_recv()
        for j, chip in enumerate(chips):
            copy(4 + j, (*chip, 1 - c), me).wait_recv()
        for cp in first + passed:
            cp.wait_send()
        mine.wait()

    return pl.pallas_call(
        body, out_shape=jax.ShapeDtypeStruct((NDEV, rs, dm), shard.dtype),
        in_specs=[ANY], out_specs=ANY,
        scratch_shapes=[pltpu.SemaphoreType.DMA((7,)), pltpu.SemaphoreType.DMA((7,)), pltpu.SemaphoreType.DMA(())],
        name="ag_weights")(shard)


def _exchange_grads(pieces, rows, offs, total):
    n = len(pieces)
    dm = pieces[0].shape[1]

    def body(*refs):
        g_refs, recv = refs[:n], refs[n]
        send_sems, recv_sems, local_sem = refs[n + 1:]
        me = _my_index()
        for w in range(n):
            pltpu.make_async_copy(g_refs[w].at[pl.ds(pl.multiple_of(me * rows[w], 16), rows[w]), :],
                                  recv.at[me, pl.ds(offs[w], rows[w]), :], local_sem).start()
        for k in range(1, NDEV):
            peer = _peer(k)
            pidx = 4 * peer[0] + 2 * peer[1] + peer[2]
            for w in range(n):
                pltpu.make_async_remote_copy(
                    src_ref=g_refs[w].at[pl.ds(pl.multiple_of(pidx * rows[w], 16), rows[w]), :],
                    dst_ref=recv.at[me, pl.ds(offs[w], rows[w]), :],
                    send_sem=send_sems.at[k - 1], recv_sem=recv_sems.at[k - 1],
                    device_id=peer, device_id_type=MESH).start()
        for k in range(1, NDEV):
            pltpu.make_async_remote_copy(
                src_ref=recv.at[0], dst_ref=recv.at[0], send_sem=send_sems.at[k - 1], recv_sem=recv_sems.at[k - 1],
                device_id=_peer(k), device_id_type=MESH).wait()
        pltpu.make_async_copy(recv.at[0], recv.at[0], local_sem).wait()

    return pl.pallas_call(
        body, out_shape=jax.ShapeDtypeStruct((NDEV, total, dm), BF16),
        in_specs=[ANY] * n, out_specs=ANY,
        scratch_shapes=[pltpu.SemaphoreType.DMA((7,)), pltpu.SemaphoreType.DMA((7,)), pltpu.SemaphoreType.DMA(())],
        name="exchange_grads")(*pieces)


def _sum_slots(recv):
    _, total, dm = recv.shape
    tr = 320 if total % 320 == 0 else total

    def body(r_ref, o_ref):
        acc = r_ref[0].astype(F32)
        for d in range(1, NDEV):
            acc = acc + r_ref[d].astype(F32)
        o_ref[...] = acc

    return pl.pallas_call(
        body, grid=(total // tr,), out_shape=jax.ShapeDtypeStruct((total, dm), F32),
        in_specs=[pl.BlockSpec((NDEV, tr, dm), lambda i: (0, i, 0))],
        out_specs=pl.BlockSpec((tr, dm), lambda i: (i, 0)),
        compiler_params=_cp(), name="sum_slots")(recv)


def _all_reduce_small(part, name):
    r, lanes = part.shape

    def body(p_ref, o_ref, buf, send_sems, recv_sems):
        me = _my_index()
        buf[me] = p_ref[...]
        cps = []
        for k in range(1, NDEV):
            cp = pltpu.make_async_remote_copy(
                src_ref=p_ref, dst_ref=buf.at[me], send_sem=send_sems.at[k - 1], recv_sem=recv_sems.at[k - 1],
                device_id=_peer(k), device_id_type=MESH)
            cp.start()
            cps.append(cp)
        for cp in cps:
            cp.wait()
        acc = buf[0]
        for d in range(1, NDEV):
            acc = acc + buf[d]
        o_ref[...] = acc

    return pl.pallas_call(
        body, out_shape=jax.ShapeDtypeStruct((r, lanes), F32),
        in_specs=[pl.BlockSpec(memory_space=pltpu.VMEM)], out_specs=pl.BlockSpec(memory_space=pltpu.VMEM),
        scratch_shapes=[pltpu.VMEM((NDEV, r, lanes), F32), pltpu.SemaphoreType.DMA((7,)), pltpu.SemaphoreType.DMA((7,))],
        name=name)(part)


def _rms_fwd(x, gain):
    r = lax.rsqrt(jnp.mean(x * x, axis=-1, keepdims=True) + EPS)
    return x * r * gain, r


def _rms_bwd(dh, x, r, gain):
    a = dh * gain
    dx = r * a - x * (r * r * r) * jnp.mean(a * x, axis=-1, keepdims=True)
    dgain = jnp.sum(dh * (x * r), axis=0, keepdims=True)
    return dx, dgain


def _gelu(x):
    return 0.5 * x * (1.0 + lax.erf(x * 0.7071067811865476))


def _gelu_grad(x):
    return 0.5 * (1.0 + lax.erf(x * 0.7071067811865476)) + x * jnp.exp(-0.5 * x * x) * 0.3989422804014327


def _sigmoid(x):
    return 1.0 / (1.0 + jnp.exp(-x))


def _norm_proj(x, gain, gath, off, rows, out_dtype, name, tm):
    t, dm = x.shape
    n = rows * NDEV

    def body(x_ref, g_ref, gath_ref, proj_ref, hb_ref, wbuf, sems):
        @pl.when(pl.program_id(0) == 0)
        def _():
            _load_weight(gath_ref, wbuf, sems, off, rows)
        h, _ = _rms_fwd(x_ref[...], g_ref[...])
        hb = h.astype(BF16)
        hb_ref[...] = hb
        proj_ref[...] = _dot(hb, wbuf[...], NT).astype(out_dtype)

    return pl.pallas_call(
        body, grid=(t // tm,),
        out_shape=(jax.ShapeDtypeStruct((t, n), out_dtype), jax.ShapeDtypeStruct((t, dm), BF16)),
        in_specs=[pl.BlockSpec((tm, dm), lambda i: (i, 0)), pl.BlockSpec((1, dm), lambda i: (0, 0)), ANY],
        out_specs=(pl.BlockSpec((tm, n), lambda i: (i, 0)), pl.BlockSpec((tm, dm), lambda i: (i, 0))),
        scratch_shapes=[pltpu.VMEM((n, dm), BF16), pltpu.SemaphoreType.DMA((NDEV,))],
        compiler_params=_cp(), name=name)(x, gain, gath)


def _proj_bwd_norm(dy, x, gain, dres, gath, off, rows, name, tm):
    t, dm = x.shape
    n = rows * NDEV

    def body(dy_ref, x_ref, g_ref, dres_ref, gath_ref, dx_ref, dxb_ref, dgain_ref, wbuf, sems):
        @pl.when(pl.program_id(0) == 0)
        def _():
            _load_weight(gath_ref, wbuf, sems, off, rows)
            dgain_ref[...] = jnp.zeros_like(dgain_ref)
        xv, gain_v = x_ref[...], g_ref[...]
        _, r = _rms_fwd(xv, gain_v)
        dh = _dot(dy_ref[...], wbuf[...], NN)
        dx, dgain = _rms_bwd(dh, xv, r, gain_v)
        dx = dres_ref[...] + dx
        dx_ref[...] = dx
        dxb_ref[...] = dx.astype(BF16)
        dgain_ref[...] += dgain

    return pl.pallas_call(
        body, grid=(t // tm,),
        out_shape=(jax.ShapeDtypeStruct((t, dm), F32), jax.ShapeDtypeStruct((t, dm), BF16),
                   jax.ShapeDtypeStruct((1, dm), F32)),
        in_specs=[pl.BlockSpec((tm, n), lambda i: (i, 0)), pl.BlockSpec((tm, dm), lambda i: (i, 0)),
                  pl.BlockSpec((1, dm), lambda i: (0, 0)), pl.BlockSpec((tm, dm), lambda i: (i, 0)), ANY],
        out_specs=(pl.BlockSpec((tm, dm), lambda i: (i, 0)), pl.BlockSpec((tm, dm), lambda i: (i, 0)),
                   pl.BlockSpec((1, dm), lambda i: (0, 0))),
        scratch_shapes=[pltpu.VMEM((n, dm), BF16), pltpu.SemaphoreType.DMA((NDEV,))],
        compiler_params=_cp(), name=name)(dy, x, gain, dres, gath)


def _wgrad(a, b, name, tmm=256):
    t, m = a.shape
    n = b.shape[1]

    def body(a_ref, b_ref, o_ref):
        o_ref[...] = _dot(a_ref[...], b_ref[...], TN).astype(BF16)

    return pl.pallas_call(
        body, grid=(m // tmm,), out_shape=jax.ShapeDtypeStruct((m, n), BF16),
        in_specs=[pl.BlockSpec((t, tmm), lambda j: (0, j)), pl.BlockSpec((t, n), lambda j: (0, 0))],
        out_specs=pl.BlockSpec((tmm, n), lambda j: (j, 0)),
        compiler_params=_cp(), name=name)(a, b)


def _halo_specs(tm, t, width, col_blocks):
    nb8 = tm // 8
    last = t // 8 - 1
    prev = [pl.BlockSpec((8, width), lambda i, cb=cb: (jnp.maximum(i * nb8 - 1, 0), cb)) for cb in col_blocks]
    nxt = [pl.BlockSpec((8, width), lambda i, cb=cb: (jnp.minimum((i + 1) * nb8, last), cb)) for cb in col_blocks]
    return prev, nxt


def _shift_rows(z, prev_row, next_row):
    tm = z.shape[0]
    row = lax.broadcasted_iota(jnp.int32, z.shape, 0)
    zm1 = jnp.where(row == 0, prev_row, pltpu.roll(z, 1, 0))
    zp1 = jnp.where(row == tm - 1, next_row, pltpu.roll(z, tm - 1, 0))
    return zm1, zp1


def _gating_fwd(proj, lng, lnb, wsp_ref, bsp_ref, aw):
    tm = proj.shape[0]
    a_u = _gelu(proj[:, 0:aw])
    gv = _gelu(proj[:, aw:2 * aw])
    mu = jnp.mean(gv, axis=-1, keepdims=True)
    xc = gv - mu
    rstd = lax.rsqrt(jnp.mean(xc * xc, axis=-1, keepdims=True) + EPS)
    vn = xc * rstd
    a_v = (vn * lng + lnb).astype(BF16)
    gd = aw // A_GROUPS
    rows = []
    for c in range(tm // CHUNK):
        cols = []
        for g in range(A_GROUPS):
            blk = a_v[c * CHUNK:(c + 1) * CHUNK, g * gd:(g + 1) * gd]
            cols.append(_dot(wsp_ref[g], blk, NN) + bsp_ref[g])
        rows.append(jnp.concatenate(cols, axis=1))
    mixed = jnp.concatenate(rows, axis=0)
    return a_u, vn, rstd, a_v, mixed


def _even_core_fwd(proj, x0, lng, lnb, wsp, bspb, cw, gath, off_woe, rows_woe, tm):
    t, dm = x0.shape
    aw = lng.shape[1]
    bw = cw.shape[1]
    assert aw == bw and 2 * aw + 3 * bw == proj.shape[1]
    nt = t // tm
    prev, nxt = _halo_specs(tm, t, bw, [3, 4])

    def body(proj_ref, cp_ref, hp_ref, cn_ref, hn_ref, x0_ref, lng_ref, lnb_ref, wsp_ref, bsp_ref, cw_ref, gath_ref,
             x1_ref, y_ref, wbuf, sems):
        i = pl.program_id(0)

        @pl.when(i == 0)
        def _():
            _load_weight(gath_ref, wbuf, sems, off_woe, rows_woe)
        proj_v = proj_ref[...]
        a_u, _, _, _, mixed = _gating_fwd(proj_v, lng_ref[...], lnb_ref[...], wsp_ref, bsp_ref, aw)
        a_out = a_u * mixed
        bb = proj_v[:, 2 * aw:2 * aw + bw]
        z = proj_v[:, 2 * aw + bw:2 * aw + 2 * bw] * proj_v[:, 2 * aw + 2 * bw:]
        zprev = jnp.where(i > 0, cp_ref[7:8, :] * hp_ref[7:8, :], 0.0)
        znext = jnp.where(i < nt - 1, cn_ref[0:1, :] * hn_ref[0:1, :], 0.0)
        zm1, zp1 = _shift_rows(z, zprev, znext)
        cwv = cw_ref[...]
        conv = zm1 * cwv[0:1, :] + z * cwv[1:2, :] + zp1 * cwv[2:3, :]
        y = jnp.concatenate([a_out, bb * conv], axis=1).astype(BF16)
        y_ref[...] = y
        x1_ref[...] = x0_ref[...] + _dot(y, wbuf[...], NN)

    full = lambda shape: pl.BlockSpec(shape, lambda i: (0,) * len(shape))
    return pl.pallas_call(
        body, grid=(nt,),
        out_shape=(jax.ShapeDtypeStruct((t, dm), F32), jax.ShapeDtypeStruct((t, aw + bw), BF16)),
        in_specs=[pl.BlockSpec((tm, proj.shape[1]), lambda i: (i, 0)), prev[0], prev[1], nxt[0], nxt[1],
                  pl.BlockSpec((tm, dm), lambda i: (i, 0)), full(lng.shape), full(lnb.shape), full(wsp.shape),
                  full(bspb.shape), full(cw.shape), ANY],
        out_specs=(pl.BlockSpec((tm, dm), lambda i: (i, 0)), pl.BlockSpec((tm, aw + bw), lambda i: (i, 0))),
        scratch_shapes=[pltpu.VMEM((rows_woe * NDEV, dm), BF16), pltpu.SemaphoreType.DMA((NDEV,))],
        compiler_params=_cp(), name="even_core_fwd")(proj, proj, proj, proj, proj, x0, lng, lnb, wsp, bspb, cw, gath)


def _even_core_bwd(proj, dx1, lng, lnb, wsp, bspb, cw, gath, off_woe, rows_woe, tm):
    t, dm = dx1.shape
    aw, bw = lng.shape[1], cw.shape[1]
    gd = aw // A_GROUPS
    nt = t // tm
    inw = proj.shape[1]
    prev, nxt = _halo_specs(tm, t, bw, [2, 3, 4])
    nb8 = tm // 8
    last8 = t // 8 - 1

    def body(proj_ref, bp_ref, cp_ref, hp_ref, bn_ref, cn_ref, hn_ref, dx_ref, dxp_ref, dxn_ref,
             lng_ref, lnb_ref, wsp_ref, bsp_ref, cw_ref, gath_ref,
             dproj_ref, dlng_ref, dlnb_ref, dwsp_ref, dbsp_ref, dcw_ref, wbuf, sems):
        i = pl.program_id(0)

        @pl.when(i == 0)
        def _():
            _load_weight(gath_ref, wbuf, sems, off_woe, rows_woe)
            dlng_ref[...] = jnp.zeros_like(dlng_ref)
            dlnb_ref[...] = jnp.zeros_like(dlnb_ref)
            dwsp_ref[...] = jnp.zeros_like(dwsp_ref)
            dbsp_ref[...] = jnp.zeros_like(dbsp_ref)
            dcw_ref[...] = jnp.zeros_like(dcw_ref)
        proj_v = proj_ref[...]
        lng_v = lng_ref[...]
        a_u, vn, rstd, a_v, mixed = _gating_fwd(proj_v, lng_v, lnb_ref[...], wsp_ref, bsp_ref, aw)
        w = wbuf[...]
        dy = _dot(dx_ref[...].astype(BF16), w, NT)
        da_out, db_out = dy[:, 0:aw], dy[:, aw:]
        da_u = da_out * mixed
        dmixed = da_out * a_u
        dmb = dmixed.astype(BF16)
        rows = []
        for c in range(tm // CHUNK):
            cols = []
            for g in range(A_GROUPS):
                r0, c0 = c * CHUNK, g * gd
                dm_cg = dmb[r0:r0 + CHUNK, c0:c0 + gd]
                cols.append(_dot(wsp_ref[g], dm_cg, TN))
                dwsp_ref[g] += _dot(dm_cg, a_v[r0:r0 + CHUNK, c0:c0 + gd], NT)
                dbsp_ref[g] += dmixed[r0:r0 + CHUNK, c0:c0 + gd]
            rows.append(jnp.concatenate(cols, axis=1))
        dav = jnp.concatenate(rows, axis=0)
        dlng_ref[...] += jnp.sum(dav * vn, axis=0, keepdims=True)
        dlnb_ref[...] += jnp.sum(dav, axis=0, keepdims=True)
        dvn = dav * lng_v
        dgv = rstd * (dvn - jnp.mean(dvn, axis=-1, keepdims=True) - vn * jnp.mean(dvn * vn, axis=-1, keepdims=True))
        dv_pre = dgv * _gelu_grad(proj_v[:, aw:2 * aw])
        du_pre = da_u * _gelu_grad(proj_v[:, 0:aw])
        bb = proj_v[:, 2 * aw:2 * aw + bw]
        bc = proj_v[:, 2 * aw + bw:2 * aw + 2 * bw]
        bh = proj_v[:, 2 * aw + 2 * bw:]
        z = bc * bh
        zprev = jnp.where(i > 0, cp_ref[7:8, :] * hp_ref[7:8, :], 0.0)
        znext = jnp.where(i < nt - 1, cn_ref[0:1, :] * hn_ref[0:1, :], 0.0)
        zm1, zp1 = _shift_rows(z, zprev, znext)
        cwv = cw_ref[...]
        conv = zm1 * cwv[0:1, :] + z * cwv[1:2, :] + zp1 * cwv[2:3, :]
        dbb = db_out * conv
        dconv = db_out * bb
        dx_edge = jnp.concatenate([dxp_ref[...], dxn_ref[...]], axis=0).astype(BF16)
        dy_edge = _dot(dx_edge, w[aw:, :], NT)
        dcprev = jnp.where(i > 0, dy_edge[7:8, :] * bp_ref[7:8, :], 0.0)
        dcnext = jnp.where(i < nt - 1, dy_edge[8:9, :] * bn_ref[0:1, :], 0.0)
        dcm1, dcp1 = _shift_rows(dconv, dcprev, dcnext)
        dz = dcp1 * cwv[0:1, :] + dconv * cwv[1:2, :] + dcm1 * cwv[2:3, :]
        dcw_ref[0:1, :] += jnp.sum(dconv * zm1, axis=0, keepdims=True)
        dcw_ref[1:2, :] += jnp.sum(dconv * z, axis=0, keepdims=True)
        dcw_ref[2:3, :] += jnp.sum(dconv * zp1, axis=0, keepdims=True)
        dproj_ref[...] = jnp.concatenate([du_pre, dv_pre, dbb, dz * bh, dz * bc], axis=1).astype(BF16)

    full = lambda shape: pl.BlockSpec(shape, lambda i: (0,) * len(shape))
    row8 = lambda f: pl.BlockSpec((8, dm), f)
    return pl.pallas_call(
        body, grid=(nt,),
        out_shape=(jax.ShapeDtypeStruct((t, inw), BF16), jax.ShapeDtypeStruct((1, aw), F32),
                   jax.ShapeDtypeStruct((1, aw), F32), jax.ShapeDtypeStruct(wsp.shape, F32),
                   jax.ShapeDtypeStruct((A_GROUPS, CHUNK, gd), F32), jax.ShapeDtypeStruct(cw.shape, F32)),
        in_specs=[pl.BlockSpec((tm, inw), lambda i: (i, 0)), prev[0], prev[1], prev[2], nxt[0], nxt[1], nxt[2],
                  pl.BlockSpec((tm, dm), lambda i: (i, 0)),
                  row8(lambda i: (jnp.maximum(i * nb8 - 1, 0), 0)), row8(lambda i: (jnp.minimum((i + 1) * nb8, last8), 0)),
                  full(lng.shape), full(lnb.shape), full(wsp.shape), full(bspb.shape), full(cw.shape), ANY],
        out_specs=(pl.BlockSpec((tm, inw), lambda i: (i, 0)), full((1, aw)), full((1, aw)), full(wsp.shape),
                   full((A_GROUPS, CHUNK, gd)), full(cw.shape)),
        scratch_shapes=[pltpu.VMEM((rows_woe * NDEV, dm), BF16), pltpu.SemaphoreType.DMA((NDEV,))],
        compiler_params=_cp(), name="even_core_bwd")(
            proj, proj, proj, proj, proj, proj, proj, dx1, dx1, dx1, lng, lnb, wsp, bspb, cw, gath)


def _ff_chunks(f, width=1024):
    return [(c0, min(c0 + width, f)) for c0 in range(0, f, width)]


def _ffn_fwd(x, gain, gath, lay, layer, tm):
    t, dm = x.shape
    rf = lay.rows["wgT0"]
    f = rf * NDEV
    og, ou, od = lay.off[f"wgT{layer}"], lay.off[f"wuT{layer}"], lay.off[f"wd{layer}"]

    def body(x_ref, g_ref, gath_ref, xo_ref, gate_ref, up_ref, wg, wu, wd, sems):
        @pl.when(pl.program_id(0) == 0)
        def _():
            _load_weight(gath_ref, wg, sems, og, rf)
            _load_weight(gath_ref, wu, sems, ou, rf)
            _load_weight(gath_ref, wd, sems, od, rf)
        xv = x_ref[...]
        h, _ = _rms_fwd(xv, g_ref[...])
        hb = h.astype(BF16)
        out = xv
        for c0, c1 in _ff_chunks(f):
            gate = _dot(hb, wg[c0:c1, :], NT)
            up = _dot(hb, wu[c0:c1, :], NT)
            gate_ref[:, c0:c1] = gate.astype(BF16)
            up_ref[:, c0:c1] = up.astype(BF16)
            act = (gate * _sigmoid(gate) * up).astype(BF16)
            out = out + _dot(act, wd[c0:c1, :], NN)
        xo_ref[...] = out

    return pl.pallas_call(
        body, grid=(t // tm,),
        out_shape=(jax.ShapeDtypeStruct((t, dm), F32), jax.ShapeDtypeStruct((t, f), BF16),
                   jax.ShapeDtypeStruct((t, f), BF16)),
        in_specs=[pl.BlockSpec((tm, dm), lambda i: (i, 0)), pl.BlockSpec((1, dm), lambda i: (0, 0)), ANY],
        out_specs=(pl.BlockSpec((tm, dm), lambda i: (i, 0)), pl.BlockSpec((tm, f), lambda i: (i, 0)),
                   pl.BlockSpec((tm, f), lambda i: (i, 0))),
        scratch_shapes=[pltpu.VMEM((f, dm), BF16), pltpu.VMEM((f, dm), BF16), pltpu.VMEM((f, dm), BF16),
                        pltpu.SemaphoreType.DMA((NDEV,))],
        compiler_params=_cp(), name=f"ffn_fwd{layer}")(x, gain, gath)


def _ffn_bwd(dxo, x, gate, up, gain, gath, lay, layer, tm):
    t, dm = x.shape
    rf = lay.rows["wgT0"]
    f = rf * NDEV
    og, ou, od = lay.off[f"wgT{layer}"], lay.off[f"wuT{layer}"], lay.off[f"wd{layer}"]

    def body(dxo_ref, x_ref, gate_ref, up_ref, g_ref, gath_ref,
             dx_ref, dxb_ref, dg_ref, du_ref, act_ref, hb_ref, dgain_ref, wg, wu, wd, sems):
        @pl.when(pl.program_id(0) == 0)
        def _():
            _load_weight(gath_ref, wg, sems, og, rf)
            _load_weight(gath_ref, wu, sems, ou, rf)
            _load_weight(gath_ref, wd, sems, od, rf)
            dgain_ref[...] = jnp.zeros_like(dgain_ref)
        xv, gain_v, dxo_v = x_ref[...], g_ref[...], dxo_ref[...]
        h, r = _rms_fwd(xv, gain_v)
        hb_ref[...] = h.astype(BF16)
        dxob = dxo_v.astype(BF16)
        dh = jnp.zeros_like(xv)
        for c0, c1 in _ff_chunks(f):
            gate_v = gate_ref[:, c0:c1].astype(F32)
            up_v = up_ref[:, c0:c1].astype(F32)
            s = _sigmoid(gate_v)
            silu = gate_v * s
            act_ref[:, c0:c1] = (silu * up_v).astype(BF16)
            dact = _dot(dxob, wd[c0:c1, :], NT)
            dg = (dact * up_v * (s * (1.0 + gate_v * (1.0 - s)))).astype(BF16)
            du = (dact * silu).astype(BF16)
            dg_ref[:, c0:c1] = dg
            du_ref[:, c0:c1] = du
            dh = dh + _dot(dg, wg[c0:c1, :], NN) + _dot(du, wu[c0:c1, :], NN)
        dx, dgain = _rms_bwd(dh, xv, r, gain_v)
        dx = dxo_v + dx
        dx_ref[...] = dx
        dxb_ref[...] = dx.astype(BF16)
        dgain_ref[...] += dgain

    tok = lambda w: pl.BlockSpec((tm, w), lambda i: (i, 0))
    return pl.pallas_call(
        body, grid=(t // tm,),
        out_shape=(jax.ShapeDtypeStruct((t, dm), F32), jax.ShapeDtypeStruct((t, dm), BF16),
                   jax.ShapeDtypeStruct((t, f), BF16), jax.ShapeDtypeStruct((t, f), BF16),
                   jax.ShapeDtypeStruct((t, f), BF16), jax.ShapeDtypeStruct((t, dm), BF16),
                   jax.ShapeDtypeStruct((1, dm), F32)),
        in_specs=[tok(dm), tok(dm), tok(f), tok(f), pl.BlockSpec((1, dm), lambda i: (0, 0)), ANY],
        out_specs=(tok(dm), tok(dm), tok(f), tok(f), tok(f), tok(dm), pl.BlockSpec((1, dm), lambda i: (0, 0))),
        scratch_shapes=[pltpu.VMEM((f, dm), BF16), pltpu.VMEM((f, dm), BF16), pltpu.VMEM((f, dm), BF16),
                        pltpu.SemaphoreType.DMA((NDEV,))],
        compiler_params=_cp(), name=f"ffn_bwd{layer}")(dxo, x, gate, up, gain, gath)


def _t5_buckets(rel):
    nb = N_BUCKETS // 2
    ret = jnp.where(rel > 0, nb, 0)
    n = jnp.abs(rel)
    max_exact = nb // 2
    nf = jnp.maximum(n, 1).astype(jnp.float32)
    large = max_exact + (jnp.log(nf / max_exact) / math.log(MAX_DISTANCE / max_exact)
                         * (nb - max_exact)).astype(jnp.int32)
    large = jnp.minimum(large, nb - 1)
    return ret + jnp.where(n < max_exact, n, large)


def _bucket_table():
    qi = jnp.arange(CHUNK, dtype=jnp.int32)[:, None]
    kj = jnp.arange(3 * CHUNK, dtype=jnp.int32)[None, :]
    rel = kj - CHUNK - qi
    return jnp.where(jnp.abs(rel) <= CHUNK, _t5_buckets(rel), -1)


def _bias_table(rel_bias_t, buckets):
    nh = rel_bias_t.shape[0]

    def body(rb_ref, bk_ref, o_ref):
        bk = bk_ref[...]
        for h in range(nh):
            acc = jnp.where(bk < 0, NEG, 0.0).astype(F32)
            for b in range(N_BUCKETS):
                acc = jnp.where(bk == b, rb_ref[h, b], acc)
            o_ref[h] = acc

    return pl.pallas_call(
        body, out_shape=jax.ShapeDtypeStruct((nh,) + buckets.shape, F32),
        in_specs=[pl.BlockSpec(memory_space=pltpu.SMEM), pl.BlockSpec(memory_space=pltpu.VMEM)],
        out_specs=pl.BlockSpec(memory_space=pltpu.VMEM), name="bias_table")(rel_bias_t, buckets)


def _rel_bias_grad(dbias, buckets):
    nh = dbias.shape[0]

    def body(db_ref, bk_ref, o_ref):
        bk = bk_ref[...]
        lane = lax.broadcasted_iota(jnp.int32, (1, 128), 1)
        for h in range(nh):
            d = db_ref[h]
            row = jnp.zeros((1, 128), F32)
            for b in range(N_BUCKETS):
                s = jnp.sum(jnp.sum(jnp.where(bk == b, d, 0.0), axis=1, keepdims=True), axis=0, keepdims=True)
                row = jnp.where(lane == b, s, row)
            o_ref[h:h + 1, :] = row

    return pl.pallas_call(
        body, out_shape=jax.ShapeDtypeStruct((nh, 128), F32),
        in_specs=[pl.BlockSpec(memory_space=pltpu.VMEM), pl.BlockSpec(memory_space=pltpu.VMEM)],
        out_specs=pl.BlockSpec(memory_space=pltpu.VMEM), compiler_params=_cp(0), name="rel_bias_grad")(dbias, buckets)


def _half_masks():
    lane = lax.broadcasted_iota(jnp.int32, (CHUNK, 128), 1)
    return lane < HEAD_DIM, lane >= HEAD_DIM


def _kv_halves(blk, hk, lo, hi):
    zero = jnp.zeros_like(blk)
    rolled = pltpu.roll(blk, HEAD_DIM, 1)
    if hk % 2 == 0:
        return jnp.where(lo, blk, zero), jnp.where(hi, rolled, zero)
    return jnp.where(lo, rolled, zero), jnp.where(hi, blk, zero)


def _attn_probs(q2, k_lo, k_hi, bias_ref, sink_ref, hk, n, nblk):
    scale = HEAD_DIM ** -0.5
    out = {}
    for half, ks in ((0, k_lo), (1, k_hi)):
        s3 = [_dot(q2, ks[jj], NT) * scale for jj in range(3)]
        for tile in range(2):
            h = 4 * hk + 2 * tile + half
            s = []
            for jj in range(3):
                sj = s3[jj][tile * CHUNK:(tile + 1) * CHUNK, :] + bias_ref[h, :, jj * CHUNK:(jj + 1) * CHUNK]
                if jj == 0:
                    sj = jnp.where(n > 0, sj, NEG)
                if jj == 2:
                    sj = jnp.where(n < nblk - 1, sj, NEG)
                s.append(sj)
            sink = sink_ref[h]
            m = jnp.maximum(jnp.max(jnp.maximum(jnp.maximum(s[0], s[1]), s[2]), axis=-1, keepdims=True), sink)
            e = [jnp.exp(sj - m) for sj in s]
            es = jnp.exp(sink - m)
            inv = 1.0 / (jnp.sum(e[0] + e[1] + e[2], axis=-1, keepdims=True) + es)
            out[(half, tile)] = ([ej * inv for ej in e], es * inv)
    return out


def _key_block_starts(n, nblk):
    return [pl.multiple_of(jnp.clip(n - 1 + jj, 0, nblk - 1) * CHUNK, CHUNK) for jj in range(3)]


def _attn_fwd(qkv, x2, bias, sink, gath, off_woa, rows_woa):
    t, dm = x2.shape
    nblk = t // CHUNK
    kvw = N_KV * HEAD_DIM
    kcb, vcb = dm // kvw, dm // kvw + 1

    def body(q_ref, k_ref, v_ref, x2_ref, bias_ref, sink_ref, gath_ref, x3_ref, att_ref, wbuf, sems):
        n = pl.program_id(0)

        @pl.when(n == 0)
        def _():
            _load_weight(gath_ref, wbuf, sems, off_woa, rows_woa)
        lo, hi = _half_masks()
        starts = _key_block_starts(n, nblk)
        tiles = []
        for hk in range(N_KV):
            kt = (hk // 2) * 128
            k_lo, k_hi, v_lo, v_hi = [], [], [], []
            for jj in range(3):
                a, b = _kv_halves(k_ref[pl.ds(starts[jj], CHUNK), kt:kt + 128], hk, lo, hi)
                k_lo.append(a)
                k_hi.append(b)
                a, b = _kv_halves(v_ref[pl.ds(starts[jj], CHUNK), kt:kt + 128], hk, lo, hi)
                v_lo.append(a)
                v_hi.append(b)
            q2 = jnp.concatenate([q_ref[:, (2 * hk) * 128:(2 * hk + 1) * 128],
                                  q_ref[:, (2 * hk + 1) * 128:(2 * hk + 2) * 128]], axis=0)
            pr = _attn_probs(q2, k_lo, k_hi, bias_ref, sink_ref, hk, n, nblk)
            o2 = jnp.zeros((2 * CHUNK, 128), F32)
            for half, vs in ((0, v_lo), (1, v_hi)):
                for jj in range(3):
                    p2 = jnp.concatenate([pr[(half, 0)][0][jj], pr[(half, 1)][0][jj]], axis=0).astype(BF16)
                    o2 = o2 + _dot(p2, vs[jj], NN)
            tiles += [o2[0:CHUNK], o2[CHUNK:]]
        att = jnp.concatenate(tiles, axis=1).astype(BF16)
        att_ref[...] = att
        x3_ref[...] = x2_ref[...] + _dot(att, wbuf[...], NN)

    return pl.pallas_call(
        body, grid=(nblk,),
        out_shape=(jax.ShapeDtypeStruct((t, dm), F32), jax.ShapeDtypeStruct((t, dm), BF16)),
        in_specs=[pl.BlockSpec((CHUNK, dm), lambda n: (n, 0)), pl.BlockSpec((t, kvw), lambda n: (0, kcb)),
                  pl.BlockSpec((t, kvw), lambda n: (0, vcb)), pl.BlockSpec((CHUNK, dm), lambda n: (n, 0)),
                  pl.BlockSpec(bias.shape, lambda n: (0, 0, 0)), pl.BlockSpec(memory_space=pltpu.SMEM), ANY],
        out_specs=(pl.BlockSpec((CHUNK, dm), lambda n: (n, 0)), pl.BlockSpec((CHUNK, dm), lambda n: (n, 0))),
        scratch_shapes=[pltpu.VMEM((rows_woa * NDEV, dm), BF16), pltpu.SemaphoreType.DMA((NDEV,))],
        compiler_params=_cp(), name="attn_fwd")(qkv, qkv, qkv, x2, bias, sink, gath)


def _attn_bwd(qkv, att, dx3, bias, sink, gath, off_woa, rows_woa):
    t, dm = dx3.shape
    nblk = t // CHUNK
    kvw = N_KV * HEAD_DIM
    kcb, vcb = dm // kvw, dm // kvw + 1
    scale = HEAD_DIM ** -0.5

    def body(q_ref, k_ref, v_ref, att_ref, dx_ref, bias_ref, sink_ref, gath_ref,
             dq_ref, dk_ref, dv_ref, dbias_ref, dsink_ref, wbuf, sems):
        n = pl.program_id(0)

        @pl.when(n == 0)
        def _():
            _load_weight(gath_ref, wbuf, sems, off_woa, rows_woa)
            dk_ref[...] = jnp.zeros_like(dk_ref)
            dv_ref[...] = jnp.zeros_like(dv_ref)
            dbias_ref[...] = jnp.zeros_like(dbias_ref)
            dsink_ref[...] = jnp.zeros_like(dsink_ref)
        lo, hi = _half_masks()
        lane1 = lax.broadcasted_iota(jnp.int32, (1, 128), 1)
        starts = _key_block_starts(n, nblk)
        dout = _dot(dx_ref[...].astype(BF16), wbuf[...], NT)
        prod = dout * att_ref[...].astype(F32)
        doutb = dout.astype(BF16)
        dq_tiles = []
        dsink_row = jnp.zeros((1, 128), F32)
        for hk in range(N_KV):
            kt = (hk // 2) * 128
            k_lo, k_hi, v_lo, v_hi = [], [], [], []
            for jj in range(3):
                a, b = _kv_halves(k_ref[pl.ds(starts[jj], CHUNK), kt:kt + 128], hk, lo, hi)
                k_lo.append(a)
                k_hi.append(b)
                a, b = _kv_halves(v_ref[pl.ds(starts[jj], CHUNK), kt:kt + 128], hk, lo, hi)
                v_lo.append(a)
                v_hi.append(b)
            c0 = (2 * hk) * 128
            q2 = jnp.concatenate([q_ref[:, c0:c0 + 128], q_ref[:, c0 + 128:c0 + 256]], axis=0)
            do2 = jnp.concatenate([doutb[:, c0:c0 + 128], doutb[:, c0 + 128:c0 + 256]], axis=0)
            pr = _attn_probs(q2, k_lo, k_hi, bias_ref, sink_ref, hk, n, nblk)
            dq2 = jnp.zeros((2 * CHUNK, 128), F32)
            dk_acc = [None] * 3
            dv_acc = [None] * 3
            for half, ks, vs, msk in ((0, k_lo, v_lo, lo), (1, k_hi, v_hi, hi)):
                dsum = []
                for tile in range(2):
                    pt = prod[:, c0 + tile * 128:c0 + (tile + 1) * 128]
                    dsum.append(jnp.sum(jnp.where(msk, pt, 0.0), axis=-1, keepdims=True))
                    h = 4 * hk + 2 * tile + half
                    contrib = -jnp.sum(pr[(half, tile)][1] * dsum[tile], axis=0, keepdims=True)
                    dsink_row = dsink_row + jnp.where(lane1 == h, contrib, 0.0)
                for jj in range(3):
                    dp2 = _dot(do2, vs[jj], NT)
                    ds_t = []
                    for tile in range(2):
                        p = pr[(half, tile)][0][jj]
                        ds = p * (dp2[tile * CHUNK:(tile + 1) * CHUNK, :] - dsum[tile])
                        h = 4 * hk + 2 * tile + half
                        dbias_ref[h, :, jj * CHUNK:(jj + 1) * CHUNK] += ds
                        ds_t.append(ds)
                    ds2 = jnp.concatenate(ds_t, axis=0).astype(BF16)
                    p2 = jnp.concatenate([pr[(half, 0)][0][jj], pr[(half, 1)][0][jj]], axis=0).astype(BF16)
                    dq2 = dq2 + _dot(ds2, ks[jj], NN) * scale
                    dkj = jnp.where(msk, _dot(ds2, q2, TN) * scale, 0.0)
                    dvj = jnp.where(msk, _dot(p2, do2, TN), 0.0)
                    dk_acc[jj] = dkj if dk_acc[jj] is None else dk_acc[jj] + dkj
                    dv_acc[jj] = dvj if dv_acc[jj] is None else dv_acc[jj] + dvj
            for jj in range(3):
                keep = lo if hk % 2 == 0 else hi
                dkj = dk_acc[jj] + pltpu.roll(dk_acc[jj], HEAD_DIM, 1)
                dvj = dv_acc[jj] + pltpu.roll(dv_acc[jj], HEAD_DIM, 1)
                dk_ref[pl.ds(starts[jj], CHUNK), kt:kt + 128] += jnp.where(keep, dkj, 0.0)
                dv_ref[pl.ds(starts[jj], CHUNK), kt:kt + 128] += jnp.where(keep, dvj, 0.0)
            dq_tiles += [dq2[0:CHUNK], dq2[CHUNK:]]
        dq_ref[...] = jnp.concatenate(dq_tiles, axis=1).astype(BF16)
        dsink_ref[...] += dsink_row

    blk = lambda: pl.BlockSpec((CHUNK, dm), lambda n: (n, 0))
    return pl.pallas_call(
        body, grid=(nblk,),
        out_shape=(jax.ShapeDtypeStruct((t, dm), BF16), jax.ShapeDtypeStruct((t, kvw), F32),
                   jax.ShapeDtypeStruct((t, kvw), F32), jax.ShapeDtypeStruct(bias.shape, F32),
                   jax.ShapeDtypeStruct((1, 128), F32)),
        in_specs=[blk(), pl.BlockSpec((t, kvw), lambda n: (0, kcb)), pl.BlockSpec((t, kvw), lambda n: (0, vcb)),
                  blk(), blk(), pl.BlockSpec(bias.shape, lambda n: (0, 0, 0)),
                  pl.BlockSpec(memory_space=pltpu.SMEM), ANY],
        out_specs=(blk(), pl.BlockSpec((t, kvw), lambda n: (0, 0)), pl.BlockSpec((t, kvw), lambda n: (0, 0)),
                   pl.BlockSpec(bias.shape, lambda n: (0, 0, 0)), pl.BlockSpec((1, 128), lambda n: (0, 0))),
        scratch_shapes=[pltpu.VMEM((rows_woa * NDEV, dm), BF16), pltpu.SemaphoreType.DMA((NDEV,))],
        compiler_params=_cp(), name="attn_bwd")(qkv, qkv, qkv, att, dx3, bias, sink, gath)


def _final_loss(x4, target, gain, tm):
    t, dm = x4.shape

    def body(x_ref, t_ref, g_ref, loss_ref, dx_ref, dxb_ref, dgain_ref, acc):
        i = pl.program_id(0)

        @pl.when(i == 0)
        def _():
            acc[...] = jnp.zeros_like(acc)
            dgain_ref[...] = jnp.zeros_like(dgain_ref)
        xv, gain_v = x_ref[...], g_ref[...]
        y, r = _rms_fwd(xv, gain_v)
        e = y - t_ref[...]
        acc[...] += jnp.sum(e * e, axis=0, keepdims=True)
        dx, dgain = _rms_bwd(e * (1.0 / dm), xv, r, gain_v)
        dx_ref[...] = dx
        dxb_ref[...] = dx.astype(BF16)
        dgain_ref[...] += dgain

        @pl.when(i == pl.num_programs(0) - 1)
        def _():
            loss_ref[...] = jnp.sum(acc[...], axis=-1, keepdims=True) * (0.5 / dm)

    return pl.pallas_call(
        body, grid=(t // tm,),
        out_shape=(jax.ShapeDtypeStruct((1, 1), F32), jax.ShapeDtypeStruct((t, dm), F32),
                   jax.ShapeDtypeStruct((t, dm), BF16), jax.ShapeDtypeStruct((1, dm), F32)),
        in_specs=[pl.BlockSpec((tm, dm), lambda i: (i, 0)), pl.BlockSpec((tm, dm), lambda i: (i, 0)),
                  pl.BlockSpec((1, dm), lambda i: (0, 0))],
        out_specs=(pl.BlockSpec((1, 1), lambda i: (0, 0)), pl.BlockSpec((tm, dm), lambda i: (i, 0)),
                   pl.BlockSpec((tm, dm), lambda i: (i, 0)), pl.BlockSpec((1, dm), lambda i: (0, 0))),
        scratch_shapes=[pltpu.VMEM((1, dm), F32)],
        compiler_params=_cp(), name="final_loss")(x4, target, gain)


def _adamw(w, g, m, v, name):
    r, c = w.shape
    tr = next(c2 for c2 in (512, 352, 256, 128, r) if r % c2 == 0)

    def body(w_ref, g_ref, m_ref, v_ref, d_ref, nm_ref, nv_ref):
        gv = g_ref[...]
        nm = ADAM_B1 * m_ref[...] + (1.0 - ADAM_B1) * gv
        nv = ADAM_B2 * v_ref[...] + (1.0 - ADAM_B2) * (gv * gv)
        m_hat = nm / (1.0 - ADAM_B1 ** ADAM_STEP)
        v_hat = nv / (1.0 - ADAM_B2 ** ADAM_STEP)
        d_ref[...] = -ADAM_LR * (m_hat / (jnp.sqrt(v_hat) + ADAM_EPS) + ADAM_WD * w_ref[...])
        nm_ref[...] = nm
        nv_ref[...] = nv

    spec = pl.BlockSpec((tr, c), lambda i: (i, 0))
    out = jax.ShapeDtypeStruct((r, c), F32)
    return pl.pallas_call(
        body, grid=(r // tr,), out_shape=(out, out, out), in_specs=[spec] * 4, out_specs=(spec,) * 3,
        compiler_params=_cp(), name=name)(w, g, m, v)


def _pack_small(parts, rows):
    flat = jnp.concatenate([p.reshape(-1) for p in parts])
    return jnp.pad(flat, (0, rows * 128 - flat.shape[0])).reshape(rows, 128)


def _unpack_small(packed, shapes):
    flat = packed.reshape(-1)
    out, o = [], 0
    for s in shapes:
        n = int(np.prod(s))
        out.append(flat[o:o + n].reshape(s))
        o += n
    return out


def kernel(x, norm_mix, norm_ffn, even_w_in, even_v_ln_g, even_v_ln_b, even_w_spatial, even_b_spatial, even_conv_w, even_w_out, attn_w_qkv, attn_sink, rel_bias, attn_w_out, ffn_w_gate, ffn_w_up, ffn_w_down, final_norm, loss_target, m_norm_mix, m_norm_ffn, m_even_w_in, m_even_v_ln_g, m_even_v_ln_b, m_even_w_spatial, m_even_b_spatial, m_even_conv_w, m_even_w_out, m_attn_w_qkv, m_attn_sink, m_rel_bias, m_attn_w_out, m_ffn_w_gate, m_ffn_w_up, m_ffn_w_down, m_final_norm, v_norm_mix, v_norm_ffn, v_even_w_in, v_even_v_ln_g, v_even_v_ln_b, v_even_w_spatial, v_even_b_spatial, v_even_conv_w, v_even_w_out, v_attn_w_qkv, v_attn_sink, v_rel_bias, v_attn_w_out, v_ffn_w_gate, v_ffn_w_up, v_ffn_w_down, v_final_norm):
    t, dm = x.shape[1], x.shape[2]
    in_w = even_w_in.shape[2] * NDEV
    qkv_w = attn_w_qkv.shape[2] * NDEV
    d_ff = ffn_w_gate.shape[2] * NDEV
    aw = even_v_ln_g.shape[1]
    bw = even_conv_w.shape[2] * NDEV
    gd = aw // A_GROUPS
    lay = _Layout(dm, in_w, qkv_w, d_ff)
    tm = min(512, t // 2)
    tmf = min(256, t // 2)
    me = _my_index()

    shard = jnp.concatenate([
        even_w_in[0].T, attn_w_qkv[0].T, ffn_w_gate[0].T, ffn_w_up[0].T, ffn_w_gate[1].T, ffn_w_up[1].T,
        even_w_out[0], attn_w_out[0], ffn_w_down[0], ffn_w_down[1]], axis=0).astype(BF16)
    gath = _all_gather_weights(shard)
    cw_rows = 3 * bw // 128
    cw_mine = lax.dynamic_update_slice(jnp.zeros((3, bw), F32), even_conv_w[0], (0, me * (bw // NDEV)))
    cw_full = _all_reduce_small(jnp.pad(cw_mine.reshape(cw_rows, 128), ((0, 16 - cw_rows), (0, 0))), "gather_conv_w")[0:cw_rows].reshape(3, bw)

    x0 = x[0]
    row = lambda a: a.reshape(1, -1)
    wsp_b = even_w_spatial[0].astype(BF16)
    bspb = jnp.broadcast_to(even_b_spatial[0][:, :, None], (A_GROUPS, CHUNK, gd))
    buckets = _bucket_table()
    bias = _bias_table(rel_bias.T, buckets)
    sink = attn_sink[0]

    proj, h0b = _norm_proj(x0, row(norm_mix[0]), gath, lay.off["winT"], lay.rows["winT"], F32, "in_proj", tm)
    x1, yb = _even_core_fwd(proj, x0, even_v_ln_g, even_v_ln_b, wsp_b, bspb, cw_full, gath,
                            lay.off["woe"], lay.rows["woe"], tm)
    x2, gate0, up0 = _ffn_fwd(x1, row(norm_ffn[0]), gath, lay, 0, tmf)
    qkv, h2b = _norm_proj(x2, row(norm_mix[1]), gath, lay.off["wqkvT"], lay.rows["wqkvT"], BF16, "qkv_proj", tm)
    x3, attb = _attn_fwd(qkv, x2, bias, sink, gath, lay.off["woa"], lay.rows["woa"])
    x4, gate1, up1 = _ffn_fwd(x3, row(norm_ffn[1]), gath, lay, 1, tmf)
    loss_part, dx4, dx4b, d_final = _final_loss(x4, loss_target[0], row(final_norm), tm)

    dx3, dx3b, dg1, du1, act1, h3b, d_nffn1 = _ffn_bwd(dx4, x3, gate1, up1, row(norm_ffn[1]), gath, lay, 1, tmf)
    g_wgT1 = _wgrad(dg1, h3b, "wgrad_gate1")
    g_wuT1 = _wgrad(du1, h3b, "wgrad_up1")
    g_wd1 = _wgrad(act1, dx4b, "wgrad_down1")
    dq, dk, dv, dbias, dsink = _attn_bwd(qkv, attb, dx3, bias, sink, gath, lay.off["woa"], lay.rows["woa"])
    g_woa = _wgrad(attb, dx3b, "wgrad_attn_out")
    d_relb = _rel_bias_grad(dbias, buckets)[:, 0:N_BUCKETS].T
    dqkv = jnp.concatenate([dq, dk.astype(BF16), dv.astype(BF16)], axis=1)
    dx2, dx2b, d_nmix1 = _proj_bwd_norm(dqkv, x2, row(norm_mix[1]), dx3, gath, lay.off["wqkvT"], lay.rows["wqkvT"],
                                        "qkv_bwd", tm)
    g_wqkvT = _wgrad(dqkv, h2b, "wgrad_qkv")
    dx1, dx1b, dg0, du0, act0, h1b, d_nffn0 = _ffn_bwd(dx2, x1, gate0, up0, row(norm_ffn[0]), gath, lay, 0, tmf)
    g_wgT0 = _wgrad(dg0, h1b, "wgrad_gate0")
    g_wuT0 = _wgrad(du0, h1b, "wgrad_up0")
    g_wd0 = _wgrad(act0, dx2b, "wgrad_down0")
    dproj, d_lng, d_lnb, d_wsp, d_bsp3, d_cw = _even_core_bwd(
        proj, dx1, even_v_ln_g, even_v_ln_b, wsp_b, bspb, cw_full, gath, lay.off["woe"], lay.rows["woe"], tm)
    g_woe = _wgrad(yb, dx1b, "wgrad_even_out")
    dx0, _, d_nmix0 = _proj_bwd_norm(dproj, x0, row(norm_mix[0]), dx1, gath, lay.off["winT"], lay.rows["winT"],
                                     "in_proj_bwd", tm)
    g_winT = _wgrad(dproj, h0b, "wgrad_in")

    pieces = [g_winT, g_wqkvT, g_wgT0, g_wuT0, g_wgT1, g_wuT1, g_woe, g_woa, g_wd0, g_wd1]
    recv = _exchange_grads(pieces, [lay.rows[n] for n in lay.names], [lay.off[n] for n in lay.names], lay.total)
    gsum = _sum_slots(recv)
    piece = lambda n: gsum[lay.off[n]:lay.off[n] + lay.rows[n]]
    grads = {
        "even_w_in": piece("winT").T[None], "attn_w_qkv": piece("wqkvT").T[None],
        "ffn_w_gate": jnp.stack([piece("wgT0").T, piece("wgT1").T]),
        "ffn_w_up": jnp.stack([piece("wuT0").T, piece("wuT1").T]),
        "even_w_out": piece("woe")[None], "attn_w_out": piece("woa")[None],
        "ffn_w_down": jnp.stack([piece("wd0"), piece("wd1")]),
    }
    small_shapes = [(2, dm), (2, dm), (1, aw), (1, aw), (1, A_GROUPS, CHUNK, CHUNK), (1, A_GROUPS, CHUNK), (3, bw),
                    (1, N_HEADS), (N_BUCKETS, N_HEADS), (dm,)]
    n_small = sum(int(np.prod(s)) for s in small_shapes)
    small_rows = 8 * ((n_small + 1023) // 1024)
    small_part = _pack_small(
        [jnp.concatenate([d_nmix0, d_nmix1]), jnp.concatenate([d_nffn0, d_nffn1]), d_lng, d_lnb, d_wsp,
         jnp.sum(d_bsp3, axis=-1), d_cw, dsink[:, 0:N_HEADS], d_relb, d_final], small_rows)
    small_sum = _unpack_small(_all_reduce_small(small_part, "all_reduce_small_grads"), small_shapes)
    (grads["norm_mix"], grads["norm_ffn"], grads["even_v_ln_g"], grads["even_v_ln_b"], grads["even_w_spatial"],
     grads["even_b_spatial"], g_cw_full, grads["attn_sink"], grads["rel_bias"], grads["final_norm"]) = small_sum
    grads["even_conv_w"] = lax.dynamic_slice(g_cw_full, (0, me * (bw // NDEV)), (3, bw // NDEV))[None]

    order = ["norm_mix", "norm_ffn", "even_w_in", "even_v_ln_g", "even_v_ln_b", "even_w_spatial", "even_b_spatial",
             "even_conv_w", "even_w_out", "attn_w_qkv", "attn_sink", "rel_bias", "attn_w_out", "ffn_w_gate",
             "ffn_w_up", "ffn_w_down", "final_norm"]
    ws = dict(norm_mix=norm_mix, norm_ffn=norm_ffn, even_w_in=even_w_in, even_v_ln_g=even_v_ln_g,
              even_v_ln_b=even_v_ln_b, even_w_spatial=even_w_spatial, even_b_spatial=even_b_spatial,
              even_conv_w=even_conv_w, even_w_out=even_w_out, attn_w_qkv=attn_w_qkv, attn_sink=attn_sink,
              rel_bias=rel_bias, attn_w_out=attn_w_out, ffn_w_gate=ffn_w_gate, ffn_w_up=ffn_w_up,
              ffn_w_down=ffn_w_down, final_norm=final_norm)
    ms = dict(norm_mix=m_norm_mix, norm_ffn=m_norm_ffn, even_w_in=m_even_w_in, even_v_ln_g=m_even_v_ln_g,
              even_v_ln_b=m_even_v_ln_b, even_w_spatial=m_even_w_spatial, even_b_spatial=m_even_b_spatial,
              even_conv_w=m_even_conv_w, even_w_out=m_even_w_out, attn_w_qkv=m_attn_w_qkv, attn_sink=m_attn_sink,
              rel_bias=m_rel_bias, attn_w_out=m_attn_w_out, ffn_w_gate=m_ffn_w_gate, ffn_w_up=m_ffn_w_up,
              ffn_w_down=m_ffn_w_down, final_norm=m_final_norm)
    vs = dict(norm_mix=v_norm_mix, norm_ffn=v_norm_ffn, even_w_in=v_even_w_in, even_v_ln_g=v_even_v_ln_g,
              even_v_ln_b=v_even_v_ln_b, even_w_spatial=v_even_w_spatial, even_b_spatial=v_even_b_spatial,
              even_conv_w=v_even_conv_w, even_w_out=v_even_w_out, attn_w_qkv=v_attn_w_qkv, attn_sink=v_attn_sink,
              rel_bias=v_rel_bias, attn_w_out=v_attn_w_out, ffn_w_gate=v_ffn_w_gate, ffn_w_up=v_ffn_w_up,
              ffn_w_down=v_ffn_w_down, final_norm=v_final_norm)
    big = ["even_w_in", "even_w_out", "attn_w_qkv", "attn_w_out", "ffn_w_gate", "ffn_w_up", "ffn_w_down"]
    delta, new_m, new_v = {}, {}, {}
    for n in big:
        shp = ws[n].shape
        two = lambda a: a.reshape(-1, shp[-1])
        d, nm, nv = _adamw(two(ws[n]), two(grads[n]), two(ms[n]), two(vs[n]), "adamw_" + n)
        delta[n], new_m[n], new_v[n] = d.reshape(shp), nm.reshape(shp), nv.reshape(shp)
    small = [n for n in order if n not in big]
    sshapes = [ws[n].shape for n in small]
    ns = sum(int(np.prod(s)) for s in sshapes)
    srows = 8 * ((ns + 1023) // 1024)
    pk = lambda dct: _pack_small([dct[n] for n in small], srows)
    d, nm, nv = _adamw(pk(ws), pk(grads), pk(ms), pk(vs), "adamw_small")
    for n, a, b, c2 in zip(small, _unpack_small(d, sshapes), _unpack_small(nm, sshapes), _unpack_small(nv, sshapes)):
        delta[n], new_m[n], new_v[n] = a, b, c2

    loss = lax.psum(loss_part[0, 0], ("x", "y", "c"))
    return (loss, dx0[None], *[grads[n] for n in order], *[delta[n] for n in order],
            *[new_m[n] for n in order], *[new_v[n] for n in order])
```

```python
import math

import jax
import jax.numpy as jnp
import numpy as np
from jax import lax
from jax.experimental import pallas as pl
from jax.experimental.pallas import tpu as pltpu

F32, BF16 = jnp.float32, jnp.bfloat16
NDEV = 8
EPS = 1e-6
CHUNK = 128
A_GROUPS = 4
N_HEADS, N_KV, HEAD_DIM = 16, 4, 64
N_BUCKETS, MAX_DISTANCE = 32, 128
NEG = -1e30
ADAM_LR, ADAM_B1, ADAM_B2, ADAM_EPS, ADAM_WD, ADAM_STEP = 0.001, 0.9, 0.999, 1e-08, 0.01, 10
VMEM_LIMIT = 56 * 1024 * 1024
MESH = pl.DeviceIdType.MESH
NT = (((1,), (1,)), ((), ()))
NN = (((1,), (0,)), ((), ()))
TN = (((0,), (0,)), ((), ()))
ANY = pl.BlockSpec(memory_space=pl.ANY)


def _cp(n_grid=1):
    return pltpu.CompilerParams(dimension_semantics=("arbitrary",) * n_grid, vmem_limit_bytes=VMEM_LIMIT)


def _dot(a, b, dims):
    return lax.dot_general(a, b, dims, preferred_element_type=F32)


def _my_index():
    return 4 * lax.axis_index("x") + 2 * lax.axis_index("y") + lax.axis_index("c")


def _peer(k):
    x, y, c = lax.axis_index("x"), lax.axis_index("y"), lax.axis_index("c")
    px = 1 - x if k & 4 else x
    py = 1 - y if k & 2 else y
    pc = 1 - c if k & 1 else c
    return (px, py, pc)


def _load_weight(gath_ref, wbuf, sems):
    rows = gath_ref.shape[1]
    cps = [pltpu.make_async_copy(gath_ref.at[d], wbuf.at[pl.ds(d * rows, rows), :], sems.at[d]) for d in range(NDEV)]
    for c in cps:
        c.start()
    for c in cps:
        c.wait()


class _GatherCarry:
    def __init__(self, pieces):
        self.inputs = list(pieces)
        self.n = len(pieces)
        self.out_shape = [jax.ShapeDtypeStruct((NDEV,) + p.shape, p.dtype) for p in pieces]
        self.scratch = [pltpu.SemaphoreType.DMA((7 * self.n,)), pltpu.SemaphoreType.DMA((7 * self.n,)),
                        pltpu.SemaphoreType.DMA((self.n,))]

    def _ctx(self):
        x, y, c = lax.axis_index("x"), lax.axis_index("y"), lax.axis_index("c")
        chips = [(1 - x, y), (x, 1 - y), (1 - x, 1 - y)]
        return (x, y, c), (x, y, 1 - c), chips, c

    def _copy(self, k, j, block, to, ins, outs, sems, src=None):
        send_sems, recv_sems, _ = sems
        slot = outs[j].at[4 * block[0] + 2 * block[1] + block[2]]
        return pltpu.make_async_remote_copy(
            src_ref=slot if src is None else src, dst_ref=slot, send_sem=send_sems.at[k * self.n + j],
            recv_sem=recv_sems.at[k * self.n + j], device_id=to, device_id_type=MESH)

    def start(self, ins, outs, sems):
        me, sibling, chips, c = self._ctx()
        for j in range(self.n):
            pltpu.make_async_copy(ins[j], outs[j].at[4 * me[0] + 2 * me[1] + me[2]], sems[2].at[j]).start()
            self._copy(0, j, me, sibling, ins, outs, sems, src=ins[j]).start()
            for q, chip in enumerate(chips):
                self._copy(1 + q, j, me, (*chip, c), ins, outs, sems, src=ins[j]).start()

    def finish(self, ins, outs, sems):
        me, sibling, chips, c = self._ctx()
        for q, chip in enumerate(chips):
            for j in range(self.n):
                self._copy(1 + q, j, (*chip, c), me, ins, outs, sems).wait_recv()
                self._copy(4 + q, j, (*chip, c), sibling, ins, outs, sems).start()
        for j in range(self.n):
            self._copy(0, j, sibling, me, ins, outs, sems).wait_recv()
            for q, chip in enumerate(chips):
                self._copy(4 + q, j, (*chip, 1 - c), me, ins, outs, sems).wait_recv()
        for j in range(self.n):
            self._copy(0, j, me, sibling, ins, outs, sems, src=ins[j]).wait_send()
            for q, chip in enumerate(chips):
                self._copy(1 + q, j, me, (*chip, c), ins, outs, sems, src=ins[j]).wait_send()
                self._copy(4 + q, j, (*chip, c), sibling, ins, outs, sems).wait_send()
            pltpu.make_async_copy(ins[j], outs[j].at[0], sems[2].at[j]).wait()


class _GradCarry:
    def __init__(self, pieces):
        self.inputs = list(pieces)
        self.n = len(pieces)
        self.rows = [p.shape[0] // NDEV for p in pieces]
        self.out_shape = [jax.ShapeDtypeStruct((NDEV, r, p.shape[1]), p.dtype) for p, r in zip(pieces, self.rows)]
        self.scratch = [pltpu.SemaphoreType.DMA((7 * self.n,)), pltpu.SemaphoreType.DMA((7 * self.n,)),
                        pltpu.SemaphoreType.DMA((self.n,))]

    def _copies(self, ins, outs, sems):
        me = _my_index()
        local, remote = [], []
        for j in range(self.n):
            r = self.rows[j]
            local.append(pltpu.make_async_copy(ins[j].at[pl.ds(pl.multiple_of(me * r, 16), r), :], outs[j].at[me],
                                               sems[2].at[j]))
            for k in range(1, NDEV):
                peer = _peer(k)
                pidx = 4 * peer[0] + 2 * peer[1] + peer[2]
                remote.append(pltpu.make_async_remote_copy(
                    src_ref=ins[j].at[pl.ds(pl.multiple_of(pidx * r, 16), r), :], dst_ref=outs[j].at[me],
                    send_sem=sems[0].at[(k - 1) * self.n + j], recv_sem=sems[1].at[(k - 1) * self.n + j],
                    device_id=peer, device_id_type=MESH))
        return local, remote

    def start(self, ins, outs, sems):
        local, remote = self._copies(ins, outs, sems)
        for cp in local + remote:
            cp.start()

    def finish(self, ins, outs, sems):
        local, remote = self._copies(ins, outs, sems)
        for cp in remote + local:
            cp.wait()


def _call(spec, carry=None):
    body, grid = spec["body"], spec["grid"]
    in_specs, out_specs, out_shape = list(spec["in_specs"]), list(spec["out_specs"]), list(spec["out_shape"])
    scratch, args = list(spec.get("scratch", [])), list(spec["args"])
    if carry is None:
        out = pl.pallas_call(body, grid=grid, in_specs=in_specs, out_specs=tuple(out_specs),
                             out_shape=tuple(out_shape), scratch_shapes=scratch, compiler_params=_cp(len(grid)),
                             name=spec["name"])(*args)
        return tuple(out), ()
    n_in, n_out, n_s = len(in_specs), len(out_specs), len(scratch)
    c_in, c_out = len(carry.inputs), len(carry.out_shape)
    steps = int(np.prod(grid))

    def wrapped(*refs):
        o = 0
        ins = refs[o:o + n_in]; o += n_in
        cins = refs[o:o + c_in]; o += c_in
        outs = refs[o:o + n_out]; o += n_out
        couts = refs[o:o + c_out]; o += c_out
        scr = refs[o:o + n_s]; o += n_s
        sems = refs[o:]
        step = pl.program_id(0)
        for ax in range(1, len(grid)):
            step = step * grid[ax] + pl.program_id(ax)

        @pl.when(step == 0)
        def _():
            carry.start(cins, couts, sems)
        body(*ins, *outs, *scr)

        @pl.when(step == steps - 1)
        def _():
            carry.finish(cins, couts, sems)

    out = pl.pallas_call(
        wrapped, grid=grid, in_specs=in_specs + [ANY] * c_in, out_specs=tuple(out_specs + [ANY] * c_out),
        out_shape=tuple(out_shape + carry.out_shape), scratch_shapes=scratch + carry.scratch,
        compiler_params=_cp(len(grid)), name=spec["name"])(*args, *carry.inputs)
    return tuple(out[:n_out]), tuple(out[n_out:])


def _exchange_only(carry, name):
    spec = dict(body=lambda: None, grid=(1,), in_specs=[], out_specs=[], out_shape=[], args=[], name=name)
    return _call(spec, carry)[1]


def _all_reduce_small(part, name):
    r, lanes = part.shape

    def body(p_ref, o_ref, buf, send_sems, recv_sems):
        me = _my_index()
        buf[me] = p_ref[...]
        cps = []
        for k in range(1, NDEV):
            cp = pltpu.make_async_remote_copy(
                src_ref=p_ref, dst_ref=buf.at[me], send_sem=send_sems.at[k - 1], recv_sem=recv_sems.at[k - 1],
                device_id=_peer(k), device_id_type=MESH)
            cp.start()
            cps.append(cp)
        for cp in cps:
            cp.wait()
        acc = buf[0]
        for d in range(1, NDEV):
            acc = acc + buf[d]
        o_ref[...] = acc

    return pl.pallas_call(
        body, out_shape=jax.ShapeDtypeStruct((r, lanes), F32),
        in_specs=[pl.BlockSpec(memory_space=pltpu.VMEM)], out_specs=pl.BlockSpec(memory_space=pltpu.VMEM),
        scratch_shapes=[pltpu.VMEM((NDEV, r, lanes), F32), pltpu.SemaphoreType.DMA((7,)), pltpu.SemaphoreType.DMA((7,))],
        name=name)(part)


def _rms_fwd(x, gain):
    r = lax.rsqrt(jnp.mean(x * x, axis=-1, keepdims=True) + EPS)
    return x * r * gain, r


def _rms_bwd(dh, x, r, gain):
    a = dh * gain
    dx = r * a - x * (r * r * r) * jnp.mean(a * x, axis=-1, keepdims=True)
    dgain = jnp.sum(dh * (x * r), axis=0, keepdims=True)
    return dx, dgain


def _gelu(x):
    return 0.5 * x * (1.0 + lax.erf(x * 0.7071067811865476))


def _gelu_grad(x):
    return 0.5 * (1.0 + lax.erf(x * 0.7071067811865476)) + x * jnp.exp(-0.5 * x * x) * 0.3989422804014327


def _sigmoid(x):
    return 1.0 / (1.0 + jnp.exp(-x))


def _adamw_math(w, g, m, v):
    nm = ADAM_B1 * m + (1.0 - ADAM_B1) * g
    nv = ADAM_B2 * v + (1.0 - ADAM_B2) * (g * g)
    m_hat = nm / (1.0 - ADAM_B1 ** ADAM_STEP)
    v_hat = nv / (1.0 - ADAM_B2 ** ADAM_STEP)
    return -ADAM_LR * (m_hat / (jnp.sqrt(v_hat) + ADAM_EPS) + ADAM_WD * w), nm, nv


def _tok(tm, w):
    return pl.BlockSpec((tm, w), lambda i: (i, 0))


def _full(shape):
    return pl.BlockSpec(shape, lambda *i: (0,) * len(shape))


def _norm_proj(x, gain, gath, out_dtype, name, tm):
    t, dm = x.shape
    n = gath.shape[1] * NDEV

    def body(x_ref, g_ref, gath_ref, proj_ref, hb_ref, wbuf, sems):
        @pl.when(pl.program_id(0) == 0)
        def _():
            _load_weight(gath_ref, wbuf, sems)
        h, _ = _rms_fwd(x_ref[...], g_ref[...])
        hb = h.astype(BF16)
        hb_ref[...] = hb
        proj_ref[...] = _dot(hb, wbuf[...], NT).astype(out_dtype)

    return dict(
        body=body, grid=(t // tm,), name=name, args=[x, gain, gath],
        out_shape=[jax.ShapeDtypeStruct((t, n), out_dtype), jax.ShapeDtypeStruct((t, dm), BF16)],
        in_specs=[_tok(tm, dm), _full((1, dm)), ANY], out_specs=[_tok(tm, n), _tok(tm, dm)],
        scratch=[pltpu.VMEM((n, dm), BF16), pltpu.SemaphoreType.DMA((NDEV,))])


def _proj_bwd_norm(dy, x, gain, dres, gath, name, tm):
    t, dm = x.shape
    n = gath.shape[1] * NDEV

    def body(dy_ref, x_ref, g_ref, dres_ref, gath_ref, dx_ref, dxb_ref, dgain_ref, wbuf, sems):
        @pl.when(pl.program_id(0) == 0)
        def _():
            _load_weight(gath_ref, wbuf, sems)
            dgain_ref[...] = jnp.zeros_like(dgain_ref)
        xv, gain_v = x_ref[...], g_ref[...]
        _, r = _rms_fwd(xv, gain_v)
        dh = _dot(dy_ref[...], wbuf[...], NN)
        dx, dgain = _rms_bwd(dh, xv, r, gain_v)
        dx = dres_ref[...] + dx
        dx_ref[...] = dx
        dxb_ref[...] = dx.astype(BF16)
        dgain_ref[...] += dgain

    return dict(
        body=body, grid=(t // tm,), name=name, args=[dy, x, gain, dres, gath],
        out_shape=[jax.ShapeDtypeStruct((t, dm), F32), jax.ShapeDtypeStruct((t, dm), BF16),
                   jax.ShapeDtypeStruct((1, dm), F32)],
        in_specs=[_tok(tm, n), _tok(tm, dm), _full((1, dm)), _tok(tm, dm), ANY],
        out_specs=[_tok(tm, dm), _tok(tm, dm), _full((1, dm))],
        scratch=[pltpu.VMEM((n, dm), BF16), pltpu.SemaphoreType.DMA((NDEV,))])


def _wgrad(a, b, name, tmm=256):
    t, m = a.shape
    n = b.shape[1]

    def body(a_ref, b_ref, o_ref):
        o_ref[...] = _dot(a_ref[...], b_ref[...], TN).astype(BF16)

    return dict(
        body=body, grid=(m // tmm,), name=name, args=[a, b], out_shape=[jax.ShapeDtypeStruct((m, n), BF16)],
        in_specs=[pl.BlockSpec((t, tmm), lambda j: (0, j)), pl.BlockSpec((t, n), lambda j: (0, 0))],
        out_specs=[pl.BlockSpec((tmm, n), lambda j: (j, 0))])


def _halo_specs(tm, t, width, col_blocks):
    nb8 = tm // 8
    last = t // 8 - 1
    prev = [pl.BlockSpec((8, width), lambda i, cb=cb: (jnp.maximum(i * nb8 - 1, 0), cb)) for cb in col_blocks]
    nxt = [pl.BlockSpec((8, width), lambda i, cb=cb: (jnp.minimum((i + 1) * nb8, last), cb)) for cb in col_blocks]
    return prev, nxt


def _shift_rows(z, prev_row, next_row):
    tm = z.shape[0]
    row = lax.broadcasted_iota(jnp.int32, z.shape, 0)
    zm1 = jnp.where(row == 0, prev_row, pltpu.roll(z, 1, 0))
    zp1 = jnp.where(row == tm - 1, next_row, pltpu.roll(z, tm - 1, 0))
    return zm1, zp1


def _gating_fwd(proj, lng, lnb, wsp_ref, bsp_ref, aw):
    tm = proj.shape[0]
    a_u = _gelu(proj[:, 0:aw])
    gv = _gelu(proj[:, aw:2 * aw])
    mu = jnp.mean(gv, axis=-1, keepdims=True)
    xc = gv - mu
    rstd = lax.rsqrt(jnp.mean(xc * xc, axis=-1, keepdims=True) + EPS)
    vn = xc * rstd
    a_v = (vn * lng + lnb).astype(BF16)
    gd = aw // A_GROUPS
    rows = []
    for c in range(tm // CHUNK):
        cols = []
        for g in range(A_GROUPS):
            blk = a_v[c * CHUNK:(c + 1) * CHUNK, g * gd:(g + 1) * gd]
            cols.append(_dot(wsp_ref[g], blk, NN) + bsp_ref[g])
        rows.append(jnp.concatenate(cols, axis=1))
    mixed = jnp.concatenate(rows, axis=0)
    return a_u, vn, rstd, a_v, mixed


def _even_core_fwd(proj, x0, lng, lnb, wsp, bspb, cw, gath, tm):
    t, dm = x0.shape
    aw = lng.shape[1]
    bw = cw.shape[1]
    assert aw == bw and 2 * aw + 3 * bw == proj.shape[1]
    nt = t // tm
    prev, nxt = _halo_specs(tm, t, bw, [3, 4])

    def body(proj_ref, cp_ref, hp_ref, cn_ref, hn_ref, x0_ref, lng_ref, lnb_ref, wsp_ref, bsp_ref, cw_ref, gath_ref,
             x1_ref, y_ref, wbuf, sems):
        i = pl.program_id(0)

        @pl.when(i == 0)
        def _():
            _load_weight(gath_ref, wbuf, sems)
        proj_v = proj_ref[...]
        a_u, _, _, _, mixed = _gating_fwd(proj_v, lng_ref[...], lnb_ref[...], wsp_ref, bsp_ref, aw)
        a_out = a_u * mixed
        bb = proj_v[:, 2 * aw:2 * aw + bw]
        z = proj_v[:, 2 * aw + bw:2 * aw + 2 * bw] * proj_v[:, 2 * aw + 2 * bw:]
        zprev = jnp.where(i > 0, cp_ref[7:8, :] * hp_ref[7:8, :], 0.0)
        znext = jnp.where(i < nt - 1, cn_ref[0:1, :] * hn_ref[0:1, :], 0.0)
        zm1, zp1 = _shift_rows(z, zprev, znext)
        cwv = cw_ref[...]
        conv = zm1 * cwv[0:1, :] + z * cwv[1:2, :] + zp1 * cwv[2:3, :]
        y = jnp.concatenate([a_out, bb * conv], axis=1).astype(BF16)
        y_ref[...] = y
        x1_ref[...] = x0_ref[...] + _dot(y, wbuf[...], NN)

    return dict(
        body=body, grid=(nt,), name="even_core_fwd",
        args=[proj, proj, proj, proj, proj, x0, lng, lnb, wsp, bspb, cw, gath],
        out_shape=[jax.ShapeDtypeStruct((t, dm), F32), jax.ShapeDtypeStruct((t, aw + bw), BF16)],
        in_specs=[_tok(tm, proj.shape[1]), prev[0], prev[1], nxt[0], nxt[1], _tok(tm, dm), _full(lng.shape),
                  _full(lnb.shape), _full(wsp.shape), _full(bspb.shape), _full(cw.shape), ANY],
        out_specs=[_tok(tm, dm), _tok(tm, aw + bw)],
        scratch=[pltpu.VMEM((gath.shape[1] * NDEV, dm), BF16), pltpu.SemaphoreType.DMA((NDEV,))])


def _even_core_bwd(proj, dx1, lng, lnb, wsp, bspb, cw, gath, tm):
    t, dm = dx1.shape
    aw, bw = lng.shape[1], cw.shape[1]
    gd = aw // A_GROUPS
    nt = t // tm
    inw = proj.shape[1]
    prev, nxt = _halo_specs(tm, t, bw, [2, 3, 4])
    nb8 = tm // 8
    last8 = t // 8 - 1

    def body(proj_ref, bp_ref, cp_ref, hp_ref, bn_ref, cn_ref, hn_ref, dx_ref, dxp_ref, dxn_ref,
             lng_ref, lnb_ref, wsp_ref, bsp_ref, cw_ref, gath_ref,
             dproj_ref, dlng_ref, dlnb_ref, dwsp_ref, dbsp_ref, dcw_ref, wbuf, sems):
        i = pl.program_id(0)

        @pl.when(i == 0)
        def _():
            _load_weight(gath_ref, wbuf, sems)
            dlng_ref[...] = jnp.zeros_like(dlng_ref)
            dlnb_ref[...] = jnp.zeros_like(dlnb_ref)
            dwsp_ref[...] = jnp.zeros_like(dwsp_ref)
            dbsp_ref[...] = jnp.zeros_like(dbsp_ref)
            dcw_ref[...] = jnp.zeros_like(dcw_ref)
        proj_v = proj_ref[...]
        lng_v = lng_ref[...]
        a_u, vn, rstd, a_v, mixed = _gating_fwd(proj_v, lng_v, lnb_ref[...], wsp_ref, bsp_ref, aw)
        w = wbuf[...]
        dy = _dot(dx_ref[...].astype(BF16), w, NT)
        da_out, db_out = dy[:, 0:aw], dy[:, aw:]
        da_u = da_out * mixed
        dmixed = da_out * a_u
        dmb = dmixed.astype(BF16)
        rows = []
        for c in range(tm // CHUNK):
            cols = []
            for g in range(A_GROUPS):
                r0, c0 = c * CHUNK, g * gd
                dm_cg = dmb[r0:r0 + CHUNK, c0:c0 + gd]
                cols.append(_dot(wsp_ref[g], dm_cg, TN))
                dwsp_ref[g] += _dot(dm_cg, a_v[r0:r0 + CHUNK, c0:c0 + gd], NT)
                dbsp_ref[g] += dmixed[r0:r0 + CHUNK, c0:c0 + gd]
            rows.append(jnp.concatenate(cols, axis=1))
        dav = jnp.concatenate(rows, axis=0)
        dlng_ref[...] += jnp.sum(dav * vn, axis=0, keepdims=True)
        dlnb_ref[...] += jnp.sum(dav, axis=0, keepdims=True)
        dvn = dav * lng_v
        dgv = rstd * (dvn - jnp.mean(dvn, axis=-1, keepdims=True) - vn * jnp.mean(dvn * vn, axis=-1, keepdims=True))
        dv_pre = dgv * _gelu_grad(proj_v[:, aw:2 * aw])
        du_pre = da_u * _gelu_grad(proj_v[:, 0:aw])
        bb = proj_v[:, 2 * aw:2 * aw + bw]
        bc = proj_v[:, 2 * aw + bw:2 * aw + 2 * bw]
        bh = proj_v[:, 2 * aw + 2 * bw:]
        z = bc * bh
        zprev = jnp.where(i > 0, cp_ref[7:8, :] * hp_ref[7:8, :], 0.0)
        znext = jnp.where(i < nt - 1, cn_ref[0:1, :] * hn_ref[0:1, :], 0.0)
        zm1, zp1 = _shift_rows(z, zprev, znext)
        cwv = cw_ref[...]
        conv = zm1 * cwv[0:1, :] + z * cwv[1:2, :] + zp1 * cwv[2:3, :]
        dbb = db_out * conv
        dconv = db_out * bb
        dx_edge = jnp.concatenate([dxp_ref[...], dxn_ref[...]], axis=0).astype(BF16)
        dy_edge = _dot(dx_edge, w[aw:, :], NT)
        dcprev = jnp.where(i > 0, dy_edge[7:8, :] * bp_ref[7:8, :], 0.0)
        dcnext = jnp.where(i < nt - 1, dy_edge[8:9, :] * bn_ref[0:1, :], 0.0)
        dcm1, dcp1 = _shift_rows(dconv, dcprev, dcnext)
        dz = dcp1 * cwv[0:1, :] + dconv * cwv[1:2, :] + dcm1 * cwv[2:3, :]
        dcw_ref[0:1, :] += jnp.sum(dconv * zm1, axis=0, keepdims=True)
        dcw_ref[1:2, :] += jnp.sum(dconv * z, axis=0, keepdims=True)
        dcw_ref[2:3, :] += jnp.sum(dconv * zp1, axis=0, keepdims=True)
        dproj_ref[...] = jnp.concatenate([du_pre, dv_pre, dbb, dz * bh, dz * bc], axis=1).astype(BF16)

    row8 = lambda f: pl.BlockSpec((8, dm), f)
    return dict(
        body=body, grid=(nt,), name="even_core_bwd",
        args=[proj, proj, proj, proj, proj, proj, proj, dx1, dx1, dx1, lng, lnb, wsp, bspb, cw, gath],
        out_shape=[jax.ShapeDtypeStruct((t, inw), BF16), jax.ShapeDtypeStruct((1, aw), F32),
                   jax.ShapeDtypeStruct((1, aw), F32), jax.ShapeDtypeStruct(wsp.shape, F32),
                   jax.ShapeDtypeStruct((A_GROUPS, CHUNK, gd), F32), jax.ShapeDtypeStruct(cw.shape, F32)],
        in_specs=[_tok(tm, inw), prev[0], prev[1], prev[2], nxt[0], nxt[1], nxt[2], _tok(tm, dm),
                  row8(lambda i: (jnp.maximum(i * nb8 - 1, 0), 0)), row8(lambda i: (jnp.minimum((i + 1) * nb8, last8), 0)),
                  _full(lng.shape), _full(lnb.shape), _full(wsp.shape), _full(bspb.shape), _full(cw.shape), ANY],
        out_specs=[_tok(tm, inw), _full((1, aw)), _full((1, aw)), _full(wsp.shape),
                   _full((A_GROUPS, CHUNK, gd)), _full(cw.shape)],
        scratch=[pltpu.VMEM((gath.shape[1] * NDEV, dm), BF16), pltpu.SemaphoreType.DMA((NDEV,))])


def _ff_chunks(f, width=1024):
    return [(c0, min(c0 + width, f)) for c0 in range(0, f, width)]


def _ffn_up(x, gain, gath_g, gath_u, name, tm):
    t, dm = x.shape
    f = gath_g.shape[1] * NDEV

    def body(x_ref, g_ref, gg_ref, gu_ref, gate_ref, up_ref, act_ref, wg, wu, sems):
        @pl.when(pl.program_id(0) == 0)
        def _():
            _load_weight(gg_ref, wg, sems)
            _load_weight(gu_ref, wu, sems)
        h, _ = _rms_fwd(x_ref[...], g_ref[...])
        hb = h.astype(BF16)
        for c0, c1 in _ff_chunks(f):
            gate = _dot(hb, wg[c0:c1, :], NT)
            up = _dot(hb, wu[c0:c1, :], NT)
            gate_ref[:, c0:c1] = gate.astype(BF16)
            up_ref[:, c0:c1] = up.astype(BF16)
            act_ref[:, c0:c1] = (gate * _sigmoid(gate) * up).astype(BF16)

    o = jax.ShapeDtypeStruct((t, f), BF16)
    return dict(
        body=body, grid=(t // tm,), name=name, args=[x, gain, gath_g, gath_u], out_shape=[o, o, o],
        in_specs=[_tok(tm, dm), _full((1, dm)), ANY, ANY], out_specs=[_tok(tm, f)] * 3,
        scratch=[pltpu.VMEM((f, dm), BF16), pltpu.VMEM((f, dm), BF16), pltpu.SemaphoreType.DMA((NDEV,))])


def _ffn_down(x, act, gath_d, name, tm):
    t, dm = x.shape
    f = act.shape[1]

    def body(x_ref, a_ref, gd_ref, xo_ref, wd, sems):
        @pl.when(pl.program_id(0) == 0)
        def _():
            _load_weight(gd_ref, wd, sems)
        xo_ref[...] = x_ref[...] + _dot(a_ref[...], wd[...], NN)

    return dict(
        body=body, grid=(t // tm,), name=name, args=[x, act, gath_d], out_shape=[jax.ShapeDtypeStruct((t, dm), F32)],
        in_specs=[_tok(tm, dm), _tok(tm, f), ANY], out_specs=[_tok(tm, dm)],
        scratch=[pltpu.VMEM((f, dm), BF16), pltpu.SemaphoreType.DMA((NDEV,))])


def _ffn_bwd(dxo, x, gate, up, gain, gath_g, gath_u, gath_d, name, tm):
    t, dm = x.shape
    f = gate.shape[1]

    def body(dxo_ref, x_ref, gate_ref, up_ref, g_ref, gg_ref, gu_ref, gd_ref,
             dx_ref, dxb_ref, dg_ref, du_ref, hb_ref, dgain_ref, wg, wu, wd, sems):
        @pl.when(pl.program_id(0) == 0)
        def _():
            _load_weight(gg_ref, wg, sems)
            _load_weight(gu_ref, wu, sems)
            _load_weight(gd_ref, wd, sems)
            dgain_ref[...] = jnp.zeros_like(dgain_ref)
        xv, gain_v, dxo_v = x_ref[...], g_ref[...], dxo_ref[...]
        h, r = _rms_fwd(xv, gain_v)
        hb_ref[...] = h.astype(BF16)
        dxob = dxo_v.astype(BF16)
        dh = jnp.zeros_like(xv)
        for c0, c1 in _ff_chunks(f):
            gate_v = gate_ref[:, c0:c1].astype(F32)
            up_v = up_ref[:, c0:c1].astype(F32)
            s = _sigmoid(gate_v)
            silu = gate_v * s
            dact = _dot(dxob, wd[c0:c1, :], NT)
            dg = (dact * up_v * (s * (1.0 + gate_v * (1.0 - s)))).astype(BF16)
            du = (dact * silu).astype(BF16)
            dg_ref[:, c0:c1] = dg
            du_ref[:, c0:c1] = du
            dh = dh + _dot(dg, wg[c0:c1, :], NN) + _dot(du, wu[c0:c1, :], NN)
        dx, dgain = _rms_bwd(dh, xv, r, gain_v)
        dx = dxo_v + dx
        dx_ref[...] = dx
        dxb_ref[...] = dx.astype(BF16)
        dgain_ref[...] += dgain

    return dict(
        body=body, grid=(t // tm,), name=name, args=[dxo, x, gate, up, gain, gath_g, gath_u, gath_d],
        out_shape=[jax.ShapeDtypeStruct((t, dm), F32), jax.ShapeDtypeStruct((t, dm), BF16),
                   jax.ShapeDtypeStruct((t, f), BF16), jax.ShapeDtypeStruct((t, f), BF16),
                   jax.ShapeDtypeStruct((t, dm), BF16), jax.ShapeDtypeStruct((1, dm), F32)],
        in_specs=[_tok(tm, dm), _tok(tm, dm), _tok(tm, f), _tok(tm, f), _full((1, dm)), ANY, ANY, ANY],
        out_specs=[_tok(tm, dm), _tok(tm, dm), _tok(tm, f), _tok(tm, f), _tok(tm, dm), _full((1, dm))],
        scratch=[pltpu.VMEM((f, dm), BF16), pltpu.VMEM((f, dm), BF16), pltpu.VMEM((f, dm), BF16),
                 pltpu.SemaphoreType.DMA((NDEV,))])


def _t5_buckets(rel):
    nb = N_BUCKETS // 2
    ret = jnp.where(rel > 0, nb, 0)
    n = jnp.abs(rel)
    max_exact = nb // 2
    nf = jnp.maximum(n, 1).astype(jnp.float32)
    large = max_exact + (jnp.log(nf / max_exact) / math.log(MAX_DISTANCE / max_exact)
                         * (nb - max_exact)).astype(jnp.int32)
    large = jnp.minimum(large, nb - 1)
    return ret + jnp.where(n < max_exact, n, large)


def _bucket_table():
    qi = jnp.arange(CHUNK, dtype=jnp.int32)[:, None]
    kj = jnp.arange(3 * CHUNK, dtype=jnp.int32)[None, :]
    rel = kj - CHUNK - qi
    return jnp.where(jnp.abs(rel) <= CHUNK, _t5_buckets(rel), -1)


def _bias_table(rel_bias_t, buckets):
    nh = rel_bias_t.shape[0]

    def body(rb_ref, bk_ref, o_ref):
        bk = bk_ref[...]
        for h in range(nh):
            acc = jnp.where(bk < 0, NEG, 0.0).astype(F32)
            for b in range(N_BUCKETS):
                acc = jnp.where(bk == b, rb_ref[h, b], acc)
            o_ref[h] = acc

    return pl.pallas_call(
        body, out_shape=jax.ShapeDtypeStruct((nh,) + buckets.shape, F32),
        in_specs=[pl.BlockSpec(memory_space=pltpu.SMEM), pl.BlockSpec(memory_space=pltpu.VMEM)],
        out_specs=pl.BlockSpec(memory_space=pltpu.VMEM), name="bias_table")(rel_bias_t, buckets)


def _rel_bias_grad(dbias, buckets):
    nh = dbias.shape[0]

    def body(db_ref, bk_ref, o_ref):
        bk = bk_ref[...]
        lane = lax.broadcasted_iota(jnp.int32, (1, 128), 1)
        for h in range(nh):
            d = db_ref[h]
            row = jnp.zeros((1, 128), F32)
            for b in range(N_BUCKETS):
                s = jnp.sum(jnp.sum(jnp.where(bk == b, d, 0.0), axis=1, keepdims=True), axis=0, keepdims=True)
                row = jnp.where(lane == b, s, row)
            o_ref[h:h + 1, :] = row

    return pl.pallas_call(
        body, out_shape=jax.ShapeDtypeStruct((nh, 128), F32),
        in_specs=[pl.BlockSpec(memory_space=pltpu.VMEM), pl.BlockSpec(memory_space=pltpu.VMEM)],
        out_specs=pl.BlockSpec(memory_space=pltpu.VMEM), compiler_params=_cp(0), name="rel_bias_grad")(dbias, buckets)


def _half_masks():
    lane = lax.broadcasted_iota(jnp.int32, (CHUNK, 128), 1)
    return lane < HEAD_DIM, lane >= HEAD_DIM


def _kv_halves(blk, hk, lo, hi):
    zero = jnp.zeros_like(blk)
    rolled = pltpu.roll(blk, HEAD_DIM, 1)
    if hk % 2 == 0:
        return jnp.where(lo, blk, zero), jnp.where(hi, rolled, zero)
    return jnp.where(lo, rolled, zero), jnp.where(hi, blk, zero)


def _attn_probs(q2, k_lo, k_hi, bias_ref, sink_ref, hk, n, nblk):
    scale = HEAD_DIM ** -0.5
    out = {}
    for half, ks in ((0, k_lo), (1, k_hi)):
        s3 = [_dot(q2, ks[jj], NT) * scale for jj in range(3)]
        for tile in range(2):
            h = 4 * hk + 2 * tile + half
            s = []
            for jj in range(3):
                sj = s3[jj][tile * CHUNK:(tile + 1) * CHUNK, :] + bias_ref[h, :, jj * CHUNK:(jj + 1) * CHUNK]
                if jj == 0:
                    sj = jnp.where(n > 0, sj, NEG)
                if jj == 2:
                    sj = jnp.where(n < nblk - 1, sj, NEG)
                s.append(sj)
            sink = sink_ref[h]
            m = jnp.maximum(jnp.max(jnp.maximum(jnp.maximum(s[0], s[1]), s[2]), axis=-1, keepdims=True), sink)
            e = [jnp.exp(sj - m) for sj in s]
            es = jnp.exp(sink - m)
            inv = 1.0 / (jnp.sum(e[0] + e[1] + e[2], axis=-1, keepdims=True) + es)
            out[(half, tile)] = ([ej * inv for ej in e], es * inv)
    return out


def _key_block_starts(n, nblk):
    return [pl.multiple_of(jnp.clip(n - 1 + jj, 0, nblk - 1) * CHUNK, CHUNK) for jj in range(3)]


def _attn_fwd(qkv, x2, bias, sink, gath):
    t, dm = x2.shape
    nblk = t // CHUNK
    kvw = N_KV * HEAD_DIM
    kcb, vcb = dm // kvw, dm // kvw + 1

    def body(q_ref, k_ref, v_ref, x2_ref, bias_ref, sink_ref, gath_ref, x3_ref, att_ref, wbuf, sems):
        n = pl.program_id(0)

        @pl.when(n == 0)
        def _():
            _load_weight(gath_ref, wbuf, sems)
        lo, hi = _half_masks()
        starts = _key_block_starts(n, nblk)
        tiles = []
        for hk in range(N_KV):
            kt = (hk // 2) * 128
            k_lo, k_hi, v_lo, v_hi = [], [], [], []
            for jj in range(3):
                a, b = _kv_halves(k_ref[pl.ds(starts[jj], CHUNK), kt:kt + 128], hk, lo, hi)
                k_lo.append(a)
                k_hi.append(b)
                a, b = _kv_halves(v_ref[pl.ds(starts[jj], CHUNK), kt:kt + 128], hk, lo, hi)
                v_lo.append(a)
                v_hi.append(b)
            q2 = jnp.concatenate([q_ref[:, (2 * hk) * 128:(2 * hk + 1) * 128],
                                  q_ref[:, (2 * hk + 1) * 128:(2 * hk + 2) * 128]], axis=0)
            pr = _attn_probs(q2, k_lo, k_hi, bias_ref, sink_ref, hk, n, nblk)
            o2 = jnp.zeros((2 * CHUNK, 128), F32)
            for half, vs in ((0, v_lo), (1, v_hi)):
                for jj in range(3):
                    p2 = jnp.concatenate([pr[(half, 0)][0][jj], pr[(half, 1)][0][jj]], axis=0).astype(BF16)
                    o2 = o2 + _dot(p2, vs[jj], NN)
            tiles += [o2[0:CHUNK], o2[CHUNK:]]
        att = jnp.concatenate(tiles, axis=1).astype(BF16)
        att_ref[...] = att
        x3_ref[...] = x2_ref[...] + _dot(att, wbuf[...], NN)

    blk = pl.BlockSpec((CHUNK, dm), lambda n: (n, 0))
    return dict(
        body=body, grid=(nblk,), name="attn_fwd", args=[qkv, qkv, qkv, x2, bias, sink, gath],
        out_shape=[jax.ShapeDtypeStruct((t, dm), F32), jax.ShapeDtypeStruct((t, dm), BF16)],
        in_specs=[blk, pl.BlockSpec((t, kvw), lambda n: (0, kcb)), pl.BlockSpec((t, kvw), lambda n: (0, vcb)), blk,
                  _full(bias.shape), pl.BlockSpec(memory_space=pltpu.SMEM), ANY],
        out_specs=[blk, blk],
        scratch=[pltpu.VMEM((gath.shape[1] * NDEV, dm), BF16), pltpu.SemaphoreType.DMA((NDEV,))])


def _attn_bwd(qkv, att, dx3, bias, sink, gath):
    t, dm = dx3.shape
    nblk = t // CHUNK
    kvw = N_KV * HEAD_DIM
    kcb, vcb = dm // kvw, dm // kvw + 1
    scale = HEAD_DIM ** -0.5

    def body(q_ref, k_ref, v_ref, att_ref, dx_ref, bias_ref, sink_ref, gath_ref,
             dq_ref, dk_ref, dv_ref, dbias_ref, dsink_ref, wbuf, sems):
        n = pl.program_id(0)

        @pl.when(n == 0)
        def _():
            _load_weight(gath_ref, wbuf, sems)
            dk_ref[...] = jnp.zeros_like(dk_ref)
            dv_ref[...] = jnp.zeros_like(dv_ref)
            dbias_ref[...] = jnp.zeros_like(dbias_ref)
            dsink_ref[...] = jnp.zeros_like(dsink_ref)
        lo, hi = _half_masks()
        lane1 = lax.broadcasted_iota(jnp.int32, (1, 128), 1)
        starts = _key_block_starts(n, nblk)
        dout = _dot(dx_ref[...].astype(BF16), wbuf[...], NT)
        prod = dout * att_ref[...].astype(F32)
        doutb = dout.astype(BF16)
        dq_tiles = []
        dsink_row = jnp.zeros((1, 128), F32)
        for hk in range(N_KV):
            kt = (hk // 2) * 128
            k_lo, k_hi, v_lo, v_hi = [], [], [], []
            for jj in range(3):
                a, b = _kv_halves(k_ref[pl.ds(starts[jj], CHUNK), kt:kt + 128], hk, lo, hi)
                k_lo.append(a)
                k_hi.append(b)
                a, b = _kv_halves(v_ref[pl.ds(starts[jj], CHUNK), kt:kt + 128], hk, lo, hi)
                v_lo.append(a)
                v_hi.append(b)
            c0 = (2 * hk) * 128
            q2 = jnp.concatenate([q_ref[:, c0:c0 + 128], q_ref[:, c0 + 128:c0 + 256]], axis=0)
            do2 = jnp.concatenate([doutb[:, c0:c0 + 128], doutb[:, c0 + 128:c0 + 256]], axis=0)
            pr = _attn_probs(q2, k_lo, k_hi, bias_ref, sink_ref, hk, n, nblk)
            dq2 = jnp.zeros((2 * CHUNK, 128), F32)
            dk_acc = [None] * 3
            dv_acc = [None] * 3
            for half, ks, vs, msk in ((0, k_lo, v_lo, lo), (1, k_hi, v_hi, hi)):
                dsum = []
                for tile in range(2):
                    pt = prod[:, c0 + tile * 128:c0 + (tile + 1) * 128]
                    dsum.append(jnp.sum(jnp.where(msk, pt, 0.0), axis=-1, keepdims=True))
                    h = 4 * hk + 2 * tile + half
                    contrib = -jnp.sum(pr[(half, tile)][1] * dsum[tile], axis=0, keepdims=True)
                    dsink_row = dsink_row + jnp.where(lane1 == h, contrib, 0.0)
                for jj in range(3):
                    dp2 = _dot(do2, vs[jj], NT)
                    ds_t = []
                    for tile in range(2):
                        p = pr[(half, tile)][0][jj]
                        ds = p * (dp2[tile * CHUNK:(tile + 1) * CHUNK, :] - dsum[tile])
                        h = 4 * hk + 2 * tile + half
                        dbias_ref[h, :, jj * CHUNK:(jj + 1) * CHUNK] += ds
                        ds_t.append(ds)
                    ds2 = jnp.concatenate(ds_t, axis=0).astype(BF16)
                    p2 = jnp.concatenate([pr[(half, 0)][0][jj], pr[(half, 1)][0][jj]], axis=0).astype(BF16)
                    dq2 = dq2 + _dot(ds2, ks[jj], NN) * scale
                    dkj = jnp.where(msk, _dot(ds2, q2, TN) * scale, 0.0)
                    dvj = jnp.where(msk, _dot(p2, do2, TN), 0.0)
                    dk_acc[jj] = dkj if dk_acc[jj] is None else dk_acc[jj] + dkj
                    dv_acc[jj] = dvj if dv_acc[jj] is None else dv_acc[jj] + dvj
            for jj in range(3):
                keep = lo if hk % 2 == 0 else hi
                dkj = dk_acc[jj] + pltpu.roll(dk_acc[jj], HEAD_DIM, 1)
                dvj = dv_acc[jj] + pltpu.roll(dv_acc[jj], HEAD_DIM, 1)
                dk_ref[pl.ds(starts[jj], CHUNK), kt:kt + 128] += jnp.where(keep, dkj, 0.0)
                dv_ref[pl.ds(starts[jj], CHUNK), kt:kt + 128] += jnp.where(keep, dvj, 0.0)
            dq_tiles += [dq2[0:CHUNK], dq2[CHUNK:]]
        dq_ref[...] = jnp.concatenate(dq_tiles, axis=1).astype(BF16)
        dsink_ref[...] += dsink_row

    blk = pl.BlockSpec((CHUNK, dm), lambda n: (n, 0))
    return dict(
        body=body, grid=(nblk,), name="attn_bwd", args=[qkv, qkv, qkv, att, dx3, bias, sink, gath],
        out_shape=[jax.ShapeDtypeStruct((t, dm), BF16), jax.ShapeDtypeStruct((t, kvw), F32),
                   jax.ShapeDtypeStruct((t, kvw), F32), jax.ShapeDtypeStruct(bias.shape, F32),
                   jax.ShapeDtypeStruct((1, 128), F32)],
        in_specs=[blk, pl.BlockSpec((t, kvw), lambda n: (0, kcb)), pl.BlockSpec((t, kvw), lambda n: (0, vcb)),
                  blk, blk, _full(bias.shape), pl.BlockSpec(memory_space=pltpu.SMEM), ANY],
        out_specs=[blk, _full((t, kvw)), _full((t, kvw)), _full(bias.shape), _full((1, 128))],
        scratch=[pltpu.VMEM((gath.shape[1] * NDEV, dm), BF16), pltpu.SemaphoreType.DMA((NDEV,))])


def _final_loss(x4, target, gain, tm):
    t, dm = x4.shape
    steps = t // tm

    def body(x_ref, t_ref, g_ref, loss_ref, dx_ref, dxb_ref, dgain_ref, acc):
        i = pl.program_id(0)

        @pl.when(i == 0)
        def _():
            acc[...] = jnp.zeros_like(acc)
            dgain_ref[...] = jnp.zeros_like(dgain_ref)
        xv, gain_v = x_ref[...], g_ref[...]
        y, r = _rms_fwd(xv, gain_v)
        e = y - t_ref[...]
        acc[...] += jnp.sum(e * e, axis=0, keepdims=True)
        dx, dgain = _rms_bwd(e * (1.0 / dm), xv, r, gain_v)
        dx_ref[...] = dx
        dxb_ref[...] = dx.astype(BF16)
        dgain_ref[...] += dgain

        @pl.when(i == steps - 1)
        def _():
            loss_ref[...] = jnp.sum(acc[...], axis=-1, keepdims=True) * (0.5 / dm)

    return dict(
        body=body, grid=(steps,), name="final_loss", args=[x4, target, gain],
        out_shape=[jax.ShapeDtypeStruct((1, 1), F32), jax.ShapeDtypeStruct((t, dm), F32),
                   jax.ShapeDtypeStruct((t, dm), BF16), jax.ShapeDtypeStruct((1, dm), F32)],
        in_specs=[_tok(tm, dm), _tok(tm, dm), _full((1, dm))],
        out_specs=[_full((1, 1)), _tok(tm, dm), _tok(tm, dm), _full((1, dm))],
        scratch=[pltpu.VMEM((1, dm), F32)])


def _finish_weight(recvs, w, m, v, transposed, name):
    nl, a, b = w.shape
    assert nl == len(recvs)
    r, dm = recvs[0].shape[1], recvs[0].shape[2]
    td = dm // 2
    if transposed:
        assert (a, b) == (dm, r)
        wspec = pl.BlockSpec((None, td, r), lambda l, j: (l, j, 0))
    else:
        assert (a, b) == (r, dm)
        wspec = pl.BlockSpec((None, r, td), lambda l, j: (l, 0, j))

    def body(*refs):
        r_refs = refs[:nl]
        w_ref, m_ref, v_ref, g_ref, d_ref, nm_ref, nv_ref = refs[nl:]
        layer = pl.program_id(0)
        for li in range(nl):
            @pl.when(layer == li)
            def _():
                g = r_refs[li][0].astype(F32)
                for d in range(1, NDEV):
                    g = g + r_refs[li][d].astype(F32)
                if transposed:
                    g = g.T
                delta, nm, nv = _adamw_math(w_ref[...], g, m_ref[...], v_ref[...])
                g_ref[...] = g
                d_ref[...] = delta
                nm_ref[...] = nm
                nv_ref[...] = nv

    o = jax.ShapeDtypeStruct(w.shape, F32)
    out = pl.pallas_call(
        body, grid=(nl, 2), out_shape=(o, o, o, o),
        in_specs=[pl.BlockSpec((NDEV, r, td), lambda l, j: (0, 0, j))] * nl + [wspec] * 3,
        out_specs=(wspec,) * 4, compiler_params=_cp(2), name=name)(*recvs, w, m, v)
    return out


def _adamw(w, g, m, v, name):
    r, c = w.shape

    def body(w_ref, g_ref, m_ref, v_ref, d_ref, nm_ref, nv_ref):
        d_ref[...], nm_ref[...], nv_ref[...] = _adamw_math(w_ref[...], g_ref[...], m_ref[...], v_ref[...])

    spec = pl.BlockSpec((r, c), lambda i: (0, 0))
    out = jax.ShapeDtypeStruct((r, c), F32)
    return pl.pallas_call(
        body, grid=(1,), out_shape=(out, out, out), in_specs=[spec] * 4, out_specs=(spec,) * 3,
        compiler_params=_cp(), name=name)(w, g, m, v)


def _pack_small(parts, rows):
    flat = jnp.concatenate([p.reshape(-1) for p in parts])
    return jnp.pad(flat, (0, rows * 128 - flat.shape[0])).reshape(rows, 128)


def _unpack_small(packed, shapes):
    flat = packed.reshape(-1)
    out, o = [], 0
    for s in shapes:
        n = int(np.prod(s))
        out.append(flat[o:o + n].reshape(s))
        o += n
    return out


def kernel(x, norm_mix, norm_ffn, even_w_in, even_v_ln_g, even_v_ln_b, even_w_spatial, even_b_spatial, even_conv_w, even_w_out, attn_w_qkv, attn_sink, rel_bias, attn_w_out, ffn_w_gate, ffn_w_up, ffn_w_down, final_norm, loss_target, m_norm_mix, m_norm_ffn, m_even_w_in, m_even_v_ln_g, m_even_v_ln_b, m_even_w_spatial, m_even_b_spatial, m_even_conv_w, m_even_w_out, m_attn_w_qkv, m_attn_sink, m_rel_bias, m_attn_w_out, m_ffn_w_gate, m_ffn_w_up, m_ffn_w_down, m_final_norm, v_norm_mix, v_norm_ffn, v_even_w_in, v_even_v_ln_g, v_even_v_ln_b, v_even_w_spatial, v_even_b_spatial, v_even_conv_w, v_even_w_out, v_attn_w_qkv, v_attn_sink, v_rel_bias, v_attn_w_out, v_ffn_w_gate, v_ffn_w_up, v_ffn_w_down, v_final_norm):
    t, dm = x.shape[1], x.shape[2]
    aw = even_v_ln_g.shape[1]
    bw = even_conv_w.shape[2] * NDEV
    gd = aw // A_GROUPS
    tm = min(512, t // 2)
    tmf = min(256, t // 2)
    me = _my_index()
    row = lambda a: a.reshape(1, -1)

    colT = lambda w: w.T.astype(BF16)
    sh = dict(winT=colT(even_w_in[0]), wqkvT=colT(attn_w_qkv[0]), wgT0=colT(ffn_w_gate[0]), wuT0=colT(ffn_w_up[0]),
              wgT1=colT(ffn_w_gate[1]), wuT1=colT(ffn_w_up[1]), woe=even_w_out[0].astype(BF16),
              woa=attn_w_out[0].astype(BF16), wd0=ffn_w_down[0].astype(BF16), wd1=ffn_w_down[1].astype(BF16))
    gather = lambda names: _GatherCarry([sh[n] for n in names])

    cw_rows = 3 * bw // 128
    cw_mine = lax.dynamic_update_slice(jnp.zeros((3, bw), F32), even_conv_w[0], (0, me * (bw // NDEV)))
    cw_full = _all_reduce_small(jnp.pad(cw_mine.reshape(cw_rows, 128), ((0, 16 - cw_rows), (0, 0))),
                                "gather_conv_w")[0:cw_rows].reshape(3, bw)

    x0 = x[0]
    wsp_b = even_w_spatial[0].astype(BF16)
    bspb = jnp.broadcast_to(even_b_spatial[0][:, :, None], (A_GROUPS, CHUNK, gd))
    buckets = _bucket_table()
    bias = _bias_table(rel_bias.T, buckets)
    sink = attn_sink[0]

    (g_winT,) = _exchange_only(gather(["winT"]), "ag_w_in")
    (proj, h0b), (g_woe, g_wgT0) = _call(_norm_proj(x0, row(norm_mix[0]), g_winT, F32, "in_proj", tm),
                                         gather(["woe", "wgT0"]))
    (x1, yb), (g_wuT0,) = _call(_even_core_fwd(proj, x0, even_v_ln_g, even_v_ln_b, wsp_b, bspb, cw_full, g_woe, tm),
                                gather(["wuT0"]))
    (gate0, up0, act0), (g_wd0,) = _call(_ffn_up(x1, row(norm_ffn[0]), g_wgT0, g_wuT0, "ffn_up0", tmf), gather(["wd0"]))
    (x2,), (g_wqkvT,) = _call(_ffn_down(x1, act0, g_wd0, "ffn_down0", tm), gather(["wqkvT"]))
    (qkv, h2b), (g_woa,) = _call(_norm_proj(x2, row(norm_mix[1]), g_wqkvT, BF16, "qkv_proj", tm), gather(["woa"]))
    (x3, attb), (g_wgT1, g_wuT1) = _call(_attn_fwd(qkv, x2, bias, sink, g_woa), gather(["wgT1", "wuT1"]))
    (gate1, up1, act1), (g_wd1,) = _call(_ffn_up(x3, row(norm_ffn[1]), g_wgT1, g_wuT1, "ffn_up1", tmf), gather(["wd1"]))
    (x4,), _ = _call(_ffn_down(x3, act1, g_wd1, "ffn_down1", tm))
    (loss_part, dx4, dx4b, d_final), _ = _call(_final_loss(x4, loss_target[0], row(final_norm), tm))

    (dx3, dx3b, dg1, du1, h3b, d_nffn1), _ = _call(
        _ffn_bwd(dx4, x3, gate1, up1, row(norm_ffn[1]), g_wgT1, g_wuT1, g_wd1, "ffn_bwd1", tmf))
    (p_wgT1,), _ = _call(_wgrad(dg1, h3b, "wgrad_gate1"))
    (p_wuT1,), _ = _call(_wgrad(du1, h3b, "wgrad_up1"))
    (p_wd1,), _ = _call(_wgrad(act1, dx4b, "wgrad_down1"))
    (dq, dk, dv, dbias, dsink), (r_wgT1, r_wuT1, r_wd1) = _call(
        _attn_bwd(qkv, attb, dx3, bias, sink, g_woa), _GradCarry([p_wgT1, p_wuT1, p_wd1]))
    (p_woa,), _ = _call(_wgrad(attb, dx3b, "wgrad_attn_out"))
    d_relb = _rel_bias_grad(dbias, buckets)[:, 0:N_BUCKETS].T
    dqkv = jnp.concatenate([dq, dk.astype(BF16), dv.astype(BF16)], axis=1)
    (dx2, dx2b, d_nmix1), _ = _call(_proj_bwd_norm(dqkv, x2, row(norm_mix[1]), dx3, g_wqkvT, "qkv_bwd", tm))
    (p_wqkvT,), _ = _call(_wgrad(dqkv, h2b, "wgrad_qkv"))
    (dx1, dx1b, dg0, du0, h1b, d_nffn0), (r_woa, r_wqkvT) = _call(
        _ffn_bwd(dx2, x1, gate0, up0, row(norm_ffn[0]), g_wgT0, g_wuT0, g_wd0, "ffn_bwd0", tmf),
        _GradCarry([p_woa, p_wqkvT]))
    (p_wgT0,), _ = _call(_wgrad(dg0, h1b, "wgrad_gate0"))
    (p_wuT0,), _ = _call(_wgrad(du0, h1b, "wgrad_up0"))
    (p_wd0,), _ = _call(_wgrad(act0, dx2b, "wgrad_down0"))
    (dproj, d_lng, d_lnb, d_wsp, d_bsp3, d_cw), (r_wgT0,) = _call(
        _even_core_bwd(proj, dx1, even_v_ln_g, even_v_ln_b, wsp_b, bspb, cw_full, g_woe, tm), _GradCarry([p_wgT0]))
    (p_woe,), _ = _call(_wgrad(yb, dx1b, "wgrad_even_out"))
    (dx0, _, d_nmix0), (r_wuT0,) = _call(
        _proj_bwd_norm(dproj, x0, row(norm_mix[0]), dx1, g_winT, "in_proj_bwd", tm), _GradCarry([p_wuT0]))
    (p_winT,), (r_wd0,) = _call(_wgrad(dproj, h0b, "wgrad_in"), _GradCarry([p_wd0]))
    r_woe, r_winT = _exchange_only(_GradCarry([p_woe, p_winT]), "exchange_last")

    small_shapes = [(2, dm), (2, dm), (1, aw), (1, aw), (1, A_GROUPS, CHUNK, CHUNK), (1, A_GROUPS, CHUNK), (3, bw),
                    (1, N_HEADS), (N_BUCKETS, N_HEADS), (dm,)]
    n_small = sum(int(np.prod(s)) for s in small_shapes)
    small_rows = 8 * ((n_small + 1023) // 1024)
    small_part = _pack_small(
        [jnp.concatenate([d_nmix0, d_nmix1]), jnp.concatenate([d_nffn0, d_nffn1]), d_lng, d_lnb, d_wsp,
         jnp.sum(d_bsp3, axis=-1), d_cw, dsink[:, 0:N_HEADS], d_relb, d_final], small_rows)
    small_sum = _unpack_small(_all_reduce_small(small_part, "all_reduce_small_grads"), small_shapes)
    grads = {}
    (grads["norm_mix"], grads["norm_ffn"], grads["even_v_ln_g"], grads["even_v_ln_b"], grads["even_w_spatial"],
     grads["even_b_spatial"], g_cw_full, grads["attn_sink"], grads["rel_bias"], grads["final_norm"]) = small_sum
    grads["even_conv_w"] = lax.dynamic_slice(g_cw_full, (0, me * (bw // NDEV)), (3, bw // NDEV))[None]

    order = ["norm_mix", "norm_ffn", "even_w_in", "even_v_ln_g", "even_v_ln_b", "even_w_spatial", "even_b_spatial",
             "even_conv_w", "even_w_out", "attn_w_qkv", "attn_sink", "rel_bias", "attn_w_out", "ffn_w_gate",
             "ffn_w_up", "ffn_w_down", "final_norm"]
    ws = dict(norm_mix=norm_mix, norm_ffn=norm_ffn, even_w_in=even_w_in, even_v_ln_g=even_v_ln_g,
              even_v_ln_b=even_v_ln_b, even_w_spatial=even_w_spatial, even_b_spatial=even_b_spatial,
              even_conv_w=even_conv_w, even_w_out=even_w_out, attn_w_qkv=attn_w_qkv, attn_sink=attn_sink,
              rel_bias=rel_bias, attn_w_out=attn_w_out, ffn_w_gate=ffn_w_gate, ffn_w_up=ffn_w_up,
              ffn_w_down=ffn_w_down, final_norm=final_norm)
    ms = dict(norm_mix=m_norm_mix, norm_ffn=m_norm_ffn, even_w_in=m_even_w_in, even_v_ln_g=m_even_v_ln_g,
              even_v_ln_b=m_even_v_ln_b, even_w_spatial=m_even_w_spatial, even_b_spatial=m_even_b_spatial,
              even_conv_w=m_even_conv_w, even_w_out=m_even_w_out, attn_w_qkv=m_attn_w_qkv, attn_sink=m_attn_sink,
              rel_bias=m_rel_bias, attn_w_out=m_attn_w_out, ffn_w_gate=m_ffn_w_gate, ffn_w_up=m_ffn_w_up,
              ffn_w_down=m_ffn_w_down, final_norm=m_final_norm)
    vs = dict(norm_mix=v_norm_mix, norm_ffn=v_norm_ffn, even_w_in=v_even_w_in, even_v_ln_g=v_even_v_ln_g,
              even_v_ln_b=v_even_v_ln_b, even_w_spatial=v_even_w_spatial, even_b_spatial=v_even_b_spatial,
              even_conv_w=v_even_conv_w, even_w_out=v_even_w_out, attn_w_qkv=v_attn_w_qkv, attn_sink=v_attn_sink,
              rel_bias=v_rel_bias, attn_w_out=v_attn_w_out, ffn_w_gate=v_ffn_w_gate, ffn_w_up=v_ffn_w_up,
              ffn_w_down=v_ffn_w_down, final_norm=v_final_norm)
    big = dict(even_w_in=([r_winT], True), even_w_out=([r_woe], False), attn_w_qkv=([r_wqkvT], True),
               attn_w_out=([r_woa], False), ffn_w_gate=([r_wgT0, r_wgT1], True), ffn_w_up=([r_wuT0, r_wuT1], True),
               ffn_w_down=([r_wd0, r_wd1], False))
    delta, new_m, new_v = {}, {}, {}
    for n, (recvs, transposed) in big.items():
        grads[n], delta[n], new_m[n], new_v[n] = _finish_weight(recvs, ws[n], ms[n], vs[n], transposed, "finish_" + n)
    small = [n for n in order if n not in big]
    sshapes = [ws[n].shape for n in small]
    ns = sum(int(np.prod(s)) for s in sshapes)
    srows = 8 * ((ns + 1023) // 1024)
    pk = lambda dct: _pack_small([dct[n] for n in small], srows)
    d, nm, nv = _adamw(pk(ws), pk(grads), pk(ms), pk(vs), "adamw_small")
    for n, a, b, c2 in zip(small, _unpack_small(d, sshapes), _unpack_small(nm, sshapes), _unpack_small(nv, sshapes)):
        delta[n], new_m[n], new_v[n] = a, b, c2

    loss = lax.psum(loss_part[0, 0], ("x", "y", "c"))
    return (loss, dx0[None], *[grads[n] for n in order], *[delta[n] for n in order],
            *[new_m[n] for n in order], *[new_v[n] for n in order])
```

```python
import math

import jax
import jax.numpy as jnp
import numpy as np
from jax import lax
from jax.experimental import pallas as pl
from jax.experimental.pallas import tpu as pltpu

F32, BF16 = jnp.float32, jnp.bfloat16
NDEV = 8
EPS = 1e-6
CHUNK = 128
A_GROUPS = 4
N_HEADS, N_KV, HEAD_DIM = 16, 4, 64
N_BUCKETS, MAX_DISTANCE = 32, 128
NEG = -1e30
ADAM_LR, ADAM_B1, ADAM_B2, ADAM_EPS, ADAM_WD, ADAM_STEP = 0.001, 0.9, 0.999, 1e-08, 0.01, 10
VMEM_LIMIT = 56 * 1024 * 1024
MESH = pl.DeviceIdType.MESH
NT = (((1,), (1,)), ((), ()))
NN = (((1,), (0,)), ((), ()))
TN = (((0,), (0,)), ((), ()))
ANY = pl.BlockSpec(memory_space=pl.ANY)


def _cp(n_grid=1):
    return pltpu.CompilerParams(dimension_semantics=("arbitrary",) * n_grid, vmem_limit_bytes=VMEM_LIMIT)


def _dot(a, b, dims):
    return lax.dot_general(a, b, dims, preferred_element_type=F32)


def _my_index():
    return 4 * lax.axis_index("x") + 2 * lax.axis_index("y") + lax.axis_index("c")


def _peer(k):
    x, y, c = lax.axis_index("x"), lax.axis_index("y"), lax.axis_index("c")
    px = 1 - x if k & 4 else x
    py = 1 - y if k & 2 else y
    pc = 1 - c if k & 1 else c
    return (px, py, pc)


def _load_weight(gath_ref, wbuf, sems):
    rows = gath_ref.shape[1]
    cps = [pltpu.make_async_copy(gath_ref.at[d], wbuf.at[pl.ds(d * rows, rows), :], sems.at[d]) for d in range(NDEV)]
    for c in cps:
        c.start()
    for c in cps:
        c.wait()


class _GatherCarry:
    def __init__(self, pieces):
        self.inputs = list(pieces)
        self.n = len(pieces)
        self.out_shape = [jax.ShapeDtypeStruct((NDEV,) + p.shape, p.dtype) for p in pieces]
        self.scratch = [pltpu.SemaphoreType.DMA((7 * self.n,)), pltpu.SemaphoreType.DMA((7 * self.n,)),
                        pltpu.SemaphoreType.DMA((self.n,))]

    def _ctx(self):
        x, y, c = lax.axis_index("x"), lax.axis_index("y"), lax.axis_index("c")
        chips = [(1 - x, y), (x, 1 - y), (1 - x, 1 - y)]
        return (x, y, c), (x, y, 1 - c), chips, c

    def _copy(self, k, j, block, to, ins, outs, sems, src=None):
        send_sems, recv_sems, _ = sems
        slot = outs[j].at[4 * block[0] + 2 * block[1] + block[2]]
        return pltpu.make_async_remote_copy(
            src_ref=slot if src is None else src, dst_ref=slot, send_sem=send_sems.at[k * self.n + j],
            recv_sem=recv_sems.at[k * self.n + j], device_id=to, device_id_type=MESH)

    def start(self, ins, outs, sems):
        me, sibling, chips, c = self._ctx()
        for j in range(self.n):
            pltpu.make_async_copy(ins[j], outs[j].at[4 * me[0] + 2 * me[1] + me[2]], sems[2].at[j]).start()
            self._copy(0, j, me, sibling, ins, outs, sems, src=ins[j]).start()
            for q, chip in enumerate(chips):
                self._copy(1 + q, j, me, (*chip, c), ins, outs, sems, src=ins[j]).start()

    def mid(self, ins, outs, sems):
        me, sibling, chips, c = self._ctx()
        for q, chip in enumerate(chips):
            for j in range(self.n):
                self._copy(1 + q, j, (*chip, c), me, ins, outs, sems).wait_recv()
                self._copy(4 + q, j, (*chip, c), sibling, ins, outs, sems).start()

    def finish(self, ins, outs, sems):
        me, sibling, chips, c = self._ctx()
        for j in range(self.n):
            self._copy(0, j, sibling, me, ins, outs, sems).wait_recv()
            for q, chip in enumerate(chips):
                self._copy(4 + q, j, (*chip, 1 - c), me, ins, outs, sems).wait_recv()
        for j in range(self.n):
            self._copy(0, j, me, sibling, ins, outs, sems, src=ins[j]).wait_send()
            for q, chip in enumerate(chips):
                self._copy(1 + q, j, me, (*chip, c), ins, outs, sems, src=ins[j]).wait_send()
                self._copy(4 + q, j, (*chip, c), sibling, ins, outs, sems).wait_send()
            pltpu.make_async_copy(ins[j], outs[j].at[0], sems[2].at[j]).wait()


class _GradCarry:
    def __init__(self, pieces):
        self.inputs = list(pieces)
        self.n = len(pieces)
        self.rows = [p.shape[0] // NDEV for p in pieces]
        self.out_shape = [jax.ShapeDtypeStruct((NDEV, r, p.shape[1]), p.dtype) for p, r in zip(pieces, self.rows)]
        self.scratch = [pltpu.SemaphoreType.DMA((7 * self.n,)), pltpu.SemaphoreType.DMA((7 * self.n,)),
                        pltpu.SemaphoreType.DMA((self.n,))]

    def _copies(self, ins, outs, sems):
        me = _my_index()
        local, remote = [], []
        for j in range(self.n):
            r = self.rows[j]
            local.append(pltpu.make_async_copy(ins[j].at[pl.ds(pl.multiple_of(me * r, 16), r), :], outs[j].at[me],
                                               sems[2].at[j]))
            for k in range(1, NDEV):
                peer = _peer(k)
                pidx = 4 * peer[0] + 2 * peer[1] + peer[2]
                remote.append(pltpu.make_async_remote_copy(
                    src_ref=ins[j].at[pl.ds(pl.multiple_of(pidx * r, 16), r), :], dst_ref=outs[j].at[me],
                    send_sem=sems[0].at[(k - 1) * self.n + j], recv_sem=sems[1].at[(k - 1) * self.n + j],
                    device_id=peer, device_id_type=MESH))
        return local, remote

    def start(self, ins, outs, sems):
        local, remote = self._copies(ins, outs, sems)
        for cp in local + remote:
            cp.start()

    def mid(self, ins, outs, sems):
        pass

    def finish(self, ins, outs, sems):
        local, remote = self._copies(ins, outs, sems)
        for cp in remote + local:
            cp.wait()


def _call(spec, carry=None):
    body, grid = spec["body"], spec["grid"]
    in_specs, out_specs, out_shape = list(spec["in_specs"]), list(spec["out_specs"]), list(spec["out_shape"])
    scratch, args = list(spec.get("scratch", [])), list(spec["args"])
    if carry is None:
        out = pl.pallas_call(body, grid=grid, in_specs=in_specs, out_specs=tuple(out_specs),
                             out_shape=tuple(out_shape), scratch_shapes=scratch, compiler_params=_cp(len(grid)),
                             name=spec["name"])(*args)
        return tuple(out), ()
    n_in, n_out, n_s = len(in_specs), len(out_specs), len(scratch)
    c_in, c_out = len(carry.inputs), len(carry.out_shape)
    steps = int(np.prod(grid))

    def wrapped(*refs):
        o = 0
        ins = refs[o:o + n_in]; o += n_in
        cins = refs[o:o + c_in]; o += c_in
        outs = refs[o:o + n_out]; o += n_out
        couts = refs[o:o + c_out]; o += c_out
        scr = refs[o:o + n_s]; o += n_s
        sems = refs[o:]
        step = pl.program_id(0)
        for ax in range(1, len(grid)):
            step = step * grid[ax] + pl.program_id(ax)

        @pl.when(step == 0)
        def _():
            carry.start(cins, couts, sems)
        if steps >= 3:
            @pl.when(step == steps - 2)
            def _():
                carry.mid(cins, couts, sems)
        body(*ins, *outs, *scr)

        @pl.when(step == steps - 1)
        def _():
            if steps < 3:
                carry.mid(cins, couts, sems)
            carry.finish(cins, couts, sems)

    out = pl.pallas_call(
        wrapped, grid=grid, in_specs=in_specs + [ANY] * c_in, out_specs=tuple(out_specs + [ANY] * c_out),
        out_shape=tuple(out_shape + carry.out_shape), scratch_shapes=scratch + carry.scratch,
        compiler_params=_cp(len(grid)), name=spec["name"])(*args, *carry.inputs)
    return tuple(out[:n_out]), tuple(out[n_out:])


def _exchange_only(carry, name):
    spec = dict(body=lambda: None, grid=(1,), in_specs=[], out_specs=[], out_shape=[], args=[], name=name)
    return _call(spec, carry)[1]


def _all_reduce_small(part, name):
    r, lanes = part.shape

    def body(p_ref, o_ref, buf, send_sems, recv_sems):
        me = _my_index()
        buf[me] = p_ref[...]
        cps = []
        for k in range(1, NDEV):
            cp = pltpu.make_async_remote_copy(
                src_ref=p_ref, dst_ref=buf.at[me], send_sem=send_sems.at[k - 1], recv_sem=recv_sems.at[k - 1],
                device_id=_peer(k), device_id_type=MESH)
            cp.start()
            cps.append(cp)
        for cp in cps:
            cp.wait()
        acc = buf[0]
        for d in range(1, NDEV):
            acc = acc + buf[d]
        o_ref[...] = acc

    return pl.pallas_call(
        body, out_shape=jax.ShapeDtypeStruct((r, lanes), F32),
        in_specs=[pl.BlockSpec(memory_space=pltpu.VMEM)], out_specs=pl.BlockSpec(memory_space=pltpu.VMEM),
        scratch_shapes=[pltpu.VMEM((NDEV, r, lanes), F32), pltpu.SemaphoreType.DMA((7,)), pltpu.SemaphoreType.DMA((7,))],
        name=name)(part)


def _rms_fwd(x, gain):
    r = lax.rsqrt(jnp.mean(x * x, axis=-1, keepdims=True) + EPS)
    return x * r * gain, r


def _rms_bwd(dh, x, r, gain):
    a = dh * gain
    dx = r * a - x * (r * r * r) * jnp.mean(a * x, axis=-1, keepdims=True)
    dgain = jnp.sum(dh * (x * r), axis=0, keepdims=True)
    return dx, dgain


def _gelu(x):
    return 0.5 * x * (1.0 + lax.erf(x * 0.7071067811865476))


def _gelu_grad(x):
    return 0.5 * (1.0 + lax.erf(x * 0.7071067811865476)) + x * jnp.exp(-0.5 * x * x) * 0.3989422804014327


def _sigmoid(x):
    return 1.0 / (1.0 + jnp.exp(-x))


def _adamw_math(w, g, m, v):
    nm = ADAM_B1 * m + (1.0 - ADAM_B1) * g
    nv = ADAM_B2 * v + (1.0 - ADAM_B2) * (g * g)
    m_hat = nm / (1.0 - ADAM_B1 ** ADAM_STEP)
    v_hat = nv / (1.0 - ADAM_B2 ** ADAM_STEP)
    return -ADAM_LR * (m_hat / (jnp.sqrt(v_hat) + ADAM_EPS) + ADAM_WD * w), nm, nv


def _tok(tm, w):
    return pl.BlockSpec((tm, w), lambda i: (i, 0))


def _full(shape):
    return pl.BlockSpec(shape, lambda *i: (0,) * len(shape))


def _norm_proj(x, gain, gath, out_dtype, name, tm):
    t, dm = x.shape
    n = gath.shape[1] * NDEV

    def body(x_ref, g_ref, gath_ref, proj_ref, hb_ref, wbuf, sems):
        @pl.when(pl.program_id(0) == 0)
        def _():
            _load_weight(gath_ref, wbuf, sems)
        h, _ = _rms_fwd(x_ref[...], g_ref[...])
        hb = h.astype(BF16)
        hb_ref[...] = hb
        proj_ref[...] = _dot(hb, wbuf[...], NT).astype(out_dtype)

    return dict(
        body=body, grid=(t // tm,), name=name, args=[x, gain, gath],
        out_shape=[jax.ShapeDtypeStruct((t, n), out_dtype), jax.ShapeDtypeStruct((t, dm), BF16)],
        in_specs=[_tok(tm, dm), _full((1, dm)), ANY], out_specs=[_tok(tm, n), _tok(tm, dm)],
        scratch=[pltpu.VMEM((n, dm), BF16), pltpu.SemaphoreType.DMA((NDEV,))])


def _proj_bwd_norm(dy, x, gain, dres, gath, name, tm):
    t, dm = x.shape
    n = gath.shape[1] * NDEV

    def body(dy_ref, x_ref, g_ref, dres_ref, gath_ref, dx_ref, dxb_ref, dgain_ref, wbuf, sems):
        @pl.when(pl.program_id(0) == 0)
        def _():
            _load_weight(gath_ref, wbuf, sems)
            dgain_ref[...] = jnp.zeros_like(dgain_ref)
        xv, gain_v = x_ref[...], g_ref[...]
        _, r = _rms_fwd(xv, gain_v)
        dh = _dot(dy_ref[...], wbuf[...], NN)
        dx, dgain = _rms_bwd(dh, xv, r, gain_v)
        dx = dres_ref[...] + dx
        dx_ref[...] = dx
        dxb_ref[...] = dx.astype(BF16)
        dgain_ref[...] += dgain

    return dict(
        body=body, grid=(t // tm,), name=name, args=[dy, x, gain, dres, gath],
        out_shape=[jax.ShapeDtypeStruct((t, dm), F32), jax.ShapeDtypeStruct((t, dm), BF16),
                   jax.ShapeDtypeStruct((1, dm), F32)],
        in_specs=[_tok(tm, n), _tok(tm, dm), _full((1, dm)), _tok(tm, dm), ANY],
        out_specs=[_tok(tm, dm), _tok(tm, dm), _full((1, dm))],
        scratch=[pltpu.VMEM((n, dm), BF16), pltpu.SemaphoreType.DMA((NDEV,))])


def _wgrad(a, b, name, tmm=256):
    t, m = a.shape
    n = b.shape[1]

    def body(a_ref, b_ref, o_ref):
        o_ref[...] = _dot(a_ref[...], b_ref[...], TN).astype(BF16)

    return dict(
        body=body, grid=(m // tmm,), name=name, args=[a, b], out_shape=[jax.ShapeDtypeStruct((m, n), BF16)],
        in_specs=[pl.BlockSpec((t, tmm), lambda j: (0, j)), pl.BlockSpec((t, n), lambda j: (0, 0))],
        out_specs=[pl.BlockSpec((tmm, n), lambda j: (j, 0))])


def _halo_specs(tm, t, width, col_blocks):
    nb8 = tm // 8
    last = t // 8 - 1
    prev = [pl.BlockSpec((8, width), lambda i, cb=cb: (jnp.maximum(i * nb8 - 1, 0), cb)) for cb in col_blocks]
    nxt = [pl.BlockSpec((8, width), lambda i, cb=cb: (jnp.minimum((i + 1) * nb8, last), cb)) for cb in col_blocks]
    return prev, nxt


def _shift_rows(z, prev_row, next_row):
    tm = z.shape[0]
    row = lax.broadcasted_iota(jnp.int32, z.shape, 0)
    zm1 = jnp.where(row == 0, prev_row, pltpu.roll(z, 1, 0))
    zp1 = jnp.where(row == tm - 1, next_row, pltpu.roll(z, tm - 1, 0))
    return zm1, zp1


def _gating_fwd(proj, lng, lnb, wsp_ref, bsp_ref, aw):
    tm = proj.shape[0]
    a_u = _gelu(proj[:, 0:aw])
    gv = _gelu(proj[:, aw:2 * aw])
    mu = jnp.mean(gv, axis=-1, keepdims=True)
    xc = gv - mu
    rstd = lax.rsqrt(jnp.mean(xc * xc, axis=-1, keepdims=True) + EPS)
    vn = xc * rstd
    a_v = (vn * lng + lnb).astype(BF16)
    gd = aw // A_GROUPS
    rows = []
    for c in range(tm // CHUNK):
        cols = []
        for g in range(A_GROUPS):
            blk = a_v[c * CHUNK:(c + 1) * CHUNK, g * gd:(g + 1) * gd]
            cols.append(_dot(wsp_ref[g], blk, NN) + bsp_ref[g])
        rows.append(jnp.concatenate(cols, axis=1))
    mixed = jnp.concatenate(rows, axis=0)
    return a_u, vn, rstd, a_v, mixed


def _even_core_fwd(proj, x0, lng, lnb, wsp, bspb, cw, gath, tm):
    t, dm = x0.shape
    aw = lng.shape[1]
    bw = cw.shape[1]
    assert aw == bw and 2 * aw + 3 * bw == proj.shape[1]
    nt = t // tm
    prev, nxt = _halo_specs(tm, t, bw, [3, 4])

    def body(proj_ref, cp_ref, hp_ref, cn_ref, hn_ref, x0_ref, lng_ref, lnb_ref, wsp_ref, bsp_ref, cw_ref, gath_ref,
             x1_ref, y_ref, wbuf, sems):
        i = pl.program_id(0)

        @pl.when(i == 0)
        def _():
            _load_weight(gath_ref, wbuf, sems)
        proj_v = proj_ref[...]
        a_u, _, _, _, mixed = _gating_fwd(proj_v, lng_ref[...], lnb_ref[...], wsp_ref, bsp_ref, aw)
        a_out = a_u * mixed
        bb = proj_v[:, 2 * aw:2 * aw + bw]
        z = proj_v[:, 2 * aw + bw:2 * aw + 2 * bw] * proj_v[:, 2 * aw + 2 * bw:]
        zprev = jnp.where(i > 0, cp_ref[7:8, :] * hp_ref[7:8, :], 0.0)
        znext = jnp.where(i < nt - 1, cn_ref[0:1, :] * hn_ref[0:1, :], 0.0)
        zm1, zp1 = _shift_rows(z, zprev, znext)
        cwv = cw_ref[...]
        conv = zm1 * cwv[0:1, :] + z * cwv[1:2, :] + zp1 * cwv[2:3, :]
        y = jnp.concatenate([a_out, bb * conv], axis=1).astype(BF16)
        y_ref[...] = y
        x1_ref[...] = x0_ref[...] + _dot(y, wbuf[...], NN)

    return dict(
        body=body, grid=(nt,), name="even_core_fwd",
        args=[proj, proj, proj, proj, proj, x0, lng, lnb, wsp, bspb, cw, gath],
        out_shape=[jax.ShapeDtypeStruct((t, dm), F32), jax.ShapeDtypeStruct((t, aw + bw), BF16)],
        in_specs=[_tok(tm, proj.shape[1]), prev[0], prev[1], nxt[0], nxt[1], _tok(tm, dm), _full(lng.shape),
                  _full(lnb.shape), _full(wsp.shape), _full(bspb.shape), _full(cw.shape), ANY],
        out_specs=[_tok(tm, dm), _tok(tm, aw + bw)],
        scratch=[pltpu.VMEM((gath.shape[1] * NDEV, dm), BF16), pltpu.SemaphoreType.DMA((NDEV,))])


def _even_core_bwd(proj, dx1, lng, lnb, wsp, bspb, cw, gath, tm):
    t, dm = dx1.shape
    aw, bw = lng.shape[1], cw.shape[1]
    gd = aw // A_GROUPS
    nt = t // tm
    inw = proj.shape[1]
    prev, nxt = _halo_specs(tm, t, bw, [2, 3, 4])
    nb8 = tm // 8
    last8 = t // 8 - 1

    def body(proj_ref, bp_ref, cp_ref, hp_ref, bn_ref, cn_ref, hn_ref, dx_ref, dxp_ref, dxn_ref,
             lng_ref, lnb_ref, wsp_ref, bsp_ref, cw_ref, gath_ref,
             dproj_ref, dlng_ref, dlnb_ref, dwsp_ref, dbsp_ref, dcw_ref, wbuf, sems):
        i = pl.program_id(0)

        @pl.when(i == 0)
        def _():
            _load_weight(gath_ref, wbuf, sems)
            dlng_ref[...] = jnp.zeros_like(dlng_ref)
            dlnb_ref[...] = jnp.zeros_like(dlnb_ref)
            dwsp_ref[...] = jnp.zeros_like(dwsp_ref)
            dbsp_ref[...] = jnp.zeros_like(dbsp_ref)
            dcw_ref[...] = jnp.zeros_like(dcw_ref)
        proj_v = proj_ref[...]
        lng_v = lng_ref[...]
        a_u, vn, rstd, a_v, mixed = _gating_fwd(proj_v, lng_v, lnb_ref[...], wsp_ref, bsp_ref, aw)
        w = wbuf[...]
        dy = _dot(dx_ref[...].astype(BF16), w, NT)
        da_out, db_out = dy[:, 0:aw], dy[:, aw:]
        da_u = da_out * mixed
        dmixed = da_out * a_u
        dmb = dmixed.astype(BF16)
        rows = []
        for c in range(tm // CHUNK):
            cols = []
            for g in range(A_GROUPS):
                r0, c0 = c * CHUNK, g * gd
                dm_cg = dmb[r0:r0 + CHUNK, c0:c0 + gd]
                cols.append(_dot(wsp_ref[g], dm_cg, TN))
                dwsp_ref[g] += _dot(dm_cg, a_v[r0:r0 + CHUNK, c0:c0 + gd], NT)
                dbsp_ref[g] += dmixed[r0:r0 + CHUNK, c0:c0 + gd]
            rows.append(jnp.concatenate(cols, axis=1))
        dav = jnp.concatenate(rows, axis=0)
        dlng_ref[...] += jnp.sum(dav * vn, axis=0, keepdims=True)
        dlnb_ref[...] += jnp.sum(dav, axis=0, keepdims=True)
        dvn = dav * lng_v
        dgv = rstd * (dvn - jnp.mean(dvn, axis=-1, keepdims=True) - vn * jnp.mean(dvn * vn, axis=-1, keepdims=True))
        dv_pre = dgv * _gelu_grad(proj_v[:, aw:2 * aw])
        du_pre = da_u * _gelu_grad(proj_v[:, 0:aw])
        bb = proj_v[:, 2 * aw:2 * aw + bw]
        bc = proj_v[:, 2 * aw + bw:2 * aw + 2 * bw]
        bh = proj_v[:, 2 * aw + 2 * bw:]
        z = bc * bh
        zprev = jnp.where(i > 0, cp_ref[7:8, :] * hp_ref[7:8, :], 0.0)
        znext = jnp.where(i < nt - 1, cn_ref[0:1, :] * hn_ref[0:1, :], 0.0)
        zm1, zp1 = _shift_rows(z, zprev, znext)
        cwv = cw_ref[...]
        conv = zm1 * cwv[0:1, :] + z * cwv[1:2, :] + zp1 * cwv[2:3, :]
        dbb = db_out * conv
        dconv = db_out * bb
        dx_edge = jnp.concatenate([dxp_ref[...], dxn_ref[...]], axis=0).astype(BF16)
        dy_edge = _dot(dx_edge, w[aw:, :], NT)
        dcprev = jnp.where(i > 0, dy_edge[7:8, :] * bp_ref[7:8, :], 0.0)
        dcnext = jnp.where(i < nt - 1, dy_edge[8:9, :] * bn_ref[0:1, :], 0.0)
        dcm1, dcp1 = _shift_rows(dconv, dcprev, dcnext)
        dz = dcp1 * cwv[0:1, :] + dconv * cwv[1:2, :] + dcm1 * cwv[2:3, :]
        dcw_ref[0:1, :] += jnp.sum(dconv * zm1, axis=0, keepdims=True)
        dcw_ref[1:2, :] += jnp.sum(dconv * z, axis=0, keepdims=True)
        dcw_ref[2:3, :] += jnp.sum(dconv * zp1, axis=0, keepdims=True)
        dproj_ref[...] = jnp.concatenate([du_pre, dv_pre, dbb, dz * bh, dz * bc], axis=1).astype(BF16)

    row8 = lambda f: pl.BlockSpec((8, dm), f)
    return dict(
        body=body, grid=(nt,), name="even_core_bwd",
        args=[proj, proj, proj, proj, proj, proj, proj, dx1, dx1, dx1, lng, lnb, wsp, bspb, cw, gath],
        out_shape=[jax.ShapeDtypeStruct((t, inw), BF16), jax.ShapeDtypeStruct((1, aw), F32),
                   jax.ShapeDtypeStruct((1, aw), F32), jax.ShapeDtypeStruct(wsp.shape, F32),
                   jax.ShapeDtypeStruct((A_GROUPS, CHUNK, gd), F32), jax.ShapeDtypeStruct(cw.shape, F32)],
        in_specs=[_tok(tm, inw), prev[0], prev[1], prev[2], nxt[0], nxt[1], nxt[2], _tok(tm, dm),
                  row8(lambda i: (jnp.maximum(i * nb8 - 1, 0), 0)), row8(lambda i: (jnp.minimum((i + 1) * nb8, last8), 0)),
                  _full(lng.shape), _full(lnb.shape), _full(wsp.shape), _full(bspb.shape), _full(cw.shape), ANY],
        out_specs=[_tok(tm, inw), _full((1, aw)), _full((1, aw)), _full(wsp.shape),
                   _full((A_GROUPS, CHUNK, gd)), _full(cw.shape)],
        scratch=[pltpu.VMEM((gath.shape[1] * NDEV, dm), BF16), pltpu.SemaphoreType.DMA((NDEV,))])


def _ff_chunks(f, width=1024):
    return [(c0, min(c0 + width, f)) for c0 in range(0, f, width)]


def _ffn_up(x, gain, gath_g, gath_u, name, tm):
    t, dm = x.shape
    f = gath_g.shape[1] * NDEV

    def body(x_ref, g_ref, gg_ref, gu_ref, gate_ref, up_ref, act_ref, wg, wu, sems):
        @pl.when(pl.program_id(0) == 0)
        def _():
            _load_weight(gg_ref, wg, sems)
            _load_weight(gu_ref, wu, sems)
        h, _ = _rms_fwd(x_ref[...], g_ref[...])
        hb = h.astype(BF16)
        for c0, c1 in _ff_chunks(f):
            gate = _dot(hb, wg[c0:c1, :], NT)
            up = _dot(hb, wu[c0:c1, :], NT)
            gate_ref[:, c0:c1] = gate.astype(BF16)
            up_ref[:, c0:c1] = up.astype(BF16)
            act_ref[:, c0:c1] = (gate * _sigmoid(gate) * up).astype(BF16)

    o = jax.ShapeDtypeStruct((t, f), BF16)
    return dict(
        body=body, grid=(t // tm,), name=name, args=[x, gain, gath_g, gath_u], out_shape=[o, o, o],
        in_specs=[_tok(tm, dm), _full((1, dm)), ANY, ANY], out_specs=[_tok(tm, f)] * 3,
        scratch=[pltpu.VMEM((f, dm), BF16), pltpu.VMEM((f, dm), BF16), pltpu.SemaphoreType.DMA((NDEV,))])


def _ffn_down(x, act, gath_d, name, tm):
    t, dm = x.shape
    f = act.shape[1]

    def body(x_ref, a_ref, gd_ref, xo_ref, wd, sems):
        @pl.when(pl.program_id(0) == 0)
        def _():
            _load_weight(gd_ref, wd, sems)
        xo_ref[...] = x_ref[...] + _dot(a_ref[...], wd[...], NN)

    return dict(
        body=body, grid=(t // tm,), name=name, args=[x, act, gath_d], out_shape=[jax.ShapeDtypeStruct((t, dm), F32)],
        in_specs=[_tok(tm, dm), _tok(tm, f), ANY], out_specs=[_tok(tm, dm)],
        scratch=[pltpu.VMEM((f, dm), BF16), pltpu.SemaphoreType.DMA((NDEV,))])


def _ffn_bwd(dxo, x, gate, up, gain, gath_g, gath_u, gath_d, name, tm):
    t, dm = x.shape
    f = gate.shape[1]

    def body(dxo_ref, x_ref, gate_ref, up_ref, g_ref, gg_ref, gu_ref, gd_ref,
             dx_ref, dxb_ref, dg_ref, du_ref, hb_ref, dgain_ref, wg, wu, wd, sems):
        @pl.when(pl.program_id(0) == 0)
        def _():
            _load_weight(gg_ref, wg, sems)
            _load_weight(gu_ref, wu, sems)
            _load_weight(gd_ref, wd, sems)
            dgain_ref[...] = jnp.zeros_like(dgain_ref)
        xv, gain_v, dxo_v = x_ref[...], g_ref[...], dxo_ref[...]
        h, r = _rms_fwd(xv, gain_v)
        hb_ref[...] = h.astype(BF16)
        dxob = dxo_v.astype(BF16)
        dh = jnp.zeros_like(xv)
        for c0, c1 in _ff_chunks(f):
            gate_v = gate_ref[:, c0:c1].astype(F32)
            up_v = up_ref[:, c0:c1].astype(F32)
            s = _sigmoid(gate_v)
            silu = gate_v * s
            dact = _dot(dxob, wd[c0:c1, :], NT)
            dg = (dact * up_v * (s * (1.0 + gate_v * (1.0 - s)))).astype(BF16)
            du = (dact * silu).astype(BF16)
            dg_ref[:, c0:c1] = dg
            du_ref[:, c0:c1] = du
            dh = dh + _dot(dg, wg[c0:c1, :], NN) + _dot(du, wu[c0:c1, :], NN)
        dx, dgain = _rms_bwd(dh, xv, r, gain_v)
        dx = dxo_v + dx
        dx_ref[...] = dx
        dxb_ref[...] = dx.astype(BF16)
        dgain_ref[...] += dgain

    return dict(
        body=body, grid=(t // tm,), name=name, args=[dxo, x, gate, up, gain, gath_g, gath_u, gath_d],
        out_shape=[jax.ShapeDtypeStruct((t, dm), F32), jax.ShapeDtypeStruct((t, dm), BF16),
                   jax.ShapeDtypeStruct((t, f), BF16), jax.ShapeDtypeStruct((t, f), BF16),
                   jax.ShapeDtypeStruct((t, dm), BF16), jax.ShapeDtypeStruct((1, dm), F32)],
        in_specs=[_tok(tm, dm), _tok(tm, dm), _tok(tm, f), _tok(tm, f), _full((1, dm)), ANY, ANY, ANY],
        out_specs=[_tok(tm, dm), _tok(tm, dm), _tok(tm, f), _tok(tm, f), _tok(tm, dm), _full((1, dm))],
        scratch=[pltpu.VMEM((f, dm), BF16), pltpu.VMEM((f, dm), BF16), pltpu.VMEM((f, dm), BF16),
                 pltpu.SemaphoreType.DMA((NDEV,))])


def _t5_buckets(rel):
    nb = N_BUCKETS // 2
    ret = jnp.where(rel > 0, nb, 0)
    n = jnp.abs(rel)
    max_exact = nb // 2
    nf = jnp.maximum(n, 1).astype(jnp.float32)
    large = max_exact + (jnp.log(nf / max_exact) / math.log(MAX_DISTANCE / max_exact)
                         * (nb - max_exact)).astype(jnp.int32)
    large = jnp.minimum(large, nb - 1)
    return ret + jnp.where(n < max_exact, n, large)


def _bucket_table():
    qi = jnp.arange(CHUNK, dtype=jnp.int32)[:, None]
    kj = jnp.arange(3 * CHUNK, dtype=jnp.int32)[None, :]
    rel = kj - CHUNK - qi
    return jnp.where(jnp.abs(rel) <= CHUNK, _t5_buckets(rel), -1)


def _bias_table(rel_bias_t, buckets):
    nh = rel_bias_t.shape[0]

    def body(rb_ref, bk_ref, o_ref):
        bk = bk_ref[...]
        for h in range(nh):
            acc = jnp.where(bk < 0, NEG, 0.0).astype(F32)
            for b in range(N_BUCKETS):
                acc = jnp.where(bk == b, rb_ref[h, b], acc)
            o_ref[h] = acc

    return pl.pallas_call(
        body, out_shape=jax.ShapeDtypeStruct((nh,) + buckets.shape, F32),
        in_specs=[pl.BlockSpec(memory_space=pltpu.SMEM), pl.BlockSpec(memory_space=pltpu.VMEM)],
        out_specs=pl.BlockSpec(memory_space=pltpu.VMEM), name="bias_table")(rel_bias_t, buckets)


def _rel_bias_grad(dbias, buckets):
    nh = dbias.shape[0]

    def body(db_ref, bk_ref, o_ref):
        bk = bk_ref[...]
        lane = lax.broadcasted_iota(jnp.int32, (1, 128), 1)
        for h in range(nh):
            d = db_ref[h]
            row = jnp.zeros((1, 128), F32)
            for b in range(N_BUCKETS):
                s = jnp.sum(jnp.sum(jnp.where(bk == b, d, 0.0), axis=1, keepdims=True), axis=0, keepdims=True)
                row = jnp.where(lane == b, s, row)
            o_ref[h:h + 1, :] = row

    return pl.pallas_call(
        body, out_shape=jax.ShapeDtypeStruct((nh, 128), F32),
        in_specs=[pl.BlockSpec(memory_space=pltpu.VMEM), pl.BlockSpec(memory_space=pltpu.VMEM)],
        out_specs=pl.BlockSpec(memory_space=pltpu.VMEM), compiler_params=_cp(0), name="rel_bias_grad")(dbias, buckets)


def _half_masks():
    lane = lax.broadcasted_iota(jnp.int32, (CHUNK, 128), 1)
    return lane < HEAD_DIM, lane >= HEAD_DIM


def _kv_halves(blk, hk, lo, hi):
    zero = jnp.zeros_like(blk)
    rolled = pltpu.roll(blk, HEAD_DIM, 1)
    if hk % 2 == 0:
        return jnp.where(lo, blk, zero), jnp.where(hi, rolled, zero)
    return jnp.where(lo, rolled, zero), jnp.where(hi, blk, zero)


def _attn_probs(q2, k_lo, k_hi, bias_ref, sink_ref, hk, n, nblk):
    scale = HEAD_DIM ** -0.5
    out = {}
    for half, ks in ((0, k_lo), (1, k_hi)):
        s3 = [_dot(q2, ks[jj], NT) * scale for jj in range(3)]
        for tile in range(2):
            h = 4 * hk + 2 * tile + half
            s = []
            for jj in range(3):
                sj = s3[jj][tile * CHUNK:(tile + 1) * CHUNK, :] + bias_ref[h, :, jj * CHUNK:(jj + 1) * CHUNK]
                if jj == 0:
                    sj = jnp.where(n > 0, sj, NEG)
                if jj == 2:
                    sj = jnp.where(n < nblk - 1, sj, NEG)
                s.append(sj)
            sink = sink_ref[h]
            m = jnp.maximum(jnp.max(jnp.maximum(jnp.maximum(s[0], s[1]), s[2]), axis=-1, keepdims=True), sink)
            e = [jnp.exp(sj - m) for sj in s]
            es = jnp.exp(sink - m)
            inv = 1.0 / (jnp.sum(e[0] + e[1] + e[2], axis=-1, keepdims=True) + es)
            out[(half, tile)] = ([ej * inv for ej in e], es * inv)
    return out


def _key_block_starts(n, nblk):
    return [pl.multiple_of(jnp.clip(n - 1 + jj, 0, nblk - 1) * CHUNK, CHUNK) for jj in range(3)]


def _attn_fwd(qkv, x2, bias, sink, gath):
    t, dm = x2.shape
    nblk = t // CHUNK
    kvw = N_KV * HEAD_DIM
    kcb, vcb = dm // kvw, dm // kvw + 1

    def body(q_ref, k_ref, v_ref, x2_ref, bias_ref, sink_ref, gath_ref, x3_ref, att_ref, wbuf, sems):
        n = pl.program_id(0)

        @pl.when(n == 0)
        def _():
            _load_weight(gath_ref, wbuf, sems)
        lo, hi = _half_masks()
        starts = _key_block_starts(n, nblk)
        tiles = []
        for hk in range(N_KV):
            kt = (hk // 2) * 128
            k_lo, k_hi, v_lo, v_hi = [], [], [], []
            for jj in range(3):
                a, b = _kv_halves(k_ref[pl.ds(starts[jj], CHUNK), kt:kt + 128], hk, lo, hi)
                k_lo.append(a)
                k_hi.append(b)
                a, b = _kv_halves(v_ref[pl.ds(starts[jj], CHUNK), kt:kt + 128], hk, lo, hi)
                v_lo.append(a)
                v_hi.append(b)
            q2 = jnp.concatenate([q_ref[:, (2 * hk) * 128:(2 * hk + 1) * 128],
                                  q_ref[:, (2 * hk + 1) * 128:(2 * hk + 2) * 128]], axis=0)
            pr = _attn_probs(q2, k_lo, k_hi, bias_ref, sink_ref, hk, n, nblk)
            o2 = jnp.zeros((2 * CHUNK, 128), F32)
            for half, vs in ((0, v_lo), (1, v_hi)):
                for jj in range(3):
                    p2 = jnp.concatenate([pr[(half, 0)][0][jj], pr[(half, 1)][0][jj]], axis=0).astype(BF16)
                    o2 = o2 + _dot(p2, vs[jj], NN)
            tiles += [o2[0:CHUNK], o2[CHUNK:]]
        att = jnp.concatenate(tiles, axis=1).astype(BF16)
        att_ref[...] = att
        x3_ref[...] = x2_ref[...] + _dot(att, wbuf[...], NN)

    blk = pl.BlockSpec((CHUNK, dm), lambda n: (n, 0))
    return dict(
        body=body, grid=(nblk,), name="attn_fwd", args=[qkv, qkv, qkv, x2, bias, sink, gath],
        out_shape=[jax.ShapeDtypeStruct((t, dm), F32), jax.ShapeDtypeStruct((t, dm), BF16)],
        in_specs=[blk, pl.BlockSpec((t, kvw), lambda n: (0, kcb)), pl.BlockSpec((t, kvw), lambda n: (0, vcb)), blk,
                  _full(bias.shape), pl.BlockSpec(memory_space=pltpu.SMEM), ANY],
        out_specs=[blk, blk],
        scratch=[pltpu.VMEM((gath.shape[1] * NDEV, dm), BF16), pltpu.SemaphoreType.DMA((NDEV,))])


def _attn_bwd(qkv, att, dx3, bias, sink, gath):
    t, dm = dx3.shape
    nblk = t // CHUNK
    kvw = N_KV * HEAD_DIM
    kcb, vcb = dm // kvw, dm // kvw + 1
    scale = HEAD_DIM ** -0.5

    def body(q_ref, k_ref, v_ref, att_ref, dx_ref, bias_ref, sink_ref, gath_ref,
             dq_ref, dk_ref, dv_ref, dbias_ref, dsink_ref, wbuf, sems):
        n = pl.program_id(0)

        @pl.when(n == 0)
        def _():
            _load_weight(gath_ref, wbuf, sems)
            dk_ref[...] = jnp.zeros_like(dk_ref)
            dv_ref[...] = jnp.zeros_like(dv_ref)
            dbias_ref[...] = jnp.zeros_like(dbias_ref)
            dsink_ref[...] = jnp.zeros_like(dsink_ref)
        lo, hi = _half_masks()
        lane1 = lax.broadcasted_iota(jnp.int32, (1, 128), 1)
        starts = _key_block_starts(n, nblk)
        dout = _dot(dx_ref[...].astype(BF16), wbuf[...], NT)
        prod = dout * att_ref[...].astype(F32)
        doutb = dout.astype(BF16)
        dq_tiles = []
        dsink_row = jnp.zeros((1, 128), F32)
        for hk in range(N_KV):
            kt = (hk // 2) * 128
            k_lo, k_hi, v_lo, v_hi = [], [], [], []
            for jj in range(3):
                a, b = _kv_halves(k_ref[pl.ds(starts[jj], CHUNK), kt:kt + 128], hk, lo, hi)
                k_lo.append(a)
                k_hi.append(b)
                a, b = _kv_halves(v_ref[pl.ds(starts[jj], CHUNK), kt:kt + 128], hk, lo, hi)
                v_lo.append(a)
                v_hi.append(b)
            c0 = (2 * hk) * 128
            q2 = jnp.concatenate([q_ref[:, c0:c0 + 128], q_ref[:, c0 + 128:c0 + 256]], axis=0)
            do2 = jnp.concatenate([doutb[:, c0:c0 + 128], doutb[:, c0 + 128:c0 + 256]], axis=0)
            pr = _attn_probs(q2, k_lo, k_hi, bias_ref, sink_ref, hk, n, nblk)
            dq2 = jnp.zeros((2 * CHUNK, 128), F32)
            dk_acc = [None] * 3
            dv_acc = [None] * 3
            for half, ks, vs, msk in ((0, k_lo, v_lo, lo), (1, k_hi, v_hi, hi)):
                dsum = []
                for tile in range(2):
                    pt = prod[:, c0 + tile * 128:c0 + (tile + 1) * 128]
                    dsum.append(jnp.sum(jnp.where(msk, pt, 0.0), axis=-1, keepdims=True))
                    h = 4 * hk + 2 * tile + half
                    contrib = -jnp.sum(pr[(half, tile)][1] * dsum[tile], axis=0, keepdims=True)
                    dsink_row = dsink_row + jnp.where(lane1 == h, contrib, 0.0)
                for jj in range(3):
                    dp2 = _dot(do2, vs[jj], NT)
                    ds_t = []
                    for tile in range(2):
                        p = pr[(half, tile)][0][jj]
                        ds = p * (dp2[tile * CHUNK:(tile + 1) * CHUNK, :] - dsum[tile])
                        h = 4 * hk + 2 * tile + half
                        dbias_ref[h, :, jj * CHUNK:(jj + 1) * CHUNK] += ds
                        ds_t.append(ds)
                    ds2 = jnp.concatenate(ds_t, axis=0).astype(BF16)
                    p2 = jnp.concatenate([pr[(half, 0)][0][jj], pr[(half, 1)][0][jj]], axis=0).astype(BF16)
                    dq2 = dq2 + _dot(ds2, ks[jj], NN) * scale
                    dkj = jnp.where(msk, _dot(ds2, q2, TN) * scale, 0.0)
                    dvj = jnp.where(msk, _dot(p2, do2, TN), 0.0)
                    dk_acc[jj] = dkj if dk_acc[jj] is None else dk_acc[jj] + dkj
                    dv_acc[jj] = dvj if dv_acc[jj] is None else dv_acc[jj] + dvj
            for jj in range(3):
                keep = lo if hk % 2 == 0 else hi
                dkj = dk_acc[jj] + pltpu.roll(dk_acc[jj], HEAD_DIM, 1)
                dvj = dv_acc[jj] + pltpu.roll(dv_acc[jj], HEAD_DIM, 1)
                dk_ref[pl.ds(starts[jj], CHUNK), kt:kt + 128] += jnp.where(keep, dkj, 0.0)
                dv_ref[pl.ds(starts[jj], CHUNK), kt:kt + 128] += jnp.where(keep, dvj, 0.0)
            dq_tiles += [dq2[0:CHUNK], dq2[CHUNK:]]
        dq_ref[...] = jnp.concatenate(dq_tiles, axis=1).astype(BF16)
        dsink_ref[...] += dsink_row

    blk = pl.BlockSpec((CHUNK, dm), lambda n: (n, 0))
    return dict(
        body=body, grid=(nblk,), name="attn_bwd", args=[qkv, qkv, qkv, att, dx3, bias, sink, gath],
        out_shape=[jax.ShapeDtypeStruct((t, dm), BF16), jax.ShapeDtypeStruct((t, kvw), F32),
                   jax.ShapeDtypeStruct((t, kvw), F32), jax.ShapeDtypeStruct(bias.shape, F32),
                   jax.ShapeDtypeStruct((1, 128), F32)],
        in_specs=[blk, pl.BlockSpec((t, kvw), lambda n: (0, kcb)), pl.BlockSpec((t, kvw), lambda n: (0, vcb)),
                  blk, blk, _full(bias.shape), pl.BlockSpec(memory_space=pltpu.SMEM), ANY],
        out_specs=[blk, _full((t, kvw)), _full((t, kvw)), _full(bias.shape), _full((1, 128))],
        scratch=[pltpu.VMEM((gath.shape[1] * NDEV, dm), BF16), pltpu.SemaphoreType.DMA((NDEV,))])


def _final_loss(x4, target, gain, tm):
    t, dm = x4.shape
    steps = t // tm

    def body(x_ref, t_ref, g_ref, loss_ref, dx_ref, dxb_ref, dgain_ref, acc):
        i = pl.program_id(0)

        @pl.when(i == 0)
        def _():
            acc[...] = jnp.zeros_like(acc)
            dgain_ref[...] = jnp.zeros_like(dgain_ref)
        xv, gain_v = x_ref[...], g_ref[...]
        y, r = _rms_fwd(xv, gain_v)
        e = y - t_ref[...]
        acc[...] += jnp.sum(e * e, axis=0, keepdims=True)
        dx, dgain = _rms_bwd(e * (1.0 / dm), xv, r, gain_v)
        dx_ref[...] = dx
        dxb_ref[...] = dx.astype(BF16)
        dgain_ref[...] += dgain

        @pl.when(i == steps - 1)
        def _():
            loss_ref[...] = jnp.sum(acc[...], axis=-1, keepdims=True) * (0.5 / dm)

    return dict(
        body=body, grid=(steps,), name="final_loss", args=[x4, target, gain],
        out_shape=[jax.ShapeDtypeStruct((1, 1), F32), jax.ShapeDtypeStruct((t, dm), F32),
                   jax.ShapeDtypeStruct((t, dm), BF16), jax.ShapeDtypeStruct((1, dm), F32)],
        in_specs=[_tok(tm, dm), _tok(tm, dm), _full((1, dm))],
        out_specs=[_full((1, 1)), _tok(tm, dm), _tok(tm, dm), _full((1, dm))],
        scratch=[pltpu.VMEM((1, dm), F32)])


def _finish_weight(recvs, w, m, v, name):
    nl, r, dm = w.shape
    assert nl == len(recvs) and recvs[0].shape[1:] == (r, dm)
    td = dm // 2
    wspec = pl.BlockSpec((None, r, td), lambda l, j: (l, 0, j))

    def body(*refs):
        r_refs = refs[:nl]
        w_ref, m_ref, v_ref, g_ref, d_ref, nm_ref, nv_ref = refs[nl:]
        layer = pl.program_id(0)
        for li in range(nl):
            @pl.when(layer == li)
            def _():
                g = r_refs[li][0].astype(F32)
                for d in range(1, NDEV):
                    g = g + r_refs[li][d].astype(F32)
                delta, nm, nv = _adamw_math(w_ref[...], g, m_ref[...], v_ref[...])
                g_ref[...] = g
                d_ref[...] = delta
                nm_ref[...] = nm
                nv_ref[...] = nv

    o = jax.ShapeDtypeStruct(w.shape, F32)
    out = pl.pallas_call(
        body, grid=(nl, 2), out_shape=(o, o, o, o),
        in_specs=[pl.BlockSpec((NDEV, r, td), lambda l, j: (0, 0, j))] * nl + [wspec] * 3,
        out_specs=(wspec,) * 4, compiler_params=_cp(2), name=name)(*recvs, w, m, v)
    return out


def _adamw(w, g, m, v, name):
    r, c = w.shape

    def body(w_ref, g_ref, m_ref, v_ref, d_ref, nm_ref, nv_ref):
        d_ref[...], nm_ref[...], nv_ref[...] = _adamw_math(w_ref[...], g_ref[...], m_ref[...], v_ref[...])

    spec = pl.BlockSpec((r, c), lambda i: (0, 0))
    out = jax.ShapeDtypeStruct((r, c), F32)
    return pl.pallas_call(
        body, grid=(1,), out_shape=(out, out, out), in_specs=[spec] * 4, out_specs=(spec,) * 3,
        compiler_params=_cp(), name=name)(w, g, m, v)


def _pack_small(parts, rows):
    flat = jnp.concatenate([p.reshape(-1) for p in parts])
    return jnp.pad(flat, (0, rows * 128 - flat.shape[0])).reshape(rows, 128)


def _unpack_small(packed, shapes):
    flat = packed.reshape(-1)
    out, o = [], 0
    for s in shapes:
        n = int(np.prod(s))
        out.append(flat[o:o + n].reshape(s))
        o += n
    return out


def kernel(x, norm_mix, norm_ffn, even_w_in, even_v_ln_g, even_v_ln_b, even_w_spatial, even_b_spatial, even_conv_w, even_w_out, attn_w_qkv, attn_sink, rel_bias, attn_w_out, ffn_w_gate, ffn_w_up, ffn_w_down, final_norm, loss_target, m_norm_mix, m_norm_ffn, m_even_w_in, m_even_v_ln_g, m_even_v_ln_b, m_even_w_spatial, m_even_b_spatial, m_even_conv_w, m_even_w_out, m_attn_w_qkv, m_attn_sink, m_rel_bias, m_attn_w_out, m_ffn_w_gate, m_ffn_w_up, m_ffn_w_down, m_final_norm, v_norm_mix, v_norm_ffn, v_even_w_in, v_even_v_ln_g, v_even_v_ln_b, v_even_w_spatial, v_even_b_spatial, v_even_conv_w, v_even_w_out, v_attn_w_qkv, v_attn_sink, v_rel_bias, v_attn_w_out, v_ffn_w_gate, v_ffn_w_up, v_ffn_w_down, v_final_norm):
    t, dm = x.shape[1], x.shape[2]
    aw = even_v_ln_g.shape[1]
    bw = even_conv_w.shape[2] * NDEV
    gd = aw // A_GROUPS
    tm = min(512, t // 2)
    tmf = min(256, t // 2)
    me = _my_index()
    row = lambda a: a.reshape(1, -1)

    colT = lambda w: w.T.astype(BF16)
    sh = dict(winT=colT(even_w_in[0]), wqkvT=colT(attn_w_qkv[0]), wgT0=colT(ffn_w_gate[0]), wuT0=colT(ffn_w_up[0]),
              wgT1=colT(ffn_w_gate[1]), wuT1=colT(ffn_w_up[1]), woe=even_w_out[0].astype(BF16),
              woa=attn_w_out[0].astype(BF16), wd0=ffn_w_down[0].astype(BF16), wd1=ffn_w_down[1].astype(BF16))
    gather = lambda names: _GatherCarry([sh[n] for n in names])

    cw_rows = 3 * bw // 128
    cw_mine = lax.dynamic_update_slice(jnp.zeros((3, bw), F32), even_conv_w[0], (0, me * (bw // NDEV)))
    cw_full = _all_reduce_small(jnp.pad(cw_mine.reshape(cw_rows, 128), ((0, 16 - cw_rows), (0, 0))),
                                "gather_conv_w")[0:cw_rows].reshape(3, bw)

    x0 = x[0]
    wsp_b = even_w_spatial[0].astype(BF16)
    bspb = jnp.broadcast_to(even_b_spatial[0][:, :, None], (A_GROUPS, CHUNK, gd))
    buckets = _bucket_table()
    bias = _bias_table(rel_bias.T, buckets)
    sink = attn_sink[0]

    (g_winT,) = _exchange_only(gather(["winT"]), "ag_w_in")
    (proj, h0b), (g_woe, g_wgT0) = _call(_norm_proj(x0, row(norm_mix[0]), g_winT, F32, "in_proj", tm),
                                         gather(["woe", "wgT0"]))
    (x1, yb), (g_wuT0,) = _call(_even_core_fwd(proj, x0, even_v_ln_g, even_v_ln_b, wsp_b, bspb, cw_full, g_woe, tm),
                                gather(["wuT0"]))
    (gate0, up0, act0), (g_wd0,) = _call(_ffn_up(x1, row(norm_ffn[0]), g_wgT0, g_wuT0, "ffn_up0", tmf), gather(["wd0"]))
    (x2,), (g_wqkvT,) = _call(_ffn_down(x1, act0, g_wd0, "ffn_down0", tm), gather(["wqkvT"]))
    (qkv, h2b), (g_woa,) = _call(_norm_proj(x2, row(norm_mix[1]), g_wqkvT, BF16, "qkv_proj", tm), gather(["woa"]))
    (x3, attb), (g_wgT1, g_wuT1) = _call(_attn_fwd(qkv, x2, bias, sink, g_woa), gather(["wgT1", "wuT1"]))
    (gate1, up1, act1), (g_wd1,) = _call(_ffn_up(x3, row(norm_ffn[1]), g_wgT1, g_wuT1, "ffn_up1", tmf), gather(["wd1"]))
    (x4,), _ = _call(_ffn_down(x3, act1, g_wd1, "ffn_down1", tm))
    (loss_part, dx4, dx4b, d_final), _ = _call(_final_loss(x4, loss_target[0], row(final_norm), tm))

    (dx3, dx3b, dg1, du1, h3b, d_nffn1), _ = _call(
        _ffn_bwd(dx4, x3, gate1, up1, row(norm_ffn[1]), g_wgT1, g_wuT1, g_wd1, "ffn_bwd1", tmf))
    (p_wgT1,), _ = _call(_wgrad(dg1, h3b, "wgrad_gate1"))
    (p_wuT1,), _ = _call(_wgrad(du1, h3b, "wgrad_up1"))
    (p_wd1,), _ = _call(_wgrad(act1, dx4b, "wgrad_down1"))
    (dq, dk, dv, dbias, dsink), (r_wgT1, r_wuT1, r_wd1) = _call(
        _attn_bwd(qkv, attb, dx3, bias, sink, g_woa), _GradCarry([p_wgT1, p_wuT1, p_wd1]))
    (p_woa,), _ = _call(_wgrad(attb, dx3b, "wgrad_attn_out"))
    d_relb = _rel_bias_grad(dbias, buckets)[:, 0:N_BUCKETS].T
    dqkv = jnp.concatenate([dq, dk.astype(BF16), dv.astype(BF16)], axis=1)
    (dx2, dx2b, d_nmix1), _ = _call(_proj_bwd_norm(dqkv, x2, row(norm_mix[1]), dx3, g_wqkvT, "qkv_bwd", tm))
    (p_wqkvT,), _ = _call(_wgrad(dqkv, h2b, "wgrad_qkv"))
    (dx1, dx1b, dg0, du0, h1b, d_nffn0), (r_woa, r_wqkvT) = _call(
        _ffn_bwd(dx2, x1, gate0, up0, row(norm_ffn[0]), g_wgT0, g_wuT0, g_wd0, "ffn_bwd0", tmf),
        _GradCarry([p_woa, p_wqkvT]))
    (p_woe,), _ = _call(_wgrad(yb, dx1b, "wgrad_even_out"))
    (p_wgT0,), (r_woe,) = _call(_wgrad(dg0, h1b, "wgrad_gate0"), _GradCarry([p_woe]))
    (p_wuT0,), (r_wgT0,) = _call(_wgrad(du0, h1b, "wgrad_up0"), _GradCarry([p_wgT0]))
    (p_wd0,), (r_wuT0,) = _call(_wgrad(act0, dx2b, "wgrad_down0"), _GradCarry([p_wuT0]))
    (dproj, d_lng, d_lnb, d_wsp, d_bsp3, d_cw), (r_wd0,) = _call(
        _even_core_bwd(proj, dx1, even_v_ln_g, even_v_ln_b, wsp_b, bspb, cw_full, g_woe, tm), _GradCarry([p_wd0]))
    (p_winT,), _ = _call(_wgrad(dproj, h0b, "wgrad_in"))
    (dx0, _, d_nmix0), (r_winT,) = _call(
        _proj_bwd_norm(dproj, x0, row(norm_mix[0]), dx1, g_winT, "in_proj_bwd", tm), _GradCarry([p_winT]))

    small_shapes = [(2, dm), (2, dm), (1, aw), (1, aw), (1, A_GROUPS, CHUNK, CHUNK), (1, A_GROUPS, CHUNK), (3, bw),
                    (1, N_HEADS), (N_BUCKETS, N_HEADS), (dm,)]
    n_small = sum(int(np.prod(s)) for s in small_shapes)
    small_rows = 8 * ((n_small + 1023) // 1024)
    small_part = _pack_small(
        [jnp.concatenate([d_nmix0, d_nmix1]), jnp.concatenate([d_nffn0, d_nffn1]), d_lng, d_lnb, d_wsp,
         jnp.sum(d_bsp3, axis=-1), d_cw, dsink[:, 0:N_HEADS], d_relb, d_final], small_rows)
    small_sum = _unpack_small(_all_reduce_small(small_part, "all_reduce_small_grads"), small_shapes)
    grads = {}
    (grads["norm_mix"], grads["norm_ffn"], grads["even_v_ln_g"], grads["even_v_ln_b"], grads["even_w_spatial"],
     grads["even_b_spatial"], g_cw_full, grads["attn_sink"], grads["rel_bias"], grads["final_norm"]) = small_sum
    grads["even_conv_w"] = lax.dynamic_slice(g_cw_full, (0, me * (bw // NDEV)), (3, bw // NDEV))[None]

    order = ["norm_mix", "norm_ffn", "even_w_in", "even_v_ln_g", "even_v_ln_b", "even_w_spatial", "even_b_spatial",
             "even_conv_w", "even_w_out", "attn_w_qkv", "attn_sink", "rel_bias", "attn_w_out", "ffn_w_gate",
             "ffn_w_up", "ffn_w_down", "final_norm"]
    ws = dict(norm_mix=norm_mix, norm_ffn=norm_ffn, even_w_in=even_w_in, even_v_ln_g=even_v_ln_g,
              even_v_ln_b=even_v_ln_b, even_w_spatial=even_w_spatial, even_b_spatial=even_b_spatial,
              even_conv_w=even_conv_w, even_w_out=even_w_out, attn_w_qkv=attn_w_qkv, attn_sink=attn_sink,
              rel_bias=rel_bias, attn_w_out=attn_w_out, ffn_w_gate=ffn_w_gate, ffn_w_up=ffn_w_up,
              ffn_w_down=ffn_w_down, final_norm=final_norm)
    ms = dict(norm_mix=m_norm_mix, norm_ffn=m_norm_ffn, even_w_in=m_even_w_in, even_v_ln_g=m_even_v_ln_g,
              even_v_ln_b=m_even_v_ln_b, even_w_spatial=m_even_w_spatial, even_b_spatial=m_even_b_spatial,
              even_conv_w=m_even_conv_w, even_w_out=m_even_w_out, attn_w_qkv=m_attn_w_qkv, attn_sink=m_attn_sink,
              rel_bias=m_rel_bias, attn_w_out=m_attn_w_out, ffn_w_gate=m_ffn_w_gate, ffn_w_up=m_ffn_w_up,
              ffn_w_down=m_ffn_w_down, final_norm=m_final_norm)
    vs = dict(norm_mix=v_norm_mix, norm_ffn=v_norm_ffn, even_w_in=v_even_w_in, even_v_ln_g=v_even_v_ln_g,
              even_v_ln_b=v_even_v_ln_b, even_w_spatial=v_even_w_spatial, even_b_spatial=v_even_b_spatial,
              even_conv_w=v_even_conv_w, even_w_out=v_even_w_out, attn_w_qkv=v_attn_w_qkv, attn_sink=v_attn_sink,
              rel_bias=v_rel_bias, attn_w_out=v_attn_w_out, ffn_w_gate=v_ffn_w_gate, ffn_w_up=v_ffn_w_up,
              ffn_w_down=v_ffn_w_down, final_norm=v_final_norm)
    big = dict(even_w_in=([r_winT], True), even_w_out=([r_woe], False), attn_w_qkv=([r_wqkvT], True),
               attn_w_out=([r_woa], False), ffn_w_gate=([r_wgT0, r_wgT1], True), ffn_w_up=([r_wuT0, r_wuT1], True),
               ffn_w_down=([r_wd0, r_wd1], False))
    delta, new_m, new_v = {}, {}, {}
    for n, (recvs, transposed) in big.items():
        lay = (lambda a: jnp.swapaxes(a, 1, 2)) if transposed else (lambda a: a)
        outs = _finish_weight(recvs, lay(ws[n]), lay(ms[n]), lay(vs[n]), "finish_" + n)
        grads[n], delta[n], new_m[n], new_v[n] = [lay(o) for o in outs]
    small = [n for n in order if n not in big]
    sshapes = [ws[n].shape for n in small]
    ns = sum(int(np.prod(s)) for s in sshapes)
    srows = 8 * ((ns + 1023) // 1024)
    pk = lambda dct: _pack_small([dct[n] for n in small], srows)
    d, nm, nv = _adamw(pk(ws), pk(grads), pk(ms), pk(vs), "adamw_small")
    for n, a, b, c2 in zip(small, _unpack_small(d, sshapes), _unpack_small(nm, sshapes), _unpack_small(nv, sshapes)):
        delta[n], new_m[n], new_v[n] = a, b, c2

    loss = lax.psum(loss_part[0, 0], ("x", "y", "c"))
    return (loss, dx0[None], *[grads[n] for n in order], *[delta[n] for n in order],
            *[new_m[n] for n in order], *[new_v[n] for n in order])
```

```python
import math

import jax
import jax.numpy as jnp
import numpy as np
from jax import lax
from jax.experimental import pallas as pl
from jax.experimental.pallas import tpu as pltpu

F32, BF16 = jnp.float32, jnp.bfloat16
NDEV = 8
EPS = 1e-6
CHUNK = 128
A_GROUPS = 4
N_HEADS, N_KV, HEAD_DIM = 16, 4, 64
N_BUCKETS, MAX_DISTANCE = 32, 128
NEG = -1e30
ADAM_LR, ADAM_B1, ADAM_B2, ADAM_EPS, ADAM_WD, ADAM_STEP = 0.001, 0.9, 0.999, 1e-08, 0.01, 10
VMEM_LIMIT = 56 * 1024 * 1024
MESH = pl.DeviceIdType.MESH
NT = (((1,), (1,)), ((), ()))
NN = (((1,), (0,)), ((), ()))
TN = (((0,), (0,)), ((), ()))
ANY = pl.BlockSpec(memory_space=pl.ANY)


def _cp(n_grid=1):
    return pltpu.CompilerParams(dimension_semantics=("arbitrary",) * n_grid, vmem_limit_bytes=VMEM_LIMIT)


def _dot(a, b, dims):
    return lax.dot_general(a, b, dims, preferred_element_type=F32)


def _my_index():
    return 4 * lax.axis_index("x") + 2 * lax.axis_index("y") + lax.axis_index("c")


def _peer(k):
    x, y, c = lax.axis_index("x"), lax.axis_index("y"), lax.axis_index("c")
    px = 1 - x if k & 4 else x
    py = 1 - y if k & 2 else y
    pc = 1 - c if k & 1 else c
    return (px, py, pc)


def _load_weight(gath_ref, wbuf, sems):
    rows = gath_ref.shape[1]
    cps = [pltpu.make_async_copy(gath_ref.at[d], wbuf.at[pl.ds(d * rows, rows), :], sems.at[d]) for d in range(NDEV)]
    for c in cps:
        c.start()
    for c in cps:
        c.wait()


class _GatherCarry:
    def __init__(self, pieces):
        self.inputs = list(pieces)
        self.n = len(pieces)
        self.out_shape = [jax.ShapeDtypeStruct((NDEV,) + p.shape, p.dtype) for p in pieces]
        self.scratch = [pltpu.SemaphoreType.DMA((7 * self.n,)), pltpu.SemaphoreType.DMA((7 * self.n,)),
                        pltpu.SemaphoreType.DMA((self.n,))]

    def _ctx(self):
        x, y, c = lax.axis_index("x"), lax.axis_index("y"), lax.axis_index("c")
        chips = [(1 - x, y), (x, 1 - y), (1 - x, 1 - y)]
        return (x, y, c), (x, y, 1 - c), chips, c

    def _copy(self, k, j, block, to, ins, outs, sems, src=None):
        send_sems, recv_sems, _ = sems
        slot = outs[j].at[4 * block[0] + 2 * block[1] + block[2]]
        return pltpu.make_async_remote_copy(
            src_ref=slot if src is None else src, dst_ref=slot, send_sem=send_sems.at[k * self.n + j],
            recv_sem=recv_sems.at[k * self.n + j], device_id=to, device_id_type=MESH)

    def start(self, ins, outs, sems):
        me, sibling, chips, c = self._ctx()
        for j in range(self.n):
            pltpu.make_async_copy(ins[j], outs[j].at[4 * me[0] + 2 * me[1] + me[2]], sems[2].at[j]).start()
            self._copy(0, j, me, sibling, ins, outs, sems, src=ins[j]).start()
            for q, chip in enumerate(chips):
                self._copy(1 + q, j, me, (*chip, c), ins, outs, sems, src=ins[j]).start()

    def mid(self, ins, outs, sems):
        me, sibling, chips, c = self._ctx()
        for q, chip in enumerate(chips):
            for j in range(self.n):
                self._copy(1 + q, j, (*chip, c), me, ins, outs, sems).wait_recv()
                self._copy(4 + q, j, (*chip, c), sibling, ins, outs, sems).start()

    def finish(self, ins, outs, sems):
        me, sibling, chips, c = self._ctx()
        for j in range(self.n):
            self._copy(0, j, sibling, me, ins, outs, sems).wait_recv()
            for q, chip in enumerate(chips):
                self._copy(4 + q, j, (*chip, 1 - c), me, ins, outs, sems).wait_recv()
        for j in range(self.n):
            self._copy(0, j, me, sibling, ins, outs, sems, src=ins[j]).wait_send()
            for q, chip in enumerate(chips):
                self._copy(1 + q, j, me, (*chip, c), ins, outs, sems, src=ins[j]).wait_send()
                self._copy(4 + q, j, (*chip, c), sibling, ins, outs, sems).wait_send()
            pltpu.make_async_copy(ins[j], outs[j].at[0], sems[2].at[j]).wait()


class _GradCarry:
    def __init__(self, pieces):
        self.inputs = list(pieces)
        self.n = len(pieces)
        self.rows = [p.shape[0] // NDEV for p in pieces]
        self.out_shape = [jax.ShapeDtypeStruct((NDEV, r, p.shape[1]), p.dtype) for p, r in zip(pieces, self.rows)]
        self.scratch = [pltpu.SemaphoreType.DMA((7 * self.n,)), pltpu.SemaphoreType.DMA((7 * self.n,)),
                        pltpu.SemaphoreType.DMA((self.n,))]

    def _copies(self, ins, outs, sems):
        me = _my_index()
        local, remote = [], []
        for j in range(self.n):
            r = self.rows[j]
            local.append(pltpu.make_async_copy(ins[j].at[pl.ds(pl.multiple_of(me * r, 16), r), :], outs[j].at[me],
                                               sems[2].at[j]))
            for k in range(1, NDEV):
                peer = _peer(k)
                pidx = 4 * peer[0] + 2 * peer[1] + peer[2]
                remote.append(pltpu.make_async_remote_copy(
                    src_ref=ins[j].at[pl.ds(pl.multiple_of(pidx * r, 16), r), :], dst_ref=outs[j].at[me],
                    send_sem=sems[0].at[(k - 1) * self.n + j], recv_sem=sems[1].at[(k - 1) * self.n + j],
                    device_id=peer, device_id_type=MESH))
        return local, remote

    def start(self, ins, outs, sems):
        local, remote = self._copies(ins, outs, sems)
        for cp in local + remote:
            cp.start()

    def mid(self, ins, outs, sems):
        pass

    def finish(self, ins, outs, sems):
        local, remote = self._copies(ins, outs, sems)
        for cp in remote + local:
            cp.wait()


def _call(spec, carry=None):
    body, grid = spec["body"], spec["grid"]
    in_specs, out_specs, out_shape = list(spec["in_specs"]), list(spec["out_specs"]), list(spec["out_shape"])
    scratch, args = list(spec.get("scratch", [])), list(spec["args"])
    if carry is None:
        out = pl.pallas_call(body, grid=grid, in_specs=in_specs, out_specs=tuple(out_specs),
                             out_shape=tuple(out_shape), scratch_shapes=scratch, compiler_params=_cp(len(grid)),
                             name=spec["name"])(*args)
        return tuple(out), ()
    n_in, n_out, n_s = len(in_specs), len(out_specs), len(scratch)
    c_in, c_out = len(carry.inputs), len(carry.out_shape)
    steps = int(np.prod(grid))

    def wrapped(*refs):
        o = 0
        ins = refs[o:o + n_in]; o += n_in
        cins = refs[o:o + c_in]; o += c_in
        outs = refs[o:o + n_out]; o += n_out
        couts = refs[o:o + c_out]; o += c_out
        scr = refs[o:o + n_s]; o += n_s
        sems = refs[o:]
        step = pl.program_id(0)
        for ax in range(1, len(grid)):
            step = step * grid[ax] + pl.program_id(ax)

        @pl.when(step == 0)
        def _():
            carry.start(cins, couts, sems)
        if steps >= 3:
            @pl.when(step == steps - 2)
            def _():
                carry.mid(cins, couts, sems)
        body(*ins, *outs, *scr)

        @pl.when(step == steps - 1)
        def _():
            if steps < 3:
                carry.mid(cins, couts, sems)
            carry.finish(cins, couts, sems)

    out = pl.pallas_call(
        wrapped, grid=grid, in_specs=in_specs + [ANY] * c_in, out_specs=tuple(out_specs + [ANY] * c_out),
        out_shape=tuple(out_shape + carry.out_shape), scratch_shapes=scratch + carry.scratch,
        compiler_params=_cp(len(grid)), name=spec["name"])(*args, *carry.inputs)
    return tuple(out[:n_out]), tuple(out[n_out:])


def _exchange_only(carry, name):
    spec = dict(body=lambda: None, grid=(1,), in_specs=[], out_specs=[], out_shape=[], args=[], name=name)
    return _call(spec, carry)[1]


def _all_reduce_small(part, name):
    r, lanes = part.shape

    def body(p_ref, o_ref, buf, send_sems, recv_sems):
        me = _my_index()
        buf[me] = p_ref[...]
        cps = []
        for k in range(1, NDEV):
            cp = pltpu.make_async_remote_copy(
                src_ref=p_ref, dst_ref=buf.at[me], send_sem=send_sems.at[k - 1], recv_sem=recv_sems.at[k - 1],
                device_id=_peer(k), device_id_type=MESH)
            cp.start()
            cps.append(cp)
        for cp in cps:
            cp.wait()
        acc = buf[0]
        for d in range(1, NDEV):
            acc = acc + buf[d]
        o_ref[...] = acc

    return pl.pallas_call(
        body, out_shape=jax.ShapeDtypeStruct((r, lanes), F32),
        in_specs=[pl.BlockSpec(memory_space=pltpu.VMEM)], out_specs=pl.BlockSpec(memory_space=pltpu.VMEM),
        scratch_shapes=[pltpu.VMEM((NDEV, r, lanes), F32), pltpu.SemaphoreType.DMA((7,)), pltpu.SemaphoreType.DMA((7,))],
        name=name)(part)


def _rms_fwd(x, gain):
    r = lax.rsqrt(jnp.mean(x * x, axis=-1, keepdims=True) + EPS)
    return x * r * gain, r


def _rms_bwd(dh, x, r, gain):
    a = dh * gain
    dx = r * a - x * (r * r * r) * jnp.mean(a * x, axis=-1, keepdims=True)
    dgain = jnp.sum(dh * (x * r), axis=0, keepdims=True)
    return dx, dgain


def _gelu(x):
    return 0.5 * x * (1.0 + lax.erf(x * 0.7071067811865476))


def _gelu_grad(x):
    return 0.5 * (1.0 + lax.erf(x * 0.7071067811865476)) + x * jnp.exp(-0.5 * x * x) * 0.3989422804014327


def _sigmoid(x):
    return 1.0 / (1.0 + jnp.exp(-x))


def _adamw_math(w, g, m, v):
    nm = ADAM_B1 * m + (1.0 - ADAM_B1) * g
    nv = ADAM_B2 * v + (1.0 - ADAM_B2) * (g * g)
    m_hat = nm / (1.0 - ADAM_B1 ** ADAM_STEP)
    v_hat = nv / (1.0 - ADAM_B2 ** ADAM_STEP)
    return -ADAM_LR * (m_hat / (jnp.sqrt(v_hat) + ADAM_EPS) + ADAM_WD * w), nm, nv


def _tok(tm, w):
    return pl.BlockSpec((tm, w), lambda i: (i, 0))


def _full(shape):
    return pl.BlockSpec(shape, lambda *i: (0,) * len(shape))


def _norm_proj(x, gain, gath, out_dtype, name, tm):
    t, dm = x.shape
    n = gath.shape[1] * NDEV

    def body(x_ref, g_ref, gath_ref, proj_ref, hb_ref, wbuf, sems):
        @pl.when(pl.program_id(0) == 0)
        def _():
            _load_weight(gath_ref, wbuf, sems)
        h, _ = _rms_fwd(x_ref[...], g_ref[...])
        hb = h.astype(BF16)
        hb_ref[...] = hb
        proj_ref[...] = _dot(hb, wbuf[...], NT).astype(out_dtype)

    return dict(
        body=body, grid=(t // tm,), name=name, args=[x, gain, gath],
        out_shape=[jax.ShapeDtypeStruct((t, n), out_dtype), jax.ShapeDtypeStruct((t, dm), BF16)],
        in_specs=[_tok(tm, dm), _full((1, dm)), ANY], out_specs=[_tok(tm, n), _tok(tm, dm)],
        scratch=[pltpu.VMEM((n, dm), BF16), pltpu.SemaphoreType.DMA((NDEV,))])


def _proj_bwd_norm(dy, x, gain, dres, gath, name, tm):
    t, dm = x.shape
    n = gath.shape[1] * NDEV

    def body(dy_ref, x_ref, g_ref, dres_ref, gath_ref, dx_ref, dxb_ref, dgain_ref, wbuf, sems):
        @pl.when(pl.program_id(0) == 0)
        def _():
            _load_weight(gath_ref, wbuf, sems)
            dgain_ref[...] = jnp.zeros_like(dgain_ref)
        xv, gain_v = x_ref[...], g_ref[...]
        _, r = _rms_fwd(xv, gain_v)
        dh = _dot(dy_ref[...], wbuf[...], NN)
        dx, dgain = _rms_bwd(dh, xv, r, gain_v)
        dx = dres_ref[...] + dx
        dx_ref[...] = dx
        dxb_ref[...] = dx.astype(BF16)
        dgain_ref[...] += dgain

    return dict(
        body=body, grid=(t // tm,), name=name, args=[dy, x, gain, dres, gath],
        out_shape=[jax.ShapeDtypeStruct((t, dm), F32), jax.ShapeDtypeStruct((t, dm), BF16),
                   jax.ShapeDtypeStruct((1, dm), F32)],
        in_specs=[_tok(tm, n), _tok(tm, dm), _full((1, dm)), _tok(tm, dm), ANY],
        out_specs=[_tok(tm, dm), _tok(tm, dm), _full((1, dm))],
        scratch=[pltpu.VMEM((n, dm), BF16), pltpu.SemaphoreType.DMA((NDEV,))])


def _wgrad(a, b, name, tmm=256):
    t, m = a.shape
    n = b.shape[1]

    def body(a_ref, b_ref, o_ref):
        o_ref[...] = _dot(a_ref[...], b_ref[...], TN).astype(BF16)

    return dict(
        body=body, grid=(m // tmm,), name=name, args=[a, b], out_shape=[jax.ShapeDtypeStruct((m, n), BF16)],
        in_specs=[pl.BlockSpec((t, tmm), lambda j: (0, j)), pl.BlockSpec((t, n), lambda j: (0, 0))],
        out_specs=[pl.BlockSpec((tmm, n), lambda j: (j, 0))])


def _halo_specs(tm, t, width, col_blocks):
    nb8 = tm // 8
    last = t // 8 - 1
    prev = [pl.BlockSpec((8, width), lambda i, cb=cb: (jnp.maximum(i * nb8 - 1, 0), cb)) for cb in col_blocks]
    nxt = [pl.BlockSpec((8, width), lambda i, cb=cb: (jnp.minimum((i + 1) * nb8, last), cb)) for cb in col_blocks]
    return prev, nxt


def _shift_rows(z, prev_row, next_row):
    tm = z.shape[0]
    row = lax.broadcasted_iota(jnp.int32, z.shape, 0)
    zm1 = jnp.where(row == 0, prev_row, pltpu.roll(z, 1, 0))
    zp1 = jnp.where(row == tm - 1, next_row, pltpu.roll(z, tm - 1, 0))
    return zm1, zp1


def _gating_fwd(proj, lng, lnb, wsp_ref, bsp_ref, aw):
    tm = proj.shape[0]
    a_u = _gelu(proj[:, 0:aw])
    gv = _gelu(proj[:, aw:2 * aw])
    mu = jnp.mean(gv, axis=-1, keepdims=True)
    xc = gv - mu
    rstd = lax.rsqrt(jnp.mean(xc * xc, axis=-1, keepdims=True) + EPS)
    vn = xc * rstd
    a_v = (vn * lng + lnb).astype(BF16)
    gd = aw // A_GROUPS
    rows = []
    for c in range(tm // CHUNK):
        cols = []
        for g in range(A_GROUPS):
            blk = a_v[c * CHUNK:(c + 1) * CHUNK, g * gd:(g + 1) * gd]
            cols.append(_dot(wsp_ref[g], blk, NN) + bsp_ref[g])
        rows.append(jnp.concatenate(cols, axis=1))
    mixed = jnp.concatenate(rows, axis=0)
    return a_u, vn, rstd, a_v, mixed


def _even_core_fwd(proj, x0, lng, lnb, wsp, bspb, cw, gath, tm):
    t, dm = x0.shape
    aw = lng.shape[1]
    bw = cw.shape[1]
    assert aw == bw and 2 * aw + 3 * bw == proj.shape[1]
    nt = t // tm
    prev, nxt = _halo_specs(tm, t, bw, [3, 4])

    def body(proj_ref, cp_ref, hp_ref, cn_ref, hn_ref, x0_ref, lng_ref, lnb_ref, wsp_ref, bsp_ref, cw_ref, gath_ref,
             x1_ref, y_ref, wbuf, sems):
        i = pl.program_id(0)

        @pl.when(i == 0)
        def _():
            _load_weight(gath_ref, wbuf, sems)
        proj_v = proj_ref[...]
        a_u, _, _, _, mixed = _gating_fwd(proj_v, lng_ref[...], lnb_ref[...], wsp_ref, bsp_ref, aw)
        a_out = a_u * mixed
        bb = proj_v[:, 2 * aw:2 * aw + bw]
        z = proj_v[:, 2 * aw + bw:2 * aw + 2 * bw] * proj_v[:, 2 * aw + 2 * bw:]
        zprev = jnp.where(i > 0, cp_ref[7:8, :] * hp_ref[7:8, :], 0.0)
        znext = jnp.where(i < nt - 1, cn_ref[0:1, :] * hn_ref[0:1, :], 0.0)
        zm1, zp1 = _shift_rows(z, zprev, znext)
        cwv = cw_ref[...]
        conv = zm1 * cwv[0:1, :] + z * cwv[1:2, :] + zp1 * cwv[2:3, :]
        y = jnp.concatenate([a_out, bb * conv], axis=1).astype(BF16)
        y_ref[...] = y
        x1_ref[...] = x0_ref[...] + _dot(y, wbuf[...], NN)

    return dict(
        body=body, grid=(nt,), name="even_core_fwd",
        args=[proj, proj, proj, proj, proj, x0, lng, lnb, wsp, bspb, cw, gath],
        out_shape=[jax.ShapeDtypeStruct((t, dm), F32), jax.ShapeDtypeStruct((t, aw + bw), BF16)],
        in_specs=[_tok(tm, proj.shape[1]), prev[0], prev[1], nxt[0], nxt[1], _tok(tm, dm), _full(lng.shape),
                  _full(lnb.shape), _full(wsp.shape), _full(bspb.shape), _full(cw.shape), ANY],
        out_specs=[_tok(tm, dm), _tok(tm, aw + bw)],
        scratch=[pltpu.VMEM((gath.shape[1] * NDEV, dm), BF16), pltpu.SemaphoreType.DMA((NDEV,))])


def _even_core_bwd(proj, dx1, lng, lnb, wsp, bspb, cw, gath, tm):
    t, dm = dx1.shape
    aw, bw = lng.shape[1], cw.shape[1]
    gd = aw // A_GROUPS
    nt = t // tm
    inw = proj.shape[1]
    prev, nxt = _halo_specs(tm, t, bw, [2, 3, 4])
    nb8 = tm // 8
    last8 = t // 8 - 1

    def body(proj_ref, bp_ref, cp_ref, hp_ref, bn_ref, cn_ref, hn_ref, dx_ref, dxp_ref, dxn_ref,
             lng_ref, lnb_ref, wsp_ref, bsp_ref, cw_ref, gath_ref,
             dproj_ref, dlng_ref, dlnb_ref, dwsp_ref, dbsp_ref, dcw_ref, wbuf, sems):
        i = pl.program_id(0)

        @pl.when(i == 0)
        def _():
            _load_weight(gath_ref, wbuf, sems)
            dlng_ref[...] = jnp.zeros_like(dlng_ref)
            dlnb_ref[...] = jnp.zeros_like(dlnb_ref)
            dwsp_ref[...] = jnp.zeros_like(dwsp_ref)
            dbsp_ref[...] = jnp.zeros_like(dbsp_ref)
            dcw_ref[...] = jnp.zeros_like(dcw_ref)
        proj_v = proj_ref[...]
        lng_v = lng_ref[...]
        a_u, vn, rstd, a_v, mixed = _gating_fwd(proj_v, lng_v, lnb_ref[...], wsp_ref, bsp_ref, aw)
        w = wbuf[...]
        dy = _dot(dx_ref[...].astype(BF16), w, NT)
        da_out, db_out = dy[:, 0:aw], dy[:, aw:]
        da_u = da_out * mixed
        dmixed = da_out * a_u
        dmb = dmixed.astype(BF16)
        rows = []
        for c in range(tm // CHUNK):
            cols = []
            for g in range(A_GROUPS):
                r0, c0 = c * CHUNK, g * gd
                dm_cg = dmb[r0:r0 + CHUNK, c0:c0 + gd]
                cols.append(_dot(wsp_ref[g], dm_cg, TN))
                dwsp_ref[g] += _dot(dm_cg, a_v[r0:r0 + CHUNK, c0:c0 + gd], NT)
                dbsp_ref[g] += dmixed[r0:r0 + CHUNK, c0:c0 + gd]
            rows.append(jnp.concatenate(cols, axis=1))
        dav = jnp.concatenate(rows, axis=0)
        dlng_ref[...] += jnp.sum(dav * vn, axis=0, keepdims=True)
        dlnb_ref[...] += jnp.sum(dav, axis=0, keepdims=True)
        dvn = dav * lng_v
        dgv = rstd * (dvn - jnp.mean(dvn, axis=-1, keepdims=True) - vn * jnp.mean(dvn * vn, axis=-1, keepdims=True))
        dv_pre = dgv * _gelu_grad(proj_v[:, aw:2 * aw])
        du_pre = da_u * _gelu_grad(proj_v[:, 0:aw])
        bb = proj_v[:, 2 * aw:2 * aw + bw]
        bc = proj_v[:, 2 * aw + bw:2 * aw + 2 * bw]
        bh = proj_v[:, 2 * aw + 2 * bw:]
        z = bc * bh
        zprev = jnp.where(i > 0, cp_ref[7:8, :] * hp_ref[7:8, :], 0.0)
        znext = jnp.where(i < nt - 1, cn_ref[0:1, :] * hn_ref[0:1, :], 0.0)
        zm1, zp1 = _shift_rows(z, zprev, znext)
        cwv = cw_ref[...]
        conv = zm1 * cwv[0:1, :] + z * cwv[1:2, :] + zp1 * cwv[2:3, :]
        dbb = db_out * conv
        dconv = db_out * bb
        dx_edge = jnp.concatenate([dxp_ref[...], dxn_ref[...]], axis=0).astype(BF16)
        dy_edge = _dot(dx_edge, w[aw:, :], NT)
        dcprev = jnp.where(i > 0, dy_edge[7:8, :] * bp_ref[7:8, :], 0.0)
        dcnext = jnp.where(i < nt - 1, dy_edge[8:9, :] * bn_ref[0:1, :], 0.0)
        dcm1, dcp1 = _shift_rows(dconv, dcprev, dcnext)
        dz = dcp1 * cwv[0:1, :] + dconv * cwv[1:2, :] + dcm1 * cwv[2:3, :]
        dcw_ref[0:1, :] += jnp.sum(dconv * zm1, axis=0, keepdims=True)
        dcw_ref[1:2, :] += jnp.sum(dconv * z, axis=0, keepdims=True)
        dcw_ref[2:3, :] += jnp.sum(dconv * zp1, axis=0, keepdims=True)
        dproj_ref[...] = jnp.concatenate([du_pre, dv_pre, dbb, dz * bh, dz * bc], axis=1).astype(BF16)

    row8 = lambda f: pl.BlockSpec((8, dm), f)
    return dict(
        body=body, grid=(nt,), name="even_core_bwd",
        args=[proj, proj, proj, proj, proj, proj, proj, dx1, dx1, dx1, lng, lnb, wsp, bspb, cw, gath],
        out_shape=[jax.ShapeDtypeStruct((t, inw), BF16), jax.ShapeDtypeStruct((1, aw), F32),
                   jax.ShapeDtypeStruct((1, aw), F32), jax.ShapeDtypeStruct(wsp.shape, F32),
                   jax.ShapeDtypeStruct((A_GROUPS, CHUNK, gd), F32), jax.ShapeDtypeStruct(cw.shape, F32)],
        in_specs=[_tok(tm, inw), prev[0], prev[1], prev[2], nxt[0], nxt[1], nxt[2], _tok(tm, dm),
                  row8(lambda i: (jnp.maximum(i * nb8 - 1, 0), 0)), row8(lambda i: (jnp.minimum((i + 1) * nb8, last8), 0)),
                  _full(lng.shape), _full(lnb.shape), _full(wsp.shape), _full(bspb.shape), _full(cw.shape), ANY],
        out_specs=[_tok(tm, inw), _full((1, aw)), _full((1, aw)), _full(wsp.shape),
                   _full((A_GROUPS, CHUNK, gd)), _full(cw.shape)],
        scratch=[pltpu.VMEM((gath.shape[1] * NDEV, dm), BF16), pltpu.SemaphoreType.DMA((NDEV,))])


def _ff_chunks(f, width=1024):
    return [(c0, min(c0 + width, f)) for c0 in range(0, f, width)]


def _ffn_up(x, gain, gath_g, gath_u, name, tm):
    t, dm = x.shape
    f = gath_g.shape[1] * NDEV

    def body(x_ref, g_ref, gg_ref, gu_ref, gate_ref, up_ref, act_ref, wg, wu, sems):
        @pl.when(pl.program_id(0) == 0)
        def _():
            _load_weight(gg_ref, wg, sems)
            _load_weight(gu_ref, wu, sems)
        h, _ = _rms_fwd(x_ref[...], g_ref[...])
        hb = h.astype(BF16)
        for c0, c1 in _ff_chunks(f):
            gate = _dot(hb, wg[c0:c1, :], NT)
            up = _dot(hb, wu[c0:c1, :], NT)
            gate_ref[:, c0:c1] = gate.astype(BF16)
            up_ref[:, c0:c1] = up.astype(BF16)
            act_ref[:, c0:c1] = (gate * _sigmoid(gate) * up).astype(BF16)

    o = jax.ShapeDtypeStruct((t, f), BF16)
    return dict(
        body=body, grid=(t // tm,), name=name, args=[x, gain, gath_g, gath_u], out_shape=[o, o, o],
        in_specs=[_tok(tm, dm), _full((1, dm)), ANY, ANY], out_specs=[_tok(tm, f)] * 3,
        scratch=[pltpu.VMEM((f, dm), BF16), pltpu.VMEM((f, dm), BF16), pltpu.SemaphoreType.DMA((NDEV,))])


def _ffn_down(x, act, gath_d, name, tm):
    t, dm = x.shape
    f = act.shape[1]

    def body(x_ref, a_ref, gd_ref, xo_ref, wd, sems):
        @pl.when(pl.program_id(0) == 0)
        def _():
            _load_weight(gd_ref, wd, sems)
        xo_ref[...] = x_ref[...] + _dot(a_ref[...], wd[...], NN)

    return dict(
        body=body, grid=(t // tm,), name=name, args=[x, act, gath_d], out_shape=[jax.ShapeDtypeStruct((t, dm), F32)],
        in_specs=[_tok(tm, dm), _tok(tm, f), ANY], out_specs=[_tok(tm, dm)],
        scratch=[pltpu.VMEM((f, dm), BF16), pltpu.SemaphoreType.DMA((NDEV,))])


def _ffn_bwd(dxo, x, gate, up, gain, gath_g, gath_u, gath_d, name, tm):
    t, dm = x.shape
    f = gate.shape[1]

    def body(dxo_ref, x_ref, gate_ref, up_ref, g_ref, gg_ref, gu_ref, gd_ref,
             dx_ref, dxb_ref, dg_ref, du_ref, hb_ref, dgain_ref, wg, wu, wd, sems):
        @pl.when(pl.program_id(0) == 0)
        def _():
            _load_weight(gg_ref, wg, sems)
            _load_weight(gu_ref, wu, sems)
            _load_weight(gd_ref, wd, sems)
            dgain_ref[...] = jnp.zeros_like(dgain_ref)
        xv, gain_v, dxo_v = x_ref[...], g_ref[...], dxo_ref[...]
        h, r = _rms_fwd(xv, gain_v)
        hb_ref[...] = h.astype(BF16)
        dxob = dxo_v.astype(BF16)
        dh = jnp.zeros_like(xv)
        for c0, c1 in _ff_chunks(f):
            gate_v = gate_ref[:, c0:c1].astype(F32)
            up_v = up_ref[:, c0:c1].astype(F32)
            s = _sigmoid(gate_v)
            silu = gate_v * s
            dact = _dot(dxob, wd[c0:c1, :], NT)
            dg = (dact * up_v * (s * (1.0 + gate_v * (1.0 - s)))).astype(BF16)
            du = (dact * silu).astype(BF16)
            dg_ref[:, c0:c1] = dg
            du_ref[:, c0:c1] = du
            dh = dh + _dot(dg, wg[c0:c1, :], NN) + _dot(du, wu[c0:c1, :], NN)
        dx, dgain = _rms_bwd(dh, xv, r, gain_v)
        dx = dxo_v + dx
        dx_ref[...] = dx
        dxb_ref[...] = dx.astype(BF16)
        dgain_ref[...] += dgain

    return dict(
        body=body, grid=(t // tm,), name=name, args=[dxo, x, gate, up, gain, gath_g, gath_u, gath_d],
        out_shape=[jax.ShapeDtypeStruct((t, dm), F32), jax.ShapeDtypeStruct((t, dm), BF16),
                   jax.ShapeDtypeStruct((t, f), BF16), jax.ShapeDtypeStruct((t, f), BF16),
                   jax.ShapeDtypeStruct((t, dm), BF16), jax.ShapeDtypeStruct((1, dm), F32)],
        in_specs=[_tok(tm, dm), _tok(tm, dm), _tok(tm, f), _tok(tm, f), _full((1, dm)), ANY, ANY, ANY],
        out_specs=[_tok(tm, dm), _tok(tm, dm), _tok(tm, f), _tok(tm, f), _tok(tm, dm), _full((1, dm))],
        scratch=[pltpu.VMEM((f, dm), BF16), pltpu.VMEM((f, dm), BF16), pltpu.VMEM((f, dm), BF16),
                 pltpu.SemaphoreType.DMA((NDEV,))])


def _t5_buckets(rel):
    nb = N_BUCKETS // 2
    ret = jnp.where(rel > 0, nb, 0)
    n = jnp.abs(rel)
    max_exact = nb // 2
    nf = jnp.maximum(n, 1).astype(jnp.float32)
    large = max_exact + (jnp.log(nf / max_exact) / math.log(MAX_DISTANCE / max_exact)
                         * (nb - max_exact)).astype(jnp.int32)
    large = jnp.minimum(large, nb - 1)
    return ret + jnp.where(n < max_exact, n, large)


def _bucket_table():
    qi = jnp.arange(CHUNK, dtype=jnp.int32)[:, None]
    kj = jnp.arange(3 * CHUNK, dtype=jnp.int32)[None, :]
    rel = kj - CHUNK - qi
    return jnp.where(jnp.abs(rel) <= CHUNK, _t5_buckets(rel), -1)


def _bias_table(rel_bias_t, buckets):
    nh = rel_bias_t.shape[0]

    def body(rb_ref, bk_ref, o_ref):
        bk = bk_ref[...]
        for h in range(nh):
            acc = jnp.where(bk < 0, NEG, 0.0).astype(F32)
            for b in range(N_BUCKETS):
                acc = jnp.where(bk == b, rb_ref[h, b], acc)
            o_ref[h] = acc

    return pl.pallas_call(
        body, out_shape=jax.ShapeDtypeStruct((nh,) + buckets.shape, F32),
        in_specs=[pl.BlockSpec(memory_space=pltpu.SMEM), pl.BlockSpec(memory_space=pltpu.VMEM)],
        out_specs=pl.BlockSpec(memory_space=pltpu.VMEM), name="bias_table")(rel_bias_t, buckets)


def _rel_bias_grad(dbias, buckets):
    nh = dbias.shape[0]

    def body(db_ref, bk_ref, o_ref):
        bk = bk_ref[...]
        lane = lax.broadcasted_iota(jnp.int32, (1, 128), 1)
        for h in range(nh):
            d = db_ref[h]
            row = jnp.zeros((1, 128), F32)
            for b in range(N_BUCKETS):
                s = jnp.sum(jnp.sum(jnp.where(bk == b, d, 0.0), axis=1, keepdims=True), axis=0, keepdims=True)
                row = jnp.where(lane == b, s, row)
            o_ref[h:h + 1, :] = row

    return pl.pallas_call(
        body, out_shape=jax.ShapeDtypeStruct((nh, 128), F32),
        in_specs=[pl.BlockSpec(memory_space=pltpu.VMEM), pl.BlockSpec(memory_space=pltpu.VMEM)],
        out_specs=pl.BlockSpec(memory_space=pltpu.VMEM), compiler_params=_cp(0), name="rel_bias_grad")(dbias, buckets)


def _half_masks():
    lane = lax.broadcasted_iota(jnp.int32, (CHUNK, 128), 1)
    return lane < HEAD_DIM, lane >= HEAD_DIM


def _kv_low(ref, starts, hk, lo):
    kt = (hk // 2) * 128
    out = []
    for jj in range(3):
        blk = ref[pl.ds(starts[jj], CHUNK), kt:kt + 128]
        if hk % 2 == 1:
            blk = pltpu.roll(blk, HEAD_DIM, 1)
        out.append(jnp.where(lo, blk, jnp.zeros_like(blk)))
    return out


def _stack_heads(tile_a, tile_b):
    return jnp.concatenate([tile_a, pltpu.roll(tile_a, HEAD_DIM, 1), tile_b, pltpu.roll(tile_b, HEAD_DIM, 1)], axis=0)


def _unstack_heads(o4):
    return (o4[0:CHUNK] + pltpu.roll(o4[CHUNK:2 * CHUNK], HEAD_DIM, 1),
            o4[2 * CHUNK:3 * CHUNK] + pltpu.roll(o4[3 * CHUNK:], HEAD_DIM, 1))


def _attn_probs(q4, k_lo, bias_ref, sink_ref, hk, n, nblk):
    scale = HEAD_DIM ** -0.5
    s4 = [_dot(q4, k_lo[jj], NT) * scale for jj in range(3)]
    out = []
    for g in range(N_HEADS // N_KV):
        h = (N_HEADS // N_KV) * hk + g
        s = []
        for jj in range(3):
            sj = s4[jj][g * CHUNK:(g + 1) * CHUNK, :] + bias_ref[h, :, jj * CHUNK:(jj + 1) * CHUNK]
            if jj == 0:
                sj = jnp.where(n > 0, sj, NEG)
            if jj == 2:
                sj = jnp.where(n < nblk - 1, sj, NEG)
            s.append(sj)
        sink = sink_ref[h]
        m = jnp.maximum(jnp.max(jnp.maximum(jnp.maximum(s[0], s[1]), s[2]), axis=-1, keepdims=True), sink)
        e = [jnp.exp(sj - m) for sj in s]
        es = jnp.exp(sink - m)
        inv = 1.0 / (jnp.sum(e[0] + e[1] + e[2], axis=-1, keepdims=True) + es)
        out.append(([ej * inv for ej in e], es * inv))
    return out


def _key_block_starts(n, nblk):
    return [pl.multiple_of(jnp.clip(n - 1 + jj, 0, nblk - 1) * CHUNK, CHUNK) for jj in range(3)]


def _attn_fwd(qkv, x2, bias, sink, gath):
    t, dm = x2.shape
    nblk = t // CHUNK
    kvw = N_KV * HEAD_DIM
    kcb, vcb = dm // kvw, dm // kvw + 1

    def body(q_ref, k_ref, v_ref, x2_ref, bias_ref, sink_ref, gath_ref, x3_ref, att_ref, wbuf, sems):
        n = pl.program_id(0)

        @pl.when(n == 0)
        def _():
            _load_weight(gath_ref, wbuf, sems)
        lo, _ = _half_masks()
        starts = _key_block_starts(n, nblk)
        tiles = []
        for hk in range(N_KV):
            c0 = (2 * hk) * 128
            k_lo = _kv_low(k_ref, starts, hk, lo)
            v_lo = _kv_low(v_ref, starts, hk, lo)
            q4 = _stack_heads(q_ref[:, c0:c0 + 128], q_ref[:, c0 + 128:c0 + 256])
            pr = _attn_probs(q4, k_lo, bias_ref, sink_ref, hk, n, nblk)
            o4 = jnp.zeros((4 * CHUNK, 128), F32)
            for jj in range(3):
                p4 = jnp.concatenate([pr[g][0][jj] for g in range(4)], axis=0).astype(BF16)
                o4 = o4 + _dot(p4, v_lo[jj], NN)
            tiles += list(_unstack_heads(o4))
        att = jnp.concatenate(tiles, axis=1).astype(BF16)
        att_ref[...] = att
        x3_ref[...] = x2_ref[...] + _dot(att, wbuf[...], NN)

    blk = pl.BlockSpec((CHUNK, dm), lambda n: (n, 0))
    return dict(
        body=body, grid=(nblk,), name="attn_fwd", args=[qkv, qkv, qkv, x2, bias, sink, gath],
        out_shape=[jax.ShapeDtypeStruct((t, dm), F32), jax.ShapeDtypeStruct((t, dm), BF16)],
        in_specs=[blk, pl.BlockSpec((t, kvw), lambda n: (0, kcb)), pl.BlockSpec((t, kvw), lambda n: (0, vcb)), blk,
                  _full(bias.shape), pl.BlockSpec(memory_space=pltpu.SMEM), ANY],
        out_specs=[blk, blk],
        scratch=[pltpu.VMEM((gath.shape[1] * NDEV, dm), BF16), pltpu.SemaphoreType.DMA((NDEV,))])


def _attn_bwd(qkv, att, dx3, bias, sink, gath):
    t, dm = dx3.shape
    nblk = t // CHUNK
    kvw = N_KV * HEAD_DIM
    kcb, vcb = dm // kvw, dm // kvw + 1
    scale = HEAD_DIM ** -0.5

    def body(q_ref, k_ref, v_ref, att_ref, dx_ref, bias_ref, sink_ref, gath_ref,
             dq_ref, dk_ref, dv_ref, dbias_ref, dsink_ref, wbuf, sems):
        n = pl.program_id(0)

        @pl.when(n == 0)
        def _():
            _load_weight(gath_ref, wbuf, sems)
            dk_ref[...] = jnp.zeros_like(dk_ref)
            dv_ref[...] = jnp.zeros_like(dv_ref)
            dbias_ref[...] = jnp.zeros_like(dbias_ref)
            dsink_ref[...] = jnp.zeros_like(dsink_ref)
        lo, hi = _half_masks()
        lane1 = lax.broadcasted_iota(jnp.int32, (1, 128), 1)
        starts = _key_block_starts(n, nblk)
        dout = _dot(dx_ref[...].astype(BF16), wbuf[...], NT)
        prod = dout * att_ref[...].astype(F32)
        doutb = dout.astype(BF16)
        dq_tiles = []
        dsink_row = jnp.zeros((1, 128), F32)
        for hk in range(N_KV):
            kt = (hk // 2) * 128
            c0 = (2 * hk) * 128
            k_lo = _kv_low(k_ref, starts, hk, lo)
            v_lo = _kv_low(v_ref, starts, hk, lo)
            q4 = _stack_heads(q_ref[:, c0:c0 + 128], q_ref[:, c0 + 128:c0 + 256])
            do4 = _stack_heads(doutb[:, c0:c0 + 128], doutb[:, c0 + 128:c0 + 256])
            pr = _attn_probs(q4, k_lo, bias_ref, sink_ref, hk, n, nblk)
            dsum = []
            for g in range(4):
                pt = prod[:, c0 + (g // 2) * 128:c0 + (g // 2 + 1) * 128]
                dsum.append(jnp.sum(jnp.where(lo if g % 2 == 0 else hi, pt, 0.0), axis=-1, keepdims=True))
                contrib = -jnp.sum(pr[g][1] * dsum[g], axis=0, keepdims=True)
                dsink_row = dsink_row + jnp.where(lane1 == 4 * hk + g, contrib, 0.0)
            dq4 = jnp.zeros((4 * CHUNK, 128), F32)
            for jj in range(3):
                dp4 = _dot(do4, v_lo[jj], NT)
                ds_g = []
                for g in range(4):
                    ds = pr[g][0][jj] * (dp4[g * CHUNK:(g + 1) * CHUNK, :] - dsum[g])
                    dbias_ref[4 * hk + g, :, jj * CHUNK:(jj + 1) * CHUNK] += ds
                    ds_g.append(ds)
                ds4 = jnp.concatenate(ds_g, axis=0).astype(BF16)
                p4 = jnp.concatenate([pr[g][0][jj] for g in range(4)], axis=0).astype(BF16)
                dq4 = dq4 + _dot(ds4, k_lo[jj], NN) * scale
                dkj = _dot(ds4, q4, TN) * scale
                dvj = _dot(p4, do4, TN)
                if hk % 2 == 1:
                    dkj, dvj = pltpu.roll(dkj, HEAD_DIM, 1), pltpu.roll(dvj, HEAD_DIM, 1)
                keep = lo if hk % 2 == 0 else hi
                dk_ref[pl.ds(starts[jj], CHUNK), kt:kt + 128] += jnp.where(keep, dkj, 0.0)
                dv_ref[pl.ds(starts[jj], CHUNK), kt:kt + 128] += jnp.where(keep, dvj, 0.0)
            dq_tiles += list(_unstack_heads(dq4))
        dq_ref[...] = jnp.concatenate(dq_tiles, axis=1).astype(BF16)
        dsink_ref[...] += dsink_row

    blk = pl.BlockSpec((CHUNK, dm), lambda n: (n, 0))
    return dict(
        body=body, grid=(nblk,), name="attn_bwd", args=[qkv, qkv, qkv, att, dx3, bias, sink, gath],
        out_shape=[jax.ShapeDtypeStruct((t, dm), BF16), jax.ShapeDtypeStruct((t, kvw), F32),
                   jax.ShapeDtypeStruct((t, kvw), F32), jax.ShapeDtypeStruct(bias.shape, F32),
                   jax.ShapeDtypeStruct((1, 128), F32)],
        in_specs=[blk, pl.BlockSpec((t, kvw), lambda n: (0, kcb)), pl.BlockSpec((t, kvw), lambda n: (0, vcb)),
                  blk, blk, _full(bias.shape), pl.BlockSpec(memory_space=pltpu.SMEM), ANY],
        out_specs=[blk, _full((t, kvw)), _full((t, kvw)), _full(bias.shape), _full((1, 128))],
        scratch=[pltpu.VMEM((gath.shape[1] * NDEV, dm), BF16), pltpu.SemaphoreType.DMA((NDEV,))])


def _final_loss(x4, target, gain, tm):
    t, dm = x4.shape
    steps = t // tm

    def body(x_ref, t_ref, g_ref, loss_ref, dx_ref, dxb_ref, dgain_ref, acc):
        i = pl.program_id(0)

        @pl.when(i == 0)
        def _():
            acc[...] = jnp.zeros_like(acc)
            dgain_ref[...] = jnp.zeros_like(dgain_ref)
        xv, gain_v = x_ref[...], g_ref[...]
        y, r = _rms_fwd(xv, gain_v)
        e = y - t_ref[...]
        acc[...] += jnp.sum(e * e, axis=0, keepdims=True)
        dx, dgain = _rms_bwd(e * (1.0 / dm), xv, r, gain_v)
        dx_ref[...] = dx
        dxb_ref[...] = dx.astype(BF16)
        dgain_ref[...] += dgain

        @pl.when(i == steps - 1)
        def _():
            loss_ref[...] = jnp.sum(acc[...], axis=-1, keepdims=True) * (0.5 / dm)

    return dict(
        body=body, grid=(steps,), name="final_loss", args=[x4, target, gain],
        out_shape=[jax.ShapeDtypeStruct((1, 1), F32), jax.ShapeDtypeStruct((t, dm), F32),
                   jax.ShapeDtypeStruct((t, dm), BF16), jax.ShapeDtypeStruct((1, dm), F32)],
        in_specs=[_tok(tm, dm), _tok(tm, dm), _full((1, dm))],
        out_specs=[_full((1, 1)), _tok(tm, dm), _tok(tm, dm), _full((1, dm))],
        scratch=[pltpu.VMEM((1, dm), F32)])


def _finish_weight(recvs, w, m, v, name):
    nl, r, dm = w.shape
    assert nl == len(recvs) and recvs[0].shape[1:] == (r, dm)
    td = dm // 2
    wspec = pl.BlockSpec((None, r, td), lambda l, j: (l, 0, j))

    def body(*refs):
        r_refs = refs[:nl]
        w_ref, m_ref, v_ref, g_ref, d_ref, nm_ref, nv_ref = refs[nl:]
        layer = pl.program_id(0)
        for li in range(nl):
            @pl.when(layer == li)
            def _():
                g = r_refs[li][0].astype(F32)
                for d in range(1, NDEV):
                    g = g + r_refs[li][d].astype(F32)
                delta, nm, nv = _adamw_math(w_ref[...], g, m_ref[...], v_ref[...])
                g_ref[...] = g
                d_ref[...] = delta
                nm_ref[...] = nm
                nv_ref[...] = nv

    o = jax.ShapeDtypeStruct(w.shape, F32)
    out = pl.pallas_call(
        body, grid=(nl, 2), out_shape=(o, o, o, o),
        in_specs=[pl.BlockSpec((NDEV, r, td), lambda l, j: (0, 0, j))] * nl + [wspec] * 3,
        out_specs=(wspec,) * 4, compiler_params=_cp(2), name=name)(*recvs, w, m, v)
    return out


def _adamw(w, g, m, v, name):
    r, c = w.shape

    def body(w_ref, g_ref, m_ref, v_ref, d_ref, nm_ref, nv_ref):
        d_ref[...], nm_ref[...], nv_ref[...] = _adamw_math(w_ref[...], g_ref[...], m_ref[...], v_ref[...])

    spec = pl.BlockSpec((r, c), lambda i: (0, 0))
    out = jax.ShapeDtypeStruct((r, c), F32)
    return pl.pallas_call(
        body, grid=(1,), out_shape=(out, out, out), in_specs=[spec] * 4, out_specs=(spec,) * 3,
        compiler_params=_cp(), name=name)(w, g, m, v)


def _pack_small(parts, rows):
    flat = jnp.concatenate([p.reshape(-1) for p in parts])
    return jnp.pad(flat, (0, rows * 128 - flat.shape[0])).reshape(rows, 128)


def _unpack_small(packed, shapes):
    flat = packed.reshape(-1)
    out, o = [], 0
    for s in shapes:
        n = int(np.prod(s))
        out.append(flat[o:o + n].reshape(s))
        o += n
    return out


def kernel(x, norm_mix, norm_ffn, even_w_in, even_v_ln_g, even_v_ln_b, even_w_spatial, even_b_spatial, even_conv_w, even_w_out, attn_w_qkv, attn_sink, rel_bias, attn_w_out, ffn_w_gate, ffn_w_up, ffn_w_down, final_norm, loss_target, m_norm_mix, m_norm_ffn, m_even_w_in, m_even_v_ln_g, m_even_v_ln_b, m_even_w_spatial, m_even_b_spatial, m_even_conv_w, m_even_w_out, m_attn_w_qkv, m_attn_sink, m_rel_bias, m_attn_w_out, m_ffn_w_gate, m_ffn_w_up, m_ffn_w_down, m_final_norm, v_norm_mix, v_norm_ffn, v_even_w_in, v_even_v_ln_g, v_even_v_ln_b, v_even_w_spatial, v_even_b_spatial, v_even_conv_w, v_even_w_out, v_attn_w_qkv, v_attn_sink, v_rel_bias, v_attn_w_out, v_ffn_w_gate, v_ffn_w_up, v_ffn_w_down, v_final_norm):
    t, dm = x.shape[1], x.shape[2]
    aw = even_v_ln_g.shape[1]
    bw = even_conv_w.shape[2] * NDEV
    gd = aw // A_GROUPS
    tm = min(512, t // 2)
    tmf = min(256, t // 2)
    me = _my_index()
    row = lambda a: a.reshape(1, -1)

    colT = lambda w: w.T.astype(BF16)
    sh = dict(winT=colT(even_w_in[0]), wqkvT=colT(attn_w_qkv[0]), wgT0=colT(ffn_w_gate[0]), wuT0=colT(ffn_w_up[0]),
              wgT1=colT(ffn_w_gate[1]), wuT1=colT(ffn_w_up[1]), woe=even_w_out[0].astype(BF16),
              woa=attn_w_out[0].astype(BF16), wd0=ffn_w_down[0].astype(BF16), wd1=ffn_w_down[1].astype(BF16))
    gather = lambda names: _GatherCarry([sh[n] for n in names])

    cw_rows = 3 * bw // 128
    cw_mine = lax.dynamic_update_slice(jnp.zeros((3, bw), F32), even_conv_w[0], (0, me * (bw // NDEV)))
    cw_full = _all_reduce_small(jnp.pad(cw_mine.reshape(cw_rows, 128), ((0, 16 - cw_rows), (0, 0))),
                                "gather_conv_w")[0:cw_rows].reshape(3, bw)

    x0 = x[0]
    wsp_b = even_w_spatial[0].astype(BF16)
    bspb = jnp.broadcast_to(even_b_spatial[0][:, :, None], (A_GROUPS, CHUNK, gd))
    buckets = _bucket_table()
    bias = _bias_table(rel_bias.T, buckets)
    sink = attn_sink[0]

    (g_winT,) = _exchange_only(gather(["winT"]), "ag_w_in")
    (proj, h0b), (g_woe, g_wgT0) = _call(_norm_proj(x0, row(norm_mix[0]), g_winT, F32, "in_proj", tm),
                                         gather(["woe", "wgT0"]))
    (x1, yb), (g_wuT0,) = _call(_even_core_fwd(proj, x0, even_v_ln_g, even_v_ln_b, wsp_b, bspb, cw_full, g_woe, tm),
                                gather(["wuT0"]))
    (gate0, up0, act0), (g_wd0,) = _call(_ffn_up(x1, row(norm_ffn[0]), g_wgT0, g_wuT0, "ffn_up0", tmf), gather(["wd0"]))
    (x2,), (g_wqkvT,) = _call(_ffn_down(x1, act0, g_wd0, "ffn_down0", tm), gather(["wqkvT"]))
    (qkv, h2b), (g_woa,) = _call(_norm_proj(x2, row(norm_mix[1]), g_wqkvT, BF16, "qkv_proj", tm), gather(["woa"]))
    (x3, attb), (g_wgT1, g_wuT1) = _call(_attn_fwd(qkv, x2, bias, sink, g_woa), gather(["wgT1", "wuT1"]))
    (gate1, up1, act1), (g_wd1,) = _call(_ffn_up(x3, row(norm_ffn[1]), g_wgT1, g_wuT1, "ffn_up1", tmf), gather(["wd1"]))
    (x4,), _ = _call(_ffn_down(x3, act1, g_wd1, "ffn_down1", tm))
    (loss_part, dx4, dx4b, d_final), _ = _call(_final_loss(x4, loss_target[0], row(final_norm), tm))

    (dx3, dx3b, dg1, du1, h3b, d_nffn1), _ = _call(
        _ffn_bwd(dx4, x3, gate1, up1, row(norm_ffn[1]), g_wgT1, g_wuT1, g_wd1, "ffn_bwd1", tmf))
    (p_wgT1,), _ = _call(_wgrad(dg1, h3b, "wgrad_gate1"))
    (p_wuT1,), _ = _call(_wgrad(du1, h3b, "wgrad_up1"))
    (p_wd1,), _ = _call(_wgrad(act1, dx4b, "wgrad_down1"))
    (dq, dk, dv, dbias, dsink), (r_wgT1, r_wuT1, r_wd1) = _call(
        _attn_bwd(qkv, attb, dx3, bias, sink, g_woa), _GradCarry([p_wgT1, p_wuT1, p_wd1]))
    (p_woa,), _ = _call(_wgrad(attb, dx3b, "wgrad_attn_out"))
    d_relb = _rel_bias_grad(dbias, buckets)[:, 0:N_BUCKETS].T
    dqkv = jnp.concatenate([dq, dk.astype(BF16), dv.astype(BF16)], axis=1)
    (dx2, dx2b, d_nmix1), _ = _call(_proj_bwd_norm(dqkv, x2, row(norm_mix[1]), dx3, g_wqkvT, "qkv_bwd", tm))
    (p_wqkvT,), _ = _call(_wgrad(dqkv, h2b, "wgrad_qkv"))
    (dx1, dx1b, dg0, du0, h1b, d_nffn0), (r_woa, r_wqkvT) = _call(
        _ffn_bwd(dx2, x1, gate0, up0, row(norm_ffn[0]), g_wgT0, g_wuT0, g_wd0, "ffn_bwd0", tmf),
        _GradCarry([p_woa, p_wqkvT]))
    (p_woe,), _ = _call(_wgrad(yb, dx1b, "wgrad_even_out"))
    (p_wgT0,), (r_woe,) = _call(_wgrad(dg0, h1b, "wgrad_gate0"), _GradCarry([p_woe]))
    (p_wuT0,), (r_wgT0,) = _call(_wgrad(du0, h1b, "wgrad_up0"), _GradCarry([p_wgT0]))
    (p_wd0,), (r_wuT0,) = _call(_wgrad(act0, dx2b, "wgrad_down0"), _GradCarry([p_wuT0]))
    (dproj, d_lng, d_lnb, d_wsp, d_bsp3, d_cw), (r_wd0,) = _call(
        _even_core_bwd(proj, dx1, even_v_ln_g, even_v_ln_b, wsp_b, bspb, cw_full, g_woe, tm), _GradCarry([p_wd0]))
    (p_winT,), _ = _call(_wgrad(dproj, h0b, "wgrad_in"))
    (dx0, _, d_nmix0), (r_winT,) = _call(
        _proj_bwd_norm(dproj, x0, row(norm_mix[0]), dx1, g_winT, "in_proj_bwd", tm), _GradCarry([p_winT]))

    small_shapes = [(2, dm), (2, dm), (1, aw), (1, aw), (1, A_GROUPS, CHUNK, CHUNK), (1, A_GROUPS, CHUNK), (3, bw),
                    (1, N_HEADS), (N_BUCKETS, N_HEADS), (dm,)]
    n_small = sum(int(np.prod(s)) for s in small_shapes)
    small_rows = 8 * ((n_small + 1023) // 1024)
    small_part = _pack_small(
        [jnp.concatenate([d_nmix0, d_nmix1]), jnp.concatenate([d_nffn0, d_nffn1]), d_lng, d_lnb, d_wsp,
         jnp.sum(d_bsp3, axis=-1), d_cw, dsink[:, 0:N_HEADS], d_relb, d_final], small_rows)
    small_sum = _unpack_small(_all_reduce_small(small_part, "all_reduce_small_grads"), small_shapes)
    grads = {}
    (grads["norm_mix"], grads["norm_ffn"], grads["even_v_ln_g"], grads["even_v_ln_b"], grads["even_w_spatial"],
     grads["even_b_spatial"], g_cw_full, grads["attn_sink"], grads["rel_bias"], grads["final_norm"]) = small_sum
    grads["even_conv_w"] = lax.dynamic_slice(g_cw_full, (0, me * (bw // NDEV)), (3, bw // NDEV))[None]

    order = ["norm_mix", "norm_ffn", "even_w_in", "even_v_ln_g", "even_v_ln_b", "even_w_spatial", "even_b_spatial",
             "even_conv_w", "even_w_out", "attn_w_qkv", "attn_sink", "rel_bias", "attn_w_out", "ffn_w_gate",
             "ffn_w_up", "ffn_w_down", "final_norm"]
    ws = dict(norm_mix=norm_mix, norm_ffn=norm_ffn, even_w_in=even_w_in, even_v_ln_g=even_v_ln_g,
              even_v_ln_b=even_v_ln_b, even_w_spatial=even_w_spatial, even_b_spatial=even_b_spatial,
              even_conv_w=even_conv_w, even_w_out=even_w_out, attn_w_qkv=attn_w_qkv, attn_sink=attn_sink,
              rel_bias=rel_bias, attn_w_out=attn_w_out, ffn_w_gate=ffn_w_gate, ffn_w_up=ffn_w_up,
              ffn_w_down=ffn_w_down, final_norm=final_norm)
    ms = dict(norm_mix=m_norm_mix, norm_ffn=m_norm_ffn, even_w_in=m_even_w_in, even_v_ln_g=m_even_v_ln_g,
              even_v_ln_b=m_even_v_ln_b, even_w_spatial=m_even_w_spatial, even_b_spatial=m_even_b_spatial,
              even_conv_w=m_even_conv_w, even_w_out=m_even_w_out, attn_w_qkv=m_attn_w_qkv, attn_sink=m_attn_sink,
              rel_bias=m_rel_bias, attn_w_out=m_attn_w_out, ffn_w_gate=m_ffn_w_gate, ffn_w_up=m_ffn_w_up,
              ffn_w_down=m_ffn_w_down, final_norm=m_final_norm)
    vs = dict(norm_mix=v_norm_mix, norm_ffn=v_norm_ffn, even_w_in=v_even_w_in, even_v_ln_g=v_even_v_ln_g,
              even_v_ln_b=v_even_v_ln_b, even_w_spatial=v_even_w_spatial, even_b_spatial=v_even_b_spatial,
              even_conv_w=v_even_conv_w, even_w_out=v_even_w_out, attn_w_qkv=v_attn_w_qkv, attn_sink=v_attn_sink,
              rel_bias=v_rel_bias, attn_w_out=v_attn_w_out, ffn_w_gate=v_ffn_w_gate, ffn_w_up=v_ffn_w_up,
              ffn_w_down=v_ffn_w_down, final_norm=v_final_norm)
    big = dict(even_w_in=([r_winT], True), even_w_out=([r_woe], False), attn_w_qkv=([r_wqkvT], True),
               attn_w_out=([r_woa], False), ffn_w_gate=([r_wgT0, r_wgT1], True), ffn_w_up=([r_wuT0, r_wuT1], True),
               ffn_w_down=([r_wd0, r_wd1], False))
    delta, new_m, new_v = {}, {}, {}
    for n, (recvs, transposed) in big.items():
        lay = (lambda a: jnp.swapaxes(a, 1, 2)) if transposed else (lambda a: a)
        outs = _finish_weight(recvs, lay(ws[n]), lay(ms[n]), lay(vs[n]), "finish_" + n)
        grads[n], delta[n], new_m[n], new_v[n] = [lay(o) for o in outs]
    small = [n for n in order if n not in big]
    sshapes = [ws[n].shape for n in small]
    ns = sum(int(np.prod(s)) for s in sshapes)
    srows = 8 * ((ns + 1023) // 1024)
    pk = lambda dct: _pack_small([dct[n] for n in small], srows)
    d, nm, nv = _adamw(pk(ws), pk(grads), pk(ms), pk(vs), "adamw_small")
    for n, a, b, c2 in zip(small, _unpack_small(d, sshapes), _unpack_small(nm, sshapes), _unpack_small(nv, sshapes)):
        delta[n], new_m[n], new_v[n] = a, b, c2

    loss = lax.psum(loss_part[0, 0], ("x", "y", "c"))
    return (loss, dx0[None], *[grads[n] for n in order], *[delta[n] for n in order],
            *[new_m[n] for n in order], *[new_v[n] for n in order])
```

```python
import math

import jax
import jax.numpy as jnp
import numpy as np
from jax import lax
from jax.experimental import pallas as pl
from jax.experimental.pallas import tpu as pltpu

F32, BF16 = jnp.float32, jnp.bfloat16
NDEV = 8
EPS = 1e-6
CHUNK = 128
A_GROUPS = 4
N_HEADS, N_KV, HEAD_DIM = 16, 4, 64
N_BUCKETS, MAX_DISTANCE = 32, 128
NEG = -1e30
LOG2E = 1.4426950408889634
ADAM_LR, ADAM_B1, ADAM_B2, ADAM_EPS, ADAM_WD, ADAM_STEP = 0.001, 0.9, 0.999, 1e-08, 0.01, 10
VMEM_LIMIT = 56 * 1024 * 1024
MESH = pl.DeviceIdType.MESH
NT = (((1,), (1,)), ((), ()))
NN = (((1,), (0,)), ((), ()))
TN = (((0,), (0,)), ((), ()))
ANY = pl.BlockSpec(memory_space=pl.ANY)


def _cp(n_grid=1):
    return pltpu.CompilerParams(dimension_semantics=("arbitrary",) * n_grid, vmem_limit_bytes=VMEM_LIMIT)


def _dot(a, b, dims):
    return lax.dot_general(a, b, dims, preferred_element_type=F32)


def _my_index():
    return 4 * lax.axis_index("x") + 2 * lax.axis_index("y") + lax.axis_index("c")


def _peer(k):
    x, y, c = lax.axis_index("x"), lax.axis_index("y"), lax.axis_index("c")
    px = 1 - x if k & 4 else x
    py = 1 - y if k & 2 else y
    pc = 1 - c if k & 1 else c
    return (px, py, pc)


def _load_weight(gath_ref, wbuf, sems):
    rows = gath_ref.shape[1]
    cps = [pltpu.make_async_copy(gath_ref.at[d], wbuf.at[pl.ds(d * rows, rows), :], sems.at[d]) for d in range(NDEV)]
    for c in cps:
        c.start()
    for c in cps:
        c.wait()


class _GatherCarry:
    def __init__(self, pieces):
        self.inputs = list(pieces)
        self.n = len(pieces)
        self.out_shape = [jax.ShapeDtypeStruct((NDEV,) + p.shape, p.dtype) for p in pieces]
        self.scratch = [pltpu.SemaphoreType.DMA((7 * self.n,)), pltpu.SemaphoreType.DMA((7 * self.n,)),
                        pltpu.SemaphoreType.DMA((self.n,))]

    def _ctx(self):
        x, y, c = lax.axis_index("x"), lax.axis_index("y"), lax.axis_index("c")
        chips = [(1 - x, y), (x, 1 - y), (1 - x, 1 - y)]
        return (x, y, c), (x, y, 1 - c), chips, c

    def _copy(self, k, j, block, to, ins, outs, sems, src=None):
        send_sems, recv_sems, _ = sems
        slot = outs[j].at[4 * block[0] + 2 * block[1] + block[2]]
        return pltpu.make_async_remote_copy(
            src_ref=slot if src is None else src, dst_ref=slot, send_sem=send_sems.at[k * self.n + j],
            recv_sem=recv_sems.at[k * self.n + j], device_id=to, device_id_type=MESH)

    def start(self, ins, outs, sems):
        me, sibling, chips, c = self._ctx()
        for j in range(self.n):
            pltpu.make_async_copy(ins[j], outs[j].at[4 * me[0] + 2 * me[1] + me[2]], sems[2].at[j]).start()
            self._copy(0, j, me, sibling, ins, outs, sems, src=ins[j]).start()
            for q, chip in enumerate(chips):
                self._copy(1 + q, j, me, (*chip, c), ins, outs, sems, src=ins[j]).start()

    def mid(self, ins, outs, sems):
        me, sibling, chips, c = self._ctx()
        for q, chip in enumerate(chips):
            for j in range(self.n):
                self._copy(1 + q, j, (*chip, c), me, ins, outs, sems).wait_recv()
                self._copy(4 + q, j, (*chip, c), sibling, ins, outs, sems).start()

    def finish(self, ins, outs, sems):
        me, sibling, chips, c = self._ctx()
        for j in range(self.n):
            self._copy(0, j, sibling, me, ins, outs, sems).wait_recv()
            for q, chip in enumerate(chips):
                self._copy(4 + q, j, (*chip, 1 - c), me, ins, outs, sems).wait_recv()
        for j in range(self.n):
            self._copy(0, j, me, sibling, ins, outs, sems, src=ins[j]).wait_send()
            for q, chip in enumerate(chips):
                self._copy(1 + q, j, me, (*chip, c), ins, outs, sems, src=ins[j]).wait_send()
                self._copy(4 + q, j, (*chip, c), sibling, ins, outs, sems).wait_send()
            pltpu.make_async_copy(ins[j], outs[j].at[0], sems[2].at[j]).wait()


class _GradCarry:
    def __init__(self, pieces):
        self.inputs = list(pieces)
        self.n = len(pieces)
        self.rows = [p.shape[0] // NDEV for p in pieces]
        self.out_shape = [jax.ShapeDtypeStruct((NDEV, r, p.shape[1]), p.dtype) for p, r in zip(pieces, self.rows)]
        self.scratch = [pltpu.SemaphoreType.DMA((7 * self.n,)), pltpu.SemaphoreType.DMA((7 * self.n,)),
                        pltpu.SemaphoreType.DMA((self.n,))]

    def _copies(self, ins, outs, sems):
        me = _my_index()
        local, remote = [], []
        for j in range(self.n):
            r = self.rows[j]
            local.append(pltpu.make_async_copy(ins[j].at[pl.ds(pl.multiple_of(me * r, 16), r), :], outs[j].at[me],
                                               sems[2].at[j]))
            for k in range(1, NDEV):
                peer = _peer(k)
                pidx = 4 * peer[0] + 2 * peer[1] + peer[2]
                remote.append(pltpu.make_async_remote_copy(
                    src_ref=ins[j].at[pl.ds(pl.multiple_of(pidx * r, 16), r), :], dst_ref=outs[j].at[me],
                    send_sem=sems[0].at[(k - 1) * self.n + j], recv_sem=sems[1].at[(k - 1) * self.n + j],
                    device_id=peer, device_id_type=MESH))
        return local, remote

    def start(self, ins, outs, sems):
        local, remote = self._copies(ins, outs, sems)
        for cp in local + remote:
            cp.start()

    def mid(self, ins, outs, sems):
        pass

    def finish(self, ins, outs, sems):
        local, remote = self._copies(ins, outs, sems)
        for cp in remote + local:
            cp.wait()


class _BroadcastCarry:
    def __init__(self, part):
        self.inputs = [part]
        self.out_shape = [jax.ShapeDtypeStruct((NDEV,) + part.shape, part.dtype)]
        self.scratch = [pltpu.SemaphoreType.DMA((7,)), pltpu.SemaphoreType.DMA((7,)), pltpu.SemaphoreType.DMA(())]

    def _copies(self, ins, outs, sems):
        me = _my_index()
        local = pltpu.make_async_copy(ins[0], outs[0].at[me], sems[2])
        remote = [pltpu.make_async_remote_copy(
            src_ref=ins[0], dst_ref=outs[0].at[me], send_sem=sems[0].at[k - 1], recv_sem=sems[1].at[k - 1],
            device_id=_peer(k), device_id_type=MESH) for k in range(1, NDEV)]
        return [local] + remote

    def start(self, ins, outs, sems):
        for cp in self._copies(ins, outs, sems):
            cp.start()

    def mid(self, ins, outs, sems):
        pass

    def finish(self, ins, outs, sems):
        for cp in self._copies(ins, outs, sems):
            cp.wait()


class _PairCarry:
    def __init__(self, piece):
        self.inputs = [piece]
        self.r = piece.shape[0] // NDEV
        self.out_shape = [jax.ShapeDtypeStruct((4, self.r, piece.shape[1]), piece.dtype)]
        self.scratch = [pltpu.SemaphoreType.DMA((4,)), pltpu.SemaphoreType.DMA((4,))]

    def _copies(self, ins, outs, sems):
        x, y, c = lax.axis_index("x"), lax.axis_index("y"), lax.axis_index("c")
        return [pltpu.make_async_remote_copy(
            src_ref=ins[0].at[pl.ds(pl.multiple_of((2 * q + 1 - c) * self.r, 16), self.r), :], dst_ref=outs[0].at[q],
            send_sem=sems[0].at[q], recv_sem=sems[1].at[q], device_id=(x, y, 1 - c), device_id_type=MESH)
            for q in range(4)]

    def start(self, ins, outs, sems):
        for cp in self._copies(ins, outs, sems):
            cp.start()

    def mid(self, ins, outs, sems):
        pass

    def finish(self, ins, outs, sems):
        for cp in self._copies(ins, outs, sems):
            cp.wait()


class _ChipSumCarry:
    def __init__(self, piece, landed):
        self.inputs = [piece, landed]
        self.r, dm = piece.shape[0] // NDEV, piece.shape[1]
        self.out_shape = [jax.ShapeDtypeStruct((4, self.r, dm), piece.dtype)]
        self.scratch = [pltpu.VMEM((4, self.r, dm), piece.dtype), pltpu.VMEM((2, self.r, dm), piece.dtype),
                        pltpu.SemaphoreType.DMA((2,)), pltpu.SemaphoreType.DMA((3,)), pltpu.SemaphoreType.DMA((3,)),
                        pltpu.SemaphoreType.DMA(())]

    def _copies(self, outs, scr):
        sums, _, _, send_sems, recv_sems, local_sem = scr
        x, y, c = lax.axis_index("x"), lax.axis_index("y"), lax.axis_index("c")
        mine = 2 * x + y
        local = pltpu.make_async_copy(sums.at[mine], outs[0].at[mine], local_sem)
        remote = []
        for k in range(1, 4):
            px = 1 - x if k & 2 else x
            py = 1 - y if k & 1 else y
            remote.append(pltpu.make_async_remote_copy(
                src_ref=sums.at[2 * px + py], dst_ref=outs[0].at[mine], send_sem=send_sems.at[k - 1],
                recv_sem=recv_sems.at[k - 1], device_id=(px, py, c), device_id_type=MESH))
        return local, remote

    def start(self, ins, outs, scr):
        sums, stage, stage_sems = scr[0], scr[1], scr[2]
        c = lax.axis_index("c")
        for q in range(4):
            a = pltpu.make_async_copy(ins[0].at[pl.ds(pl.multiple_of((2 * q + c) * self.r, 16), self.r), :],
                                      stage.at[0], stage_sems.at[0])
            b = pltpu.make_async_copy(ins[1].at[q], stage.at[1], stage_sems.at[1])
            a.start()
            b.start()
            a.wait()
            b.wait()
            sums[q] = (stage[0].astype(F32) + stage[1].astype(F32)).astype(sums.dtype)
        local, remote = self._copies(outs, scr)
        for cp in [local] + remote:
            cp.start()

    def mid(self, ins, outs, scr):
        pass

    def finish(self, ins, outs, scr):
        local, remote = self._copies(outs, scr)
        for cp in remote + [local]:
            cp.wait()


def _call(spec, carry=None):
    body, grid = spec["body"], spec["grid"]
    in_specs, out_specs, out_shape = list(spec["in_specs"]), list(spec["out_specs"]), list(spec["out_shape"])
    scratch, args = list(spec.get("scratch", [])), list(spec["args"])
    if carry is None:
        out = pl.pallas_call(body, grid=grid, in_specs=in_specs, out_specs=tuple(out_specs),
                             out_shape=tuple(out_shape), scratch_shapes=scratch, compiler_params=_cp(len(grid)),
                             name=spec["name"])(*args)
        return tuple(out), ()
    carries = list(carry) if isinstance(carry, (list, tuple)) else [carry]
    n_in, n_out, n_s = len(in_specs), len(out_specs), len(scratch)
    steps = int(np.prod(grid))

    def split(refs, counts):
        parts, o = [], 0
        for cnt in counts:
            parts.append(refs[o:o + cnt])
            o += cnt
        return parts

    c_in = [len(cr.inputs) for cr in carries]
    c_out = [len(cr.out_shape) for cr in carries]
    c_scr = [len(cr.scratch) for cr in carries]

    def wrapped(*refs):
        ins, cins, outs, couts, scr, cscr = split(refs, [n_in, sum(c_in), n_out, sum(c_out), n_s, sum(c_scr)])
        per = list(zip(carries, split(cins, c_in), split(couts, c_out), split(cscr, c_scr)))
        step = pl.program_id(0)
        for ax in range(1, len(grid)):
            step = step * grid[ax] + pl.program_id(ax)

        @pl.when(step == 0)
        def _():
            for cr, ci, co, cs in per:
                cr.start(ci, co, cs)
        if steps >= 3:
            @pl.when(step == steps - 2)
            def _():
                for cr, ci, co, cs in per:
                    cr.mid(ci, co, cs)
        body(*ins, *outs, *scr)

        @pl.when(step == steps - 1)
        def _():
            for cr, ci, co, cs in per:
                if steps < 3:
                    cr.mid(ci, co, cs)
                cr.finish(ci, co, cs)

    out = pl.pallas_call(
        wrapped, grid=grid, in_specs=in_specs + [ANY] * sum(c_in), out_specs=tuple(out_specs + [ANY] * sum(c_out)),
        out_shape=tuple(out_shape + [s for cr in carries for s in cr.out_shape]),
        scratch_shapes=scratch + [s for cr in carries for s in cr.scratch],
        compiler_params=_cp(len(grid)), name=spec["name"])(*args, *[a for cr in carries for a in cr.inputs])
    c_res = [tuple(p) for p in split(out[n_out:], c_out)]
    return tuple(out[:n_out]), (c_res if isinstance(carry, (list, tuple)) else c_res[0])


def _exchange_only(carry, name):
    spec = dict(body=lambda: None, grid=(1,), in_specs=[], out_specs=[], out_shape=[], args=[], name=name)
    return _call(spec, carry)[1]


def _rms_fwd(x, gain):
    r = lax.rsqrt(jnp.mean(x * x, axis=-1, keepdims=True) + EPS)
    return x * r * gain, r


def _rms_bwd(dh, x, r, gain):
    a = dh * gain
    dx = r * a - x * (r * r * r) * jnp.mean(a * x, axis=-1, keepdims=True)
    dgain = jnp.sum(dh * (x * r), axis=0, keepdims=True)
    return dx, dgain


def _gelu(x):
    return 0.5 * x * (1.0 + lax.erf(x * 0.7071067811865476))


def _gelu_grad(x):
    return 0.5 * (1.0 + lax.erf(x * 0.7071067811865476)) + x * jnp.exp(-0.5 * x * x) * 0.3989422804014327


def _sigmoid(x):
    return 1.0 / (1.0 + jnp.exp(-x))


def _adamw_math(w, g, m, v):
    nm = ADAM_B1 * m + (1.0 - ADAM_B1) * g
    nv = ADAM_B2 * v + (1.0 - ADAM_B2) * (g * g)
    m_hat = nm / (1.0 - ADAM_B1 ** ADAM_STEP)
    v_hat = nv / (1.0 - ADAM_B2 ** ADAM_STEP)
    return -ADAM_LR * (m_hat / (jnp.sqrt(v_hat) + ADAM_EPS) + ADAM_WD * w), nm, nv


def _tok(tm, w):
    return pl.BlockSpec((tm, w), lambda i: (i, 0))


def _full(shape):
    return pl.BlockSpec(shape, lambda *i: (0,) * len(shape))


def _norm_proj(x, gain, gath, out_dtype, name, tm):
    t, dm = x.shape
    n = gath.shape[1] * NDEV

    def body(x_ref, g_ref, gath_ref, proj_ref, hb_ref, wbuf, sems):
        @pl.when(pl.program_id(0) == 0)
        def _():
            _load_weight(gath_ref, wbuf, sems)
        h, _ = _rms_fwd(x_ref[...], g_ref[...])
        hb = h.astype(BF16)
        hb_ref[...] = hb
        proj_ref[...] = _dot(hb, wbuf[...], NT).astype(out_dtype)

    return dict(
        body=body, grid=(t // tm,), name=name, args=[x, gain, gath],
        out_shape=[jax.ShapeDtypeStruct((t, n), out_dtype), jax.ShapeDtypeStruct((t, dm), BF16)],
        in_specs=[_tok(tm, dm), _full((1, dm)), ANY], out_specs=[_tok(tm, n), _tok(tm, dm)],
        scratch=[pltpu.VMEM((n, dm), BF16), pltpu.SemaphoreType.DMA((NDEV,))])


def _proj_bwd_norm(dy, x, gain, dres, gath, name, tm):
    t, dm = x.shape
    n = gath.shape[1] * NDEV

    def body(dy_ref, x_ref, g_ref, dres_ref, gath_ref, dx_ref, dxb_ref, dgain_ref, wbuf, sems):
        @pl.when(pl.program_id(0) == 0)
        def _():
            _load_weight(gath_ref, wbuf, sems)
            dgain_ref[...] = jnp.zeros_like(dgain_ref)
        xv, gain_v = x_ref[...], g_ref[...]
        _, r = _rms_fwd(xv, gain_v)
        dh = _dot(dy_ref[...], wbuf[...], NN)
        dx, dgain = _rms_bwd(dh, xv, r, gain_v)
        dx = dres_ref[...] + dx
        dx_ref[...] = dx
        dxb_ref[...] = dx.astype(BF16)
        dgain_ref[...] += dgain

    return dict(
        body=body, grid=(t // tm,), name=name, args=[dy, x, gain, dres, gath],
        out_shape=[jax.ShapeDtypeStruct((t, dm), F32), jax.ShapeDtypeStruct((t, dm), BF16),
                   jax.ShapeDtypeStruct((1, dm), F32)],
        in_specs=[_tok(tm, n), _tok(tm, dm), _full((1, dm)), _tok(tm, dm), ANY],
        out_specs=[_tok(tm, dm), _tok(tm, dm), _full((1, dm))],
        scratch=[pltpu.VMEM((n, dm), BF16), pltpu.SemaphoreType.DMA((NDEV,))])


def _wgrad(a, b, name, tmm=256):
    t, m = a.shape
    n = b.shape[1]

    def body(a_ref, b_ref, o_ref):
        o_ref[...] = _dot(a_ref[...], b_ref[...], TN).astype(BF16)

    return dict(
        body=body, grid=(m // tmm,), name=name, args=[a, b], out_shape=[jax.ShapeDtypeStruct((m, n), BF16)],
        in_specs=[pl.BlockSpec((t, tmm), lambda j: (0, j)), pl.BlockSpec((t, n), lambda j: (0, 0))],
        out_specs=[pl.BlockSpec((tmm, n), lambda j: (j, 0))])


def _halo_specs(tm, t, width, col_blocks):
    nb8 = tm // 8
    last = t // 8 - 1
    prev = [pl.BlockSpec((8, width), lambda i, cb=cb: (jnp.maximum(i * nb8 - 1, 0), cb)) for cb in col_blocks]
    nxt = [pl.BlockSpec((8, width), lambda i, cb=cb: (jnp.minimum((i + 1) * nb8, last), cb)) for cb in col_blocks]
    return prev, nxt


def _shift_rows(z, prev_row, next_row):
    tm = z.shape[0]
    row = lax.broadcasted_iota(jnp.int32, z.shape, 0)
    zm1 = jnp.where(row == 0, prev_row, pltpu.roll(z, 1, 0))
    zp1 = jnp.where(row == tm - 1, next_row, pltpu.roll(z, tm - 1, 0))
    return zm1, zp1


def _gating_fwd(proj, lng, lnb, wsp_ref, bsp_ref, aw):
    tm = proj.shape[0]
    a_u = _gelu(proj[:, 0:aw])
    gv = _gelu(proj[:, aw:2 * aw])
    mu = jnp.mean(gv, axis=-1, keepdims=True)
    xc = gv - mu
    rstd = lax.rsqrt(jnp.mean(xc * xc, axis=-1, keepdims=True) + EPS)
    vn = xc * rstd
    a_v = (vn * lng + lnb).astype(BF16)
    gd = aw // A_GROUPS
    rows = []
    for c in range(tm // CHUNK):
        cols = []
        for g in range(A_GROUPS):
            blk = a_v[c * CHUNK:(c + 1) * CHUNK, g * gd:(g + 1) * gd]
            cols.append(_dot(wsp_ref[g], blk, NN) + bsp_ref[g])
        rows.append(jnp.concatenate(cols, axis=1))
    mixed = jnp.concatenate(rows, axis=0)
    return a_u, vn, rstd, a_v, mixed


def _even_core_fwd(proj, x0, lng, lnb, wsp, bspb, cw, gath, tm):
    t, dm = x0.shape
    aw = lng.shape[1]
    bw = cw.shape[1]
    assert aw == bw and 2 * aw + 3 * bw == proj.shape[1]
    nt = t // tm
    prev, nxt = _halo_specs(tm, t, bw, [3, 4])

    def body(proj_ref, cp_ref, hp_ref, cn_ref, hn_ref, x0_ref, lng_ref, lnb_ref, wsp_ref, bsp_ref, cw_ref, gath_ref,
             x1_ref, y_ref, wbuf, sems):
        i = pl.program_id(0)

        @pl.when(i == 0)
        def _():
            _load_weight(gath_ref, wbuf, sems)
        proj_v = proj_ref[...]
        a_u, _, _, _, mixed = _gating_fwd(proj_v, lng_ref[...], lnb_ref[...], wsp_ref, bsp_ref, aw)
        a_out = a_u * mixed
        bb = proj_v[:, 2 * aw:2 * aw + bw]
        z = proj_v[:, 2 * aw + bw:2 * aw + 2 * bw] * proj_v[:, 2 * aw + 2 * bw:]
        zprev = jnp.where(i > 0, cp_ref[7:8, :] * hp_ref[7:8, :], 0.0)
        znext = jnp.where(i < nt - 1, cn_ref[0:1, :] * hn_ref[0:1, :], 0.0)
        zm1, zp1 = _shift_rows(z, zprev, znext)
        cwv = cw_ref[...]
        conv = zm1 * cwv[0:1, :] + z * cwv[1:2, :] + zp1 * cwv[2:3, :]
        y = jnp.concatenate([a_out, bb * conv], axis=1).astype(BF16)
        y_ref[...] = y
        x1_ref[...] = x0_ref[...] + _dot(y, wbuf[...], NN)

    return dict(
        body=body, grid=(nt,), name="even_core_fwd",
        args=[proj, proj, proj, proj, proj, x0, lng, lnb, wsp, bspb, cw, gath],
        out_shape=[jax.ShapeDtypeStruct((t, dm), F32), jax.ShapeDtypeStruct((t, aw + bw), BF16)],
        in_specs=[_tok(tm, proj.shape[1]), prev[0], prev[1], nxt[0], nxt[1], _tok(tm, dm), _full(lng.shape),
                  _full(lnb.shape), _full(wsp.shape), _full(bspb.shape), _full(cw.shape), ANY],
        out_specs=[_tok(tm, dm), _tok(tm, aw + bw)],
        scratch=[pltpu.VMEM((gath.shape[1] * NDEV, dm), BF16), pltpu.SemaphoreType.DMA((NDEV,))])


def _even_core_bwd(proj, dx1, lng, lnb, wsp, bspb, cw, gath, tm):
    t, dm = dx1.shape
    aw, bw = lng.shape[1], cw.shape[1]
    gd = aw // A_GROUPS
    nt = t // tm
    inw = proj.shape[1]
    prev, nxt = _halo_specs(tm, t, bw, [2, 3, 4])
    nb8 = tm // 8
    last8 = t // 8 - 1

    def body(proj_ref, bp_ref, cp_ref, hp_ref, bn_ref, cn_ref, hn_ref, dx_ref, dxp_ref, dxn_ref,
             lng_ref, lnb_ref, wsp_ref, bsp_ref, cw_ref, gath_ref,
             dproj_ref, dlng_ref, dlnb_ref, dwsp_ref, dbsp_ref, dcw_ref, wbuf, sems):
        i = pl.program_id(0)

        @pl.when(i == 0)
        def _():
            _load_weight(gath_ref, wbuf, sems)
            dlng_ref[...] = jnp.zeros_like(dlng_ref)
            dlnb_ref[...] = jnp.zeros_like(dlnb_ref)
            dwsp_ref[...] = jnp.zeros_like(dwsp_ref)
            dbsp_ref[...] = jnp.zeros_like(dbsp_ref)
            dcw_ref[...] = jnp.zeros_like(dcw_ref)
        proj_v = proj_ref[...]
        lng_v = lng_ref[...]
        a_u, vn, rstd, a_v, mixed = _gating_fwd(proj_v, lng_v, lnb_ref[...], wsp_ref, bsp_ref, aw)
        w = wbuf[...]
        dy = _dot(dx_ref[...].astype(BF16), w, NT)
        da_out, db_out = dy[:, 0:aw], dy[:, aw:]
        da_u = da_out * mixed
        dmixed = da_out * a_u
        dmb = dmixed.astype(BF16)
        rows = []
        for c in range(tm // CHUNK):
            cols = []
            for g in range(A_GROUPS):
                r0, c0 = c * CHUNK, g * gd
                dm_cg = dmb[r0:r0 + CHUNK, c0:c0 + gd]
                cols.append(_dot(wsp_ref[g], dm_cg, TN))
                dwsp_ref[g] += _dot(dm_cg, a_v[r0:r0 + CHUNK, c0:c0 + gd], NT)
                dbsp_ref[g] += dmixed[r0:r0 + CHUNK, c0:c0 + gd]
            rows.append(jnp.concatenate(cols, axis=1))
        dav = jnp.concatenate(rows, axis=0)
        dlng_ref[...] += jnp.sum(dav * vn, axis=0, keepdims=True)
        dlnb_ref[...] += jnp.sum(dav, axis=0, keepdims=True)
        dvn = dav * lng_v
        dgv = rstd * (dvn - jnp.mean(dvn, axis=-1, keepdims=True) - vn * jnp.mean(dvn * vn, axis=-1, keepdims=True))
        dv_pre = dgv * _gelu_grad(proj_v[:, aw:2 * aw])
        du_pre = da_u * _gelu_grad(proj_v[:, 0:aw])
        bb = proj_v[:, 2 * aw:2 * aw + bw]
        bc = proj_v[:, 2 * aw + bw:2 * aw + 2 * bw]
        bh = proj_v[:, 2 * aw + 2 * bw:]
        z = bc * bh
        zprev = jnp.where(i > 0, cp_ref[7:8, :] * hp_ref[7:8, :], 0.0)
        znext = jnp.where(i < nt - 1, cn_ref[0:1, :] * hn_ref[0:1, :], 0.0)
        zm1, zp1 = _shift_rows(z, zprev, znext)
        cwv = cw_ref[...]
        conv = zm1 * cwv[0:1, :] + z * cwv[1:2, :] + zp1 * cwv[2:3, :]
        dbb = db_out * conv
        dconv = db_out * bb
        dx_edge = jnp.concatenate([dxp_ref[...], dxn_ref[...]], axis=0).astype(BF16)
        dy_edge = _dot(dx_edge, w[aw:, :], NT)
        dcprev = jnp.where(i > 0, dy_edge[7:8, :] * bp_ref[7:8, :], 0.0)
        dcnext = jnp.where(i < nt - 1, dy_edge[8:9, :] * bn_ref[0:1, :], 0.0)
        dcm1, dcp1 = _shift_rows(dconv, dcprev, dcnext)
        dz = dcp1 * cwv[0:1, :] + dconv * cwv[1:2, :] + dcm1 * cwv[2:3, :]
        dcw_ref[0:1, :] += jnp.sum(dconv * zm1, axis=0, keepdims=True)
        dcw_ref[1:2, :] += jnp.sum(dconv * z, axis=0, keepdims=True)
        dcw_ref[2:3, :] += jnp.sum(dconv * zp1, axis=0, keepdims=True)
        dproj_ref[...] = jnp.concatenate([du_pre, dv_pre, dbb, dz * bh, dz * bc], axis=1).astype(BF16)

    row8 = lambda f: pl.BlockSpec((8, dm), f)
    return dict(
        body=body, grid=(nt,), name="even_core_bwd",
        args=[proj, proj, proj, proj, proj, proj, proj, dx1, dx1, dx1, lng, lnb, wsp, bspb, cw, gath],
        out_shape=[jax.ShapeDtypeStruct((t, inw), BF16), jax.ShapeDtypeStruct((1, aw), F32),
                   jax.ShapeDtypeStruct((1, aw), F32), jax.ShapeDtypeStruct(wsp.shape, F32),
                   jax.ShapeDtypeStruct((A_GROUPS, CHUNK, gd), F32), jax.ShapeDtypeStruct(cw.shape, F32)],
        in_specs=[_tok(tm, inw), prev[0], prev[1], prev[2], nxt[0], nxt[1], nxt[2], _tok(tm, dm),
                  row8(lambda i: (jnp.maximum(i * nb8 - 1, 0), 0)), row8(lambda i: (jnp.minimum((i + 1) * nb8, last8), 0)),
                  _full(lng.shape), _full(lnb.shape), _full(wsp.shape), _full(bspb.shape), _full(cw.shape), ANY],
        out_specs=[_tok(tm, inw), _full((1, aw)), _full((1, aw)), _full(wsp.shape),
                   _full((A_GROUPS, CHUNK, gd)), _full(cw.shape)],
        scratch=[pltpu.VMEM((gath.shape[1] * NDEV, dm), BF16), pltpu.SemaphoreType.DMA((NDEV,))])


def _ff_chunks(f, width=1024):
    return [(c0, min(c0 + width, f)) for c0 in range(0, f, width)]


def _ffn_up(x, gain, gath_g, gath_u, name, tm):
    t, dm = x.shape
    f = gath_g.shape[1] * NDEV

    def body(x_ref, g_ref, gg_ref, gu_ref, gate_ref, up_ref, act_ref, wg, wu, sems):
        @pl.when(pl.program_id(0) == 0)
        def _():
            _load_weight(gg_ref, wg, sems)
            _load_weight(gu_ref, wu, sems)
        h, _ = _rms_fwd(x_ref[...], g_ref[...])
        hb = h.astype(BF16)
        for c0, c1 in _ff_chunks(f):
            gate = _dot(hb, wg[c0:c1, :], NT)
            up = _dot(hb, wu[c0:c1, :], NT)
            gate_ref[:, c0:c1] = gate.astype(BF16)
            up_ref[:, c0:c1] = up.astype(BF16)
            act_ref[:, c0:c1] = (gate * _sigmoid(gate) * up).astype(BF16)

    o = jax.ShapeDtypeStruct((t, f), BF16)
    return dict(
        body=body, grid=(t // tm,), name=name, args=[x, gain, gath_g, gath_u], out_shape=[o, o, o],
        in_specs=[_tok(tm, dm), _full((1, dm)), ANY, ANY], out_specs=[_tok(tm, f)] * 3,
        scratch=[pltpu.VMEM((f, dm), BF16), pltpu.VMEM((f, dm), BF16), pltpu.SemaphoreType.DMA((NDEV,))])


def _ffn_down(x, act, gath_d, name, tm):
    t, dm = x.shape
    f = act.shape[1]

    def body(x_ref, a_ref, gd_ref, xo_ref, wd, sems):
        @pl.when(pl.program_id(0) == 0)
        def _():
            _load_weight(gd_ref, wd, sems)
        xo_ref[...] = x_ref[...] + _dot(a_ref[...], wd[...], NN)

    return dict(
        body=body, grid=(t // tm,), name=name, args=[x, act, gath_d], out_shape=[jax.ShapeDtypeStruct((t, dm), F32)],
        in_specs=[_tok(tm, dm), _tok(tm, f), ANY], out_specs=[_tok(tm, dm)],
        scratch=[pltpu.VMEM((f, dm), BF16), pltpu.SemaphoreType.DMA((NDEV,))])


def _ffn_bwd(dxo, x, gate, up, gain, gath_g, gath_u, gath_d, name, tm):
    t, dm = x.shape
    f = gate.shape[1]

    def body(dxo_ref, x_ref, gate_ref, up_ref, g_ref, gg_ref, gu_ref, gd_ref,
             dx_ref, dxb_ref, dg_ref, du_ref, hb_ref, dgain_ref, wg, wu, wd, sems):
        @pl.when(pl.program_id(0) == 0)
        def _():
            _load_weight(gg_ref, wg, sems)
            _load_weight(gu_ref, wu, sems)
            _load_weight(gd_ref, wd, sems)
            dgain_ref[...] = jnp.zeros_like(dgain_ref)
        xv, gain_v, dxo_v = x_ref[...], g_ref[...], dxo_ref[...]
        h, r = _rms_fwd(xv, gain_v)
        hb_ref[...] = h.astype(BF16)
        dxob = dxo_v.astype(BF16)
        dh = jnp.zeros_like(xv)
        for c0, c1 in _ff_chunks(f):
            gate_v = gate_ref[:, c0:c1].astype(F32)
            up_v = up_ref[:, c0:c1].astype(F32)
            s = _sigmoid(gate_v)
            silu = gate_v * s
            dact = _dot(dxob, wd[c0:c1, :], NT)
            dg = (dact * up_v * (s * (1.0 + gate_v * (1.0 - s)))).astype(BF16)
            du = (dact * silu).astype(BF16)
            dg_ref[:, c0:c1] = dg
            du_ref[:, c0:c1] = du
            dh = dh + _dot(dg, wg[c0:c1, :], NN) + _dot(du, wu[c0:c1, :], NN)
        dx, dgain = _rms_bwd(dh, xv, r, gain_v)
        dx = dxo_v + dx
        dx_ref[...] = dx
        dxb_ref[...] = dx.astype(BF16)
        dgain_ref[...] += dgain

    return dict(
        body=body, grid=(t // tm,), name=name, args=[dxo, x, gate, up, gain, gath_g, gath_u, gath_d],
        out_shape=[jax.ShapeDtypeStruct((t, dm), F32), jax.ShapeDtypeStruct((t, dm), BF16),
                   jax.ShapeDtypeStruct((t, f), BF16), jax.ShapeDtypeStruct((t, f), BF16),
                   jax.ShapeDtypeStruct((t, dm), BF16), jax.ShapeDtypeStruct((1, dm), F32)],
        in_specs=[_tok(tm, dm), _tok(tm, dm), _tok(tm, f), _tok(tm, f), _full((1, dm)), ANY, ANY, ANY],
        out_specs=[_tok(tm, dm), _tok(tm, dm), _tok(tm, f), _tok(tm, f), _tok(tm, dm), _full((1, dm))],
        scratch=[pltpu.VMEM((f, dm), BF16), pltpu.VMEM((f, dm), BF16), pltpu.VMEM((f, dm), BF16),
                 pltpu.SemaphoreType.DMA((NDEV,))])


def _t5_buckets(rel):
    nb = N_BUCKETS // 2
    ret = jnp.where(rel > 0, nb, 0)
    n = jnp.abs(rel)
    max_exact = nb // 2
    nf = jnp.maximum(n, 1).astype(jnp.float32)
    large = max_exact + (jnp.log(nf / max_exact) / math.log(MAX_DISTANCE / max_exact)
                         * (nb - max_exact)).astype(jnp.int32)
    large = jnp.minimum(large, nb - 1)
    return ret + jnp.where(n < max_exact, n, large)


def _bucket_table():
    qi = jnp.arange(CHUNK, dtype=jnp.int32)[:, None]
    kj = jnp.arange(3 * CHUNK, dtype=jnp.int32)[None, :]
    rel = kj - CHUNK - qi
    return jnp.where(jnp.abs(rel) <= CHUNK, _t5_buckets(rel), -1)


def _bias_table(rel_bias_t, buckets):
    nh = rel_bias_t.shape[0]

    def body(rb_ref, bk_ref, o_ref):
        bk = bk_ref[...]
        for h in range(nh):
            acc = jnp.where(bk < 0, NEG, 0.0).astype(F32)
            for b in range(N_BUCKETS):
                acc = jnp.where(bk == b, rb_ref[h, b] * LOG2E, acc)
            o_ref[h] = acc

    return pl.pallas_call(
        body, out_shape=jax.ShapeDtypeStruct((nh,) + buckets.shape, F32),
        in_specs=[pl.BlockSpec(memory_space=pltpu.SMEM), pl.BlockSpec(memory_space=pltpu.VMEM)],
        out_specs=pl.BlockSpec(memory_space=pltpu.VMEM), name="bias_table")(rel_bias_t, buckets)


def _rel_bias_grad(dbias, buckets):
    nh = dbias.shape[0]

    def body(db_ref, bk_ref, o_ref):
        bk = bk_ref[...]
        lane = lax.broadcasted_iota(jnp.int32, (1, 128), 1)
        for h in range(nh):
            d = db_ref[h]
            row = jnp.zeros((1, 128), F32)
            for b in range(N_BUCKETS):
                s = jnp.sum(jnp.sum(jnp.where(bk == b, d, 0.0), axis=1, keepdims=True), axis=0, keepdims=True)
                row = jnp.where(lane == b, s, row)
            o_ref[h:h + 1, :] = row

    return pl.pallas_call(
        body, out_shape=jax.ShapeDtypeStruct((nh, 128), F32),
        in_specs=[pl.BlockSpec(memory_space=pltpu.VMEM), pl.BlockSpec(memory_space=pltpu.VMEM)],
        out_specs=pl.BlockSpec(memory_space=pltpu.VMEM), compiler_params=_cp(0), name="rel_bias_grad")(dbias, buckets)


def _half_masks():
    lane = lax.broadcasted_iota(jnp.int32, (CHUNK, 128), 1)
    return lane < HEAD_DIM, lane >= HEAD_DIM


def _kv_low(ref, starts, hk, lo):
    kt = (hk // 2) * 128
    out = []
    for jj in range(3):
        blk = ref[pl.ds(starts[jj], CHUNK), kt:kt + 128]
        if hk % 2 == 1:
            blk = pltpu.roll(blk, HEAD_DIM, 1)
        out.append(jnp.where(lo, blk, jnp.zeros_like(blk)))
    return out


def _stack_heads(tile_a, tile_b):
    return jnp.concatenate([tile_a, pltpu.roll(tile_a, HEAD_DIM, 1), tile_b, pltpu.roll(tile_b, HEAD_DIM, 1)], axis=0)


def _unstack_heads(o4):
    return (o4[0:CHUNK] + pltpu.roll(o4[CHUNK:2 * CHUNK], HEAD_DIM, 1),
            o4[2 * CHUNK:3 * CHUNK] + pltpu.roll(o4[3 * CHUNK:], HEAD_DIM, 1))


ATT_SLAB = 32


def _softmax_slab(s_scr, hk, g, r0, bias_ref, sink_ref, n, nblk):
    scale = HEAD_DIM ** -0.5 * LOG2E
    h = (N_HEADS // N_KV) * hk + g
    s = []
    for jj in range(3):
        sj = (s_scr[hk, jj, pl.ds(g * CHUNK + r0, ATT_SLAB), :] * scale
              + bias_ref[h, pl.ds(r0, ATT_SLAB), jj * CHUNK:(jj + 1) * CHUNK])
        if jj == 0:
            sj = jnp.where(n > 0, sj, NEG)
        if jj == 2:
            sj = jnp.where(n < nblk - 1, sj, NEG)
        s.append(sj)
    sink = sink_ref[h] * LOG2E
    m = jnp.maximum(jnp.max(jnp.maximum(jnp.maximum(s[0], s[1]), s[2]), axis=-1, keepdims=True), sink)
    e = [jnp.exp2(sj - m) for sj in s]
    es = jnp.exp2(sink - m)
    inv = 1.0 / (jnp.sum(e[0] + e[1] + e[2], axis=-1, keepdims=True) + es)
    return [ej * inv for ej in e], es * inv


def _key_block_starts(n, nblk):
    return [pl.multiple_of(jnp.clip(n - 1 + jj, 0, nblk - 1) * CHUNK, CHUNK) for jj in range(3)]


def _attn_fwd(qkv, x2, bias, sink, gath):
    t, dm = x2.shape
    nblk = t // CHUNK
    kvw = N_KV * HEAD_DIM
    kcb, vcb = dm // kvw, dm // kvw + 1

    def body(q_ref, k_ref, v_ref, x2_ref, bias_ref, sink_ref, gath_ref, x3_ref, att_ref, wbuf, s_scr, p_scr, sems):
        n = pl.program_id(0)

        @pl.when(n == 0)
        def _():
            _load_weight(gath_ref, wbuf, sems)
        lo, _ = _half_masks()
        starts = _key_block_starts(n, nblk)
        tiles = []
        for hk in range(N_KV):
            c0 = (2 * hk) * 128
            k_lo = _kv_low(k_ref, starts, hk, lo)
            v_lo = _kv_low(v_ref, starts, hk, lo)
            q4 = _stack_heads(q_ref[:, c0:c0 + 128], q_ref[:, c0 + 128:c0 + 256])
            for jj in range(3):
                s_scr[hk, jj] = _dot(q4, k_lo[jj], NT)
            for g in range(4):
                for r0 in range(0, CHUNK, ATT_SLAB):
                    p, _ = _softmax_slab(s_scr, hk, g, r0, bias_ref, sink_ref, n, nblk)
                    for jj in range(3):
                        p_scr[hk, jj, g * CHUNK + r0:g * CHUNK + r0 + ATT_SLAB, :] = p[jj].astype(BF16)
            o4 = _dot(p_scr[hk, 0], v_lo[0], NN) + _dot(p_scr[hk, 1], v_lo[1], NN) + _dot(p_scr[hk, 2], v_lo[2], NN)
            tiles += list(_unstack_heads(o4))
        att = jnp.concatenate(tiles, axis=1).astype(BF16)
        att_ref[...] = att
        x3_ref[...] = x2_ref[...] + _dot(att, wbuf[...], NN)

    blk = pl.BlockSpec((CHUNK, dm), lambda n: (n, 0))
    return dict(
        body=body, grid=(nblk,), name="attn_fwd", args=[qkv, qkv, qkv, x2, bias, sink, gath],
        out_shape=[jax.ShapeDtypeStruct((t, dm), F32), jax.ShapeDtypeStruct((t, dm), BF16)],
        in_specs=[blk, pl.BlockSpec((t, kvw), lambda n: (0, kcb)), pl.BlockSpec((t, kvw), lambda n: (0, vcb)), blk,
                  _full(bias.shape), pl.BlockSpec(memory_space=pltpu.SMEM), ANY],
        out_specs=[blk, blk],
        scratch=[pltpu.VMEM((gath.shape[1] * NDEV, dm), BF16), pltpu.VMEM((N_KV, 3, 4 * CHUNK, CHUNK), F32),
                 pltpu.VMEM((N_KV, 3, 4 * CHUNK, CHUNK), BF16), pltpu.SemaphoreType.DMA((NDEV,))])


def _attn_bwd(qkv, att, dx3, bias, sink, gath):
    t, dm = dx3.shape
    nblk = t // CHUNK
    kvw = N_KV * HEAD_DIM
    kcb, vcb = dm // kvw, dm // kvw + 1
    scale = HEAD_DIM ** -0.5
    slab = (N_KV, 3, 4 * CHUNK, CHUNK)

    def body(q_ref, k_ref, v_ref, att_ref, dx_ref, bias_ref, sink_ref, gath_ref,
             dq_ref, dk_ref, dv_ref, dbias_ref, dsink_ref, wbuf, s_scr, dp_scr, p_scr, ds_scr, prod_scr, sems):
        n = pl.program_id(0)

        @pl.when(n == 0)
        def _():
            _load_weight(gath_ref, wbuf, sems)
            dk_ref[...] = jnp.zeros_like(dk_ref)
            dv_ref[...] = jnp.zeros_like(dv_ref)
            dbias_ref[...] = jnp.zeros_like(dbias_ref)
            dsink_ref[...] = jnp.zeros_like(dsink_ref)
        lo, hi = _half_masks()
        lane1 = lax.broadcasted_iota(jnp.int32, (1, 128), 1)
        starts = _key_block_starts(n, nblk)
        dout = _dot(dx_ref[...].astype(BF16), wbuf[...], NT)
        prod_scr[...] = dout * att_ref[...].astype(F32)
        doutb = dout.astype(BF16)
        dq_tiles = []
        dsink_row = jnp.zeros((1, 128), F32)
        for hk in range(N_KV):
            kt = (hk // 2) * 128
            c0 = (2 * hk) * 128
            k_lo = _kv_low(k_ref, starts, hk, lo)
            v_lo = _kv_low(v_ref, starts, hk, lo)
            q4 = _stack_heads(q_ref[:, c0:c0 + 128], q_ref[:, c0 + 128:c0 + 256])
            do4 = _stack_heads(doutb[:, c0:c0 + 128], doutb[:, c0 + 128:c0 + 256])
            for jj in range(3):
                s_scr[hk, jj] = _dot(q4, k_lo[jj], NT)
                dp_scr[hk, jj] = _dot(do4, v_lo[jj], NT)
            for g in range(4):
                h = 4 * hk + g
                sink_acc = jnp.zeros((1, 1), F32)
                for r0 in range(0, CHUNK, ATT_SLAB):
                    rows = slice(g * CHUNK + r0, g * CHUNK + r0 + ATT_SLAB)
                    p, ps = _softmax_slab(s_scr, hk, g, r0, bias_ref, sink_ref, n, nblk)
                    pt = prod_scr[r0:r0 + ATT_SLAB, c0 + (g // 2) * 128:c0 + (g // 2 + 1) * 128]
                    lane_s = lax.broadcasted_iota(jnp.int32, (ATT_SLAB, 128), 1)
                    msk = lane_s < HEAD_DIM if g % 2 == 0 else lane_s >= HEAD_DIM
                    dsum = jnp.sum(jnp.where(msk, pt, 0.0), axis=-1, keepdims=True)
                    sink_acc = sink_acc + jnp.sum(ps * dsum, axis=0, keepdims=True)
                    for jj in range(3):
                        ds = p[jj] * (dp_scr[hk, jj, rows, :] - dsum)
                        dbias_ref[h, r0:r0 + ATT_SLAB, jj * CHUNK:(jj + 1) * CHUNK] += ds
                        ds_scr[hk, jj, rows, :] = ds.astype(BF16)
                        p_scr[hk, jj, rows, :] = p[jj].astype(BF16)
                dsink_row = dsink_row + jnp.where(lane1 == h, -sink_acc, 0.0)
            dq4 = jnp.zeros((4 * CHUNK, 128), F32)
            for jj in range(3):
                ds4 = ds_scr[hk, jj]
                dq4 = dq4 + _dot(ds4, k_lo[jj], NN) * scale
                dkj = _dot(ds4, q4, TN) * scale
                dvj = _dot(p_scr[hk, jj], do4, TN)
                if hk % 2 == 1:
                    dkj, dvj = pltpu.roll(dkj, HEAD_DIM, 1), pltpu.roll(dvj, HEAD_DIM, 1)
                keep = lo if hk % 2 == 0 else hi
                dk_ref[pl.ds(starts[jj], CHUNK), kt:kt + 128] += jnp.where(keep, dkj, 0.0)
                dv_ref[pl.ds(starts[jj], CHUNK), kt:kt + 128] += jnp.where(keep, dvj, 0.0)
            dq_tiles += list(_unstack_heads(dq4))
        dq_ref[...] = jnp.concatenate(dq_tiles, axis=1).astype(BF16)
        dsink_ref[...] += dsink_row

    blk = pl.BlockSpec((CHUNK, dm), lambda n: (n, 0))
    return dict(
        body=body, grid=(nblk,), name="attn_bwd", args=[qkv, qkv, qkv, att, dx3, bias, sink, gath],
        out_shape=[jax.ShapeDtypeStruct((t, dm), BF16), jax.ShapeDtypeStruct((t, kvw), F32),
                   jax.ShapeDtypeStruct((t, kvw), F32), jax.ShapeDtypeStruct(bias.shape, F32),
                   jax.ShapeDtypeStruct((1, 128), F32)],
        in_specs=[blk, pl.BlockSpec((t, kvw), lambda n: (0, kcb)), pl.BlockSpec((t, kvw), lambda n: (0, vcb)),
                  blk, blk, _full(bias.shape), pl.BlockSpec(memory_space=pltpu.SMEM), ANY],
        out_specs=[blk, _full((t, kvw)), _full((t, kvw)), _full(bias.shape), _full((1, 128))],
        scratch=[pltpu.VMEM((gath.shape[1] * NDEV, dm), BF16),
                 pltpu.VMEM(slab, F32), pltpu.VMEM(slab, F32), pltpu.VMEM(slab, BF16), pltpu.VMEM(slab, BF16),
                 pltpu.VMEM((CHUNK, dm), F32), pltpu.SemaphoreType.DMA((NDEV,))])


def _final_loss(x4, target, gain, tm):
    t, dm = x4.shape
    steps = t // tm

    def body(x_ref, t_ref, g_ref, loss_ref, dx_ref, dxb_ref, dgain_ref, acc):
        i = pl.program_id(0)

        @pl.when(i == 0)
        def _():
            acc[...] = jnp.zeros_like(acc)
            dgain_ref[...] = jnp.zeros_like(dgain_ref)
        xv, gain_v = x_ref[...], g_ref[...]
        y, r = _rms_fwd(xv, gain_v)
        e = y - t_ref[...]
        acc[...] += jnp.sum(e * e, axis=0, keepdims=True)
        dx, dgain = _rms_bwd(e * (1.0 / dm), xv, r, gain_v)
        dx_ref[...] = dx
        dxb_ref[...] = dx.astype(BF16)
        dgain_ref[...] += dgain

        @pl.when(i == steps - 1)
        def _():
            loss_ref[...] = jnp.sum(acc[...], axis=-1, keepdims=True) * (0.5 / dm)

    return dict(
        body=body, grid=(steps,), name="final_loss", args=[x4, target, gain],
        out_shape=[jax.ShapeDtypeStruct((1, 1), F32), jax.ShapeDtypeStruct((t, dm), F32),
                   jax.ShapeDtypeStruct((t, dm), BF16), jax.ShapeDtypeStruct((1, dm), F32)],
        in_specs=[_tok(tm, dm), _tok(tm, dm), _full((1, dm))],
        out_specs=[_full((1, 1)), _tok(tm, dm), _tok(tm, dm), _full((1, dm))],
        scratch=[pltpu.VMEM((1, dm), F32)])


def _finish_weight(recvs, w, m, v, name):
    nl, r, dm = w.shape
    assert nl == len(recvs) and all(rc.shape[1:] == (r, dm) for rc in recvs)
    td = dm // 2
    wspec = pl.BlockSpec((None, r, td), lambda l, j: (l, 0, j))

    def body(*refs):
        r_refs = refs[:nl]
        w_ref, m_ref, v_ref, g_ref, d_ref, nm_ref, nv_ref = refs[nl:]
        layer = pl.program_id(0)
        for li in range(nl):
            @pl.when(layer == li)
            def _():
                g = r_refs[li][0].astype(F32)
                for d in range(1, recvs[li].shape[0]):
                    g = g + r_refs[li][d].astype(F32)
                delta, nm, nv = _adamw_math(w_ref[...], g, m_ref[...], v_ref[...])
                g_ref[...] = g
                d_ref[...] = delta
                nm_ref[...] = nm
                nv_ref[...] = nv

    o = jax.ShapeDtypeStruct(w.shape, F32)
    return dict(
        body=body, grid=(nl, 2), name=name, args=[*recvs, w, m, v], out_shape=[o, o, o, o],
        in_specs=[pl.BlockSpec((rc.shape[0], r, td), lambda l, j: (0, 0, j)) for rc in recvs] + [wspec] * 3,
        out_specs=[wspec] * 4)


def _adamw_small(w, g_slots, m, v, name):
    r, c = w.shape

    def body(w_ref, g_ref, m_ref, v_ref, gs_ref, d_ref, nm_ref, nv_ref):
        g = g_ref[0]
        for d in range(1, NDEV):
            g = g + g_ref[d]
        gs_ref[...] = g
        d_ref[...], nm_ref[...], nv_ref[...] = _adamw_math(w_ref[...], g, m_ref[...], v_ref[...])

    spec = pl.BlockSpec((r, c), lambda i: (0, 0))
    out = jax.ShapeDtypeStruct((r, c), F32)
    return pl.pallas_call(
        body, grid=(1,), out_shape=(out,) * 4,
        in_specs=[spec, pl.BlockSpec((NDEV, r, c), lambda i: (0, 0, 0)), spec, spec], out_specs=(spec,) * 4,
        compiler_params=_cp(), name=name)(w, g_slots, m, v)


def _pack_small(parts, rows):
    flat = jnp.concatenate([p.reshape(-1) for p in parts])
    return jnp.pad(flat, (0, rows * 128 - flat.shape[0])).reshape(rows, 128)


def _unpack_small(packed, shapes):
    flat = packed.reshape(-1)
    out, o = [], 0
    for s in shapes:
        n = int(np.prod(s))
        out.append(flat[o:o + n].reshape(s))
        o += n
    return out


def kernel(x, norm_mix, norm_ffn, even_w_in, even_v_ln_g, even_v_ln_b, even_w_spatial, even_b_spatial, even_conv_w, even_w_out, attn_w_qkv, attn_sink, rel_bias, attn_w_out, ffn_w_gate, ffn_w_up, ffn_w_down, final_norm, loss_target, m_norm_mix, m_norm_ffn, m_even_w_in, m_even_v_ln_g, m_even_v_ln_b, m_even_w_spatial, m_even_b_spatial, m_even_conv_w, m_even_w_out, m_attn_w_qkv, m_attn_sink, m_rel_bias, m_attn_w_out, m_ffn_w_gate, m_ffn_w_up, m_ffn_w_down, m_final_norm, v_norm_mix, v_norm_ffn, v_even_w_in, v_even_v_ln_g, v_even_v_ln_b, v_even_w_spatial, v_even_b_spatial, v_even_conv_w, v_even_w_out, v_attn_w_qkv, v_attn_sink, v_rel_bias, v_attn_w_out, v_ffn_w_gate, v_ffn_w_up, v_ffn_w_down, v_final_norm):
    t, dm = x.shape[1], x.shape[2]
    aw = even_v_ln_g.shape[1]
    bw = even_conv_w.shape[2] * NDEV
    gd = aw // A_GROUPS
    tm = min(512, t // 2)
    tmf = min(256, t // 2)
    me = _my_index()
    row = lambda a: a.reshape(1, -1)

    colT = lambda w: w.T.astype(BF16)
    sh = dict(winT=colT(even_w_in[0]), wqkvT=colT(attn_w_qkv[0]), wgT0=colT(ffn_w_gate[0]), wuT0=colT(ffn_w_up[0]),
              wgT1=colT(ffn_w_gate[1]), wuT1=colT(ffn_w_up[1]), woe=even_w_out[0].astype(BF16),
              woa=attn_w_out[0].astype(BF16), wd0=ffn_w_down[0].astype(BF16), wd1=ffn_w_down[1].astype(BF16))
    gather = lambda names: _GatherCarry([sh[n] for n in names])

    in_full = lambda a: lax.dynamic_update_slice(jnp.zeros((3, bw), F32), a[0], (0, me * (bw // NDEV)))
    cw_rows = 3 * bw // 128
    cw_mine = jnp.pad(in_full(even_conv_w).reshape(cw_rows, 128), ((0, 16 - cw_rows), (0, 0)))

    x0 = x[0]
    wsp_b = even_w_spatial[0].astype(BF16)
    bspb = jnp.broadcast_to(even_b_spatial[0][:, :, None], (A_GROUPS, CHUNK, gd))
    buckets = _bucket_table()
    bias = _bias_table(rel_bias.T, buckets)
    sink = attn_sink[0]

    (g_winT,), (cw_slots,) = _exchange_only([gather(["winT"]), _BroadcastCarry(cw_mine)], "ag_w_in")
    cw_full = jnp.sum(cw_slots, axis=0)[0:cw_rows].reshape(3, bw)
    (proj, h0b), (g_woe, g_wgT0) = _call(_norm_proj(x0, row(norm_mix[0]), g_winT, F32, "in_proj", tm),
                                         gather(["woe", "wgT0"]))
    (x1, yb), (g_wuT0,) = _call(_even_core_fwd(proj, x0, even_v_ln_g, even_v_ln_b, wsp_b, bspb, cw_full, g_woe, tm),
                                gather(["wuT0"]))
    (gate0, up0, act0), (g_wd0,) = _call(_ffn_up(x1, row(norm_ffn[0]), g_wgT0, g_wuT0, "ffn_up0", tmf), gather(["wd0"]))
    (x2,), (g_wqkvT,) = _call(_ffn_down(x1, act0, g_wd0, "ffn_down0", tm), gather(["wqkvT"]))
    (qkv, h2b), (g_woa,) = _call(_norm_proj(x2, row(norm_mix[1]), g_wqkvT, BF16, "qkv_proj", tm), gather(["woa"]))
    (x3, attb), (g_wgT1, g_wuT1) = _call(_attn_fwd(qkv, x2, bias, sink, g_woa), gather(["wgT1", "wuT1"]))
    (gate1, up1, act1), (g_wd1,) = _call(_ffn_up(x3, row(norm_ffn[1]), g_wgT1, g_wuT1, "ffn_up1", tmf), gather(["wd1"]))
    (x4,), _ = _call(_ffn_down(x3, act1, g_wd1, "ffn_down1", tm))
    (loss_part, dx4, dx4b, d_final), _ = _call(_final_loss(x4, loss_target[0], row(final_norm), tm))

    (dx3, dx3b, dg1, du1, h3b, d_nffn1), _ = _call(
        _ffn_bwd(dx4, x3, gate1, up1, row(norm_ffn[1]), g_wgT1, g_wuT1, g_wd1, "ffn_bwd1", tmf))
    (p_wgT1,), _ = _call(_wgrad(dg1, h3b, "wgrad_gate1"))
    (p_wuT1,), _ = _call(_wgrad(du1, h3b, "wgrad_up1"))
    (p_wd1,), _ = _call(_wgrad(act1, dx4b, "wgrad_down1"))
    (dq, dk, dv, dbias, dsink), (r_wgT1, r_wuT1, r_wd1) = _call(
        _attn_bwd(qkv, attb, dx3, bias, sink, g_woa), _GradCarry([p_wgT1, p_wuT1, p_wd1]))
    (p_woa,), _ = _call(_wgrad(attb, dx3b, "wgrad_attn_out"))
    d_relb = _rel_bias_grad(dbias, buckets)[:, 0:N_BUCKETS].T
    dqkv = jnp.concatenate([dq, dk.astype(BF16), dv.astype(BF16)], axis=1)
    (dx2, dx2b, d_nmix1), _ = _call(_proj_bwd_norm(dqkv, x2, row(norm_mix[1]), dx3, g_wqkvT, "qkv_bwd", tm))
    (p_wqkvT,), _ = _call(_wgrad(dqkv, h2b, "wgrad_qkv"))
    (dx1, dx1b, dg0, du0, h1b, d_nffn0), (r_woa, r_wqkvT) = _call(
        _ffn_bwd(dx2, x1, gate0, up0, row(norm_ffn[0]), g_wgT0, g_wuT0, g_wd0, "ffn_bwd0", tmf),
        _GradCarry([p_woa, p_wqkvT]))
    (dproj, d_lng, d_lnb, d_wsp, d_bsp3, d_cw), _ = _call(
        _even_core_bwd(proj, dx1, even_v_ln_g, even_v_ln_b, wsp_b, bspb, cw_full, g_woe, tm))
    (p_winT,), _ = _call(_wgrad(dproj, h0b, "wgrad_in"))
    (p_wgT0,), (a_winT,) = _call(_wgrad(dg0, h1b, "wgrad_gate0"), _PairCarry(p_winT))
    (p_wuT0,), ((r_winT,), (a_wgT0,)) = _call(
        _wgrad(du0, h1b, "wgrad_up0"), [_ChipSumCarry(p_winT, a_winT), _PairCarry(p_wgT0)])
    (p_wd0,), ((r_wgT0,), (a_wuT0,)) = _call(
        _wgrad(act0, dx2b, "wgrad_down0"), [_ChipSumCarry(p_wgT0, a_wgT0), _PairCarry(p_wuT0)])
    (p_woe,), ((r_wuT0,), (a_wd0,)) = _call(
        _wgrad(yb, dx1b, "wgrad_even_out"), [_ChipSumCarry(p_wuT0, a_wuT0), _PairCarry(p_wd0)])
    (dx0, _, d_nmix0), ((r_wd0,), (r_woe,)) = _call(
        _proj_bwd_norm(dproj, x0, row(norm_mix[0]), dx1, g_winT, "in_proj_bwd", tm),
        [_ChipSumCarry(p_wd0, a_wd0), _GradCarry([p_woe])])

    small_shapes = [(2, dm), (2, dm), (1, aw), (1, aw), (1, A_GROUPS, CHUNK, CHUNK), (1, A_GROUPS, CHUNK), (3, bw),
                    (1, N_HEADS), (N_BUCKETS, N_HEADS), (dm,), (1, 1)]
    small_names = ["norm_mix", "norm_ffn", "even_v_ln_g", "even_v_ln_b", "even_w_spatial", "even_b_spatial",
                   "even_conv_w", "attn_sink", "rel_bias", "final_norm", "loss"]
    n_small = sum(int(np.prod(s)) for s in small_shapes)
    small_rows = 8 * ((n_small + 1023) // 1024)
    small_part = _pack_small(
        [jnp.concatenate([d_nmix0, d_nmix1]), jnp.concatenate([d_nffn0, d_nffn1]), d_lng, d_lnb, d_wsp,
         jnp.sum(d_bsp3, axis=-1), d_cw, dsink[:, 0:N_HEADS], d_relb, d_final, loss_part], small_rows)
    grads = {}

    order = ["norm_mix", "norm_ffn", "even_w_in", "even_v_ln_g", "even_v_ln_b", "even_w_spatial", "even_b_spatial",
             "even_conv_w", "even_w_out", "attn_w_qkv", "attn_sink", "rel_bias", "attn_w_out", "ffn_w_gate",
             "ffn_w_up", "ffn_w_down", "final_norm"]
    ws = dict(norm_mix=norm_mix, norm_ffn=norm_ffn, even_w_in=even_w_in, even_v_ln_g=even_v_ln_g,
              even_v_ln_b=even_v_ln_b, even_w_spatial=even_w_spatial, even_b_spatial=even_b_spatial,
              even_conv_w=even_conv_w, even_w_out=even_w_out, attn_w_qkv=attn_w_qkv, attn_sink=attn_sink,
              rel_bias=rel_bias, attn_w_out=attn_w_out, ffn_w_gate=ffn_w_gate, ffn_w_up=ffn_w_up,
              ffn_w_down=ffn_w_down, final_norm=final_norm)
    ms = dict(norm_mix=m_norm_mix, norm_ffn=m_norm_ffn, even_w_in=m_even_w_in, even_v_ln_g=m_even_v_ln_g,
              even_v_ln_b=m_even_v_ln_b, even_w_spatial=m_even_w_spatial, even_b_spatial=m_even_b_spatial,
              even_conv_w=m_even_conv_w, even_w_out=m_even_w_out, attn_w_qkv=m_attn_w_qkv, attn_sink=m_attn_sink,
              rel_bias=m_rel_bias, attn_w_out=m_attn_w_out, ffn_w_gate=m_ffn_w_gate, ffn_w_up=m_ffn_w_up,
              ffn_w_down=m_ffn_w_down, final_norm=m_final_norm)
    vs = dict(norm_mix=v_norm_mix, norm_ffn=v_norm_ffn, even_w_in=v_even_w_in, even_v_ln_g=v_even_v_ln_g,
              even_v_ln_b=v_even_v_ln_b, even_w_spatial=v_even_w_spatial, even_b_spatial=v_even_b_spatial,
              even_conv_w=v_even_conv_w, even_w_out=v_even_w_out, attn_w_qkv=v_attn_w_qkv, attn_sink=v_attn_sink,
              rel_bias=v_rel_bias, attn_w_out=v_attn_w_out, ffn_w_gate=v_ffn_w_gate, ffn_w_up=v_ffn_w_up,
              ffn_w_down=v_ffn_w_down, final_norm=v_final_norm)
    big = dict(ffn_w_gate=([r_wgT0, r_wgT1], True), even_w_in=([r_winT], True), even_w_out=([r_woe], False),
               attn_w_qkv=([r_wqkvT], True), attn_w_out=([r_woa], False), ffn_w_up=([r_wuT0, r_wuT1], True),
               ffn_w_down=([r_wd0, r_wd1], False))
    delta, new_m, new_v = {}, {}, {}
    small_slots = None
    for n, (recvs, transposed) in big.items():
        lay = (lambda a: jnp.swapaxes(a, 1, 2)) if transposed else (lambda a: a)
        spec = _finish_weight(recvs, lay(ws[n]), lay(ms[n]), lay(vs[n]), "finish_" + n)
        if small_slots is None:
            outs, (small_slots,) = _call(spec, _BroadcastCarry(small_part))
        else:
            outs, _ = _call(spec)
        grads[n], delta[n], new_m[n], new_v[n] = [lay(o) for o in outs]
    pk = lambda dct: _pack_small(
        [in_full(dct[n]) if n == "even_conv_w" else (jnp.zeros((1, 1), F32) if n == "loss" else dct[n])
         for n in small_names], small_rows)
    packed = _adamw_small(pk(ws), small_slots, pk(ms), pk(vs), "adamw_small")
    mine = lambda a: lax.dynamic_slice(a, (0, me * (bw // NDEV)), (3, bw // NDEV))[None]
    for dst, arr in zip((grads, delta, new_m, new_v), packed):
        for n, a in zip(small_names, _unpack_small(arr, small_shapes)):
            dst[n] = mine(a) if n == "even_conv_w" else a
    loss = grads["loss"][0, 0]
    return (loss, dx0[None], *[grads[n] for n in order], *[delta[n] for n in order],
            *[new_m[n] for n in order], *[new_v[n] for n in order])
```

```python
import math

import jax
import jax.numpy as jnp
import numpy as np
from jax import lax
from jax.experimental import pallas as pl
from jax.experimental.pallas import tpu as pltpu

F32, BF16 = jnp.float32, jnp.bfloat16
NDEV = 8
EPS = 1e-6
CHUNK = 128
A_GROUPS = 4
N_HEADS, N_KV, HEAD_DIM = 16, 4, 64
N_BUCKETS, MAX_DISTANCE = 32, 128
NEG = -1e30
LOG2E = 1.4426950408889634
ADAM_LR, ADAM_B1, ADAM_B2, ADAM_EPS, ADAM_WD, ADAM_STEP = 0.001, 0.9, 0.999, 1e-08, 0.01, 10
VMEM_LIMIT = 56 * 1024 * 1024
MESH = pl.DeviceIdType.MESH
NT = (((1,), (1,)), ((), ()))
NN = (((1,), (0,)), ((), ()))
TN = (((0,), (0,)), ((), ()))
ANY = pl.BlockSpec(memory_space=pl.ANY)


def _cp(n_grid=1):
    return pltpu.CompilerParams(dimension_semantics=("arbitrary",) * n_grid, vmem_limit_bytes=VMEM_LIMIT)


def _dot(a, b, dims):
    return lax.dot_general(a, b, dims, preferred_element_type=F32)


def _my_index():
    return 4 * lax.axis_index("x") + 2 * lax.axis_index("y") + lax.axis_index("c")


def _peer(k):
    x, y, c = lax.axis_index("x"), lax.axis_index("y"), lax.axis_index("c")
    px = 1 - x if k & 4 else x
    py = 1 - y if k & 2 else y
    pc = 1 - c if k & 1 else c
    return (px, py, pc)


def _load_weight(gath_ref, wbuf, sems):
    rows = gath_ref.shape[1]
    cps = [pltpu.make_async_copy(gath_ref.at[d], wbuf.at[pl.ds(d * rows, rows), :], sems.at[d]) for d in range(NDEV)]
    for c in cps:
        c.start()
    for c in cps:
        c.wait()


class _GatherCarry:
    def __init__(self, pieces):
        self.inputs = list(pieces)
        self.n = len(pieces)
        self.out_shape = [jax.ShapeDtypeStruct((NDEV,) + p.shape, p.dtype) for p in pieces]
        self.scratch = [pltpu.SemaphoreType.DMA((7 * self.n,)), pltpu.SemaphoreType.DMA((7 * self.n,)),
                        pltpu.SemaphoreType.DMA((self.n,))]

    def _ctx(self):
        x, y, c = lax.axis_index("x"), lax.axis_index("y"), lax.axis_index("c")
        chips = [(1 - x, y), (x, 1 - y), (1 - x, 1 - y)]
        return (x, y, c), (x, y, 1 - c), chips, c

    def _copy(self, k, j, block, to, ins, outs, sems, src=None):
        send_sems, recv_sems, _ = sems
        slot = outs[j].at[4 * block[0] + 2 * block[1] + block[2]]
        return pltpu.make_async_remote_copy(
            src_ref=slot if src is None else src, dst_ref=slot, send_sem=send_sems.at[k * self.n + j],
            recv_sem=recv_sems.at[k * self.n + j], device_id=to, device_id_type=MESH)

    def start(self, ins, outs, sems):
        me, sibling, chips, c = self._ctx()
        for j in range(self.n):
            pltpu.make_async_copy(ins[j], outs[j].at[4 * me[0] + 2 * me[1] + me[2]], sems[2].at[j]).start()
            self._copy(0, j, me, sibling, ins, outs, sems, src=ins[j]).start()
            for q, chip in enumerate(chips):
                self._copy(1 + q, j, me, (*chip, c), ins, outs, sems, src=ins[j]).start()

    def mid(self, ins, outs, sems):
        me, sibling, chips, c = self._ctx()
        for q, chip in enumerate(chips):
            for j in range(self.n):
                self._copy(1 + q, j, (*chip, c), me, ins, outs, sems).wait_recv()
                self._copy(4 + q, j, (*chip, c), sibling, ins, outs, sems).start()

    def finish(self, ins, outs, sems):
        me, sibling, chips, c = self._ctx()
        for j in range(self.n):
            self._copy(0, j, sibling, me, ins, outs, sems).wait_recv()
            for q, chip in enumerate(chips):
                self._copy(4 + q, j, (*chip, 1 - c), me, ins, outs, sems).wait_recv()
        for j in range(self.n):
            self._copy(0, j, me, sibling, ins, outs, sems, src=ins[j]).wait_send()
            for q, chip in enumerate(chips):
                self._copy(1 + q, j, me, (*chip, c), ins, outs, sems, src=ins[j]).wait_send()
                self._copy(4 + q, j, (*chip, c), sibling, ins, outs, sems).wait_send()
            pltpu.make_async_copy(ins[j], outs[j].at[0], sems[2].at[j]).wait()


class _GradCarry:
    def __init__(self, pieces):
        self.inputs = list(pieces)
        self.n = len(pieces)
        self.rows = [p.shape[0] // NDEV for p in pieces]
        self.out_shape = [jax.ShapeDtypeStruct((NDEV, r, p.shape[1]), p.dtype) for p, r in zip(pieces, self.rows)]
        self.scratch = [pltpu.SemaphoreType.DMA((7 * self.n,)), pltpu.SemaphoreType.DMA((7 * self.n,)),
                        pltpu.SemaphoreType.DMA((self.n,))]

    def _copies(self, ins, outs, sems):
        me = _my_index()
        local, remote = [], []
        for j in range(self.n):
            r = self.rows[j]
            local.append(pltpu.make_async_copy(ins[j].at[pl.ds(pl.multiple_of(me * r, 16), r), :], outs[j].at[me],
                                               sems[2].at[j]))
            for k in range(1, NDEV):
                peer = _peer(k)
                pidx = 4 * peer[0] + 2 * peer[1] + peer[2]
                remote.append(pltpu.make_async_remote_copy(
                    src_ref=ins[j].at[pl.ds(pl.multiple_of(pidx * r, 16), r), :], dst_ref=outs[j].at[me],
                    send_sem=sems[0].at[(k - 1) * self.n + j], recv_sem=sems[1].at[(k - 1) * self.n + j],
                    device_id=peer, device_id_type=MESH))
        return local, remote

    def start(self, ins, outs, sems):
        local, remote = self._copies(ins, outs, sems)
        for cp in local + remote:
            cp.start()

    def mid(self, ins, outs, sems):
        pass

    def finish(self, ins, outs, sems):
        local, remote = self._copies(ins, outs, sems)
        for cp in remote + local:
            cp.wait()


class _BroadcastCarry:
    def __init__(self, part):
        self.inputs = [part]
        self.out_shape = [jax.ShapeDtypeStruct((NDEV,) + part.shape, part.dtype)]
        self.scratch = [pltpu.SemaphoreType.DMA((7,)), pltpu.SemaphoreType.DMA((7,)), pltpu.SemaphoreType.DMA(())]

    def _copies(self, ins, outs, sems):
        me = _my_index()
        local = pltpu.make_async_copy(ins[0], outs[0].at[me], sems[2])
        remote = [pltpu.make_async_remote_copy(
            src_ref=ins[0], dst_ref=outs[0].at[me], send_sem=sems[0].at[k - 1], recv_sem=sems[1].at[k - 1],
            device_id=_peer(k), device_id_type=MESH) for k in range(1, NDEV)]
        return [local] + remote

    def start(self, ins, outs, sems):
        for cp in self._copies(ins, outs, sems):
            cp.start()

    def mid(self, ins, outs, sems):
        pass

    def finish(self, ins, outs, sems):
        for cp in self._copies(ins, outs, sems):
            cp.wait()


class _PairCarry:
    def __init__(self, piece):
        self.inputs = [piece]
        self.r = piece.shape[0] // NDEV
        self.out_shape = [jax.ShapeDtypeStruct((4, self.r, piece.shape[1]), piece.dtype)]
        self.scratch = [pltpu.SemaphoreType.DMA((4,)), pltpu.SemaphoreType.DMA((4,))]

    def _copies(self, ins, outs, sems):
        x, y, c = lax.axis_index("x"), lax.axis_index("y"), lax.axis_index("c")
        return [pltpu.make_async_remote_copy(
            src_ref=ins[0].at[pl.ds(pl.multiple_of((2 * q + 1 - c) * self.r, 16), self.r), :], dst_ref=outs[0].at[q],
            send_sem=sems[0].at[q], recv_sem=sems[1].at[q], device_id=(x, y, 1 - c), device_id_type=MESH)
            for q in range(4)]

    def start(self, ins, outs, sems):
        for cp in self._copies(ins, outs, sems):
            cp.start()

    def mid(self, ins, outs, sems):
        pass

    def finish(self, ins, outs, sems):
        for cp in self._copies(ins, outs, sems):
            cp.wait()


class _ChipSumCarry:
    def __init__(self, piece, landed):
        self.inputs = [piece, landed]
        self.r, dm = piece.shape[0] // NDEV, piece.shape[1]
        self.out_shape = [jax.ShapeDtypeStruct((4, self.r, dm), piece.dtype)]
        self.scratch = [pltpu.VMEM((4, self.r, dm), piece.dtype), pltpu.VMEM((2, self.r, dm), piece.dtype),
                        pltpu.SemaphoreType.DMA((2,)), pltpu.SemaphoreType.DMA((3,)), pltpu.SemaphoreType.DMA((3,)),
                        pltpu.SemaphoreType.DMA(())]

    def _copies(self, outs, scr):
        sums, _, _, send_sems, recv_sems, local_sem = scr
        x, y, c = lax.axis_index("x"), lax.axis_index("y"), lax.axis_index("c")
        mine = 2 * x + y
        local = pltpu.make_async_copy(sums.at[mine], outs[0].at[mine], local_sem)
        remote = []
        for k in range(1, 4):
            px = 1 - x if k & 2 else x
            py = 1 - y if k & 1 else y
            remote.append(pltpu.make_async_remote_copy(
                src_ref=sums.at[2 * px + py], dst_ref=outs[0].at[mine], send_sem=send_sems.at[k - 1],
                recv_sem=recv_sems.at[k - 1], device_id=(px, py, c), device_id_type=MESH))
        return local, remote

    def start(self, ins, outs, scr):
        sums, stage, stage_sems = scr[0], scr[1], scr[2]
        c = lax.axis_index("c")
        for q in range(4):
            a = pltpu.make_async_copy(ins[0].at[pl.ds(pl.multiple_of((2 * q + c) * self.r, 16), self.r), :],
                                      stage.at[0], stage_sems.at[0])
            b = pltpu.make_async_copy(ins[1].at[q], stage.at[1], stage_sems.at[1])
            a.start()
            b.start()
            a.wait()
            b.wait()
            sums[q] = (stage[0].astype(F32) + stage[1].astype(F32)).astype(sums.dtype)
        local, remote = self._copies(outs, scr)
        for cp in [local] + remote:
            cp.start()

    def mid(self, ins, outs, scr):
        pass

    def finish(self, ins, outs, scr):
        local, remote = self._copies(outs, scr)
        for cp in remote + [local]:
            cp.wait()


def _call(spec, carry=None):
    body, grid = spec["body"], spec["grid"]
    in_specs, out_specs, out_shape = list(spec["in_specs"]), list(spec["out_specs"]), list(spec["out_shape"])
    scratch, args = list(spec.get("scratch", [])), list(spec["args"])
    if carry is None:
        out = pl.pallas_call(body, grid=grid, in_specs=in_specs, out_specs=tuple(out_specs),
                             out_shape=tuple(out_shape), scratch_shapes=scratch, compiler_params=_cp(len(grid)),
                             name=spec["name"])(*args)
        return tuple(out), ()
    carries = list(carry) if isinstance(carry, (list, tuple)) else [carry]
    n_in, n_out, n_s = len(in_specs), len(out_specs), len(scratch)
    steps = int(np.prod(grid))

    def split(refs, counts):
        parts, o = [], 0
        for cnt in counts:
            parts.append(refs[o:o + cnt])
            o += cnt
        return parts

    c_in = [len(cr.inputs) for cr in carries]
    c_out = [len(cr.out_shape) for cr in carries]
    c_scr = [len(cr.scratch) for cr in carries]

    def wrapped(*refs):
        ins, cins, outs, couts, scr, cscr = split(refs, [n_in, sum(c_in), n_out, sum(c_out), n_s, sum(c_scr)])
        per = list(zip(carries, split(cins, c_in), split(couts, c_out), split(cscr, c_scr)))
        step = pl.program_id(0)
        for ax in range(1, len(grid)):
            step = step * grid[ax] + pl.program_id(ax)

        @pl.when(step == 0)
        def _():
            for cr, ci, co, cs in per:
                cr.start(ci, co, cs)
        if steps >= 3:
            @pl.when(step == steps - 2)
            def _():
                for cr, ci, co, cs in per:
                    cr.mid(ci, co, cs)
        body(*ins, *outs, *scr)

        @pl.when(step == steps - 1)
        def _():
            for cr, ci, co, cs in per:
                if steps < 3:
                    cr.mid(ci, co, cs)
                cr.finish(ci, co, cs)

    out = pl.pallas_call(
        wrapped, grid=grid, in_specs=in_specs + [ANY] * sum(c_in), out_specs=tuple(out_specs + [ANY] * sum(c_out)),
        out_shape=tuple(out_shape + [s for cr in carries for s in cr.out_shape]),
        scratch_shapes=scratch + [s for cr in carries for s in cr.scratch],
        compiler_params=_cp(len(grid)), name=spec["name"])(*args, *[a for cr in carries for a in cr.inputs])
    c_res = [tuple(p) for p in split(out[n_out:], c_out)]
    return tuple(out[:n_out]), (c_res if isinstance(carry, (list, tuple)) else c_res[0])


def _exchange_only(carry, name):
    spec = dict(body=lambda: None, grid=(1,), in_specs=[], out_specs=[], out_shape=[], args=[], name=name)
    return _call(spec, carry)[1]


def _rms_fwd(x, gain):
    r = lax.rsqrt(jnp.mean(x * x, axis=-1, keepdims=True) + EPS)
    return x * r * gain, r


def _rms_bwd(dh, x, r, gain):
    a = dh * gain
    dx = r * a - x * (r * r * r) * jnp.mean(a * x, axis=-1, keepdims=True)
    dgain = jnp.sum(dh * (x * r), axis=0, keepdims=True)
    return dx, dgain


def _gelu(x):
    return 0.5 * x * (1.0 + lax.erf(x * 0.7071067811865476))


def _gelu_grad(x):
    return 0.5 * (1.0 + lax.erf(x * 0.7071067811865476)) + x * jnp.exp(-0.5 * x * x) * 0.3989422804014327


def _sigmoid(x):
    return 1.0 / (1.0 + jnp.exp(-x))


def _adamw_math(w, g, m, v):
    nm = ADAM_B1 * m + (1.0 - ADAM_B1) * g
    nv = ADAM_B2 * v + (1.0 - ADAM_B2) * (g * g)
    m_hat = nm / (1.0 - ADAM_B1 ** ADAM_STEP)
    v_hat = nv / (1.0 - ADAM_B2 ** ADAM_STEP)
    return -ADAM_LR * (m_hat / (jnp.sqrt(v_hat) + ADAM_EPS) + ADAM_WD * w), nm, nv


def _tok(tm, w):
    return pl.BlockSpec((tm, w), lambda i: (i, 0))


def _full(shape):
    return pl.BlockSpec(shape, lambda *i: (0,) * len(shape))


def _norm_proj(x, gain, gath, out_dtype, name, tm):
    t, dm = x.shape
    n = gath.shape[1] * NDEV

    def body(x_ref, g_ref, gath_ref, proj_ref, hb_ref, wbuf, sems):
        @pl.when(pl.program_id(0) == 0)
        def _():
            _load_weight(gath_ref, wbuf, sems)
        h, _ = _rms_fwd(x_ref[...], g_ref[...])
        hb = h.astype(BF16)
        hb_ref[...] = hb
        proj_ref[...] = _dot(hb, wbuf[...], NT).astype(out_dtype)

    return dict(
        body=body, grid=(t // tm,), name=name, args=[x, gain, gath],
        out_shape=[jax.ShapeDtypeStruct((t, n), out_dtype), jax.ShapeDtypeStruct((t, dm), BF16)],
        in_specs=[_tok(tm, dm), _full((1, dm)), ANY], out_specs=[_tok(tm, n), _tok(tm, dm)],
        scratch=[pltpu.VMEM((n, dm), BF16), pltpu.SemaphoreType.DMA((NDEV,))])


def _proj_bwd_norm(dy, x, gain, dres, gath, name, tm):
    t, dm = x.shape
    n = gath.shape[1] * NDEV

    def body(dy_ref, x_ref, g_ref, dres_ref, gath_ref, dx_ref, dxb_ref, dgain_ref, wbuf, sems):
        @pl.when(pl.program_id(0) == 0)
        def _():
            _load_weight(gath_ref, wbuf, sems)
            dgain_ref[...] = jnp.zeros_like(dgain_ref)
        xv, gain_v = x_ref[...], g_ref[...]
        _, r = _rms_fwd(xv, gain_v)
        dh = _dot(dy_ref[...], wbuf[...], NN)
        dx, dgain = _rms_bwd(dh, xv, r, gain_v)
        dx = dres_ref[...] + dx
        dx_ref[...] = dx
        dxb_ref[...] = dx.astype(BF16)
        dgain_ref[...] += dgain

    return dict(
        body=body, grid=(t // tm,), name=name, args=[dy, x, gain, dres, gath],
        out_shape=[jax.ShapeDtypeStruct((t, dm), F32), jax.ShapeDtypeStruct((t, dm), BF16),
                   jax.ShapeDtypeStruct((1, dm), F32)],
        in_specs=[_tok(tm, n), _tok(tm, dm), _full((1, dm)), _tok(tm, dm), ANY],
        out_specs=[_tok(tm, dm), _tok(tm, dm), _full((1, dm))],
        scratch=[pltpu.VMEM((n, dm), BF16), pltpu.SemaphoreType.DMA((NDEV,))])


def _wgrad(a, b, name, tmm=256):
    t, m = a.shape
    n = b.shape[1]

    def body(a_ref, b_ref, o_ref):
        o_ref[...] = _dot(a_ref[...], b_ref[...], TN).astype(BF16)

    return dict(
        body=body, grid=(m // tmm,), name=name, args=[a, b], out_shape=[jax.ShapeDtypeStruct((m, n), BF16)],
        in_specs=[pl.BlockSpec((t, tmm), lambda j: (0, j)), pl.BlockSpec((t, n), lambda j: (0, 0))],
        out_specs=[pl.BlockSpec((tmm, n), lambda j: (j, 0))])


def _halo_specs(tm, t, width, col_blocks):
    nb8 = tm // 8
    last = t // 8 - 1
    prev = [pl.BlockSpec((8, width), lambda i, cb=cb: (jnp.maximum(i * nb8 - 1, 0), cb)) for cb in col_blocks]
    nxt = [pl.BlockSpec((8, width), lambda i, cb=cb: (jnp.minimum((i + 1) * nb8, last), cb)) for cb in col_blocks]
    return prev, nxt


def _shift_rows(z, prev_row, next_row):
    tm = z.shape[0]
    row = lax.broadcasted_iota(jnp.int32, z.shape, 0)
    zm1 = jnp.where(row == 0, prev_row, pltpu.roll(z, 1, 0))
    zp1 = jnp.where(row == tm - 1, next_row, pltpu.roll(z, tm - 1, 0))
    return zm1, zp1


def _gating_fwd(proj, lng, lnb, wsp_ref, bsp_ref, aw):
    tm = proj.shape[0]
    a_u = _gelu(proj[:, 0:aw])
    gv = _gelu(proj[:, aw:2 * aw])
    mu = jnp.mean(gv, axis=-1, keepdims=True)
    xc = gv - mu
    rstd = lax.rsqrt(jnp.mean(xc * xc, axis=-1, keepdims=True) + EPS)
    vn = xc * rstd
    a_v = (vn * lng + lnb).astype(BF16)
    gd = aw // A_GROUPS
    rows = []
    for c in range(tm // CHUNK):
        cols = []
        for g in range(A_GROUPS):
            blk = a_v[c * CHUNK:(c + 1) * CHUNK, g * gd:(g + 1) * gd]
            cols.append(_dot(wsp_ref[g], blk, NN) + bsp_ref[g])
        rows.append(jnp.concatenate(cols, axis=1))
    mixed = jnp.concatenate(rows, axis=0)
    return a_u, vn, rstd, a_v, mixed


def _even_core_fwd(proj, x0, lng, lnb, wsp, bspb, cw, gath, tm):
    t, dm = x0.shape
    aw = lng.shape[1]
    bw = cw.shape[1]
    assert aw == bw and 2 * aw + 3 * bw == proj.shape[1]
    nt = t // tm
    prev, nxt = _halo_specs(tm, t, bw, [3, 4])

    def body(proj_ref, cp_ref, hp_ref, cn_ref, hn_ref, x0_ref, lng_ref, lnb_ref, wsp_ref, bsp_ref, cw_ref, gath_ref,
             x1_ref, y_ref, wbuf, sems):
        i = pl.program_id(0)

        @pl.when(i == 0)
        def _():
            _load_weight(gath_ref, wbuf, sems)
        proj_v = proj_ref[...]
        a_u, _, _, _, mixed = _gating_fwd(proj_v, lng_ref[...], lnb_ref[...], wsp_ref, bsp_ref, aw)
        a_out = a_u * mixed
        bb = proj_v[:, 2 * aw:2 * aw + bw]
        z = proj_v[:, 2 * aw + bw:2 * aw + 2 * bw] * proj_v[:, 2 * aw + 2 * bw:]
        zprev = jnp.where(i > 0, cp_ref[7:8, :] * hp_ref[7:8, :], 0.0)
        znext = jnp.where(i < nt - 1, cn_ref[0:1, :] * hn_ref[0:1, :], 0.0)
        zm1, zp1 = _shift_rows(z, zprev, znext)
        cwv = cw_ref[...]
        conv = zm1 * cwv[0:1, :] + z * cwv[1:2, :] + zp1 * cwv[2:3, :]
        y = jnp.concatenate([a_out, bb * conv], axis=1).astype(BF16)
        y_ref[...] = y
        x1_ref[...] = x0_ref[...] + _dot(y, wbuf[...], NN)

    return dict(
        body=body, grid=(nt,), name="even_core_fwd",
        args=[proj, proj, proj, proj, proj, x0, lng, lnb, wsp, bspb, cw, gath],
        out_shape=[jax.ShapeDtypeStruct((t, dm), F32), jax.ShapeDtypeStruct((t, aw + bw), BF16)],
        in_specs=[_tok(tm, proj.shape[1]), prev[0], prev[1], nxt[0], nxt[1], _tok(tm, dm), _full(lng.shape),
                  _full(lnb.shape), _full(wsp.shape), _full(bspb.shape), _full(cw.shape), ANY],
        out_specs=[_tok(tm, dm), _tok(tm, aw + bw)],
        scratch=[pltpu.VMEM((gath.shape[1] * NDEV, dm), BF16), pltpu.SemaphoreType.DMA((NDEV,))])


def _even_core_bwd(proj, dx1, lng, lnb, wsp, bspb, cw, gath, tm):
    t, dm = dx1.shape
    aw, bw = lng.shape[1], cw.shape[1]
    gd = aw // A_GROUPS
    nt = t // tm
    inw = proj.shape[1]
    prev, nxt = _halo_specs(tm, t, bw, [2, 3, 4])
    nb8 = tm // 8
    last8 = t // 8 - 1

    def body(proj_ref, bp_ref, cp_ref, hp_ref, bn_ref, cn_ref, hn_ref, dx_ref, dxp_ref, dxn_ref,
             lng_ref, lnb_ref, wsp_ref, bsp_ref, cw_ref, gath_ref,
             dproj_ref, dlng_ref, dlnb_ref, dwsp_ref, dbsp_ref, dcw_ref, wbuf, sems):
        i = pl.program_id(0)

        @pl.when(i == 0)
        def _():
            _load_weight(gath_ref, wbuf, sems)
            dlng_ref[...] = jnp.zeros_like(dlng_ref)
            dlnb_ref[...] = jnp.zeros_like(dlnb_ref)
            dwsp_ref[...] = jnp.zeros_like(dwsp_ref)
            dbsp_ref[...] = jnp.zeros_like(dbsp_ref)
            dcw_ref[...] = jnp.zeros_like(dcw_ref)
        proj_v = proj_ref[...]
        lng_v = lng_ref[...]
        a_u, vn, rstd, a_v, mixed = _gating_fwd(proj_v, lng_v, lnb_ref[...], wsp_ref, bsp_ref, aw)
        w = wbuf[...]
        dy = _dot(dx_ref[...].astype(BF16), w, NT)
        da_out, db_out = dy[:, 0:aw], dy[:, aw:]
        da_u = da_out * mixed
        dmixed = da_out * a_u
        dmb = dmixed.astype(BF16)
        rows = []
        for c in range(tm // CHUNK):
            cols = []
            for g in range(A_GROUPS):
                r0, c0 = c * CHUNK, g * gd
                dm_cg = dmb[r0:r0 + CHUNK, c0:c0 + gd]
                cols.append(_dot(wsp_ref[g], dm_cg, TN))
                dwsp_ref[g] += _dot(dm_cg, a_v[r0:r0 + CHUNK, c0:c0 + gd], NT)
                dbsp_ref[g] += dmixed[r0:r0 + CHUNK, c0:c0 + gd]
            rows.append(jnp.concatenate(cols, axis=1))
        dav = jnp.concatenate(rows, axis=0)
        dlng_ref[...] += jnp.sum(dav * vn, axis=0, keepdims=True)
        dlnb_ref[...] += jnp.sum(dav, axis=0, keepdims=True)
        dvn = dav * lng_v
        dgv = rstd * (dvn - jnp.mean(dvn, axis=-1, keepdims=True) - vn * jnp.mean(dvn * vn, axis=-1, keepdims=True))
        dv_pre = dgv * _gelu_grad(proj_v[:, aw:2 * aw])
        du_pre = da_u * _gelu_grad(proj_v[:, 0:aw])
        bb = proj_v[:, 2 * aw:2 * aw + bw]
        bc = proj_v[:, 2 * aw + bw:2 * aw + 2 * bw]
        bh = proj_v[:, 2 * aw + 2 * bw:]
        z = bc * bh
        zprev = jnp.where(i > 0, cp_ref[7:8, :] * hp_ref[7:8, :], 0.0)
        znext = jnp.where(i < nt - 1, cn_ref[0:1, :] * hn_ref[0:1, :], 0.0)
        zm1, zp1 = _shift_rows(z, zprev, znext)
        cwv = cw_ref[...]
        conv = zm1 * cwv[0:1, :] + z * cwv[1:2, :] + zp1 * cwv[2:3, :]
        dbb = db_out * conv
        dconv = db_out * bb
        dx_edge = jnp.concatenate([dxp_ref[...], dxn_ref[...]], axis=0).astype(BF16)
        dy_edge = _dot(dx_edge, w[aw:, :], NT)
        dcprev = jnp.where(i > 0, dy_edge[7:8, :] * bp_ref[7:8, :], 0.0)
        dcnext = jnp.where(i < nt - 1, dy_edge[8:9, :] * bn_ref[0:1, :], 0.0)
        dcm1, dcp1 = _shift_rows(dconv, dcprev, dcnext)
        dz = dcp1 * cwv[0:1, :] + dconv * cwv[1:2, :] + dcm1 * cwv[2:3, :]
        dcw_ref[0:1, :] += jnp.sum(dconv * zm1, axis=0, keepdims=True)
        dcw_ref[1:2, :] += jnp.sum(dconv * z, axis=0, keepdims=True)
        dcw_ref[2:3, :] += jnp.sum(dconv * zp1, axis=0, keepdims=True)
        dproj_ref[...] = jnp.concatenate([du_pre, dv_pre, dbb, dz * bh, dz * bc], axis=1).astype(BF16)

    row8 = lambda f: pl.BlockSpec((8, dm), f)
    return dict(
        body=body, grid=(nt,), name="even_core_bwd",
        args=[proj, proj, proj, proj, proj, proj, proj, dx1, dx1, dx1, lng, lnb, wsp, bspb, cw, gath],
        out_shape=[jax.ShapeDtypeStruct((t, inw), BF16), jax.ShapeDtypeStruct((1, aw), F32),
                   jax.ShapeDtypeStruct((1, aw), F32), jax.ShapeDtypeStruct(wsp.shape, F32),
                   jax.ShapeDtypeStruct((A_GROUPS, CHUNK, gd), F32), jax.ShapeDtypeStruct(cw.shape, F32)],
        in_specs=[_tok(tm, inw), prev[0], prev[1], prev[2], nxt[0], nxt[1], nxt[2], _tok(tm, dm),
                  row8(lambda i: (jnp.maximum(i * nb8 - 1, 0), 0)), row8(lambda i: (jnp.minimum((i + 1) * nb8, last8), 0)),
                  _full(lng.shape), _full(lnb.shape), _full(wsp.shape), _full(bspb.shape), _full(cw.shape), ANY],
        out_specs=[_tok(tm, inw), _full((1, aw)), _full((1, aw)), _full(wsp.shape),
                   _full((A_GROUPS, CHUNK, gd)), _full(cw.shape)],
        scratch=[pltpu.VMEM((gath.shape[1] * NDEV, dm), BF16), pltpu.SemaphoreType.DMA((NDEV,))])


def _ff_chunks(f, width=1024):
    return [(c0, min(c0 + width, f)) for c0 in range(0, f, width)]


def _ffn_up(x, gain, gath_g, gath_u, name, tm):
    t, dm = x.shape
    f = gath_g.shape[1] * NDEV

    def body(x_ref, g_ref, gg_ref, gu_ref, gate_ref, up_ref, act_ref, wg, wu, sems):
        @pl.when(pl.program_id(0) == 0)
        def _():
            _load_weight(gg_ref, wg, sems)
            _load_weight(gu_ref, wu, sems)
        h, _ = _rms_fwd(x_ref[...], g_ref[...])
        hb = h.astype(BF16)
        for c0, c1 in _ff_chunks(f):
            gate = _dot(hb, wg[c0:c1, :], NT)
            up = _dot(hb, wu[c0:c1, :], NT)
            gate_ref[:, c0:c1] = gate.astype(BF16)
            up_ref[:, c0:c1] = up.astype(BF16)
            act_ref[:, c0:c1] = (gate * _sigmoid(gate) * up).astype(BF16)

    o = jax.ShapeDtypeStruct((t, f), BF16)
    return dict(
        body=body, grid=(t // tm,), name=name, args=[x, gain, gath_g, gath_u], out_shape=[o, o, o],
        in_specs=[_tok(tm, dm), _full((1, dm)), ANY, ANY], out_specs=[_tok(tm, f)] * 3,
        scratch=[pltpu.VMEM((f, dm), BF16), pltpu.VMEM((f, dm), BF16), pltpu.SemaphoreType.DMA((NDEV,))])


def _ffn_down(x, act, gath_d, name, tm):
    t, dm = x.shape
    f = act.shape[1]

    def body(x_ref, a_ref, gd_ref, xo_ref, wd, sems):
        @pl.when(pl.program_id(0) == 0)
        def _():
            _load_weight(gd_ref, wd, sems)
        xo_ref[...] = x_ref[...] + _dot(a_ref[...], wd[...], NN)

    return dict(
        body=body, grid=(t // tm,), name=name, args=[x, act, gath_d], out_shape=[jax.ShapeDtypeStruct((t, dm), F32)],
        in_specs=[_tok(tm, dm), _tok(tm, f), ANY], out_specs=[_tok(tm, dm)],
        scratch=[pltpu.VMEM((f, dm), BF16), pltpu.SemaphoreType.DMA((NDEV,))])


def _ffn_bwd(dxo, x, gate, up, gain, gath_g, gath_u, gath_d, name, tm):
    t, dm = x.shape
    f = gate.shape[1]

    def body(dxo_ref, x_ref, gate_ref, up_ref, g_ref, gg_ref, gu_ref, gd_ref,
             dx_ref, dxb_ref, dg_ref, du_ref, hb_ref, dgain_ref, wg, wu, wd, sems):
        @pl.when(pl.program_id(0) == 0)
        def _():
            _load_weight(gg_ref, wg, sems)
            _load_weight(gu_ref, wu, sems)
            _load_weight(gd_ref, wd, sems)
            dgain_ref[...] = jnp.zeros_like(dgain_ref)
        xv, gain_v, dxo_v = x_ref[...], g_ref[...], dxo_ref[...]
        h, r = _rms_fwd(xv, gain_v)
        hb_ref[...] = h.astype(BF16)
        dxob = dxo_v.astype(BF16)
        dh = jnp.zeros_like(xv)
        for c0, c1 in _ff_chunks(f):
            gate_v = gate_ref[:, c0:c1].astype(F32)
            up_v = up_ref[:, c0:c1].astype(F32)
            s = _sigmoid(gate_v)
            silu = gate_v * s
            dact = _dot(dxob, wd[c0:c1, :], NT)
            dg = (dact * up_v * (s * (1.0 + gate_v * (1.0 - s)))).astype(BF16)
            du = (dact * silu).astype(BF16)
            dg_ref[:, c0:c1] = dg
            du_ref[:, c0:c1] = du
            dh = dh + _dot(dg, wg[c0:c1, :], NN) + _dot(du, wu[c0:c1, :], NN)
        dx, dgain = _rms_bwd(dh, xv, r, gain_v)
        dx = dxo_v + dx
        dx_ref[...] = dx
        dxb_ref[...] = dx.astype(BF16)
        dgain_ref[...] += dgain

    return dict(
        body=body, grid=(t // tm,), name=name, args=[dxo, x, gate, up, gain, gath_g, gath_u, gath_d],
        out_shape=[jax.ShapeDtypeStruct((t, dm), F32), jax.ShapeDtypeStruct((t, dm), BF16),
                   jax.ShapeDtypeStruct((t, f), BF16), jax.ShapeDtypeStruct((t, f), BF16),
                   jax.ShapeDtypeStruct((t, dm), BF16), jax.ShapeDtypeStruct((1, dm), F32)],
        in_specs=[_tok(tm, dm), _tok(tm, dm), _tok(tm, f), _tok(tm, f), _full((1, dm)), ANY, ANY, ANY],
        out_specs=[_tok(tm, dm), _tok(tm, dm), _tok(tm, f), _tok(tm, f), _tok(tm, dm), _full((1, dm))],
        scratch=[pltpu.VMEM((f, dm), BF16), pltpu.VMEM((f, dm), BF16), pltpu.VMEM((f, dm), BF16),
                 pltpu.SemaphoreType.DMA((NDEV,))])


def _t5_buckets(rel):
    nb = N_BUCKETS // 2
    ret = jnp.where(rel > 0, nb, 0)
    n = jnp.abs(rel)
    max_exact = nb // 2
    nf = jnp.maximum(n, 1).astype(jnp.float32)
    large = max_exact + (jnp.log(nf / max_exact) / math.log(MAX_DISTANCE / max_exact)
                         * (nb - max_exact)).astype(jnp.int32)
    large = jnp.minimum(large, nb - 1)
    return ret + jnp.where(n < max_exact, n, large)


def _bucket_table():
    qi = jnp.arange(CHUNK, dtype=jnp.int32)[:, None]
    kj = jnp.arange(3 * CHUNK, dtype=jnp.int32)[None, :]
    rel = kj - CHUNK - qi
    return jnp.where(jnp.abs(rel) <= CHUNK, _t5_buckets(rel), -1)


def _bias_table(rel_bias_t, buckets):
    nh = rel_bias_t.shape[0]

    def body(rb_ref, bk_ref, o_ref):
        bk = bk_ref[...]
        for h in range(nh):
            acc = jnp.where(bk < 0, NEG, 0.0).astype(F32)
            for b in range(N_BUCKETS):
                acc = jnp.where(bk == b, rb_ref[h, b] * LOG2E, acc)
            o_ref[h] = acc

    return pl.pallas_call(
        body, out_shape=jax.ShapeDtypeStruct((nh,) + buckets.shape, F32),
        in_specs=[pl.BlockSpec(memory_space=pltpu.SMEM), pl.BlockSpec(memory_space=pltpu.VMEM)],
        out_specs=pl.BlockSpec(memory_space=pltpu.VMEM), name="bias_table")(rel_bias_t, buckets)


def _rel_bias_grad(dbias, buckets):
    nh = dbias.shape[0]

    def body(db_ref, bk_ref, o_ref):
        bk = bk_ref[...]
        lane = lax.broadcasted_iota(jnp.int32, (1, 128), 1)
        for h in range(nh):
            d = db_ref[h]
            row = jnp.zeros((1, 128), F32)
            for b in range(N_BUCKETS):
                s = jnp.sum(jnp.sum(jnp.where(bk == b, d, 0.0), axis=1, keepdims=True), axis=0, keepdims=True)
                row = jnp.where(lane == b, s, row)
            o_ref[h:h + 1, :] = row

    return pl.pallas_call(
        body, out_shape=jax.ShapeDtypeStruct((nh, 128), F32),
        in_specs=[pl.BlockSpec(memory_space=pltpu.VMEM), pl.BlockSpec(memory_space=pltpu.VMEM)],
        out_specs=pl.BlockSpec(memory_space=pltpu.VMEM), compiler_params=_cp(0), name="rel_bias_grad")(dbias, buckets)


def _half_masks():
    lane = lax.broadcasted_iota(jnp.int32, (CHUNK, 128), 1)
    return lane < HEAD_DIM, lane >= HEAD_DIM


def _kv_low(ref, starts, hk, lo):
    kt = (hk // 2) * 128
    out = []
    for jj in range(3):
        blk = ref[pl.ds(starts[jj], CHUNK), kt:kt + 128]
        if hk % 2 == 1:
            blk = pltpu.roll(blk, HEAD_DIM, 1)
        out.append(jnp.where(lo, blk, jnp.zeros_like(blk)))
    return out


def _stack_heads(tile_a, tile_b):
    return jnp.concatenate([tile_a, pltpu.roll(tile_a, HEAD_DIM, 1), tile_b, pltpu.roll(tile_b, HEAD_DIM, 1)], axis=0)


def _unstack_heads(o4):
    return (o4[0:CHUNK] + pltpu.roll(o4[CHUNK:2 * CHUNK], HEAD_DIM, 1),
            o4[2 * CHUNK:3 * CHUNK] + pltpu.roll(o4[3 * CHUNK:], HEAD_DIM, 1))


ATT_SLAB = 32


def _softmax_slab(s_scr, hk, g, r0, bias_ref, sink_ref, n, nblk):
    scale = HEAD_DIM ** -0.5 * LOG2E
    h = (N_HEADS // N_KV) * hk + g
    s = []
    for jj in range(3):
        sj = (s_scr[hk, jj, pl.ds(g * CHUNK + r0, ATT_SLAB), :] * scale
              + bias_ref[h, pl.ds(r0, ATT_SLAB), jj * CHUNK:(jj + 1) * CHUNK])
        if jj == 0:
            sj = jnp.where(n > 0, sj, NEG)
        if jj == 2:
            sj = jnp.where(n < nblk - 1, sj, NEG)
        s.append(sj)
    sink = sink_ref[h] * LOG2E
    m = jnp.maximum(jnp.max(jnp.maximum(jnp.maximum(s[0], s[1]), s[2]), axis=-1, keepdims=True), sink)
    e = [jnp.exp2(sj - m) for sj in s]
    es = jnp.exp2(sink - m)
    inv = 1.0 / (jnp.sum(e[0] + e[1] + e[2], axis=-1, keepdims=True) + es)
    return [ej * inv for ej in e], es * inv


def _key_block_starts(n, nblk):
    return [pl.multiple_of(jnp.clip(n - 1 + jj, 0, nblk - 1) * CHUNK, CHUNK) for jj in range(3)]


def _attn_fwd(qkv, x2, bias, sink, gath):
    t, dm = x2.shape
    nblk = t // CHUNK
    kvw = N_KV * HEAD_DIM
    kcb, vcb = dm // kvw, dm // kvw + 1

    def body(q_ref, k_ref, v_ref, x2_ref, bias_ref, sink_ref, gath_ref, x3_ref, att_ref, wbuf, s_scr, p_scr, sems):
        n = pl.program_id(0)

        @pl.when(n == 0)
        def _():
            _load_weight(gath_ref, wbuf, sems)
        lo, _ = _half_masks()
        starts = _key_block_starts(n, nblk)
        tiles = []
        for hk in range(N_KV):
            c0 = (2 * hk) * 128
            k_lo = _kv_low(k_ref, starts, hk, lo)
            v_lo = _kv_low(v_ref, starts, hk, lo)
            q4 = _stack_heads(q_ref[:, c0:c0 + 128], q_ref[:, c0 + 128:c0 + 256])
            for jj in range(3):
                s_scr[hk, jj] = _dot(q4, k_lo[jj], NT)
            for g in range(4):
                for r0 in range(0, CHUNK, ATT_SLAB):
                    p, _ = _softmax_slab(s_scr, hk, g, r0, bias_ref, sink_ref, n, nblk)
                    for jj in range(3):
                        p_scr[hk, jj, g * CHUNK + r0:g * CHUNK + r0 + ATT_SLAB, :] = p[jj].astype(BF16)
            o4 = _dot(p_scr[hk, 0], v_lo[0], NN) + _dot(p_scr[hk, 1], v_lo[1], NN) + _dot(p_scr[hk, 2], v_lo[2], NN)
            tiles += list(_unstack_heads(o4))
        att = jnp.concatenate(tiles, axis=1).astype(BF16)
        att_ref[...] = att
        x3_ref[...] = x2_ref[...] + _dot(att, wbuf[...], NN)

    blk = pl.BlockSpec((CHUNK, dm), lambda n: (n, 0))
    return dict(
        body=body, grid=(nblk,), name="attn_fwd", args=[qkv, qkv, qkv, x2, bias, sink, gath],
        out_shape=[jax.ShapeDtypeStruct((t, dm), F32), jax.ShapeDtypeStruct((t, dm), BF16)],
        in_specs=[blk, pl.BlockSpec((t, kvw), lambda n: (0, kcb)), pl.BlockSpec((t, kvw), lambda n: (0, vcb)), blk,
                  _full(bias.shape), pl.BlockSpec(memory_space=pltpu.SMEM), ANY],
        out_specs=[blk, blk],
        scratch=[pltpu.VMEM((gath.shape[1] * NDEV, dm), BF16), pltpu.VMEM((N_KV, 3, 4 * CHUNK, CHUNK), F32),
                 pltpu.VMEM((N_KV, 3, 4 * CHUNK, CHUNK), BF16), pltpu.SemaphoreType.DMA((NDEV,))])


def _attn_bwd(qkv, att, dx3, bias, sink, gath):
    t, dm = dx3.shape
    nblk = t // CHUNK
    kvw = N_KV * HEAD_DIM
    kcb, vcb = dm // kvw, dm // kvw + 1
    scale = HEAD_DIM ** -0.5
    slab = (N_KV, 3, 4 * CHUNK, CHUNK)

    def body(q_ref, k_ref, v_ref, att_ref, dx_ref, bias_ref, sink_ref, gath_ref,
             dq_ref, dk_ref, dv_ref, dbias_ref, dsink_ref, wbuf, s_scr, dp_scr, p_scr, ds_scr, prod_scr, sems):
        n = pl.program_id(0)

        @pl.when(n == 0)
        def _():
            _load_weight(gath_ref, wbuf, sems)
            dk_ref[...] = jnp.zeros_like(dk_ref)
            dv_ref[...] = jnp.zeros_like(dv_ref)
            dbias_ref[...] = jnp.zeros_like(dbias_ref)
            dsink_ref[...] = jnp.zeros_like(dsink_ref)
        lo, hi = _half_masks()
        lane1 = lax.broadcasted_iota(jnp.int32, (1, 128), 1)
        starts = _key_block_starts(n, nblk)
        dout = _dot(dx_ref[...].astype(BF16), wbuf[...], NT)
        prod_scr[...] = dout * att_ref[...].astype(F32)
        doutb = dout.astype(BF16)
        dq_tiles = []
        dsink_row = jnp.zeros((1, 128), F32)
        for hk in range(N_KV):
            kt = (hk // 2) * 128
            c0 = (2 * hk) * 128
            k_lo = _kv_low(k_ref, starts, hk, lo)
            v_lo = _kv_low(v_ref, starts, hk, lo)
            q4 = _stack_heads(q_ref[:, c0:c0 + 128], q_ref[:, c0 + 128:c0 + 256])
            do4 = _stack_heads(doutb[:, c0:c0 + 128], doutb[:, c0 + 128:c0 + 256])
            for jj in range(3):
                s_scr[hk, jj] = _dot(q4, k_lo[jj], NT)
                dp_scr[hk, jj] = _dot(do4, v_lo[jj], NT)
            for g in range(4):
                h = 4 * hk + g
                sink_acc = jnp.zeros((1, 1), F32)
                for r0 in range(0, CHUNK, ATT_SLAB):
                    rows = slice(g * CHUNK + r0, g * CHUNK + r0 + ATT_SLAB)
                    p, ps = _softmax_slab(s_scr, hk, g, r0, bias_ref, sink_ref, n, nblk)
                    pt = prod_scr[r0:r0 + ATT_SLAB, c0 + (g // 2) * 128:c0 + (g // 2 + 1) * 128]
                    lane_s = lax.broadcasted_iota(jnp.int32, (ATT_SLAB, 128), 1)
                    msk = lane_s < HEAD_DIM if g % 2 == 0 else lane_s >= HEAD_DIM
                    dsum = jnp.sum(jnp.where(msk, pt, 0.0), axis=-1, keepdims=True)
                    sink_acc = sink_acc + jnp.sum(ps * dsum, axis=0, keepdims=True)
                    for jj in range(3):
                        ds = p[jj] * (dp_scr[hk, jj, rows, :] - dsum)
                        dbias_ref[h, r0:r0 + ATT_SLAB, jj * CHUNK:(jj + 1) * CHUNK] += ds
                        ds_scr[hk, jj, rows, :] = ds.astype(BF16)
                        p_scr[hk, jj, rows, :] = p[jj].astype(BF16)
                dsink_row = dsink_row + jnp.where(lane1 == h, -sink_acc, 0.0)
            dq4 = jnp.zeros((4 * CHUNK, 128), F32)
            for jj in range(3):
                ds4 = ds_scr[hk, jj]
                dq4 = dq4 + _dot(ds4, k_lo[jj], NN) * scale
                dkj = _dot(ds4, q4, TN) * scale
                dvj = _dot(p_scr[hk, jj], do4, TN)
                if hk % 2 == 1:
                    dkj, dvj = pltpu.roll(dkj, HEAD_DIM, 1), pltpu.roll(dvj, HEAD_DIM, 1)
                keep = lo if hk % 2 == 0 else hi
                dk_ref[pl.ds(starts[jj], CHUNK), kt:kt + 128] += jnp.where(keep, dkj, 0.0)
                dv_ref[pl.ds(starts[jj], CHUNK), kt:kt + 128] += jnp.where(keep, dvj, 0.0)
            dq_tiles += list(_unstack_heads(dq4))
        dq_ref[...] = jnp.concatenate(dq_tiles, axis=1).astype(BF16)
        dsink_ref[...] += dsink_row

    blk = pl.BlockSpec((CHUNK, dm), lambda n: (n, 0))
    return dict(
        body=body, grid=(nblk,), name="attn_bwd", args=[qkv, qkv, qkv, att, dx3, bias, sink, gath],
        out_shape=[jax.ShapeDtypeStruct((t, dm), BF16), jax.ShapeDtypeStruct((t, kvw), F32),
                   jax.ShapeDtypeStruct((t, kvw), F32), jax.ShapeDtypeStruct(bias.shape, F32),
                   jax.ShapeDtypeStruct((1, 128), F32)],
        in_specs=[blk, pl.BlockSpec((t, kvw), lambda n: (0, kcb)), pl.BlockSpec((t, kvw), lambda n: (0, vcb)),
                  blk, blk, _full(bias.shape), pl.BlockSpec(memory_space=pltpu.SMEM), ANY],
        out_specs=[blk, _full((t, kvw)), _full((t, kvw)), _full(bias.shape), _full((1, 128))],
        scratch=[pltpu.VMEM((gath.shape[1] * NDEV, dm), BF16),
                 pltpu.VMEM(slab, F32), pltpu.VMEM(slab, F32), pltpu.VMEM(slab, BF16), pltpu.VMEM(slab, BF16),
                 pltpu.VMEM((CHUNK, dm), F32), pltpu.SemaphoreType.DMA((NDEV,))])


def _final_loss(x4, target, gain, tm):
    t, dm = x4.shape
    steps = t // tm

    def body(x_ref, t_ref, g_ref, loss_ref, dx_ref, dxb_ref, dgain_ref, acc):
        i = pl.program_id(0)

        @pl.when(i == 0)
        def _():
            acc[...] = jnp.zeros_like(acc)
            dgain_ref[...] = jnp.zeros_like(dgain_ref)
        xv, gain_v = x_ref[...], g_ref[...]
        y, r = _rms_fwd(xv, gain_v)
        e = y - t_ref[...]
        acc[...] += jnp.sum(e * e, axis=0, keepdims=True)
        dx, dgain = _rms_bwd(e * (1.0 / dm), xv, r, gain_v)
        dx_ref[...] = dx
        dxb_ref[...] = dx.astype(BF16)
        dgain_ref[...] += dgain

        @pl.when(i == steps - 1)
        def _():
            loss_ref[...] = jnp.sum(acc[...], axis=-1, keepdims=True) * (0.5 / dm)

    return dict(
        body=body, grid=(steps,), name="final_loss", args=[x4, target, gain],
        out_shape=[jax.ShapeDtypeStruct((1, 1), F32), jax.ShapeDtypeStruct((t, dm), F32),
                   jax.ShapeDtypeStruct((t, dm), BF16), jax.ShapeDtypeStruct((1, dm), F32)],
        in_specs=[_tok(tm, dm), _tok(tm, dm), _full((1, dm))],
        out_specs=[_full((1, 1)), _tok(tm, dm), _tok(tm, dm), _full((1, dm))],
        scratch=[pltpu.VMEM((1, dm), F32)])


def _finish_weight(recvs, w, m, v, name):
    nl, r, dm = w.shape
    assert nl == len(recvs) and all(rc.shape[1:] == (r, dm) for rc in recvs)
    td = dm // 2
    wspec = pl.BlockSpec((None, r, td), lambda l, j: (l, 0, j))

    def body(*refs):
        r_refs = refs[:nl]
        w_ref, m_ref, v_ref, g_ref, d_ref, nm_ref, nv_ref = refs[nl:]
        layer = pl.program_id(0)
        for li in range(nl):
            @pl.when(layer == li)
            def _():
                g = r_refs[li][0].astype(F32)
                for d in range(1, recvs[li].shape[0]):
                    g = g + r_refs[li][d].astype(F32)
                delta, nm, nv = _adamw_math(w_ref[...], g, m_ref[...], v_ref[...])
                g_ref[...] = g
                d_ref[...] = delta
                nm_ref[...] = nm
                nv_ref[...] = nv

    o = jax.ShapeDtypeStruct(w.shape, F32)
    return dict(
        body=body, grid=(nl, 2), name=name, args=[*recvs, w, m, v], out_shape=[o, o, o, o],
        in_specs=[pl.BlockSpec((rc.shape[0], r, td), lambda l, j: (0, 0, j)) for rc in recvs] + [wspec] * 3,
        out_specs=[wspec] * 4)


def _adamw_small(w, g_slots, late_slots, m, v, name):
    r, c = w.shape
    nlate = late_slots.shape[1]

    def body(w_ref, g_ref, late_ref, m_ref, v_ref, gs_ref, d_ref, nm_ref, nv_ref):
        g = g_ref[0]
        late = late_ref[0]
        for d in range(1, NDEV):
            g = g + g_ref[d]
            late = late + late_ref[d]
        gs_ref[...] = g
        gs_ref[0:nlate, :] = late
        d_ref[...], nm_ref[...], nv_ref[...] = _adamw_math(w_ref[...], gs_ref[...], m_ref[...], v_ref[...])

    spec = pl.BlockSpec((r, c), lambda i: (0, 0))
    out = jax.ShapeDtypeStruct((r, c), F32)
    return pl.pallas_call(
        body, grid=(1,), out_shape=(out,) * 4,
        in_specs=[spec, pl.BlockSpec((NDEV, r, c), lambda i: (0, 0, 0)),
                  pl.BlockSpec((NDEV, nlate, c), lambda i: (0, 0, 0)), spec, spec], out_specs=(spec,) * 4,
        compiler_params=_cp(), name=name)(w, g_slots, late_slots, m, v)


def _pack_small(parts, rows):
    flat = jnp.concatenate([p.reshape(-1) for p in parts])
    return jnp.pad(flat, (0, rows * 128 - flat.shape[0])).reshape(rows, 128)


def _unpack_small(packed, shapes):
    flat = packed.reshape(-1)
    out, o = [], 0
    for s in shapes:
        n = int(np.prod(s))
        out.append(flat[o:o + n].reshape(s))
        o += n
    return out


def kernel(x, norm_mix, norm_ffn, even_w_in, even_v_ln_g, even_v_ln_b, even_w_spatial, even_b_spatial, even_conv_w, even_w_out, attn_w_qkv, attn_sink, rel_bias, attn_w_out, ffn_w_gate, ffn_w_up, ffn_w_down, final_norm, loss_target, m_norm_mix, m_norm_ffn, m_even_w_in, m_even_v_ln_g, m_even_v_ln_b, m_even_w_spatial, m_even_b_spatial, m_even_conv_w, m_even_w_out, m_attn_w_qkv, m_attn_sink, m_rel_bias, m_attn_w_out, m_ffn_w_gate, m_ffn_w_up, m_ffn_w_down, m_final_norm, v_norm_mix, v_norm_ffn, v_even_w_in, v_even_v_ln_g, v_even_v_ln_b, v_even_w_spatial, v_even_b_spatial, v_even_conv_w, v_even_w_out, v_attn_w_qkv, v_attn_sink, v_rel_bias, v_attn_w_out, v_ffn_w_gate, v_ffn_w_up, v_ffn_w_down, v_final_norm):
    t, dm = x.shape[1], x.shape[2]
    aw = even_v_ln_g.shape[1]
    bw = even_conv_w.shape[2] * NDEV
    gd = aw // A_GROUPS
    tm = min(512, t // 2)
    tmf = min(256, t // 2)
    me = _my_index()
    row = lambda a: a.reshape(1, -1)

    colT = lambda w: w.T.astype(BF16)
    sh = dict(winT=colT(even_w_in[0]), wqkvT=colT(attn_w_qkv[0]), wgT0=colT(ffn_w_gate[0]), wuT0=colT(ffn_w_up[0]),
              wgT1=colT(ffn_w_gate[1]), wuT1=colT(ffn_w_up[1]), woe=even_w_out[0].astype(BF16),
              woa=attn_w_out[0].astype(BF16), wd0=ffn_w_down[0].astype(BF16), wd1=ffn_w_down[1].astype(BF16))
    gather = lambda names: _GatherCarry([sh[n] for n in names])

    in_full = lambda a: lax.dynamic_update_slice(jnp.zeros((3, bw), F32), a[0], (0, me * (bw // NDEV)))
    cw_rows = 3 * bw // 128
    cw_mine = jnp.pad(in_full(even_conv_w).reshape(cw_rows, 128), ((0, 16 - cw_rows), (0, 0)))

    x0 = x[0]
    wsp_b = even_w_spatial[0].astype(BF16)
    bspb = jnp.broadcast_to(even_b_spatial[0][:, :, None], (A_GROUPS, CHUNK, gd))
    buckets = _bucket_table()
    bias = _bias_table(rel_bias.T, buckets)
    sink = attn_sink[0]

    (g_winT,), (cw_slots,) = _exchange_only([gather(["winT"]), _BroadcastCarry(cw_mine)], "ag_w_in")
    cw_full = jnp.sum(cw_slots, axis=0)[0:cw_rows].reshape(3, bw)
    (proj, h0b), (g_woe, g_wgT0) = _call(_norm_proj(x0, row(norm_mix[0]), g_winT, F32, "in_proj", tm),
                                         gather(["woe", "wgT0"]))
    (x1, yb), (g_wuT0,) = _call(_even_core_fwd(proj, x0, even_v_ln_g, even_v_ln_b, wsp_b, bspb, cw_full, g_woe, tm),
                                gather(["wuT0"]))
    (gate0, up0, act0), (g_wd0,) = _call(_ffn_up(x1, row(norm_ffn[0]), g_wgT0, g_wuT0, "ffn_up0", tmf), gather(["wd0"]))
    (x2,), (g_wqkvT,) = _call(_ffn_down(x1, act0, g_wd0, "ffn_down0", tm), gather(["wqkvT"]))
    (qkv, h2b), (g_woa,) = _call(_norm_proj(x2, row(norm_mix[1]), g_wqkvT, BF16, "qkv_proj", tm), gather(["woa"]))
    (x3, attb), (g_wgT1, g_wuT1) = _call(_attn_fwd(qkv, x2, bias, sink, g_woa), gather(["wgT1", "wuT1"]))
    (gate1, up1, act1), (g_wd1,) = _call(_ffn_up(x3, row(norm_ffn[1]), g_wgT1, g_wuT1, "ffn_up1", tmf), gather(["wd1"]))
    (x4,), _ = _call(_ffn_down(x3, act1, g_wd1, "ffn_down1", tm))
    (loss_part, dx4, dx4b, d_final), _ = _call(_final_loss(x4, loss_target[0], row(final_norm), tm))

    (dx3, dx3b, dg1, du1, h3b, d_nffn1), _ = _call(
        _ffn_bwd(dx4, x3, gate1, up1, row(norm_ffn[1]), g_wgT1, g_wuT1, g_wd1, "ffn_bwd1", tmf))
    (p_wgT1,), _ = _call(_wgrad(dg1, h3b, "wgrad_gate1"))
    (p_wuT1,), _ = _call(_wgrad(du1, h3b, "wgrad_up1"))
    (p_wd1,), _ = _call(_wgrad(act1, dx4b, "wgrad_down1"))
    (dq, dk, dv, dbias, dsink), (r_wgT1, r_wuT1, r_wd1) = _call(
        _attn_bwd(qkv, attb, dx3, bias, sink, g_woa), _GradCarry([p_wgT1, p_wuT1, p_wd1]))
    (p_woa,), _ = _call(_wgrad(attb, dx3b, "wgrad_attn_out"))
    d_relb = _rel_bias_grad(dbias, buckets)[:, 0:N_BUCKETS].T
    dqkv = jnp.concatenate([dq, dk.astype(BF16), dv.astype(BF16)], axis=1)
    (dx2, dx2b, d_nmix1), _ = _call(_proj_bwd_norm(dqkv, x2, row(norm_mix[1]), dx3, g_wqkvT, "qkv_bwd", tm))
    (p_wqkvT,), _ = _call(_wgrad(dqkv, h2b, "wgrad_qkv"))
    (dx1, dx1b, dg0, du0, h1b, d_nffn0), (r_woa, r_wqkvT) = _call(
        _ffn_bwd(dx2, x1, gate0, up0, row(norm_ffn[0]), g_wgT0, g_wuT0, g_wd0, "ffn_bwd0", tmf),
        _GradCarry([p_woa, p_wqkvT]))
    (p_wgT0,), _ = _call(_wgrad(dg0, h1b, "wgrad_gate0"))
    (p_wuT0,), (a_wgT0,) = _call(_wgrad(du0, h1b, "wgrad_up0"), _PairCarry(p_wgT0))
    (dproj, d_lng, d_lnb, d_wsp, d_bsp3, d_cw), ((r_wgT0,), (a_wuT0,)) = _call(
        _even_core_bwd(proj, dx1, even_v_ln_g, even_v_ln_b, wsp_b, bspb, cw_full, g_woe, tm),
        [_ChipSumCarry(p_wgT0, a_wgT0), _PairCarry(p_wuT0)])
    (p_winT,), (r_wuT0,) = _call(_wgrad(dproj, h0b, "wgrad_in"), _ChipSumCarry(p_wuT0, a_wuT0))
    small_shapes = [(2, dm), (2, dm), (1, aw), (1, aw), (1, A_GROUPS, CHUNK, CHUNK), (1, A_GROUPS, CHUNK), (3, bw),
                    (1, N_HEADS), (N_BUCKETS, N_HEADS), (dm,), (1, 1)]
    small_names = ["norm_mix", "norm_ffn", "even_v_ln_g", "even_v_ln_b", "even_w_spatial", "even_b_spatial",
                   "even_conv_w", "attn_sink", "rel_bias", "final_norm", "loss"]
    n_small = sum(int(np.prod(s)) for s in small_shapes)
    small_rows = 8 * ((n_small + 1023) // 1024)
    assert dm == 8 * 128
    small_part = _pack_small(
        [jnp.concatenate([jnp.zeros_like(d_nmix1), d_nmix1]), jnp.concatenate([d_nffn0, d_nffn1]), d_lng, d_lnb,
         d_wsp, jnp.sum(d_bsp3, axis=-1), d_cw, dsink[:, 0:N_HEADS], d_relb, d_final, loss_part], small_rows)
    (p_wd0,), ((a_winT,), (small_slots,)) = _call(
        _wgrad(act0, dx2b, "wgrad_down0"), [_PairCarry(p_winT), _BroadcastCarry(small_part)])
    (p_woe,), ((r_winT,), (a_wd0,)) = _call(
        _wgrad(yb, dx1b, "wgrad_even_out"), [_ChipSumCarry(p_winT, a_winT), _PairCarry(p_wd0)])
    (dx0, _, d_nmix0), ((r_wd0,), (r_woe,)) = _call(
        _proj_bwd_norm(dproj, x0, row(norm_mix[0]), dx1, g_winT, "in_proj_bwd", tm),
        [_ChipSumCarry(p_wd0, a_wd0), _GradCarry([p_woe])])

    grads = {}

    order = ["norm_mix", "norm_ffn", "even_w_in", "even_v_ln_g", "even_v_ln_b", "even_w_spatial", "even_b_spatial",
             "even_conv_w", "even_w_out", "attn_w_qkv", "attn_sink", "rel_bias", "attn_w_out", "ffn_w_gate",
             "ffn_w_up", "ffn_w_down", "final_norm"]
    ws = dict(norm_mix=norm_mix, norm_ffn=norm_ffn, even_w_in=even_w_in, even_v_ln_g=even_v_ln_g,
              even_v_ln_b=even_v_ln_b, even_w_spatial=even_w_spatial, even_b_spatial=even_b_spatial,
              even_conv_w=even_conv_w, even_w_out=even_w_out, attn_w_qkv=attn_w_qkv, attn_sink=attn_sink,
              rel_bias=rel_bias, attn_w_out=attn_w_out, ffn_w_gate=ffn_w_gate, ffn_w_up=ffn_w_up,
              ffn_w_down=ffn_w_down, final_norm=final_norm)
    ms = dict(norm_mix=m_norm_mix, norm_ffn=m_norm_ffn, even_w_in=m_even_w_in, even_v_ln_g=m_even_v_ln_g,
              even_v_ln_b=m_even_v_ln_b, even_w_spatial=m_even_w_spatial, even_b_spatial=m_even_b_spatial,
              even_conv_w=m_even_conv_w, even_w_out=m_even_w_out, attn_w_qkv=m_attn_w_qkv, attn_sink=m_attn_sink,
              rel_bias=m_rel_bias, attn_w_out=m_attn_w_out, ffn_w_gate=m_ffn_w_gate, ffn_w_up=m_ffn_w_up,
              ffn_w_down=m_ffn_w_down, final_norm=m_final_norm)
    vs = dict(norm_mix=v_norm_mix, norm_ffn=v_norm_ffn, even_w_in=v_even_w_in, even_v_ln_g=v_even_v_ln_g,
              even_v_ln_b=v_even_v_ln_b, even_w_spatial=v_even_w_spatial, even_b_spatial=v_even_b_spatial,
              even_conv_w=v_even_conv_w, even_w_out=v_even_w_out, attn_w_qkv=v_attn_w_qkv, attn_sink=v_attn_sink,
              rel_bias=v_rel_bias, attn_w_out=v_attn_w_out, ffn_w_gate=v_ffn_w_gate, ffn_w_up=v_ffn_w_up,
              ffn_w_down=v_ffn_w_down, final_norm=v_final_norm)
    big = dict(ffn_w_gate=([r_wgT0, r_wgT1], True), even_w_in=([r_winT], True), even_w_out=([r_woe], False),
               attn_w_qkv=([r_wqkvT], True), attn_w_out=([r_woa], False), ffn_w_up=([r_wuT0, r_wuT1], True),
               ffn_w_down=([r_wd0, r_wd1], False))
    delta, new_m, new_v = {}, {}, {}
    late_slots = None
    for n, (recvs, transposed) in big.items():
        lay = (lambda a: jnp.swapaxes(a, 1, 2)) if transposed else (lambda a: a)
        spec = _finish_weight(recvs, lay(ws[n]), lay(ms[n]), lay(vs[n]), "finish_" + n)
        if late_slots is None:
            outs, (late_slots,) = _call(spec, _BroadcastCarry(d_nmix0.reshape(8, 128)))
        else:
            outs, _ = _call(spec)
        grads[n], delta[n], new_m[n], new_v[n] = [lay(o) for o in outs]
    pk = lambda dct: _pack_small(
        [in_full(dct[n]) if n == "even_conv_w" else (jnp.zeros((1, 1), F32) if n == "loss" else dct[n])
         for n in small_names], small_rows)
    packed = _adamw_small(pk(ws), small_slots, late_slots, pk(ms), pk(vs), "adamw_small")
    mine = lambda a: lax.dynamic_slice(a, (0, me * (bw // NDEV)), (3, bw // NDEV))[None]
    for dst, arr in zip((grads, delta, new_m, new_v), packed):
        for n, a in zip(small_names, _unpack_small(arr, small_shapes)):
            dst[n] = mine(a) if n == "even_conv_w" else a
    loss = grads["loss"][0, 0]
    return (loss, dx0[None], *[grads[n] for n in order], *[delta[n] for n in order],
            *[new_m[n] for n in order], *[new_v[n] for n in order])
```

```python
import math

import jax
import jax.numpy as jnp
import numpy as np
from jax import lax
from jax.experimental import pallas as pl
from jax.experimental.pallas import tpu as pltpu

F32, BF16 = jnp.float32, jnp.bfloat16
NDEV = 8
EPS = 1e-6
CHUNK = 128
A_GROUPS = 4
N_HEADS, N_KV, HEAD_DIM = 16, 4, 64
N_BUCKETS, MAX_DISTANCE = 32, 128
NEG = -1e30
LOG2E = 1.4426950408889634
ADAM_LR, ADAM_B1, ADAM_B2, ADAM_EPS, ADAM_WD, ADAM_STEP = 0.001, 0.9, 0.999, 1e-08, 0.01, 10
VMEM_LIMIT = 56 * 1024 * 1024
MESH = pl.DeviceIdType.MESH
NT = (((1,), (1,)), ((), ()))
NN = (((1,), (0,)), ((), ()))
TN = (((0,), (0,)), ((), ()))
ANY = pl.BlockSpec(memory_space=pl.ANY)


def _cp(n_grid=1):
    return pltpu.CompilerParams(dimension_semantics=("arbitrary",) * n_grid, vmem_limit_bytes=VMEM_LIMIT)


def _dot(a, b, dims):
    return lax.dot_general(a, b, dims, preferred_element_type=F32)


def _my_index():
    return 4 * lax.axis_index("x") + 2 * lax.axis_index("y") + lax.axis_index("c")


def _peer(k):
    x, y, c = lax.axis_index("x"), lax.axis_index("y"), lax.axis_index("c")
    px = 1 - x if k & 4 else x
    py = 1 - y if k & 2 else y
    pc = 1 - c if k & 1 else c
    return (px, py, pc)


def _load_weight(gath_ref, wbuf, sems):
    rows = gath_ref.shape[1]
    cps = [pltpu.make_async_copy(gath_ref.at[d], wbuf.at[pl.ds(d * rows, rows), :], sems.at[d]) for d in range(NDEV)]
    for c in cps:
        c.start()
    for c in cps:
        c.wait()


class _GatherCarry:
    def __init__(self, pieces):
        self.inputs = list(pieces)
        self.n = len(pieces)
        self.out_shape = [jax.ShapeDtypeStruct((NDEV,) + p.shape, p.dtype) for p in pieces]
        self.scratch = [pltpu.SemaphoreType.DMA((7 * self.n,)), pltpu.SemaphoreType.DMA((7 * self.n,)),
                        pltpu.SemaphoreType.DMA((self.n,))]

    def _ctx(self):
        x, y, c = lax.axis_index("x"), lax.axis_index("y"), lax.axis_index("c")
        chips = [(1 - x, y), (x, 1 - y), (1 - x, 1 - y)]
        return (x, y, c), (x, y, 1 - c), chips, c

    def _copy(self, k, j, block, to, ins, outs, sems, src=None):
        send_sems, recv_sems, _ = sems
        slot = outs[j].at[4 * block[0] + 2 * block[1] + block[2]]
        return pltpu.make_async_remote_copy(
            src_ref=slot if src is None else src, dst_ref=slot, send_sem=send_sems.at[k * self.n + j],
            recv_sem=recv_sems.at[k * self.n + j], device_id=to, device_id_type=MESH)

    def start(self, ins, outs, sems):
        me, sibling, chips, c = self._ctx()
        for j in range(self.n):
            pltpu.make_async_copy(ins[j], outs[j].at[4 * me[0] + 2 * me[1] + me[2]], sems[2].at[j]).start()
            self._copy(0, j, me, sibling, ins, outs, sems, src=ins[j]).start()
            for q, chip in enumerate(chips):
                self._copy(1 + q, j, me, (*chip, c), ins, outs, sems, src=ins[j]).start()

    def mid(self, ins, outs, sems):
        me, sibling, chips, c = self._ctx()
        for q, chip in enumerate(chips):
            for j in range(self.n):
                self._copy(1 + q, j, (*chip, c), me, ins, outs, sems).wait_recv()
                self._copy(4 + q, j, (*chip, c), sibling, ins, outs, sems).start()

    def finish(self, ins, outs, sems):
        me, sibling, chips, c = self._ctx()
        for j in range(self.n):
            self._copy(0, j, sibling, me, ins, outs, sems).wait_recv()
            for q, chip in enumerate(chips):
                self._copy(4 + q, j, (*chip, 1 - c), me, ins, outs, sems).wait_recv()
        for j in range(self.n):
            self._copy(0, j, me, sibling, ins, outs, sems, src=ins[j]).wait_send()
            for q, chip in enumerate(chips):
                self._copy(1 + q, j, me, (*chip, c), ins, outs, sems, src=ins[j]).wait_send()
                self._copy(4 + q, j, (*chip, c), sibling, ins, outs, sems).wait_send()
            pltpu.make_async_copy(ins[j], outs[j].at[0], sems[2].at[j]).wait()


class _GradCarry:
    def __init__(self, pieces):
        self.inputs = list(pieces)
        self.n = len(pieces)
        self.rows = [p.shape[0] // NDEV for p in pieces]
        self.out_shape = [jax.ShapeDtypeStruct((NDEV, r, p.shape[1]), p.dtype) for p, r in zip(pieces, self.rows)]
        self.scratch = [pltpu.SemaphoreType.DMA((7 * self.n,)), pltpu.SemaphoreType.DMA((7 * self.n,)),
                        pltpu.SemaphoreType.DMA((self.n,))]

    def _copies(self, ins, outs, sems):
        me = _my_index()
        local, remote = [], []
        for j in range(self.n):
            r = self.rows[j]
            local.append(pltpu.make_async_copy(ins[j].at[pl.ds(pl.multiple_of(me * r, 16), r), :], outs[j].at[me],
                                               sems[2].at[j]))
            for k in range(1, NDEV):
                peer = _peer(k)
                pidx = 4 * peer[0] + 2 * peer[1] + peer[2]
                remote.append(pltpu.make_async_remote_copy(
                    src_ref=ins[j].at[pl.ds(pl.multiple_of(pidx * r, 16), r), :], dst_ref=outs[j].at[me],
                    send_sem=sems[0].at[(k - 1) * self.n + j], recv_sem=sems[1].at[(k - 1) * self.n + j],
                    device_id=peer, device_id_type=MESH))
        return local, remote

    def start(self, ins, outs, sems):
        local, remote = self._copies(ins, outs, sems)
        for cp in local + remote:
            cp.start()

    def mid(self, ins, outs, sems):
        pass

    def finish(self, ins, outs, sems):
        local, remote = self._copies(ins, outs, sems)
        for cp in remote + local:
            cp.wait()


class _BroadcastCarry:
    def __init__(self, part):
        self.inputs = [part]
        self.out_shape = [jax.ShapeDtypeStruct((NDEV,) + part.shape, part.dtype)]
        self.scratch = [pltpu.SemaphoreType.DMA((7,)), pltpu.SemaphoreType.DMA((7,)), pltpu.SemaphoreType.DMA(())]

    def _copies(self, ins, outs, sems):
        me = _my_index()
        local = pltpu.make_async_copy(ins[0], outs[0].at[me], sems[2])
        remote = [pltpu.make_async_remote_copy(
            src_ref=ins[0], dst_ref=outs[0].at[me], send_sem=sems[0].at[k - 1], recv_sem=sems[1].at[k - 1],
            device_id=_peer(k), device_id_type=MESH) for k in range(1, NDEV)]
        return [local] + remote

    def start(self, ins, outs, sems):
        for cp in self._copies(ins, outs, sems):
            cp.start()

    def mid(self, ins, outs, sems):
        pass

    def finish(self, ins, outs, sems):
        for cp in self._copies(ins, outs, sems):
            cp.wait()


class _PairCarry:
    def __init__(self, piece):
        self.inputs = [piece]
        self.r = piece.shape[0] // NDEV
        self.out_shape = [jax.ShapeDtypeStruct((4, self.r, piece.shape[1]), piece.dtype)]
        self.scratch = [pltpu.SemaphoreType.DMA((4,)), pltpu.SemaphoreType.DMA((4,))]

    def _copies(self, ins, outs, sems):
        x, y, c = lax.axis_index("x"), lax.axis_index("y"), lax.axis_index("c")
        return [pltpu.make_async_remote_copy(
            src_ref=ins[0].at[pl.ds(pl.multiple_of((2 * q + 1 - c) * self.r, 16), self.r), :], dst_ref=outs[0].at[q],
            send_sem=sems[0].at[q], recv_sem=sems[1].at[q], device_id=(x, y, 1 - c), device_id_type=MESH)
            for q in range(4)]

    def start(self, ins, outs, sems):
        for cp in self._copies(ins, outs, sems):
            cp.start()

    def mid(self, ins, outs, sems):
        pass

    def finish(self, ins, outs, sems):
        for cp in self._copies(ins, outs, sems):
            cp.wait()


class _ChipSumCarry:
    def __init__(self, piece, landed):
        self.inputs = [piece, landed]
        self.r, dm = piece.shape[0] // NDEV, piece.shape[1]
        self.out_shape = [jax.ShapeDtypeStruct((4, self.r, dm), piece.dtype)]
        self.scratch = [pltpu.VMEM((4, self.r, dm), piece.dtype), pltpu.VMEM((8, self.r, dm), piece.dtype),
                        pltpu.SemaphoreType.DMA((8,)), pltpu.SemaphoreType.DMA((3,)), pltpu.SemaphoreType.DMA((3,)),
                        pltpu.SemaphoreType.DMA(())]

    def _copies(self, outs, scr):
        sums, _, _, send_sems, recv_sems, local_sem = scr
        x, y, c = lax.axis_index("x"), lax.axis_index("y"), lax.axis_index("c")
        mine = 2 * x + y
        local = pltpu.make_async_copy(sums.at[mine], outs[0].at[mine], local_sem)
        remote = []
        for k in range(1, 4):
            px = 1 - x if k & 2 else x
            py = 1 - y if k & 1 else y
            remote.append(pltpu.make_async_remote_copy(
                src_ref=sums.at[2 * px + py], dst_ref=outs[0].at[mine], send_sem=send_sems.at[k - 1],
                recv_sem=recv_sems.at[k - 1], device_id=(px, py, c), device_id_type=MESH))
        return local, remote

    def start(self, ins, outs, scr):
        sums, stage, stage_sems = scr[0], scr[1], scr[2]
        c = lax.axis_index("c")
        loads = []
        for q in range(4):
            loads.append((
                pltpu.make_async_copy(ins[0].at[pl.ds(pl.multiple_of((2 * q + c) * self.r, 16), self.r), :],
                                      stage.at[2 * q], stage_sems.at[2 * q]),
                pltpu.make_async_copy(ins[1].at[q], stage.at[2 * q + 1], stage_sems.at[2 * q + 1])))
        for a, b in loads:
            a.start()
            b.start()
        for q, (a, b) in enumerate(loads):
            a.wait()
            b.wait()
            sums[q] = (stage[2 * q].astype(F32) + stage[2 * q + 1].astype(F32)).astype(sums.dtype)
        local, remote = self._copies(outs, scr)
        for cp in [local] + remote:
            cp.start()

    def mid(self, ins, outs, scr):
        pass

    def finish(self, ins, outs, scr):
        local, remote = self._copies(outs, scr)
        for cp in remote + [local]:
            cp.wait()


def _call(spec, carry=None):
    body, grid = spec["body"], spec["grid"]
    in_specs, out_specs, out_shape = list(spec["in_specs"]), list(spec["out_specs"]), list(spec["out_shape"])
    scratch, args = list(spec.get("scratch", [])), list(spec["args"])
    if carry is None:
        out = pl.pallas_call(body, grid=grid, in_specs=in_specs, out_specs=tuple(out_specs),
                             out_shape=tuple(out_shape), scratch_shapes=scratch, compiler_params=_cp(len(grid)),
                             name=spec["name"])(*args)
        return tuple(out), ()
    carries = list(carry) if isinstance(carry, (list, tuple)) else [carry]
    n_in, n_out, n_s = len(in_specs), len(out_specs), len(scratch)
    steps = int(np.prod(grid))

    def split(refs, counts):
        parts, o = [], 0
        for cnt in counts:
            parts.append(refs[o:o + cnt])
            o += cnt
        return parts

    c_in = [len(cr.inputs) for cr in carries]
    c_out = [len(cr.out_shape) for cr in carries]
    c_scr = [len(cr.scratch) for cr in carries]

    def wrapped(*refs):
        ins, cins, outs, couts, scr, cscr = split(refs, [n_in, sum(c_in), n_out, sum(c_out), n_s, sum(c_scr)])
        per = list(zip(carries, split(cins, c_in), split(couts, c_out), split(cscr, c_scr)))
        step = pl.program_id(0)
        for ax in range(1, len(grid)):
            step = step * grid[ax] + pl.program_id(ax)

        @pl.when(step == 0)
        def _():
            for cr, ci, co, cs in per:
                cr.start(ci, co, cs)
        if steps >= 3:
            @pl.when(step == steps - 2)
            def _():
                for cr, ci, co, cs in per:
                    cr.mid(ci, co, cs)
        body(*ins, *outs, *scr)

        @pl.when(step == steps - 1)
        def _():
            for cr, ci, co, cs in per:
                if steps < 3:
                    cr.mid(ci, co, cs)
                cr.finish(ci, co, cs)

    out = pl.pallas_call(
        wrapped, grid=grid, in_specs=in_specs + [ANY] * sum(c_in), out_specs=tuple(out_specs + [ANY] * sum(c_out)),
        out_shape=tuple(out_shape + [s for cr in carries for s in cr.out_shape]),
        scratch_shapes=scratch + [s for cr in carries for s in cr.scratch],
        compiler_params=_cp(len(grid)), name=spec["name"])(*args, *[a for cr in carries for a in cr.inputs])
    c_res = [tuple(p) for p in split(out[n_out:], c_out)]
    return tuple(out[:n_out]), (c_res if isinstance(carry, (list, tuple)) else c_res[0])


def _exchange_only(carry, name):
    spec = dict(body=lambda: None, grid=(1,), in_specs=[], out_specs=[], out_shape=[], args=[], name=name)
    return _call(spec, carry)[1]


def _rms_fwd(x, gain):
    r = lax.rsqrt(jnp.mean(x * x, axis=-1, keepdims=True) + EPS)
    return x * r * gain, r


def _rms_bwd(dh, x, r, gain):
    a = dh * gain
    dx = r * a - x * (r * r * r) * jnp.mean(a * x, axis=-1, keepdims=True)
    dgain = jnp.sum(dh * (x * r), axis=0, keepdims=True)
    return dx, dgain


def _gelu(x):
    return 0.5 * x * (1.0 + lax.erf(x * 0.7071067811865476))


def _gelu_grad(x):
    return 0.5 * (1.0 + lax.erf(x * 0.7071067811865476)) + x * jnp.exp(-0.5 * x * x) * 0.3989422804014327


def _sigmoid(x):
    return 1.0 / (1.0 + jnp.exp(-x))


def _adamw_math(w, g, m, v):
    nm = ADAM_B1 * m + (1.0 - ADAM_B1) * g
    nv = ADAM_B2 * v + (1.0 - ADAM_B2) * (g * g)
    m_hat = nm / (1.0 - ADAM_B1 ** ADAM_STEP)
    v_hat = nv / (1.0 - ADAM_B2 ** ADAM_STEP)
    return -ADAM_LR * (m_hat / (jnp.sqrt(v_hat) + ADAM_EPS) + ADAM_WD * w), nm, nv


def _tok(tm, w):
    return pl.BlockSpec((tm, w), lambda i: (i, 0))


def _full(shape):
    return pl.BlockSpec(shape, lambda *i: (0,) * len(shape))


def _norm_proj(x, gain, gath, out_dtype, name, tm):
    t, dm = x.shape
    n = gath.shape[1] * NDEV

    def body(x_ref, g_ref, gath_ref, proj_ref, hb_ref, wbuf, sems):
        @pl.when(pl.program_id(0) == 0)
        def _():
            _load_weight(gath_ref, wbuf, sems)
        h, _ = _rms_fwd(x_ref[...], g_ref[...])
        hb = h.astype(BF16)
        hb_ref[...] = hb
        proj_ref[...] = _dot(hb, wbuf[...], NT).astype(out_dtype)

    return dict(
        body=body, grid=(t // tm,), name=name, args=[x, gain, gath],
        out_shape=[jax.ShapeDtypeStruct((t, n), out_dtype), jax.ShapeDtypeStruct((t, dm), BF16)],
        in_specs=[_tok(tm, dm), _full((1, dm)), ANY], out_specs=[_tok(tm, n), _tok(tm, dm)],
        scratch=[pltpu.VMEM((n, dm), BF16), pltpu.SemaphoreType.DMA((NDEV,))])


def _proj_bwd_norm(dy, x, gain, dres, gath, name, tm):
    t, dm = x.shape
    n = gath.shape[1] * NDEV

    def body(dy_ref, x_ref, g_ref, dres_ref, gath_ref, dx_ref, dxb_ref, dgain_ref, wbuf, sems):
        @pl.when(pl.program_id(0) == 0)
        def _():
            _load_weight(gath_ref, wbuf, sems)
            dgain_ref[...] = jnp.zeros_like(dgain_ref)
        xv, gain_v = x_ref[...], g_ref[...]
        _, r = _rms_fwd(xv, gain_v)
        dh = _dot(dy_ref[...], wbuf[...], NN)
        dx, dgain = _rms_bwd(dh, xv, r, gain_v)
        dx = dres_ref[...] + dx
        dx_ref[...] = dx
        dxb_ref[...] = dx.astype(BF16)
        dgain_ref[...] += dgain

    return dict(
        body=body, grid=(t // tm,), name=name, args=[dy, x, gain, dres, gath],
        out_shape=[jax.ShapeDtypeStruct((t, dm), F32), jax.ShapeDtypeStruct((t, dm), BF16),
                   jax.ShapeDtypeStruct((1, dm), F32)],
        in_specs=[_tok(tm, n), _tok(tm, dm), _full((1, dm)), _tok(tm, dm), ANY],
        out_specs=[_tok(tm, dm), _tok(tm, dm), _full((1, dm))],
        scratch=[pltpu.VMEM((n, dm), BF16), pltpu.SemaphoreType.DMA((NDEV,))])


def _wgrad(a, b, name, tmm=256):
    t, m = a.shape
    n = b.shape[1]

    def body(a_ref, b_ref, o_ref):
        o_ref[...] = _dot(a_ref[...], b_ref[...], TN).astype(BF16)

    return dict(
        body=body, grid=(m // tmm,), name=name, args=[a, b], out_shape=[jax.ShapeDtypeStruct((m, n), BF16)],
        in_specs=[pl.BlockSpec((t, tmm), lambda j: (0, j)), pl.BlockSpec((t, n), lambda j: (0, 0))],
        out_specs=[pl.BlockSpec((tmm, n), lambda j: (j, 0))])


def _halo_specs(tm, t, width, col_blocks):
    nb8 = tm // 8
    last = t // 8 - 1
    prev = [pl.BlockSpec((8, width), lambda i, cb=cb: (jnp.maximum(i * nb8 - 1, 0), cb)) for cb in col_blocks]
    nxt = [pl.BlockSpec((8, width), lambda i, cb=cb: (jnp.minimum((i + 1) * nb8, last), cb)) for cb in col_blocks]
    return prev, nxt


def _shift_rows(z, prev_row, next_row):
    tm = z.shape[0]
    row = lax.broadcasted_iota(jnp.int32, z.shape, 0)
    zm1 = jnp.where(row == 0, prev_row, pltpu.roll(z, 1, 0))
    zp1 = jnp.where(row == tm - 1, next_row, pltpu.roll(z, tm - 1, 0))
    return zm1, zp1


def _gating_fwd(proj, lng, lnb, wsp_ref, bsp_ref, aw):
    tm = proj.shape[0]
    a_u = _gelu(proj[:, 0:aw])
    gv = _gelu(proj[:, aw:2 * aw])
    mu = jnp.mean(gv, axis=-1, keepdims=True)
    xc = gv - mu
    rstd = lax.rsqrt(jnp.mean(xc * xc, axis=-1, keepdims=True) + EPS)
    vn = xc * rstd
    a_v = (vn * lng + lnb).astype(BF16)
    gd = aw // A_GROUPS
    rows = []
    for c in range(tm // CHUNK):
        cols = []
        for g in range(A_GROUPS):
            blk = a_v[c * CHUNK:(c + 1) * CHUNK, g * gd:(g + 1) * gd]
            cols.append(_dot(wsp_ref[g], blk, NN) + bsp_ref[g])
        rows.append(jnp.concatenate(cols, axis=1))
    mixed = jnp.concatenate(rows, axis=0)
    return a_u, vn, rstd, a_v, mixed


def _even_core_fwd(proj, x0, lng, lnb, wsp, bspb, cw, gath, tm):
    t, dm = x0.shape
    aw = lng.shape[1]
    bw = cw.shape[1]
    assert aw == bw and 2 * aw + 3 * bw == proj.shape[1]
    nt = t // tm
    prev, nxt = _halo_specs(tm, t, bw, [3, 4])

    def body(proj_ref, cp_ref, hp_ref, cn_ref, hn_ref, x0_ref, lng_ref, lnb_ref, wsp_ref, bsp_ref, cw_ref, gath_ref,
             x1_ref, y_ref, wbuf, sems):
        i = pl.program_id(0)

        @pl.when(i == 0)
        def _():
            _load_weight(gath_ref, wbuf, sems)
        proj_v = proj_ref[...]
        a_u, _, _, _, mixed = _gating_fwd(proj_v, lng_ref[...], lnb_ref[...], wsp_ref, bsp_ref, aw)
        a_out = a_u * mixed
        bb = proj_v[:, 2 * aw:2 * aw + bw]
        z = proj_v[:, 2 * aw + bw:2 * aw + 2 * bw] * proj_v[:, 2 * aw + 2 * bw:]
        zprev = jnp.where(i > 0, cp_ref[7:8, :] * hp_ref[7:8, :], 0.0)
        znext = jnp.where(i < nt - 1, cn_ref[0:1, :] * hn_ref[0:1, :], 0.0)
        zm1, zp1 = _shift_rows(z, zprev, znext)
        cwv = cw_ref[...]
        conv = zm1 * cwv[0:1, :] + z * cwv[1:2, :] + zp1 * cwv[2:3, :]
        y = jnp.concatenate([a_out, bb * conv], axis=1).astype(BF16)
        y_ref[...] = y
        x1_ref[...] = x0_ref[...] + _dot(y, wbuf[...], NN)

    return dict(
        body=body, grid=(nt,), name="even_core_fwd",
        args=[proj, proj, proj, proj, proj, x0, lng, lnb, wsp, bspb, cw, gath],
        out_shape=[jax.ShapeDtypeStruct((t, dm), F32), jax.ShapeDtypeStruct((t, aw + bw), BF16)],
        in_specs=[_tok(tm, proj.shape[1]), prev[0], prev[1], nxt[0], nxt[1], _tok(tm, dm), _full(lng.shape),
                  _full(lnb.shape), _full(wsp.shape), _full(bspb.shape), _full(cw.shape), ANY],
        out_specs=[_tok(tm, dm), _tok(tm, aw + bw)],
        scratch=[pltpu.VMEM((gath.shape[1] * NDEV, dm), BF16), pltpu.SemaphoreType.DMA((NDEV,))])


def _even_core_bwd(proj, dx1, lng, lnb, wsp, bspb, cw, gath, tm):
    t, dm = dx1.shape
    aw, bw = lng.shape[1], cw.shape[1]
    gd = aw // A_GROUPS
    nt = t // tm
    inw = proj.shape[1]
    prev, nxt = _halo_specs(tm, t, bw, [2, 3, 4])
    nb8 = tm // 8
    last8 = t // 8 - 1

    def body(proj_ref, bp_ref, cp_ref, hp_ref, bn_ref, cn_ref, hn_ref, dx_ref, dxp_ref, dxn_ref,
             lng_ref, lnb_ref, wsp_ref, bsp_ref, cw_ref, gath_ref,
             dproj_ref, dlng_ref, dlnb_ref, dwsp_ref, dbsp_ref, dcw_ref, wbuf, sems):
        i = pl.program_id(0)

        @pl.when(i == 0)
        def _():
            _load_weight(gath_ref, wbuf, sems)
            dlng_ref[...] = jnp.zeros_like(dlng_ref)
            dlnb_ref[...] = jnp.zeros_like(dlnb_ref)
            dwsp_ref[...] = jnp.zeros_like(dwsp_ref)
            dbsp_ref[...] = jnp.zeros_like(dbsp_ref)
            dcw_ref[...] = jnp.zeros_like(dcw_ref)
        proj_v = proj_ref[...]
        lng_v = lng_ref[...]
        a_u, vn, rstd, a_v, mixed = _gating_fwd(proj_v, lng_v, lnb_ref[...], wsp_ref, bsp_ref, aw)
        w = wbuf[...]
        dy = _dot(dx_ref[...].astype(BF16), w, NT)
        da_out, db_out = dy[:, 0:aw], dy[:, aw:]
        da_u = da_out * mixed
        dmixed = da_out * a_u
        dmb = dmixed.astype(BF16)
        rows = []
        for c in range(tm // CHUNK):
            cols = []
            for g in range(A_GROUPS):
                r0, c0 = c * CHUNK, g * gd
                dm_cg = dmb[r0:r0 + CHUNK, c0:c0 + gd]
                cols.append(_dot(wsp_ref[g], dm_cg, TN))
                dwsp_ref[g] += _dot(dm_cg, a_v[r0:r0 + CHUNK, c0:c0 + gd], NT)
                dbsp_ref[g] += dmixed[r0:r0 + CHUNK, c0:c0 + gd]
            rows.append(jnp.concatenate(cols, axis=1))
        dav = jnp.concatenate(rows, axis=0)
        dlng_ref[...] += jnp.sum(dav * vn, axis=0, keepdims=True)
        dlnb_ref[...] += jnp.sum(dav, axis=0, keepdims=True)
        dvn = dav * lng_v
        dgv = rstd * (dvn - jnp.mean(dvn, axis=-1, keepdims=True) - vn * jnp.mean(dvn * vn, axis=-1, keepdims=True))
        dv_pre = dgv * _gelu_grad(proj_v[:, aw:2 * aw])
        du_pre = da_u * _gelu_grad(proj_v[:, 0:aw])
        bb = proj_v[:, 2 * aw:2 * aw + bw]
        bc = proj_v[:, 2 * aw + bw:2 * aw + 2 * bw]
        bh = proj_v[:, 2 * aw + 2 * bw:]
        z = bc * bh
        zprev = jnp.where(i > 0, cp_ref[7:8, :] * hp_ref[7:8, :], 0.0)
        znext = jnp.where(i < nt - 1, cn_ref[0:1, :] * hn_ref[0:1, :], 0.0)
        zm1, zp1 = _shift_rows(z, zprev, znext)
        cwv = cw_ref[...]
        conv = zm1 * cwv[0:1, :] + z * cwv[1:2, :] + zp1 * cwv[2:3, :]
        dbb = db_out * conv
        dconv = db_out * bb
        dx_edge = jnp.concatenate([dxp_ref[...], dxn_ref[...]], axis=0).astype(BF16)
        dy_edge = _dot(dx_edge, w[aw:, :], NT)
        dcprev = jnp.where(i > 0, dy_edge[7:8, :] * bp_ref[7:8, :], 0.0)
        dcnext = jnp.where(i < nt - 1, dy_edge[8:9, :] * bn_ref[0:1, :], 0.0)
        dcm1, dcp1 = _shift_rows(dconv, dcprev, dcnext)
        dz = dcp1 * cwv[0:1, :] + dconv * cwv[1:2, :] + dcm1 * cwv[2:3, :]
        dcw_ref[0:1, :] += jnp.sum(dconv * zm1, axis=0, keepdims=True)
        dcw_ref[1:2, :] += jnp.sum(dconv * z, axis=0, keepdims=True)
        dcw_ref[2:3, :] += jnp.sum(dconv * zp1, axis=0, keepdims=True)
        dproj_ref[...] = jnp.concatenate([du_pre, dv_pre, dbb, dz * bh, dz * bc], axis=1).astype(BF16)

    row8 = lambda f: pl.BlockSpec((8, dm), f)
    return dict(
        body=body, grid=(nt,), name="even_core_bwd",
        args=[proj, proj, proj, proj, proj, proj, proj, dx1, dx1, dx1, lng, lnb, wsp, bspb, cw, gath],
        out_shape=[jax.ShapeDtypeStruct((t, inw), BF16), jax.ShapeDtypeStruct((1, aw), F32),
                   jax.ShapeDtypeStruct((1, aw), F32), jax.ShapeDtypeStruct(wsp.shape, F32),
                   jax.ShapeDtypeStruct((A_GROUPS, CHUNK, gd), F32), jax.ShapeDtypeStruct(cw.shape, F32)],
        in_specs=[_tok(tm, inw), prev[0], prev[1], prev[2], nxt[0], nxt[1], nxt[2], _tok(tm, dm),
                  row8(lambda i: (jnp.maximum(i * nb8 - 1, 0), 0)), row8(lambda i: (jnp.minimum((i + 1) * nb8, last8), 0)),
                  _full(lng.shape), _full(lnb.shape), _full(wsp.shape), _full(bspb.shape), _full(cw.shape), ANY],
        out_specs=[_tok(tm, inw), _full((1, aw)), _full((1, aw)), _full(wsp.shape),
                   _full((A_GROUPS, CHUNK, gd)), _full(cw.shape)],
        scratch=[pltpu.VMEM((gath.shape[1] * NDEV, dm), BF16), pltpu.SemaphoreType.DMA((NDEV,))])


def _ff_chunks(f, width=1024):
    return [(c0, min(c0 + width, f)) for c0 in range(0, f, width)]


def _ffn_up(x, gain, gath_g, gath_u, name, tm):
    t, dm = x.shape
    f = gath_g.shape[1] * NDEV

    def body(x_ref, g_ref, gg_ref, gu_ref, gate_ref, up_ref, act_ref, wg, wu, sems):
        @pl.when(pl.program_id(0) == 0)
        def _():
            _load_weight(gg_ref, wg, sems)
            _load_weight(gu_ref, wu, sems)
        h, _ = _rms_fwd(x_ref[...], g_ref[...])
        hb = h.astype(BF16)
        for c0, c1 in _ff_chunks(f):
            gate = _dot(hb, wg[c0:c1, :], NT)
            up = _dot(hb, wu[c0:c1, :], NT)
            gate_ref[:, c0:c1] = gate.astype(BF16)
            up_ref[:, c0:c1] = up.astype(BF16)
            act_ref[:, c0:c1] = (gate * _sigmoid(gate) * up).astype(BF16)

    o = jax.ShapeDtypeStruct((t, f), BF16)
    return dict(
        body=body, grid=(t // tm,), name=name, args=[x, gain, gath_g, gath_u], out_shape=[o, o, o],
        in_specs=[_tok(tm, dm), _full((1, dm)), ANY, ANY], out_specs=[_tok(tm, f)] * 3,
        scratch=[pltpu.VMEM((f, dm), BF16), pltpu.VMEM((f, dm), BF16), pltpu.SemaphoreType.DMA((NDEV,))])


def _ffn_down(x, act, gath_d, name, tm):
    t, dm = x.shape
    f = act.shape[1]

    def body(x_ref, a_ref, gd_ref, xo_ref, wd, sems):
        @pl.when(pl.program_id(0) == 0)
        def _():
            _load_weight(gd_ref, wd, sems)
        xo_ref[...] = x_ref[...] + _dot(a_ref[...], wd[...], NN)

    return dict(
        body=body, grid=(t // tm,), name=name, args=[x, act, gath_d], out_shape=[jax.ShapeDtypeStruct((t, dm), F32)],
        in_specs=[_tok(tm, dm), _tok(tm, f), ANY], out_specs=[_tok(tm, dm)],
        scratch=[pltpu.VMEM((f, dm), BF16), pltpu.SemaphoreType.DMA((NDEV,))])


def _ffn_down_loss(x, act, gath_d, target, gain, name, tm):
    t, dm = x.shape
    f = act.shape[1]
    steps = t // tm

    def body(x_ref, a_ref, gd_ref, t_ref, g_ref, loss_ref, dx_ref, dxb_ref, dgain_ref, wd, acc, sems):
        i = pl.program_id(0)

        @pl.when(i == 0)
        def _():
            _load_weight(gd_ref, wd, sems)
            acc[...] = jnp.zeros_like(acc)
            dgain_ref[...] = jnp.zeros_like(dgain_ref)
        xv = x_ref[...] + _dot(a_ref[...], wd[...], NN)
        gain_v = g_ref[...]
        y, r = _rms_fwd(xv, gain_v)
        e = y - t_ref[...]
        acc[...] += jnp.sum(e * e, axis=0, keepdims=True)
        dx, dgain = _rms_bwd(e * (1.0 / dm), xv, r, gain_v)
        dx_ref[...] = dx
        dxb_ref[...] = dx.astype(BF16)
        dgain_ref[...] += dgain

        @pl.when(i == steps - 1)
        def _():
            loss_ref[...] = jnp.sum(acc[...], axis=-1, keepdims=True) * (0.5 / dm)

    return dict(
        body=body, grid=(steps,), name=name, args=[x, act, gath_d, target, gain],
        out_shape=[jax.ShapeDtypeStruct((1, 1), F32), jax.ShapeDtypeStruct((t, dm), F32),
                   jax.ShapeDtypeStruct((t, dm), BF16), jax.ShapeDtypeStruct((1, dm), F32)],
        in_specs=[_tok(tm, dm), _tok(tm, f), ANY, _tok(tm, dm), _full((1, dm))],
        out_specs=[_full((1, 1)), _tok(tm, dm), _tok(tm, dm), _full((1, dm))],
        scratch=[pltpu.VMEM((f, dm), BF16), pltpu.VMEM((1, dm), F32), pltpu.SemaphoreType.DMA((NDEV,))])


def _ffn_bwd(dxo, x, gate, up, gain, gath_g, gath_u, gath_d, name, tm):
    t, dm = x.shape
    f = gate.shape[1]

    def body(dxo_ref, x_ref, gate_ref, up_ref, g_ref, gg_ref, gu_ref, gd_ref,
             dx_ref, dxb_ref, dg_ref, du_ref, hb_ref, dgain_ref, wg, wu, wd, sems):
        @pl.when(pl.program_id(0) == 0)
        def _():
            _load_weight(gg_ref, wg, sems)
            _load_weight(gu_ref, wu, sems)
            _load_weight(gd_ref, wd, sems)
            dgain_ref[...] = jnp.zeros_like(dgain_ref)
        xv, gain_v, dxo_v = x_ref[...], g_ref[...], dxo_ref[...]
        h, r = _rms_fwd(xv, gain_v)
        hb_ref[...] = h.astype(BF16)
        dxob = dxo_v.astype(BF16)
        dh = jnp.zeros_like(xv)
        for c0, c1 in _ff_chunks(f):
            gate_v = gate_ref[:, c0:c1].astype(F32)
            up_v = up_ref[:, c0:c1].astype(F32)
            s = _sigmoid(gate_v)
            silu = gate_v * s
            dact = _dot(dxob, wd[c0:c1, :], NT)
            dg = (dact * up_v * (s * (1.0 + gate_v * (1.0 - s)))).astype(BF16)
            du = (dact * silu).astype(BF16)
            dg_ref[:, c0:c1] = dg
            du_ref[:, c0:c1] = du
            dh = dh + _dot(dg, wg[c0:c1, :], NN) + _dot(du, wu[c0:c1, :], NN)
        dx, dgain = _rms_bwd(dh, xv, r, gain_v)
        dx = dxo_v + dx
        dx_ref[...] = dx
        dxb_ref[...] = dx.astype(BF16)
        dgain_ref[...] += dgain

    return dict(
        body=body, grid=(t // tm,), name=name, args=[dxo, x, gate, up, gain, gath_g, gath_u, gath_d],
        out_shape=[jax.ShapeDtypeStruct((t, dm), F32), jax.ShapeDtypeStruct((t, dm), BF16),
                   jax.ShapeDtypeStruct((t, f), BF16), jax.ShapeDtypeStruct((t, f), BF16),
                   jax.ShapeDtypeStruct((t, dm), BF16), jax.ShapeDtypeStruct((1, dm), F32)],
        in_specs=[_tok(tm, dm), _tok(tm, dm), _tok(tm, f), _tok(tm, f), _full((1, dm)), ANY, ANY, ANY],
        out_specs=[_tok(tm, dm), _tok(tm, dm), _tok(tm, f), _tok(tm, f), _tok(tm, dm), _full((1, dm))],
        scratch=[pltpu.VMEM((f, dm), BF16), pltpu.VMEM((f, dm), BF16), pltpu.VMEM((f, dm), BF16),
                 pltpu.SemaphoreType.DMA((NDEV,))])


def _t5_buckets(rel):
    nb = N_BUCKETS // 2
    ret = jnp.where(rel > 0, nb, 0)
    n = jnp.abs(rel)
    max_exact = nb // 2
    nf = jnp.maximum(n, 1).astype(jnp.float32)
    large = max_exact + (jnp.log(nf / max_exact) / math.log(MAX_DISTANCE / max_exact)
                         * (nb - max_exact)).astype(jnp.int32)
    large = jnp.minimum(large, nb - 1)
    return ret + jnp.where(n < max_exact, n, large)


def _bucket_table():
    qi = jnp.arange(CHUNK, dtype=jnp.int32)[:, None]
    kj = jnp.arange(3 * CHUNK, dtype=jnp.int32)[None, :]
    rel = kj - CHUNK - qi
    return jnp.where(jnp.abs(rel) <= CHUNK, _t5_buckets(rel), -1)


def _bias_table(rel_bias_t, buckets):
    nh = rel_bias_t.shape[0]

    def body(rb_ref, bk_ref, o_ref):
        bk = bk_ref[...]
        for h in range(nh):
            acc = jnp.where(bk < 0, NEG, 0.0).astype(F32)
            for b in range(N_BUCKETS):
                acc = jnp.where(bk == b, rb_ref[h, b] * LOG2E, acc)
            o_ref[h] = acc

    return pl.pallas_call(
        body, out_shape=jax.ShapeDtypeStruct((nh,) + buckets.shape, F32),
        in_specs=[pl.BlockSpec(memory_space=pltpu.SMEM), pl.BlockSpec(memory_space=pltpu.VMEM)],
        out_specs=pl.BlockSpec(memory_space=pltpu.VMEM), name="bias_table")(rel_bias_t, buckets)


def _rel_bias_grad(dbias, buckets):
    nh = dbias.shape[0]

    def body(db_ref, bk_ref, o_ref):
        bk = bk_ref[...]
        lane = lax.broadcasted_iota(jnp.int32, (1, 128), 1)
        for h in range(nh):
            d = db_ref[h]
            row = jnp.zeros((1, 128), F32)
            for b in range(N_BUCKETS):
                s = jnp.sum(jnp.sum(jnp.where(bk == b, d, 0.0), axis=1, keepdims=True), axis=0, keepdims=True)
                row = jnp.where(lane == b, s, row)
            o_ref[h:h + 1, :] = row

    return pl.pallas_call(
        body, out_shape=jax.ShapeDtypeStruct((nh, 128), F32),
        in_specs=[pl.BlockSpec(memory_space=pltpu.VMEM), pl.BlockSpec(memory_space=pltpu.VMEM)],
        out_specs=pl.BlockSpec(memory_space=pltpu.VMEM), compiler_params=_cp(0), name="rel_bias_grad")(dbias, buckets)


def _half_masks():
    lane = lax.broadcasted_iota(jnp.int32, (CHUNK, 128), 1)
    return lane < HEAD_DIM, lane >= HEAD_DIM


def _kv_low(ref, starts, hk, lo):
    kt = (hk // 2) * 128
    out = []
    for jj in range(3):
        blk = ref[pl.ds(starts[jj], CHUNK), kt:kt + 128]
        if hk % 2 == 1:
            blk = pltpu.roll(blk, HEAD_DIM, 1)
        out.append(jnp.where(lo, blk, jnp.zeros_like(blk)))
    return out


def _stack_heads(tile_a, tile_b):
    return jnp.concatenate([tile_a, pltpu.roll(tile_a, HEAD_DIM, 1), tile_b, pltpu.roll(tile_b, HEAD_DIM, 1)], axis=0)


def _unstack_heads(o4):
    return (o4[0:CHUNK] + pltpu.roll(o4[CHUNK:2 * CHUNK], HEAD_DIM, 1),
            o4[2 * CHUNK:3 * CHUNK] + pltpu.roll(o4[3 * CHUNK:], HEAD_DIM, 1))


ATT_SLAB = 32


def _softmax_slab(s_scr, hk, g, r0, bias_ref, sink_ref, n, nblk):
    scale = HEAD_DIM ** -0.5 * LOG2E
    h = (N_HEADS // N_KV) * hk + g
    s = []
    for jj in range(3):
        sj = (s_scr[hk, jj, pl.ds(g * CHUNK + r0, ATT_SLAB), :] * scale
              + bias_ref[h, pl.ds(r0, ATT_SLAB), jj * CHUNK:(jj + 1) * CHUNK])
        if jj == 0:
            sj = jnp.where(n > 0, sj, NEG)
        if jj == 2:
            sj = jnp.where(n < nblk - 1, sj, NEG)
        s.append(sj)
    sink = sink_ref[h] * LOG2E
    m = jnp.maximum(jnp.max(jnp.maximum(jnp.maximum(s[0], s[1]), s[2]), axis=-1, keepdims=True), sink)
    e = [jnp.exp2(sj - m) for sj in s]
    es = jnp.exp2(sink - m)
    inv = 1.0 / (jnp.sum(e[0] + e[1] + e[2], axis=-1, keepdims=True) + es)
    return [ej * inv for ej in e], es * inv


def _key_block_starts(n, nblk):
    return [pl.multiple_of(jnp.clip(n - 1 + jj, 0, nblk - 1) * CHUNK, CHUNK) for jj in range(3)]


def _attn_fwd(qkv, x2, bias, sink, gath):
    t, dm = x2.shape
    nblk = t // CHUNK
    kvw = N_KV * HEAD_DIM
    kcb, vcb = dm // kvw, dm // kvw + 1

    def body(q_ref, k_ref, v_ref, x2_ref, bias_ref, sink_ref, gath_ref, x3_ref, att_ref, wbuf, s_scr, p_scr, sems):
        n = pl.program_id(0)

        @pl.when(n == 0)
        def _():
            _load_weight(gath_ref, wbuf, sems)
        lo, _ = _half_masks()
        starts = _key_block_starts(n, nblk)
        tiles = []
        for hk in range(N_KV):
            c0 = (2 * hk) * 128
            k_lo = _kv_low(k_ref, starts, hk, lo)
            v_lo = _kv_low(v_ref, starts, hk, lo)
            q4 = _stack_heads(q_ref[:, c0:c0 + 128], q_ref[:, c0 + 128:c0 + 256])
            for jj in range(3):
                s_scr[hk, jj] = _dot(q4, k_lo[jj], NT)
            for g in range(4):
                for r0 in range(0, CHUNK, ATT_SLAB):
                    p, _ = _softmax_slab(s_scr, hk, g, r0, bias_ref, sink_ref, n, nblk)
                    for jj in range(3):
                        p_scr[hk, jj, g * CHUNK + r0:g * CHUNK + r0 + ATT_SLAB, :] = p[jj].astype(BF16)
            o4 = _dot(p_scr[hk, 0], v_lo[0], NN) + _dot(p_scr[hk, 1], v_lo[1], NN) + _dot(p_scr[hk, 2], v_lo[2], NN)
            tiles += list(_unstack_heads(o4))
        att = jnp.concatenate(tiles, axis=1).astype(BF16)
        att_ref[...] = att
        x3_ref[...] = x2_ref[...] + _dot(att, wbuf[...], NN)

    blk = pl.BlockSpec((CHUNK, dm), lambda n: (n, 0))
    return dict(
        body=body, grid=(nblk,), name="attn_fwd", args=[qkv, qkv, qkv, x2, bias, sink, gath],
        out_shape=[jax.ShapeDtypeStruct((t, dm), F32), jax.ShapeDtypeStruct((t, dm), BF16)],
        in_specs=[blk, pl.BlockSpec((t, kvw), lambda n: (0, kcb)), pl.BlockSpec((t, kvw), lambda n: (0, vcb)), blk,
                  _full(bias.shape), pl.BlockSpec(memory_space=pltpu.SMEM), ANY],
        out_specs=[blk, blk],
        scratch=[pltpu.VMEM((gath.shape[1] * NDEV, dm), BF16), pltpu.VMEM((N_KV, 3, 4 * CHUNK, CHUNK), F32),
                 pltpu.VMEM((N_KV, 3, 4 * CHUNK, CHUNK), BF16), pltpu.SemaphoreType.DMA((NDEV,))])


def _attn_bwd(qkv, att, dx3, bias, sink, gath):
    t, dm = dx3.shape
    nblk = t // CHUNK
    kvw = N_KV * HEAD_DIM
    kcb, vcb = dm // kvw, dm // kvw + 1
    scale = HEAD_DIM ** -0.5
    slab = (N_KV, 3, 4 * CHUNK, CHUNK)

    def body(q_ref, k_ref, v_ref, att_ref, dx_ref, bias_ref, sink_ref, gath_ref,
             dq_ref, dk_ref, dv_ref, dbias_ref, dsink_ref, wbuf, s_scr, dp_scr, p_scr, ds_scr, prod_scr, sems):
        n = pl.program_id(0)

        @pl.when(n == 0)
        def _():
            _load_weight(gath_ref, wbuf, sems)
            dk_ref[...] = jnp.zeros_like(dk_ref)
            dv_ref[...] = jnp.zeros_like(dv_ref)
            dbias_ref[...] = jnp.zeros_like(dbias_ref)
            dsink_ref[...] = jnp.zeros_like(dsink_ref)
        lo, hi = _half_masks()
        lane1 = lax.broadcasted_iota(jnp.int32, (1, 128), 1)
        starts = _key_block_starts(n, nblk)
        dout = _dot(dx_ref[...].astype(BF16), wbuf[...], NT)
        prod_scr[...] = dout * att_ref[...].astype(F32)
        doutb = dout.astype(BF16)
        dq_tiles = []
        dsink_row = jnp.zeros((1, 128), F32)
        for hk in range(N_KV):
            kt = (hk // 2) * 128
            c0 = (2 * hk) * 128
            k_lo = _kv_low(k_ref, starts, hk, lo)
            v_lo = _kv_low(v_ref, starts, hk, lo)
            q4 = _stack_heads(q_ref[:, c0:c0 + 128], q_ref[:, c0 + 128:c0 + 256])
            do4 = _stack_heads(doutb[:, c0:c0 + 128], doutb[:, c0 + 128:c0 + 256])
            for jj in range(3):
                s_scr[hk, jj] = _dot(q4, k_lo[jj], NT)
                dp_scr[hk, jj] = _dot(do4, v_lo[jj], NT)
            for g in range(4):
                h = 4 * hk + g
                sink_acc = jnp.zeros((1, 1), F32)
                for r0 in range(0, CHUNK, ATT_SLAB):
                    rows = slice(g * CHUNK + r0, g * CHUNK + r0 + ATT_SLAB)
                    p, ps = _softmax_slab(s_scr, hk, g, r0, bias_ref, sink_ref, n, nblk)
                    pt = prod_scr[r0:r0 + ATT_SLAB, c0 + (g // 2) * 128:c0 + (g // 2 + 1) * 128]
                    lane_s = lax.broadcasted_iota(jnp.int32, (ATT_SLAB, 128), 1)
                    msk = lane_s < HEAD_DIM if g % 2 == 0 else lane_s >= HEAD_DIM
                    dsum = jnp.sum(jnp.where(msk, pt, 0.0), axis=-1, keepdims=True)
                    sink_acc = sink_acc + jnp.sum(ps * dsum, axis=0, keepdims=True)
                    for jj in range(3):
                        ds = p[jj] * (dp_scr[hk, jj, rows, :] - dsum)
                        dbias_ref[h, r0:r0 + ATT_SLAB, jj * CHUNK:(jj + 1) * CHUNK] += ds
                        ds_scr[hk, jj, rows, :] = ds.astype(BF16)
                        p_scr[hk, jj, rows, :] = p[jj].astype(BF16)
                dsink_row = dsink_row + jnp.where(lane1 == h, -sink_acc, 0.0)
            dq4 = jnp.zeros((4 * CHUNK, 128), F32)
            for jj in range(3):
                ds4 = ds_scr[hk, jj]
                dq4 = dq4 + _dot(ds4, k_lo[jj], NN) * scale
                dkj = _dot(ds4, q4, TN) * scale
                dvj = _dot(p_scr[hk, jj], do4, TN)
                if hk % 2 == 1:
                    dkj, dvj = pltpu.roll(dkj, HEAD_DIM, 1), pltpu.roll(dvj, HEAD_DIM, 1)
                keep = lo if hk % 2 == 0 else hi
                dk_ref[pl.ds(starts[jj], CHUNK), kt:kt + 128] += jnp.where(keep, dkj, 0.0)
                dv_ref[pl.ds(starts[jj], CHUNK), kt:kt + 128] += jnp.where(keep, dvj, 0.0)
            dq_tiles += list(_unstack_heads(dq4))
        dq_ref[...] = jnp.concatenate(dq_tiles, axis=1).astype(BF16)
        dsink_ref[...] += dsink_row

    blk = pl.BlockSpec((CHUNK, dm), lambda n: (n, 0))
    return dict(
        body=body, grid=(nblk,), name="attn_bwd", args=[qkv, qkv, qkv, att, dx3, bias, sink, gath],
        out_shape=[jax.ShapeDtypeStruct((t, dm), BF16), jax.ShapeDtypeStruct((t, kvw), F32),
                   jax.ShapeDtypeStruct((t, kvw), F32), jax.ShapeDtypeStruct(bias.shape, F32),
                   jax.ShapeDtypeStruct((1, 128), F32)],
        in_specs=[blk, pl.BlockSpec((t, kvw), lambda n: (0, kcb)), pl.BlockSpec((t, kvw), lambda n: (0, vcb)),
                  blk, blk, _full(bias.shape), pl.BlockSpec(memory_space=pltpu.SMEM), ANY],
        out_specs=[blk, _full((t, kvw)), _full((t, kvw)), _full(bias.shape), _full((1, 128))],
        scratch=[pltpu.VMEM((gath.shape[1] * NDEV, dm), BF16),
                 pltpu.VMEM(slab, F32), pltpu.VMEM(slab, F32), pltpu.VMEM(slab, BF16), pltpu.VMEM(slab, BF16),
                 pltpu.VMEM((CHUNK, dm), F32), pltpu.SemaphoreType.DMA((NDEV,))])


def _finish_weight(recvs, w, m, v, name):
    nl, r, dm = w.shape
    assert nl == len(recvs) and all(rc.shape[1:] == (r, dm) for rc in recvs)
    td = dm // 2
    wspec = pl.BlockSpec((None, r, td), lambda l, j: (l, 0, j))

    def body(*refs):
        r_refs = refs[:nl]
        w_ref, m_ref, v_ref, g_ref, d_ref, nm_ref, nv_ref = refs[nl:]
        layer = pl.program_id(0)
        for li in range(nl):
            @pl.when(layer == li)
            def _():
                g = r_refs[li][0].astype(F32)
                for d in range(1, recvs[li].shape[0]):
                    g = g + r_refs[li][d].astype(F32)
                delta, nm, nv = _adamw_math(w_ref[...], g, m_ref[...], v_ref[...])
                g_ref[...] = g
                d_ref[...] = delta
                nm_ref[...] = nm
                nv_ref[...] = nv

    o = jax.ShapeDtypeStruct(w.shape, F32)
    return dict(
        body=body, grid=(nl, 2), name=name, args=[*recvs, w, m, v], out_shape=[o, o, o, o],
        in_specs=[pl.BlockSpec((rc.shape[0], r, td), lambda l, j: (0, 0, j)) for rc in recvs] + [wspec] * 3,
        out_specs=[wspec] * 4)


def _adamw_small(w, g_slots, late_slots, m, v, name):
    r, c = w.shape
    nlate = late_slots.shape[1]

    def body(w_ref, g_ref, late_ref, m_ref, v_ref, gs_ref, d_ref, nm_ref, nv_ref):
        g = g_ref[0]
        late = late_ref[0]
        for d in range(1, NDEV):
            g = g + g_ref[d]
            late = late + late_ref[d]
        gs_ref[...] = g
        gs_ref[0:nlate, :] = late
        d_ref[...], nm_ref[...], nv_ref[...] = _adamw_math(w_ref[...], gs_ref[...], m_ref[...], v_ref[...])

    spec = pl.BlockSpec((r, c), lambda i: (0, 0))
    out = jax.ShapeDtypeStruct((r, c), F32)
    return pl.pallas_call(
        body, grid=(1,), out_shape=(out,) * 4,
        in_specs=[spec, pl.BlockSpec((NDEV, r, c), lambda i: (0, 0, 0)),
                  pl.BlockSpec((NDEV, nlate, c), lambda i: (0, 0, 0)), spec, spec], out_specs=(spec,) * 4,
        compiler_params=_cp(), name=name)(w, g_slots, late_slots, m, v)


def _pack_small(parts, rows):
    flat = jnp.concatenate([p.reshape(-1) for p in parts])
    return jnp.pad(flat, (0, rows * 128 - flat.shape[0])).reshape(rows, 128)


def _unpack_small(packed, shapes):
    flat = packed.reshape(-1)
    out, o = [], 0
    for s in shapes:
        n = int(np.prod(s))
        out.append(flat[o:o + n].reshape(s))
        o += n
    return out


def kernel(x, norm_mix, norm_ffn, even_w_in, even_v_ln_g, even_v_ln_b, even_w_spatial, even_b_spatial, even_conv_w, even_w_out, attn_w_qkv, attn_sink, rel_bias, attn_w_out, ffn_w_gate, ffn_w_up, ffn_w_down, final_norm, loss_target, m_norm_mix, m_norm_ffn, m_even_w_in, m_even_v_ln_g, m_even_v_ln_b, m_even_w_spatial, m_even_b_spatial, m_even_conv_w, m_even_w_out, m_attn_w_qkv, m_attn_sink, m_rel_bias, m_attn_w_out, m_ffn_w_gate, m_ffn_w_up, m_ffn_w_down, m_final_norm, v_norm_mix, v_norm_ffn, v_even_w_in, v_even_v_ln_g, v_even_v_ln_b, v_even_w_spatial, v_even_b_spatial, v_even_conv_w, v_even_w_out, v_attn_w_qkv, v_attn_sink, v_rel_bias, v_attn_w_out, v_ffn_w_gate, v_ffn_w_up, v_ffn_w_down, v_final_norm):
    t, dm = x.shape[1], x.shape[2]
    aw = even_v_ln_g.shape[1]
    bw = even_conv_w.shape[2] * NDEV
    gd = aw // A_GROUPS
    tm = min(512, t // 2)
    tmf = min(256, t // 2)
    me = _my_index()
    row = lambda a: a.reshape(1, -1)

    colT = lambda w: w.T.astype(BF16)
    sh = dict(winT=colT(even_w_in[0]), wqkvT=colT(attn_w_qkv[0]), wgT0=colT(ffn_w_gate[0]), wuT0=colT(ffn_w_up[0]),
              wgT1=colT(ffn_w_gate[1]), wuT1=colT(ffn_w_up[1]), woe=even_w_out[0].astype(BF16),
              woa=attn_w_out[0].astype(BF16), wd0=ffn_w_down[0].astype(BF16), wd1=ffn_w_down[1].astype(BF16))
    gather = lambda names: _GatherCarry([sh[n] for n in names])

    in_full = lambda a: lax.dynamic_update_slice(jnp.zeros((3, bw), F32), a[0], (0, me * (bw // NDEV)))
    cw_rows = 3 * bw // 128
    cw_mine = jnp.pad(in_full(even_conv_w).reshape(cw_rows, 128), ((0, 16 - cw_rows), (0, 0)))

    x0 = x[0]
    wsp_b = even_w_spatial[0].astype(BF16)
    bspb = jnp.broadcast_to(even_b_spatial[0][:, :, None], (A_GROUPS, CHUNK, gd))
    buckets = _bucket_table()
    bias = _bias_table(rel_bias.T, buckets)
    sink = attn_sink[0]

    (g_winT,), (cw_slots,) = _exchange_only([gather(["winT"]), _BroadcastCarry(cw_mine)], "ag_w_in")
    cw_full = jnp.sum(cw_slots, axis=0)[0:cw_rows].reshape(3, bw)
    (proj, h0b), (g_woe, g_wgT0) = _call(_norm_proj(x0, row(norm_mix[0]), g_winT, F32, "in_proj", tm),
                                         gather(["woe", "wgT0"]))
    (x1, yb), (g_wuT0,) = _call(_even_core_fwd(proj, x0, even_v_ln_g, even_v_ln_b, wsp_b, bspb, cw_full, g_woe, tm),
                                gather(["wuT0"]))
    (gate0, up0, act0), (g_wd0,) = _call(_ffn_up(x1, row(norm_ffn[0]), g_wgT0, g_wuT0, "ffn_up0", tmf), gather(["wd0"]))
    (x2,), (g_wqkvT,) = _call(_ffn_down(x1, act0, g_wd0, "ffn_down0", tm), gather(["wqkvT"]))
    (qkv, h2b), (g_woa,) = _call(_norm_proj(x2, row(norm_mix[1]), g_wqkvT, BF16, "qkv_proj", tm), gather(["woa"]))
    (x3, attb), (g_wgT1, g_wuT1) = _call(_attn_fwd(qkv, x2, bias, sink, g_woa), gather(["wgT1", "wuT1"]))
    (gate1, up1, act1), (g_wd1,) = _call(_ffn_up(x3, row(norm_ffn[1]), g_wgT1, g_wuT1, "ffn_up1", tmf), gather(["wd1"]))
    (loss_part, dx4, dx4b, d_final), _ = _call(
        _ffn_down_loss(x3, act1, g_wd1, loss_target[0], row(final_norm), "ffn_down1_loss", tm))

    (dx3, dx3b, dg1, du1, h3b, d_nffn1), _ = _call(
        _ffn_bwd(dx4, x3, gate1, up1, row(norm_ffn[1]), g_wgT1, g_wuT1, g_wd1, "ffn_bwd1", tmf))
    (p_wgT1,), _ = _call(_wgrad(dg1, h3b, "wgrad_gate1"))
    (p_wuT1,), _ = _call(_wgrad(du1, h3b, "wgrad_up1"))
    (p_wd1,), _ = _call(_wgrad(act1, dx4b, "wgrad_down1"))
    (dq, dk, dv, dbias, dsink), (r_wgT1, r_wuT1, r_wd1) = _call(
        _attn_bwd(qkv, attb, dx3, bias, sink, g_woa), _GradCarry([p_wgT1, p_wuT1, p_wd1]))
    (p_woa,), _ = _call(_wgrad(attb, dx3b, "wgrad_attn_out"))
    d_relb = _rel_bias_grad(dbias, buckets)[:, 0:N_BUCKETS].T
    dqkv = jnp.concatenate([dq, dk.astype(BF16), dv.astype(BF16)], axis=1)
    (dx2, dx2b, d_nmix1), _ = _call(_proj_bwd_norm(dqkv, x2, row(norm_mix[1]), dx3, g_wqkvT, "qkv_bwd", tm))
    (p_wqkvT,), _ = _call(_wgrad(dqkv, h2b, "wgrad_qkv"))
    (dx1, dx1b, dg0, du0, h1b, d_nffn0), (r_woa, r_wqkvT) = _call(
        _ffn_bwd(dx2, x1, gate0, up0, row(norm_ffn[0]), g_wgT0, g_wuT0, g_wd0, "ffn_bwd0", tmf),
        _GradCarry([p_woa, p_wqkvT]))
    (p_wgT0,), _ = _call(_wgrad(dg0, h1b, "wgrad_gate0"))
    (p_wuT0,), (a_wgT0,) = _call(_wgrad(du0, h1b, "wgrad_up0"), _PairCarry(p_wgT0))
    (dproj, d_lng, d_lnb, d_wsp, d_bsp3, d_cw), ((r_wgT0,), (a_wuT0,)) = _call(
        _even_core_bwd(proj, dx1, even_v_ln_g, even_v_ln_b, wsp_b, bspb, cw_full, g_woe, tm),
        [_ChipSumCarry(p_wgT0, a_wgT0), _PairCarry(p_wuT0)])
    (p_winT,), (r_wuT0,) = _call(_wgrad(dproj, h0b, "wgrad_in"), _ChipSumCarry(p_wuT0, a_wuT0))
    small_shapes = [(2, dm), (2, dm), (1, aw), (1, aw), (1, A_GROUPS, CHUNK, CHUNK), (1, A_GROUPS, CHUNK), (3, bw),
                    (1, N_HEADS), (N_BUCKETS, N_HEADS), (dm,), (1, 1)]
    small_names = ["norm_mix", "norm_ffn", "even_v_ln_g", "even_v_ln_b", "even_w_spatial", "even_b_spatial",
                   "even_conv_w", "attn_sink", "rel_bias", "final_norm", "loss"]
    n_small = sum(int(np.prod(s)) for s in small_shapes)
    small_rows = 8 * ((n_small + 1023) // 1024)
    assert dm == 8 * 128
    small_part = _pack_small(
        [jnp.concatenate([jnp.zeros_like(d_nmix1), d_nmix1]), jnp.concatenate([d_nffn0, d_nffn1]), d_lng, d_lnb,
         d_wsp, jnp.sum(d_bsp3, axis=-1), d_cw, dsink[:, 0:N_HEADS], d_relb, d_final, loss_part], small_rows)
    (p_wd0,), ((a_winT,), (small_slots,)) = _call(
        _wgrad(act0, dx2b, "wgrad_down0"), [_PairCarry(p_winT), _BroadcastCarry(small_part)])
    (p_woe,), ((r_winT,), (a_wd0,)) = _call(
        _wgrad(yb, dx1b, "wgrad_even_out"), [_ChipSumCarry(p_winT, a_winT), _PairCarry(p_wd0)])
    (dx0, _, d_nmix0), ((r_wd0,), (r_woe,)) = _call(
        _proj_bwd_norm(dproj, x0, row(norm_mix[0]), dx1, g_winT, "in_proj_bwd", tm),
        [_ChipSumCarry(p_wd0, a_wd0), _GradCarry([p_woe])])

    grads = {}

    order = ["norm_mix", "norm_ffn", "even_w_in", "even_v_ln_g", "even_v_ln_b", "even_w_spatial", "even_b_spatial",
             "even_conv_w", "even_w_out", "attn_w_qkv", "attn_sink", "rel_bias", "attn_w_out", "ffn_w_gate",
             "ffn_w_up", "ffn_w_down", "final_norm"]
    ws = dict(norm_mix=norm_mix, norm_ffn=norm_ffn, even_w_in=even_w_in, even_v_ln_g=even_v_ln_g,
              even_v_ln_b=even_v_ln_b, even_w_spatial=even_w_spatial, even_b_spatial=even_b_spatial,
              even_conv_w=even_conv_w, even_w_out=even_w_out, attn_w_qkv=attn_w_qkv, attn_sink=attn_sink,
              rel_bias=rel_bias, attn_w_out=attn_w_out, ffn_w_gate=ffn_w_gate, ffn_w_up=ffn_w_up,
              ffn_w_down=ffn_w_down, final_norm=final_norm)
    ms = dict(norm_mix=m_norm_mix, norm_ffn=m_norm_ffn, even_w_in=m_even_w_in, even_v_ln_g=m_even_v_ln_g,
              even_v_ln_b=m_even_v_ln_b, even_w_spatial=m_even_w_spatial, even_b_spatial=m_even_b_spatial,
              even_conv_w=m_even_conv_w, even_w_out=m_even_w_out, attn_w_qkv=m_attn_w_qkv, attn_sink=m_attn_sink,
              rel_bias=m_rel_bias, attn_w_out=m_attn_w_out, ffn_w_gate=m_ffn_w_gate, ffn_w_up=m_ffn_w_up,
              ffn_w_down=m_ffn_w_down, final_norm=m_final_norm)
    vs = dict(norm_mix=v_norm_mix, norm_ffn=v_norm_ffn, even_w_in=v_even_w_in, even_v_ln_g=v_even_v_ln_g,
              even_v_ln_b=v_even_v_ln_b, even_w_spatial=v_even_w_spatial, even_b_spatial=v_even_b_spatial,
              even_conv_w=v_even_conv_w, even_w_out=v_even_w_out, attn_w_qkv=v_attn_w_qkv, attn_sink=v_attn_sink,
              rel_bias=v_rel_bias, attn_w_out=v_attn_w_out, ffn_w_gate=v_ffn_w_gate, ffn_w_up=v_ffn_w_up,
              ffn_w_down=v_ffn_w_down, final_norm=v_final_norm)
    big = dict(ffn_w_gate=([r_wgT0, r_wgT1], True), even_w_in=([r_winT], True), even_w_out=([r_woe], False),
               attn_w_qkv=([r_wqkvT], True), attn_w_out=([r_woa], False), ffn_w_up=([r_wuT0, r_wuT1], True),
               ffn_w_down=([r_wd0, r_wd1], False))
    delta, new_m, new_v = {}, {}, {}
    late_slots = None
    for n, (recvs, transposed) in big.items():
        lay = (lambda a: jnp.swapaxes(a, 1, 2)) if transposed else (lambda a: a)
        spec = _finish_weight(recvs, lay(ws[n]), lay(ms[n]), lay(vs[n]), "finish_" + n)
        if late_slots is None:
            outs, (late_slots,) = _call(spec, _BroadcastCarry(d_nmix0.reshape(8, 128)))
        else:
            outs, _ = _call(spec)
        grads[n], delta[n], new_m[n], new_v[n] = [lay(o) for o in outs]
    pk = lambda dct: _pack_small(
        [in_full(dct[n]) if n == "even_conv_w" else (jnp.zeros((1, 1), F32) if n == "loss" else dct[n])
         for n in small_names], small_rows)
    packed = _adamw_small(pk(ws), small_slots, late_slots, pk(ms), pk(vs), "adamw_small")
    mine = lambda a: lax.dynamic_slice(a, (0, me * (bw // NDEV)), (3, bw // NDEV))[None]
    for dst, arr in zip((grads, delta, new_m, new_v), packed):
        for n, a in zip(small_names, _unpack_small(arr, small_shapes)):
            dst[n] = mine(a) if n == "even_conv_w" else a
    loss = grads["loss"][0, 0]
    return (loss, dx0[None], *[grads[n] for n in order], *[delta[n] for n in order],
            *[new_m[n] for n in order], *[new_v[n] for n in order])
```

```python
import math

import jax
import jax.numpy as jnp
import numpy as np
from jax import lax
from jax.experimental import pallas as pl
from jax.experimental.pallas import tpu as pltpu

F32, BF16 = jnp.float32, jnp.bfloat16
NDEV = 8
EPS = 1e-6
CHUNK = 128
A_GROUPS = 4
N_HEADS, N_KV, HEAD_DIM = 16, 4, 64
N_BUCKETS, MAX_DISTANCE = 32, 128
NEG = -1e30
LOG2E = 1.4426950408889634
ADAM_LR, ADAM_B1, ADAM_B2, ADAM_EPS, ADAM_WD, ADAM_STEP = 0.001, 0.9, 0.999, 1e-08, 0.01, 10
VMEM_LIMIT = 56 * 1024 * 1024
MESH = pl.DeviceIdType.MESH
NT = (((1,), (1,)), ((), ()))
NN = (((1,), (0,)), ((), ()))
TN = (((0,), (0,)), ((), ()))
ANY = pl.BlockSpec(memory_space=pl.ANY)


def _cp(n_grid=1):
    return pltpu.CompilerParams(dimension_semantics=("arbitrary",) * n_grid, vmem_limit_bytes=VMEM_LIMIT)


def _dot(a, b, dims):
    return lax.dot_general(a, b, dims, preferred_element_type=F32)


def _my_index():
    return 4 * lax.axis_index("x") + 2 * lax.axis_index("y") + lax.axis_index("c")


def _peer(k):
    x, y, c = lax.axis_index("x"), lax.axis_index("y"), lax.axis_index("c")
    px = 1 - x if k & 4 else x
    py = 1 - y if k & 2 else y
    pc = 1 - c if k & 1 else c
    return (px, py, pc)


def _load_weight(gath_ref, wbuf, sems):
    rows = gath_ref.shape[1]
    cps = [pltpu.make_async_copy(gath_ref.at[d], wbuf.at[pl.ds(d * rows, rows), :], sems.at[d]) for d in range(NDEV)]
    for c in cps:
        c.start()
    for c in cps:
        c.wait()


class _GatherCarry:
    def __init__(self, pieces):
        self.inputs = list(pieces)
        self.n = len(pieces)
        self.out_shape = [jax.ShapeDtypeStruct((NDEV,) + p.shape, p.dtype) for p in pieces]
        self.scratch = [pltpu.SemaphoreType.DMA((7 * self.n,)), pltpu.SemaphoreType.DMA((7 * self.n,)),
                        pltpu.SemaphoreType.DMA((self.n,))]

    def _ctx(self):
        x, y, c = lax.axis_index("x"), lax.axis_index("y"), lax.axis_index("c")
        chips = [(1 - x, y), (x, 1 - y), (1 - x, 1 - y)]
        return (x, y, c), (x, y, 1 - c), chips, c

    def _copy(self, k, j, block, to, ins, outs, sems, src=None):
        send_sems, recv_sems, _ = sems
        slot = outs[j].at[4 * block[0] + 2 * block[1] + block[2]]
        return pltpu.make_async_remote_copy(
            src_ref=slot if src is None else src, dst_ref=slot, send_sem=send_sems.at[k * self.n + j],
            recv_sem=recv_sems.at[k * self.n + j], device_id=to, device_id_type=MESH)

    def start(self, ins, outs, sems):
        me, sibling, chips, c = self._ctx()
        for j in range(self.n):
            pltpu.make_async_copy(ins[j], outs[j].at[4 * me[0] + 2 * me[1] + me[2]], sems[2].at[j]).start()
            self._copy(0, j, me, sibling, ins, outs, sems, src=ins[j]).start()
            for q, chip in enumerate(chips):
                self._copy(1 + q, j, me, (*chip, c), ins, outs, sems, src=ins[j]).start()

    def mid(self, ins, outs, sems):
        me, sibling, chips, c = self._ctx()
        for q, chip in enumerate(chips):
            for j in range(self.n):
                self._copy(1 + q, j, (*chip, c), me, ins, outs, sems).wait_recv()
                self._copy(4 + q, j, (*chip, c), sibling, ins, outs, sems).start()

    def finish(self, ins, outs, sems):
        me, sibling, chips, c = self._ctx()
        for j in range(self.n):
            self._copy(0, j, sibling, me, ins, outs, sems).wait_recv()
            for q, chip in enumerate(chips):
                self._copy(4 + q, j, (*chip, 1 - c), me, ins, outs, sems).wait_recv()
        for j in range(self.n):
            self._copy(0, j, me, sibling, ins, outs, sems, src=ins[j]).wait_send()
            for q, chip in enumerate(chips):
                self._copy(1 + q, j, me, (*chip, c), ins, outs, sems, src=ins[j]).wait_send()
                self._copy(4 + q, j, (*chip, c), sibling, ins, outs, sems).wait_send()
            pltpu.make_async_copy(ins[j], outs[j].at[0], sems[2].at[j]).wait()


class _GradCarry:
    def __init__(self, pieces):
        self.inputs = list(pieces)
        self.n = len(pieces)
        self.rows = [p.shape[0] // NDEV for p in pieces]
        self.out_shape = [jax.ShapeDtypeStruct((NDEV, r, p.shape[1]), p.dtype) for p, r in zip(pieces, self.rows)]
        self.scratch = [pltpu.SemaphoreType.DMA((7 * self.n,)), pltpu.SemaphoreType.DMA((7 * self.n,)),
                        pltpu.SemaphoreType.DMA((self.n,))]

    def _copies(self, ins, outs, sems):
        me = _my_index()
        local, remote = [], []
        for j in range(self.n):
            r = self.rows[j]
            local.append(pltpu.make_async_copy(ins[j].at[pl.ds(pl.multiple_of(me * r, 16), r), :], outs[j].at[me],
                                               sems[2].at[j]))
            for k in range(1, NDEV):
                peer = _peer(k)
                pidx = 4 * peer[0] + 2 * peer[1] + peer[2]
                remote.append(pltpu.make_async_remote_copy(
                    src_ref=ins[j].at[pl.ds(pl.multiple_of(pidx * r, 16), r), :], dst_ref=outs[j].at[me],
                    send_sem=sems[0].at[(k - 1) * self.n + j], recv_sem=sems[1].at[(k - 1) * self.n + j],
                    device_id=peer, device_id_type=MESH))
        return local, remote

    def start(self, ins, outs, sems):
        local, remote = self._copies(ins, outs, sems)
        for cp in local + remote:
            cp.start()

    def mid(self, ins, outs, sems):
        pass

    def finish(self, ins, outs, sems):
        local, remote = self._copies(ins, outs, sems)
        for cp in remote + local:
            cp.wait()


class _BroadcastCarry:
    def __init__(self, part):
        self.inputs = [part]
        self.out_shape = [jax.ShapeDtypeStruct((NDEV,) + part.shape, part.dtype)]
        self.scratch = [pltpu.SemaphoreType.DMA((7,)), pltpu.SemaphoreType.DMA((7,)), pltpu.SemaphoreType.DMA(())]

    def _copies(self, ins, outs, sems):
        me = _my_index()
        local = pltpu.make_async_copy(ins[0], outs[0].at[me], sems[2])
        remote = [pltpu.make_async_remote_copy(
            src_ref=ins[0], dst_ref=outs[0].at[me], send_sem=sems[0].at[k - 1], recv_sem=sems[1].at[k - 1],
            device_id=_peer(k), device_id_type=MESH) for k in range(1, NDEV)]
        return [local] + remote

    def start(self, ins, outs, sems):
        for cp in self._copies(ins, outs, sems):
            cp.start()

    def mid(self, ins, outs, sems):
        pass

    def finish(self, ins, outs, sems):
        for cp in self._copies(ins, outs, sems):
            cp.wait()


class _PairCarry:
    def __init__(self, piece):
        self.inputs = [piece]
        self.r = piece.shape[0] // NDEV
        self.out_shape = [jax.ShapeDtypeStruct((4, self.r, piece.shape[1]), piece.dtype)]
        self.scratch = [pltpu.SemaphoreType.DMA((4,)), pltpu.SemaphoreType.DMA((4,))]

    def _copies(self, ins, outs, sems):
        x, y, c = lax.axis_index("x"), lax.axis_index("y"), lax.axis_index("c")
        return [pltpu.make_async_remote_copy(
            src_ref=ins[0].at[pl.ds(pl.multiple_of((2 * q + 1 - c) * self.r, 16), self.r), :], dst_ref=outs[0].at[q],
            send_sem=sems[0].at[q], recv_sem=sems[1].at[q], device_id=(x, y, 1 - c), device_id_type=MESH)
            for q in range(4)]

    def start(self, ins, outs, sems):
        for cp in self._copies(ins, outs, sems):
            cp.start()

    def mid(self, ins, outs, sems):
        pass

    def finish(self, ins, outs, sems):
        for cp in self._copies(ins, outs, sems):
            cp.wait()


class _ChipSumCarry:
    def __init__(self, piece, landed):
        self.inputs = [piece, landed]
        self.r, dm = piece.shape[0] // NDEV, piece.shape[1]
        self.out_shape = [jax.ShapeDtypeStruct((4, self.r, dm), piece.dtype)]
        self.scratch = [pltpu.VMEM((4, self.r, dm), piece.dtype), pltpu.VMEM((8, self.r, dm), piece.dtype),
                        pltpu.SemaphoreType.DMA((8,)), pltpu.SemaphoreType.DMA((3,)), pltpu.SemaphoreType.DMA((3,)),
                        pltpu.SemaphoreType.DMA(())]

    def _copies(self, outs, scr):
        sums, _, _, send_sems, recv_sems, local_sem = scr
        x, y, c = lax.axis_index("x"), lax.axis_index("y"), lax.axis_index("c")
        mine = 2 * x + y
        local = pltpu.make_async_copy(sums.at[mine], outs[0].at[mine], local_sem)
        remote = []
        for k in range(1, 4):
            px = 1 - x if k & 2 else x
            py = 1 - y if k & 1 else y
            remote.append(pltpu.make_async_remote_copy(
                src_ref=sums.at[2 * px + py], dst_ref=outs[0].at[mine], send_sem=send_sems.at[k - 1],
                recv_sem=recv_sems.at[k - 1], device_id=(px, py, c), device_id_type=MESH))
        return local, remote

    def start(self, ins, outs, scr):
        sums, stage, stage_sems = scr[0], scr[1], scr[2]
        c = lax.axis_index("c")
        loads = []
        for q in range(4):
            loads.append((
                pltpu.make_async_copy(ins[0].at[pl.ds(pl.multiple_of((2 * q + c) * self.r, 16), self.r), :],
                                      stage.at[2 * q], stage_sems.at[2 * q]),
                pltpu.make_async_copy(ins[1].at[q], stage.at[2 * q + 1], stage_sems.at[2 * q + 1])))
        for a, b in loads:
            a.start()
            b.start()
        for q, (a, b) in enumerate(loads):
            a.wait()
            b.wait()
            sums[q] = (stage[2 * q].astype(F32) + stage[2 * q + 1].astype(F32)).astype(sums.dtype)
        local, remote = self._copies(outs, scr)
        for cp in [local] + remote:
            cp.start()

    def mid(self, ins, outs, scr):
        pass

    def finish(self, ins, outs, scr):
        local, remote = self._copies(outs, scr)
        for cp in remote + [local]:
            cp.wait()


def _call(spec, carry=None):
    body, grid = spec["body"], spec["grid"]
    in_specs, out_specs, out_shape = list(spec["in_specs"]), list(spec["out_specs"]), list(spec["out_shape"])
    scratch, args = list(spec.get("scratch", [])), list(spec["args"])
    if carry is None:
        out = pl.pallas_call(body, grid=grid, in_specs=in_specs, out_specs=tuple(out_specs),
                             out_shape=tuple(out_shape), scratch_shapes=scratch, compiler_params=_cp(len(grid)),
                             name=spec["name"])(*args)
        return tuple(out), ()
    carries = list(carry) if isinstance(carry, (list, tuple)) else [carry]
    n_in, n_out, n_s = len(in_specs), len(out_specs), len(scratch)
    steps = int(np.prod(grid))

    def split(refs, counts):
        parts, o = [], 0
        for cnt in counts:
            parts.append(refs[o:o + cnt])
            o += cnt
        return parts

    c_in = [len(cr.inputs) for cr in carries]
    c_out = [len(cr.out_shape) for cr in carries]
    c_scr = [len(cr.scratch) for cr in carries]

    def wrapped(*refs):
        ins, cins, outs, couts, scr, cscr = split(refs, [n_in, sum(c_in), n_out, sum(c_out), n_s, sum(c_scr)])
        per = list(zip(carries, split(cins, c_in), split(couts, c_out), split(cscr, c_scr)))
        step = pl.program_id(0)
        for ax in range(1, len(grid)):
            step = step * grid[ax] + pl.program_id(ax)

        @pl.when(step == 0)
        def _():
            for cr, ci, co, cs in per:
                cr.start(ci, co, cs)
        if steps >= 3:
            @pl.when(step == steps - 2)
            def _():
                for cr, ci, co, cs in per:
                    cr.mid(ci, co, cs)
        body(*ins, *outs, *scr)

        @pl.when(step == steps - 1)
        def _():
            for cr, ci, co, cs in per:
                if steps < 3:
                    cr.mid(ci, co, cs)
                cr.finish(ci, co, cs)

    out = pl.pallas_call(
        wrapped, grid=grid, in_specs=in_specs + [ANY] * sum(c_in), out_specs=tuple(out_specs + [ANY] * sum(c_out)),
        out_shape=tuple(out_shape + [s for cr in carries for s in cr.out_shape]),
        scratch_shapes=scratch + [s for cr in carries for s in cr.scratch],
        compiler_params=_cp(len(grid)), name=spec["name"])(*args, *[a for cr in carries for a in cr.inputs])
    c_res = [tuple(p) for p in split(out[n_out:], c_out)]
    return tuple(out[:n_out]), (c_res if isinstance(carry, (list, tuple)) else c_res[0])


def _exchange_only(carry, name):
    spec = dict(body=lambda: None, grid=(1,), in_specs=[], out_specs=[], out_shape=[], args=[], name=name)
    return _call(spec, carry)[1]


def _rms_fwd(x, gain):
    r = lax.rsqrt(jnp.mean(x * x, axis=-1, keepdims=True) + EPS)
    return x * r * gain, r


def _rms_bwd(dh, x, r, gain):
    a = dh * gain
    dx = r * a - x * (r * r * r) * jnp.mean(a * x, axis=-1, keepdims=True)
    dgain = jnp.sum(dh * (x * r), axis=0, keepdims=True)
    return dx, dgain


def _gelu(x):
    return 0.5 * x * (1.0 + lax.erf(x * 0.7071067811865476))


def _gelu_grad(x):
    return 0.5 * (1.0 + lax.erf(x * 0.7071067811865476)) + x * jnp.exp(-0.5 * x * x) * 0.3989422804014327


def _sigmoid(x):
    return 1.0 / (1.0 + jnp.exp(-x))


def _adamw_math(w, g, m, v):
    nm = ADAM_B1 * m + (1.0 - ADAM_B1) * g
    nv = ADAM_B2 * v + (1.0 - ADAM_B2) * (g * g)
    m_hat = nm / (1.0 - ADAM_B1 ** ADAM_STEP)
    v_hat = nv / (1.0 - ADAM_B2 ** ADAM_STEP)
    return -ADAM_LR * (m_hat / (jnp.sqrt(v_hat) + ADAM_EPS) + ADAM_WD * w), nm, nv


def _tok(tm, w):
    return pl.BlockSpec((tm, w), lambda i: (i, 0))


def _full(shape):
    return pl.BlockSpec(shape, lambda *i: (0,) * len(shape))


def _norm_proj(x, gain, gath, out_dtype, name, tm):
    t, dm = x.shape
    n = gath.shape[1] * NDEV

    def body(x_ref, g_ref, gath_ref, proj_ref, hb_ref, wbuf, sems):
        @pl.when(pl.program_id(0) == 0)
        def _():
            _load_weight(gath_ref, wbuf, sems)
        h, _ = _rms_fwd(x_ref[...], g_ref[...])
        hb = h.astype(BF16)
        hb_ref[...] = hb
        proj_ref[...] = _dot(hb, wbuf[...], NT).astype(out_dtype)

    return dict(
        body=body, grid=(t // tm,), name=name, args=[x, gain, gath],
        out_shape=[jax.ShapeDtypeStruct((t, n), out_dtype), jax.ShapeDtypeStruct((t, dm), BF16)],
        in_specs=[_tok(tm, dm), _full((1, dm)), ANY], out_specs=[_tok(tm, n), _tok(tm, dm)],
        scratch=[pltpu.VMEM((n, dm), BF16), pltpu.SemaphoreType.DMA((NDEV,))])


def _proj_bwd_norm(dy, x, gain, dres, gath, name, tm):
    t, dm = x.shape
    n = gath.shape[1] * NDEV

    def body(dy_ref, x_ref, g_ref, dres_ref, gath_ref, dx_ref, dxb_ref, dgain_ref, wbuf, sems):
        @pl.when(pl.program_id(0) == 0)
        def _():
            _load_weight(gath_ref, wbuf, sems)
            dgain_ref[...] = jnp.zeros_like(dgain_ref)
        xv, gain_v = x_ref[...], g_ref[...]
        _, r = _rms_fwd(xv, gain_v)
        dh = _dot(dy_ref[...], wbuf[...], NN)
        dx, dgain = _rms_bwd(dh, xv, r, gain_v)
        dx = dres_ref[...] + dx
        dx_ref[...] = dx
        dxb_ref[...] = dx.astype(BF16)
        dgain_ref[...] += dgain

    return dict(
        body=body, grid=(t // tm,), name=name, args=[dy, x, gain, dres, gath],
        out_shape=[jax.ShapeDtypeStruct((t, dm), F32), jax.ShapeDtypeStruct((t, dm), BF16),
                   jax.ShapeDtypeStruct((1, dm), F32)],
        in_specs=[_tok(tm, n), _tok(tm, dm), _full((1, dm)), _tok(tm, dm), ANY],
        out_specs=[_tok(tm, dm), _tok(tm, dm), _full((1, dm))],
        scratch=[pltpu.VMEM((n, dm), BF16), pltpu.SemaphoreType.DMA((NDEV,))])


def _wgrad(a, b, name, tmm=256):
    t, m = a.shape
    n = b.shape[1]

    def body(a_ref, b_ref, o_ref):
        o_ref[...] = _dot(a_ref[...], b_ref[...], TN).astype(BF16)

    return dict(
        body=body, grid=(m // tmm,), name=name, args=[a, b], out_shape=[jax.ShapeDtypeStruct((m, n), BF16)],
        in_specs=[pl.BlockSpec((t, tmm), lambda j: (0, j)), pl.BlockSpec((t, n), lambda j: (0, 0))],
        out_specs=[pl.BlockSpec((tmm, n), lambda j: (j, 0))])


def _halo_specs(tm, t, width, col_blocks):
    nb8 = tm // 8
    last = t // 8 - 1
    prev = [pl.BlockSpec((8, width), lambda i, cb=cb: (jnp.maximum(i * nb8 - 1, 0), cb)) for cb in col_blocks]
    nxt = [pl.BlockSpec((8, width), lambda i, cb=cb: (jnp.minimum((i + 1) * nb8, last), cb)) for cb in col_blocks]
    return prev, nxt


def _shift_rows(z, prev_row, next_row):
    tm = z.shape[0]
    row = lax.broadcasted_iota(jnp.int32, z.shape, 0)
    zm1 = jnp.where(row == 0, prev_row, pltpu.roll(z, 1, 0))
    zp1 = jnp.where(row == tm - 1, next_row, pltpu.roll(z, tm - 1, 0))
    return zm1, zp1


def _gating_fwd(proj, lng, lnb, wsp_ref, bsp_ref, aw):
    tm = proj.shape[0]
    a_u = _gelu(proj[:, 0:aw])
    gv = _gelu(proj[:, aw:2 * aw])
    mu = jnp.mean(gv, axis=-1, keepdims=True)
    xc = gv - mu
    rstd = lax.rsqrt(jnp.mean(xc * xc, axis=-1, keepdims=True) + EPS)
    vn = xc * rstd
    a_v = (vn * lng + lnb).astype(BF16)
    gd = aw // A_GROUPS
    rows = []
    for c in range(tm // CHUNK):
        cols = []
        for g in range(A_GROUPS):
            blk = a_v[c * CHUNK:(c + 1) * CHUNK, g * gd:(g + 1) * gd]
            cols.append(_dot(wsp_ref[g], blk, NN) + bsp_ref[g])
        rows.append(jnp.concatenate(cols, axis=1))
    mixed = jnp.concatenate(rows, axis=0)
    return a_u, vn, rstd, a_v, mixed


def _even_core_fwd(proj, x0, lng, lnb, wsp, bspb, cw, gath, tm):
    t, dm = x0.shape
    aw = lng.shape[1]
    bw = cw.shape[1]
    assert aw == bw and 2 * aw + 3 * bw == proj.shape[1]
    nt = t // tm
    prev, nxt = _halo_specs(tm, t, bw, [3, 4])

    def body(proj_ref, cp_ref, hp_ref, cn_ref, hn_ref, x0_ref, lng_ref, lnb_ref, wsp_ref, bsp_ref, cw_ref, gath_ref,
             x1_ref, y_ref, wbuf, sems):
        i = pl.program_id(0)

        @pl.when(i == 0)
        def _():
            _load_weight(gath_ref, wbuf, sems)
        proj_v = proj_ref[...]
        a_u, _, _, _, mixed = _gating_fwd(proj_v, lng_ref[...], lnb_ref[...], wsp_ref, bsp_ref, aw)
        a_out = a_u * mixed
        bb = proj_v[:, 2 * aw:2 * aw + bw]
        z = proj_v[:, 2 * aw + bw:2 * aw + 2 * bw] * proj_v[:, 2 * aw + 2 * bw:]
        zprev = jnp.where(i > 0, cp_ref[7:8, :] * hp_ref[7:8, :], 0.0)
        znext = jnp.where(i < nt - 1, cn_ref[0:1, :] * hn_ref[0:1, :], 0.0)
        zm1, zp1 = _shift_rows(z, zprev, znext)
        cwv = cw_ref[...]
        conv = zm1 * cwv[0:1, :] + z * cwv[1:2, :] + zp1 * cwv[2:3, :]
        y = jnp.concatenate([a_out, bb * conv], axis=1).astype(BF16)
        y_ref[...] = y
        x1_ref[...] = x0_ref[...] + _dot(y, wbuf[...], NN)

    return dict(
        body=body, grid=(nt,), name="even_core_fwd",
        args=[proj, proj, proj, proj, proj, x0, lng, lnb, wsp, bspb, cw, gath],
        out_shape=[jax.ShapeDtypeStruct((t, dm), F32), jax.ShapeDtypeStruct((t, aw + bw), BF16)],
        in_specs=[_tok(tm, proj.shape[1]), prev[0], prev[1], nxt[0], nxt[1], _tok(tm, dm), _full(lng.shape),
                  _full(lnb.shape), _full(wsp.shape), _full(bspb.shape), _full(cw.shape), ANY],
        out_specs=[_tok(tm, dm), _tok(tm, aw + bw)],
        scratch=[pltpu.VMEM((gath.shape[1] * NDEV, dm), BF16), pltpu.SemaphoreType.DMA((NDEV,))])


def _even_core_bwd(proj, dx1, lng, lnb, wsp, bspb, cw, gath, tm):
    t, dm = dx1.shape
    aw, bw = lng.shape[1], cw.shape[1]
    gd = aw // A_GROUPS
    nt = t // tm
    inw = proj.shape[1]
    prev, nxt = _halo_specs(tm, t, bw, [2, 3, 4])
    nb8 = tm // 8
    last8 = t // 8 - 1

    def body(proj_ref, bp_ref, cp_ref, hp_ref, bn_ref, cn_ref, hn_ref, dx_ref, dxp_ref, dxn_ref,
             lng_ref, lnb_ref, wsp_ref, bsp_ref, cw_ref, gath_ref,
             dproj_ref, dlng_ref, dlnb_ref, dwsp_ref, dbsp_ref, dcw_ref, wbuf, sems):
        i = pl.program_id(0)

        @pl.when(i == 0)
        def _():
            _load_weight(gath_ref, wbuf, sems)
            dlng_ref[...] = jnp.zeros_like(dlng_ref)
            dlnb_ref[...] = jnp.zeros_like(dlnb_ref)
            dwsp_ref[...] = jnp.zeros_like(dwsp_ref)
            dbsp_ref[...] = jnp.zeros_like(dbsp_ref)
            dcw_ref[...] = jnp.zeros_like(dcw_ref)
        proj_v = proj_ref[...]
        lng_v = lng_ref[...]
        a_u, vn, rstd, a_v, mixed = _gating_fwd(proj_v, lng_v, lnb_ref[...], wsp_ref, bsp_ref, aw)
        w = wbuf[...]
        dy = _dot(dx_ref[...].astype(BF16), w, NT)
        da_out, db_out = dy[:, 0:aw], dy[:, aw:]
        da_u = da_out * mixed
        dmixed = da_out * a_u
        dmb = dmixed.astype(BF16)
        rows = []
        for c in range(tm // CHUNK):
            cols = []
            for g in range(A_GROUPS):
                r0, c0 = c * CHUNK, g * gd
                dm_cg = dmb[r0:r0 + CHUNK, c0:c0 + gd]
                cols.append(_dot(wsp_ref[g], dm_cg, TN))
                dwsp_ref[g] += _dot(dm_cg, a_v[r0:r0 + CHUNK, c0:c0 + gd], NT)
                dbsp_ref[g] += dmixed[r0:r0 + CHUNK, c0:c0 + gd]
            rows.append(jnp.concatenate(cols, axis=1))
        dav = jnp.concatenate(rows, axis=0)
        dlng_ref[...] += jnp.sum(dav * vn, axis=0, keepdims=True)
        dlnb_ref[...] += jnp.sum(dav, axis=0, keepdims=True)
        dvn = dav * lng_v
        dgv = rstd * (dvn - jnp.mean(dvn, axis=-1, keepdims=True) - vn * jnp.mean(dvn * vn, axis=-1, keepdims=True))
        dv_pre = dgv * _gelu_grad(proj_v[:, aw:2 * aw])
        du_pre = da_u * _gelu_grad(proj_v[:, 0:aw])
        bb = proj_v[:, 2 * aw:2 * aw + bw]
        bc = proj_v[:, 2 * aw + bw:2 * aw + 2 * bw]
        bh = proj_v[:, 2 * aw + 2 * bw:]
        z = bc * bh
        zprev = jnp.where(i > 0, cp_ref[7:8, :] * hp_ref[7:8, :], 0.0)
        znext = jnp.where(i < nt - 1, cn_ref[0:1, :] * hn_ref[0:1, :], 0.0)
        zm1, zp1 = _shift_rows(z, zprev, znext)
        cwv = cw_ref[...]
        conv = zm1 * cwv[0:1, :] + z * cwv[1:2, :] + zp1 * cwv[2:3, :]
        dbb = db_out * conv
        dconv = db_out * bb
        dx_edge = jnp.concatenate([dxp_ref[...], dxn_ref[...]], axis=0).astype(BF16)
        dy_edge = _dot(dx_edge, w[aw:, :], NT)
        dcprev = jnp.where(i > 0, dy_edge[7:8, :] * bp_ref[7:8, :], 0.0)
        dcnext = jnp.where(i < nt - 1, dy_edge[8:9, :] * bn_ref[0:1, :], 0.0)
        dcm1, dcp1 = _shift_rows(dconv, dcprev, dcnext)
        dz = dcp1 * cwv[0:1, :] + dconv * cwv[1:2, :] + dcm1 * cwv[2:3, :]
        dcw_ref[0:1, :] += jnp.sum(dconv * zm1, axis=0, keepdims=True)
        dcw_ref[1:2, :] += jnp.sum(dconv * z, axis=0, keepdims=True)
        dcw_ref[2:3, :] += jnp.sum(dconv * zp1, axis=0, keepdims=True)
        dproj_ref[...] = jnp.concatenate([du_pre, dv_pre, dbb, dz * bh, dz * bc], axis=1).astype(BF16)

    row8 = lambda f: pl.BlockSpec((8, dm), f)
    return dict(
        body=body, grid=(nt,), name="even_core_bwd",
        args=[proj, proj, proj, proj, proj, proj, proj, dx1, dx1, dx1, lng, lnb, wsp, bspb, cw, gath],
        out_shape=[jax.ShapeDtypeStruct((t, inw), BF16), jax.ShapeDtypeStruct((1, aw), F32),
                   jax.ShapeDtypeStruct((1, aw), F32), jax.ShapeDtypeStruct(wsp.shape, F32),
                   jax.ShapeDtypeStruct((A_GROUPS, CHUNK, gd), F32), jax.ShapeDtypeStruct(cw.shape, F32)],
        in_specs=[_tok(tm, inw), prev[0], prev[1], prev[2], nxt[0], nxt[1], nxt[2], _tok(tm, dm),
                  row8(lambda i: (jnp.maximum(i * nb8 - 1, 0), 0)), row8(lambda i: (jnp.minimum((i + 1) * nb8, last8), 0)),
                  _full(lng.shape), _full(lnb.shape), _full(wsp.shape), _full(bspb.shape), _full(cw.shape), ANY],
        out_specs=[_tok(tm, inw), _full((1, aw)), _full((1, aw)), _full(wsp.shape),
                   _full((A_GROUPS, CHUNK, gd)), _full(cw.shape)],
        scratch=[pltpu.VMEM((gath.shape[1] * NDEV, dm), BF16), pltpu.SemaphoreType.DMA((NDEV,))])


def _ff_chunks(f, width=1024):
    return [(c0, min(c0 + width, f)) for c0 in range(0, f, width)]


def _ffn_up(x, gain, gath_g, gath_u, name, tm):
    t, dm = x.shape
    f = gath_g.shape[1] * NDEV

    def body(x_ref, g_ref, gg_ref, gu_ref, gate_ref, up_ref, act_ref, wg, wu, sems):
        @pl.when(pl.program_id(0) == 0)
        def _():
            _load_weight(gg_ref, wg, sems)
            _load_weight(gu_ref, wu, sems)
        h, _ = _rms_fwd(x_ref[...], g_ref[...])
        hb = h.astype(BF16)
        for c0, c1 in _ff_chunks(f):
            gate = _dot(hb, wg[c0:c1, :], NT)
            up = _dot(hb, wu[c0:c1, :], NT)
            gate_ref[:, c0:c1] = gate.astype(BF16)
            up_ref[:, c0:c1] = up.astype(BF16)
            act_ref[:, c0:c1] = (gate * _sigmoid(gate) * up).astype(BF16)

    o = jax.ShapeDtypeStruct((t, f), BF16)
    return dict(
        body=body, grid=(t // tm,), name=name, args=[x, gain, gath_g, gath_u], out_shape=[o, o, o],
        in_specs=[_tok(tm, dm), _full((1, dm)), ANY, ANY], out_specs=[_tok(tm, f)] * 3,
        scratch=[pltpu.VMEM((f, dm), BF16), pltpu.VMEM((f, dm), BF16), pltpu.SemaphoreType.DMA((NDEV,))])


def _ffn_down(x, act, gath_d, name, tm):
    t, dm = x.shape
    f = act.shape[1]

    def body(x_ref, a_ref, gd_ref, xo_ref, wd, sems):
        @pl.when(pl.program_id(0) == 0)
        def _():
            _load_weight(gd_ref, wd, sems)
        xo_ref[...] = x_ref[...] + _dot(a_ref[...], wd[...], NN)

    return dict(
        body=body, grid=(t // tm,), name=name, args=[x, act, gath_d], out_shape=[jax.ShapeDtypeStruct((t, dm), F32)],
        in_specs=[_tok(tm, dm), _tok(tm, f), ANY], out_specs=[_tok(tm, dm)],
        scratch=[pltpu.VMEM((f, dm), BF16), pltpu.SemaphoreType.DMA((NDEV,))])


def _ffn_down_loss(x, act, gath_d, target, gain, name, tm):
    t, dm = x.shape
    f = act.shape[1]
    steps = t // tm

    def body(x_ref, a_ref, gd_ref, t_ref, g_ref, loss_ref, dx_ref, dxb_ref, dgain_ref, wd, acc, sems):
        i = pl.program_id(0)

        @pl.when(i == 0)
        def _():
            _load_weight(gd_ref, wd, sems)
            acc[...] = jnp.zeros_like(acc)
            dgain_ref[...] = jnp.zeros_like(dgain_ref)
        xv = x_ref[...] + _dot(a_ref[...], wd[...], NN)
        gain_v = g_ref[...]
        y, r = _rms_fwd(xv, gain_v)
        e = y - t_ref[...]
        acc[...] += jnp.sum(e * e, axis=0, keepdims=True)
        dx, dgain = _rms_bwd(e * (1.0 / dm), xv, r, gain_v)
        dx_ref[...] = dx
        dxb_ref[...] = dx.astype(BF16)
        dgain_ref[...] += dgain

        @pl.when(i == steps - 1)
        def _():
            loss_ref[...] = jnp.sum(acc[...], axis=-1, keepdims=True) * (0.5 / dm)

    return dict(
        body=body, grid=(steps,), name=name, args=[x, act, gath_d, target, gain],
        out_shape=[jax.ShapeDtypeStruct((1, 1), F32), jax.ShapeDtypeStruct((t, dm), F32),
                   jax.ShapeDtypeStruct((t, dm), BF16), jax.ShapeDtypeStruct((1, dm), F32)],
        in_specs=[_tok(tm, dm), _tok(tm, f), ANY, _tok(tm, dm), _full((1, dm))],
        out_specs=[_full((1, 1)), _tok(tm, dm), _tok(tm, dm), _full((1, dm))],
        scratch=[pltpu.VMEM((f, dm), BF16), pltpu.VMEM((1, dm), F32), pltpu.SemaphoreType.DMA((NDEV,))])


def _ffn_bwd(dxo, x, gate, up, gain, gath_g, gath_u, gath_d, name, tm):
    t, dm = x.shape
    f = gate.shape[1]

    def body(dxo_ref, x_ref, gate_ref, up_ref, g_ref, gg_ref, gu_ref, gd_ref,
             dx_ref, dxb_ref, dg_ref, du_ref, hb_ref, dgain_ref, wg, wu, wd, sems):
        @pl.when(pl.program_id(0) == 0)
        def _():
            _load_weight(gg_ref, wg, sems)
            _load_weight(gu_ref, wu, sems)
            _load_weight(gd_ref, wd, sems)
            dgain_ref[...] = jnp.zeros_like(dgain_ref)
        xv, gain_v, dxo_v = x_ref[...], g_ref[...], dxo_ref[...]
        h, r = _rms_fwd(xv, gain_v)
        hb_ref[...] = h.astype(BF16)
        dxob = dxo_v.astype(BF16)
        dh = jnp.zeros_like(xv)
        for c0, c1 in _ff_chunks(f):
            gate_v = gate_ref[:, c0:c1].astype(F32)
            up_v = up_ref[:, c0:c1].astype(F32)
            s = _sigmoid(gate_v)
            silu = gate_v * s
            dact = _dot(dxob, wd[c0:c1, :], NT)
            dg = (dact * up_v * (s * (1.0 + gate_v * (1.0 - s)))).astype(BF16)
            du = (dact * silu).astype(BF16)
            dg_ref[:, c0:c1] = dg
            du_ref[:, c0:c1] = du
            dh = dh + _dot(dg, wg[c0:c1, :], NN) + _dot(du, wu[c0:c1, :], NN)
        dx, dgain = _rms_bwd(dh, xv, r, gain_v)
        dx = dxo_v + dx
        dx_ref[...] = dx
        dxb_ref[...] = dx.astype(BF16)
        dgain_ref[...] += dgain

    return dict(
        body=body, grid=(t // tm,), name=name, args=[dxo, x, gate, up, gain, gath_g, gath_u, gath_d],
        out_shape=[jax.ShapeDtypeStruct((t, dm), F32), jax.ShapeDtypeStruct((t, dm), BF16),
                   jax.ShapeDtypeStruct((t, f), BF16), jax.ShapeDtypeStruct((t, f), BF16),
                   jax.ShapeDtypeStruct((t, dm), BF16), jax.ShapeDtypeStruct((1, dm), F32)],
        in_specs=[_tok(tm, dm), _tok(tm, dm), _tok(tm, f), _tok(tm, f), _full((1, dm)), ANY, ANY, ANY],
        out_specs=[_tok(tm, dm), _tok(tm, dm), _tok(tm, f), _tok(tm, f), _tok(tm, dm), _full((1, dm))],
        scratch=[pltpu.VMEM((f, dm), BF16), pltpu.VMEM((f, dm), BF16), pltpu.VMEM((f, dm), BF16),
                 pltpu.SemaphoreType.DMA((NDEV,))])


def _t5_buckets(rel):
    nb = N_BUCKETS // 2
    ret = jnp.where(rel > 0, nb, 0)
    n = jnp.abs(rel)
    max_exact = nb // 2
    nf = jnp.maximum(n, 1).astype(jnp.float32)
    large = max_exact + (jnp.log(nf / max_exact) / math.log(MAX_DISTANCE / max_exact)
                         * (nb - max_exact)).astype(jnp.int32)
    large = jnp.minimum(large, nb - 1)
    return ret + jnp.where(n < max_exact, n, large)


def _bucket_table():
    qi = jnp.arange(CHUNK, dtype=jnp.int32)[:, None]
    kj = jnp.arange(3 * CHUNK, dtype=jnp.int32)[None, :]
    rel = kj - CHUNK - qi
    return jnp.where(jnp.abs(rel) <= CHUNK, _t5_buckets(rel), -1)


def _bias_table(rel_bias_t, buckets):
    nh = rel_bias_t.shape[0]

    def body(rb_ref, bk_ref, o_ref):
        bk = bk_ref[...]
        for h in range(nh):
            acc = jnp.where(bk < 0, NEG, 0.0).astype(F32)
            for b in range(N_BUCKETS):
                acc = jnp.where(bk == b, rb_ref[h, b] * LOG2E, acc)
            o_ref[h] = acc

    return pl.pallas_call(
        body, out_shape=jax.ShapeDtypeStruct((nh,) + buckets.shape, F32),
        in_specs=[pl.BlockSpec(memory_space=pltpu.SMEM), pl.BlockSpec(memory_space=pltpu.VMEM)],
        out_specs=pl.BlockSpec(memory_space=pltpu.VMEM), name="bias_table")(rel_bias_t, buckets)


def _rel_bias_grad(dbias, buckets):
    nh = dbias.shape[0]

    def body(db_ref, bk_ref, o_ref):
        bk = bk_ref[...]
        lane = lax.broadcasted_iota(jnp.int32, (1, 128), 1)
        for h in range(nh):
            d = db_ref[h]
            row = jnp.zeros((1, 128), F32)
            for b in range(N_BUCKETS):
                s = jnp.sum(jnp.sum(jnp.where(bk == b, d, 0.0), axis=1, keepdims=True), axis=0, keepdims=True)
                row = jnp.where(lane == b, s, row)
            o_ref[h:h + 1, :] = row

    return pl.pallas_call(
        body, out_shape=jax.ShapeDtypeStruct((nh, 128), F32),
        in_specs=[pl.BlockSpec(memory_space=pltpu.VMEM), pl.BlockSpec(memory_space=pltpu.VMEM)],
        out_specs=pl.BlockSpec(memory_space=pltpu.VMEM), compiler_params=_cp(0), name="rel_bias_grad")(dbias, buckets)


def _half_masks():
    lane = lax.broadcasted_iota(jnp.int32, (CHUNK, 128), 1)
    return lane < HEAD_DIM, lane >= HEAD_DIM


def _kv_low(ref, starts, hk, lo):
    kt = (hk // 2) * 128
    out = []
    for jj in range(3):
        blk = ref[pl.ds(starts[jj], CHUNK), kt:kt + 128]
        if hk % 2 == 1:
            blk = pltpu.roll(blk, HEAD_DIM, 1)
        out.append(jnp.where(lo, blk, jnp.zeros_like(blk)))
    return out


def _stack_heads(tile_a, tile_b):
    return jnp.concatenate([tile_a, pltpu.roll(tile_a, HEAD_DIM, 1), tile_b, pltpu.roll(tile_b, HEAD_DIM, 1)], axis=0)


def _unstack_heads(o4):
    return (o4[0:CHUNK] + pltpu.roll(o4[CHUNK:2 * CHUNK], HEAD_DIM, 1),
            o4[2 * CHUNK:3 * CHUNK] + pltpu.roll(o4[3 * CHUNK:], HEAD_DIM, 1))


ATT_SLAB = 32


def _softmax_slab(s_scr, hk, g, r0, bias_ref, sink_ref, n, nblk):
    scale = HEAD_DIM ** -0.5 * LOG2E
    h = (N_HEADS // N_KV) * hk + g
    s = []
    for jj in range(3):
        sj = (s_scr[hk, jj, pl.ds(g * CHUNK + r0, ATT_SLAB), :] * scale
              + bias_ref[h, pl.ds(r0, ATT_SLAB), jj * CHUNK:(jj + 1) * CHUNK])
        if jj == 0:
            sj = jnp.where(n > 0, sj, NEG)
        if jj == 2:
            sj = jnp.where(n < nblk - 1, sj, NEG)
        s.append(sj)
    sink = sink_ref[h] * LOG2E
    m = jnp.maximum(jnp.max(jnp.maximum(jnp.maximum(s[0], s[1]), s[2]), axis=-1, keepdims=True), sink)
    e = [jnp.exp2(sj - m) for sj in s]
    es = jnp.exp2(sink - m)
    inv = 1.0 / (jnp.sum(e[0] + e[1] + e[2], axis=-1, keepdims=True) + es)
    return [ej * inv for ej in e], es * inv


def _key_block_starts(n, nblk):
    return [pl.multiple_of(jnp.clip(n - 1 + jj, 0, nblk - 1) * CHUNK, CHUNK) for jj in range(3)]


def _attn_fwd(qkv, x2, bias, sink, gath):
    t, dm = x2.shape
    nblk = t // CHUNK
    kvw = N_KV * HEAD_DIM
    kcb, vcb = dm // kvw, dm // kvw + 1
    slab = (N_KV, 3, 4 * CHUNK, CHUNK)

    def body(q_ref, k_ref, v_ref, x2_ref, bias_ref, sink_ref, gath_ref, x3_ref, att_ref, p_ref, ps_ref,
             wbuf, s_scr, sems):
        n = pl.program_id(0)

        @pl.when(n == 0)
        def _():
            _load_weight(gath_ref, wbuf, sems)
        lo, _ = _half_masks()
        lane_s = lax.broadcasted_iota(jnp.int32, (ATT_SLAB, 128), 1)
        starts = _key_block_starts(n, nblk)
        tiles = []
        for hk in range(N_KV):
            c0 = (2 * hk) * 128
            k_lo = _kv_low(k_ref, starts, hk, lo)
            v_lo = _kv_low(v_ref, starts, hk, lo)
            q4 = _stack_heads(q_ref[:, c0:c0 + 128], q_ref[:, c0 + 128:c0 + 256])
            for jj in range(3):
                s_scr[hk, jj] = _dot(q4, k_lo[jj], NT)
            for g in range(4):
                h = 4 * hk + g
                for r0 in range(0, CHUNK, ATT_SLAB):
                    p, ps = _softmax_slab(s_scr, hk, g, r0, bias_ref, sink_ref, n, nblk)
                    for jj in range(3):
                        p_ref[hk, jj, g * CHUNK + r0:g * CHUNK + r0 + ATT_SLAB, :] = p[jj].astype(BF16)
                    rest = jnp.zeros((ATT_SLAB, 128), F32) if h == 0 else ps_ref[r0:r0 + ATT_SLAB, :]
                    ps_ref[r0:r0 + ATT_SLAB, :] = jnp.where(lane_s == h, ps, rest)
            o4 = _dot(p_ref[hk, 0], v_lo[0], NN) + _dot(p_ref[hk, 1], v_lo[1], NN) + _dot(p_ref[hk, 2], v_lo[2], NN)
            tiles += list(_unstack_heads(o4))
        att = jnp.concatenate(tiles, axis=1).astype(BF16)
        att_ref[...] = att
        x3_ref[...] = x2_ref[...] + _dot(att, wbuf[...], NN)

    blk = pl.BlockSpec((CHUNK, dm), lambda n: (n, 0))
    return dict(
        body=body, grid=(nblk,), name="attn_fwd", args=[qkv, qkv, qkv, x2, bias, sink, gath],
        out_shape=[jax.ShapeDtypeStruct((t, dm), F32), jax.ShapeDtypeStruct((t, dm), BF16),
                   jax.ShapeDtypeStruct((nblk,) + slab, BF16), jax.ShapeDtypeStruct((t, 128), F32)],
        in_specs=[blk, pl.BlockSpec((t, kvw), lambda n: (0, kcb)), pl.BlockSpec((t, kvw), lambda n: (0, vcb)), blk,
                  _full(bias.shape), pl.BlockSpec(memory_space=pltpu.SMEM), ANY],
        out_specs=[blk, blk, pl.BlockSpec((None,) + slab, lambda n: (n, 0, 0, 0, 0)),
                   pl.BlockSpec((CHUNK, 128), lambda n: (n, 0))],
        scratch=[pltpu.VMEM((gath.shape[1] * NDEV, dm), BF16), pltpu.VMEM(slab, F32),
                 pltpu.SemaphoreType.DMA((NDEV,))])


def _attn_bwd(qkv, att, probs, sink_probs, dx3, bias_shape, gath):
    t, dm = dx3.shape
    nblk = t // CHUNK
    kvw = N_KV * HEAD_DIM
    kcb, vcb = dm // kvw, dm // kvw + 1
    scale = HEAD_DIM ** -0.5
    slab = (N_KV, 3, 4 * CHUNK, CHUNK)

    def body(q_ref, k_ref, v_ref, att_ref, p_ref, ps_ref, dx_ref, gath_ref,
             dq_ref, dk_ref, dv_ref, dbias_ref, dsink_ref, wbuf, dp_scr, ds_scr, prod_scr, dsum_scr, sems):
        n = pl.program_id(0)

        @pl.when(n == 0)
        def _():
            _load_weight(gath_ref, wbuf, sems)
            dk_ref[...] = jnp.zeros_like(dk_ref)
            dv_ref[...] = jnp.zeros_like(dv_ref)
            dbias_ref[...] = jnp.zeros_like(dbias_ref)
            dsink_ref[...] = jnp.zeros_like(dsink_ref)
        lo, hi = _half_masks()
        lane_s = lax.broadcasted_iota(jnp.int32, (ATT_SLAB, 128), 1)
        starts = _key_block_starts(n, nblk)
        dout = _dot(dx_ref[...].astype(BF16), wbuf[...], NT)
        prod_scr[...] = dout * att_ref[...].astype(F32)
        doutb = dout.astype(BF16)
        dq_tiles = []
        for hk in range(N_KV):
            kt = (hk // 2) * 128
            c0 = (2 * hk) * 128
            k_lo = _kv_low(k_ref, starts, hk, lo)
            v_lo = _kv_low(v_ref, starts, hk, lo)
            q4 = _stack_heads(q_ref[:, c0:c0 + 128], q_ref[:, c0 + 128:c0 + 256])
            do4 = _stack_heads(doutb[:, c0:c0 + 128], doutb[:, c0 + 128:c0 + 256])
            for jj in range(3):
                dp_scr[hk, jj] = _dot(do4, v_lo[jj], NT)
            for g in range(4):
                h = 4 * hk + g
                for r0 in range(0, CHUNK, ATT_SLAB):
                    rows = slice(g * CHUNK + r0, g * CHUNK + r0 + ATT_SLAB)
                    pt = prod_scr[r0:r0 + ATT_SLAB, c0 + (g // 2) * 128:c0 + (g // 2 + 1) * 128]
                    msk = lane_s < HEAD_DIM if g % 2 == 0 else lane_s >= HEAD_DIM
                    dsum = jnp.sum(jnp.where(msk, pt, 0.0), axis=-1, keepdims=True)
                    rest = jnp.zeros((ATT_SLAB, 128), F32) if h == 0 else dsum_scr[r0:r0 + ATT_SLAB, :]
                    dsum_scr[r0:r0 + ATT_SLAB, :] = jnp.where(lane_s == h, dsum, rest)
                    for jj in range(3):
                        ds = p_ref[hk, jj, rows, :].astype(F32) * (dp_scr[hk, jj, rows, :] - dsum)
                        dbias_ref[h, r0:r0 + ATT_SLAB, jj * CHUNK:(jj + 1) * CHUNK] += ds
                        ds_scr[hk, jj, rows, :] = ds.astype(BF16)
            dq4 = jnp.zeros((4 * CHUNK, 128), F32)
            for jj in range(3):
                ds4 = ds_scr[hk, jj]
                dq4 = dq4 + _dot(ds4, k_lo[jj], NN) * scale
                dkj = _dot(ds4, q4, TN) * scale
                dvj = _dot(p_ref[hk, jj], do4, TN)
                if hk % 2 == 1:
                    dkj, dvj = pltpu.roll(dkj, HEAD_DIM, 1), pltpu.roll(dvj, HEAD_DIM, 1)
                keep = lo if hk % 2 == 0 else hi
                dk_ref[pl.ds(starts[jj], CHUNK), kt:kt + 128] += jnp.where(keep, dkj, 0.0)
                dv_ref[pl.ds(starts[jj], CHUNK), kt:kt + 128] += jnp.where(keep, dvj, 0.0)
            dq_tiles += list(_unstack_heads(dq4))
        dq_ref[...] = jnp.concatenate(dq_tiles, axis=1).astype(BF16)
        dsink_ref[...] -= jnp.sum(ps_ref[...] * dsum_scr[...], axis=0, keepdims=True)

    blk = pl.BlockSpec((CHUNK, dm), lambda n: (n, 0))
    return dict(
        body=body, grid=(nblk,), name="attn_bwd", args=[qkv, qkv, qkv, att, probs, sink_probs, dx3, gath],
        out_shape=[jax.ShapeDtypeStruct((t, dm), BF16), jax.ShapeDtypeStruct((t, kvw), F32),
                   jax.ShapeDtypeStruct((t, kvw), F32), jax.ShapeDtypeStruct(bias_shape, F32),
                   jax.ShapeDtypeStruct((1, 128), F32)],
        in_specs=[blk, pl.BlockSpec((t, kvw), lambda n: (0, kcb)), pl.BlockSpec((t, kvw), lambda n: (0, vcb)),
                  blk, pl.BlockSpec((None,) + slab, lambda n: (n, 0, 0, 0, 0)),
                  pl.BlockSpec((CHUNK, 128), lambda n: (n, 0)), blk, ANY],
        out_specs=[blk, _full((t, kvw)), _full((t, kvw)), _full(bias_shape), _full((1, 128))],
        scratch=[pltpu.VMEM((gath.shape[1] * NDEV, dm), BF16), pltpu.VMEM(slab, F32), pltpu.VMEM(slab, BF16),
                 pltpu.VMEM((CHUNK, dm), F32), pltpu.VMEM((CHUNK, 128), F32), pltpu.SemaphoreType.DMA((NDEV,))])


def _finish_weight(recvs, w, m, v, name):
    nl, r, dm = w.shape
    assert nl == len(recvs) and all(rc.shape[1:] == (r, dm) for rc in recvs)
    td = dm // 2
    wspec = pl.BlockSpec((None, r, td), lambda l, j: (l, 0, j))

    def body(*refs):
        r_refs = refs[:nl]
        w_ref, m_ref, v_ref, g_ref, d_ref, nm_ref, nv_ref = refs[nl:]
        layer = pl.program_id(0)
        for li in range(nl):
            @pl.when(layer == li)
            def _():
                g = r_refs[li][0].astype(F32)
                for d in range(1, recvs[li].shape[0]):
                    g = g + r_refs[li][d].astype(F32)
                delta, nm, nv = _adamw_math(w_ref[...], g, m_ref[...], v_ref[...])
                g_ref[...] = g
                d_ref[...] = delta
                nm_ref[...] = nm
                nv_ref[...] = nv

    o = jax.ShapeDtypeStruct(w.shape, F32)
    return dict(
        body=body, grid=(nl, 2), name=name, args=[*recvs, w, m, v], out_shape=[o, o, o, o],
        in_specs=[pl.BlockSpec((rc.shape[0], r, td), lambda l, j: (0, 0, j)) for rc in recvs] + [wspec] * 3,
        out_specs=[wspec] * 4)


def _adamw_small(w, g_slots, late_slots, m, v, name):
    r, c = w.shape
    nlate = late_slots.shape[1]

    def body(w_ref, g_ref, late_ref, m_ref, v_ref, gs_ref, d_ref, nm_ref, nv_ref):
        g = g_ref[0]
        late = late_ref[0]
        for d in range(1, NDEV):
            g = g + g_ref[d]
            late = late + late_ref[d]
        gs_ref[...] = g
        gs_ref[0:nlate, :] = late
        d_ref[...], nm_ref[...], nv_ref[...] = _adamw_math(w_ref[...], gs_ref[...], m_ref[...], v_ref[...])

    spec = pl.BlockSpec((r, c), lambda i: (0, 0))
    out = jax.ShapeDtypeStruct((r, c), F32)
    return pl.pallas_call(
        body, grid=(1,), out_shape=(out,) * 4,
        in_specs=[spec, pl.BlockSpec((NDEV, r, c), lambda i: (0, 0, 0)),
                  pl.BlockSpec((NDEV, nlate, c), lambda i: (0, 0, 0)), spec, spec], out_specs=(spec,) * 4,
        compiler_params=_cp(), name=name)(w, g_slots, late_slots, m, v)


def _pack_small(parts, rows):
    flat = jnp.concatenate([p.reshape(-1) for p in parts])
    return jnp.pad(flat, (0, rows * 128 - flat.shape[0])).reshape(rows, 128)


def _unpack_small(packed, shapes):
    flat = packed.reshape(-1)
    out, o = [], 0
    for s in shapes:
        n = int(np.prod(s))
        out.append(flat[o:o + n].reshape(s))
        o += n
    return out


def kernel(x, norm_mix, norm_ffn, even_w_in, even_v_ln_g, even_v_ln_b, even_w_spatial, even_b_spatial, even_conv_w, even_w_out, attn_w_qkv, attn_sink, rel_bias, attn_w_out, ffn_w_gate, ffn_w_up, ffn_w_down, final_norm, loss_target, m_norm_mix, m_norm_ffn, m_even_w_in, m_even_v_ln_g, m_even_v_ln_b, m_even_w_spatial, m_even_b_spatial, m_even_conv_w, m_even_w_out, m_attn_w_qkv, m_attn_sink, m_rel_bias, m_attn_w_out, m_ffn_w_gate, m_ffn_w_up, m_ffn_w_down, m_final_norm, v_norm_mix, v_norm_ffn, v_even_w_in, v_even_v_ln_g, v_even_v_ln_b, v_even_w_spatial, v_even_b_spatial, v_even_conv_w, v_even_w_out, v_attn_w_qkv, v_attn_sink, v_rel_bias, v_attn_w_out, v_ffn_w_gate, v_ffn_w_up, v_ffn_w_down, v_final_norm):
    t, dm = x.shape[1], x.shape[2]
    aw = even_v_ln_g.shape[1]
    bw = even_conv_w.shape[2] * NDEV
    gd = aw // A_GROUPS
    tm = min(512, t // 2)
    tmf = min(256, t // 2)
    me = _my_index()
    row = lambda a: a.reshape(1, -1)

    colT = lambda w: w.T.astype(BF16)
    sh = dict(winT=colT(even_w_in[0]), wqkvT=colT(attn_w_qkv[0]), wgT0=colT(ffn_w_gate[0]), wuT0=colT(ffn_w_up[0]),
              wgT1=colT(ffn_w_gate[1]), wuT1=colT(ffn_w_up[1]), woe=even_w_out[0].astype(BF16),
              woa=attn_w_out[0].astype(BF16), wd0=ffn_w_down[0].astype(BF16), wd1=ffn_w_down[1].astype(BF16))
    gather = lambda names: _GatherCarry([sh[n] for n in names])

    in_full = lambda a: lax.dynamic_update_slice(jnp.zeros((3, bw), F32), a[0], (0, me * (bw // NDEV)))
    cw_rows = 3 * bw // 128
    cw_mine = jnp.pad(in_full(even_conv_w).reshape(cw_rows, 128), ((0, 16 - cw_rows), (0, 0)))

    x0 = x[0]
    wsp_b = even_w_spatial[0].astype(BF16)
    bspb = jnp.broadcast_to(even_b_spatial[0][:, :, None], (A_GROUPS, CHUNK, gd))
    buckets = _bucket_table()
    bias = _bias_table(rel_bias.T, buckets)
    sink = attn_sink[0]

    (g_winT,), (cw_slots,) = _exchange_only([gather(["winT"]), _BroadcastCarry(cw_mine)], "ag_w_in")
    cw_full = jnp.sum(cw_slots, axis=0)[0:cw_rows].reshape(3, bw)
    (proj, h0b), (g_woe, g_wgT0) = _call(_norm_proj(x0, row(norm_mix[0]), g_winT, F32, "in_proj", tm),
                                         gather(["woe", "wgT0"]))
    (x1, yb), (g_wuT0,) = _call(_even_core_fwd(proj, x0, even_v_ln_g, even_v_ln_b, wsp_b, bspb, cw_full, g_woe, tm),
                                gather(["wuT0"]))
    (gate0, up0, act0), (g_wd0,) = _call(_ffn_up(x1, row(norm_ffn[0]), g_wgT0, g_wuT0, "ffn_up0", tmf), gather(["wd0"]))
    (x2,), (g_wqkvT,) = _call(_ffn_down(x1, act0, g_wd0, "ffn_down0", tm), gather(["wqkvT"]))
    (qkv, h2b), (g_woa,) = _call(_norm_proj(x2, row(norm_mix[1]), g_wqkvT, BF16, "qkv_proj", tm), gather(["woa"]))
    (x3, attb, probs, sink_probs), (g_wgT1, g_wuT1) = _call(
        _attn_fwd(qkv, x2, bias, sink, g_woa), gather(["wgT1", "wuT1"]))
    (gate1, up1, act1), (g_wd1,) = _call(_ffn_up(x3, row(norm_ffn[1]), g_wgT1, g_wuT1, "ffn_up1", tmf), gather(["wd1"]))
    (loss_part, dx4, dx4b, d_final), _ = _call(
        _ffn_down_loss(x3, act1, g_wd1, loss_target[0], row(final_norm), "ffn_down1_loss", tm))

    (dx3, dx3b, dg1, du1, h3b, d_nffn1), _ = _call(
        _ffn_bwd(dx4, x3, gate1, up1, row(norm_ffn[1]), g_wgT1, g_wuT1, g_wd1, "ffn_bwd1", tmf))
    (p_wgT1,), _ = _call(_wgrad(dg1, h3b, "wgrad_gate1"))
    (p_wuT1,), _ = _call(_wgrad(du1, h3b, "wgrad_up1"))
    (p_wd1,), _ = _call(_wgrad(act1, dx4b, "wgrad_down1"))
    (dq, dk, dv, dbias, dsink), (r_wgT1, r_wuT1, r_wd1) = _call(
        _attn_bwd(qkv, attb, probs, sink_probs, dx3, bias.shape, g_woa), _GradCarry([p_wgT1, p_wuT1, p_wd1]))
    (p_woa,), _ = _call(_wgrad(attb, dx3b, "wgrad_attn_out"))
    d_relb = _rel_bias_grad(dbias, buckets)[:, 0:N_BUCKETS].T
    dqkv = jnp.concatenate([dq, dk.astype(BF16), dv.astype(BF16)], axis=1)
    (dx2, dx2b, d_nmix1), _ = _call(_proj_bwd_norm(dqkv, x2, row(norm_mix[1]), dx3, g_wqkvT, "qkv_bwd", tm))
    (p_wqkvT,), _ = _call(_wgrad(dqkv, h2b, "wgrad_qkv"))
    (dx1, dx1b, dg0, du0, h1b, d_nffn0), (r_woa, r_wqkvT) = _call(
        _ffn_bwd(dx2, x1, gate0, up0, row(norm_ffn[0]), g_wgT0, g_wuT0, g_wd0, "ffn_bwd0", tmf),
        _GradCarry([p_woa, p_wqkvT]))
    (p_wgT0,), _ = _call(_wgrad(dg0, h1b, "wgrad_gate0"))
    (p_wuT0,), (a_wgT0,) = _call(_wgrad(du0, h1b, "wgrad_up0"), _PairCarry(p_wgT0))
    (dproj, d_lng, d_lnb, d_wsp, d_bsp3, d_cw), ((r_wgT0,), (a_wuT0,)) = _call(
        _even_core_bwd(proj, dx1, even_v_ln_g, even_v_ln_b, wsp_b, bspb, cw_full, g_woe, tm),
        [_ChipSumCarry(p_wgT0, a_wgT0), _PairCarry(p_wuT0)])
    (p_winT,), (r_wuT0,) = _call(_wgrad(dproj, h0b, "wgrad_in"), _ChipSumCarry(p_wuT0, a_wuT0))
    small_shapes = [(2, dm), (2, dm), (1, aw), (1, aw), (1, A_GROUPS, CHUNK, CHUNK), (1, A_GROUPS, CHUNK), (3, bw),
                    (1, N_HEADS), (N_BUCKETS, N_HEADS), (dm,), (1, 1)]
    small_names = ["norm_mix", "norm_ffn", "even_v_ln_g", "even_v_ln_b", "even_w_spatial", "even_b_spatial",
                   "even_conv_w", "attn_sink", "rel_bias", "final_norm", "loss"]
    n_small = sum(int(np.prod(s)) for s in small_shapes)
    small_rows = 8 * ((n_small + 1023) // 1024)
    assert dm == 8 * 128
    small_part = _pack_small(
        [jnp.concatenate([jnp.zeros_like(d_nmix1), d_nmix1]), jnp.concatenate([d_nffn0, d_nffn1]), d_lng, d_lnb,
         d_wsp, jnp.sum(d_bsp3, axis=-1), d_cw, dsink[:, 0:N_HEADS], d_relb, d_final, loss_part], small_rows)
    (p_wd0,), ((a_winT,), (small_slots,)) = _call(
        _wgrad(act0, dx2b, "wgrad_down0"), [_PairCarry(p_winT), _BroadcastCarry(small_part)])
    (p_woe,), ((r_winT,), (a_wd0,)) = _call(
        _wgrad(yb, dx1b, "wgrad_even_out"), [_ChipSumCarry(p_winT, a_winT), _PairCarry(p_wd0)])
    (dx0, _, d_nmix0), ((r_wd0,), (r_woe,)) = _call(
        _proj_bwd_norm(dproj, x0, row(norm_mix[0]), dx1, g_winT, "in_proj_bwd", tm),
        [_ChipSumCarry(p_wd0, a_wd0), _GradCarry([p_woe])])

    grads = {}

    order = ["norm_mix", "norm_ffn", "even_w_in", "even_v_ln_g", "even_v_ln_b", "even_w_spatial", "even_b_spatial",
             "even_conv_w", "even_w_out", "attn_w_qkv", "attn_sink", "rel_bias", "attn_w_out", "ffn_w_gate",
             "ffn_w_up", "ffn_w_down", "final_norm"]
    ws = dict(norm_mix=norm_mix, norm_ffn=norm_ffn, even_w_in=even_w_in, even_v_ln_g=even_v_ln_g,
              even_v_ln_b=even_v_ln_b, even_w_spatial=even_w_spatial, even_b_spatial=even_b_spatial,
              even_conv_w=even_conv_w, even_w_out=even_w_out, attn_w_qkv=attn_w_qkv, attn_sink=attn_sink,
              rel_bias=rel_bias, attn_w_out=attn_w_out, ffn_w_gate=ffn_w_gate, ffn_w_up=ffn_w_up,
              ffn_w_down=ffn_w_down, final_norm=final_norm)
    ms = dict(norm_mix=m_norm_mix, norm_ffn=m_norm_ffn, even_w_in=m_even_w_in, even_v_ln_g=m_even_v_ln_g,
              even_v_ln_b=m_even_v_ln_b, even_w_spatial=m_even_w_spatial, even_b_spatial=m_even_b_spatial,
              even_conv_w=m_even_conv_w, even_w_out=m_even_w_out, attn_w_qkv=m_attn_w_qkv, attn_sink=m_attn_sink,
              rel_bias=m_rel_bias, attn_w_out=m_attn_w_out, ffn_w_gate=m_ffn_w_gate, ffn_w_up=m_ffn_w_up,
              ffn_w_down=m_ffn_w_down, final_norm=m_final_norm)
    vs = dict(norm_mix=v_norm_mix, norm_ffn=v_norm_ffn, even_w_in=v_even_w_in, even_v_ln_g=v_even_v_ln_g,
              even_v_ln_b=v_even_v_ln_b, even_w_spatial=v_even_w_spatial, even_b_spatial=v_even_b_spatial,
              even_conv_w=v_even_conv_w, even_w_out=v_even_w_out, attn_w_qkv=v_attn_w_qkv, attn_sink=v_attn_sink,
              rel_bias=v_rel_bias, attn_w_out=v_attn_w_out, ffn_w_gate=v_ffn_w_gate, ffn_w_up=v_ffn_w_up,
              ffn_w_down=v_ffn_w_down, final_norm=v_final_norm)
    big = dict(ffn_w_gate=([r_wgT0, r_wgT1], True), even_w_in=([r_winT], True), even_w_out=([r_woe], False),
               attn_w_qkv=([r_wqkvT], True), attn_w_out=([r_woa], False), ffn_w_up=([r_wuT0, r_wuT1], True),
               ffn_w_down=([r_wd0, r_wd1], False))
    delta, new_m, new_v = {}, {}, {}
    late_slots = None
    for n, (recvs, transposed) in big.items():
        lay = (lambda a: jnp.swapaxes(a, 1, 2)) if transposed else (lambda a: a)
        spec = _finish_weight(recvs, lay(ws[n]), lay(ms[n]), lay(vs[n]), "finish_" + n)
        if late_slots is None:
            outs, (late_slots,) = _call(spec, _BroadcastCarry(d_nmix0.reshape(8, 128)))
        else:
            outs, _ = _call(spec)
        grads[n], delta[n], new_m[n], new_v[n] = [lay(o) for o in outs]
    pk = lambda dct: _pack_small(
        [in_full(dct[n]) if n == "even_conv_w" else (jnp.zeros((1, 1), F32) if n == "loss" else dct[n])
         for n in small_names], small_rows)
    packed = _adamw_small(pk(ws), small_slots, late_slots, pk(ms), pk(vs), "adamw_small")
    mine = lambda a: lax.dynamic_slice(a, (0, me * (bw // NDEV)), (3, bw // NDEV))[None]
    for dst, arr in zip((grads, delta, new_m, new_v), packed):
        for n, a in zip(small_names, _unpack_small(arr, small_shapes)):
            dst[n] = mine(a) if n == "even_conv_w" else a
    loss = grads["loss"][0, 0]
    return (loss, dx0[None], *[grads[n] for n in order], *[delta[n] for n in order],
            *[new_m[n] for n in order], *[new_v[n] for n in order])
```

```python
import math

import jax
import jax.numpy as jnp
import numpy as np
from jax import lax
from jax.experimental import pallas as pl
from jax.experimental.pallas import tpu as pltpu

F32, BF16 = jnp.float32, jnp.bfloat16
NDEV = 8
EPS = 1e-6
CHUNK = 128
A_GROUPS = 4
N_HEADS, N_KV, HEAD_DIM = 16, 4, 64
N_BUCKETS, MAX_DISTANCE = 32, 128
NEG = -1e30
LOG2E = 1.4426950408889634
ADAM_LR, ADAM_B1, ADAM_B2, ADAM_EPS, ADAM_WD, ADAM_STEP = 0.001, 0.9, 0.999, 1e-08, 0.01, 10
VMEM_LIMIT = 56 * 1024 * 1024
MESH = pl.DeviceIdType.MESH
NT = (((1,), (1,)), ((), ()))
NN = (((1,), (0,)), ((), ()))
TN = (((0,), (0,)), ((), ()))
ANY = pl.BlockSpec(memory_space=pl.ANY)


def _cp(n_grid=1):
    return pltpu.CompilerParams(dimension_semantics=("arbitrary",) * n_grid, vmem_limit_bytes=VMEM_LIMIT)


def _dot(a, b, dims):
    return lax.dot_general(a, b, dims, preferred_element_type=F32)


def _my_index():
    return 4 * lax.axis_index("x") + 2 * lax.axis_index("y") + lax.axis_index("c")


def _peer(k):
    x, y, c = lax.axis_index("x"), lax.axis_index("y"), lax.axis_index("c")
    px = 1 - x if k & 4 else x
    py = 1 - y if k & 2 else y
    pc = 1 - c if k & 1 else c
    return (px, py, pc)


def _load_weight(gath_ref, wbuf, sems):
    rows = gath_ref.shape[1]
    cps = [pltpu.make_async_copy(gath_ref.at[d], wbuf.at[pl.ds(d * rows, rows), :], sems.at[d]) for d in range(NDEV)]
    for c in cps:
        c.start()
    for c in cps:
        c.wait()


class _GatherCarry:
    def __init__(self, pieces):
        self.inputs = list(pieces)
        self.n = len(pieces)
        self.out_shape = [jax.ShapeDtypeStruct((NDEV,) + p.shape, p.dtype) for p in pieces]
        self.scratch = [pltpu.SemaphoreType.DMA((7 * self.n,)), pltpu.SemaphoreType.DMA((7 * self.n,)),
                        pltpu.SemaphoreType.DMA((self.n,))]

    def _ctx(self):
        x, y, c = lax.axis_index("x"), lax.axis_index("y"), lax.axis_index("c")
        chips = [(1 - x, y), (x, 1 - y), (1 - x, 1 - y)]
        return (x, y, c), (x, y, 1 - c), chips, c

    def _copy(self, k, j, block, to, ins, outs, sems, src=None):
        send_sems, recv_sems, _ = sems
        slot = outs[j].at[4 * block[0] + 2 * block[1] + block[2]]
        return pltpu.make_async_remote_copy(
            src_ref=slot if src is None else src, dst_ref=slot, send_sem=send_sems.at[k * self.n + j],
            recv_sem=recv_sems.at[k * self.n + j], device_id=to, device_id_type=MESH)

    def start(self, ins, outs, sems):
        me, sibling, chips, c = self._ctx()
        for j in range(self.n):
            pltpu.make_async_copy(ins[j], outs[j].at[4 * me[0] + 2 * me[1] + me[2]], sems[2].at[j]).start()
            self._copy(0, j, me, sibling, ins, outs, sems, src=ins[j]).start()
            for q, chip in enumerate(chips):
                self._copy(1 + q, j, me, (*chip, c), ins, outs, sems, src=ins[j]).start()

    def mid(self, ins, outs, sems):
        me, sibling, chips, c = self._ctx()
        for q, chip in enumerate(chips):
            for j in range(self.n):
                self._copy(1 + q, j, (*chip, c), me, ins, outs, sems).wait_recv()
                self._copy(4 + q, j, (*chip, c), sibling, ins, outs, sems).start()

    def finish(self, ins, outs, sems):
        me, sibling, chips, c = self._ctx()
        for j in range(self.n):
            self._copy(0, j, sibling, me, ins, outs, sems).wait_recv()
            for q, chip in enumerate(chips):
                self._copy(4 + q, j, (*chip, 1 - c), me, ins, outs, sems).wait_recv()
        for j in range(self.n):
            self._copy(0, j, me, sibling, ins, outs, sems, src=ins[j]).wait_send()
            for q, chip in enumerate(chips):
                self._copy(1 + q, j, me, (*chip, c), ins, outs, sems, src=ins[j]).wait_send()
                self._copy(4 + q, j, (*chip, c), sibling, ins, outs, sems).wait_send()
            pltpu.make_async_copy(ins[j], outs[j].at[0], sems[2].at[j]).wait()


class _GradCarry:
    def __init__(self, pieces):
        self.inputs = list(pieces)
        self.n = len(pieces)
        self.rows = [p.shape[0] // NDEV for p in pieces]
        self.out_shape = [jax.ShapeDtypeStruct((NDEV, r, p.shape[1]), p.dtype) for p, r in zip(pieces, self.rows)]
        self.scratch = [pltpu.SemaphoreType.DMA((7 * self.n,)), pltpu.SemaphoreType.DMA((7 * self.n,)),
                        pltpu.SemaphoreType.DMA((self.n,))]

    def _copies(self, ins, outs, sems):
        me = _my_index()
        local, remote = [], []
        for j in range(self.n):
            r = self.rows[j]
            local.append(pltpu.make_async_copy(ins[j].at[pl.ds(pl.multiple_of(me * r, 16), r), :], outs[j].at[me],
                                               sems[2].at[j]))
            for k in range(1, NDEV):
                peer = _peer(k)
                pidx = 4 * peer[0] + 2 * peer[1] + peer[2]
                remote.append(pltpu.make_async_remote_copy(
                    src_ref=ins[j].at[pl.ds(pl.multiple_of(pidx * r, 16), r), :], dst_ref=outs[j].at[me],
                    send_sem=sems[0].at[(k - 1) * self.n + j], recv_sem=sems[1].at[(k - 1) * self.n + j],
                    device_id=peer, device_id_type=MESH))
        return local, remote

    def start(self, ins, outs, sems):
        local, remote = self._copies(ins, outs, sems)
        for cp in local + remote:
            cp.start()

    def mid(self, ins, outs, sems):
        pass

    def finish(self, ins, outs, sems):
        local, remote = self._copies(ins, outs, sems)
        for cp in remote + local:
            cp.wait()


class _BroadcastCarry:
    def __init__(self, part):
        self.inputs = [part]
        self.out_shape = [jax.ShapeDtypeStruct((NDEV,) + part.shape, part.dtype)]
        self.scratch = [pltpu.SemaphoreType.DMA((7,)), pltpu.SemaphoreType.DMA((7,)), pltpu.SemaphoreType.DMA(())]

    def _copies(self, ins, outs, sems):
        me = _my_index()
        local = pltpu.make_async_copy(ins[0], outs[0].at[me], sems[2])
        remote = [pltpu.make_async_remote_copy(
            src_ref=ins[0], dst_ref=outs[0].at[me], send_sem=sems[0].at[k - 1], recv_sem=sems[1].at[k - 1],
            device_id=_peer(k), device_id_type=MESH) for k in range(1, NDEV)]
        return [local] + remote

    def start(self, ins, outs, sems):
        for cp in self._copies(ins, outs, sems):
            cp.start()

    def mid(self, ins, outs, sems):
        pass

    def finish(self, ins, outs, sems):
        for cp in self._copies(ins, outs, sems):
            cp.wait()


class _PairCarry:
    def __init__(self, piece):
        self.inputs = [piece]
        self.r = piece.shape[0] // NDEV
        self.out_shape = [jax.ShapeDtypeStruct((4, self.r, piece.shape[1]), piece.dtype)]
        self.scratch = [pltpu.SemaphoreType.DMA((4,)), pltpu.SemaphoreType.DMA((4,))]

    def _copies(self, ins, outs, sems):
        x, y, c = lax.axis_index("x"), lax.axis_index("y"), lax.axis_index("c")
        return [pltpu.make_async_remote_copy(
            src_ref=ins[0].at[pl.ds(pl.multiple_of((2 * q + 1 - c) * self.r, 16), self.r), :], dst_ref=outs[0].at[q],
            send_sem=sems[0].at[q], recv_sem=sems[1].at[q], device_id=(x, y, 1 - c), device_id_type=MESH)
            for q in range(4)]

    def start(self, ins, outs, sems):
        for cp in self._copies(ins, outs, sems):
            cp.start()

    def mid(self, ins, outs, sems):
        pass

    def finish(self, ins, outs, sems):
        for cp in self._copies(ins, outs, sems):
            cp.wait()


class _ChipSumCarry:
    def __init__(self, piece, landed):
        self.inputs = [piece, landed]
        self.r, dm = piece.shape[0] // NDEV, piece.shape[1]
        self.out_shape = [jax.ShapeDtypeStruct((4, self.r, dm), piece.dtype)]
        self.scratch = [pltpu.VMEM((4, self.r, dm), piece.dtype), pltpu.VMEM((8, self.r, dm), piece.dtype),
                        pltpu.SemaphoreType.DMA((8,)), pltpu.SemaphoreType.DMA((3,)), pltpu.SemaphoreType.DMA((3,)),
                        pltpu.SemaphoreType.DMA(())]

    def _copies(self, outs, scr):
        sums, _, _, send_sems, recv_sems, local_sem = scr
        x, y, c = lax.axis_index("x"), lax.axis_index("y"), lax.axis_index("c")
        mine = 2 * x + y
        local = pltpu.make_async_copy(sums.at[mine], outs[0].at[mine], local_sem)
        remote = []
        for k in range(1, 4):
            px = 1 - x if k & 2 else x
            py = 1 - y if k & 1 else y
            remote.append(pltpu.make_async_remote_copy(
                src_ref=sums.at[2 * px + py], dst_ref=outs[0].at[mine], send_sem=send_sems.at[k - 1],
                recv_sem=recv_sems.at[k - 1], device_id=(px, py, c), device_id_type=MESH))
        return local, remote

    def start(self, ins, outs, scr):
        sums, stage, stage_sems = scr[0], scr[1], scr[2]
        c = lax.axis_index("c")
        loads = []
        for q in range(4):
            loads.append((
                pltpu.make_async_copy(ins[0].at[pl.ds(pl.multiple_of((2 * q + c) * self.r, 16), self.r), :],
                                      stage.at[2 * q], stage_sems.at[2 * q]),
                pltpu.make_async_copy(ins[1].at[q], stage.at[2 * q + 1], stage_sems.at[2 * q + 1])))
        for a, b in loads:
            a.start()
            b.start()
        for q, (a, b) in enumerate(loads):
            a.wait()
            b.wait()
            sums[q] = (stage[2 * q].astype(F32) + stage[2 * q + 1].astype(F32)).astype(sums.dtype)
        local, remote = self._copies(outs, scr)
        for cp in [local] + remote:
            cp.start()

    def mid(self, ins, outs, scr):
        pass

    def finish(self, ins, outs, scr):
        local, remote = self._copies(outs, scr)
        for cp in remote + [local]:
            cp.wait()


def _call(spec, carry=None):
    body, grid = spec["body"], spec["grid"]
    in_specs, out_specs, out_shape = list(spec["in_specs"]), list(spec["out_specs"]), list(spec["out_shape"])
    scratch, args = list(spec.get("scratch", [])), list(spec["args"])
    if carry is None:
        out = pl.pallas_call(body, grid=grid, in_specs=in_specs, out_specs=tuple(out_specs),
                             out_shape=tuple(out_shape), scratch_shapes=scratch, compiler_params=_cp(len(grid)),
                             name=spec["name"])(*args)
        return tuple(out), ()
    carries = list(carry) if isinstance(carry, (list, tuple)) else [carry]
    n_in, n_out, n_s = len(in_specs), len(out_specs), len(scratch)
    steps = int(np.prod(grid))

    def split(refs, counts):
        parts, o = [], 0
        for cnt in counts:
            parts.append(refs[o:o + cnt])
            o += cnt
        return parts

    c_in = [len(cr.inputs) for cr in carries]
    c_out = [len(cr.out_shape) for cr in carries]
    c_scr = [len(cr.scratch) for cr in carries]

    def wrapped(*refs):
        ins, cins, outs, couts, scr, cscr = split(refs, [n_in, sum(c_in), n_out, sum(c_out), n_s, sum(c_scr)])
        per = list(zip(carries, split(cins, c_in), split(couts, c_out), split(cscr, c_scr)))
        step = pl.program_id(0)
        for ax in range(1, len(grid)):
            step = step * grid[ax] + pl.program_id(ax)

        @pl.when(step == 0)
        def _():
            for cr, ci, co, cs in per:
                cr.start(ci, co, cs)
        if steps >= 3:
            @pl.when(step == steps - 2)
            def _():
                for cr, ci, co, cs in per:
                    cr.mid(ci, co, cs)
        body(*ins, *outs, *scr)

        @pl.when(step == steps - 1)
        def _():
            for cr, ci, co, cs in per:
                if steps < 3:
                    cr.mid(ci, co, cs)
                cr.finish(ci, co, cs)

    out = pl.pallas_call(
        wrapped, grid=grid, in_specs=in_specs + [ANY] * sum(c_in), out_specs=tuple(out_specs + [ANY] * sum(c_out)),
        out_shape=tuple(out_shape + [s for cr in carries for s in cr.out_shape]),
        scratch_shapes=scratch + [s for cr in carries for s in cr.scratch],
        compiler_params=_cp(len(grid)), name=spec["name"])(*args, *[a for cr in carries for a in cr.inputs])
    c_res = [tuple(p) for p in split(out[n_out:], c_out)]
    return tuple(out[:n_out]), (c_res if isinstance(carry, (list, tuple)) else c_res[0])


def _exchange_only(carry, name):
    spec = dict(body=lambda: None, grid=(1,), in_specs=[], out_specs=[], out_shape=[], args=[], name=name)
    return _call(spec, carry)[1]


def _rms_fwd(x, gain):
    r = lax.rsqrt(jnp.mean(x * x, axis=-1, keepdims=True) + EPS)
    return x * r * gain, r


def _rms_bwd(dh, x, r, gain):
    a = dh * gain
    dx = r * a - x * (r * r * r) * jnp.mean(a * x, axis=-1, keepdims=True)
    dgain = jnp.sum(dh * (x * r), axis=0, keepdims=True)
    return dx, dgain


def _gelu(x):
    return 0.5 * x * (1.0 + lax.erf(x * 0.7071067811865476))


def _gelu_grad(x):
    return 0.5 * (1.0 + lax.erf(x * 0.7071067811865476)) + x * jnp.exp(-0.5 * x * x) * 0.3989422804014327


def _sigmoid(x):
    return 1.0 / (1.0 + jnp.exp(-x))


def _adamw_math(w, g, m, v):
    nm = ADAM_B1 * m + (1.0 - ADAM_B1) * g
    nv = ADAM_B2 * v + (1.0 - ADAM_B2) * (g * g)
    m_hat = nm / (1.0 - ADAM_B1 ** ADAM_STEP)
    v_hat = nv / (1.0 - ADAM_B2 ** ADAM_STEP)
    return -ADAM_LR * (m_hat / (jnp.sqrt(v_hat) + ADAM_EPS) + ADAM_WD * w), nm, nv


def _tok(tm, w):
    return pl.BlockSpec((tm, w), lambda i: (i, 0))


def _full(shape):
    return pl.BlockSpec(shape, lambda *i: (0,) * len(shape))


def _norm_proj(x, gain, gath, out_dtype, name, tm):
    t, dm = x.shape
    n = gath.shape[1] * NDEV

    def body(x_ref, g_ref, gath_ref, proj_ref, hb_ref, wbuf, sems):
        @pl.when(pl.program_id(0) == 0)
        def _():
            _load_weight(gath_ref, wbuf, sems)
        h, _ = _rms_fwd(x_ref[...], g_ref[...])
        hb = h.astype(BF16)
        hb_ref[...] = hb
        proj_ref[...] = _dot(hb, wbuf[...], NT).astype(out_dtype)

    return dict(
        body=body, grid=(t // tm,), name=name, args=[x, gain, gath],
        out_shape=[jax.ShapeDtypeStruct((t, n), out_dtype), jax.ShapeDtypeStruct((t, dm), BF16)],
        in_specs=[_tok(tm, dm), _full((1, dm)), ANY], out_specs=[_tok(tm, n), _tok(tm, dm)],
        scratch=[pltpu.VMEM((n, dm), BF16), pltpu.SemaphoreType.DMA((NDEV,))])


def _proj_bwd_norm(dy, x, gain, dres, gath, name, tm):
    t, dm = x.shape
    n = gath.shape[1] * NDEV

    def body(dy_ref, x_ref, g_ref, dres_ref, gath_ref, dx_ref, dxb_ref, dgain_ref, wbuf, sems):
        @pl.when(pl.program_id(0) == 0)
        def _():
            _load_weight(gath_ref, wbuf, sems)
            dgain_ref[...] = jnp.zeros_like(dgain_ref)
        xv, gain_v = x_ref[...], g_ref[...]
        _, r = _rms_fwd(xv, gain_v)
        dh = _dot(dy_ref[...], wbuf[...], NN)
        dx, dgain = _rms_bwd(dh, xv, r, gain_v)
        dx = dres_ref[...] + dx
        dx_ref[...] = dx
        dxb_ref[...] = dx.astype(BF16)
        dgain_ref[...] += dgain

    return dict(
        body=body, grid=(t // tm,), name=name, args=[dy, x, gain, dres, gath],
        out_shape=[jax.ShapeDtypeStruct((t, dm), F32), jax.ShapeDtypeStruct((t, dm), BF16),
                   jax.ShapeDtypeStruct((1, dm), F32)],
        in_specs=[_tok(tm, n), _tok(tm, dm), _full((1, dm)), _tok(tm, dm), ANY],
        out_specs=[_tok(tm, dm), _tok(tm, dm), _full((1, dm))],
        scratch=[pltpu.VMEM((n, dm), BF16), pltpu.SemaphoreType.DMA((NDEV,))])


def _wgrad(a, b, name, tmm=256):
    t, m = a.shape
    n = b.shape[1]

    def body(a_ref, b_ref, o_ref):
        o_ref[...] = _dot(a_ref[...], b_ref[...], TN).astype(BF16)

    return dict(
        body=body, grid=(m // tmm,), name=name, args=[a, b], out_shape=[jax.ShapeDtypeStruct((m, n), BF16)],
        in_specs=[pl.BlockSpec((t, tmm), lambda j: (0, j)), pl.BlockSpec((t, n), lambda j: (0, 0))],
        out_specs=[pl.BlockSpec((tmm, n), lambda j: (j, 0))])


def _halo_specs(tm, t, width, col_blocks):
    nb8 = tm // 8
    last = t // 8 - 1
    prev = [pl.BlockSpec((8, width), lambda i, cb=cb: (jnp.maximum(i * nb8 - 1, 0), cb)) for cb in col_blocks]
    nxt = [pl.BlockSpec((8, width), lambda i, cb=cb: (jnp.minimum((i + 1) * nb8, last), cb)) for cb in col_blocks]
    return prev, nxt


def _shift_rows(z, prev_row, next_row):
    tm = z.shape[0]
    row = lax.broadcasted_iota(jnp.int32, z.shape, 0)
    zm1 = jnp.where(row == 0, prev_row, pltpu.roll(z, 1, 0))
    zp1 = jnp.where(row == tm - 1, next_row, pltpu.roll(z, tm - 1, 0))
    return zm1, zp1


def _gating_fwd(proj, lng, lnb, wsp_ref, bsp_ref, aw):
    tm = proj.shape[0]
    a_u = _gelu(proj[:, 0:aw])
    gv = _gelu(proj[:, aw:2 * aw])
    mu = jnp.mean(gv, axis=-1, keepdims=True)
    xc = gv - mu
    rstd = lax.rsqrt(jnp.mean(xc * xc, axis=-1, keepdims=True) + EPS)
    vn = xc * rstd
    a_v = (vn * lng + lnb).astype(BF16)
    gd = aw // A_GROUPS
    rows = []
    for c in range(tm // CHUNK):
        cols = []
        for g in range(A_GROUPS):
            blk = a_v[c * CHUNK:(c + 1) * CHUNK, g * gd:(g + 1) * gd]
            cols.append(_dot(wsp_ref[g], blk, NN) + bsp_ref[g])
        rows.append(jnp.concatenate(cols, axis=1))
    mixed = jnp.concatenate(rows, axis=0)
    return a_u, vn, rstd, a_v, mixed


def _even_core_fwd(proj, x0, lng, lnb, wsp, bspb, cw, gath, tm):
    t, dm = x0.shape
    aw = lng.shape[1]
    bw = cw.shape[1]
    assert aw == bw and 2 * aw + 3 * bw == proj.shape[1]
    nt = t // tm
    prev, nxt = _halo_specs(tm, t, bw, [3, 4])

    def body(proj_ref, cp_ref, hp_ref, cn_ref, hn_ref, x0_ref, lng_ref, lnb_ref, wsp_ref, bsp_ref, cw_ref, gath_ref,
             x1_ref, y_ref, wbuf, sems):
        i = pl.program_id(0)

        @pl.when(i == 0)
        def _():
            _load_weight(gath_ref, wbuf, sems)
        proj_v = proj_ref[...]
        a_u, _, _, _, mixed = _gating_fwd(proj_v, lng_ref[...], lnb_ref[...], wsp_ref, bsp_ref, aw)
        a_out = a_u * mixed
        bb = proj_v[:, 2 * aw:2 * aw + bw]
        z = proj_v[:, 2 * aw + bw:2 * aw + 2 * bw] * proj_v[:, 2 * aw + 2 * bw:]
        zprev = jnp.where(i > 0, cp_ref[7:8, :] * hp_ref[7:8, :], 0.0)
        znext = jnp.where(i < nt - 1, cn_ref[0:1, :] * hn_ref[0:1, :], 0.0)
        zm1, zp1 = _shift_rows(z, zprev, znext)
        cwv = cw_ref[...]
        conv = zm1 * cwv[0:1, :] + z * cwv[1:2, :] + zp1 * cwv[2:3, :]
        y = jnp.concatenate([a_out, bb * conv], axis=1).astype(BF16)
        y_ref[...] = y
        x1_ref[...] = x0_ref[...] + _dot(y, wbuf[...], NN)

    return dict(
        body=body, grid=(nt,), name="even_core_fwd",
        args=[proj, proj, proj, proj, proj, x0, lng, lnb, wsp, bspb, cw, gath],
        out_shape=[jax.ShapeDtypeStruct((t, dm), F32), jax.ShapeDtypeStruct((t, aw + bw), BF16)],
        in_specs=[_tok(tm, proj.shape[1]), prev[0], prev[1], nxt[0], nxt[1], _tok(tm, dm), _full(lng.shape),
                  _full(lnb.shape), _full(wsp.shape), _full(bspb.shape), _full(cw.shape), ANY],
        out_specs=[_tok(tm, dm), _tok(tm, aw + bw)],
        scratch=[pltpu.VMEM((gath.shape[1] * NDEV, dm), BF16), pltpu.SemaphoreType.DMA((NDEV,))])


def _even_core_bwd(proj, dx1, lng, lnb, wsp, bspb, cw, gath, tm):
    t, dm = dx1.shape
    aw, bw = lng.shape[1], cw.shape[1]
    gd = aw // A_GROUPS
    nt = t // tm
    inw = proj.shape[1]
    prev, nxt = _halo_specs(tm, t, bw, [2, 3, 4])
    nb8 = tm // 8
    last8 = t // 8 - 1

    def body(proj_ref, bp_ref, cp_ref, hp_ref, bn_ref, cn_ref, hn_ref, dx_ref, dxp_ref, dxn_ref,
             lng_ref, lnb_ref, wsp_ref, bsp_ref, cw_ref, gath_ref,
             dproj_ref, dlng_ref, dlnb_ref, dwsp_ref, dbsp_ref, dcw_ref, wbuf, sems):
        i = pl.program_id(0)

        @pl.when(i == 0)
        def _():
            _load_weight(gath_ref, wbuf, sems)
            dlng_ref[...] = jnp.zeros_like(dlng_ref)
            dlnb_ref[...] = jnp.zeros_like(dlnb_ref)
            dwsp_ref[...] = jnp.zeros_like(dwsp_ref)
            dbsp_ref[...] = jnp.zeros_like(dbsp_ref)
            dcw_ref[...] = jnp.zeros_like(dcw_ref)
        proj_v = proj_ref[...]
        lng_v = lng_ref[...]
        a_u, vn, rstd, a_v, mixed = _gating_fwd(proj_v, lng_v, lnb_ref[...], wsp_ref, bsp_ref, aw)
        w = wbuf[...]
        dy = _dot(dx_ref[...].astype(BF16), w, NT)
        da_out, db_out = dy[:, 0:aw], dy[:, aw:]
        da_u = da_out * mixed
        dmixed = da_out * a_u
        dmb = dmixed.astype(BF16)
        rows = []
        for c in range(tm // CHUNK):
            cols = []
            for g in range(A_GROUPS):
                r0, c0 = c * CHUNK, g * gd
                dm_cg = dmb[r0:r0 + CHUNK, c0:c0 + gd]
                cols.append(_dot(wsp_ref[g], dm_cg, TN))
                dwsp_ref[g] += _dot(dm_cg, a_v[r0:r0 + CHUNK, c0:c0 + gd], NT)
                dbsp_ref[g] += dmixed[r0:r0 + CHUNK, c0:c0 + gd]
            rows.append(jnp.concatenate(cols, axis=1))
        dav = jnp.concatenate(rows, axis=0)
        dlng_ref[...] += jnp.sum(dav * vn, axis=0, keepdims=True)
        dlnb_ref[...] += jnp.sum(dav, axis=0, keepdims=True)
        dvn = dav * lng_v
        dgv = rstd * (dvn - jnp.mean(dvn, axis=-1, keepdims=True) - vn * jnp.mean(dvn * vn, axis=-1, keepdims=True))
        dv_pre = dgv * _gelu_grad(proj_v[:, aw:2 * aw])
        du_pre = da_u * _gelu_grad(proj_v[:, 0:aw])
        bb = proj_v[:, 2 * aw:2 * aw + bw]
        bc = proj_v[:, 2 * aw + bw:2 * aw + 2 * bw]
        bh = proj_v[:, 2 * aw + 2 * bw:]
        z = bc * bh
        zprev = jnp.where(i > 0, cp_ref[7:8, :] * hp_ref[7:8, :], 0.0)
        znext = jnp.where(i < nt - 1, cn_ref[0:1, :] * hn_ref[0:1, :], 0.0)
        zm1, zp1 = _shift_rows(z, zprev, znext)
        cwv = cw_ref[...]
        conv = zm1 * cwv[0:1, :] + z * cwv[1:2, :] + zp1 * cwv[2:3, :]
        dbb = db_out * conv
        dconv = db_out * bb
        dx_edge = jnp.concatenate([dxp_ref[...], dxn_ref[...]], axis=0).astype(BF16)
        dy_edge = _dot(dx_edge, w[aw:, :], NT)
        dcprev = jnp.where(i > 0, dy_edge[7:8, :] * bp_ref[7:8, :], 0.0)
        dcnext = jnp.where(i < nt - 1, dy_edge[8:9, :] * bn_ref[0:1, :], 0.0)
        dcm1, dcp1 = _shift_rows(dconv, dcprev, dcnext)
        dz = dcp1 * cwv[0:1, :] + dconv * cwv[1:2, :] + dcm1 * cwv[2:3, :]
        dcw_ref[0:1, :] += jnp.sum(dconv * zm1, axis=0, keepdims=True)
        dcw_ref[1:2, :] += jnp.sum(dconv * z, axis=0, keepdims=True)
        dcw_ref[2:3, :] += jnp.sum(dconv * zp1, axis=0, keepdims=True)
        dproj_ref[...] = jnp.concatenate([du_pre, dv_pre, dbb, dz * bh, dz * bc], axis=1).astype(BF16)

    row8 = lambda f: pl.BlockSpec((8, dm), f)
    return dict(
        body=body, grid=(nt,), name="even_core_bwd",
        args=[proj, proj, proj, proj, proj, proj, proj, dx1, dx1, dx1, lng, lnb, wsp, bspb, cw, gath],
        out_shape=[jax.ShapeDtypeStruct((t, inw), BF16), jax.ShapeDtypeStruct((1, aw), F32),
                   jax.ShapeDtypeStruct((1, aw), F32), jax.ShapeDtypeStruct(wsp.shape, F32),
                   jax.ShapeDtypeStruct((A_GROUPS, CHUNK, gd), F32), jax.ShapeDtypeStruct(cw.shape, F32)],
        in_specs=[_tok(tm, inw), prev[0], prev[1], prev[2], nxt[0], nxt[1], nxt[2], _tok(tm, dm),
                  row8(lambda i: (jnp.maximum(i * nb8 - 1, 0), 0)), row8(lambda i: (jnp.minimum((i + 1) * nb8, last8), 0)),
                  _full(lng.shape), _full(lnb.shape), _full(wsp.shape), _full(bspb.shape), _full(cw.shape), ANY],
        out_specs=[_tok(tm, inw), _full((1, aw)), _full((1, aw)), _full(wsp.shape),
                   _full((A_GROUPS, CHUNK, gd)), _full(cw.shape)],
        scratch=[pltpu.VMEM((gath.shape[1] * NDEV, dm), BF16), pltpu.SemaphoreType.DMA((NDEV,))])


def _ff_chunks(f, width=1024):
    return [(c0, min(c0 + width, f)) for c0 in range(0, f, width)]


def _ffn_up(x, gain, gath_g, gath_u, name, tm):
    t, dm = x.shape
    f = gath_g.shape[1] * NDEV

    def body(x_ref, g_ref, gg_ref, gu_ref, gate_ref, up_ref, act_ref, wg, wu, sems):
        @pl.when(pl.program_id(0) == 0)
        def _():
            _load_weight(gg_ref, wg, sems)
            _load_weight(gu_ref, wu, sems)
        h, _ = _rms_fwd(x_ref[...], g_ref[...])
        hb = h.astype(BF16)
        for c0, c1 in _ff_chunks(f):
            gate = _dot(hb, wg[c0:c1, :], NT)
            up = _dot(hb, wu[c0:c1, :], NT)
            gate_ref[:, c0:c1] = gate.astype(BF16)
            up_ref[:, c0:c1] = up.astype(BF16)
            act_ref[:, c0:c1] = (gate * _sigmoid(gate) * up).astype(BF16)

    o = jax.ShapeDtypeStruct((t, f), BF16)
    return dict(
        body=body, grid=(t // tm,), name=name, args=[x, gain, gath_g, gath_u], out_shape=[o, o, o],
        in_specs=[_tok(tm, dm), _full((1, dm)), ANY, ANY], out_specs=[_tok(tm, f)] * 3,
        scratch=[pltpu.VMEM((f, dm), BF16), pltpu.VMEM((f, dm), BF16), pltpu.SemaphoreType.DMA((NDEV,))])


def _ffn_down(x, act, gath_d, name, tm):
    t, dm = x.shape
    f = act.shape[1]

    def body(x_ref, a_ref, gd_ref, xo_ref, wd, sems):
        @pl.when(pl.program_id(0) == 0)
        def _():
            _load_weight(gd_ref, wd, sems)
        xo_ref[...] = x_ref[...] + _dot(a_ref[...], wd[...], NN)

    return dict(
        body=body, grid=(t // tm,), name=name, args=[x, act, gath_d], out_shape=[jax.ShapeDtypeStruct((t, dm), F32)],
        in_specs=[_tok(tm, dm), _tok(tm, f), ANY], out_specs=[_tok(tm, dm)],
        scratch=[pltpu.VMEM((f, dm), BF16), pltpu.SemaphoreType.DMA((NDEV,))])


def _ffn_down_loss(x, act, gath_d, target, gain, name, tm):
    t, dm = x.shape
    f = act.shape[1]
    steps = t // tm

    def body(x_ref, a_ref, gd_ref, t_ref, g_ref, loss_ref, dx_ref, dxb_ref, dgain_ref, wd, acc, sems):
        i = pl.program_id(0)

        @pl.when(i == 0)
        def _():
            _load_weight(gd_ref, wd, sems)
            acc[...] = jnp.zeros_like(acc)
            dgain_ref[...] = jnp.zeros_like(dgain_ref)
        xv = x_ref[...] + _dot(a_ref[...], wd[...], NN)
        gain_v = g_ref[...]
        y, r = _rms_fwd(xv, gain_v)
        e = y - t_ref[...]
        acc[...] += jnp.sum(e * e, axis=0, keepdims=True)
        dx, dgain = _rms_bwd(e * (1.0 / dm), xv, r, gain_v)
        dx_ref[...] = dx
        dxb_ref[...] = dx.astype(BF16)
        dgain_ref[...] += dgain

        @pl.when(i == steps - 1)
        def _():
            loss_ref[...] = jnp.sum(acc[...], axis=-1, keepdims=True) * (0.5 / dm)

    return dict(
        body=body, grid=(steps,), name=name, args=[x, act, gath_d, target, gain],
        out_shape=[jax.ShapeDtypeStruct((1, 1), F32), jax.ShapeDtypeStruct((t, dm), F32),
                   jax.ShapeDtypeStruct((t, dm), BF16), jax.ShapeDtypeStruct((1, dm), F32)],
        in_specs=[_tok(tm, dm), _tok(tm, f), ANY, _tok(tm, dm), _full((1, dm))],
        out_specs=[_full((1, 1)), _tok(tm, dm), _tok(tm, dm), _full((1, dm))],
        scratch=[pltpu.VMEM((f, dm), BF16), pltpu.VMEM((1, dm), F32), pltpu.SemaphoreType.DMA((NDEV,))])


def _ffn_bwd(dxo, x, gate, up, gain, gath_g, gath_u, gath_d, name, tm):
    t, dm = x.shape
    f = gate.shape[1]

    def body(dxo_ref, x_ref, gate_ref, up_ref, g_ref, gg_ref, gu_ref, gd_ref,
             dx_ref, dxb_ref, dg_ref, du_ref, hb_ref, dgain_ref, wg, wu, wd, sems):
        @pl.when(pl.program_id(0) == 0)
        def _():
            _load_weight(gg_ref, wg, sems)
            _load_weight(gu_ref, wu, sems)
            _load_weight(gd_ref, wd, sems)
            dgain_ref[...] = jnp.zeros_like(dgain_ref)
        xv, gain_v, dxo_v = x_ref[...], g_ref[...], dxo_ref[...]
        h, r = _rms_fwd(xv, gain_v)
        hb_ref[...] = h.astype(BF16)
        dxob = dxo_v.astype(BF16)
        dh = jnp.zeros_like(xv)
        for c0, c1 in _ff_chunks(f):
            gate_v = gate_ref[:, c0:c1].astype(F32)
            up_v = up_ref[:, c0:c1].astype(F32)
            s = _sigmoid(gate_v)
            silu = gate_v * s
            dact = _dot(dxob, wd[c0:c1, :], NT)
            dg = (dact * up_v * (s * (1.0 + gate_v * (1.0 - s)))).astype(BF16)
            du = (dact * silu).astype(BF16)
            dg_ref[:, c0:c1] = dg
            du_ref[:, c0:c1] = du
            dh = dh + _dot(dg, wg[c0:c1, :], NN) + _dot(du, wu[c0:c1, :], NN)
        dx, dgain = _rms_bwd(dh, xv, r, gain_v)
        dx = dxo_v + dx
        dx_ref[...] = dx
        dxb_ref[...] = dx.astype(BF16)
        dgain_ref[...] += dgain

    return dict(
        body=body, grid=(t // tm,), name=name, args=[dxo, x, gate, up, gain, gath_g, gath_u, gath_d],
        out_shape=[jax.ShapeDtypeStruct((t, dm), F32), jax.ShapeDtypeStruct((t, dm), BF16),
                   jax.ShapeDtypeStruct((t, f), BF16), jax.ShapeDtypeStruct((t, f), BF16),
                   jax.ShapeDtypeStruct((t, dm), BF16), jax.ShapeDtypeStruct((1, dm), F32)],
        in_specs=[_tok(tm, dm), _tok(tm, dm), _tok(tm, f), _tok(tm, f), _full((1, dm)), ANY, ANY, ANY],
        out_specs=[_tok(tm, dm), _tok(tm, dm), _tok(tm, f), _tok(tm, f), _tok(tm, dm), _full((1, dm))],
        scratch=[pltpu.VMEM((f, dm), BF16), pltpu.VMEM((f, dm), BF16), pltpu.VMEM((f, dm), BF16),
                 pltpu.SemaphoreType.DMA((NDEV,))])


def _t5_buckets(rel):
    nb = N_BUCKETS // 2
    ret = jnp.where(rel > 0, nb, 0)
    n = jnp.abs(rel)
    max_exact = nb // 2
    nf = jnp.maximum(n, 1).astype(jnp.float32)
    large = max_exact + (jnp.log(nf / max_exact) / math.log(MAX_DISTANCE / max_exact)
                         * (nb - max_exact)).astype(jnp.int32)
    large = jnp.minimum(large, nb - 1)
    return ret + jnp.where(n < max_exact, n, large)


def _bucket_table():
    qi = jnp.arange(CHUNK, dtype=jnp.int32)[:, None]
    kj = jnp.arange(3 * CHUNK, dtype=jnp.int32)[None, :]
    rel = kj - CHUNK - qi
    return jnp.where(jnp.abs(rel) <= CHUNK, _t5_buckets(rel), -1)


def _bias_table(rel_bias_t, buckets):
    nh = rel_bias_t.shape[0]

    def body(rb_ref, bk_ref, o_ref):
        bk = bk_ref[...]
        for h in range(nh):
            acc = jnp.where(bk < 0, NEG, 0.0).astype(F32)
            for b in range(N_BUCKETS):
                acc = jnp.where(bk == b, rb_ref[h, b] * LOG2E, acc)
            o_ref[h] = acc

    return pl.pallas_call(
        body, out_shape=jax.ShapeDtypeStruct((nh,) + buckets.shape, F32),
        in_specs=[pl.BlockSpec(memory_space=pltpu.SMEM), pl.BlockSpec(memory_space=pltpu.VMEM)],
        out_specs=pl.BlockSpec(memory_space=pltpu.VMEM), name="bias_table")(rel_bias_t, buckets)


def _rel_bias_grad(dbias, buckets):
    nh = dbias.shape[0]

    def body(db_ref, bk_ref, o_ref):
        bk = bk_ref[...]
        lane = lax.broadcasted_iota(jnp.int32, (1, 128), 1)
        for h in range(nh):
            d = db_ref[h]
            row = jnp.zeros((1, 128), F32)
            for b in range(N_BUCKETS):
                s = jnp.sum(jnp.sum(jnp.where(bk == b, d, 0.0), axis=1, keepdims=True), axis=0, keepdims=True)
                row = jnp.where(lane == b, s, row)
            o_ref[h:h + 1, :] = row

    return pl.pallas_call(
        body, out_shape=jax.ShapeDtypeStruct((nh, 128), F32),
        in_specs=[pl.BlockSpec(memory_space=pltpu.VMEM), pl.BlockSpec(memory_space=pltpu.VMEM)],
        out_specs=pl.BlockSpec(memory_space=pltpu.VMEM), compiler_params=_cp(0), name="rel_bias_grad")(dbias, buckets)


def _half_masks():
    lane = lax.broadcasted_iota(jnp.int32, (CHUNK, 128), 1)
    return lane < HEAD_DIM, lane >= HEAD_DIM


def _kv_low(ref, starts, hk, lo):
    kt = (hk // 2) * 128
    out = []
    for jj in range(3):
        blk = ref[pl.ds(starts[jj], CHUNK), kt:kt + 128]
        if hk % 2 == 1:
            blk = pltpu.roll(blk, HEAD_DIM, 1)
        out.append(jnp.where(lo, blk, jnp.zeros_like(blk)))
    return out


def _stack_heads(tile_a, tile_b):
    return jnp.concatenate([tile_a, pltpu.roll(tile_a, HEAD_DIM, 1), tile_b, pltpu.roll(tile_b, HEAD_DIM, 1)], axis=0)


def _unstack_heads(o4):
    return (o4[0:CHUNK] + pltpu.roll(o4[CHUNK:2 * CHUNK], HEAD_DIM, 1),
            o4[2 * CHUNK:3 * CHUNK] + pltpu.roll(o4[3 * CHUNK:], HEAD_DIM, 1))


ATT_SLAB = 32


def _softmax_slab(s_scr, hk, g, r0, bias_ref, sink_ref, n, nblk):
    scale = HEAD_DIM ** -0.5 * LOG2E
    h = (N_HEADS // N_KV) * hk + g
    s = []
    for jj in range(3):
        sj = (s_scr[hk, jj, pl.ds(g * CHUNK + r0, ATT_SLAB), :] * scale
              + bias_ref[h, pl.ds(r0, ATT_SLAB), jj * CHUNK:(jj + 1) * CHUNK])
        if jj == 0:
            sj = jnp.where(n > 0, sj, NEG)
        if jj == 2:
            sj = jnp.where(n < nblk - 1, sj, NEG)
        s.append(sj)
    sink = sink_ref[h] * LOG2E
    m = jnp.maximum(jnp.max(jnp.maximum(jnp.maximum(s[0], s[1]), s[2]), axis=-1, keepdims=True), sink)
    e = [jnp.exp2(sj - m) for sj in s]
    es = jnp.exp2(sink - m)
    inv = 1.0 / (jnp.sum(e[0] + e[1] + e[2], axis=-1, keepdims=True) + es)
    return [ej * inv for ej in e], es * inv


def _key_block_starts(n, nblk):
    return [pl.multiple_of(jnp.clip(n - 1 + jj, 0, nblk - 1) * CHUNK, CHUNK) for jj in range(3)]


def _attn_fwd(qkv, x2, bias, sink, gath):
    t, dm = x2.shape
    nblk = t // CHUNK
    kvw = N_KV * HEAD_DIM
    kcb, vcb = dm // kvw, dm // kvw + 1
    slab = (N_KV, 3, 4 * CHUNK, CHUNK)

    def body(q_ref, k_ref, v_ref, x2_ref, bias_ref, sink_ref, gath_ref, x3_ref, att_ref, p_ref, ps_ref,
             wbuf, s_scr, sems):
        n = pl.program_id(0)

        @pl.when(n == 0)
        def _():
            _load_weight(gath_ref, wbuf, sems)
        lo, _ = _half_masks()
        lane_s = lax.broadcasted_iota(jnp.int32, (ATT_SLAB, 128), 1)
        starts = _key_block_starts(n, nblk)
        tiles = []
        for hk in range(N_KV):
            c0 = (2 * hk) * 128
            k_lo = _kv_low(k_ref, starts, hk, lo)
            v_lo = _kv_low(v_ref, starts, hk, lo)
            q4 = _stack_heads(q_ref[:, c0:c0 + 128], q_ref[:, c0 + 128:c0 + 256])
            for jj in range(3):
                s_scr[hk, jj] = _dot(q4, k_lo[jj], NT)
            for g in range(4):
                h = 4 * hk + g
                for r0 in range(0, CHUNK, ATT_SLAB):
                    p, ps = _softmax_slab(s_scr, hk, g, r0, bias_ref, sink_ref, n, nblk)
                    for jj in range(3):
                        p_ref[hk, jj, g * CHUNK + r0:g * CHUNK + r0 + ATT_SLAB, :] = p[jj].astype(BF16)
                    rest = jnp.zeros((ATT_SLAB, 128), F32) if h == 0 else ps_ref[r0:r0 + ATT_SLAB, :]
                    ps_ref[r0:r0 + ATT_SLAB, :] = jnp.where(lane_s == h, ps, rest)
            o4 = _dot(p_ref[hk, 0], v_lo[0], NN) + _dot(p_ref[hk, 1], v_lo[1], NN) + _dot(p_ref[hk, 2], v_lo[2], NN)
            tiles += list(_unstack_heads(o4))
        att = jnp.concatenate(tiles, axis=1).astype(BF16)
        att_ref[...] = att
        x3_ref[...] = x2_ref[...] + _dot(att, wbuf[...], NN)

    blk = pl.BlockSpec((CHUNK, dm), lambda n: (n, 0))
    return dict(
        body=body, grid=(nblk,), name="attn_fwd", args=[qkv, qkv, qkv, x2, bias, sink, gath],
        out_shape=[jax.ShapeDtypeStruct((t, dm), F32), jax.ShapeDtypeStruct((t, dm), BF16),
                   jax.ShapeDtypeStruct((nblk,) + slab, BF16), jax.ShapeDtypeStruct((t, 128), F32)],
        in_specs=[blk, pl.BlockSpec((t, kvw), lambda n: (0, kcb)), pl.BlockSpec((t, kvw), lambda n: (0, vcb)), blk,
                  _full(bias.shape), pl.BlockSpec(memory_space=pltpu.SMEM), ANY],
        out_specs=[blk, blk, pl.BlockSpec((None,) + slab, lambda n: (n, 0, 0, 0, 0)),
                   pl.BlockSpec((CHUNK, 128), lambda n: (n, 0))],
        scratch=[pltpu.VMEM((gath.shape[1] * NDEV, dm), BF16), pltpu.VMEM(slab, F32),
                 pltpu.SemaphoreType.DMA((NDEV,))])


def _attn_bwd(qkv, att, probs, sink_probs, dx3, bias_shape, gath):
    t, dm = dx3.shape
    nblk = t // CHUNK
    kvw = N_KV * HEAD_DIM
    kcb, vcb = dm // kvw, dm // kvw + 1
    scale = HEAD_DIM ** -0.5
    slab = (N_KV, 3, 4 * CHUNK, CHUNK)

    def body(q_ref, k_ref, v_ref, att_ref, p_ref, ps_ref, dx_ref, gath_ref,
             dq_ref, dk_ref, dv_ref, dbias_ref, dsink_ref, wbuf, dp_scr, ds_scr, prod_scr, dsum_scr, sems):
        n = pl.program_id(0)

        @pl.when(n == 0)
        def _():
            _load_weight(gath_ref, wbuf, sems)
            dk_ref[...] = jnp.zeros_like(dk_ref)
            dv_ref[...] = jnp.zeros_like(dv_ref)
            dbias_ref[...] = jnp.zeros_like(dbias_ref)
            dsink_ref[...] = jnp.zeros_like(dsink_ref)
        lo, hi = _half_masks()
        lane_s = lax.broadcasted_iota(jnp.int32, (ATT_SLAB, 128), 1)
        starts = _key_block_starts(n, nblk)
        dout = _dot(dx_ref[...].astype(BF16), wbuf[...], NT)
        prod_scr[...] = dout * att_ref[...].astype(F32)
        doutb = dout.astype(BF16)
        dq_tiles = []
        for hk in range(N_KV):
            kt = (hk // 2) * 128
            c0 = (2 * hk) * 128
            k_lo = _kv_low(k_ref, starts, hk, lo)
            v_lo = _kv_low(v_ref, starts, hk, lo)
            q4 = _stack_heads(q_ref[:, c0:c0 + 128], q_ref[:, c0 + 128:c0 + 256])
            do4 = _stack_heads(doutb[:, c0:c0 + 128], doutb[:, c0 + 128:c0 + 256])
            for jj in range(3):
                dp_scr[hk, jj] = _dot(do4, v_lo[jj], NT)
            for g in range(4):
                h = 4 * hk + g
                for r0 in range(0, CHUNK, ATT_SLAB):
                    rows = slice(g * CHUNK + r0, g * CHUNK + r0 + ATT_SLAB)
                    pt = prod_scr[r0:r0 + ATT_SLAB, c0 + (g // 2) * 128:c0 + (g // 2 + 1) * 128]
                    msk = lane_s < HEAD_DIM if g % 2 == 0 else lane_s >= HEAD_DIM
                    dsum = jnp.sum(jnp.where(msk, pt, 0.0), axis=-1, keepdims=True)
                    rest = jnp.zeros((ATT_SLAB, 128), F32) if h == 0 else dsum_scr[r0:r0 + ATT_SLAB, :]
                    dsum_scr[r0:r0 + ATT_SLAB, :] = jnp.where(lane_s == h, dsum, rest)
                    for jj in range(3):
                        ds = p_ref[hk, jj, rows, :].astype(F32) * (dp_scr[hk, jj, rows, :] - dsum)
                        dbias_ref[h, r0:r0 + ATT_SLAB, jj * CHUNK:(jj + 1) * CHUNK] += ds
                        ds_scr[hk, jj, rows, :] = ds.astype(BF16)
            dq4 = jnp.zeros((4 * CHUNK, 128), F32)
            for jj in range(3):
                ds4 = ds_scr[hk, jj]
                dq4 = dq4 + _dot(ds4, k_lo[jj], NN) * scale
                dkj = _dot(ds4, q4, TN) * scale
                dvj = _dot(p_ref[hk, jj], do4, TN)
                if hk % 2 == 1:
                    dkj, dvj = pltpu.roll(dkj, HEAD_DIM, 1), pltpu.roll(dvj, HEAD_DIM, 1)
                keep = lo if hk % 2 == 0 else hi
                dk_ref[pl.ds(starts[jj], CHUNK), kt:kt + 128] += jnp.where(keep, dkj, 0.0)
                dv_ref[pl.ds(starts[jj], CHUNK), kt:kt + 128] += jnp.where(keep, dvj, 0.0)
            dq_tiles += list(_unstack_heads(dq4))
        dq_ref[...] = jnp.concatenate(dq_tiles, axis=1).astype(BF16)
        dsink_ref[...] -= jnp.sum(ps_ref[...] * dsum_scr[...], axis=0, keepdims=True)

    blk = pl.BlockSpec((CHUNK, dm), lambda n: (n, 0))
    return dict(
        body=body, grid=(nblk,), name="attn_bwd", args=[qkv, qkv, qkv, att, probs, sink_probs, dx3, gath],
        out_shape=[jax.ShapeDtypeStruct((t, dm), BF16), jax.ShapeDtypeStruct((t, kvw), F32),
                   jax.ShapeDtypeStruct((t, kvw), F32), jax.ShapeDtypeStruct(bias_shape, F32),
                   jax.ShapeDtypeStruct((1, 128), F32)],
        in_specs=[blk, pl.BlockSpec((t, kvw), lambda n: (0, kcb)), pl.BlockSpec((t, kvw), lambda n: (0, vcb)),
                  blk, pl.BlockSpec((None,) + slab, lambda n: (n, 0, 0, 0, 0)),
                  pl.BlockSpec((CHUNK, 128), lambda n: (n, 0)), blk, ANY],
        out_specs=[blk, _full((t, kvw)), _full((t, kvw)), _full(bias_shape), _full((1, 128))],
        scratch=[pltpu.VMEM((gath.shape[1] * NDEV, dm), BF16), pltpu.VMEM(slab, F32), pltpu.VMEM(slab, BF16),
                 pltpu.VMEM((CHUNK, dm), F32), pltpu.VMEM((CHUNK, 128), F32), pltpu.SemaphoreType.DMA((NDEV,))])


def _finish_weight(recvs, w, m, v, name):
    nl, r, dm = w.shape
    assert nl == len(recvs) and all(rc.shape[1:] == (r, dm) for rc in recvs)
    td = dm // 2
    wspec = pl.BlockSpec((None, r, td), lambda l, j: (l, 0, j))

    def body(*refs):
        r_refs = refs[:nl]
        w_ref, m_ref, v_ref, g_ref, d_ref, nm_ref, nv_ref = refs[nl:]
        layer = pl.program_id(0)
        for li in range(nl):
            @pl.when(layer == li)
            def _():
                g = r_refs[li][0].astype(F32)
                for d in range(1, recvs[li].shape[0]):
                    g = g + r_refs[li][d].astype(F32)
                delta, nm, nv = _adamw_math(w_ref[...], g, m_ref[...], v_ref[...])
                g_ref[...] = g
                d_ref[...] = delta
                nm_ref[...] = nm
                nv_ref[...] = nv

    o = jax.ShapeDtypeStruct(w.shape, F32)
    return dict(
        body=body, grid=(nl, 2), name=name, args=[*recvs, w, m, v], out_shape=[o, o, o, o],
        in_specs=[pl.BlockSpec((rc.shape[0], r, td), lambda l, j: (0, 0, j)) for rc in recvs] + [wspec] * 3,
        out_specs=[wspec] * 4)


def _adamw_small(w, g_slots, late_slots, m, v, name):
    r, c = w.shape
    nlate = late_slots.shape[1]

    def body(w_ref, g_ref, late_ref, m_ref, v_ref, gs_ref, d_ref, nm_ref, nv_ref):
        g = g_ref[0]
        late = late_ref[0]
        for d in range(1, NDEV):
            g = g + g_ref[d]
            late = late + late_ref[d]
        gs_ref[...] = g
        gs_ref[0:nlate, :] = late
        d_ref[...], nm_ref[...], nv_ref[...] = _adamw_math(w_ref[...], gs_ref[...], m_ref[...], v_ref[...])

    spec = pl.BlockSpec((r, c), lambda i: (0, 0))
    out = jax.ShapeDtypeStruct((r, c), F32)
    return pl.pallas_call(
        body, grid=(1,), out_shape=(out,) * 4,
        in_specs=[spec, pl.BlockSpec((NDEV, r, c), lambda i: (0, 0, 0)),
                  pl.BlockSpec((NDEV, nlate, c), lambda i: (0, 0, 0)), spec, spec], out_specs=(spec,) * 4,
        compiler_params=_cp(), name=name)(w, g_slots, late_slots, m, v)


def _pack_small(parts, rows):
    flat = jnp.concatenate([p.reshape(-1) for p in parts])
    return jnp.pad(flat, (0, rows * 128 - flat.shape[0])).reshape(rows, 128)


def _unpack_small(packed, shapes):
    flat = packed.reshape(-1)
    out, o = [], 0
    for s in shapes:
        n = int(np.prod(s))
        out.append(flat[o:o + n].reshape(s))
        o += n
    return out


def kernel(x, norm_mix, norm_ffn, even_w_in, even_v_ln_g, even_v_ln_b, even_w_spatial, even_b_spatial, even_conv_w, even_w_out, attn_w_qkv, attn_sink, rel_bias, attn_w_out, ffn_w_gate, ffn_w_up, ffn_w_down, final_norm, loss_target, m_norm_mix, m_norm_ffn, m_even_w_in, m_even_v_ln_g, m_even_v_ln_b, m_even_w_spatial, m_even_b_spatial, m_even_conv_w, m_even_w_out, m_attn_w_qkv, m_attn_sink, m_rel_bias, m_attn_w_out, m_ffn_w_gate, m_ffn_w_up, m_ffn_w_down, m_final_norm, v_norm_mix, v_norm_ffn, v_even_w_in, v_even_v_ln_g, v_even_v_ln_b, v_even_w_spatial, v_even_b_spatial, v_even_conv_w, v_even_w_out, v_attn_w_qkv, v_attn_sink, v_rel_bias, v_attn_w_out, v_ffn_w_gate, v_ffn_w_up, v_ffn_w_down, v_final_norm):
    t, dm = x.shape[1], x.shape[2]
    aw = even_v_ln_g.shape[1]
    bw = even_conv_w.shape[2] * NDEV
    gd = aw // A_GROUPS
    tm = min(512, t // 2)
    tmf = min(256, t // 2)
    me = _my_index()
    row = lambda a: a.reshape(1, -1)

    colT = lambda w: w.T.astype(BF16)
    sh = dict(winT=colT(even_w_in[0]), wqkvT=colT(attn_w_qkv[0]), wgT0=colT(ffn_w_gate[0]), wuT0=colT(ffn_w_up[0]),
              wgT1=colT(ffn_w_gate[1]), wuT1=colT(ffn_w_up[1]), woe=even_w_out[0].astype(BF16),
              woa=attn_w_out[0].astype(BF16), wd0=ffn_w_down[0].astype(BF16), wd1=ffn_w_down[1].astype(BF16))
    gather = lambda names: _GatherCarry([sh[n] for n in names])

    in_full = lambda a: lax.dynamic_update_slice(jnp.zeros((3, bw), F32), a[0], (0, me * (bw // NDEV)))
    cw_rows = 3 * bw // 128
    cw_mine = jnp.pad(in_full(even_conv_w).reshape(cw_rows, 128), ((0, 16 - cw_rows), (0, 0)))

    x0 = x[0]
    wsp_b = even_w_spatial[0].astype(BF16)
    bspb = jnp.broadcast_to(even_b_spatial[0][:, :, None], (A_GROUPS, CHUNK, gd))
    buckets = _bucket_table()
    bias = _bias_table(rel_bias.T, buckets)
    sink = attn_sink[0]

    (g_winT,), (cw_slots,) = _exchange_only([gather(["winT"]), _BroadcastCarry(cw_mine)], "ag_w_in")
    cw_full = jnp.sum(cw_slots, axis=0)[0:cw_rows].reshape(3, bw)
    (proj, h0b), (g_woe, g_wgT0) = _call(_norm_proj(x0, row(norm_mix[0]), g_winT, F32, "in_proj", tm),
                                         gather(["woe", "wgT0"]))
    (x1, yb), (g_wuT0,) = _call(_even_core_fwd(proj, x0, even_v_ln_g, even_v_ln_b, wsp_b, bspb, cw_full, g_woe, tm),
                                gather(["wuT0"]))
    (gate0, up0, act0), (g_wd0,) = _call(_ffn_up(x1, row(norm_ffn[0]), g_wgT0, g_wuT0, "ffn_up0", tmf), gather(["wd0"]))
    (x2,), (g_wqkvT,) = _call(_ffn_down(x1, act0, g_wd0, "ffn_down0", tm), gather(["wqkvT"]))
    (qkv, h2b), (g_woa,) = _call(_norm_proj(x2, row(norm_mix[1]), g_wqkvT, BF16, "qkv_proj", tm), gather(["woa"]))
    (x3, attb, probs, sink_probs), (g_wgT1, g_wuT1) = _call(
        _attn_fwd(qkv, x2, bias, sink, g_woa), gather(["wgT1", "wuT1"]))
    (gate1, up1, act1), (g_wd1,) = _call(_ffn_up(x3, row(norm_ffn[1]), g_wgT1, g_wuT1, "ffn_up1", tmf), gather(["wd1"]))
    (loss_part, dx4, dx4b, d_final), _ = _call(
        _ffn_down_loss(x3, act1, g_wd1, loss_target[0], row(final_norm), "ffn_down1_loss", tm))

    (dx3, dx3b, dg1, du1, h3b, d_nffn1), _ = _call(
        _ffn_bwd(dx4, x3, gate1, up1, row(norm_ffn[1]), g_wgT1, g_wuT1, g_wd1, "ffn_bwd1", tmf))
    (p_wgT1,), _ = _call(_wgrad(dg1, h3b, "wgrad_gate1"))
    (p_wuT1,), (a_wgT1,) = _call(_wgrad(du1, h3b, "wgrad_up1"), _PairCarry(p_wgT1))
    (p_wd1,), ((r_wgT1,), (a_wuT1,)) = _call(
        _wgrad(act1, dx4b, "wgrad_down1"), [_ChipSumCarry(p_wgT1, a_wgT1), _PairCarry(p_wuT1)])
    (dq, dk, dv, dbias, dsink), ((r_wuT1,), (a_wd1,)) = _call(
        _attn_bwd(qkv, attb, probs, sink_probs, dx3, bias.shape, g_woa),
        [_ChipSumCarry(p_wuT1, a_wuT1), _PairCarry(p_wd1)])
    (p_woa,), _ = _call(_wgrad(attb, dx3b, "wgrad_attn_out"))
    d_relb = _rel_bias_grad(dbias, buckets)[:, 0:N_BUCKETS].T
    dqkv = jnp.concatenate([dq, dk.astype(BF16), dv.astype(BF16)], axis=1)
    (dx2, dx2b, d_nmix1), (r_wd1,) = _call(
        _proj_bwd_norm(dqkv, x2, row(norm_mix[1]), dx3, g_wqkvT, "qkv_bwd", tm), _ChipSumCarry(p_wd1, a_wd1))
    (p_wqkvT,), _ = _call(_wgrad(dqkv, h2b, "wgrad_qkv"))
    (dx1, dx1b, dg0, du0, h1b, d_nffn0), (r_woa, r_wqkvT) = _call(
        _ffn_bwd(dx2, x1, gate0, up0, row(norm_ffn[0]), g_wgT0, g_wuT0, g_wd0, "ffn_bwd0", tmf),
        _GradCarry([p_woa, p_wqkvT]))
    (p_wgT0,), _ = _call(_wgrad(dg0, h1b, "wgrad_gate0"))
    (p_wuT0,), (a_wgT0,) = _call(_wgrad(du0, h1b, "wgrad_up0"), _PairCarry(p_wgT0))
    (dproj, d_lng, d_lnb, d_wsp, d_bsp3, d_cw), ((r_wgT0,), (a_wuT0,)) = _call(
        _even_core_bwd(proj, dx1, even_v_ln_g, even_v_ln_b, wsp_b, bspb, cw_full, g_woe, tm),
        [_ChipSumCarry(p_wgT0, a_wgT0), _PairCarry(p_wuT0)])
    (p_winT,), (r_wuT0,) = _call(_wgrad(dproj, h0b, "wgrad_in"), _ChipSumCarry(p_wuT0, a_wuT0))
    small_shapes = [(2, dm), (2, dm), (1, aw), (1, aw), (1, A_GROUPS, CHUNK, CHUNK), (1, A_GROUPS, CHUNK), (3, bw),
                    (1, N_HEADS), (N_BUCKETS, N_HEADS), (dm,), (1, 1)]
    small_names = ["norm_mix", "norm_ffn", "even_v_ln_g", "even_v_ln_b", "even_w_spatial", "even_b_spatial",
                   "even_conv_w", "attn_sink", "rel_bias", "final_norm", "loss"]
    n_small = sum(int(np.prod(s)) for s in small_shapes)
    small_rows = 8 * ((n_small + 1023) // 1024)
    assert dm == 8 * 128
    small_part = _pack_small(
        [jnp.concatenate([jnp.zeros_like(d_nmix1), d_nmix1]), jnp.concatenate([d_nffn0, d_nffn1]), d_lng, d_lnb,
         d_wsp, jnp.sum(d_bsp3, axis=-1), d_cw, dsink[:, 0:N_HEADS], d_relb, d_final, loss_part], small_rows)
    (p_wd0,), ((a_winT,), (small_slots,)) = _call(
        _wgrad(act0, dx2b, "wgrad_down0"), [_PairCarry(p_winT), _BroadcastCarry(small_part)])
    (p_woe,), ((r_winT,), (a_wd0,)) = _call(
        _wgrad(yb, dx1b, "wgrad_even_out"), [_ChipSumCarry(p_winT, a_winT), _PairCarry(p_wd0)])
    (dx0, _, d_nmix0), ((r_wd0,), (r_woe,)) = _call(
        _proj_bwd_norm(dproj, x0, row(norm_mix[0]), dx1, g_winT, "in_proj_bwd", tm),
        [_ChipSumCarry(p_wd0, a_wd0), _GradCarry([p_woe])])

    grads = {}

    order = ["norm_mix", "norm_ffn", "even_w_in", "even_v_ln_g", "even_v_ln_b", "even_w_spatial", "even_b_spatial",
             "even_conv_w", "even_w_out", "attn_w_qkv", "attn_sink", "rel_bias", "attn_w_out", "ffn_w_gate",
             "ffn_w_up", "ffn_w_down", "final_norm"]
    ws = dict(norm_mix=norm_mix, norm_ffn=norm_ffn, even_w_in=even_w_in, even_v_ln_g=even_v_ln_g,
              even_v_ln_b=even_v_ln_b, even_w_spatial=even_w_spatial, even_b_spatial=even_b_spatial,
              even_conv_w=even_conv_w, even_w_out=even_w_out, attn_w_qkv=attn_w_qkv, attn_sink=attn_sink,
              rel_bias=rel_bias, attn_w_out=attn_w_out, ffn_w_gate=ffn_w_gate, ffn_w_up=ffn_w_up,
              ffn_w_down=ffn_w_down, final_norm=final_norm)
    ms = dict(norm_mix=m_norm_mix, norm_ffn=m_norm_ffn, even_w_in=m_even_w_in, even_v_ln_g=m_even_v_ln_g,
              even_v_ln_b=m_even_v_ln_b, even_w_spatial=m_even_w_spatial, even_b_spatial=m_even_b_spatial,
              even_conv_w=m_even_conv_w, even_w_out=m_even_w_out, attn_w_qkv=m_attn_w_qkv, attn_sink=m_attn_sink,
              rel_bias=m_rel_bias, attn_w_out=m_attn_w_out, ffn_w_gate=m_ffn_w_gate, ffn_w_up=m_ffn_w_up,
              ffn_w_down=m_ffn_w_down, final_norm=m_final_norm)
    vs = dict(norm_mix=v_norm_mix, norm_ffn=v_norm_ffn, even_w_in=v_even_w_in, even_v_ln_g=v_even_v_ln_g,
              even_v_ln_b=v_even_v_ln_b, even_w_spatial=v_even_w_spatial, even_b_spatial=v_even_b_spatial,
              even_conv_w=v_even_conv_w, even_w_out=v_even_w_out, attn_w_qkv=v_attn_w_qkv, attn_sink=v_attn_sink,
              rel_bias=v_rel_bias, attn_w_out=v_attn_w_out, ffn_w_gate=v_ffn_w_gate, ffn_w_up=v_ffn_w_up,
              ffn_w_down=v_ffn_w_down, final_norm=v_final_norm)
    big = dict(ffn_w_gate=([r_wgT0, r_wgT1], True), even_w_in=([r_winT], True), even_w_out=([r_woe], False),
               attn_w_qkv=([r_wqkvT], True), attn_w_out=([r_woa], False), ffn_w_up=([r_wuT0, r_wuT1], True),
               ffn_w_down=([r_wd0, r_wd1], False))
    delta, new_m, new_v = {}, {}, {}
    late_slots = None
    for n, (recvs, transposed) in big.items():
        lay = (lambda a: jnp.swapaxes(a, 1, 2)) if transposed else (lambda a: a)
        spec = _finish_weight(recvs, lay(ws[n]), lay(ms[n]), lay(vs[n]), "finish_" + n)
        if late_slots is None:
            outs, (late_slots,) = _call(spec, _BroadcastCarry(d_nmix0.reshape(8, 128)))
        else:
            outs, _ = _call(spec)
        grads[n], delta[n], new_m[n], new_v[n] = [lay(o) for o in outs]
    pk = lambda dct: _pack_small(
        [in_full(dct[n]) if n == "even_conv_w" else (jnp.zeros((1, 1), F32) if n == "loss" else dct[n])
         for n in small_names], small_rows)
    packed = _adamw_small(pk(ws), small_slots, late_slots, pk(ms), pk(vs), "adamw_small")
    mine = lambda a: lax.dynamic_slice(a, (0, me * (bw // NDEV)), (3, bw // NDEV))[None]
    for dst, arr in zip((grads, delta, new_m, new_v), packed):
        for n, a in zip(small_names, _unpack_small(arr, small_shapes)):
            dst[n] = mine(a) if n == "even_conv_w" else a
    loss = grads["loss"][0, 0]
    return (loss, dx0[None], *[grads[n] for n in order], *[delta[n] for n in order],
            *[new_m[n] for n in order], *[new_v[n] for n in order])
```

```python
import math

import jax
import jax.numpy as jnp
import numpy as np
from jax import lax
from jax.experimental import pallas as pl
from jax.experimental.pallas import tpu as pltpu

F32, BF16 = jnp.float32, jnp.bfloat16
NDEV = 8
EPS = 1e-6
CHUNK = 128
A_GROUPS = 4
N_HEADS, N_KV, HEAD_DIM = 16, 4, 64
N_BUCKETS, MAX_DISTANCE = 32, 128
NEG = -1e30
LOG2E = 1.4426950408889634
ADAM_LR, ADAM_B1, ADAM_B2, ADAM_EPS, ADAM_WD, ADAM_STEP = 0.001, 0.9, 0.999, 1e-08, 0.01, 10
VMEM_LIMIT = 56 * 1024 * 1024
MESH = pl.DeviceIdType.MESH
NT = (((1,), (1,)), ((), ()))
NN = (((1,), (0,)), ((), ()))
TN = (((0,), (0,)), ((), ()))
ANY = pl.BlockSpec(memory_space=pl.ANY)


def _cp(n_grid=1):
    return pltpu.CompilerParams(dimension_semantics=("arbitrary",) * n_grid, vmem_limit_bytes=VMEM_LIMIT)


def _dot(a, b, dims):
    return lax.dot_general(a, b, dims, preferred_element_type=F32)


def _my_index():
    return 4 * lax.axis_index("x") + 2 * lax.axis_index("y") + lax.axis_index("c")


def _peer(k):
    x, y, c = lax.axis_index("x"), lax.axis_index("y"), lax.axis_index("c")
    px = 1 - x if k & 4 else x
    py = 1 - y if k & 2 else y
    pc = 1 - c if k & 1 else c
    return (px, py, pc)


def _load_weight(gath_ref, wbuf, sems):
    rows = gath_ref.shape[1]
    cps = [pltpu.make_async_copy(gath_ref.at[d], wbuf.at[pl.ds(d * rows, rows), :], sems.at[d]) for d in range(NDEV)]
    for c in cps:
        c.start()
    for c in cps:
        c.wait()


class _GatherCarry:
    def __init__(self, pieces):
        self.inputs = list(pieces)
        self.n = len(pieces)
        self.out_shape = [jax.ShapeDtypeStruct((NDEV,) + p.shape, p.dtype) for p in pieces]
        self.scratch = [pltpu.SemaphoreType.DMA((7 * self.n,)), pltpu.SemaphoreType.DMA((7 * self.n,)),
                        pltpu.SemaphoreType.DMA((self.n,))]

    def _ctx(self):
        x, y, c = lax.axis_index("x"), lax.axis_index("y"), lax.axis_index("c")
        chips = [(1 - x, y), (x, 1 - y), (1 - x, 1 - y)]
        return (x, y, c), (x, y, 1 - c), chips, c

    def _copy(self, k, j, block, to, ins, outs, sems, src=None):
        send_sems, recv_sems, _ = sems
        slot = outs[j].at[4 * block[0] + 2 * block[1] + block[2]]
        return pltpu.make_async_remote_copy(
            src_ref=slot if src is None else src, dst_ref=slot, send_sem=send_sems.at[k * self.n + j],
            recv_sem=recv_sems.at[k * self.n + j], device_id=to, device_id_type=MESH)

    def start(self, ins, outs, sems):
        me, sibling, chips, c = self._ctx()
        for j in range(self.n):
            pltpu.make_async_copy(ins[j], outs[j].at[4 * me[0] + 2 * me[1] + me[2]], sems[2].at[j]).start()
            self._copy(0, j, me, sibling, ins, outs, sems, src=ins[j]).start()
            for q, chip in enumerate(chips):
                self._copy(1 + q, j, me, (*chip, c), ins, outs, sems, src=ins[j]).start()

    def mid(self, ins, outs, sems):
        me, sibling, chips, c = self._ctx()
        for q, chip in enumerate(chips):
            for j in range(self.n):
                self._copy(1 + q, j, (*chip, c), me, ins, outs, sems).wait_recv()
                self._copy(4 + q, j, (*chip, c), sibling, ins, outs, sems).start()

    def finish(self, ins, outs, sems):
        me, sibling, chips, c = self._ctx()
        for j in range(self.n):
            self._copy(0, j, sibling, me, ins, outs, sems).wait_recv()
            for q, chip in enumerate(chips):
                self._copy(4 + q, j, (*chip, 1 - c), me, ins, outs, sems).wait_recv()
        for j in range(self.n):
            self._copy(0, j, me, sibling, ins, outs, sems, src=ins[j]).wait_send()
            for q, chip in enumerate(chips):
                self._copy(1 + q, j, me, (*chip, c), ins, outs, sems, src=ins[j]).wait_send()
                self._copy(4 + q, j, (*chip, c), sibling, ins, outs, sems).wait_send()
            pltpu.make_async_copy(ins[j], outs[j].at[0], sems[2].at[j]).wait()


class _GradCarry:
    def __init__(self, pieces):
        self.inputs = list(pieces)
        self.n = len(pieces)
        self.rows = [p.shape[0] // NDEV for p in pieces]
        self.out_shape = [jax.ShapeDtypeStruct((NDEV, r, p.shape[1]), p.dtype) for p, r in zip(pieces, self.rows)]
        self.scratch = [pltpu.SemaphoreType.DMA((7 * self.n,)), pltpu.SemaphoreType.DMA((7 * self.n,)),
                        pltpu.SemaphoreType.DMA((self.n,))]

    def _copies(self, ins, outs, sems):
        me = _my_index()
        local, remote = [], []
        for j in range(self.n):
            r = self.rows[j]
            local.append(pltpu.make_async_copy(ins[j].at[pl.ds(pl.multiple_of(me * r, 16), r), :], outs[j].at[me],
                                               sems[2].at[j]))
            for k in range(1, NDEV):
                peer = _peer(k)
                pidx = 4 * peer[0] + 2 * peer[1] + peer[2]
                remote.append(pltpu.make_async_remote_copy(
                    src_ref=ins[j].at[pl.ds(pl.multiple_of(pidx * r, 16), r), :], dst_ref=outs[j].at[me],
                    send_sem=sems[0].at[(k - 1) * self.n + j], recv_sem=sems[1].at[(k - 1) * self.n + j],
                    device_id=peer, device_id_type=MESH))
        return local, remote

    def start(self, ins, outs, sems):
        local, remote = self._copies(ins, outs, sems)
        for cp in local + remote:
            cp.start()

    def mid(self, ins, outs, sems):
        pass

    def finish(self, ins, outs, sems):
        local, remote = self._copies(ins, outs, sems)
        for cp in remote + local:
            cp.wait()


class _BroadcastCarry:
    def __init__(self, part):
        self.inputs = [part]
        self.out_shape = [jax.ShapeDtypeStruct((NDEV,) + part.shape, part.dtype)]
        self.scratch = [pltpu.SemaphoreType.DMA((7,)), pltpu.SemaphoreType.DMA((7,)), pltpu.SemaphoreType.DMA(())]

    def _copies(self, ins, outs, sems):
        me = _my_index()
        local = pltpu.make_async_copy(ins[0], outs[0].at[me], sems[2])
        remote = [pltpu.make_async_remote_copy(
            src_ref=ins[0], dst_ref=outs[0].at[me], send_sem=sems[0].at[k - 1], recv_sem=sems[1].at[k - 1],
            device_id=_peer(k), device_id_type=MESH) for k in range(1, NDEV)]
        return [local] + remote

    def start(self, ins, outs, sems):
        for cp in self._copies(ins, outs, sems):
            cp.start()

    def mid(self, ins, outs, sems):
        pass

    def finish(self, ins, outs, sems):
        for cp in self._copies(ins, outs, sems):
            cp.wait()


class _PairCarry:
    def __init__(self, piece):
        self.inputs = [piece]
        self.r = piece.shape[0] // NDEV
        self.out_shape = [jax.ShapeDtypeStruct((4, self.r, piece.shape[1]), piece.dtype)]
        self.scratch = [pltpu.SemaphoreType.DMA((4,)), pltpu.SemaphoreType.DMA((4,))]

    def _copies(self, ins, outs, sems):
        x, y, c = lax.axis_index("x"), lax.axis_index("y"), lax.axis_index("c")
        return [pltpu.make_async_remote_copy(
            src_ref=ins[0].at[pl.ds(pl.multiple_of((2 * q + 1 - c) * self.r, 16), self.r), :], dst_ref=outs[0].at[q],
            send_sem=sems[0].at[q], recv_sem=sems[1].at[q], device_id=(x, y, 1 - c), device_id_type=MESH)
            for q in range(4)]

    def start(self, ins, outs, sems):
        for cp in self._copies(ins, outs, sems):
            cp.start()

    def mid(self, ins, outs, sems):
        pass

    def finish(self, ins, outs, sems):
        for cp in self._copies(ins, outs, sems):
            cp.wait()


class _ChipSumCarry:
    def __init__(self, piece, landed):
        self.inputs = [piece, landed]
        self.r, dm = piece.shape[0] // NDEV, piece.shape[1]
        self.out_shape = [jax.ShapeDtypeStruct((4, self.r, dm), piece.dtype)]
        self.scratch = [pltpu.VMEM((4, self.r, dm), piece.dtype), pltpu.VMEM((8, self.r, dm), piece.dtype),
                        pltpu.SemaphoreType.DMA((8,)), pltpu.SemaphoreType.DMA((3,)), pltpu.SemaphoreType.DMA((3,)),
                        pltpu.SemaphoreType.DMA(())]

    def _copies(self, outs, scr):
        sums, _, _, send_sems, recv_sems, local_sem = scr
        x, y, c = lax.axis_index("x"), lax.axis_index("y"), lax.axis_index("c")
        mine = 2 * x + y
        local = pltpu.make_async_copy(sums.at[mine], outs[0].at[mine], local_sem)
        remote = []
        for k in range(1, 4):
            px = 1 - x if k & 2 else x
            py = 1 - y if k & 1 else y
            remote.append(pltpu.make_async_remote_copy(
                src_ref=sums.at[2 * px + py], dst_ref=outs[0].at[mine], send_sem=send_sems.at[k - 1],
                recv_sem=recv_sems.at[k - 1], device_id=(px, py, c), device_id_type=MESH))
        return local, remote

    def start(self, ins, outs, scr):
        sums, stage, stage_sems = scr[0], scr[1], scr[2]
        c = lax.axis_index("c")
        loads = []
        for q in range(4):
            loads.append((
                pltpu.make_async_copy(ins[0].at[pl.ds(pl.multiple_of((2 * q + c) * self.r, 16), self.r), :],
                                      stage.at[2 * q], stage_sems.at[2 * q]),
                pltpu.make_async_copy(ins[1].at[q], stage.at[2 * q + 1], stage_sems.at[2 * q + 1])))
        for a, b in loads:
            a.start()
            b.start()
        for q, (a, b) in enumerate(loads):
            a.wait()
            b.wait()
            sums[q] = (stage[2 * q].astype(F32) + stage[2 * q + 1].astype(F32)).astype(sums.dtype)
        local, remote = self._copies(outs, scr)
        for cp in [local] + remote:
            cp.start()

    def mid(self, ins, outs, scr):
        pass

    def finish(self, ins, outs, scr):
        local, remote = self._copies(outs, scr)
        for cp in remote + [local]:
            cp.wait()


def _call(spec, carry=None):
    body, grid = spec["body"], spec["grid"]
    in_specs, out_specs, out_shape = list(spec["in_specs"]), list(spec["out_specs"]), list(spec["out_shape"])
    scratch, args = list(spec.get("scratch", [])), list(spec["args"])
    if carry is None:
        out = pl.pallas_call(body, grid=grid, in_specs=in_specs, out_specs=tuple(out_specs),
                             out_shape=tuple(out_shape), scratch_shapes=scratch, compiler_params=_cp(len(grid)),
                             name=spec["name"])(*args)
        return tuple(out), ()
    carries = list(carry) if isinstance(carry, (list, tuple)) else [carry]
    n_in, n_out, n_s = len(in_specs), len(out_specs), len(scratch)
    steps = int(np.prod(grid))

    def split(refs, counts):
        parts, o = [], 0
        for cnt in counts:
            parts.append(refs[o:o + cnt])
            o += cnt
        return parts

    c_in = [len(cr.inputs) for cr in carries]
    c_out = [len(cr.out_shape) for cr in carries]
    c_scr = [len(cr.scratch) for cr in carries]

    def wrapped(*refs):
        ins, cins, outs, couts, scr, cscr = split(refs, [n_in, sum(c_in), n_out, sum(c_out), n_s, sum(c_scr)])
        per = list(zip(carries, split(cins, c_in), split(couts, c_out), split(cscr, c_scr)))
        step = pl.program_id(0)
        for ax in range(1, len(grid)):
            step = step * grid[ax] + pl.program_id(ax)

        @pl.when(step == 0)
        def _():
            for cr, ci, co, cs in per:
                cr.start(ci, co, cs)
        if steps >= 3:
            @pl.when(step == steps - 2)
            def _():
                for cr, ci, co, cs in per:
                    cr.mid(ci, co, cs)
        body(*ins, *outs, *scr)

        @pl.when(step == steps - 1)
        def _():
            for cr, ci, co, cs in per:
                if steps < 3:
                    cr.mid(ci, co, cs)
                cr.finish(ci, co, cs)

    out = pl.pallas_call(
        wrapped, grid=grid, in_specs=in_specs + [ANY] * sum(c_in), out_specs=tuple(out_specs + [ANY] * sum(c_out)),
        out_shape=tuple(out_shape + [s for cr in carries for s in cr.out_shape]),
        scratch_shapes=scratch + [s for cr in carries for s in cr.scratch],
        compiler_params=_cp(len(grid)), name=spec["name"])(*args, *[a for cr in carries for a in cr.inputs])
    c_res = [tuple(p) for p in split(out[n_out:], c_out)]
    return tuple(out[:n_out]), (c_res if isinstance(carry, (list, tuple)) else c_res[0])


def _rms_fwd(x, gain):
    r = lax.rsqrt(jnp.mean(x * x, axis=-1, keepdims=True) + EPS)
    return x * r * gain, r


def _rms_bwd(dh, x, r, gain):
    a = dh * gain
    dx = r * a - x * (r * r * r) * jnp.mean(a * x, axis=-1, keepdims=True)
    dgain = jnp.sum(dh * (x * r), axis=0, keepdims=True)
    return dx, dgain


def _gelu(x):
    return 0.5 * x * (1.0 + lax.erf(x * 0.7071067811865476))


def _gelu_grad(x):
    return 0.5 * (1.0 + lax.erf(x * 0.7071067811865476)) + x * jnp.exp(-0.5 * x * x) * 0.3989422804014327


def _sigmoid(x):
    return 1.0 / (1.0 + jnp.exp(-x))


def _adamw_math(w, g, m, v):
    nm = ADAM_B1 * m + (1.0 - ADAM_B1) * g
    nv = ADAM_B2 * v + (1.0 - ADAM_B2) * (g * g)
    m_hat = nm / (1.0 - ADAM_B1 ** ADAM_STEP)
    v_hat = nv / (1.0 - ADAM_B2 ** ADAM_STEP)
    return -ADAM_LR * (m_hat / (jnp.sqrt(v_hat) + ADAM_EPS) + ADAM_WD * w), nm, nv


def _tok(tm, w):
    return pl.BlockSpec((tm, w), lambda i: (i, 0))


def _full(shape):
    return pl.BlockSpec(shape, lambda *i: (0,) * len(shape))


def _norm_proj(x, gain, gath, out_dtype, name, tm):
    t, dm = x.shape
    n = gath.shape[1] * NDEV

    def body(x_ref, g_ref, gath_ref, proj_ref, hb_ref, wbuf, sems):
        @pl.when(pl.program_id(0) == 0)
        def _():
            _load_weight(gath_ref, wbuf, sems)
        h, _ = _rms_fwd(x_ref[...], g_ref[...])
        hb = h.astype(BF16)
        hb_ref[...] = hb
        proj_ref[...] = _dot(hb, wbuf[...], NT).astype(out_dtype)

    return dict(
        body=body, grid=(t // tm,), name=name, args=[x, gain, gath],
        out_shape=[jax.ShapeDtypeStruct((t, n), out_dtype), jax.ShapeDtypeStruct((t, dm), BF16)],
        in_specs=[_tok(tm, dm), _full((1, dm)), ANY], out_specs=[_tok(tm, n), _tok(tm, dm)],
        scratch=[pltpu.VMEM((n, dm), BF16), pltpu.SemaphoreType.DMA((NDEV,))])


def _proj_bwd_norm(dy, x, gain, dres, gath, name, tm):
    t, dm = x.shape
    n = gath.shape[1] * NDEV

    def body(dy_ref, x_ref, g_ref, dres_ref, gath_ref, dx_ref, dxb_ref, dgain_ref, wbuf, sems):
        @pl.when(pl.program_id(0) == 0)
        def _():
            _load_weight(gath_ref, wbuf, sems)
            dgain_ref[...] = jnp.zeros_like(dgain_ref)
        xv, gain_v = x_ref[...], g_ref[...]
        _, r = _rms_fwd(xv, gain_v)
        dh = _dot(dy_ref[...], wbuf[...], NN)
        dx, dgain = _rms_bwd(dh, xv, r, gain_v)
        dx = dres_ref[...] + dx
        dx_ref[...] = dx
        dxb_ref[...] = dx.astype(BF16)
        dgain_ref[...] += dgain

    return dict(
        body=body, grid=(t // tm,), name=name, args=[dy, x, gain, dres, gath],
        out_shape=[jax.ShapeDtypeStruct((t, dm), F32), jax.ShapeDtypeStruct((t, dm), BF16),
                   jax.ShapeDtypeStruct((1, dm), F32)],
        in_specs=[_tok(tm, n), _tok(tm, dm), _full((1, dm)), _tok(tm, dm), ANY],
        out_specs=[_tok(tm, dm), _tok(tm, dm), _full((1, dm))],
        scratch=[pltpu.VMEM((n, dm), BF16), pltpu.SemaphoreType.DMA((NDEV,))])


def _wgrad(a, b, name, tmm=256):
    t, m = a.shape
    n = b.shape[1]

    def body(a_ref, b_ref, o_ref):
        o_ref[...] = _dot(a_ref[...], b_ref[...], TN).astype(BF16)

    return dict(
        body=body, grid=(m // tmm,), name=name, args=[a, b], out_shape=[jax.ShapeDtypeStruct((m, n), BF16)],
        in_specs=[pl.BlockSpec((t, tmm), lambda j: (0, j)), pl.BlockSpec((t, n), lambda j: (0, 0))],
        out_specs=[pl.BlockSpec((tmm, n), lambda j: (j, 0))])


def _halo_specs(tm, t, width, col_blocks):
    nb8 = tm // 8
    last = t // 8 - 1
    prev = [pl.BlockSpec((8, width), lambda i, cb=cb: (jnp.maximum(i * nb8 - 1, 0), cb)) for cb in col_blocks]
    nxt = [pl.BlockSpec((8, width), lambda i, cb=cb: (jnp.minimum((i + 1) * nb8, last), cb)) for cb in col_blocks]
    return prev, nxt


def _shift_rows(z, prev_row, next_row):
    tm = z.shape[0]
    row = lax.broadcasted_iota(jnp.int32, z.shape, 0)
    zm1 = jnp.where(row == 0, prev_row, pltpu.roll(z, 1, 0))
    zp1 = jnp.where(row == tm - 1, next_row, pltpu.roll(z, tm - 1, 0))
    return zm1, zp1


def _gating_fwd(proj, lng, lnb, wsp_ref, bsp_ref, aw):
    tm = proj.shape[0]
    a_u = _gelu(proj[:, 0:aw])
    gv = _gelu(proj[:, aw:2 * aw])
    mu = jnp.mean(gv, axis=-1, keepdims=True)
    xc = gv - mu
    rstd = lax.rsqrt(jnp.mean(xc * xc, axis=-1, keepdims=True) + EPS)
    vn = xc * rstd
    a_v = (vn * lng + lnb).astype(BF16)
    gd = aw // A_GROUPS
    rows = []
    for c in range(tm // CHUNK):
        cols = []
        for g in range(A_GROUPS):
            blk = a_v[c * CHUNK:(c + 1) * CHUNK, g * gd:(g + 1) * gd]
            cols.append(_dot(wsp_ref[g], blk, NN) + bsp_ref[g])
        rows.append(jnp.concatenate(cols, axis=1))
    mixed = jnp.concatenate(rows, axis=0)
    return a_u, vn, rstd, a_v, mixed


def _even_core_fwd(proj, x0, lng, lnb, wsp, bspb, cw, gath, tm):
    t, dm = x0.shape
    aw = lng.shape[1]
    bw = cw.shape[1]
    assert aw == bw and 2 * aw + 3 * bw == proj.shape[1]
    nt = t // tm
    prev, nxt = _halo_specs(tm, t, bw, [3, 4])

    def body(proj_ref, cp_ref, hp_ref, cn_ref, hn_ref, x0_ref, lng_ref, lnb_ref, wsp_ref, bsp_ref, cw_ref, gath_ref,
             x1_ref, y_ref, wbuf, sems):
        i = pl.program_id(0)

        @pl.when(i == 0)
        def _():
            _load_weight(gath_ref, wbuf, sems)
        proj_v = proj_ref[...]
        a_u, _, _, _, mixed = _gating_fwd(proj_v, lng_ref[...], lnb_ref[...], wsp_ref, bsp_ref, aw)
        a_out = a_u * mixed
        bb = proj_v[:, 2 * aw:2 * aw + bw]
        z = proj_v[:, 2 * aw + bw:2 * aw + 2 * bw] * proj_v[:, 2 * aw + 2 * bw:]
        zprev = jnp.where(i > 0, cp_ref[7:8, :] * hp_ref[7:8, :], 0.0)
        znext = jnp.where(i < nt - 1, cn_ref[0:1, :] * hn_ref[0:1, :], 0.0)
        zm1, zp1 = _shift_rows(z, zprev, znext)
        cwv = cw_ref[...]
        conv = zm1 * cwv[0:1, :] + z * cwv[1:2, :] + zp1 * cwv[2:3, :]
        y = jnp.concatenate([a_out, bb * conv], axis=1).astype(BF16)
        y_ref[...] = y
        x1_ref[...] = x0_ref[...] + _dot(y, wbuf[...], NN)

    return dict(
        body=body, grid=(nt,), name="even_core_fwd",
        args=[proj, proj, proj, proj, proj, x0, lng, lnb, wsp, bspb, cw, gath],
        out_shape=[jax.ShapeDtypeStruct((t, dm), F32), jax.ShapeDtypeStruct((t, aw + bw), BF16)],
        in_specs=[_tok(tm, proj.shape[1]), prev[0], prev[1], nxt[0], nxt[1], _tok(tm, dm), _full(lng.shape),
                  _full(lnb.shape), _full(wsp.shape), _full(bspb.shape), _full(cw.shape), ANY],
        out_specs=[_tok(tm, dm), _tok(tm, aw + bw)],
        scratch=[pltpu.VMEM((gath.shape[1] * NDEV, dm), BF16), pltpu.SemaphoreType.DMA((NDEV,))])


def _even_core_bwd(proj, dx1, lng, lnb, wsp, bspb, cw, gath, tm):
    t, dm = dx1.shape
    aw, bw = lng.shape[1], cw.shape[1]
    gd = aw // A_GROUPS
    nt = t // tm
    inw = proj.shape[1]
    prev, nxt = _halo_specs(tm, t, bw, [2, 3, 4])
    nb8 = tm // 8
    last8 = t // 8 - 1

    def body(proj_ref, bp_ref, cp_ref, hp_ref, bn_ref, cn_ref, hn_ref, dx_ref, dxp_ref, dxn_ref,
             lng_ref, lnb_ref, wsp_ref, bsp_ref, cw_ref, gath_ref,
             dproj_ref, dlng_ref, dlnb_ref, dwsp_ref, dbsp_ref, dcw_ref, wbuf, sems):
        i = pl.program_id(0)

        @pl.when(i == 0)
        def _():
            _load_weight(gath_ref, wbuf, sems)
            dlng_ref[...] = jnp.zeros_like(dlng_ref)
            dlnb_ref[...] = jnp.zeros_like(dlnb_ref)
            dwsp_ref[...] = jnp.zeros_like(dwsp_ref)
            dbsp_ref[...] = jnp.zeros_like(dbsp_ref)
            dcw_ref[...] = jnp.zeros_like(dcw_ref)
        proj_v = proj_ref[...]
        lng_v = lng_ref[...]
        a_u, vn, rstd, a_v, mixed = _gating_fwd(proj_v, lng_v, lnb_ref[...], wsp_ref, bsp_ref, aw)
        w = wbuf[...]
        dy = _dot(dx_ref[...].astype(BF16), w, NT)
        da_out, db_out = dy[:, 0:aw], dy[:, aw:]
        da_u = da_out * mixed
        dmixed = da_out * a_u
        dmb = dmixed.astype(BF16)
        rows = []
        for c in range(tm // CHUNK):
            cols = []
            for g in range(A_GROUPS):
                r0, c0 = c * CHUNK, g * gd
                dm_cg = dmb[r0:r0 + CHUNK, c0:c0 + gd]
                cols.append(_dot(wsp_ref[g], dm_cg, TN))
                dwsp_ref[g] += _dot(dm_cg, a_v[r0:r0 + CHUNK, c0:c0 + gd], NT)
                dbsp_ref[g] += dmixed[r0:r0 + CHUNK, c0:c0 + gd]
            rows.append(jnp.concatenate(cols, axis=1))
        dav = jnp.concatenate(rows, axis=0)
        dlng_ref[...] += jnp.sum(dav * vn, axis=0, keepdims=True)
        dlnb_ref[...] += jnp.sum(dav, axis=0, keepdims=True)
        dvn = dav * lng_v
        dgv = rstd * (dvn - jnp.mean(dvn, axis=-1, keepdims=True) - vn * jnp.mean(dvn * vn, axis=-1, keepdims=True))
        dv_pre = dgv * _gelu_grad(proj_v[:, aw:2 * aw])
        du_pre = da_u * _gelu_grad(proj_v[:, 0:aw])
        bb = proj_v[:, 2 * aw:2 * aw + bw]
        bc = proj_v[:, 2 * aw + bw:2 * aw + 2 * bw]
        bh = proj_v[:, 2 * aw + 2 * bw:]
        z = bc * bh
        zprev = jnp.where(i > 0, cp_ref[7:8, :] * hp_ref[7:8, :], 0.0)
        znext = jnp.where(i < nt - 1, cn_ref[0:1, :] * hn_ref[0:1, :], 0.0)
        zm1, zp1 = _shift_rows(z, zprev, znext)
        cwv = cw_ref[...]
        conv = zm1 * cwv[0:1, :] + z * cwv[1:2, :] + zp1 * cwv[2:3, :]
        dbb = db_out * conv
        dconv = db_out * bb
        dx_edge = jnp.concatenate([dxp_ref[...], dxn_ref[...]], axis=0).astype(BF16)
        dy_edge = _dot(dx_edge, w[aw:, :], NT)
        dcprev = jnp.where(i > 0, dy_edge[7:8, :] * bp_ref[7:8, :], 0.0)
        dcnext = jnp.where(i < nt - 1, dy_edge[8:9, :] * bn_ref[0:1, :], 0.0)
        dcm1, dcp1 = _shift_rows(dconv, dcprev, dcnext)
        dz = dcp1 * cwv[0:1, :] + dconv * cwv[1:2, :] + dcm1 * cwv[2:3, :]
        dcw_ref[0:1, :] += jnp.sum(dconv * zm1, axis=0, keepdims=True)
        dcw_ref[1:2, :] += jnp.sum(dconv * z, axis=0, keepdims=True)
        dcw_ref[2:3, :] += jnp.sum(dconv * zp1, axis=0, keepdims=True)
        dproj_ref[...] = jnp.concatenate([du_pre, dv_pre, dbb, dz * bh, dz * bc], axis=1).astype(BF16)

    row8 = lambda f: pl.BlockSpec((8, dm), f)
    return dict(
        body=body, grid=(nt,), name="even_core_bwd",
        args=[proj, proj, proj, proj, proj, proj, proj, dx1, dx1, dx1, lng, lnb, wsp, bspb, cw, gath],
        out_shape=[jax.ShapeDtypeStruct((t, inw), BF16), jax.ShapeDtypeStruct((1, aw), F32),
                   jax.ShapeDtypeStruct((1, aw), F32), jax.ShapeDtypeStruct(wsp.shape, F32),
                   jax.ShapeDtypeStruct((A_GROUPS, CHUNK, gd), F32), jax.ShapeDtypeStruct(cw.shape, F32)],
        in_specs=[_tok(tm, inw), prev[0], prev[1], prev[2], nxt[0], nxt[1], nxt[2], _tok(tm, dm),
                  row8(lambda i: (jnp.maximum(i * nb8 - 1, 0), 0)), row8(lambda i: (jnp.minimum((i + 1) * nb8, last8), 0)),
                  _full(lng.shape), _full(lnb.shape), _full(wsp.shape), _full(bspb.shape), _full(cw.shape), ANY],
        out_specs=[_tok(tm, inw), _full((1, aw)), _full((1, aw)), _full(wsp.shape),
                   _full((A_GROUPS, CHUNK, gd)), _full(cw.shape)],
        scratch=[pltpu.VMEM((gath.shape[1] * NDEV, dm), BF16), pltpu.SemaphoreType.DMA((NDEV,))])


def _ff_chunks(f, width=1024):
    return [(c0, min(c0 + width, f)) for c0 in range(0, f, width)]


def _ffn_up(x, gain, gath_g, gath_u, name, tm):
    t, dm = x.shape
    f = gath_g.shape[1] * NDEV

    def body(x_ref, g_ref, gg_ref, gu_ref, gate_ref, up_ref, act_ref, wg, wu, sems):
        @pl.when(pl.program_id(0) == 0)
        def _():
            _load_weight(gg_ref, wg, sems)
            _load_weight(gu_ref, wu, sems)
        h, _ = _rms_fwd(x_ref[...], g_ref[...])
        hb = h.astype(BF16)
        for c0, c1 in _ff_chunks(f):
            gate = _dot(hb, wg[c0:c1, :], NT)
            up = _dot(hb, wu[c0:c1, :], NT)
            gate_ref[:, c0:c1] = gate.astype(BF16)
            up_ref[:, c0:c1] = up.astype(BF16)
            act_ref[:, c0:c1] = (gate * _sigmoid(gate) * up).astype(BF16)

    o = jax.ShapeDtypeStruct((t, f), BF16)
    return dict(
        body=body, grid=(t // tm,), name=name, args=[x, gain, gath_g, gath_u], out_shape=[o, o, o],
        in_specs=[_tok(tm, dm), _full((1, dm)), ANY, ANY], out_specs=[_tok(tm, f)] * 3,
        scratch=[pltpu.VMEM((f, dm), BF16), pltpu.VMEM((f, dm), BF16), pltpu.SemaphoreType.DMA((NDEV,))])


def _ffn_down(x, act, gath_d, name, tm):
    t, dm = x.shape
    f = act.shape[1]

    def body(x_ref, a_ref, gd_ref, xo_ref, wd, sems):
        @pl.when(pl.program_id(0) == 0)
        def _():
            _load_weight(gd_ref, wd, sems)
        xo_ref[...] = x_ref[...] + _dot(a_ref[...], wd[...], NN)

    return dict(
        body=body, grid=(t // tm,), name=name, args=[x, act, gath_d], out_shape=[jax.ShapeDtypeStruct((t, dm), F32)],
        in_specs=[_tok(tm, dm), _tok(tm, f), ANY], out_specs=[_tok(tm, dm)],
        scratch=[pltpu.VMEM((f, dm), BF16), pltpu.SemaphoreType.DMA((NDEV,))])


def _ffn_down_loss(x, act, gath_d, target, gain, name, tm):
    t, dm = x.shape
    f = act.shape[1]
    steps = t // tm

    def body(x_ref, a_ref, gd_ref, t_ref, g_ref, loss_ref, dx_ref, dxb_ref, dgain_ref, wd, acc, sems):
        i = pl.program_id(0)

        @pl.when(i == 0)
        def _():
            _load_weight(gd_ref, wd, sems)
            acc[...] = jnp.zeros_like(acc)
            dgain_ref[...] = jnp.zeros_like(dgain_ref)
        xv = x_ref[...] + _dot(a_ref[...], wd[...], NN)
        gain_v = g_ref[...]
        y, r = _rms_fwd(xv, gain_v)
        e = y - t_ref[...]
        acc[...] += jnp.sum(e * e, axis=0, keepdims=True)
        dx, dgain = _rms_bwd(e * (1.0 / dm), xv, r, gain_v)
        dx_ref[...] = dx
        dxb_ref[...] = dx.astype(BF16)
        dgain_ref[...] += dgain

        @pl.when(i == steps - 1)
        def _():
            loss_ref[...] = jnp.sum(acc[...], axis=-1, keepdims=True) * (0.5 / dm)

    return dict(
        body=body, grid=(steps,), name=name, args=[x, act, gath_d, target, gain],
        out_shape=[jax.ShapeDtypeStruct((1, 1), F32), jax.ShapeDtypeStruct((t, dm), F32),
                   jax.ShapeDtypeStruct((t, dm), BF16), jax.ShapeDtypeStruct((1, dm), F32)],
        in_specs=[_tok(tm, dm), _tok(tm, f), ANY, _tok(tm, dm), _full((1, dm))],
        out_specs=[_full((1, 1)), _tok(tm, dm), _tok(tm, dm), _full((1, dm))],
        scratch=[pltpu.VMEM((f, dm), BF16), pltpu.VMEM((1, dm), F32), pltpu.SemaphoreType.DMA((NDEV,))])


def _ffn_bwd(dxo, x, gate, up, gain, gath_g, gath_u, gath_d, name, tm):
    t, dm = x.shape
    f = gate.shape[1]

    def body(dxo_ref, x_ref, gate_ref, up_ref, g_ref, gg_ref, gu_ref, gd_ref,
             dx_ref, dxb_ref, dg_ref, du_ref, hb_ref, dgain_ref, wg, wu, wd, sems):
        @pl.when(pl.program_id(0) == 0)
        def _():
            _load_weight(gg_ref, wg, sems)
            _load_weight(gu_ref, wu, sems)
            _load_weight(gd_ref, wd, sems)
            dgain_ref[...] = jnp.zeros_like(dgain_ref)
        xv, gain_v, dxo_v = x_ref[...], g_ref[...], dxo_ref[...]
        h, r = _rms_fwd(xv, gain_v)
        hb_ref[...] = h.astype(BF16)
        dxob = dxo_v.astype(BF16)
        dh = jnp.zeros_like(xv)
        for c0, c1 in _ff_chunks(f):
            gate_v = gate_ref[:, c0:c1].astype(F32)
            up_v = up_ref[:, c0:c1].astype(F32)
            s = _sigmoid(gate_v)
            silu = gate_v * s
            dact = _dot(dxob, wd[c0:c1, :], NT)
            dg = (dact * up_v * (s * (1.0 + gate_v * (1.0 - s)))).astype(BF16)
            du = (dact * silu).astype(BF16)
            dg_ref[:, c0:c1] = dg
            du_ref[:, c0:c1] = du
            dh = dh + _dot(dg, wg[c0:c1, :], NN) + _dot(du, wu[c0:c1, :], NN)
        dx, dgain = _rms_bwd(dh, xv, r, gain_v)
        dx = dxo_v + dx
        dx_ref[...] = dx
        dxb_ref[...] = dx.astype(BF16)
        dgain_ref[...] += dgain

    return dict(
        body=body, grid=(t // tm,), name=name, args=[dxo, x, gate, up, gain, gath_g, gath_u, gath_d],
        out_shape=[jax.ShapeDtypeStruct((t, dm), F32), jax.ShapeDtypeStruct((t, dm), BF16),
                   jax.ShapeDtypeStruct((t, f), BF16), jax.ShapeDtypeStruct((t, f), BF16),
                   jax.ShapeDtypeStruct((t, dm), BF16), jax.ShapeDtypeStruct((1, dm), F32)],
        in_specs=[_tok(tm, dm), _tok(tm, dm), _tok(tm, f), _tok(tm, f), _full((1, dm)), ANY, ANY, ANY],
        out_specs=[_tok(tm, dm), _tok(tm, dm), _tok(tm, f), _tok(tm, f), _tok(tm, dm), _full((1, dm))],
        scratch=[pltpu.VMEM((f, dm), BF16), pltpu.VMEM((f, dm), BF16), pltpu.VMEM((f, dm), BF16),
                 pltpu.SemaphoreType.DMA((NDEV,))])


def _t5_buckets(rel):
    nb = N_BUCKETS // 2
    ret = jnp.where(rel > 0, nb, 0)
    n = jnp.abs(rel)
    max_exact = nb // 2
    nf = jnp.maximum(n, 1).astype(jnp.float32)
    large = max_exact + (jnp.log(nf / max_exact) / math.log(MAX_DISTANCE / max_exact)
                         * (nb - max_exact)).astype(jnp.int32)
    large = jnp.minimum(large, nb - 1)
    return ret + jnp.where(n < max_exact, n, large)


def _bucket_table():
    qi = jnp.arange(CHUNK, dtype=jnp.int32)[:, None]
    kj = jnp.arange(3 * CHUNK, dtype=jnp.int32)[None, :]
    rel = kj - CHUNK - qi
    return jnp.where(jnp.abs(rel) <= CHUNK, _t5_buckets(rel), -1)


def _bias_table(rel_bias_t, buckets):
    nh = rel_bias_t.shape[0]

    def body(rb_ref, bk_ref, o_ref):
        bk = bk_ref[...]
        for h in range(nh):
            acc = jnp.where(bk < 0, NEG, 0.0).astype(F32)
            for b in range(N_BUCKETS):
                acc = jnp.where(bk == b, rb_ref[h, b] * LOG2E, acc)
            o_ref[h] = acc

    return dict(
        body=body, grid=(1,), name="bias_table", args=[rel_bias_t, buckets],
        out_shape=[jax.ShapeDtypeStruct((nh,) + buckets.shape, F32)],
        in_specs=[pl.BlockSpec(memory_space=pltpu.SMEM), _full(buckets.shape)],
        out_specs=[_full((nh,) + buckets.shape)])


def _rel_bias_grad(dbias, buckets):
    nh = dbias.shape[0]

    def body(db_ref, bk_ref, o_ref):
        bk = bk_ref[...]
        lane = lax.broadcasted_iota(jnp.int32, (1, 128), 1)
        for h in range(nh):
            d = db_ref[h]
            row = jnp.zeros((1, 128), F32)
            for b in range(N_BUCKETS):
                s = jnp.sum(jnp.sum(jnp.where(bk == b, d, 0.0), axis=1, keepdims=True), axis=0, keepdims=True)
                row = jnp.where(lane == b, s, row)
            o_ref[h:h + 1, :] = row

    return pl.pallas_call(
        body, out_shape=jax.ShapeDtypeStruct((nh, 128), F32),
        in_specs=[pl.BlockSpec(memory_space=pltpu.VMEM), pl.BlockSpec(memory_space=pltpu.VMEM)],
        out_specs=pl.BlockSpec(memory_space=pltpu.VMEM), compiler_params=_cp(0), name="rel_bias_grad")(dbias, buckets)


def _half_masks():
    lane = lax.broadcasted_iota(jnp.int32, (CHUNK, 128), 1)
    return lane < HEAD_DIM, lane >= HEAD_DIM


def _kv_low(ref, starts, hk, lo):
    kt = (hk // 2) * 128
    out = []
    for jj in range(3):
        blk = ref[pl.ds(starts[jj], CHUNK), kt:kt + 128]
        if hk % 2 == 1:
            blk = pltpu.roll(blk, HEAD_DIM, 1)
        out.append(jnp.where(lo, blk, jnp.zeros_like(blk)))
    return out


def _stack_heads(tile_a, tile_b):
    return jnp.concatenate([tile_a, pltpu.roll(tile_a, HEAD_DIM, 1), tile_b, pltpu.roll(tile_b, HEAD_DIM, 1)], axis=0)


def _unstack_heads(o4):
    return (o4[0:CHUNK] + pltpu.roll(o4[CHUNK:2 * CHUNK], HEAD_DIM, 1),
            o4[2 * CHUNK:3 * CHUNK] + pltpu.roll(o4[3 * CHUNK:], HEAD_DIM, 1))


ATT_SLAB = 32


def _softmax_slab(s_scr, hk, g, r0, bias_ref, sink_ref, n, nblk):
    scale = HEAD_DIM ** -0.5 * LOG2E
    h = (N_HEADS // N_KV) * hk + g
    s = []
    for jj in range(3):
        sj = (s_scr[hk, jj, pl.ds(g * CHUNK + r0, ATT_SLAB), :] * scale
              + bias_ref[h, pl.ds(r0, ATT_SLAB), jj * CHUNK:(jj + 1) * CHUNK])
        if jj == 0:
            sj = jnp.where(n > 0, sj, NEG)
        if jj == 2:
            sj = jnp.where(n < nblk - 1, sj, NEG)
        s.append(sj)
    sink = sink_ref[h] * LOG2E
    m = jnp.maximum(jnp.max(jnp.maximum(jnp.maximum(s[0], s[1]), s[2]), axis=-1, keepdims=True), sink)
    e = [jnp.exp2(sj - m) for sj in s]
    es = jnp.exp2(sink - m)
    inv = 1.0 / (jnp.sum(e[0] + e[1] + e[2], axis=-1, keepdims=True) + es)
    return [ej * inv for ej in e], es * inv


def _key_block_starts(n, nblk):
    return [pl.multiple_of(jnp.clip(n - 1 + jj, 0, nblk - 1) * CHUNK, CHUNK) for jj in range(3)]


def _attn_fwd(qkv, x2, bias, sink, gath):
    t, dm = x2.shape
    nblk = t // CHUNK
    kvw = N_KV * HEAD_DIM
    kcb, vcb = dm // kvw, dm // kvw + 1
    slab = (N_KV, 3, 4 * CHUNK, CHUNK)

    def body(q_ref, k_ref, v_ref, x2_ref, bias_ref, sink_ref, gath_ref, x3_ref, att_ref, p_ref, ps_ref,
             wbuf, s_scr, sems):
        n = pl.program_id(0)

        @pl.when(n == 0)
        def _():
            _load_weight(gath_ref, wbuf, sems)
        lo, _ = _half_masks()
        lane_s = lax.broadcasted_iota(jnp.int32, (ATT_SLAB, 128), 1)
        starts = _key_block_starts(n, nblk)
        tiles = []
        for hk in range(N_KV):
            c0 = (2 * hk) * 128
            k_lo = _kv_low(k_ref, starts, hk, lo)
            v_lo = _kv_low(v_ref, starts, hk, lo)
            q4 = _stack_heads(q_ref[:, c0:c0 + 128], q_ref[:, c0 + 128:c0 + 256])
            for jj in range(3):
                s_scr[hk, jj] = _dot(q4, k_lo[jj], NT)
            for g in range(4):
                h = 4 * hk + g
                for r0 in range(0, CHUNK, ATT_SLAB):
                    p, ps = _softmax_slab(s_scr, hk, g, r0, bias_ref, sink_ref, n, nblk)
                    for jj in range(3):
                        p_ref[hk, jj, g * CHUNK + r0:g * CHUNK + r0 + ATT_SLAB, :] = p[jj].astype(BF16)
                    rest = jnp.zeros((ATT_SLAB, 128), F32) if h == 0 else ps_ref[r0:r0 + ATT_SLAB, :]
                    ps_ref[r0:r0 + ATT_SLAB, :] = jnp.where(lane_s == h, ps, rest)
            o4 = _dot(p_ref[hk, 0], v_lo[0], NN) + _dot(p_ref[hk, 1], v_lo[1], NN) + _dot(p_ref[hk, 2], v_lo[2], NN)
            tiles += list(_unstack_heads(o4))
        att = jnp.concatenate(tiles, axis=1).astype(BF16)
        att_ref[...] = att
        x3_ref[...] = x2_ref[...] + _dot(att, wbuf[...], NN)

    blk = pl.BlockSpec((CHUNK, dm), lambda n: (n, 0))
    return dict(
        body=body, grid=(nblk,), name="attn_fwd", args=[qkv, qkv, qkv, x2, bias, sink, gath],
        out_shape=[jax.ShapeDtypeStruct((t, dm), F32), jax.ShapeDtypeStruct((t, dm), BF16),
                   jax.ShapeDtypeStruct((nblk,) + slab, BF16), jax.ShapeDtypeStruct((t, 128), F32)],
        in_specs=[blk, pl.BlockSpec((t, kvw), lambda n: (0, kcb)), pl.BlockSpec((t, kvw), lambda n: (0, vcb)), blk,
                  _full(bias.shape), pl.BlockSpec(memory_space=pltpu.SMEM), ANY],
        out_specs=[blk, blk, pl.BlockSpec((None,) + slab, lambda n: (n, 0, 0, 0, 0)),
                   pl.BlockSpec((CHUNK, 128), lambda n: (n, 0))],
        scratch=[pltpu.VMEM((gath.shape[1] * NDEV, dm), BF16), pltpu.VMEM(slab, F32),
                 pltpu.SemaphoreType.DMA((NDEV,))])


def _attn_bwd(qkv, att, probs, sink_probs, dx3, bias_shape, gath):
    t, dm = dx3.shape
    nblk = t // CHUNK
    kvw = N_KV * HEAD_DIM
    kcb, vcb = dm // kvw, dm // kvw + 1
    scale = HEAD_DIM ** -0.5
    slab = (N_KV, 3, 4 * CHUNK, CHUNK)

    def body(q_ref, k_ref, v_ref, att_ref, p_ref, ps_ref, dx_ref, gath_ref,
             dq_ref, dk_ref, dv_ref, dbias_ref, dsink_ref, wbuf, dp_scr, ds_scr, prod_scr, dsum_scr, sems):
        n = pl.program_id(0)

        @pl.when(n == 0)
        def _():
            _load_weight(gath_ref, wbuf, sems)
            dk_ref[...] = jnp.zeros_like(dk_ref)
            dv_ref[...] = jnp.zeros_like(dv_ref)
            dbias_ref[...] = jnp.zeros_like(dbias_ref)
            dsink_ref[...] = jnp.zeros_like(dsink_ref)
        lo, hi = _half_masks()
        lane_s = lax.broadcasted_iota(jnp.int32, (ATT_SLAB, 128), 1)
        starts = _key_block_starts(n, nblk)
        dout = _dot(dx_ref[...].astype(BF16), wbuf[...], NT)
        prod_scr[...] = dout * att_ref[...].astype(F32)
        doutb = dout.astype(BF16)
        dq_tiles = []
        for hk in range(N_KV):
            kt = (hk // 2) * 128
            c0 = (2 * hk) * 128
            k_lo = _kv_low(k_ref, starts, hk, lo)
            v_lo = _kv_low(v_ref, starts, hk, lo)
            q4 = _stack_heads(q_ref[:, c0:c0 + 128], q_ref[:, c0 + 128:c0 + 256])
            do4 = _stack_heads(doutb[:, c0:c0 + 128], doutb[:, c0 + 128:c0 + 256])
            for jj in range(3):
                dp_scr[hk, jj] = _dot(do4, v_lo[jj], NT)
            for g in range(4):
                h = 4 * hk + g
                for r0 in range(0, CHUNK, ATT_SLAB):
                    rows = slice(g * CHUNK + r0, g * CHUNK + r0 + ATT_SLAB)
                    pt = prod_scr[r0:r0 + ATT_SLAB, c0 + (g // 2) * 128:c0 + (g // 2 + 1) * 128]
                    msk = lane_s < HEAD_DIM if g % 2 == 0 else lane_s >= HEAD_DIM
                    dsum = jnp.sum(jnp.where(msk, pt, 0.0), axis=-1, keepdims=True)
                    rest = jnp.zeros((ATT_SLAB, 128), F32) if h == 0 else dsum_scr[r0:r0 + ATT_SLAB, :]
                    dsum_scr[r0:r0 + ATT_SLAB, :] = jnp.where(lane_s == h, dsum, rest)
                    for jj in range(3):
                        ds = p_ref[hk, jj, rows, :].astype(F32) * (dp_scr[hk, jj, rows, :] - dsum)
                        dbias_ref[h, r0:r0 + ATT_SLAB, jj * CHUNK:(jj + 1) * CHUNK] += ds
                        ds_scr[hk, jj, rows, :] = ds.astype(BF16)
            dq4 = jnp.zeros((4 * CHUNK, 128), F32)
            for jj in range(3):
                ds4 = ds_scr[hk, jj]
                dq4 = dq4 + _dot(ds4, k_lo[jj], NN) * scale
                dkj = _dot(ds4, q4, TN) * scale
                dvj = _dot(p_ref[hk, jj], do4, TN)
                if hk % 2 == 1:
                    dkj, dvj = pltpu.roll(dkj, HEAD_DIM, 1), pltpu.roll(dvj, HEAD_DIM, 1)
                keep = lo if hk % 2 == 0 else hi
                dk_ref[pl.ds(starts[jj], CHUNK), kt:kt + 128] += jnp.where(keep, dkj, 0.0)
                dv_ref[pl.ds(starts[jj], CHUNK), kt:kt + 128] += jnp.where(keep, dvj, 0.0)
            dq_tiles += list(_unstack_heads(dq4))
        dq_ref[...] = jnp.concatenate(dq_tiles, axis=1).astype(BF16)
        dsink_ref[...] -= jnp.sum(ps_ref[...] * dsum_scr[...], axis=0, keepdims=True)

    blk = pl.BlockSpec((CHUNK, dm), lambda n: (n, 0))
    return dict(
        body=body, grid=(nblk,), name="attn_bwd", args=[qkv, qkv, qkv, att, probs, sink_probs, dx3, gath],
        out_shape=[jax.ShapeDtypeStruct((t, dm), BF16), jax.ShapeDtypeStruct((t, kvw), F32),
                   jax.ShapeDtypeStruct((t, kvw), F32), jax.ShapeDtypeStruct(bias_shape, F32),
                   jax.ShapeDtypeStruct((1, 128), F32)],
        in_specs=[blk, pl.BlockSpec((t, kvw), lambda n: (0, kcb)), pl.BlockSpec((t, kvw), lambda n: (0, vcb)),
                  blk, pl.BlockSpec((None,) + slab, lambda n: (n, 0, 0, 0, 0)),
                  pl.BlockSpec((CHUNK, 128), lambda n: (n, 0)), blk, ANY],
        out_specs=[blk, _full((t, kvw)), _full((t, kvw)), _full(bias_shape), _full((1, 128))],
        scratch=[pltpu.VMEM((gath.shape[1] * NDEV, dm), BF16), pltpu.VMEM(slab, F32), pltpu.VMEM(slab, BF16),
                 pltpu.VMEM((CHUNK, dm), F32), pltpu.VMEM((CHUNK, 128), F32), pltpu.SemaphoreType.DMA((NDEV,))])


def _finish_weight(recvs, w, m, v, name):
    nl, r, dm = w.shape
    assert nl == len(recvs) and all(rc.shape[1:] == (r, dm) for rc in recvs)
    td = dm // 2
    wspec = pl.BlockSpec((None, r, td), lambda l, j: (l, 0, j))

    def body(*refs):
        r_refs = refs[:nl]
        w_ref, m_ref, v_ref, g_ref, d_ref, nm_ref, nv_ref = refs[nl:]
        layer = pl.program_id(0)
        for li in range(nl):
            @pl.when(layer == li)
            def _():
                g = r_refs[li][0].astype(F32)
                for d in range(1, recvs[li].shape[0]):
                    g = g + r_refs[li][d].astype(F32)
                delta, nm, nv = _adamw_math(w_ref[...], g, m_ref[...], v_ref[...])
                g_ref[...] = g
                d_ref[...] = delta
                nm_ref[...] = nm
                nv_ref[...] = nv

    o = jax.ShapeDtypeStruct(w.shape, F32)
    return dict(
        body=body, grid=(nl, 2), name=name, args=[*recvs, w, m, v], out_shape=[o, o, o, o],
        in_specs=[pl.BlockSpec((rc.shape[0], r, td), lambda l, j: (0, 0, j)) for rc in recvs] + [wspec] * 3,
        out_specs=[wspec] * 4)


def _adamw_small(w, g_slots, late_slots, m, v, name):
    r, c = w.shape
    nlate = late_slots.shape[1]

    def body(w_ref, g_ref, late_ref, m_ref, v_ref, gs_ref, d_ref, nm_ref, nv_ref):
        g = g_ref[0]
        late = late_ref[0]
        for d in range(1, NDEV):
            g = g + g_ref[d]
            late = late + late_ref[d]
        gs_ref[...] = g
        gs_ref[0:nlate, :] = late
        d_ref[...], nm_ref[...], nv_ref[...] = _adamw_math(w_ref[...], gs_ref[...], m_ref[...], v_ref[...])

    spec = pl.BlockSpec((r, c), lambda i: (0, 0))
    out = jax.ShapeDtypeStruct((r, c), F32)
    return pl.pallas_call(
        body, grid=(1,), out_shape=(out,) * 4,
        in_specs=[spec, pl.BlockSpec((NDEV, r, c), lambda i: (0, 0, 0)),
                  pl.BlockSpec((NDEV, nlate, c), lambda i: (0, 0, 0)), spec, spec], out_specs=(spec,) * 4,
        compiler_params=_cp(), name=name)(w, g_slots, late_slots, m, v)


def _pack_small(parts, rows):
    flat = jnp.concatenate([p.reshape(-1) for p in parts])
    return jnp.pad(flat, (0, rows * 128 - flat.shape[0])).reshape(rows, 128)


def _unpack_small(packed, shapes):
    flat = packed.reshape(-1)
    out, o = [], 0
    for s in shapes:
        n = int(np.prod(s))
        out.append(flat[o:o + n].reshape(s))
        o += n
    return out


def kernel(x, norm_mix, norm_ffn, even_w_in, even_v_ln_g, even_v_ln_b, even_w_spatial, even_b_spatial, even_conv_w, even_w_out, attn_w_qkv, attn_sink, rel_bias, attn_w_out, ffn_w_gate, ffn_w_up, ffn_w_down, final_norm, loss_target, m_norm_mix, m_norm_ffn, m_even_w_in, m_even_v_ln_g, m_even_v_ln_b, m_even_w_spatial, m_even_b_spatial, m_even_conv_w, m_even_w_out, m_attn_w_qkv, m_attn_sink, m_rel_bias, m_attn_w_out, m_ffn_w_gate, m_ffn_w_up, m_ffn_w_down, m_final_norm, v_norm_mix, v_norm_ffn, v_even_w_in, v_even_v_ln_g, v_even_v_ln_b, v_even_w_spatial, v_even_b_spatial, v_even_conv_w, v_even_w_out, v_attn_w_qkv, v_attn_sink, v_rel_bias, v_attn_w_out, v_ffn_w_gate, v_ffn_w_up, v_ffn_w_down, v_final_norm):
    t, dm = x.shape[1], x.shape[2]
    aw = even_v_ln_g.shape[1]
    bw = even_conv_w.shape[2] * NDEV
    gd = aw // A_GROUPS
    tm = min(512, t // 2)
    tmf = min(256, t // 2)
    me = _my_index()
    row = lambda a: a.reshape(1, -1)

    colT = lambda w: w.T.astype(BF16)
    sh = dict(winT=colT(even_w_in[0]), wqkvT=colT(attn_w_qkv[0]), wgT0=colT(ffn_w_gate[0]), wuT0=colT(ffn_w_up[0]),
              wgT1=colT(ffn_w_gate[1]), wuT1=colT(ffn_w_up[1]), woe=even_w_out[0].astype(BF16),
              woa=attn_w_out[0].astype(BF16), wd0=ffn_w_down[0].astype(BF16), wd1=ffn_w_down[1].astype(BF16))
    gather = lambda names: _GatherCarry([sh[n] for n in names])

    in_full = lambda a: lax.dynamic_update_slice(jnp.zeros((3, bw), F32), a[0], (0, me * (bw // NDEV)))
    cw_rows = 3 * bw // 128
    cw_mine = jnp.pad(in_full(even_conv_w).reshape(cw_rows, 128), ((0, 16 - cw_rows), (0, 0)))

    x0 = x[0]
    wsp_b = even_w_spatial[0].astype(BF16)
    bspb = jnp.broadcast_to(even_b_spatial[0][:, :, None], (A_GROUPS, CHUNK, gd))
    buckets = _bucket_table()
    sink = attn_sink[0]

    (bias,), ((g_winT,), (cw_slots,)) = _call(
        _bias_table(rel_bias.T, buckets), [gather(["winT"]), _BroadcastCarry(cw_mine)])
    cw_full = jnp.sum(cw_slots, axis=0)[0:cw_rows].reshape(3, bw)
    (proj, h0b), (g_woe, g_wgT0) = _call(_norm_proj(x0, row(norm_mix[0]), g_winT, F32, "in_proj", tm),
                                         gather(["woe", "wgT0"]))
    (x1, yb), (g_wuT0,) = _call(_even_core_fwd(proj, x0, even_v_ln_g, even_v_ln_b, wsp_b, bspb, cw_full, g_woe, tm),
                                gather(["wuT0"]))
    (gate0, up0, act0), (g_wd0,) = _call(_ffn_up(x1, row(norm_ffn[0]), g_wgT0, g_wuT0, "ffn_up0", tmf), gather(["wd0"]))
    (x2,), (g_wqkvT,) = _call(_ffn_down(x1, act0, g_wd0, "ffn_down0", tm), gather(["wqkvT"]))
    (qkv, h2b), (g_woa,) = _call(_norm_proj(x2, row(norm_mix[1]), g_wqkvT, BF16, "qkv_proj", tm), gather(["woa"]))
    (x3, attb, probs, sink_probs), (g_wgT1, g_wuT1) = _call(
        _attn_fwd(qkv, x2, bias, sink, g_woa), gather(["wgT1", "wuT1"]))
    (gate1, up1, act1), (g_wd1,) = _call(_ffn_up(x3, row(norm_ffn[1]), g_wgT1, g_wuT1, "ffn_up1", tmf), gather(["wd1"]))
    (loss_part, dx4, dx4b, d_final), _ = _call(
        _ffn_down_loss(x3, act1, g_wd1, loss_target[0], row(final_norm), "ffn_down1_loss", tm))

    (dx3, dx3b, dg1, du1, h3b, d_nffn1), _ = _call(
        _ffn_bwd(dx4, x3, gate1, up1, row(norm_ffn[1]), g_wgT1, g_wuT1, g_wd1, "ffn_bwd1", tmf))
    (p_wgT1,), _ = _call(_wgrad(dg1, h3b, "wgrad_gate1"))
    (p_wuT1,), (a_wgT1,) = _call(_wgrad(du1, h3b, "wgrad_up1"), _PairCarry(p_wgT1))
    (p_wd1,), (a_wuT1,) = _call(_wgrad(act1, dx4b, "wgrad_down1"), _PairCarry(p_wuT1))
    (dq, dk, dv, dbias, dsink), ((r_wgT1,), (a_wd1,)) = _call(
        _attn_bwd(qkv, attb, probs, sink_probs, dx3, bias.shape, g_woa),
        [_ChipSumCarry(p_wgT1, a_wgT1), _PairCarry(p_wd1)])
    (p_woa,), _ = _call(_wgrad(attb, dx3b, "wgrad_attn_out"))
    d_relb = _rel_bias_grad(dbias, buckets)[:, 0:N_BUCKETS].T
    dqkv = jnp.concatenate([dq, dk.astype(BF16), dv.astype(BF16)], axis=1)
    (dx2, dx2b, d_nmix1), (r_wuT1,) = _call(
        _proj_bwd_norm(dqkv, x2, row(norm_mix[1]), dx3, g_wqkvT, "qkv_bwd", tm), _ChipSumCarry(p_wuT1, a_wuT1))
    (p_wqkvT,), _ = _call(_wgrad(dqkv, h2b, "wgrad_qkv"))
    (dx1, dx1b, dg0, du0, h1b, d_nffn0), ((r_wd1,), (r_woa, r_wqkvT)) = _call(
        _ffn_bwd(dx2, x1, gate0, up0, row(norm_ffn[0]), g_wgT0, g_wuT0, g_wd0, "ffn_bwd0", tmf),
        [_ChipSumCarry(p_wd1, a_wd1), _GradCarry([p_woa, p_wqkvT])])
    (p_wgT0,), _ = _call(_wgrad(dg0, h1b, "wgrad_gate0"))
    (p_wuT0,), (a_wgT0,) = _call(_wgrad(du0, h1b, "wgrad_up0"), _PairCarry(p_wgT0))
    (p_wd0,), ((r_wgT0,), (a_wuT0,)) = _call(
        _wgrad(act0, dx2b, "wgrad_down0"), [_ChipSumCarry(p_wgT0, a_wgT0), _PairCarry(p_wuT0)])
    (dproj, d_lng, d_lnb, d_wsp, d_bsp3, d_cw), ((r_wuT0,), (a_wd0,)) = _call(
        _even_core_bwd(proj, dx1, even_v_ln_g, even_v_ln_b, wsp_b, bspb, cw_full, g_woe, tm),
        [_ChipSumCarry(p_wuT0, a_wuT0), _PairCarry(p_wd0)])
    small_shapes = [(2, dm), (2, dm), (1, aw), (1, aw), (1, A_GROUPS, CHUNK, CHUNK), (1, A_GROUPS, CHUNK), (3, bw),
                    (1, N_HEADS), (N_BUCKETS, N_HEADS), (dm,), (1, 1)]
    small_names = ["norm_mix", "norm_ffn", "even_v_ln_g", "even_v_ln_b", "even_w_spatial", "even_b_spatial",
                   "even_conv_w", "attn_sink", "rel_bias", "final_norm", "loss"]
    n_small = sum(int(np.prod(s)) for s in small_shapes)
    small_rows = 8 * ((n_small + 1023) // 1024)
    assert dm == 8 * 128
    small_part = _pack_small(
        [jnp.concatenate([jnp.zeros_like(d_nmix1), d_nmix1]), jnp.concatenate([d_nffn0, d_nffn1]), d_lng, d_lnb,
         d_wsp, jnp.sum(d_bsp3, axis=-1), d_cw, dsink[:, 0:N_HEADS], d_relb, d_final, loss_part], small_rows)
    (p_winT,), ((r_wd0,), (small_slots,)) = _call(
        _wgrad(dproj, h0b, "wgrad_in"), [_ChipSumCarry(p_wd0, a_wd0), _BroadcastCarry(small_part)])
    (p_woe,), (a_winT,) = _call(_wgrad(yb, dx1b, "wgrad_even_out"), _PairCarry(p_winT))
    (dx0, _, d_nmix0), ((r_winT,), (r_woe,)) = _call(
        _proj_bwd_norm(dproj, x0, row(norm_mix[0]), dx1, g_winT, "in_proj_bwd", tm),
        [_ChipSumCarry(p_winT, a_winT), _GradCarry([p_woe])])

    grads = {}

    order = ["norm_mix", "norm_ffn", "even_w_in", "even_v_ln_g", "even_v_ln_b", "even_w_spatial", "even_b_spatial",
             "even_conv_w", "even_w_out", "attn_w_qkv", "attn_sink", "rel_bias", "attn_w_out", "ffn_w_gate",
             "ffn_w_up", "ffn_w_down", "final_norm"]
    ws = dict(norm_mix=norm_mix, norm_ffn=norm_ffn, even_w_in=even_w_in, even_v_ln_g=even_v_ln_g,
              even_v_ln_b=even_v_ln_b, even_w_spatial=even_w_spatial, even_b_spatial=even_b_spatial,
              even_conv_w=even_conv_w, even_w_out=even_w_out, attn_w_qkv=attn_w_qkv, attn_sink=attn_sink,
              rel_bias=rel_bias, attn_w_out=attn_w_out, ffn_w_gate=ffn_w_gate, ffn_w_up=ffn_w_up,
              ffn_w_down=ffn_w_down, final_norm=final_norm)
    ms = dict(norm_mix=m_norm_mix, norm_ffn=m_norm_ffn, even_w_in=m_even_w_in, even_v_ln_g=m_even_v_ln_g,
              even_v_ln_b=m_even_v_ln_b, even_w_spatial=m_even_w_spatial, even_b_spatial=m_even_b_spatial,
              even_conv_w=m_even_conv_w, even_w_out=m_even_w_out, attn_w_qkv=m_attn_w_qkv, attn_sink=m_attn_sink,
              rel_bias=m_rel_bias, attn_w_out=m_attn_w_out, ffn_w_gate=m_ffn_w_gate, ffn_w_up=m_ffn_w_up,
              ffn_w_down=m_ffn_w_down, final_norm=m_final_norm)
    vs = dict(norm_mix=v_norm_mix, norm_ffn=v_norm_ffn, even_w_in=v_even_w_in, even_v_ln_g=v_even_v_ln_g,
              even_v_ln_b=v_even_v_ln_b, even_w_spatial=v_even_w_spatial, even_b_spatial=v_even_b_spatial,
              even_conv_w=v_even_conv_w, even_w_out=v_even_w_out, attn_w_qkv=v_attn_w_qkv, attn_sink=v_attn_sink,
              rel_bias=v_rel_bias, attn_w_out=v_attn_w_out, ffn_w_gate=v_ffn_w_gate, ffn_w_up=v_ffn_w_up,
              ffn_w_down=v_ffn_w_down, final_norm=v_final_norm)
    big = dict(ffn_w_gate=([r_wgT0, r_wgT1], True), even_w_in=([r_winT], True), even_w_out=([r_woe], False),
               attn_w_qkv=([r_wqkvT], True), attn_w_out=([r_woa], False), ffn_w_up=([r_wuT0, r_wuT1], True),
               ffn_w_down=([r_wd0, r_wd1], False))
    delta, new_m, new_v = {}, {}, {}
    late_slots = None
    for n, (recvs, transposed) in big.items():
        lay = (lambda a: jnp.swapaxes(a, 1, 2)) if transposed else (lambda a: a)
        spec = _finish_weight(recvs, lay(ws[n]), lay(ms[n]), lay(vs[n]), "finish_" + n)
        if late_slots is None:
            outs, (late_slots,) = _call(spec, _BroadcastCarry(d_nmix0.reshape(8, 128)))
        else:
            outs, _ = _call(spec)
        grads[n], delta[n], new_m[n], new_v[n] = [lay(o) for o in outs]
    pk = lambda dct: _pack_small(
        [in_full(dct[n]) if n == "even_conv_w" else (jnp.zeros((1, 1), F32) if n == "loss" else dct[n])
         for n in small_names], small_rows)
    packed = _adamw_small(pk(ws), small_slots, late_slots, pk(ms), pk(vs), "adamw_small")
    mine = lambda a: lax.dynamic_slice(a, (0, me * (bw // NDEV)), (3, bw // NDEV))[None]
    for dst, arr in zip((grads, delta, new_m, new_v), packed):
        for n, a in zip(small_names, _unpack_small(arr, small_shapes)):
            dst[n] = mine(a) if n == "even_conv_w" else a
    loss = grads["loss"][0, 0]
    return (loss, dx0[None], *[grads[n] for n in order], *[delta[n] for n in order],
            *[new_m[n] for n in order], *[new_v[n] for n in order])
```

```python
import math

import jax
import jax.numpy as jnp
import numpy as np
from jax import lax
from jax.experimental import pallas as pl
from jax.experimental.pallas import tpu as pltpu

F32, BF16 = jnp.float32, jnp.bfloat16
NDEV = 8
EPS = 1e-6
CHUNK = 128
A_GROUPS = 4
N_HEADS, N_KV, HEAD_DIM = 16, 4, 64
N_BUCKETS, MAX_DISTANCE = 32, 128
NEG = -1e30
LOG2E = 1.4426950408889634
ADAM_LR, ADAM_B1, ADAM_B2, ADAM_EPS, ADAM_WD, ADAM_STEP = 0.001, 0.9, 0.999, 1e-08, 0.01, 10
VMEM_LIMIT = 56 * 1024 * 1024
MESH = pl.DeviceIdType.MESH
NT = (((1,), (1,)), ((), ()))
NN = (((1,), (0,)), ((), ()))
TN = (((0,), (0,)), ((), ()))
ANY = pl.BlockSpec(memory_space=pl.ANY)


def _cp(n_grid=1):
    return pltpu.CompilerParams(dimension_semantics=("arbitrary",) * n_grid, vmem_limit_bytes=VMEM_LIMIT)


def _dot(a, b, dims):
    return lax.dot_general(a, b, dims, preferred_element_type=F32)


def _my_index():
    return 4 * lax.axis_index("x") + 2 * lax.axis_index("y") + lax.axis_index("c")


def _peer(k):
    x, y, c = lax.axis_index("x"), lax.axis_index("y"), lax.axis_index("c")
    px = 1 - x if k & 4 else x
    py = 1 - y if k & 2 else y
    pc = 1 - c if k & 1 else c
    return (px, py, pc)


def _load_weight(gath_ref, wbuf, sems):
    rows = gath_ref.shape[1]
    cps = [pltpu.make_async_copy(gath_ref.at[d], wbuf.at[pl.ds(d * rows, rows), :], sems.at[d]) for d in range(NDEV)]
    for c in cps:
        c.start()
    for c in cps:
        c.wait()


class _GatherCarry:
    def __init__(self, pieces):
        self.inputs = list(pieces)
        self.n = len(pieces)
        self.out_shape = [jax.ShapeDtypeStruct((NDEV,) + p.shape, p.dtype) for p in pieces]
        self.scratch = [pltpu.SemaphoreType.DMA((7 * self.n,)), pltpu.SemaphoreType.DMA((7 * self.n,)),
                        pltpu.SemaphoreType.DMA((self.n,))]

    def _ctx(self):
        x, y, c = lax.axis_index("x"), lax.axis_index("y"), lax.axis_index("c")
        chips = [(1 - x, y), (x, 1 - y), (1 - x, 1 - y)]
        return (x, y, c), (x, y, 1 - c), chips, c

    def _copy(self, k, j, block, to, ins, outs, sems, src=None):
        send_sems, recv_sems, _ = sems
        slot = outs[j].at[4 * block[0] + 2 * block[1] + block[2]]
        return pltpu.make_async_remote_copy(
            src_ref=slot if src is None else src, dst_ref=slot, send_sem=send_sems.at[k * self.n + j],
            recv_sem=recv_sems.at[k * self.n + j], device_id=to, device_id_type=MESH)

    def start(self, ins, outs, sems):
        me, sibling, chips, c = self._ctx()
        for j in range(self.n):
            pltpu.make_async_copy(ins[j], outs[j].at[4 * me[0] + 2 * me[1] + me[2]], sems[2].at[j]).start()
            self._copy(0, j, me, sibling, ins, outs, sems, src=ins[j]).start()
            for q, chip in enumerate(chips):
                self._copy(1 + q, j, me, (*chip, c), ins, outs, sems, src=ins[j]).start()

    def mid(self, ins, outs, sems):
        me, sibling, chips, c = self._ctx()
        for q, chip in enumerate(chips):
            for j in range(self.n):
                self._copy(1 + q, j, (*chip, c), me, ins, outs, sems).wait_recv()
                self._copy(4 + q, j, (*chip, c), sibling, ins, outs, sems).start()

    def finish(self, ins, outs, sems):
        me, sibling, chips, c = self._ctx()
        for j in range(self.n):
            self._copy(0, j, sibling, me, ins, outs, sems).wait_recv()
            for q, chip in enumerate(chips):
                self._copy(4 + q, j, (*chip, 1 - c), me, ins, outs, sems).wait_recv()
        for j in range(self.n):
            self._copy(0, j, me, sibling, ins, outs, sems, src=ins[j]).wait_send()
            for q, chip in enumerate(chips):
                self._copy(1 + q, j, me, (*chip, c), ins, outs, sems, src=ins[j]).wait_send()
                self._copy(4 + q, j, (*chip, c), sibling, ins, outs, sems).wait_send()
            pltpu.make_async_copy(ins[j], outs[j].at[0], sems[2].at[j]).wait()


class _GradCarry:
    def __init__(self, pieces):
        self.inputs = list(pieces)
        self.n = len(pieces)
        self.rows = [p.shape[0] // NDEV for p in pieces]
        self.out_shape = [jax.ShapeDtypeStruct((NDEV, r, p.shape[1]), p.dtype) for p, r in zip(pieces, self.rows)]
        self.scratch = [pltpu.SemaphoreType.DMA((7 * self.n,)), pltpu.SemaphoreType.DMA((7 * self.n,)),
                        pltpu.SemaphoreType.DMA((self.n,))]

    def _copies(self, ins, outs, sems):
        me = _my_index()
        local, remote = [], []
        for j in range(self.n):
            r = self.rows[j]
            local.append(pltpu.make_async_copy(ins[j].at[pl.ds(pl.multiple_of(me * r, 16), r), :], outs[j].at[me],
                                               sems[2].at[j]))
            for k in range(1, NDEV):
                peer = _peer(k)
                pidx = 4 * peer[0] + 2 * peer[1] + peer[2]
                remote.append(pltpu.make_async_remote_copy(
                    src_ref=ins[j].at[pl.ds(pl.multiple_of(pidx * r, 16), r), :], dst_ref=outs[j].at[me],
                    send_sem=sems[0].at[(k - 1) * self.n + j], recv_sem=sems[1].at[(k - 1) * self.n + j],
                    device_id=peer, device_id_type=MESH))
        return local, remote

    def start(self, ins, outs, sems):
        local, remote = self._copies(ins, outs, sems)
        for cp in local + remote:
            cp.start()

    def mid(self, ins, outs, sems):
        pass

    def finish(self, ins, outs, sems):
        local, remote = self._copies(ins, outs, sems)
        for cp in remote + local:
            cp.wait()


class _BroadcastCarry:
    def __init__(self, part):
        self.inputs = [part]
        self.out_shape = [jax.ShapeDtypeStruct((NDEV,) + part.shape, part.dtype)]
        self.scratch = [pltpu.SemaphoreType.DMA((7,)), pltpu.SemaphoreType.DMA((7,)), pltpu.SemaphoreType.DMA(())]

    def _copies(self, ins, outs, sems):
        me = _my_index()
        local = pltpu.make_async_copy(ins[0], outs[0].at[me], sems[2])
        remote = [pltpu.make_async_remote_copy(
            src_ref=ins[0], dst_ref=outs[0].at[me], send_sem=sems[0].at[k - 1], recv_sem=sems[1].at[k - 1],
            device_id=_peer(k), device_id_type=MESH) for k in range(1, NDEV)]
        return [local] + remote

    def start(self, ins, outs, sems):
        for cp in self._copies(ins, outs, sems):
            cp.start()

    def mid(self, ins, outs, sems):
        pass

    def finish(self, ins, outs, sems):
        for cp in self._copies(ins, outs, sems):
            cp.wait()


class _PairCarry:
    def __init__(self, piece):
        self.inputs = [piece]
        self.r = piece.shape[0] // NDEV
        self.out_shape = [jax.ShapeDtypeStruct((4, self.r, piece.shape[1]), piece.dtype)]
        self.scratch = [pltpu.SemaphoreType.DMA((4,)), pltpu.SemaphoreType.DMA((4,))]

    def _copies(self, ins, outs, sems):
        x, y, c = lax.axis_index("x"), lax.axis_index("y"), lax.axis_index("c")
        return [pltpu.make_async_remote_copy(
            src_ref=ins[0].at[pl.ds(pl.multiple_of((2 * q + 1 - c) * self.r, 16), self.r), :], dst_ref=outs[0].at[q],
            send_sem=sems[0].at[q], recv_sem=sems[1].at[q], device_id=(x, y, 1 - c), device_id_type=MESH)
            for q in range(4)]

    def start(self, ins, outs, sems):
        for cp in self._copies(ins, outs, sems):
            cp.start()

    def mid(self, ins, outs, sems):
        pass

    def finish(self, ins, outs, sems):
        for cp in self._copies(ins, outs, sems):
            cp.wait()


class _ChipSumCarry:
    def __init__(self, piece, landed):
        self.inputs = [piece, landed]
        self.r, dm = piece.shape[0] // NDEV, piece.shape[1]
        self.out_shape = [jax.ShapeDtypeStruct((4, self.r, dm), piece.dtype)]
        self.scratch = [pltpu.VMEM((4, self.r, dm), piece.dtype), pltpu.VMEM((8, self.r, dm), piece.dtype),
                        pltpu.SemaphoreType.DMA((8,)), pltpu.SemaphoreType.DMA((3,)), pltpu.SemaphoreType.DMA((3,)),
                        pltpu.SemaphoreType.DMA(())]

    def _copies(self, outs, scr):
        sums, _, _, send_sems, recv_sems, local_sem = scr
        x, y, c = lax.axis_index("x"), lax.axis_index("y"), lax.axis_index("c")
        mine = 2 * x + y
        local = pltpu.make_async_copy(sums.at[mine], outs[0].at[mine], local_sem)
        remote = []
        for k in range(1, 4):
            px = 1 - x if k & 2 else x
            py = 1 - y if k & 1 else y
            remote.append(pltpu.make_async_remote_copy(
                src_ref=sums.at[2 * px + py], dst_ref=outs[0].at[mine], send_sem=send_sems.at[k - 1],
                recv_sem=recv_sems.at[k - 1], device_id=(px, py, c), device_id_type=MESH))
        return local, remote

    def start(self, ins, outs, scr):
        sums, stage, stage_sems = scr[0], scr[1], scr[2]
        c = lax.axis_index("c")
        loads = []
        for q in range(4):
            loads.append((
                pltpu.make_async_copy(ins[0].at[pl.ds(pl.multiple_of((2 * q + c) * self.r, 16), self.r), :],
                                      stage.at[2 * q], stage_sems.at[2 * q]),
                pltpu.make_async_copy(ins[1].at[q], stage.at[2 * q + 1], stage_sems.at[2 * q + 1])))
        for a, b in loads:
            a.start()
            b.start()
        for q, (a, b) in enumerate(loads):
            a.wait()
            b.wait()
            sums[q] = (stage[2 * q].astype(F32) + stage[2 * q + 1].astype(F32)).astype(sums.dtype)
        local, remote = self._copies(outs, scr)
        for cp in [local] + remote:
            cp.start()

    def mid(self, ins, outs, scr):
        pass

    def finish(self, ins, outs, scr):
        local, remote = self._copies(outs, scr)
        for cp in remote + [local]:
            cp.wait()


def _call(spec, carry=None):
    body, grid = spec["body"], spec["grid"]
    in_specs, out_specs, out_shape = list(spec["in_specs"]), list(spec["out_specs"]), list(spec["out_shape"])
    scratch, args = list(spec.get("scratch", [])), list(spec["args"])
    if carry is None:
        out = pl.pallas_call(body, grid=grid, in_specs=in_specs, out_specs=tuple(out_specs),
                             out_shape=tuple(out_shape), scratch_shapes=scratch, compiler_params=_cp(len(grid)),
                             name=spec["name"])(*args)
        return tuple(out), ()
    carries = list(carry) if isinstance(carry, (list, tuple)) else [carry]
    n_in, n_out, n_s = len(in_specs), len(out_specs), len(scratch)
    steps = int(np.prod(grid))

    def split(refs, counts):
        parts, o = [], 0
        for cnt in counts:
            parts.append(refs[o:o + cnt])
            o += cnt
        return parts

    c_in = [len(cr.inputs) for cr in carries]
    c_out = [len(cr.out_shape) for cr in carries]
    c_scr = [len(cr.scratch) for cr in carries]

    def wrapped(*refs):
        ins, cins, outs, couts, scr, cscr = split(refs, [n_in, sum(c_in), n_out, sum(c_out), n_s, sum(c_scr)])
        per = list(zip(carries, split(cins, c_in), split(couts, c_out), split(cscr, c_scr)))
        step = pl.program_id(0)
        for ax in range(1, len(grid)):
            step = step * grid[ax] + pl.program_id(ax)

        @pl.when(step == 0)
        def _():
            for cr, ci, co, cs in per:
                cr.start(ci, co, cs)
        if steps >= 3:
            @pl.when(step == steps - 2)
            def _():
                for cr, ci, co, cs in per:
                    cr.mid(ci, co, cs)
        body(*ins, *outs, *scr)

        @pl.when(step == steps - 1)
        def _():
            for cr, ci, co, cs in per:
                if steps < 3:
                    cr.mid(ci, co, cs)
                cr.finish(ci, co, cs)

    out = pl.pallas_call(
        wrapped, grid=grid, in_specs=in_specs + [ANY] * sum(c_in), out_specs=tuple(out_specs + [ANY] * sum(c_out)),
        out_shape=tuple(out_shape + [s for cr in carries for s in cr.out_shape]),
        scratch_shapes=scratch + [s for cr in carries for s in cr.scratch],
        compiler_params=_cp(len(grid)), name=spec["name"])(*args, *[a for cr in carries for a in cr.inputs])
    c_res = [tuple(p) for p in split(out[n_out:], c_out)]
    return tuple(out[:n_out]), (c_res if isinstance(carry, (list, tuple)) else c_res[0])


def _rms_fwd(x, gain):
    r = lax.rsqrt(jnp.mean(x * x, axis=-1, keepdims=True) + EPS)
    return x * r * gain, r


def _rms_bwd(dh, x, r, gain):
    a = dh * gain
    dx = r * a - x * (r * r * r) * jnp.mean(a * x, axis=-1, keepdims=True)
    dgain = jnp.sum(dh * (x * r), axis=0, keepdims=True)
    return dx, dgain


def _gelu(x):
    return 0.5 * x * (1.0 + lax.erf(x * 0.7071067811865476))


def _gelu_grad(x):
    return 0.5 * (1.0 + lax.erf(x * 0.7071067811865476)) + x * jnp.exp(-0.5 * x * x) * 0.3989422804014327


def _sigmoid(x):
    return 1.0 / (1.0 + jnp.exp(-x))


def _adamw_math(w, g, m, v):
    nm = ADAM_B1 * m + (1.0 - ADAM_B1) * g
    nv = ADAM_B2 * v + (1.0 - ADAM_B2) * (g * g)
    m_hat = nm / (1.0 - ADAM_B1 ** ADAM_STEP)
    v_hat = nv / (1.0 - ADAM_B2 ** ADAM_STEP)
    return -ADAM_LR * (m_hat / (jnp.sqrt(v_hat) + ADAM_EPS) + ADAM_WD * w), nm, nv


def _tok(tm, w):
    return pl.BlockSpec((tm, w), lambda i: (i, 0))


def _full(shape):
    return pl.BlockSpec(shape, lambda *i: (0,) * len(shape))


def _norm_proj(x, gain, gath, out_dtype, name, tm):
    t, dm = x.shape
    n = gath.shape[1] * NDEV

    def body(x_ref, g_ref, gath_ref, proj_ref, hb_ref, wbuf, sems):
        @pl.when(pl.program_id(0) == 0)
        def _():
            _load_weight(gath_ref, wbuf, sems)
        h, _ = _rms_fwd(x_ref[...], g_ref[...])
        hb = h.astype(BF16)
        hb_ref[...] = hb
        proj_ref[...] = _dot(hb, wbuf[...], NT).astype(out_dtype)

    return dict(
        body=body, grid=(t // tm,), name=name, args=[x, gain, gath],
        out_shape=[jax.ShapeDtypeStruct((t, n), out_dtype), jax.ShapeDtypeStruct((t, dm), BF16)],
        in_specs=[_tok(tm, dm), _full((1, dm)), ANY], out_specs=[_tok(tm, n), _tok(tm, dm)],
        scratch=[pltpu.VMEM((n, dm), BF16), pltpu.SemaphoreType.DMA((NDEV,))])


def _proj_bwd_norm(dy, x, gain, dres, gath, name, tm):
    t, dm = x.shape
    n = gath.shape[1] * NDEV

    def body(dy_ref, x_ref, g_ref, dres_ref, gath_ref, dx_ref, dxb_ref, dgain_ref, wbuf, sems):
        @pl.when(pl.program_id(0) == 0)
        def _():
            _load_weight(gath_ref, wbuf, sems)
            dgain_ref[...] = jnp.zeros_like(dgain_ref)
        xv, gain_v = x_ref[...], g_ref[...]
        _, r = _rms_fwd(xv, gain_v)
        dh = _dot(dy_ref[...], wbuf[...], NN)
        dx, dgain = _rms_bwd(dh, xv, r, gain_v)
        dx = dres_ref[...] + dx
        dx_ref[...] = dx
        dxb_ref[...] = dx.astype(BF16)
        dgain_ref[...] += dgain

    return dict(
        body=body, grid=(t // tm,), name=name, args=[dy, x, gain, dres, gath],
        out_shape=[jax.ShapeDtypeStruct((t, dm), F32), jax.ShapeDtypeStruct((t, dm), BF16),
                   jax.ShapeDtypeStruct((1, dm), F32)],
        in_specs=[_tok(tm, n), _tok(tm, dm), _full((1, dm)), _tok(tm, dm), ANY],
        out_specs=[_tok(tm, dm), _tok(tm, dm), _full((1, dm))],
        scratch=[pltpu.VMEM((n, dm), BF16), pltpu.SemaphoreType.DMA((NDEV,))])


def _wgrad(a, b, name, tmm=256):
    t, m = a.shape
    n = b.shape[1]

    def body(a_ref, b_ref, o_ref):
        o_ref[...] = _dot(a_ref[...], b_ref[...], TN).astype(BF16)

    return dict(
        body=body, grid=(m // tmm,), name=name, args=[a, b], out_shape=[jax.ShapeDtypeStruct((m, n), BF16)],
        in_specs=[pl.BlockSpec((t, tmm), lambda j: (0, j)), pl.BlockSpec((t, n), lambda j: (0, 0))],
        out_specs=[pl.BlockSpec((tmm, n), lambda j: (j, 0))])


def _halo_specs(tm, t, width, col_blocks):
    nb8 = tm // 8
    last = t // 8 - 1
    prev = [pl.BlockSpec((8, width), lambda i, cb=cb: (jnp.maximum(i * nb8 - 1, 0), cb)) for cb in col_blocks]
    nxt = [pl.BlockSpec((8, width), lambda i, cb=cb: (jnp.minimum((i + 1) * nb8, last), cb)) for cb in col_blocks]
    return prev, nxt


def _shift_rows(z, prev_row, next_row):
    tm = z.shape[0]
    row = lax.broadcasted_iota(jnp.int32, z.shape, 0)
    zm1 = jnp.where(row == 0, prev_row, pltpu.roll(z, 1, 0))
    zp1 = jnp.where(row == tm - 1, next_row, pltpu.roll(z, tm - 1, 0))
    return zm1, zp1


def _gating_fwd(proj, lng, lnb, wsp_ref, bsp_ref, aw):
    tm = proj.shape[0]
    a_u = _gelu(proj[:, 0:aw])
    gv = _gelu(proj[:, aw:2 * aw])
    mu = jnp.mean(gv, axis=-1, keepdims=True)
    xc = gv - mu
    rstd = lax.rsqrt(jnp.mean(xc * xc, axis=-1, keepdims=True) + EPS)
    vn = xc * rstd
    a_v = (vn * lng + lnb).astype(BF16)
    gd = aw // A_GROUPS
    rows = []
    for c in range(tm // CHUNK):
        cols = []
        for g in range(A_GROUPS):
            blk = a_v[c * CHUNK:(c + 1) * CHUNK, g * gd:(g + 1) * gd]
            cols.append(_dot(wsp_ref[g], blk, NN) + bsp_ref[g])
        rows.append(jnp.concatenate(cols, axis=1))
    mixed = jnp.concatenate(rows, axis=0)
    return a_u, vn, rstd, a_v, mixed


def _even_core_fwd(proj, x0, lng, lnb, wsp, bspb, cw, gath, tm):
    t, dm = x0.shape
    aw = lng.shape[1]
    bw = cw.shape[1]
    assert aw == bw and 2 * aw + 3 * bw == proj.shape[1]
    nt = t // tm
    prev, nxt = _halo_specs(tm, t, bw, [3, 4])

    def body(proj_ref, cp_ref, hp_ref, cn_ref, hn_ref, x0_ref, lng_ref, lnb_ref, wsp_ref, bsp_ref, cw_ref, gath_ref,
             x1_ref, y_ref, wbuf, sems):
        i = pl.program_id(0)

        @pl.when(i == 0)
        def _():
            _load_weight(gath_ref, wbuf, sems)
        proj_v = proj_ref[...]
        a_u, _, _, _, mixed = _gating_fwd(proj_v, lng_ref[...], lnb_ref[...], wsp_ref, bsp_ref, aw)
        a_out = a_u * mixed
        bb = proj_v[:, 2 * aw:2 * aw + bw]
        z = proj_v[:, 2 * aw + bw:2 * aw + 2 * bw] * proj_v[:, 2 * aw + 2 * bw:]
        zprev = jnp.where(i > 0, cp_ref[7:8, :] * hp_ref[7:8, :], 0.0)
        znext = jnp.where(i < nt - 1, cn_ref[0:1, :] * hn_ref[0:1, :], 0.0)
        zm1, zp1 = _shift_rows(z, zprev, znext)
        cwv = cw_ref[...]
        conv = zm1 * cwv[0:1, :] + z * cwv[1:2, :] + zp1 * cwv[2:3, :]
        y = jnp.concatenate([a_out, bb * conv], axis=1).astype(BF16)
        y_ref[...] = y
        x1_ref[...] = x0_ref[...] + _dot(y, wbuf[...], NN)

    return dict(
        body=body, grid=(nt,), name="even_core_fwd",
        args=[proj, proj, proj, proj, proj, x0, lng, lnb, wsp, bspb, cw, gath],
        out_shape=[jax.ShapeDtypeStruct((t, dm), F32), jax.ShapeDtypeStruct((t, aw + bw), BF16)],
        in_specs=[_tok(tm, proj.shape[1]), prev[0], prev[1], nxt[0], nxt[1], _tok(tm, dm), _full(lng.shape),
                  _full(lnb.shape), _full(wsp.shape), _full(bspb.shape), _full(cw.shape), ANY],
        out_specs=[_tok(tm, dm), _tok(tm, aw + bw)],
        scratch=[pltpu.VMEM((gath.shape[1] * NDEV, dm), BF16), pltpu.SemaphoreType.DMA((NDEV,))])


def _even_core_bwd(proj, dx1, lng, lnb, wsp, bspb, cw, gath, tm):
    t, dm = dx1.shape
    aw, bw = lng.shape[1], cw.shape[1]
    gd = aw // A_GROUPS
    nt = t // tm
    inw = proj.shape[1]
    prev, nxt = _halo_specs(tm, t, bw, [2, 3, 4])
    nb8 = tm // 8
    last8 = t // 8 - 1

    def body(proj_ref, bp_ref, cp_ref, hp_ref, bn_ref, cn_ref, hn_ref, dx_ref, dxp_ref, dxn_ref,
             lng_ref, lnb_ref, wsp_ref, bsp_ref, cw_ref, gath_ref,
             dproj_ref, dlng_ref, dlnb_ref, dwsp_ref, dbsp_ref, dcw_ref, wbuf, sems):
        i = pl.program_id(0)

        @pl.when(i == 0)
        def _():
            _load_weight(gath_ref, wbuf, sems)
            dlng_ref[...] = jnp.zeros_like(dlng_ref)
            dlnb_ref[...] = jnp.zeros_like(dlnb_ref)
            dwsp_ref[...] = jnp.zeros_like(dwsp_ref)
            dbsp_ref[...] = jnp.zeros_like(dbsp_ref)
            dcw_ref[...] = jnp.zeros_like(dcw_ref)
        proj_v = proj_ref[...]
        lng_v = lng_ref[...]
        a_u, vn, rstd, a_v, mixed = _gating_fwd(proj_v, lng_v, lnb_ref[...], wsp_ref, bsp_ref, aw)
        w = wbuf[...]
        dy = _dot(dx_ref[...].astype(BF16), w, NT)
        da_out, db_out = dy[:, 0:aw], dy[:, aw:]
        da_u = da_out * mixed
        dmixed = da_out * a_u
        dmb = dmixed.astype(BF16)
        rows = []
        for c in range(tm // CHUNK):
            cols = []
            for g in range(A_GROUPS):
                r0, c0 = c * CHUNK, g * gd
                dm_cg = dmb[r0:r0 + CHUNK, c0:c0 + gd]
                cols.append(_dot(wsp_ref[g], dm_cg, TN))
                dwsp_ref[g] += _dot(dm_cg, a_v[r0:r0 + CHUNK, c0:c0 + gd], NT)
                dbsp_ref[g] += dmixed[r0:r0 + CHUNK, c0:c0 + gd]
            rows.append(jnp.concatenate(cols, axis=1))
        dav = jnp.concatenate(rows, axis=0)
        dlng_ref[...] += jnp.sum(dav * vn, axis=0, keepdims=True)
        dlnb_ref[...] += jnp.sum(dav, axis=0, keepdims=True)
        dvn = dav * lng_v
        dgv = rstd * (dvn - jnp.mean(dvn, axis=-1, keepdims=True) - vn * jnp.mean(dvn * vn, axis=-1, keepdims=True))
        dv_pre = dgv * _gelu_grad(proj_v[:, aw:2 * aw])
        du_pre = da_u * _gelu_grad(proj_v[:, 0:aw])
        bb = proj_v[:, 2 * aw:2 * aw + bw]
        bc = proj_v[:, 2 * aw + bw:2 * aw + 2 * bw]
        bh = proj_v[:, 2 * aw + 2 * bw:]
        z = bc * bh
        zprev = jnp.where(i > 0, cp_ref[7:8, :] * hp_ref[7:8, :], 0.0)
        znext = jnp.where(i < nt - 1, cn_ref[0:1, :] * hn_ref[0:1, :], 0.0)
        zm1, zp1 = _shift_rows(z, zprev, znext)
        cwv = cw_ref[...]
        conv = zm1 * cwv[0:1, :] + z * cwv[1:2, :] + zp1 * cwv[2:3, :]
        dbb = db_out * conv
        dconv = db_out * bb
        dx_edge = jnp.concatenate([dxp_ref[...], dxn_ref[...]], axis=0).astype(BF16)
        dy_edge = _dot(dx_edge, w[aw:, :], NT)
        dcprev = jnp.where(i > 0, dy_edge[7:8, :] * bp_ref[7:8, :], 0.0)
        dcnext = jnp.where(i < nt - 1, dy_edge[8:9, :] * bn_ref[0:1, :], 0.0)
        dcm1, dcp1 = _shift_rows(dconv, dcprev, dcnext)
        dz = dcp1 * cwv[0:1, :] + dconv * cwv[1:2, :] + dcm1 * cwv[2:3, :]
        dcw_ref[0:1, :] += jnp.sum(dconv * zm1, axis=0, keepdims=True)
        dcw_ref[1:2, :] += jnp.sum(dconv * z, axis=0, keepdims=True)
        dcw_ref[2:3, :] += jnp.sum(dconv * zp1, axis=0, keepdims=True)
        dproj_ref[...] = jnp.concatenate([du_pre, dv_pre, dbb, dz * bh, dz * bc], axis=1).astype(BF16)

    row8 = lambda f: pl.BlockSpec((8, dm), f)
    return dict(
        body=body, grid=(nt,), name="even_core_bwd",
        args=[proj, proj, proj, proj, proj, proj, proj, dx1, dx1, dx1, lng, lnb, wsp, bspb, cw, gath],
        out_shape=[jax.ShapeDtypeStruct((t, inw), BF16), jax.ShapeDtypeStruct((1, aw), F32),
                   jax.ShapeDtypeStruct((1, aw), F32), jax.ShapeDtypeStruct(wsp.shape, F32),
                   jax.ShapeDtypeStruct((A_GROUPS, CHUNK, gd), F32), jax.ShapeDtypeStruct(cw.shape, F32)],
        in_specs=[_tok(tm, inw), prev[0], prev[1], prev[2], nxt[0], nxt[1], nxt[2], _tok(tm, dm),
                  row8(lambda i: (jnp.maximum(i * nb8 - 1, 0), 0)), row8(lambda i: (jnp.minimum((i + 1) * nb8, last8), 0)),
                  _full(lng.shape), _full(lnb.shape), _full(wsp.shape), _full(bspb.shape), _full(cw.shape), ANY],
        out_specs=[_tok(tm, inw), _full((1, aw)), _full((1, aw)), _full(wsp.shape),
                   _full((A_GROUPS, CHUNK, gd)), _full(cw.shape)],
        scratch=[pltpu.VMEM((gath.shape[1] * NDEV, dm), BF16), pltpu.SemaphoreType.DMA((NDEV,))])


def _ff_chunks(f, width=1024):
    return [(c0, min(c0 + width, f)) for c0 in range(0, f, width)]


def _ffn_up(x, gain, gath_g, gath_u, name, tm):
    t, dm = x.shape
    f = gath_g.shape[1] * NDEV

    def body(x_ref, g_ref, gg_ref, gu_ref, gate_ref, up_ref, act_ref, wg, wu, sems):
        @pl.when(pl.program_id(0) == 0)
        def _():
            _load_weight(gg_ref, wg, sems)
            _load_weight(gu_ref, wu, sems)
        h, _ = _rms_fwd(x_ref[...], g_ref[...])
        hb = h.astype(BF16)
        for c0, c1 in _ff_chunks(f):
            gate = _dot(hb, wg[c0:c1, :], NT)
            up = _dot(hb, wu[c0:c1, :], NT)
            gate_ref[:, c0:c1] = gate.astype(BF16)
            up_ref[:, c0:c1] = up.astype(BF16)
            act_ref[:, c0:c1] = (gate * _sigmoid(gate) * up).astype(BF16)

    o = jax.ShapeDtypeStruct((t, f), BF16)
    return dict(
        body=body, grid=(t // tm,), name=name, args=[x, gain, gath_g, gath_u], out_shape=[o, o, o],
        in_specs=[_tok(tm, dm), _full((1, dm)), ANY, ANY], out_specs=[_tok(tm, f)] * 3,
        scratch=[pltpu.VMEM((f, dm), BF16), pltpu.VMEM((f, dm), BF16), pltpu.SemaphoreType.DMA((NDEV,))])


def _ffn_down(x, act, gath_d, name, tm):
    t, dm = x.shape
    f = act.shape[1]

    def body(x_ref, a_ref, gd_ref, xo_ref, wd, sems):
        @pl.when(pl.program_id(0) == 0)
        def _():
            _load_weight(gd_ref, wd, sems)
        xo_ref[...] = x_ref[...] + _dot(a_ref[...], wd[...], NN)

    return dict(
        body=body, grid=(t // tm,), name=name, args=[x, act, gath_d], out_shape=[jax.ShapeDtypeStruct((t, dm), F32)],
        in_specs=[_tok(tm, dm), _tok(tm, f), ANY], out_specs=[_tok(tm, dm)],
        scratch=[pltpu.VMEM((f, dm), BF16), pltpu.SemaphoreType.DMA((NDEV,))])


def _ffn_down_loss(x, act, gath_d, target, gain, name, tm):
    t, dm = x.shape
    f = act.shape[1]
    steps = t // tm

    def body(x_ref, a_ref, gd_ref, t_ref, g_ref, loss_ref, dx_ref, dxb_ref, dgain_ref, wd, acc, sems):
        i = pl.program_id(0)

        @pl.when(i == 0)
        def _():
            _load_weight(gd_ref, wd, sems)
            acc[...] = jnp.zeros_like(acc)
            dgain_ref[...] = jnp.zeros_like(dgain_ref)
        xv = x_ref[...] + _dot(a_ref[...], wd[...], NN)
        gain_v = g_ref[...]
        y, r = _rms_fwd(xv, gain_v)
        e = y - t_ref[...]
        acc[...] += jnp.sum(e * e, axis=0, keepdims=True)
        dx, dgain = _rms_bwd(e * (1.0 / dm), xv, r, gain_v)
        dx_ref[...] = dx
        dxb_ref[...] = dx.astype(BF16)
        dgain_ref[...] += dgain

        @pl.when(i == steps - 1)
        def _():
            loss_ref[...] = jnp.sum(acc[...], axis=-1, keepdims=True) * (0.5 / dm)

    return dict(
        body=body, grid=(steps,), name=name, args=[x, act, gath_d, target, gain],
        out_shape=[jax.ShapeDtypeStruct((1, 1), F32), jax.ShapeDtypeStruct((t, dm), F32),
                   jax.ShapeDtypeStruct((t, dm), BF16), jax.ShapeDtypeStruct((1, dm), F32)],
        in_specs=[_tok(tm, dm), _tok(tm, f), ANY, _tok(tm, dm), _full((1, dm))],
        out_specs=[_full((1, 1)), _tok(tm, dm), _tok(tm, dm), _full((1, dm))],
        scratch=[pltpu.VMEM((f, dm), BF16), pltpu.VMEM((1, dm), F32), pltpu.SemaphoreType.DMA((NDEV,))])


def _ffn_bwd(dxo, x, gate, up, gain, gath_g, gath_u, gath_d, name, tm):
    t, dm = x.shape
    f = gate.shape[1]

    def body(dxo_ref, x_ref, gate_ref, up_ref, g_ref, gg_ref, gu_ref, gd_ref,
             dx_ref, dxb_ref, dg_ref, du_ref, hb_ref, dgain_ref, wg, wu, wd, sems):
        @pl.when(pl.program_id(0) == 0)
        def _():
            _load_weight(gg_ref, wg, sems)
            _load_weight(gu_ref, wu, sems)
            _load_weight(gd_ref, wd, sems)
            dgain_ref[...] = jnp.zeros_like(dgain_ref)
        xv, gain_v, dxo_v = x_ref[...], g_ref[...], dxo_ref[...]
        h, r = _rms_fwd(xv, gain_v)
        hb_ref[...] = h.astype(BF16)
        dxob = dxo_v.astype(BF16)
        dh = jnp.zeros_like(xv)
        for c0, c1 in _ff_chunks(f):
            gate_v = gate_ref[:, c0:c1].astype(F32)
            up_v = up_ref[:, c0:c1].astype(F32)
            s = _sigmoid(gate_v)
            silu = gate_v * s
            dact = _dot(dxob, wd[c0:c1, :], NT)
            dg = (dact * up_v * (s * (1.0 + gate_v * (1.0 - s)))).astype(BF16)
            du = (dact * silu).astype(BF16)
            dg_ref[:, c0:c1] = dg
            du_ref[:, c0:c1] = du
            dh = dh + _dot(dg, wg[c0:c1, :], NN) + _dot(du, wu[c0:c1, :], NN)
        dx, dgain = _rms_bwd(dh, xv, r, gain_v)
        dx = dxo_v + dx
        dx_ref[...] = dx
        dxb_ref[...] = dx.astype(BF16)
        dgain_ref[...] += dgain

    return dict(
        body=body, grid=(t // tm,), name=name, args=[dxo, x, gate, up, gain, gath_g, gath_u, gath_d],
        out_shape=[jax.ShapeDtypeStruct((t, dm), F32), jax.ShapeDtypeStruct((t, dm), BF16),
                   jax.ShapeDtypeStruct((t, f), BF16), jax.ShapeDtypeStruct((t, f), BF16),
                   jax.ShapeDtypeStruct((t, dm), BF16), jax.ShapeDtypeStruct((1, dm), F32)],
        in_specs=[_tok(tm, dm), _tok(tm, dm), _tok(tm, f), _tok(tm, f), _full((1, dm)), ANY, ANY, ANY],
        out_specs=[_tok(tm, dm), _tok(tm, dm), _tok(tm, f), _tok(tm, f), _tok(tm, dm), _full((1, dm))],
        scratch=[pltpu.VMEM((f, dm), BF16), pltpu.VMEM((f, dm), BF16), pltpu.VMEM((f, dm), BF16),
                 pltpu.SemaphoreType.DMA((NDEV,))])


def _t5_buckets(rel):
    nb = N_BUCKETS // 2
    ret = jnp.where(rel > 0, nb, 0)
    n = jnp.abs(rel)
    max_exact = nb // 2
    nf = jnp.maximum(n, 1).astype(jnp.float32)
    large = max_exact + (jnp.log(nf / max_exact) / math.log(MAX_DISTANCE / max_exact)
                         * (nb - max_exact)).astype(jnp.int32)
    large = jnp.minimum(large, nb - 1)
    return ret + jnp.where(n < max_exact, n, large)


def _bucket_table():
    qi = jnp.arange(CHUNK, dtype=jnp.int32)[:, None]
    kj = jnp.arange(3 * CHUNK, dtype=jnp.int32)[None, :]
    rel = kj - CHUNK - qi
    return jnp.where(jnp.abs(rel) <= CHUNK, _t5_buckets(rel), -1)


def _bias_table(rel_bias_t, buckets):
    nh = rel_bias_t.shape[0]

    def body(rb_ref, bk_ref, o_ref):
        bk = bk_ref[...]
        for h in range(nh):
            acc = jnp.where(bk < 0, NEG, 0.0).astype(F32)
            for b in range(N_BUCKETS):
                acc = jnp.where(bk == b, rb_ref[h, b] * LOG2E, acc)
            o_ref[h] = acc

    return dict(
        body=body, grid=(1,), name="bias_table", args=[rel_bias_t, buckets],
        out_shape=[jax.ShapeDtypeStruct((nh,) + buckets.shape, F32)],
        in_specs=[pl.BlockSpec(memory_space=pltpu.SMEM), _full(buckets.shape)],
        out_specs=[_full((nh,) + buckets.shape)])


def _rel_bias_grad(dbias, buckets):
    nh = dbias.shape[0]

    def body(db_ref, bk_ref, o_ref):
        bk = bk_ref[...]
        lane = lax.broadcasted_iota(jnp.int32, (1, 128), 1)
        for h in range(nh):
            d = db_ref[h]
            row = jnp.zeros((1, 128), F32)
            for b in range(N_BUCKETS):
                s = jnp.sum(jnp.sum(jnp.where(bk == b, d, 0.0), axis=1, keepdims=True), axis=0, keepdims=True)
                row = jnp.where(lane == b, s, row)
            o_ref[h:h + 1, :] = row

    return pl.pallas_call(
        body, out_shape=jax.ShapeDtypeStruct((nh, 128), F32),
        in_specs=[pl.BlockSpec(memory_space=pltpu.VMEM), pl.BlockSpec(memory_space=pltpu.VMEM)],
        out_specs=pl.BlockSpec(memory_space=pltpu.VMEM), compiler_params=_cp(0), name="rel_bias_grad")(dbias, buckets)


def _half_masks():
    lane = lax.broadcasted_iota(jnp.int32, (CHUNK, 128), 1)
    return lane < HEAD_DIM, lane >= HEAD_DIM


def _kv_low(ref, starts, hk, lo):
    kt = (hk // 2) * 128
    out = []
    for jj in range(3):
        blk = ref[pl.ds(starts[jj], CHUNK), kt:kt + 128]
        if hk % 2 == 1:
            blk = pltpu.roll(blk, HEAD_DIM, 1)
        out.append(jnp.where(lo, blk, jnp.zeros_like(blk)))
    return out


def _stack_heads(tile_a, tile_b):
    return jnp.concatenate([tile_a, pltpu.roll(tile_a, HEAD_DIM, 1), tile_b, pltpu.roll(tile_b, HEAD_DIM, 1)], axis=0)


def _unstack_heads(o4):
    return (o4[0:CHUNK] + pltpu.roll(o4[CHUNK:2 * CHUNK], HEAD_DIM, 1),
            o4[2 * CHUNK:3 * CHUNK] + pltpu.roll(o4[3 * CHUNK:], HEAD_DIM, 1))


ATT_SLAB = 32


def _softmax_slab(s_scr, hk, g, r0, bias_ref, sink_ref, n, nblk):
    scale = HEAD_DIM ** -0.5 * LOG2E
    h = (N_HEADS // N_KV) * hk + g
    s = []
    for jj in range(3):
        sj = (s_scr[hk, jj, pl.ds(g * CHUNK + r0, ATT_SLAB), :] * scale
              + bias_ref[h, pl.ds(r0, ATT_SLAB), jj * CHUNK:(jj + 1) * CHUNK])
        if jj == 0:
            sj = jnp.where(n > 0, sj, NEG)
        if jj == 2:
            sj = jnp.where(n < nblk - 1, sj, NEG)
        s.append(sj)
    sink = sink_ref[h] * LOG2E
    m = jnp.maximum(jnp.max(jnp.maximum(jnp.maximum(s[0], s[1]), s[2]), axis=-1, keepdims=True), sink)
    e = [jnp.exp2(sj - m) for sj in s]
    es = jnp.exp2(sink - m)
    inv = 1.0 / (jnp.sum(e[0] + e[1] + e[2], axis=-1, keepdims=True) + es)
    return [ej * inv for ej in e], es * inv


def _key_block_starts(n, nblk):
    return [pl.multiple_of(jnp.clip(n - 1 + jj, 0, nblk - 1) * CHUNK, CHUNK) for jj in range(3)]


def _attn_fwd(qkv, x2, bias, sink, gath):
    t, dm = x2.shape
    nblk = t // CHUNK
    kvw = N_KV * HEAD_DIM
    kcb, vcb = dm // kvw, dm // kvw + 1
    slab = (N_KV, 3, 4 * CHUNK, CHUNK)

    def body(q_ref, k_ref, v_ref, x2_ref, bias_ref, sink_ref, gath_ref, x3_ref, att_ref, p_ref, ps_ref,
             wbuf, s_scr, sems):
        n = pl.program_id(0)

        @pl.when(n == 0)
        def _():
            _load_weight(gath_ref, wbuf, sems)
        lo, _ = _half_masks()
        lane_s = lax.broadcasted_iota(jnp.int32, (ATT_SLAB, 128), 1)
        starts = _key_block_starts(n, nblk)
        tiles = []
        for hk in range(N_KV):
            c0 = (2 * hk) * 128
            k_lo = _kv_low(k_ref, starts, hk, lo)
            v_lo = _kv_low(v_ref, starts, hk, lo)
            q4 = _stack_heads(q_ref[:, c0:c0 + 128], q_ref[:, c0 + 128:c0 + 256])
            for jj in range(3):
                s_scr[hk, jj] = _dot(q4, k_lo[jj], NT)
            for g in range(4):
                h = 4 * hk + g
                for r0 in range(0, CHUNK, ATT_SLAB):
                    p, ps = _softmax_slab(s_scr, hk, g, r0, bias_ref, sink_ref, n, nblk)
                    for jj in range(3):
                        p_ref[hk, jj, g * CHUNK + r0:g * CHUNK + r0 + ATT_SLAB, :] = p[jj].astype(BF16)
                    rest = jnp.zeros((ATT_SLAB, 128), F32) if h == 0 else ps_ref[r0:r0 + ATT_SLAB, :]
                    ps_ref[r0:r0 + ATT_SLAB, :] = jnp.where(lane_s == h, ps, rest)
            o4 = _dot(p_ref[hk, 0], v_lo[0], NN) + _dot(p_ref[hk, 1], v_lo[1], NN) + _dot(p_ref[hk, 2], v_lo[2], NN)
            tiles += list(_unstack_heads(o4))
        att = jnp.concatenate(tiles, axis=1).astype(BF16)
        att_ref[...] = att
        x3_ref[...] = x2_ref[...] + _dot(att, wbuf[...], NN)

    blk = pl.BlockSpec((CHUNK, dm), lambda n: (n, 0))
    return dict(
        body=body, grid=(nblk,), name="attn_fwd", args=[qkv, qkv, qkv, x2, bias, sink, gath],
        out_shape=[jax.ShapeDtypeStruct((t, dm), F32), jax.ShapeDtypeStruct((t, dm), BF16),
                   jax.ShapeDtypeStruct((nblk,) + slab, BF16), jax.ShapeDtypeStruct((t, 128), F32)],
        in_specs=[blk, pl.BlockSpec((t, kvw), lambda n: (0, kcb)), pl.BlockSpec((t, kvw), lambda n: (0, vcb)), blk,
                  _full(bias.shape), pl.BlockSpec(memory_space=pltpu.SMEM), ANY],
        out_specs=[blk, blk, pl.BlockSpec((None,) + slab, lambda n: (n, 0, 0, 0, 0)),
                   pl.BlockSpec((CHUNK, 128), lambda n: (n, 0))],
        scratch=[pltpu.VMEM((gath.shape[1] * NDEV, dm), BF16), pltpu.VMEM(slab, F32),
                 pltpu.SemaphoreType.DMA((NDEV,))])


def _attn_bwd(qkv, att, probs, sink_probs, dx3, bias_shape, gath):
    t, dm = dx3.shape
    nblk = t // CHUNK
    kvw = N_KV * HEAD_DIM
    kcb, vcb = dm // kvw, dm // kvw + 1
    scale = HEAD_DIM ** -0.5
    slab = (N_KV, 3, 4 * CHUNK, CHUNK)

    def body(q_ref, k_ref, v_ref, att_ref, p_ref, ps_ref, dx_ref, gath_ref,
             dq_ref, dk_ref, dv_ref, dbias_ref, dsink_ref, wbuf, dp_scr, ds_scr, prod_scr, dsum_scr, sems):
        n = pl.program_id(0)

        @pl.when(n == 0)
        def _():
            _load_weight(gath_ref, wbuf, sems)
            dk_ref[...] = jnp.zeros_like(dk_ref)
            dv_ref[...] = jnp.zeros_like(dv_ref)
            dbias_ref[...] = jnp.zeros_like(dbias_ref)
            dsink_ref[...] = jnp.zeros_like(dsink_ref)
        lo, hi = _half_masks()
        lane_s = lax.broadcasted_iota(jnp.int32, (ATT_SLAB, 128), 1)
        starts = _key_block_starts(n, nblk)
        dout = _dot(dx_ref[...].astype(BF16), wbuf[...], NT)
        prod_scr[...] = dout * att_ref[...].astype(F32)
        doutb = dout.astype(BF16)
        dq_tiles = []
        for hk in range(N_KV):
            kt = (hk // 2) * 128
            c0 = (2 * hk) * 128
            k_lo = _kv_low(k_ref, starts, hk, lo)
            v_lo = _kv_low(v_ref, starts, hk, lo)
            q4 = _stack_heads(q_ref[:, c0:c0 + 128], q_ref[:, c0 + 128:c0 + 256])
            do4 = _stack_heads(doutb[:, c0:c0 + 128], doutb[:, c0 + 128:c0 + 256])
            for jj in range(3):
                dp_scr[hk, jj] = _dot(do4, v_lo[jj], NT)
            for g in range(4):
                h = 4 * hk + g
                for r0 in range(0, CHUNK, ATT_SLAB):
                    rows = slice(g * CHUNK + r0, g * CHUNK + r0 + ATT_SLAB)
                    pt = prod_scr[r0:r0 + ATT_SLAB, c0 + (g // 2) * 128:c0 + (g // 2 + 1) * 128]
                    msk = lane_s < HEAD_DIM if g % 2 == 0 else lane_s >= HEAD_DIM
                    dsum = jnp.sum(jnp.where(msk, pt, 0.0), axis=-1, keepdims=True)
                    rest = jnp.zeros((ATT_SLAB, 128), F32) if h == 0 else dsum_scr[r0:r0 + ATT_SLAB, :]
                    dsum_scr[r0:r0 + ATT_SLAB, :] = jnp.where(lane_s == h, dsum, rest)
                    for jj in range(3):
                        ds = p_ref[hk, jj, rows, :].astype(F32) * (dp_scr[hk, jj, rows, :] - dsum)
                        dbias_ref[h, r0:r0 + ATT_SLAB, jj * CHUNK:(jj + 1) * CHUNK] += ds
                        ds_scr[hk, jj, rows, :] = ds.astype(BF16)
            dq4 = jnp.zeros((4 * CHUNK, 128), F32)
            for jj in range(3):
                ds4 = ds_scr[hk, jj]
                dq4 = dq4 + _dot(ds4, k_lo[jj], NN) * scale
                dkj = _dot(ds4, q4, TN) * scale
                dvj = _dot(p_ref[hk, jj], do4, TN)
                if hk % 2 == 1:
                    dkj, dvj = pltpu.roll(dkj, HEAD_DIM, 1), pltpu.roll(dvj, HEAD_DIM, 1)
                keep = lo if hk % 2 == 0 else hi
                dk_ref[pl.ds(starts[jj], CHUNK), kt:kt + 128] += jnp.where(keep, dkj, 0.0)
                dv_ref[pl.ds(starts[jj], CHUNK), kt:kt + 128] += jnp.where(keep, dvj, 0.0)
            dq_tiles += list(_unstack_heads(dq4))
        dq_ref[...] = jnp.concatenate(dq_tiles, axis=1).astype(BF16)
        dsink_ref[...] -= jnp.sum(ps_ref[...] * dsum_scr[...], axis=0, keepdims=True)

    blk = pl.BlockSpec((CHUNK, dm), lambda n: (n, 0))
    return dict(
        body=body, grid=(nblk,), name="attn_bwd", args=[qkv, qkv, qkv, att, probs, sink_probs, dx3, gath],
        out_shape=[jax.ShapeDtypeStruct((t, dm), BF16), jax.ShapeDtypeStruct((t, kvw), F32),
                   jax.ShapeDtypeStruct((t, kvw), F32), jax.ShapeDtypeStruct(bias_shape, F32),
                   jax.ShapeDtypeStruct((1, 128), F32)],
        in_specs=[blk, pl.BlockSpec((t, kvw), lambda n: (0, kcb)), pl.BlockSpec((t, kvw), lambda n: (0, vcb)),
                  blk, pl.BlockSpec((None,) + slab, lambda n: (n, 0, 0, 0, 0)),
                  pl.BlockSpec((CHUNK, 128), lambda n: (n, 0)), blk, ANY],
        out_specs=[blk, _full((t, kvw)), _full((t, kvw)), _full(bias_shape), _full((1, 128))],
        scratch=[pltpu.VMEM((gath.shape[1] * NDEV, dm), BF16), pltpu.VMEM(slab, F32), pltpu.VMEM(slab, BF16),
                 pltpu.VMEM((CHUNK, dm), F32), pltpu.VMEM((CHUNK, 128), F32), pltpu.SemaphoreType.DMA((NDEV,))])


def _finish_weight(recvs, w, m, v, name):
    nl, r, dm = w.shape
    assert nl == len(recvs) and all(rc.shape[1:] == (r, dm) for rc in recvs)
    td = dm // 2
    wspec = pl.BlockSpec((None, r, td), lambda l, j: (l, 0, j))

    def body(*refs):
        r_refs = refs[:nl]
        w_ref, m_ref, v_ref, g_ref, d_ref, nm_ref, nv_ref = refs[nl:]
        layer = pl.program_id(0)
        for li in range(nl):
            @pl.when(layer == li)
            def _():
                g = r_refs[li][0].astype(F32)
                for d in range(1, recvs[li].shape[0]):
                    g = g + r_refs[li][d].astype(F32)
                delta, nm, nv = _adamw_math(w_ref[...], g, m_ref[...], v_ref[...])
                g_ref[...] = g
                d_ref[...] = delta
                nm_ref[...] = nm
                nv_ref[...] = nv

    o = jax.ShapeDtypeStruct(w.shape, F32)
    return dict(
        body=body, grid=(nl, 2), name=name, args=[*recvs, w, m, v], out_shape=[o, o, o, o],
        in_specs=[pl.BlockSpec((rc.shape[0], r, td), lambda l, j: (0, 0, j)) for rc in recvs] + [wspec] * 3,
        out_specs=[wspec] * 4)


def _adamw_small(w, pieces, m, v, name):
    r, c = w.shape
    npiece = len(pieces)

    def body(*refs):
        w_ref = refs[0]
        piece_refs = refs[1:1 + npiece]
        m_ref, v_ref, gs_ref, d_ref, nm_ref, nv_ref = refs[1 + npiece:]
        for (row0, _, src0, rows), p_ref in zip(pieces, piece_refs):
            part = p_ref[0, src0:src0 + rows, :].astype(F32)
            for d in range(1, NDEV):
                part = part + p_ref[d, src0:src0 + rows, :].astype(F32)
            gs_ref[row0:row0 + rows, :] = part
        d_ref[...], nm_ref[...], nv_ref[...] = _adamw_math(w_ref[...], gs_ref[...], m_ref[...], v_ref[...])

    spec = pl.BlockSpec((r, c), lambda i: (0, 0))
    out = jax.ShapeDtypeStruct((r, c), F32)
    return pl.pallas_call(
        body, grid=(1,), out_shape=(out,) * 4,
        in_specs=[spec] + [pl.BlockSpec(p[1].shape, lambda i: (0, 0, 0)) for p in pieces] + [spec, spec],
        out_specs=(spec,) * 4, compiler_params=_cp(), name=name)(w, *[p[1] for p in pieces], m, v)


def _pack_small(parts, rows):
    flat = jnp.concatenate([p.reshape(-1) for p in parts])
    return jnp.pad(flat, (0, rows * 128 - flat.shape[0])).reshape(rows, 128)


def _unpack_small(packed, shapes):
    flat = packed.reshape(-1)
    out, o = [], 0
    for s in shapes:
        n = int(np.prod(s))
        out.append(flat[o:o + n].reshape(s))
        o += n
    return out


def kernel(x, norm_mix, norm_ffn, even_w_in, even_v_ln_g, even_v_ln_b, even_w_spatial, even_b_spatial, even_conv_w, even_w_out, attn_w_qkv, attn_sink, rel_bias, attn_w_out, ffn_w_gate, ffn_w_up, ffn_w_down, final_norm, loss_target, m_norm_mix, m_norm_ffn, m_even_w_in, m_even_v_ln_g, m_even_v_ln_b, m_even_w_spatial, m_even_b_spatial, m_even_conv_w, m_even_w_out, m_attn_w_qkv, m_attn_sink, m_rel_bias, m_attn_w_out, m_ffn_w_gate, m_ffn_w_up, m_ffn_w_down, m_final_norm, v_norm_mix, v_norm_ffn, v_even_w_in, v_even_v_ln_g, v_even_v_ln_b, v_even_w_spatial, v_even_b_spatial, v_even_conv_w, v_even_w_out, v_attn_w_qkv, v_attn_sink, v_rel_bias, v_attn_w_out, v_ffn_w_gate, v_ffn_w_up, v_ffn_w_down, v_final_norm):
    t, dm = x.shape[1], x.shape[2]
    aw = even_v_ln_g.shape[1]
    bw = even_conv_w.shape[2] * NDEV
    gd = aw // A_GROUPS
    tm = min(512, t // 2)
    tmf = min(256, t // 2)
    me = _my_index()
    row = lambda a: a.reshape(1, -1)

    colT = lambda w: w.T.astype(BF16)
    sh = dict(winT=colT(even_w_in[0]), wqkvT=colT(attn_w_qkv[0]), wgT0=colT(ffn_w_gate[0]), wuT0=colT(ffn_w_up[0]),
              wgT1=colT(ffn_w_gate[1]), wuT1=colT(ffn_w_up[1]), woe=even_w_out[0].astype(BF16),
              woa=attn_w_out[0].astype(BF16), wd0=ffn_w_down[0].astype(BF16), wd1=ffn_w_down[1].astype(BF16))
    gather = lambda names: _GatherCarry([sh[n] for n in names])

    in_full = lambda a: lax.dynamic_update_slice(jnp.zeros((3, bw), F32), a[0], (0, me * (bw // NDEV)))
    cw_rows = 3 * bw // 128
    cw_mine = jnp.pad(in_full(even_conv_w).reshape(cw_rows, 128), ((0, 16 - cw_rows), (0, 0)))

    x0 = x[0]
    wsp_b = even_w_spatial[0].astype(BF16)
    bspb = jnp.broadcast_to(even_b_spatial[0][:, :, None], (A_GROUPS, CHUNK, gd))
    buckets = _bucket_table()
    sink = attn_sink[0]

    (bias,), ((g_winT,), (cw_slots,)) = _call(
        _bias_table(rel_bias.T, buckets), [gather(["winT"]), _BroadcastCarry(cw_mine)])
    cw_full = jnp.sum(cw_slots, axis=0)[0:cw_rows].reshape(3, bw)
    (proj, h0b), (g_woe, g_wgT0) = _call(_norm_proj(x0, row(norm_mix[0]), g_winT, F32, "in_proj", tm),
                                         gather(["woe", "wgT0"]))
    (x1, yb), (g_wuT0,) = _call(_even_core_fwd(proj, x0, even_v_ln_g, even_v_ln_b, wsp_b, bspb, cw_full, g_woe, tm),
                                gather(["wuT0"]))
    (gate0, up0, act0), (g_wd0,) = _call(_ffn_up(x1, row(norm_ffn[0]), g_wgT0, g_wuT0, "ffn_up0", tmf), gather(["wd0"]))
    (x2,), (g_wqkvT,) = _call(_ffn_down(x1, act0, g_wd0, "ffn_down0", tm), gather(["wqkvT"]))
    (qkv, h2b), (g_woa,) = _call(_norm_proj(x2, row(norm_mix[1]), g_wqkvT, BF16, "qkv_proj", tm), gather(["woa"]))
    (x3, attb, probs, sink_probs), (g_wgT1, g_wuT1) = _call(
        _attn_fwd(qkv, x2, bias, sink, g_woa), gather(["wgT1", "wuT1"]))
    (gate1, up1, act1), (g_wd1,) = _call(_ffn_up(x3, row(norm_ffn[1]), g_wgT1, g_wuT1, "ffn_up1", tmf), gather(["wd1"]))
    (loss_part, dx4, dx4b, d_final), _ = _call(
        _ffn_down_loss(x3, act1, g_wd1, loss_target[0], row(final_norm), "ffn_down1_loss", tm))

    (dx3, dx3b, dg1, du1, h3b, d_nffn1), _ = _call(
        _ffn_bwd(dx4, x3, gate1, up1, row(norm_ffn[1]), g_wgT1, g_wuT1, g_wd1, "ffn_bwd1", tmf))
    (p_wgT1,), _ = _call(_wgrad(dg1, h3b, "wgrad_gate1"))
    (p_wuT1,), (a_wgT1,) = _call(_wgrad(du1, h3b, "wgrad_up1"), _PairCarry(p_wgT1))
    (p_wd1,), (a_wuT1,) = _call(_wgrad(act1, dx4b, "wgrad_down1"), _PairCarry(p_wuT1))
    (dq, dk, dv, dbias, dsink), ((r_wgT1,), (a_wd1,)) = _call(
        _attn_bwd(qkv, attb, probs, sink_probs, dx3, bias.shape, g_woa),
        [_ChipSumCarry(p_wgT1, a_wgT1), _PairCarry(p_wd1)])
    (p_woa,), _ = _call(_wgrad(attb, dx3b, "wgrad_attn_out"))
    d_relb = _rel_bias_grad(dbias, buckets)[:, 0:N_BUCKETS].T
    dqkv = jnp.concatenate([dq, dk.astype(BF16), dv.astype(BF16)], axis=1)
    (dx2, dx2b, d_nmix1), (r_wuT1,) = _call(
        _proj_bwd_norm(dqkv, x2, row(norm_mix[1]), dx3, g_wqkvT, "qkv_bwd", tm), _ChipSumCarry(p_wuT1, a_wuT1))
    (p_wqkvT,), _ = _call(_wgrad(dqkv, h2b, "wgrad_qkv"))
    (dx1, dx1b, dg0, du0, h1b, d_nffn0), ((r_wd1,), (r_woa, r_wqkvT)) = _call(
        _ffn_bwd(dx2, x1, gate0, up0, row(norm_ffn[0]), g_wgT0, g_wuT0, g_wd0, "ffn_bwd0", tmf),
        [_ChipSumCarry(p_wd1, a_wd1), _GradCarry([p_woa, p_wqkvT])])
    (p_wgT0,), _ = _call(_wgrad(dg0, h1b, "wgrad_gate0"))
    (p_wuT0,), (a_wgT0,) = _call(_wgrad(du0, h1b, "wgrad_up0"), _PairCarry(p_wgT0))
    (p_wd0,), ((r_wgT0,), (a_wuT0,)) = _call(
        _wgrad(act0, dx2b, "wgrad_down0"), [_ChipSumCarry(p_wgT0, a_wgT0), _PairCarry(p_wuT0)])
    (dproj, d_lng, d_lnb, d_wsp, d_bsp3, d_cw), ((r_wuT0,), (a_wd0,)) = _call(
        _even_core_bwd(proj, dx1, even_v_ln_g, even_v_ln_b, wsp_b, bspb, cw_full, g_woe, tm),
        [_ChipSumCarry(p_wuT0, a_wuT0), _PairCarry(p_wd0)])
    small_shapes = [(2, dm), (2, dm), (1, aw), (1, aw), (1, A_GROUPS, CHUNK, CHUNK), (1, A_GROUPS, CHUNK), (3, bw),
                    (1, N_HEADS), (N_BUCKETS, N_HEADS), (dm,), (1, 1)]
    small_names = ["norm_mix", "norm_ffn", "even_v_ln_g", "even_v_ln_b", "even_w_spatial", "even_b_spatial",
                   "even_conv_w", "attn_sink", "rel_bias", "final_norm", "loss"]
    n_small = sum(int(np.prod(s)) for s in small_shapes)
    small_rows = 8 * ((n_small + 1023) // 1024)
    assert dm == 8 * 128
    small_part = _pack_small(
        [jnp.concatenate([jnp.zeros_like(d_nmix1), d_nmix1]), jnp.concatenate([d_nffn0, d_nffn1]), d_lng, d_lnb,
         d_wsp, jnp.sum(d_bsp3, axis=-1), d_cw, dsink[:, 0:N_HEADS], d_relb, d_final, loss_part], small_rows)
    wsp_at = small_names.index("even_w_spatial")
    wsp_row0 = sum(int(np.prod(s)) for s in small_shapes[:wsp_at]) // 128
    wsp_rows = int(np.prod(small_shapes[wsp_at])) // 128
    assert wsp_row0 % 8 == 0 and wsp_rows % 16 == 0
    small_rest = jnp.concatenate([small_part[0:wsp_row0], small_part[wsp_row0 + wsp_rows:]])
    small_wsp = small_part[wsp_row0:wsp_row0 + wsp_rows].astype(BF16)
    (p_winT,), (r_wd0,) = _call(_wgrad(dproj, h0b, "wgrad_in"), _ChipSumCarry(p_wd0, a_wd0))
    (p_woe,), ((a_winT,), (rest_slots,), (wsp_slots,)) = _call(
        _wgrad(yb, dx1b, "wgrad_even_out"),
        [_PairCarry(p_winT), _BroadcastCarry(small_rest), _BroadcastCarry(small_wsp)])
    (dx0, _, d_nmix0), ((r_winT,), (r_woe,)) = _call(
        _proj_bwd_norm(dproj, x0, row(norm_mix[0]), dx1, g_winT, "in_proj_bwd", tm),
        [_ChipSumCarry(p_winT, a_winT), _GradCarry([p_woe])])

    grads = {}

    order = ["norm_mix", "norm_ffn", "even_w_in", "even_v_ln_g", "even_v_ln_b", "even_w_spatial", "even_b_spatial",
             "even_conv_w", "even_w_out", "attn_w_qkv", "attn_sink", "rel_bias", "attn_w_out", "ffn_w_gate",
             "ffn_w_up", "ffn_w_down", "final_norm"]
    ws = dict(norm_mix=norm_mix, norm_ffn=norm_ffn, even_w_in=even_w_in, even_v_ln_g=even_v_ln_g,
              even_v_ln_b=even_v_ln_b, even_w_spatial=even_w_spatial, even_b_spatial=even_b_spatial,
              even_conv_w=even_conv_w, even_w_out=even_w_out, attn_w_qkv=attn_w_qkv, attn_sink=attn_sink,
              rel_bias=rel_bias, attn_w_out=attn_w_out, ffn_w_gate=ffn_w_gate, ffn_w_up=ffn_w_up,
              ffn_w_down=ffn_w_down, final_norm=final_norm)
    ms = dict(norm_mix=m_norm_mix, norm_ffn=m_norm_ffn, even_w_in=m_even_w_in, even_v_ln_g=m_even_v_ln_g,
              even_v_ln_b=m_even_v_ln_b, even_w_spatial=m_even_w_spatial, even_b_spatial=m_even_b_spatial,
              even_conv_w=m_even_conv_w, even_w_out=m_even_w_out, attn_w_qkv=m_attn_w_qkv, attn_sink=m_attn_sink,
              rel_bias=m_rel_bias, attn_w_out=m_attn_w_out, ffn_w_gate=m_ffn_w_gate, ffn_w_up=m_ffn_w_up,
              ffn_w_down=m_ffn_w_down, final_norm=m_final_norm)
    vs = dict(norm_mix=v_norm_mix, norm_ffn=v_norm_ffn, even_w_in=v_even_w_in, even_v_ln_g=v_even_v_ln_g,
              even_v_ln_b=v_even_v_ln_b, even_w_spatial=v_even_w_spatial, even_b_spatial=v_even_b_spatial,
              even_conv_w=v_even_conv_w, even_w_out=v_even_w_out, attn_w_qkv=v_attn_w_qkv, attn_sink=v_attn_sink,
              rel_bias=v_rel_bias, attn_w_out=v_attn_w_out, ffn_w_gate=v_ffn_w_gate, ffn_w_up=v_ffn_w_up,
              ffn_w_down=v_ffn_w_down, final_norm=v_final_norm)
    big = dict(ffn_w_gate=([r_wgT0, r_wgT1], True), even_w_in=([r_winT], True), even_w_out=([r_woe], False),
               attn_w_qkv=([r_wqkvT], True), attn_w_out=([r_woa], False), ffn_w_up=([r_wuT0, r_wuT1], True),
               ffn_w_down=([r_wd0, r_wd1], False))
    delta, new_m, new_v = {}, {}, {}
    late_slots = None
    for n, (recvs, transposed) in big.items():
        lay = (lambda a: jnp.swapaxes(a, 1, 2)) if transposed else (lambda a: a)
        spec = _finish_weight(recvs, lay(ws[n]), lay(ms[n]), lay(vs[n]), "finish_" + n)
        if late_slots is None:
            outs, (late_slots,) = _call(spec, _BroadcastCarry(d_nmix0.reshape(8, 128)))
        else:
            outs, _ = _call(spec)
        grads[n], delta[n], new_m[n], new_v[n] = [lay(o) for o in outs]
    pk = lambda dct: _pack_small(
        [in_full(dct[n]) if n == "even_conv_w" else (jnp.zeros((1, 1), F32) if n == "loss" else dct[n])
         for n in small_names], small_rows)
    tail_rows = small_rows - wsp_row0 - wsp_rows
    pieces = [(0, rest_slots, 0, wsp_row0), (wsp_row0 + wsp_rows, rest_slots, wsp_row0, tail_rows),
              (wsp_row0, wsp_slots, 0, wsp_rows), (0, late_slots, 0, 8)]
    packed = _adamw_small(pk(ws), pieces, pk(ms), pk(vs), "adamw_small")
    mine = lambda a: lax.dynamic_slice(a, (0, me * (bw // NDEV)), (3, bw // NDEV))[None]
    for dst, arr in zip((grads, delta, new_m, new_v), packed):
        for n, a in zip(small_names, _unpack_small(arr, small_shapes)):
            dst[n] = mine(a) if n == "even_conv_w" else a
    loss = grads["loss"][0, 0]
    return (loss, dx0[None], *[grads[n] for n in order], *[delta[n] for n in order],
            *[new_m[n] for n in order], *[new_v[n] for n in order])
```

```python
import math

import jax
import jax.numpy as jnp
import numpy as np
from jax import lax
from jax.experimental import pallas as pl
from jax.experimental.pallas import tpu as pltpu

F32, BF16 = jnp.float32, jnp.bfloat16
NDEV = 8
EPS = 1e-6
CHUNK = 128
A_GROUPS = 4
N_HEADS, N_KV, HEAD_DIM = 16, 4, 64
N_BUCKETS, MAX_DISTANCE = 32, 128
NEG = -1e30
LOG2E = 1.4426950408889634
ADAM_LR, ADAM_B1, ADAM_B2, ADAM_EPS, ADAM_WD, ADAM_STEP = 0.001, 0.9, 0.999, 1e-08, 0.01, 10
VMEM_LIMIT = 56 * 1024 * 1024
MESH = pl.DeviceIdType.MESH
NT = (((1,), (1,)), ((), ()))
NN = (((1,), (0,)), ((), ()))
TN = (((0,), (0,)), ((), ()))
ANY = pl.BlockSpec(memory_space=pl.ANY)


def _cp(n_grid=1):
    return pltpu.CompilerParams(dimension_semantics=("arbitrary",) * n_grid, vmem_limit_bytes=VMEM_LIMIT)


def _dot(a, b, dims):
    return lax.dot_general(a, b, dims, preferred_element_type=F32)


def _my_index():
    return 4 * lax.axis_index("x") + 2 * lax.axis_index("y") + lax.axis_index("c")


def _peer(k):
    x, y, c = lax.axis_index("x"), lax.axis_index("y"), lax.axis_index("c")
    px = 1 - x if k & 4 else x
    py = 1 - y if k & 2 else y
    pc = 1 - c if k & 1 else c
    return (px, py, pc)


def _load_weight(gath_ref, wbuf, sems):
    rows = gath_ref.shape[1]
    cps = [pltpu.make_async_copy(gath_ref.at[d], wbuf.at[pl.ds(d * rows, rows), :], sems.at[d]) for d in range(NDEV)]
    for c in cps:
        c.start()
    for c in cps:
        c.wait()


class _GatherCarry:
    def __init__(self, pieces):
        self.inputs = list(pieces)
        self.n = len(pieces)
        self.out_shape = [jax.ShapeDtypeStruct((NDEV,) + p.shape, p.dtype) for p in pieces]
        self.scratch = [pltpu.SemaphoreType.DMA((7 * self.n,)), pltpu.SemaphoreType.DMA((7 * self.n,)),
                        pltpu.SemaphoreType.DMA((self.n,))]

    def _ctx(self):
        x, y, c = lax.axis_index("x"), lax.axis_index("y"), lax.axis_index("c")
        chips = [(1 - x, y), (x, 1 - y), (1 - x, 1 - y)]
        return (x, y, c), (x, y, 1 - c), chips, c

    def _copy(self, k, j, block, to, ins, outs, sems, src=None):
        send_sems, recv_sems, _ = sems
        slot = outs[j].at[4 * block[0] + 2 * block[1] + block[2]]
        return pltpu.make_async_remote_copy(
            src_ref=slot if src is None else src, dst_ref=slot, send_sem=send_sems.at[k * self.n + j],
            recv_sem=recv_sems.at[k * self.n + j], device_id=to, device_id_type=MESH)

    def start(self, ins, outs, sems):
        me, sibling, chips, c = self._ctx()
        for j in range(self.n):
            pltpu.make_async_copy(ins[j], outs[j].at[4 * me[0] + 2 * me[1] + me[2]], sems[2].at[j]).start()
            self._copy(0, j, me, sibling, ins, outs, sems, src=ins[j]).start()
            for q, chip in enumerate(chips):
                self._copy(1 + q, j, me, (*chip, c), ins, outs, sems, src=ins[j]).start()

    def mid(self, ins, outs, sems):
        me, sibling, chips, c = self._ctx()
        for q, chip in enumerate(chips):
            for j in range(self.n):
                self._copy(1 + q, j, (*chip, c), me, ins, outs, sems).wait_recv()
                self._copy(4 + q, j, (*chip, c), sibling, ins, outs, sems).start()

    def finish(self, ins, outs, sems):
        me, sibling, chips, c = self._ctx()
        for j in range(self.n):
            self._copy(0, j, sibling, me, ins, outs, sems).wait_recv()
            for q, chip in enumerate(chips):
                self._copy(4 + q, j, (*chip, 1 - c), me, ins, outs, sems).wait_recv()
        for j in range(self.n):
            self._copy(0, j, me, sibling, ins, outs, sems, src=ins[j]).wait_send()
            for q, chip in enumerate(chips):
                self._copy(1 + q, j, me, (*chip, c), ins, outs, sems, src=ins[j]).wait_send()
                self._copy(4 + q, j, (*chip, c), sibling, ins, outs, sems).wait_send()
            pltpu.make_async_copy(ins[j], outs[j].at[0], sems[2].at[j]).wait()


class _GradCarry:
    def __init__(self, pieces):
        self.inputs = list(pieces)
        self.n = len(pieces)
        self.rows = [p.shape[0] // NDEV for p in pieces]
        self.out_shape = [jax.ShapeDtypeStruct((NDEV, r, p.shape[1]), p.dtype) for p, r in zip(pieces, self.rows)]
        self.scratch = [pltpu.SemaphoreType.DMA((7 * self.n,)), pltpu.SemaphoreType.DMA((7 * self.n,)),
                        pltpu.SemaphoreType.DMA((self.n,))]

    def _copies(self, ins, outs, sems):
        me = _my_index()
        local, remote = [], []
        for j in range(self.n):
            r = self.rows[j]
            local.append(pltpu.make_async_copy(ins[j].at[pl.ds(pl.multiple_of(me * r, 16), r), :], outs[j].at[me],
                                               sems[2].at[j]))
            for k in range(1, NDEV):
                peer = _peer(k)
                pidx = 4 * peer[0] + 2 * peer[1] + peer[2]
                remote.append(pltpu.make_async_remote_copy(
                    src_ref=ins[j].at[pl.ds(pl.multiple_of(pidx * r, 16), r), :], dst_ref=outs[j].at[me],
                    send_sem=sems[0].at[(k - 1) * self.n + j], recv_sem=sems[1].at[(k - 1) * self.n + j],
                    device_id=peer, device_id_type=MESH))
        return local, remote

    def start(self, ins, outs, sems):
        local, remote = self._copies(ins, outs, sems)
        for cp in local + remote:
            cp.start()

    def mid(self, ins, outs, sems):
        pass

    def finish(self, ins, outs, sems):
        local, remote = self._copies(ins, outs, sems)
        for cp in remote + local:
            cp.wait()


class _BroadcastCarry:
    def __init__(self, parts):
        self.inputs = list(parts)
        self.n = len(self.inputs)
        self.out_shape = [jax.ShapeDtypeStruct((NDEV,) + p.shape, p.dtype) for p in self.inputs]
        self.scratch = [pltpu.SemaphoreType.DMA((7 * self.n,)), pltpu.SemaphoreType.DMA((7 * self.n,)),
                        pltpu.SemaphoreType.DMA((self.n,))]

    def _copies(self, ins, outs, sems):
        me = _my_index()
        cps = []
        for j in range(self.n):
            cps.append(pltpu.make_async_copy(ins[j], outs[j].at[me], sems[2].at[j]))
            cps += [pltpu.make_async_remote_copy(
                src_ref=ins[j], dst_ref=outs[j].at[me], send_sem=sems[0].at[(k - 1) * self.n + j],
                recv_sem=sems[1].at[(k - 1) * self.n + j], device_id=_peer(k), device_id_type=MESH)
                for k in range(1, NDEV)]
        return cps

    def start(self, ins, outs, sems):
        for cp in self._copies(ins, outs, sems):
            cp.start()

    def mid(self, ins, outs, sems):
        pass

    def finish(self, ins, outs, sems):
        for cp in self._copies(ins, outs, sems):
            cp.wait()


class _PairCarry:
    def __init__(self, piece):
        self.inputs = [piece]
        self.r = piece.shape[0] // NDEV
        self.out_shape = [jax.ShapeDtypeStruct((4, self.r, piece.shape[1]), piece.dtype)]
        self.scratch = [pltpu.SemaphoreType.DMA((4,)), pltpu.SemaphoreType.DMA((4,))]

    def _copies(self, ins, outs, sems):
        x, y, c = lax.axis_index("x"), lax.axis_index("y"), lax.axis_index("c")
        return [pltpu.make_async_remote_copy(
            src_ref=ins[0].at[pl.ds(pl.multiple_of((2 * q + 1 - c) * self.r, 16), self.r), :], dst_ref=outs[0].at[q],
            send_sem=sems[0].at[q], recv_sem=sems[1].at[q], device_id=(x, y, 1 - c), device_id_type=MESH)
            for q in range(4)]

    def start(self, ins, outs, sems):
        for cp in self._copies(ins, outs, sems):
            cp.start()

    def mid(self, ins, outs, sems):
        pass

    def finish(self, ins, outs, sems):
        for cp in self._copies(ins, outs, sems):
            cp.wait()


class _ChipSumCarry:
    def __init__(self, piece, landed):
        self.inputs = [piece, landed]
        self.r, dm = piece.shape[0] // NDEV, piece.shape[1]
        self.out_shape = [jax.ShapeDtypeStruct((4, self.r, dm), piece.dtype)]
        self.scratch = [pltpu.VMEM((4, self.r, dm), piece.dtype), pltpu.VMEM((8, self.r, dm), piece.dtype),
                        pltpu.SemaphoreType.DMA((8,)), pltpu.SemaphoreType.DMA((3,)), pltpu.SemaphoreType.DMA((3,)),
                        pltpu.SemaphoreType.DMA(())]

    def _copies(self, outs, scr):
        sums, _, _, send_sems, recv_sems, local_sem = scr
        x, y, c = lax.axis_index("x"), lax.axis_index("y"), lax.axis_index("c")
        mine = 2 * x + y
        local = pltpu.make_async_copy(sums.at[mine], outs[0].at[mine], local_sem)
        remote = []
        for k in range(1, 4):
            px = 1 - x if k & 2 else x
            py = 1 - y if k & 1 else y
            remote.append(pltpu.make_async_remote_copy(
                src_ref=sums.at[2 * px + py], dst_ref=outs[0].at[mine], send_sem=send_sems.at[k - 1],
                recv_sem=recv_sems.at[k - 1], device_id=(px, py, c), device_id_type=MESH))
        return local, remote

    def start(self, ins, outs, scr):
        sums, stage, stage_sems = scr[0], scr[1], scr[2]
        c = lax.axis_index("c")
        loads = []
        for q in range(4):
            loads.append((
                pltpu.make_async_copy(ins[0].at[pl.ds(pl.multiple_of((2 * q + c) * self.r, 16), self.r), :],
                                      stage.at[2 * q], stage_sems.at[2 * q]),
                pltpu.make_async_copy(ins[1].at[q], stage.at[2 * q + 1], stage_sems.at[2 * q + 1])))
        for a, b in loads:
            a.start()
            b.start()
        for q, (a, b) in enumerate(loads):
            a.wait()
            b.wait()
            sums[q] = (stage[2 * q].astype(F32) + stage[2 * q + 1].astype(F32)).astype(sums.dtype)
        local, remote = self._copies(outs, scr)
        for cp in [local] + remote:
            cp.start()

    def mid(self, ins, outs, scr):
        pass

    def finish(self, ins, outs, scr):
        local, remote = self._copies(outs, scr)
        for cp in remote + [local]:
            cp.wait()


def _call(spec, carry=None):
    body, grid = spec["body"], spec["grid"]
    in_specs, out_specs, out_shape = list(spec["in_specs"]), list(spec["out_specs"]), list(spec["out_shape"])
    scratch, args = list(spec.get("scratch", [])), list(spec["args"])
    if carry is None:
        out = pl.pallas_call(body, grid=grid, in_specs=in_specs, out_specs=tuple(out_specs),
                             out_shape=tuple(out_shape), scratch_shapes=scratch, compiler_params=_cp(len(grid)),
                             name=spec["name"])(*args)
        return tuple(out), ()
    carries = list(carry) if isinstance(carry, (list, tuple)) else [carry]
    n_in, n_out, n_s = len(in_specs), len(out_specs), len(scratch)
    steps = int(np.prod(grid))

    def split(refs, counts):
        parts, o = [], 0
        for cnt in counts:
            parts.append(refs[o:o + cnt])
            o += cnt
        return parts

    c_in = [len(cr.inputs) for cr in carries]
    c_out = [len(cr.out_shape) for cr in carries]
    c_scr = [len(cr.scratch) for cr in carries]

    def wrapped(*refs):
        ins, cins, outs, couts, scr, cscr = split(refs, [n_in, sum(c_in), n_out, sum(c_out), n_s, sum(c_scr)])
        per = list(zip(carries, split(cins, c_in), split(couts, c_out), split(cscr, c_scr)))
        step = pl.program_id(0)
        for ax in range(1, len(grid)):
            step = step * grid[ax] + pl.program_id(ax)

        @pl.when(step == 0)
        def _():
            for cr, ci, co, cs in per:
                cr.start(ci, co, cs)
        if steps >= 3:
            @pl.when(step == steps - 2)
            def _():
                for cr, ci, co, cs in per:
                    cr.mid(ci, co, cs)
        body(*ins, *outs, *scr)

        @pl.when(step == steps - 1)
        def _():
            for cr, ci, co, cs in per:
                if steps < 3:
                    cr.mid(ci, co, cs)
                cr.finish(ci, co, cs)

    out = pl.pallas_call(
        wrapped, grid=grid, in_specs=in_specs + [ANY] * sum(c_in), out_specs=tuple(out_specs + [ANY] * sum(c_out)),
        out_shape=tuple(out_shape + [s for cr in carries for s in cr.out_shape]),
        scratch_shapes=scratch + [s for cr in carries for s in cr.scratch],
        compiler_params=_cp(len(grid)), name=spec["name"])(*args, *[a for cr in carries for a in cr.inputs])
    c_res = [tuple(p) for p in split(out[n_out:], c_out)]
    return tuple(out[:n_out]), (c_res if isinstance(carry, (list, tuple)) else c_res[0])


def _rms_fwd(x, gain):
    r = lax.rsqrt(jnp.mean(x * x, axis=-1, keepdims=True) + EPS)
    return x * r * gain, r


def _rms_bwd(dh, x, r, gain):
    a = dh * gain
    dx = r * a - x * (r * r * r) * jnp.mean(a * x, axis=-1, keepdims=True)
    dgain = jnp.sum(dh * (x * r), axis=0, keepdims=True)
    return dx, dgain


def _gelu(x):
    return 0.5 * x * (1.0 + lax.erf(x * 0.7071067811865476))


def _gelu_grad(x):
    return 0.5 * (1.0 + lax.erf(x * 0.7071067811865476)) + x * jnp.exp(-0.5 * x * x) * 0.3989422804014327


def _sigmoid(x):
    return 1.0 / (1.0 + jnp.exp(-x))


def _adamw_math(w, g, m, v):
    nm = ADAM_B1 * m + (1.0 - ADAM_B1) * g
    nv = ADAM_B2 * v + (1.0 - ADAM_B2) * (g * g)
    m_hat = nm / (1.0 - ADAM_B1 ** ADAM_STEP)
    v_hat = nv / (1.0 - ADAM_B2 ** ADAM_STEP)
    return -ADAM_LR * (m_hat / (jnp.sqrt(v_hat) + ADAM_EPS) + ADAM_WD * w), nm, nv


def _tok(tm, w):
    return pl.BlockSpec((tm, w), lambda i: (i, 0))


def _full(shape):
    return pl.BlockSpec(shape, lambda *i: (0,) * len(shape))


def _norm_proj(x, gain, gath, out_dtype, name, tm):
    t, dm = x.shape
    n = gath.shape[1] * NDEV

    def body(x_ref, g_ref, gath_ref, proj_ref, hb_ref, wbuf, sems):
        @pl.when(pl.program_id(0) == 0)
        def _():
            _load_weight(gath_ref, wbuf, sems)
        h, _ = _rms_fwd(x_ref[...], g_ref[...])
        hb = h.astype(BF16)
        hb_ref[...] = hb
        proj_ref[...] = _dot(hb, wbuf[...], NT).astype(out_dtype)

    return dict(
        body=body, grid=(t // tm,), name=name, args=[x, gain, gath],
        out_shape=[jax.ShapeDtypeStruct((t, n), out_dtype), jax.ShapeDtypeStruct((t, dm), BF16)],
        in_specs=[_tok(tm, dm), _full((1, dm)), ANY], out_specs=[_tok(tm, n), _tok(tm, dm)],
        scratch=[pltpu.VMEM((n, dm), BF16), pltpu.SemaphoreType.DMA((NDEV,))])


def _proj_bwd_norm(dy, x, gain, dres, gath, name, tm):
    t, dm = x.shape
    n = gath.shape[1] * NDEV

    def body(dy_ref, x_ref, g_ref, dres_ref, gath_ref, dx_ref, dxb_ref, dgain_ref, wbuf, sems):
        @pl.when(pl.program_id(0) == 0)
        def _():
            _load_weight(gath_ref, wbuf, sems)
            dgain_ref[...] = jnp.zeros_like(dgain_ref)
        xv, gain_v = x_ref[...], g_ref[...]
        _, r = _rms_fwd(xv, gain_v)
        dh = _dot(dy_ref[...], wbuf[...], NN)
        dx, dgain = _rms_bwd(dh, xv, r, gain_v)
        dx = dres_ref[...] + dx
        dx_ref[...] = dx
        dxb_ref[...] = dx.astype(BF16)
        dgain_ref[...] += dgain

    return dict(
        body=body, grid=(t // tm,), name=name, args=[dy, x, gain, dres, gath],
        out_shape=[jax.ShapeDtypeStruct((t, dm), F32), jax.ShapeDtypeStruct((t, dm), BF16),
                   jax.ShapeDtypeStruct((1, dm), F32)],
        in_specs=[_tok(tm, n), _tok(tm, dm), _full((1, dm)), _tok(tm, dm), ANY],
        out_specs=[_tok(tm, dm), _tok(tm, dm), _full((1, dm))],
        scratch=[pltpu.VMEM((n, dm), BF16), pltpu.SemaphoreType.DMA((NDEV,))])


def _wgrad(a, b, name, tmm=256):
    t, m = a.shape
    n = b.shape[1]

    def body(a_ref, b_ref, o_ref):
        o_ref[...] = _dot(a_ref[...], b_ref[...], TN).astype(BF16)

    return dict(
        body=body, grid=(m // tmm,), name=name, args=[a, b], out_shape=[jax.ShapeDtypeStruct((m, n), BF16)],
        in_specs=[pl.BlockSpec((t, tmm), lambda j: (0, j)), pl.BlockSpec((t, n), lambda j: (0, 0))],
        out_specs=[pl.BlockSpec((tmm, n), lambda j: (j, 0))])


def _halo_specs(tm, t, width, col_blocks):
    nb8 = tm // 8
    last = t // 8 - 1
    prev = [pl.BlockSpec((8, width), lambda i, cb=cb: (jnp.maximum(i * nb8 - 1, 0), cb)) for cb in col_blocks]
    nxt = [pl.BlockSpec((8, width), lambda i, cb=cb: (jnp.minimum((i + 1) * nb8, last), cb)) for cb in col_blocks]
    return prev, nxt


def _shift_rows(z, prev_row, next_row):
    tm = z.shape[0]
    row = lax.broadcasted_iota(jnp.int32, z.shape, 0)
    zm1 = jnp.where(row == 0, prev_row, pltpu.roll(z, 1, 0))
    zp1 = jnp.where(row == tm - 1, next_row, pltpu.roll(z, tm - 1, 0))
    return zm1, zp1


def _gating_fwd(proj, lng, lnb, wsp_ref, bsp_ref, aw):
    tm = proj.shape[0]
    a_u = _gelu(proj[:, 0:aw])
    gv = _gelu(proj[:, aw:2 * aw])
    mu = jnp.mean(gv, axis=-1, keepdims=True)
    xc = gv - mu
    rstd = lax.rsqrt(jnp.mean(xc * xc, axis=-1, keepdims=True) + EPS)
    vn = xc * rstd
    a_v = (vn * lng + lnb).astype(BF16)
    gd = aw // A_GROUPS
    rows = []
    for c in range(tm // CHUNK):
        cols = []
        for g in range(A_GROUPS):
            blk = a_v[c * CHUNK:(c + 1) * CHUNK, g * gd:(g + 1) * gd]
            cols.append(_dot(wsp_ref[g], blk, NN) + bsp_ref[g])
        rows.append(jnp.concatenate(cols, axis=1))
    mixed = jnp.concatenate(rows, axis=0)
    return a_u, vn, rstd, a_v, mixed


def _even_core_fwd(proj, x0, lng, lnb, wsp, bspb, cw, gath, tm):
    t, dm = x0.shape
    aw = lng.shape[1]
    bw = cw.shape[1]
    assert aw == bw and 2 * aw + 3 * bw == proj.shape[1]
    nt = t // tm
    prev, nxt = _halo_specs(tm, t, bw, [3, 4])

    def body(proj_ref, cp_ref, hp_ref, cn_ref, hn_ref, x0_ref, lng_ref, lnb_ref, wsp_ref, bsp_ref, cw_ref, gath_ref,
             x1_ref, y_ref, wbuf, sems):
        i = pl.program_id(0)

        @pl.when(i == 0)
        def _():
            _load_weight(gath_ref, wbuf, sems)
        proj_v = proj_ref[...]
        a_u, _, _, _, mixed = _gating_fwd(proj_v, lng_ref[...], lnb_ref[...], wsp_ref, bsp_ref, aw)
        a_out = a_u * mixed
        bb = proj_v[:, 2 * aw:2 * aw + bw]
        z = proj_v[:, 2 * aw + bw:2 * aw + 2 * bw] * proj_v[:, 2 * aw + 2 * bw:]
        zprev = jnp.where(i > 0, cp_ref[7:8, :] * hp_ref[7:8, :], 0.0)
        znext = jnp.where(i < nt - 1, cn_ref[0:1, :] * hn_ref[0:1, :], 0.0)
        zm1, zp1 = _shift_rows(z, zprev, znext)
        cwv = cw_ref[...]
        conv = zm1 * cwv[0:1, :] + z * cwv[1:2, :] + zp1 * cwv[2:3, :]
        y = jnp.concatenate([a_out, bb * conv], axis=1).astype(BF16)
        y_ref[...] = y
        x1_ref[...] = x0_ref[...] + _dot(y, wbuf[...], NN)

    return dict(
        body=body, grid=(nt,), name="even_core_fwd",
        args=[proj, proj, proj, proj, proj, x0, lng, lnb, wsp, bspb, cw, gath],
        out_shape=[jax.ShapeDtypeStruct((t, dm), F32), jax.ShapeDtypeStruct((t, aw + bw), BF16)],
        in_specs=[_tok(tm, proj.shape[1]), prev[0], prev[1], nxt[0], nxt[1], _tok(tm, dm), _full(lng.shape),
                  _full(lnb.shape), _full(wsp.shape), _full(bspb.shape), _full(cw.shape), ANY],
        out_specs=[_tok(tm, dm), _tok(tm, aw + bw)],
        scratch=[pltpu.VMEM((gath.shape[1] * NDEV, dm), BF16), pltpu.SemaphoreType.DMA((NDEV,))])


def _even_core_bwd(proj, dx1, lng, lnb, wsp, bspb, cw, gath, tm):
    t, dm = dx1.shape
    aw, bw = lng.shape[1], cw.shape[1]
    gd = aw // A_GROUPS
    nt = t // tm
    inw = proj.shape[1]
    prev, nxt = _halo_specs(tm, t, bw, [2, 3, 4])
    nb8 = tm // 8
    last8 = t // 8 - 1

    def body(proj_ref, bp_ref, cp_ref, hp_ref, bn_ref, cn_ref, hn_ref, dx_ref, dxp_ref, dxn_ref,
             lng_ref, lnb_ref, wsp_ref, bsp_ref, cw_ref, gath_ref,
             dproj_ref, dlng_ref, dlnb_ref, dwsp_ref, dbsp_ref, dcw_ref, wbuf, sems):
        i = pl.program_id(0)

        @pl.when(i == 0)
        def _():
            _load_weight(gath_ref, wbuf, sems)
            dlng_ref[...] = jnp.zeros_like(dlng_ref)
            dlnb_ref[...] = jnp.zeros_like(dlnb_ref)
            dwsp_ref[...] = jnp.zeros_like(dwsp_ref)
            dbsp_ref[...] = jnp.zeros_like(dbsp_ref)
            dcw_ref[...] = jnp.zeros_like(dcw_ref)
        proj_v = proj_ref[...]
        lng_v = lng_ref[...]
        a_u, vn, rstd, a_v, mixed = _gating_fwd(proj_v, lng_v, lnb_ref[...], wsp_ref, bsp_ref, aw)
        w = wbuf[...]
        dy = _dot(dx_ref[...].astype(BF16), w, NT)
        da_out, db_out = dy[:, 0:aw], dy[:, aw:]
        da_u = da_out * mixed
        dmixed = da_out * a_u
        dmb = dmixed.astype(BF16)
        rows = []
        for c in range(tm // CHUNK):
            cols = []
            for g in range(A_GROUPS):
                r0, c0 = c * CHUNK, g * gd
                dm_cg = dmb[r0:r0 + CHUNK, c0:c0 + gd]
                cols.append(_dot(wsp_ref[g], dm_cg, TN))
                dwsp_ref[g] += _dot(dm_cg, a_v[r0:r0 + CHUNK, c0:c0 + gd], NT)
                dbsp_ref[g] += dmixed[r0:r0 + CHUNK, c0:c0 + gd]
            rows.append(jnp.concatenate(cols, axis=1))
        dav = jnp.concatenate(rows, axis=0)
        dlng_ref[...] += jnp.sum(dav * vn, axis=0, keepdims=True)
        dlnb_ref[...] += jnp.sum(dav, axis=0, keepdims=True)
        dvn = dav * lng_v
        dgv = rstd * (dvn - jnp.mean(dvn, axis=-1, keepdims=True) - vn * jnp.mean(dvn * vn, axis=-1, keepdims=True))
        dv_pre = dgv * _gelu_grad(proj_v[:, aw:2 * aw])
        du_pre = da_u * _gelu_grad(proj_v[:, 0:aw])
        bb = proj_v[:, 2 * aw:2 * aw + bw]
        bc = proj_v[:, 2 * aw + bw:2 * aw + 2 * bw]
        bh = proj_v[:, 2 * aw + 2 * bw:]
        z = bc * bh
        zprev = jnp.where(i > 0, cp_ref[7:8, :] * hp_ref[7:8, :], 0.0)
        znext = jnp.where(i < nt - 1, cn_ref[0:1, :] * hn_ref[0:1, :], 0.0)
        zm1, zp1 = _shift_rows(z, zprev, znext)
        cwv = cw_ref[...]
        conv = zm1 * cwv[0:1, :] + z * cwv[1:2, :] + zp1 * cwv[2:3, :]
        dbb = db_out * conv
        dconv = db_out * bb
        dx_edge = jnp.concatenate([dxp_ref[...], dxn_ref[...]], axis=0).astype(BF16)
        dy_edge = _dot(dx_edge, w[aw:, :], NT)
        dcprev = jnp.where(i > 0, dy_edge[7:8, :] * bp_ref[7:8, :], 0.0)
        dcnext = jnp.where(i < nt - 1, dy_edge[8:9, :] * bn_ref[0:1, :], 0.0)
        dcm1, dcp1 = _shift_rows(dconv, dcprev, dcnext)
        dz = dcp1 * cwv[0:1, :] + dconv * cwv[1:2, :] + dcm1 * cwv[2:3, :]
        dcw_ref[0:1, :] += jnp.sum(dconv * zm1, axis=0, keepdims=True)
        dcw_ref[1:2, :] += jnp.sum(dconv * z, axis=0, keepdims=True)
        dcw_ref[2:3, :] += jnp.sum(dconv * zp1, axis=0, keepdims=True)
        dproj_ref[...] = jnp.concatenate([du_pre, dv_pre, dbb, dz * bh, dz * bc], axis=1).astype(BF16)

    row8 = lambda f: pl.BlockSpec((8, dm), f)
    return dict(
        body=body, grid=(nt,), name="even_core_bwd",
        args=[proj, proj, proj, proj, proj, proj, proj, dx1, dx1, dx1, lng, lnb, wsp, bspb, cw, gath],
        out_shape=[jax.ShapeDtypeStruct((t, inw), BF16), jax.ShapeDtypeStruct((1, aw), F32),
                   jax.ShapeDtypeStruct((1, aw), F32), jax.ShapeDtypeStruct(wsp.shape, F32),
                   jax.ShapeDtypeStruct((A_GROUPS, CHUNK, gd), F32), jax.ShapeDtypeStruct(cw.shape, F32)],
        in_specs=[_tok(tm, inw), prev[0], prev[1], prev[2], nxt[0], nxt[1], nxt[2], _tok(tm, dm),
                  row8(lambda i: (jnp.maximum(i * nb8 - 1, 0), 0)), row8(lambda i: (jnp.minimum((i + 1) * nb8, last8), 0)),
                  _full(lng.shape), _full(lnb.shape), _full(wsp.shape), _full(bspb.shape), _full(cw.shape), ANY],
        out_specs=[_tok(tm, inw), _full((1, aw)), _full((1, aw)), _full(wsp.shape),
                   _full((A_GROUPS, CHUNK, gd)), _full(cw.shape)],
        scratch=[pltpu.VMEM((gath.shape[1] * NDEV, dm), BF16), pltpu.SemaphoreType.DMA((NDEV,))])


def _ff_chunks(f, width=1024):
    return [(c0, min(c0 + width, f)) for c0 in range(0, f, width)]


def _ffn_up(x, gain, gath_g, gath_u, name, tm):
    t, dm = x.shape
    f = gath_g.shape[1] * NDEV

    def body(x_ref, g_ref, gg_ref, gu_ref, gate_ref, up_ref, act_ref, wg, wu, sems):
        @pl.when(pl.program_id(0) == 0)
        def _():
            _load_weight(gg_ref, wg, sems)
            _load_weight(gu_ref, wu, sems)
        h, _ = _rms_fwd(x_ref[...], g_ref[...])
        hb = h.astype(BF16)
        for c0, c1 in _ff_chunks(f):
            gate = _dot(hb, wg[c0:c1, :], NT)
            up = _dot(hb, wu[c0:c1, :], NT)
            gate_ref[:, c0:c1] = gate.astype(BF16)
            up_ref[:, c0:c1] = up.astype(BF16)
            act_ref[:, c0:c1] = (gate * _sigmoid(gate) * up).astype(BF16)

    o = jax.ShapeDtypeStruct((t, f), BF16)
    return dict(
        body=body, grid=(t // tm,), name=name, args=[x, gain, gath_g, gath_u], out_shape=[o, o, o],
        in_specs=[_tok(tm, dm), _full((1, dm)), ANY, ANY], out_specs=[_tok(tm, f)] * 3,
        scratch=[pltpu.VMEM((f, dm), BF16), pltpu.VMEM((f, dm), BF16), pltpu.SemaphoreType.DMA((NDEV,))])


def _ffn_down(x, act, gath_d, name, tm):
    t, dm = x.shape
    f = act.shape[1]

    def body(x_ref, a_ref, gd_ref, xo_ref, wd, sems):
        @pl.when(pl.program_id(0) == 0)
        def _():
            _load_weight(gd_ref, wd, sems)
        xo_ref[...] = x_ref[...] + _dot(a_ref[...], wd[...], NN)

    return dict(
        body=body, grid=(t // tm,), name=name, args=[x, act, gath_d], out_shape=[jax.ShapeDtypeStruct((t, dm), F32)],
        in_specs=[_tok(tm, dm), _tok(tm, f), ANY], out_specs=[_tok(tm, dm)],
        scratch=[pltpu.VMEM((f, dm), BF16), pltpu.SemaphoreType.DMA((NDEV,))])


def _ffn_down_loss(x, act, gath_d, target, gain, name, tm):
    t, dm = x.shape
    f = act.shape[1]
    steps = t // tm

    def body(x_ref, a_ref, gd_ref, t_ref, g_ref, loss_ref, dx_ref, dxb_ref, dgain_ref, wd, acc, sems):
        i = pl.program_id(0)

        @pl.when(i == 0)
        def _():
            _load_weight(gd_ref, wd, sems)
            acc[...] = jnp.zeros_like(acc)
            dgain_ref[...] = jnp.zeros_like(dgain_ref)
        xv = x_ref[...] + _dot(a_ref[...], wd[...], NN)
        gain_v = g_ref[...]
        y, r = _rms_fwd(xv, gain_v)
        e = y - t_ref[...]
        acc[...] += jnp.sum(e * e, axis=0, keepdims=True)
        dx, dgain = _rms_bwd(e * (1.0 / dm), xv, r, gain_v)
        dx_ref[...] = dx
        dxb_ref[...] = dx.astype(BF16)
        dgain_ref[...] += dgain

        @pl.when(i == steps - 1)
        def _():
            loss_ref[...] = jnp.sum(acc[...], axis=-1, keepdims=True) * (0.5 / dm)

    return dict(
        body=body, grid=(steps,), name=name, args=[x, act, gath_d, target, gain],
        out_shape=[jax.ShapeDtypeStruct((1, 1), F32), jax.ShapeDtypeStruct((t, dm), F32),
                   jax.ShapeDtypeStruct((t, dm), BF16), jax.ShapeDtypeStruct((1, dm), F32)],
        in_specs=[_tok(tm, dm), _tok(tm, f), ANY, _tok(tm, dm), _full((1, dm))],
        out_specs=[_full((1, 1)), _tok(tm, dm), _tok(tm, dm), _full((1, dm))],
        scratch=[pltpu.VMEM((f, dm), BF16), pltpu.VMEM((1, dm), F32), pltpu.SemaphoreType.DMA((NDEV,))])


def _ffn_bwd(dxo, x, gate, up, gain, gath_g, gath_u, gath_d, name, tm):
    t, dm = x.shape
    f = gate.shape[1]

    def body(dxo_ref, x_ref, gate_ref, up_ref, g_ref, gg_ref, gu_ref, gd_ref,
             dx_ref, dxb_ref, dg_ref, du_ref, hb_ref, dgain_ref, wg, wu, wd, sems):
        @pl.when(pl.program_id(0) == 0)
        def _():
            _load_weight(gg_ref, wg, sems)
            _load_weight(gu_ref, wu, sems)
            _load_weight(gd_ref, wd, sems)
            dgain_ref[...] = jnp.zeros_like(dgain_ref)
        xv, gain_v, dxo_v = x_ref[...], g_ref[...], dxo_ref[...]
        h, r = _rms_fwd(xv, gain_v)
        hb_ref[...] = h.astype(BF16)
        dxob = dxo_v.astype(BF16)
        dh = jnp.zeros_like(xv)
        for c0, c1 in _ff_chunks(f):
            gate_v = gate_ref[:, c0:c1].astype(F32)
            up_v = up_ref[:, c0:c1].astype(F32)
            s = _sigmoid(gate_v)
            silu = gate_v * s
            dact = _dot(dxob, wd[c0:c1, :], NT)
            dg = (dact * up_v * (s * (1.0 + gate_v * (1.0 - s)))).astype(BF16)
            du = (dact * silu).astype(BF16)
            dg_ref[:, c0:c1] = dg
            du_ref[:, c0:c1] = du
            dh = dh + _dot(dg, wg[c0:c1, :], NN) + _dot(du, wu[c0:c1, :], NN)
        dx, dgain = _rms_bwd(dh, xv, r, gain_v)
        dx = dxo_v + dx
        dx_ref[...] = dx
        dxb_ref[...] = dx.astype(BF16)
        dgain_ref[...] += dgain

    return dict(
        body=body, grid=(t // tm,), name=name, args=[dxo, x, gate, up, gain, gath_g, gath_u, gath_d],
        out_shape=[jax.ShapeDtypeStruct((t, dm), F32), jax.ShapeDtypeStruct((t, dm), BF16),
                   jax.ShapeDtypeStruct((t, f), BF16), jax.ShapeDtypeStruct((t, f), BF16),
                   jax.ShapeDtypeStruct((t, dm), BF16), jax.ShapeDtypeStruct((1, dm), F32)],
        in_specs=[_tok(tm, dm), _tok(tm, dm), _tok(tm, f), _tok(tm, f), _full((1, dm)), ANY, ANY, ANY],
        out_specs=[_tok(tm, dm), _tok(tm, dm), _tok(tm, f), _tok(tm, f), _tok(tm, dm), _full((1, dm))],
        scratch=[pltpu.VMEM((f, dm), BF16), pltpu.VMEM((f, dm), BF16), pltpu.VMEM((f, dm), BF16),
                 pltpu.SemaphoreType.DMA((NDEV,))])


def _t5_buckets(rel):
    nb = N_BUCKETS // 2
    ret = jnp.where(rel > 0, nb, 0)
    n = jnp.abs(rel)
    max_exact = nb // 2
    nf = jnp.maximum(n, 1).astype(jnp.float32)
    large = max_exact + (jnp.log(nf / max_exact) / math.log(MAX_DISTANCE / max_exact)
                         * (nb - max_exact)).astype(jnp.int32)
    large = jnp.minimum(large, nb - 1)
    return ret + jnp.where(n < max_exact, n, large)


def _bucket_table():
    qi = jnp.arange(CHUNK, dtype=jnp.int32)[:, None]
    kj = jnp.arange(3 * CHUNK, dtype=jnp.int32)[None, :]
    rel = kj - CHUNK - qi
    return jnp.where(jnp.abs(rel) <= CHUNK, _t5_buckets(rel), -1)


def _bias_table(rel_bias_t, buckets):
    nh = rel_bias_t.shape[0]

    def body(rb_ref, bk_ref, o_ref):
        bk = bk_ref[...]
        for h in range(nh):
            acc = jnp.where(bk < 0, NEG, 0.0).astype(F32)
            for b in range(N_BUCKETS):
                acc = jnp.where(bk == b, rb_ref[h, b] * LOG2E, acc)
            o_ref[h] = acc

    return dict(
        body=body, grid=(1,), name="bias_table", args=[rel_bias_t, buckets],
        out_shape=[jax.ShapeDtypeStruct((nh,) + buckets.shape, F32)],
        in_specs=[pl.BlockSpec(memory_space=pltpu.SMEM), _full(buckets.shape)],
        out_specs=[_full((nh,) + buckets.shape)])


def _rel_bias_grad(dbias, buckets):
    nh = dbias.shape[0]

    def body(db_ref, bk_ref, o_ref):
        bk = bk_ref[...]
        lane = lax.broadcasted_iota(jnp.int32, (1, 128), 1)
        for h in range(nh):
            d = db_ref[h]
            row = jnp.zeros((1, 128), F32)
            for b in range(N_BUCKETS):
                s = jnp.sum(jnp.sum(jnp.where(bk == b, d, 0.0), axis=1, keepdims=True), axis=0, keepdims=True)
                row = jnp.where(lane == b, s, row)
            o_ref[h:h + 1, :] = row

    return pl.pallas_call(
        body, out_shape=jax.ShapeDtypeStruct((nh, 128), F32),
        in_specs=[pl.BlockSpec(memory_space=pltpu.VMEM), pl.BlockSpec(memory_space=pltpu.VMEM)],
        out_specs=pl.BlockSpec(memory_space=pltpu.VMEM), compiler_params=_cp(0), name="rel_bias_grad")(dbias, buckets)


def _half_masks():
    lane = lax.broadcasted_iota(jnp.int32, (CHUNK, 128), 1)
    return lane < HEAD_DIM, lane >= HEAD_DIM


def _kv_low(ref, starts, hk, lo):
    kt = (hk // 2) * 128
    out = []
    for jj in range(3):
        blk = ref[pl.ds(starts[jj], CHUNK), kt:kt + 128]
        if hk % 2 == 1:
            blk = pltpu.roll(blk, HEAD_DIM, 1)
        out.append(jnp.where(lo, blk, jnp.zeros_like(blk)))
    return out


def _stack_heads(tile_a, tile_b):
    return jnp.concatenate([tile_a, pltpu.roll(tile_a, HEAD_DIM, 1), tile_b, pltpu.roll(tile_b, HEAD_DIM, 1)], axis=0)


def _unstack_heads(o4):
    return (o4[0:CHUNK] + pltpu.roll(o4[CHUNK:2 * CHUNK], HEAD_DIM, 1),
            o4[2 * CHUNK:3 * CHUNK] + pltpu.roll(o4[3 * CHUNK:], HEAD_DIM, 1))


ATT_SLAB = 32


def _softmax_slab(s_scr, hk, g, r0, bias_ref, sink_ref, n, nblk):
    scale = HEAD_DIM ** -0.5 * LOG2E
    h = (N_HEADS // N_KV) * hk + g
    s = []
    for jj in range(3):
        sj = (s_scr[hk, jj, pl.ds(g * CHUNK + r0, ATT_SLAB), :] * scale
              + bias_ref[h, pl.ds(r0, ATT_SLAB), jj * CHUNK:(jj + 1) * CHUNK])
        if jj == 0:
            sj = jnp.where(n > 0, sj, NEG)
        if jj == 2:
            sj = jnp.where(n < nblk - 1, sj, NEG)
        s.append(sj)
    sink = sink_ref[h] * LOG2E
    m = jnp.maximum(jnp.max(jnp.maximum(jnp.maximum(s[0], s[1]), s[2]), axis=-1, keepdims=True), sink)
    e = [jnp.exp2(sj - m) for sj in s]
    es = jnp.exp2(sink - m)
    inv = 1.0 / (jnp.sum(e[0] + e[1] + e[2], axis=-1, keepdims=True) + es)
    return [ej * inv for ej in e], es * inv


def _key_block_starts(n, nblk):
    return [pl.multiple_of(jnp.clip(n - 1 + jj, 0, nblk - 1) * CHUNK, CHUNK) for jj in range(3)]


def _attn_fwd(qkv, x2, bias, sink, gath):
    t, dm = x2.shape
    nblk = t // CHUNK
    kvw = N_KV * HEAD_DIM
    kcb, vcb = dm // kvw, dm // kvw + 1
    slab = (N_KV, 3, 4 * CHUNK, CHUNK)

    def body(q_ref, k_ref, v_ref, x2_ref, bias_ref, sink_ref, gath_ref, x3_ref, att_ref, p_ref, ps_ref,
             wbuf, s_scr, sems):
        n = pl.program_id(0)

        @pl.when(n == 0)
        def _():
            _load_weight(gath_ref, wbuf, sems)
        lo, _ = _half_masks()
        lane_s = lax.broadcasted_iota(jnp.int32, (ATT_SLAB, 128), 1)
        starts = _key_block_starts(n, nblk)
        tiles = []
        for hk in range(N_KV):
            c0 = (2 * hk) * 128
            k_lo = _kv_low(k_ref, starts, hk, lo)
            v_lo = _kv_low(v_ref, starts, hk, lo)
            q4 = _stack_heads(q_ref[:, c0:c0 + 128], q_ref[:, c0 + 128:c0 + 256])
            for jj in range(3):
                s_scr[hk, jj] = _dot(q4, k_lo[jj], NT)
            for g in range(4):
                h = 4 * hk + g
                for r0 in range(0, CHUNK, ATT_SLAB):
                    p, ps = _softmax_slab(s_scr, hk, g, r0, bias_ref, sink_ref, n, nblk)
                    for jj in range(3):
                        p_ref[hk, jj, g * CHUNK + r0:g * CHUNK + r0 + ATT_SLAB, :] = p[jj].astype(BF16)
                    rest = jnp.zeros((ATT_SLAB, 128), F32) if h == 0 else ps_ref[r0:r0 + ATT_SLAB, :]
                    ps_ref[r0:r0 + ATT_SLAB, :] = jnp.where(lane_s == h, ps, rest)
            o4 = _dot(p_ref[hk, 0], v_lo[0], NN) + _dot(p_ref[hk, 1], v_lo[1], NN) + _dot(p_ref[hk, 2], v_lo[2], NN)
            tiles += list(_unstack_heads(o4))
        att = jnp.concatenate(tiles, axis=1).astype(BF16)
        att_ref[...] = att
        x3_ref[...] = x2_ref[...] + _dot(att, wbuf[...], NN)

    blk = pl.BlockSpec((CHUNK, dm), lambda n: (n, 0))
    return dict(
        body=body, grid=(nblk,), name="attn_fwd", args=[qkv, qkv, qkv, x2, bias, sink, gath],
        out_shape=[jax.ShapeDtypeStruct((t, dm), F32), jax.ShapeDtypeStruct((t, dm), BF16),
                   jax.ShapeDtypeStruct((nblk,) + slab, BF16), jax.ShapeDtypeStruct((t, 128), F32)],
        in_specs=[blk, pl.BlockSpec((t, kvw), lambda n: (0, kcb)), pl.BlockSpec((t, kvw), lambda n: (0, vcb)), blk,
                  _full(bias.shape), pl.BlockSpec(memory_space=pltpu.SMEM), ANY],
        out_specs=[blk, blk, pl.BlockSpec((None,) + slab, lambda n: (n, 0, 0, 0, 0)),
                   pl.BlockSpec((CHUNK, 128), lambda n: (n, 0))],
        scratch=[pltpu.VMEM((gath.shape[1] * NDEV, dm), BF16), pltpu.VMEM(slab, F32),
                 pltpu.SemaphoreType.DMA((NDEV,))])


def _attn_bwd(qkv, att, probs, sink_probs, dx3, bias_shape, gath):
    t, dm = dx3.shape
    nblk = t // CHUNK
    kvw = N_KV * HEAD_DIM
    kcb, vcb = dm // kvw, dm // kvw + 1
    scale = HEAD_DIM ** -0.5
    slab = (N_KV, 3, 4 * CHUNK, CHUNK)

    def body(q_ref, k_ref, v_ref, att_ref, p_ref, ps_ref, dx_ref, gath_ref,
             dq_ref, dk_ref, dv_ref, dbias_ref, dsink_ref, wbuf, dp_scr, ds_scr, prod_scr, dsum_scr, sems):
        n = pl.program_id(0)

        @pl.when(n == 0)
        def _():
            _load_weight(gath_ref, wbuf, sems)
            dk_ref[...] = jnp.zeros_like(dk_ref)
            dv_ref[...] = jnp.zeros_like(dv_ref)
            dbias_ref[...] = jnp.zeros_like(dbias_ref)
            dsink_ref[...] = jnp.zeros_like(dsink_ref)
        lo, hi = _half_masks()
        lane_s = lax.broadcasted_iota(jnp.int32, (ATT_SLAB, 128), 1)
        starts = _key_block_starts(n, nblk)
        dout = _dot(dx_ref[...].astype(BF16), wbuf[...], NT)
        prod_scr[...] = dout * att_ref[...].astype(F32)
        doutb = dout.astype(BF16)
        dq_tiles = []
        for hk in range(N_KV):
            kt = (hk // 2) * 128
            c0 = (2 * hk) * 128
            k_lo = _kv_low(k_ref, starts, hk, lo)
            v_lo = _kv_low(v_ref, starts, hk, lo)
            q4 = _stack_heads(q_ref[:, c0:c0 + 128], q_ref[:, c0 + 128:c0 + 256])
            do4 = _stack_heads(doutb[:, c0:c0 + 128], doutb[:, c0 + 128:c0 + 256])
            for jj in range(3):
                dp_scr[hk, jj] = _dot(do4, v_lo[jj], NT)
            for g in range(4):
                h = 4 * hk + g
                for r0 in range(0, CHUNK, ATT_SLAB):
                    rows = slice(g * CHUNK + r0, g * CHUNK + r0 + ATT_SLAB)
                    pt = prod_scr[r0:r0 + ATT_SLAB, c0 + (g // 2) * 128:c0 + (g // 2 + 1) * 128]
                    msk = lane_s < HEAD_DIM if g % 2 == 0 else lane_s >= HEAD_DIM
                    dsum = jnp.sum(jnp.where(msk, pt, 0.0), axis=-1, keepdims=True)
                    rest = jnp.zeros((ATT_SLAB, 128), F32) if h == 0 else dsum_scr[r0:r0 + ATT_SLAB, :]
                    dsum_scr[r0:r0 + ATT_SLAB, :] = jnp.where(lane_s == h, dsum, rest)
                    for jj in range(3):
                        ds = p_ref[hk, jj, rows, :].astype(F32) * (dp_scr[hk, jj, rows, :] - dsum)
                        dbias_ref[h, r0:r0 + ATT_SLAB, jj * CHUNK:(jj + 1) * CHUNK] += ds
                        ds_scr[hk, jj, rows, :] = ds.astype(BF16)
            dq4 = jnp.zeros((4 * CHUNK, 128), F32)
            for jj in range(3):
                ds4 = ds_scr[hk, jj]
                dq4 = dq4 + _dot(ds4, k_lo[jj], NN) * scale
                dkj = _dot(ds4, q4, TN) * scale
                dvj = _dot(p_ref[hk, jj], do4, TN)
                if hk % 2 == 1:
                    dkj, dvj = pltpu.roll(dkj, HEAD_DIM, 1), pltpu.roll(dvj, HEAD_DIM, 1)
                keep = lo if hk % 2 == 0 else hi
                dk_ref[pl.ds(starts[jj], CHUNK), kt:kt + 128] += jnp.where(keep, dkj, 0.0)
                dv_ref[pl.ds(starts[jj], CHUNK), kt:kt + 128] += jnp.where(keep, dvj, 0.0)
            dq_tiles += list(_unstack_heads(dq4))
        dq_ref[...] = jnp.concatenate(dq_tiles, axis=1).astype(BF16)
        dsink_ref[...] -= jnp.sum(ps_ref[...] * dsum_scr[...], axis=0, keepdims=True)

    blk = pl.BlockSpec((CHUNK, dm), lambda n: (n, 0))
    return dict(
        body=body, grid=(nblk,), name="attn_bwd", args=[qkv, qkv, qkv, att, probs, sink_probs, dx3, gath],
        out_shape=[jax.ShapeDtypeStruct((t, dm), BF16), jax.ShapeDtypeStruct((t, kvw), F32),
                   jax.ShapeDtypeStruct((t, kvw), F32), jax.ShapeDtypeStruct(bias_shape, F32),
                   jax.ShapeDtypeStruct((1, 128), F32)],
        in_specs=[blk, pl.BlockSpec((t, kvw), lambda n: (0, kcb)), pl.BlockSpec((t, kvw), lambda n: (0, vcb)),
                  blk, pl.BlockSpec((None,) + slab, lambda n: (n, 0, 0, 0, 0)),
                  pl.BlockSpec((CHUNK, 128), lambda n: (n, 0)), blk, ANY],
        out_specs=[blk, _full((t, kvw)), _full((t, kvw)), _full(bias_shape), _full((1, 128))],
        scratch=[pltpu.VMEM((gath.shape[1] * NDEV, dm), BF16), pltpu.VMEM(slab, F32), pltpu.VMEM(slab, BF16),
                 pltpu.VMEM((CHUNK, dm), F32), pltpu.VMEM((CHUNK, 128), F32), pltpu.SemaphoreType.DMA((NDEV,))])


def _finish_weight(recvs, w, m, v, name):
    nl, r, dm = w.shape
    assert nl == len(recvs) and all(rc.shape[1:] == (r, dm) for rc in recvs)
    td = dm // 2
    wspec = pl.BlockSpec((None, r, td), lambda l, j: (l, 0, j))

    def body(*refs):
        r_refs = refs[:nl]
        w_ref, m_ref, v_ref, g_ref, d_ref, nm_ref, nv_ref = refs[nl:]
        layer = pl.program_id(0)
        for li in range(nl):
            @pl.when(layer == li)
            def _():
                g = r_refs[li][0].astype(F32)
                for d in range(1, recvs[li].shape[0]):
                    g = g + r_refs[li][d].astype(F32)
                delta, nm, nv = _adamw_math(w_ref[...], g, m_ref[...], v_ref[...])
                g_ref[...] = g
                d_ref[...] = delta
                nm_ref[...] = nm
                nv_ref[...] = nv

    o = jax.ShapeDtypeStruct(w.shape, F32)
    return dict(
        body=body, grid=(nl, 2), name=name, args=[*recvs, w, m, v], out_shape=[o, o, o, o],
        in_specs=[pl.BlockSpec((rc.shape[0], r, td), lambda l, j: (0, 0, j)) for rc in recvs] + [wspec] * 3,
        out_specs=[wspec] * 4)


def _adamw_small(ws, ms, vs, slots, late_slots, loss_slots, name):
    n = len(ws)

    def total(ref):
        acc = ref[0].astype(F32)
        for d in range(1, NDEV):
            acc = acc + ref[d].astype(F32)
        return acc

    def body(*refs):
        ins, outs = refs[:4 * n + 2], refs[4 * n + 2:]
        for i in range(n):
            w_ref, m_ref, v_ref, s_ref = ins[4 * i:4 * i + 4]
            g_ref, d_ref, nm_ref, nv_ref = outs[4 * i:4 * i + 4]
            g_ref[...] = total(s_ref)
            if i == 0:
                g_ref[0:1, :] = total(ins[4 * n])
            d_ref[...], nm_ref[...], nv_ref[...] = _adamw_math(w_ref[...], g_ref[...], m_ref[...], v_ref[...])
        outs[4 * n][...] = total(ins[4 * n + 1])

    args, out_shape = [], []
    for w, m, v, s in zip(ws, ms, vs, slots):
        args += [w, m, v, s]
        out_shape += [jax.ShapeDtypeStruct(w.shape, F32)] * 4
    args += [late_slots, loss_slots]
    out_shape.append(jax.ShapeDtypeStruct((1, 1), F32))
    out = pl.pallas_call(
        body, grid=(1,), out_shape=tuple(out_shape), in_specs=[_full(a.shape) for a in args],
        out_specs=tuple(_full(o.shape) for o in out_shape), compiler_params=_cp(), name=name)(*args)
    return [tuple(out[4 * i:4 * i + 4]) for i in range(n)], out[4 * n]


def kernel(x, norm_mix, norm_ffn, even_w_in, even_v_ln_g, even_v_ln_b, even_w_spatial, even_b_spatial, even_conv_w, even_w_out, attn_w_qkv, attn_sink, rel_bias, attn_w_out, ffn_w_gate, ffn_w_up, ffn_w_down, final_norm, loss_target, m_norm_mix, m_norm_ffn, m_even_w_in, m_even_v_ln_g, m_even_v_ln_b, m_even_w_spatial, m_even_b_spatial, m_even_conv_w, m_even_w_out, m_attn_w_qkv, m_attn_sink, m_rel_bias, m_attn_w_out, m_ffn_w_gate, m_ffn_w_up, m_ffn_w_down, m_final_norm, v_norm_mix, v_norm_ffn, v_even_w_in, v_even_v_ln_g, v_even_v_ln_b, v_even_w_spatial, v_even_b_spatial, v_even_conv_w, v_even_w_out, v_attn_w_qkv, v_attn_sink, v_rel_bias, v_attn_w_out, v_ffn_w_gate, v_ffn_w_up, v_ffn_w_down, v_final_norm):
    t, dm = x.shape[1], x.shape[2]
    aw = even_v_ln_g.shape[1]
    bw = even_conv_w.shape[2] * NDEV
    gd = aw // A_GROUPS
    tm = min(512, t // 2)
    tmf = min(256, t // 2)
    me = _my_index()
    row = lambda a: a.reshape(1, -1)

    colT = lambda w: w.T.astype(BF16)
    sh = dict(winT=colT(even_w_in[0]), wqkvT=colT(attn_w_qkv[0]), wgT0=colT(ffn_w_gate[0]), wuT0=colT(ffn_w_up[0]),
              wgT1=colT(ffn_w_gate[1]), wuT1=colT(ffn_w_up[1]), woe=even_w_out[0].astype(BF16),
              woa=attn_w_out[0].astype(BF16), wd0=ffn_w_down[0].astype(BF16), wd1=ffn_w_down[1].astype(BF16))
    gather = lambda names: _GatherCarry([sh[n] for n in names])

    in_full = lambda a: lax.dynamic_update_slice(jnp.zeros((3, bw), F32), a[0], (0, me * (bw // NDEV)))

    x0 = x[0]
    wsp_b = even_w_spatial[0].astype(BF16)
    bspb = jnp.broadcast_to(even_b_spatial[0][:, :, None], (A_GROUPS, CHUNK, gd))
    buckets = _bucket_table()
    sink = attn_sink[0]

    (bias,), ((g_winT,), (cw_slots,)) = _call(
        _bias_table(rel_bias.T, buckets), [gather(["winT"]), _BroadcastCarry([in_full(even_conv_w)])])
    cw_full = jnp.sum(cw_slots, axis=0)
    (proj, h0b), (g_woe, g_wgT0) = _call(_norm_proj(x0, row(norm_mix[0]), g_winT, F32, "in_proj", tm),
                                         gather(["woe", "wgT0"]))
    (x1, yb), (g_wuT0,) = _call(_even_core_fwd(proj, x0, even_v_ln_g, even_v_ln_b, wsp_b, bspb, cw_full, g_woe, tm),
                                gather(["wuT0"]))
    (gate0, up0, act0), (g_wd0,) = _call(_ffn_up(x1, row(norm_ffn[0]), g_wgT0, g_wuT0, "ffn_up0", tmf), gather(["wd0"]))
    (x2,), (g_wqkvT,) = _call(_ffn_down(x1, act0, g_wd0, "ffn_down0", tm), gather(["wqkvT"]))
    (qkv, h2b), (g_woa,) = _call(_norm_proj(x2, row(norm_mix[1]), g_wqkvT, BF16, "qkv_proj", tm), gather(["woa"]))
    (x3, attb, probs, sink_probs), (g_wgT1, g_wuT1) = _call(
        _attn_fwd(qkv, x2, bias, sink, g_woa), gather(["wgT1", "wuT1"]))
    (gate1, up1, act1), (g_wd1,) = _call(_ffn_up(x3, row(norm_ffn[1]), g_wgT1, g_wuT1, "ffn_up1", tmf), gather(["wd1"]))
    (loss_part, dx4, dx4b, d_final), _ = _call(
        _ffn_down_loss(x3, act1, g_wd1, loss_target[0], row(final_norm), "ffn_down1_loss", tm))

    (dx3, dx3b, dg1, du1, h3b, d_nffn1), _ = _call(
        _ffn_bwd(dx4, x3, gate1, up1, row(norm_ffn[1]), g_wgT1, g_wuT1, g_wd1, "ffn_bwd1", tmf))
    (p_wgT1,), _ = _call(_wgrad(dg1, h3b, "wgrad_gate1"))
    (p_wuT1,), (a_wgT1,) = _call(_wgrad(du1, h3b, "wgrad_up1"), _PairCarry(p_wgT1))
    (p_wd1,), (a_wuT1,) = _call(_wgrad(act1, dx4b, "wgrad_down1"), _PairCarry(p_wuT1))
    (dq, dk, dv, dbias, dsink), ((r_wgT1,), (a_wd1,)) = _call(
        _attn_bwd(qkv, attb, probs, sink_probs, dx3, bias.shape, g_woa),
        [_ChipSumCarry(p_wgT1, a_wgT1), _PairCarry(p_wd1)])
    (p_woa,), _ = _call(_wgrad(attb, dx3b, "wgrad_attn_out"))
    d_relb = _rel_bias_grad(dbias, buckets)[:, 0:N_BUCKETS].T
    dqkv = jnp.concatenate([dq, dk.astype(BF16), dv.astype(BF16)], axis=1)
    (dx2, dx2b, d_nmix1), (r_wuT1,) = _call(
        _proj_bwd_norm(dqkv, x2, row(norm_mix[1]), dx3, g_wqkvT, "qkv_bwd", tm), _ChipSumCarry(p_wuT1, a_wuT1))
    (p_wqkvT,), _ = _call(_wgrad(dqkv, h2b, "wgrad_qkv"))
    (dx1, dx1b, dg0, du0, h1b, d_nffn0), ((r_wd1,), (r_woa, r_wqkvT)) = _call(
        _ffn_bwd(dx2, x1, gate0, up0, row(norm_ffn[0]), g_wgT0, g_wuT0, g_wd0, "ffn_bwd0", tmf),
        [_ChipSumCarry(p_wd1, a_wd1), _GradCarry([p_woa, p_wqkvT])])
    (p_wgT0,), _ = _call(_wgrad(dg0, h1b, "wgrad_gate0"))
    (p_wuT0,), (a_wgT0,) = _call(_wgrad(du0, h1b, "wgrad_up0"), _PairCarry(p_wgT0))
    (p_wd0,), ((r_wgT0,), (a_wuT0,)) = _call(
        _wgrad(act0, dx2b, "wgrad_down0"), [_ChipSumCarry(p_wgT0, a_wgT0), _PairCarry(p_wuT0)])
    (dproj, d_lng, d_lnb, d_wsp, d_bsp3, d_cw), ((r_wuT0,), (a_wd0,)) = _call(
        _even_core_bwd(proj, dx1, even_v_ln_g, even_v_ln_b, wsp_b, bspb, cw_full, g_woe, tm),
        [_ChipSumCarry(p_wuT0, a_wuT0), _PairCarry(p_wd0)])
    small_names = ["norm_mix", "norm_ffn", "even_v_ln_g", "even_v_ln_b", "even_w_spatial", "even_b_spatial",
                   "even_conv_w", "attn_sink", "rel_bias", "final_norm"]
    small_parts = [jnp.concatenate([jnp.zeros_like(d_nmix1), d_nmix1]), jnp.concatenate([d_nffn0, d_nffn1]),
                   d_lng, d_lnb, d_wsp[None].astype(BF16), jnp.sum(d_bsp3, axis=-1)[None], d_cw,
                   dsink[:, 0:N_HEADS], d_relb, d_final, loss_part]
    (p_winT,), (r_wd0,) = _call(_wgrad(dproj, h0b, "wgrad_in"), _ChipSumCarry(p_wd0, a_wd0))
    (p_woe,), ((a_winT,), small_slots) = _call(
        _wgrad(yb, dx1b, "wgrad_even_out"), [_PairCarry(p_winT), _BroadcastCarry(small_parts)])
    (dx0, _, d_nmix0), ((r_winT,), (r_woe,)) = _call(
        _proj_bwd_norm(dproj, x0, row(norm_mix[0]), dx1, g_winT, "in_proj_bwd", tm),
        [_ChipSumCarry(p_winT, a_winT), _GradCarry([p_woe])])

    grads = {}

    order = ["norm_mix", "norm_ffn", "even_w_in", "even_v_ln_g", "even_v_ln_b", "even_w_spatial", "even_b_spatial",
             "even_conv_w", "even_w_out", "attn_w_qkv", "attn_sink", "rel_bias", "attn_w_out", "ffn_w_gate",
             "ffn_w_up", "ffn_w_down", "final_norm"]
    ws = dict(norm_mix=norm_mix, norm_ffn=norm_ffn, even_w_in=even_w_in, even_v_ln_g=even_v_ln_g,
              even_v_ln_b=even_v_ln_b, even_w_spatial=even_w_spatial, even_b_spatial=even_b_spatial,
              even_conv_w=even_conv_w, even_w_out=even_w_out, attn_w_qkv=attn_w_qkv, attn_sink=attn_sink,
              rel_bias=rel_bias, attn_w_out=attn_w_out, ffn_w_gate=ffn_w_gate, ffn_w_up=ffn_w_up,
              ffn_w_down=ffn_w_down, final_norm=final_norm)
    ms = dict(norm_mix=m_norm_mix, norm_ffn=m_norm_ffn, even_w_in=m_even_w_in, even_v_ln_g=m_even_v_ln_g,
              even_v_ln_b=m_even_v_ln_b, even_w_spatial=m_even_w_spatial, even_b_spatial=m_even_b_spatial,
              even_conv_w=m_even_conv_w, even_w_out=m_even_w_out, attn_w_qkv=m_attn_w_qkv, attn_sink=m_attn_sink,
              rel_bias=m_rel_bias, attn_w_out=m_attn_w_out, ffn_w_gate=m_ffn_w_gate, ffn_w_up=m_ffn_w_up,
              ffn_w_down=m_ffn_w_down, final_norm=m_final_norm)
    vs = dict(norm_mix=v_norm_mix, norm_ffn=v_norm_ffn, even_w_in=v_even_w_in, even_v_ln_g=v_even_v_ln_g,
              even_v_ln_b=v_even_v_ln_b, even_w_spatial=v_even_w_spatial, even_b_spatial=v_even_b_spatial,
              even_conv_w=v_even_conv_w, even_w_out=v_even_w_out, attn_w_qkv=v_attn_w_qkv, attn_sink=v_attn_sink,
              rel_bias=v_rel_bias, attn_w_out=v_attn_w_out, ffn_w_gate=v_ffn_w_gate, ffn_w_up=v_ffn_w_up,
              ffn_w_down=v_ffn_w_down, final_norm=v_final_norm)
    big = dict(ffn_w_gate=([r_wgT0, r_wgT1], True), even_w_in=([r_winT], True), even_w_out=([r_woe], False),
               attn_w_qkv=([r_wqkvT], True), attn_w_out=([r_woa], False), ffn_w_up=([r_wuT0, r_wuT1], True),
               ffn_w_down=([r_wd0, r_wd1], False))
    delta, new_m, new_v = {}, {}, {}
    late_slots = None
    for n, (recvs, transposed) in big.items():
        lay = (lambda a: jnp.swapaxes(a, 1, 2)) if transposed else (lambda a: a)
        spec = _finish_weight(recvs, lay(ws[n]), lay(ms[n]), lay(vs[n]), "finish_" + n)
        if late_slots is None:
            outs, (late_slots,) = _call(spec, _BroadcastCarry([d_nmix0]))
        else:
            outs, _ = _call(spec)
        grads[n], delta[n], new_m[n], new_v[n] = [lay(o) for o in outs]
    shaped = lambda n, a: in_full(a) if n == "even_conv_w" else (a.reshape(1, dm) if n == "final_norm" else a)
    pick = lambda dct: [shaped(n, dct[n]) for n in small_names]
    results, loss11 = _adamw_small(pick(ws), pick(ms), pick(vs), small_slots[:-1], late_slots, small_slots[-1],
                                   "adamw_small")
    mine = lambda a: lax.dynamic_slice(a, (0, me * (bw // NDEV)), (3, bw // NDEV))[None]
    for n, res in zip(small_names, results):
        for dst, a in zip((grads, delta, new_m, new_v), res):
            dst[n] = mine(a) if n == "even_conv_w" else (a.reshape(dm) if n == "final_norm" else a)
    loss = loss11[0, 0]
    return (loss, dx0[None], *[grads[n] for n in order], *[delta[n] for n in order],
            *[new_m[n] for n in order], *[new_v[n] for n in order])
```

```python
import math

import jax
import jax.numpy as jnp
import numpy as np
from jax import lax
from jax.experimental import pallas as pl
from jax.experimental.pallas import tpu as pltpu

F32, BF16 = jnp.float32, jnp.bfloat16
NDEV = 8
EPS = 1e-6
CHUNK = 128
A_GROUPS = 4
N_HEADS, N_KV, HEAD_DIM = 16, 4, 64
N_BUCKETS, MAX_DISTANCE = 32, 128
NEG = -1e30
LOG2E = 1.4426950408889634
ADAM_LR, ADAM_B1, ADAM_B2, ADAM_EPS, ADAM_WD, ADAM_STEP = 0.001, 0.9, 0.999, 1e-08, 0.01, 10
VMEM_LIMIT = 56 * 1024 * 1024
MESH = pl.DeviceIdType.MESH
NT = (((1,), (1,)), ((), ()))
NN = (((1,), (0,)), ((), ()))
TN = (((0,), (0,)), ((), ()))
ANY = pl.BlockSpec(memory_space=pl.ANY)


def _cp(n_grid=1):
    return pltpu.CompilerParams(dimension_semantics=("arbitrary",) * n_grid, vmem_limit_bytes=VMEM_LIMIT)


def _dot(a, b, dims):
    return lax.dot_general(a, b, dims, preferred_element_type=F32)


def _my_index():
    return 4 * lax.axis_index("x") + 2 * lax.axis_index("y") + lax.axis_index("c")


def _peer(k):
    x, y, c = lax.axis_index("x"), lax.axis_index("y"), lax.axis_index("c")
    px = 1 - x if k & 4 else x
    py = 1 - y if k & 2 else y
    pc = 1 - c if k & 1 else c
    return (px, py, pc)


def _weight_copies(gath_ref, wbuf, sems, first_sem=0):
    rows = gath_ref.shape[1]
    return [pltpu.make_async_copy(gath_ref.at[d], wbuf.at[pl.ds(d * rows, rows), :], sems.at[first_sem + d])
            for d in range(NDEV)]


def _load_weight(gath_ref, wbuf, sems):
    cps = _weight_copies(gath_ref, wbuf, sems)
    for c in cps:
        c.start()
    for c in cps:
        c.wait()


def _load_weights_by_chunk(pairs, sems, chunks, first_step):
    copies = [_weight_copies(g, w, sems, NDEV * i) for i, (g, w) in enumerate(pairs)]
    rows = pairs[0][0].shape[1]

    @pl.when(first_step)
    def _():
        for cps in copies:
            for c in cps:
                c.start()

    need = [min(NDEV, -(-c1 // rows)) for _, c1 in chunks]

    def wait_for(i):
        lo = need[i - 1] if i else 0

        @pl.when(first_step)
        def _():
            for cps in copies:
                for c in cps[lo:need[i]]:
                    c.wait()

    return wait_for


class _GatherCarry:
    def __init__(self, pieces):
        self.inputs = list(pieces)
        self.n = len(pieces)
        self.out_shape = [jax.ShapeDtypeStruct((NDEV,) + p.shape, p.dtype) for p in pieces]
        self.scratch = [pltpu.SemaphoreType.DMA((7 * self.n,)), pltpu.SemaphoreType.DMA((7 * self.n,)),
                        pltpu.SemaphoreType.DMA((self.n,))]

    def _ctx(self):
        x, y, c = lax.axis_index("x"), lax.axis_index("y"), lax.axis_index("c")
        chips = [(1 - x, y), (x, 1 - y), (1 - x, 1 - y)]
        return (x, y, c), (x, y, 1 - c), chips, c

    def _copy(self, k, j, block, to, ins, outs, sems, src=None):
        send_sems, recv_sems, _ = sems
        slot = outs[j].at[4 * block[0] + 2 * block[1] + block[2]]
        return pltpu.make_async_remote_copy(
            src_ref=slot if src is None else src, dst_ref=slot, send_sem=send_sems.at[k * self.n + j],
            recv_sem=recv_sems.at[k * self.n + j], device_id=to, device_id_type=MESH)

    def start(self, ins, outs, sems):
        me, sibling, chips, c = self._ctx()
        for j in range(self.n):
            pltpu.make_async_copy(ins[j], outs[j].at[4 * me[0] + 2 * me[1] + me[2]], sems[2].at[j]).start()
            self._copy(0, j, me, sibling, ins, outs, sems, src=ins[j]).start()
            for q, chip in enumerate(chips):
                self._copy(1 + q, j, me, (*chip, c), ins, outs, sems, src=ins[j]).start()

    def mid(self, ins, outs, sems):
        me, sibling, chips, c = self._ctx()
        for q, chip in enumerate(chips):
            for j in range(self.n):
                self._copy(1 + q, j, (*chip, c), me, ins, outs, sems).wait_recv()
                self._copy(4 + q, j, (*chip, c), sibling, ins, outs, sems).start()

    def finish(self, ins, outs, sems):
        me, sibling, chips, c = self._ctx()
        for j in range(self.n):
            self._copy(0, j, sibling, me, ins, outs, sems).wait_recv()
            for q, chip in enumerate(chips):
                self._copy(4 + q, j, (*chip, 1 - c), me, ins, outs, sems).wait_recv()
        for j in range(self.n):
            self._copy(0, j, me, sibling, ins, outs, sems, src=ins[j]).wait_send()
            for q, chip in enumerate(chips):
                self._copy(1 + q, j, me, (*chip, c), ins, outs, sems, src=ins[j]).wait_send()
                self._copy(4 + q, j, (*chip, c), sibling, ins, outs, sems).wait_send()
            pltpu.make_async_copy(ins[j], outs[j].at[0], sems[2].at[j]).wait()


class _GradCarry:
    def __init__(self, pieces):
        self.inputs = list(pieces)
        self.n = len(pieces)
        self.rows = [p.shape[0] // NDEV for p in pieces]
        self.out_shape = [jax.ShapeDtypeStruct((NDEV, r, p.shape[1]), p.dtype) for p, r in zip(pieces, self.rows)]
        self.scratch = [pltpu.SemaphoreType.DMA((7 * self.n,)), pltpu.SemaphoreType.DMA((7 * self.n,)),
                        pltpu.SemaphoreType.DMA((self.n,))]

    def _copies(self, ins, outs, sems):
        me = _my_index()
        local, remote = [], []
        for j in range(self.n):
            r = self.rows[j]
            local.append(pltpu.make_async_copy(ins[j].at[pl.ds(pl.multiple_of(me * r, 16), r), :], outs[j].at[me],
                                               sems[2].at[j]))
            for k in range(1, NDEV):
                peer = _peer(k)
                pidx = 4 * peer[0] + 2 * peer[1] + peer[2]
                remote.append(pltpu.make_async_remote_copy(
                    src_ref=ins[j].at[pl.ds(pl.multiple_of(pidx * r, 16), r), :], dst_ref=outs[j].at[me],
                    send_sem=sems[0].at[(k - 1) * self.n + j], recv_sem=sems[1].at[(k - 1) * self.n + j],
                    device_id=peer, device_id_type=MESH))
        return local, remote

    def start(self, ins, outs, sems):
        local, remote = self._copies(ins, outs, sems)
        for cp in local + remote:
            cp.start()

    def mid(self, ins, outs, sems):
        pass

    def finish(self, ins, outs, sems):
        local, remote = self._copies(ins, outs, sems)
        for cp in remote + local:
            cp.wait()


class _BroadcastCarry:
    def __init__(self, parts):
        self.inputs = list(parts)
        self.n = len(self.inputs)
        self.out_shape = [jax.ShapeDtypeStruct((NDEV,) + p.shape, p.dtype) for p in self.inputs]
        self.scratch = [pltpu.SemaphoreType.DMA((7 * self.n,)), pltpu.SemaphoreType.DMA((7 * self.n,)),
                        pltpu.SemaphoreType.DMA((self.n,))]

    def _copies(self, ins, outs, sems):
        me = _my_index()
        cps = []
        for j in range(self.n):
            cps.append(pltpu.make_async_copy(ins[j], outs[j].at[me], sems[2].at[j]))
            cps += [pltpu.make_async_remote_copy(
                src_ref=ins[j], dst_ref=outs[j].at[me], send_sem=sems[0].at[(k - 1) * self.n + j],
                recv_sem=sems[1].at[(k - 1) * self.n + j], device_id=_peer(k), device_id_type=MESH)
                for k in range(1, NDEV)]
        return cps

    def start(self, ins, outs, sems):
        for cp in self._copies(ins, outs, sems):
            cp.start()

    def mid(self, ins, outs, sems):
        pass

    def finish(self, ins, outs, sems):
        for cp in self._copies(ins, outs, sems):
            cp.wait()


class _PairCarry:
    def __init__(self, piece):
        self.inputs = [piece]
        self.r = piece.shape[0] // NDEV
        self.out_shape = [jax.ShapeDtypeStruct((4, self.r, piece.shape[1]), piece.dtype)]
        self.scratch = [pltpu.SemaphoreType.DMA((4,)), pltpu.SemaphoreType.DMA((4,))]

    def _copies(self, ins, outs, sems):
        x, y, c = lax.axis_index("x"), lax.axis_index("y"), lax.axis_index("c")
        return [pltpu.make_async_remote_copy(
            src_ref=ins[0].at[pl.ds(pl.multiple_of((2 * q + 1 - c) * self.r, 16), self.r), :], dst_ref=outs[0].at[q],
            send_sem=sems[0].at[q], recv_sem=sems[1].at[q], device_id=(x, y, 1 - c), device_id_type=MESH)
            for q in range(4)]

    def start(self, ins, outs, sems):
        for cp in self._copies(ins, outs, sems):
            cp.start()

    def mid(self, ins, outs, sems):
        pass

    def finish(self, ins, outs, sems):
        for cp in self._copies(ins, outs, sems):
            cp.wait()


class _ChipSumCarry:
    def __init__(self, piece, landed):
        self.inputs = [piece, landed]
        self.r, dm = piece.shape[0] // NDEV, piece.shape[1]
        self.out_shape = [jax.ShapeDtypeStruct((4, self.r, dm), piece.dtype)]
        self.scratch = [pltpu.VMEM((4, self.r, dm), piece.dtype), pltpu.VMEM((8, self.r, dm), piece.dtype),
                        pltpu.SemaphoreType.DMA((8,)), pltpu.SemaphoreType.DMA((3,)), pltpu.SemaphoreType.DMA((3,)),
                        pltpu.SemaphoreType.DMA(())]

    def _copies(self, outs, scr):
        sums, _, _, send_sems, recv_sems, local_sem = scr
        x, y, c = lax.axis_index("x"), lax.axis_index("y"), lax.axis_index("c")
        mine = 2 * x + y
        local = pltpu.make_async_copy(sums.at[mine], outs[0].at[mine], local_sem)
        remote = []
        for k in range(1, 4):
            px = 1 - x if k & 2 else x
            py = 1 - y if k & 1 else y
            remote.append(pltpu.make_async_remote_copy(
                src_ref=sums.at[2 * px + py], dst_ref=outs[0].at[mine], send_sem=send_sems.at[k - 1],
                recv_sem=recv_sems.at[k - 1], device_id=(px, py, c), device_id_type=MESH))
        return local, remote

    def start(self, ins, outs, scr):
        sums, stage, stage_sems = scr[0], scr[1], scr[2]
        c = lax.axis_index("c")
        loads = []
        for q in range(4):
            loads.append((
                pltpu.make_async_copy(ins[0].at[pl.ds(pl.multiple_of((2 * q + c) * self.r, 16), self.r), :],
                                      stage.at[2 * q], stage_sems.at[2 * q]),
                pltpu.make_async_copy(ins[1].at[q], stage.at[2 * q + 1], stage_sems.at[2 * q + 1])))
        for a, b in loads:
            a.start()
            b.start()
        for q, (a, b) in enumerate(loads):
            a.wait()
            b.wait()
            sums[q] = (stage[2 * q].astype(F32) + stage[2 * q + 1].astype(F32)).astype(sums.dtype)
        local, remote = self._copies(outs, scr)
        for cp in [local] + remote:
            cp.start()

    def mid(self, ins, outs, scr):
        pass

    def finish(self, ins, outs, scr):
        local, remote = self._copies(outs, scr)
        for cp in remote + [local]:
            cp.wait()


def _call(spec, carry=None):
    body, grid = spec["body"], spec["grid"]
    in_specs, out_specs, out_shape = list(spec["in_specs"]), list(spec["out_specs"]), list(spec["out_shape"])
    scratch, args = list(spec.get("scratch", [])), list(spec["args"])
    if carry is None:
        out = pl.pallas_call(body, grid=grid, in_specs=in_specs, out_specs=tuple(out_specs),
                             out_shape=tuple(out_shape), scratch_shapes=scratch, compiler_params=_cp(len(grid)),
                             name=spec["name"])(*args)
        return tuple(out), ()
    carries = list(carry) if isinstance(carry, (list, tuple)) else [carry]
    n_in, n_out, n_s = len(in_specs), len(out_specs), len(scratch)
    steps = int(np.prod(grid))

    def split(refs, counts):
        parts, o = [], 0
        for cnt in counts:
            parts.append(refs[o:o + cnt])
            o += cnt
        return parts

    c_in = [len(cr.inputs) for cr in carries]
    c_out = [len(cr.out_shape) for cr in carries]
    c_scr = [len(cr.scratch) for cr in carries]

    def wrapped(*refs):
        ins, cins, outs, couts, scr, cscr = split(refs, [n_in, sum(c_in), n_out, sum(c_out), n_s, sum(c_scr)])
        per = list(zip(carries, split(cins, c_in), split(couts, c_out), split(cscr, c_scr)))
        step = pl.program_id(0)
        for ax in range(1, len(grid)):
            step = step * grid[ax] + pl.program_id(ax)

        @pl.when(step == 0)
        def _():
            for cr, ci, co, cs in per:
                cr.start(ci, co, cs)
        if steps >= 3:
            @pl.when(step == steps - 2)
            def _():
                for cr, ci, co, cs in per:
                    cr.mid(ci, co, cs)
        body(*ins, *outs, *scr)

        @pl.when(step == steps - 1)
        def _():
            for cr, ci, co, cs in per:
                if steps < 3:
                    cr.mid(ci, co, cs)
                cr.finish(ci, co, cs)

    out = pl.pallas_call(
        wrapped, grid=grid, in_specs=in_specs + [ANY] * sum(c_in), out_specs=tuple(out_specs + [ANY] * sum(c_out)),
        out_shape=tuple(out_shape + [s for cr in carries for s in cr.out_shape]),
        scratch_shapes=scratch + [s for cr in carries for s in cr.scratch],
        compiler_params=_cp(len(grid)), name=spec["name"])(*args, *[a for cr in carries for a in cr.inputs])
    c_res = [tuple(p) for p in split(out[n_out:], c_out)]
    return tuple(out[:n_out]), (c_res if isinstance(carry, (list, tuple)) else c_res[0])


def _rms_fwd(x, gain):
    r = lax.rsqrt(jnp.mean(x * x, axis=-1, keepdims=True) + EPS)
    return x * r * gain, r


def _rms_bwd(dh, x, r, gain):
    a = dh * gain
    dx = r * a - x * (r * r * r) * jnp.mean(a * x, axis=-1, keepdims=True)
    dgain = jnp.sum(dh * (x * r), axis=0, keepdims=True)
    return dx, dgain


def _gelu(x):
    return 0.5 * x * (1.0 + lax.erf(x * 0.7071067811865476))


def _gelu_grad(x):
    return 0.5 * (1.0 + lax.erf(x * 0.7071067811865476)) + x * jnp.exp(-0.5 * x * x) * 0.3989422804014327


def _sigmoid(x):
    return 1.0 / (1.0 + jnp.exp(-x))


def _adamw_math(w, g, m, v):
    nm = ADAM_B1 * m + (1.0 - ADAM_B1) * g
    nv = ADAM_B2 * v + (1.0 - ADAM_B2) * (g * g)
    m_hat = nm / (1.0 - ADAM_B1 ** ADAM_STEP)
    v_hat = nv / (1.0 - ADAM_B2 ** ADAM_STEP)
    return -ADAM_LR * (m_hat / (jnp.sqrt(v_hat) + ADAM_EPS) + ADAM_WD * w), nm, nv


def _tok(tm, w):
    return pl.BlockSpec((tm, w), lambda i: (i, 0))


def _full(shape):
    return pl.BlockSpec(shape, lambda *i: (0,) * len(shape))


def _norm_proj(x, gain, gath, out_dtype, name, tm):
    t, dm = x.shape
    n = gath.shape[1] * NDEV

    def body(x_ref, g_ref, gath_ref, proj_ref, hb_ref, wbuf, sems):
        @pl.when(pl.program_id(0) == 0)
        def _():
            _load_weight(gath_ref, wbuf, sems)
        h, _ = _rms_fwd(x_ref[...], g_ref[...])
        hb = h.astype(BF16)
        hb_ref[...] = hb
        proj_ref[...] = _dot(hb, wbuf[...], NT).astype(out_dtype)

    return dict(
        body=body, grid=(t // tm,), name=name, args=[x, gain, gath],
        out_shape=[jax.ShapeDtypeStruct((t, n), out_dtype), jax.ShapeDtypeStruct((t, dm), BF16)],
        in_specs=[_tok(tm, dm), _full((1, dm)), ANY], out_specs=[_tok(tm, n), _tok(tm, dm)],
        scratch=[pltpu.VMEM((n, dm), BF16), pltpu.SemaphoreType.DMA((NDEV,))])


def _proj_bwd_norm(dys, x, gain, dres, gath, name, tm):
    t, dm = x.shape
    n = gath.shape[1] * NDEV
    widths = [d.shape[1] for d in dys]
    assert sum(widths) == n
    nd = len(dys)

    def body(*refs):
        dy_refs = refs[:nd]
        x_ref, g_ref, dres_ref, gath_ref, dx_ref, dxb_ref, dgain_ref, wbuf, sems = refs[nd:]

        @pl.when(pl.program_id(0) == 0)
        def _():
            _load_weight(gath_ref, wbuf, sems)
            dgain_ref[...] = jnp.zeros_like(dgain_ref)
        xv, gain_v = x_ref[...], g_ref[...]
        _, r = _rms_fwd(xv, gain_v)
        dh, c0 = None, 0
        for dy_ref, wd in zip(dy_refs, widths):
            part = _dot(dy_ref[...], wbuf[c0:c0 + wd, :], NN)
            dh = part if dh is None else dh + part
            c0 += wd
        dx, dgain = _rms_bwd(dh, xv, r, gain_v)
        dx = dres_ref[...] + dx
        dx_ref[...] = dx
        dxb_ref[...] = dx.astype(BF16)
        dgain_ref[...] += dgain

    return dict(
        body=body, grid=(t // tm,), name=name, args=[*dys, x, gain, dres, gath],
        out_shape=[jax.ShapeDtypeStruct((t, dm), F32), jax.ShapeDtypeStruct((t, dm), BF16),
                   jax.ShapeDtypeStruct((1, dm), F32)],
        in_specs=[_tok(tm, wd) for wd in widths] + [_tok(tm, dm), _full((1, dm)), _tok(tm, dm), ANY],
        out_specs=[_tok(tm, dm), _tok(tm, dm), _full((1, dm))],
        scratch=[pltpu.VMEM((n, dm), BF16), pltpu.SemaphoreType.DMA((NDEV,))])


def _wgrad(a, b, name, tmm=256):
    parts = list(a) if isinstance(a, (list, tuple)) else [a]
    t = parts[0].shape[0]
    n = b.shape[1]
    tiles = [p.shape[1] // tmm for p in parts]
    first = [sum(tiles[:i]) for i in range(len(parts))]
    m = sum(tiles) * tmm

    def body(*refs):
        a_refs, b_ref, o_ref = refs[:len(parts)], refs[len(parts)], refs[len(parts) + 1]
        j = pl.program_id(0)
        for a_ref, j0, nt in zip(a_refs, first, tiles):
            if len(parts) == 1:
                o_ref[...] = _dot(a_ref[...], b_ref[...], TN).astype(BF16)
            else:
                @pl.when((j >= j0) & (j < j0 + nt))
                def _():
                    o_ref[...] = _dot(a_ref[...], b_ref[...], TN).astype(BF16)

    a_specs = [pl.BlockSpec((t, tmm), lambda j, j0=j0, nt=nt: (0, jnp.clip(j - j0, 0, nt - 1)))
               for j0, nt in zip(first, tiles)]
    return dict(
        body=body, grid=(sum(tiles),), name=name, args=[*parts, b], out_shape=[jax.ShapeDtypeStruct((m, n), BF16)],
        in_specs=a_specs + [pl.BlockSpec((t, n), lambda j: (0, 0))],
        out_specs=[pl.BlockSpec((tmm, n), lambda j: (j, 0))])


def _halo_specs(tm, t, width, col_blocks):
    nb8 = tm // 8
    last = t // 8 - 1
    prev = [pl.BlockSpec((8, width), lambda i, cb=cb: (jnp.maximum(i * nb8 - 1, 0), cb)) for cb in col_blocks]
    nxt = [pl.BlockSpec((8, width), lambda i, cb=cb: (jnp.minimum((i + 1) * nb8, last), cb)) for cb in col_blocks]
    return prev, nxt


def _shift_rows(z, prev_row, next_row):
    tm = z.shape[0]
    row = lax.broadcasted_iota(jnp.int32, z.shape, 0)
    zm1 = jnp.where(row == 0, prev_row, pltpu.roll(z, 1, 0))
    zp1 = jnp.where(row == tm - 1, next_row, pltpu.roll(z, tm - 1, 0))
    return zm1, zp1


def _gating_fwd(proj, lng, lnb, wsp_ref, bsp_ref, aw):
    tm = proj.shape[0]
    a_u = _gelu(proj[:, 0:aw])
    gv = _gelu(proj[:, aw:2 * aw])
    mu = jnp.mean(gv, axis=-1, keepdims=True)
    xc = gv - mu
    rstd = lax.rsqrt(jnp.mean(xc * xc, axis=-1, keepdims=True) + EPS)
    vn = xc * rstd
    a_v = (vn * lng + lnb).astype(BF16)
    gd = aw // A_GROUPS
    rows = []
    for c in range(tm // CHUNK):
        cols = []
        for g in range(A_GROUPS):
            blk = a_v[c * CHUNK:(c + 1) * CHUNK, g * gd:(g + 1) * gd]
            cols.append(_dot(wsp_ref[g], blk, NN) + bsp_ref[g])
        rows.append(jnp.concatenate(cols, axis=1))
    mixed = jnp.concatenate(rows, axis=0)
    return a_u, vn, rstd, a_v, mixed


def _even_core_fwd(proj, x0, lng, lnb, wsp, bspb, cw, gath, tm):
    t, dm = x0.shape
    aw = lng.shape[1]
    bw = cw.shape[1]
    assert aw == bw and 2 * aw + 3 * bw == proj.shape[1]
    nt = t // tm
    prev, nxt = _halo_specs(tm, t, bw, [3, 4])

    def body(proj_ref, cp_ref, hp_ref, cn_ref, hn_ref, x0_ref, lng_ref, lnb_ref, wsp_ref, bsp_ref, cw_ref, gath_ref,
             x1_ref, y_ref, wbuf, sems):
        i = pl.program_id(0)

        @pl.when(i == 0)
        def _():
            _load_weight(gath_ref, wbuf, sems)
        proj_v = proj_ref[...]
        a_u, _, _, _, mixed = _gating_fwd(proj_v, lng_ref[...], lnb_ref[...], wsp_ref, bsp_ref, aw)
        a_out = a_u * mixed
        bb = proj_v[:, 2 * aw:2 * aw + bw]
        z = proj_v[:, 2 * aw + bw:2 * aw + 2 * bw] * proj_v[:, 2 * aw + 2 * bw:]
        zprev = jnp.where(i > 0, cp_ref[7:8, :] * hp_ref[7:8, :], 0.0)
        znext = jnp.where(i < nt - 1, cn_ref[0:1, :] * hn_ref[0:1, :], 0.0)
        zm1, zp1 = _shift_rows(z, zprev, znext)
        cwv = cw_ref[...]
        conv = zm1 * cwv[0:1, :] + z * cwv[1:2, :] + zp1 * cwv[2:3, :]
        y = jnp.concatenate([a_out, bb * conv], axis=1).astype(BF16)
        y_ref[...] = y
        x1_ref[...] = x0_ref[...] + _dot(y, wbuf[...], NN)

    return dict(
        body=body, grid=(nt,), name="even_core_fwd",
        args=[proj, proj, proj, proj, proj, x0, lng, lnb, wsp, bspb, cw, gath],
        out_shape=[jax.ShapeDtypeStruct((t, dm), F32), jax.ShapeDtypeStruct((t, aw + bw), BF16)],
        in_specs=[_tok(tm, proj.shape[1]), prev[0], prev[1], nxt[0], nxt[1], _tok(tm, dm), _full(lng.shape),
                  _full(lnb.shape), _full(wsp.shape), _full(bspb.shape), _full(cw.shape), ANY],
        out_specs=[_tok(tm, dm), _tok(tm, aw + bw)],
        scratch=[pltpu.VMEM((gath.shape[1] * NDEV, dm), BF16), pltpu.SemaphoreType.DMA((NDEV,))])


def _even_core_bwd(proj, dx1, lng, lnb, wsp, bspb, cw, gath, tm):
    t, dm = dx1.shape
    aw, bw = lng.shape[1], cw.shape[1]
    gd = aw // A_GROUPS
    nt = t // tm
    inw = proj.shape[1]
    prev, nxt = _halo_specs(tm, t, bw, [2, 3, 4])
    nb8 = tm // 8
    last8 = t // 8 - 1

    def body(proj_ref, bp_ref, cp_ref, hp_ref, bn_ref, cn_ref, hn_ref, dx_ref, dxp_ref, dxn_ref,
             lng_ref, lnb_ref, wsp_ref, bsp_ref, cw_ref, gath_ref,
             dproj_ref, dlng_ref, dlnb_ref, dwsp_ref, dbsp_ref, dcw_ref, wbuf, sems):
        i = pl.program_id(0)

        @pl.when(i == 0)
        def _():
            _load_weight(gath_ref, wbuf, sems)
            dlng_ref[...] = jnp.zeros_like(dlng_ref)
            dlnb_ref[...] = jnp.zeros_like(dlnb_ref)
            dwsp_ref[...] = jnp.zeros_like(dwsp_ref)
            dbsp_ref[...] = jnp.zeros_like(dbsp_ref)
            dcw_ref[...] = jnp.zeros_like(dcw_ref)
        proj_v = proj_ref[...]
        lng_v = lng_ref[...]
        a_u, vn, rstd, a_v, mixed = _gating_fwd(proj_v, lng_v, lnb_ref[...], wsp_ref, bsp_ref, aw)
        w = wbuf[...]
        dy = _dot(dx_ref[...].astype(BF16), w, NT)
        da_out, db_out = dy[:, 0:aw], dy[:, aw:]
        da_u = da_out * mixed
        dmixed = da_out * a_u
        dmb = dmixed.astype(BF16)
        rows = []
        for c in range(tm // CHUNK):
            cols = []
            for g in range(A_GROUPS):
                r0, c0 = c * CHUNK, g * gd
                dm_cg = dmb[r0:r0 + CHUNK, c0:c0 + gd]
                cols.append(_dot(wsp_ref[g], dm_cg, TN))
                dwsp_ref[g] += _dot(dm_cg, a_v[r0:r0 + CHUNK, c0:c0 + gd], NT)
                dbsp_ref[g] += dmixed[r0:r0 + CHUNK, c0:c0 + gd]
            rows.append(jnp.concatenate(cols, axis=1))
        dav = jnp.concatenate(rows, axis=0)
        dlng_ref[...] += jnp.sum(dav * vn, axis=0, keepdims=True)
        dlnb_ref[...] += jnp.sum(dav, axis=0, keepdims=True)
        dvn = dav * lng_v
        dgv = rstd * (dvn - jnp.mean(dvn, axis=-1, keepdims=True) - vn * jnp.mean(dvn * vn, axis=-1, keepdims=True))
        dv_pre = dgv * _gelu_grad(proj_v[:, aw:2 * aw])
        du_pre = da_u * _gelu_grad(proj_v[:, 0:aw])
        bb = proj_v[:, 2 * aw:2 * aw + bw]
        bc = proj_v[:, 2 * aw + bw:2 * aw + 2 * bw]
        bh = proj_v[:, 2 * aw + 2 * bw:]
        z = bc * bh
        zprev = jnp.where(i > 0, cp_ref[7:8, :] * hp_ref[7:8, :], 0.0)
        znext = jnp.where(i < nt - 1, cn_ref[0:1, :] * hn_ref[0:1, :], 0.0)
        zm1, zp1 = _shift_rows(z, zprev, znext)
        cwv = cw_ref[...]
        conv = zm1 * cwv[0:1, :] + z * cwv[1:2, :] + zp1 * cwv[2:3, :]
        dbb = db_out * conv
        dconv = db_out * bb
        dx_edge = jnp.concatenate([dxp_ref[...], dxn_ref[...]], axis=0).astype(BF16)
        dy_edge = _dot(dx_edge, w[aw:, :], NT)
        dcprev = jnp.where(i > 0, dy_edge[7:8, :] * bp_ref[7:8, :], 0.0)
        dcnext = jnp.where(i < nt - 1, dy_edge[8:9, :] * bn_ref[0:1, :], 0.0)
        dcm1, dcp1 = _shift_rows(dconv, dcprev, dcnext)
        dz = dcp1 * cwv[0:1, :] + dconv * cwv[1:2, :] + dcm1 * cwv[2:3, :]
        dcw_ref[0:1, :] += jnp.sum(dconv * zm1, axis=0, keepdims=True)
        dcw_ref[1:2, :] += jnp.sum(dconv * z, axis=0, keepdims=True)
        dcw_ref[2:3, :] += jnp.sum(dconv * zp1, axis=0, keepdims=True)
        dproj_ref[...] = jnp.concatenate([du_pre, dv_pre, dbb, dz * bh, dz * bc], axis=1).astype(BF16)

    row8 = lambda f: pl.BlockSpec((8, dm), f)
    return dict(
        body=body, grid=(nt,), name="even_core_bwd",
        args=[proj, proj, proj, proj, proj, proj, proj, dx1, dx1, dx1, lng, lnb, wsp, bspb, cw, gath],
        out_shape=[jax.ShapeDtypeStruct((t, inw), BF16), jax.ShapeDtypeStruct((1, aw), F32),
                   jax.ShapeDtypeStruct((1, aw), F32), jax.ShapeDtypeStruct(wsp.shape, F32),
                   jax.ShapeDtypeStruct((A_GROUPS, CHUNK, gd), F32), jax.ShapeDtypeStruct(cw.shape, F32)],
        in_specs=[_tok(tm, inw), prev[0], prev[1], prev[2], nxt[0], nxt[1], nxt[2], _tok(tm, dm),
                  row8(lambda i: (jnp.maximum(i * nb8 - 1, 0), 0)), row8(lambda i: (jnp.minimum((i + 1) * nb8, last8), 0)),
                  _full(lng.shape), _full(lnb.shape), _full(wsp.shape), _full(bspb.shape), _full(cw.shape), ANY],
        out_specs=[_tok(tm, inw), _full((1, aw)), _full((1, aw)), _full(wsp.shape),
                   _full((A_GROUPS, CHUNK, gd)), _full(cw.shape)],
        scratch=[pltpu.VMEM((gath.shape[1] * NDEV, dm), BF16), pltpu.SemaphoreType.DMA((NDEV,))])


def _ff_chunks(f, width=1024):
    return [(c0, min(c0 + width, f)) for c0 in range(0, f, width)]


def _ffn_up(x, gain, gath_g, gath_u, name, tm):
    t, dm = x.shape
    f = gath_g.shape[1] * NDEV

    def body(x_ref, g_ref, gg_ref, gu_ref, gate_ref, up_ref, act_ref, wg, wu, sems):
        chunks = _ff_chunks(f)
        wait_for = _load_weights_by_chunk([(gg_ref, wg), (gu_ref, wu)], sems, chunks, pl.program_id(0) == 0)
        h, _ = _rms_fwd(x_ref[...], g_ref[...])
        hb = h.astype(BF16)
        for ci, (c0, c1) in enumerate(chunks):
            wait_for(ci)
            gate = _dot(hb, wg[c0:c1, :], NT)
            up = _dot(hb, wu[c0:c1, :], NT)
            gate_ref[:, c0:c1] = gate.astype(BF16)
            up_ref[:, c0:c1] = up.astype(BF16)
            act_ref[:, c0:c1] = (gate * _sigmoid(gate) * up).astype(BF16)

    o = jax.ShapeDtypeStruct((t, f), BF16)
    return dict(
        body=body, grid=(t // tm,), name=name, args=[x, gain, gath_g, gath_u], out_shape=[o, o, o],
        in_specs=[_tok(tm, dm), _full((1, dm)), ANY, ANY], out_specs=[_tok(tm, f)] * 3,
        scratch=[pltpu.VMEM((f, dm), BF16), pltpu.VMEM((f, dm), BF16), pltpu.SemaphoreType.DMA((2 * NDEV,))])


def _ffn_down(x, act, gath_d, name, tm):
    t, dm = x.shape
    f = act.shape[1]

    def body(x_ref, a_ref, gd_ref, xo_ref, wd, sems):
        @pl.when(pl.program_id(0) == 0)
        def _():
            _load_weight(gd_ref, wd, sems)
        xo_ref[...] = x_ref[...] + _dot(a_ref[...], wd[...], NN)

    return dict(
        body=body, grid=(t // tm,), name=name, args=[x, act, gath_d], out_shape=[jax.ShapeDtypeStruct((t, dm), F32)],
        in_specs=[_tok(tm, dm), _tok(tm, f), ANY], out_specs=[_tok(tm, dm)],
        scratch=[pltpu.VMEM((f, dm), BF16), pltpu.SemaphoreType.DMA((NDEV,))])


def _ffn_down_loss(x, act, gath_d, target, gain, name, tm):
    t, dm = x.shape
    f = act.shape[1]
    steps = t // tm

    def body(x_ref, a_ref, gd_ref, t_ref, g_ref, loss_ref, dx_ref, dxb_ref, dgain_ref, wd, acc, sems):
        i = pl.program_id(0)

        @pl.when(i == 0)
        def _():
            _load_weight(gd_ref, wd, sems)
            acc[...] = jnp.zeros_like(acc)
            dgain_ref[...] = jnp.zeros_like(dgain_ref)
        xv = x_ref[...] + _dot(a_ref[...], wd[...], NN)
        gain_v = g_ref[...]
        y, r = _rms_fwd(xv, gain_v)
        e = y - t_ref[...]
        acc[...] += jnp.sum(e * e, axis=0, keepdims=True)
        dx, dgain = _rms_bwd(e * (1.0 / dm), xv, r, gain_v)
        dx_ref[...] = dx
        dxb_ref[...] = dx.astype(BF16)
        dgain_ref[...] += dgain

        @pl.when(i == steps - 1)
        def _():
            loss_ref[...] = jnp.sum(acc[...], axis=-1, keepdims=True) * (0.5 / dm)

    return dict(
        body=body, grid=(steps,), name=name, args=[x, act, gath_d, target, gain],
        out_shape=[jax.ShapeDtypeStruct((1, 1), F32), jax.ShapeDtypeStruct((t, dm), F32),
                   jax.ShapeDtypeStruct((t, dm), BF16), jax.ShapeDtypeStruct((1, dm), F32)],
        in_specs=[_tok(tm, dm), _tok(tm, f), ANY, _tok(tm, dm), _full((1, dm))],
        out_specs=[_full((1, 1)), _tok(tm, dm), _tok(tm, dm), _full((1, dm))],
        scratch=[pltpu.VMEM((f, dm), BF16), pltpu.VMEM((1, dm), F32), pltpu.SemaphoreType.DMA((NDEV,))])


def _ffn_bwd(dxo, x, gate, up, gain, gath_g, gath_u, gath_d, name, tm):
    t, dm = x.shape
    f = gate.shape[1]

    def body(dxo_ref, x_ref, gate_ref, up_ref, g_ref, gg_ref, gu_ref, gd_ref,
             dx_ref, dxb_ref, dg_ref, du_ref, hb_ref, dgain_ref, wg, wu, wd, sems):
        chunks = _ff_chunks(f)
        first_step = pl.program_id(0) == 0
        wait_for = _load_weights_by_chunk([(gd_ref, wd), (gg_ref, wg), (gu_ref, wu)], sems, chunks, first_step)

        @pl.when(first_step)
        def _():
            dgain_ref[...] = jnp.zeros_like(dgain_ref)
        xv, gain_v, dxo_v = x_ref[...], g_ref[...], dxo_ref[...]
        h, r = _rms_fwd(xv, gain_v)
        hb_ref[...] = h.astype(BF16)
        dxob = dxo_v.astype(BF16)
        dh = jnp.zeros_like(xv)
        for ci, (c0, c1) in enumerate(chunks):
            wait_for(ci)
            gate_v = gate_ref[:, c0:c1].astype(F32)
            up_v = up_ref[:, c0:c1].astype(F32)
            s = _sigmoid(gate_v)
            silu = gate_v * s
            dact = _dot(dxob, wd[c0:c1, :], NT)
            dg = (dact * up_v * (s * (1.0 + gate_v * (1.0 - s)))).astype(BF16)
            du = (dact * silu).astype(BF16)
            dg_ref[:, c0:c1] = dg
            du_ref[:, c0:c1] = du
            dh = dh + _dot(dg, wg[c0:c1, :], NN) + _dot(du, wu[c0:c1, :], NN)
        dx, dgain = _rms_bwd(dh, xv, r, gain_v)
        dx = dxo_v + dx
        dx_ref[...] = dx
        dxb_ref[...] = dx.astype(BF16)
        dgain_ref[...] += dgain

    return dict(
        body=body, grid=(t // tm,), name=name, args=[dxo, x, gate, up, gain, gath_g, gath_u, gath_d],
        out_shape=[jax.ShapeDtypeStruct((t, dm), F32), jax.ShapeDtypeStruct((t, dm), BF16),
                   jax.ShapeDtypeStruct((t, f), BF16), jax.ShapeDtypeStruct((t, f), BF16),
                   jax.ShapeDtypeStruct((t, dm), BF16), jax.ShapeDtypeStruct((1, dm), F32)],
        in_specs=[_tok(tm, dm), _tok(tm, dm), _tok(tm, f), _tok(tm, f), _full((1, dm)), ANY, ANY, ANY],
        out_specs=[_tok(tm, dm), _tok(tm, dm), _tok(tm, f), _tok(tm, f), _tok(tm, dm), _full((1, dm))],
        scratch=[pltpu.VMEM((f, dm), BF16), pltpu.VMEM((f, dm), BF16), pltpu.VMEM((f, dm), BF16),
                 pltpu.SemaphoreType.DMA((3 * NDEV,))])


def _t5_buckets(rel):
    nb = N_BUCKETS // 2
    ret = jnp.where(rel > 0, nb, 0)
    n = jnp.abs(rel)
    max_exact = nb // 2
    nf = jnp.maximum(n, 1).astype(jnp.float32)
    large = max_exact + (jnp.log(nf / max_exact) / math.log(MAX_DISTANCE / max_exact)
                         * (nb - max_exact)).astype(jnp.int32)
    large = jnp.minimum(large, nb - 1)
    return ret + jnp.where(n < max_exact, n, large)


def _bucket_table():
    qi = jnp.arange(CHUNK, dtype=jnp.int32)[:, None]
    kj = jnp.arange(3 * CHUNK, dtype=jnp.int32)[None, :]
    rel = kj - CHUNK - qi
    return jnp.where(jnp.abs(rel) <= CHUNK, _t5_buckets(rel), -1)


def _bias_table(rel_bias_t, buckets):
    nh = rel_bias_t.shape[0]

    def body(rb_ref, bk_ref, o_ref):
        bk = bk_ref[...]
        for h in range(nh):
            acc = jnp.where(bk < 0, NEG, 0.0).astype(F32)
            for b in range(N_BUCKETS):
                acc = jnp.where(bk == b, rb_ref[h, b] * LOG2E, acc)
            o_ref[h] = acc

    return dict(
        body=body, grid=(1,), name="bias_table", args=[rel_bias_t, buckets],
        out_shape=[jax.ShapeDtypeStruct((nh,) + buckets.shape, F32)],
        in_specs=[pl.BlockSpec(memory_space=pltpu.SMEM), _full(buckets.shape)],
        out_specs=[_full((nh,) + buckets.shape)])


def _rel_bias_grad(dbias, buckets):
    nh = dbias.shape[0]

    def body(db_ref, bk_ref, o_ref):
        bk = bk_ref[...]
        lane = lax.broadcasted_iota(jnp.int32, (1, 128), 1)
        for h in range(nh):
            d = db_ref[h]
            row = jnp.zeros((1, 128), F32)
            for b in range(N_BUCKETS):
                s = jnp.sum(jnp.sum(jnp.where(bk == b, d, 0.0), axis=1, keepdims=True), axis=0, keepdims=True)
                row = jnp.where(lane == b, s, row)
            o_ref[h:h + 1, :] = row

    return pl.pallas_call(
        body, out_shape=jax.ShapeDtypeStruct((nh, 128), F32),
        in_specs=[pl.BlockSpec(memory_space=pltpu.VMEM), pl.BlockSpec(memory_space=pltpu.VMEM)],
        out_specs=pl.BlockSpec(memory_space=pltpu.VMEM), compiler_params=_cp(0), name="rel_bias_grad")(dbias, buckets)


def _half_masks():
    lane = lax.broadcasted_iota(jnp.int32, (CHUNK, 128), 1)
    return lane < HEAD_DIM, lane >= HEAD_DIM


def _kv_low(ref, starts, hk, lo):
    kt = (hk // 2) * 128
    out = []
    for jj in range(3):
        blk = ref[pl.ds(starts[jj], CHUNK), kt:kt + 128]
        if hk % 2 == 1:
            blk = pltpu.roll(blk, HEAD_DIM, 1)
        out.append(jnp.where(lo, blk, jnp.zeros_like(blk)))
    return out


def _stack_heads(tile_a, tile_b):
    return jnp.concatenate([tile_a, pltpu.roll(tile_a, HEAD_DIM, 1), tile_b, pltpu.roll(tile_b, HEAD_DIM, 1)], axis=0)


def _unstack_heads(o4):
    return (o4[0:CHUNK] + pltpu.roll(o4[CHUNK:2 * CHUNK], HEAD_DIM, 1),
            o4[2 * CHUNK:3 * CHUNK] + pltpu.roll(o4[3 * CHUNK:], HEAD_DIM, 1))


ATT_SLAB = 32


def _softmax_slab(s_scr, hk, g, r0, bias_ref, sink_ref, n, nblk):
    scale = HEAD_DIM ** -0.5 * LOG2E
    h = (N_HEADS // N_KV) * hk + g
    s = []
    for jj in range(3):
        sj = (s_scr[hk, jj, pl.ds(g * CHUNK + r0, ATT_SLAB), :] * scale
              + bias_ref[h, pl.ds(r0, ATT_SLAB), jj * CHUNK:(jj + 1) * CHUNK])
        if jj == 0:
            sj = jnp.where(n > 0, sj, NEG)
        if jj == 2:
            sj = jnp.where(n < nblk - 1, sj, NEG)
        s.append(sj)
    sink = sink_ref[h] * LOG2E
    m = jnp.maximum(jnp.max(jnp.maximum(jnp.maximum(s[0], s[1]), s[2]), axis=-1, keepdims=True), sink)
    e = [jnp.exp2(sj - m) for sj in s]
    es = jnp.exp2(sink - m)
    inv = 1.0 / (jnp.sum(e[0] + e[1] + e[2], axis=-1, keepdims=True) + es)
    return [ej * inv for ej in e], es * inv


def _key_block_starts(n, nblk):
    return [pl.multiple_of(jnp.clip(n - 1 + jj, 0, nblk - 1) * CHUNK, CHUNK) for jj in range(3)]


def _attn_fwd(qkv, x2, bias, sink, gath):
    t, dm = x2.shape
    nblk = t // CHUNK
    kvw = N_KV * HEAD_DIM
    kcb, vcb = dm // kvw, dm // kvw + 1
    slab = (N_KV, 3, 4 * CHUNK, CHUNK)

    def body(q_ref, k_ref, v_ref, x2_ref, bias_ref, sink_ref, gath_ref, x3_ref, att_ref, p_ref, ps_ref,
             wbuf, s_scr, sems):
        n = pl.program_id(0)

        @pl.when(n == 0)
        def _():
            _load_weight(gath_ref, wbuf, sems)
        lo, _ = _half_masks()
        lane_s = lax.broadcasted_iota(jnp.int32, (ATT_SLAB, 128), 1)
        starts = _key_block_starts(n, nblk)
        tiles = []
        for hk in range(N_KV):
            c0 = (2 * hk) * 128
            k_lo = _kv_low(k_ref, starts, hk, lo)
            v_lo = _kv_low(v_ref, starts, hk, lo)
            q4 = _stack_heads(q_ref[:, c0:c0 + 128], q_ref[:, c0 + 128:c0 + 256])
            for jj in range(3):
                s_scr[hk, jj] = _dot(q4, k_lo[jj], NT)
            for g in range(4):
                h = 4 * hk + g
                for r0 in range(0, CHUNK, ATT_SLAB):
                    p, ps = _softmax_slab(s_scr, hk, g, r0, bias_ref, sink_ref, n, nblk)
                    for jj in range(3):
                        p_ref[hk, jj, g * CHUNK + r0:g * CHUNK + r0 + ATT_SLAB, :] = p[jj].astype(BF16)
                    rest = jnp.zeros((ATT_SLAB, 128), F32) if h == 0 else ps_ref[r0:r0 + ATT_SLAB, :]
                    ps_ref[r0:r0 + ATT_SLAB, :] = jnp.where(lane_s == h, ps, rest)
            o4 = _dot(p_ref[hk, 0], v_lo[0], NN) + _dot(p_ref[hk, 1], v_lo[1], NN) + _dot(p_ref[hk, 2], v_lo[2], NN)
            tiles += list(_unstack_heads(o4))
        att = jnp.concatenate(tiles, axis=1).astype(BF16)
        att_ref[...] = att
        x3_ref[...] = x2_ref[...] + _dot(att, wbuf[...], NN)

    blk = pl.BlockSpec((CHUNK, dm), lambda n: (n, 0))
    return dict(
        body=body, grid=(nblk,), name="attn_fwd", args=[qkv, qkv, qkv, x2, bias, sink, gath],
        out_shape=[jax.ShapeDtypeStruct((t, dm), F32), jax.ShapeDtypeStruct((t, dm), BF16),
                   jax.ShapeDtypeStruct((nblk,) + slab, BF16), jax.ShapeDtypeStruct((t, 128), F32)],
        in_specs=[blk, pl.BlockSpec((t, kvw), lambda n: (0, kcb)), pl.BlockSpec((t, kvw), lambda n: (0, vcb)), blk,
                  _full(bias.shape), pl.BlockSpec(memory_space=pltpu.SMEM), ANY],
        out_specs=[blk, blk, pl.BlockSpec((None,) + slab, lambda n: (n, 0, 0, 0, 0)),
                   pl.BlockSpec((CHUNK, 128), lambda n: (n, 0))],
        scratch=[pltpu.VMEM((gath.shape[1] * NDEV, dm), BF16), pltpu.VMEM(slab, F32),
                 pltpu.SemaphoreType.DMA((NDEV,))])


def _attn_bwd(qkv, att, probs, sink_probs, dx3, bias_shape, gath):
    t, dm = dx3.shape
    nblk = t // CHUNK
    kvw = N_KV * HEAD_DIM
    kcb, vcb = dm // kvw, dm // kvw + 1
    scale = HEAD_DIM ** -0.5
    slab = (N_KV, 3, 4 * CHUNK, CHUNK)

    def body(q_ref, k_ref, v_ref, att_ref, p_ref, ps_ref, dx_ref, gath_ref,
             dq_ref, dkb_ref, dvb_ref, dbias_ref, dsink_ref,
             wbuf, dp_scr, ds_scr, prod_scr, dsum_scr, dk_ref, dv_ref, sems):
        n = pl.program_id(0)

        @pl.when(n == 0)
        def _():
            _load_weight(gath_ref, wbuf, sems)
            dk_ref[...] = jnp.zeros_like(dk_ref)
            dv_ref[...] = jnp.zeros_like(dv_ref)
            dbias_ref[...] = jnp.zeros_like(dbias_ref)
            dsink_ref[...] = jnp.zeros_like(dsink_ref)
        lo, hi = _half_masks()
        lane_s = lax.broadcasted_iota(jnp.int32, (ATT_SLAB, 128), 1)
        starts = _key_block_starts(n, nblk)
        dout = _dot(dx_ref[...].astype(BF16), wbuf[...], NT)
        prod_scr[...] = dout * att_ref[...].astype(F32)
        doutb = dout.astype(BF16)
        dq_tiles = []
        for hk in range(N_KV):
            kt = (hk // 2) * 128
            c0 = (2 * hk) * 128
            k_lo = _kv_low(k_ref, starts, hk, lo)
            v_lo = _kv_low(v_ref, starts, hk, lo)
            q4 = _stack_heads(q_ref[:, c0:c0 + 128], q_ref[:, c0 + 128:c0 + 256])
            do4 = _stack_heads(doutb[:, c0:c0 + 128], doutb[:, c0 + 128:c0 + 256])
            for jj in range(3):
                dp_scr[hk, jj] = _dot(do4, v_lo[jj], NT)
            for g in range(4):
                h = 4 * hk + g
                for r0 in range(0, CHUNK, ATT_SLAB):
                    rows = slice(g * CHUNK + r0, g * CHUNK + r0 + ATT_SLAB)
                    pt = prod_scr[r0:r0 + ATT_SLAB, c0 + (g // 2) * 128:c0 + (g // 2 + 1) * 128]
                    msk = lane_s < HEAD_DIM if g % 2 == 0 else lane_s >= HEAD_DIM
                    dsum = jnp.sum(jnp.where(msk, pt, 0.0), axis=-1, keepdims=True)
                    rest = jnp.zeros((ATT_SLAB, 128), F32) if h == 0 else dsum_scr[r0:r0 + ATT_SLAB, :]
                    dsum_scr[r0:r0 + ATT_SLAB, :] = jnp.where(lane_s == h, dsum, rest)
                    for jj in range(3):
                        ds = p_ref[hk, jj, rows, :].astype(F32) * (dp_scr[hk, jj, rows, :] - dsum)
                        dbias_ref[h, r0:r0 + ATT_SLAB, jj * CHUNK:(jj + 1) * CHUNK] += ds
                        ds_scr[hk, jj, rows, :] = ds.astype(BF16)
            dq4 = jnp.zeros((4 * CHUNK, 128), F32)
            for jj in range(3):
                ds4 = ds_scr[hk, jj]
                dq4 = dq4 + _dot(ds4, k_lo[jj], NN) * scale
                dkj = _dot(ds4, q4, TN) * scale
                dvj = _dot(p_ref[hk, jj], do4, TN)
                if hk % 2 == 1:
                    dkj, dvj = pltpu.roll(dkj, HEAD_DIM, 1), pltpu.roll(dvj, HEAD_DIM, 1)
                keep = lo if hk % 2 == 0 else hi
                dk_ref[pl.ds(starts[jj], CHUNK), kt:kt + 128] += jnp.where(keep, dkj, 0.0)
                dv_ref[pl.ds(starts[jj], CHUNK), kt:kt + 128] += jnp.where(keep, dvj, 0.0)
            dq_tiles += list(_unstack_heads(dq4))
        dq_ref[...] = jnp.concatenate(dq_tiles, axis=1).astype(BF16)
        dsink_ref[...] -= jnp.sum(ps_ref[...] * dsum_scr[...], axis=0, keepdims=True)

        @pl.when(n == nblk - 1)
        def _():
            dkb_ref[...] = dk_ref[...].astype(BF16)
            dvb_ref[...] = dv_ref[...].astype(BF16)

    blk = pl.BlockSpec((CHUNK, dm), lambda n: (n, 0))
    return dict(
        body=body, grid=(nblk,), name="attn_bwd", args=[qkv, qkv, qkv, att, probs, sink_probs, dx3, gath],
        out_shape=[jax.ShapeDtypeStruct((t, dm), BF16), jax.ShapeDtypeStruct((t, kvw), BF16),
                   jax.ShapeDtypeStruct((t, kvw), BF16), jax.ShapeDtypeStruct(bias_shape, F32),
                   jax.ShapeDtypeStruct((1, 128), F32)],
        in_specs=[blk, pl.BlockSpec((t, kvw), lambda n: (0, kcb)), pl.BlockSpec((t, kvw), lambda n: (0, vcb)),
                  blk, pl.BlockSpec((None,) + slab, lambda n: (n, 0, 0, 0, 0)),
                  pl.BlockSpec((CHUNK, 128), lambda n: (n, 0)), blk, ANY],
        out_specs=[blk, _full((t, kvw)), _full((t, kvw)), _full(bias_shape), _full((1, 128))],
        scratch=[pltpu.VMEM((gath.shape[1] * NDEV, dm), BF16), pltpu.VMEM(slab, F32), pltpu.VMEM(slab, BF16),
                 pltpu.VMEM((CHUNK, dm), F32), pltpu.VMEM((CHUNK, 128), F32),
                 pltpu.VMEM((t, kvw), F32), pltpu.VMEM((t, kvw), F32), pltpu.SemaphoreType.DMA((NDEV,))])


def _finish_weight(recvs, w, m, v, name):
    nl, r, dm = w.shape
    assert nl == len(recvs) and all(rc.shape[1:] == (r, dm) for rc in recvs)
    td = dm // 2
    wspec = pl.BlockSpec((None, r, td), lambda l, j: (l, 0, j))

    def body(*refs):
        r_refs = refs[:nl]
        w_ref, m_ref, v_ref, g_ref, d_ref, nm_ref, nv_ref = refs[nl:]
        layer = pl.program_id(0)
        for li in range(nl):
            @pl.when(layer == li)
            def _():
                g = r_refs[li][0].astype(F32)
                for d in range(1, recvs[li].shape[0]):
                    g = g + r_refs[li][d].astype(F32)
                delta, nm, nv = _adamw_math(w_ref[...], g, m_ref[...], v_ref[...])
                g_ref[...] = g
                d_ref[...] = delta
                nm_ref[...] = nm
                nv_ref[...] = nv

    o = jax.ShapeDtypeStruct(w.shape, F32)
    return dict(
        body=body, grid=(nl, 2), name=name, args=[*recvs, w, m, v], out_shape=[o, o, o, o],
        in_specs=[pl.BlockSpec((rc.shape[0], r, td), lambda l, j: (0, 0, j)) for rc in recvs] + [wspec] * 3,
        out_specs=[wspec] * 4)


def _adamw_small(ws, ms, vs, slots, late_slots, loss_slots, name):
    n = len(ws)

    def total(ref):
        acc = ref[0].astype(F32)
        for d in range(1, NDEV):
            acc = acc + ref[d].astype(F32)
        return acc

    def body(*refs):
        ins, outs = refs[:4 * n + 2], refs[4 * n + 2:]
        for i in range(n):
            w_ref, m_ref, v_ref, s_ref = ins[4 * i:4 * i + 4]
            g_ref, d_ref, nm_ref, nv_ref = outs[4 * i:4 * i + 4]
            g_ref[...] = total(s_ref)
            if i == 0:
                g_ref[0:1, :] = total(ins[4 * n])
            d_ref[...], nm_ref[...], nv_ref[...] = _adamw_math(w_ref[...], g_ref[...], m_ref[...], v_ref[...])
        outs[4 * n][...] = total(ins[4 * n + 1])

    args, out_shape = [], []
    for w, m, v, s in zip(ws, ms, vs, slots):
        args += [w, m, v, s]
        out_shape += [jax.ShapeDtypeStruct(w.shape, F32)] * 4
    args += [late_slots, loss_slots]
    out_shape.append(jax.ShapeDtypeStruct((1, 1), F32))
    out = pl.pallas_call(
        body, grid=(1,), out_shape=tuple(out_shape), in_specs=[_full(a.shape) for a in args],
        out_specs=tuple(_full(o.shape) for o in out_shape), compiler_params=_cp(), name=name)(*args)
    return [tuple(out[4 * i:4 * i + 4]) for i in range(n)], out[4 * n]


def kernel(x, norm_mix, norm_ffn, even_w_in, even_v_ln_g, even_v_ln_b, even_w_spatial, even_b_spatial, even_conv_w, even_w_out, attn_w_qkv, attn_sink, rel_bias, attn_w_out, ffn_w_gate, ffn_w_up, ffn_w_down, final_norm, loss_target, m_norm_mix, m_norm_ffn, m_even_w_in, m_even_v_ln_g, m_even_v_ln_b, m_even_w_spatial, m_even_b_spatial, m_even_conv_w, m_even_w_out, m_attn_w_qkv, m_attn_sink, m_rel_bias, m_attn_w_out, m_ffn_w_gate, m_ffn_w_up, m_ffn_w_down, m_final_norm, v_norm_mix, v_norm_ffn, v_even_w_in, v_even_v_ln_g, v_even_v_ln_b, v_even_w_spatial, v_even_b_spatial, v_even_conv_w, v_even_w_out, v_attn_w_qkv, v_attn_sink, v_rel_bias, v_attn_w_out, v_ffn_w_gate, v_ffn_w_up, v_ffn_w_down, v_final_norm):
    t, dm = x.shape[1], x.shape[2]
    aw = even_v_ln_g.shape[1]
    bw = even_conv_w.shape[2] * NDEV
    gd = aw // A_GROUPS
    tm = min(512, t // 2)
    tmf = min(256, t // 2)
    me = _my_index()
    row = lambda a: a.reshape(1, -1)

    colT = lambda w: w.T.astype(BF16)
    sh = dict(winT=colT(even_w_in[0]), wqkvT=colT(attn_w_qkv[0]), wgT0=colT(ffn_w_gate[0]), wuT0=colT(ffn_w_up[0]),
              wgT1=colT(ffn_w_gate[1]), wuT1=colT(ffn_w_up[1]), woe=even_w_out[0].astype(BF16),
              woa=attn_w_out[0].astype(BF16), wd0=ffn_w_down[0].astype(BF16), wd1=ffn_w_down[1].astype(BF16))
    gather = lambda names: _GatherCarry([sh[n] for n in names])

    in_full = lambda a: lax.dynamic_update_slice(jnp.zeros((3, bw), F32), a[0], (0, me * (bw // NDEV)))

    x0 = x[0]
    wsp_b = even_w_spatial[0].astype(BF16)
    bspb = jnp.broadcast_to(even_b_spatial[0][:, :, None], (A_GROUPS, CHUNK, gd))
    buckets = _bucket_table()
    sink = attn_sink[0]

    (bias,), ((g_winT,), (cw_slots,)) = _call(
        _bias_table(rel_bias.T, buckets), [gather(["winT"]), _BroadcastCarry([in_full(even_conv_w)])])
    cw_full = jnp.sum(cw_slots, axis=0)
    (proj, h0b), (g_woe, g_wgT0) = _call(_norm_proj(x0, row(norm_mix[0]), g_winT, F32, "in_proj", tm),
                                         gather(["woe", "wgT0"]))
    (x1, yb), (g_wuT0,) = _call(_even_core_fwd(proj, x0, even_v_ln_g, even_v_ln_b, wsp_b, bspb, cw_full, g_woe, tm),
                                gather(["wuT0"]))
    (gate0, up0, act0), (g_wd0,) = _call(_ffn_up(x1, row(norm_ffn[0]), g_wgT0, g_wuT0, "ffn_up0", tmf), gather(["wd0"]))
    (x2,), (g_wqkvT,) = _call(_ffn_down(x1, act0, g_wd0, "ffn_down0", tm), gather(["wqkvT"]))
    (qkv, h2b), (g_woa,) = _call(_norm_proj(x2, row(norm_mix[1]), g_wqkvT, BF16, "qkv_proj", tm), gather(["woa"]))
    (x3, attb, probs, sink_probs), (g_wgT1, g_wuT1) = _call(
        _attn_fwd(qkv, x2, bias, sink, g_woa), gather(["wgT1", "wuT1"]))
    (gate1, up1, act1), (g_wd1,) = _call(_ffn_up(x3, row(norm_ffn[1]), g_wgT1, g_wuT1, "ffn_up1", tmf), gather(["wd1"]))
    (loss_part, dx4, dx4b, d_final), _ = _call(
        _ffn_down_loss(x3, act1, g_wd1, loss_target[0], row(final_norm), "ffn_down1_loss", tm))

    (dx3, dx3b, dg1, du1, h3b, d_nffn1), _ = _call(
        _ffn_bwd(dx4, x3, gate1, up1, row(norm_ffn[1]), g_wgT1, g_wuT1, g_wd1, "ffn_bwd1", tmf))
    (p_wgT1,), _ = _call(_wgrad(dg1, h3b, "wgrad_gate1"))
    (p_wuT1,), _ = _call(_wgrad(du1, h3b, "wgrad_up1"))
    (p_wd1,), ((a_wgT1,), (a_wuT1,)) = _call(
        _wgrad(act1, dx4b, "wgrad_down1"), [_PairCarry(p_wgT1), _PairCarry(p_wuT1)])
    (dq, dk, dv, dbias, dsink), ((r_wgT1,), (a_wd1,)) = _call(
        _attn_bwd(qkv, attb, probs, sink_probs, dx3, bias.shape, g_woa),
        [_ChipSumCarry(p_wgT1, a_wgT1), _PairCarry(p_wd1)])
    (p_woa,), _ = _call(_wgrad(attb, dx3b, "wgrad_attn_out"))
    d_relb = _rel_bias_grad(dbias, buckets)[:, 0:N_BUCKETS].T
    (dx2, dx2b, d_nmix1), (r_wuT1,) = _call(
        _proj_bwd_norm([dq, dk, dv], x2, row(norm_mix[1]), dx3, g_wqkvT, "qkv_bwd", tm),
        _ChipSumCarry(p_wuT1, a_wuT1))
    (p_wqkvT,), _ = _call(_wgrad([dq, dk, dv], h2b, "wgrad_qkv"))
    (dx1, dx1b, dg0, du0, h1b, d_nffn0), ((r_wd1,), (r_woa, r_wqkvT)) = _call(
        _ffn_bwd(dx2, x1, gate0, up0, row(norm_ffn[0]), g_wgT0, g_wuT0, g_wd0, "ffn_bwd0", tmf),
        [_ChipSumCarry(p_wd1, a_wd1), _GradCarry([p_woa, p_wqkvT])])
    (p_wgT0,), _ = _call(_wgrad(dg0, h1b, "wgrad_gate0"))
    (p_wuT0,), (a_wgT0,) = _call(_wgrad(du0, h1b, "wgrad_up0"), _PairCarry(p_wgT0))
    (p_wd0,), ((r_wgT0,), (a_wuT0,)) = _call(
        _wgrad(act0, dx2b, "wgrad_down0"), [_ChipSumCarry(p_wgT0, a_wgT0), _PairCarry(p_wuT0)])
    (dproj, d_lng, d_lnb, d_wsp, d_bsp3, d_cw), ((r_wuT0,), (a_wd0,)) = _call(
        _even_core_bwd(proj, dx1, even_v_ln_g, even_v_ln_b, wsp_b, bspb, cw_full, g_woe, tm),
        [_ChipSumCarry(p_wuT0, a_wuT0), _PairCarry(p_wd0)])
    small_names = ["norm_mix", "norm_ffn", "even_v_ln_g", "even_v_ln_b", "even_w_spatial", "even_b_spatial",
                   "even_conv_w", "attn_sink", "rel_bias", "final_norm"]
    small_parts = [jnp.concatenate([jnp.zeros_like(d_nmix1), d_nmix1]), jnp.concatenate([d_nffn0, d_nffn1]),
                   d_lng, d_lnb, d_wsp[None].astype(BF16), jnp.sum(d_bsp3, axis=-1)[None], d_cw,
                   dsink[:, 0:N_HEADS], d_relb, d_final, loss_part]
    (p_winT,), (r_wd0,) = _call(_wgrad(dproj, h0b, "wgrad_in"), _ChipSumCarry(p_wd0, a_wd0))
    (p_woe,), ((a_winT,), small_slots) = _call(
        _wgrad(yb, dx1b, "wgrad_even_out"), [_PairCarry(p_winT), _BroadcastCarry(small_parts)])
    (dx0, _, d_nmix0), ((r_winT,), (r_woe,)) = _call(
        _proj_bwd_norm([dproj], x0, row(norm_mix[0]), dx1, g_winT, "in_proj_bwd", tm),
        [_ChipSumCarry(p_winT, a_winT), _GradCarry([p_woe])])

    grads = {}

    order = ["norm_mix", "norm_ffn", "even_w_in", "even_v_ln_g", "even_v_ln_b", "even_w_spatial", "even_b_spatial",
             "even_conv_w", "even_w_out", "attn_w_qkv", "attn_sink", "rel_bias", "attn_w_out", "ffn_w_gate",
             "ffn_w_up", "ffn_w_down", "final_norm"]
    ws = dict(norm_mix=norm_mix, norm_ffn=norm_ffn, even_w_in=even_w_in, even_v_ln_g=even_v_ln_g,
              even_v_ln_b=even_v_ln_b, even_w_spatial=even_w_spatial, even_b_spatial=even_b_spatial,
              even_conv_w=even_conv_w, even_w_out=even_w_out, attn_w_qkv=attn_w_qkv, attn_sink=attn_sink,
              rel_bias=rel_bias, attn_w_out=attn_w_out, ffn_w_gate=ffn_w_gate, ffn_w_up=ffn_w_up,
              ffn_w_down=ffn_w_down, final_norm=final_norm)
    ms = dict(norm_mix=m_norm_mix, norm_ffn=m_norm_ffn, even_w_in=m_even_w_in, even_v_ln_g=m_even_v_ln_g,
              even_v_ln_b=m_even_v_ln_b, even_w_spatial=m_even_w_spatial, even_b_spatial=m_even_b_spatial,
              even_conv_w=m_even_conv_w, even_w_out=m_even_w_out, attn_w_qkv=m_attn_w_qkv, attn_sink=m_attn_sink,
              rel_bias=m_rel_bias, attn_w_out=m_attn_w_out, ffn_w_gate=m_ffn_w_gate, ffn_w_up=m_ffn_w_up,
              ffn_w_down=m_ffn_w_down, final_norm=m_final_norm)
    vs = dict(norm_mix=v_norm_mix, norm_ffn=v_norm_ffn, even_w_in=v_even_w_in, even_v_ln_g=v_even_v_ln_g,
              even_v_ln_b=v_even_v_ln_b, even_w_spatial=v_even_w_spatial, even_b_spatial=v_even_b_spatial,
              even_conv_w=v_even_conv_w, even_w_out=v_even_w_out, attn_w_qkv=v_attn_w_qkv, attn_sink=v_attn_sink,
              rel_bias=v_rel_bias, attn_w_out=v_attn_w_out, ffn_w_gate=v_ffn_w_gate, ffn_w_up=v_ffn_w_up,
              ffn_w_down=v_ffn_w_down, final_norm=v_final_norm)
    big = dict(ffn_w_gate=([r_wgT0, r_wgT1], True), even_w_in=([r_winT], True), even_w_out=([r_woe], False),
               attn_w_qkv=([r_wqkvT], True), attn_w_out=([r_woa], False), ffn_w_up=([r_wuT0, r_wuT1], True),
               ffn_w_down=([r_wd0, r_wd1], False))
    delta, new_m, new_v = {}, {}, {}
    late_slots = None
    for n, (recvs, transposed) in big.items():
        lay = (lambda a: jnp.swapaxes(a, 1, 2)) if transposed else (lambda a: a)
        spec = _finish_weight(recvs, lay(ws[n]), lay(ms[n]), lay(vs[n]), "finish_" + n)
        if late_slots is None:
            outs, (late_slots,) = _call(spec, _BroadcastCarry([d_nmix0]))
        else:
            outs, _ = _call(spec)
        grads[n], delta[n], new_m[n], new_v[n] = [lay(o) for o in outs]
    shaped = lambda n, a: in_full(a) if n == "even_conv_w" else (a.reshape(1, dm) if n == "final_norm" else a)
    pick = lambda dct: [shaped(n, dct[n]) for n in small_names]
    results, loss11 = _adamw_small(pick(ws), pick(ms), pick(vs), small_slots[:-1], late_slots, small_slots[-1],
                                   "adamw_small")
    mine = lambda a: lax.dynamic_slice(a, (0, me * (bw // NDEV)), (3, bw // NDEV))[None]
    for n, res in zip(small_names, results):
        for dst, a in zip((grads, delta, new_m, new_v), res):
            dst[n] = mine(a) if n == "even_conv_w" else (a.reshape(dm) if n == "final_norm" else a)
    loss = loss11[0, 0]
    return (loss, dx0[None], *[grads[n] for n in order], *[delta[n] for n in order],
            *[new_m[n] for n in order], *[new_v[n] for n in order])
```

```python
import math

import jax
import jax.numpy as jnp
import numpy as np
from jax import lax
from jax.experimental import pallas as pl
from jax.experimental.pallas import tpu as pltpu

F32, BF16 = jnp.float32, jnp.bfloat16
NDEV = 8
EPS = 1e-6
CHUNK = 128
A_GROUPS = 4
N_HEADS, N_KV, HEAD_DIM = 16, 4, 64
N_BUCKETS, MAX_DISTANCE = 32, 128
NEG = -1e30
LOG2E = 1.4426950408889634
ADAM_LR, ADAM_B1, ADAM_B2, ADAM_EPS, ADAM_WD, ADAM_STEP = 0.001, 0.9, 0.999, 1e-08, 0.01, 10
VMEM_LIMIT = 56 * 1024 * 1024
MESH = pl.DeviceIdType.MESH
NT = (((1,), (1,)), ((), ()))
NN = (((1,), (0,)), ((), ()))
TN = (((0,), (0,)), ((), ()))
ANY = pl.BlockSpec(memory_space=pl.ANY)


def _cp(n_grid=1):
    return pltpu.CompilerParams(dimension_semantics=("arbitrary",) * n_grid, vmem_limit_bytes=VMEM_LIMIT)


def _dot(a, b, dims):
    return lax.dot_general(a, b, dims, preferred_element_type=F32)


def _my_index():
    return 4 * lax.axis_index("x") + 2 * lax.axis_index("y") + lax.axis_index("c")


def _peer(k):
    x, y, c = lax.axis_index("x"), lax.axis_index("y"), lax.axis_index("c")
    px = 1 - x if k & 4 else x
    py = 1 - y if k & 2 else y
    pc = 1 - c if k & 1 else c
    return (px, py, pc)


def _load_weight(gath_ref, wbuf, sems):
    rows = gath_ref.shape[1]
    cps = [pltpu.make_async_copy(gath_ref.at[d], wbuf.at[pl.ds(d * rows, rows), :], sems.at[d]) for d in range(NDEV)]
    for c in cps:
        c.start()
    for c in cps:
        c.wait()


class _GatherCarry:
    def __init__(self, pieces):
        self.inputs = list(pieces)
        self.n = len(pieces)
        self.out_shape = [jax.ShapeDtypeStruct((NDEV,) + p.shape, p.dtype) for p in pieces]
        self.scratch = [pltpu.SemaphoreType.DMA((7 * self.n,)), pltpu.SemaphoreType.DMA((7 * self.n,)),
                        pltpu.SemaphoreType.DMA((self.n,))]

    def _ctx(self):
        x, y, c = lax.axis_index("x"), lax.axis_index("y"), lax.axis_index("c")
        chips = [(1 - x, y), (x, 1 - y), (1 - x, 1 - y)]
        return (x, y, c), (x, y, 1 - c), chips, c

    def _copy(self, k, j, block, to, ins, outs, sems, src=None):
        send_sems, recv_sems, _ = sems
        slot = outs[j].at[4 * block[0] + 2 * block[1] + block[2]]
        return pltpu.make_async_remote_copy(
            src_ref=slot if src is None else src, dst_ref=slot, send_sem=send_sems.at[k * self.n + j],
            recv_sem=recv_sems.at[k * self.n + j], device_id=to, device_id_type=MESH)

    def start(self, ins, outs, sems):
        me, sibling, chips, c = self._ctx()
        for j in range(self.n):
            pltpu.make_async_copy(ins[j], outs[j].at[4 * me[0] + 2 * me[1] + me[2]], sems[2].at[j]).start()
            self._copy(0, j, me, sibling, ins, outs, sems, src=ins[j]).start()
            for q, chip in enumerate(chips):
                self._copy(1 + q, j, me, (*chip, c), ins, outs, sems, src=ins[j]).start()

    def mid(self, ins, outs, sems):
        me, sibling, chips, c = self._ctx()
        for q, chip in enumerate(chips):
            for j in range(self.n):
                self._copy(1 + q, j, (*chip, c), me, ins, outs, sems).wait_recv()
                self._copy(4 + q, j, (*chip, c), sibling, ins, outs, sems).start()

    def finish(self, ins, outs, sems):
        me, sibling, chips, c = self._ctx()
        for j in range(self.n):
            self._copy(0, j, sibling, me, ins, outs, sems).wait_recv()
            for q, chip in enumerate(chips):
                self._copy(4 + q, j, (*chip, 1 - c), me, ins, outs, sems).wait_recv()
        for j in range(self.n):
            self._copy(0, j, me, sibling, ins, outs, sems, src=ins[j]).wait_send()
            for q, chip in enumerate(chips):
                self._copy(1 + q, j, me, (*chip, c), ins, outs, sems, src=ins[j]).wait_send()
                self._copy(4 + q, j, (*chip, c), sibling, ins, outs, sems).wait_send()
            pltpu.make_async_copy(ins[j], outs[j].at[0], sems[2].at[j]).wait()


class _GradCarry:
    def __init__(self, pieces):
        self.inputs = list(pieces)
        self.n = len(pieces)
        self.rows = [p.shape[0] // NDEV for p in pieces]
        self.out_shape = [jax.ShapeDtypeStruct((NDEV, r, p.shape[1]), p.dtype) for p, r in zip(pieces, self.rows)]
        self.scratch = [pltpu.SemaphoreType.DMA((7 * self.n,)), pltpu.SemaphoreType.DMA((7 * self.n,)),
                        pltpu.SemaphoreType.DMA((self.n,))]

    def _copies(self, ins, outs, sems):
        me = _my_index()
        local, remote = [], []
        for j in range(self.n):
            r = self.rows[j]
            local.append(pltpu.make_async_copy(ins[j].at[pl.ds(pl.multiple_of(me * r, 16), r), :], outs[j].at[me],
                                               sems[2].at[j]))
            for k in range(1, NDEV):
                peer = _peer(k)
                pidx = 4 * peer[0] + 2 * peer[1] + peer[2]
                remote.append(pltpu.make_async_remote_copy(
                    src_ref=ins[j].at[pl.ds(pl.multiple_of(pidx * r, 16), r), :], dst_ref=outs[j].at[me],
                    send_sem=sems[0].at[(k - 1) * self.n + j], recv_sem=sems[1].at[(k - 1) * self.n + j],
                    device_id=peer, device_id_type=MESH))
        return local, remote

    def start(self, ins, outs, sems):
        local, remote = self._copies(ins, outs, sems)
        for cp in local + remote:
            cp.start()

    def mid(self, ins, outs, sems):
        pass

    def finish(self, ins, outs, sems):
        local, remote = self._copies(ins, outs, sems)
        for cp in remote + local:
            cp.wait()


class _BroadcastCarry:
    def __init__(self, parts):
        self.inputs = list(parts)
        self.n = len(self.inputs)
        self.out_shape = [jax.ShapeDtypeStruct((NDEV,) + p.shape, p.dtype) for p in self.inputs]
        self.scratch = [pltpu.SemaphoreType.DMA((7 * self.n,)), pltpu.SemaphoreType.DMA((7 * self.n,)),
                        pltpu.SemaphoreType.DMA((self.n,))]

    def _copies(self, ins, outs, sems):
        me = _my_index()
        cps = []
        for j in range(self.n):
            cps.append(pltpu.make_async_copy(ins[j], outs[j].at[me], sems[2].at[j]))
            cps += [pltpu.make_async_remote_copy(
                src_ref=ins[j], dst_ref=outs[j].at[me], send_sem=sems[0].at[(k - 1) * self.n + j],
                recv_sem=sems[1].at[(k - 1) * self.n + j], device_id=_peer(k), device_id_type=MESH)
                for k in range(1, NDEV)]
        return cps

    def start(self, ins, outs, sems):
        for cp in self._copies(ins, outs, sems):
            cp.start()

    def mid(self, ins, outs, sems):
        pass

    def finish(self, ins, outs, sems):
        for cp in self._copies(ins, outs, sems):
            cp.wait()


class _PairCarry:
    def __init__(self, piece):
        self.inputs = [piece]
        self.r = piece.shape[0] // NDEV
        self.out_shape = [jax.ShapeDtypeStruct((4, self.r, piece.shape[1]), piece.dtype)]
        self.scratch = [pltpu.SemaphoreType.DMA((4,)), pltpu.SemaphoreType.DMA((4,))]

    def _copies(self, ins, outs, sems):
        x, y, c = lax.axis_index("x"), lax.axis_index("y"), lax.axis_index("c")
        return [pltpu.make_async_remote_copy(
            src_ref=ins[0].at[pl.ds(pl.multiple_of((2 * q + 1 - c) * self.r, 16), self.r), :], dst_ref=outs[0].at[q],
            send_sem=sems[0].at[q], recv_sem=sems[1].at[q], device_id=(x, y, 1 - c), device_id_type=MESH)
            for q in range(4)]

    def start(self, ins, outs, sems):
        for cp in self._copies(ins, outs, sems):
            cp.start()

    def mid(self, ins, outs, sems):
        pass

    def finish(self, ins, outs, sems):
        for cp in self._copies(ins, outs, sems):
            cp.wait()


class _ChipSumCarry:
    def __init__(self, piece, landed):
        self.inputs = [piece, landed]
        self.r, dm = piece.shape[0] // NDEV, piece.shape[1]
        self.out_shape = [jax.ShapeDtypeStruct((4, self.r, dm), piece.dtype)]
        self.scratch = [pltpu.VMEM((4, self.r, dm), piece.dtype), pltpu.VMEM((8, self.r, dm), piece.dtype),
                        pltpu.SemaphoreType.DMA((8,)), pltpu.SemaphoreType.DMA((3,)), pltpu.SemaphoreType.DMA((3,)),
                        pltpu.SemaphoreType.DMA(())]

    def _copies(self, outs, scr):
        sums, _, _, send_sems, recv_sems, local_sem = scr
        x, y, c = lax.axis_index("x"), lax.axis_index("y"), lax.axis_index("c")
        mine = 2 * x + y
        local = pltpu.make_async_copy(sums.at[mine], outs[0].at[mine], local_sem)
        remote = []
        for k in range(1, 4):
            px = 1 - x if k & 2 else x
            py = 1 - y if k & 1 else y
            remote.append(pltpu.make_async_remote_copy(
                src_ref=sums.at[2 * px + py], dst_ref=outs[0].at[mine], send_sem=send_sems.at[k - 1],
                recv_sem=recv_sems.at[k - 1], device_id=(px, py, c), device_id_type=MESH))
        return local, remote

    def start(self, ins, outs, scr):
        sums, stage, stage_sems = scr[0], scr[1], scr[2]
        c = lax.axis_index("c")
        loads = []
        for q in range(4):
            loads.append((
                pltpu.make_async_copy(ins[0].at[pl.ds(pl.multiple_of((2 * q + c) * self.r, 16), self.r), :],
                                      stage.at[2 * q], stage_sems.at[2 * q]),
                pltpu.make_async_copy(ins[1].at[q], stage.at[2 * q + 1], stage_sems.at[2 * q + 1])))
        for a, b in loads:
            a.start()
            b.start()
        for q, (a, b) in enumerate(loads):
            a.wait()
            b.wait()
            sums[q] = (stage[2 * q].astype(F32) + stage[2 * q + 1].astype(F32)).astype(sums.dtype)
        local, remote = self._copies(outs, scr)
        for cp in [local] + remote:
            cp.start()

    def mid(self, ins, outs, scr):
        pass

    def finish(self, ins, outs, scr):
        local, remote = self._copies(outs, scr)
        for cp in remote + [local]:
            cp.wait()


def _call(spec, carry=None):
    body, grid = spec["body"], spec["grid"]
    in_specs, out_specs, out_shape = list(spec["in_specs"]), list(spec["out_specs"]), list(spec["out_shape"])
    scratch, args = list(spec.get("scratch", [])), list(spec["args"])
    if carry is None:
        out = pl.pallas_call(body, grid=grid, in_specs=in_specs, out_specs=tuple(out_specs),
                             out_shape=tuple(out_shape), scratch_shapes=scratch, compiler_params=_cp(len(grid)),
                             name=spec["name"])(*args)
        return tuple(out), ()
    carries = list(carry) if isinstance(carry, (list, tuple)) else [carry]
    n_in, n_out, n_s = len(in_specs), len(out_specs), len(scratch)
    steps = int(np.prod(grid))

    def split(refs, counts):
        parts, o = [], 0
        for cnt in counts:
            parts.append(refs[o:o + cnt])
            o += cnt
        return parts

    c_in = [len(cr.inputs) for cr in carries]
    c_out = [len(cr.out_shape) for cr in carries]
    c_scr = [len(cr.scratch) for cr in carries]

    def wrapped(*refs):
        ins, cins, outs, couts, scr, cscr = split(refs, [n_in, sum(c_in), n_out, sum(c_out), n_s, sum(c_scr)])
        per = list(zip(carries, split(cins, c_in), split(couts, c_out), split(cscr, c_scr)))
        step = pl.program_id(0)
        for ax in range(1, len(grid)):
            step = step * grid[ax] + pl.program_id(ax)

        @pl.when(step == 0)
        def _():
            for cr, ci, co, cs in per:
                cr.start(ci, co, cs)
        if steps >= 3:
            @pl.when(step == steps - 2)
            def _():
                for cr, ci, co, cs in per:
                    cr.mid(ci, co, cs)
        body(*ins, *outs, *scr)

        @pl.when(step == steps - 1)
        def _():
            for cr, ci, co, cs in per:
                if steps < 3:
                    cr.mid(ci, co, cs)
                cr.finish(ci, co, cs)

    out = pl.pallas_call(
        wrapped, grid=grid, in_specs=in_specs + [ANY] * sum(c_in), out_specs=tuple(out_specs + [ANY] * sum(c_out)),
        out_shape=tuple(out_shape + [s for cr in carries for s in cr.out_shape]),
        scratch_shapes=scratch + [s for cr in carries for s in cr.scratch],
        compiler_params=_cp(len(grid)), name=spec["name"])(*args, *[a for cr in carries for a in cr.inputs])
    c_res = [tuple(p) for p in split(out[n_out:], c_out)]
    return tuple(out[:n_out]), (c_res if isinstance(carry, (list, tuple)) else c_res[0])


def _rms_fwd(x, gain):
    r = lax.rsqrt(jnp.mean(x * x, axis=-1, keepdims=True) + EPS)
    return x * r * gain, r


def _rms_bwd(dh, x, r, gain):
    a = dh * gain
    dx = r * a - x * (r * r * r) * jnp.mean(a * x, axis=-1, keepdims=True)
    dgain = jnp.sum(dh * (x * r), axis=0, keepdims=True)
    return dx, dgain


def _gelu(x):
    return 0.5 * x * (1.0 + lax.erf(x * 0.7071067811865476))


def _gelu_grad(x):
    return 0.5 * (1.0 + lax.erf(x * 0.7071067811865476)) + x * jnp.exp(-0.5 * x * x) * 0.3989422804014327


def _sigmoid(x):
    return 1.0 / (1.0 + jnp.exp(-x))


def _adamw_math(w, g, m, v):
    nm = ADAM_B1 * m + (1.0 - ADAM_B1) * g
    nv = ADAM_B2 * v + (1.0 - ADAM_B2) * (g * g)
    m_hat = nm / (1.0 - ADAM_B1 ** ADAM_STEP)
    v_hat = nv / (1.0 - ADAM_B2 ** ADAM_STEP)
    return -ADAM_LR * (m_hat / (jnp.sqrt(v_hat) + ADAM_EPS) + ADAM_WD * w), nm, nv


def _tok(tm, w):
    return pl.BlockSpec((tm, w), lambda i: (i, 0))


def _full(shape):
    return pl.BlockSpec(shape, lambda *i: (0,) * len(shape))


def _norm_proj(x, gain, gath, out_dtype, name, tm):
    t, dm = x.shape
    n = gath.shape[1] * NDEV

    def body(x_ref, g_ref, gath_ref, proj_ref, hb_ref, wbuf, sems):
        @pl.when(pl.program_id(0) == 0)
        def _():
            _load_weight(gath_ref, wbuf, sems)
        h, _ = _rms_fwd(x_ref[...], g_ref[...])
        hb = h.astype(BF16)
        hb_ref[...] = hb
        proj_ref[...] = _dot(hb, wbuf[...], NT).astype(out_dtype)

    return dict(
        body=body, grid=(t // tm,), name=name, args=[x, gain, gath],
        out_shape=[jax.ShapeDtypeStruct((t, n), out_dtype), jax.ShapeDtypeStruct((t, dm), BF16)],
        in_specs=[_tok(tm, dm), _full((1, dm)), ANY], out_specs=[_tok(tm, n), _tok(tm, dm)],
        scratch=[pltpu.VMEM((n, dm), BF16), pltpu.SemaphoreType.DMA((NDEV,))])


def _proj_bwd_norm(dys, x, gain, dres, gath, name, tm):
    t, dm = x.shape
    n = gath.shape[1] * NDEV
    widths = [d.shape[1] for d in dys]
    assert sum(widths) == n
    nd = len(dys)

    def body(*refs):
        dy_refs = refs[:nd]
        x_ref, g_ref, dres_ref, gath_ref, dx_ref, dxb_ref, dgain_ref, wbuf, sems = refs[nd:]

        @pl.when(pl.program_id(0) == 0)
        def _():
            _load_weight(gath_ref, wbuf, sems)
            dgain_ref[...] = jnp.zeros_like(dgain_ref)
        xv, gain_v = x_ref[...], g_ref[...]
        _, r = _rms_fwd(xv, gain_v)
        dh, c0 = None, 0
        for dy_ref, wd in zip(dy_refs, widths):
            part = _dot(dy_ref[...], wbuf[c0:c0 + wd, :], NN)
            dh = part if dh is None else dh + part
            c0 += wd
        dx, dgain = _rms_bwd(dh, xv, r, gain_v)
        dx = dres_ref[...] + dx
        dx_ref[...] = dx
        dxb_ref[...] = dx.astype(BF16)
        dgain_ref[...] += dgain

    return dict(
        body=body, grid=(t // tm,), name=name, args=[*dys, x, gain, dres, gath],
        out_shape=[jax.ShapeDtypeStruct((t, dm), F32), jax.ShapeDtypeStruct((t, dm), BF16),
                   jax.ShapeDtypeStruct((1, dm), F32)],
        in_specs=[_tok(tm, wd) for wd in widths] + [_tok(tm, dm), _full((1, dm)), _tok(tm, dm), ANY],
        out_specs=[_tok(tm, dm), _tok(tm, dm), _full((1, dm))],
        scratch=[pltpu.VMEM((n, dm), BF16), pltpu.SemaphoreType.DMA((NDEV,))])


def _wgrad(a, b, name, tmm=256):
    parts = list(a) if isinstance(a, (list, tuple)) else [a]
    t = parts[0].shape[0]
    n = b.shape[1]
    tiles = [p.shape[1] // tmm for p in parts]
    first = [sum(tiles[:i]) for i in range(len(parts))]
    m = sum(tiles) * tmm

    def body(*refs):
        a_refs, b_ref, o_ref = refs[:len(parts)], refs[len(parts)], refs[len(parts) + 1]
        j = pl.program_id(0)
        for a_ref, j0, nt in zip(a_refs, first, tiles):
            if len(parts) == 1:
                o_ref[...] = _dot(a_ref[...], b_ref[...], TN).astype(BF16)
            else:
                @pl.when((j >= j0) & (j < j0 + nt))
                def _():
                    o_ref[...] = _dot(a_ref[...], b_ref[...], TN).astype(BF16)

    a_specs = [pl.BlockSpec((t, tmm), lambda j, j0=j0, nt=nt: (0, jnp.clip(j - j0, 0, nt - 1)))
               for j0, nt in zip(first, tiles)]
    return dict(
        body=body, grid=(sum(tiles),), name=name, args=[*parts, b], out_shape=[jax.ShapeDtypeStruct((m, n), BF16)],
        in_specs=a_specs + [pl.BlockSpec((t, n), lambda j: (0, 0))],
        out_specs=[pl.BlockSpec((tmm, n), lambda j: (j, 0))])


def _halo_specs(tm, t, width, col_blocks):
    nb8 = tm // 8
    last = t // 8 - 1
    prev = [pl.BlockSpec((8, width), lambda i, cb=cb: (jnp.maximum(i * nb8 - 1, 0), cb)) for cb in col_blocks]
    nxt = [pl.BlockSpec((8, width), lambda i, cb=cb: (jnp.minimum((i + 1) * nb8, last), cb)) for cb in col_blocks]
    return prev, nxt


def _shift_rows(z, prev_row, next_row):
    tm = z.shape[0]
    row = lax.broadcasted_iota(jnp.int32, z.shape, 0)
    zm1 = jnp.where(row == 0, prev_row, pltpu.roll(z, 1, 0))
    zp1 = jnp.where(row == tm - 1, next_row, pltpu.roll(z, tm - 1, 0))
    return zm1, zp1


def _gating_fwd(proj, lng, lnb, wsp_ref, bsp_ref, aw):
    tm = proj.shape[0]
    a_u = _gelu(proj[:, 0:aw])
    gv = _gelu(proj[:, aw:2 * aw])
    mu = jnp.mean(gv, axis=-1, keepdims=True)
    xc = gv - mu
    rstd = lax.rsqrt(jnp.mean(xc * xc, axis=-1, keepdims=True) + EPS)
    vn = xc * rstd
    a_v = (vn * lng + lnb).astype(BF16)
    gd = aw // A_GROUPS
    rows = []
    for c in range(tm // CHUNK):
        cols = []
        for g in range(A_GROUPS):
            blk = a_v[c * CHUNK:(c + 1) * CHUNK, g * gd:(g + 1) * gd]
            cols.append(_dot(wsp_ref[g], blk, NN) + bsp_ref[g])
        rows.append(jnp.concatenate(cols, axis=1))
    mixed = jnp.concatenate(rows, axis=0)
    return a_u, vn, rstd, a_v, mixed


def _even_core_fwd(proj, x0, lng, lnb, wsp, bspb, cw, gath, tm):
    t, dm = x0.shape
    aw = lng.shape[1]
    bw = cw.shape[1]
    assert aw == bw and 2 * aw + 3 * bw == proj.shape[1]
    nt = t // tm
    prev, nxt = _halo_specs(tm, t, bw, [3, 4])

    def body(proj_ref, cp_ref, hp_ref, cn_ref, hn_ref, x0_ref, lng_ref, lnb_ref, wsp_ref, bsp_ref, cw_ref, gath_ref,
             x1_ref, y_ref, wbuf, sems):
        i = pl.program_id(0)

        @pl.when(i == 0)
        def _():
            _load_weight(gath_ref, wbuf, sems)
        proj_v = proj_ref[...]
        a_u, _, _, _, mixed = _gating_fwd(proj_v, lng_ref[...], lnb_ref[...], wsp_ref, bsp_ref, aw)
        a_out = a_u * mixed
        bb = proj_v[:, 2 * aw:2 * aw + bw]
        z = proj_v[:, 2 * aw + bw:2 * aw + 2 * bw] * proj_v[:, 2 * aw + 2 * bw:]
        zprev = jnp.where(i > 0, cp_ref[7:8, :] * hp_ref[7:8, :], 0.0)
        znext = jnp.where(i < nt - 1, cn_ref[0:1, :] * hn_ref[0:1, :], 0.0)
        zm1, zp1 = _shift_rows(z, zprev, znext)
        cwv = cw_ref[...]
        conv = zm1 * cwv[0:1, :] + z * cwv[1:2, :] + zp1 * cwv[2:3, :]
        y = jnp.concatenate([a_out, bb * conv], axis=1).astype(BF16)
        y_ref[...] = y
        x1_ref[...] = x0_ref[...] + _dot(y, wbuf[...], NN)

    return dict(
        body=body, grid=(nt,), name="even_core_fwd",
        args=[proj, proj, proj, proj, proj, x0, lng, lnb, wsp, bspb, cw, gath],
        out_shape=[jax.ShapeDtypeStruct((t, dm), F32), jax.ShapeDtypeStruct((t, aw + bw), BF16)],
        in_specs=[_tok(tm, proj.shape[1]), prev[0], prev[1], nxt[0], nxt[1], _tok(tm, dm), _full(lng.shape),
                  _full(lnb.shape), _full(wsp.shape), _full(bspb.shape), _full(cw.shape), ANY],
        out_specs=[_tok(tm, dm), _tok(tm, aw + bw)],
        scratch=[pltpu.VMEM((gath.shape[1] * NDEV, dm), BF16), pltpu.SemaphoreType.DMA((NDEV,))])


def _even_core_bwd(proj, dx1, lng, lnb, wsp, bspb, cw, gath, tm):
    t, dm = dx1.shape
    aw, bw = lng.shape[1], cw.shape[1]
    gd = aw // A_GROUPS
    nt = t // tm
    inw = proj.shape[1]
    prev, nxt = _halo_specs(tm, t, bw, [2, 3, 4])
    nb8 = tm // 8
    last8 = t // 8 - 1

    def body(proj_ref, bp_ref, cp_ref, hp_ref, bn_ref, cn_ref, hn_ref, dx_ref, dxp_ref, dxn_ref,
             lng_ref, lnb_ref, wsp_ref, bsp_ref, cw_ref, gath_ref,
             dproj_ref, dlng_ref, dlnb_ref, dwsp_ref, dbsp_ref, dcw_ref, wbuf, sems):
        i = pl.program_id(0)

        @pl.when(i == 0)
        def _():
            _load_weight(gath_ref, wbuf, sems)
            dlng_ref[...] = jnp.zeros_like(dlng_ref)
            dlnb_ref[...] = jnp.zeros_like(dlnb_ref)
            dwsp_ref[...] = jnp.zeros_like(dwsp_ref)
            dbsp_ref[...] = jnp.zeros_like(dbsp_ref)
            dcw_ref[...] = jnp.zeros_like(dcw_ref)
        proj_v = proj_ref[...]
        lng_v = lng_ref[...]
        a_u, vn, rstd, a_v, mixed = _gating_fwd(proj_v, lng_v, lnb_ref[...], wsp_ref, bsp_ref, aw)
        w = wbuf[...]
        dy = _dot(dx_ref[...].astype(BF16), w, NT)
        da_out, db_out = dy[:, 0:aw], dy[:, aw:]
        da_u = da_out * mixed
        dmixed = da_out * a_u
        dmb = dmixed.astype(BF16)
        rows = []
        for c in range(tm // CHUNK):
            cols = []
            for g in range(A_GROUPS):
                r0, c0 = c * CHUNK, g * gd
                dm_cg = dmb[r0:r0 + CHUNK, c0:c0 + gd]
                cols.append(_dot(wsp_ref[g], dm_cg, TN))
                dwsp_ref[g] += _dot(dm_cg, a_v[r0:r0 + CHUNK, c0:c0 + gd], NT)
                dbsp_ref[g] += dmixed[r0:r0 + CHUNK, c0:c0 + gd]
            rows.append(jnp.concatenate(cols, axis=1))
        dav = jnp.concatenate(rows, axis=0)
        dlng_ref[...] += jnp.sum(dav * vn, axis=0, keepdims=True)
        dlnb_ref[...] += jnp.sum(dav, axis=0, keepdims=True)
        dvn = dav * lng_v
        dgv = rstd * (dvn - jnp.mean(dvn, axis=-1, keepdims=True) - vn * jnp.mean(dvn * vn, axis=-1, keepdims=True))
        dv_pre = dgv * _gelu_grad(proj_v[:, aw:2 * aw])
        du_pre = da_u * _gelu_grad(proj_v[:, 0:aw])
        bb = proj_v[:, 2 * aw:2 * aw + bw]
        bc = proj_v[:, 2 * aw + bw:2 * aw + 2 * bw]
        bh = proj_v[:, 2 * aw + 2 * bw:]
        z = bc * bh
        zprev = jnp.where(i > 0, cp_ref[7:8, :] * hp_ref[7:8, :], 0.0)
        znext = jnp.where(i < nt - 1, cn_ref[0:1, :] * hn_ref[0:1, :], 0.0)
        zm1, zp1 = _shift_rows(z, zprev, znext)
        cwv = cw_ref[...]
        conv = zm1 * cwv[0:1, :] + z * cwv[1:2, :] + zp1 * cwv[2:3, :]
        dbb = db_out * conv
        dconv = db_out * bb
        dx_edge = jnp.concatenate([dxp_ref[...], dxn_ref[...]], axis=0).astype(BF16)
        dy_edge = _dot(dx_edge, w[aw:, :], NT)
        dcprev = jnp.where(i > 0, dy_edge[7:8, :] * bp_ref[7:8, :], 0.0)
        dcnext = jnp.where(i < nt - 1, dy_edge[8:9, :] * bn_ref[0:1, :], 0.0)
        dcm1, dcp1 = _shift_rows(dconv, dcprev, dcnext)
        dz = dcp1 * cwv[0:1, :] + dconv * cwv[1:2, :] + dcm1 * cwv[2:3, :]
        dcw_ref[0:1, :] += jnp.sum(dconv * zm1, axis=0, keepdims=True)
        dcw_ref[1:2, :] += jnp.sum(dconv * z, axis=0, keepdims=True)
        dcw_ref[2:3, :] += jnp.sum(dconv * zp1, axis=0, keepdims=True)
        dproj_ref[...] = jnp.concatenate([du_pre, dv_pre, dbb, dz * bh, dz * bc], axis=1).astype(BF16)

    row8 = lambda f: pl.BlockSpec((8, dm), f)
    return dict(
        body=body, grid=(nt,), name="even_core_bwd",
        args=[proj, proj, proj, proj, proj, proj, proj, dx1, dx1, dx1, lng, lnb, wsp, bspb, cw, gath],
        out_shape=[jax.ShapeDtypeStruct((t, inw), BF16), jax.ShapeDtypeStruct((1, aw), F32),
                   jax.ShapeDtypeStruct((1, aw), F32), jax.ShapeDtypeStruct(wsp.shape, F32),
                   jax.ShapeDtypeStruct((A_GROUPS, CHUNK, gd), F32), jax.ShapeDtypeStruct(cw.shape, F32)],
        in_specs=[_tok(tm, inw), prev[0], prev[1], prev[2], nxt[0], nxt[1], nxt[2], _tok(tm, dm),
                  row8(lambda i: (jnp.maximum(i * nb8 - 1, 0), 0)), row8(lambda i: (jnp.minimum((i + 1) * nb8, last8), 0)),
                  _full(lng.shape), _full(lnb.shape), _full(wsp.shape), _full(bspb.shape), _full(cw.shape), ANY],
        out_specs=[_tok(tm, inw), _full((1, aw)), _full((1, aw)), _full(wsp.shape),
                   _full((A_GROUPS, CHUNK, gd)), _full(cw.shape)],
        scratch=[pltpu.VMEM((gath.shape[1] * NDEV, dm), BF16), pltpu.SemaphoreType.DMA((NDEV,))])


def _ff_chunks(f, width=1024):
    return [(c0, min(c0 + width, f)) for c0 in range(0, f, width)]


def _ffn_up(x, gain, gath_g, gath_u, name, tm):
    t, dm = x.shape
    f = gath_g.shape[1] * NDEV

    def body(x_ref, g_ref, gg_ref, gu_ref, gate_ref, up_ref, act_ref, wg, wu, sems):
        @pl.when(pl.program_id(0) == 0)
        def _():
            _load_weight(gg_ref, wg, sems)
            _load_weight(gu_ref, wu, sems)
        h, _ = _rms_fwd(x_ref[...], g_ref[...])
        hb = h.astype(BF16)
        for c0, c1 in _ff_chunks(f):
            gate = _dot(hb, wg[c0:c1, :], NT)
            up = _dot(hb, wu[c0:c1, :], NT)
            gate_ref[:, c0:c1] = gate.astype(BF16)
            up_ref[:, c0:c1] = up.astype(BF16)
            act_ref[:, c0:c1] = (gate * _sigmoid(gate) * up).astype(BF16)

    o = jax.ShapeDtypeStruct((t, f), BF16)
    return dict(
        body=body, grid=(t // tm,), name=name, args=[x, gain, gath_g, gath_u], out_shape=[o, o, o],
        in_specs=[_tok(tm, dm), _full((1, dm)), ANY, ANY], out_specs=[_tok(tm, f)] * 3,
        scratch=[pltpu.VMEM((f, dm), BF16), pltpu.VMEM((f, dm), BF16), pltpu.SemaphoreType.DMA((NDEV,))])


def _ffn_down(x, act, gath_d, name, tm):
    t, dm = x.shape
    f = act.shape[1]

    def body(x_ref, a_ref, gd_ref, xo_ref, wd, sems):
        @pl.when(pl.program_id(0) == 0)
        def _():
            _load_weight(gd_ref, wd, sems)
        xo_ref[...] = x_ref[...] + _dot(a_ref[...], wd[...], NN)

    return dict(
        body=body, grid=(t // tm,), name=name, args=[x, act, gath_d], out_shape=[jax.ShapeDtypeStruct((t, dm), F32)],
        in_specs=[_tok(tm, dm), _tok(tm, f), ANY], out_specs=[_tok(tm, dm)],
        scratch=[pltpu.VMEM((f, dm), BF16), pltpu.SemaphoreType.DMA((NDEV,))])


def _ffn_down_loss(x, act, gath_d, target, gain, name, tm):
    t, dm = x.shape
    f = act.shape[1]
    steps = t // tm

    def body(x_ref, a_ref, gd_ref, t_ref, g_ref, loss_ref, dx_ref, dxb_ref, dgain_ref, wd, acc, sems):
        i = pl.program_id(0)

        @pl.when(i == 0)
        def _():
            _load_weight(gd_ref, wd, sems)
            acc[...] = jnp.zeros_like(acc)
            dgain_ref[...] = jnp.zeros_like(dgain_ref)
        xv = x_ref[...] + _dot(a_ref[...], wd[...], NN)
        gain_v = g_ref[...]
        y, r = _rms_fwd(xv, gain_v)
        e = y - t_ref[...]
        acc[...] += jnp.sum(e * e, axis=0, keepdims=True)
        dx, dgain = _rms_bwd(e * (1.0 / dm), xv, r, gain_v)
        dx_ref[...] = dx
        dxb_ref[...] = dx.astype(BF16)
        dgain_ref[...] += dgain

        @pl.when(i == steps - 1)
        def _():
            loss_ref[...] = jnp.sum(acc[...], axis=-1, keepdims=True) * (0.5 / dm)

    return dict(
        body=body, grid=(steps,), name=name, args=[x, act, gath_d, target, gain],
        out_shape=[jax.ShapeDtypeStruct((1, 1), F32), jax.ShapeDtypeStruct((t, dm), F32),
                   jax.ShapeDtypeStruct((t, dm), BF16), jax.ShapeDtypeStruct((1, dm), F32)],
        in_specs=[_tok(tm, dm), _tok(tm, f), ANY, _tok(tm, dm), _full((1, dm))],
        out_specs=[_full((1, 1)), _tok(tm, dm), _tok(tm, dm), _full((1, dm))],
        scratch=[pltpu.VMEM((f, dm), BF16), pltpu.VMEM((1, dm), F32), pltpu.SemaphoreType.DMA((NDEV,))])


def _ffn_bwd(dxo, x, gate, up, gain, gath_g, gath_u, gath_d, name, tm):
    t, dm = x.shape
    f = gate.shape[1]

    def body(dxo_ref, x_ref, gate_ref, up_ref, g_ref, gg_ref, gu_ref, gd_ref,
             dx_ref, dxb_ref, dg_ref, du_ref, hb_ref, dgain_ref, wg, wu, wd, sems):
        @pl.when(pl.program_id(0) == 0)
        def _():
            _load_weight(gg_ref, wg, sems)
            _load_weight(gu_ref, wu, sems)
            _load_weight(gd_ref, wd, sems)
            dgain_ref[...] = jnp.zeros_like(dgain_ref)
        xv, gain_v, dxo_v = x_ref[...], g_ref[...], dxo_ref[...]
        h, r = _rms_fwd(xv, gain_v)
        hb_ref[...] = h.astype(BF16)
        dxob = dxo_v.astype(BF16)
        dh = jnp.zeros_like(xv)
        for c0, c1 in _ff_chunks(f):
            gate_v = gate_ref[:, c0:c1].astype(F32)
            up_v = up_ref[:, c0:c1].astype(F32)
            s = _sigmoid(gate_v)
            silu = gate_v * s
            dact = _dot(dxob, wd[c0:c1, :], NT)
            dg = (dact * up_v * (s * (1.0 + gate_v * (1.0 - s)))).astype(BF16)
            du = (dact * silu).astype(BF16)
            dg_ref[:, c0:c1] = dg
            du_ref[:, c0:c1] = du
            dh = dh + _dot(dg, wg[c0:c1, :], NN) + _dot(du, wu[c0:c1, :], NN)
        dx, dgain = _rms_bwd(dh, xv, r, gain_v)
        dx = dxo_v + dx
        dx_ref[...] = dx
        dxb_ref[...] = dx.astype(BF16)
        dgain_ref[...] += dgain

    return dict(
        body=body, grid=(t // tm,), name=name, args=[dxo, x, gate, up, gain, gath_g, gath_u, gath_d],
        out_shape=[jax.ShapeDtypeStruct((t, dm), F32), jax.ShapeDtypeStruct((t, dm), BF16),
                   jax.ShapeDtypeStruct((t, f), BF16), jax.ShapeDtypeStruct((t, f), BF16),
                   jax.ShapeDtypeStruct((t, dm), BF16), jax.ShapeDtypeStruct((1, dm), F32)],
        in_specs=[_tok(tm, dm), _tok(tm, dm), _tok(tm, f), _tok(tm, f), _full((1, dm)), ANY, ANY, ANY],
        out_specs=[_tok(tm, dm), _tok(tm, dm), _tok(tm, f), _tok(tm, f), _tok(tm, dm), _full((1, dm))],
        scratch=[pltpu.VMEM((f, dm), BF16), pltpu.VMEM((f, dm), BF16), pltpu.VMEM((f, dm), BF16),
                 pltpu.SemaphoreType.DMA((NDEV,))])


def _t5_buckets(rel):
    nb = N_BUCKETS // 2
    ret = jnp.where(rel > 0, nb, 0)
    n = jnp.abs(rel)
    max_exact = nb // 2
    nf = jnp.maximum(n, 1).astype(jnp.float32)
    large = max_exact + (jnp.log(nf / max_exact) / math.log(MAX_DISTANCE / max_exact)
                         * (nb - max_exact)).astype(jnp.int32)
    large = jnp.minimum(large, nb - 1)
    return ret + jnp.where(n < max_exact, n, large)


def _bucket_table():
    qi = jnp.arange(CHUNK, dtype=jnp.int32)[:, None]
    kj = jnp.arange(3 * CHUNK, dtype=jnp.int32)[None, :]
    rel = kj - CHUNK - qi
    return jnp.where(jnp.abs(rel) <= CHUNK, _t5_buckets(rel), -1)


def _bias_table(rel_bias_t, buckets):
    nh = rel_bias_t.shape[0]

    def body(rb_ref, bk_ref, o_ref):
        bk = bk_ref[...]
        for h in range(nh):
            acc = jnp.where(bk < 0, NEG, 0.0).astype(F32)
            for b in range(N_BUCKETS):
                acc = jnp.where(bk == b, rb_ref[h, b] * LOG2E, acc)
            o_ref[h] = acc

    return dict(
        body=body, grid=(1,), name="bias_table", args=[rel_bias_t, buckets],
        out_shape=[jax.ShapeDtypeStruct((nh,) + buckets.shape, F32)],
        in_specs=[pl.BlockSpec(memory_space=pltpu.SMEM), _full(buckets.shape)],
        out_specs=[_full((nh,) + buckets.shape)])


def _rel_bias_grad(dbias, buckets):
    nh = dbias.shape[0]

    def body(db_ref, bk_ref, o_ref):
        bk = bk_ref[...]
        lane = lax.broadcasted_iota(jnp.int32, (1, 128), 1)
        for h in range(nh):
            d = db_ref[h]
            row = jnp.zeros((1, 128), F32)
            for b in range(N_BUCKETS):
                s = jnp.sum(jnp.sum(jnp.where(bk == b, d, 0.0), axis=1, keepdims=True), axis=0, keepdims=True)
                row = jnp.where(lane == b, s, row)
            o_ref[h:h + 1, :] = row

    return pl.pallas_call(
        body, out_shape=jax.ShapeDtypeStruct((nh, 128), F32),
        in_specs=[pl.BlockSpec(memory_space=pltpu.VMEM), pl.BlockSpec(memory_space=pltpu.VMEM)],
        out_specs=pl.BlockSpec(memory_space=pltpu.VMEM), compiler_params=_cp(0), name="rel_bias_grad")(dbias, buckets)


def _half_masks():
    lane = lax.broadcasted_iota(jnp.int32, (CHUNK, 128), 1)
    return lane < HEAD_DIM, lane >= HEAD_DIM


def _kv_low(ref, starts, hk, lo):
    kt = (hk // 2) * 128
    out = []
    for jj in range(3):
        blk = ref[pl.ds(starts[jj], CHUNK), kt:kt + 128]
        if hk % 2 == 1:
            blk = pltpu.roll(blk, HEAD_DIM, 1)
        out.append(jnp.where(lo, blk, jnp.zeros_like(blk)))
    return out


def _stack_heads(tile_a, tile_b):
    return jnp.concatenate([tile_a, pltpu.roll(tile_a, HEAD_DIM, 1), tile_b, pltpu.roll(tile_b, HEAD_DIM, 1)], axis=0)


def _unstack_heads(o4):
    return (o4[0:CHUNK] + pltpu.roll(o4[CHUNK:2 * CHUNK], HEAD_DIM, 1),
            o4[2 * CHUNK:3 * CHUNK] + pltpu.roll(o4[3 * CHUNK:], HEAD_DIM, 1))


ATT_SLAB = 32


def _softmax_slab(s_scr, hk, g, r0, bias_ref, sink_ref, n, nblk):
    scale = HEAD_DIM ** -0.5 * LOG2E
    h = (N_HEADS // N_KV) * hk + g
    s = []
    for jj in range(3):
        sj = (s_scr[hk, jj, pl.ds(g * CHUNK + r0, ATT_SLAB), :] * scale
              + bias_ref[h, pl.ds(r0, ATT_SLAB), jj * CHUNK:(jj + 1) * CHUNK])
        if jj == 0:
            sj = jnp.where(n > 0, sj, NEG)
        if jj == 2:
            sj = jnp.where(n < nblk - 1, sj, NEG)
        s.append(sj)
    sink = sink_ref[h] * LOG2E
    m = jnp.maximum(jnp.max(jnp.maximum(jnp.maximum(s[0], s[1]), s[2]), axis=-1, keepdims=True), sink)
    e = [jnp.exp2(sj - m) for sj in s]
    es = jnp.exp2(sink - m)
    inv = 1.0 / (jnp.sum(e[0] + e[1] + e[2], axis=-1, keepdims=True) + es)
    return [ej * inv for ej in e], es * inv


def _key_block_starts(n, nblk):
    return [pl.multiple_of(jnp.clip(n - 1 + jj, 0, nblk - 1) * CHUNK, CHUNK) for jj in range(3)]


def _attn_fwd(qkv, x2, bias, sink, gath):
    t, dm = x2.shape
    nblk = t // CHUNK
    kvw = N_KV * HEAD_DIM
    kcb, vcb = dm // kvw, dm // kvw + 1
    slab = (N_KV, 3, 4 * CHUNK, CHUNK)

    def body(q_ref, k_ref, v_ref, x2_ref, bias_ref, sink_ref, gath_ref, x3_ref, att_ref, p_ref, ps_ref,
             wbuf, s_scr, sems):
        n = pl.program_id(0)

        @pl.when(n == 0)
        def _():
            _load_weight(gath_ref, wbuf, sems)
        lo, _ = _half_masks()
        lane_s = lax.broadcasted_iota(jnp.int32, (ATT_SLAB, 128), 1)
        starts = _key_block_starts(n, nblk)
        tiles = []
        for hk in range(N_KV):
            c0 = (2 * hk) * 128
            k_lo = _kv_low(k_ref, starts, hk, lo)
            v_lo = _kv_low(v_ref, starts, hk, lo)
            q4 = _stack_heads(q_ref[:, c0:c0 + 128], q_ref[:, c0 + 128:c0 + 256])
            for jj in range(3):
                s_scr[hk, jj] = _dot(q4, k_lo[jj], NT)
            for g in range(4):
                h = 4 * hk + g
                for r0 in range(0, CHUNK, ATT_SLAB):
                    p, ps = _softmax_slab(s_scr, hk, g, r0, bias_ref, sink_ref, n, nblk)
                    for jj in range(3):
                        p_ref[hk, jj, g * CHUNK + r0:g * CHUNK + r0 + ATT_SLAB, :] = p[jj].astype(BF16)
                    rest = jnp.zeros((ATT_SLAB, 128), F32) if h == 0 else ps_ref[r0:r0 + ATT_SLAB, :]
                    ps_ref[r0:r0 + ATT_SLAB, :] = jnp.where(lane_s == h, ps, rest)
            o4 = _dot(p_ref[hk, 0], v_lo[0], NN) + _dot(p_ref[hk, 1], v_lo[1], NN) + _dot(p_ref[hk, 2], v_lo[2], NN)
            tiles += list(_unstack_heads(o4))
        att = jnp.concatenate(tiles, axis=1).astype(BF16)
        att_ref[...] = att
        x3_ref[...] = x2_ref[...] + _dot(att, wbuf[...], NN)

    blk = pl.BlockSpec((CHUNK, dm), lambda n: (n, 0))
    return dict(
        body=body, grid=(nblk,), name="attn_fwd", args=[qkv, qkv, qkv, x2, bias, sink, gath],
        out_shape=[jax.ShapeDtypeStruct((t, dm), F32), jax.ShapeDtypeStruct((t, dm), BF16),
                   jax.ShapeDtypeStruct((nblk,) + slab, BF16), jax.ShapeDtypeStruct((t, 128), F32)],
        in_specs=[blk, pl.BlockSpec((t, kvw), lambda n: (0, kcb)), pl.BlockSpec((t, kvw), lambda n: (0, vcb)), blk,
                  _full(bias.shape), pl.BlockSpec(memory_space=pltpu.SMEM), ANY],
        out_specs=[blk, blk, pl.BlockSpec((None,) + slab, lambda n: (n, 0, 0, 0, 0)),
                   pl.BlockSpec((CHUNK, 128), lambda n: (n, 0))],
        scratch=[pltpu.VMEM((gath.shape[1] * NDEV, dm), BF16), pltpu.VMEM(slab, F32),
                 pltpu.SemaphoreType.DMA((NDEV,))])


def _attn_bwd(qkv, att, probs, sink_probs, dx3, bias_shape, gath):
    t, dm = dx3.shape
    nblk = t // CHUNK
    kvw = N_KV * HEAD_DIM
    kcb, vcb = dm // kvw, dm // kvw + 1
    scale = HEAD_DIM ** -0.5
    slab = (N_KV, 3, 4 * CHUNK, CHUNK)

    def body(q_ref, k_ref, v_ref, att_ref, p_ref, ps_ref, dx_ref, gath_ref,
             dq_ref, dkb_ref, dvb_ref, dbias_ref, dsink_ref,
             wbuf, dp_scr, ds_scr, prod_scr, dsum_scr, dk_ref, dv_ref, sems):
        n = pl.program_id(0)

        @pl.when(n == 0)
        def _():
            _load_weight(gath_ref, wbuf, sems)
            dk_ref[...] = jnp.zeros_like(dk_ref)
            dv_ref[...] = jnp.zeros_like(dv_ref)
            dbias_ref[...] = jnp.zeros_like(dbias_ref)
            dsink_ref[...] = jnp.zeros_like(dsink_ref)
        lo, hi = _half_masks()
        lane_s = lax.broadcasted_iota(jnp.int32, (ATT_SLAB, 128), 1)
        starts = _key_block_starts(n, nblk)
        dout = _dot(dx_ref[...].astype(BF16), wbuf[...], NT)
        prod_scr[...] = dout * att_ref[...].astype(F32)
        doutb = dout.astype(BF16)
        dq_tiles = []
        for hk in range(N_KV):
            kt = (hk // 2) * 128
            c0 = (2 * hk) * 128
            k_lo = _kv_low(k_ref, starts, hk, lo)
            v_lo = _kv_low(v_ref, starts, hk, lo)
            q4 = _stack_heads(q_ref[:, c0:c0 + 128], q_ref[:, c0 + 128:c0 + 256])
            do4 = _stack_heads(doutb[:, c0:c0 + 128], doutb[:, c0 + 128:c0 + 256])
            for jj in range(3):
                dp_scr[hk, jj] = _dot(do4, v_lo[jj], NT)
            for g in range(4):
                h = 4 * hk + g
                for r0 in range(0, CHUNK, ATT_SLAB):
                    rows = slice(g * CHUNK + r0, g * CHUNK + r0 + ATT_SLAB)
                    pt = prod_scr[r0:r0 + ATT_SLAB, c0 + (g // 2) * 128:c0 + (g // 2 + 1) * 128]
                    msk = lane_s < HEAD_DIM if g % 2 == 0 else lane_s >= HEAD_DIM
                    dsum = jnp.sum(jnp.where(msk, pt, 0.0), axis=-1, keepdims=True)
                    rest = jnp.zeros((ATT_SLAB, 128), F32) if h == 0 else dsum_scr[r0:r0 + ATT_SLAB, :]
                    dsum_scr[r0:r0 + ATT_SLAB, :] = jnp.where(lane_s == h, dsum, rest)
                    for jj in range(3):
                        ds = p_ref[hk, jj, rows, :].astype(F32) * (dp_scr[hk, jj, rows, :] - dsum)
                        dbias_ref[h, r0:r0 + ATT_SLAB, jj * CHUNK:(jj + 1) * CHUNK] += ds
                        ds_scr[hk, jj, rows, :] = ds.astype(BF16)
            dq4 = jnp.zeros((4 * CHUNK, 128), F32)
            for jj in range(3):
                ds4 = ds_scr[hk, jj]
                dq4 = dq4 + _dot(ds4, k_lo[jj], NN) * scale
                dkj = _dot(ds4, q4, TN) * scale
                dvj = _dot(p_ref[hk, jj], do4, TN)
                if hk % 2 == 1:
                    dkj, dvj = pltpu.roll(dkj, HEAD_DIM, 1), pltpu.roll(dvj, HEAD_DIM, 1)
                keep = lo if hk % 2 == 0 else hi
                dk_ref[pl.ds(starts[jj], CHUNK), kt:kt + 128] += jnp.where(keep, dkj, 0.0)
                dv_ref[pl.ds(starts[jj], CHUNK), kt:kt + 128] += jnp.where(keep, dvj, 0.0)
            dq_tiles += list(_unstack_heads(dq4))
        dq_ref[...] = jnp.concatenate(dq_tiles, axis=1).astype(BF16)
        dsink_ref[...] -= jnp.sum(ps_ref[...] * dsum_scr[...], axis=0, keepdims=True)

        @pl.when(n == nblk - 1)
        def _():
            dkb_ref[...] = dk_ref[...].astype(BF16)
            dvb_ref[...] = dv_ref[...].astype(BF16)

    blk = pl.BlockSpec((CHUNK, dm), lambda n: (n, 0))
    return dict(
        body=body, grid=(nblk,), name="attn_bwd", args=[qkv, qkv, qkv, att, probs, sink_probs, dx3, gath],
        out_shape=[jax.ShapeDtypeStruct((t, dm), BF16), jax.ShapeDtypeStruct((t, kvw), BF16),
                   jax.ShapeDtypeStruct((t, kvw), BF16), jax.ShapeDtypeStruct(bias_shape, F32),
                   jax.ShapeDtypeStruct((1, 128), F32)],
        in_specs=[blk, pl.BlockSpec((t, kvw), lambda n: (0, kcb)), pl.BlockSpec((t, kvw), lambda n: (0, vcb)),
                  blk, pl.BlockSpec((None,) + slab, lambda n: (n, 0, 0, 0, 0)),
                  pl.BlockSpec((CHUNK, 128), lambda n: (n, 0)), blk, ANY],
        out_specs=[blk, _full((t, kvw)), _full((t, kvw)), _full(bias_shape), _full((1, 128))],
        scratch=[pltpu.VMEM((gath.shape[1] * NDEV, dm), BF16), pltpu.VMEM(slab, F32), pltpu.VMEM(slab, BF16),
                 pltpu.VMEM((CHUNK, dm), F32), pltpu.VMEM((CHUNK, 128), F32),
                 pltpu.VMEM((t, kvw), F32), pltpu.VMEM((t, kvw), F32), pltpu.SemaphoreType.DMA((NDEV,))])


def _finish_weight(recvs, w, m, v, name):
    nl, r, dm = w.shape
    assert nl == len(recvs) and all(rc.shape[1:] == (r, dm) for rc in recvs)
    td = dm // 2
    wspec = pl.BlockSpec((None, r, td), lambda l, j: (l, 0, j))

    def body(*refs):
        r_refs = refs[:nl]
        w_ref, m_ref, v_ref, g_ref, d_ref, nm_ref, nv_ref = refs[nl:]
        layer = pl.program_id(0)
        for li in range(nl):
            @pl.when(layer == li)
            def _():
                g = r_refs[li][0].astype(F32)
                for d in range(1, recvs[li].shape[0]):
                    g = g + r_refs[li][d].astype(F32)
                delta, nm, nv = _adamw_math(w_ref[...], g, m_ref[...], v_ref[...])
                g_ref[...] = g
                d_ref[...] = delta
                nm_ref[...] = nm
                nv_ref[...] = nv

    o = jax.ShapeDtypeStruct(w.shape, F32)
    return dict(
        body=body, grid=(nl, 2), name=name, args=[*recvs, w, m, v], out_shape=[o, o, o, o],
        in_specs=[pl.BlockSpec((rc.shape[0], r, td), lambda l, j: (0, 0, j)) for rc in recvs] + [wspec] * 3,
        out_specs=[wspec] * 4)


def _adamw_small(ws, ms, vs, slots, late_slots, loss_slots, name):
    n = len(ws)

    def total(ref):
        acc = ref[0].astype(F32)
        for d in range(1, NDEV):
            acc = acc + ref[d].astype(F32)
        return acc

    def body(*refs):
        ins, outs = refs[:4 * n + 2], refs[4 * n + 2:]
        for i in range(n):
            w_ref, m_ref, v_ref, s_ref = ins[4 * i:4 * i + 4]
            g_ref, d_ref, nm_ref, nv_ref = outs[4 * i:4 * i + 4]
            g_ref[...] = total(s_ref)
            if i == 0:
                g_ref[0:1, :] = total(ins[4 * n])
            d_ref[...], nm_ref[...], nv_ref[...] = _adamw_math(w_ref[...], g_ref[...], m_ref[...], v_ref[...])
        outs[4 * n][...] = total(ins[4 * n + 1])

    args, out_shape = [], []
    for w, m, v, s in zip(ws, ms, vs, slots):
        args += [w, m, v, s]
        out_shape += [jax.ShapeDtypeStruct(w.shape, F32)] * 4
    args += [late_slots, loss_slots]
    out_shape.append(jax.ShapeDtypeStruct((1, 1), F32))
    out = pl.pallas_call(
        body, grid=(1,), out_shape=tuple(out_shape), in_specs=[_full(a.shape) for a in args],
        out_specs=tuple(_full(o.shape) for o in out_shape), compiler_params=_cp(), name=name)(*args)
    return [tuple(out[4 * i:4 * i + 4]) for i in range(n)], out[4 * n]


def kernel(x, norm_mix, norm_ffn, even_w_in, even_v_ln_g, even_v_ln_b, even_w_spatial, even_b_spatial, even_conv_w, even_w_out, attn_w_qkv, attn_sink, rel_bias, attn_w_out, ffn_w_gate, ffn_w_up, ffn_w_down, final_norm, loss_target, m_norm_mix, m_norm_ffn, m_even_w_in, m_even_v_ln_g, m_even_v_ln_b, m_even_w_spatial, m_even_b_spatial, m_even_conv_w, m_even_w_out, m_attn_w_qkv, m_attn_sink, m_rel_bias, m_attn_w_out, m_ffn_w_gate, m_ffn_w_up, m_ffn_w_down, m_final_norm, v_norm_mix, v_norm_ffn, v_even_w_in, v_even_v_ln_g, v_even_v_ln_b, v_even_w_spatial, v_even_b_spatial, v_even_conv_w, v_even_w_out, v_attn_w_qkv, v_attn_sink, v_rel_bias, v_attn_w_out, v_ffn_w_gate, v_ffn_w_up, v_ffn_w_down, v_final_norm):
    t, dm = x.shape[1], x.shape[2]
    aw = even_v_ln_g.shape[1]
    bw = even_conv_w.shape[2] * NDEV
    gd = aw // A_GROUPS
    tm = min(512, t // 2)
    tmf = min(256, t // 2)
    me = _my_index()
    row = lambda a: a.reshape(1, -1)

    colT = lambda w: w.T.astype(BF16)
    sh = dict(winT=colT(even_w_in[0]), wqkvT=colT(attn_w_qkv[0]), wgT0=colT(ffn_w_gate[0]), wuT0=colT(ffn_w_up[0]),
              wgT1=colT(ffn_w_gate[1]), wuT1=colT(ffn_w_up[1]), woe=even_w_out[0].astype(BF16),
              woa=attn_w_out[0].astype(BF16), wd0=ffn_w_down[0].astype(BF16), wd1=ffn_w_down[1].astype(BF16))
    gather = lambda names: _GatherCarry([sh[n] for n in names])

    in_full = lambda a: lax.dynamic_update_slice(jnp.zeros((3, bw), F32), a[0], (0, me * (bw // NDEV)))

    x0 = x[0]
    wsp_b = even_w_spatial[0].astype(BF16)
    bspb = jnp.broadcast_to(even_b_spatial[0][:, :, None], (A_GROUPS, CHUNK, gd))
    buckets = _bucket_table()
    sink = attn_sink[0]

    (bias,), ((g_winT,), (cw_slots,)) = _call(
        _bias_table(rel_bias.T, buckets), [gather(["winT"]), _BroadcastCarry([in_full(even_conv_w)])])
    cw_full = jnp.sum(cw_slots, axis=0)
    (proj, h0b), (g_woe, g_wgT0) = _call(_norm_proj(x0, row(norm_mix[0]), g_winT, F32, "in_proj", tm),
                                         gather(["woe", "wgT0"]))
    (x1, yb), (g_wuT0,) = _call(_even_core_fwd(proj, x0, even_v_ln_g, even_v_ln_b, wsp_b, bspb, cw_full, g_woe, tm),
                                gather(["wuT0"]))
    (gate0, up0, act0), (g_wd0,) = _call(_ffn_up(x1, row(norm_ffn[0]), g_wgT0, g_wuT0, "ffn_up0", tmf), gather(["wd0"]))
    (x2,), (g_wqkvT,) = _call(_ffn_down(x1, act0, g_wd0, "ffn_down0", tm), gather(["wqkvT"]))
    (qkv, h2b), (g_woa,) = _call(_norm_proj(x2, row(norm_mix[1]), g_wqkvT, BF16, "qkv_proj", tm), gather(["woa"]))
    (x3, attb, probs, sink_probs), (g_wgT1, g_wuT1) = _call(
        _attn_fwd(qkv, x2, bias, sink, g_woa), gather(["wgT1", "wuT1"]))
    (gate1, up1, act1), (g_wd1,) = _call(_ffn_up(x3, row(norm_ffn[1]), g_wgT1, g_wuT1, "ffn_up1", tmf), gather(["wd1"]))
    (loss_part, dx4, dx4b, d_final), _ = _call(
        _ffn_down_loss(x3, act1, g_wd1, loss_target[0], row(final_norm), "ffn_down1_loss", tm))

    (dx3, dx3b, dg1, du1, h3b, d_nffn1), _ = _call(
        _ffn_bwd(dx4, x3, gate1, up1, row(norm_ffn[1]), g_wgT1, g_wuT1, g_wd1, "ffn_bwd1", tmf))
    (p_wgT1,), _ = _call(_wgrad(dg1, h3b, "wgrad_gate1"))
    (p_wuT1,), _ = _call(_wgrad(du1, h3b, "wgrad_up1"))
    (p_wd1,), ((a_wgT1,), (a_wuT1,)) = _call(
        _wgrad(act1, dx4b, "wgrad_down1"), [_PairCarry(p_wgT1), _PairCarry(p_wuT1)])
    (dq, dk, dv, dbias, dsink), ((r_wgT1,), (a_wd1,)) = _call(
        _attn_bwd(qkv, attb, probs, sink_probs, dx3, bias.shape, g_woa),
        [_ChipSumCarry(p_wgT1, a_wgT1), _PairCarry(p_wd1)])
    (p_woa,), _ = _call(_wgrad(attb, dx3b, "wgrad_attn_out"))
    d_relb = _rel_bias_grad(dbias, buckets)[:, 0:N_BUCKETS].T
    (dx2, dx2b, d_nmix1), (r_wuT1,) = _call(
        _proj_bwd_norm([dq, dk, dv], x2, row(norm_mix[1]), dx3, g_wqkvT, "qkv_bwd", tm),
        _ChipSumCarry(p_wuT1, a_wuT1))
    (p_wqkvT,), _ = _call(_wgrad([dq, dk, dv], h2b, "wgrad_qkv"))
    (dx1, dx1b, dg0, du0, h1b, d_nffn0), ((r_wd1,), (r_woa, r_wqkvT)) = _call(
        _ffn_bwd(dx2, x1, gate0, up0, row(norm_ffn[0]), g_wgT0, g_wuT0, g_wd0, "ffn_bwd0", tmf),
        [_ChipSumCarry(p_wd1, a_wd1), _GradCarry([p_woa, p_wqkvT])])
    (p_wgT0,), _ = _call(_wgrad(dg0, h1b, "wgrad_gate0"))
    (p_wuT0,), (a_wgT0,) = _call(_wgrad(du0, h1b, "wgrad_up0"), _PairCarry(p_wgT0))
    (p_wd0,), ((r_wgT0,), (a_wuT0,)) = _call(
        _wgrad(act0, dx2b, "wgrad_down0"), [_ChipSumCarry(p_wgT0, a_wgT0), _PairCarry(p_wuT0)])
    (dproj, d_lng, d_lnb, d_wsp, d_bsp3, d_cw), ((r_wuT0,), (a_wd0,)) = _call(
        _even_core_bwd(proj, dx1, even_v_ln_g, even_v_ln_b, wsp_b, bspb, cw_full, g_woe, tm),
        [_ChipSumCarry(p_wuT0, a_wuT0), _PairCarry(p_wd0)])
    small_names = ["norm_mix", "norm_ffn", "even_v_ln_g", "even_v_ln_b", "even_w_spatial", "even_b_spatial",
                   "even_conv_w", "attn_sink", "rel_bias", "final_norm"]
    small_parts = [jnp.concatenate([jnp.zeros_like(d_nmix1), d_nmix1]), jnp.concatenate([d_nffn0, d_nffn1]),
                   d_lng, d_lnb, d_wsp[None].astype(BF16), jnp.sum(d_bsp3, axis=-1)[None], d_cw,
                   dsink[:, 0:N_HEADS], d_relb, d_final, loss_part]
    (p_winT,), (r_wd0,) = _call(_wgrad(dproj, h0b, "wgrad_in"), _ChipSumCarry(p_wd0, a_wd0))
    (p_woe,), ((a_winT,), small_slots) = _call(
        _wgrad(yb, dx1b, "wgrad_even_out"), [_PairCarry(p_winT), _BroadcastCarry(small_parts)])
    (dx0, _, d_nmix0), ((r_winT,), (r_woe,)) = _call(
        _proj_bwd_norm([dproj], x0, row(norm_mix[0]), dx1, g_winT, "in_proj_bwd", tm),
        [_ChipSumCarry(p_winT, a_winT), _GradCarry([p_woe])])

    grads = {}

    order = ["norm_mix", "norm_ffn", "even_w_in", "even_v_ln_g", "even_v_ln_b", "even_w_spatial", "even_b_spatial",
             "even_conv_w", "even_w_out", "attn_w_qkv", "attn_sink", "rel_bias", "attn_w_out", "ffn_w_gate",
             "ffn_w_up", "ffn_w_down", "final_norm"]
    ws = dict(norm_mix=norm_mix, norm_ffn=norm_ffn, even_w_in=even_w_in, even_v_ln_g=even_v_ln_g,
              even_v_ln_b=even_v_ln_b, even_w_spatial=even_w_spatial, even_b_spatial=even_b_spatial,
              even_conv_w=even_conv_w, even_w_out=even_w_out, attn_w_qkv=attn_w_qkv, attn_sink=attn_sink,
              rel_bias=rel_bias, attn_w_out=attn_w_out, ffn_w_gate=ffn_w_gate, ffn_w_up=ffn_w_up,
              ffn_w_down=ffn_w_down, final_norm=final_norm)
    ms = dict(norm_mix=m_norm_mix, norm_ffn=m_norm_ffn, even_w_in=m_even_w_in, even_v_ln_g=m_even_v_ln_g,
              even_v_ln_b=m_even_v_ln_b, even_w_spatial=m_even_w_spatial, even_b_spatial=m_even_b_spatial,
              even_conv_w=m_even_conv_w, even_w_out=m_even_w_out, attn_w_qkv=m_attn_w_qkv, attn_sink=m_attn_sink,
              rel_bias=m_rel_bias, attn_w_out=m_attn_w_out, ffn_w_gate=m_ffn_w_gate, ffn_w_up=m_ffn_w_up,
              ffn_w_down=m_ffn_w_down, final_norm=m_final_norm)
    vs = dict(norm_mix=v_norm_mix, norm_ffn=v_norm_ffn, even_w_in=v_even_w_in, even_v_ln_g=v_even_v_ln_g,
              even_v_ln_b=v_even_v_ln_b, even_w_spatial=v_even_w_spatial, even_b_spatial=v_even_b_spatial,
              even_conv_w=v_even_conv_w, even_w_out=v_even_w_out, attn_w_qkv=v_attn_w_qkv, attn_sink=v_attn_sink,
              rel_bias=v_rel_bias, attn_w_out=v_attn_w_out, ffn_w_gate=v_ffn_w_gate, ffn_w_up=v_ffn_w_up,
              ffn_w_down=v_ffn_w_down, final_norm=v_final_norm)
    big = dict(ffn_w_gate=([r_wgT0, r_wgT1], True), even_w_in=([r_winT], True), even_w_out=([r_woe], False),
               attn_w_qkv=([r_wqkvT], True), attn_w_out=([r_woa], False), ffn_w_up=([r_wuT0, r_wuT1], True),
               ffn_w_down=([r_wd0, r_wd1], False))
    delta, new_m, new_v = {}, {}, {}
    late_slots = None
    for n, (recvs, transposed) in big.items():
        lay = (lambda a: jnp.swapaxes(a, 1, 2)) if transposed else (lambda a: a)
        spec = _finish_weight(recvs, lay(ws[n]), lay(ms[n]), lay(vs[n]), "finish_" + n)
        if late_slots is None:
            outs, (late_slots,) = _call(spec, _BroadcastCarry([d_nmix0]))
        else:
            outs, _ = _call(spec)
        grads[n], delta[n], new_m[n], new_v[n] = [lay(o) for o in outs]
    shaped = lambda n, a: in_full(a) if n == "even_conv_w" else (a.reshape(1, dm) if n == "final_norm" else a)
    pick = lambda dct: [shaped(n, dct[n]) for n in small_names]
    results, loss11 = _adamw_small(pick(ws), pick(ms), pick(vs), small_slots[:-1], late_slots, small_slots[-1],
                                   "adamw_small")
    mine = lambda a: lax.dynamic_slice(a, (0, me * (bw // NDEV)), (3, bw // NDEV))[None]
    for n, res in zip(small_names, results):
        for dst, a in zip((grads, delta, new_m, new_v), res):
            dst[n] = mine(a) if n == "even_conv_w" else (a.reshape(dm) if n == "final_norm" else a)
    loss = loss11[0, 0]
    return (loss, dx0[None], *[grads[n] for n in order], *[delta[n] for n in order],
            *[new_m[n] for n in order], *[new_v[n] for n in order])
```

```python
import math

import jax
import jax.numpy as jnp
import numpy as np
from jax import lax
from jax.experimental import pallas as pl
from jax.experimental.pallas import tpu as pltpu

F32, BF16 = jnp.float32, jnp.bfloat16
NDEV = 8
EPS = 1e-6
CHUNK = 128
A_GROUPS = 4
N_HEADS, N_KV, HEAD_DIM = 16, 4, 64
N_BUCKETS, MAX_DISTANCE = 32, 128
NEG = -1e30
LOG2E = 1.4426950408889634
ADAM_LR, ADAM_B1, ADAM_B2, ADAM_EPS, ADAM_WD, ADAM_STEP = 0.001, 0.9, 0.999, 1e-08, 0.01, 10
VMEM_LIMIT = 56 * 1024 * 1024
MESH = pl.DeviceIdType.MESH
NT = (((1,), (1,)), ((), ()))
NN = (((1,), (0,)), ((), ()))
TN = (((0,), (0,)), ((), ()))
ANY = pl.BlockSpec(memory_space=pl.ANY)


def _cp(n_grid=1):
    return pltpu.CompilerParams(dimension_semantics=("arbitrary",) * n_grid, vmem_limit_bytes=VMEM_LIMIT)


def _dot(a, b, dims):
    return lax.dot_general(a, b, dims, preferred_element_type=F32)


def _my_index():
    return 4 * lax.axis_index("x") + 2 * lax.axis_index("y") + lax.axis_index("c")


def _peer(k):
    x, y, c = lax.axis_index("x"), lax.axis_index("y"), lax.axis_index("c")
    px = 1 - x if k & 4 else x
    py = 1 - y if k & 2 else y
    pc = 1 - c if k & 1 else c
    return (px, py, pc)


def _load_weight(gath_ref, wbuf, sems):
    rows = gath_ref.shape[1]
    cps = [pltpu.make_async_copy(gath_ref.at[d], wbuf.at[pl.ds(d * rows, rows), :], sems.at[d]) for d in range(NDEV)]
    for c in cps:
        c.start()
    for c in cps:
        c.wait()


class _GatherCarry:
    def __init__(self, pieces):
        self.inputs = list(pieces)
        self.n = len(pieces)
        self.out_shape = [jax.ShapeDtypeStruct((NDEV,) + p.shape, p.dtype) for p in pieces]
        self.scratch = [pltpu.SemaphoreType.DMA((7 * self.n,)), pltpu.SemaphoreType.DMA((7 * self.n,)),
                        pltpu.SemaphoreType.DMA((self.n,))]

    def _ctx(self):
        x, y, c = lax.axis_index("x"), lax.axis_index("y"), lax.axis_index("c")
        chips = [(1 - x, y), (x, 1 - y), (1 - x, 1 - y)]
        return (x, y, c), (x, y, 1 - c), chips, c

    def _copy(self, k, j, block, to, ins, outs, sems, src=None):
        send_sems, recv_sems, _ = sems
        slot = outs[j].at[4 * block[0] + 2 * block[1] + block[2]]
        return pltpu.make_async_remote_copy(
            src_ref=slot if src is None else src, dst_ref=slot, send_sem=send_sems.at[k * self.n + j],
            recv_sem=recv_sems.at[k * self.n + j], device_id=to, device_id_type=MESH)

    def start(self, ins, outs, sems):
        me, sibling, chips, c = self._ctx()
        for j in range(self.n):
            pltpu.make_async_copy(ins[j], outs[j].at[4 * me[0] + 2 * me[1] + me[2]], sems[2].at[j]).start()
            self._copy(0, j, me, sibling, ins, outs, sems, src=ins[j]).start()
            for q, chip in enumerate(chips):
                self._copy(1 + q, j, me, (*chip, c), ins, outs, sems, src=ins[j]).start()

    def mid(self, ins, outs, sems):
        me, sibling, chips, c = self._ctx()
        for q, chip in enumerate(chips):
            for j in range(self.n):
                self._copy(1 + q, j, (*chip, c), me, ins, outs, sems).wait_recv()
                self._copy(4 + q, j, (*chip, c), sibling, ins, outs, sems).start()

    def finish(self, ins, outs, sems):
        me, sibling, chips, c = self._ctx()
        for j in range(self.n):
            self._copy(0, j, sibling, me, ins, outs, sems).wait_recv()
            for q, chip in enumerate(chips):
                self._copy(4 + q, j, (*chip, 1 - c), me, ins, outs, sems).wait_recv()
        for j in range(self.n):
            self._copy(0, j, me, sibling, ins, outs, sems, src=ins[j]).wait_send()
            for q, chip in enumerate(chips):
                self._copy(1 + q, j, me, (*chip, c), ins, outs, sems, src=ins[j]).wait_send()
                self._copy(4 + q, j, (*chip, c), sibling, ins, outs, sems).wait_send()
            pltpu.make_async_copy(ins[j], outs[j].at[0], sems[2].at[j]).wait()


class _GradCarry:
    def __init__(self, pieces):
        self.inputs = list(pieces)
        self.n = len(pieces)
        self.rows = [p.shape[0] // NDEV for p in pieces]
        self.out_shape = [jax.ShapeDtypeStruct((NDEV, r, p.shape[1]), p.dtype) for p, r in zip(pieces, self.rows)]
        self.scratch = [pltpu.SemaphoreType.DMA((7 * self.n,)), pltpu.SemaphoreType.DMA((7 * self.n,)),
                        pltpu.SemaphoreType.DMA((self.n,))]

    def _copies(self, ins, outs, sems):
        me = _my_index()
        local, remote = [], []
        for j in range(self.n):
            r = self.rows[j]
            local.append(pltpu.make_async_copy(ins[j].at[pl.ds(pl.multiple_of(me * r, 16), r), :], outs[j].at[me],
                                               sems[2].at[j]))
            for k in range(1, NDEV):
                peer = _peer(k)
                pidx = 4 * peer[0] + 2 * peer[1] + peer[2]
                remote.append(pltpu.make_async_remote_copy(
                    src_ref=ins[j].at[pl.ds(pl.multiple_of(pidx * r, 16), r), :], dst_ref=outs[j].at[me],
                    send_sem=sems[0].at[(k - 1) * self.n + j], recv_sem=sems[1].at[(k - 1) * self.n + j],
                    device_id=peer, device_id_type=MESH))
        return local, remote

    def start(self, ins, outs, sems):
        local, remote = self._copies(ins, outs, sems)
        for cp in local + remote:
            cp.start()

    def mid(self, ins, outs, sems):
        pass

    def finish(self, ins, outs, sems):
        local, remote = self._copies(ins, outs, sems)
        for cp in remote + local:
            cp.wait()


class _BroadcastCarry:
    def __init__(self, parts):
        self.inputs = list(parts)
        self.n = len(self.inputs)
        self.out_shape = [jax.ShapeDtypeStruct((NDEV,) + p.shape, p.dtype) for p in self.inputs]
        self.scratch = [pltpu.SemaphoreType.DMA((7 * self.n,)), pltpu.SemaphoreType.DMA((7 * self.n,)),
                        pltpu.SemaphoreType.DMA((self.n,))]

    def _copies(self, ins, outs, sems):
        me = _my_index()
        cps = []
        for j in range(self.n):
            cps.append(pltpu.make_async_copy(ins[j], outs[j].at[me], sems[2].at[j]))
            cps += [pltpu.make_async_remote_copy(
                src_ref=ins[j], dst_ref=outs[j].at[me], send_sem=sems[0].at[(k - 1) * self.n + j],
                recv_sem=sems[1].at[(k - 1) * self.n + j], device_id=_peer(k), device_id_type=MESH)
                for k in range(1, NDEV)]
        return cps

    def start(self, ins, outs, sems):
        for cp in self._copies(ins, outs, sems):
            cp.start()

    def mid(self, ins, outs, sems):
        pass

    def finish(self, ins, outs, sems):
        for cp in self._copies(ins, outs, sems):
            cp.wait()


class _PairCarry:
    def __init__(self, piece):
        self.inputs = [piece]
        self.r = piece.shape[0] // NDEV
        self.out_shape = [jax.ShapeDtypeStruct((4, self.r, piece.shape[1]), piece.dtype)]
        self.scratch = [pltpu.SemaphoreType.DMA((4,)), pltpu.SemaphoreType.DMA((4,))]

    def _copies(self, ins, outs, sems):
        x, y, c = lax.axis_index("x"), lax.axis_index("y"), lax.axis_index("c")
        return [pltpu.make_async_remote_copy(
            src_ref=ins[0].at[pl.ds(pl.multiple_of((2 * q + 1 - c) * self.r, 16), self.r), :], dst_ref=outs[0].at[q],
            send_sem=sems[0].at[q], recv_sem=sems[1].at[q], device_id=(x, y, 1 - c), device_id_type=MESH)
            for q in range(4)]

    def start(self, ins, outs, sems):
        for cp in self._copies(ins, outs, sems):
            cp.start()

    def mid(self, ins, outs, sems):
        pass

    def finish(self, ins, outs, sems):
        for cp in self._copies(ins, outs, sems):
            cp.wait()


class _ChipSumCarry:
    def __init__(self, piece, landed):
        self.inputs = [piece, landed]
        self.r, dm = piece.shape[0] // NDEV, piece.shape[1]
        self.out_shape = [jax.ShapeDtypeStruct((4, self.r, dm), piece.dtype)]
        self.scratch = [pltpu.VMEM((4, self.r, dm), piece.dtype), pltpu.VMEM((8, self.r, dm), piece.dtype),
                        pltpu.SemaphoreType.DMA((8,)), pltpu.SemaphoreType.DMA((3,)), pltpu.SemaphoreType.DMA((3,)),
                        pltpu.SemaphoreType.DMA(())]

    def _copies(self, outs, scr):
        sums, _, _, send_sems, recv_sems, local_sem = scr
        x, y, c = lax.axis_index("x"), lax.axis_index("y"), lax.axis_index("c")
        mine = 2 * x + y
        local = pltpu.make_async_copy(sums.at[mine], outs[0].at[mine], local_sem)
        remote = []
        for k in range(1, 4):
            px = 1 - x if k & 2 else x
            py = 1 - y if k & 1 else y
            remote.append(pltpu.make_async_remote_copy(
                src_ref=sums.at[2 * px + py], dst_ref=outs[0].at[mine], send_sem=send_sems.at[k - 1],
                recv_sem=recv_sems.at[k - 1], device_id=(px, py, c), device_id_type=MESH))
        return local, remote

    def start(self, ins, outs, scr):
        sums, stage, stage_sems = scr[0], scr[1], scr[2]
        c = lax.axis_index("c")
        loads = []
        for q in range(4):
            loads.append((
                pltpu.make_async_copy(ins[0].at[pl.ds(pl.multiple_of((2 * q + c) * self.r, 16), self.r), :],
                                      stage.at[2 * q], stage_sems.at[2 * q]),
                pltpu.make_async_copy(ins[1].at[q], stage.at[2 * q + 1], stage_sems.at[2 * q + 1])))
        for a, b in loads:
            a.start()
            b.start()
        for q, (a, b) in enumerate(loads):
            a.wait()
            b.wait()
            sums[q] = (stage[2 * q].astype(F32) + stage[2 * q + 1].astype(F32)).astype(sums.dtype)
        local, remote = self._copies(outs, scr)
        for cp in [local] + remote:
            cp.start()

    def mid(self, ins, outs, scr):
        pass

    def finish(self, ins, outs, scr):
        local, remote = self._copies(outs, scr)
        for cp in remote + [local]:
            cp.wait()


def _call(spec, carry=None):
    body, grid = spec["body"], spec["grid"]
    in_specs, out_specs, out_shape = list(spec["in_specs"]), list(spec["out_specs"]), list(spec["out_shape"])
    scratch, args = list(spec.get("scratch", [])), list(spec["args"])
    if carry is None:
        out = pl.pallas_call(body, grid=grid, in_specs=in_specs, out_specs=tuple(out_specs),
                             out_shape=tuple(out_shape), scratch_shapes=scratch, compiler_params=_cp(len(grid)),
                             name=spec["name"])(*args)
        return tuple(out), ()
    carries = list(carry) if isinstance(carry, (list, tuple)) else [carry]
    n_in, n_out, n_s = len(in_specs), len(out_specs), len(scratch)
    steps = int(np.prod(grid))

    def split(refs, counts):
        parts, o = [], 0
        for cnt in counts:
            parts.append(refs[o:o + cnt])
            o += cnt
        return parts

    c_in = [len(cr.inputs) for cr in carries]
    c_out = [len(cr.out_shape) for cr in carries]
    c_scr = [len(cr.scratch) for cr in carries]

    def wrapped(*refs):
        ins, cins, outs, couts, scr, cscr = split(refs, [n_in, sum(c_in), n_out, sum(c_out), n_s, sum(c_scr)])
        per = list(zip(carries, split(cins, c_in), split(couts, c_out), split(cscr, c_scr)))
        step = pl.program_id(0)
        for ax in range(1, len(grid)):
            step = step * grid[ax] + pl.program_id(ax)

        @pl.when(step == 0)
        def _():
            for cr, ci, co, cs in per:
                cr.start(ci, co, cs)
        if steps >= 3:
            @pl.when(step == steps - 2)
            def _():
                for cr, ci, co, cs in per:
                    cr.mid(ci, co, cs)
        body(*ins, *outs, *scr)

        @pl.when(step == steps - 1)
        def _():
            for cr, ci, co, cs in per:
                if steps < 3:
                    cr.mid(ci, co, cs)
                cr.finish(ci, co, cs)

    out = pl.pallas_call(
        wrapped, grid=grid, in_specs=in_specs + [ANY] * sum(c_in), out_specs=tuple(out_specs + [ANY] * sum(c_out)),
        out_shape=tuple(out_shape + [s for cr in carries for s in cr.out_shape]),
        scratch_shapes=scratch + [s for cr in carries for s in cr.scratch],
        compiler_params=_cp(len(grid)), name=spec["name"])(*args, *[a for cr in carries for a in cr.inputs])
    c_res = [tuple(p) for p in split(out[n_out:], c_out)]
    return tuple(out[:n_out]), (c_res if isinstance(carry, (list, tuple)) else c_res[0])


def _rms_fwd(x, gain):
    r = lax.rsqrt(jnp.mean(x * x, axis=-1, keepdims=True) + EPS)
    return x * r * gain, r


def _rms_bwd(dh, x, r, gain):
    a = dh * gain
    dx = r * a - x * (r * r * r) * jnp.mean(a * x, axis=-1, keepdims=True)
    dgain = jnp.sum(dh * (x * r), axis=0, keepdims=True)
    return dx, dgain


def _gelu(x):
    return 0.5 * x * (1.0 + lax.erf(x * 0.7071067811865476))


def _gelu_grad(x):
    return 0.5 * (1.0 + lax.erf(x * 0.7071067811865476)) + x * jnp.exp(-0.5 * x * x) * 0.3989422804014327


def _sigmoid(x):
    return 1.0 / (1.0 + jnp.exp(-x))


def _adamw_math(w, g, m, v):
    nm = ADAM_B1 * m + (1.0 - ADAM_B1) * g
    nv = ADAM_B2 * v + (1.0 - ADAM_B2) * (g * g)
    m_hat = nm / (1.0 - ADAM_B1 ** ADAM_STEP)
    v_hat = nv / (1.0 - ADAM_B2 ** ADAM_STEP)
    return -ADAM_LR * (m_hat / (jnp.sqrt(v_hat) + ADAM_EPS) + ADAM_WD * w), nm, nv


def _tok(tm, w):
    return pl.BlockSpec((tm, w), lambda i: (i, 0))


def _full(shape):
    return pl.BlockSpec(shape, lambda *i: (0,) * len(shape))


def _norm_proj(x, gain, gath, out_dtype, name, tm):
    t, dm = x.shape
    n = gath.shape[1] * NDEV

    def body(x_ref, g_ref, gath_ref, proj_ref, hb_ref, wbuf, sems):
        @pl.when(pl.program_id(0) == 0)
        def _():
            _load_weight(gath_ref, wbuf, sems)
        h, _ = _rms_fwd(x_ref[...], g_ref[...])
        hb = h.astype(BF16)
        hb_ref[...] = hb
        proj_ref[...] = _dot(hb, wbuf[...], NT).astype(out_dtype)

    return dict(
        body=body, grid=(t // tm,), name=name, args=[x, gain, gath],
        out_shape=[jax.ShapeDtypeStruct((t, n), out_dtype), jax.ShapeDtypeStruct((t, dm), BF16)],
        in_specs=[_tok(tm, dm), _full((1, dm)), ANY], out_specs=[_tok(tm, n), _tok(tm, dm)],
        scratch=[pltpu.VMEM((n, dm), BF16), pltpu.SemaphoreType.DMA((NDEV,))])


def _proj_bwd_norm(dys, x, gain, dres, gath, name, tm):
    t, dm = x.shape
    n = gath.shape[1] * NDEV
    widths = [d.shape[1] for d in dys]
    assert sum(widths) == n
    nd = len(dys)

    def body(*refs):
        dy_refs = refs[:nd]
        x_ref, g_ref, dres_ref, gath_ref, dx_ref, dxb_ref, dgain_ref, wbuf, sems = refs[nd:]

        @pl.when(pl.program_id(0) == 0)
        def _():
            _load_weight(gath_ref, wbuf, sems)
            dgain_ref[...] = jnp.zeros_like(dgain_ref)
        xv, gain_v = x_ref[...], g_ref[...]
        _, r = _rms_fwd(xv, gain_v)
        dh, c0 = None, 0
        for dy_ref, wd in zip(dy_refs, widths):
            part = _dot(dy_ref[...], wbuf[c0:c0 + wd, :], NN)
            dh = part if dh is None else dh + part
            c0 += wd
        dx, dgain = _rms_bwd(dh, xv, r, gain_v)
        dx = dres_ref[...] + dx
        dx_ref[...] = dx
        dxb_ref[...] = dx.astype(BF16)
        dgain_ref[...] += dgain

    return dict(
        body=body, grid=(t // tm,), name=name, args=[*dys, x, gain, dres, gath],
        out_shape=[jax.ShapeDtypeStruct((t, dm), F32), jax.ShapeDtypeStruct((t, dm), BF16),
                   jax.ShapeDtypeStruct((1, dm), F32)],
        in_specs=[_tok(tm, wd) for wd in widths] + [_tok(tm, dm), _full((1, dm)), _tok(tm, dm), ANY],
        out_specs=[_tok(tm, dm), _tok(tm, dm), _full((1, dm))],
        scratch=[pltpu.VMEM((n, dm), BF16), pltpu.SemaphoreType.DMA((NDEV,))])


def _wgrad(a, b, name, tmm=256):
    parts = list(a) if isinstance(a, (list, tuple)) else [a]
    t = parts[0].shape[0]
    n = b.shape[1]
    tiles = [p.shape[1] // tmm for p in parts]
    first = [sum(tiles[:i]) for i in range(len(parts))]
    m = sum(tiles) * tmm

    def body(*refs):
        a_refs, b_ref, o_ref = refs[:len(parts)], refs[len(parts)], refs[len(parts) + 1]
        j = pl.program_id(0)
        for a_ref, j0, nt in zip(a_refs, first, tiles):
            if len(parts) == 1:
                o_ref[...] = _dot(a_ref[...], b_ref[...], TN).astype(BF16)
            else:
                @pl.when((j >= j0) & (j < j0 + nt))
                def _():
                    o_ref[...] = _dot(a_ref[...], b_ref[...], TN).astype(BF16)

    a_specs = [pl.BlockSpec((t, tmm), lambda j, j0=j0, nt=nt: (0, jnp.clip(j - j0, 0, nt - 1)))
               for j0, nt in zip(first, tiles)]
    return dict(
        body=body, grid=(sum(tiles),), name=name, args=[*parts, b], out_shape=[jax.ShapeDtypeStruct((m, n), BF16)],
        in_specs=a_specs + [pl.BlockSpec((t, n), lambda j: (0, 0))],
        out_specs=[pl.BlockSpec((tmm, n), lambda j: (j, 0))])


def _halo_specs(tm, t, width, col_blocks):
    nb8 = tm // 8
    last = t // 8 - 1
    prev = [pl.BlockSpec((8, width), lambda i, cb=cb: (jnp.maximum(i * nb8 - 1, 0), cb)) for cb in col_blocks]
    nxt = [pl.BlockSpec((8, width), lambda i, cb=cb: (jnp.minimum((i + 1) * nb8, last), cb)) for cb in col_blocks]
    return prev, nxt


def _shift_rows(z, prev_row, next_row):
    tm = z.shape[0]
    row = lax.broadcasted_iota(jnp.int32, z.shape, 0)
    zm1 = jnp.where(row == 0, prev_row, pltpu.roll(z, 1, 0))
    zp1 = jnp.where(row == tm - 1, next_row, pltpu.roll(z, tm - 1, 0))
    return zm1, zp1


def _gating_fwd(proj, lng, lnb, wsp_ref, bsp_ref, aw):
    tm = proj.shape[0]
    a_u = _gelu(proj[:, 0:aw])
    gv = _gelu(proj[:, aw:2 * aw])
    mu = jnp.mean(gv, axis=-1, keepdims=True)
    xc = gv - mu
    rstd = lax.rsqrt(jnp.mean(xc * xc, axis=-1, keepdims=True) + EPS)
    vn = xc * rstd
    a_v = (vn * lng + lnb).astype(BF16)
    gd = aw // A_GROUPS
    rows = []
    for c in range(tm // CHUNK):
        cols = []
        for g in range(A_GROUPS):
            blk = a_v[c * CHUNK:(c + 1) * CHUNK, g * gd:(g + 1) * gd]
            cols.append(_dot(wsp_ref[g], blk, NN) + bsp_ref[g])
        rows.append(jnp.concatenate(cols, axis=1))
    mixed = jnp.concatenate(rows, axis=0)
    return a_u, vn, rstd, a_v, mixed


def _even_core_fwd(proj, x0, lng, lnb, wsp, bspb, cw, gath, tm):
    t, dm = x0.shape
    aw = lng.shape[1]
    bw = cw.shape[1]
    assert aw == bw and 2 * aw + 3 * bw == proj.shape[1]
    nt = t // tm
    prev, nxt = _halo_specs(tm, t, bw, [3, 4])

    def body(proj_ref, cp_ref, hp_ref, cn_ref, hn_ref, x0_ref, lng_ref, lnb_ref, wsp_ref, bsp_ref, cw_ref, gath_ref,
             x1_ref, y_ref, wbuf, sems):
        i = pl.program_id(0)

        @pl.when(i == 0)
        def _():
            _load_weight(gath_ref, wbuf, sems)
        proj_v = proj_ref[...]
        a_u, _, _, _, mixed = _gating_fwd(proj_v, lng_ref[...], lnb_ref[...], wsp_ref, bsp_ref, aw)
        a_out = a_u * mixed
        bb = proj_v[:, 2 * aw:2 * aw + bw]
        z = proj_v[:, 2 * aw + bw:2 * aw + 2 * bw] * proj_v[:, 2 * aw + 2 * bw:]
        zprev = jnp.where(i > 0, cp_ref[7:8, :] * hp_ref[7:8, :], 0.0)
        znext = jnp.where(i < nt - 1, cn_ref[0:1, :] * hn_ref[0:1, :], 0.0)
        zm1, zp1 = _shift_rows(z, zprev, znext)
        cwv = cw_ref[...]
        conv = zm1 * cwv[0:1, :] + z * cwv[1:2, :] + zp1 * cwv[2:3, :]
        y = jnp.concatenate([a_out, bb * conv], axis=1).astype(BF16)
        y_ref[...] = y
        x1_ref[...] = x0_ref[...] + _dot(y, wbuf[...], NN)

    return dict(
        body=body, grid=(nt,), name="even_core_fwd",
        args=[proj, proj, proj, proj, proj, x0, lng, lnb, wsp, bspb, cw, gath],
        out_shape=[jax.ShapeDtypeStruct((t, dm), F32), jax.ShapeDtypeStruct((t, aw + bw), BF16)],
        in_specs=[_tok(tm, proj.shape[1]), prev[0], prev[1], nxt[0], nxt[1], _tok(tm, dm), _full(lng.shape),
                  _full(lnb.shape), _full(wsp.shape), _full(bspb.shape), _full(cw.shape), ANY],
        out_specs=[_tok(tm, dm), _tok(tm, aw + bw)],
        scratch=[pltpu.VMEM((gath.shape[1] * NDEV, dm), BF16), pltpu.SemaphoreType.DMA((NDEV,))])


def _even_core_bwd(proj, dx1, lng, lnb, wsp, bspb, cw, gath, tm):
    t, dm = dx1.shape
    aw, bw = lng.shape[1], cw.shape[1]
    gd = aw // A_GROUPS
    nt = t // tm
    inw = proj.shape[1]
    prev, nxt = _halo_specs(tm, t, bw, [2, 3, 4])
    nb8 = tm // 8
    last8 = t // 8 - 1

    def body(proj_ref, bp_ref, cp_ref, hp_ref, bn_ref, cn_ref, hn_ref, dx_ref, dxp_ref, dxn_ref,
             lng_ref, lnb_ref, wsp_ref, bsp_ref, cw_ref, gath_ref,
             dproj_ref, dlng_ref, dlnb_ref, dwsp_ref, dbsp_ref, dcw_ref, wbuf, sems):
        i = pl.program_id(0)

        @pl.when(i == 0)
        def _():
            _load_weight(gath_ref, wbuf, sems)
            dlng_ref[...] = jnp.zeros_like(dlng_ref)
            dlnb_ref[...] = jnp.zeros_like(dlnb_ref)
            dwsp_ref[...] = jnp.zeros_like(dwsp_ref)
            dbsp_ref[...] = jnp.zeros_like(dbsp_ref)
            dcw_ref[...] = jnp.zeros_like(dcw_ref)
        proj_v = proj_ref[...]
        lng_v = lng_ref[...]
        a_u, vn, rstd, a_v, mixed = _gating_fwd(proj_v, lng_v, lnb_ref[...], wsp_ref, bsp_ref, aw)
        w = wbuf[...]
        dy = _dot(dx_ref[...].astype(BF16), w, NT)
        da_out, db_out = dy[:, 0:aw], dy[:, aw:]
        da_u = da_out * mixed
        dmixed = da_out * a_u
        dmb = dmixed.astype(BF16)
        rows = []
        for c in range(tm // CHUNK):
            cols = []
            for g in range(A_GROUPS):
                r0, c0 = c * CHUNK, g * gd
                dm_cg = dmb[r0:r0 + CHUNK, c0:c0 + gd]
                cols.append(_dot(wsp_ref[g], dm_cg, TN))
                dwsp_ref[g] += _dot(dm_cg, a_v[r0:r0 + CHUNK, c0:c0 + gd], NT)
                dbsp_ref[g] += dmixed[r0:r0 + CHUNK, c0:c0 + gd]
            rows.append(jnp.concatenate(cols, axis=1))
        dav = jnp.concatenate(rows, axis=0)
        dlng_ref[...] += jnp.sum(dav * vn, axis=0, keepdims=True)
        dlnb_ref[...] += jnp.sum(dav, axis=0, keepdims=True)
        dvn = dav * lng_v
        dgv = rstd * (dvn - jnp.mean(dvn, axis=-1, keepdims=True) - vn * jnp.mean(dvn * vn, axis=-1, keepdims=True))
        dv_pre = dgv * _gelu_grad(proj_v[:, aw:2 * aw])
        du_pre = da_u * _gelu_grad(proj_v[:, 0:aw])
        bb = proj_v[:, 2 * aw:2 * aw + bw]
        bc = proj_v[:, 2 * aw + bw:2 * aw + 2 * bw]
        bh = proj_v[:, 2 * aw + 2 * bw:]
        z = bc * bh
        zprev = jnp.where(i > 0, cp_ref[7:8, :] * hp_ref[7:8, :], 0.0)
        znext = jnp.where(i < nt - 1, cn_ref[0:1, :] * hn_ref[0:1, :], 0.0)
        zm1, zp1 = _shift_rows(z, zprev, znext)
        cwv = cw_ref[...]
        conv = zm1 * cwv[0:1, :] + z * cwv[1:2, :] + zp1 * cwv[2:3, :]
        dbb = db_out * conv
        dconv = db_out * bb
        dx_edge = jnp.concatenate([dxp_ref[...], dxn_ref[...]], axis=0).astype(BF16)
        dy_edge = _dot(dx_edge, w[aw:, :], NT)
        dcprev = jnp.where(i > 0, dy_edge[7:8, :] * bp_ref[7:8, :], 0.0)
        dcnext = jnp.where(i < nt - 1, dy_edge[8:9, :] * bn_ref[0:1, :], 0.0)
        dcm1, dcp1 = _shift_rows(dconv, dcprev, dcnext)
        dz = dcp1 * cwv[0:1, :] + dconv * cwv[1:2, :] + dcm1 * cwv[2:3, :]
        dcw_ref[0:1, :] += jnp.sum(dconv * zm1, axis=0, keepdims=True)
        dcw_ref[1:2, :] += jnp.sum(dconv * z, axis=0, keepdims=True)
        dcw_ref[2:3, :] += jnp.sum(dconv * zp1, axis=0, keepdims=True)
        dproj_ref[...] = jnp.concatenate([du_pre, dv_pre, dbb, dz * bh, dz * bc], axis=1).astype(BF16)

    row8 = lambda f: pl.BlockSpec((8, dm), f)
    return dict(
        body=body, grid=(nt,), name="even_core_bwd",
        args=[proj, proj, proj, proj, proj, proj, proj, dx1, dx1, dx1, lng, lnb, wsp, bspb, cw, gath],
        out_shape=[jax.ShapeDtypeStruct((t, inw), BF16), jax.ShapeDtypeStruct((1, aw), F32),
                   jax.ShapeDtypeStruct((1, aw), F32), jax.ShapeDtypeStruct(wsp.shape, F32),
                   jax.ShapeDtypeStruct((A_GROUPS, CHUNK, gd), F32), jax.ShapeDtypeStruct(cw.shape, F32)],
        in_specs=[_tok(tm, inw), prev[0], prev[1], prev[2], nxt[0], nxt[1], nxt[2], _tok(tm, dm),
                  row8(lambda i: (jnp.maximum(i * nb8 - 1, 0), 0)), row8(lambda i: (jnp.minimum((i + 1) * nb8, last8), 0)),
                  _full(lng.shape), _full(lnb.shape), _full(wsp.shape), _full(bspb.shape), _full(cw.shape), ANY],
        out_specs=[_tok(tm, inw), _full((1, aw)), _full((1, aw)), _full(wsp.shape),
                   _full((A_GROUPS, CHUNK, gd)), _full(cw.shape)],
        scratch=[pltpu.VMEM((gath.shape[1] * NDEV, dm), BF16), pltpu.SemaphoreType.DMA((NDEV,))])


def _ff_chunks(f, width=1024):
    return [(c0, min(c0 + width, f)) for c0 in range(0, f, width)]


def _ffn_up(x, gain, gath_g, gath_u, name, tm):
    t, dm = x.shape
    f = gath_g.shape[1] * NDEV

    def body(x_ref, g_ref, gg_ref, gu_ref, gate_ref, up_ref, act_ref, wg, wu, sems):
        @pl.when(pl.program_id(0) == 0)
        def _():
            _load_weight(gg_ref, wg, sems)
            _load_weight(gu_ref, wu, sems)
        h, _ = _rms_fwd(x_ref[...], g_ref[...])
        hb = h.astype(BF16)
        for c0, c1 in _ff_chunks(f):
            gate = _dot(hb, wg[c0:c1, :], NT)
            up = _dot(hb, wu[c0:c1, :], NT)
            gate_ref[:, c0:c1] = gate.astype(BF16)
            up_ref[:, c0:c1] = up.astype(BF16)
            act_ref[:, c0:c1] = (gate * _sigmoid(gate) * up).astype(BF16)

    o = jax.ShapeDtypeStruct((t, f), BF16)
    return dict(
        body=body, grid=(t // tm,), name=name, args=[x, gain, gath_g, gath_u], out_shape=[o, o, o],
        in_specs=[_tok(tm, dm), _full((1, dm)), ANY, ANY], out_specs=[_tok(tm, f)] * 3,
        scratch=[pltpu.VMEM((f, dm), BF16), pltpu.VMEM((f, dm), BF16), pltpu.SemaphoreType.DMA((NDEV,))])


def _ffn_down(x, act, gath_d, name, tm):
    t, dm = x.shape
    f = act.shape[1]

    def body(x_ref, a_ref, gd_ref, xo_ref, wd, sems):
        @pl.when(pl.program_id(0) == 0)
        def _():
            _load_weight(gd_ref, wd, sems)
        xo_ref[...] = x_ref[...] + _dot(a_ref[...], wd[...], NN)

    return dict(
        body=body, grid=(t // tm,), name=name, args=[x, act, gath_d], out_shape=[jax.ShapeDtypeStruct((t, dm), F32)],
        in_specs=[_tok(tm, dm), _tok(tm, f), ANY], out_specs=[_tok(tm, dm)],
        scratch=[pltpu.VMEM((f, dm), BF16), pltpu.SemaphoreType.DMA((NDEV,))])


def _ffn_down_loss(x, act, gath_d, target, gain, name, tm):
    t, dm = x.shape
    f = act.shape[1]
    steps = t // tm

    def body(x_ref, a_ref, gd_ref, t_ref, g_ref, loss_ref, dx_ref, dxb_ref, dgain_ref, wd, acc, sems):
        i = pl.program_id(0)

        @pl.when(i == 0)
        def _():
            _load_weight(gd_ref, wd, sems)
            acc[...] = jnp.zeros_like(acc)
            dgain_ref[...] = jnp.zeros_like(dgain_ref)
        xv = x_ref[...] + _dot(a_ref[...], wd[...], NN)
        gain_v = g_ref[...]
        y, r = _rms_fwd(xv, gain_v)
        e = y - t_ref[...]
        acc[...] += jnp.sum(e * e, axis=0, keepdims=True)
        dx, dgain = _rms_bwd(e * (1.0 / dm), xv, r, gain_v)
        dx_ref[...] = dx
        dxb_ref[...] = dx.astype(BF16)
        dgain_ref[...] += dgain

        @pl.when(i == steps - 1)
        def _():
            loss_ref[...] = jnp.sum(acc[...], axis=-1, keepdims=True) * (0.5 / dm)

    return dict(
        body=body, grid=(steps,), name=name, args=[x, act, gath_d, target, gain],
        out_shape=[jax.ShapeDtypeStruct((1, 1), F32), jax.ShapeDtypeStruct((t, dm), F32),
                   jax.ShapeDtypeStruct((t, dm), BF16), jax.ShapeDtypeStruct((1, dm), F32)],
        in_specs=[_tok(tm, dm), _tok(tm, f), ANY, _tok(tm, dm), _full((1, dm))],
        out_specs=[_full((1, 1)), _tok(tm, dm), _tok(tm, dm), _full((1, dm))],
        scratch=[pltpu.VMEM((f, dm), BF16), pltpu.VMEM((1, dm), F32), pltpu.SemaphoreType.DMA((NDEV,))])


def _ffn_bwd(dxo, x, gate, up, gain, gath_g, gath_u, gath_d, name, tm):
    t, dm = x.shape
    f = gate.shape[1]

    def body(dxo_ref, x_ref, gate_ref, up_ref, g_ref, gg_ref, gu_ref, gd_ref,
             dx_ref, dxb_ref, dg_ref, du_ref, hb_ref, dgain_ref, wg, wu, wd, sems):
        @pl.when(pl.program_id(0) == 0)
        def _():
            _load_weight(gg_ref, wg, sems)
            _load_weight(gu_ref, wu, sems)
            _load_weight(gd_ref, wd, sems)
            dgain_ref[...] = jnp.zeros_like(dgain_ref)
        xv, gain_v, dxo_v = x_ref[...], g_ref[...], dxo_ref[...]
        h, r = _rms_fwd(xv, gain_v)
        hb_ref[...] = h.astype(BF16)
        dxob = dxo_v.astype(BF16)
        dh = jnp.zeros_like(xv)
        for c0, c1 in _ff_chunks(f):
            gate_v = gate_ref[:, c0:c1].astype(F32)
            up_v = up_ref[:, c0:c1].astype(F32)
            s = _sigmoid(gate_v)
            silu = gate_v * s
            dact = _dot(dxob, wd[c0:c1, :], NT)
            dg = (dact * up_v * (s * (1.0 + gate_v * (1.0 - s)))).astype(BF16)
            du = (dact * silu).astype(BF16)
            dg_ref[:, c0:c1] = dg
            du_ref[:, c0:c1] = du
            dh = dh + _dot(dg, wg[c0:c1, :], NN) + _dot(du, wu[c0:c1, :], NN)
        dx, dgain = _rms_bwd(dh, xv, r, gain_v)
        dx = dxo_v + dx
        dx_ref[...] = dx
        dxb_ref[...] = dx.astype(BF16)
        dgain_ref[...] += dgain

    return dict(
        body=body, grid=(t // tm,), name=name, args=[dxo, x, gate, up, gain, gath_g, gath_u, gath_d],
        out_shape=[jax.ShapeDtypeStruct((t, dm), F32), jax.ShapeDtypeStruct((t, dm), BF16),
                   jax.ShapeDtypeStruct((t, f), BF16), jax.ShapeDtypeStruct((t, f), BF16),
                   jax.ShapeDtypeStruct((t, dm), BF16), jax.ShapeDtypeStruct((1, dm), F32)],
        in_specs=[_tok(tm, dm), _tok(tm, dm), _tok(tm, f), _tok(tm, f), _full((1, dm)), ANY, ANY, ANY],
        out_specs=[_tok(tm, dm), _tok(tm, dm), _tok(tm, f), _tok(tm, f), _tok(tm, dm), _full((1, dm))],
        scratch=[pltpu.VMEM((f, dm), BF16), pltpu.VMEM((f, dm), BF16), pltpu.VMEM((f, dm), BF16),
                 pltpu.SemaphoreType.DMA((NDEV,))])


def _t5_buckets(rel):
    nb = N_BUCKETS // 2
    ret = jnp.where(rel > 0, nb, 0)
    n = jnp.abs(rel)
    max_exact = nb // 2
    nf = jnp.maximum(n, 1).astype(jnp.float32)
    large = max_exact + (jnp.log(nf / max_exact) / math.log(MAX_DISTANCE / max_exact)
                         * (nb - max_exact)).astype(jnp.int32)
    large = jnp.minimum(large, nb - 1)
    return ret + jnp.where(n < max_exact, n, large)


def _bucket_table():
    qi = jnp.arange(CHUNK, dtype=jnp.int32)[:, None]
    kj = jnp.arange(3 * CHUNK, dtype=jnp.int32)[None, :]
    rel = kj - CHUNK - qi
    return jnp.where(jnp.abs(rel) <= CHUNK, _t5_buckets(rel), -1)


def _bias_table(rel_bias_t, buckets):
    nh = rel_bias_t.shape[0]

    def body(rb_ref, bk_ref, o_ref):
        bk = bk_ref[...]
        for h in range(nh):
            acc = jnp.where(bk < 0, NEG, 0.0).astype(F32)
            for b in range(N_BUCKETS):
                acc = jnp.where(bk == b, rb_ref[h, b] * LOG2E, acc)
            o_ref[h] = acc

    return dict(
        body=body, grid=(1,), name="bias_table", args=[rel_bias_t, buckets],
        out_shape=[jax.ShapeDtypeStruct((nh,) + buckets.shape, F32)],
        in_specs=[pl.BlockSpec(memory_space=pltpu.SMEM), _full(buckets.shape)],
        out_specs=[_full((nh,) + buckets.shape)])


def _rel_bias_grad(dbias, buckets):
    nh = dbias.shape[0]

    def body(db_ref, bk_ref, o_ref):
        bk = bk_ref[...]
        lane = lax.broadcasted_iota(jnp.int32, (1, 128), 1)
        for h in range(nh):
            d = db_ref[h]
            row = jnp.zeros((1, 128), F32)
            for b in range(N_BUCKETS):
                s = jnp.sum(jnp.sum(jnp.where(bk == b, d, 0.0), axis=1, keepdims=True), axis=0, keepdims=True)
                row = jnp.where(lane == b, s, row)
            o_ref[h:h + 1, :] = row

    return pl.pallas_call(
        body, out_shape=jax.ShapeDtypeStruct((nh, 128), F32),
        in_specs=[pl.BlockSpec(memory_space=pltpu.VMEM), pl.BlockSpec(memory_space=pltpu.VMEM)],
        out_specs=pl.BlockSpec(memory_space=pltpu.VMEM), compiler_params=_cp(0), name="rel_bias_grad")(dbias, buckets)


def _half_masks():
    lane = lax.broadcasted_iota(jnp.int32, (CHUNK, 128), 1)
    return lane < HEAD_DIM, lane >= HEAD_DIM


def _kv_low(ref, starts, hk, lo):
    kt = (hk // 2) * 128
    out = []
    for jj in range(3):
        blk = ref[pl.ds(starts[jj], CHUNK), kt:kt + 128]
        if hk % 2 == 1:
            blk = pltpu.roll(blk, HEAD_DIM, 1)
        out.append(jnp.where(lo, blk, jnp.zeros_like(blk)))
    return out


def _stack_heads(tile_a, tile_b):
    return jnp.concatenate([tile_a, pltpu.roll(tile_a, HEAD_DIM, 1), tile_b, pltpu.roll(tile_b, HEAD_DIM, 1)], axis=0)


def _unstack_heads(o4):
    return (o4[0:CHUNK] + pltpu.roll(o4[CHUNK:2 * CHUNK], HEAD_DIM, 1),
            o4[2 * CHUNK:3 * CHUNK] + pltpu.roll(o4[3 * CHUNK:], HEAD_DIM, 1))


ATT_SLAB = 32


def _softmax_slab(s_scr, hk, g, r0, bias_ref, sink_ref, n, nblk):
    scale = HEAD_DIM ** -0.5 * LOG2E
    h = (N_HEADS // N_KV) * hk + g
    s = []
    for jj in range(3):
        sj = (s_scr[hk, jj, pl.ds(g * CHUNK + r0, ATT_SLAB), :] * scale
              + bias_ref[h, pl.ds(r0, ATT_SLAB), jj * CHUNK:(jj + 1) * CHUNK])
        if jj == 0:
            sj = jnp.where(n > 0, sj, NEG)
        if jj == 2:
            sj = jnp.where(n < nblk - 1, sj, NEG)
        s.append(sj)
    sink = sink_ref[h] * LOG2E
    m = jnp.maximum(jnp.max(jnp.maximum(jnp.maximum(s[0], s[1]), s[2]), axis=-1, keepdims=True), sink)
    e = [jnp.exp2(sj - m) for sj in s]
    es = jnp.exp2(sink - m)
    inv = 1.0 / (jnp.sum(e[0] + e[1] + e[2], axis=-1, keepdims=True) + es)
    return [ej * inv for ej in e], es * inv


def _key_block_starts(n, nblk):
    return [pl.multiple_of(jnp.clip(n - 1 + jj, 0, nblk - 1) * CHUNK, CHUNK) for jj in range(3)]


def _attn_fwd(qkv, x2, bias, sink, gath):
    t, dm = x2.shape
    nblk = t // CHUNK
    kvw = N_KV * HEAD_DIM
    kcb, vcb = dm // kvw, dm // kvw + 1
    slab = (N_KV, 3, 4 * CHUNK, CHUNK)

    def body(q_ref, k_ref, v_ref, x2_ref, bias_ref, sink_ref, gath_ref, x3_ref, att_ref, p_ref, ps_ref,
             wbuf, s_scr, sems):
        n = pl.program_id(0)

        @pl.when(n == 0)
        def _():
            _load_weight(gath_ref, wbuf, sems)
        lo, _ = _half_masks()
        lane_s = lax.broadcasted_iota(jnp.int32, (ATT_SLAB, 128), 1)
        starts = _key_block_starts(n, nblk)
        tiles = []
        for hk in range(N_KV):
            c0 = (2 * hk) * 128
            k_lo = _kv_low(k_ref, starts, hk, lo)
            v_lo = _kv_low(v_ref, starts, hk, lo)
            q4 = _stack_heads(q_ref[:, c0:c0 + 128], q_ref[:, c0 + 128:c0 + 256])
            for jj in range(3):
                s_scr[hk, jj] = _dot(q4, k_lo[jj], NT)
            for g in range(4):
                h = 4 * hk + g
                for r0 in range(0, CHUNK, ATT_SLAB):
                    p, ps = _softmax_slab(s_scr, hk, g, r0, bias_ref, sink_ref, n, nblk)
                    for jj in range(3):
                        p_ref[hk, jj, g * CHUNK + r0:g * CHUNK + r0 + ATT_SLAB, :] = p[jj].astype(BF16)
                    rest = jnp.zeros((ATT_SLAB, 128), F32) if h == 0 else ps_ref[r0:r0 + ATT_SLAB, :]
                    ps_ref[r0:r0 + ATT_SLAB, :] = jnp.where(lane_s == h, ps, rest)
            o4 = _dot(p_ref[hk, 0], v_lo[0], NN) + _dot(p_ref[hk, 1], v_lo[1], NN) + _dot(p_ref[hk, 2], v_lo[2], NN)
            tiles += list(_unstack_heads(o4))
        att = jnp.concatenate(tiles, axis=1).astype(BF16)
        att_ref[...] = att
        x3_ref[...] = x2_ref[...] + _dot(att, wbuf[...], NN)

    blk = pl.BlockSpec((CHUNK, dm), lambda n: (n, 0))
    return dict(
        body=body, grid=(nblk,), name="attn_fwd", args=[qkv, qkv, qkv, x2, bias, sink, gath],
        out_shape=[jax.ShapeDtypeStruct((t, dm), F32), jax.ShapeDtypeStruct((t, dm), BF16),
                   jax.ShapeDtypeStruct((nblk,) + slab, BF16), jax.ShapeDtypeStruct((t, 128), F32)],
        in_specs=[blk, pl.BlockSpec((t, kvw), lambda n: (0, kcb)), pl.BlockSpec((t, kvw), lambda n: (0, vcb)), blk,
                  _full(bias.shape), pl.BlockSpec(memory_space=pltpu.SMEM), ANY],
        out_specs=[blk, blk, pl.BlockSpec((None,) + slab, lambda n: (n, 0, 0, 0, 0)),
                   pl.BlockSpec((CHUNK, 128), lambda n: (n, 0))],
        scratch=[pltpu.VMEM((gath.shape[1] * NDEV, dm), BF16), pltpu.VMEM(slab, F32),
                 pltpu.SemaphoreType.DMA((NDEV,))])


def _attn_bwd(qkv, att, probs, sink_probs, dx3, bias_shape, gath):
    t, dm = dx3.shape
    nblk = t // CHUNK
    kvw = N_KV * HEAD_DIM
    kcb, vcb = dm // kvw, dm // kvw + 1
    scale = HEAD_DIM ** -0.5
    slab = (N_KV, 3, 4 * CHUNK, CHUNK)

    def body(q_ref, k_ref, v_ref, att_ref, p_ref, ps_ref, dx_ref, gath_ref,
             dq_ref, dkb_ref, dvb_ref, dbias_ref, dsink_ref,
             wbuf, dp_scr, ds_scr, prod_scr, dsum_scr, dk_ref, dv_ref, sems):
        n = pl.program_id(0)

        @pl.when(n == 0)
        def _():
            _load_weight(gath_ref, wbuf, sems)
            dk_ref[...] = jnp.zeros_like(dk_ref)
            dv_ref[...] = jnp.zeros_like(dv_ref)
            dbias_ref[...] = jnp.zeros_like(dbias_ref)
            dsink_ref[...] = jnp.zeros_like(dsink_ref)
        lo, hi = _half_masks()
        lane_s = lax.broadcasted_iota(jnp.int32, (ATT_SLAB, 128), 1)
        starts = _key_block_starts(n, nblk)
        dout = _dot(dx_ref[...].astype(BF16), wbuf[...], NT)
        prod_scr[...] = dout * att_ref[...].astype(F32)
        doutb = dout.astype(BF16)
        dq_tiles = []
        for hk in range(N_KV):
            kt = (hk // 2) * 128
            c0 = (2 * hk) * 128
            k_lo = _kv_low(k_ref, starts, hk, lo)
            v_lo = _kv_low(v_ref, starts, hk, lo)
            q4 = _stack_heads(q_ref[:, c0:c0 + 128], q_ref[:, c0 + 128:c0 + 256])
            do4 = _stack_heads(doutb[:, c0:c0 + 128], doutb[:, c0 + 128:c0 + 256])
            for jj in range(3):
                dp_scr[hk, jj] = _dot(do4, v_lo[jj], NT)
            for g in range(4):
                h = 4 * hk + g
                for r0 in range(0, CHUNK, ATT_SLAB):
                    rows = slice(g * CHUNK + r0, g * CHUNK + r0 + ATT_SLAB)
                    pt = prod_scr[r0:r0 + ATT_SLAB, c0 + (g // 2) * 128:c0 + (g // 2 + 1) * 128]
                    msk = lane_s < HEAD_DIM if g % 2 == 0 else lane_s >= HEAD_DIM
                    dsum = jnp.sum(jnp.where(msk, pt, 0.0), axis=-1, keepdims=True)
                    rest = jnp.zeros((ATT_SLAB, 128), F32) if h == 0 else dsum_scr[r0:r0 + ATT_SLAB, :]
                    dsum_scr[r0:r0 + ATT_SLAB, :] = jnp.where(lane_s == h, dsum, rest)
                    for jj in range(3):
                        ds = p_ref[hk, jj, rows, :].astype(F32) * (dp_scr[hk, jj, rows, :] - dsum)
                        dbias_ref[h, r0:r0 + ATT_SLAB, jj * CHUNK:(jj + 1) * CHUNK] += ds
                        ds_scr[hk, jj, rows, :] = ds.astype(BF16)
            dq4 = jnp.zeros((4 * CHUNK, 128), F32)
            for jj in range(3):
                ds4 = ds_scr[hk, jj]
                dq4 = dq4 + _dot(ds4, k_lo[jj], NN) * scale
                dkj = _dot(ds4, q4, TN) * scale
                dvj = _dot(p_ref[hk, jj], do4, TN)
                if hk % 2 == 1:
                    dkj, dvj = pltpu.roll(dkj, HEAD_DIM, 1), pltpu.roll(dvj, HEAD_DIM, 1)
                keep = lo if hk % 2 == 0 else hi
                dk_ref[pl.ds(starts[jj], CHUNK), kt:kt + 128] += jnp.where(keep, dkj, 0.0)
                dv_ref[pl.ds(starts[jj], CHUNK), kt:kt + 128] += jnp.where(keep, dvj, 0.0)
            dq_tiles += list(_unstack_heads(dq4))
        dq_ref[...] = jnp.concatenate(dq_tiles, axis=1).astype(BF16)
        dsink_ref[...] -= jnp.sum(ps_ref[...] * dsum_scr[...], axis=0, keepdims=True)

        @pl.when(n == nblk - 1)
        def _():
            dkb_ref[...] = dk_ref[...].astype(BF16)
            dvb_ref[...] = dv_ref[...].astype(BF16)

    blk = pl.BlockSpec((CHUNK, dm), lambda n: (n, 0))
    return dict(
        body=body, grid=(nblk,), name="attn_bwd", args=[qkv, qkv, qkv, att, probs, sink_probs, dx3, gath],
        out_shape=[jax.ShapeDtypeStruct((t, dm), BF16), jax.ShapeDtypeStruct((t, kvw), BF16),
                   jax.ShapeDtypeStruct((t, kvw), BF16), jax.ShapeDtypeStruct(bias_shape, F32),
                   jax.ShapeDtypeStruct((1, 128), F32)],
        in_specs=[blk, pl.BlockSpec((t, kvw), lambda n: (0, kcb)), pl.BlockSpec((t, kvw), lambda n: (0, vcb)),
                  blk, pl.BlockSpec((None,) + slab, lambda n: (n, 0, 0, 0, 0)),
                  pl.BlockSpec((CHUNK, 128), lambda n: (n, 0)), blk, ANY],
        out_specs=[blk, _full((t, kvw)), _full((t, kvw)), _full(bias_shape), _full((1, 128))],
        scratch=[pltpu.VMEM((gath.shape[1] * NDEV, dm), BF16), pltpu.VMEM(slab, F32), pltpu.VMEM(slab, BF16),
                 pltpu.VMEM((CHUNK, dm), F32), pltpu.VMEM((CHUNK, 128), F32),
                 pltpu.VMEM((t, kvw), F32), pltpu.VMEM((t, kvw), F32), pltpu.SemaphoreType.DMA((NDEV,))])


def _finish_weight(recvs, w, m, v, name):
    nl, r, dm = w.shape
    assert nl == len(recvs) and all(rc.shape[1:] == (r, dm) for rc in recvs)
    td = dm
    wspec = pl.BlockSpec((None, r, td), lambda l, j: (l, 0, j))

    def body(*refs):
        r_refs = refs[:nl]
        w_ref, m_ref, v_ref, g_ref, d_ref, nm_ref, nv_ref = refs[nl:]
        layer = pl.program_id(0)
        for li in range(nl):
            @pl.when(layer == li)
            def _():
                g = r_refs[li][0].astype(F32)
                for d in range(1, recvs[li].shape[0]):
                    g = g + r_refs[li][d].astype(F32)
                delta, nm, nv = _adamw_math(w_ref[...], g, m_ref[...], v_ref[...])
                g_ref[...] = g
                d_ref[...] = delta
                nm_ref[...] = nm
                nv_ref[...] = nv

    o = jax.ShapeDtypeStruct(w.shape, F32)
    return dict(
        body=body, grid=(nl, dm // td), name=name, args=[*recvs, w, m, v], out_shape=[o, o, o, o],
        in_specs=[pl.BlockSpec((rc.shape[0], r, td), lambda l, j: (0, 0, j)) for rc in recvs] + [wspec] * 3,
        out_specs=[wspec] * 4)


def _adamw_small(ws, ms, vs, slots, late_slots, loss_slots, name):
    n = len(ws)

    def total(ref):
        acc = ref[0].astype(F32)
        for d in range(1, NDEV):
            acc = acc + ref[d].astype(F32)
        return acc

    def body(*refs):
        ins, outs = refs[:4 * n + 2], refs[4 * n + 2:]
        for i in range(n):
            w_ref, m_ref, v_ref, s_ref = ins[4 * i:4 * i + 4]
            g_ref, d_ref, nm_ref, nv_ref = outs[4 * i:4 * i + 4]
            g_ref[...] = total(s_ref)
            if i == 0:
                g_ref[0:1, :] = total(ins[4 * n])
            d_ref[...], nm_ref[...], nv_ref[...] = _adamw_math(w_ref[...], g_ref[...], m_ref[...], v_ref[...])
        outs[4 * n][...] = total(ins[4 * n + 1])

    args, out_shape = [], []
    for w, m, v, s in zip(ws, ms, vs, slots):
        args += [w, m, v, s]
        out_shape += [jax.ShapeDtypeStruct(w.shape, F32)] * 4
    args += [late_slots, loss_slots]
    out_shape.append(jax.ShapeDtypeStruct((1, 1), F32))
    out = pl.pallas_call(
        body, grid=(1,), out_shape=tuple(out_shape), in_specs=[_full(a.shape) for a in args],
        out_specs=tuple(_full(o.shape) for o in out_shape), compiler_params=_cp(), name=name)(*args)
    return [tuple(out[4 * i:4 * i + 4]) for i in range(n)], out[4 * n]


def kernel(x, norm_mix, norm_ffn, even_w_in, even_v_ln_g, even_v_ln_b, even_w_spatial, even_b_spatial, even_conv_w, even_w_out, attn_w_qkv, attn_sink, rel_bias, attn_w_out, ffn_w_gate, ffn_w_up, ffn_w_down, final_norm, loss_target, m_norm_mix, m_norm_ffn, m_even_w_in, m_even_v_ln_g, m_even_v_ln_b, m_even_w_spatial, m_even_b_spatial, m_even_conv_w, m_even_w_out, m_attn_w_qkv, m_attn_sink, m_rel_bias, m_attn_w_out, m_ffn_w_gate, m_ffn_w_up, m_ffn_w_down, m_final_norm, v_norm_mix, v_norm_ffn, v_even_w_in, v_even_v_ln_g, v_even_v_ln_b, v_even_w_spatial, v_even_b_spatial, v_even_conv_w, v_even_w_out, v_attn_w_qkv, v_attn_sink, v_rel_bias, v_attn_w_out, v_ffn_w_gate, v_ffn_w_up, v_ffn_w_down, v_final_norm):
    t, dm = x.shape[1], x.shape[2]
    aw = even_v_ln_g.shape[1]
    bw = even_conv_w.shape[2] * NDEV
    gd = aw // A_GROUPS
    tm = min(512, t // 2)
    tmf = min(256, t // 2)
    me = _my_index()
    row = lambda a: a.reshape(1, -1)

    colT = lambda w: w.T.astype(BF16)
    sh = dict(winT=colT(even_w_in[0]), wqkvT=colT(attn_w_qkv[0]), wgT0=colT(ffn_w_gate[0]), wuT0=colT(ffn_w_up[0]),
              wgT1=colT(ffn_w_gate[1]), wuT1=colT(ffn_w_up[1]), woe=even_w_out[0].astype(BF16),
              woa=attn_w_out[0].astype(BF16), wd0=ffn_w_down[0].astype(BF16), wd1=ffn_w_down[1].astype(BF16))
    gather = lambda names: _GatherCarry([sh[n] for n in names])

    in_full = lambda a: lax.dynamic_update_slice(jnp.zeros((3, bw), F32), a[0], (0, me * (bw // NDEV)))

    x0 = x[0]
    wsp_b = even_w_spatial[0].astype(BF16)
    bspb = jnp.broadcast_to(even_b_spatial[0][:, :, None], (A_GROUPS, CHUNK, gd))
    buckets = _bucket_table()
    sink = attn_sink[0]

    (bias,), ((g_winT,), (cw_slots,)) = _call(
        _bias_table(rel_bias.T, buckets), [gather(["winT"]), _BroadcastCarry([in_full(even_conv_w)])])
    cw_full = jnp.sum(cw_slots, axis=0)
    (proj, h0b), (g_woe, g_wgT0) = _call(_norm_proj(x0, row(norm_mix[0]), g_winT, F32, "in_proj", tm),
                                         gather(["woe", "wgT0"]))
    (x1, yb), (g_wuT0,) = _call(_even_core_fwd(proj, x0, even_v_ln_g, even_v_ln_b, wsp_b, bspb, cw_full, g_woe, tm),
                                gather(["wuT0"]))
    (gate0, up0, act0), (g_wd0,) = _call(_ffn_up(x1, row(norm_ffn[0]), g_wgT0, g_wuT0, "ffn_up0", tm), gather(["wd0"]))
    (x2,), (g_wqkvT,) = _call(_ffn_down(x1, act0, g_wd0, "ffn_down0", tm), gather(["wqkvT"]))
    (qkv, h2b), (g_woa,) = _call(_norm_proj(x2, row(norm_mix[1]), g_wqkvT, BF16, "qkv_proj", tm), gather(["woa"]))
    (x3, attb, probs, sink_probs), (g_wgT1, g_wuT1) = _call(
        _attn_fwd(qkv, x2, bias, sink, g_woa), gather(["wgT1", "wuT1"]))
    (gate1, up1, act1), (g_wd1,) = _call(_ffn_up(x3, row(norm_ffn[1]), g_wgT1, g_wuT1, "ffn_up1", tm), gather(["wd1"]))
    (loss_part, dx4, dx4b, d_final), _ = _call(
        _ffn_down_loss(x3, act1, g_wd1, loss_target[0], row(final_norm), "ffn_down1_loss", tm))

    (dx3, dx3b, dg1, du1, h3b, d_nffn1), _ = _call(
        _ffn_bwd(dx4, x3, gate1, up1, row(norm_ffn[1]), g_wgT1, g_wuT1, g_wd1, "ffn_bwd1", tmf))
    (p_wgT1,), _ = _call(_wgrad(dg1, h3b, "wgrad_gate1"))
    (p_wuT1,), _ = _call(_wgrad(du1, h3b, "wgrad_up1"))
    (p_wd1,), ((a_wgT1,), (a_wuT1,)) = _call(
        _wgrad(act1, dx4b, "wgrad_down1"), [_PairCarry(p_wgT1), _PairCarry(p_wuT1)])
    (dq, dk, dv, dbias, dsink), ((r_wgT1,), (a_wd1,)) = _call(
        _attn_bwd(qkv, attb, probs, sink_probs, dx3, bias.shape, g_woa),
        [_ChipSumCarry(p_wgT1, a_wgT1), _PairCarry(p_wd1)])
    (p_woa,), _ = _call(_wgrad(attb, dx3b, "wgrad_attn_out"))
    d_relb = _rel_bias_grad(dbias, buckets)[:, 0:N_BUCKETS].T
    (dx2, dx2b, d_nmix1), (r_wuT1,) = _call(
        _proj_bwd_norm([dq, dk, dv], x2, row(norm_mix[1]), dx3, g_wqkvT, "qkv_bwd", tm),
        _ChipSumCarry(p_wuT1, a_wuT1))
    (p_wqkvT,), _ = _call(_wgrad([dq, dk, dv], h2b, "wgrad_qkv"))
    (dx1, dx1b, dg0, du0, h1b, d_nffn0), ((r_wd1,), (r_woa, r_wqkvT)) = _call(
        _ffn_bwd(dx2, x1, gate0, up0, row(norm_ffn[0]), g_wgT0, g_wuT0, g_wd0, "ffn_bwd0", tmf),
        [_ChipSumCarry(p_wd1, a_wd1), _GradCarry([p_woa, p_wqkvT])])
    (p_wgT0,), _ = _call(_wgrad(dg0, h1b, "wgrad_gate0"))
    (p_wuT0,), (a_wgT0,) = _call(_wgrad(du0, h1b, "wgrad_up0"), _PairCarry(p_wgT0))
    (p_wd0,), ((r_wgT0,), (a_wuT0,)) = _call(
        _wgrad(act0, dx2b, "wgrad_down0"), [_ChipSumCarry(p_wgT0, a_wgT0), _PairCarry(p_wuT0)])
    (dproj, d_lng, d_lnb, d_wsp, d_bsp3, d_cw), ((r_wuT0,), (a_wd0,)) = _call(
        _even_core_bwd(proj, dx1, even_v_ln_g, even_v_ln_b, wsp_b, bspb, cw_full, g_woe, tm),
        [_ChipSumCarry(p_wuT0, a_wuT0), _PairCarry(p_wd0)])
    small_names = ["norm_mix", "norm_ffn", "even_v_ln_g", "even_v_ln_b", "even_w_spatial", "even_b_spatial",
                   "even_conv_w", "attn_sink", "rel_bias", "final_norm"]
    small_parts = [jnp.concatenate([jnp.zeros_like(d_nmix1), d_nmix1]), jnp.concatenate([d_nffn0, d_nffn1]),
                   d_lng, d_lnb, d_wsp[None].astype(BF16), jnp.sum(d_bsp3, axis=-1)[None], d_cw,
                   dsink[:, 0:N_HEADS], d_relb, d_final, loss_part]
    (p_winT,), (r_wd0,) = _call(_wgrad(dproj, h0b, "wgrad_in"), _ChipSumCarry(p_wd0, a_wd0))
    (p_woe,), ((a_winT,), small_slots) = _call(
        _wgrad(yb, dx1b, "wgrad_even_out"), [_PairCarry(p_winT), _BroadcastCarry(small_parts)])
    (dx0, _, d_nmix0), ((r_winT,), (r_woe,)) = _call(
        _proj_bwd_norm([dproj], x0, row(norm_mix[0]), dx1, g_winT, "in_proj_bwd", tm),
        [_ChipSumCarry(p_winT, a_winT), _GradCarry([p_woe])])

    grads = {}

    order = ["norm_mix", "norm_ffn", "even_w_in", "even_v_ln_g", "even_v_ln_b", "even_w_spatial", "even_b_spatial",
             "even_conv_w", "even_w_out", "attn_w_qkv", "attn_sink", "rel_bias", "attn_w_out", "ffn_w_gate",
             "ffn_w_up", "ffn_w_down", "final_norm"]
    ws = dict(norm_mix=norm_mix, norm_ffn=norm_ffn, even_w_in=even_w_in, even_v_ln_g=even_v_ln_g,
              even_v_ln_b=even_v_ln_b, even_w_spatial=even_w_spatial, even_b_spatial=even_b_spatial,
              even_conv_w=even_conv_w, even_w_out=even_w_out, attn_w_qkv=attn_w_qkv, attn_sink=attn_sink,
              rel_bias=rel_bias, attn_w_out=attn_w_out, ffn_w_gate=ffn_w_gate, ffn_w_up=ffn_w_up,
              ffn_w_down=ffn_w_down, final_norm=final_norm)
    ms = dict(norm_mix=m_norm_mix, norm_ffn=m_norm_ffn, even_w_in=m_even_w_in, even_v_ln_g=m_even_v_ln_g,
              even_v_ln_b=m_even_v_ln_b, even_w_spatial=m_even_w_spatial, even_b_spatial=m_even_b_spatial,
              even_conv_w=m_even_conv_w, even_w_out=m_even_w_out, attn_w_qkv=m_attn_w_qkv, attn_sink=m_attn_sink,
              rel_bias=m_rel_bias, attn_w_out=m_attn_w_out, ffn_w_gate=m_ffn_w_gate, ffn_w_up=m_ffn_w_up,
              ffn_w_down=m_ffn_w_down, final_norm=m_final_norm)
    vs = dict(norm_mix=v_norm_mix, norm_ffn=v_norm_ffn, even_w_in=v_even_w_in, even_v_ln_g=v_even_v_ln_g,
              even_v_ln_b=v_even_v_ln_b, even_w_spatial=v_even_w_spatial, even_b_spatial=v_even_b_spatial,
              even_conv_w=v_even_conv_w, even_w_out=v_even_w_out, attn_w_qkv=v_attn_w_qkv, attn_sink=v_attn_sink,
              rel_bias=v_rel_bias, attn_w_out=v_attn_w_out, ffn_w_gate=v_ffn_w_gate, ffn_w_up=v_ffn_w_up,
              ffn_w_down=v_ffn_w_down, final_norm=v_final_norm)
    big = dict(ffn_w_gate=([r_wgT0, r_wgT1], True), even_w_in=([r_winT], True), even_w_out=([r_woe], False),
               attn_w_qkv=([r_wqkvT], True), attn_w_out=([r_woa], False), ffn_w_up=([r_wuT0, r_wuT1], True),
               ffn_w_down=([r_wd0, r_wd1], False))
    delta, new_m, new_v = {}, {}, {}
    late_slots = None
    for n, (recvs, transposed) in big.items():
        lay = (lambda a: jnp.swapaxes(a, 1, 2)) if transposed else (lambda a: a)
        spec = _finish_weight(recvs, lay(ws[n]), lay(ms[n]), lay(vs[n]), "finish_" + n)
        if late_slots is None:
            outs, (late_slots,) = _call(spec, _BroadcastCarry([d_nmix0]))
        else:
            outs, _ = _call(spec)
        grads[n], delta[n], new_m[n], new_v[n] = [lay(o) for o in outs]
    shaped = lambda n, a: in_full(a) if n == "even_conv_w" else (a.reshape(1, dm) if n == "final_norm" else a)
    pick = lambda dct: [shaped(n, dct[n]) for n in small_names]
    results, loss11 = _adamw_small(pick(ws), pick(ms), pick(vs), small_slots[:-1], late_slots, small_slots[-1],
                                   "adamw_small")
    mine = lambda a: lax.dynamic_slice(a, (0, me * (bw // NDEV)), (3, bw // NDEV))[None]
    for n, res in zip(small_names, results):
        for dst, a in zip((grads, delta, new_m, new_v), res):
            dst[n] = mine(a) if n == "even_conv_w" else (a.reshape(dm) if n == "final_norm" else a)
    loss = loss11[0, 0]
    return (loss, dx0[None], *[grads[n] for n in order], *[delta[n] for n in order],
            *[new_m[n] for n in order], *[new_v[n] for n in order])
```

```python
import math

import jax
import jax.numpy as jnp
import numpy as np
from jax import lax
from jax.experimental import pallas as pl
from jax.experimental.pallas import tpu as pltpu

F32, BF16 = jnp.float32, jnp.bfloat16
NDEV = 8
EPS = 1e-6
CHUNK = 128
A_GROUPS = 4
N_HEADS, N_KV, HEAD_DIM = 16, 4, 64
N_BUCKETS, MAX_DISTANCE = 32, 128
NEG = -1e30
LOG2E = 1.4426950408889634
ADAM_LR, ADAM_B1, ADAM_B2, ADAM_EPS, ADAM_WD, ADAM_STEP = 0.001, 0.9, 0.999, 1e-08, 0.01, 10
VMEM_LIMIT = 56 * 1024 * 1024
MESH = pl.DeviceIdType.MESH
NT = (((1,), (1,)), ((), ()))
NN = (((1,), (0,)), ((), ()))
TN = (((0,), (0,)), ((), ()))
ANY = pl.BlockSpec(memory_space=pl.ANY)


def _cp(n_grid=1):
    return pltpu.CompilerParams(dimension_semantics=("arbitrary",) * n_grid, vmem_limit_bytes=VMEM_LIMIT)


def _dot(a, b, dims):
    return lax.dot_general(a, b, dims, preferred_element_type=F32)


def _my_index():
    return 4 * lax.axis_index("x") + 2 * lax.axis_index("y") + lax.axis_index("c")


def _peer(k):
    x, y, c = lax.axis_index("x"), lax.axis_index("y"), lax.axis_index("c")
    px = 1 - x if k & 4 else x
    py = 1 - y if k & 2 else y
    pc = 1 - c if k & 1 else c
    return (px, py, pc)


def _load_weight(gath_ref, wbuf, sems):
    rows = gath_ref.shape[1]
    cps = [pltpu.make_async_copy(gath_ref.at[d], wbuf.at[pl.ds(d * rows, rows), :], sems.at[d]) for d in range(NDEV)]
    for c in cps:
        c.start()
    for c in cps:
        c.wait()


class _GatherCarry:
    def __init__(self, pieces):
        self.inputs = list(pieces)
        self.n = len(pieces)
        self.out_shape = [jax.ShapeDtypeStruct((NDEV,) + p.shape, p.dtype) for p in pieces]
        self.scratch = [pltpu.SemaphoreType.DMA((7 * self.n,)), pltpu.SemaphoreType.DMA((7 * self.n,)),
                        pltpu.SemaphoreType.DMA((self.n,))]

    def _ctx(self):
        x, y, c = lax.axis_index("x"), lax.axis_index("y"), lax.axis_index("c")
        chips = [(1 - x, y), (x, 1 - y), (1 - x, 1 - y)]
        return (x, y, c), (x, y, 1 - c), chips, c

    def _copy(self, k, j, block, to, ins, outs, sems, src=None):
        send_sems, recv_sems, _ = sems
        slot = outs[j].at[4 * block[0] + 2 * block[1] + block[2]]
        return pltpu.make_async_remote_copy(
            src_ref=slot if src is None else src, dst_ref=slot, send_sem=send_sems.at[k * self.n + j],
            recv_sem=recv_sems.at[k * self.n + j], device_id=to, device_id_type=MESH)

    def start(self, ins, outs, sems):
        me, sibling, chips, c = self._ctx()
        for j in range(self.n):
            pltpu.make_async_copy(ins[j], outs[j].at[4 * me[0] + 2 * me[1] + me[2]], sems[2].at[j]).start()
            self._copy(0, j, me, sibling, ins, outs, sems, src=ins[j]).start()
            for q, chip in enumerate(chips):
                self._copy(1 + q, j, me, (*chip, c), ins, outs, sems, src=ins[j]).start()

    def mid(self, ins, outs, sems):
        me, sibling, chips, c = self._ctx()
        for q, chip in enumerate(chips):
            for j in range(self.n):
                self._copy(1 + q, j, (*chip, c), me, ins, outs, sems).wait_recv()
                self._copy(4 + q, j, (*chip, c), sibling, ins, outs, sems).start()

    def finish(self, ins, outs, sems):
        me, sibling, chips, c = self._ctx()
        for j in range(self.n):
            self._copy(0, j, sibling, me, ins, outs, sems).wait_recv()
            for q, chip in enumerate(chips):
                self._copy(4 + q, j, (*chip, 1 - c), me, ins, outs, sems).wait_recv()
        for j in range(self.n):
            self._copy(0, j, me, sibling, ins, outs, sems, src=ins[j]).wait_send()
            for q, chip in enumerate(chips):
                self._copy(1 + q, j, me, (*chip, c), ins, outs, sems, src=ins[j]).wait_send()
                self._copy(4 + q, j, (*chip, c), sibling, ins, outs, sems).wait_send()
            pltpu.make_async_copy(ins[j], outs[j].at[0], sems[2].at[j]).wait()


class _GradCarry:
    def __init__(self, pieces):
        self.inputs = list(pieces)
        self.n = len(pieces)
        self.rows = [p.shape[0] // NDEV for p in pieces]
        self.out_shape = [jax.ShapeDtypeStruct((NDEV, r, p.shape[1]), p.dtype) for p, r in zip(pieces, self.rows)]
        self.scratch = [pltpu.SemaphoreType.DMA((7 * self.n,)), pltpu.SemaphoreType.DMA((7 * self.n,)),
                        pltpu.SemaphoreType.DMA((self.n,))]

    def _copies(self, ins, outs, sems):
        me = _my_index()
        local, remote = [], []
        for j in range(self.n):
            r = self.rows[j]
            local.append(pltpu.make_async_copy(ins[j].at[pl.ds(pl.multiple_of(me * r, 16), r), :], outs[j].at[me],
                                               sems[2].at[j]))
            for k in range(1, NDEV):
                peer = _peer(k)
                pidx = 4 * peer[0] + 2 * peer[1] + peer[2]
                remote.append(pltpu.make_async_remote_copy(
                    src_ref=ins[j].at[pl.ds(pl.multiple_of(pidx * r, 16), r), :], dst_ref=outs[j].at[me],
                    send_sem=sems[0].at[(k - 1) * self.n + j], recv_sem=sems[1].at[(k - 1) * self.n + j],
                    device_id=peer, device_id_type=MESH))
        return local, remote

    def start(self, ins, outs, sems):
        local, remote = self._copies(ins, outs, sems)
        for cp in local + remote:
            cp.start()

    def mid(self, ins, outs, sems):
        pass

    def finish(self, ins, outs, sems):
        local, remote = self._copies(ins, outs, sems)
        for cp in remote + local:
            cp.wait()


class _BroadcastCarry:
    def __init__(self, parts):
        self.inputs = list(parts)
        self.n = len(self.inputs)
        self.out_shape = [jax.ShapeDtypeStruct((NDEV,) + p.shape, p.dtype) for p in self.inputs]
        self.scratch = [pltpu.SemaphoreType.DMA((7 * self.n,)), pltpu.SemaphoreType.DMA((7 * self.n,)),
                        pltpu.SemaphoreType.DMA((self.n,))]

    def _copies(self, ins, outs, sems):
        me = _my_index()
        cps = []
        for j in range(self.n):
            cps.append(pltpu.make_async_copy(ins[j], outs[j].at[me], sems[2].at[j]))
            cps += [pltpu.make_async_remote_copy(
                src_ref=ins[j], dst_ref=outs[j].at[me], send_sem=sems[0].at[(k - 1) * self.n + j],
                recv_sem=sems[1].at[(k - 1) * self.n + j], device_id=_peer(k), device_id_type=MESH)
                for k in range(1, NDEV)]
        return cps

    def start(self, ins, outs, sems):
        for cp in self._copies(ins, outs, sems):
            cp.start()

    def mid(self, ins, outs, sems):
        pass

    def finish(self, ins, outs, sems):
        for cp in self._copies(ins, outs, sems):
            cp.wait()


class _PairCarry:
    def __init__(self, piece):
        self.inputs = [piece]
        self.r = piece.shape[0] // NDEV
        self.out_shape = [jax.ShapeDtypeStruct((4, self.r, piece.shape[1]), piece.dtype)]
        self.scratch = [pltpu.SemaphoreType.DMA((4,)), pltpu.SemaphoreType.DMA((4,))]

    def _copies(self, ins, outs, sems):
        x, y, c = lax.axis_index("x"), lax.axis_index("y"), lax.axis_index("c")
        return [pltpu.make_async_remote_copy(
            src_ref=ins[0].at[pl.ds(pl.multiple_of((2 * q + 1 - c) * self.r, 16), self.r), :], dst_ref=outs[0].at[q],
            send_sem=sems[0].at[q], recv_sem=sems[1].at[q], device_id=(x, y, 1 - c), device_id_type=MESH)
            for q in range(4)]

    def start(self, ins, outs, sems):
        for cp in self._copies(ins, outs, sems):
            cp.start()

    def mid(self, ins, outs, sems):
        pass

    def finish(self, ins, outs, sems):
        for cp in self._copies(ins, outs, sems):
            cp.wait()


class _ChipSumCarry:
    def __init__(self, piece, landed, staged=4):
        self.inputs = [piece, landed]
        self.staged = staged
        self.r, dm = piece.shape[0] // NDEV, piece.shape[1]
        self.out_shape = [jax.ShapeDtypeStruct((4, self.r, dm), piece.dtype)]
        self.scratch = [pltpu.VMEM((4, self.r, dm), piece.dtype), pltpu.VMEM((2 * staged, self.r, dm), piece.dtype),
                        pltpu.SemaphoreType.DMA((2 * staged,)), pltpu.SemaphoreType.DMA((3,)),
                        pltpu.SemaphoreType.DMA((3,)), pltpu.SemaphoreType.DMA(())]

    def _copies(self, outs, scr):
        sums, _, _, send_sems, recv_sems, local_sem = scr
        x, y, c = lax.axis_index("x"), lax.axis_index("y"), lax.axis_index("c")
        mine = 2 * x + y
        local = pltpu.make_async_copy(sums.at[mine], outs[0].at[mine], local_sem)
        remote = []
        for k in range(1, 4):
            px = 1 - x if k & 2 else x
            py = 1 - y if k & 1 else y
            remote.append(pltpu.make_async_remote_copy(
                src_ref=sums.at[2 * px + py], dst_ref=outs[0].at[mine], send_sem=send_sems.at[k - 1],
                recv_sem=recv_sems.at[k - 1], device_id=(px, py, c), device_id_type=MESH))
        return local, remote

    def start(self, ins, outs, scr):
        sums, stage, stage_sems = scr[0], scr[1], scr[2]
        c = lax.axis_index("c")
        for q0 in range(0, 4, self.staged):
            loads = []
            for i in range(self.staged):
                q = q0 + i
                loads.append((
                    pltpu.make_async_copy(ins[0].at[pl.ds(pl.multiple_of((2 * q + c) * self.r, 16), self.r), :],
                                          stage.at[2 * i], stage_sems.at[2 * i]),
                    pltpu.make_async_copy(ins[1].at[q], stage.at[2 * i + 1], stage_sems.at[2 * i + 1])))
            for a, b in loads:
                a.start()
                b.start()
            for i, (a, b) in enumerate(loads):
                a.wait()
                b.wait()
                sums[q0 + i] = (stage[2 * i].astype(F32) + stage[2 * i + 1].astype(F32)).astype(sums.dtype)
        local, remote = self._copies(outs, scr)
        for cp in [local] + remote:
            cp.start()

    def mid(self, ins, outs, scr):
        pass

    def finish(self, ins, outs, scr):
        local, remote = self._copies(outs, scr)
        for cp in remote + [local]:
            cp.wait()


def _call(spec, carry=None):
    body, grid = spec["body"], spec["grid"]
    in_specs, out_specs, out_shape = list(spec["in_specs"]), list(spec["out_specs"]), list(spec["out_shape"])
    scratch, args = list(spec.get("scratch", [])), list(spec["args"])
    if carry is None:
        out = pl.pallas_call(body, grid=grid, in_specs=in_specs, out_specs=tuple(out_specs),
                             out_shape=tuple(out_shape), scratch_shapes=scratch, compiler_params=_cp(len(grid)),
                             name=spec["name"])(*args)
        return tuple(out), ()
    carries = list(carry) if isinstance(carry, (list, tuple)) else [carry]
    n_in, n_out, n_s = len(in_specs), len(out_specs), len(scratch)
    steps = int(np.prod(grid))

    def split(refs, counts):
        parts, o = [], 0
        for cnt in counts:
            parts.append(refs[o:o + cnt])
            o += cnt
        return parts

    c_in = [len(cr.inputs) for cr in carries]
    c_out = [len(cr.out_shape) for cr in carries]
    c_scr = [len(cr.scratch) for cr in carries]

    def wrapped(*refs):
        ins, cins, outs, couts, scr, cscr = split(refs, [n_in, sum(c_in), n_out, sum(c_out), n_s, sum(c_scr)])
        per = list(zip(carries, split(cins, c_in), split(couts, c_out), split(cscr, c_scr)))
        step = pl.program_id(0)
        for ax in range(1, len(grid)):
            step = step * grid[ax] + pl.program_id(ax)

        @pl.when(step == 0)
        def _():
            for cr, ci, co, cs in per:
                cr.start(ci, co, cs)
        if steps >= 3:
            @pl.when(step == steps - 2)
            def _():
                for cr, ci, co, cs in per:
                    cr.mid(ci, co, cs)
        body(*ins, *outs, *scr)

        @pl.when(step == steps - 1)
        def _():
            for cr, ci, co, cs in per:
                if steps < 3:
                    cr.mid(ci, co, cs)
                cr.finish(ci, co, cs)

    out = pl.pallas_call(
        wrapped, grid=grid, in_specs=in_specs + [ANY] * sum(c_in), out_specs=tuple(out_specs + [ANY] * sum(c_out)),
        out_shape=tuple(out_shape + [s for cr in carries for s in cr.out_shape]),
        scratch_shapes=scratch + [s for cr in carries for s in cr.scratch],
        compiler_params=_cp(len(grid)), name=spec["name"])(*args, *[a for cr in carries for a in cr.inputs])
    c_res = [tuple(p) for p in split(out[n_out:], c_out)]
    return tuple(out[:n_out]), (c_res if isinstance(carry, (list, tuple)) else c_res[0])


def _rms_fwd(x, gain):
    r = lax.rsqrt(jnp.mean(x * x, axis=-1, keepdims=True) + EPS)
    return x * r * gain, r


def _rms_bwd(dh, x, r, gain):
    a = dh * gain
    dx = r * a - x * (r * r * r) * jnp.mean(a * x, axis=-1, keepdims=True)
    dgain = jnp.sum(dh * (x * r), axis=0, keepdims=True)
    return dx, dgain


def _gelu(x):
    return 0.5 * x * (1.0 + lax.erf(x * 0.7071067811865476))


def _gelu_grad(x):
    return 0.5 * (1.0 + lax.erf(x * 0.7071067811865476)) + x * jnp.exp(-0.5 * x * x) * 0.3989422804014327


def _sigmoid(x):
    return 1.0 / (1.0 + jnp.exp(-x))


def _adamw_math(w, g, m, v):
    nm = ADAM_B1 * m + (1.0 - ADAM_B1) * g
    nv = ADAM_B2 * v + (1.0 - ADAM_B2) * (g * g)
    m_hat = nm / (1.0 - ADAM_B1 ** ADAM_STEP)
    v_hat = nv / (1.0 - ADAM_B2 ** ADAM_STEP)
    return -ADAM_LR * (m_hat / (jnp.sqrt(v_hat) + ADAM_EPS) + ADAM_WD * w), nm, nv


def _tok(tm, w):
    return pl.BlockSpec((tm, w), lambda i: (i, 0))


def _full(shape):
    return pl.BlockSpec(shape, lambda *i: (0,) * len(shape))


def _norm_proj(x, gain, gath, out_dtype, name, tm):
    t, dm = x.shape
    n = gath.shape[1] * NDEV

    def body(x_ref, g_ref, gath_ref, proj_ref, hb_ref, wbuf, sems):
        @pl.when(pl.program_id(0) == 0)
        def _():
            _load_weight(gath_ref, wbuf, sems)
        h, _ = _rms_fwd(x_ref[...], g_ref[...])
        hb = h.astype(BF16)
        hb_ref[...] = hb
        proj_ref[...] = _dot(hb, wbuf[...], NT).astype(out_dtype)

    return dict(
        body=body, grid=(t // tm,), name=name, args=[x, gain, gath],
        out_shape=[jax.ShapeDtypeStruct((t, n), out_dtype), jax.ShapeDtypeStruct((t, dm), BF16)],
        in_specs=[_tok(tm, dm), _full((1, dm)), ANY], out_specs=[_tok(tm, n), _tok(tm, dm)],
        scratch=[pltpu.VMEM((n, dm), BF16), pltpu.SemaphoreType.DMA((NDEV,))])


def _proj_bwd_norm(dys, x, gain, dres, gath, name, tm):
    t, dm = x.shape
    n = gath.shape[1] * NDEV
    widths = [d.shape[1] for d in dys]
    assert sum(widths) == n
    nd = len(dys)

    def body(*refs):
        dy_refs = refs[:nd]
        x_ref, g_ref, dres_ref, gath_ref, dx_ref, dxb_ref, dgain_ref, wbuf, sems = refs[nd:]

        @pl.when(pl.program_id(0) == 0)
        def _():
            _load_weight(gath_ref, wbuf, sems)
            dgain_ref[...] = jnp.zeros_like(dgain_ref)
        xv, gain_v = x_ref[...], g_ref[...]
        _, r = _rms_fwd(xv, gain_v)
        dh, c0 = None, 0
        for dy_ref, wd in zip(dy_refs, widths):
            part = _dot(dy_ref[...], wbuf[c0:c0 + wd, :], NN)
            dh = part if dh is None else dh + part
            c0 += wd
        dx, dgain = _rms_bwd(dh, xv, r, gain_v)
        dx = dres_ref[...] + dx
        dx_ref[...] = dx
        dxb_ref[...] = dx.astype(BF16)
        dgain_ref[...] += dgain

    return dict(
        body=body, grid=(t // tm,), name=name, args=[*dys, x, gain, dres, gath],
        out_shape=[jax.ShapeDtypeStruct((t, dm), F32), jax.ShapeDtypeStruct((t, dm), BF16),
                   jax.ShapeDtypeStruct((1, dm), F32)],
        in_specs=[_tok(tm, wd) for wd in widths] + [_tok(tm, dm), _full((1, dm)), _tok(tm, dm), ANY],
        out_specs=[_tok(tm, dm), _tok(tm, dm), _full((1, dm))],
        scratch=[pltpu.VMEM((n, dm), BF16), pltpu.SemaphoreType.DMA((NDEV,))])


def _wgrad(a, b, name, tmm=256):
    parts = list(a) if isinstance(a, (list, tuple)) else [a]
    t = parts[0].shape[0]
    n = b.shape[1]
    tiles = [p.shape[1] // tmm for p in parts]
    first = [sum(tiles[:i]) for i in range(len(parts))]
    m = sum(tiles) * tmm

    def body(*refs):
        a_refs, b_ref, o_ref = refs[:len(parts)], refs[len(parts)], refs[len(parts) + 1]
        j = pl.program_id(0)
        for a_ref, j0, nt in zip(a_refs, first, tiles):
            if len(parts) == 1:
                o_ref[...] = _dot(a_ref[...], b_ref[...], TN).astype(BF16)
            else:
                @pl.when((j >= j0) & (j < j0 + nt))
                def _():
                    o_ref[...] = _dot(a_ref[...], b_ref[...], TN).astype(BF16)

    a_specs = [pl.BlockSpec((t, tmm), lambda j, j0=j0, nt=nt: (0, jnp.clip(j - j0, 0, nt - 1)))
               for j0, nt in zip(first, tiles)]
    return dict(
        body=body, grid=(sum(tiles),), name=name, args=[*parts, b], out_shape=[jax.ShapeDtypeStruct((m, n), BF16)],
        in_specs=a_specs + [pl.BlockSpec((t, n), lambda j: (0, 0))],
        out_specs=[pl.BlockSpec((tmm, n), lambda j: (j, 0))])


def _halo_specs(tm, t, width, col_blocks):
    nb8 = tm // 8
    last = t // 8 - 1
    prev = [pl.BlockSpec((8, width), lambda i, cb=cb: (jnp.maximum(i * nb8 - 1, 0), cb)) for cb in col_blocks]
    nxt = [pl.BlockSpec((8, width), lambda i, cb=cb: (jnp.minimum((i + 1) * nb8, last), cb)) for cb in col_blocks]
    return prev, nxt


def _shift_rows(z, prev_row, next_row):
    tm = z.shape[0]
    row = lax.broadcasted_iota(jnp.int32, z.shape, 0)
    zm1 = jnp.where(row == 0, prev_row, pltpu.roll(z, 1, 0))
    zp1 = jnp.where(row == tm - 1, next_row, pltpu.roll(z, tm - 1, 0))
    return zm1, zp1


def _gating_fwd(proj, lng, lnb, wsp_ref, bsp_ref, aw):
    tm = proj.shape[0]
    a_u = _gelu(proj[:, 0:aw])
    gv = _gelu(proj[:, aw:2 * aw])
    mu = jnp.mean(gv, axis=-1, keepdims=True)
    xc = gv - mu
    rstd = lax.rsqrt(jnp.mean(xc * xc, axis=-1, keepdims=True) + EPS)
    vn = xc * rstd
    a_v = (vn * lng + lnb).astype(BF16)
    gd = aw // A_GROUPS
    rows = []
    for c in range(tm // CHUNK):
        cols = []
        for g in range(A_GROUPS):
            blk = a_v[c * CHUNK:(c + 1) * CHUNK, g * gd:(g + 1) * gd]
            cols.append(_dot(wsp_ref[g], blk, NN) + bsp_ref[g])
        rows.append(jnp.concatenate(cols, axis=1))
    mixed = jnp.concatenate(rows, axis=0)
    return a_u, vn, rstd, a_v, mixed


def _even_core_fwd(proj, x0, lng, lnb, wsp, bspb, cw, gath, tm):
    t, dm = x0.shape
    aw = lng.shape[1]
    bw = cw.shape[1]
    assert aw == bw and 2 * aw + 3 * bw == proj.shape[1]
    nt = t // tm
    prev, nxt = _halo_specs(tm, t, bw, [3, 4])

    def body(proj_ref, cp_ref, hp_ref, cn_ref, hn_ref, x0_ref, lng_ref, lnb_ref, wsp_ref, bsp_ref, cw_ref, gath_ref,
             x1_ref, y_ref, wbuf, sems):
        i = pl.program_id(0)

        @pl.when(i == 0)
        def _():
            _load_weight(gath_ref, wbuf, sems)
        proj_v = proj_ref[...]
        a_u, _, _, _, mixed = _gating_fwd(proj_v, lng_ref[...], lnb_ref[...], wsp_ref, bsp_ref, aw)
        a_out = a_u * mixed
        bb = proj_v[:, 2 * aw:2 * aw + bw]
        z = proj_v[:, 2 * aw + bw:2 * aw + 2 * bw] * proj_v[:, 2 * aw + 2 * bw:]
        zprev = jnp.where(i > 0, cp_ref[7:8, :] * hp_ref[7:8, :], 0.0)
        znext = jnp.where(i < nt - 1, cn_ref[0:1, :] * hn_ref[0:1, :], 0.0)
        zm1, zp1 = _shift_rows(z, zprev, znext)
        cwv = cw_ref[...]
        conv = zm1 * cwv[0:1, :] + z * cwv[1:2, :] + zp1 * cwv[2:3, :]
        y = jnp.concatenate([a_out, bb * conv], axis=1).astype(BF16)
        y_ref[...] = y
        x1_ref[...] = x0_ref[...] + _dot(y, wbuf[...], NN)

    return dict(
        body=body, grid=(nt,), name="even_core_fwd",
        args=[proj, proj, proj, proj, proj, x0, lng, lnb, wsp, bspb, cw, gath],
        out_shape=[jax.ShapeDtypeStruct((t, dm), F32), jax.ShapeDtypeStruct((t, aw + bw), BF16)],
        in_specs=[_tok(tm, proj.shape[1]), prev[0], prev[1], nxt[0], nxt[1], _tok(tm, dm), _full(lng.shape),
                  _full(lnb.shape), _full(wsp.shape), _full(bspb.shape), _full(cw.shape), ANY],
        out_specs=[_tok(tm, dm), _tok(tm, aw + bw)],
        scratch=[pltpu.VMEM((gath.shape[1] * NDEV, dm), BF16), pltpu.SemaphoreType.DMA((NDEV,))])


def _even_core_bwd(proj, dx1, lng, lnb, wsp, bspb, cw, gath, tm):
    t, dm = dx1.shape
    aw, bw = lng.shape[1], cw.shape[1]
    gd = aw // A_GROUPS
    nt = t // tm
    inw = proj.shape[1]
    prev, nxt = _halo_specs(tm, t, bw, [2, 3, 4])
    nb8 = tm // 8
    last8 = t // 8 - 1

    def body(proj_ref, bp_ref, cp_ref, hp_ref, bn_ref, cn_ref, hn_ref, dx_ref, dxp_ref, dxn_ref,
             lng_ref, lnb_ref, wsp_ref, bsp_ref, cw_ref, gath_ref,
             dproj_ref, dlng_ref, dlnb_ref, dwsp_ref, dbsp_ref, dcw_ref, wbuf, sems):
        i = pl.program_id(0)

        @pl.when(i == 0)
        def _():
            _load_weight(gath_ref, wbuf, sems)
            dlng_ref[...] = jnp.zeros_like(dlng_ref)
            dlnb_ref[...] = jnp.zeros_like(dlnb_ref)
            dwsp_ref[...] = jnp.zeros_like(dwsp_ref)
            dbsp_ref[...] = jnp.zeros_like(dbsp_ref)
            dcw_ref[...] = jnp.zeros_like(dcw_ref)
        proj_v = proj_ref[...]
        lng_v = lng_ref[...]
        a_u, vn, rstd, a_v, mixed = _gating_fwd(proj_v, lng_v, lnb_ref[...], wsp_ref, bsp_ref, aw)
        w = wbuf[...]
        dy = _dot(dx_ref[...].astype(BF16), w, NT)
        da_out, db_out = dy[:, 0:aw], dy[:, aw:]
        da_u = da_out * mixed
        dmixed = da_out * a_u
        dmb = dmixed.astype(BF16)
        rows = []
        for c in range(tm // CHUNK):
            cols = []
            for g in range(A_GROUPS):
                r0, c0 = c * CHUNK, g * gd
                dm_cg = dmb[r0:r0 + CHUNK, c0:c0 + gd]
                cols.append(_dot(wsp_ref[g], dm_cg, TN))
                dwsp_ref[g] += _dot(dm_cg, a_v[r0:r0 + CHUNK, c0:c0 + gd], NT)
                dbsp_ref[g] += dmixed[r0:r0 + CHUNK, c0:c0 + gd]
            rows.append(jnp.concatenate(cols, axis=1))
        dav = jnp.concatenate(rows, axis=0)
        dlng_ref[...] += jnp.sum(dav * vn, axis=0, keepdims=True)
        dlnb_ref[...] += jnp.sum(dav, axis=0, keepdims=True)
        dvn = dav * lng_v
        dgv = rstd * (dvn - jnp.mean(dvn, axis=-1, keepdims=True) - vn * jnp.mean(dvn * vn, axis=-1, keepdims=True))
        dv_pre = dgv * _gelu_grad(proj_v[:, aw:2 * aw])
        du_pre = da_u * _gelu_grad(proj_v[:, 0:aw])
        bb = proj_v[:, 2 * aw:2 * aw + bw]
        bc = proj_v[:, 2 * aw + bw:2 * aw + 2 * bw]
        bh = proj_v[:, 2 * aw + 2 * bw:]
        z = bc * bh
        zprev = jnp.where(i > 0, cp_ref[7:8, :] * hp_ref[7:8, :], 0.0)
        znext = jnp.where(i < nt - 1, cn_ref[0:1, :] * hn_ref[0:1, :], 0.0)
        zm1, zp1 = _shift_rows(z, zprev, znext)
        cwv = cw_ref[...]
        conv = zm1 * cwv[0:1, :] + z * cwv[1:2, :] + zp1 * cwv[2:3, :]
        dbb = db_out * conv
        dconv = db_out * bb
        dx_edge = jnp.concatenate([dxp_ref[...], dxn_ref[...]], axis=0).astype(BF16)
        dy_edge = _dot(dx_edge, w[aw:, :], NT)
        dcprev = jnp.where(i > 0, dy_edge[7:8, :] * bp_ref[7:8, :], 0.0)
        dcnext = jnp.where(i < nt - 1, dy_edge[8:9, :] * bn_ref[0:1, :], 0.0)
        dcm1, dcp1 = _shift_rows(dconv, dcprev, dcnext)
        dz = dcp1 * cwv[0:1, :] + dconv * cwv[1:2, :] + dcm1 * cwv[2:3, :]
        dcw_ref[0:1, :] += jnp.sum(dconv * zm1, axis=0, keepdims=True)
        dcw_ref[1:2, :] += jnp.sum(dconv * z, axis=0, keepdims=True)
        dcw_ref[2:3, :] += jnp.sum(dconv * zp1, axis=0, keepdims=True)
        dproj_ref[...] = jnp.concatenate([du_pre, dv_pre, dbb, dz * bh, dz * bc], axis=1).astype(BF16)

    row8 = lambda f: pl.BlockSpec((8, dm), f)
    return dict(
        body=body, grid=(nt,), name="even_core_bwd",
        args=[proj, proj, proj, proj, proj, proj, proj, dx1, dx1, dx1, lng, lnb, wsp, bspb, cw, gath],
        out_shape=[jax.ShapeDtypeStruct((t, inw), BF16), jax.ShapeDtypeStruct((1, aw), F32),
                   jax.ShapeDtypeStruct((1, aw), F32), jax.ShapeDtypeStruct(wsp.shape, F32),
                   jax.ShapeDtypeStruct((A_GROUPS, CHUNK, gd), F32), jax.ShapeDtypeStruct(cw.shape, F32)],
        in_specs=[_tok(tm, inw), prev[0], prev[1], prev[2], nxt[0], nxt[1], nxt[2], _tok(tm, dm),
                  row8(lambda i: (jnp.maximum(i * nb8 - 1, 0), 0)), row8(lambda i: (jnp.minimum((i + 1) * nb8, last8), 0)),
                  _full(lng.shape), _full(lnb.shape), _full(wsp.shape), _full(bspb.shape), _full(cw.shape), ANY],
        out_specs=[_tok(tm, inw), _full((1, aw)), _full((1, aw)), _full(wsp.shape),
                   _full((A_GROUPS, CHUNK, gd)), _full(cw.shape)],
        scratch=[pltpu.VMEM((gath.shape[1] * NDEV, dm), BF16), pltpu.SemaphoreType.DMA((NDEV,))])


def _ff_chunks(f, width=1024):
    return [(c0, min(c0 + width, f)) for c0 in range(0, f, width)]


def _ffn_up(x, gain, gath_g, gath_u, name, tm):
    t, dm = x.shape
    f = gath_g.shape[1] * NDEV

    def body(x_ref, g_ref, gg_ref, gu_ref, gate_ref, up_ref, act_ref, wg, wu, sems):
        @pl.when(pl.program_id(0) == 0)
        def _():
            _load_weight(gg_ref, wg, sems)
            _load_weight(gu_ref, wu, sems)
        h, _ = _rms_fwd(x_ref[...], g_ref[...])
        hb = h.astype(BF16)
        for c0, c1 in _ff_chunks(f):
            gate = _dot(hb, wg[c0:c1, :], NT)
            up = _dot(hb, wu[c0:c1, :], NT)
            gate_ref[:, c0:c1] = gate.astype(BF16)
            up_ref[:, c0:c1] = up.astype(BF16)
            act_ref[:, c0:c1] = (gate * _sigmoid(gate) * up).astype(BF16)

    o = jax.ShapeDtypeStruct((t, f), BF16)
    return dict(
        body=body, grid=(t // tm,), name=name, args=[x, gain, gath_g, gath_u], out_shape=[o, o, o],
        in_specs=[_tok(tm, dm), _full((1, dm)), ANY, ANY], out_specs=[_tok(tm, f)] * 3,
        scratch=[pltpu.VMEM((f, dm), BF16), pltpu.VMEM((f, dm), BF16), pltpu.SemaphoreType.DMA((NDEV,))])


def _ffn_down(x, act, gath_d, name, tm):
    t, dm = x.shape
    f = act.shape[1]

    def body(x_ref, a_ref, gd_ref, xo_ref, wd, sems):
        @pl.when(pl.program_id(0) == 0)
        def _():
            _load_weight(gd_ref, wd, sems)
        xo_ref[...] = x_ref[...] + _dot(a_ref[...], wd[...], NN)

    return dict(
        body=body, grid=(t // tm,), name=name, args=[x, act, gath_d], out_shape=[jax.ShapeDtypeStruct((t, dm), F32)],
        in_specs=[_tok(tm, dm), _tok(tm, f), ANY], out_specs=[_tok(tm, dm)],
        scratch=[pltpu.VMEM((f, dm), BF16), pltpu.SemaphoreType.DMA((NDEV,))])


def _ffn_down_loss(x, act, gath_d, target, gain, name, tm):
    t, dm = x.shape
    f = act.shape[1]
    steps = t // tm

    def body(x_ref, a_ref, gd_ref, t_ref, g_ref, loss_ref, dx_ref, dxb_ref, dgain_ref, wd, acc, sems):
        i = pl.program_id(0)

        @pl.when(i == 0)
        def _():
            _load_weight(gd_ref, wd, sems)
            acc[...] = jnp.zeros_like(acc)
            dgain_ref[...] = jnp.zeros_like(dgain_ref)
        xv = x_ref[...] + _dot(a_ref[...], wd[...], NN)
        gain_v = g_ref[...]
        y, r = _rms_fwd(xv, gain_v)
        e = y - t_ref[...]
        acc[...] += jnp.sum(e * e, axis=0, keepdims=True)
        dx, dgain = _rms_bwd(e * (1.0 / dm), xv, r, gain_v)
        dx_ref[...] = dx
        dxb_ref[...] = dx.astype(BF16)
        dgain_ref[...] += dgain

        @pl.when(i == steps - 1)
        def _():
            loss_ref[...] = jnp.sum(acc[...], axis=-1, keepdims=True) * (0.5 / dm)

    return dict(
        body=body, grid=(steps,), name=name, args=[x, act, gath_d, target, gain],
        out_shape=[jax.ShapeDtypeStruct((1, 1), F32), jax.ShapeDtypeStruct((t, dm), F32),
                   jax.ShapeDtypeStruct((t, dm), BF16), jax.ShapeDtypeStruct((1, dm), F32)],
        in_specs=[_tok(tm, dm), _tok(tm, f), ANY, _tok(tm, dm), _full((1, dm))],
        out_specs=[_full((1, 1)), _tok(tm, dm), _tok(tm, dm), _full((1, dm))],
        scratch=[pltpu.VMEM((f, dm), BF16), pltpu.VMEM((1, dm), F32), pltpu.SemaphoreType.DMA((NDEV,))])


def _ffn_bwd(dxo, x, gate, up, gain, gath_g, gath_u, gath_d, name, tm):
    t, dm = x.shape
    f = gate.shape[1]

    def body(dxo_ref, x_ref, gate_ref, up_ref, g_ref, gg_ref, gu_ref, gd_ref,
             dx_ref, dxb_ref, dg_ref, du_ref, hb_ref, dgain_ref, wg, wu, wd, sems):
        @pl.when(pl.program_id(0) == 0)
        def _():
            _load_weight(gg_ref, wg, sems)
            _load_weight(gu_ref, wu, sems)
            _load_weight(gd_ref, wd, sems)
            dgain_ref[...] = jnp.zeros_like(dgain_ref)
        xv, gain_v, dxo_v = x_ref[...], g_ref[...], dxo_ref[...]
        h, r = _rms_fwd(xv, gain_v)
        hb_ref[...] = h.astype(BF16)
        dxob = dxo_v.astype(BF16)
        dh = jnp.zeros_like(xv)
        for c0, c1 in _ff_chunks(f):
            gate_v = gate_ref[:, c0:c1].astype(F32)
            up_v = up_ref[:, c0:c1].astype(F32)
            s = _sigmoid(gate_v)
            silu = gate_v * s
            dact = _dot(dxob, wd[c0:c1, :], NT)
            dg = (dact * up_v * (s * (1.0 + gate_v * (1.0 - s)))).astype(BF16)
            du = (dact * silu).astype(BF16)
            dg_ref[:, c0:c1] = dg
            du_ref[:, c0:c1] = du
            dh = dh + _dot(dg, wg[c0:c1, :], NN) + _dot(du, wu[c0:c1, :], NN)
        dx, dgain = _rms_bwd(dh, xv, r, gain_v)
        dx = dxo_v + dx
        dx_ref[...] = dx
        dxb_ref[...] = dx.astype(BF16)
        dgain_ref[...] += dgain

    return dict(
        body=body, grid=(t // tm,), name=name, args=[dxo, x, gate, up, gain, gath_g, gath_u, gath_d],
        out_shape=[jax.ShapeDtypeStruct((t, dm), F32), jax.ShapeDtypeStruct((t, dm), BF16),
                   jax.ShapeDtypeStruct((t, f), BF16), jax.ShapeDtypeStruct((t, f), BF16),
                   jax.ShapeDtypeStruct((t, dm), BF16), jax.ShapeDtypeStruct((1, dm), F32)],
        in_specs=[_tok(tm, dm), _tok(tm, dm), _tok(tm, f), _tok(tm, f), _full((1, dm)), ANY, ANY, ANY],
        out_specs=[_tok(tm, dm), _tok(tm, dm), _tok(tm, f), _tok(tm, f), _tok(tm, dm), _full((1, dm))],
        scratch=[pltpu.VMEM((f, dm), BF16), pltpu.VMEM((f, dm), BF16), pltpu.VMEM((f, dm), BF16),
                 pltpu.SemaphoreType.DMA((NDEV,))])


def _t5_buckets(rel):
    nb = N_BUCKETS // 2
    ret = jnp.where(rel > 0, nb, 0)
    n = jnp.abs(rel)
    max_exact = nb // 2
    nf = jnp.maximum(n, 1).astype(jnp.float32)
    large = max_exact + (jnp.log(nf / max_exact) / math.log(MAX_DISTANCE / max_exact)
                         * (nb - max_exact)).astype(jnp.int32)
    large = jnp.minimum(large, nb - 1)
    return ret + jnp.where(n < max_exact, n, large)


def _bucket_table():
    qi = jnp.arange(CHUNK, dtype=jnp.int32)[:, None]
    kj = jnp.arange(3 * CHUNK, dtype=jnp.int32)[None, :]
    rel = kj - CHUNK - qi
    return jnp.where(jnp.abs(rel) <= CHUNK, _t5_buckets(rel), -1)


def _bias_table(rel_bias_t, buckets):
    nh = rel_bias_t.shape[0]

    def body(rb_ref, bk_ref, o_ref):
        bk = bk_ref[...]
        for h in range(nh):
            acc = jnp.where(bk < 0, NEG, 0.0).astype(F32)
            for b in range(N_BUCKETS):
                acc = jnp.where(bk == b, rb_ref[h, b] * LOG2E, acc)
            o_ref[h] = acc

    return dict(
        body=body, grid=(1,), name="bias_table", args=[rel_bias_t, buckets],
        out_shape=[jax.ShapeDtypeStruct((nh,) + buckets.shape, F32)],
        in_specs=[pl.BlockSpec(memory_space=pltpu.SMEM), _full(buckets.shape)],
        out_specs=[_full((nh,) + buckets.shape)])


def _rel_bias_grad(dbias, buckets):
    nh = dbias.shape[0]

    def body(db_ref, bk_ref, o_ref):
        bk = bk_ref[...]
        lane = lax.broadcasted_iota(jnp.int32, (1, 128), 1)
        for h in range(nh):
            d = db_ref[h]
            row = jnp.zeros((1, 128), F32)
            for b in range(N_BUCKETS):
                s = jnp.sum(jnp.sum(jnp.where(bk == b, d, 0.0), axis=1, keepdims=True), axis=0, keepdims=True)
                row = jnp.where(lane == b, s, row)
            o_ref[h:h + 1, :] = row

    return pl.pallas_call(
        body, out_shape=jax.ShapeDtypeStruct((nh, 128), F32),
        in_specs=[pl.BlockSpec(memory_space=pltpu.VMEM), pl.BlockSpec(memory_space=pltpu.VMEM)],
        out_specs=pl.BlockSpec(memory_space=pltpu.VMEM), compiler_params=_cp(0), name="rel_bias_grad")(dbias, buckets)


def _half_masks():
    lane = lax.broadcasted_iota(jnp.int32, (CHUNK, 128), 1)
    return lane < HEAD_DIM, lane >= HEAD_DIM


def _kv_low(ref, starts, hk, lo):
    kt = (hk // 2) * 128
    out = []
    for jj in range(3):
        blk = ref[pl.ds(starts[jj], CHUNK), kt:kt + 128]
        if hk % 2 == 1:
            blk = pltpu.roll(blk, HEAD_DIM, 1)
        out.append(jnp.where(lo, blk, jnp.zeros_like(blk)))
    return out


def _stack_heads(tile_a, tile_b):
    return jnp.concatenate([tile_a, pltpu.roll(tile_a, HEAD_DIM, 1), tile_b, pltpu.roll(tile_b, HEAD_DIM, 1)], axis=0)


def _unstack_heads(o4):
    return (o4[0:CHUNK] + pltpu.roll(o4[CHUNK:2 * CHUNK], HEAD_DIM, 1),
            o4[2 * CHUNK:3 * CHUNK] + pltpu.roll(o4[3 * CHUNK:], HEAD_DIM, 1))


ATT_SLAB = 32


def _softmax_slab(s_scr, hk, g, r0, bias_ref, sink_ref, n, nblk):
    scale = HEAD_DIM ** -0.5 * LOG2E
    h = (N_HEADS // N_KV) * hk + g
    s = []
    for jj in range(3):
        sj = (s_scr[hk, jj, pl.ds(g * CHUNK + r0, ATT_SLAB), :] * scale
              + bias_ref[h, pl.ds(r0, ATT_SLAB), jj * CHUNK:(jj + 1) * CHUNK])
        if jj == 0:
            sj = jnp.where(n > 0, sj, NEG)
        if jj == 2:
            sj = jnp.where(n < nblk - 1, sj, NEG)
        s.append(sj)
    sink = sink_ref[h] * LOG2E
    m = jnp.maximum(jnp.max(jnp.maximum(jnp.maximum(s[0], s[1]), s[2]), axis=-1, keepdims=True), sink)
    e = [jnp.exp2(sj - m) for sj in s]
    es = jnp.exp2(sink - m)
    inv = 1.0 / (jnp.sum(e[0] + e[1] + e[2], axis=-1, keepdims=True) + es)
    return [ej * inv for ej in e], es * inv


def _key_block_starts(n, nblk):
    return [pl.multiple_of(jnp.clip(n - 1 + jj, 0, nblk - 1) * CHUNK, CHUNK) for jj in range(3)]


def _attn_fwd(qkv, x2, bias, sink, gath):
    t, dm = x2.shape
    nblk = t // CHUNK
    kvw = N_KV * HEAD_DIM
    kcb, vcb = dm // kvw, dm // kvw + 1
    slab = (N_KV, 3, 4 * CHUNK, CHUNK)

    def body(q_ref, k_ref, v_ref, x2_ref, bias_ref, sink_ref, gath_ref, x3_ref, att_ref, p_ref, ps_ref,
             wbuf, s_scr, sems):
        n = pl.program_id(0)

        @pl.when(n == 0)
        def _():
            _load_weight(gath_ref, wbuf, sems)
        lo, _ = _half_masks()
        lane_s = lax.broadcasted_iota(jnp.int32, (ATT_SLAB, 128), 1)
        starts = _key_block_starts(n, nblk)
        tiles = []
        for hk in range(N_KV):
            c0 = (2 * hk) * 128
            k_lo = _kv_low(k_ref, starts, hk, lo)
            v_lo = _kv_low(v_ref, starts, hk, lo)
            q4 = _stack_heads(q_ref[:, c0:c0 + 128], q_ref[:, c0 + 128:c0 + 256])
            for jj in range(3):
                s_scr[hk, jj] = _dot(q4, k_lo[jj], NT)
            for g in range(4):
                h = 4 * hk + g
                for r0 in range(0, CHUNK, ATT_SLAB):
                    p, ps = _softmax_slab(s_scr, hk, g, r0, bias_ref, sink_ref, n, nblk)
                    for jj in range(3):
                        p_ref[hk, jj, g * CHUNK + r0:g * CHUNK + r0 + ATT_SLAB, :] = p[jj].astype(BF16)
                    rest = jnp.zeros((ATT_SLAB, 128), F32) if h == 0 else ps_ref[r0:r0 + ATT_SLAB, :]
                    ps_ref[r0:r0 + ATT_SLAB, :] = jnp.where(lane_s == h, ps, rest)
            o4 = _dot(p_ref[hk, 0], v_lo[0], NN) + _dot(p_ref[hk, 1], v_lo[1], NN) + _dot(p_ref[hk, 2], v_lo[2], NN)
            tiles += list(_unstack_heads(o4))
        att = jnp.concatenate(tiles, axis=1).astype(BF16)
        att_ref[...] = att
        x3_ref[...] = x2_ref[...] + _dot(att, wbuf[...], NN)

    blk = pl.BlockSpec((CHUNK, dm), lambda n: (n, 0))
    return dict(
        body=body, grid=(nblk,), name="attn_fwd", args=[qkv, qkv, qkv, x2, bias, sink, gath],
        out_shape=[jax.ShapeDtypeStruct((t, dm), F32), jax.ShapeDtypeStruct((t, dm), BF16),
                   jax.ShapeDtypeStruct((nblk,) + slab, BF16), jax.ShapeDtypeStruct((t, 128), F32)],
        in_specs=[blk, pl.BlockSpec((t, kvw), lambda n: (0, kcb)), pl.BlockSpec((t, kvw), lambda n: (0, vcb)), blk,
                  _full(bias.shape), pl.BlockSpec(memory_space=pltpu.SMEM), ANY],
        out_specs=[blk, blk, pl.BlockSpec((None,) + slab, lambda n: (n, 0, 0, 0, 0)),
                   pl.BlockSpec((CHUNK, 128), lambda n: (n, 0))],
        scratch=[pltpu.VMEM((gath.shape[1] * NDEV, dm), BF16), pltpu.VMEM(slab, F32),
                 pltpu.SemaphoreType.DMA((NDEV,))])


def _attn_bwd(qkv, att, probs, sink_probs, dx3, bias_shape, gath):
    t, dm = dx3.shape
    nblk = t // CHUNK
    kvw = N_KV * HEAD_DIM
    kcb, vcb = dm // kvw, dm // kvw + 1
    scale = HEAD_DIM ** -0.5
    slab = (N_KV, 3, 4 * CHUNK, CHUNK)

    def body(q_ref, k_ref, v_ref, att_ref, p_ref, ps_ref, dx_ref, gath_ref,
             dq_ref, dkb_ref, dvb_ref, dbias_ref, dsink_ref,
             wbuf, dp_scr, ds_scr, prod_scr, dsum_scr, dk_ref, dv_ref, sems):
        n = pl.program_id(0)

        @pl.when(n == 0)
        def _():
            _load_weight(gath_ref, wbuf, sems)
            dk_ref[...] = jnp.zeros_like(dk_ref)
            dv_ref[...] = jnp.zeros_like(dv_ref)
            dbias_ref[...] = jnp.zeros_like(dbias_ref)
            dsink_ref[...] = jnp.zeros_like(dsink_ref)
        lo, hi = _half_masks()
        lane_s = lax.broadcasted_iota(jnp.int32, (ATT_SLAB, 128), 1)
        starts = _key_block_starts(n, nblk)
        dout = _dot(dx_ref[...].astype(BF16), wbuf[...], NT)
        prod_scr[...] = dout * att_ref[...].astype(F32)
        doutb = dout.astype(BF16)
        dq_tiles = []
        for hk in range(N_KV):
            kt = (hk // 2) * 128
            c0 = (2 * hk) * 128
            k_lo = _kv_low(k_ref, starts, hk, lo)
            v_lo = _kv_low(v_ref, starts, hk, lo)
            q4 = _stack_heads(q_ref[:, c0:c0 + 128], q_ref[:, c0 + 128:c0 + 256])
            do4 = _stack_heads(doutb[:, c0:c0 + 128], doutb[:, c0 + 128:c0 + 256])
            for jj in range(3):
                dp_scr[hk, jj] = _dot(do4, v_lo[jj], NT)
            for g in range(4):
                h = 4 * hk + g
                for r0 in range(0, CHUNK, ATT_SLAB):
                    rows = slice(g * CHUNK + r0, g * CHUNK + r0 + ATT_SLAB)
                    pt = prod_scr[r0:r0 + ATT_SLAB, c0 + (g // 2) * 128:c0 + (g // 2 + 1) * 128]
                    msk = lane_s < HEAD_DIM if g % 2 == 0 else lane_s >= HEAD_DIM
                    dsum = jnp.sum(jnp.where(msk, pt, 0.0), axis=-1, keepdims=True)
                    rest = jnp.zeros((ATT_SLAB, 128), F32) if h == 0 else dsum_scr[r0:r0 + ATT_SLAB, :]
                    dsum_scr[r0:r0 + ATT_SLAB, :] = jnp.where(lane_s == h, dsum, rest)
                    for jj in range(3):
                        ds = p_ref[hk, jj, rows, :].astype(F32) * (dp_scr[hk, jj, rows, :] - dsum)
                        dbias_ref[h, r0:r0 + ATT_SLAB, jj * CHUNK:(jj + 1) * CHUNK] += ds
                        ds_scr[hk, jj, rows, :] = ds.astype(BF16)
            dq4 = jnp.zeros((4 * CHUNK, 128), F32)
            for jj in range(3):
                ds4 = ds_scr[hk, jj]
                dq4 = dq4 + _dot(ds4, k_lo[jj], NN) * scale
                dkj = _dot(ds4, q4, TN) * scale
                dvj = _dot(p_ref[hk, jj], do4, TN)
                if hk % 2 == 1:
                    dkj, dvj = pltpu.roll(dkj, HEAD_DIM, 1), pltpu.roll(dvj, HEAD_DIM, 1)
                keep = lo if hk % 2 == 0 else hi
                dk_ref[pl.ds(starts[jj], CHUNK), kt:kt + 128] += jnp.where(keep, dkj, 0.0)
                dv_ref[pl.ds(starts[jj], CHUNK), kt:kt + 128] += jnp.where(keep, dvj, 0.0)
            dq_tiles += list(_unstack_heads(dq4))
        dq_ref[...] = jnp.concatenate(dq_tiles, axis=1).astype(BF16)
        dsink_ref[...] -= jnp.sum(ps_ref[...] * dsum_scr[...], axis=0, keepdims=True)

        @pl.when(n == nblk - 1)
        def _():
            dkb_ref[...] = dk_ref[...].astype(BF16)
            dvb_ref[...] = dv_ref[...].astype(BF16)

    blk = pl.BlockSpec((CHUNK, dm), lambda n: (n, 0))
    return dict(
        body=body, grid=(nblk,), name="attn_bwd", args=[qkv, qkv, qkv, att, probs, sink_probs, dx3, gath],
        out_shape=[jax.ShapeDtypeStruct((t, dm), BF16), jax.ShapeDtypeStruct((t, kvw), BF16),
                   jax.ShapeDtypeStruct((t, kvw), BF16), jax.ShapeDtypeStruct(bias_shape, F32),
                   jax.ShapeDtypeStruct((1, 128), F32)],
        in_specs=[blk, pl.BlockSpec((t, kvw), lambda n: (0, kcb)), pl.BlockSpec((t, kvw), lambda n: (0, vcb)),
                  blk, pl.BlockSpec((None,) + slab, lambda n: (n, 0, 0, 0, 0)),
                  pl.BlockSpec((CHUNK, 128), lambda n: (n, 0)), blk, ANY],
        out_specs=[blk, _full((t, kvw)), _full((t, kvw)), _full(bias_shape), _full((1, 128))],
        scratch=[pltpu.VMEM((gath.shape[1] * NDEV, dm), BF16), pltpu.VMEM(slab, F32), pltpu.VMEM(slab, BF16),
                 pltpu.VMEM((CHUNK, dm), F32), pltpu.VMEM((CHUNK, 128), F32),
                 pltpu.VMEM((t, kvw), F32), pltpu.VMEM((t, kvw), F32), pltpu.SemaphoreType.DMA((NDEV,))])


def _finish_weight(recvs, w, m, v, name):
    nl, r, dm = w.shape
    assert nl == len(recvs) and all(rc.shape[1:] == (r, dm) for rc in recvs)
    td = dm // 2
    wspec = pl.BlockSpec((None, r, td), lambda l, j: (l, 0, j))

    def body(*refs):
        r_refs = refs[:nl]
        w_ref, m_ref, v_ref, g_ref, d_ref, nm_ref, nv_ref = refs[nl:]
        layer = pl.program_id(0)
        for li in range(nl):
            @pl.when(layer == li)
            def _():
                g = r_refs[li][0].astype(F32)
                for d in range(1, recvs[li].shape[0]):
                    g = g + r_refs[li][d].astype(F32)
                delta, nm, nv = _adamw_math(w_ref[...], g, m_ref[...], v_ref[...])
                g_ref[...] = g
                d_ref[...] = delta
                nm_ref[...] = nm
                nv_ref[...] = nv

    o = jax.ShapeDtypeStruct(w.shape, F32)
    return dict(
        body=body, grid=(nl, dm // td), name=name, args=[*recvs, w, m, v], out_shape=[o, o, o, o],
        in_specs=[pl.BlockSpec((rc.shape[0], r, td), lambda l, j: (0, 0, j)) for rc in recvs] + [wspec] * 3,
        out_specs=[wspec] * 4)


def _adamw_small(ws, ms, vs, slots, late_slots, loss_slots, name):
    n = len(ws)

    def total(ref):
        acc = ref[0].astype(F32)
        for d in range(1, NDEV):
            acc = acc + ref[d].astype(F32)
        return acc

    def body(*refs):
        ins, outs = refs[:4 * n + 2], refs[4 * n + 2:]
        for i in range(n):
            w_ref, m_ref, v_ref, s_ref = ins[4 * i:4 * i + 4]
            g_ref, d_ref, nm_ref, nv_ref = outs[4 * i:4 * i + 4]
            g_ref[...] = total(s_ref)
            if i == 0:
                g_ref[0:1, :] = total(ins[4 * n])
            d_ref[...], nm_ref[...], nv_ref[...] = _adamw_math(w_ref[...], g_ref[...], m_ref[...], v_ref[...])
        outs[4 * n][...] = total(ins[4 * n + 1])

    args, out_shape = [], []
    for w, m, v, s in zip(ws, ms, vs, slots):
        args += [w, m, v, s]
        out_shape += [jax.ShapeDtypeStruct(w.shape, F32)] * 4
    args += [late_slots, loss_slots]
    out_shape.append(jax.ShapeDtypeStruct((1, 1), F32))
    out = pl.pallas_call(
        body, grid=(1,), out_shape=tuple(out_shape), in_specs=[_full(a.shape) for a in args],
        out_specs=tuple(_full(o.shape) for o in out_shape), compiler_params=_cp(), name=name)(*args)
    return [tuple(out[4 * i:4 * i + 4]) for i in range(n)], out[4 * n]


def kernel(x, norm_mix, norm_ffn, even_w_in, even_v_ln_g, even_v_ln_b, even_w_spatial, even_b_spatial, even_conv_w, even_w_out, attn_w_qkv, attn_sink, rel_bias, attn_w_out, ffn_w_gate, ffn_w_up, ffn_w_down, final_norm, loss_target, m_norm_mix, m_norm_ffn, m_even_w_in, m_even_v_ln_g, m_even_v_ln_b, m_even_w_spatial, m_even_b_spatial, m_even_conv_w, m_even_w_out, m_attn_w_qkv, m_attn_sink, m_rel_bias, m_attn_w_out, m_ffn_w_gate, m_ffn_w_up, m_ffn_w_down, m_final_norm, v_norm_mix, v_norm_ffn, v_even_w_in, v_even_v_ln_g, v_even_v_ln_b, v_even_w_spatial, v_even_b_spatial, v_even_conv_w, v_even_w_out, v_attn_w_qkv, v_attn_sink, v_rel_bias, v_attn_w_out, v_ffn_w_gate, v_ffn_w_up, v_ffn_w_down, v_final_norm):
    t, dm = x.shape[1], x.shape[2]
    aw = even_v_ln_g.shape[1]
    bw = even_conv_w.shape[2] * NDEV
    gd = aw // A_GROUPS
    tm = min(512, t // 2)
    tmf = min(256, t // 2)
    me = _my_index()
    row = lambda a: a.reshape(1, -1)

    colT = lambda w: w.T.astype(BF16)
    sh = dict(winT=colT(even_w_in[0]), wqkvT=colT(attn_w_qkv[0]), wgT0=colT(ffn_w_gate[0]), wuT0=colT(ffn_w_up[0]),
              wgT1=colT(ffn_w_gate[1]), wuT1=colT(ffn_w_up[1]), woe=even_w_out[0].astype(BF16),
              woa=attn_w_out[0].astype(BF16), wd0=ffn_w_down[0].astype(BF16), wd1=ffn_w_down[1].astype(BF16))
    gather = lambda names: _GatherCarry([sh[n] for n in names])

    in_full = lambda a: lax.dynamic_update_slice(jnp.zeros((3, bw), F32), a[0], (0, me * (bw // NDEV)))

    x0 = x[0]
    wsp_b = even_w_spatial[0].astype(BF16)
    bspb = jnp.broadcast_to(even_b_spatial[0][:, :, None], (A_GROUPS, CHUNK, gd))
    buckets = _bucket_table()
    sink = attn_sink[0]

    (bias,), ((g_winT,), (cw_slots,)) = _call(
        _bias_table(rel_bias.T, buckets), [gather(["winT"]), _BroadcastCarry([in_full(even_conv_w)])])
    cw_full = jnp.sum(cw_slots, axis=0)
    (proj, h0b), (g_woe, g_wgT0) = _call(_norm_proj(x0, row(norm_mix[0]), g_winT, F32, "in_proj", tm),
                                         gather(["woe", "wgT0"]))
    (x1, yb), (g_wuT0,) = _call(_even_core_fwd(proj, x0, even_v_ln_g, even_v_ln_b, wsp_b, bspb, cw_full, g_woe, tm),
                                gather(["wuT0"]))
    (gate0, up0, act0), (g_wd0,) = _call(_ffn_up(x1, row(norm_ffn[0]), g_wgT0, g_wuT0, "ffn_up0", tmf), gather(["wd0"]))
    (x2,), (g_wqkvT,) = _call(_ffn_down(x1, act0, g_wd0, "ffn_down0", tm), gather(["wqkvT"]))
    (qkv, h2b), (g_woa,) = _call(_norm_proj(x2, row(norm_mix[1]), g_wqkvT, BF16, "qkv_proj", tm), gather(["woa"]))
    (x3, attb, probs, sink_probs), (g_wgT1, g_wuT1) = _call(
        _attn_fwd(qkv, x2, bias, sink, g_woa), gather(["wgT1", "wuT1"]))
    (gate1, up1, act1), (g_wd1,) = _call(_ffn_up(x3, row(norm_ffn[1]), g_wgT1, g_wuT1, "ffn_up1", tmf), gather(["wd1"]))
    (loss_part, dx4, dx4b, d_final), _ = _call(
        _ffn_down_loss(x3, act1, g_wd1, loss_target[0], row(final_norm), "ffn_down1_loss", tm))

    (dx3, dx3b, dg1, du1, h3b, d_nffn1), _ = _call(
        _ffn_bwd(dx4, x3, gate1, up1, row(norm_ffn[1]), g_wgT1, g_wuT1, g_wd1, "ffn_bwd1", tmf))
    (p_wgT1,), _ = _call(_wgrad(dg1, h3b, "wgrad_gate1"))
    (p_wuT1,), _ = _call(_wgrad(du1, h3b, "wgrad_up1"))
    (p_wd1,), ((a_wgT1,), (a_wuT1,)) = _call(
        _wgrad(act1, dx4b, "wgrad_down1"), [_PairCarry(p_wgT1), _PairCarry(p_wuT1)])
    (dq, dk, dv, dbias, dsink), ((r_wgT1,), (r_wuT1,), (a_wd1,)) = _call(
        _attn_bwd(qkv, attb, probs, sink_probs, dx3, bias.shape, g_woa),
        [_ChipSumCarry(p_wgT1, a_wgT1, staged=1), _ChipSumCarry(p_wuT1, a_wuT1, staged=1), _PairCarry(p_wd1)])
    (p_woa,), _ = _call(_wgrad(attb, dx3b, "wgrad_attn_out"))
    d_relb = _rel_bias_grad(dbias, buckets)[:, 0:N_BUCKETS].T
    (dx2, dx2b, d_nmix1), _ = _call(
        _proj_bwd_norm([dq, dk, dv], x2, row(norm_mix[1]), dx3, g_wqkvT, "qkv_bwd", tm))
    (p_wqkvT,), _ = _call(_wgrad([dq, dk, dv], h2b, "wgrad_qkv"))
    (dx1, dx1b, dg0, du0, h1b, d_nffn0), ((r_wd1,), (r_woa, r_wqkvT)) = _call(
        _ffn_bwd(dx2, x1, gate0, up0, row(norm_ffn[0]), g_wgT0, g_wuT0, g_wd0, "ffn_bwd0", tmf),
        [_ChipSumCarry(p_wd1, a_wd1), _GradCarry([p_woa, p_wqkvT])])
    (p_wgT0,), _ = _call(_wgrad(dg0, h1b, "wgrad_gate0"))
    (p_wuT0,), (a_wgT0,) = _call(_wgrad(du0, h1b, "wgrad_up0"), _PairCarry(p_wgT0))
    (p_wd0,), ((r_wgT0,), (a_wuT0,)) = _call(
        _wgrad(act0, dx2b, "wgrad_down0"), [_ChipSumCarry(p_wgT0, a_wgT0), _PairCarry(p_wuT0)])
    (dproj, d_lng, d_lnb, d_wsp, d_bsp3, d_cw), ((r_wuT0,), (a_wd0,)) = _call(
        _even_core_bwd(proj, dx1, even_v_ln_g, even_v_ln_b, wsp_b, bspb, cw_full, g_woe, tm),
        [_ChipSumCarry(p_wuT0, a_wuT0), _PairCarry(p_wd0)])
    small_names = ["norm_mix", "norm_ffn", "even_v_ln_g", "even_v_ln_b", "even_w_spatial", "even_b_spatial",
                   "even_conv_w", "attn_sink", "rel_bias", "final_norm"]
    small_parts = [jnp.concatenate([jnp.zeros_like(d_nmix1), d_nmix1]), jnp.concatenate([d_nffn0, d_nffn1]),
                   d_lng, d_lnb, d_wsp[None].astype(BF16), jnp.sum(d_bsp3, axis=-1)[None], d_cw,
                   dsink[:, 0:N_HEADS], d_relb, d_final, loss_part]
    (p_winT,), (r_wd0,) = _call(_wgrad(dproj, h0b, "wgrad_in"), _ChipSumCarry(p_wd0, a_wd0))
    (p_woe,), ((a_winT,), small_slots) = _call(
        _wgrad(yb, dx1b, "wgrad_even_out"), [_PairCarry(p_winT), _BroadcastCarry(small_parts)])
    (dx0, _, d_nmix0), ((r_winT,), (r_woe,)) = _call(
        _proj_bwd_norm([dproj], x0, row(norm_mix[0]), dx1, g_winT, "in_proj_bwd", tm),
        [_ChipSumCarry(p_winT, a_winT), _GradCarry([p_woe])])

    grads = {}

    order = ["norm_mix", "norm_ffn", "even_w_in", "even_v_ln_g", "even_v_ln_b", "even_w_spatial", "even_b_spatial",
             "even_conv_w", "even_w_out", "attn_w_qkv", "attn_sink", "rel_bias", "attn_w_out", "ffn_w_gate",
             "ffn_w_up", "ffn_w_down", "final_norm"]
    ws = dict(norm_mix=norm_mix, norm_ffn=norm_ffn, even_w_in=even_w_in, even_v_ln_g=even_v_ln_g,
              even_v_ln_b=even_v_ln_b, even_w_spatial=even_w_spatial, even_b_spatial=even_b_spatial,
              even_conv_w=even_conv_w, even_w_out=even_w_out, attn_w_qkv=attn_w_qkv, attn_sink=attn_sink,
              rel_bias=rel_bias, attn_w_out=attn_w_out, ffn_w_gate=ffn_w_gate, ffn_w_up=ffn_w_up,
              ffn_w_down=ffn_w_down, final_norm=final_norm)
    ms = dict(norm_mix=m_norm_mix, norm_ffn=m_norm_ffn, even_w_in=m_even_w_in, even_v_ln_g=m_even_v_ln_g,
              even_v_ln_b=m_even_v_ln_b, even_w_spatial=m_even_w_spatial, even_b_spatial=m_even_b_spatial,
              even_conv_w=m_even_conv_w, even_w_out=m_even_w_out, attn_w_qkv=m_attn_w_qkv, attn_sink=m_attn_sink,
              rel_bias=m_rel_bias, attn_w_out=m_attn_w_out, ffn_w_gate=m_ffn_w_gate, ffn_w_up=m_ffn_w_up,
              ffn_w_down=m_ffn_w_down, final_norm=m_final_norm)
    vs = dict(norm_mix=v_norm_mix, norm_ffn=v_norm_ffn, even_w_in=v_even_w_in, even_v_ln_g=v_even_v_ln_g,
              even_v_ln_b=v_even_v_ln_b, even_w_spatial=v_even_w_spatial, even_b_spatial=v_even_b_spatial,
              even_conv_w=v_even_conv_w, even_w_out=v_even_w_out, attn_w_qkv=v_attn_w_qkv, attn_sink=v_attn_sink,
              rel_bias=v_rel_bias, attn_w_out=v_attn_w_out, ffn_w_gate=v_ffn_w_gate, ffn_w_up=v_ffn_w_up,
              ffn_w_down=v_ffn_w_down, final_norm=v_final_norm)
    big = dict(ffn_w_gate=([r_wgT0, r_wgT1], True), even_w_in=([r_winT], True), even_w_out=([r_woe], False),
               attn_w_qkv=([r_wqkvT], True), attn_w_out=([r_woa], False), ffn_w_up=([r_wuT0, r_wuT1], True),
               ffn_w_down=([r_wd0, r_wd1], False))
    delta, new_m, new_v = {}, {}, {}
    late_slots = None
    for n, (recvs, transposed) in big.items():
        lay = (lambda a: jnp.swapaxes(a, 1, 2)) if transposed else (lambda a: a)
        spec = _finish_weight(recvs, lay(ws[n]), lay(ms[n]), lay(vs[n]), "finish_" + n)
        if late_slots is None:
            outs, (late_slots,) = _call(spec, _BroadcastCarry([d_nmix0]))
        else:
            outs, _ = _call(spec)
        grads[n], delta[n], new_m[n], new_v[n] = [lay(o) for o in outs]
    shaped = lambda n, a: in_full(a) if n == "even_conv_w" else (a.reshape(1, dm) if n == "final_norm" else a)
    pick = lambda dct: [shaped(n, dct[n]) for n in small_names]
    results, loss11 = _adamw_small(pick(ws), pick(ms), pick(vs), small_slots[:-1], late_slots, small_slots[-1],
                                   "adamw_small")
    mine = lambda a: lax.dynamic_slice(a, (0, me * (bw // NDEV)), (3, bw // NDEV))[None]
    for n, res in zip(small_names, results):
        for dst, a in zip((grads, delta, new_m, new_v), res):
            dst[n] = mine(a) if n == "even_conv_w" else (a.reshape(dm) if n == "final_norm" else a)
    loss = loss11[0, 0]
    return (loss, dx0[None], *[grads[n] for n in order], *[delta[n] for n in order],
            *[new_m[n] for n in order], *[new_v[n] for n in order])
```

```python
import math

import jax
import jax.numpy as jnp
import numpy as np
from jax import lax
from jax.experimental import pallas as pl
from jax.experimental.pallas import tpu as pltpu

F32, BF16 = jnp.float32, jnp.bfloat16
NDEV = 8
EPS = 1e-6
CHUNK = 128
A_GROUPS = 4
N_HEADS, N_KV, HEAD_DIM = 16, 4, 64
N_BUCKETS, MAX_DISTANCE = 32, 128
NEG = -1e30
LOG2E = 1.4426950408889634
ADAM_LR, ADAM_B1, ADAM_B2, ADAM_EPS, ADAM_WD, ADAM_STEP = 0.001, 0.9, 0.999, 1e-08, 0.01, 10
VMEM_LIMIT = 56 * 1024 * 1024
MESH = pl.DeviceIdType.MESH
NT = (((1,), (1,)), ((), ()))
NN = (((1,), (0,)), ((), ()))
TN = (((0,), (0,)), ((), ()))
ANY = pl.BlockSpec(memory_space=pl.ANY)


def _cp(n_grid=1):
    return pltpu.CompilerParams(dimension_semantics=("arbitrary",) * n_grid, vmem_limit_bytes=VMEM_LIMIT)


def _dot(a, b, dims):
    return lax.dot_general(a, b, dims, preferred_element_type=F32)


def _my_index():
    return 4 * lax.axis_index("x") + 2 * lax.axis_index("y") + lax.axis_index("c")


def _peer(k):
    x, y, c = lax.axis_index("x"), lax.axis_index("y"), lax.axis_index("c")
    px = 1 - x if k & 4 else x
    py = 1 - y if k & 2 else y
    pc = 1 - c if k & 1 else c
    return (px, py, pc)


def _load_weight(gath_ref, wbuf, sems):
    rows = gath_ref.shape[1]
    cps = [pltpu.make_async_copy(gath_ref.at[d], wbuf.at[pl.ds(d * rows, rows), :], sems.at[d]) for d in range(NDEV)]
    for c in cps:
        c.start()
    for c in cps:
        c.wait()


class _GatherCarry:
    def __init__(self, pieces):
        self.inputs = list(pieces)
        self.n = len(pieces)
        self.out_shape = [jax.ShapeDtypeStruct((NDEV,) + p.shape, p.dtype) for p in pieces]
        self.scratch = [pltpu.SemaphoreType.DMA((7 * self.n,)), pltpu.SemaphoreType.DMA((7 * self.n,)),
                        pltpu.SemaphoreType.DMA((self.n,))]

    def _ctx(self):
        x, y, c = lax.axis_index("x"), lax.axis_index("y"), lax.axis_index("c")
        chips = [(1 - x, y), (x, 1 - y), (1 - x, 1 - y)]
        return (x, y, c), (x, y, 1 - c), chips, c

    def _copy(self, k, j, block, to, ins, outs, sems, src=None):
        send_sems, recv_sems, _ = sems
        slot = outs[j].at[4 * block[0] + 2 * block[1] + block[2]]
        return pltpu.make_async_remote_copy(
            src_ref=slot if src is None else src, dst_ref=slot, send_sem=send_sems.at[k * self.n + j],
            recv_sem=recv_sems.at[k * self.n + j], device_id=to, device_id_type=MESH)

    def start(self, ins, outs, sems):
        me, sibling, chips, c = self._ctx()
        for j in range(self.n):
            pltpu.make_async_copy(ins[j], outs[j].at[4 * me[0] + 2 * me[1] + me[2]], sems[2].at[j]).start()
            self._copy(0, j, me, sibling, ins, outs, sems, src=ins[j]).start()
            for q, chip in enumerate(chips):
                self._copy(1 + q, j, me, (*chip, c), ins, outs, sems, src=ins[j]).start()

    def mid(self, ins, outs, sems):
        me, sibling, chips, c = self._ctx()
        for q, chip in enumerate(chips):
            for j in range(self.n):
                self._copy(1 + q, j, (*chip, c), me, ins, outs, sems).wait_recv()
                self._copy(4 + q, j, (*chip, c), sibling, ins, outs, sems).start()

    def finish(self, ins, outs, sems):
        me, sibling, chips, c = self._ctx()
        for j in range(self.n):
            self._copy(0, j, sibling, me, ins, outs, sems).wait_recv()
            for q, chip in enumerate(chips):
                self._copy(4 + q, j, (*chip, 1 - c), me, ins, outs, sems).wait_recv()
        for j in range(self.n):
            self._copy(0, j, me, sibling, ins, outs, sems, src=ins[j]).wait_send()
            for q, chip in enumerate(chips):
                self._copy(1 + q, j, me, (*chip, c), ins, outs, sems, src=ins[j]).wait_send()
                self._copy(4 + q, j, (*chip, c), sibling, ins, outs, sems).wait_send()
            pltpu.make_async_copy(ins[j], outs[j].at[0], sems[2].at[j]).wait()


class _GradCarry:
    def __init__(self, pieces):
        self.inputs = list(pieces)
        self.n = len(pieces)
        self.rows = [p.shape[0] // NDEV for p in pieces]
        self.out_shape = [jax.ShapeDtypeStruct((NDEV, r, p.shape[1]), p.dtype) for p, r in zip(pieces, self.rows)]
        self.scratch = [pltpu.SemaphoreType.DMA((7 * self.n,)), pltpu.SemaphoreType.DMA((7 * self.n,)),
                        pltpu.SemaphoreType.DMA((self.n,))]

    def _copies(self, ins, outs, sems):
        me = _my_index()
        local, remote = [], []
        for j in range(self.n):
            r = self.rows[j]
            local.append(pltpu.make_async_copy(ins[j].at[pl.ds(pl.multiple_of(me * r, 16), r), :], outs[j].at[me],
                                               sems[2].at[j]))
            for k in range(1, NDEV):
                peer = _peer(k)
                pidx = 4 * peer[0] + 2 * peer[1] + peer[2]
                remote.append(pltpu.make_async_remote_copy(
                    src_ref=ins[j].at[pl.ds(pl.multiple_of(pidx * r, 16), r), :], dst_ref=outs[j].at[me],
                    send_sem=sems[0].at[(k - 1) * self.n + j], recv_sem=sems[1].at[(k - 1) * self.n + j],
                    device_id=peer, device_id_type=MESH))
        return local, remote

    def start(self, ins, outs, sems):
        local, remote = self._copies(ins, outs, sems)
        for cp in local + remote:
            cp.start()

    def mid(self, ins, outs, sems):
        pass

    def finish(self, ins, outs, sems):
        local, remote = self._copies(ins, outs, sems)
        for cp in remote + local:
            cp.wait()


class _BroadcastCarry:
    def __init__(self, parts):
        self.inputs = list(parts)
        self.n = len(self.inputs)
        self.out_shape = [jax.ShapeDtypeStruct((NDEV,) + p.shape, p.dtype) for p in self.inputs]
        self.scratch = [pltpu.SemaphoreType.DMA((7 * self.n,)), pltpu.SemaphoreType.DMA((7 * self.n,)),
                        pltpu.SemaphoreType.DMA((self.n,))]

    def _copies(self, ins, outs, sems):
        me = _my_index()
        cps = []
        for j in range(self.n):
            cps.append(pltpu.make_async_copy(ins[j], outs[j].at[me], sems[2].at[j]))
            cps += [pltpu.make_async_remote_copy(
                src_ref=ins[j], dst_ref=outs[j].at[me], send_sem=sems[0].at[(k - 1) * self.n + j],
                recv_sem=sems[1].at[(k - 1) * self.n + j], device_id=_peer(k), device_id_type=MESH)
                for k in range(1, NDEV)]
        return cps

    def start(self, ins, outs, sems):
        for cp in self._copies(ins, outs, sems):
            cp.start()

    def mid(self, ins, outs, sems):
        pass

    def finish(self, ins, outs, sems):
        for cp in self._copies(ins, outs, sems):
            cp.wait()


class _PairCarry:
    def __init__(self, piece):
        self.inputs = [piece]
        self.r = piece.shape[0] // NDEV
        self.out_shape = [jax.ShapeDtypeStruct((4, self.r, piece.shape[1]), piece.dtype)]
        self.scratch = [pltpu.SemaphoreType.DMA((4,)), pltpu.SemaphoreType.DMA((4,))]

    def _copies(self, ins, outs, sems):
        x, y, c = lax.axis_index("x"), lax.axis_index("y"), lax.axis_index("c")
        return [pltpu.make_async_remote_copy(
            src_ref=ins[0].at[pl.ds(pl.multiple_of((2 * q + 1 - c) * self.r, 16), self.r), :], dst_ref=outs[0].at[q],
            send_sem=sems[0].at[q], recv_sem=sems[1].at[q], device_id=(x, y, 1 - c), device_id_type=MESH)
            for q in range(4)]

    def start(self, ins, outs, sems):
        for cp in self._copies(ins, outs, sems):
            cp.start()

    def mid(self, ins, outs, sems):
        pass

    def finish(self, ins, outs, sems):
        for cp in self._copies(ins, outs, sems):
            cp.wait()


class _ChipSumCarry:
    def __init__(self, piece, landed):
        self.inputs = [piece, landed]
        self.r, dm = piece.shape[0] // NDEV, piece.shape[1]
        self.out_shape = [jax.ShapeDtypeStruct((4, self.r, dm), piece.dtype)]
        self.scratch = [pltpu.VMEM((4, self.r, dm), piece.dtype), pltpu.VMEM((8, self.r, dm), piece.dtype),
                        pltpu.SemaphoreType.DMA((8,)), pltpu.SemaphoreType.DMA((3,)), pltpu.SemaphoreType.DMA((3,)),
                        pltpu.SemaphoreType.DMA(())]

    def _copies(self, outs, scr):
        sums, _, _, send_sems, recv_sems, local_sem = scr
        x, y, c = lax.axis_index("x"), lax.axis_index("y"), lax.axis_index("c")
        mine = 2 * x + y
        local = pltpu.make_async_copy(sums.at[mine], outs[0].at[mine], local_sem)
        remote = []
        for k in range(1, 4):
            px = 1 - x if k & 2 else x
            py = 1 - y if k & 1 else y
            remote.append(pltpu.make_async_remote_copy(
                src_ref=sums.at[2 * px + py], dst_ref=outs[0].at[mine], send_sem=send_sems.at[k - 1],
                recv_sem=recv_sems.at[k - 1], device_id=(px, py, c), device_id_type=MESH))
        return local, remote

    def _loads(self, ins, scr):
        stage, stage_sems = scr[1], scr[2]
        c = lax.axis_index("c")
        return [(pltpu.make_async_copy(ins[0].at[pl.ds(pl.multiple_of((2 * q + c) * self.r, 16), self.r), :],
                                       stage.at[2 * q], stage_sems.at[2 * q]),
                 pltpu.make_async_copy(ins[1].at[q], stage.at[2 * q + 1], stage_sems.at[2 * q + 1]))
                for q in range(4)]

    def start(self, ins, outs, scr):
        for a, b in self._loads(ins, scr):
            a.start()
            b.start()

    def after_first_step(self, ins, outs, scr):
        sums, stage = scr[0], scr[1]
        for q, (a, b) in enumerate(self._loads(ins, scr)):
            a.wait()
            b.wait()
            sums[q] = (stage[2 * q].astype(F32) + stage[2 * q + 1].astype(F32)).astype(sums.dtype)
        local, remote = self._copies(outs, scr)
        for cp in [local] + remote:
            cp.start()

    def mid(self, ins, outs, scr):
        pass

    def finish(self, ins, outs, scr):
        local, remote = self._copies(outs, scr)
        for cp in remote + [local]:
            cp.wait()


def _call(spec, carry=None):
    body, grid = spec["body"], spec["grid"]
    in_specs, out_specs, out_shape = list(spec["in_specs"]), list(spec["out_specs"]), list(spec["out_shape"])
    scratch, args = list(spec.get("scratch", [])), list(spec["args"])
    if carry is None:
        out = pl.pallas_call(body, grid=grid, in_specs=in_specs, out_specs=tuple(out_specs),
                             out_shape=tuple(out_shape), scratch_shapes=scratch, compiler_params=_cp(len(grid)),
                             name=spec["name"])(*args)
        return tuple(out), ()
    carries = list(carry) if isinstance(carry, (list, tuple)) else [carry]
    n_in, n_out, n_s = len(in_specs), len(out_specs), len(scratch)
    steps = int(np.prod(grid))

    def split(refs, counts):
        parts, o = [], 0
        for cnt in counts:
            parts.append(refs[o:o + cnt])
            o += cnt
        return parts

    c_in = [len(cr.inputs) for cr in carries]
    c_out = [len(cr.out_shape) for cr in carries]
    c_scr = [len(cr.scratch) for cr in carries]

    def wrapped(*refs):
        ins, cins, outs, couts, scr, cscr = split(refs, [n_in, sum(c_in), n_out, sum(c_out), n_s, sum(c_scr)])
        per = list(zip(carries, split(cins, c_in), split(couts, c_out), split(cscr, c_scr)))
        step = pl.program_id(0)
        for ax in range(1, len(grid)):
            step = step * grid[ax] + pl.program_id(ax)

        @pl.when(step == 0)
        def _():
            for cr, ci, co, cs in per:
                cr.start(ci, co, cs)
        if steps >= 3:
            @pl.when(step == steps - 2)
            def _():
                for cr, ci, co, cs in per:
                    cr.mid(ci, co, cs)
        body(*ins, *outs, *scr)
        second = [p for p in per if hasattr(p[0], "after_first_step")]
        if second:
            @pl.when(step == 0)
            def _():
                for cr, ci, co, cs in second:
                    cr.after_first_step(ci, co, cs)

        @pl.when(step == steps - 1)
        def _():
            for cr, ci, co, cs in per:
                if steps < 3:
                    cr.mid(ci, co, cs)
                cr.finish(ci, co, cs)

    out = pl.pallas_call(
        wrapped, grid=grid, in_specs=in_specs + [ANY] * sum(c_in), out_specs=tuple(out_specs + [ANY] * sum(c_out)),
        out_shape=tuple(out_shape + [s for cr in carries for s in cr.out_shape]),
        scratch_shapes=scratch + [s for cr in carries for s in cr.scratch],
        compiler_params=_cp(len(grid)), name=spec["name"])(*args, *[a for cr in carries for a in cr.inputs])
    c_res = [tuple(p) for p in split(out[n_out:], c_out)]
    return tuple(out[:n_out]), (c_res if isinstance(carry, (list, tuple)) else c_res[0])


def _rms_fwd(x, gain):
    r = lax.rsqrt(jnp.mean(x * x, axis=-1, keepdims=True) + EPS)
    return x * r * gain, r


def _rms_bwd(dh, x, r, gain):
    a = dh * gain
    dx = r * a - x * (r * r * r) * jnp.mean(a * x, axis=-1, keepdims=True)
    dgain = jnp.sum(dh * (x * r), axis=0, keepdims=True)
    return dx, dgain


def _gelu(x):
    return 0.5 * x * (1.0 + lax.erf(x * 0.7071067811865476))


def _gelu_grad(x):
    return 0.5 * (1.0 + lax.erf(x * 0.7071067811865476)) + x * jnp.exp(-0.5 * x * x) * 0.3989422804014327


def _sigmoid(x):
    return 1.0 / (1.0 + jnp.exp(-x))


def _adamw_math(w, g, m, v):
    nm = ADAM_B1 * m + (1.0 - ADAM_B1) * g
    nv = ADAM_B2 * v + (1.0 - ADAM_B2) * (g * g)
    m_hat = nm / (1.0 - ADAM_B1 ** ADAM_STEP)
    v_hat = nv / (1.0 - ADAM_B2 ** ADAM_STEP)
    return -ADAM_LR * (m_hat / (jnp.sqrt(v_hat) + ADAM_EPS) + ADAM_WD * w), nm, nv


def _tok(tm, w):
    return pl.BlockSpec((tm, w), lambda i: (i, 0))


def _full(shape):
    return pl.BlockSpec(shape, lambda *i: (0,) * len(shape))


def _norm_proj(x, gain, gath, out_dtype, name, tm):
    t, dm = x.shape
    n = gath.shape[1] * NDEV

    def body(x_ref, g_ref, gath_ref, proj_ref, hb_ref, wbuf, sems):
        @pl.when(pl.program_id(0) == 0)
        def _():
            _load_weight(gath_ref, wbuf, sems)
        h, _ = _rms_fwd(x_ref[...], g_ref[...])
        hb = h.astype(BF16)
        hb_ref[...] = hb
        proj_ref[...] = _dot(hb, wbuf[...], NT).astype(out_dtype)

    return dict(
        body=body, grid=(t // tm,), name=name, args=[x, gain, gath],
        out_shape=[jax.ShapeDtypeStruct((t, n), out_dtype), jax.ShapeDtypeStruct((t, dm), BF16)],
        in_specs=[_tok(tm, dm), _full((1, dm)), ANY], out_specs=[_tok(tm, n), _tok(tm, dm)],
        scratch=[pltpu.VMEM((n, dm), BF16), pltpu.SemaphoreType.DMA((NDEV,))])


def _proj_bwd_norm(dys, x, gain, dres, gath, name, tm):
    t, dm = x.shape
    n = gath.shape[1] * NDEV
    widths = [d.shape[1] for d in dys]
    assert sum(widths) == n
    nd = len(dys)

    def body(*refs):
        dy_refs = refs[:nd]
        x_ref, g_ref, dres_ref, gath_ref, dx_ref, dxb_ref, dgain_ref, wbuf, sems = refs[nd:]

        @pl.when(pl.program_id(0) == 0)
        def _():
            _load_weight(gath_ref, wbuf, sems)
            dgain_ref[...] = jnp.zeros_like(dgain_ref)
        xv, gain_v = x_ref[...], g_ref[...]
        _, r = _rms_fwd(xv, gain_v)
        dh, c0 = None, 0
        for dy_ref, wd in zip(dy_refs, widths):
            part = _dot(dy_ref[...], wbuf[c0:c0 + wd, :], NN)
            dh = part if dh is None else dh + part
            c0 += wd
        dx, dgain = _rms_bwd(dh, xv, r, gain_v)
        dx = dres_ref[...] + dx
        dx_ref[...] = dx
        dxb_ref[...] = dx.astype(BF16)
        dgain_ref[...] += dgain

    return dict(
        body=body, grid=(t // tm,), name=name, args=[*dys, x, gain, dres, gath],
        out_shape=[jax.ShapeDtypeStruct((t, dm), F32), jax.ShapeDtypeStruct((t, dm), BF16),
                   jax.ShapeDtypeStruct((1, dm), F32)],
        in_specs=[_tok(tm, wd) for wd in widths] + [_tok(tm, dm), _full((1, dm)), _tok(tm, dm), ANY],
        out_specs=[_tok(tm, dm), _tok(tm, dm), _full((1, dm))],
        scratch=[pltpu.VMEM((n, dm), BF16), pltpu.SemaphoreType.DMA((NDEV,))])


def _wgrad(a, b, name, tmm=256):
    parts = list(a) if isinstance(a, (list, tuple)) else [a]
    t = parts[0].shape[0]
    n = b.shape[1]
    tiles = [p.shape[1] // tmm for p in parts]
    first = [sum(tiles[:i]) for i in range(len(parts))]
    m = sum(tiles) * tmm

    def body(*refs):
        a_refs, b_ref, o_ref = refs[:len(parts)], refs[len(parts)], refs[len(parts) + 1]
        j = pl.program_id(0)
        for a_ref, j0, nt in zip(a_refs, first, tiles):
            if len(parts) == 1:
                o_ref[...] = _dot(a_ref[...], b_ref[...], TN).astype(BF16)
            else:
                @pl.when((j >= j0) & (j < j0 + nt))
                def _():
                    o_ref[...] = _dot(a_ref[...], b_ref[...], TN).astype(BF16)

    a_specs = [pl.BlockSpec((t, tmm), lambda j, j0=j0, nt=nt: (0, jnp.clip(j - j0, 0, nt - 1)))
               for j0, nt in zip(first, tiles)]
    return dict(
        body=body, grid=(sum(tiles),), name=name, args=[*parts, b], out_shape=[jax.ShapeDtypeStruct((m, n), BF16)],
        in_specs=a_specs + [pl.BlockSpec((t, n), lambda j: (0, 0))],
        out_specs=[pl.BlockSpec((tmm, n), lambda j: (j, 0))])


def _halo_specs(tm, t, width, col_blocks):
    nb8 = tm // 8
    last = t // 8 - 1
    prev = [pl.BlockSpec((8, width), lambda i, cb=cb: (jnp.maximum(i * nb8 - 1, 0), cb)) for cb in col_blocks]
    nxt = [pl.BlockSpec((8, width), lambda i, cb=cb: (jnp.minimum((i + 1) * nb8, last), cb)) for cb in col_blocks]
    return prev, nxt


def _shift_rows(z, prev_row, next_row):
    tm = z.shape[0]
    row = lax.broadcasted_iota(jnp.int32, z.shape, 0)
    zm1 = jnp.where(row == 0, prev_row, pltpu.roll(z, 1, 0))
    zp1 = jnp.where(row == tm - 1, next_row, pltpu.roll(z, tm - 1, 0))
    return zm1, zp1


def _gating_fwd(proj, lng, lnb, wsp_ref, bsp_ref, aw):
    tm = proj.shape[0]
    a_u = _gelu(proj[:, 0:aw])
    gv = _gelu(proj[:, aw:2 * aw])
    mu = jnp.mean(gv, axis=-1, keepdims=True)
    xc = gv - mu
    rstd = lax.rsqrt(jnp.mean(xc * xc, axis=-1, keepdims=True) + EPS)
    vn = xc * rstd
    a_v = (vn * lng + lnb).astype(BF16)
    gd = aw // A_GROUPS
    rows = []
    for c in range(tm // CHUNK):
        cols = []
        for g in range(A_GROUPS):
            blk = a_v[c * CHUNK:(c + 1) * CHUNK, g * gd:(g + 1) * gd]
            cols.append(_dot(wsp_ref[g], blk, NN) + bsp_ref[g])
        rows.append(jnp.concatenate(cols, axis=1))
    mixed = jnp.concatenate(rows, axis=0)
    return a_u, vn, rstd, a_v, mixed


def _even_core_fwd(proj, x0, lng, lnb, wsp, bspb, cw, gath, tm):
    t, dm = x0.shape
    aw = lng.shape[1]
    bw = cw.shape[1]
    assert aw == bw and 2 * aw + 3 * bw == proj.shape[1]
    nt = t // tm
    prev, nxt = _halo_specs(tm, t, bw, [3, 4])

    def body(proj_ref, cp_ref, hp_ref, cn_ref, hn_ref, x0_ref, lng_ref, lnb_ref, wsp_ref, bsp_ref, cw_ref, gath_ref,
             x1_ref, y_ref, wbuf, sems):
        i = pl.program_id(0)

        @pl.when(i == 0)
        def _():
            _load_weight(gath_ref, wbuf, sems)
        proj_v = proj_ref[...]
        a_u, _, _, _, mixed = _gating_fwd(proj_v, lng_ref[...], lnb_ref[...], wsp_ref, bsp_ref, aw)
        a_out = a_u * mixed
        bb = proj_v[:, 2 * aw:2 * aw + bw]
        z = proj_v[:, 2 * aw + bw:2 * aw + 2 * bw] * proj_v[:, 2 * aw + 2 * bw:]
        zprev = jnp.where(i > 0, cp_ref[7:8, :] * hp_ref[7:8, :], 0.0)
        znext = jnp.where(i < nt - 1, cn_ref[0:1, :] * hn_ref[0:1, :], 0.0)
        zm1, zp1 = _shift_rows(z, zprev, znext)
        cwv = cw_ref[...]
        conv = zm1 * cwv[0:1, :] + z * cwv[1:2, :] + zp1 * cwv[2:3, :]
        y = jnp.concatenate([a_out, bb * conv], axis=1).astype(BF16)
        y_ref[...] = y
        x1_ref[...] = x0_ref[...] + _dot(y, wbuf[...], NN)

    return dict(
        body=body, grid=(nt,), name="even_core_fwd",
        args=[proj, proj, proj, proj, proj, x0, lng, lnb, wsp, bspb, cw, gath],
        out_shape=[jax.ShapeDtypeStruct((t, dm), F32), jax.ShapeDtypeStruct((t, aw + bw), BF16)],
        in_specs=[_tok(tm, proj.shape[1]), prev[0], prev[1], nxt[0], nxt[1], _tok(tm, dm), _full(lng.shape),
                  _full(lnb.shape), _full(wsp.shape), _full(bspb.shape), _full(cw.shape), ANY],
        out_specs=[_tok(tm, dm), _tok(tm, aw + bw)],
        scratch=[pltpu.VMEM((gath.shape[1] * NDEV, dm), BF16), pltpu.SemaphoreType.DMA((NDEV,))])


def _even_core_bwd(proj, dx1, lng, lnb, wsp, bspb, cw, gath, tm):
    t, dm = dx1.shape
    aw, bw = lng.shape[1], cw.shape[1]
    gd = aw // A_GROUPS
    nt = t // tm
    inw = proj.shape[1]
    prev, nxt = _halo_specs(tm, t, bw, [2, 3, 4])
    nb8 = tm // 8
    last8 = t // 8 - 1

    def body(proj_ref, bp_ref, cp_ref, hp_ref, bn_ref, cn_ref, hn_ref, dx_ref, dxp_ref, dxn_ref,
             lng_ref, lnb_ref, wsp_ref, bsp_ref, cw_ref, gath_ref,
             dproj_ref, dlng_ref, dlnb_ref, dwsp_ref, dbsp_ref, dcw_ref, wbuf, sems):
        i = pl.program_id(0)

        @pl.when(i == 0)
        def _():
            _load_weight(gath_ref, wbuf, sems)
            dlng_ref[...] = jnp.zeros_like(dlng_ref)
            dlnb_ref[...] = jnp.zeros_like(dlnb_ref)
            dwsp_ref[...] = jnp.zeros_like(dwsp_ref)
            dbsp_ref[...] = jnp.zeros_like(dbsp_ref)
            dcw_ref[...] = jnp.zeros_like(dcw_ref)
        proj_v = proj_ref[...]
        lng_v = lng_ref[...]
        a_u, vn, rstd, a_v, mixed = _gating_fwd(proj_v, lng_v, lnb_ref[...], wsp_ref, bsp_ref, aw)
        w = wbuf[...]
        dy = _dot(dx_ref[...].astype(BF16), w, NT)
        da_out, db_out = dy[:, 0:aw], dy[:, aw:]
        da_u = da_out * mixed
        dmixed = da_out * a_u
        dmb = dmixed.astype(BF16)
        rows = []
        for c in range(tm // CHUNK):
            cols = []
            for g in range(A_GROUPS):
                r0, c0 = c * CHUNK, g * gd
                dm_cg = dmb[r0:r0 + CHUNK, c0:c0 + gd]
                cols.append(_dot(wsp_ref[g], dm_cg, TN))
                dwsp_ref[g] += _dot(dm_cg, a_v[r0:r0 + CHUNK, c0:c0 + gd], NT)
                dbsp_ref[g] += dmixed[r0:r0 + CHUNK, c0:c0 + gd]
            rows.append(jnp.concatenate(cols, axis=1))
        dav = jnp.concatenate(rows, axis=0)
        dlng_ref[...] += jnp.sum(dav * vn, axis=0, keepdims=True)
        dlnb_ref[...] += jnp.sum(dav, axis=0, keepdims=True)
        dvn = dav * lng_v
        dgv = rstd * (dvn - jnp.mean(dvn, axis=-1, keepdims=True) - vn * jnp.mean(dvn * vn, axis=-1, keepdims=True))
        dv_pre = dgv * _gelu_grad(proj_v[:, aw:2 * aw])
        du_pre = da_u * _gelu_grad(proj_v[:, 0:aw])
        bb = proj_v[:, 2 * aw:2 * aw + bw]
        bc = proj_v[:, 2 * aw + bw:2 * aw + 2 * bw]
        bh = proj_v[:, 2 * aw + 2 * bw:]
        z = bc * bh
        zprev = jnp.where(i > 0, cp_ref[7:8, :] * hp_ref[7:8, :], 0.0)
        znext = jnp.where(i < nt - 1, cn_ref[0:1, :] * hn_ref[0:1, :], 0.0)
        zm1, zp1 = _shift_rows(z, zprev, znext)
        cwv = cw_ref[...]
        conv = zm1 * cwv[0:1, :] + z * cwv[1:2, :] + zp1 * cwv[2:3, :]
        dbb = db_out * conv
        dconv = db_out * bb
        dx_edge = jnp.concatenate([dxp_ref[...], dxn_ref[...]], axis=0).astype(BF16)
        dy_edge = _dot(dx_edge, w[aw:, :], NT)
        dcprev = jnp.where(i > 0, dy_edge[7:8, :] * bp_ref[7:8, :], 0.0)
        dcnext = jnp.where(i < nt - 1, dy_edge[8:9, :] * bn_ref[0:1, :], 0.0)
        dcm1, dcp1 = _shift_rows(dconv, dcprev, dcnext)
        dz = dcp1 * cwv[0:1, :] + dconv * cwv[1:2, :] + dcm1 * cwv[2:3, :]
        dcw_ref[0:1, :] += jnp.sum(dconv * zm1, axis=0, keepdims=True)
        dcw_ref[1:2, :] += jnp.sum(dconv * z, axis=0, keepdims=True)
        dcw_ref[2:3, :] += jnp.sum(dconv * zp1, axis=0, keepdims=True)
        dproj_ref[...] = jnp.concatenate([du_pre, dv_pre, dbb, dz * bh, dz * bc], axis=1).astype(BF16)

    row8 = lambda f: pl.BlockSpec((8, dm), f)
    return dict(
        body=body, grid=(nt,), name="even_core_bwd",
        args=[proj, proj, proj, proj, proj, proj, proj, dx1, dx1, dx1, lng, lnb, wsp, bspb, cw, gath],
        out_shape=[jax.ShapeDtypeStruct((t, inw), BF16), jax.ShapeDtypeStruct((1, aw), F32),
                   jax.ShapeDtypeStruct((1, aw), F32), jax.ShapeDtypeStruct(wsp.shape, F32),
                   jax.ShapeDtypeStruct((A_GROUPS, CHUNK, gd), F32), jax.ShapeDtypeStruct(cw.shape, F32)],
        in_specs=[_tok(tm, inw), prev[0], prev[1], prev[2], nxt[0], nxt[1], nxt[2], _tok(tm, dm),
                  row8(lambda i: (jnp.maximum(i * nb8 - 1, 0), 0)), row8(lambda i: (jnp.minimum((i + 1) * nb8, last8), 0)),
                  _full(lng.shape), _full(lnb.shape), _full(wsp.shape), _full(bspb.shape), _full(cw.shape), ANY],
        out_specs=[_tok(tm, inw), _full((1, aw)), _full((1, aw)), _full(wsp.shape),
                   _full((A_GROUPS, CHUNK, gd)), _full(cw.shape)],
        scratch=[pltpu.VMEM((gath.shape[1] * NDEV, dm), BF16), pltpu.SemaphoreType.DMA((NDEV,))])


def _ff_chunks(f, width=1024):
    return [(c0, min(c0 + width, f)) for c0 in range(0, f, width)]


def _ffn_up(x, gain, gath_g, gath_u, name, tm):
    t, dm = x.shape
    f = gath_g.shape[1] * NDEV

    def body(x_ref, g_ref, gg_ref, gu_ref, gate_ref, up_ref, act_ref, wg, wu, sems):
        @pl.when(pl.program_id(0) == 0)
        def _():
            _load_weight(gg_ref, wg, sems)
            _load_weight(gu_ref, wu, sems)
        h, _ = _rms_fwd(x_ref[...], g_ref[...])
        hb = h.astype(BF16)
        for c0, c1 in _ff_chunks(f):
            gate = _dot(hb, wg[c0:c1, :], NT)
            up = _dot(hb, wu[c0:c1, :], NT)
            gate_ref[:, c0:c1] = gate.astype(BF16)
            up_ref[:, c0:c1] = up.astype(BF16)
            act_ref[:, c0:c1] = (gate * _sigmoid(gate) * up).astype(BF16)

    o = jax.ShapeDtypeStruct((t, f), BF16)
    return dict(
        body=body, grid=(t // tm,), name=name, args=[x, gain, gath_g, gath_u], out_shape=[o, o, o],
        in_specs=[_tok(tm, dm), _full((1, dm)), ANY, ANY], out_specs=[_tok(tm, f)] * 3,
        scratch=[pltpu.VMEM((f, dm), BF16), pltpu.VMEM((f, dm), BF16), pltpu.SemaphoreType.DMA((NDEV,))])


def _ffn_down(x, act, gath_d, name, tm):
    t, dm = x.shape
    f = act.shape[1]

    def body(x_ref, a_ref, gd_ref, xo_ref, wd, sems):
        @pl.when(pl.program_id(0) == 0)
        def _():
            _load_weight(gd_ref, wd, sems)
        xo_ref[...] = x_ref[...] + _dot(a_ref[...], wd[...], NN)

    return dict(
        body=body, grid=(t // tm,), name=name, args=[x, act, gath_d], out_shape=[jax.ShapeDtypeStruct((t, dm), F32)],
        in_specs=[_tok(tm, dm), _tok(tm, f), ANY], out_specs=[_tok(tm, dm)],
        scratch=[pltpu.VMEM((f, dm), BF16), pltpu.SemaphoreType.DMA((NDEV,))])


def _ffn_down_loss(x, act, gath_d, target, gain, name, tm):
    t, dm = x.shape
    f = act.shape[1]
    steps = t // tm

    def body(x_ref, a_ref, gd_ref, t_ref, g_ref, loss_ref, dx_ref, dxb_ref, dgain_ref, wd, acc, sems):
        i = pl.program_id(0)

        @pl.when(i == 0)
        def _():
            _load_weight(gd_ref, wd, sems)
            acc[...] = jnp.zeros_like(acc)
            dgain_ref[...] = jnp.zeros_like(dgain_ref)
        xv = x_ref[...] + _dot(a_ref[...], wd[...], NN)
        gain_v = g_ref[...]
        y, r = _rms_fwd(xv, gain_v)
        e = y - t_ref[...]
        acc[...] += jnp.sum(e * e, axis=0, keepdims=True)
        dx, dgain = _rms_bwd(e * (1.0 / dm), xv, r, gain_v)
        dx_ref[...] = dx
        dxb_ref[...] = dx.astype(BF16)
        dgain_ref[...] += dgain

        @pl.when(i == steps - 1)
        def _():
            loss_ref[...] = jnp.sum(acc[...], axis=-1, keepdims=True) * (0.5 / dm)

    return dict(
        body=body, grid=(steps,), name=name, args=[x, act, gath_d, target, gain],
        out_shape=[jax.ShapeDtypeStruct((1, 1), F32), jax.ShapeDtypeStruct((t, dm), F32),
                   jax.ShapeDtypeStruct((t, dm), BF16), jax.ShapeDtypeStruct((1, dm), F32)],
        in_specs=[_tok(tm, dm), _tok(tm, f), ANY, _tok(tm, dm), _full((1, dm))],
        out_specs=[_full((1, 1)), _tok(tm, dm), _tok(tm, dm), _full((1, dm))],
        scratch=[pltpu.VMEM((f, dm), BF16), pltpu.VMEM((1, dm), F32), pltpu.SemaphoreType.DMA((NDEV,))])


def _ffn_bwd(dxo, x, gate, up, gain, gath_g, gath_u, gath_d, name, tm):
    t, dm = x.shape
    f = gate.shape[1]

    def body(dxo_ref, x_ref, gate_ref, up_ref, g_ref, gg_ref, gu_ref, gd_ref,
             dx_ref, dxb_ref, dg_ref, du_ref, hb_ref, dgain_ref, wg, wu, wd, sems):
        @pl.when(pl.program_id(0) == 0)
        def _():
            _load_weight(gg_ref, wg, sems)
            _load_weight(gu_ref, wu, sems)
            _load_weight(gd_ref, wd, sems)
            dgain_ref[...] = jnp.zeros_like(dgain_ref)
        xv, gain_v, dxo_v = x_ref[...], g_ref[...], dxo_ref[...]
        h, r = _rms_fwd(xv, gain_v)
        hb_ref[...] = h.astype(BF16)
        dxob = dxo_v.astype(BF16)
        dh = jnp.zeros_like(xv)
        for c0, c1 in _ff_chunks(f):
            gate_v = gate_ref[:, c0:c1].astype(F32)
            up_v = up_ref[:, c0:c1].astype(F32)
            s = _sigmoid(gate_v)
            silu = gate_v * s
            dact = _dot(dxob, wd[c0:c1, :], NT)
            dg = (dact * up_v * (s * (1.0 + gate_v * (1.0 - s)))).astype(BF16)
            du = (dact * silu).astype(BF16)
            dg_ref[:, c0:c1] = dg
            du_ref[:, c0:c1] = du
            dh = dh + _dot(dg, wg[c0:c1, :], NN) + _dot(du, wu[c0:c1, :], NN)
        dx, dgain = _rms_bwd(dh, xv, r, gain_v)
        dx = dxo_v + dx
        dx_ref[...] = dx
        dxb_ref[...] = dx.astype(BF16)
        dgain_ref[...] += dgain

    return dict(
        body=body, grid=(t // tm,), name=name, args=[dxo, x, gate, up, gain, gath_g, gath_u, gath_d],
        out_shape=[jax.ShapeDtypeStruct((t, dm), F32), jax.ShapeDtypeStruct((t, dm), BF16),
                   jax.ShapeDtypeStruct((t, f), BF16), jax.ShapeDtypeStruct((t, f), BF16),
                   jax.ShapeDtypeStruct((t, dm), BF16), jax.ShapeDtypeStruct((1, dm), F32)],
        in_specs=[_tok(tm, dm), _tok(tm, dm), _tok(tm, f), _tok(tm, f), _full((1, dm)), ANY, ANY, ANY],
        out_specs=[_tok(tm, dm), _tok(tm, dm), _tok(tm, f), _tok(tm, f), _tok(tm, dm), _full((1, dm))],
        scratch=[pltpu.VMEM((f, dm), BF16), pltpu.VMEM((f, dm), BF16), pltpu.VMEM((f, dm), BF16),
                 pltpu.SemaphoreType.DMA((NDEV,))])


def _t5_buckets(rel):
    nb = N_BUCKETS // 2
    ret = jnp.where(rel > 0, nb, 0)
    n = jnp.abs(rel)
    max_exact = nb // 2
    nf = jnp.maximum(n, 1).astype(jnp.float32)
    large = max_exact + (jnp.log(nf / max_exact) / math.log(MAX_DISTANCE / max_exact)
                         * (nb - max_exact)).astype(jnp.int32)
    large = jnp.minimum(large, nb - 1)
    return ret + jnp.where(n < max_exact, n, large)


def _bucket_table():
    qi = jnp.arange(CHUNK, dtype=jnp.int32)[:, None]
    kj = jnp.arange(3 * CHUNK, dtype=jnp.int32)[None, :]
    rel = kj - CHUNK - qi
    return jnp.where(jnp.abs(rel) <= CHUNK, _t5_buckets(rel), -1)


def _bias_table(rel_bias_t, buckets):
    nh = rel_bias_t.shape[0]

    def body(rb_ref, bk_ref, o_ref):
        bk = bk_ref[...]
        for h in range(nh):
            acc = jnp.where(bk < 0, NEG, 0.0).astype(F32)
            for b in range(N_BUCKETS):
                acc = jnp.where(bk == b, rb_ref[h, b] * LOG2E, acc)
            o_ref[h] = acc

    return dict(
        body=body, grid=(1,), name="bias_table", args=[rel_bias_t, buckets],
        out_shape=[jax.ShapeDtypeStruct((nh,) + buckets.shape, F32)],
        in_specs=[pl.BlockSpec(memory_space=pltpu.SMEM), _full(buckets.shape)],
        out_specs=[_full((nh,) + buckets.shape)])


def _rel_bias_grad(dbias, buckets):
    nh = dbias.shape[0]

    def body(db_ref, bk_ref, o_ref):
        bk = bk_ref[...]
        lane = lax.broadcasted_iota(jnp.int32, (1, 128), 1)
        for h in range(nh):
            d = db_ref[h]
            row = jnp.zeros((1, 128), F32)
            for b in range(N_BUCKETS):
                s = jnp.sum(jnp.sum(jnp.where(bk == b, d, 0.0), axis=1, keepdims=True), axis=0, keepdims=True)
                row = jnp.where(lane == b, s, row)
            o_ref[h:h + 1, :] = row

    return pl.pallas_call(
        body, out_shape=jax.ShapeDtypeStruct((nh, 128), F32),
        in_specs=[pl.BlockSpec(memory_space=pltpu.VMEM), pl.BlockSpec(memory_space=pltpu.VMEM)],
        out_specs=pl.BlockSpec(memory_space=pltpu.VMEM), compiler_params=_cp(0), name="rel_bias_grad")(dbias, buckets)


def _half_masks():
    lane = lax.broadcasted_iota(jnp.int32, (CHUNK, 128), 1)
    return lane < HEAD_DIM, lane >= HEAD_DIM


def _kv_low(ref, starts, hk, lo):
    kt = (hk // 2) * 128
    out = []
    for jj in range(3):
        blk = ref[pl.ds(starts[jj], CHUNK), kt:kt + 128]
        if hk % 2 == 1:
            blk = pltpu.roll(blk, HEAD_DIM, 1)
        out.append(jnp.where(lo, blk, jnp.zeros_like(blk)))
    return out


def _stack_heads(tile_a, tile_b):
    return jnp.concatenate([tile_a, pltpu.roll(tile_a, HEAD_DIM, 1), tile_b, pltpu.roll(tile_b, HEAD_DIM, 1)], axis=0)


def _unstack_heads(o4):
    return (o4[0:CHUNK] + pltpu.roll(o4[CHUNK:2 * CHUNK], HEAD_DIM, 1),
            o4[2 * CHUNK:3 * CHUNK] + pltpu.roll(o4[3 * CHUNK:], HEAD_DIM, 1))


ATT_SLAB = 32


def _softmax_slab(s_scr, hk, g, r0, bias_ref, sink_ref, n, nblk):
    scale = HEAD_DIM ** -0.5 * LOG2E
    h = (N_HEADS // N_KV) * hk + g
    s = []
    for jj in range(3):
        sj = (s_scr[hk, jj, pl.ds(g * CHUNK + r0, ATT_SLAB), :] * scale
              + bias_ref[h, pl.ds(r0, ATT_SLAB), jj * CHUNK:(jj + 1) * CHUNK])
        if jj == 0:
            sj = jnp.where(n > 0, sj, NEG)
        if jj == 2:
            sj = jnp.where(n < nblk - 1, sj, NEG)
        s.append(sj)
    sink = sink_ref[h] * LOG2E
    m = jnp.maximum(jnp.max(jnp.maximum(jnp.maximum(s[0], s[1]), s[2]), axis=-1, keepdims=True), sink)
    e = [jnp.exp2(sj - m) for sj in s]
    es = jnp.exp2(sink - m)
    inv = 1.0 / (jnp.sum(e[0] + e[1] + e[2], axis=-1, keepdims=True) + es)
    return [ej * inv for ej in e], es * inv


def _key_block_starts(n, nblk):
    return [pl.multiple_of(jnp.clip(n - 1 + jj, 0, nblk - 1) * CHUNK, CHUNK) for jj in range(3)]


def _attn_fwd(qkv, x2, bias, sink, gath):
    t, dm = x2.shape
    nblk = t // CHUNK
    kvw = N_KV * HEAD_DIM
    kcb, vcb = dm // kvw, dm // kvw + 1
    slab = (N_KV, 3, 4 * CHUNK, CHUNK)

    def body(q_ref, k_ref, v_ref, x2_ref, bias_ref, sink_ref, gath_ref, x3_ref, att_ref, p_ref, ps_ref,
             wbuf, s_scr, sems):
        n = pl.program_id(0)

        @pl.when(n == 0)
        def _():
            _load_weight(gath_ref, wbuf, sems)
        lo, _ = _half_masks()
        lane_s = lax.broadcasted_iota(jnp.int32, (ATT_SLAB, 128), 1)
        starts = _key_block_starts(n, nblk)
        tiles = []
        for hk in range(N_KV):
            c0 = (2 * hk) * 128
            k_lo = _kv_low(k_ref, starts, hk, lo)
            v_lo = _kv_low(v_ref, starts, hk, lo)
            q4 = _stack_heads(q_ref[:, c0:c0 + 128], q_ref[:, c0 + 128:c0 + 256])
            for jj in range(3):
                s_scr[hk, jj] = _dot(q4, k_lo[jj], NT)
            for g in range(4):
                h = 4 * hk + g
                for r0 in range(0, CHUNK, ATT_SLAB):
                    p, ps = _softmax_slab(s_scr, hk, g, r0, bias_ref, sink_ref, n, nblk)
                    for jj in range(3):
                        p_ref[hk, jj, g * CHUNK + r0:g * CHUNK + r0 + ATT_SLAB, :] = p[jj].astype(BF16)
                    rest = jnp.zeros((ATT_SLAB, 128), F32) if h == 0 else ps_ref[r0:r0 + ATT_SLAB, :]
                    ps_ref[r0:r0 + ATT_SLAB, :] = jnp.where(lane_s == h, ps, rest)
            o4 = _dot(p_ref[hk, 0], v_lo[0], NN) + _dot(p_ref[hk, 1], v_lo[1], NN) + _dot(p_ref[hk, 2], v_lo[2], NN)
            tiles += list(_unstack_heads(o4))
        att = jnp.concatenate(tiles, axis=1).astype(BF16)
        att_ref[...] = att
        x3_ref[...] = x2_ref[...] + _dot(att, wbuf[...], NN)

    blk = pl.BlockSpec((CHUNK, dm), lambda n: (n, 0))
    return dict(
        body=body, grid=(nblk,), name="attn_fwd", args=[qkv, qkv, qkv, x2, bias, sink, gath],
        out_shape=[jax.ShapeDtypeStruct((t, dm), F32), jax.ShapeDtypeStruct((t, dm), BF16),
                   jax.ShapeDtypeStruct((nblk,) + slab, BF16), jax.ShapeDtypeStruct((t, 128), F32)],
        in_specs=[blk, pl.BlockSpec((t, kvw), lambda n: (0, kcb)), pl.BlockSpec((t, kvw), lambda n: (0, vcb)), blk,
                  _full(bias.shape), pl.BlockSpec(memory_space=pltpu.SMEM), ANY],
        out_specs=[blk, blk, pl.BlockSpec((None,) + slab, lambda n: (n, 0, 0, 0, 0)),
                   pl.BlockSpec((CHUNK, 128), lambda n: (n, 0))],
        scratch=[pltpu.VMEM((gath.shape[1] * NDEV, dm), BF16), pltpu.VMEM(slab, F32),
                 pltpu.SemaphoreType.DMA((NDEV,))])


def _attn_bwd(qkv, att, probs, sink_probs, dx3, bias_shape, gath):
    t, dm = dx3.shape
    nblk = t // CHUNK
    kvw = N_KV * HEAD_DIM
    kcb, vcb = dm // kvw, dm // kvw + 1
    scale = HEAD_DIM ** -0.5
    slab = (N_KV, 3, 4 * CHUNK, CHUNK)

    def body(q_ref, k_ref, v_ref, att_ref, p_ref, ps_ref, dx_ref, gath_ref,
             dq_ref, dkb_ref, dvb_ref, dbias_ref, dsink_ref,
             wbuf, dp_scr, ds_scr, prod_scr, dsum_scr, dk_ref, dv_ref, sems):
        n = pl.program_id(0)

        @pl.when(n == 0)
        def _():
            _load_weight(gath_ref, wbuf, sems)
            dk_ref[...] = jnp.zeros_like(dk_ref)
            dv_ref[...] = jnp.zeros_like(dv_ref)
            dbias_ref[...] = jnp.zeros_like(dbias_ref)
            dsink_ref[...] = jnp.zeros_like(dsink_ref)
        lo, hi = _half_masks()
        lane_s = lax.broadcasted_iota(jnp.int32, (ATT_SLAB, 128), 1)
        starts = _key_block_starts(n, nblk)
        dout = _dot(dx_ref[...].astype(BF16), wbuf[...], NT)
        prod_scr[...] = dout * att_ref[...].astype(F32)
        doutb = dout.astype(BF16)
        dq_tiles = []
        for hk in range(N_KV):
            kt = (hk // 2) * 128
            c0 = (2 * hk) * 128
            k_lo = _kv_low(k_ref, starts, hk, lo)
            v_lo = _kv_low(v_ref, starts, hk, lo)
            q4 = _stack_heads(q_ref[:, c0:c0 + 128], q_ref[:, c0 + 128:c0 + 256])
            do4 = _stack_heads(doutb[:, c0:c0 + 128], doutb[:, c0 + 128:c0 + 256])
            for jj in range(3):
                dp_scr[hk, jj] = _dot(do4, v_lo[jj], NT)
            for g in range(4):
                h = 4 * hk + g
                for r0 in range(0, CHUNK, ATT_SLAB):
                    rows = slice(g * CHUNK + r0, g * CHUNK + r0 + ATT_SLAB)
                    pt = prod_scr[r0:r0 + ATT_SLAB, c0 + (g // 2) * 128:c0 + (g // 2 + 1) * 128]
                    msk = lane_s < HEAD_DIM if g % 2 == 0 else lane_s >= HEAD_DIM
                    dsum = jnp.sum(jnp.where(msk, pt, 0.0), axis=-1, keepdims=True)
                    rest = jnp.zeros((ATT_SLAB, 128), F32) if h == 0 else dsum_scr[r0:r0 + ATT_SLAB, :]
                    dsum_scr[r0:r0 + ATT_SLAB, :] = jnp.where(lane_s == h, dsum, rest)
                    for jj in range(3):
                        ds = p_ref[hk, jj, rows, :].astype(F32) * (dp_scr[hk, jj, rows, :] - dsum)
                        dbias_ref[h, r0:r0 + ATT_SLAB, jj * CHUNK:(jj + 1) * CHUNK] += ds
                        ds_scr[hk, jj, rows, :] = ds.astype(BF16)
            dq4 = jnp.zeros((4 * CHUNK, 128), F32)
            for jj in range(3):
                ds4 = ds_scr[hk, jj]
                dq4 = dq4 + _dot(ds4, k_lo[jj], NN) * scale
                dkj = _dot(ds4, q4, TN) * scale
                dvj = _dot(p_ref[hk, jj], do4, TN)
                if hk % 2 == 1:
                    dkj, dvj = pltpu.roll(dkj, HEAD_DIM, 1), pltpu.roll(dvj, HEAD_DIM, 1)
                keep = lo if hk % 2 == 0 else hi
                dk_ref[pl.ds(starts[jj], CHUNK), kt:kt + 128] += jnp.where(keep, dkj, 0.0)
                dv_ref[pl.ds(starts[jj], CHUNK), kt:kt + 128] += jnp.where(keep, dvj, 0.0)
            dq_tiles += list(_unstack_heads(dq4))
        dq_ref[...] = jnp.concatenate(dq_tiles, axis=1).astype(BF16)
        dsink_ref[...] -= jnp.sum(ps_ref[...] * dsum_scr[...], axis=0, keepdims=True)

        @pl.when(n == nblk - 1)
        def _():
            dkb_ref[...] = dk_ref[...].astype(BF16)
            dvb_ref[...] = dv_ref[...].astype(BF16)

    blk = pl.BlockSpec((CHUNK, dm), lambda n: (n, 0))
    return dict(
        body=body, grid=(nblk,), name="attn_bwd", args=[qkv, qkv, qkv, att, probs, sink_probs, dx3, gath],
        out_shape=[jax.ShapeDtypeStruct((t, dm), BF16), jax.ShapeDtypeStruct((t, kvw), BF16),
                   jax.ShapeDtypeStruct((t, kvw), BF16), jax.ShapeDtypeStruct(bias_shape, F32),
                   jax.ShapeDtypeStruct((1, 128), F32)],
        in_specs=[blk, pl.BlockSpec((t, kvw), lambda n: (0, kcb)), pl.BlockSpec((t, kvw), lambda n: (0, vcb)),
                  blk, pl.BlockSpec((None,) + slab, lambda n: (n, 0, 0, 0, 0)),
                  pl.BlockSpec((CHUNK, 128), lambda n: (n, 0)), blk, ANY],
        out_specs=[blk, _full((t, kvw)), _full((t, kvw)), _full(bias_shape), _full((1, 128))],
        scratch=[pltpu.VMEM((gath.shape[1] * NDEV, dm), BF16), pltpu.VMEM(slab, F32), pltpu.VMEM(slab, BF16),
                 pltpu.VMEM((CHUNK, dm), F32), pltpu.VMEM((CHUNK, 128), F32),
                 pltpu.VMEM((t, kvw), F32), pltpu.VMEM((t, kvw), F32), pltpu.SemaphoreType.DMA((NDEV,))])


def _finish_weight(recvs, w, m, v, name):
    nl, r, dm = w.shape
    assert nl == len(recvs) and all(rc.shape[1:] == (r, dm) for rc in recvs)
    td = dm // 2
    wspec = pl.BlockSpec((None, r, td), lambda l, j: (l, 0, j))

    def body(*refs):
        r_refs = refs[:nl]
        w_ref, m_ref, v_ref, g_ref, d_ref, nm_ref, nv_ref = refs[nl:]
        layer = pl.program_id(0)
        for li in range(nl):
            @pl.when(layer == li)
            def _():
                g = r_refs[li][0].astype(F32)
                for d in range(1, recvs[li].shape[0]):
                    g = g + r_refs[li][d].astype(F32)
                delta, nm, nv = _adamw_math(w_ref[...], g, m_ref[...], v_ref[...])
                g_ref[...] = g
                d_ref[...] = delta
                nm_ref[...] = nm
                nv_ref[...] = nv

    o = jax.ShapeDtypeStruct(w.shape, F32)
    return dict(
        body=body, grid=(nl, dm // td), name=name, args=[*recvs, w, m, v], out_shape=[o, o, o, o],
        in_specs=[pl.BlockSpec((rc.shape[0], r, td), lambda l, j: (0, 0, j)) for rc in recvs] + [wspec] * 3,
        out_specs=[wspec] * 4)


def _adamw_small(ws, ms, vs, slots, late_slots, loss_slots, name):
    n = len(ws)

    def total(ref):
        acc = ref[0].astype(F32)
        for d in range(1, NDEV):
            acc = acc + ref[d].astype(F32)
        return acc

    def body(*refs):
        ins, outs = refs[:4 * n + 2], refs[4 * n + 2:]
        for i in range(n):
            w_ref, m_ref, v_ref, s_ref = ins[4 * i:4 * i + 4]
            g_ref, d_ref, nm_ref, nv_ref = outs[4 * i:4 * i + 4]
            g_ref[...] = total(s_ref)
            if i == 0:
                g_ref[0:1, :] = total(ins[4 * n])
            d_ref[...], nm_ref[...], nv_ref[...] = _adamw_math(w_ref[...], g_ref[...], m_ref[...], v_ref[...])
        outs[4 * n][...] = total(ins[4 * n + 1])

    args, out_shape = [], []
    for w, m, v, s in zip(ws, ms, vs, slots):
        args += [w, m, v, s]
        out_shape += [jax.ShapeDtypeStruct(w.shape, F32)] * 4
    args += [late_slots, loss_slots]
    out_shape.append(jax.ShapeDtypeStruct((1, 1), F32))
    out = pl.pallas_call(
        body, grid=(1,), out_shape=tuple(out_shape), in_specs=[_full(a.shape) for a in args],
        out_specs=tuple(_full(o.shape) for o in out_shape), compiler_params=_cp(), name=name)(*args)
    return [tuple(out[4 * i:4 * i + 4]) for i in range(n)], out[4 * n]


def kernel(x, norm_mix, norm_ffn, even_w_in, even_v_ln_g, even_v_ln_b, even_w_spatial, even_b_spatial, even_conv_w, even_w_out, attn_w_qkv, attn_sink, rel_bias, attn_w_out, ffn_w_gate, ffn_w_up, ffn_w_down, final_norm, loss_target, m_norm_mix, m_norm_ffn, m_even_w_in, m_even_v_ln_g, m_even_v_ln_b, m_even_w_spatial, m_even_b_spatial, m_even_conv_w, m_even_w_out, m_attn_w_qkv, m_attn_sink, m_rel_bias, m_attn_w_out, m_ffn_w_gate, m_ffn_w_up, m_ffn_w_down, m_final_norm, v_norm_mix, v_norm_ffn, v_even_w_in, v_even_v_ln_g, v_even_v_ln_b, v_even_w_spatial, v_even_b_spatial, v_even_conv_w, v_even_w_out, v_attn_w_qkv, v_attn_sink, v_rel_bias, v_attn_w_out, v_ffn_w_gate, v_ffn_w_up, v_ffn_w_down, v_final_norm):
    t, dm = x.shape[1], x.shape[2]
    aw = even_v_ln_g.shape[1]
    bw = even_conv_w.shape[2] * NDEV
    gd = aw // A_GROUPS
    tm = min(512, t // 2)
    tmf = min(256, t // 2)
    me = _my_index()
    row = lambda a: a.reshape(1, -1)

    colT = lambda w: w.T.astype(BF16)
    sh = dict(winT=colT(even_w_in[0]), wqkvT=colT(attn_w_qkv[0]), wgT0=colT(ffn_w_gate[0]), wuT0=colT(ffn_w_up[0]),
              wgT1=colT(ffn_w_gate[1]), wuT1=colT(ffn_w_up[1]), woe=even_w_out[0].astype(BF16),
              woa=attn_w_out[0].astype(BF16), wd0=ffn_w_down[0].astype(BF16), wd1=ffn_w_down[1].astype(BF16))
    gather = lambda names: _GatherCarry([sh[n] for n in names])

    in_full = lambda a: lax.dynamic_update_slice(jnp.zeros((3, bw), F32), a[0], (0, me * (bw // NDEV)))

    x0 = x[0]
    wsp_b = even_w_spatial[0].astype(BF16)
    bspb = jnp.broadcast_to(even_b_spatial[0][:, :, None], (A_GROUPS, CHUNK, gd))
    buckets = _bucket_table()
    sink = attn_sink[0]

    (bias,), ((g_winT,), (cw_slots,)) = _call(
        _bias_table(rel_bias.T, buckets), [gather(["winT"]), _BroadcastCarry([in_full(even_conv_w)])])
    cw_full = jnp.sum(cw_slots, axis=0)
    (proj, h0b), (g_woe, g_wgT0) = _call(_norm_proj(x0, row(norm_mix[0]), g_winT, F32, "in_proj", tm),
                                         gather(["woe", "wgT0"]))
    (x1, yb), (g_wuT0,) = _call(_even_core_fwd(proj, x0, even_v_ln_g, even_v_ln_b, wsp_b, bspb, cw_full, g_woe, tm),
                                gather(["wuT0"]))
    (gate0, up0, act0), (g_wd0,) = _call(_ffn_up(x1, row(norm_ffn[0]), g_wgT0, g_wuT0, "ffn_up0", tmf), gather(["wd0"]))
    (x2,), (g_wqkvT,) = _call(_ffn_down(x1, act0, g_wd0, "ffn_down0", tm), gather(["wqkvT"]))
    (qkv, h2b), (g_woa,) = _call(_norm_proj(x2, row(norm_mix[1]), g_wqkvT, BF16, "qkv_proj", tm), gather(["woa"]))
    (x3, attb, probs, sink_probs), (g_wgT1, g_wuT1) = _call(
        _attn_fwd(qkv, x2, bias, sink, g_woa), gather(["wgT1", "wuT1"]))
    (gate1, up1, act1), (g_wd1,) = _call(_ffn_up(x3, row(norm_ffn[1]), g_wgT1, g_wuT1, "ffn_up1", tmf), gather(["wd1"]))
    (loss_part, dx4, dx4b, d_final), _ = _call(
        _ffn_down_loss(x3, act1, g_wd1, loss_target[0], row(final_norm), "ffn_down1_loss", tm))

    (dx3, dx3b, dg1, du1, h3b, d_nffn1), _ = _call(
        _ffn_bwd(dx4, x3, gate1, up1, row(norm_ffn[1]), g_wgT1, g_wuT1, g_wd1, "ffn_bwd1", tmf))
    (p_wgT1,), _ = _call(_wgrad(dg1, h3b, "wgrad_gate1"))
    (p_wuT1,), _ = _call(_wgrad(du1, h3b, "wgrad_up1"))
    (p_wd1,), ((a_wgT1,), (a_wuT1,)) = _call(
        _wgrad(act1, dx4b, "wgrad_down1"), [_PairCarry(p_wgT1), _PairCarry(p_wuT1)])
    (dq, dk, dv, dbias, dsink), ((r_wgT1,), (a_wd1,)) = _call(
        _attn_bwd(qkv, attb, probs, sink_probs, dx3, bias.shape, g_woa),
        [_ChipSumCarry(p_wgT1, a_wgT1), _PairCarry(p_wd1)])
    (p_woa,), _ = _call(_wgrad(attb, dx3b, "wgrad_attn_out"))
    d_relb = _rel_bias_grad(dbias, buckets)[:, 0:N_BUCKETS].T
    (dx2, dx2b, d_nmix1), (r_wuT1,) = _call(
        _proj_bwd_norm([dq, dk, dv], x2, row(norm_mix[1]), dx3, g_wqkvT, "qkv_bwd", tm),
        _ChipSumCarry(p_wuT1, a_wuT1))
    (p_wqkvT,), _ = _call(_wgrad([dq, dk, dv], h2b, "wgrad_qkv"))
    (dx1, dx1b, dg0, du0, h1b, d_nffn0), ((r_wd1,), (r_woa, r_wqkvT)) = _call(
        _ffn_bwd(dx2, x1, gate0, up0, row(norm_ffn[0]), g_wgT0, g_wuT0, g_wd0, "ffn_bwd0", tmf),
        [_ChipSumCarry(p_wd1, a_wd1), _GradCarry([p_woa, p_wqkvT])])
    (p_wgT0,), _ = _call(_wgrad(dg0, h1b, "wgrad_gate0"))
    (p_wuT0,), (a_wgT0,) = _call(_wgrad(du0, h1b, "wgrad_up0"), _PairCarry(p_wgT0))
    (p_wd0,), ((r_wgT0,), (a_wuT0,)) = _call(
        _wgrad(act0, dx2b, "wgrad_down0"), [_ChipSumCarry(p_wgT0, a_wgT0), _PairCarry(p_wuT0)])
    (dproj, d_lng, d_lnb, d_wsp, d_bsp3, d_cw), ((r_wuT0,), (a_wd0,)) = _call(
        _even_core_bwd(proj, dx1, even_v_ln_g, even_v_ln_b, wsp_b, bspb, cw_full, g_woe, tm),
        [_ChipSumCarry(p_wuT0, a_wuT0), _PairCarry(p_wd0)])
    small_names = ["norm_mix", "norm_ffn", "even_v_ln_g", "even_v_ln_b", "even_w_spatial", "even_b_spatial",
                   "even_conv_w", "attn_sink", "rel_bias", "final_norm"]
    small_parts = [jnp.concatenate([jnp.zeros_like(d_nmix1), d_nmix1]), jnp.concatenate([d_nffn0, d_nffn1]),
                   d_lng, d_lnb, d_wsp[None].astype(BF16), jnp.sum(d_bsp3, axis=-1)[None], d_cw,
                   dsink[:, 0:N_HEADS], d_relb, d_final, loss_part]
    (p_winT,), (r_wd0,) = _call(_wgrad(dproj, h0b, "wgrad_in"), _ChipSumCarry(p_wd0, a_wd0))
    (p_woe,), ((a_winT,), small_slots) = _call(
        _wgrad(yb, dx1b, "wgrad_even_out"), [_PairCarry(p_winT), _BroadcastCarry(small_parts)])
    (dx0, _, d_nmix0), ((r_winT,), (r_woe,)) = _call(
        _proj_bwd_norm([dproj], x0, row(norm_mix[0]), dx1, g_winT, "in_proj_bwd", tm),
        [_ChipSumCarry(p_winT, a_winT), _GradCarry([p_woe])])

    grads = {}

    order = ["norm_mix", "norm_ffn", "even_w_in", "even_v_ln_g", "even_v_ln_b", "even_w_spatial", "even_b_spatial",
             "even_conv_w", "even_w_out", "attn_w_qkv", "attn_sink", "rel_bias", "attn_w_out", "ffn_w_gate",
             "ffn_w_up", "ffn_w_down", "final_norm"]
    ws = dict(norm_mix=norm_mix, norm_ffn=norm_ffn, even_w_in=even_w_in, even_v_ln_g=even_v_ln_g,
              even_v_ln_b=even_v_ln_b, even_w_spatial=even_w_spatial, even_b_spatial=even_b_spatial,
              even_conv_w=even_conv_w, even_w_out=even_w_out, attn_w_qkv=attn_w_qkv, attn_sink=attn_sink,
              rel_bias=rel_bias, attn_w_out=attn_w_out, ffn_w_gate=ffn_w_gate, ffn_w_up=ffn_w_up,
              ffn_w_down=ffn_w_down, final_norm=final_norm)
    ms = dict(norm_mix=m_norm_mix, norm_ffn=m_norm_ffn, even_w_in=m_even_w_in, even_v_ln_g=m_even_v_ln_g,
              even_v_ln_b=m_even_v_ln_b, even_w_spatial=m_even_w_spatial, even_b_spatial=m_even_b_spatial,
              even_conv_w=m_even_conv_w, even_w_out=m_even_w_out, attn_w_qkv=m_attn_w_qkv, attn_sink=m_attn_sink,
              rel_bias=m_rel_bias, attn_w_out=m_attn_w_out, ffn_w_gate=m_ffn_w_gate, ffn_w_up=m_ffn_w_up,
              ffn_w_down=m_ffn_w_down, final_norm=m_final_norm)
    vs = dict(norm_mix=v_norm_mix, norm_ffn=v_norm_ffn, even_w_in=v_even_w_in, even_v_ln_g=v_even_v_ln_g,
              even_v_ln_b=v_even_v_ln_b, even_w_spatial=v_even_w_spatial, even_b_spatial=v_even_b_spatial,
              even_conv_w=v_even_conv_w, even_w_out=v_even_w_out, attn_w_qkv=v_attn_w_qkv, attn_sink=v_attn_sink,
              rel_bias=v_rel_bias, attn_w_out=v_attn_w_out, ffn_w_gate=v_ffn_w_gate, ffn_w_up=v_ffn_w_up,
              ffn_w_down=v_ffn_w_down, final_norm=v_final_norm)
    big = dict(ffn_w_gate=([r_wgT0, r_wgT1], True), even_w_in=([r_winT], True), even_w_out=([r_woe], False),
               attn_w_qkv=([r_wqkvT], True), attn_w_out=([r_woa], False), ffn_w_up=([r_wuT0, r_wuT1], True),
               ffn_w_down=([r_wd0, r_wd1], False))
    delta, new_m, new_v = {}, {}, {}
    late_slots = None
    for n, (recvs, transposed) in big.items():
        lay = (lambda a: jnp.swapaxes(a, 1, 2)) if transposed else (lambda a: a)
        spec = _finish_weight(recvs, lay(ws[n]), lay(ms[n]), lay(vs[n]), "finish_" + n)
        if late_slots is None:
            outs, (late_slots,) = _call(spec, _BroadcastCarry([d_nmix0]))
        else:
            outs, _ = _call(spec)
        grads[n], delta[n], new_m[n], new_v[n] = [lay(o) for o in outs]
    shaped = lambda n, a: in_full(a) if n == "even_conv_w" else (a.reshape(1, dm) if n == "final_norm" else a)
    pick = lambda dct: [shaped(n, dct[n]) for n in small_names]
    results, loss11 = _adamw_small(pick(ws), pick(ms), pick(vs), small_slots[:-1], late_slots, small_slots[-1],
                                   "adamw_small")
    mine = lambda a: lax.dynamic_slice(a, (0, me * (bw // NDEV)), (3, bw // NDEV))[None]
    for n, res in zip(small_names, results):
        for dst, a in zip((grads, delta, new_m, new_v), res):
            dst[n] = mine(a) if n == "even_conv_w" else (a.reshape(dm) if n == "final_norm" else a)
    loss = loss11[0, 0]
    return (loss, dx0[None], *[grads[n] for n in order], *[delta[n] for n in order],
            *[new_m[n] for n in order], *[new_v[n] for n in order])
```

```python
import math

import jax
import jax.numpy as jnp
import numpy as np
from jax import lax
from jax.experimental import pallas as pl
from jax.experimental.pallas import tpu as pltpu

F32, BF16 = jnp.float32, jnp.bfloat16
NDEV = 8
EPS = 1e-6
CHUNK = 128
A_GROUPS = 4
N_HEADS, N_KV, HEAD_DIM = 16, 4, 64
N_BUCKETS, MAX_DISTANCE = 32, 128
NEG = -1e30
LOG2E = 1.4426950408889634
ADAM_LR, ADAM_B1, ADAM_B2, ADAM_EPS, ADAM_WD, ADAM_STEP = 0.001, 0.9, 0.999, 1e-08, 0.01, 10
VMEM_LIMIT = 56 * 1024 * 1024
MESH = pl.DeviceIdType.MESH
NT = (((1,), (1,)), ((), ()))
NN = (((1,), (0,)), ((), ()))
TN = (((0,), (0,)), ((), ()))
ANY = pl.BlockSpec(memory_space=pl.ANY)


def _cp(n_grid=1):
    return pltpu.CompilerParams(dimension_semantics=("arbitrary",) * n_grid, vmem_limit_bytes=VMEM_LIMIT)


def _dot(a, b, dims):
    return lax.dot_general(a, b, dims, preferred_element_type=F32)


def _my_index():
    return 4 * lax.axis_index("x") + 2 * lax.axis_index("y") + lax.axis_index("c")


def _peer(k):
    x, y, c = lax.axis_index("x"), lax.axis_index("y"), lax.axis_index("c")
    px = 1 - x if k & 4 else x
    py = 1 - y if k & 2 else y
    pc = 1 - c if k & 1 else c
    return (px, py, pc)


def _load_weight(gath_ref, wbuf, sems):
    rows = gath_ref.shape[1]
    cps = [pltpu.make_async_copy(gath_ref.at[d], wbuf.at[pl.ds(d * rows, rows), :], sems.at[d]) for d in range(NDEV)]
    for c in cps:
        c.start()
    for c in cps:
        c.wait()


class _GatherCarry:
    def __init__(self, pieces):
        self.inputs = list(pieces)
        self.n = len(pieces)
        self.out_shape = [jax.ShapeDtypeStruct((NDEV,) + p.shape, p.dtype) for p in pieces]
        self.scratch = [pltpu.SemaphoreType.DMA((7 * self.n,)), pltpu.SemaphoreType.DMA((7 * self.n,)),
                        pltpu.SemaphoreType.DMA((self.n,))]

    def _ctx(self):
        x, y, c = lax.axis_index("x"), lax.axis_index("y"), lax.axis_index("c")
        chips = [(1 - x, y), (x, 1 - y), (1 - x, 1 - y)]
        return (x, y, c), (x, y, 1 - c), chips, c

    def _copy(self, k, j, block, to, ins, outs, sems, src=None):
        send_sems, recv_sems, _ = sems
        slot = outs[j].at[4 * block[0] + 2 * block[1] + block[2]]
        return pltpu.make_async_remote_copy(
            src_ref=slot if src is None else src, dst_ref=slot, send_sem=send_sems.at[k * self.n + j],
            recv_sem=recv_sems.at[k * self.n + j], device_id=to, device_id_type=MESH)

    def start(self, ins, outs, sems):
        me, sibling, chips, c = self._ctx()
        for j in range(self.n):
            pltpu.make_async_copy(ins[j], outs[j].at[4 * me[0] + 2 * me[1] + me[2]], sems[2].at[j]).start()
            self._copy(0, j, me, sibling, ins, outs, sems, src=ins[j]).start()
            for q, chip in enumerate(chips):
                self._copy(1 + q, j, me, (*chip, c), ins, outs, sems, src=ins[j]).start()

    def mid(self, ins, outs, sems):
        me, sibling, chips, c = self._ctx()
        for q, chip in enumerate(chips):
            for j in range(self.n):
                self._copy(1 + q, j, (*chip, c), me, ins, outs, sems).wait_recv()
                self._copy(4 + q, j, (*chip, c), sibling, ins, outs, sems).start()

    def finish(self, ins, outs, sems):
        me, sibling, chips, c = self._ctx()
        for j in range(self.n):
            self._copy(0, j, sibling, me, ins, outs, sems).wait_recv()
            for q, chip in enumerate(chips):
                self._copy(4 + q, j, (*chip, 1 - c), me, ins, outs, sems).wait_recv()
        for j in range(self.n):
            self._copy(0, j, me, sibling, ins, outs, sems, src=ins[j]).wait_send()
            for q, chip in enumerate(chips):
                self._copy(1 + q, j, me, (*chip, c), ins, outs, sems, src=ins[j]).wait_send()
                self._copy(4 + q, j, (*chip, c), sibling, ins, outs, sems).wait_send()
            pltpu.make_async_copy(ins[j], outs[j].at[0], sems[2].at[j]).wait()


class _GradCarry:
    def __init__(self, pieces):
        self.inputs = list(pieces)
        self.n = len(pieces)
        self.rows = [p.shape[0] // NDEV for p in pieces]
        self.out_shape = [jax.ShapeDtypeStruct((NDEV, r, p.shape[1]), p.dtype) for p, r in zip(pieces, self.rows)]
        self.scratch = [pltpu.SemaphoreType.DMA((7 * self.n,)), pltpu.SemaphoreType.DMA((7 * self.n,)),
                        pltpu.SemaphoreType.DMA((self.n,))]

    def _copies(self, ins, outs, sems):
        me = _my_index()
        local, remote = [], []
        for j in range(self.n):
            r = self.rows[j]
            local.append(pltpu.make_async_copy(ins[j].at[pl.ds(pl.multiple_of(me * r, 16), r), :], outs[j].at[me],
                                               sems[2].at[j]))
            for k in range(1, NDEV):
                peer = _peer(k)
                pidx = 4 * peer[0] + 2 * peer[1] + peer[2]
                remote.append(pltpu.make_async_remote_copy(
                    src_ref=ins[j].at[pl.ds(pl.multiple_of(pidx * r, 16), r), :], dst_ref=outs[j].at[me],
                    send_sem=sems[0].at[(k - 1) * self.n + j], recv_sem=sems[1].at[(k - 1) * self.n + j],
                    device_id=peer, device_id_type=MESH))
        return local, remote

    def start(self, ins, outs, sems):
        local, remote = self._copies(ins, outs, sems)
        for cp in local + remote:
            cp.start()

    def mid(self, ins, outs, sems):
        pass

    def finish(self, ins, outs, sems):
        local, remote = self._copies(ins, outs, sems)
        for cp in remote + local:
            cp.wait()


class _BroadcastCarry:
    def __init__(self, parts):
        self.inputs = list(parts)
        self.n = len(self.inputs)
        self.out_shape = [jax.ShapeDtypeStruct((NDEV,) + p.shape, p.dtype) for p in self.inputs]
        self.scratch = [pltpu.SemaphoreType.DMA((7 * self.n,)), pltpu.SemaphoreType.DMA((7 * self.n,)),
                        pltpu.SemaphoreType.DMA((self.n,))]

    def _copies(self, ins, outs, sems):
        me = _my_index()
        cps = []
        for j in range(self.n):
            cps.append(pltpu.make_async_copy(ins[j], outs[j].at[me], sems[2].at[j]))
            cps += [pltpu.make_async_remote_copy(
                src_ref=ins[j], dst_ref=outs[j].at[me], send_sem=sems[0].at[(k - 1) * self.n + j],
                recv_sem=sems[1].at[(k - 1) * self.n + j], device_id=_peer(k), device_id_type=MESH)
                for k in range(1, NDEV)]
        return cps

    def start(self, ins, outs, sems):
        for cp in self._copies(ins, outs, sems):
            cp.start()

    def mid(self, ins, outs, sems):
        pass

    def finish(self, ins, outs, sems):
        for cp in self._copies(ins, outs, sems):
            cp.wait()


class _PairCarry:
    def __init__(self, piece):
        self.inputs = [piece]
        self.r = piece.shape[0] // NDEV
        self.out_shape = [jax.ShapeDtypeStruct((4, self.r, piece.shape[1]), piece.dtype)]
        self.scratch = [pltpu.SemaphoreType.DMA((4,)), pltpu.SemaphoreType.DMA((4,))]

    def _copies(self, ins, outs, sems):
        x, y, c = lax.axis_index("x"), lax.axis_index("y"), lax.axis_index("c")
        return [pltpu.make_async_remote_copy(
            src_ref=ins[0].at[pl.ds(pl.multiple_of((2 * q + 1 - c) * self.r, 16), self.r), :], dst_ref=outs[0].at[q],
            send_sem=sems[0].at[q], recv_sem=sems[1].at[q], device_id=(x, y, 1 - c), device_id_type=MESH)
            for q in range(4)]

    def start(self, ins, outs, sems):
        for cp in self._copies(ins, outs, sems):
            cp.start()

    def mid(self, ins, outs, sems):
        pass

    def finish(self, ins, outs, sems):
        for cp in self._copies(ins, outs, sems):
            cp.wait()


class _ChipSumCarry:
    def __init__(self, piece, landed):
        self.inputs = [piece, landed]
        self.r, dm = piece.shape[0] // NDEV, piece.shape[1]
        self.out_shape = [jax.ShapeDtypeStruct((4, self.r, dm), piece.dtype)]
        self.scratch = [pltpu.VMEM((4, self.r, dm), piece.dtype), pltpu.VMEM((8, self.r, dm), piece.dtype),
                        pltpu.SemaphoreType.DMA((8,)), pltpu.SemaphoreType.DMA((3,)), pltpu.SemaphoreType.DMA((3,)),
                        pltpu.SemaphoreType.DMA(())]

    def _copies(self, outs, scr):
        sums, _, _, send_sems, recv_sems, local_sem = scr
        x, y, c = lax.axis_index("x"), lax.axis_index("y"), lax.axis_index("c")
        mine = 2 * x + y
        local = pltpu.make_async_copy(sums.at[mine], outs[0].at[mine], local_sem)
        remote = []
        for k in range(1, 4):
            px = 1 - x if k & 2 else x
            py = 1 - y if k & 1 else y
            remote.append(pltpu.make_async_remote_copy(
                src_ref=sums.at[2 * px + py], dst_ref=outs[0].at[mine], send_sem=send_sems.at[k - 1],
                recv_sem=recv_sems.at[k - 1], device_id=(px, py, c), device_id_type=MESH))
        return local, remote

    def start(self, ins, outs, scr):
        sums, stage, stage_sems = scr[0], scr[1], scr[2]
        c = lax.axis_index("c")
        loads = []
        for q in range(4):
            loads.append((
                pltpu.make_async_copy(ins[0].at[pl.ds(pl.multiple_of((2 * q + c) * self.r, 16), self.r), :],
                                      stage.at[2 * q], stage_sems.at[2 * q]),
                pltpu.make_async_copy(ins[1].at[q], stage.at[2 * q + 1], stage_sems.at[2 * q + 1])))
        for a, b in loads:
            a.start()
            b.start()
        for q, (a, b) in enumerate(loads):
            a.wait()
            b.wait()
            sums[q] = (stage[2 * q].astype(F32) + stage[2 * q + 1].astype(F32)).astype(sums.dtype)
        local, remote = self._copies(outs, scr)
        for cp in [local] + remote:
            cp.start()

    def mid(self, ins, outs, scr):
        pass

    def finish(self, ins, outs, scr):
        local, remote = self._copies(outs, scr)
        for cp in remote + [local]:
            cp.wait()


def _call(spec, carry=None, late=False):
    body, grid = spec["body"], spec["grid"]
    in_specs, out_specs, out_shape = list(spec["in_specs"]), list(spec["out_specs"]), list(spec["out_shape"])
    scratch, args = list(spec.get("scratch", [])), list(spec["args"])
    if carry is None:
        out = pl.pallas_call(body, grid=grid, in_specs=in_specs, out_specs=tuple(out_specs),
                             out_shape=tuple(out_shape), scratch_shapes=scratch, compiler_params=_cp(len(grid)),
                             name=spec["name"])(*args)
        return tuple(out), ()
    carries = list(carry) if isinstance(carry, (list, tuple)) else [carry]
    n_in, n_out, n_s = len(in_specs), len(out_specs), len(scratch)
    steps = int(np.prod(grid))

    def split(refs, counts):
        parts, o = [], 0
        for cnt in counts:
            parts.append(refs[o:o + cnt])
            o += cnt
        return parts

    c_in = [len(cr.inputs) for cr in carries]
    c_out = [len(cr.out_shape) for cr in carries]
    c_scr = [len(cr.scratch) for cr in carries]

    def wrapped(*refs):
        ins, cins, outs, couts, scr, cscr = split(refs, [n_in, sum(c_in), n_out, sum(c_out), n_s, sum(c_scr)])
        per = list(zip(carries, split(cins, c_in), split(couts, c_out), split(cscr, c_scr)))
        step = pl.program_id(0)
        for ax in range(1, len(grid)):
            step = step * grid[ax] + pl.program_id(ax)

        def start():
            @pl.when(step == 0)
            def _():
                for cr, ci, co, cs in per:
                    cr.start(ci, co, cs)

        if not late:
            start()
        if steps >= 3:
            @pl.when(step == steps - 2)
            def _():
                for cr, ci, co, cs in per:
                    cr.mid(ci, co, cs)
        body(*ins, *outs, *scr)
        if late:
            start()

        @pl.when(step == steps - 1)
        def _():
            for cr, ci, co, cs in per:
                if steps < 3:
                    cr.mid(ci, co, cs)
                cr.finish(ci, co, cs)

    out = pl.pallas_call(
        wrapped, grid=grid, in_specs=in_specs + [ANY] * sum(c_in), out_specs=tuple(out_specs + [ANY] * sum(c_out)),
        out_shape=tuple(out_shape + [s for cr in carries for s in cr.out_shape]),
        scratch_shapes=scratch + [s for cr in carries for s in cr.scratch],
        compiler_params=_cp(len(grid)), name=spec["name"])(*args, *[a for cr in carries for a in cr.inputs])
    c_res = [tuple(p) for p in split(out[n_out:], c_out)]
    return tuple(out[:n_out]), (c_res if isinstance(carry, (list, tuple)) else c_res[0])


def _rms_fwd(x, gain):
    r = lax.rsqrt(jnp.mean(x * x, axis=-1, keepdims=True) + EPS)
    return x * r * gain, r


def _rms_bwd(dh, x, r, gain):
    a = dh * gain
    dx = r * a - x * (r * r * r) * jnp.mean(a * x, axis=-1, keepdims=True)
    dgain = jnp.sum(dh * (x * r), axis=0, keepdims=True)
    return dx, dgain


def _gelu(x):
    return 0.5 * x * (1.0 + lax.erf(x * 0.7071067811865476))


def _gelu_grad(x):
    return 0.5 * (1.0 + lax.erf(x * 0.7071067811865476)) + x * jnp.exp(-0.5 * x * x) * 0.3989422804014327


def _sigmoid(x):
    return 1.0 / (1.0 + jnp.exp(-x))


def _adamw_math(w, g, m, v):
    nm = ADAM_B1 * m + (1.0 - ADAM_B1) * g
    nv = ADAM_B2 * v + (1.0 - ADAM_B2) * (g * g)
    m_hat = nm / (1.0 - ADAM_B1 ** ADAM_STEP)
    v_hat = nv / (1.0 - ADAM_B2 ** ADAM_STEP)
    return -ADAM_LR * (m_hat / (jnp.sqrt(v_hat) + ADAM_EPS) + ADAM_WD * w), nm, nv


def _tok(tm, w):
    return pl.BlockSpec((tm, w), lambda i: (i, 0))


def _full(shape):
    return pl.BlockSpec(shape, lambda *i: (0,) * len(shape))


def _norm_proj(x, gain, gath, out_dtype, name, tm):
    t, dm = x.shape
    n = gath.shape[1] * NDEV

    def body(x_ref, g_ref, gath_ref, proj_ref, hb_ref, wbuf, sems):
        @pl.when(pl.program_id(0) == 0)
        def _():
            _load_weight(gath_ref, wbuf, sems)
        h, _ = _rms_fwd(x_ref[...], g_ref[...])
        hb = h.astype(BF16)
        hb_ref[...] = hb
        proj_ref[...] = _dot(hb, wbuf[...], NT).astype(out_dtype)

    return dict(
        body=body, grid=(t // tm,), name=name, args=[x, gain, gath],
        out_shape=[jax.ShapeDtypeStruct((t, n), out_dtype), jax.ShapeDtypeStruct((t, dm), BF16)],
        in_specs=[_tok(tm, dm), _full((1, dm)), ANY], out_specs=[_tok(tm, n), _tok(tm, dm)],
        scratch=[pltpu.VMEM((n, dm), BF16), pltpu.SemaphoreType.DMA((NDEV,))])


def _proj_bwd_norm(dys, x, gain, dres, gath, name, tm):
    t, dm = x.shape
    n = gath.shape[1] * NDEV
    widths = [d.shape[1] for d in dys]
    assert sum(widths) == n
    nd = len(dys)

    def body(*refs):
        dy_refs = refs[:nd]
        x_ref, g_ref, dres_ref, gath_ref, dx_ref, dxb_ref, dgain_ref, wbuf, sems = refs[nd:]

        @pl.when(pl.program_id(0) == 0)
        def _():
            _load_weight(gath_ref, wbuf, sems)
            dgain_ref[...] = jnp.zeros_like(dgain_ref)
        xv, gain_v = x_ref[...], g_ref[...]
        _, r = _rms_fwd(xv, gain_v)
        dh, c0 = None, 0
        for dy_ref, wd in zip(dy_refs, widths):
            part = _dot(dy_ref[...], wbuf[c0:c0 + wd, :], NN)
            dh = part if dh is None else dh + part
            c0 += wd
        dx, dgain = _rms_bwd(dh, xv, r, gain_v)
        dx = dres_ref[...] + dx
        dx_ref[...] = dx
        dxb_ref[...] = dx.astype(BF16)
        dgain_ref[...] += dgain

    return dict(
        body=body, grid=(t // tm,), name=name, args=[*dys, x, gain, dres, gath],
        out_shape=[jax.ShapeDtypeStruct((t, dm), F32), jax.ShapeDtypeStruct((t, dm), BF16),
                   jax.ShapeDtypeStruct((1, dm), F32)],
        in_specs=[_tok(tm, wd) for wd in widths] + [_tok(tm, dm), _full((1, dm)), _tok(tm, dm), ANY],
        out_specs=[_tok(tm, dm), _tok(tm, dm), _full((1, dm))],
        scratch=[pltpu.VMEM((n, dm), BF16), pltpu.SemaphoreType.DMA((NDEV,))])


def _wgrad(a, b, name, tmm=256):
    parts = list(a) if isinstance(a, (list, tuple)) else [a]
    t = parts[0].shape[0]
    n = b.shape[1]
    tiles = [p.shape[1] // tmm for p in parts]
    first = [sum(tiles[:i]) for i in range(len(parts))]
    m = sum(tiles) * tmm

    def body(*refs):
        a_refs, b_ref, o_ref = refs[:len(parts)], refs[len(parts)], refs[len(parts) + 1]
        j = pl.program_id(0)
        for a_ref, j0, nt in zip(a_refs, first, tiles):
            if len(parts) == 1:
                o_ref[...] = _dot(a_ref[...], b_ref[...], TN).astype(BF16)
            else:
                @pl.when((j >= j0) & (j < j0 + nt))
                def _():
                    o_ref[...] = _dot(a_ref[...], b_ref[...], TN).astype(BF16)

    a_specs = [pl.BlockSpec((t, tmm), lambda j, j0=j0, nt=nt: (0, jnp.clip(j - j0, 0, nt - 1)))
               for j0, nt in zip(first, tiles)]
    return dict(
        body=body, grid=(sum(tiles),), name=name, args=[*parts, b], out_shape=[jax.ShapeDtypeStruct((m, n), BF16)],
        in_specs=a_specs + [pl.BlockSpec((t, n), lambda j: (0, 0))],
        out_specs=[pl.BlockSpec((tmm, n), lambda j: (j, 0))])


def _halo_specs(tm, t, width, col_blocks):
    nb8 = tm // 8
    last = t // 8 - 1
    prev = [pl.BlockSpec((8, width), lambda i, cb=cb: (jnp.maximum(i * nb8 - 1, 0), cb)) for cb in col_blocks]
    nxt = [pl.BlockSpec((8, width), lambda i, cb=cb: (jnp.minimum((i + 1) * nb8, last), cb)) for cb in col_blocks]
    return prev, nxt


def _shift_rows(z, prev_row, next_row):
    tm = z.shape[0]
    row = lax.broadcasted_iota(jnp.int32, z.shape, 0)
    zm1 = jnp.where(row == 0, prev_row, pltpu.roll(z, 1, 0))
    zp1 = jnp.where(row == tm - 1, next_row, pltpu.roll(z, tm - 1, 0))
    return zm1, zp1


def _gating_fwd(proj, lng, lnb, wsp_ref, bsp_ref, aw):
    tm = proj.shape[0]
    a_u = _gelu(proj[:, 0:aw])
    gv = _gelu(proj[:, aw:2 * aw])
    mu = jnp.mean(gv, axis=-1, keepdims=True)
    xc = gv - mu
    rstd = lax.rsqrt(jnp.mean(xc * xc, axis=-1, keepdims=True) + EPS)
    vn = xc * rstd
    a_v = (vn * lng + lnb).astype(BF16)
    gd = aw // A_GROUPS
    rows = []
    for c in range(tm // CHUNK):
        cols = []
        for g in range(A_GROUPS):
            blk = a_v[c * CHUNK:(c + 1) * CHUNK, g * gd:(g + 1) * gd]
            cols.append(_dot(wsp_ref[g], blk, NN) + bsp_ref[g])
        rows.append(jnp.concatenate(cols, axis=1))
    mixed = jnp.concatenate(rows, axis=0)
    return a_u, vn, rstd, a_v, mixed


def _even_core_fwd(proj, x0, lng, lnb, wsp, bspb, cw, gath, tm):
    t, dm = x0.shape
    aw = lng.shape[1]
    bw = cw.shape[1]
    assert aw == bw and 2 * aw + 3 * bw == proj.shape[1]
    nt = t // tm
    prev, nxt = _halo_specs(tm, t, bw, [3, 4])

    def body(proj_ref, cp_ref, hp_ref, cn_ref, hn_ref, x0_ref, lng_ref, lnb_ref, wsp_ref, bsp_ref, cw_ref, gath_ref,
             x1_ref, y_ref, wbuf, sems):
        i = pl.program_id(0)

        @pl.when(i == 0)
        def _():
            _load_weight(gath_ref, wbuf, sems)
        proj_v = proj_ref[...]
        a_u, _, _, _, mixed = _gating_fwd(proj_v, lng_ref[...], lnb_ref[...], wsp_ref, bsp_ref, aw)
        a_out = a_u * mixed
        bb = proj_v[:, 2 * aw:2 * aw + bw]
        z = proj_v[:, 2 * aw + bw:2 * aw + 2 * bw] * proj_v[:, 2 * aw + 2 * bw:]
        zprev = jnp.where(i > 0, cp_ref[7:8, :] * hp_ref[7:8, :], 0.0)
        znext = jnp.where(i < nt - 1, cn_ref[0:1, :] * hn_ref[0:1, :], 0.0)
        zm1, zp1 = _shift_rows(z, zprev, znext)
        cwv = cw_ref[...]
        conv = zm1 * cwv[0:1, :] + z * cwv[1:2, :] + zp1 * cwv[2:3, :]
        y = jnp.concatenate([a_out, bb * conv], axis=1).astype(BF16)
        y_ref[...] = y
        x1_ref[...] = x0_ref[...] + _dot(y, wbuf[...], NN)

    return dict(
        body=body, grid=(nt,), name="even_core_fwd",
        args=[proj, proj, proj, proj, proj, x0, lng, lnb, wsp, bspb, cw, gath],
        out_shape=[jax.ShapeDtypeStruct((t, dm), F32), jax.ShapeDtypeStruct((t, aw + bw), BF16)],
        in_specs=[_tok(tm, proj.shape[1]), prev[0], prev[1], nxt[0], nxt[1], _tok(tm, dm), _full(lng.shape),
                  _full(lnb.shape), _full(wsp.shape), _full(bspb.shape), _full(cw.shape), ANY],
        out_specs=[_tok(tm, dm), _tok(tm, aw + bw)],
        scratch=[pltpu.VMEM((gath.shape[1] * NDEV, dm), BF16), pltpu.SemaphoreType.DMA((NDEV,))])


def _even_core_bwd(proj, dx1, lng, lnb, wsp, bspb, cw, gath, tm):
    t, dm = dx1.shape
    aw, bw = lng.shape[1], cw.shape[1]
    gd = aw // A_GROUPS
    nt = t // tm
    inw = proj.shape[1]
    prev, nxt = _halo_specs(tm, t, bw, [2, 3, 4])
    nb8 = tm // 8
    last8 = t // 8 - 1

    def body(proj_ref, bp_ref, cp_ref, hp_ref, bn_ref, cn_ref, hn_ref, dx_ref, dxp_ref, dxn_ref,
             lng_ref, lnb_ref, wsp_ref, bsp_ref, cw_ref, gath_ref,
             dproj_ref, dlng_ref, dlnb_ref, dwsp_ref, dbsp_ref, dcw_ref, wbuf, sems):
        i = pl.program_id(0)

        @pl.when(i == 0)
        def _():
            _load_weight(gath_ref, wbuf, sems)
            dlng_ref[...] = jnp.zeros_like(dlng_ref)
            dlnb_ref[...] = jnp.zeros_like(dlnb_ref)
            dwsp_ref[...] = jnp.zeros_like(dwsp_ref)
            dbsp_ref[...] = jnp.zeros_like(dbsp_ref)
            dcw_ref[...] = jnp.zeros_like(dcw_ref)
        proj_v = proj_ref[...]
        lng_v = lng_ref[...]
        a_u, vn, rstd, a_v, mixed = _gating_fwd(proj_v, lng_v, lnb_ref[...], wsp_ref, bsp_ref, aw)
        w = wbuf[...]
        dy = _dot(dx_ref[...].astype(BF16), w, NT)
        da_out, db_out = dy[:, 0:aw], dy[:, aw:]
        da_u = da_out * mixed
        dmixed = da_out * a_u
        dmb = dmixed.astype(BF16)
        rows = []
        for c in range(tm // CHUNK):
            cols = []
            for g in range(A_GROUPS):
                r0, c0 = c * CHUNK, g * gd
                dm_cg = dmb[r0:r0 + CHUNK, c0:c0 + gd]
                cols.append(_dot(wsp_ref[g], dm_cg, TN))
                dwsp_ref[g] += _dot(dm_cg, a_v[r0:r0 + CHUNK, c0:c0 + gd], NT)
                dbsp_ref[g] += dmixed[r0:r0 + CHUNK, c0:c0 + gd]
            rows.append(jnp.concatenate(cols, axis=1))
        dav = jnp.concatenate(rows, axis=0)
        dlng_ref[...] += jnp.sum(dav * vn, axis=0, keepdims=True)
        dlnb_ref[...] += jnp.sum(dav, axis=0, keepdims=True)
        dvn = dav * lng_v
        dgv = rstd * (dvn - jnp.mean(dvn, axis=-1, keepdims=True) - vn * jnp.mean(dvn * vn, axis=-1, keepdims=True))
        dv_pre = dgv * _gelu_grad(proj_v[:, aw:2 * aw])
        du_pre = da_u * _gelu_grad(proj_v[:, 0:aw])
        bb = proj_v[:, 2 * aw:2 * aw + bw]
        bc = proj_v[:, 2 * aw + bw:2 * aw + 2 * bw]
        bh = proj_v[:, 2 * aw + 2 * bw:]
        z = bc * bh
        zprev = jnp.where(i > 0, cp_ref[7:8, :] * hp_ref[7:8, :], 0.0)
        znext = jnp.where(i < nt - 1, cn_ref[0:1, :] * hn_ref[0:1, :], 0.0)
        zm1, zp1 = _shift_rows(z, zprev, znext)
        cwv = cw_ref[...]
        conv = zm1 * cwv[0:1, :] + z * cwv[1:2, :] + zp1 * cwv[2:3, :]
        dbb = db_out * conv
        dconv = db_out * bb
        dx_edge = jnp.concatenate([dxp_ref[...], dxn_ref[...]], axis=0).astype(BF16)
        dy_edge = _dot(dx_edge, w[aw:, :], NT)
        dcprev = jnp.where(i > 0, dy_edge[7:8, :] * bp_ref[7:8, :], 0.0)
        dcnext = jnp.where(i < nt - 1, dy_edge[8:9, :] * bn_ref[0:1, :], 0.0)
        dcm1, dcp1 = _shift_rows(dconv, dcprev, dcnext)
        dz = dcp1 * cwv[0:1, :] + dconv * cwv[1:2, :] + dcm1 * cwv[2:3, :]
        dcw_ref[0:1, :] += jnp.sum(dconv * zm1, axis=0, keepdims=True)
        dcw_ref[1:2, :] += jnp.sum(dconv * z, axis=0, keepdims=True)
        dcw_ref[2:3, :] += jnp.sum(dconv * zp1, axis=0, keepdims=True)
        dproj_ref[...] = jnp.concatenate([du_pre, dv_pre, dbb, dz * bh, dz * bc], axis=1).astype(BF16)

    row8 = lambda f: pl.BlockSpec((8, dm), f)
    return dict(
        body=body, grid=(nt,), name="even_core_bwd",
        args=[proj, proj, proj, proj, proj, proj, proj, dx1, dx1, dx1, lng, lnb, wsp, bspb, cw, gath],
        out_shape=[jax.ShapeDtypeStruct((t, inw), BF16), jax.ShapeDtypeStruct((1, aw), F32),
                   jax.ShapeDtypeStruct((1, aw), F32), jax.ShapeDtypeStruct(wsp.shape, F32),
                   jax.ShapeDtypeStruct((A_GROUPS, CHUNK, gd), F32), jax.ShapeDtypeStruct(cw.shape, F32)],
        in_specs=[_tok(tm, inw), prev[0], prev[1], prev[2], nxt[0], nxt[1], nxt[2], _tok(tm, dm),
                  row8(lambda i: (jnp.maximum(i * nb8 - 1, 0), 0)), row8(lambda i: (jnp.minimum((i + 1) * nb8, last8), 0)),
                  _full(lng.shape), _full(lnb.shape), _full(wsp.shape), _full(bspb.shape), _full(cw.shape), ANY],
        out_specs=[_tok(tm, inw), _full((1, aw)), _full((1, aw)), _full(wsp.shape),
                   _full((A_GROUPS, CHUNK, gd)), _full(cw.shape)],
        scratch=[pltpu.VMEM((gath.shape[1] * NDEV, dm), BF16), pltpu.SemaphoreType.DMA((NDEV,))])


def _ff_chunks(f, width=1024):
    return [(c0, min(c0 + width, f)) for c0 in range(0, f, width)]


def _ffn_up(x, gain, gath_g, gath_u, name, tm):
    t, dm = x.shape
    f = gath_g.shape[1] * NDEV

    def body(x_ref, g_ref, gg_ref, gu_ref, gate_ref, up_ref, act_ref, wg, wu, sems):
        @pl.when(pl.program_id(0) == 0)
        def _():
            _load_weight(gg_ref, wg, sems)
            _load_weight(gu_ref, wu, sems)
        h, _ = _rms_fwd(x_ref[...], g_ref[...])
        hb = h.astype(BF16)
        for c0, c1 in _ff_chunks(f):
            gate = _dot(hb, wg[c0:c1, :], NT)
            up = _dot(hb, wu[c0:c1, :], NT)
            gate_ref[:, c0:c1] = gate.astype(BF16)
            up_ref[:, c0:c1] = up.astype(BF16)
            act_ref[:, c0:c1] = (gate * _sigmoid(gate) * up).astype(BF16)

    o = jax.ShapeDtypeStruct((t, f), BF16)
    return dict(
        body=body, grid=(t // tm,), name=name, args=[x, gain, gath_g, gath_u], out_shape=[o, o, o],
        in_specs=[_tok(tm, dm), _full((1, dm)), ANY, ANY], out_specs=[_tok(tm, f)] * 3,
        scratch=[pltpu.VMEM((f, dm), BF16), pltpu.VMEM((f, dm), BF16), pltpu.SemaphoreType.DMA((NDEV,))])


def _ffn_down(x, act, gath_d, name, tm):
    t, dm = x.shape
    f = act.shape[1]

    def body(x_ref, a_ref, gd_ref, xo_ref, wd, sems):
        @pl.when(pl.program_id(0) == 0)
        def _():
            _load_weight(gd_ref, wd, sems)
        xo_ref[...] = x_ref[...] + _dot(a_ref[...], wd[...], NN)

    return dict(
        body=body, grid=(t // tm,), name=name, args=[x, act, gath_d], out_shape=[jax.ShapeDtypeStruct((t, dm), F32)],
        in_specs=[_tok(tm, dm), _tok(tm, f), ANY], out_specs=[_tok(tm, dm)],
        scratch=[pltpu.VMEM((f, dm), BF16), pltpu.SemaphoreType.DMA((NDEV,))])


def _ffn_down_loss(x, act, gath_d, target, gain, name, tm):
    t, dm = x.shape
    f = act.shape[1]
    steps = t // tm

    def body(x_ref, a_ref, gd_ref, t_ref, g_ref, loss_ref, dx_ref, dxb_ref, dgain_ref, wd, acc, sems):
        i = pl.program_id(0)

        @pl.when(i == 0)
        def _():
            _load_weight(gd_ref, wd, sems)
            acc[...] = jnp.zeros_like(acc)
            dgain_ref[...] = jnp.zeros_like(dgain_ref)
        xv = x_ref[...] + _dot(a_ref[...], wd[...], NN)
        gain_v = g_ref[...]
        y, r = _rms_fwd(xv, gain_v)
        e = y - t_ref[...]
        acc[...] += jnp.sum(e * e, axis=0, keepdims=True)
        dx, dgain = _rms_bwd(e * (1.0 / dm), xv, r, gain_v)
        dx_ref[...] = dx
        dxb_ref[...] = dx.astype(BF16)
        dgain_ref[...] += dgain

        @pl.when(i == steps - 1)
        def _():
            loss_ref[...] = jnp.sum(acc[...], axis=-1, keepdims=True) * (0.5 / dm)

    return dict(
        body=body, grid=(steps,), name=name, args=[x, act, gath_d, target, gain],
        out_shape=[jax.ShapeDtypeStruct((1, 1), F32), jax.ShapeDtypeStruct((t, dm), F32),
                   jax.ShapeDtypeStruct((t, dm), BF16), jax.ShapeDtypeStruct((1, dm), F32)],
        in_specs=[_tok(tm, dm), _tok(tm, f), ANY, _tok(tm, dm), _full((1, dm))],
        out_specs=[_full((1, 1)), _tok(tm, dm), _tok(tm, dm), _full((1, dm))],
        scratch=[pltpu.VMEM((f, dm), BF16), pltpu.VMEM((1, dm), F32), pltpu.SemaphoreType.DMA((NDEV,))])


def _ffn_bwd(dxo, x, gate, up, gain, gath_g, gath_u, gath_d, name, tm):
    t, dm = x.shape
    f = gate.shape[1]

    def body(dxo_ref, x_ref, gate_ref, up_ref, g_ref, gg_ref, gu_ref, gd_ref,
             dx_ref, dxb_ref, dg_ref, du_ref, hb_ref, dgain_ref, wg, wu, wd, sems):
        @pl.when(pl.program_id(0) == 0)
        def _():
            _load_weight(gg_ref, wg, sems)
            _load_weight(gu_ref, wu, sems)
            _load_weight(gd_ref, wd, sems)
            dgain_ref[...] = jnp.zeros_like(dgain_ref)
        xv, gain_v, dxo_v = x_ref[...], g_ref[...], dxo_ref[...]
        h, r = _rms_fwd(xv, gain_v)
        hb_ref[...] = h.astype(BF16)
        dxob = dxo_v.astype(BF16)
        dh = jnp.zeros_like(xv)
        for c0, c1 in _ff_chunks(f):
            gate_v = gate_ref[:, c0:c1].astype(F32)
            up_v = up_ref[:, c0:c1].astype(F32)
            s = _sigmoid(gate_v)
            silu = gate_v * s
            dact = _dot(dxob, wd[c0:c1, :], NT)
            dg = (dact * up_v * (s * (1.0 + gate_v * (1.0 - s)))).astype(BF16)
            du = (dact * silu).astype(BF16)
            dg_ref[:, c0:c1] = dg
            du_ref[:, c0:c1] = du
            dh = dh + _dot(dg, wg[c0:c1, :], NN) + _dot(du, wu[c0:c1, :], NN)
        dx, dgain = _rms_bwd(dh, xv, r, gain_v)
        dx = dxo_v + dx
        dx_ref[...] = dx
        dxb_ref[...] = dx.astype(BF16)
        dgain_ref[...] += dgain

    return dict(
        body=body, grid=(t // tm,), name=name, args=[dxo, x, gate, up, gain, gath_g, gath_u, gath_d],
        out_shape=[jax.ShapeDtypeStruct((t, dm), F32), jax.ShapeDtypeStruct((t, dm), BF16),
                   jax.ShapeDtypeStruct((t, f), BF16), jax.ShapeDtypeStruct((t, f), BF16),
                   jax.ShapeDtypeStruct((t, dm), BF16), jax.ShapeDtypeStruct((1, dm), F32)],
        in_specs=[_tok(tm, dm), _tok(tm, dm), _tok(tm, f), _tok(tm, f), _full((1, dm)), ANY, ANY, ANY],
        out_specs=[_tok(tm, dm), _tok(tm, dm), _tok(tm, f), _tok(tm, f), _tok(tm, dm), _full((1, dm))],
        scratch=[pltpu.VMEM((f, dm), BF16), pltpu.VMEM((f, dm), BF16), pltpu.VMEM((f, dm), BF16),
                 pltpu.SemaphoreType.DMA((NDEV,))])


def _t5_buckets(rel):
    nb = N_BUCKETS // 2
    ret = jnp.where(rel > 0, nb, 0)
    n = jnp.abs(rel)
    max_exact = nb // 2
    nf = jnp.maximum(n, 1).astype(jnp.float32)
    large = max_exact + (jnp.log(nf / max_exact) / math.log(MAX_DISTANCE / max_exact)
                         * (nb - max_exact)).astype(jnp.int32)
    large = jnp.minimum(large, nb - 1)
    return ret + jnp.where(n < max_exact, n, large)


def _bucket_table():
    qi = jnp.arange(CHUNK, dtype=jnp.int32)[:, None]
    kj = jnp.arange(3 * CHUNK, dtype=jnp.int32)[None, :]
    rel = kj - CHUNK - qi
    return jnp.where(jnp.abs(rel) <= CHUNK, _t5_buckets(rel), -1)


def _bias_table(rel_bias_t, buckets):
    nh = rel_bias_t.shape[0]

    def body(rb_ref, bk_ref, o_ref):
        bk = bk_ref[...]
        for h in range(nh):
            acc = jnp.where(bk < 0, NEG, 0.0).astype(F32)
            for b in range(N_BUCKETS):
                acc = jnp.where(bk == b, rb_ref[h, b] * LOG2E, acc)
            o_ref[h] = acc

    return dict(
        body=body, grid=(1,), name="bias_table", args=[rel_bias_t, buckets],
        out_shape=[jax.ShapeDtypeStruct((nh,) + buckets.shape, F32)],
        in_specs=[pl.BlockSpec(memory_space=pltpu.SMEM), _full(buckets.shape)],
        out_specs=[_full((nh,) + buckets.shape)])


def _rel_bias_grad(dbias, buckets):
    nh = dbias.shape[0]

    def body(db_ref, bk_ref, o_ref):
        bk = bk_ref[...]
        lane = lax.broadcasted_iota(jnp.int32, (1, 128), 1)
        for h in range(nh):
            d = db_ref[h]
            row = jnp.zeros((1, 128), F32)
            for b in range(N_BUCKETS):
                s = jnp.sum(jnp.sum(jnp.where(bk == b, d, 0.0), axis=1, keepdims=True), axis=0, keepdims=True)
                row = jnp.where(lane == b, s, row)
            o_ref[h:h + 1, :] = row

    return pl.pallas_call(
        body, out_shape=jax.ShapeDtypeStruct((nh, 128), F32),
        in_specs=[pl.BlockSpec(memory_space=pltpu.VMEM), pl.BlockSpec(memory_space=pltpu.VMEM)],
        out_specs=pl.BlockSpec(memory_space=pltpu.VMEM), compiler_params=_cp(0), name="rel_bias_grad")(dbias, buckets)


def _half_masks():
    lane = lax.broadcasted_iota(jnp.int32, (CHUNK, 128), 1)
    return lane < HEAD_DIM, lane >= HEAD_DIM


def _kv_low(ref, starts, hk, lo):
    kt = (hk // 2) * 128
    out = []
    for jj in range(3):
        blk = ref[pl.ds(starts[jj], CHUNK), kt:kt + 128]
        if hk % 2 == 1:
            blk = pltpu.roll(blk, HEAD_DIM, 1)
        out.append(jnp.where(lo, blk, jnp.zeros_like(blk)))
    return out


def _stack_heads(tile_a, tile_b):
    return jnp.concatenate([tile_a, pltpu.roll(tile_a, HEAD_DIM, 1), tile_b, pltpu.roll(tile_b, HEAD_DIM, 1)], axis=0)


def _unstack_heads(o4):
    return (o4[0:CHUNK] + pltpu.roll(o4[CHUNK:2 * CHUNK], HEAD_DIM, 1),
            o4[2 * CHUNK:3 * CHUNK] + pltpu.roll(o4[3 * CHUNK:], HEAD_DIM, 1))


ATT_SLAB = 32


def _softmax_slab(s_scr, hk, g, r0, bias_ref, sink_ref, n, nblk):
    scale = HEAD_DIM ** -0.5 * LOG2E
    h = (N_HEADS // N_KV) * hk + g
    s = []
    for jj in range(3):
        sj = (s_scr[hk, jj, pl.ds(g * CHUNK + r0, ATT_SLAB), :] * scale
              + bias_ref[h, pl.ds(r0, ATT_SLAB), jj * CHUNK:(jj + 1) * CHUNK])
        if jj == 0:
            sj = jnp.where(n > 0, sj, NEG)
        if jj == 2:
            sj = jnp.where(n < nblk - 1, sj, NEG)
        s.append(sj)
    sink = sink_ref[h] * LOG2E
    m = jnp.maximum(jnp.max(jnp.maximum(jnp.maximum(s[0], s[1]), s[2]), axis=-1, keepdims=True), sink)
    e = [jnp.exp2(sj - m) for sj in s]
    es = jnp.exp2(sink - m)
    inv = 1.0 / (jnp.sum(e[0] + e[1] + e[2], axis=-1, keepdims=True) + es)
    return [ej * inv for ej in e], es * inv


def _key_block_starts(n, nblk):
    return [pl.multiple_of(jnp.clip(n - 1 + jj, 0, nblk - 1) * CHUNK, CHUNK) for jj in range(3)]


def _attn_fwd(qkv, x2, bias, sink, gath):
    t, dm = x2.shape
    nblk = t // CHUNK
    kvw = N_KV * HEAD_DIM
    kcb, vcb = dm // kvw, dm // kvw + 1
    slab = (N_KV, 3, 4 * CHUNK, CHUNK)

    def body(q_ref, k_ref, v_ref, x2_ref, bias_ref, sink_ref, gath_ref, x3_ref, att_ref, p_ref, ps_ref,
             wbuf, s_scr, sems):
        n = pl.program_id(0)

        @pl.when(n == 0)
        def _():
            _load_weight(gath_ref, wbuf, sems)
        lo, _ = _half_masks()
        lane_s = lax.broadcasted_iota(jnp.int32, (ATT_SLAB, 128), 1)
        starts = _key_block_starts(n, nblk)
        tiles = []
        for hk in range(N_KV):
            c0 = (2 * hk) * 128
            k_lo = _kv_low(k_ref, starts, hk, lo)
            v_lo = _kv_low(v_ref, starts, hk, lo)
            q4 = _stack_heads(q_ref[:, c0:c0 + 128], q_ref[:, c0 + 128:c0 + 256])
            for jj in range(3):
                s_scr[hk, jj] = _dot(q4, k_lo[jj], NT)
            for g in range(4):
                h = 4 * hk + g
                for r0 in range(0, CHUNK, ATT_SLAB):
                    p, ps = _softmax_slab(s_scr, hk, g, r0, bias_ref, sink_ref, n, nblk)
                    for jj in range(3):
                        p_ref[hk, jj, g * CHUNK + r0:g * CHUNK + r0 + ATT_SLAB, :] = p[jj].astype(BF16)
                    rest = jnp.zeros((ATT_SLAB, 128), F32) if h == 0 else ps_ref[r0:r0 + ATT_SLAB, :]
                    ps_ref[r0:r0 + ATT_SLAB, :] = jnp.where(lane_s == h, ps, rest)
            o4 = _dot(p_ref[hk, 0], v_lo[0], NN) + _dot(p_ref[hk, 1], v_lo[1], NN) + _dot(p_ref[hk, 2], v_lo[2], NN)
            tiles += list(_unstack_heads(o4))
        att = jnp.concatenate(tiles, axis=1).astype(BF16)
        att_ref[...] = att
        x3_ref[...] = x2_ref[...] + _dot(att, wbuf[...], NN)

    blk = pl.BlockSpec((CHUNK, dm), lambda n: (n, 0))
    return dict(
        body=body, grid=(nblk,), name="attn_fwd", args=[qkv, qkv, qkv, x2, bias, sink, gath],
        out_shape=[jax.ShapeDtypeStruct((t, dm), F32), jax.ShapeDtypeStruct((t, dm), BF16),
                   jax.ShapeDtypeStruct((nblk,) + slab, BF16), jax.ShapeDtypeStruct((t, 128), F32)],
        in_specs=[blk, pl.BlockSpec((t, kvw), lambda n: (0, kcb)), pl.BlockSpec((t, kvw), lambda n: (0, vcb)), blk,
                  _full(bias.shape), pl.BlockSpec(memory_space=pltpu.SMEM), ANY],
        out_specs=[blk, blk, pl.BlockSpec((None,) + slab, lambda n: (n, 0, 0, 0, 0)),
                   pl.BlockSpec((CHUNK, 128), lambda n: (n, 0))],
        scratch=[pltpu.VMEM((gath.shape[1] * NDEV, dm), BF16), pltpu.VMEM(slab, F32),
                 pltpu.SemaphoreType.DMA((NDEV,))])


def _attn_bwd(qkv, att, probs, sink_probs, dx3, bias_shape, gath):
    t, dm = dx3.shape
    nblk = t // CHUNK
    kvw = N_KV * HEAD_DIM
    kcb, vcb = dm // kvw, dm // kvw + 1
    scale = HEAD_DIM ** -0.5
    slab = (N_KV, 3, 4 * CHUNK, CHUNK)

    def body(q_ref, k_ref, v_ref, att_ref, p_ref, ps_ref, dx_ref, gath_ref,
             dq_ref, dkb_ref, dvb_ref, dbias_ref, dsink_ref,
             wbuf, dp_scr, ds_scr, prod_scr, dsum_scr, dk_ref, dv_ref, sems):
        n = pl.program_id(0)

        @pl.when(n == 0)
        def _():
            _load_weight(gath_ref, wbuf, sems)
            dk_ref[...] = jnp.zeros_like(dk_ref)
            dv_ref[...] = jnp.zeros_like(dv_ref)
            dbias_ref[...] = jnp.zeros_like(dbias_ref)
            dsink_ref[...] = jnp.zeros_like(dsink_ref)
        lo, hi = _half_masks()
        lane_s = lax.broadcasted_iota(jnp.int32, (ATT_SLAB, 128), 1)
        starts = _key_block_starts(n, nblk)
        dout = _dot(dx_ref[...].astype(BF16), wbuf[...], NT)
        prod_scr[...] = dout * att_ref[...].astype(F32)
        doutb = dout.astype(BF16)
        dq_tiles = []
        for hk in range(N_KV):
            kt = (hk // 2) * 128
            c0 = (2 * hk) * 128
            k_lo = _kv_low(k_ref, starts, hk, lo)
            v_lo = _kv_low(v_ref, starts, hk, lo)
            q4 = _stack_heads(q_ref[:, c0:c0 + 128], q_ref[:, c0 + 128:c0 + 256])
            do4 = _stack_heads(doutb[:, c0:c0 + 128], doutb[:, c0 + 128:c0 + 256])
            for jj in range(3):
                dp_scr[hk, jj] = _dot(do4, v_lo[jj], NT)
            for g in range(4):
                h = 4 * hk + g
                for r0 in range(0, CHUNK, ATT_SLAB):
                    rows = slice(g * CHUNK + r0, g * CHUNK + r0 + ATT_SLAB)
                    pt = prod_scr[r0:r0 + ATT_SLAB, c0 + (g // 2) * 128:c0 + (g // 2 + 1) * 128]
                    msk = lane_s < HEAD_DIM if g % 2 == 0 else lane_s >= HEAD_DIM
                    dsum = jnp.sum(jnp.where(msk, pt, 0.0), axis=-1, keepdims=True)
                    rest = jnp.zeros((ATT_SLAB, 128), F32) if h == 0 else dsum_scr[r0:r0 + ATT_SLAB, :]
                    dsum_scr[r0:r0 + ATT_SLAB, :] = jnp.where(lane_s == h, dsum, rest)
                    for jj in range(3):
                        ds = p_ref[hk, jj, rows, :].astype(F32) * (dp_scr[hk, jj, rows, :] - dsum)
                        dbias_ref[h, r0:r0 + ATT_SLAB, jj * CHUNK:(jj + 1) * CHUNK] += ds
                        ds_scr[hk, jj, rows, :] = ds.astype(BF16)
            dq4 = jnp.zeros((4 * CHUNK, 128), F32)
            for jj in range(3):
                ds4 = ds_scr[hk, jj]
                dq4 = dq4 + _dot(ds4, k_lo[jj], NN) * scale
                dkj = _dot(ds4, q4, TN) * scale
                dvj = _dot(p_ref[hk, jj], do4, TN)
                if hk % 2 == 1:
                    dkj, dvj = pltpu.roll(dkj, HEAD_DIM, 1), pltpu.roll(dvj, HEAD_DIM, 1)
                keep = lo if hk % 2 == 0 else hi
                dk_ref[pl.ds(starts[jj], CHUNK), kt:kt + 128] += jnp.where(keep, dkj, 0.0)
                dv_ref[pl.ds(starts[jj], CHUNK), kt:kt + 128] += jnp.where(keep, dvj, 0.0)
            dq_tiles += list(_unstack_heads(dq4))
        dq_ref[...] = jnp.concatenate(dq_tiles, axis=1).astype(BF16)
        dsink_ref[...] -= jnp.sum(ps_ref[...] * dsum_scr[...], axis=0, keepdims=True)

        @pl.when(n == nblk - 1)
        def _():
            dkb_ref[...] = dk_ref[...].astype(BF16)
            dvb_ref[...] = dv_ref[...].astype(BF16)

    blk = pl.BlockSpec((CHUNK, dm), lambda n: (n, 0))
    return dict(
        body=body, grid=(nblk,), name="attn_bwd", args=[qkv, qkv, qkv, att, probs, sink_probs, dx3, gath],
        out_shape=[jax.ShapeDtypeStruct((t, dm), BF16), jax.ShapeDtypeStruct((t, kvw), BF16),
                   jax.ShapeDtypeStruct((t, kvw), BF16), jax.ShapeDtypeStruct(bias_shape, F32),
                   jax.ShapeDtypeStruct((1, 128), F32)],
        in_specs=[blk, pl.BlockSpec((t, kvw), lambda n: (0, kcb)), pl.BlockSpec((t, kvw), lambda n: (0, vcb)),
                  blk, pl.BlockSpec((None,) + slab, lambda n: (n, 0, 0, 0, 0)),
                  pl.BlockSpec((CHUNK, 128), lambda n: (n, 0)), blk, ANY],
        out_specs=[blk, _full((t, kvw)), _full((t, kvw)), _full(bias_shape), _full((1, 128))],
        scratch=[pltpu.VMEM((gath.shape[1] * NDEV, dm), BF16), pltpu.VMEM(slab, F32), pltpu.VMEM(slab, BF16),
                 pltpu.VMEM((CHUNK, dm), F32), pltpu.VMEM((CHUNK, 128), F32),
                 pltpu.VMEM((t, kvw), F32), pltpu.VMEM((t, kvw), F32), pltpu.SemaphoreType.DMA((NDEV,))])


def _finish_weight(recvs, w, m, v, name):
    nl, r, dm = w.shape
    assert nl == len(recvs) and all(rc.shape[1:] == (r, dm) for rc in recvs)
    td = dm // 2
    wspec = pl.BlockSpec((None, r, td), lambda l, j: (l, 0, j))

    def body(*refs):
        r_refs = refs[:nl]
        w_ref, m_ref, v_ref, g_ref, d_ref, nm_ref, nv_ref = refs[nl:]
        layer = pl.program_id(0)
        for li in range(nl):
            @pl.when(layer == li)
            def _():
                g = r_refs[li][0].astype(F32)
                for d in range(1, recvs[li].shape[0]):
                    g = g + r_refs[li][d].astype(F32)
                delta, nm, nv = _adamw_math(w_ref[...], g, m_ref[...], v_ref[...])
                g_ref[...] = g
                d_ref[...] = delta
                nm_ref[...] = nm
                nv_ref[...] = nv

    o = jax.ShapeDtypeStruct(w.shape, F32)
    return dict(
        body=body, grid=(nl, 2), name=name, args=[*recvs, w, m, v], out_shape=[o, o, o, o],
        in_specs=[pl.BlockSpec((rc.shape[0], r, td), lambda l, j: (0, 0, j)) for rc in recvs] + [wspec] * 3,
        out_specs=[wspec] * 4)


def _adamw_small(ws, ms, vs, slots, late_slots, loss_slots, name):
    n = len(ws)

    def total(ref):
        acc = ref[0].astype(F32)
        for d in range(1, NDEV):
            acc = acc + ref[d].astype(F32)
        return acc

    def body(*refs):
        ins, outs = refs[:4 * n + 2], refs[4 * n + 2:]
        for i in range(n):
            w_ref, m_ref, v_ref, s_ref = ins[4 * i:4 * i + 4]
            g_ref, d_ref, nm_ref, nv_ref = outs[4 * i:4 * i + 4]
            g_ref[...] = total(s_ref)
            if i == 0:
                g_ref[0:1, :] = total(ins[4 * n])
            d_ref[...], nm_ref[...], nv_ref[...] = _adamw_math(w_ref[...], g_ref[...], m_ref[...], v_ref[...])
        outs[4 * n][...] = total(ins[4 * n + 1])

    args, out_shape = [], []
    for w, m, v, s in zip(ws, ms, vs, slots):
        args += [w, m, v, s]
        out_shape += [jax.ShapeDtypeStruct(w.shape, F32)] * 4
    args += [late_slots, loss_slots]
    out_shape.append(jax.ShapeDtypeStruct((1, 1), F32))
    out = pl.pallas_call(
        body, grid=(1,), out_shape=tuple(out_shape), in_specs=[_full(a.shape) for a in args],
        out_specs=tuple(_full(o.shape) for o in out_shape), compiler_params=_cp(), name=name)(*args)
    return [tuple(out[4 * i:4 * i + 4]) for i in range(n)], out[4 * n]


def kernel(x, norm_mix, norm_ffn, even_w_in, even_v_ln_g, even_v_ln_b, even_w_spatial, even_b_spatial, even_conv_w, even_w_out, attn_w_qkv, attn_sink, rel_bias, attn_w_out, ffn_w_gate, ffn_w_up, ffn_w_down, final_norm, loss_target, m_norm_mix, m_norm_ffn, m_even_w_in, m_even_v_ln_g, m_even_v_ln_b, m_even_w_spatial, m_even_b_spatial, m_even_conv_w, m_even_w_out, m_attn_w_qkv, m_attn_sink, m_rel_bias, m_attn_w_out, m_ffn_w_gate, m_ffn_w_up, m_ffn_w_down, m_final_norm, v_norm_mix, v_norm_ffn, v_even_w_in, v_even_v_ln_g, v_even_v_ln_b, v_even_w_spatial, v_even_b_spatial, v_even_conv_w, v_even_w_out, v_attn_w_qkv, v_attn_sink, v_rel_bias, v_attn_w_out, v_ffn_w_gate, v_ffn_w_up, v_ffn_w_down, v_final_norm):
    t, dm = x.shape[1], x.shape[2]
    aw = even_v_ln_g.shape[1]
    bw = even_conv_w.shape[2] * NDEV
    gd = aw // A_GROUPS
    tm = min(512, t // 2)
    tmf = min(256, t // 2)
    me = _my_index()
    row = lambda a: a.reshape(1, -1)

    colT = lambda w: w.T.astype(BF16)
    sh = dict(winT=colT(even_w_in[0]), wqkvT=colT(attn_w_qkv[0]), wgT0=colT(ffn_w_gate[0]), wuT0=colT(ffn_w_up[0]),
              wgT1=colT(ffn_w_gate[1]), wuT1=colT(ffn_w_up[1]), woe=even_w_out[0].astype(BF16),
              woa=attn_w_out[0].astype(BF16), wd0=ffn_w_down[0].astype(BF16), wd1=ffn_w_down[1].astype(BF16))
    gather = lambda names: _GatherCarry([sh[n] for n in names])

    in_full = lambda a: lax.dynamic_update_slice(jnp.zeros((3, bw), F32), a[0], (0, me * (bw // NDEV)))

    x0 = x[0]
    wsp_b = even_w_spatial[0].astype(BF16)
    bspb = jnp.broadcast_to(even_b_spatial[0][:, :, None], (A_GROUPS, CHUNK, gd))
    buckets = _bucket_table()
    sink = attn_sink[0]

    (bias,), ((g_winT,), (cw_slots,)) = _call(
        _bias_table(rel_bias.T, buckets), [gather(["winT"]), _BroadcastCarry([in_full(even_conv_w)])])
    cw_full = jnp.sum(cw_slots, axis=0)
    (proj, h0b), (g_woe, g_wgT0) = _call(_norm_proj(x0, row(norm_mix[0]), g_winT, F32, "in_proj", tm),
                                         gather(["woe", "wgT0"]))
    (x1, yb), (g_wuT0,) = _call(_even_core_fwd(proj, x0, even_v_ln_g, even_v_ln_b, wsp_b, bspb, cw_full, g_woe, tm),
                                gather(["wuT0"]))
    (gate0, up0, act0), (g_wd0,) = _call(
        _ffn_up(x1, row(norm_ffn[0]), g_wgT0, g_wuT0, "ffn_up0", tmf), gather(["wd0"]), late=True)
    (x2,), (g_wqkvT,) = _call(_ffn_down(x1, act0, g_wd0, "ffn_down0", tm), gather(["wqkvT"]), late=True)
    (qkv, h2b), (g_woa,) = _call(
        _norm_proj(x2, row(norm_mix[1]), g_wqkvT, BF16, "qkv_proj", tm), gather(["woa"]), late=True)
    (x3, attb, probs, sink_probs), (g_wgT1, g_wuT1) = _call(
        _attn_fwd(qkv, x2, bias, sink, g_woa), gather(["wgT1", "wuT1"]), late=True)
    (gate1, up1, act1), (g_wd1,) = _call(
        _ffn_up(x3, row(norm_ffn[1]), g_wgT1, g_wuT1, "ffn_up1", tmf), gather(["wd1"]), late=True)
    (loss_part, dx4, dx4b, d_final), _ = _call(
        _ffn_down_loss(x3, act1, g_wd1, loss_target[0], row(final_norm), "ffn_down1_loss", tm))

    (dx3, dx3b, dg1, du1, h3b, d_nffn1), _ = _call(
        _ffn_bwd(dx4, x3, gate1, up1, row(norm_ffn[1]), g_wgT1, g_wuT1, g_wd1, "ffn_bwd1", tmf))
    (p_wgT1,), _ = _call(_wgrad(dg1, h3b, "wgrad_gate1"))
    (p_wuT1,), _ = _call(_wgrad(du1, h3b, "wgrad_up1"))
    (p_wd1,), ((a_wgT1,), (a_wuT1,)) = _call(
        _wgrad(act1, dx4b, "wgrad_down1"), [_PairCarry(p_wgT1), _PairCarry(p_wuT1)])
    (dq, dk, dv, dbias, dsink), ((r_wgT1,), (a_wd1,)) = _call(
        _attn_bwd(qkv, attb, probs, sink_probs, dx3, bias.shape, g_woa),
        [_ChipSumCarry(p_wgT1, a_wgT1), _PairCarry(p_wd1)])
    (p_woa,), _ = _call(_wgrad(attb, dx3b, "wgrad_attn_out"))
    d_relb = _rel_bias_grad(dbias, buckets)[:, 0:N_BUCKETS].T
    (dx2, dx2b, d_nmix1), (r_wuT1,) = _call(
        _proj_bwd_norm([dq, dk, dv], x2, row(norm_mix[1]), dx3, g_wqkvT, "qkv_bwd", tm),
        _ChipSumCarry(p_wuT1, a_wuT1))
    (p_wqkvT,), _ = _call(_wgrad([dq, dk, dv], h2b, "wgrad_qkv"))
    (dx1, dx1b, dg0, du0, h1b, d_nffn0), ((r_wd1,), (r_woa, r_wqkvT)) = _call(
        _ffn_bwd(dx2, x1, gate0, up0, row(norm_ffn[0]), g_wgT0, g_wuT0, g_wd0, "ffn_bwd0", tmf),
        [_ChipSumCarry(p_wd1, a_wd1), _GradCarry([p_woa, p_wqkvT])])
    (p_wgT0,), _ = _call(_wgrad(dg0, h1b, "wgrad_gate0"))
    (p_wuT0,), (a_wgT0,) = _call(_wgrad(du0, h1b, "wgrad_up0"), _PairCarry(p_wgT0))
    (p_wd0,), ((r_wgT0,), (a_wuT0,)) = _call(
        _wgrad(act0, dx2b, "wgrad_down0"), [_ChipSumCarry(p_wgT0, a_wgT0), _PairCarry(p_wuT0)])
    (dproj, d_lng, d_lnb, d_wsp, d_bsp3, d_cw), ((r_wuT0,), (a_wd0,)) = _call(
        _even_core_bwd(proj, dx1, even_v_ln_g, even_v_ln_b, wsp_b, bspb, cw_full, g_woe, tm),
        [_ChipSumCarry(p_wuT0, a_wuT0), _PairCarry(p_wd0)])
    small_names = ["norm_mix", "norm_ffn", "even_v_ln_g", "even_v_ln_b", "even_w_spatial", "even_b_spatial",
                   "even_conv_w", "attn_sink", "rel_bias", "final_norm"]
    small_parts = [jnp.concatenate([jnp.zeros_like(d_nmix1), d_nmix1]), jnp.concatenate([d_nffn0, d_nffn1]),
                   d_lng, d_lnb, d_wsp[None].astype(BF16), jnp.sum(d_bsp3, axis=-1)[None], d_cw,
                   dsink[:, 0:N_HEADS], d_relb, d_final, loss_part]
    (p_winT,), (r_wd0,) = _call(_wgrad(dproj, h0b, "wgrad_in"), _ChipSumCarry(p_wd0, a_wd0))
    (p_woe,), ((a_winT,), small_slots) = _call(
        _wgrad(yb, dx1b, "wgrad_even_out"), [_PairCarry(p_winT), _BroadcastCarry(small_parts)])
    (dx0, _, d_nmix0), ((r_winT,), (r_woe,)) = _call(
        _proj_bwd_norm([dproj], x0, row(norm_mix[0]), dx1, g_winT, "in_proj_bwd", tm),
        [_ChipSumCarry(p_winT, a_winT), _GradCarry([p_woe])])

    grads = {}

    order = ["norm_mix", "norm_ffn", "even_w_in", "even_v_ln_g", "even_v_ln_b", "even_w_spatial", "even_b_spatial",
             "even_conv_w", "even_w_out", "attn_w_qkv", "attn_sink", "rel_bias", "attn_w_out", "ffn_w_gate",
             "ffn_w_up", "ffn_w_down", "final_norm"]
    ws = dict(norm_mix=norm_mix, norm_ffn=norm_ffn, even_w_in=even_w_in, even_v_ln_g=even_v_ln_g,
              even_v_ln_b=even_v_ln_b, even_w_spatial=even_w_spatial, even_b_spatial=even_b_spatial,
              even_conv_w=even_conv_w, even_w_out=even_w_out, attn_w_qkv=attn_w_qkv, attn_sink=attn_sink,
              rel_bias=rel_bias, attn_w_out=attn_w_out, ffn_w_gate=ffn_w_gate, ffn_w_up=ffn_w_up,
              ffn_w_down=ffn_w_down, final_norm=final_norm)
    ms = dict(norm_mix=m_norm_mix, norm_ffn=m_norm_ffn, even_w_in=m_even_w_in, even_v_ln_g=m_even_v_ln_g,
              even_v_ln_b=m_even_v_ln_b, even_w_spatial=m_even_w_spatial, even_b_spatial=m_even_b_spatial,
              even_conv_w=m_even_conv_w, even_w_out=m_even_w_out, attn_w_qkv=m_attn_w_qkv, attn_sink=m_attn_sink,
              rel_bias=m_rel_bias, attn_w_out=m_attn_w_out, ffn_w_gate=m_ffn_w_gate, ffn_w_up=m_ffn_w_up,
              ffn_w_down=m_ffn_w_down, final_norm=m_final_norm)
    vs = dict(norm_mix=v_norm_mix, norm_ffn=v_norm_ffn, even_w_in=v_even_w_in, even_v_ln_g=v_even_v_ln_g,
              even_v_ln_b=v_even_v_ln_b, even_w_spatial=v_even_w_spatial, even_b_spatial=v_even_b_spatial,
              even_conv_w=v_even_conv_w, even_w_out=v_even_w_out, attn_w_qkv=v_attn_w_qkv, attn_sink=v_attn_sink,
              rel_bias=v_rel_bias, attn_w_out=v_attn_w_out, ffn_w_gate=v_ffn_w_gate, ffn_w_up=v_ffn_w_up,
              ffn_w_down=v_ffn_w_down, final_norm=v_final_norm)
    big = dict(ffn_w_gate=([r_wgT0, r_wgT1], True), even_w_in=([r_winT], True), even_w_out=([r_woe], False),
               attn_w_qkv=([r_wqkvT], True), attn_w_out=([r_woa], False), ffn_w_up=([r_wuT0, r_wuT1], True),
               ffn_w_down=([r_wd0, r_wd1], False))
    delta, new_m, new_v = {}, {}, {}
    late_slots = None
    for n, (recvs, transposed) in big.items():
        lay = (lambda a: jnp.swapaxes(a, 1, 2)) if transposed else (lambda a: a)
        spec = _finish_weight(recvs, lay(ws[n]), lay(ms[n]), lay(vs[n]), "finish_" + n)
        if late_slots is None:
            outs, (late_slots,) = _call(spec, _BroadcastCarry([d_nmix0]))
        else:
            outs, _ = _call(spec)
        grads[n], delta[n], new_m[n], new_v[n] = [lay(o) for o in outs]
    shaped = lambda n, a: in_full(a) if n == "even_conv_w" else (a.reshape(1, dm) if n == "final_norm" else a)
    pick = lambda dct: [shaped(n, dct[n]) for n in small_names]
    results, loss11 = _adamw_small(pick(ws), pick(ms), pick(vs), small_slots[:-1], late_slots, small_slots[-1],
                                   "adamw_small")
    mine = lambda a: lax.dynamic_slice(a, (0, me * (bw // NDEV)), (3, bw // NDEV))[None]
    for n, res in zip(small_names, results):
        for dst, a in zip((grads, delta, new_m, new_v), res):
            dst[n] = mine(a) if n == "even_conv_w" else (a.reshape(dm) if n == "final_norm" else a)
    loss = loss11[0, 0]
    return (loss, dx0[None], *[grads[n] for n in order], *[delta[n] for n in order],
            *[new_m[n] for n in order], *[new_v[n] for n in order])
```

```python
import math

import jax
import jax.numpy as jnp
import numpy as np
from jax import lax
from jax.experimental import pallas as pl
from jax.experimental.pallas import tpu as pltpu

F32, BF16 = jnp.float32, jnp.bfloat16
NDEV = 8
EPS = 1e-6
CHUNK = 128
A_GROUPS = 4
N_HEADS, N_KV, HEAD_DIM = 16, 4, 64
N_BUCKETS, MAX_DISTANCE = 32, 128
NEG = -1e30
LOG2E = 1.4426950408889634
ADAM_LR, ADAM_B1, ADAM_B2, ADAM_EPS, ADAM_WD, ADAM_STEP = 0.001, 0.9, 0.999, 1e-08, 0.01, 10
VMEM_LIMIT = 56 * 1024 * 1024
MESH = pl.DeviceIdType.MESH
NT = (((1,), (1,)), ((), ()))
NN = (((1,), (0,)), ((), ()))
TN = (((0,), (0,)), ((), ()))
ANY = pl.BlockSpec(memory_space=pl.ANY)


def _cp(n_grid=1):
    return pltpu.CompilerParams(dimension_semantics=("arbitrary",) * n_grid, vmem_limit_bytes=VMEM_LIMIT)


def _dot(a, b, dims):
    return lax.dot_general(a, b, dims, preferred_element_type=F32)


def _my_index():
    return 4 * lax.axis_index("x") + 2 * lax.axis_index("y") + lax.axis_index("c")


def _peer(k):
    x, y, c = lax.axis_index("x"), lax.axis_index("y"), lax.axis_index("c")
    px = 1 - x if k & 4 else x
    py = 1 - y if k & 2 else y
    pc = 1 - c if k & 1 else c
    return (px, py, pc)


def _load_weight(gath_ref, wbuf, sems):
    rows = gath_ref.shape[1]
    cps = [pltpu.make_async_copy(gath_ref.at[d], wbuf.at[pl.ds(d * rows, rows), :], sems.at[d]) for d in range(NDEV)]
    for c in cps:
        c.start()
    for c in cps:
        c.wait()


class _GatherCarry:
    def __init__(self, pieces):
        self.inputs = list(pieces)
        self.n = len(pieces)
        self.out_shape = [jax.ShapeDtypeStruct((NDEV,) + p.shape, p.dtype) for p in pieces]
        self.scratch = [pltpu.SemaphoreType.DMA((7 * self.n,)), pltpu.SemaphoreType.DMA((7 * self.n,)),
                        pltpu.SemaphoreType.DMA((self.n,))]

    def _ctx(self):
        x, y, c = lax.axis_index("x"), lax.axis_index("y"), lax.axis_index("c")
        chips = [(1 - x, y), (x, 1 - y), (1 - x, 1 - y)]
        return (x, y, c), (x, y, 1 - c), chips, c

    def _copy(self, k, j, block, to, ins, outs, sems, src=None):
        send_sems, recv_sems, _ = sems
        slot = outs[j].at[4 * block[0] + 2 * block[1] + block[2]]
        return pltpu.make_async_remote_copy(
            src_ref=slot if src is None else src, dst_ref=slot, send_sem=send_sems.at[k * self.n + j],
            recv_sem=recv_sems.at[k * self.n + j], device_id=to, device_id_type=MESH)

    def start(self, ins, outs, sems):
        me, sibling, chips, c = self._ctx()
        for j in range(self.n):
            pltpu.make_async_copy(ins[j], outs[j].at[4 * me[0] + 2 * me[1] + me[2]], sems[2].at[j]).start()
            self._copy(0, j, me, sibling, ins, outs, sems, src=ins[j]).start()
            for q, chip in enumerate(chips):
                self._copy(1 + q, j, me, (*chip, c), ins, outs, sems, src=ins[j]).start()

    def mid(self, ins, outs, sems):
        me, sibling, chips, c = self._ctx()
        for q, chip in enumerate(chips):
            for j in range(self.n):
                self._copy(1 + q, j, (*chip, c), me, ins, outs, sems).wait_recv()
                self._copy(4 + q, j, (*chip, c), sibling, ins, outs, sems).start()

    def finish(self, ins, outs, sems):
        me, sibling, chips, c = self._ctx()
        for j in range(self.n):
            self._copy(0, j, sibling, me, ins, outs, sems).wait_recv()
            for q, chip in enumerate(chips):
                self._copy(4 + q, j, (*chip, 1 - c), me, ins, outs, sems).wait_recv()
        for j in range(self.n):
            self._copy(0, j, me, sibling, ins, outs, sems, src=ins[j]).wait_send()
            for q, chip in enumerate(chips):
                self._copy(1 + q, j, me, (*chip, c), ins, outs, sems, src=ins[j]).wait_send()
                self._copy(4 + q, j, (*chip, c), sibling, ins, outs, sems).wait_send()
            pltpu.make_async_copy(ins[j], outs[j].at[0], sems[2].at[j]).wait()


class _GradCarry:
    def __init__(self, pieces):
        self.inputs = list(pieces)
        self.n = len(pieces)
        self.rows = [p.shape[0] // NDEV for p in pieces]
        self.out_shape = [jax.ShapeDtypeStruct((NDEV, r, p.shape[1]), p.dtype) for p, r in zip(pieces, self.rows)]
        self.scratch = [pltpu.SemaphoreType.DMA((7 * self.n,)), pltpu.SemaphoreType.DMA((7 * self.n,)),
                        pltpu.SemaphoreType.DMA((self.n,))]

    def _copies(self, ins, outs, sems):
        me = _my_index()
        local, remote = [], []
        for j in range(self.n):
            r = self.rows[j]
            local.append(pltpu.make_async_copy(ins[j].at[pl.ds(pl.multiple_of(me * r, 16), r), :], outs[j].at[me],
                                               sems[2].at[j]))
            for k in range(1, NDEV):
                peer = _peer(k)
                pidx = 4 * peer[0] + 2 * peer[1] + peer[2]
                remote.append(pltpu.make_async_remote_copy(
                    src_ref=ins[j].at[pl.ds(pl.multiple_of(pidx * r, 16), r), :], dst_ref=outs[j].at[me],
                    send_sem=sems[0].at[(k - 1) * self.n + j], recv_sem=sems[1].at[(k - 1) * self.n + j],
                    device_id=peer, device_id_type=MESH))
        return local, remote

    def start(self, ins, outs, sems):
        local, remote = self._copies(ins, outs, sems)
        for cp in local + remote:
            cp.start()

    def mid(self, ins, outs, sems):
        pass

    def finish(self, ins, outs, sems):
        local, remote = self._copies(ins, outs, sems)
        for cp in remote + local:
            cp.wait()


class _BroadcastCarry:
    def __init__(self, parts):
        self.inputs = list(parts)
        self.n = len(self.inputs)
        self.out_shape = [jax.ShapeDtypeStruct((NDEV,) + p.shape, p.dtype) for p in self.inputs]
        self.scratch = [pltpu.SemaphoreType.DMA((7 * self.n,)), pltpu.SemaphoreType.DMA((7 * self.n,)),
                        pltpu.SemaphoreType.DMA((self.n,))]

    def _copies(self, ins, outs, sems):
        me = _my_index()
        cps = []
        for j in range(self.n):
            cps.append(pltpu.make_async_copy(ins[j], outs[j].at[me], sems[2].at[j]))
            cps += [pltpu.make_async_remote_copy(
                src_ref=ins[j], dst_ref=outs[j].at[me], send_sem=sems[0].at[(k - 1) * self.n + j],
                recv_sem=sems[1].at[(k - 1) * self.n + j], device_id=_peer(k), device_id_type=MESH)
                for k in range(1, NDEV)]
        return cps

    def start(self, ins, outs, sems):
        for cp in self._copies(ins, outs, sems):
            cp.start()

    def mid(self, ins, outs, sems):
        pass

    def finish(self, ins, outs, sems):
        for cp in self._copies(ins, outs, sems):
            cp.wait()


class _PairCarry:
    def __init__(self, piece):
        self.inputs = [piece]
        self.r = piece.shape[0] // NDEV
        self.out_shape = [jax.ShapeDtypeStruct((4, self.r, piece.shape[1]), piece.dtype)]
        self.scratch = [pltpu.SemaphoreType.DMA((4,)), pltpu.SemaphoreType.DMA((4,))]

    def _copies(self, ins, outs, sems):
        x, y, c = lax.axis_index("x"), lax.axis_index("y"), lax.axis_index("c")
        return [pltpu.make_async_remote_copy(
            src_ref=ins[0].at[pl.ds(pl.multiple_of((2 * q + 1 - c) * self.r, 16), self.r), :], dst_ref=outs[0].at[q],
            send_sem=sems[0].at[q], recv_sem=sems[1].at[q], device_id=(x, y, 1 - c), device_id_type=MESH)
            for q in range(4)]

    def start(self, ins, outs, sems):
        for cp in self._copies(ins, outs, sems):
            cp.start()

    def mid(self, ins, outs, sems):
        pass

    def finish(self, ins, outs, sems):
        for cp in self._copies(ins, outs, sems):
            cp.wait()


class _ChipSumCarry:
    def __init__(self, piece, landed):
        self.inputs = [piece, landed]
        self.r, dm = piece.shape[0] // NDEV, piece.shape[1]
        self.out_shape = [jax.ShapeDtypeStruct((4, self.r, dm), piece.dtype)]
        self.scratch = [pltpu.VMEM((4, self.r, dm), piece.dtype), pltpu.VMEM((8, self.r, dm), piece.dtype),
                        pltpu.SemaphoreType.DMA((8,)), pltpu.SemaphoreType.DMA((3,)), pltpu.SemaphoreType.DMA((3,)),
                        pltpu.SemaphoreType.DMA(())]

    def _copies(self, outs, scr):
        sums, _, _, send_sems, recv_sems, local_sem = scr
        x, y, c = lax.axis_index("x"), lax.axis_index("y"), lax.axis_index("c")
        mine = 2 * x + y
        local = pltpu.make_async_copy(sums.at[mine], outs[0].at[mine], local_sem)
        remote = []
        for k in range(1, 4):
            px = 1 - x if k & 2 else x
            py = 1 - y if k & 1 else y
            remote.append(pltpu.make_async_remote_copy(
                src_ref=sums.at[2 * px + py], dst_ref=outs[0].at[mine], send_sem=send_sems.at[k - 1],
                recv_sem=recv_sems.at[k - 1], device_id=(px, py, c), device_id_type=MESH))
        return local, remote

    def start(self, ins, outs, scr):
        sums, stage, stage_sems = scr[0], scr[1], scr[2]
        c = lax.axis_index("c")
        loads = []
        for q in range(4):
            loads.append((
                pltpu.make_async_copy(ins[0].at[pl.ds(pl.multiple_of((2 * q + c) * self.r, 16), self.r), :],
                                      stage.at[2 * q], stage_sems.at[2 * q]),
                pltpu.make_async_copy(ins[1].at[q], stage.at[2 * q + 1], stage_sems.at[2 * q + 1])))
        for a, b in loads:
            a.start()
            b.start()
        for q, (a, b) in enumerate(loads):
            a.wait()
            b.wait()
            sums[q] = (stage[2 * q].astype(F32) + stage[2 * q + 1].astype(F32)).astype(sums.dtype)
        local, remote = self._copies(outs, scr)
        for cp in [local] + remote:
            cp.start()

    def mid(self, ins, outs, scr):
        pass

    def finish(self, ins, outs, scr):
        local, remote = self._copies(outs, scr)
        for cp in remote + [local]:
            cp.wait()


def _call(spec, carry=None):
    body, grid = spec["body"], spec["grid"]
    in_specs, out_specs, out_shape = list(spec["in_specs"]), list(spec["out_specs"]), list(spec["out_shape"])
    scratch, args = list(spec.get("scratch", [])), list(spec["args"])
    if carry is None:
        out = pl.pallas_call(body, grid=grid, in_specs=in_specs, out_specs=tuple(out_specs),
                             out_shape=tuple(out_shape), scratch_shapes=scratch, compiler_params=_cp(len(grid)),
                             name=spec["name"])(*args)
        return tuple(out), ()
    carries = list(carry) if isinstance(carry, (list, tuple)) else [carry]
    n_in, n_out, n_s = len(in_specs), len(out_specs), len(scratch)
    steps = int(np.prod(grid))

    def split(refs, counts):
        parts, o = [], 0
        for cnt in counts:
            parts.append(refs[o:o + cnt])
            o += cnt
        return parts

    c_in = [len(cr.inputs) for cr in carries]
    c_out = [len(cr.out_shape) for cr in carries]
    c_scr = [len(cr.scratch) for cr in carries]

    def wrapped(*refs):
        ins, cins, outs, couts, scr, cscr = split(refs, [n_in, sum(c_in), n_out, sum(c_out), n_s, sum(c_scr)])
        per = list(zip(carries, split(cins, c_in), split(couts, c_out), split(cscr, c_scr)))
        step = pl.program_id(0)
        for ax in range(1, len(grid)):
            step = step * grid[ax] + pl.program_id(ax)

        @pl.when(step == 0)
        def _():
            for cr, ci, co, cs in per:
                cr.start(ci, co, cs)
        if steps >= 3:
            @pl.when(step == steps - 2)
            def _():
                for cr, ci, co, cs in per:
                    cr.mid(ci, co, cs)
        body(*ins, *outs, *scr)

        @pl.when(step == steps - 1)
        def _():
            for cr, ci, co, cs in per:
                if steps < 3:
                    cr.mid(ci, co, cs)
                cr.finish(ci, co, cs)

    out = pl.pallas_call(
        wrapped, grid=grid, in_specs=in_specs + [ANY] * sum(c_in), out_specs=tuple(out_specs + [ANY] * sum(c_out)),
        out_shape=tuple(out_shape + [s for cr in carries for s in cr.out_shape]),
        scratch_shapes=scratch + [s for cr in carries for s in cr.scratch],
        compiler_params=_cp(len(grid)), name=spec["name"])(*args, *[a for cr in carries for a in cr.inputs])
    c_res = [tuple(p) for p in split(out[n_out:], c_out)]
    return tuple(out[:n_out]), (c_res if isinstance(carry, (list, tuple)) else c_res[0])


def _rms_fwd(x, gain):
    r = lax.rsqrt(jnp.mean(x * x, axis=-1, keepdims=True) + EPS)
    return x * r * gain, r


def _rms_bwd(dh, x, r, gain):
    a = dh * gain
    dx = r * a - x * (r * r * r) * jnp.mean(a * x, axis=-1, keepdims=True)
    dgain = jnp.sum(dh * (x * r), axis=0, keepdims=True)
    return dx, dgain


def _gelu(x):
    return 0.5 * x * (1.0 + lax.erf(x * 0.7071067811865476))


def _gelu_grad(x):
    return 0.5 * (1.0 + lax.erf(x * 0.7071067811865476)) + x * jnp.exp(-0.5 * x * x) * 0.3989422804014327


def _sigmoid(x):
    return 1.0 / (1.0 + jnp.exp(-x))


def _adamw_math(w, g, m, v):
    nm = ADAM_B1 * m + (1.0 - ADAM_B1) * g
    nv = ADAM_B2 * v + (1.0 - ADAM_B2) * (g * g)
    m_hat = nm / (1.0 - ADAM_B1 ** ADAM_STEP)
    v_hat = nv / (1.0 - ADAM_B2 ** ADAM_STEP)
    return -ADAM_LR * (m_hat / (jnp.sqrt(v_hat) + ADAM_EPS) + ADAM_WD * w), nm, nv


def _tok(tm, w):
    return pl.BlockSpec((tm, w), lambda i: (i, 0))


def _full(shape):
    return pl.BlockSpec(shape, lambda *i: (0,) * len(shape))


def _norm_proj(x, gain, gath, out_dtype, name, tm):
    t, dm = x.shape
    n = gath.shape[1] * NDEV

    def body(x_ref, g_ref, gath_ref, proj_ref, hb_ref, wbuf, sems):
        @pl.when(pl.program_id(0) == 0)
        def _():
            _load_weight(gath_ref, wbuf, sems)
        h, _ = _rms_fwd(x_ref[...], g_ref[...])
        hb = h.astype(BF16)
        hb_ref[...] = hb
        proj_ref[...] = _dot(hb, wbuf[...], NT).astype(out_dtype)

    return dict(
        body=body, grid=(t // tm,), name=name, args=[x, gain, gath],
        out_shape=[jax.ShapeDtypeStruct((t, n), out_dtype), jax.ShapeDtypeStruct((t, dm), BF16)],
        in_specs=[_tok(tm, dm), _full((1, dm)), ANY], out_specs=[_tok(tm, n), _tok(tm, dm)],
        scratch=[pltpu.VMEM((n, dm), BF16), pltpu.SemaphoreType.DMA((NDEV,))])


def _proj_bwd_norm(dys, x, gain, dres, gath, name, tm):
    t, dm = x.shape
    n = gath.shape[1] * NDEV
    widths = [d.shape[1] for d in dys]
    assert sum(widths) == n
    nd = len(dys)

    def body(*refs):
        dy_refs = refs[:nd]
        x_ref, g_ref, dres_ref, gath_ref, dx_ref, dxb_ref, dgain_ref, wbuf, sems = refs[nd:]

        @pl.when(pl.program_id(0) == 0)
        def _():
            _load_weight(gath_ref, wbuf, sems)
            dgain_ref[...] = jnp.zeros_like(dgain_ref)
        xv, gain_v = x_ref[...], g_ref[...]
        _, r = _rms_fwd(xv, gain_v)
        dh, c0 = None, 0
        for dy_ref, wd in zip(dy_refs, widths):
            part = _dot(dy_ref[...], wbuf[c0:c0 + wd, :], NN)
            dh = part if dh is None else dh + part
            c0 += wd
        dx, dgain = _rms_bwd(dh, xv, r, gain_v)
        dx = dres_ref[...] + dx
        dx_ref[...] = dx
        dxb_ref[...] = dx.astype(BF16)
        dgain_ref[...] += dgain

    return dict(
        body=body, grid=(t // tm,), name=name, args=[*dys, x, gain, dres, gath],
        out_shape=[jax.ShapeDtypeStruct((t, dm), F32), jax.ShapeDtypeStruct((t, dm), BF16),
                   jax.ShapeDtypeStruct((1, dm), F32)],
        in_specs=[_tok(tm, wd) for wd in widths] + [_tok(tm, dm), _full((1, dm)), _tok(tm, dm), ANY],
        out_specs=[_tok(tm, dm), _tok(tm, dm), _full((1, dm))],
        scratch=[pltpu.VMEM((n, dm), BF16), pltpu.SemaphoreType.DMA((NDEV,))])


def _wgrad(a, b, name, tmm=256):
    parts = list(a) if isinstance(a, (list, tuple)) else [a]
    t = parts[0].shape[0]
    n = b.shape[1]
    tiles = [p.shape[1] // tmm for p in parts]
    first = [sum(tiles[:i]) for i in range(len(parts))]
    m = sum(tiles) * tmm

    def body(*refs):
        a_refs, b_ref, o_ref = refs[:len(parts)], refs[len(parts)], refs[len(parts) + 1]
        j = pl.program_id(0)
        for a_ref, j0, nt in zip(a_refs, first, tiles):
            if len(parts) == 1:
                o_ref[...] = _dot(a_ref[...], b_ref[...], TN).astype(BF16)
            else:
                @pl.when((j >= j0) & (j < j0 + nt))
                def _():
                    o_ref[...] = _dot(a_ref[...], b_ref[...], TN).astype(BF16)

    a_specs = [pl.BlockSpec((t, tmm), lambda j, j0=j0, nt=nt: (0, jnp.clip(j - j0, 0, nt - 1)))
               for j0, nt in zip(first, tiles)]
    return dict(
        body=body, grid=(sum(tiles),), name=name, args=[*parts, b], out_shape=[jax.ShapeDtypeStruct((m, n), BF16)],
        in_specs=a_specs + [pl.BlockSpec((t, n), lambda j: (0, 0))],
        out_specs=[pl.BlockSpec((tmm, n), lambda j: (j, 0))])


def _halo_specs(tm, t, width, col_blocks):
    nb8 = tm // 8
    last = t // 8 - 1
    prev = [pl.BlockSpec((8, width), lambda i, cb=cb: (jnp.maximum(i * nb8 - 1, 0), cb)) for cb in col_blocks]
    nxt = [pl.BlockSpec((8, width), lambda i, cb=cb: (jnp.minimum((i + 1) * nb8, last), cb)) for cb in col_blocks]
    return prev, nxt


def _shift_rows(z, prev_row, next_row):
    tm = z.shape[0]
    row = lax.broadcasted_iota(jnp.int32, z.shape, 0)
    zm1 = jnp.where(row == 0, prev_row, pltpu.roll(z, 1, 0))
    zp1 = jnp.where(row == tm - 1, next_row, pltpu.roll(z, tm - 1, 0))
    return zm1, zp1


def _gating_fwd(proj, lng, lnb, wsp_ref, bsp_ref, aw):
    tm = proj.shape[0]
    a_u = _gelu(proj[:, 0:aw])
    gv = _gelu(proj[:, aw:2 * aw])
    mu = jnp.mean(gv, axis=-1, keepdims=True)
    xc = gv - mu
    rstd = lax.rsqrt(jnp.mean(xc * xc, axis=-1, keepdims=True) + EPS)
    vn = xc * rstd
    a_v = (vn * lng + lnb).astype(BF16)
    gd = aw // A_GROUPS
    rows = []
    for c in range(tm // CHUNK):
        cols = []
        for g in range(A_GROUPS):
            blk = a_v[c * CHUNK:(c + 1) * CHUNK, g * gd:(g + 1) * gd]
            cols.append(_dot(wsp_ref[g], blk, NN) + bsp_ref[g])
        rows.append(jnp.concatenate(cols, axis=1))
    mixed = jnp.concatenate(rows, axis=0)
    return a_u, vn, rstd, a_v, mixed


def _even_core_fwd(proj, x0, lng, lnb, wsp, bspb, cw, gath, tm):
    t, dm = x0.shape
    aw = lng.shape[1]
    bw = cw.shape[1]
    assert aw == bw and 2 * aw + 3 * bw == proj.shape[1]
    nt = t // tm
    prev, nxt = _halo_specs(tm, t, bw, [3, 4])

    def body(proj_ref, cp_ref, hp_ref, cn_ref, hn_ref, x0_ref, lng_ref, lnb_ref, wsp_ref, bsp_ref, cw_ref, gath_ref,
             x1_ref, y_ref, wbuf, sems):
        i = pl.program_id(0)

        @pl.when(i == 0)
        def _():
            _load_weight(gath_ref, wbuf, sems)
        proj_v = proj_ref[...]
        a_u, _, _, _, mixed = _gating_fwd(proj_v, lng_ref[...], lnb_ref[...], wsp_ref, bsp_ref, aw)
        a_out = a_u * mixed
        bb = proj_v[:, 2 * aw:2 * aw + bw]
        z = proj_v[:, 2 * aw + bw:2 * aw + 2 * bw] * proj_v[:, 2 * aw + 2 * bw:]
        zprev = jnp.where(i > 0, cp_ref[7:8, :] * hp_ref[7:8, :], 0.0)
        znext = jnp.where(i < nt - 1, cn_ref[0:1, :] * hn_ref[0:1, :], 0.0)
        zm1, zp1 = _shift_rows(z, zprev, znext)
        cwv = cw_ref[...]
        conv = zm1 * cwv[0:1, :] + z * cwv[1:2, :] + zp1 * cwv[2:3, :]
        y = jnp.concatenate([a_out, bb * conv], axis=1).astype(BF16)
        y_ref[...] = y
        x1_ref[...] = x0_ref[...] + _dot(y, wbuf[...], NN)

    return dict(
        body=body, grid=(nt,), name="even_core_fwd",
        args=[proj, proj, proj, proj, proj, x0, lng, lnb, wsp, bspb, cw, gath],
        out_shape=[jax.ShapeDtypeStruct((t, dm), F32), jax.ShapeDtypeStruct((t, aw + bw), BF16)],
        in_specs=[_tok(tm, proj.shape[1]), prev[0], prev[1], nxt[0], nxt[1], _tok(tm, dm), _full(lng.shape),
                  _full(lnb.shape), _full(wsp.shape), _full(bspb.shape), _full(cw.shape), ANY],
        out_specs=[_tok(tm, dm), _tok(tm, aw + bw)],
        scratch=[pltpu.VMEM((gath.shape[1] * NDEV, dm), BF16), pltpu.SemaphoreType.DMA((NDEV,))])


def _even_core_bwd(proj, dx1, lng, lnb, wsp, bspb, cw, gath, tm):
    t, dm = dx1.shape
    aw, bw = lng.shape[1], cw.shape[1]
    gd = aw // A_GROUPS
    nt = t // tm
    inw = proj.shape[1]
    prev, nxt = _halo_specs(tm, t, bw, [2, 3, 4])
    nb8 = tm // 8
    last8 = t // 8 - 1

    def body(proj_ref, bp_ref, cp_ref, hp_ref, bn_ref, cn_ref, hn_ref, dx_ref, dxp_ref, dxn_ref,
             lng_ref, lnb_ref, wsp_ref, bsp_ref, cw_ref, gath_ref,
             dproj_ref, dlng_ref, dlnb_ref, dwsp_ref, dbsp_ref, dcw_ref, wbuf, sems):
        i = pl.program_id(0)

        @pl.when(i == 0)
        def _():
            _load_weight(gath_ref, wbuf, sems)
            dlng_ref[...] = jnp.zeros_like(dlng_ref)
            dlnb_ref[...] = jnp.zeros_like(dlnb_ref)
            dwsp_ref[...] = jnp.zeros_like(dwsp_ref)
            dbsp_ref[...] = jnp.zeros_like(dbsp_ref)
            dcw_ref[...] = jnp.zeros_like(dcw_ref)
        proj_v = proj_ref[...]
        lng_v = lng_ref[...]
        a_u, vn, rstd, a_v, mixed = _gating_fwd(proj_v, lng_v, lnb_ref[...], wsp_ref, bsp_ref, aw)
        w = wbuf[...]
        dy = _dot(dx_ref[...].astype(BF16), w, NT)
        da_out, db_out = dy[:, 0:aw], dy[:, aw:]
        da_u = da_out * mixed
        dmixed = da_out * a_u
        dmb = dmixed.astype(BF16)
        rows = []
        for c in range(tm // CHUNK):
            cols = []
            for g in range(A_GROUPS):
                r0, c0 = c * CHUNK, g * gd
                dm_cg = dmb[r0:r0 + CHUNK, c0:c0 + gd]
                cols.append(_dot(wsp_ref[g], dm_cg, TN))
                dwsp_ref[g] += _dot(dm_cg, a_v[r0:r0 + CHUNK, c0:c0 + gd], NT)
                dbsp_ref[g] += dmixed[r0:r0 + CHUNK, c0:c0 + gd]
            rows.append(jnp.concatenate(cols, axis=1))
        dav = jnp.concatenate(rows, axis=0)
        dlng_ref[...] += jnp.sum(dav * vn, axis=0, keepdims=True)
        dlnb_ref[...] += jnp.sum(dav, axis=0, keepdims=True)
        dvn = dav * lng_v
        dgv = rstd * (dvn - jnp.mean(dvn, axis=-1, keepdims=True) - vn * jnp.mean(dvn * vn, axis=-1, keepdims=True))
        dv_pre = dgv * _gelu_grad(proj_v[:, aw:2 * aw])
        du_pre = da_u * _gelu_grad(proj_v[:, 0:aw])
        bb = proj_v[:, 2 * aw:2 * aw + bw]
        bc = proj_v[:, 2 * aw + bw:2 * aw + 2 * bw]
        bh = proj_v[:, 2 * aw + 2 * bw:]
        z = bc * bh
        zprev = jnp.where(i > 0, cp_ref[7:8, :] * hp_ref[7:8, :], 0.0)
        znext = jnp.where(i < nt - 1, cn_ref[0:1, :] * hn_ref[0:1, :], 0.0)
        zm1, zp1 = _shift_rows(z, zprev, znext)
        cwv = cw_ref[...]
        conv = zm1 * cwv[0:1, :] + z * cwv[1:2, :] + zp1 * cwv[2:3, :]
        dbb = db_out * conv
        dconv = db_out * bb
        dx_edge = jnp.concatenate([dxp_ref[...], dxn_ref[...]], axis=0).astype(BF16)
        dy_edge = _dot(dx_edge, w[aw:, :], NT)
        dcprev = jnp.where(i > 0, dy_edge[7:8, :] * bp_ref[7:8, :], 0.0)
        dcnext = jnp.where(i < nt - 1, dy_edge[8:9, :] * bn_ref[0:1, :], 0.0)
        dcm1, dcp1 = _shift_rows(dconv, dcprev, dcnext)
        dz = dcp1 * cwv[0:1, :] + dconv * cwv[1:2, :] + dcm1 * cwv[2:3, :]
        dcw_ref[0:1, :] += jnp.sum(dconv * zm1, axis=0, keepdims=True)
        dcw_ref[1:2, :] += jnp.sum(dconv * z, axis=0, keepdims=True)
        dcw_ref[2:3, :] += jnp.sum(dconv * zp1, axis=0, keepdims=True)
        dproj_ref[...] = jnp.concatenate([du_pre, dv_pre, dbb, dz * bh, dz * bc], axis=1).astype(BF16)

    row8 = lambda f: pl.BlockSpec((8, dm), f)
    return dict(
        body=body, grid=(nt,), name="even_core_bwd",
        args=[proj, proj, proj, proj, proj, proj, proj, dx1, dx1, dx1, lng, lnb, wsp, bspb, cw, gath],
        out_shape=[jax.ShapeDtypeStruct((t, inw), BF16), jax.ShapeDtypeStruct((1, aw), F32),
                   jax.ShapeDtypeStruct((1, aw), F32), jax.ShapeDtypeStruct(wsp.shape, F32),
                   jax.ShapeDtypeStruct((A_GROUPS, CHUNK, gd), F32), jax.ShapeDtypeStruct(cw.shape, F32)],
        in_specs=[_tok(tm, inw), prev[0], prev[1], prev[2], nxt[0], nxt[1], nxt[2], _tok(tm, dm),
                  row8(lambda i: (jnp.maximum(i * nb8 - 1, 0), 0)), row8(lambda i: (jnp.minimum((i + 1) * nb8, last8), 0)),
                  _full(lng.shape), _full(lnb.shape), _full(wsp.shape), _full(bspb.shape), _full(cw.shape), ANY],
        out_specs=[_tok(tm, inw), _full((1, aw)), _full((1, aw)), _full(wsp.shape),
                   _full((A_GROUPS, CHUNK, gd)), _full(cw.shape)],
        scratch=[pltpu.VMEM((gath.shape[1] * NDEV, dm), BF16), pltpu.SemaphoreType.DMA((NDEV,))])


def _ff_chunks(f, width=1024):
    return [(c0, min(c0 + width, f)) for c0 in range(0, f, width)]


def _ffn_up(x, gain, gath_g, gath_u, name, tm):
    t, dm = x.shape
    f = gath_g.shape[1] * NDEV

    def body(x_ref, g_ref, gg_ref, gu_ref, gate_ref, up_ref, act_ref, wg, wu, sems):
        @pl.when(pl.program_id(0) == 0)
        def _():
            _load_weight(gg_ref, wg, sems)
            _load_weight(gu_ref, wu, sems)
        h, _ = _rms_fwd(x_ref[...], g_ref[...])
        hb = h.astype(BF16)
        for c0, c1 in _ff_chunks(f):
            gate = _dot(hb, wg[c0:c1, :], NT)
            up = _dot(hb, wu[c0:c1, :], NT)
            gate_ref[:, c0:c1] = gate.astype(BF16)
            up_ref[:, c0:c1] = up.astype(BF16)
            act_ref[:, c0:c1] = (gate * _sigmoid(gate) * up).astype(BF16)

    o = jax.ShapeDtypeStruct((t, f), BF16)
    return dict(
        body=body, grid=(t // tm,), name=name, args=[x, gain, gath_g, gath_u], out_shape=[o, o, o],
        in_specs=[_tok(tm, dm), _full((1, dm)), ANY, ANY], out_specs=[_tok(tm, f)] * 3,
        scratch=[pltpu.VMEM((f, dm), BF16), pltpu.VMEM((f, dm), BF16), pltpu.SemaphoreType.DMA((NDEV,))])


def _ffn_down(x, act, gath_d, name, tm):
    t, dm = x.shape
    f = act.shape[1]

    def body(x_ref, a_ref, gd_ref, xo_ref, wd, sems):
        @pl.when(pl.program_id(0) == 0)
        def _():
            _load_weight(gd_ref, wd, sems)
        xo_ref[...] = x_ref[...] + _dot(a_ref[...], wd[...], NN)

    return dict(
        body=body, grid=(t // tm,), name=name, args=[x, act, gath_d], out_shape=[jax.ShapeDtypeStruct((t, dm), F32)],
        in_specs=[_tok(tm, dm), _tok(tm, f), ANY], out_specs=[_tok(tm, dm)],
        scratch=[pltpu.VMEM((f, dm), BF16), pltpu.SemaphoreType.DMA((NDEV,))])


def _ffn_down_loss(x, act, gath_d, target, gain, name, tm):
    t, dm = x.shape
    f = act.shape[1]
    steps = t // tm

    def body(x_ref, a_ref, gd_ref, t_ref, g_ref, loss_ref, dx_ref, dxb_ref, dgain_ref, wd, acc, sems):
        i = pl.program_id(0)

        @pl.when(i == 0)
        def _():
            _load_weight(gd_ref, wd, sems)
            acc[...] = jnp.zeros_like(acc)
            dgain_ref[...] = jnp.zeros_like(dgain_ref)
        xv = x_ref[...] + _dot(a_ref[...], wd[...], NN)
        gain_v = g_ref[...]
        y, r = _rms_fwd(xv, gain_v)
        e = y - t_ref[...]
        acc[...] += jnp.sum(e * e, axis=0, keepdims=True)
        dx, dgain = _rms_bwd(e * (1.0 / dm), xv, r, gain_v)
        dx_ref[...] = dx
        dxb_ref[...] = dx.astype(BF16)
        dgain_ref[...] += dgain

        @pl.when(i == steps - 1)
        def _():
            loss_ref[...] = jnp.sum(acc[...], axis=-1, keepdims=True) * (0.5 / dm)

    return dict(
        body=body, grid=(steps,), name=name, args=[x, act, gath_d, target, gain],
        out_shape=[jax.ShapeDtypeStruct((1, 1), F32), jax.ShapeDtypeStruct((t, dm), F32),
                   jax.ShapeDtypeStruct((t, dm), BF16), jax.ShapeDtypeStruct((1, dm), F32)],
        in_specs=[_tok(tm, dm), _tok(tm, f), ANY, _tok(tm, dm), _full((1, dm))],
        out_specs=[_full((1, 1)), _tok(tm, dm), _tok(tm, dm), _full((1, dm))],
        scratch=[pltpu.VMEM((f, dm), BF16), pltpu.VMEM((1, dm), F32), pltpu.SemaphoreType.DMA((NDEV,))])


def _ffn_bwd(dxo, x, gate, up, gain, gath_g, gath_u, gath_d, name, tm):
    t, dm = x.shape
    f = gate.shape[1]

    def body(dxo_ref, x_ref, gate_ref, up_ref, g_ref, gg_ref, gu_ref, gd_ref,
             dx_ref, dxb_ref, dg_ref, du_ref, hb_ref, dgain_ref, wg, wu, wd, sems):
        @pl.when(pl.program_id(0) == 0)
        def _():
            _load_weight(gg_ref, wg, sems)
            _load_weight(gu_ref, wu, sems)
            _load_weight(gd_ref, wd, sems)
            dgain_ref[...] = jnp.zeros_like(dgain_ref)
        xv, gain_v, dxo_v = x_ref[...], g_ref[...], dxo_ref[...]
        h, r = _rms_fwd(xv, gain_v)
        hb_ref[...] = h.astype(BF16)
        dxob = dxo_v.astype(BF16)
        dh = jnp.zeros_like(xv)
        for c0, c1 in _ff_chunks(f):
            gate_v = gate_ref[:, c0:c1].astype(F32)
            up_v = up_ref[:, c0:c1].astype(F32)
            s = _sigmoid(gate_v)
            silu = gate_v * s
            dact = _dot(dxob, wd[c0:c1, :], NT)
            dg = (dact * up_v * (s * (1.0 + gate_v * (1.0 - s)))).astype(BF16)
            du = (dact * silu).astype(BF16)
            dg_ref[:, c0:c1] = dg
            du_ref[:, c0:c1] = du
            dh = dh + _dot(dg, wg[c0:c1, :], NN) + _dot(du, wu[c0:c1, :], NN)
        dx, dgain = _rms_bwd(dh, xv, r, gain_v)
        dx = dxo_v + dx
        dx_ref[...] = dx
        dxb_ref[...] = dx.astype(BF16)
        dgain_ref[...] += dgain

    return dict(
        body=body, grid=(t // tm,), name=name, args=[dxo, x, gate, up, gain, gath_g, gath_u, gath_d],
        out_shape=[jax.ShapeDtypeStruct((t, dm), F32), jax.ShapeDtypeStruct((t, dm), BF16),
                   jax.ShapeDtypeStruct((t, f), BF16), jax.ShapeDtypeStruct((t, f), BF16),
                   jax.ShapeDtypeStruct((t, dm), BF16), jax.ShapeDtypeStruct((1, dm), F32)],
        in_specs=[_tok(tm, dm), _tok(tm, dm), _tok(tm, f), _tok(tm, f), _full((1, dm)), ANY, ANY, ANY],
        out_specs=[_tok(tm, dm), _tok(tm, dm), _tok(tm, f), _tok(tm, f), _tok(tm, dm), _full((1, dm))],
        scratch=[pltpu.VMEM((f, dm), BF16), pltpu.VMEM((f, dm), BF16), pltpu.VMEM((f, dm), BF16),
                 pltpu.SemaphoreType.DMA((NDEV,))])


def _t5_buckets(rel):
    nb = N_BUCKETS // 2
    ret = jnp.where(rel > 0, nb, 0)
    n = jnp.abs(rel)
    max_exact = nb // 2
    nf = jnp.maximum(n, 1).astype(jnp.float32)
    large = max_exact + (jnp.log(nf / max_exact) / math.log(MAX_DISTANCE / max_exact)
                         * (nb - max_exact)).astype(jnp.int32)
    large = jnp.minimum(large, nb - 1)
    return ret + jnp.where(n < max_exact, n, large)


def _bucket_table():
    qi = jnp.arange(CHUNK, dtype=jnp.int32)[:, None]
    kj = jnp.arange(3 * CHUNK, dtype=jnp.int32)[None, :]
    rel = kj - CHUNK - qi
    return jnp.where(jnp.abs(rel) <= CHUNK, _t5_buckets(rel), -1)


def _bias_table(rel_bias_t, buckets):
    nh = rel_bias_t.shape[0]

    def body(rb_ref, bk_ref, o_ref):
        bk = bk_ref[...]
        for h in range(nh):
            acc = jnp.where(bk < 0, NEG, 0.0).astype(F32)
            for b in range(N_BUCKETS):
                acc = jnp.where(bk == b, rb_ref[h, b] * LOG2E, acc)
            o_ref[h] = acc

    return dict(
        body=body, grid=(1,), name="bias_table", args=[rel_bias_t, buckets],
        out_shape=[jax.ShapeDtypeStruct((nh,) + buckets.shape, F32)],
        in_specs=[pl.BlockSpec(memory_space=pltpu.SMEM), _full(buckets.shape)],
        out_specs=[_full((nh,) + buckets.shape)])


def _rel_bias_grad(dbias, buckets):
    nh = dbias.shape[0]

    def body(db_ref, bk_ref, o_ref):
        bk = bk_ref[...]
        lane = lax.broadcasted_iota(jnp.int32, (1, 128), 1)
        for h in range(nh):
            d = db_ref[h]
            row = jnp.zeros((1, 128), F32)
            for b in range(N_BUCKETS):
                s = jnp.sum(jnp.sum(jnp.where(bk == b, d, 0.0), axis=1, keepdims=True), axis=0, keepdims=True)
                row = jnp.where(lane == b, s, row)
            o_ref[h:h + 1, :] = row

    return dict(
        body=body, grid=(1,), name="rel_bias_grad", args=[dbias, buckets],
        out_shape=[jax.ShapeDtypeStruct((nh, 128), F32)],
        in_specs=[_full(dbias.shape), _full(buckets.shape)], out_specs=[_full((nh, 128))])


def _half_masks():
    lane = lax.broadcasted_iota(jnp.int32, (CHUNK, 128), 1)
    return lane < HEAD_DIM, lane >= HEAD_DIM


def _kv_low(ref, starts, hk, lo):
    kt = (hk // 2) * 128
    out = []
    for jj in range(3):
        blk = ref[pl.ds(starts[jj], CHUNK), kt:kt + 128]
        if hk % 2 == 1:
            blk = pltpu.roll(blk, HEAD_DIM, 1)
        out.append(jnp.where(lo, blk, jnp.zeros_like(blk)))
    return out


def _stack_heads(tile_a, tile_b):
    return jnp.concatenate([tile_a, pltpu.roll(tile_a, HEAD_DIM, 1), tile_b, pltpu.roll(tile_b, HEAD_DIM, 1)], axis=0)


def _unstack_heads(o4):
    return (o4[0:CHUNK] + pltpu.roll(o4[CHUNK:2 * CHUNK], HEAD_DIM, 1),
            o4[2 * CHUNK:3 * CHUNK] + pltpu.roll(o4[3 * CHUNK:], HEAD_DIM, 1))


ATT_SLAB = 32


def _softmax_slab(s_scr, hk, g, r0, bias_ref, sink_ref, n, nblk):
    scale = HEAD_DIM ** -0.5 * LOG2E
    h = (N_HEADS // N_KV) * hk + g
    s = []
    for jj in range(3):
        sj = (s_scr[hk, jj, pl.ds(g * CHUNK + r0, ATT_SLAB), :] * scale
              + bias_ref[h, pl.ds(r0, ATT_SLAB), jj * CHUNK:(jj + 1) * CHUNK])
        if jj == 0:
            sj = jnp.where(n > 0, sj, NEG)
        if jj == 2:
            sj = jnp.where(n < nblk - 1, sj, NEG)
        s.append(sj)
    sink = sink_ref[h] * LOG2E
    m = jnp.maximum(jnp.max(jnp.maximum(jnp.maximum(s[0], s[1]), s[2]), axis=-1, keepdims=True), sink)
    e = [jnp.exp2(sj - m) for sj in s]
    es = jnp.exp2(sink - m)
    inv = 1.0 / (jnp.sum(e[0] + e[1] + e[2], axis=-1, keepdims=True) + es)
    return [ej * inv for ej in e], es * inv


def _key_block_starts(n, nblk):
    return [pl.multiple_of(jnp.clip(n - 1 + jj, 0, nblk - 1) * CHUNK, CHUNK) for jj in range(3)]


def _attn_fwd(qkv, x2, bias, sink, gath):
    t, dm = x2.shape
    nblk = t // CHUNK
    kvw = N_KV * HEAD_DIM
    kcb, vcb = dm // kvw, dm // kvw + 1
    slab = (N_KV, 3, 4 * CHUNK, CHUNK)

    def body(q_ref, k_ref, v_ref, x2_ref, bias_ref, sink_ref, gath_ref, x3_ref, att_ref, p_ref, ps_ref,
             wbuf, s_scr, sems):
        n = pl.program_id(0)

        @pl.when(n == 0)
        def _():
            _load_weight(gath_ref, wbuf, sems)
        lo, _ = _half_masks()
        lane_s = lax.broadcasted_iota(jnp.int32, (ATT_SLAB, 128), 1)
        starts = _key_block_starts(n, nblk)
        tiles = []
        for hk in range(N_KV):
            c0 = (2 * hk) * 128
            k_lo = _kv_low(k_ref, starts, hk, lo)
            v_lo = _kv_low(v_ref, starts, hk, lo)
            q4 = _stack_heads(q_ref[:, c0:c0 + 128], q_ref[:, c0 + 128:c0 + 256])
            for jj in range(3):
                s_scr[hk, jj] = _dot(q4, k_lo[jj], NT)
            for g in range(4):
                h = 4 * hk + g
                for r0 in range(0, CHUNK, ATT_SLAB):
                    p, ps = _softmax_slab(s_scr, hk, g, r0, bias_ref, sink_ref, n, nblk)
                    for jj in range(3):
                        p_ref[hk, jj, g * CHUNK + r0:g * CHUNK + r0 + ATT_SLAB, :] = p[jj].astype(BF16)
                    rest = jnp.zeros((ATT_SLAB, 128), F32) if h == 0 else ps_ref[r0:r0 + ATT_SLAB, :]
                    ps_ref[r0:r0 + ATT_SLAB, :] = jnp.where(lane_s == h, ps, rest)
            o4 = _dot(p_ref[hk, 0], v_lo[0], NN) + _dot(p_ref[hk, 1], v_lo[1], NN) + _dot(p_ref[hk, 2], v_lo[2], NN)
            tiles += list(_unstack_heads(o4))
        att = jnp.concatenate(tiles, axis=1).astype(BF16)
        att_ref[...] = att
        x3_ref[...] = x2_ref[...] + _dot(att, wbuf[...], NN)

    blk = pl.BlockSpec((CHUNK, dm), lambda n: (n, 0))
    return dict(
        body=body, grid=(nblk,), name="attn_fwd", args=[qkv, qkv, qkv, x2, bias, sink, gath],
        out_shape=[jax.ShapeDtypeStruct((t, dm), F32), jax.ShapeDtypeStruct((t, dm), BF16),
                   jax.ShapeDtypeStruct((nblk,) + slab, BF16), jax.ShapeDtypeStruct((t, 128), F32)],
        in_specs=[blk, pl.BlockSpec((t, kvw), lambda n: (0, kcb)), pl.BlockSpec((t, kvw), lambda n: (0, vcb)), blk,
                  _full(bias.shape), pl.BlockSpec(memory_space=pltpu.SMEM), ANY],
        out_specs=[blk, blk, pl.BlockSpec((None,) + slab, lambda n: (n, 0, 0, 0, 0)),
                   pl.BlockSpec((CHUNK, 128), lambda n: (n, 0))],
        scratch=[pltpu.VMEM((gath.shape[1] * NDEV, dm), BF16), pltpu.VMEM(slab, F32),
                 pltpu.SemaphoreType.DMA((NDEV,))])


def _attn_bwd(qkv, att, probs, sink_probs, dx3, bias_shape, gath):
    t, dm = dx3.shape
    nblk = t // CHUNK
    kvw = N_KV * HEAD_DIM
    kcb, vcb = dm // kvw, dm // kvw + 1
    scale = HEAD_DIM ** -0.5
    slab = (N_KV, 3, 4 * CHUNK, CHUNK)

    def body(q_ref, k_ref, v_ref, att_ref, p_ref, ps_ref, dx_ref, gath_ref,
             dq_ref, dkb_ref, dvb_ref, dbias_ref, dsink_ref,
             wbuf, dp_scr, ds_scr, prod_scr, dsum_scr, dk_ref, dv_ref, sems):
        n = pl.program_id(0)

        @pl.when(n == 0)
        def _():
            _load_weight(gath_ref, wbuf, sems)
            dk_ref[...] = jnp.zeros_like(dk_ref)
            dv_ref[...] = jnp.zeros_like(dv_ref)
            dbias_ref[...] = jnp.zeros_like(dbias_ref)
            dsink_ref[...] = jnp.zeros_like(dsink_ref)
        lo, hi = _half_masks()
        lane_s = lax.broadcasted_iota(jnp.int32, (ATT_SLAB, 128), 1)
        starts = _key_block_starts(n, nblk)
        dout = _dot(dx_ref[...].astype(BF16), wbuf[...], NT)
        prod_scr[...] = dout * att_ref[...].astype(F32)
        doutb = dout.astype(BF16)
        dq_tiles = []
        for hk in range(N_KV):
            kt = (hk // 2) * 128
            c0 = (2 * hk) * 128
            k_lo = _kv_low(k_ref, starts, hk, lo)
            v_lo = _kv_low(v_ref, starts, hk, lo)
            q4 = _stack_heads(q_ref[:, c0:c0 + 128], q_ref[:, c0 + 128:c0 + 256])
            do4 = _stack_heads(doutb[:, c0:c0 + 128], doutb[:, c0 + 128:c0 + 256])
            for jj in range(3):
                dp_scr[hk, jj] = _dot(do4, v_lo[jj], NT)
            for g in range(4):
                h = 4 * hk + g
                for r0 in range(0, CHUNK, ATT_SLAB):
                    rows = slice(g * CHUNK + r0, g * CHUNK + r0 + ATT_SLAB)
                    pt = prod_scr[r0:r0 + ATT_SLAB, c0 + (g // 2) * 128:c0 + (g // 2 + 1) * 128]
                    msk = lane_s < HEAD_DIM if g % 2 == 0 else lane_s >= HEAD_DIM
                    dsum = jnp.sum(jnp.where(msk, pt, 0.0), axis=-1, keepdims=True)
                    rest = jnp.zeros((ATT_SLAB, 128), F32) if h == 0 else dsum_scr[r0:r0 + ATT_SLAB, :]
                    dsum_scr[r0:r0 + ATT_SLAB, :] = jnp.where(lane_s == h, dsum, rest)
                    for jj in range(3):
                        ds = p_ref[hk, jj, rows, :].astype(F32) * (dp_scr[hk, jj, rows, :] - dsum)
                        dbias_ref[h, r0:r0 + ATT_SLAB, jj * CHUNK:(jj + 1) * CHUNK] += ds
                        ds_scr[hk, jj, rows, :] = ds.astype(BF16)
            dq4 = jnp.zeros((4 * CHUNK, 128), F32)
            for jj in range(3):
                ds4 = ds_scr[hk, jj]
                dq4 = dq4 + _dot(ds4, k_lo[jj], NN) * scale
                dkj = _dot(ds4, q4, TN) * scale
                dvj = _dot(p_ref[hk, jj], do4, TN)
                if hk % 2 == 1:
                    dkj, dvj = pltpu.roll(dkj, HEAD_DIM, 1), pltpu.roll(dvj, HEAD_DIM, 1)
                keep = lo if hk % 2 == 0 else hi
                dk_ref[pl.ds(starts[jj], CHUNK), kt:kt + 128] += jnp.where(keep, dkj, 0.0)
                dv_ref[pl.ds(starts[jj], CHUNK), kt:kt + 128] += jnp.where(keep, dvj, 0.0)
            dq_tiles += list(_unstack_heads(dq4))
        dq_ref[...] = jnp.concatenate(dq_tiles, axis=1).astype(BF16)
        dsink_ref[...] -= jnp.sum(ps_ref[...] * dsum_scr[...], axis=0, keepdims=True)

        @pl.when(n == nblk - 1)
        def _():
            dkb_ref[...] = dk_ref[...].astype(BF16)
            dvb_ref[...] = dv_ref[...].astype(BF16)

    blk = pl.BlockSpec((CHUNK, dm), lambda n: (n, 0))
    return dict(
        body=body, grid=(nblk,), name="attn_bwd", args=[qkv, qkv, qkv, att, probs, sink_probs, dx3, gath],
        out_shape=[jax.ShapeDtypeStruct((t, dm), BF16), jax.ShapeDtypeStruct((t, kvw), BF16),
                   jax.ShapeDtypeStruct((t, kvw), BF16), jax.ShapeDtypeStruct(bias_shape, F32),
                   jax.ShapeDtypeStruct((1, 128), F32)],
        in_specs=[blk, pl.BlockSpec((t, kvw), lambda n: (0, kcb)), pl.BlockSpec((t, kvw), lambda n: (0, vcb)),
                  blk, pl.BlockSpec((None,) + slab, lambda n: (n, 0, 0, 0, 0)),
                  pl.BlockSpec((CHUNK, 128), lambda n: (n, 0)), blk, ANY],
        out_specs=[blk, _full((t, kvw)), _full((t, kvw)), _full(bias_shape), _full((1, 128))],
        scratch=[pltpu.VMEM((gath.shape[1] * NDEV, dm), BF16), pltpu.VMEM(slab, F32), pltpu.VMEM(slab, BF16),
                 pltpu.VMEM((CHUNK, dm), F32), pltpu.VMEM((CHUNK, 128), F32),
                 pltpu.VMEM((t, kvw), F32), pltpu.VMEM((t, kvw), F32), pltpu.SemaphoreType.DMA((NDEV,))])


def _finish_weight(recvs, w, m, v, name):
    nl, r, dm = w.shape
    assert nl == len(recvs) and all(rc.shape[1:] == (r, dm) for rc in recvs)
    td = dm // 2
    wspec = pl.BlockSpec((None, r, td), lambda l, j: (l, 0, j))

    def body(*refs):
        r_refs = refs[:nl]
        w_ref, m_ref, v_ref, g_ref, d_ref, nm_ref, nv_ref = refs[nl:]
        layer = pl.program_id(0)
        for li in range(nl):
            @pl.when(layer == li)
            def _():
                g = r_refs[li][0].astype(F32)
                for d in range(1, recvs[li].shape[0]):
                    g = g + r_refs[li][d].astype(F32)
                delta, nm, nv = _adamw_math(w_ref[...], g, m_ref[...], v_ref[...])
                g_ref[...] = g
                d_ref[...] = delta
                nm_ref[...] = nm
                nv_ref[...] = nv

    o = jax.ShapeDtypeStruct(w.shape, F32)
    return dict(
        body=body, grid=(nl, 2), name=name, args=[*recvs, w, m, v], out_shape=[o, o, o, o],
        in_specs=[pl.BlockSpec((rc.shape[0], r, td), lambda l, j: (0, 0, j)) for rc in recvs] + [wspec] * 3,
        out_specs=[wspec] * 4)


def _adamw_small(ws, ms, vs, slots, lates, loss_slots, name):
    n = len(ws)
    nl = len(lates)

    def total(ref):
        acc = ref[0].astype(F32)
        for d in range(1, NDEV):
            acc = acc + ref[d].astype(F32)
        return acc

    def body(*refs):
        ins, outs = refs[:4 * n + nl + 1], refs[4 * n + nl + 1:]
        for i in range(n):
            w_ref, m_ref, v_ref, s_ref = ins[4 * i:4 * i + 4]
            g_ref, d_ref, nm_ref, nv_ref = outs[4 * i:4 * i + 4]
            g_ref[...] = total(s_ref)
            for k, (at, late) in enumerate(lates):
                if at == i:
                    g_ref[0:late.shape[1], :] = total(ins[4 * n + k])
            d_ref[...], nm_ref[...], nv_ref[...] = _adamw_math(w_ref[...], g_ref[...], m_ref[...], v_ref[...])
        outs[4 * n][...] = total(ins[4 * n + nl])

    args, out_shape = [], []
    for w, m, v, s in zip(ws, ms, vs, slots):
        args += [w, m, v, s]
        out_shape += [jax.ShapeDtypeStruct(w.shape, F32)] * 4
    args += [late for _, late in lates] + [loss_slots]
    out_shape.append(jax.ShapeDtypeStruct((1, 1), F32))
    out = pl.pallas_call(
        body, grid=(1,), out_shape=tuple(out_shape), in_specs=[_full(a.shape) for a in args],
        out_specs=tuple(_full(o.shape) for o in out_shape), compiler_params=_cp(), name=name)(*args)
    return [tuple(out[4 * i:4 * i + 4]) for i in range(n)], out[4 * n]


def kernel(x, norm_mix, norm_ffn, even_w_in, even_v_ln_g, even_v_ln_b, even_w_spatial, even_b_spatial, even_conv_w, even_w_out, attn_w_qkv, attn_sink, rel_bias, attn_w_out, ffn_w_gate, ffn_w_up, ffn_w_down, final_norm, loss_target, m_norm_mix, m_norm_ffn, m_even_w_in, m_even_v_ln_g, m_even_v_ln_b, m_even_w_spatial, m_even_b_spatial, m_even_conv_w, m_even_w_out, m_attn_w_qkv, m_attn_sink, m_rel_bias, m_attn_w_out, m_ffn_w_gate, m_ffn_w_up, m_ffn_w_down, m_final_norm, v_norm_mix, v_norm_ffn, v_even_w_in, v_even_v_ln_g, v_even_v_ln_b, v_even_w_spatial, v_even_b_spatial, v_even_conv_w, v_even_w_out, v_attn_w_qkv, v_attn_sink, v_rel_bias, v_attn_w_out, v_ffn_w_gate, v_ffn_w_up, v_ffn_w_down, v_final_norm):
    t, dm = x.shape[1], x.shape[2]
    aw = even_v_ln_g.shape[1]
    bw = even_conv_w.shape[2] * NDEV
    gd = aw // A_GROUPS
    tm = min(512, t // 2)
    tmf = min(256, t // 2)
    me = _my_index()
    row = lambda a: a.reshape(1, -1)

    colT = lambda w: w.T.astype(BF16)
    sh = dict(winT=colT(even_w_in[0]), wqkvT=colT(attn_w_qkv[0]), wgT0=colT(ffn_w_gate[0]), wuT0=colT(ffn_w_up[0]),
              wgT1=colT(ffn_w_gate[1]), wuT1=colT(ffn_w_up[1]), woe=even_w_out[0].astype(BF16),
              woa=attn_w_out[0].astype(BF16), wd0=ffn_w_down[0].astype(BF16), wd1=ffn_w_down[1].astype(BF16))
    gather = lambda names: _GatherCarry([sh[n] for n in names])

    in_full = lambda a: lax.dynamic_update_slice(jnp.zeros((3, bw), F32), a[0], (0, me * (bw // NDEV)))

    x0 = x[0]
    wsp_b = even_w_spatial[0].astype(BF16)
    bspb = jnp.broadcast_to(even_b_spatial[0][:, :, None], (A_GROUPS, CHUNK, gd))
    buckets = _bucket_table()
    sink = attn_sink[0]

    (bias,), ((g_winT,), (cw_slots,)) = _call(
        _bias_table(rel_bias.T, buckets), [gather(["winT"]), _BroadcastCarry([in_full(even_conv_w)])])
    cw_full = jnp.sum(cw_slots, axis=0)
    (proj, h0b), (g_woe, g_wgT0) = _call(_norm_proj(x0, row(norm_mix[0]), g_winT, F32, "in_proj", tm),
                                         gather(["woe", "wgT0"]))
    (x1, yb), (g_wuT0,) = _call(_even_core_fwd(proj, x0, even_v_ln_g, even_v_ln_b, wsp_b, bspb, cw_full, g_woe, tm),
                                gather(["wuT0"]))
    (gate0, up0, act0), (g_wd0,) = _call(_ffn_up(x1, row(norm_ffn[0]), g_wgT0, g_wuT0, "ffn_up0", tmf), gather(["wd0"]))
    (x2,), (g_wqkvT,) = _call(_ffn_down(x1, act0, g_wd0, "ffn_down0", tm), gather(["wqkvT"]))
    (qkv, h2b), (g_woa,) = _call(_norm_proj(x2, row(norm_mix[1]), g_wqkvT, BF16, "qkv_proj", tm), gather(["woa"]))
    (x3, attb, probs, sink_probs), (g_wgT1, g_wuT1) = _call(
        _attn_fwd(qkv, x2, bias, sink, g_woa), gather(["wgT1", "wuT1"]))
    (gate1, up1, act1), (g_wd1,) = _call(_ffn_up(x3, row(norm_ffn[1]), g_wgT1, g_wuT1, "ffn_up1", tmf), gather(["wd1"]))
    (loss_part, dx4, dx4b, d_final), _ = _call(
        _ffn_down_loss(x3, act1, g_wd1, loss_target[0], row(final_norm), "ffn_down1_loss", tm))

    (dx3, dx3b, dg1, du1, h3b, d_nffn1), _ = _call(
        _ffn_bwd(dx4, x3, gate1, up1, row(norm_ffn[1]), g_wgT1, g_wuT1, g_wd1, "ffn_bwd1", tmf))
    (p_wgT1,), _ = _call(_wgrad(dg1, h3b, "wgrad_gate1"))
    (p_wuT1,), _ = _call(_wgrad(du1, h3b, "wgrad_up1"))
    (p_wd1,), ((a_wgT1,), (a_wuT1,)) = _call(
        _wgrad(act1, dx4b, "wgrad_down1"), [_PairCarry(p_wgT1), _PairCarry(p_wuT1)])
    (dq, dk, dv, dbias, dsink), ((r_wgT1,), (a_wd1,)) = _call(
        _attn_bwd(qkv, attb, probs, sink_probs, dx3, bias.shape, g_woa),
        [_ChipSumCarry(p_wgT1, a_wgT1), _PairCarry(p_wd1)])
    (p_woa,), _ = _call(_wgrad(attb, dx3b, "wgrad_attn_out"))
    (dx2, dx2b, d_nmix1), (r_wuT1,) = _call(
        _proj_bwd_norm([dq, dk, dv], x2, row(norm_mix[1]), dx3, g_wqkvT, "qkv_bwd", tm),
        _ChipSumCarry(p_wuT1, a_wuT1))
    (p_wqkvT,), _ = _call(_wgrad([dq, dk, dv], h2b, "wgrad_qkv"))
    (dx1, dx1b, dg0, du0, h1b, d_nffn0), ((r_wd1,), (r_woa, r_wqkvT)) = _call(
        _ffn_bwd(dx2, x1, gate0, up0, row(norm_ffn[0]), g_wgT0, g_wuT0, g_wd0, "ffn_bwd0", tmf),
        [_ChipSumCarry(p_wd1, a_wd1), _GradCarry([p_woa, p_wqkvT])])
    (p_woe,), _ = _call(_wgrad(yb, dx1b, "wgrad_even_out"))
    (p_wgT0,), (r_woe,) = _call(_wgrad(dg0, h1b, "wgrad_gate0"), _GradCarry([p_woe]))
    (p_wuT0,), (a_wgT0,) = _call(_wgrad(du0, h1b, "wgrad_up0"), _PairCarry(p_wgT0))
    (p_wd0,), ((r_wgT0,), (a_wuT0,)) = _call(
        _wgrad(act0, dx2b, "wgrad_down0"), [_ChipSumCarry(p_wgT0, a_wgT0), _PairCarry(p_wuT0)])
    (dproj, d_lng, d_lnb, d_wsp, d_bsp3, d_cw), ((r_wuT0,), (a_wd0,)) = _call(
        _even_core_bwd(proj, dx1, even_v_ln_g, even_v_ln_b, wsp_b, bspb, cw_full, g_woe, tm),
        [_ChipSumCarry(p_wuT0, a_wuT0), _PairCarry(p_wd0)])
    small_names = ["norm_mix", "norm_ffn", "even_v_ln_g", "even_v_ln_b", "even_w_spatial", "even_b_spatial",
                   "even_conv_w", "attn_sink", "rel_bias", "final_norm"]
    small_parts = [jnp.concatenate([jnp.zeros_like(d_nmix1), d_nmix1]), jnp.concatenate([d_nffn0, d_nffn1]),
                   d_lng, d_lnb, d_wsp[None].astype(BF16), jnp.sum(d_bsp3, axis=-1)[None], d_cw,
                   dsink[:, 0:N_HEADS], jnp.zeros_like(rel_bias), d_final, loss_part]
    (p_winT,), (r_wd0,) = _call(_wgrad(dproj, h0b, "wgrad_in"), _ChipSumCarry(p_wd0, a_wd0))
    (d_relb_t,), ((a_winT,), small_slots) = _call(
        _rel_bias_grad(dbias, buckets), [_PairCarry(p_winT), _BroadcastCarry(small_parts)])
    d_relb = d_relb_t[:, 0:N_BUCKETS].T
    (dx0, _, d_nmix0), (r_winT,) = _call(
        _proj_bwd_norm([dproj], x0, row(norm_mix[0]), dx1, g_winT, "in_proj_bwd", tm),
        _ChipSumCarry(p_winT, a_winT))

    grads = {}

    order = ["norm_mix", "norm_ffn", "even_w_in", "even_v_ln_g", "even_v_ln_b", "even_w_spatial", "even_b_spatial",
             "even_conv_w", "even_w_out", "attn_w_qkv", "attn_sink", "rel_bias", "attn_w_out", "ffn_w_gate",
             "ffn_w_up", "ffn_w_down", "final_norm"]
    ws = dict(norm_mix=norm_mix, norm_ffn=norm_ffn, even_w_in=even_w_in, even_v_ln_g=even_v_ln_g,
              even_v_ln_b=even_v_ln_b, even_w_spatial=even_w_spatial, even_b_spatial=even_b_spatial,
              even_conv_w=even_conv_w, even_w_out=even_w_out, attn_w_qkv=attn_w_qkv, attn_sink=attn_sink,
              rel_bias=rel_bias, attn_w_out=attn_w_out, ffn_w_gate=ffn_w_gate, ffn_w_up=ffn_w_up,
              ffn_w_down=ffn_w_down, final_norm=final_norm)
    ms = dict(norm_mix=m_norm_mix, norm_ffn=m_norm_ffn, even_w_in=m_even_w_in, even_v_ln_g=m_even_v_ln_g,
              even_v_ln_b=m_even_v_ln_b, even_w_spatial=m_even_w_spatial, even_b_spatial=m_even_b_spatial,
              even_conv_w=m_even_conv_w, even_w_out=m_even_w_out, attn_w_qkv=m_attn_w_qkv, attn_sink=m_attn_sink,
              rel_bias=m_rel_bias, attn_w_out=m_attn_w_out, ffn_w_gate=m_ffn_w_gate, ffn_w_up=m_ffn_w_up,
              ffn_w_down=m_ffn_w_down, final_norm=m_final_norm)
    vs = dict(norm_mix=v_norm_mix, norm_ffn=v_norm_ffn, even_w_in=v_even_w_in, even_v_ln_g=v_even_v_ln_g,
              even_v_ln_b=v_even_v_ln_b, even_w_spatial=v_even_w_spatial, even_b_spatial=v_even_b_spatial,
              even_conv_w=v_even_conv_w, even_w_out=v_even_w_out, attn_w_qkv=v_attn_w_qkv, attn_sink=v_attn_sink,
              rel_bias=v_rel_bias, attn_w_out=v_attn_w_out, ffn_w_gate=v_ffn_w_gate, ffn_w_up=v_ffn_w_up,
              ffn_w_down=v_ffn_w_down, final_norm=v_final_norm)
    big = dict(ffn_w_gate=([r_wgT0, r_wgT1], True), even_w_in=([r_winT], True), even_w_out=([r_woe], False),
               attn_w_qkv=([r_wqkvT], True), attn_w_out=([r_woa], False), ffn_w_up=([r_wuT0, r_wuT1], True),
               ffn_w_down=([r_wd0, r_wd1], False))
    delta, new_m, new_v = {}, {}, {}
    late_slots = None
    for n, (recvs, transposed) in big.items():
        lay = (lambda a: jnp.swapaxes(a, 1, 2)) if transposed else (lambda a: a)
        spec = _finish_weight(recvs, lay(ws[n]), lay(ms[n]), lay(vs[n]), "finish_" + n)
        if late_slots is None:
            outs, late_slots = _call(spec, _BroadcastCarry([d_nmix0, d_relb]))
        else:
            outs, _ = _call(spec)
        grads[n], delta[n], new_m[n], new_v[n] = [lay(o) for o in outs]
    shaped = lambda n, a: in_full(a) if n == "even_conv_w" else (a.reshape(1, dm) if n == "final_norm" else a)
    pick = lambda dct: [shaped(n, dct[n]) for n in small_names]
    lates = [(small_names.index("norm_mix"), late_slots[0]), (small_names.index("rel_bias"), late_slots[1])]
    results, loss11 = _adamw_small(pick(ws), pick(ms), pick(vs), small_slots[:-1], lates, small_slots[-1],
                                   "adamw_small")
    mine = lambda a: lax.dynamic_slice(a, (0, me * (bw // NDEV)), (3, bw // NDEV))[None]
    for n, res in zip(small_names, results):
        for dst, a in zip((grads, delta, new_m, new_v), res):
            dst[n] = mine(a) if n == "even_conv_w" else (a.reshape(dm) if n == "final_norm" else a)
    loss = loss11[0, 0]
    return (loss, dx0[None], *[grads[n] for n in order], *[delta[n] for n in order],
            *[new_m[n] for n in order], *[new_v[n] for n in order])
```

```python
import math

import jax
import jax.numpy as jnp
import numpy as np
from jax import lax
from jax.experimental import pallas as pl
from jax.experimental.pallas import tpu as pltpu

F32, BF16 = jnp.float32, jnp.bfloat16
NDEV = 8
EPS = 1e-6
CHUNK = 128
A_GROUPS = 4
N_HEADS, N_KV, HEAD_DIM = 16, 4, 64
N_BUCKETS, MAX_DISTANCE = 32, 128
NEG = -1e30
LOG2E = 1.4426950408889634
ADAM_LR, ADAM_B1, ADAM_B2, ADAM_EPS, ADAM_WD, ADAM_STEP = 0.001, 0.9, 0.999, 1e-08, 0.01, 10
VMEM_LIMIT = 56 * 1024 * 1024
MESH = pl.DeviceIdType.MESH
NT = (((1,), (1,)), ((), ()))
NN = (((1,), (0,)), ((), ()))
TN = (((0,), (0,)), ((), ()))
ANY = pl.BlockSpec(memory_space=pl.ANY)


def _cp(n_grid=1):
    return pltpu.CompilerParams(dimension_semantics=("arbitrary",) * n_grid, vmem_limit_bytes=VMEM_LIMIT)


def _dot(a, b, dims):
    return lax.dot_general(a, b, dims, preferred_element_type=F32)


def _my_index():
    return 4 * lax.axis_index("x") + 2 * lax.axis_index("y") + lax.axis_index("c")


def _peer(k):
    x, y, c = lax.axis_index("x"), lax.axis_index("y"), lax.axis_index("c")
    px = 1 - x if k & 4 else x
    py = 1 - y if k & 2 else y
    pc = 1 - c if k & 1 else c
    return (px, py, pc)


def _load_weight(gath_ref, wbuf, sems):
    rows = gath_ref.shape[1]
    cps = [pltpu.make_async_copy(gath_ref.at[d], wbuf.at[pl.ds(d * rows, rows), :], sems.at[d]) for d in range(NDEV)]
    for c in cps:
        c.start()
    for c in cps:
        c.wait()


class _GatherCarry:
    def __init__(self, pieces):
        self.inputs = list(pieces)
        self.n = len(pieces)
        self.out_shape = [jax.ShapeDtypeStruct((NDEV,) + p.shape, p.dtype) for p in pieces]
        self.scratch = [pltpu.SemaphoreType.DMA((7 * self.n,)), pltpu.SemaphoreType.DMA((7 * self.n,)),
                        pltpu.SemaphoreType.DMA((self.n,))]

    def _ctx(self):
        x, y, c = lax.axis_index("x"), lax.axis_index("y"), lax.axis_index("c")
        chips = [(1 - x, y), (x, 1 - y), (1 - x, 1 - y)]
        return (x, y, c), (x, y, 1 - c), chips, c

    def _copy(self, k, j, block, to, ins, outs, sems, src=None):
        send_sems, recv_sems, _ = sems
        slot = outs[j].at[4 * block[0] + 2 * block[1] + block[2]]
        return pltpu.make_async_remote_copy(
            src_ref=slot if src is None else src, dst_ref=slot, send_sem=send_sems.at[k * self.n + j],
            recv_sem=recv_sems.at[k * self.n + j], device_id=to, device_id_type=MESH)

    def start(self, ins, outs, sems):
        me, sibling, chips, c = self._ctx()
        for j in range(self.n):
            pltpu.make_async_copy(ins[j], outs[j].at[4 * me[0] + 2 * me[1] + me[2]], sems[2].at[j]).start()
            self._copy(0, j, me, sibling, ins, outs, sems, src=ins[j]).start()
            for q, chip in enumerate(chips):
                self._copy(1 + q, j, me, (*chip, c), ins, outs, sems, src=ins[j]).start()

    def mid(self, ins, outs, sems):
        me, sibling, chips, c = self._ctx()
        for q, chip in enumerate(chips):
            for j in range(self.n):
                self._copy(1 + q, j, (*chip, c), me, ins, outs, sems).wait_recv()
                self._copy(4 + q, j, (*chip, c), sibling, ins, outs, sems).start()

    def finish(self, ins, outs, sems):
        me, sibling, chips, c = self._ctx()
        for j in range(self.n):
            self._copy(0, j, sibling, me, ins, outs, sems).wait_recv()
            for q, chip in enumerate(chips):
                self._copy(4 + q, j, (*chip, 1 - c), me, ins, outs, sems).wait_recv()
        for j in range(self.n):
            self._copy(0, j, me, sibling, ins, outs, sems, src=ins[j]).wait_send()
            for q, chip in enumerate(chips):
                self._copy(1 + q, j, me, (*chip, c), ins, outs, sems, src=ins[j]).wait_send()
                self._copy(4 + q, j, (*chip, c), sibling, ins, outs, sems).wait_send()
            pltpu.make_async_copy(ins[j], outs[j].at[0], sems[2].at[j]).wait()


class _GradCarry:
    def __init__(self, pieces):
        self.inputs = list(pieces)
        self.n = len(pieces)
        self.rows = [p.shape[0] // NDEV for p in pieces]
        self.out_shape = [jax.ShapeDtypeStruct((NDEV, r, p.shape[1]), p.dtype) for p, r in zip(pieces, self.rows)]
        self.scratch = [pltpu.SemaphoreType.DMA((7 * self.n,)), pltpu.SemaphoreType.DMA((7 * self.n,)),
                        pltpu.SemaphoreType.DMA((self.n,))]

    def _copies(self, ins, outs, sems):
        me = _my_index()
        local, remote = [], []
        for j in range(self.n):
            r = self.rows[j]
            local.append(pltpu.make_async_copy(ins[j].at[pl.ds(pl.multiple_of(me * r, 16), r), :], outs[j].at[me],
                                               sems[2].at[j]))
            for k in range(1, NDEV):
                peer = _peer(k)
                pidx = 4 * peer[0] + 2 * peer[1] + peer[2]
                remote.append(pltpu.make_async_remote_copy(
                    src_ref=ins[j].at[pl.ds(pl.multiple_of(pidx * r, 16), r), :], dst_ref=outs[j].at[me],
                    send_sem=sems[0].at[(k - 1) * self.n + j], recv_sem=sems[1].at[(k - 1) * self.n + j],
                    device_id=peer, device_id_type=MESH))
        return local, remote

    def start(self, ins, outs, sems):
        local, remote = self._copies(ins, outs, sems)
        for cp in local + remote:
            cp.start()

    def mid(self, ins, outs, sems):
        pass

    def finish(self, ins, outs, sems):
        local, remote = self._copies(ins, outs, sems)
        for cp in remote + local:
            cp.wait()


class _BroadcastCarry:
    def __init__(self, parts):
        self.inputs = list(parts)
        self.n = len(self.inputs)
        self.out_shape = [jax.ShapeDtypeStruct((NDEV,) + p.shape, p.dtype) for p in self.inputs]
        self.scratch = [pltpu.SemaphoreType.DMA((7 * self.n,)), pltpu.SemaphoreType.DMA((7 * self.n,)),
                        pltpu.SemaphoreType.DMA((self.n,))]

    def _copies(self, ins, outs, sems):
        me = _my_index()
        cps = []
        for j in range(self.n):
            cps.append(pltpu.make_async_copy(ins[j], outs[j].at[me], sems[2].at[j]))
            cps += [pltpu.make_async_remote_copy(
                src_ref=ins[j], dst_ref=outs[j].at[me], send_sem=sems[0].at[(k - 1) * self.n + j],
                recv_sem=sems[1].at[(k - 1) * self.n + j], device_id=_peer(k), device_id_type=MESH)
                for k in range(1, NDEV)]
        return cps

    def start(self, ins, outs, sems):
        for cp in self._copies(ins, outs, sems):
            cp.start()

    def mid(self, ins, outs, sems):
        pass

    def finish(self, ins, outs, sems):
        for cp in self._copies(ins, outs, sems):
            cp.wait()


class _PairCarry:
    def __init__(self, piece):
        self.inputs = [piece]
        self.r = piece.shape[0] // NDEV
        self.out_shape = [jax.ShapeDtypeStruct((4, self.r, piece.shape[1]), piece.dtype)]
        self.scratch = [pltpu.SemaphoreType.DMA((4,)), pltpu.SemaphoreType.DMA((4,))]

    def _copies(self, ins, outs, sems):
        x, y, c = lax.axis_index("x"), lax.axis_index("y"), lax.axis_index("c")
        return [pltpu.make_async_remote_copy(
            src_ref=ins[0].at[pl.ds(pl.multiple_of((2 * q + 1 - c) * self.r, 16), self.r), :], dst_ref=outs[0].at[q],
            send_sem=sems[0].at[q], recv_sem=sems[1].at[q], device_id=(x, y, 1 - c), device_id_type=MESH)
            for q in range(4)]

    def start(self, ins, outs, sems):
        for cp in self._copies(ins, outs, sems):
            cp.start()

    def mid(self, ins, outs, sems):
        pass

    def finish(self, ins, outs, sems):
        for cp in self._copies(ins, outs, sems):
            cp.wait()


class _ChipSumCarry:
    def __init__(self, piece, landed):
        self.inputs = [piece, landed]
        self.r, dm = piece.shape[0] // NDEV, piece.shape[1]
        self.out_shape = [jax.ShapeDtypeStruct((4, self.r, dm), piece.dtype)]
        self.scratch = [pltpu.VMEM((4, self.r, dm), piece.dtype), pltpu.VMEM((8, self.r, dm), piece.dtype),
                        pltpu.SemaphoreType.DMA((8,)), pltpu.SemaphoreType.DMA((3,)), pltpu.SemaphoreType.DMA((3,)),
                        pltpu.SemaphoreType.DMA(())]

    def _copies(self, outs, scr):
        sums, _, _, send_sems, recv_sems, local_sem = scr
        x, y, c = lax.axis_index("x"), lax.axis_index("y"), lax.axis_index("c")
        mine = 2 * x + y
        local = pltpu.make_async_copy(sums.at[mine], outs[0].at[mine], local_sem)
        remote = []
        for k in range(1, 4):
            px = 1 - x if k & 2 else x
            py = 1 - y if k & 1 else y
            remote.append(pltpu.make_async_remote_copy(
                src_ref=sums.at[2 * px + py], dst_ref=outs[0].at[mine], send_sem=send_sems.at[k - 1],
                recv_sem=recv_sems.at[k - 1], device_id=(px, py, c), device_id_type=MESH))
        return local, remote

    def start(self, ins, outs, scr):
        sums, stage, stage_sems = scr[0], scr[1], scr[2]
        c = lax.axis_index("c")
        loads = []
        for q in range(4):
            loads.append((
                pltpu.make_async_copy(ins[0].at[pl.ds(pl.multiple_of((2 * q + c) * self.r, 16), self.r), :],
                                      stage.at[2 * q], stage_sems.at[2 * q]),
                pltpu.make_async_copy(ins[1].at[q], stage.at[2 * q + 1], stage_sems.at[2 * q + 1])))
        for a, b in loads:
            a.start()
            b.start()
        for q, (a, b) in enumerate(loads):
            a.wait()
            b.wait()
            sums[q] = (stage[2 * q].astype(F32) + stage[2 * q + 1].astype(F32)).astype(sums.dtype)
        local, remote = self._copies(outs, scr)
        for cp in [local] + remote:
            cp.start()

    def mid(self, ins, outs, scr):
        pass

    def finish(self, ins, outs, scr):
        local, remote = self._copies(outs, scr)
        for cp in remote + [local]:
            cp.wait()


def _call(spec, carry=None):
    body, grid = spec["body"], spec["grid"]
    in_specs, out_specs, out_shape = list(spec["in_specs"]), list(spec["out_specs"]), list(spec["out_shape"])
    scratch, args = list(spec.get("scratch", [])), list(spec["args"])
    if carry is None:
        out = pl.pallas_call(body, grid=grid, in_specs=in_specs, out_specs=tuple(out_specs),
                             out_shape=tuple(out_shape), scratch_shapes=scratch, compiler_params=_cp(len(grid)),
                             name=spec["name"])(*args)
        return tuple(out), ()
    carries = list(carry) if isinstance(carry, (list, tuple)) else [carry]
    n_in, n_out, n_s = len(in_specs), len(out_specs), len(scratch)
    steps = int(np.prod(grid))

    def split(refs, counts):
        parts, o = [], 0
        for cnt in counts:
            parts.append(refs[o:o + cnt])
            o += cnt
        return parts

    c_in = [len(cr.inputs) for cr in carries]
    c_out = [len(cr.out_shape) for cr in carries]
    c_scr = [len(cr.scratch) for cr in carries]

    def wrapped(*refs):
        ins, cins, outs, couts, scr, cscr = split(refs, [n_in, sum(c_in), n_out, sum(c_out), n_s, sum(c_scr)])
        per = list(zip(carries, split(cins, c_in), split(couts, c_out), split(cscr, c_scr)))
        step = pl.program_id(0)
        for ax in range(1, len(grid)):
            step = step * grid[ax] + pl.program_id(ax)

        @pl.when(step == 0)
        def _():
            for cr, ci, co, cs in per:
                cr.start(ci, co, cs)
        if steps >= 3:
            @pl.when(step == steps - 2)
            def _():
                for cr, ci, co, cs in per:
                    cr.mid(ci, co, cs)
        body(*ins, *outs, *scr)

        @pl.when(step == steps - 1)
        def _():
            for cr, ci, co, cs in per:
                if steps < 3:
                    cr.mid(ci, co, cs)
                cr.finish(ci, co, cs)

    out = pl.pallas_call(
        wrapped, grid=grid, in_specs=in_specs + [ANY] * sum(c_in), out_specs=tuple(out_specs + [ANY] * sum(c_out)),
        out_shape=tuple(out_shape + [s for cr in carries for s in cr.out_shape]),
        scratch_shapes=scratch + [s for cr in carries for s in cr.scratch],
        compiler_params=_cp(len(grid)), name=spec["name"])(*args, *[a for cr in carries for a in cr.inputs])
    c_res = [tuple(p) for p in split(out[n_out:], c_out)]
    return tuple(out[:n_out]), (c_res if isinstance(carry, (list, tuple)) else c_res[0])


def _rms_fwd(x, gain):
    r = lax.rsqrt(jnp.mean(x * x, axis=-1, keepdims=True) + EPS)
    return x * r * gain, r


def _rms_bwd(dh, x, r, gain):
    a = dh * gain
    dx = r * a - x * (r * r * r) * jnp.mean(a * x, axis=-1, keepdims=True)
    dgain = jnp.sum(dh * (x * r), axis=0, keepdims=True)
    return dx, dgain


def _gelu(x):
    return 0.5 * x * (1.0 + lax.erf(x * 0.7071067811865476))


def _gelu_grad(x):
    return 0.5 * (1.0 + lax.erf(x * 0.7071067811865476)) + x * jnp.exp(-0.5 * x * x) * 0.3989422804014327


def _sigmoid(x):
    return 1.0 / (1.0 + jnp.exp(-x))


def _adamw_math(w, g, m, v):
    nm = ADAM_B1 * m + (1.0 - ADAM_B1) * g
    nv = ADAM_B2 * v + (1.0 - ADAM_B2) * (g * g)
    m_hat = nm / (1.0 - ADAM_B1 ** ADAM_STEP)
    v_hat = nv / (1.0 - ADAM_B2 ** ADAM_STEP)
    return -ADAM_LR * (m_hat / (jnp.sqrt(v_hat) + ADAM_EPS) + ADAM_WD * w), nm, nv


def _tok(tm, w):
    return pl.BlockSpec((tm, w), lambda i: (i, 0))


def _full(shape):
    return pl.BlockSpec(shape, lambda *i: (0,) * len(shape))


def _norm_proj(x, gain, gath, out_dtype, name, tm):
    t, dm = x.shape
    n = gath.shape[1] * NDEV

    def body(x_ref, g_ref, gath_ref, proj_ref, hb_ref, wbuf, sems):
        @pl.when(pl.program_id(0) == 0)
        def _():
            _load_weight(gath_ref, wbuf, sems)
        h, _ = _rms_fwd(x_ref[...], g_ref[...])
        hb = h.astype(BF16)
        hb_ref[...] = hb
        proj_ref[...] = _dot(hb, wbuf[...], NT).astype(out_dtype)

    return dict(
        body=body, grid=(t // tm,), name=name, args=[x, gain, gath],
        out_shape=[jax.ShapeDtypeStruct((t, n), out_dtype), jax.ShapeDtypeStruct((t, dm), BF16)],
        in_specs=[_tok(tm, dm), _full((1, dm)), ANY], out_specs=[_tok(tm, n), _tok(tm, dm)],
        scratch=[pltpu.VMEM((n, dm), BF16), pltpu.SemaphoreType.DMA((NDEV,))])


def _proj_bwd_norm(dys, x, gain, dres, gath, name, tm):
    t, dm = x.shape
    n = gath.shape[1] * NDEV
    widths = [d.shape[1] for d in dys]
    assert sum(widths) == n
    nd = len(dys)

    def body(*refs):
        dy_refs = refs[:nd]
        x_ref, g_ref, dres_ref, gath_ref, dx_ref, dxb_ref, dgain_ref, wbuf, sems = refs[nd:]

        @pl.when(pl.program_id(0) == 0)
        def _():
            _load_weight(gath_ref, wbuf, sems)
            dgain_ref[...] = jnp.zeros_like(dgain_ref)
        xv, gain_v = x_ref[...], g_ref[...]
        _, r = _rms_fwd(xv, gain_v)
        dh, c0 = None, 0
        for dy_ref, wd in zip(dy_refs, widths):
            part = _dot(dy_ref[...], wbuf[c0:c0 + wd, :], NN)
            dh = part if dh is None else dh + part
            c0 += wd
        dx, dgain = _rms_bwd(dh, xv, r, gain_v)
        dx = dres_ref[...] + dx
        dx_ref[...] = dx
        dxb_ref[...] = dx.astype(BF16)
        dgain_ref[...] += dgain

    return dict(
        body=body, grid=(t // tm,), name=name, args=[*dys, x, gain, dres, gath],
        out_shape=[jax.ShapeDtypeStruct((t, dm), F32), jax.ShapeDtypeStruct((t, dm), BF16),
                   jax.ShapeDtypeStruct((1, dm), F32)],
        in_specs=[_tok(tm, wd) for wd in widths] + [_tok(tm, dm), _full((1, dm)), _tok(tm, dm), ANY],
        out_specs=[_tok(tm, dm), _tok(tm, dm), _full((1, dm))],
        scratch=[pltpu.VMEM((n, dm), BF16), pltpu.SemaphoreType.DMA((NDEV,))])


def _wgrad(a, b, name, tmm=256):
    parts = list(a) if isinstance(a, (list, tuple)) else [a]
    t = parts[0].shape[0]
    n = b.shape[1]
    tiles = [p.shape[1] // tmm for p in parts]
    first = [sum(tiles[:i]) for i in range(len(parts))]
    m = sum(tiles) * tmm

    def body(*refs):
        a_refs, b_ref, o_ref = refs[:len(parts)], refs[len(parts)], refs[len(parts) + 1]
        j = pl.program_id(0)
        for a_ref, j0, nt in zip(a_refs, first, tiles):
            if len(parts) == 1:
                o_ref[...] = _dot(a_ref[...], b_ref[...], TN).astype(BF16)
            else:
                @pl.when((j >= j0) & (j < j0 + nt))
                def _():
                    o_ref[...] = _dot(a_ref[...], b_ref[...], TN).astype(BF16)

    a_specs = [pl.BlockSpec((t, tmm), lambda j, j0=j0, nt=nt: (0, jnp.clip(j - j0, 0, nt - 1)))
               for j0, nt in zip(first, tiles)]
    return dict(
        body=body, grid=(sum(tiles),), name=name, args=[*parts, b], out_shape=[jax.ShapeDtypeStruct((m, n), BF16)],
        in_specs=a_specs + [pl.BlockSpec((t, n), lambda j: (0, 0))],
        out_specs=[pl.BlockSpec((tmm, n), lambda j: (j, 0))])


def _halo_specs(tm, t, width, col_blocks):
    nb8 = tm // 8
    last = t // 8 - 1
    prev = [pl.BlockSpec((8, width), lambda i, cb=cb: (jnp.maximum(i * nb8 - 1, 0), cb)) for cb in col_blocks]
    nxt = [pl.BlockSpec((8, width), lambda i, cb=cb: (jnp.minimum((i + 1) * nb8, last), cb)) for cb in col_blocks]
    return prev, nxt


def _shift_rows(z, prev_row, next_row):
    tm = z.shape[0]
    row = lax.broadcasted_iota(jnp.int32, z.shape, 0)
    zm1 = jnp.where(row == 0, prev_row, pltpu.roll(z, 1, 0))
    zp1 = jnp.where(row == tm - 1, next_row, pltpu.roll(z, tm - 1, 0))
    return zm1, zp1


def _gating_fwd(proj, lng, lnb, wsp_ref, bsp_ref, aw):
    tm = proj.shape[0]
    a_u = _gelu(proj[:, 0:aw])
    gv = _gelu(proj[:, aw:2 * aw])
    mu = jnp.mean(gv, axis=-1, keepdims=True)
    xc = gv - mu
    rstd = lax.rsqrt(jnp.mean(xc * xc, axis=-1, keepdims=True) + EPS)
    vn = xc * rstd
    a_v = (vn * lng + lnb).astype(BF16)
    gd = aw // A_GROUPS
    rows = []
    for c in range(tm // CHUNK):
        cols = []
        for g in range(A_GROUPS):
            blk = a_v[c * CHUNK:(c + 1) * CHUNK, g * gd:(g + 1) * gd]
            cols.append(_dot(wsp_ref[g], blk, NN) + bsp_ref[g])
        rows.append(jnp.concatenate(cols, axis=1))
    mixed = jnp.concatenate(rows, axis=0)
    return a_u, vn, rstd, a_v, mixed


def _even_core_fwd(proj, x0, lng, lnb, wsp, bspb, cw, gath, tm):
    t, dm = x0.shape
    aw = lng.shape[1]
    bw = cw.shape[1]
    assert aw == bw and 2 * aw + 3 * bw == proj.shape[1]
    nt = t // tm
    prev, nxt = _halo_specs(tm, t, bw, [3, 4])

    def body(proj_ref, cp_ref, hp_ref, cn_ref, hn_ref, x0_ref, lng_ref, lnb_ref, wsp_ref, bsp_ref, cw_ref, gath_ref,
             x1_ref, y_ref, wbuf, sems):
        i = pl.program_id(0)

        @pl.when(i == 0)
        def _():
            _load_weight(gath_ref, wbuf, sems)
        proj_v = proj_ref[...]
        a_u, _, _, _, mixed = _gating_fwd(proj_v, lng_ref[...], lnb_ref[...], wsp_ref, bsp_ref, aw)
        a_out = a_u * mixed
        bb = proj_v[:, 2 * aw:2 * aw + bw]
        z = proj_v[:, 2 * aw + bw:2 * aw + 2 * bw] * proj_v[:, 2 * aw + 2 * bw:]
        zprev = jnp.where(i > 0, cp_ref[7:8, :] * hp_ref[7:8, :], 0.0)
        znext = jnp.where(i < nt - 1, cn_ref[0:1, :] * hn_ref[0:1, :], 0.0)
        zm1, zp1 = _shift_rows(z, zprev, znext)
        cwv = cw_ref[...]
        conv = zm1 * cwv[0:1, :] + z * cwv[1:2, :] + zp1 * cwv[2:3, :]
        y = jnp.concatenate([a_out, bb * conv], axis=1).astype(BF16)
        y_ref[...] = y
        x1_ref[...] = x0_ref[...] + _dot(y, wbuf[...], NN)

    return dict(
        body=body, grid=(nt,), name="even_core_fwd",
        args=[proj, proj, proj, proj, proj, x0, lng, lnb, wsp, bspb, cw, gath],
        out_shape=[jax.ShapeDtypeStruct((t, dm), F32), jax.ShapeDtypeStruct((t, aw + bw), BF16)],
        in_specs=[_tok(tm, proj.shape[1]), prev[0], prev[1], nxt[0], nxt[1], _tok(tm, dm), _full(lng.shape),
                  _full(lnb.shape), _full(wsp.shape), _full(bspb.shape), _full(cw.shape), ANY],
        out_specs=[_tok(tm, dm), _tok(tm, aw + bw)],
        scratch=[pltpu.VMEM((gath.shape[1] * NDEV, dm), BF16), pltpu.SemaphoreType.DMA((NDEV,))])


def _even_core_bwd(proj, dx1, lng, lnb, wsp, bspb, cw, gath, tm):
    t, dm = dx1.shape
    aw, bw = lng.shape[1], cw.shape[1]
    gd = aw // A_GROUPS
    nt = t // tm
    inw = proj.shape[1]
    prev, nxt = _halo_specs(tm, t, bw, [2, 3, 4])
    nb8 = tm // 8
    last8 = t // 8 - 1

    def body(proj_ref, bp_ref, cp_ref, hp_ref, bn_ref, cn_ref, hn_ref, dx_ref, dxp_ref, dxn_ref,
             lng_ref, lnb_ref, wsp_ref, bsp_ref, cw_ref, gath_ref,
             dproj_ref, dlng_ref, dlnb_ref, dwsp_ref, dbsp_ref, dcw_ref, wbuf, sems):
        i = pl.program_id(0)

        @pl.when(i == 0)
        def _():
            _load_weight(gath_ref, wbuf, sems)
            dlng_ref[...] = jnp.zeros_like(dlng_ref)
            dlnb_ref[...] = jnp.zeros_like(dlnb_ref)
            dwsp_ref[...] = jnp.zeros_like(dwsp_ref)
            dbsp_ref[...] = jnp.zeros_like(dbsp_ref)
            dcw_ref[...] = jnp.zeros_like(dcw_ref)
        proj_v = proj_ref[...]
        lng_v = lng_ref[...]
        a_u, vn, rstd, a_v, mixed = _gating_fwd(proj_v, lng_v, lnb_ref[...], wsp_ref, bsp_ref, aw)
        w = wbuf[...]
        dy = _dot(dx_ref[...].astype(BF16), w, NT)
        da_out, db_out = dy[:, 0:aw], dy[:, aw:]
        da_u = da_out * mixed
        dmixed = da_out * a_u
        dmb = dmixed.astype(BF16)
        rows = []
        for c in range(tm // CHUNK):
            cols = []
            for g in range(A_GROUPS):
                r0, c0 = c * CHUNK, g * gd
                dm_cg = dmb[r0:r0 + CHUNK, c0:c0 + gd]
                cols.append(_dot(wsp_ref[g], dm_cg, TN))
                dwsp_ref[g] += _dot(dm_cg, a_v[r0:r0 + CHUNK, c0:c0 + gd], NT)
                dbsp_ref[g] += dmixed[r0:r0 + CHUNK, c0:c0 + gd]
            rows.append(jnp.concatenate(cols, axis=1))
        dav = jnp.concatenate(rows, axis=0)
        dlng_ref[...] += jnp.sum(dav * vn, axis=0, keepdims=True)
        dlnb_ref[...] += jnp.sum(dav, axis=0, keepdims=True)
        dvn = dav * lng_v
        dgv = rstd * (dvn - jnp.mean(dvn, axis=-1, keepdims=True) - vn * jnp.mean(dvn * vn, axis=-1, keepdims=True))
        dv_pre = dgv * _gelu_grad(proj_v[:, aw:2 * aw])
        du_pre = da_u * _gelu_grad(proj_v[:, 0:aw])
        bb = proj_v[:, 2 * aw:2 * aw + bw]
        bc = proj_v[:, 2 * aw + bw:2 * aw + 2 * bw]
        bh = proj_v[:, 2 * aw + 2 * bw:]
        z = bc * bh
        zprev = jnp.where(i > 0, cp_ref[7:8, :] * hp_ref[7:8, :], 0.0)
        znext = jnp.where(i < nt - 1, cn_ref[0:1, :] * hn_ref[0:1, :], 0.0)
        zm1, zp1 = _shift_rows(z, zprev, znext)
        cwv = cw_ref[...]
        conv = zm1 * cwv[0:1, :] + z * cwv[1:2, :] + zp1 * cwv[2:3, :]
        dbb = db_out * conv
        dconv = db_out * bb
        dx_edge = jnp.concatenate([dxp_ref[...], dxn_ref[...]], axis=0).astype(BF16)
        dy_edge = _dot(dx_edge, w[aw:, :], NT)
        dcprev = jnp.where(i > 0, dy_edge[7:8, :] * bp_ref[7:8, :], 0.0)
        dcnext = jnp.where(i < nt - 1, dy_edge[8:9, :] * bn_ref[0:1, :], 0.0)
        dcm1, dcp1 = _shift_rows(dconv, dcprev, dcnext)
        dz = dcp1 * cwv[0:1, :] + dconv * cwv[1:2, :] + dcm1 * cwv[2:3, :]
        dcw_ref[0:1, :] += jnp.sum(dconv * zm1, axis=0, keepdims=True)
        dcw_ref[1:2, :] += jnp.sum(dconv * z, axis=0, keepdims=True)
        dcw_ref[2:3, :] += jnp.sum(dconv * zp1, axis=0, keepdims=True)
        dproj_ref[...] = jnp.concatenate([du_pre, dv_pre, dbb, dz * bh, dz * bc], axis=1).astype(BF16)

    row8 = lambda f: pl.BlockSpec((8, dm), f)
    return dict(
        body=body, grid=(nt,), name="even_core_bwd",
        args=[proj, proj, proj, proj, proj, proj, proj, dx1, dx1, dx1, lng, lnb, wsp, bspb, cw, gath],
        out_shape=[jax.ShapeDtypeStruct((t, inw), BF16), jax.ShapeDtypeStruct((1, aw), F32),
                   jax.ShapeDtypeStruct((1, aw), F32), jax.ShapeDtypeStruct(wsp.shape, F32),
                   jax.ShapeDtypeStruct((A_GROUPS, CHUNK, gd), F32), jax.ShapeDtypeStruct(cw.shape, F32)],
        in_specs=[_tok(tm, inw), prev[0], prev[1], prev[2], nxt[0], nxt[1], nxt[2], _tok(tm, dm),
                  row8(lambda i: (jnp.maximum(i * nb8 - 1, 0), 0)), row8(lambda i: (jnp.minimum((i + 1) * nb8, last8), 0)),
                  _full(lng.shape), _full(lnb.shape), _full(wsp.shape), _full(bspb.shape), _full(cw.shape), ANY],
        out_specs=[_tok(tm, inw), _full((1, aw)), _full((1, aw)), _full(wsp.shape),
                   _full((A_GROUPS, CHUNK, gd)), _full(cw.shape)],
        scratch=[pltpu.VMEM((gath.shape[1] * NDEV, dm), BF16), pltpu.SemaphoreType.DMA((NDEV,))])


def _ff_chunks(f, width=1024):
    return [(c0, min(c0 + width, f)) for c0 in range(0, f, width)]


def _ffn_up(x, gain, gath_g, gath_u, name, tm):
    t, dm = x.shape
    f = gath_g.shape[1] * NDEV

    def body(x_ref, g_ref, gg_ref, gu_ref, gate_ref, up_ref, act_ref, wg, wu, sems):
        @pl.when(pl.program_id(0) == 0)
        def _():
            _load_weight(gg_ref, wg, sems)
            _load_weight(gu_ref, wu, sems)
        h, _ = _rms_fwd(x_ref[...], g_ref[...])
        hb = h.astype(BF16)
        for c0, c1 in _ff_chunks(f):
            gate = _dot(hb, wg[c0:c1, :], NT)
            up = _dot(hb, wu[c0:c1, :], NT)
            gate_ref[:, c0:c1] = gate.astype(BF16)
            up_ref[:, c0:c1] = up.astype(BF16)
            act_ref[:, c0:c1] = (gate * _sigmoid(gate) * up).astype(BF16)

    o = jax.ShapeDtypeStruct((t, f), BF16)
    return dict(
        body=body, grid=(t // tm,), name=name, args=[x, gain, gath_g, gath_u], out_shape=[o, o, o],
        in_specs=[_tok(tm, dm), _full((1, dm)), ANY, ANY], out_specs=[_tok(tm, f)] * 3,
        scratch=[pltpu.VMEM((f, dm), BF16), pltpu.VMEM((f, dm), BF16), pltpu.SemaphoreType.DMA((NDEV,))])


def _ffn_down(x, act, gath_d, name, tm):
    t, dm = x.shape
    f = act.shape[1]

    def body(x_ref, a_ref, gd_ref, xo_ref, wd, sems):
        @pl.when(pl.program_id(0) == 0)
        def _():
            _load_weight(gd_ref, wd, sems)
        xo_ref[...] = x_ref[...] + _dot(a_ref[...], wd[...], NN)

    return dict(
        body=body, grid=(t // tm,), name=name, args=[x, act, gath_d], out_shape=[jax.ShapeDtypeStruct((t, dm), F32)],
        in_specs=[_tok(tm, dm), _tok(tm, f), ANY], out_specs=[_tok(tm, dm)],
        scratch=[pltpu.VMEM((f, dm), BF16), pltpu.SemaphoreType.DMA((NDEV,))])


def _ffn_down_loss(x, act, gath_d, target, gain, name, tm):
    t, dm = x.shape
    f = act.shape[1]
    steps = t // tm

    def body(x_ref, a_ref, gd_ref, t_ref, g_ref, loss_ref, dx_ref, dxb_ref, dgain_ref, wd, acc, sems):
        i = pl.program_id(0)

        @pl.when(i == 0)
        def _():
            _load_weight(gd_ref, wd, sems)
            acc[...] = jnp.zeros_like(acc)
            dgain_ref[...] = jnp.zeros_like(dgain_ref)
        xv = x_ref[...] + _dot(a_ref[...], wd[...], NN)
        gain_v = g_ref[...]
        y, r = _rms_fwd(xv, gain_v)
        e = y - t_ref[...]
        acc[...] += jnp.sum(e * e, axis=0, keepdims=True)
        dx, dgain = _rms_bwd(e * (1.0 / dm), xv, r, gain_v)
        dx_ref[...] = dx
        dxb_ref[...] = dx.astype(BF16)
        dgain_ref[...] += dgain

        @pl.when(i == steps - 1)
        def _():
            loss_ref[...] = jnp.sum(acc[...], axis=-1, keepdims=True) * (0.5 / dm)

    return dict(
        body=body, grid=(steps,), name=name, args=[x, act, gath_d, target, gain],
        out_shape=[jax.ShapeDtypeStruct((1, 1), F32), jax.ShapeDtypeStruct((t, dm), F32),
                   jax.ShapeDtypeStruct((t, dm), BF16), jax.ShapeDtypeStruct((1, dm), F32)],
        in_specs=[_tok(tm, dm), _tok(tm, f), ANY, _tok(tm, dm), _full((1, dm))],
        out_specs=[_full((1, 1)), _tok(tm, dm), _tok(tm, dm), _full((1, dm))],
        scratch=[pltpu.VMEM((f, dm), BF16), pltpu.VMEM((1, dm), F32), pltpu.SemaphoreType.DMA((NDEV,))])


def _ffn_bwd(dxo, x, gate, up, gain, gath_g, gath_u, gath_d, name, tm):
    t, dm = x.shape
    f = gate.shape[1]

    def body(dxo_ref, x_ref, gate_ref, up_ref, g_ref, gg_ref, gu_ref, gd_ref,
             dx_ref, dxb_ref, dg_ref, du_ref, hb_ref, dgain_ref, wg, wu, wd, sems):
        @pl.when(pl.program_id(0) == 0)
        def _():
            _load_weight(gg_ref, wg, sems)
            _load_weight(gu_ref, wu, sems)
            _load_weight(gd_ref, wd, sems)
            dgain_ref[...] = jnp.zeros_like(dgain_ref)
        xv, gain_v, dxo_v = x_ref[...], g_ref[...], dxo_ref[...]
        h, r = _rms_fwd(xv, gain_v)
        hb_ref[...] = h.astype(BF16)
        dxob = dxo_v.astype(BF16)
        dh = jnp.zeros_like(xv)
        for c0, c1 in _ff_chunks(f):
            gate_v = gate_ref[:, c0:c1].astype(F32)
            up_v = up_ref[:, c0:c1].astype(F32)
            s = _sigmoid(gate_v)
            silu = gate_v * s
            dact = _dot(dxob, wd[c0:c1, :], NT)
            dg = (dact * up_v * (s * (1.0 + gate_v * (1.0 - s)))).astype(BF16)
            du = (dact * silu).astype(BF16)
            dg_ref[:, c0:c1] = dg
            du_ref[:, c0:c1] = du
            dh = dh + _dot(dg, wg[c0:c1, :], NN) + _dot(du, wu[c0:c1, :], NN)
        dx, dgain = _rms_bwd(dh, xv, r, gain_v)
        dx = dxo_v + dx
        dx_ref[...] = dx
        dxb_ref[...] = dx.astype(BF16)
        dgain_ref[...] += dgain

    return dict(
        body=body, grid=(t // tm,), name=name, args=[dxo, x, gate, up, gain, gath_g, gath_u, gath_d],
        out_shape=[jax.ShapeDtypeStruct((t, dm), F32), jax.ShapeDtypeStruct((t, dm), BF16),
                   jax.ShapeDtypeStruct((t, f), BF16), jax.ShapeDtypeStruct((t, f), BF16),
                   jax.ShapeDtypeStruct((t, dm), BF16), jax.ShapeDtypeStruct((1, dm), F32)],
        in_specs=[_tok(tm, dm), _tok(tm, dm), _tok(tm, f), _tok(tm, f), _full((1, dm)), ANY, ANY, ANY],
        out_specs=[_tok(tm, dm), _tok(tm, dm), _tok(tm, f), _tok(tm, f), _tok(tm, dm), _full((1, dm))],
        scratch=[pltpu.VMEM((f, dm), BF16), pltpu.VMEM((f, dm), BF16), pltpu.VMEM((f, dm), BF16),
                 pltpu.SemaphoreType.DMA((NDEV,))])


def _t5_buckets(rel):
    nb = N_BUCKETS // 2
    ret = jnp.where(rel > 0, nb, 0)
    n = jnp.abs(rel)
    max_exact = nb // 2
    nf = jnp.maximum(n, 1).astype(jnp.float32)
    large = max_exact + (jnp.log(nf / max_exact) / math.log(MAX_DISTANCE / max_exact)
                         * (nb - max_exact)).astype(jnp.int32)
    large = jnp.minimum(large, nb - 1)
    return ret + jnp.where(n < max_exact, n, large)


def _bucket_table():
    qi = jnp.arange(CHUNK, dtype=jnp.int32)[:, None]
    kj = jnp.arange(3 * CHUNK, dtype=jnp.int32)[None, :]
    rel = kj - CHUNK - qi
    return jnp.where(jnp.abs(rel) <= CHUNK, _t5_buckets(rel), -1)


def _bias_table(rel_bias_t, buckets):
    nh = rel_bias_t.shape[0]

    def body(rb_ref, bk_ref, o_ref):
        bk = bk_ref[...]
        for h in range(nh):
            acc = jnp.where(bk < 0, NEG, 0.0).astype(F32)
            for b in range(N_BUCKETS):
                acc = jnp.where(bk == b, rb_ref[h, b] * LOG2E, acc)
            o_ref[h] = acc

    return dict(
        body=body, grid=(1,), name="bias_table", args=[rel_bias_t, buckets],
        out_shape=[jax.ShapeDtypeStruct((nh,) + buckets.shape, F32)],
        in_specs=[pl.BlockSpec(memory_space=pltpu.SMEM), _full(buckets.shape)],
        out_specs=[_full((nh,) + buckets.shape)])


def _rel_bias_grad(dbias, buckets):
    nh = dbias.shape[0]

    def body(db_ref, bk_ref, o_ref):
        bk = bk_ref[...]
        lane = lax.broadcasted_iota(jnp.int32, (1, 128), 1)
        for h in range(nh):
            d = db_ref[h]
            row = jnp.zeros((1, 128), F32)
            for b in range(N_BUCKETS):
                s = jnp.sum(jnp.sum(jnp.where(bk == b, d, 0.0), axis=1, keepdims=True), axis=0, keepdims=True)
                row = jnp.where(lane == b, s, row)
            o_ref[h:h + 1, :] = row

    return dict(
        body=body, grid=(1,), name="rel_bias_grad", args=[dbias, buckets],
        out_shape=[jax.ShapeDtypeStruct((nh, 128), F32)],
        in_specs=[_full(dbias.shape), _full(buckets.shape)], out_specs=[_full((nh, 128))])


def _half_masks():
    lane = lax.broadcasted_iota(jnp.int32, (CHUNK, 128), 1)
    return lane < HEAD_DIM, lane >= HEAD_DIM


def _kv_low(ref, starts, hk, lo):
    kt = (hk // 2) * 128
    out = []
    for jj in range(3):
        blk = ref[pl.ds(starts[jj], CHUNK), kt:kt + 128]
        if hk % 2 == 1:
            blk = pltpu.roll(blk, HEAD_DIM, 1)
        out.append(jnp.where(lo, blk, jnp.zeros_like(blk)))
    return out


def _stack_heads(tile_a, tile_b):
    return jnp.concatenate([tile_a, pltpu.roll(tile_a, HEAD_DIM, 1), tile_b, pltpu.roll(tile_b, HEAD_DIM, 1)], axis=0)


def _unstack_heads(o4):
    return (o4[0:CHUNK] + pltpu.roll(o4[CHUNK:2 * CHUNK], HEAD_DIM, 1),
            o4[2 * CHUNK:3 * CHUNK] + pltpu.roll(o4[3 * CHUNK:], HEAD_DIM, 1))


ATT_SLAB = 32


def _softmax_slab(s_scr, hk, g, r0, bias_ref, sink_ref, n, nblk):
    scale = HEAD_DIM ** -0.5 * LOG2E
    h = (N_HEADS // N_KV) * hk + g
    s = []
    for jj in range(3):
        sj = (s_scr[hk, jj, pl.ds(g * CHUNK + r0, ATT_SLAB), :] * scale
              + bias_ref[h, pl.ds(r0, ATT_SLAB), jj * CHUNK:(jj + 1) * CHUNK])
        if jj == 0:
            sj = jnp.where(n > 0, sj, NEG)
        if jj == 2:
            sj = jnp.where(n < nblk - 1, sj, NEG)
        s.append(sj)
    sink = sink_ref[h] * LOG2E
    m = jnp.maximum(jnp.max(jnp.maximum(jnp.maximum(s[0], s[1]), s[2]), axis=-1, keepdims=True), sink)
    e = [jnp.exp2(sj - m) for sj in s]
    es = jnp.exp2(sink - m)
    inv = 1.0 / (jnp.sum(e[0] + e[1] + e[2], axis=-1, keepdims=True) + es)
    return [ej * inv for ej in e], es * inv


def _key_block_starts(n, nblk):
    return [pl.multiple_of(jnp.clip(n - 1 + jj, 0, nblk - 1) * CHUNK, CHUNK) for jj in range(3)]


def _attn_fwd(qkv, x2, bias, sink, gath):
    t, dm = x2.shape
    nblk = t // CHUNK
    kvw = N_KV * HEAD_DIM
    kcb, vcb = dm // kvw, dm // kvw + 1
    slab = (N_KV, 3, 4 * CHUNK, CHUNK)

    def body(q_ref, k_ref, v_ref, x2_ref, bias_ref, sink_ref, gath_ref, x3_ref, att_ref, p_ref, ps_ref,
             wbuf, s_scr, sems):
        n = pl.program_id(0)

        @pl.when(n == 0)
        def _():
            _load_weight(gath_ref, wbuf, sems)
        lo, _ = _half_masks()
        lane_s = lax.broadcasted_iota(jnp.int32, (ATT_SLAB, 128), 1)
        starts = _key_block_starts(n, nblk)
        tiles = []
        for hk in range(N_KV):
            c0 = (2 * hk) * 128
            k_lo = _kv_low(k_ref, starts, hk, lo)
            v_lo = _kv_low(v_ref, starts, hk, lo)
            q4 = _stack_heads(q_ref[:, c0:c0 + 128], q_ref[:, c0 + 128:c0 + 256])
            for jj in range(3):
                s_scr[hk, jj] = _dot(q4, k_lo[jj], NT)
            for g in range(4):
                h = 4 * hk + g
                for r0 in range(0, CHUNK, ATT_SLAB):
                    p, ps = _softmax_slab(s_scr, hk, g, r0, bias_ref, sink_ref, n, nblk)
                    for jj in range(3):
                        p_ref[hk, jj, g * CHUNK + r0:g * CHUNK + r0 + ATT_SLAB, :] = p[jj].astype(BF16)
                    rest = jnp.zeros((ATT_SLAB, 128), F32) if h == 0 else ps_ref[r0:r0 + ATT_SLAB, :]
                    ps_ref[r0:r0 + ATT_SLAB, :] = jnp.where(lane_s == h, ps, rest)
            o4 = _dot(p_ref[hk, 0], v_lo[0], NN) + _dot(p_ref[hk, 1], v_lo[1], NN) + _dot(p_ref[hk, 2], v_lo[2], NN)
            tiles += list(_unstack_heads(o4))
        att = jnp.concatenate(tiles, axis=1).astype(BF16)
        att_ref[...] = att
        x3_ref[...] = x2_ref[...] + _dot(att, wbuf[...], NN)

    blk = pl.BlockSpec((CHUNK, dm), lambda n: (n, 0))
    return dict(
        body=body, grid=(nblk,), name="attn_fwd", args=[qkv, qkv, qkv, x2, bias, sink, gath],
        out_shape=[jax.ShapeDtypeStruct((t, dm), F32), jax.ShapeDtypeStruct((t, dm), BF16),
                   jax.ShapeDtypeStruct((nblk,) + slab, BF16), jax.ShapeDtypeStruct((t, 128), F32)],
        in_specs=[blk, pl.BlockSpec((t, kvw), lambda n: (0, kcb)), pl.BlockSpec((t, kvw), lambda n: (0, vcb)), blk,
                  _full(bias.shape), pl.BlockSpec(memory_space=pltpu.SMEM), ANY],
        out_specs=[blk, blk, pl.BlockSpec((None,) + slab, lambda n: (n, 0, 0, 0, 0)),
                   pl.BlockSpec((CHUNK, 128), lambda n: (n, 0))],
        scratch=[pltpu.VMEM((gath.shape[1] * NDEV, dm), BF16), pltpu.VMEM(slab, F32),
                 pltpu.SemaphoreType.DMA((NDEV,))])


def _attn_bwd(qkv, att, probs, sink_probs, dx3, bias_shape, gath):
    t, dm = dx3.shape
    nblk = t // CHUNK
    kvw = N_KV * HEAD_DIM
    kcb, vcb = dm // kvw, dm // kvw + 1
    scale = HEAD_DIM ** -0.5
    slab = (N_KV, 3, 4 * CHUNK, CHUNK)

    def body(q_ref, k_ref, v_ref, att_ref, p_ref, ps_ref, dx_ref, gath_ref,
             dq_ref, dkb_ref, dvb_ref, dbias_ref, dsink_ref,
             wbuf, dp_scr, ds_scr, prod_scr, dsum_scr, dk_ref, dv_ref, sems):
        n = pl.program_id(0)

        @pl.when(n == 0)
        def _():
            _load_weight(gath_ref, wbuf, sems)
            dk_ref[...] = jnp.zeros_like(dk_ref)
            dv_ref[...] = jnp.zeros_like(dv_ref)
            dbias_ref[...] = jnp.zeros_like(dbias_ref)
            dsink_ref[...] = jnp.zeros_like(dsink_ref)
        lo, hi = _half_masks()
        lane_s = lax.broadcasted_iota(jnp.int32, (ATT_SLAB, 128), 1)
        starts = _key_block_starts(n, nblk)
        dout = _dot(dx_ref[...].astype(BF16), wbuf[...], NT)
        prod_scr[...] = dout * att_ref[...].astype(F32)
        doutb = dout.astype(BF16)
        dq_tiles = []
        for hk in range(N_KV):
            kt = (hk // 2) * 128
            c0 = (2 * hk) * 128
            k_lo = _kv_low(k_ref, starts, hk, lo)
            v_lo = _kv_low(v_ref, starts, hk, lo)
            q4 = _stack_heads(q_ref[:, c0:c0 + 128], q_ref[:, c0 + 128:c0 + 256])
            do4 = _stack_heads(doutb[:, c0:c0 + 128], doutb[:, c0 + 128:c0 + 256])
            for jj in range(3):
                dp_scr[hk, jj] = _dot(do4, v_lo[jj], NT)
            for g in range(4):
                h = 4 * hk + g
                for r0 in range(0, CHUNK, ATT_SLAB):
                    rows = slice(g * CHUNK + r0, g * CHUNK + r0 + ATT_SLAB)
                    pt = prod_scr[r0:r0 + ATT_SLAB, c0 + (g // 2) * 128:c0 + (g // 2 + 1) * 128]
                    msk = lane_s < HEAD_DIM if g % 2 == 0 else lane_s >= HEAD_DIM
                    dsum = jnp.sum(jnp.where(msk, pt, 0.0), axis=-1, keepdims=True)
                    rest = jnp.zeros((ATT_SLAB, 128), F32) if h == 0 else dsum_scr[r0:r0 + ATT_SLAB, :]
                    dsum_scr[r0:r0 + ATT_SLAB, :] = jnp.where(lane_s == h, dsum, rest)
                    for jj in range(3):
                        ds = p_ref[hk, jj, rows, :].astype(F32) * (dp_scr[hk, jj, rows, :] - dsum)
                        dbias_ref[h, r0:r0 + ATT_SLAB, jj * CHUNK:(jj + 1) * CHUNK] += ds
                        ds_scr[hk, jj, rows, :] = ds.astype(BF16)
            dq4 = jnp.zeros((4 * CHUNK, 128), F32)
            for jj in range(3):
                ds4 = ds_scr[hk, jj]
                dq4 = dq4 + _dot(ds4, k_lo[jj], NN) * scale
                dkj = _dot(ds4, q4, TN) * scale
                dvj = _dot(p_ref[hk, jj], do4, TN)
                if hk % 2 == 1:
                    dkj, dvj = pltpu.roll(dkj, HEAD_DIM, 1), pltpu.roll(dvj, HEAD_DIM, 1)
                keep = lo if hk % 2 == 0 else hi
                dk_ref[pl.ds(starts[jj], CHUNK), kt:kt + 128] += jnp.where(keep, dkj, 0.0)
                dv_ref[pl.ds(starts[jj], CHUNK), kt:kt + 128] += jnp.where(keep, dvj, 0.0)
            dq_tiles += list(_unstack_heads(dq4))
        dq_ref[...] = jnp.concatenate(dq_tiles, axis=1).astype(BF16)
        dsink_ref[...] -= jnp.sum(ps_ref[...] * dsum_scr[...], axis=0, keepdims=True)

        @pl.when(n == nblk - 1)
        def _():
            dkb_ref[...] = dk_ref[...].astype(BF16)
            dvb_ref[...] = dv_ref[...].astype(BF16)

    blk = pl.BlockSpec((CHUNK, dm), lambda n: (n, 0))
    return dict(
        body=body, grid=(nblk,), name="attn_bwd", args=[qkv, qkv, qkv, att, probs, sink_probs, dx3, gath],
        out_shape=[jax.ShapeDtypeStruct((t, dm), BF16), jax.ShapeDtypeStruct((t, kvw), BF16),
                   jax.ShapeDtypeStruct((t, kvw), BF16), jax.ShapeDtypeStruct(bias_shape, F32),
                   jax.ShapeDtypeStruct((1, 128), F32)],
        in_specs=[blk, pl.BlockSpec((t, kvw), lambda n: (0, kcb)), pl.BlockSpec((t, kvw), lambda n: (0, vcb)),
                  blk, pl.BlockSpec((None,) + slab, lambda n: (n, 0, 0, 0, 0)),
                  pl.BlockSpec((CHUNK, 128), lambda n: (n, 0)), blk, ANY],
        out_specs=[blk, _full((t, kvw)), _full((t, kvw)), _full(bias_shape), _full((1, 128))],
        scratch=[pltpu.VMEM((gath.shape[1] * NDEV, dm), BF16), pltpu.VMEM(slab, F32), pltpu.VMEM(slab, BF16),
                 pltpu.VMEM((CHUNK, dm), F32), pltpu.VMEM((CHUNK, 128), F32),
                 pltpu.VMEM((t, kvw), F32), pltpu.VMEM((t, kvw), F32), pltpu.SemaphoreType.DMA((NDEV,))])


def _finish_weight(recvs, w, m, v, name):
    nl, r, dm = w.shape
    assert nl == len(recvs) and all(rc.shape[1:] == (r, dm) for rc in recvs)
    td = dm // 2
    wspec = pl.BlockSpec((None, r, td), lambda l, j: (l, 0, j))

    def body(*refs):
        r_refs = refs[:nl]
        w_ref, m_ref, v_ref, g_ref, d_ref, nm_ref, nv_ref = refs[nl:]
        layer = pl.program_id(0)
        for li in range(nl):
            @pl.when(layer == li)
            def _():
                g = r_refs[li][0].astype(F32)
                for d in range(1, recvs[li].shape[0]):
                    g = g + r_refs[li][d].astype(F32)
                delta, nm, nv = _adamw_math(w_ref[...], g, m_ref[...], v_ref[...])
                g_ref[...] = g
                d_ref[...] = delta
                nm_ref[...] = nm
                nv_ref[...] = nv

    o = jax.ShapeDtypeStruct(w.shape, F32)
    return dict(
        body=body, grid=(nl, 2), name=name, args=[*recvs, w, m, v], out_shape=[o, o, o, o],
        in_specs=[pl.BlockSpec((rc.shape[0], r, td), lambda l, j: (0, 0, j)) for rc in recvs] + [wspec] * 3,
        out_specs=[wspec] * 4)


def _adamw_small(ws, ms, vs, slots, lates, loss_slots, name):
    n = len(ws)
    nl = len(lates)

    def total(ref):
        acc = ref[0].astype(F32)
        for d in range(1, NDEV):
            acc = acc + ref[d].astype(F32)
        return acc

    def body(*refs):
        ins, outs = refs[:4 * n + nl + 1], refs[4 * n + nl + 1:]
        for i in range(n):
            w_ref, m_ref, v_ref, s_ref = ins[4 * i:4 * i + 4]
            g_ref, d_ref, nm_ref, nv_ref = outs[4 * i:4 * i + 4]
            g_ref[...] = total(s_ref)
            for k, (at, late) in enumerate(lates):
                if at == i:
                    g_ref[0:late.shape[1], :] = total(ins[4 * n + k])
            d_ref[...], nm_ref[...], nv_ref[...] = _adamw_math(w_ref[...], g_ref[...], m_ref[...], v_ref[...])
        outs[4 * n][...] = total(ins[4 * n + nl])

    args, out_shape = [], []
    for w, m, v, s in zip(ws, ms, vs, slots):
        args += [w, m, v, s]
        out_shape += [jax.ShapeDtypeStruct(w.shape, F32)] * 4
    args += [late for _, late in lates] + [loss_slots]
    out_shape.append(jax.ShapeDtypeStruct((1, 1), F32))
    out = pl.pallas_call(
        body, grid=(1,), out_shape=tuple(out_shape), in_specs=[_full(a.shape) for a in args],
        out_specs=tuple(_full(o.shape) for o in out_shape), compiler_params=_cp(), name=name)(*args)
    return [tuple(out[4 * i:4 * i + 4]) for i in range(n)], out[4 * n]


def kernel(x, norm_mix, norm_ffn, even_w_in, even_v_ln_g, even_v_ln_b, even_w_spatial, even_b_spatial, even_conv_w, even_w_out, attn_w_qkv, attn_sink, rel_bias, attn_w_out, ffn_w_gate, ffn_w_up, ffn_w_down, final_norm, loss_target, m_norm_mix, m_norm_ffn, m_even_w_in, m_even_v_ln_g, m_even_v_ln_b, m_even_w_spatial, m_even_b_spatial, m_even_conv_w, m_even_w_out, m_attn_w_qkv, m_attn_sink, m_rel_bias, m_attn_w_out, m_ffn_w_gate, m_ffn_w_up, m_ffn_w_down, m_final_norm, v_norm_mix, v_norm_ffn, v_even_w_in, v_even_v_ln_g, v_even_v_ln_b, v_even_w_spatial, v_even_b_spatial, v_even_conv_w, v_even_w_out, v_attn_w_qkv, v_attn_sink, v_rel_bias, v_attn_w_out, v_ffn_w_gate, v_ffn_w_up, v_ffn_w_down, v_final_norm):
    t, dm = x.shape[1], x.shape[2]
    aw = even_v_ln_g.shape[1]
    bw = even_conv_w.shape[2] * NDEV
    gd = aw // A_GROUPS
    tm = min(512, t // 2)
    tmf = min(256, t // 2)
    me = _my_index()
    row = lambda a: a.reshape(1, -1)

    colT = lambda w: w.T.astype(BF16)
    sh = dict(winT=colT(even_w_in[0]), wqkvT=colT(attn_w_qkv[0]), wgT0=colT(ffn_w_gate[0]), wuT0=colT(ffn_w_up[0]),
              wgT1=colT(ffn_w_gate[1]), wuT1=colT(ffn_w_up[1]), woe=even_w_out[0].astype(BF16),
              woa=attn_w_out[0].astype(BF16), wd0=ffn_w_down[0].astype(BF16), wd1=ffn_w_down[1].astype(BF16))
    gather = lambda names: _GatherCarry([sh[n] for n in names])

    in_full = lambda a: lax.dynamic_update_slice(jnp.zeros((3, bw), F32), a[0], (0, me * (bw // NDEV)))

    x0 = x[0]
    wsp_b = even_w_spatial[0].astype(BF16)
    bspb = jnp.broadcast_to(even_b_spatial[0][:, :, None], (A_GROUPS, CHUNK, gd))
    buckets = _bucket_table()
    sink = attn_sink[0]

    (bias,), ((g_winT,), (cw_slots,)) = _call(
        _bias_table(rel_bias.T, buckets), [gather(["winT"]), _BroadcastCarry([in_full(even_conv_w)])])
    cw_full = jnp.sum(cw_slots, axis=0)
    (proj, h0b), (g_woe, g_wgT0) = _call(_norm_proj(x0, row(norm_mix[0]), g_winT, F32, "in_proj", tm),
                                         gather(["woe", "wgT0"]))
    (x1, yb), (g_wuT0,) = _call(_even_core_fwd(proj, x0, even_v_ln_g, even_v_ln_b, wsp_b, bspb, cw_full, g_woe, tm),
                                gather(["wuT0"]))
    (gate0, up0, act0), (g_wd0,) = _call(_ffn_up(x1, row(norm_ffn[0]), g_wgT0, g_wuT0, "ffn_up0", tmf), gather(["wd0"]))
    (x2,), (g_wqkvT,) = _call(_ffn_down(x1, act0, g_wd0, "ffn_down0", tm), gather(["wqkvT"]))
    (qkv, h2b), (g_woa,) = _call(_norm_proj(x2, row(norm_mix[1]), g_wqkvT, BF16, "qkv_proj", tm), gather(["woa"]))
    (x3, attb, probs, sink_probs), (g_wgT1, g_wuT1) = _call(
        _attn_fwd(qkv, x2, bias, sink, g_woa), gather(["wgT1", "wuT1"]))
    (gate1, up1, act1), (g_wd1,) = _call(_ffn_up(x3, row(norm_ffn[1]), g_wgT1, g_wuT1, "ffn_up1", tmf), gather(["wd1"]))
    (loss_part, dx4, dx4b, d_final), _ = _call(
        _ffn_down_loss(x3, act1, g_wd1, loss_target[0], row(final_norm), "ffn_down1_loss", tm))

    (dx3, dx3b, dg1, du1, h3b, d_nffn1), _ = _call(
        _ffn_bwd(dx4, x3, gate1, up1, row(norm_ffn[1]), g_wgT1, g_wuT1, g_wd1, "ffn_bwd1", tmf))
    (p_wgT1,), _ = _call(_wgrad(dg1, h3b, "wgrad_gate1"))
    (p_wuT1,), _ = _call(_wgrad(du1, h3b, "wgrad_up1"))
    (p_wd1,), ((a_wgT1,), (a_wuT1,)) = _call(
        _wgrad(act1, dx4b, "wgrad_down1"), [_PairCarry(p_wgT1), _PairCarry(p_wuT1)])
    (dq, dk, dv, dbias, dsink), ((r_wgT1,), (a_wd1,)) = _call(
        _attn_bwd(qkv, attb, probs, sink_probs, dx3, bias.shape, g_woa),
        [_ChipSumCarry(p_wgT1, a_wgT1), _PairCarry(p_wd1)])
    (p_woa,), _ = _call(_wgrad(attb, dx3b, "wgrad_attn_out"))
    (dx2, dx2b, d_nmix1), (r_wuT1,) = _call(
        _proj_bwd_norm([dq, dk, dv], x2, row(norm_mix[1]), dx3, g_wqkvT, "qkv_bwd", tm),
        _ChipSumCarry(p_wuT1, a_wuT1))
    (p_wqkvT,), _ = _call(_wgrad([dq, dk, dv], h2b, "wgrad_qkv"))
    (dx1, dx1b, dg0, du0, h1b, d_nffn0), ((r_wd1,), (r_woa, r_wqkvT)) = _call(
        _ffn_bwd(dx2, x1, gate0, up0, row(norm_ffn[0]), g_wgT0, g_wuT0, g_wd0, "ffn_bwd0", tmf),
        [_ChipSumCarry(p_wd1, a_wd1), _GradCarry([p_woa, p_wqkvT])])
    (p_woe,), _ = _call(_wgrad(yb, dx1b, "wgrad_even_out"))
    (p_wgT0,), _ = _call(_wgrad(dg0, h1b, "wgrad_gate0"))
    (p_wuT0,), ((a_wgT0,), (r_woe,)) = _call(
        _wgrad(du0, h1b, "wgrad_up0"), [_PairCarry(p_wgT0), _GradCarry([p_woe])])
    (p_wd0,), ((r_wgT0,), (a_wuT0,)) = _call(
        _wgrad(act0, dx2b, "wgrad_down0"), [_ChipSumCarry(p_wgT0, a_wgT0), _PairCarry(p_wuT0)])
    (dproj, d_lng, d_lnb, d_wsp, d_bsp3, d_cw), ((r_wuT0,), (a_wd0,)) = _call(
        _even_core_bwd(proj, dx1, even_v_ln_g, even_v_ln_b, wsp_b, bspb, cw_full, g_woe, tm),
        [_ChipSumCarry(p_wuT0, a_wuT0), _PairCarry(p_wd0)])
    small_names = ["norm_mix", "norm_ffn", "even_v_ln_g", "even_v_ln_b", "even_w_spatial", "even_b_spatial",
                   "even_conv_w", "attn_sink", "rel_bias", "final_norm"]
    wsp_at = small_names.index("even_w_spatial")
    small_parts = [jnp.concatenate([jnp.zeros_like(d_nmix1), d_nmix1]), jnp.concatenate([d_nffn0, d_nffn1]),
                   d_lng, d_lnb, jnp.sum(d_bsp3, axis=-1)[None], d_cw,
                   dsink[:, 0:N_HEADS], jnp.zeros_like(rel_bias), d_final, loss_part]
    (p_winT,), (r_wd0,) = _call(_wgrad(dproj, h0b, "wgrad_in"), _ChipSumCarry(p_wd0, a_wd0))
    (d_relb_t,), ((a_winT,), rest_slots) = _call(
        _rel_bias_grad(dbias, buckets), [_PairCarry(p_winT), _BroadcastCarry(small_parts)])
    d_relb = d_relb_t[:, 0:N_BUCKETS].T
    (dx0, _, d_nmix0), ((r_winT,), (wsp_slots,)) = _call(
        _proj_bwd_norm([dproj], x0, row(norm_mix[0]), dx1, g_winT, "in_proj_bwd", tm),
        [_ChipSumCarry(p_winT, a_winT), _BroadcastCarry([d_wsp[None].astype(BF16)])])
    small_slots = list(rest_slots[:wsp_at]) + [wsp_slots] + list(rest_slots[wsp_at:])

    grads = {}

    order = ["norm_mix", "norm_ffn", "even_w_in", "even_v_ln_g", "even_v_ln_b", "even_w_spatial", "even_b_spatial",
             "even_conv_w", "even_w_out", "attn_w_qkv", "attn_sink", "rel_bias", "attn_w_out", "ffn_w_gate",
             "ffn_w_up", "ffn_w_down", "final_norm"]
    ws = dict(norm_mix=norm_mix, norm_ffn=norm_ffn, even_w_in=even_w_in, even_v_ln_g=even_v_ln_g,
              even_v_ln_b=even_v_ln_b, even_w_spatial=even_w_spatial, even_b_spatial=even_b_spatial,
              even_conv_w=even_conv_w, even_w_out=even_w_out, attn_w_qkv=attn_w_qkv, attn_sink=attn_sink,
              rel_bias=rel_bias, attn_w_out=attn_w_out, ffn_w_gate=ffn_w_gate, ffn_w_up=ffn_w_up,
              ffn_w_down=ffn_w_down, final_norm=final_norm)
    ms = dict(norm_mix=m_norm_mix, norm_ffn=m_norm_ffn, even_w_in=m_even_w_in, even_v_ln_g=m_even_v_ln_g,
              even_v_ln_b=m_even_v_ln_b, even_w_spatial=m_even_w_spatial, even_b_spatial=m_even_b_spatial,
              even_conv_w=m_even_conv_w, even_w_out=m_even_w_out, attn_w_qkv=m_attn_w_qkv, attn_sink=m_attn_sink,
              rel_bias=m_rel_bias, attn_w_out=m_attn_w_out, ffn_w_gate=m_ffn_w_gate, ffn_w_up=m_ffn_w_up,
              ffn_w_down=m_ffn_w_down, final_norm=m_final_norm)
    vs = dict(norm_mix=v_norm_mix, norm_ffn=v_norm_ffn, even_w_in=v_even_w_in, even_v_ln_g=v_even_v_ln_g,
              even_v_ln_b=v_even_v_ln_b, even_w_spatial=v_even_w_spatial, even_b_spatial=v_even_b_spatial,
              even_conv_w=v_even_conv_w, even_w_out=v_even_w_out, attn_w_qkv=v_attn_w_qkv, attn_sink=v_attn_sink,
              rel_bias=v_rel_bias, attn_w_out=v_attn_w_out, ffn_w_gate=v_ffn_w_gate, ffn_w_up=v_ffn_w_up,
              ffn_w_down=v_ffn_w_down, final_norm=v_final_norm)
    big = dict(ffn_w_gate=([r_wgT0, r_wgT1], True), even_w_in=([r_winT], True), even_w_out=([r_woe], False),
               attn_w_qkv=([r_wqkvT], True), attn_w_out=([r_woa], False), ffn_w_up=([r_wuT0, r_wuT1], True),
               ffn_w_down=([r_wd0, r_wd1], False))
    delta, new_m, new_v = {}, {}, {}
    late_slots = None
    for n, (recvs, transposed) in big.items():
        lay = (lambda a: jnp.swapaxes(a, 1, 2)) if transposed else (lambda a: a)
        spec = _finish_weight(recvs, lay(ws[n]), lay(ms[n]), lay(vs[n]), "finish_" + n)
        if late_slots is None:
            outs, late_slots = _call(spec, _BroadcastCarry([d_nmix0, d_relb]))
        else:
            outs, _ = _call(spec)
        grads[n], delta[n], new_m[n], new_v[n] = [lay(o) for o in outs]
    shaped = lambda n, a: in_full(a) if n == "even_conv_w" else (a.reshape(1, dm) if n == "final_norm" else a)
    pick = lambda dct: [shaped(n, dct[n]) for n in small_names]
    lates = [(small_names.index("norm_mix"), late_slots[0]), (small_names.index("rel_bias"), late_slots[1])]
    results, loss11 = _adamw_small(pick(ws), pick(ms), pick(vs), small_slots[:-1], lates, small_slots[-1],
                                   "adamw_small")
    mine = lambda a: lax.dynamic_slice(a, (0, me * (bw // NDEV)), (3, bw // NDEV))[None]
    for n, res in zip(small_names, results):
        for dst, a in zip((grads, delta, new_m, new_v), res):
            dst[n] = mine(a) if n == "even_conv_w" else (a.reshape(dm) if n == "final_norm" else a)
    loss = loss11[0, 0]
    return (loss, dx0[None], *[grads[n] for n in order], *[delta[n] for n in order],
            *[new_m[n] for n in order], *[new_v[n] for n in order])
```

```python
import math

import jax
import jax.numpy as jnp
import numpy as np
from jax import lax
from jax.experimental import pallas as pl
from jax.experimental.pallas import tpu as pltpu

F32, BF16 = jnp.float32, jnp.bfloat16
NDEV = 8
EPS = 1e-6
CHUNK = 128
A_GROUPS = 4
N_HEADS, N_KV, HEAD_DIM = 16, 4, 64
N_BUCKETS, MAX_DISTANCE = 32, 128
NEG = -1e30
LOG2E = 1.4426950408889634
ADAM_LR, ADAM_B1, ADAM_B2, ADAM_EPS, ADAM_WD, ADAM_STEP = 0.001, 0.9, 0.999, 1e-08, 0.01, 10
VMEM_LIMIT = 56 * 1024 * 1024
MESH = pl.DeviceIdType.MESH
NT = (((1,), (1,)), ((), ()))
NN = (((1,), (0,)), ((), ()))
TN = (((0,), (0,)), ((), ()))
ANY = pl.BlockSpec(memory_space=pl.ANY)


def _cp(n_grid=1):
    return pltpu.CompilerParams(dimension_semantics=("arbitrary",) * n_grid, vmem_limit_bytes=VMEM_LIMIT)


def _dot(a, b, dims):
    return lax.dot_general(a, b, dims, preferred_element_type=F32)


def _my_index():
    return 4 * lax.axis_index("x") + 2 * lax.axis_index("y") + lax.axis_index("c")


def _peer(k):
    x, y, c = lax.axis_index("x"), lax.axis_index("y"), lax.axis_index("c")
    px = 1 - x if k & 4 else x
    py = 1 - y if k & 2 else y
    pc = 1 - c if k & 1 else c
    return (px, py, pc)


def _load_weight(gath_ref, wbuf, sems):
    rows = gath_ref.shape[1]
    cps = [pltpu.make_async_copy(gath_ref.at[d], wbuf.at[pl.ds(d * rows, rows), :], sems.at[d]) for d in range(NDEV)]
    for c in cps:
        c.start()
    for c in cps:
        c.wait()


class _GatherCarry:
    def __init__(self, pieces):
        self.inputs = list(pieces)
        self.n = len(pieces)
        self.out_shape = [jax.ShapeDtypeStruct((NDEV,) + p.shape, p.dtype) for p in pieces]
        self.scratch = [pltpu.SemaphoreType.DMA((7 * self.n,)), pltpu.SemaphoreType.DMA((7 * self.n,)),
                        pltpu.SemaphoreType.DMA((self.n,))]

    def _ctx(self):
        x, y, c = lax.axis_index("x"), lax.axis_index("y"), lax.axis_index("c")
        chips = [(1 - x, y), (x, 1 - y), (1 - x, 1 - y)]
        return (x, y, c), (x, y, 1 - c), chips, c

    def _copy(self, k, j, block, to, ins, outs, sems, src=None):
        send_sems, recv_sems, _ = sems
        slot = outs[j].at[4 * block[0] + 2 * block[1] + block[2]]
        return pltpu.make_async_remote_copy(
            src_ref=slot if src is None else src, dst_ref=slot, send_sem=send_sems.at[k * self.n + j],
            recv_sem=recv_sems.at[k * self.n + j], device_id=to, device_id_type=MESH)

    def start(self, ins, outs, sems):
        me, sibling, chips, c = self._ctx()
        for j in range(self.n):
            pltpu.make_async_copy(ins[j], outs[j].at[4 * me[0] + 2 * me[1] + me[2]], sems[2].at[j]).start()
            self._copy(0, j, me, sibling, ins, outs, sems, src=ins[j]).start()
            for q, chip in enumerate(chips):
                self._copy(1 + q, j, me, (*chip, c), ins, outs, sems, src=ins[j]).start()

    def mid(self, ins, outs, sems):
        me, sibling, chips, c = self._ctx()
        for q, chip in enumerate(chips):
            for j in range(self.n):
                self._copy(1 + q, j, (*chip, c), me, ins, outs, sems).wait_recv()
                self._copy(4 + q, j, (*chip, c), sibling, ins, outs, sems).start()

    def finish(self, ins, outs, sems):
        me, sibling, chips, c = self._ctx()
        for j in range(self.n):
            self._copy(0, j, sibling, me, ins, outs, sems).wait_recv()
            for q, chip in enumerate(chips):
                self._copy(4 + q, j, (*chip, 1 - c), me, ins, outs, sems).wait_recv()
        for j in range(self.n):
            self._copy(0, j, me, sibling, ins, outs, sems, src=ins[j]).wait_send()
            for q, chip in enumerate(chips):
                self._copy(1 + q, j, me, (*chip, c), ins, outs, sems, src=ins[j]).wait_send()
                self._copy(4 + q, j, (*chip, c), sibling, ins, outs, sems).wait_send()
            pltpu.make_async_copy(ins[j], outs[j].at[0], sems[2].at[j]).wait()


class _GradCarry:
    def __init__(self, pieces):
        self.inputs = list(pieces)
        self.n = len(pieces)
        self.rows = [p.shape[0] // NDEV for p in pieces]
        self.out_shape = [jax.ShapeDtypeStruct((NDEV, r, p.shape[1]), p.dtype) for p, r in zip(pieces, self.rows)]
        self.scratch = [pltpu.SemaphoreType.DMA((7 * self.n,)), pltpu.SemaphoreType.DMA((7 * self.n,)),
                        pltpu.SemaphoreType.DMA((self.n,))]

    def _copies(self, ins, outs, sems):
        me = _my_index()
        local, remote = [], []
        for j in range(self.n):
            r = self.rows[j]
            local.append(pltpu.make_async_copy(ins[j].at[pl.ds(pl.multiple_of(me * r, 16), r), :], outs[j].at[me],
                                               sems[2].at[j]))
            for k in range(1, NDEV):
                peer = _peer(k)
                pidx = 4 * peer[0] + 2 * peer[1] + peer[2]
                remote.append(pltpu.make_async_remote_copy(
                    src_ref=ins[j].at[pl.ds(pl.multiple_of(pidx * r, 16), r), :], dst_ref=outs[j].at[me],
                    send_sem=sems[0].at[(k - 1) * self.n + j], recv_sem=sems[1].at[(k - 1) * self.n + j],
                    device_id=peer, device_id_type=MESH))
        return local, remote

    def start(self, ins, outs, sems):
        local, remote = self._copies(ins, outs, sems)
        for cp in local + remote:
            cp.start()

    def mid(self, ins, outs, sems):
        pass

    def finish(self, ins, outs, sems):
        local, remote = self._copies(ins, outs, sems)
        for cp in remote + local:
            cp.wait()


class _BroadcastCarry:
    def __init__(self, parts):
        self.inputs = list(parts)
        self.n = len(self.inputs)
        self.out_shape = [jax.ShapeDtypeStruct((NDEV,) + p.shape, p.dtype) for p in self.inputs]
        self.scratch = [pltpu.SemaphoreType.DMA((7 * self.n,)), pltpu.SemaphoreType.DMA((7 * self.n,)),
                        pltpu.SemaphoreType.DMA((self.n,))]

    def _copies(self, ins, outs, sems):
        me = _my_index()
        cps = []
        for j in range(self.n):
            cps.append(pltpu.make_async_copy(ins[j], outs[j].at[me], sems[2].at[j]))
            cps += [pltpu.make_async_remote_copy(
                src_ref=ins[j], dst_ref=outs[j].at[me], send_sem=sems[0].at[(k - 1) * self.n + j],
                recv_sem=sems[1].at[(k - 1) * self.n + j], device_id=_peer(k), device_id_type=MESH)
                for k in range(1, NDEV)]
        return cps

    def start(self, ins, outs, sems):
        for cp in self._copies(ins, outs, sems):
            cp.start()

    def mid(self, ins, outs, sems):
        pass

    def finish(self, ins, outs, sems):
        for cp in self._copies(ins, outs, sems):
            cp.wait()


class _PairCarry:
    def __init__(self, piece):
        self.inputs = [piece]
        self.r = piece.shape[0] // NDEV
        self.out_shape = [jax.ShapeDtypeStruct((4, self.r, piece.shape[1]), piece.dtype)]
        self.scratch = [pltpu.SemaphoreType.DMA((4,)), pltpu.SemaphoreType.DMA((4,))]

    def _copies(self, ins, outs, sems):
        x, y, c = lax.axis_index("x"), lax.axis_index("y"), lax.axis_index("c")
        return [pltpu.make_async_remote_copy(
            src_ref=ins[0].at[pl.ds(pl.multiple_of((2 * q + 1 - c) * self.r, 16), self.r), :], dst_ref=outs[0].at[q],
            send_sem=sems[0].at[q], recv_sem=sems[1].at[q], device_id=(x, y, 1 - c), device_id_type=MESH)
            for q in range(4)]

    def start(self, ins, outs, sems):
        for cp in self._copies(ins, outs, sems):
            cp.start()

    def mid(self, ins, outs, sems):
        pass

    def finish(self, ins, outs, sems):
        for cp in self._copies(ins, outs, sems):
            cp.wait()


class _ChipSumCarry:
    def __init__(self, piece, landed):
        self.inputs = [piece, landed]
        self.r, dm = piece.shape[0] // NDEV, piece.shape[1]
        self.out_shape = [jax.ShapeDtypeStruct((4, self.r, dm), piece.dtype)]
        self.scratch = [pltpu.VMEM((4, self.r, dm), piece.dtype), pltpu.VMEM((8, self.r, dm), piece.dtype),
                        pltpu.SemaphoreType.DMA((8,)), pltpu.SemaphoreType.DMA((3,)), pltpu.SemaphoreType.DMA((3,)),
                        pltpu.SemaphoreType.DMA(())]

    def _copies(self, outs, scr):
        sums, _, _, send_sems, recv_sems, local_sem = scr
        x, y, c = lax.axis_index("x"), lax.axis_index("y"), lax.axis_index("c")
        mine = 2 * x + y
        local = pltpu.make_async_copy(sums.at[mine], outs[0].at[mine], local_sem)
        remote = []
        for k in range(1, 4):
            px = 1 - x if k & 2 else x
            py = 1 - y if k & 1 else y
            remote.append(pltpu.make_async_remote_copy(
                src_ref=sums.at[2 * px + py], dst_ref=outs[0].at[mine], send_sem=send_sems.at[k - 1],
                recv_sem=recv_sems.at[k - 1], device_id=(px, py, c), device_id_type=MESH))
        return local, remote

    def start(self, ins, outs, scr):
        sums, stage, stage_sems = scr[0], scr[1], scr[2]
        c = lax.axis_index("c")
        loads = []
        for q in range(4):
            loads.append((
                pltpu.make_async_copy(ins[0].at[pl.ds(pl.multiple_of((2 * q + c) * self.r, 16), self.r), :],
                                      stage.at[2 * q], stage_sems.at[2 * q]),
                pltpu.make_async_copy(ins[1].at[q], stage.at[2 * q + 1], stage_sems.at[2 * q + 1])))
        for a, b in loads:
            a.start()
            b.start()
        for q, (a, b) in enumerate(loads):
            a.wait()
            b.wait()
            sums[q] = (stage[2 * q].astype(F32) + stage[2 * q + 1].astype(F32)).astype(sums.dtype)
        local, remote = self._copies(outs, scr)
        for cp in [local] + remote:
            cp.start()

    def mid(self, ins, outs, scr):
        pass

    def finish(self, ins, outs, scr):
        local, remote = self._copies(outs, scr)
        for cp in remote + [local]:
            cp.wait()


def _call(spec, carry=None):
    body, grid = spec["body"], spec["grid"]
    in_specs, out_specs, out_shape = list(spec["in_specs"]), list(spec["out_specs"]), list(spec["out_shape"])
    scratch, args = list(spec.get("scratch", [])), list(spec["args"])
    if carry is None:
        out = pl.pallas_call(body, grid=grid, in_specs=in_specs, out_specs=tuple(out_specs),
                             out_shape=tuple(out_shape), scratch_shapes=scratch, compiler_params=_cp(len(grid)),
                             name=spec["name"])(*args)
        return tuple(out), ()
    carries = list(carry) if isinstance(carry, (list, tuple)) else [carry]
    n_in, n_out, n_s = len(in_specs), len(out_specs), len(scratch)
    steps = int(np.prod(grid))

    def split(refs, counts):
        parts, o = [], 0
        for cnt in counts:
            parts.append(refs[o:o + cnt])
            o += cnt
        return parts

    c_in = [len(cr.inputs) for cr in carries]
    c_out = [len(cr.out_shape) for cr in carries]
    c_scr = [len(cr.scratch) for cr in carries]

    def wrapped(*refs):
        ins, cins, outs, couts, scr, cscr = split(refs, [n_in, sum(c_in), n_out, sum(c_out), n_s, sum(c_scr)])
        per = list(zip(carries, split(cins, c_in), split(couts, c_out), split(cscr, c_scr)))
        step = pl.program_id(0)
        for ax in range(1, len(grid)):
            step = step * grid[ax] + pl.program_id(ax)

        @pl.when(step == 0)
        def _():
            for cr, ci, co, cs in per:
                cr.start(ci, co, cs)
        if steps >= 3:
            @pl.when(step == steps - 2)
            def _():
                for cr, ci, co, cs in per:
                    cr.mid(ci, co, cs)
        body(*ins, *outs, *scr)

        @pl.when(step == steps - 1)
        def _():
            for cr, ci, co, cs in per:
                if steps < 3:
                    cr.mid(ci, co, cs)
                cr.finish(ci, co, cs)

    out = pl.pallas_call(
        wrapped, grid=grid, in_specs=in_specs + [ANY] * sum(c_in), out_specs=tuple(out_specs + [ANY] * sum(c_out)),
        out_shape=tuple(out_shape + [s for cr in carries for s in cr.out_shape]),
        scratch_shapes=scratch + [s for cr in carries for s in cr.scratch],
        compiler_params=_cp(len(grid)), name=spec["name"])(*args, *[a for cr in carries for a in cr.inputs])
    c_res = [tuple(p) for p in split(out[n_out:], c_out)]
    return tuple(out[:n_out]), (c_res if isinstance(carry, (list, tuple)) else c_res[0])


def _rms_fwd(x, gain):
    r = lax.rsqrt(jnp.mean(x * x, axis=-1, keepdims=True) + EPS)
    return x * r * gain, r


def _rms_bwd(dh, x, r, gain):
    a = dh * gain
    dx = r * a - x * (r * r * r) * jnp.mean(a * x, axis=-1, keepdims=True)
    dgain = jnp.sum(dh * (x * r), axis=0, keepdims=True)
    return dx, dgain


def _gelu(x):
    return 0.5 * x * (1.0 + lax.erf(x * 0.7071067811865476))


def _gelu_grad(x):
    return 0.5 * (1.0 + lax.erf(x * 0.7071067811865476)) + x * jnp.exp(-0.5 * x * x) * 0.3989422804014327


def _sigmoid(x):
    return 1.0 / (1.0 + jnp.exp(-x))


def _adamw_math(w, g, m, v):
    nm = ADAM_B1 * m + (1.0 - ADAM_B1) * g
    nv = ADAM_B2 * v + (1.0 - ADAM_B2) * (g * g)
    m_hat = nm / (1.0 - ADAM_B1 ** ADAM_STEP)
    v_hat = nv / (1.0 - ADAM_B2 ** ADAM_STEP)
    return -ADAM_LR * (m_hat / (jnp.sqrt(v_hat) + ADAM_EPS) + ADAM_WD * w), nm, nv


def _tok(tm, w):
    return pl.BlockSpec((tm, w), lambda i: (i, 0))


def _full(shape):
    return pl.BlockSpec(shape, lambda *i: (0,) * len(shape))


def _norm_proj(x, gain, gath, out_dtype, name, tm):
    t, dm = x.shape
    n = gath.shape[1] * NDEV

    def body(x_ref, g_ref, gath_ref, proj_ref, hb_ref, wbuf, sems):
        @pl.when(pl.program_id(0) == 0)
        def _():
            _load_weight(gath_ref, wbuf, sems)
        h, _ = _rms_fwd(x_ref[...], g_ref[...])
        hb = h.astype(BF16)
        hb_ref[...] = hb
        proj_ref[...] = _dot(hb, wbuf[...], NT).astype(out_dtype)

    return dict(
        body=body, grid=(t // tm,), name=name, args=[x, gain, gath],
        out_shape=[jax.ShapeDtypeStruct((t, n), out_dtype), jax.ShapeDtypeStruct((t, dm), BF16)],
        in_specs=[_tok(tm, dm), _full((1, dm)), ANY], out_specs=[_tok(tm, n), _tok(tm, dm)],
        scratch=[pltpu.VMEM((n, dm), BF16), pltpu.SemaphoreType.DMA((NDEV,))])


def _proj_bwd_norm(dys, x, gain, dres, gath, name, tm):
    t, dm = x.shape
    n = gath.shape[1] * NDEV
    widths = [d.shape[1] for d in dys]
    assert sum(widths) == n
    nd = len(dys)

    def body(*refs):
        dy_refs = refs[:nd]
        x_ref, g_ref, dres_ref, gath_ref, dx_ref, dxb_ref, dgain_ref, wbuf, sems = refs[nd:]

        @pl.when(pl.program_id(0) == 0)
        def _():
            _load_weight(gath_ref, wbuf, sems)
            dgain_ref[...] = jnp.zeros_like(dgain_ref)
        xv, gain_v = x_ref[...], g_ref[...]
        _, r = _rms_fwd(xv, gain_v)
        dh, c0 = None, 0
        for dy_ref, wd in zip(dy_refs, widths):
            part = _dot(dy_ref[...], wbuf[c0:c0 + wd, :], NN)
            dh = part if dh is None else dh + part
            c0 += wd
        dx, dgain = _rms_bwd(dh, xv, r, gain_v)
        dx = dres_ref[...] + dx
        dx_ref[...] = dx
        dxb_ref[...] = dx.astype(BF16)
        dgain_ref[...] += dgain

    return dict(
        body=body, grid=(t // tm,), name=name, args=[*dys, x, gain, dres, gath],
        out_shape=[jax.ShapeDtypeStruct((t, dm), F32), jax.ShapeDtypeStruct((t, dm), BF16),
                   jax.ShapeDtypeStruct((1, dm), F32)],
        in_specs=[_tok(tm, wd) for wd in widths] + [_tok(tm, dm), _full((1, dm)), _tok(tm, dm), ANY],
        out_specs=[_tok(tm, dm), _tok(tm, dm), _full((1, dm))],
        scratch=[pltpu.VMEM((n, dm), BF16), pltpu.SemaphoreType.DMA((NDEV,))])


def _wgrad(a, b, name, tmm=256):
    parts = list(a) if isinstance(a, (list, tuple)) else [a]
    t = parts[0].shape[0]
    n = b.shape[1]
    tiles = [p.shape[1] // tmm for p in parts]
    first = [sum(tiles[:i]) for i in range(len(parts))]
    m = sum(tiles) * tmm

    def body(*refs):
        a_refs, b_ref, o_ref = refs[:len(parts)], refs[len(parts)], refs[len(parts) + 1]
        j = pl.program_id(0)
        for a_ref, j0, nt in zip(a_refs, first, tiles):
            if len(parts) == 1:
                o_ref[...] = _dot(a_ref[...], b_ref[...], TN).astype(BF16)
            else:
                @pl.when((j >= j0) & (j < j0 + nt))
                def _():
                    o_ref[...] = _dot(a_ref[...], b_ref[...], TN).astype(BF16)

    a_specs = [pl.BlockSpec((t, tmm), lambda j, j0=j0, nt=nt: (0, jnp.clip(j - j0, 0, nt - 1)))
               for j0, nt in zip(first, tiles)]
    return dict(
        body=body, grid=(sum(tiles),), name=name, args=[*parts, b], out_shape=[jax.ShapeDtypeStruct((m, n), BF16)],
        in_specs=a_specs + [pl.BlockSpec((t, n), lambda j: (0, 0))],
        out_specs=[pl.BlockSpec((tmm, n), lambda j: (j, 0))])


def _halo_specs(tm, t, width, col_blocks):
    nb8 = tm // 8
    last = t // 8 - 1
    prev = [pl.BlockSpec((8, width), lambda i, cb=cb: (jnp.maximum(i * nb8 - 1, 0), cb)) for cb in col_blocks]
    nxt = [pl.BlockSpec((8, width), lambda i, cb=cb: (jnp.minimum((i + 1) * nb8, last), cb)) for cb in col_blocks]
    return prev, nxt


def _shift_rows(z, prev_row, next_row):
    tm = z.shape[0]
    row = lax.broadcasted_iota(jnp.int32, z.shape, 0)
    zm1 = jnp.where(row == 0, prev_row, pltpu.roll(z, 1, 0))
    zp1 = jnp.where(row == tm - 1, next_row, pltpu.roll(z, tm - 1, 0))
    return zm1, zp1


def _gating_fwd(proj, lng, lnb, wsp_ref, bsp_ref, aw):
    tm = proj.shape[0]
    a_u = _gelu(proj[:, 0:aw])
    gv = _gelu(proj[:, aw:2 * aw])
    mu = jnp.mean(gv, axis=-1, keepdims=True)
    xc = gv - mu
    rstd = lax.rsqrt(jnp.mean(xc * xc, axis=-1, keepdims=True) + EPS)
    vn = xc * rstd
    a_v = (vn * lng + lnb).astype(BF16)
    gd = aw // A_GROUPS
    rows = []
    for c in range(tm // CHUNK):
        cols = []
        for g in range(A_GROUPS):
            blk = a_v[c * CHUNK:(c + 1) * CHUNK, g * gd:(g + 1) * gd]
            cols.append(_dot(wsp_ref[g], blk, NN) + bsp_ref[g])
        rows.append(jnp.concatenate(cols, axis=1))
    mixed = jnp.concatenate(rows, axis=0)
    return a_u, vn, rstd, a_v, mixed


def _even_core_fwd(proj, x0, lng, lnb, wsp, bspb, cw, gath, tm):
    t, dm = x0.shape
    aw = lng.shape[1]
    bw = cw.shape[1]
    assert aw == bw and 2 * aw + 3 * bw == proj.shape[1]
    nt = t // tm
    prev, nxt = _halo_specs(tm, t, bw, [3, 4])

    def body(proj_ref, cp_ref, hp_ref, cn_ref, hn_ref, x0_ref, lng_ref, lnb_ref, wsp_ref, bsp_ref, cw_ref, gath_ref,
             x1_ref, y_ref, wbuf, sems):
        i = pl.program_id(0)

        @pl.when(i == 0)
        def _():
            _load_weight(gath_ref, wbuf, sems)
        proj_v = proj_ref[...]
        a_u, _, _, _, mixed = _gating_fwd(proj_v, lng_ref[...], lnb_ref[...], wsp_ref, bsp_ref, aw)
        a_out = a_u * mixed
        bb = proj_v[:, 2 * aw:2 * aw + bw]
        z = proj_v[:, 2 * aw + bw:2 * aw + 2 * bw] * proj_v[:, 2 * aw + 2 * bw:]
        zprev = jnp.where(i > 0, cp_ref[7:8, :] * hp_ref[7:8, :], 0.0)
        znext = jnp.where(i < nt - 1, cn_ref[0:1, :] * hn_ref[0:1, :], 0.0)
        zm1, zp1 = _shift_rows(z, zprev, znext)
        cwv = cw_ref[...]
        conv = zm1 * cwv[0:1, :] + z * cwv[1:2, :] + zp1 * cwv[2:3, :]
        y = jnp.concatenate([a_out, bb * conv], axis=1).astype(BF16)
        y_ref[...] = y
        x1_ref[...] = x0_ref[...] + _dot(y, wbuf[...], NN)

    return dict(
        body=body, grid=(nt,), name="even_core_fwd",
        args=[proj, proj, proj, proj, proj, x0, lng, lnb, wsp, bspb, cw, gath],
        out_shape=[jax.ShapeDtypeStruct((t, dm), F32), jax.ShapeDtypeStruct((t, aw + bw), BF16)],
        in_specs=[_tok(tm, proj.shape[1]), prev[0], prev[1], nxt[0], nxt[1], _tok(tm, dm), _full(lng.shape),
                  _full(lnb.shape), _full(wsp.shape), _full(bspb.shape), _full(cw.shape), ANY],
        out_specs=[_tok(tm, dm), _tok(tm, aw + bw)],
        scratch=[pltpu.VMEM((gath.shape[1] * NDEV, dm), BF16), pltpu.SemaphoreType.DMA((NDEV,))])


def _even_core_bwd(proj, dx1, lng, lnb, wsp, bspb, cw, gath, tm):
    t, dm = dx1.shape
    aw, bw = lng.shape[1], cw.shape[1]
    gd = aw // A_GROUPS
    nt = t // tm
    inw = proj.shape[1]
    prev, nxt = _halo_specs(tm, t, bw, [2, 3, 4])
    nb8 = tm // 8
    last8 = t // 8 - 1

    def body(proj_ref, bp_ref, cp_ref, hp_ref, bn_ref, cn_ref, hn_ref, dx_ref, dxp_ref, dxn_ref,
             lng_ref, lnb_ref, wsp_ref, bsp_ref, cw_ref, gath_ref,
             dproj_ref, dlng_ref, dlnb_ref, dwsp_ref, dbsp_ref, dcw_ref, wbuf, sems):
        i = pl.program_id(0)

        @pl.when(i == 0)
        def _():
            _load_weight(gath_ref, wbuf, sems)
            dlng_ref[...] = jnp.zeros_like(dlng_ref)
            dlnb_ref[...] = jnp.zeros_like(dlnb_ref)
            dwsp_ref[...] = jnp.zeros_like(dwsp_ref)
            dbsp_ref[...] = jnp.zeros_like(dbsp_ref)
            dcw_ref[...] = jnp.zeros_like(dcw_ref)
        proj_v = proj_ref[...]
        lng_v = lng_ref[...]
        a_u, vn, rstd, a_v, mixed = _gating_fwd(proj_v, lng_v, lnb_ref[...], wsp_ref, bsp_ref, aw)
        w = wbuf[...]
        dy = _dot(dx_ref[...].astype(BF16), w, NT)
        da_out, db_out = dy[:, 0:aw], dy[:, aw:]
        da_u = da_out * mixed
        dmixed = da_out * a_u
        dmb = dmixed.astype(BF16)
        rows = []
        for c in range(tm // CHUNK):
            cols = []
            for g in range(A_GROUPS):
                r0, c0 = c * CHUNK, g * gd
                dm_cg = dmb[r0:r0 + CHUNK, c0:c0 + gd]
                cols.append(_dot(wsp_ref[g], dm_cg, TN))
                dwsp_ref[g] += _dot(dm_cg, a_v[r0:r0 + CHUNK, c0:c0 + gd], NT)
                dbsp_ref[g] += dmixed[r0:r0 + CHUNK, c0:c0 + gd]
            rows.append(jnp.concatenate(cols, axis=1))
        dav = jnp.concatenate(rows, axis=0)
        dlng_ref[...] += jnp.sum(dav * vn, axis=0, keepdims=True)
        dlnb_ref[...] += jnp.sum(dav, axis=0, keepdims=True)
        dvn = dav * lng_v
        dgv = rstd * (dvn - jnp.mean(dvn, axis=-1, keepdims=True) - vn * jnp.mean(dvn * vn, axis=-1, keepdims=True))
        dv_pre = dgv * _gelu_grad(proj_v[:, aw:2 * aw])
        du_pre = da_u * _gelu_grad(proj_v[:, 0:aw])
        bb = proj_v[:, 2 * aw:2 * aw + bw]
        bc = proj_v[:, 2 * aw + bw:2 * aw + 2 * bw]
        bh = proj_v[:, 2 * aw + 2 * bw:]
        z = bc * bh
        zprev = jnp.where(i > 0, cp_ref[7:8, :] * hp_ref[7:8, :], 0.0)
        znext = jnp.where(i < nt - 1, cn_ref[0:1, :] * hn_ref[0:1, :], 0.0)
        zm1, zp1 = _shift_rows(z, zprev, znext)
        cwv = cw_ref[...]
        conv = zm1 * cwv[0:1, :] + z * cwv[1:2, :] + zp1 * cwv[2:3, :]
        dbb = db_out * conv
        dconv = db_out * bb
        dx_edge = jnp.concatenate([dxp_ref[...], dxn_ref[...]], axis=0).astype(BF16)
        dy_edge = _dot(dx_edge, w[aw:, :], NT)
        dcprev = jnp.where(i > 0, dy_edge[7:8, :] * bp_ref[7:8, :], 0.0)
        dcnext = jnp.where(i < nt - 1, dy_edge[8:9, :] * bn_ref[0:1, :], 0.0)
        dcm1, dcp1 = _shift_rows(dconv, dcprev, dcnext)
        dz = dcp1 * cwv[0:1, :] + dconv * cwv[1:2, :] + dcm1 * cwv[2:3, :]
        dcw_ref[0:1, :] += jnp.sum(dconv * zm1, axis=0, keepdims=True)
        dcw_ref[1:2, :] += jnp.sum(dconv * z, axis=0, keepdims=True)
        dcw_ref[2:3, :] += jnp.sum(dconv * zp1, axis=0, keepdims=True)
        dproj_ref[...] = jnp.concatenate([du_pre, dv_pre, dbb, dz * bh, dz * bc], axis=1).astype(BF16)

    row8 = lambda f: pl.BlockSpec((8, dm), f)
    return dict(
        body=body, grid=(nt,), name="even_core_bwd",
        args=[proj, proj, proj, proj, proj, proj, proj, dx1, dx1, dx1, lng, lnb, wsp, bspb, cw, gath],
        out_shape=[jax.ShapeDtypeStruct((t, inw), BF16), jax.ShapeDtypeStruct((1, aw), F32),
                   jax.ShapeDtypeStruct((1, aw), F32), jax.ShapeDtypeStruct(wsp.shape, F32),
                   jax.ShapeDtypeStruct((A_GROUPS, CHUNK, gd), F32), jax.ShapeDtypeStruct(cw.shape, F32)],
        in_specs=[_tok(tm, inw), prev[0], prev[1], prev[2], nxt[0], nxt[1], nxt[2], _tok(tm, dm),
                  row8(lambda i: (jnp.maximum(i * nb8 - 1, 0), 0)), row8(lambda i: (jnp.minimum((i + 1) * nb8, last8), 0)),
                  _full(lng.shape), _full(lnb.shape), _full(wsp.shape), _full(bspb.shape), _full(cw.shape), ANY],
        out_specs=[_tok(tm, inw), _full((1, aw)), _full((1, aw)), _full(wsp.shape),
                   _full((A_GROUPS, CHUNK, gd)), _full(cw.shape)],
        scratch=[pltpu.VMEM((gath.shape[1] * NDEV, dm), BF16), pltpu.SemaphoreType.DMA((NDEV,))])


def _ff_chunks(f, width=1024):
    return [(c0, min(c0 + width, f)) for c0 in range(0, f, width)]


def _ffn_up(x, gain, gath_g, gath_u, name, tm):
    t, dm = x.shape
    f = gath_g.shape[1] * NDEV

    def body(x_ref, g_ref, gg_ref, gu_ref, gate_ref, up_ref, act_ref, wg, wu, sems):
        @pl.when(pl.program_id(0) == 0)
        def _():
            _load_weight(gg_ref, wg, sems)
            _load_weight(gu_ref, wu, sems)
        h, _ = _rms_fwd(x_ref[...], g_ref[...])
        hb = h.astype(BF16)
        for c0, c1 in _ff_chunks(f):
            gate = _dot(hb, wg[c0:c1, :], NT)
            up = _dot(hb, wu[c0:c1, :], NT)
            gate_ref[:, c0:c1] = gate.astype(BF16)
            up_ref[:, c0:c1] = up.astype(BF16)
            act_ref[:, c0:c1] = (gate * _sigmoid(gate) * up).astype(BF16)

    o = jax.ShapeDtypeStruct((t, f), BF16)
    return dict(
        body=body, grid=(t // tm,), name=name, args=[x, gain, gath_g, gath_u], out_shape=[o, o, o],
        in_specs=[_tok(tm, dm), _full((1, dm)), ANY, ANY], out_specs=[_tok(tm, f)] * 3,
        scratch=[pltpu.VMEM((f, dm), BF16), pltpu.VMEM((f, dm), BF16), pltpu.SemaphoreType.DMA((NDEV,))])


def _ffn_down(x, act, gath_d, name, tm):
    t, dm = x.shape
    f = act.shape[1]

    def body(x_ref, a_ref, gd_ref, xo_ref, wd, sems):
        @pl.when(pl.program_id(0) == 0)
        def _():
            _load_weight(gd_ref, wd, sems)
        xo_ref[...] = x_ref[...] + _dot(a_ref[...], wd[...], NN)

    return dict(
        body=body, grid=(t // tm,), name=name, args=[x, act, gath_d], out_shape=[jax.ShapeDtypeStruct((t, dm), F32)],
        in_specs=[_tok(tm, dm), _tok(tm, f), ANY], out_specs=[_tok(tm, dm)],
        scratch=[pltpu.VMEM((f, dm), BF16), pltpu.SemaphoreType.DMA((NDEV,))])


def _ffn_down_loss(x, act, gath_d, target, gain, name, tm):
    t, dm = x.shape
    f = act.shape[1]
    steps = t // tm

    def body(x_ref, a_ref, gd_ref, t_ref, g_ref, loss_ref, dx_ref, dxb_ref, dgain_ref, wd, acc, sems):
        i = pl.program_id(0)

        @pl.when(i == 0)
        def _():
            _load_weight(gd_ref, wd, sems)
            acc[...] = jnp.zeros_like(acc)
            dgain_ref[...] = jnp.zeros_like(dgain_ref)
        xv = x_ref[...] + _dot(a_ref[...], wd[...], NN)
        gain_v = g_ref[...]
        y, r = _rms_fwd(xv, gain_v)
        e = y - t_ref[...]
        acc[...] += jnp.sum(e * e, axis=0, keepdims=True)
        dx, dgain = _rms_bwd(e * (1.0 / dm), xv, r, gain_v)
        dx_ref[...] = dx
        dxb_ref[...] = dx.astype(BF16)
        dgain_ref[...] += dgain

        @pl.when(i == steps - 1)
        def _():
            loss_ref[...] = jnp.sum(acc[...], axis=-1, keepdims=True) * (0.5 / dm)

    return dict(
        body=body, grid=(steps,), name=name, args=[x, act, gath_d, target, gain],
        out_shape=[jax.ShapeDtypeStruct((1, 1), F32), jax.ShapeDtypeStruct((t, dm), F32),
                   jax.ShapeDtypeStruct((t, dm), BF16), jax.ShapeDtypeStruct((1, dm), F32)],
        in_specs=[_tok(tm, dm), _tok(tm, f), ANY, _tok(tm, dm), _full((1, dm))],
        out_specs=[_full((1, 1)), _tok(tm, dm), _tok(tm, dm), _full((1, dm))],
        scratch=[pltpu.VMEM((f, dm), BF16), pltpu.VMEM((1, dm), F32), pltpu.SemaphoreType.DMA((NDEV,))])


def _ffn_bwd(dxo, x, gate, up, gain, gath_g, gath_u, gath_d, name, tm):
    t, dm = x.shape
    f = gate.shape[1]

    def body(dxo_ref, x_ref, gate_ref, up_ref, g_ref, gg_ref, gu_ref, gd_ref,
             dx_ref, dxb_ref, dg_ref, du_ref, hb_ref, dgain_ref, wg, wu, wd, sems):
        @pl.when(pl.program_id(0) == 0)
        def _():
            _load_weight(gg_ref, wg, sems)
            _load_weight(gu_ref, wu, sems)
            _load_weight(gd_ref, wd, sems)
            dgain_ref[...] = jnp.zeros_like(dgain_ref)
        xv, gain_v, dxo_v = x_ref[...], g_ref[...], dxo_ref[...]
        h, r = _rms_fwd(xv, gain_v)
        hb_ref[...] = h.astype(BF16)
        dxob = dxo_v.astype(BF16)
        dh = jnp.zeros_like(xv)
        for c0, c1 in _ff_chunks(f):
            gate_v = gate_ref[:, c0:c1].astype(F32)
            up_v = up_ref[:, c0:c1].astype(F32)
            s = _sigmoid(gate_v)
            silu = gate_v * s
            dact = _dot(dxob, wd[c0:c1, :], NT)
            dg = (dact * up_v * (s * (1.0 + gate_v * (1.0 - s)))).astype(BF16)
            du = (dact * silu).astype(BF16)
            dg_ref[:, c0:c1] = dg
            du_ref[:, c0:c1] = du
            dh = dh + _dot(dg, wg[c0:c1, :], NN) + _dot(du, wu[c0:c1, :], NN)
        dx, dgain = _rms_bwd(dh, xv, r, gain_v)
        dx = dxo_v + dx
        dx_ref[...] = dx
        dxb_ref[...] = dx.astype(BF16)
        dgain_ref[...] += dgain

    return dict(
        body=body, grid=(t // tm,), name=name, args=[dxo, x, gate, up, gain, gath_g, gath_u, gath_d],
        out_shape=[jax.ShapeDtypeStruct((t, dm), F32), jax.ShapeDtypeStruct((t, dm), BF16),
                   jax.ShapeDtypeStruct((t, f), BF16), jax.ShapeDtypeStruct((t, f), BF16),
                   jax.ShapeDtypeStruct((t, dm), BF16), jax.ShapeDtypeStruct((1, dm), F32)],
        in_specs=[_tok(tm, dm), _tok(tm, dm), _tok(tm, f), _tok(tm, f), _full((1, dm)), ANY, ANY, ANY],
        out_specs=[_tok(tm, dm), _tok(tm, dm), _tok(tm, f), _tok(tm, f), _tok(tm, dm), _full((1, dm))],
        scratch=[pltpu.VMEM((f, dm), BF16), pltpu.VMEM((f, dm), BF16), pltpu.VMEM((f, dm), BF16),
                 pltpu.SemaphoreType.DMA((NDEV,))])


def _t5_buckets(rel):
    nb = N_BUCKETS // 2
    ret = jnp.where(rel > 0, nb, 0)
    n = jnp.abs(rel)
    max_exact = nb // 2
    nf = jnp.maximum(n, 1).astype(jnp.float32)
    large = max_exact + (jnp.log(nf / max_exact) / math.log(MAX_DISTANCE / max_exact)
                         * (nb - max_exact)).astype(jnp.int32)
    large = jnp.minimum(large, nb - 1)
    return ret + jnp.where(n < max_exact, n, large)


def _bucket_table():
    qi = jnp.arange(CHUNK, dtype=jnp.int32)[:, None]
    kj = jnp.arange(3 * CHUNK, dtype=jnp.int32)[None, :]
    rel = kj - CHUNK - qi
    return jnp.where(jnp.abs(rel) <= CHUNK, _t5_buckets(rel), -1)


def _bias_table(rel_bias_t, buckets):
    nh = rel_bias_t.shape[0]

    def body(rb_ref, bk_ref, o_ref):
        bk = bk_ref[...]
        for h in range(nh):
            acc = jnp.where(bk < 0, NEG, 0.0).astype(F32)
            for b in range(N_BUCKETS):
                acc = jnp.where(bk == b, rb_ref[h, b] * LOG2E, acc)
            o_ref[h] = acc

    return dict(
        body=body, grid=(1,), name="bias_table", args=[rel_bias_t, buckets],
        out_shape=[jax.ShapeDtypeStruct((nh,) + buckets.shape, F32)],
        in_specs=[pl.BlockSpec(memory_space=pltpu.SMEM), _full(buckets.shape)],
        out_specs=[_full((nh,) + buckets.shape)])


def _rel_bias_grad(dbias, buckets):
    nh = dbias.shape[0]

    def body(db_ref, bk_ref, o_ref):
        bk = bk_ref[...]
        lane = lax.broadcasted_iota(jnp.int32, (1, 128), 1)
        for h in range(nh):
            d = db_ref[h]
            row = jnp.zeros((1, 128), F32)
            for b in range(N_BUCKETS):
                s = jnp.sum(jnp.sum(jnp.where(bk == b, d, 0.0), axis=1, keepdims=True), axis=0, keepdims=True)
                row = jnp.where(lane == b, s, row)
            o_ref[h:h + 1, :] = row

    return dict(
        body=body, grid=(1,), name="rel_bias_grad", args=[dbias, buckets],
        out_shape=[jax.ShapeDtypeStruct((nh, 128), F32)],
        in_specs=[_full(dbias.shape), _full(buckets.shape)], out_specs=[_full((nh, 128))])


def _half_masks():
    lane = lax.broadcasted_iota(jnp.int32, (CHUNK, 128), 1)
    return lane < HEAD_DIM, lane >= HEAD_DIM


def _kv_low(ref, starts, hk, lo):
    kt = (hk // 2) * 128
    out = []
    for jj in range(3):
        blk = ref[pl.ds(starts[jj], CHUNK), kt:kt + 128]
        if hk % 2 == 1:
            blk = pltpu.roll(blk, HEAD_DIM, 1)
        out.append(jnp.where(lo, blk, jnp.zeros_like(blk)))
    return out


def _stack_heads(tile_a, tile_b):
    return jnp.concatenate([tile_a, pltpu.roll(tile_a, HEAD_DIM, 1), tile_b, pltpu.roll(tile_b, HEAD_DIM, 1)], axis=0)


def _unstack_heads(o4):
    return (o4[0:CHUNK] + pltpu.roll(o4[CHUNK:2 * CHUNK], HEAD_DIM, 1),
            o4[2 * CHUNK:3 * CHUNK] + pltpu.roll(o4[3 * CHUNK:], HEAD_DIM, 1))


ATT_SLAB = 32


def _softmax_slab(s_scr, hk, g, r0, bias_ref, sink_ref, n, nblk):
    scale = HEAD_DIM ** -0.5 * LOG2E
    h = (N_HEADS // N_KV) * hk + g
    s = []
    for jj in range(3):
        sj = (s_scr[hk, jj, pl.ds(g * CHUNK + r0, ATT_SLAB), :] * scale
              + bias_ref[h, pl.ds(r0, ATT_SLAB), jj * CHUNK:(jj + 1) * CHUNK])
        if jj == 0:
            sj = jnp.where(n > 0, sj, NEG)
        if jj == 2:
            sj = jnp.where(n < nblk - 1, sj, NEG)
        s.append(sj)
    sink = sink_ref[h] * LOG2E
    m = jnp.maximum(jnp.max(jnp.maximum(jnp.maximum(s[0], s[1]), s[2]), axis=-1, keepdims=True), sink)
    e = [jnp.exp2(sj - m) for sj in s]
    es = jnp.exp2(sink - m)
    inv = 1.0 / (jnp.sum(e[0] + e[1] + e[2], axis=-1, keepdims=True) + es)
    return [ej * inv for ej in e], es * inv


def _key_block_starts(n, nblk):
    return [pl.multiple_of(jnp.clip(n - 1 + jj, 0, nblk - 1) * CHUNK, CHUNK) for jj in range(3)]


def _attn_fwd(qkv, x2, bias, sink, gath):
    t, dm = x2.shape
    nblk = t // CHUNK
    kvw = N_KV * HEAD_DIM
    kcb, vcb = dm // kvw, dm // kvw + 1
    slab = (N_KV, 3, 4 * CHUNK, CHUNK)

    def body(q_ref, k_ref, v_ref, x2_ref, bias_ref, sink_ref, gath_ref, x3_ref, att_ref, p_ref, ps_ref,
             wbuf, s_scr, sems):
        n = pl.program_id(0)

        @pl.when(n == 0)
        def _():
            _load_weight(gath_ref, wbuf, sems)
        lo, _ = _half_masks()
        lane_s = lax.broadcasted_iota(jnp.int32, (ATT_SLAB, 128), 1)
        starts = _key_block_starts(n, nblk)
        tiles = []
        for hk in range(N_KV):
            c0 = (2 * hk) * 128
            k_lo = _kv_low(k_ref, starts, hk, lo)
            v_lo = _kv_low(v_ref, starts, hk, lo)
            q4 = _stack_heads(q_ref[:, c0:c0 + 128], q_ref[:, c0 + 128:c0 + 256])
            for jj in range(3):
                s_scr[hk, jj] = _dot(q4, k_lo[jj], NT)
            for g in range(4):
                h = 4 * hk + g
                for r0 in range(0, CHUNK, ATT_SLAB):
                    p, ps = _softmax_slab(s_scr, hk, g, r0, bias_ref, sink_ref, n, nblk)
                    for jj in range(3):
                        p_ref[hk, jj, g * CHUNK + r0:g * CHUNK + r0 + ATT_SLAB, :] = p[jj].astype(BF16)
                    rest = jnp.zeros((ATT_SLAB, 128), F32) if h == 0 else ps_ref[r0:r0 + ATT_SLAB, :]
                    ps_ref[r0:r0 + ATT_SLAB, :] = jnp.where(lane_s == h, ps, rest)
            o4 = _dot(p_ref[hk, 0], v_lo[0], NN) + _dot(p_ref[hk, 1], v_lo[1], NN) + _dot(p_ref[hk, 2], v_lo[2], NN)
            tiles += list(_unstack_heads(o4))
        att = jnp.concatenate(tiles, axis=1).astype(BF16)
        att_ref[...] = att
        x3_ref[...] = x2_ref[...] + _dot(att, wbuf[...], NN)

    blk = pl.BlockSpec((CHUNK, dm), lambda n: (n, 0))
    return dict(
        body=body, grid=(nblk,), name="attn_fwd", args=[qkv, qkv, qkv, x2, bias, sink, gath],
        out_shape=[jax.ShapeDtypeStruct((t, dm), F32), jax.ShapeDtypeStruct((t, dm), BF16),
                   jax.ShapeDtypeStruct((nblk,) + slab, BF16), jax.ShapeDtypeStruct((t, 128), F32)],
        in_specs=[blk, pl.BlockSpec((t, kvw), lambda n: (0, kcb)), pl.BlockSpec((t, kvw), lambda n: (0, vcb)), blk,
                  _full(bias.shape), pl.BlockSpec(memory_space=pltpu.SMEM), ANY],
        out_specs=[blk, blk, pl.BlockSpec((None,) + slab, lambda n: (n, 0, 0, 0, 0)),
                   pl.BlockSpec((CHUNK, 128), lambda n: (n, 0))],
        scratch=[pltpu.VMEM((gath.shape[1] * NDEV, dm), BF16), pltpu.VMEM(slab, F32),
                 pltpu.SemaphoreType.DMA((NDEV,))])


def _attn_bwd(qkv, att, probs, sink_probs, dx3, bias_shape, gath):
    t, dm = dx3.shape
    nblk = t // CHUNK
    kvw = N_KV * HEAD_DIM
    kcb, vcb = dm // kvw, dm // kvw + 1
    scale = HEAD_DIM ** -0.5
    slab = (N_KV, 3, 4 * CHUNK, CHUNK)

    def body(q_ref, k_ref, v_ref, att_ref, p_ref, ps_ref, dx_ref, gath_ref,
             dq_ref, dkb_ref, dvb_ref, dbias_ref, dsink_ref,
             wbuf, dp_scr, ds_scr, prod_scr, dsum_scr, dk_ref, dv_ref, sems):
        n = pl.program_id(0)

        @pl.when(n == 0)
        def _():
            _load_weight(gath_ref, wbuf, sems)
            dk_ref[...] = jnp.zeros_like(dk_ref)
            dv_ref[...] = jnp.zeros_like(dv_ref)
            dbias_ref[...] = jnp.zeros_like(dbias_ref)
            dsink_ref[...] = jnp.zeros_like(dsink_ref)
        lo, hi = _half_masks()
        lane_s = lax.broadcasted_iota(jnp.int32, (ATT_SLAB, 128), 1)
        starts = _key_block_starts(n, nblk)
        dout = _dot(dx_ref[...].astype(BF16), wbuf[...], NT)
        prod_scr[...] = dout * att_ref[...].astype(F32)
        doutb = dout.astype(BF16)
        dq_tiles = []
        for hk in range(N_KV):
            kt = (hk // 2) * 128
            c0 = (2 * hk) * 128
            k_lo = _kv_low(k_ref, starts, hk, lo)
            v_lo = _kv_low(v_ref, starts, hk, lo)
            q4 = _stack_heads(q_ref[:, c0:c0 + 128], q_ref[:, c0 + 128:c0 + 256])
            do4 = _stack_heads(doutb[:, c0:c0 + 128], doutb[:, c0 + 128:c0 + 256])
            for jj in range(3):
                dp_scr[hk, jj] = _dot(do4, v_lo[jj], NT)
            for g in range(4):
                h = 4 * hk + g
                for r0 in range(0, CHUNK, ATT_SLAB):
                    rows = slice(g * CHUNK + r0, g * CHUNK + r0 + ATT_SLAB)
                    pt = prod_scr[r0:r0 + ATT_SLAB, c0 + (g // 2) * 128:c0 + (g // 2 + 1) * 128]
                    msk = lane_s < HEAD_DIM if g % 2 == 0 else lane_s >= HEAD_DIM
                    dsum = jnp.sum(jnp.where(msk, pt, 0.0), axis=-1, keepdims=True)
                    rest = jnp.zeros((ATT_SLAB, 128), F32) if h == 0 else dsum_scr[r0:r0 + ATT_SLAB, :]
                    dsum_scr[r0:r0 + ATT_SLAB, :] = jnp.where(lane_s == h, dsum, rest)
                    for jj in range(3):
                        ds = p_ref[hk, jj, rows, :].astype(F32) * (dp_scr[hk, jj, rows, :] - dsum)
                        dbias_ref[h, r0:r0 + ATT_SLAB, jj * CHUNK:(jj + 1) * CHUNK] += ds
                        ds_scr[hk, jj, rows, :] = ds.astype(BF16)
            dq4 = jnp.zeros((4 * CHUNK, 128), F32)
            for jj in range(3):
                ds4 = ds_scr[hk, jj]
                dq4 = dq4 + _dot(ds4, k_lo[jj], NN) * scale
                dkj = _dot(ds4, q4, TN) * scale
                dvj = _dot(p_ref[hk, jj], do4, TN)
                if hk % 2 == 1:
                    dkj, dvj = pltpu.roll(dkj, HEAD_DIM, 1), pltpu.roll(dvj, HEAD_DIM, 1)
                keep = lo if hk % 2 == 0 else hi
                dk_ref[pl.ds(starts[jj], CHUNK), kt:kt + 128] += jnp.where(keep, dkj, 0.0)
                dv_ref[pl.ds(starts[jj], CHUNK), kt:kt + 128] += jnp.where(keep, dvj, 0.0)
            dq_tiles += list(_unstack_heads(dq4))
        dq_ref[...] = jnp.concatenate(dq_tiles, axis=1).astype(BF16)
        dsink_ref[...] -= jnp.sum(ps_ref[...] * dsum_scr[...], axis=0, keepdims=True)

        @pl.when(n == nblk - 1)
        def _():
            dkb_ref[...] = dk_ref[...].astype(BF16)
            dvb_ref[...] = dv_ref[...].astype(BF16)

    blk = pl.BlockSpec((CHUNK, dm), lambda n: (n, 0))
    return dict(
        body=body, grid=(nblk,), name="attn_bwd", args=[qkv, qkv, qkv, att, probs, sink_probs, dx3, gath],
        out_shape=[jax.ShapeDtypeStruct((t, dm), BF16), jax.ShapeDtypeStruct((t, kvw), BF16),
                   jax.ShapeDtypeStruct((t, kvw), BF16), jax.ShapeDtypeStruct(bias_shape, F32),
                   jax.ShapeDtypeStruct((1, 128), F32)],
        in_specs=[blk, pl.BlockSpec((t, kvw), lambda n: (0, kcb)), pl.BlockSpec((t, kvw), lambda n: (0, vcb)),
                  blk, pl.BlockSpec((None,) + slab, lambda n: (n, 0, 0, 0, 0)),
                  pl.BlockSpec((CHUNK, 128), lambda n: (n, 0)), blk, ANY],
        out_specs=[blk, _full((t, kvw)), _full((t, kvw)), _full(bias_shape), _full((1, 128))],
        scratch=[pltpu.VMEM((gath.shape[1] * NDEV, dm), BF16), pltpu.VMEM(slab, F32), pltpu.VMEM(slab, BF16),
                 pltpu.VMEM((CHUNK, dm), F32), pltpu.VMEM((CHUNK, 128), F32),
                 pltpu.VMEM((t, kvw), F32), pltpu.VMEM((t, kvw), F32), pltpu.SemaphoreType.DMA((NDEV,))])


def _finish_weight(recvs, w, m, v, name):
    nl, r, dm = w.shape
    assert nl == len(recvs) and all(rc.shape[1:] == (r, dm) for rc in recvs)
    td = dm // 2
    wspec = pl.BlockSpec((None, r, td), lambda l, j: (l, 0, j))

    def body(*refs):
        r_refs = refs[:nl]
        w_ref, m_ref, v_ref, g_ref, d_ref, nm_ref, nv_ref = refs[nl:]
        layer = pl.program_id(0)
        for li in range(nl):
            @pl.when(layer == li)
            def _():
                g = r_refs[li][0].astype(F32)
                for d in range(1, recvs[li].shape[0]):
                    g = g + r_refs[li][d].astype(F32)
                delta, nm, nv = _adamw_math(w_ref[...], g, m_ref[...], v_ref[...])
                g_ref[...] = g
                d_ref[...] = delta
                nm_ref[...] = nm
                nv_ref[...] = nv

    o = jax.ShapeDtypeStruct(w.shape, F32)
    return dict(
        body=body, grid=(nl, 2), name=name, args=[*recvs, w, m, v], out_shape=[o, o, o, o],
        in_specs=[pl.BlockSpec((rc.shape[0], r, td), lambda l, j: (0, 0, j)) for rc in recvs] + [wspec] * 3,
        out_specs=[wspec] * 4)


def _adamw_small(ws, ms, vs, slots, lates, loss_slots, name):
    n = len(ws)
    nl = len(lates)

    def total(ref):
        acc = ref[0].astype(F32)
        for d in range(1, NDEV):
            acc = acc + ref[d].astype(F32)
        return acc

    def body(*refs):
        ins, outs = refs[:4 * n + nl + 1], refs[4 * n + nl + 1:]
        for i in range(n):
            w_ref, m_ref, v_ref, s_ref = ins[4 * i:4 * i + 4]
            g_ref, d_ref, nm_ref, nv_ref = outs[4 * i:4 * i + 4]
            g_ref[...] = total(s_ref)
            for k, (at, late) in enumerate(lates):
                if at == i:
                    g_ref[0:late.shape[1], :] = total(ins[4 * n + k])
            d_ref[...], nm_ref[...], nv_ref[...] = _adamw_math(w_ref[...], g_ref[...], m_ref[...], v_ref[...])
        outs[4 * n][...] = total(ins[4 * n + nl])

    args, out_shape = [], []
    for w, m, v, s in zip(ws, ms, vs, slots):
        args += [w, m, v, s]
        out_shape += [jax.ShapeDtypeStruct(w.shape, F32)] * 4
    args += [late for _, late in lates] + [loss_slots]
    out_shape.append(jax.ShapeDtypeStruct((1, 1), F32))
    out = pl.pallas_call(
        body, grid=(1,), out_shape=tuple(out_shape), in_specs=[_full(a.shape) for a in args],
        out_specs=tuple(_full(o.shape) for o in out_shape), compiler_params=_cp(), name=name)(*args)
    return [tuple(out[4 * i:4 * i + 4]) for i in range(n)], out[4 * n]


def kernel(x, norm_mix, norm_ffn, even_w_in, even_v_ln_g, even_v_ln_b, even_w_spatial, even_b_spatial, even_conv_w, even_w_out, attn_w_qkv, attn_sink, rel_bias, attn_w_out, ffn_w_gate, ffn_w_up, ffn_w_down, final_norm, loss_target, m_norm_mix, m_norm_ffn, m_even_w_in, m_even_v_ln_g, m_even_v_ln_b, m_even_w_spatial, m_even_b_spatial, m_even_conv_w, m_even_w_out, m_attn_w_qkv, m_attn_sink, m_rel_bias, m_attn_w_out, m_ffn_w_gate, m_ffn_w_up, m_ffn_w_down, m_final_norm, v_norm_mix, v_norm_ffn, v_even_w_in, v_even_v_ln_g, v_even_v_ln_b, v_even_w_spatial, v_even_b_spatial, v_even_conv_w, v_even_w_out, v_attn_w_qkv, v_attn_sink, v_rel_bias, v_attn_w_out, v_ffn_w_gate, v_ffn_w_up, v_ffn_w_down, v_final_norm):
    t, dm = x.shape[1], x.shape[2]
    aw = even_v_ln_g.shape[1]
    bw = even_conv_w.shape[2] * NDEV
    gd = aw // A_GROUPS
    tm = min(512, t // 2)
    tmf = min(256, t // 2)
    me = _my_index()
    row = lambda a: a.reshape(1, -1)

    colT = lambda w: w.T.astype(BF16)
    sh = dict(winT=colT(even_w_in[0]), wqkvT=colT(attn_w_qkv[0]), wgT0=colT(ffn_w_gate[0]), wuT0=colT(ffn_w_up[0]),
              wgT1=colT(ffn_w_gate[1]), wuT1=colT(ffn_w_up[1]), woe=even_w_out[0].astype(BF16),
              woa=attn_w_out[0].astype(BF16), wd0=ffn_w_down[0].astype(BF16), wd1=ffn_w_down[1].astype(BF16))
    gather = lambda names: _GatherCarry([sh[n] for n in names])

    in_full = lambda a: lax.dynamic_update_slice(jnp.zeros((3, bw), F32), a[0], (0, me * (bw // NDEV)))

    x0 = x[0]
    wsp_b = even_w_spatial[0].astype(BF16)
    bspb = jnp.broadcast_to(even_b_spatial[0][:, :, None], (A_GROUPS, CHUNK, gd))
    buckets = _bucket_table()
    sink = attn_sink[0]

    (bias,), ((g_winT,), (cw_slots,)) = _call(
        _bias_table(rel_bias.T, buckets), [gather(["winT"]), _BroadcastCarry([in_full(even_conv_w)])])
    cw_full = jnp.sum(cw_slots, axis=0)
    (proj, h0b), (g_woe, g_wgT0) = _call(_norm_proj(x0, row(norm_mix[0]), g_winT, F32, "in_proj", tm),
                                         gather(["woe", "wgT0"]))
    (x1, yb), (g_wuT0,) = _call(_even_core_fwd(proj, x0, even_v_ln_g, even_v_ln_b, wsp_b, bspb, cw_full, g_woe, tm),
                                gather(["wuT0"]))
    (gate0, up0, act0), (g_wd0,) = _call(_ffn_up(x1, row(norm_ffn[0]), g_wgT0, g_wuT0, "ffn_up0", tmf), gather(["wd0"]))
    (x2,), (g_wqkvT,) = _call(_ffn_down(x1, act0, g_wd0, "ffn_down0", tm), gather(["wqkvT"]))
    (qkv, h2b), (g_woa,) = _call(_norm_proj(x2, row(norm_mix[1]), g_wqkvT, BF16, "qkv_proj", tm), gather(["woa"]))
    (x3, attb, probs, sink_probs), (g_wgT1, g_wuT1) = _call(
        _attn_fwd(qkv, x2, bias, sink, g_woa), gather(["wgT1", "wuT1"]))
    (gate1, up1, act1), (g_wd1,) = _call(_ffn_up(x3, row(norm_ffn[1]), g_wgT1, g_wuT1, "ffn_up1", tmf), gather(["wd1"]))
    (loss_part, dx4, dx4b, d_final), _ = _call(
        _ffn_down_loss(x3, act1, g_wd1, loss_target[0], row(final_norm), "ffn_down1_loss", tm))

    (dx3, dx3b, dg1, du1, h3b, d_nffn1), _ = _call(
        _ffn_bwd(dx4, x3, gate1, up1, row(norm_ffn[1]), g_wgT1, g_wuT1, g_wd1, "ffn_bwd1", tmf))
    (p_wgT1,), _ = _call(_wgrad(dg1, h3b, "wgrad_gate1"))
    (p_wuT1,), _ = _call(_wgrad(du1, h3b, "wgrad_up1"))
    (p_wd1,), ((a_wgT1,), (a_wuT1,)) = _call(
        _wgrad(act1, dx4b, "wgrad_down1"), [_PairCarry(p_wgT1), _PairCarry(p_wuT1)])
    (dq, dk, dv, dbias, dsink), ((r_wgT1,), (a_wd1,)) = _call(
        _attn_bwd(qkv, attb, probs, sink_probs, dx3, bias.shape, g_woa),
        [_ChipSumCarry(p_wgT1, a_wgT1), _PairCarry(p_wd1)])
    (p_woa,), _ = _call(_wgrad(attb, dx3b, "wgrad_attn_out"))
    (dx2, dx2b, d_nmix1), (r_wuT1,) = _call(
        _proj_bwd_norm([dq, dk, dv], x2, row(norm_mix[1]), dx3, g_wqkvT, "qkv_bwd", tm),
        _ChipSumCarry(p_wuT1, a_wuT1))
    (p_wqkvT,), _ = _call(_wgrad([dq, dk, dv], h2b, "wgrad_qkv"))
    (dx1, dx1b, dg0, du0, h1b, d_nffn0), ((r_wd1,), (r_woa, r_wqkvT)) = _call(
        _ffn_bwd(dx2, x1, gate0, up0, row(norm_ffn[0]), g_wgT0, g_wuT0, g_wd0, "ffn_bwd0", tmf),
        [_ChipSumCarry(p_wd1, a_wd1), _GradCarry([p_woa, p_wqkvT])])
    (p_woe,), _ = _call(_wgrad(yb, dx1b, "wgrad_even_out"))
    (p_wgT0,), _ = _call(_wgrad(dg0, h1b, "wgrad_gate0"))
    (p_wuT0,), ((a_wgT0,), (r_woe,)) = _call(
        _wgrad(du0, h1b, "wgrad_up0"), [_PairCarry(p_wgT0), _GradCarry([p_woe])])
    (p_wd0,), ((r_wgT0,), (a_wuT0,)) = _call(
        _wgrad(act0, dx2b, "wgrad_down0"), [_ChipSumCarry(p_wgT0, a_wgT0), _PairCarry(p_wuT0)])
    (dproj, d_lng, d_lnb, d_wsp, d_bsp3, d_cw), ((r_wuT0,), (a_wd0,)) = _call(
        _even_core_bwd(proj, dx1, even_v_ln_g, even_v_ln_b, wsp_b, bspb, cw_full, g_woe, tm),
        [_ChipSumCarry(p_wuT0, a_wuT0), _PairCarry(p_wd0)])
    small_names = ["norm_mix", "norm_ffn", "even_v_ln_g", "even_v_ln_b", "even_w_spatial", "even_b_spatial",
                   "even_conv_w", "attn_sink", "rel_bias", "final_norm"]
    wsp_at = small_names.index("even_w_spatial")
    small_parts = [jnp.concatenate([jnp.zeros_like(d_nmix1), d_nmix1]), jnp.concatenate([d_nffn0, d_nffn1]),
                   d_lng, d_lnb, jnp.sum(d_bsp3, axis=-1)[None], d_cw,
                   dsink[:, 0:N_HEADS], jnp.zeros_like(rel_bias), d_final, loss_part]
    (p_winT,), (r_wd0,) = _call(_wgrad(dproj, h0b, "wgrad_in"), _ChipSumCarry(p_wd0, a_wd0))
    (d_relb_t,), ((a_winT,), rest_slots) = _call(
        _rel_bias_grad(dbias, buckets), [_PairCarry(p_winT), _BroadcastCarry(small_parts)])
    d_relb = d_relb_t[:, 0:N_BUCKETS].T
    (dx0, _, d_nmix0), ((r_winT,), (wsp_slots,)) = _call(
        _proj_bwd_norm([dproj], x0, row(norm_mix[0]), dx1, g_winT, "in_proj_bwd", tm),
        [_ChipSumCarry(p_winT, a_winT), _GatherCarry([d_wsp[None].astype(BF16)])])
    small_slots = list(rest_slots[:wsp_at]) + [wsp_slots] + list(rest_slots[wsp_at:])

    grads = {}

    order = ["norm_mix", "norm_ffn", "even_w_in", "even_v_ln_g", "even_v_ln_b", "even_w_spatial", "even_b_spatial",
             "even_conv_w", "even_w_out", "attn_w_qkv", "attn_sink", "rel_bias", "attn_w_out", "ffn_w_gate",
             "ffn_w_up", "ffn_w_down", "final_norm"]
    ws = dict(norm_mix=norm_mix, norm_ffn=norm_ffn, even_w_in=even_w_in, even_v_ln_g=even_v_ln_g,
              even_v_ln_b=even_v_ln_b, even_w_spatial=even_w_spatial, even_b_spatial=even_b_spatial,
              even_conv_w=even_conv_w, even_w_out=even_w_out, attn_w_qkv=attn_w_qkv, attn_sink=attn_sink,
              rel_bias=rel_bias, attn_w_out=attn_w_out, ffn_w_gate=ffn_w_gate, ffn_w_up=ffn_w_up,
              ffn_w_down=ffn_w_down, final_norm=final_norm)
    ms = dict(norm_mix=m_norm_mix, norm_ffn=m_norm_ffn, even_w_in=m_even_w_in, even_v_ln_g=m_even_v_ln_g,
              even_v_ln_b=m_even_v_ln_b, even_w_spatial=m_even_w_spatial, even_b_spatial=m_even_b_spatial,
              even_conv_w=m_even_conv_w, even_w_out=m_even_w_out, attn_w_qkv=m_attn_w_qkv, attn_sink=m_attn_sink,
              rel_bias=m_rel_bias, attn_w_out=m_attn_w_out, ffn_w_gate=m_ffn_w_gate, ffn_w_up=m_ffn_w_up,
              ffn_w_down=m_ffn_w_down, final_norm=m_final_norm)
    vs = dict(norm_mix=v_norm_mix, norm_ffn=v_norm_ffn, even_w_in=v_even_w_in, even_v_ln_g=v_even_v_ln_g,
              even_v_ln_b=v_even_v_ln_b, even_w_spatial=v_even_w_spatial, even_b_spatial=v_even_b_spatial,
              even_conv_w=v_even_conv_w, even_w_out=v_even_w_out, attn_w_qkv=v_attn_w_qkv, attn_sink=v_attn_sink,
              rel_bias=v_rel_bias, attn_w_out=v_attn_w_out, ffn_w_gate=v_ffn_w_gate, ffn_w_up=v_ffn_w_up,
              ffn_w_down=v_ffn_w_down, final_norm=v_final_norm)
    big = dict(ffn_w_gate=([r_wgT0, r_wgT1], True), even_w_in=([r_winT], True), even_w_out=([r_woe], False),
               attn_w_qkv=([r_wqkvT], True), attn_w_out=([r_woa], False), ffn_w_up=([r_wuT0, r_wuT1], True),
               ffn_w_down=([r_wd0, r_wd1], False))
    delta, new_m, new_v = {}, {}, {}
    late_slots = None
    for n, (recvs, transposed) in big.items():
        lay = (lambda a: jnp.swapaxes(a, 1, 2)) if transposed else (lambda a: a)
        spec = _finish_weight(recvs, lay(ws[n]), lay(ms[n]), lay(vs[n]), "finish_" + n)
        if late_slots is None:
            outs, late_slots = _call(spec, _BroadcastCarry([d_nmix0, d_relb]))
        else:
            outs, _ = _call(spec)
        grads[n], delta[n], new_m[n], new_v[n] = [lay(o) for o in outs]
    shaped = lambda n, a: in_full(a) if n == "even_conv_w" else (a.reshape(1, dm) if n == "final_norm" else a)
    pick = lambda dct: [shaped(n, dct[n]) for n in small_names]
    lates = [(small_names.index("norm_mix"), late_slots[0]), (small_names.index("rel_bias"), late_slots[1])]
    results, loss11 = _adamw_small(pick(ws), pick(ms), pick(vs), small_slots[:-1], lates, small_slots[-1],
                                   "adamw_small")
    mine = lambda a: lax.dynamic_slice(a, (0, me * (bw // NDEV)), (3, bw // NDEV))[None]
    for n, res in zip(small_names, results):
        for dst, a in zip((grads, delta, new_m, new_v), res):
            dst[n] = mine(a) if n == "even_conv_w" else (a.reshape(dm) if n == "final_norm" else a)
    loss = loss11[0, 0]
    return (loss, dx0[None], *[grads[n] for n in order], *[delta[n] for n in order],
            *[new_m[n] for n in order], *[new_v[n] for n in order])
```

```python
import math

import jax
import jax.numpy as jnp
import numpy as np
from jax import lax
from jax.experimental import pallas as pl
from jax.experimental.pallas import tpu as pltpu

F32, BF16 = jnp.float32, jnp.bfloat16
NDEV = 8
EPS = 1e-6
CHUNK = 128
A_GROUPS = 4
N_HEADS, N_KV, HEAD_DIM = 16, 4, 64
N_BUCKETS, MAX_DISTANCE = 32, 128
NEG = -1e30
LOG2E = 1.4426950408889634
ADAM_LR, ADAM_B1, ADAM_B2, ADAM_EPS, ADAM_WD, ADAM_STEP = 0.001, 0.9, 0.999, 1e-08, 0.01, 10
VMEM_LIMIT = 56 * 1024 * 1024
MESH = pl.DeviceIdType.MESH
NT = (((1,), (1,)), ((), ()))
NN = (((1,), (0,)), ((), ()))
TN = (((0,), (0,)), ((), ()))
ANY = pl.BlockSpec(memory_space=pl.ANY)


def _cp(n_grid=1):
    return pltpu.CompilerParams(dimension_semantics=("arbitrary",) * n_grid, vmem_limit_bytes=VMEM_LIMIT)


def _dot(a, b, dims):
    return lax.dot_general(a, b, dims, preferred_element_type=F32)


def _my_index():
    return 4 * lax.axis_index("x") + 2 * lax.axis_index("y") + lax.axis_index("c")


def _peer(k):
    x, y, c = lax.axis_index("x"), lax.axis_index("y"), lax.axis_index("c")
    px = 1 - x if k & 4 else x
    py = 1 - y if k & 2 else y
    pc = 1 - c if k & 1 else c
    return (px, py, pc)


def _load_weight(gath_ref, wbuf, sems):
    rows = gath_ref.shape[1]
    cps = [pltpu.make_async_copy(gath_ref.at[d], wbuf.at[pl.ds(d * rows, rows), :], sems.at[d]) for d in range(NDEV)]
    for c in cps:
        c.start()
    for c in cps:
        c.wait()


class _GatherCarry:
    def __init__(self, pieces):
        self.inputs = list(pieces)
        self.n = len(pieces)
        self.out_shape = [jax.ShapeDtypeStruct((NDEV,) + p.shape, p.dtype) for p in pieces]
        self.scratch = [pltpu.SemaphoreType.DMA((7 * self.n,)), pltpu.SemaphoreType.DMA((7 * self.n,)),
                        pltpu.SemaphoreType.DMA((self.n,))]

    def _ctx(self):
        x, y, c = lax.axis_index("x"), lax.axis_index("y"), lax.axis_index("c")
        chips = [(1 - x, y), (x, 1 - y), (1 - x, 1 - y)]
        return (x, y, c), (x, y, 1 - c), chips, c

    def _copy(self, k, j, block, to, ins, outs, sems, src=None):
        send_sems, recv_sems, _ = sems
        slot = outs[j].at[4 * block[0] + 2 * block[1] + block[2]]
        return pltpu.make_async_remote_copy(
            src_ref=slot if src is None else src, dst_ref=slot, send_sem=send_sems.at[k * self.n + j],
            recv_sem=recv_sems.at[k * self.n + j], device_id=to, device_id_type=MESH)

    def start(self, ins, outs, sems):
        me, sibling, chips, c = self._ctx()
        for j in range(self.n):
            pltpu.make_async_copy(ins[j], outs[j].at[4 * me[0] + 2 * me[1] + me[2]], sems[2].at[j]).start()
            self._copy(0, j, me, sibling, ins, outs, sems, src=ins[j]).start()
            for q, chip in enumerate(chips):
                self._copy(1 + q, j, me, (*chip, c), ins, outs, sems, src=ins[j]).start()

    def mid(self, ins, outs, sems):
        me, sibling, chips, c = self._ctx()
        for q, chip in enumerate(chips):
            for j in range(self.n):
                self._copy(1 + q, j, (*chip, c), me, ins, outs, sems).wait_recv()
                self._copy(4 + q, j, (*chip, c), sibling, ins, outs, sems).start()

    def finish(self, ins, outs, sems):
        me, sibling, chips, c = self._ctx()
        for j in range(self.n):
            self._copy(0, j, sibling, me, ins, outs, sems).wait_recv()
            for q, chip in enumerate(chips):
                self._copy(4 + q, j, (*chip, 1 - c), me, ins, outs, sems).wait_recv()
        for j in range(self.n):
            self._copy(0, j, me, sibling, ins, outs, sems, src=ins[j]).wait_send()
            for q, chip in enumerate(chips):
                self._copy(1 + q, j, me, (*chip, c), ins, outs, sems, src=ins[j]).wait_send()
                self._copy(4 + q, j, (*chip, c), sibling, ins, outs, sems).wait_send()
            pltpu.make_async_copy(ins[j], outs[j].at[0], sems[2].at[j]).wait()


class _GradCarry:
    def __init__(self, pieces):
        self.inputs = list(pieces)
        self.n = len(pieces)
        self.rows = [p.shape[0] // NDEV for p in pieces]
        self.out_shape = [jax.ShapeDtypeStruct((NDEV, r, p.shape[1]), p.dtype) for p, r in zip(pieces, self.rows)]
        self.scratch = [pltpu.SemaphoreType.DMA((7 * self.n,)), pltpu.SemaphoreType.DMA((7 * self.n,)),
                        pltpu.SemaphoreType.DMA((self.n,))]

    def _copies(self, ins, outs, sems):
        me = _my_index()
        local, remote = [], []
        for j in range(self.n):
            r = self.rows[j]
            local.append(pltpu.make_async_copy(ins[j].at[pl.ds(pl.multiple_of(me * r, 16), r), :], outs[j].at[me],
                                               sems[2].at[j]))
            for k in range(1, NDEV):
                peer = _peer(k)
                pidx = 4 * peer[0] + 2 * peer[1] + peer[2]
                remote.append(pltpu.make_async_remote_copy(
                    src_ref=ins[j].at[pl.ds(pl.multiple_of(pidx * r, 16), r), :], dst_ref=outs[j].at[me],
                    send_sem=sems[0].at[(k - 1) * self.n + j], recv_sem=sems[1].at[(k - 1) * self.n + j],
                    device_id=peer, device_id_type=MESH))
        return local, remote

    def start(self, ins, outs, sems):
        local, remote = self._copies(ins, outs, sems)
        for cp in local + remote:
            cp.start()

    def mid(self, ins, outs, sems):
        pass

    def finish(self, ins, outs, sems):
        local, remote = self._copies(ins, outs, sems)
        for cp in remote + local:
            cp.wait()


class _BroadcastCarry:
    def __init__(self, parts):
        self.inputs = list(parts)
        self.n = len(self.inputs)
        self.out_shape = [jax.ShapeDtypeStruct((NDEV,) + p.shape, p.dtype) for p in self.inputs]
        self.scratch = [pltpu.SemaphoreType.DMA((7 * self.n,)), pltpu.SemaphoreType.DMA((7 * self.n,)),
                        pltpu.SemaphoreType.DMA((self.n,))]

    def _copies(self, ins, outs, sems):
        me = _my_index()
        cps = []
        for j in range(self.n):
            cps.append(pltpu.make_async_copy(ins[j], outs[j].at[me], sems[2].at[j]))
            cps += [pltpu.make_async_remote_copy(
                src_ref=ins[j], dst_ref=outs[j].at[me], send_sem=sems[0].at[(k - 1) * self.n + j],
                recv_sem=sems[1].at[(k - 1) * self.n + j], device_id=_peer(k), device_id_type=MESH)
                for k in range(1, NDEV)]
        return cps

    def start(self, ins, outs, sems):
        for cp in self._copies(ins, outs, sems):
            cp.start()

    def mid(self, ins, outs, sems):
        pass

    def finish(self, ins, outs, sems):
        for cp in self._copies(ins, outs, sems):
            cp.wait()


class _PairCarry:
    def __init__(self, piece):
        self.inputs = [piece]
        self.r = piece.shape[0] // NDEV
        self.out_shape = [jax.ShapeDtypeStruct((4, self.r, piece.shape[1]), piece.dtype)]
        self.scratch = [pltpu.SemaphoreType.DMA((4,)), pltpu.SemaphoreType.DMA((4,))]

    def _copies(self, ins, outs, sems):
        x, y, c = lax.axis_index("x"), lax.axis_index("y"), lax.axis_index("c")
        return [pltpu.make_async_remote_copy(
            src_ref=ins[0].at[pl.ds(pl.multiple_of((2 * q + 1 - c) * self.r, 16), self.r), :], dst_ref=outs[0].at[q],
            send_sem=sems[0].at[q], recv_sem=sems[1].at[q], device_id=(x, y, 1 - c), device_id_type=MESH)
            for q in range(4)]

    def start(self, ins, outs, sems):
        for cp in self._copies(ins, outs, sems):
            cp.start()

    def mid(self, ins, outs, sems):
        pass

    def finish(self, ins, outs, sems):
        for cp in self._copies(ins, outs, sems):
            cp.wait()


class _ChipSumCarry:
    def __init__(self, piece, landed):
        self.inputs = [piece, landed]
        self.r, dm = piece.shape[0] // NDEV, piece.shape[1]
        self.out_shape = [jax.ShapeDtypeStruct((4, self.r, dm), piece.dtype)]
        self.scratch = [pltpu.VMEM((4, self.r, dm), piece.dtype), pltpu.VMEM((8, self.r, dm), piece.dtype),
                        pltpu.SemaphoreType.DMA((8,)), pltpu.SemaphoreType.DMA((3,)), pltpu.SemaphoreType.DMA((3,)),
                        pltpu.SemaphoreType.DMA(())]

    def _copies(self, outs, scr):
        sums, _, _, send_sems, recv_sems, local_sem = scr
        x, y, c = lax.axis_index("x"), lax.axis_index("y"), lax.axis_index("c")
        mine = 2 * x + y
        local = pltpu.make_async_copy(sums.at[mine], outs[0].at[mine], local_sem)
        remote = []
        for k in range(1, 4):
            px = 1 - x if k & 2 else x
            py = 1 - y if k & 1 else y
            remote.append(pltpu.make_async_remote_copy(
                src_ref=sums.at[2 * px + py], dst_ref=outs[0].at[mine], send_sem=send_sems.at[k - 1],
                recv_sem=recv_sems.at[k - 1], device_id=(px, py, c), device_id_type=MESH))
        return local, remote

    def start(self, ins, outs, scr):
        sums, stage, stage_sems = scr[0], scr[1], scr[2]
        c = lax.axis_index("c")
        loads = []
        for q in range(4):
            loads.append((
                pltpu.make_async_copy(ins[0].at[pl.ds(pl.multiple_of((2 * q + c) * self.r, 16), self.r), :],
                                      stage.at[2 * q], stage_sems.at[2 * q]),
                pltpu.make_async_copy(ins[1].at[q], stage.at[2 * q + 1], stage_sems.at[2 * q + 1])))
        for a, b in loads:
            a.start()
            b.start()
        for q, (a, b) in enumerate(loads):
            a.wait()
            b.wait()
            sums[q] = (stage[2 * q].astype(F32) + stage[2 * q + 1].astype(F32)).astype(sums.dtype)
        local, remote = self._copies(outs, scr)
        for cp in [local] + remote:
            cp.start()

    def mid(self, ins, outs, scr):
        pass

    def finish(self, ins, outs, scr):
        local, remote = self._copies(outs, scr)
        for cp in remote + [local]:
            cp.wait()


def _call(spec, carry=None):
    body, grid = spec["body"], spec["grid"]
    in_specs, out_specs, out_shape = list(spec["in_specs"]), list(spec["out_specs"]), list(spec["out_shape"])
    scratch, args = list(spec.get("scratch", [])), list(spec["args"])
    if carry is None:
        out = pl.pallas_call(body, grid=grid, in_specs=in_specs, out_specs=tuple(out_specs),
                             out_shape=tuple(out_shape), scratch_shapes=scratch, compiler_params=_cp(len(grid)),
                             name=spec["name"])(*args)
        return tuple(out), ()
    carries = list(carry) if isinstance(carry, (list, tuple)) else [carry]
    n_in, n_out, n_s = len(in_specs), len(out_specs), len(scratch)
    steps = int(np.prod(grid))

    def split(refs, counts):
        parts, o = [], 0
        for cnt in counts:
            parts.append(refs[o:o + cnt])
            o += cnt
        return parts

    c_in = [len(cr.inputs) for cr in carries]
    c_out = [len(cr.out_shape) for cr in carries]
    c_scr = [len(cr.scratch) for cr in carries]

    def wrapped(*refs):
        ins, cins, outs, couts, scr, cscr = split(refs, [n_in, sum(c_in), n_out, sum(c_out), n_s, sum(c_scr)])
        per = list(zip(carries, split(cins, c_in), split(couts, c_out), split(cscr, c_scr)))
        step = pl.program_id(0)
        for ax in range(1, len(grid)):
            step = step * grid[ax] + pl.program_id(ax)

        @pl.when(step == 0)
        def _():
            for cr, ci, co, cs in per:
                cr.start(ci, co, cs)
        if steps >= 3:
            @pl.when(step == steps - 2)
            def _():
                for cr, ci, co, cs in per:
                    cr.mid(ci, co, cs)
        body(*ins, *outs, *scr)

        @pl.when(step == steps - 1)
        def _():
            for cr, ci, co, cs in per:
                if steps < 3:
                    cr.mid(ci, co, cs)
                cr.finish(ci, co, cs)

    out = pl.pallas_call(
        wrapped, grid=grid, in_specs=in_specs + [ANY] * sum(c_in), out_specs=tuple(out_specs + [ANY] * sum(c_out)),
        out_shape=tuple(out_shape + [s for cr in carries for s in cr.out_shape]),
        scratch_shapes=scratch + [s for cr in carries for s in cr.scratch],
        compiler_params=_cp(len(grid)), name=spec["name"])(*args, *[a for cr in carries for a in cr.inputs])
    c_res = [tuple(p) for p in split(out[n_out:], c_out)]
    return tuple(out[:n_out]), (c_res if isinstance(carry, (list, tuple)) else c_res[0])


def _rms_fwd(x, gain):
    r = lax.rsqrt(jnp.mean(x * x, axis=-1, keepdims=True) + EPS)
    return x * r * gain, r


def _rms_bwd(dh, x, r, gain):
    a = dh * gain
    dx = r * a - x * (r * r * r) * jnp.mean(a * x, axis=-1, keepdims=True)
    dgain = jnp.sum(dh * (x * r), axis=0, keepdims=True)
    return dx, dgain


def _gelu(x):
    return 0.5 * x * (1.0 + lax.erf(x * 0.7071067811865476))


def _gelu_grad(x):
    return 0.5 * (1.0 + lax.erf(x * 0.7071067811865476)) + x * jnp.exp(-0.5 * x * x) * 0.3989422804014327


def _sigmoid(x):
    return 1.0 / (1.0 + jnp.exp(-x))


def _adamw_math(w, g, m, v):
    nm = ADAM_B1 * m + (1.0 - ADAM_B1) * g
    nv = ADAM_B2 * v + (1.0 - ADAM_B2) * (g * g)
    m_hat = nm / (1.0 - ADAM_B1 ** ADAM_STEP)
    v_hat = nv / (1.0 - ADAM_B2 ** ADAM_STEP)
    return -ADAM_LR * (m_hat / (jnp.sqrt(v_hat) + ADAM_EPS) + ADAM_WD * w), nm, nv


def _tok(tm, w):
    return pl.BlockSpec((tm, w), lambda i: (i, 0))


def _full(shape):
    return pl.BlockSpec(shape, lambda *i: (0,) * len(shape))


def _norm_proj(x, gain, gath, out_dtype, name, tm):
    t, dm = x.shape
    n = gath.shape[1] * NDEV

    def body(x_ref, g_ref, gath_ref, proj_ref, hb_ref, wbuf, sems):
        @pl.when(pl.program_id(0) == 0)
        def _():
            _load_weight(gath_ref, wbuf, sems)
        h, _ = _rms_fwd(x_ref[...], g_ref[...])
        hb = h.astype(BF16)
        hb_ref[...] = hb
        proj_ref[...] = _dot(hb, wbuf[...], NT).astype(out_dtype)

    return dict(
        body=body, grid=(t // tm,), name=name, args=[x, gain, gath],
        out_shape=[jax.ShapeDtypeStruct((t, n), out_dtype), jax.ShapeDtypeStruct((t, dm), BF16)],
        in_specs=[_tok(tm, dm), _full((1, dm)), ANY], out_specs=[_tok(tm, n), _tok(tm, dm)],
        scratch=[pltpu.VMEM((n, dm), BF16), pltpu.SemaphoreType.DMA((NDEV,))])


def _proj_bwd_norm(dys, x, gain, dres, gath, name, tm):
    t, dm = x.shape
    n = gath.shape[1] * NDEV
    widths = [d.shape[1] for d in dys]
    assert sum(widths) == n
    nd = len(dys)

    def body(*refs):
        dy_refs = refs[:nd]
        x_ref, g_ref, dres_ref, gath_ref, dx_ref, dxb_ref, dgain_ref, wbuf, sems = refs[nd:]

        @pl.when(pl.program_id(0) == 0)
        def _():
            _load_weight(gath_ref, wbuf, sems)
            dgain_ref[...] = jnp.zeros_like(dgain_ref)
        xv, gain_v = x_ref[...], g_ref[...]
        _, r = _rms_fwd(xv, gain_v)
        dh, c0 = None, 0
        for dy_ref, wd in zip(dy_refs, widths):
            part = _dot(dy_ref[...], wbuf[c0:c0 + wd, :], NN)
            dh = part if dh is None else dh + part
            c0 += wd
        dx, dgain = _rms_bwd(dh, xv, r, gain_v)
        dx = dres_ref[...] + dx
        dx_ref[...] = dx
        dxb_ref[...] = dx.astype(BF16)
        dgain_ref[...] += dgain

    return dict(
        body=body, grid=(t // tm,), name=name, args=[*dys, x, gain, dres, gath],
        out_shape=[jax.ShapeDtypeStruct((t, dm), F32), jax.ShapeDtypeStruct((t, dm), BF16),
                   jax.ShapeDtypeStruct((1, dm), F32)],
        in_specs=[_tok(tm, wd) for wd in widths] + [_tok(tm, dm), _full((1, dm)), _tok(tm, dm), ANY],
        out_specs=[_tok(tm, dm), _tok(tm, dm), _full((1, dm))],
        scratch=[pltpu.VMEM((n, dm), BF16), pltpu.SemaphoreType.DMA((NDEV,))])


def _wgrad(a, b, name, tmm=256):
    parts = list(a) if isinstance(a, (list, tuple)) else [a]
    t = parts[0].shape[0]
    n = b.shape[1]
    tiles = [p.shape[1] // tmm for p in parts]
    first = [sum(tiles[:i]) for i in range(len(parts))]
    m = sum(tiles) * tmm

    def body(*refs):
        a_refs, b_ref, o_ref = refs[:len(parts)], refs[len(parts)], refs[len(parts) + 1]
        j = pl.program_id(0)
        for a_ref, j0, nt in zip(a_refs, first, tiles):
            if len(parts) == 1:
                o_ref[...] = _dot(a_ref[...], b_ref[...], TN).astype(BF16)
            else:
                @pl.when((j >= j0) & (j < j0 + nt))
                def _():
                    o_ref[...] = _dot(a_ref[...], b_ref[...], TN).astype(BF16)

    a_specs = [pl.BlockSpec((t, tmm), lambda j, j0=j0, nt=nt: (0, jnp.clip(j - j0, 0, nt - 1)))
               for j0, nt in zip(first, tiles)]
    return dict(
        body=body, grid=(sum(tiles),), name=name, args=[*parts, b], out_shape=[jax.ShapeDtypeStruct((m, n), BF16)],
        in_specs=a_specs + [pl.BlockSpec((t, n), lambda j: (0, 0))],
        out_specs=[pl.BlockSpec((tmm, n), lambda j: (j, 0))])


def _wgrad_pair(a1, a2, b, name, tmm=256):
    t, m = a1.shape
    n = b.shape[1]
    nt = m // tmm

    def body(a1_ref, a2_ref, b_ref, o1_ref, o2_ref):
        j = pl.program_id(0)

        @pl.when(j < nt)
        def _():
            o1_ref[...] = _dot(a1_ref[...], b_ref[...], TN).astype(BF16)

        @pl.when(j >= nt)
        def _():
            o2_ref[...] = _dot(a2_ref[...], b_ref[...], TN).astype(BF16)

    first = lambda j: jnp.minimum(j, nt - 1)
    second = lambda j: jnp.maximum(j - nt, 0)
    o = jax.ShapeDtypeStruct((m, n), BF16)
    return dict(
        body=body, grid=(2 * nt,), name=name, args=[a1, a2, b], out_shape=[o, o],
        in_specs=[pl.BlockSpec((t, tmm), lambda j: (0, first(j))), pl.BlockSpec((t, tmm), lambda j: (0, second(j))),
                  pl.BlockSpec((t, n), lambda j: (0, 0))],
        out_specs=[pl.BlockSpec((tmm, n), lambda j: (first(j), 0)), pl.BlockSpec((tmm, n), lambda j: (second(j), 0))])


def _halo_specs(tm, t, width, col_blocks):
    nb8 = tm // 8
    last = t // 8 - 1
    prev = [pl.BlockSpec((8, width), lambda i, cb=cb: (jnp.maximum(i * nb8 - 1, 0), cb)) for cb in col_blocks]
    nxt = [pl.BlockSpec((8, width), lambda i, cb=cb: (jnp.minimum((i + 1) * nb8, last), cb)) for cb in col_blocks]
    return prev, nxt


def _shift_rows(z, prev_row, next_row):
    tm = z.shape[0]
    row = lax.broadcasted_iota(jnp.int32, z.shape, 0)
    zm1 = jnp.where(row == 0, prev_row, pltpu.roll(z, 1, 0))
    zp1 = jnp.where(row == tm - 1, next_row, pltpu.roll(z, tm - 1, 0))
    return zm1, zp1


def _gating_fwd(proj, lng, lnb, wsp_ref, bsp_ref, aw):
    tm = proj.shape[0]
    a_u = _gelu(proj[:, 0:aw])
    gv = _gelu(proj[:, aw:2 * aw])
    mu = jnp.mean(gv, axis=-1, keepdims=True)
    xc = gv - mu
    rstd = lax.rsqrt(jnp.mean(xc * xc, axis=-1, keepdims=True) + EPS)
    vn = xc * rstd
    a_v = (vn * lng + lnb).astype(BF16)
    gd = aw // A_GROUPS
    rows = []
    for c in range(tm // CHUNK):
        cols = []
        for g in range(A_GROUPS):
            blk = a_v[c * CHUNK:(c + 1) * CHUNK, g * gd:(g + 1) * gd]
            cols.append(_dot(wsp_ref[g], blk, NN) + bsp_ref[g])
        rows.append(jnp.concatenate(cols, axis=1))
    mixed = jnp.concatenate(rows, axis=0)
    return a_u, vn, rstd, a_v, mixed


def _even_core_fwd(proj, x0, lng, lnb, wsp, bspb, cw, gath, tm):
    t, dm = x0.shape
    aw = lng.shape[1]
    bw = cw.shape[1]
    assert aw == bw and 2 * aw + 3 * bw == proj.shape[1]
    nt = t // tm
    prev, nxt = _halo_specs(tm, t, bw, [3, 4])

    def body(proj_ref, cp_ref, hp_ref, cn_ref, hn_ref, x0_ref, lng_ref, lnb_ref, wsp_ref, bsp_ref, cw_ref, gath_ref,
             x1_ref, y_ref, wbuf, sems):
        i = pl.program_id(0)

        @pl.when(i == 0)
        def _():
            _load_weight(gath_ref, wbuf, sems)
        proj_v = proj_ref[...]
        a_u, _, _, _, mixed = _gating_fwd(proj_v, lng_ref[...], lnb_ref[...], wsp_ref, bsp_ref, aw)
        a_out = a_u * mixed
        bb = proj_v[:, 2 * aw:2 * aw + bw]
        z = proj_v[:, 2 * aw + bw:2 * aw + 2 * bw] * proj_v[:, 2 * aw + 2 * bw:]
        zprev = jnp.where(i > 0, cp_ref[7:8, :] * hp_ref[7:8, :], 0.0)
        znext = jnp.where(i < nt - 1, cn_ref[0:1, :] * hn_ref[0:1, :], 0.0)
        zm1, zp1 = _shift_rows(z, zprev, znext)
        cwv = cw_ref[...]
        conv = zm1 * cwv[0:1, :] + z * cwv[1:2, :] + zp1 * cwv[2:3, :]
        y = jnp.concatenate([a_out, bb * conv], axis=1).astype(BF16)
        y_ref[...] = y
        x1_ref[...] = x0_ref[...] + _dot(y, wbuf[...], NN)

    return dict(
        body=body, grid=(nt,), name="even_core_fwd",
        args=[proj, proj, proj, proj, proj, x0, lng, lnb, wsp, bspb, cw, gath],
        out_shape=[jax.ShapeDtypeStruct((t, dm), F32), jax.ShapeDtypeStruct((t, aw + bw), BF16)],
        in_specs=[_tok(tm, proj.shape[1]), prev[0], prev[1], nxt[0], nxt[1], _tok(tm, dm), _full(lng.shape),
                  _full(lnb.shape), _full(wsp.shape), _full(bspb.shape), _full(cw.shape), ANY],
        out_specs=[_tok(tm, dm), _tok(tm, aw + bw)],
        scratch=[pltpu.VMEM((gath.shape[1] * NDEV, dm), BF16), pltpu.SemaphoreType.DMA((NDEV,))])


def _even_core_bwd(proj, dx1, lng, lnb, wsp, bspb, cw, gath, tm):
    t, dm = dx1.shape
    aw, bw = lng.shape[1], cw.shape[1]
    gd = aw // A_GROUPS
    nt = t // tm
    inw = proj.shape[1]
    prev, nxt = _halo_specs(tm, t, bw, [2, 3, 4])
    nb8 = tm // 8
    last8 = t // 8 - 1

    def body(proj_ref, bp_ref, cp_ref, hp_ref, bn_ref, cn_ref, hn_ref, dx_ref, dxp_ref, dxn_ref,
             lng_ref, lnb_ref, wsp_ref, bsp_ref, cw_ref, gath_ref,
             dproj_ref, dlng_ref, dlnb_ref, dwsp_ref, dbsp_ref, dcw_ref, wbuf, sems):
        i = pl.program_id(0)

        @pl.when(i == 0)
        def _():
            _load_weight(gath_ref, wbuf, sems)
            dlng_ref[...] = jnp.zeros_like(dlng_ref)
            dlnb_ref[...] = jnp.zeros_like(dlnb_ref)
            dwsp_ref[...] = jnp.zeros_like(dwsp_ref)
            dbsp_ref[...] = jnp.zeros_like(dbsp_ref)
            dcw_ref[...] = jnp.zeros_like(dcw_ref)
        proj_v = proj_ref[...]
        lng_v = lng_ref[...]
        a_u, vn, rstd, a_v, mixed = _gating_fwd(proj_v, lng_v, lnb_ref[...], wsp_ref, bsp_ref, aw)
        w = wbuf[...]
        dy = _dot(dx_ref[...].astype(BF16), w, NT)
        da_out, db_out = dy[:, 0:aw], dy[:, aw:]
        da_u = da_out * mixed
        dmixed = da_out * a_u
        dmb = dmixed.astype(BF16)
        rows = []
        for c in range(tm // CHUNK):
            cols = []
            for g in range(A_GROUPS):
                r0, c0 = c * CHUNK, g * gd
                dm_cg = dmb[r0:r0 + CHUNK, c0:c0 + gd]
                cols.append(_dot(wsp_ref[g], dm_cg, TN))
                dwsp_ref[g] += _dot(dm_cg, a_v[r0:r0 + CHUNK, c0:c0 + gd], NT)
                dbsp_ref[g] += dmixed[r0:r0 + CHUNK, c0:c0 + gd]
            rows.append(jnp.concatenate(cols, axis=1))
        dav = jnp.concatenate(rows, axis=0)
        dlng_ref[...] += jnp.sum(dav * vn, axis=0, keepdims=True)
        dlnb_ref[...] += jnp.sum(dav, axis=0, keepdims=True)
        dvn = dav * lng_v
        dgv = rstd * (dvn - jnp.mean(dvn, axis=-1, keepdims=True) - vn * jnp.mean(dvn * vn, axis=-1, keepdims=True))
        dv_pre = dgv * _gelu_grad(proj_v[:, aw:2 * aw])
        du_pre = da_u * _gelu_grad(proj_v[:, 0:aw])
        bb = proj_v[:, 2 * aw:2 * aw + bw]
        bc = proj_v[:, 2 * aw + bw:2 * aw + 2 * bw]
        bh = proj_v[:, 2 * aw + 2 * bw:]
        z = bc * bh
        zprev = jnp.where(i > 0, cp_ref[7:8, :] * hp_ref[7:8, :], 0.0)
        znext = jnp.where(i < nt - 1, cn_ref[0:1, :] * hn_ref[0:1, :], 0.0)
        zm1, zp1 = _shift_rows(z, zprev, znext)
        cwv = cw_ref[...]
        conv = zm1 * cwv[0:1, :] + z * cwv[1:2, :] + zp1 * cwv[2:3, :]
        dbb = db_out * conv
        dconv = db_out * bb
        dx_edge = jnp.concatenate([dxp_ref[...], dxn_ref[...]], axis=0).astype(BF16)
        dy_edge = _dot(dx_edge, w[aw:, :], NT)
        dcprev = jnp.where(i > 0, dy_edge[7:8, :] * bp_ref[7:8, :], 0.0)
        dcnext = jnp.where(i < nt - 1, dy_edge[8:9, :] * bn_ref[0:1, :], 0.0)
        dcm1, dcp1 = _shift_rows(dconv, dcprev, dcnext)
        dz = dcp1 * cwv[0:1, :] + dconv * cwv[1:2, :] + dcm1 * cwv[2:3, :]
        dcw_ref[0:1, :] += jnp.sum(dconv * zm1, axis=0, keepdims=True)
        dcw_ref[1:2, :] += jnp.sum(dconv * z, axis=0, keepdims=True)
        dcw_ref[2:3, :] += jnp.sum(dconv * zp1, axis=0, keepdims=True)
        dproj_ref[...] = jnp.concatenate([du_pre, dv_pre, dbb, dz * bh, dz * bc], axis=1).astype(BF16)

    row8 = lambda f: pl.BlockSpec((8, dm), f)
    return dict(
        body=body, grid=(nt,), name="even_core_bwd",
        args=[proj, proj, proj, proj, proj, proj, proj, dx1, dx1, dx1, lng, lnb, wsp, bspb, cw, gath],
        out_shape=[jax.ShapeDtypeStruct((t, inw), BF16), jax.ShapeDtypeStruct((1, aw), F32),
                   jax.ShapeDtypeStruct((1, aw), F32), jax.ShapeDtypeStruct(wsp.shape, F32),
                   jax.ShapeDtypeStruct((A_GROUPS, CHUNK, gd), F32), jax.ShapeDtypeStruct(cw.shape, F32)],
        in_specs=[_tok(tm, inw), prev[0], prev[1], prev[2], nxt[0], nxt[1], nxt[2], _tok(tm, dm),
                  row8(lambda i: (jnp.maximum(i * nb8 - 1, 0), 0)), row8(lambda i: (jnp.minimum((i + 1) * nb8, last8), 0)),
                  _full(lng.shape), _full(lnb.shape), _full(wsp.shape), _full(bspb.shape), _full(cw.shape), ANY],
        out_specs=[_tok(tm, inw), _full((1, aw)), _full((1, aw)), _full(wsp.shape),
                   _full((A_GROUPS, CHUNK, gd)), _full(cw.shape)],
        scratch=[pltpu.VMEM((gath.shape[1] * NDEV, dm), BF16), pltpu.SemaphoreType.DMA((NDEV,))])


def _ff_chunks(f, width=1024):
    return [(c0, min(c0 + width, f)) for c0 in range(0, f, width)]


def _ffn_up(x, gain, gath_g, gath_u, name, tm):
    t, dm = x.shape
    f = gath_g.shape[1] * NDEV

    def body(x_ref, g_ref, gg_ref, gu_ref, gate_ref, up_ref, act_ref, wg, wu, sems):
        @pl.when(pl.program_id(0) == 0)
        def _():
            _load_weight(gg_ref, wg, sems)
            _load_weight(gu_ref, wu, sems)
        h, _ = _rms_fwd(x_ref[...], g_ref[...])
        hb = h.astype(BF16)
        for c0, c1 in _ff_chunks(f):
            gate = _dot(hb, wg[c0:c1, :], NT)
            up = _dot(hb, wu[c0:c1, :], NT)
            gate_ref[:, c0:c1] = gate.astype(BF16)
            up_ref[:, c0:c1] = up.astype(BF16)
            act_ref[:, c0:c1] = (gate * _sigmoid(gate) * up).astype(BF16)

    o = jax.ShapeDtypeStruct((t, f), BF16)
    return dict(
        body=body, grid=(t // tm,), name=name, args=[x, gain, gath_g, gath_u], out_shape=[o, o, o],
        in_specs=[_tok(tm, dm), _full((1, dm)), ANY, ANY], out_specs=[_tok(tm, f)] * 3,
        scratch=[pltpu.VMEM((f, dm), BF16), pltpu.VMEM((f, dm), BF16), pltpu.SemaphoreType.DMA((NDEV,))])


def _ffn_down(x, act, gath_d, name, tm):
    t, dm = x.shape
    f = act.shape[1]

    def body(x_ref, a_ref, gd_ref, xo_ref, wd, sems):
        @pl.when(pl.program_id(0) == 0)
        def _():
            _load_weight(gd_ref, wd, sems)
        xo_ref[...] = x_ref[...] + _dot(a_ref[...], wd[...], NN)

    return dict(
        body=body, grid=(t // tm,), name=name, args=[x, act, gath_d], out_shape=[jax.ShapeDtypeStruct((t, dm), F32)],
        in_specs=[_tok(tm, dm), _tok(tm, f), ANY], out_specs=[_tok(tm, dm)],
        scratch=[pltpu.VMEM((f, dm), BF16), pltpu.SemaphoreType.DMA((NDEV,))])


def _ffn_down_loss(x, act, gath_d, target, gain, name, tm):
    t, dm = x.shape
    f = act.shape[1]
    steps = t // tm

    def body(x_ref, a_ref, gd_ref, t_ref, g_ref, loss_ref, dx_ref, dxb_ref, dgain_ref, wd, acc, sems):
        i = pl.program_id(0)

        @pl.when(i == 0)
        def _():
            _load_weight(gd_ref, wd, sems)
            acc[...] = jnp.zeros_like(acc)
            dgain_ref[...] = jnp.zeros_like(dgain_ref)
        xv = x_ref[...] + _dot(a_ref[...], wd[...], NN)
        gain_v = g_ref[...]
        y, r = _rms_fwd(xv, gain_v)
        e = y - t_ref[...]
        acc[...] += jnp.sum(e * e, axis=0, keepdims=True)
        dx, dgain = _rms_bwd(e * (1.0 / dm), xv, r, gain_v)
        dx_ref[...] = dx
        dxb_ref[...] = dx.astype(BF16)
        dgain_ref[...] += dgain

        @pl.when(i == steps - 1)
        def _():
            loss_ref[...] = jnp.sum(acc[...], axis=-1, keepdims=True) * (0.5 / dm)

    return dict(
        body=body, grid=(steps,), name=name, args=[x, act, gath_d, target, gain],
        out_shape=[jax.ShapeDtypeStruct((1, 1), F32), jax.ShapeDtypeStruct((t, dm), F32),
                   jax.ShapeDtypeStruct((t, dm), BF16), jax.ShapeDtypeStruct((1, dm), F32)],
        in_specs=[_tok(tm, dm), _tok(tm, f), ANY, _tok(tm, dm), _full((1, dm))],
        out_specs=[_full((1, 1)), _tok(tm, dm), _tok(tm, dm), _full((1, dm))],
        scratch=[pltpu.VMEM((f, dm), BF16), pltpu.VMEM((1, dm), F32), pltpu.SemaphoreType.DMA((NDEV,))])


def _ffn_bwd(dxo, x, gate, up, gain, gath_g, gath_u, gath_d, name, tm):
    t, dm = x.shape
    f = gate.shape[1]

    def body(dxo_ref, x_ref, gate_ref, up_ref, g_ref, gg_ref, gu_ref, gd_ref,
             dx_ref, dxb_ref, dg_ref, du_ref, hb_ref, dgain_ref, wg, wu, wd, sems):
        @pl.when(pl.program_id(0) == 0)
        def _():
            _load_weight(gg_ref, wg, sems)
            _load_weight(gu_ref, wu, sems)
            _load_weight(gd_ref, wd, sems)
            dgain_ref[...] = jnp.zeros_like(dgain_ref)
        xv, gain_v, dxo_v = x_ref[...], g_ref[...], dxo_ref[...]
        h, r = _rms_fwd(xv, gain_v)
        hb_ref[...] = h.astype(BF16)
        dxob = dxo_v.astype(BF16)
        dh = jnp.zeros_like(xv)
        for c0, c1 in _ff_chunks(f):
            gate_v = gate_ref[:, c0:c1].astype(F32)
            up_v = up_ref[:, c0:c1].astype(F32)
            s = _sigmoid(gate_v)
            silu = gate_v * s
            dact = _dot(dxob, wd[c0:c1, :], NT)
            dg = (dact * up_v * (s * (1.0 + gate_v * (1.0 - s)))).astype(BF16)
            du = (dact * silu).astype(BF16)
            dg_ref[:, c0:c1] = dg
            du_ref[:, c0:c1] = du
            dh = dh + _dot(dg, wg[c0:c1, :], NN) + _dot(du, wu[c0:c1, :], NN)
        dx, dgain = _rms_bwd(dh, xv, r, gain_v)
        dx = dxo_v + dx
        dx_ref[...] = dx
        dxb_ref[...] = dx.astype(BF16)
        dgain_ref[...] += dgain

    return dict(
        body=body, grid=(t // tm,), name=name, args=[dxo, x, gate, up, gain, gath_g, gath_u, gath_d],
        out_shape=[jax.ShapeDtypeStruct((t, dm), F32), jax.ShapeDtypeStruct((t, dm), BF16),
                   jax.ShapeDtypeStruct((t, f), BF16), jax.ShapeDtypeStruct((t, f), BF16),
                   jax.ShapeDtypeStruct((t, dm), BF16), jax.ShapeDtypeStruct((1, dm), F32)],
        in_specs=[_tok(tm, dm), _tok(tm, dm), _tok(tm, f), _tok(tm, f), _full((1, dm)), ANY, ANY, ANY],
        out_specs=[_tok(tm, dm), _tok(tm, dm), _tok(tm, f), _tok(tm, f), _tok(tm, dm), _full((1, dm))],
        scratch=[pltpu.VMEM((f, dm), BF16), pltpu.VMEM((f, dm), BF16), pltpu.VMEM((f, dm), BF16),
                 pltpu.SemaphoreType.DMA((NDEV,))])


def _t5_buckets(rel):
    nb = N_BUCKETS // 2
    ret = jnp.where(rel > 0, nb, 0)
    n = jnp.abs(rel)
    max_exact = nb // 2
    nf = jnp.maximum(n, 1).astype(jnp.float32)
    large = max_exact + (jnp.log(nf / max_exact) / math.log(MAX_DISTANCE / max_exact)
                         * (nb - max_exact)).astype(jnp.int32)
    large = jnp.minimum(large, nb - 1)
    return ret + jnp.where(n < max_exact, n, large)


def _bucket_table():
    qi = jnp.arange(CHUNK, dtype=jnp.int32)[:, None]
    kj = jnp.arange(3 * CHUNK, dtype=jnp.int32)[None, :]
    rel = kj - CHUNK - qi
    return jnp.where(jnp.abs(rel) <= CHUNK, _t5_buckets(rel), -1)


def _bias_table(rel_bias_t, buckets):
    nh = rel_bias_t.shape[0]

    def body(rb_ref, bk_ref, o_ref):
        bk = bk_ref[...]
        for h in range(nh):
            acc = jnp.where(bk < 0, NEG, 0.0).astype(F32)
            for b in range(N_BUCKETS):
                acc = jnp.where(bk == b, rb_ref[h, b] * LOG2E, acc)
            o_ref[h] = acc

    return dict(
        body=body, grid=(1,), name="bias_table", args=[rel_bias_t, buckets],
        out_shape=[jax.ShapeDtypeStruct((nh,) + buckets.shape, F32)],
        in_specs=[pl.BlockSpec(memory_space=pltpu.SMEM), _full(buckets.shape)],
        out_specs=[_full((nh,) + buckets.shape)])


def _rel_bias_grad(dbias, buckets):
    nh = dbias.shape[0]

    def body(db_ref, bk_ref, o_ref):
        bk = bk_ref[...]
        lane = lax.broadcasted_iota(jnp.int32, (1, 128), 1)
        for h in range(nh):
            d = db_ref[h]
            row = jnp.zeros((1, 128), F32)
            for b in range(N_BUCKETS):
                s = jnp.sum(jnp.sum(jnp.where(bk == b, d, 0.0), axis=1, keepdims=True), axis=0, keepdims=True)
                row = jnp.where(lane == b, s, row)
            o_ref[h:h + 1, :] = row

    return dict(
        body=body, grid=(1,), name="rel_bias_grad", args=[dbias, buckets],
        out_shape=[jax.ShapeDtypeStruct((nh, 128), F32)],
        in_specs=[_full(dbias.shape), _full(buckets.shape)], out_specs=[_full((nh, 128))])


def _half_masks():
    lane = lax.broadcasted_iota(jnp.int32, (CHUNK, 128), 1)
    return lane < HEAD_DIM, lane >= HEAD_DIM


def _kv_low(ref, starts, hk, lo):
    kt = (hk // 2) * 128
    out = []
    for jj in range(3):
        blk = ref[pl.ds(starts[jj], CHUNK), kt:kt + 128]
        if hk % 2 == 1:
            blk = pltpu.roll(blk, HEAD_DIM, 1)
        out.append(jnp.where(lo, blk, jnp.zeros_like(blk)))
    return out


def _stack_heads(tile_a, tile_b):
    return jnp.concatenate([tile_a, pltpu.roll(tile_a, HEAD_DIM, 1), tile_b, pltpu.roll(tile_b, HEAD_DIM, 1)], axis=0)


def _unstack_heads(o4):
    return (o4[0:CHUNK] + pltpu.roll(o4[CHUNK:2 * CHUNK], HEAD_DIM, 1),
            o4[2 * CHUNK:3 * CHUNK] + pltpu.roll(o4[3 * CHUNK:], HEAD_DIM, 1))


ATT_SLAB = 32


def _softmax_slab(s_scr, hk, g, r0, bias_ref, sink_ref, n, nblk):
    scale = HEAD_DIM ** -0.5 * LOG2E
    h = (N_HEADS // N_KV) * hk + g
    s = []
    for jj in range(3):
        sj = (s_scr[hk, jj, pl.ds(g * CHUNK + r0, ATT_SLAB), :] * scale
              + bias_ref[h, pl.ds(r0, ATT_SLAB), jj * CHUNK:(jj + 1) * CHUNK])
        if jj == 0:
            sj = jnp.where(n > 0, sj, NEG)
        if jj == 2:
            sj = jnp.where(n < nblk - 1, sj, NEG)
        s.append(sj)
    sink = sink_ref[h] * LOG2E
    m = jnp.maximum(jnp.max(jnp.maximum(jnp.maximum(s[0], s[1]), s[2]), axis=-1, keepdims=True), sink)
    e = [jnp.exp2(sj - m) for sj in s]
    es = jnp.exp2(sink - m)
    inv = 1.0 / (jnp.sum(e[0] + e[1] + e[2], axis=-1, keepdims=True) + es)
    return [ej * inv for ej in e], es * inv


def _key_block_starts(n, nblk):
    return [pl.multiple_of(jnp.clip(n - 1 + jj, 0, nblk - 1) * CHUNK, CHUNK) for jj in range(3)]


def _attn_fwd(qkv, x2, bias, sink, gath):
    t, dm = x2.shape
    nblk = t // CHUNK
    kvw = N_KV * HEAD_DIM
    kcb, vcb = dm // kvw, dm // kvw + 1
    slab = (N_KV, 3, 4 * CHUNK, CHUNK)

    def body(q_ref, k_ref, v_ref, x2_ref, bias_ref, sink_ref, gath_ref, x3_ref, att_ref, p_ref, ps_ref,
             wbuf, s_scr, sems):
        n = pl.program_id(0)

        @pl.when(n == 0)
        def _():
            _load_weight(gath_ref, wbuf, sems)
        lo, _ = _half_masks()
        lane_s = lax.broadcasted_iota(jnp.int32, (ATT_SLAB, 128), 1)
        starts = _key_block_starts(n, nblk)
        tiles = []
        for hk in range(N_KV):
            c0 = (2 * hk) * 128
            k_lo = _kv_low(k_ref, starts, hk, lo)
            v_lo = _kv_low(v_ref, starts, hk, lo)
            q4 = _stack_heads(q_ref[:, c0:c0 + 128], q_ref[:, c0 + 128:c0 + 256])
            for jj in range(3):
                s_scr[hk, jj] = _dot(q4, k_lo[jj], NT)
            for g in range(4):
                h = 4 * hk + g
                for r0 in range(0, CHUNK, ATT_SLAB):
                    p, ps = _softmax_slab(s_scr, hk, g, r0, bias_ref, sink_ref, n, nblk)
                    for jj in range(3):
                        p_ref[hk, jj, g * CHUNK + r0:g * CHUNK + r0 + ATT_SLAB, :] = p[jj].astype(BF16)
                    rest = jnp.zeros((ATT_SLAB, 128), F32) if h == 0 else ps_ref[r0:r0 + ATT_SLAB, :]
                    ps_ref[r0:r0 + ATT_SLAB, :] = jnp.where(lane_s == h, ps, rest)
            o4 = _dot(p_ref[hk, 0], v_lo[0], NN) + _dot(p_ref[hk, 1], v_lo[1], NN) + _dot(p_ref[hk, 2], v_lo[2], NN)
            tiles += list(_unstack_heads(o4))
        att = jnp.concatenate(tiles, axis=1).astype(BF16)
        att_ref[...] = att
        x3_ref[...] = x2_ref[...] + _dot(att, wbuf[...], NN)

    blk = pl.BlockSpec((CHUNK, dm), lambda n: (n, 0))
    return dict(
        body=body, grid=(nblk,), name="attn_fwd", args=[qkv, qkv, qkv, x2, bias, sink, gath],
        out_shape=[jax.ShapeDtypeStruct((t, dm), F32), jax.ShapeDtypeStruct((t, dm), BF16),
                   jax.ShapeDtypeStruct((nblk,) + slab, BF16), jax.ShapeDtypeStruct((t, 128), F32)],
        in_specs=[blk, pl.BlockSpec((t, kvw), lambda n: (0, kcb)), pl.BlockSpec((t, kvw), lambda n: (0, vcb)), blk,
                  _full(bias.shape), pl.BlockSpec(memory_space=pltpu.SMEM), ANY],
        out_specs=[blk, blk, pl.BlockSpec((None,) + slab, lambda n: (n, 0, 0, 0, 0)),
                   pl.BlockSpec((CHUNK, 128), lambda n: (n, 0))],
        scratch=[pltpu.VMEM((gath.shape[1] * NDEV, dm), BF16), pltpu.VMEM(slab, F32),
                 pltpu.SemaphoreType.DMA((NDEV,))])


def _attn_bwd(qkv, att, probs, sink_probs, dx3, bias_shape, gath):
    t, dm = dx3.shape
    nblk = t // CHUNK
    kvw = N_KV * HEAD_DIM
    kcb, vcb = dm // kvw, dm // kvw + 1
    scale = HEAD_DIM ** -0.5
    slab = (N_KV, 3, 4 * CHUNK, CHUNK)

    def body(q_ref, k_ref, v_ref, att_ref, p_ref, ps_ref, dx_ref, gath_ref,
             dq_ref, dkb_ref, dvb_ref, dbias_ref, dsink_ref,
             wbuf, dp_scr, ds_scr, prod_scr, dsum_scr, dk_ref, dv_ref, sems):
        n = pl.program_id(0)

        @pl.when(n == 0)
        def _():
            _load_weight(gath_ref, wbuf, sems)
            dk_ref[...] = jnp.zeros_like(dk_ref)
            dv_ref[...] = jnp.zeros_like(dv_ref)
            dbias_ref[...] = jnp.zeros_like(dbias_ref)
            dsink_ref[...] = jnp.zeros_like(dsink_ref)
        lo, hi = _half_masks()
        lane_s = lax.broadcasted_iota(jnp.int32, (ATT_SLAB, 128), 1)
        starts = _key_block_starts(n, nblk)
        dout = _dot(dx_ref[...].astype(BF16), wbuf[...], NT)
        prod_scr[...] = dout * att_ref[...].astype(F32)
        doutb = dout.astype(BF16)
        dq_tiles = []
        for hk in range(N_KV):
            kt = (hk // 2) * 128
            c0 = (2 * hk) * 128
            k_lo = _kv_low(k_ref, starts, hk, lo)
            v_lo = _kv_low(v_ref, starts, hk, lo)
            q4 = _stack_heads(q_ref[:, c0:c0 + 128], q_ref[:, c0 + 128:c0 + 256])
            do4 = _stack_heads(doutb[:, c0:c0 + 128], doutb[:, c0 + 128:c0 + 256])
            for jj in range(3):
                dp_scr[hk, jj] = _dot(do4, v_lo[jj], NT)
            for g in range(4):
                h = 4 * hk + g
                for r0 in range(0, CHUNK, ATT_SLAB):
                    rows = slice(g * CHUNK + r0, g * CHUNK + r0 + ATT_SLAB)
                    pt = prod_scr[r0:r0 + ATT_SLAB, c0 + (g // 2) * 128:c0 + (g // 2 + 1) * 128]
                    msk = lane_s < HEAD_DIM if g % 2 == 0 else lane_s >= HEAD_DIM
                    dsum = jnp.sum(jnp.where(msk, pt, 0.0), axis=-1, keepdims=True)
                    rest = jnp.zeros((ATT_SLAB, 128), F32) if h == 0 else dsum_scr[r0:r0 + ATT_SLAB, :]
                    dsum_scr[r0:r0 + ATT_SLAB, :] = jnp.where(lane_s == h, dsum, rest)
                    for jj in range(3):
                        ds = p_ref[hk, jj, rows, :].astype(F32) * (dp_scr[hk, jj, rows, :] - dsum)
                        dbias_ref[h, r0:r0 + ATT_SLAB, jj * CHUNK:(jj + 1) * CHUNK] += ds
                        ds_scr[hk, jj, rows, :] = ds.astype(BF16)
            dq4 = jnp.zeros((4 * CHUNK, 128), F32)
            for jj in range(3):
                ds4 = ds_scr[hk, jj]
                dq4 = dq4 + _dot(ds4, k_lo[jj], NN) * scale
                dkj = _dot(ds4, q4, TN) * scale
                dvj = _dot(p_ref[hk, jj], do4, TN)
                if hk % 2 == 1:
                    dkj, dvj = pltpu.roll(dkj, HEAD_DIM, 1), pltpu.roll(dvj, HEAD_DIM, 1)
                keep = lo if hk % 2 == 0 else hi
                dk_ref[pl.ds(starts[jj], CHUNK), kt:kt + 128] += jnp.where(keep, dkj, 0.0)
                dv_ref[pl.ds(starts[jj], CHUNK), kt:kt + 128] += jnp.where(keep, dvj, 0.0)
            dq_tiles += list(_unstack_heads(dq4))
        dq_ref[...] = jnp.concatenate(dq_tiles, axis=1).astype(BF16)
        dsink_ref[...] -= jnp.sum(ps_ref[...] * dsum_scr[...], axis=0, keepdims=True)

        @pl.when(n == nblk - 1)
        def _():
            dkb_ref[...] = dk_ref[...].astype(BF16)
            dvb_ref[...] = dv_ref[...].astype(BF16)

    blk = pl.BlockSpec((CHUNK, dm), lambda n: (n, 0))
    return dict(
        body=body, grid=(nblk,), name="attn_bwd", args=[qkv, qkv, qkv, att, probs, sink_probs, dx3, gath],
        out_shape=[jax.ShapeDtypeStruct((t, dm), BF16), jax.ShapeDtypeStruct((t, kvw), BF16),
                   jax.ShapeDtypeStruct((t, kvw), BF16), jax.ShapeDtypeStruct(bias_shape, F32),
                   jax.ShapeDtypeStruct((1, 128), F32)],
        in_specs=[blk, pl.BlockSpec((t, kvw), lambda n: (0, kcb)), pl.BlockSpec((t, kvw), lambda n: (0, vcb)),
                  blk, pl.BlockSpec((None,) + slab, lambda n: (n, 0, 0, 0, 0)),
                  pl.BlockSpec((CHUNK, 128), lambda n: (n, 0)), blk, ANY],
        out_specs=[blk, _full((t, kvw)), _full((t, kvw)), _full(bias_shape), _full((1, 128))],
        scratch=[pltpu.VMEM((gath.shape[1] * NDEV, dm), BF16), pltpu.VMEM(slab, F32), pltpu.VMEM(slab, BF16),
                 pltpu.VMEM((CHUNK, dm), F32), pltpu.VMEM((CHUNK, 128), F32),
                 pltpu.VMEM((t, kvw), F32), pltpu.VMEM((t, kvw), F32), pltpu.SemaphoreType.DMA((NDEV,))])


def _finish_weight(recvs, w, m, v, name):
    nl, r, dm = w.shape
    assert nl == len(recvs) and all(rc.shape[1:] == (r, dm) for rc in recvs)
    td = dm // 2
    wspec = pl.BlockSpec((None, r, td), lambda l, j: (l, 0, j))

    def body(*refs):
        r_refs = refs[:nl]
        w_ref, m_ref, v_ref, g_ref, d_ref, nm_ref, nv_ref = refs[nl:]
        layer = pl.program_id(0)
        for li in range(nl):
            @pl.when(layer == li)
            def _():
                g = r_refs[li][0].astype(F32)
                for d in range(1, recvs[li].shape[0]):
                    g = g + r_refs[li][d].astype(F32)
                delta, nm, nv = _adamw_math(w_ref[...], g, m_ref[...], v_ref[...])
                g_ref[...] = g
                d_ref[...] = delta
                nm_ref[...] = nm
                nv_ref[...] = nv

    o = jax.ShapeDtypeStruct(w.shape, F32)
    return dict(
        body=body, grid=(nl, 2), name=name, args=[*recvs, w, m, v], out_shape=[o, o, o, o],
        in_specs=[pl.BlockSpec((rc.shape[0], r, td), lambda l, j: (0, 0, j)) for rc in recvs] + [wspec] * 3,
        out_specs=[wspec] * 4)


def _adamw_small(ws, ms, vs, slots, lates, loss_slots, name):
    n = len(ws)
    nl = len(lates)

    def total(ref):
        acc = ref[0].astype(F32)
        for d in range(1, NDEV):
            acc = acc + ref[d].astype(F32)
        return acc

    def body(*refs):
        ins, outs = refs[:4 * n + nl + 1], refs[4 * n + nl + 1:]
        for i in range(n):
            w_ref, m_ref, v_ref, s_ref = ins[4 * i:4 * i + 4]
            g_ref, d_ref, nm_ref, nv_ref = outs[4 * i:4 * i + 4]
            g_ref[...] = total(s_ref)
            for k, (at, late) in enumerate(lates):
                if at == i:
                    g_ref[0:late.shape[1], :] = total(ins[4 * n + k])
            d_ref[...], nm_ref[...], nv_ref[...] = _adamw_math(w_ref[...], g_ref[...], m_ref[...], v_ref[...])
        outs[4 * n][...] = total(ins[4 * n + nl])

    args, out_shape = [], []
    for w, m, v, s in zip(ws, ms, vs, slots):
        args += [w, m, v, s]
        out_shape += [jax.ShapeDtypeStruct(w.shape, F32)] * 4
    args += [late for _, late in lates] + [loss_slots]
    out_shape.append(jax.ShapeDtypeStruct((1, 1), F32))
    out = pl.pallas_call(
        body, grid=(1,), out_shape=tuple(out_shape), in_specs=[_full(a.shape) for a in args],
        out_specs=tuple(_full(o.shape) for o in out_shape), compiler_params=_cp(), name=name)(*args)
    return [tuple(out[4 * i:4 * i + 4]) for i in range(n)], out[4 * n]


def kernel(x, norm_mix, norm_ffn, even_w_in, even_v_ln_g, even_v_ln_b, even_w_spatial, even_b_spatial, even_conv_w, even_w_out, attn_w_qkv, attn_sink, rel_bias, attn_w_out, ffn_w_gate, ffn_w_up, ffn_w_down, final_norm, loss_target, m_norm_mix, m_norm_ffn, m_even_w_in, m_even_v_ln_g, m_even_v_ln_b, m_even_w_spatial, m_even_b_spatial, m_even_conv_w, m_even_w_out, m_attn_w_qkv, m_attn_sink, m_rel_bias, m_attn_w_out, m_ffn_w_gate, m_ffn_w_up, m_ffn_w_down, m_final_norm, v_norm_mix, v_norm_ffn, v_even_w_in, v_even_v_ln_g, v_even_v_ln_b, v_even_w_spatial, v_even_b_spatial, v_even_conv_w, v_even_w_out, v_attn_w_qkv, v_attn_sink, v_rel_bias, v_attn_w_out, v_ffn_w_gate, v_ffn_w_up, v_ffn_w_down, v_final_norm):
    t, dm = x.shape[1], x.shape[2]
    aw = even_v_ln_g.shape[1]
    bw = even_conv_w.shape[2] * NDEV
    gd = aw // A_GROUPS
    tm = min(512, t // 2)
    tmf = min(256, t // 2)
    me = _my_index()
    row = lambda a: a.reshape(1, -1)

    colT = lambda w: w.T.astype(BF16)
    sh = dict(winT=colT(even_w_in[0]), wqkvT=colT(attn_w_qkv[0]), wgT0=colT(ffn_w_gate[0]), wuT0=colT(ffn_w_up[0]),
              wgT1=colT(ffn_w_gate[1]), wuT1=colT(ffn_w_up[1]), woe=even_w_out[0].astype(BF16),
              woa=attn_w_out[0].astype(BF16), wd0=ffn_w_down[0].astype(BF16), wd1=ffn_w_down[1].astype(BF16))
    gather = lambda names: _GatherCarry([sh[n] for n in names])

    in_full = lambda a: lax.dynamic_update_slice(jnp.zeros((3, bw), F32), a[0], (0, me * (bw // NDEV)))

    x0 = x[0]
    wsp_b = even_w_spatial[0].astype(BF16)
    bspb = jnp.broadcast_to(even_b_spatial[0][:, :, None], (A_GROUPS, CHUNK, gd))
    buckets = _bucket_table()
    sink = attn_sink[0]

    (bias,), ((g_winT,), (cw_slots,)) = _call(
        _bias_table(rel_bias.T, buckets), [gather(["winT"]), _BroadcastCarry([in_full(even_conv_w)])])
    cw_full = jnp.sum(cw_slots, axis=0)
    (proj, h0b), (g_woe, g_wgT0) = _call(_norm_proj(x0, row(norm_mix[0]), g_winT, F32, "in_proj", tm),
                                         gather(["woe", "wgT0"]))
    (x1, yb), (g_wuT0,) = _call(_even_core_fwd(proj, x0, even_v_ln_g, even_v_ln_b, wsp_b, bspb, cw_full, g_woe, tm),
                                gather(["wuT0"]))
    (gate0, up0, act0), (g_wd0,) = _call(_ffn_up(x1, row(norm_ffn[0]), g_wgT0, g_wuT0, "ffn_up0", tmf), gather(["wd0"]))
    (x2,), (g_wqkvT,) = _call(_ffn_down(x1, act0, g_wd0, "ffn_down0", tm), gather(["wqkvT"]))
    (qkv, h2b), (g_woa,) = _call(_norm_proj(x2, row(norm_mix[1]), g_wqkvT, BF16, "qkv_proj", tm), gather(["woa"]))
    (x3, attb, probs, sink_probs), (g_wgT1, g_wuT1) = _call(
        _attn_fwd(qkv, x2, bias, sink, g_woa), gather(["wgT1", "wuT1"]))
    (gate1, up1, act1), (g_wd1,) = _call(_ffn_up(x3, row(norm_ffn[1]), g_wgT1, g_wuT1, "ffn_up1", tmf), gather(["wd1"]))
    (loss_part, dx4, dx4b, d_final), _ = _call(
        _ffn_down_loss(x3, act1, g_wd1, loss_target[0], row(final_norm), "ffn_down1_loss", tm))

    (dx3, dx3b, dg1, du1, h3b, d_nffn1), _ = _call(
        _ffn_bwd(dx4, x3, gate1, up1, row(norm_ffn[1]), g_wgT1, g_wuT1, g_wd1, "ffn_bwd1", tmf))
    (p_wgT1, p_wuT1), _ = _call(_wgrad_pair(dg1, du1, h3b, "wgrad_gate_up1"))
    (p_wd1,), ((a_wgT1,), (a_wuT1,)) = _call(
        _wgrad(act1, dx4b, "wgrad_down1"), [_PairCarry(p_wgT1), _PairCarry(p_wuT1)])
    (dq, dk, dv, dbias, dsink), ((r_wgT1,), (a_wd1,)) = _call(
        _attn_bwd(qkv, attb, probs, sink_probs, dx3, bias.shape, g_woa),
        [_ChipSumCarry(p_wgT1, a_wgT1), _PairCarry(p_wd1)])
    (p_woa,), _ = _call(_wgrad(attb, dx3b, "wgrad_attn_out"))
    (dx2, dx2b, d_nmix1), (r_wuT1,) = _call(
        _proj_bwd_norm([dq, dk, dv], x2, row(norm_mix[1]), dx3, g_wqkvT, "qkv_bwd", tm),
        _ChipSumCarry(p_wuT1, a_wuT1))
    (p_wqkvT,), _ = _call(_wgrad([dq, dk, dv], h2b, "wgrad_qkv"))
    (dx1, dx1b, dg0, du0, h1b, d_nffn0), ((r_wd1,), (r_woa, r_wqkvT)) = _call(
        _ffn_bwd(dx2, x1, gate0, up0, row(norm_ffn[0]), g_wgT0, g_wuT0, g_wd0, "ffn_bwd0", tmf),
        [_ChipSumCarry(p_wd1, a_wd1), _GradCarry([p_woa, p_wqkvT])])
    (p_woe,), _ = _call(_wgrad(yb, dx1b, "wgrad_even_out"))
    (p_wgT0,), _ = _call(_wgrad(dg0, h1b, "wgrad_gate0"))
    (p_wuT0,), ((a_wgT0,), (r_woe,)) = _call(
        _wgrad(du0, h1b, "wgrad_up0"), [_PairCarry(p_wgT0), _GradCarry([p_woe])])
    (p_wd0,), ((r_wgT0,), (a_wuT0,)) = _call(
        _wgrad(act0, dx2b, "wgrad_down0"), [_ChipSumCarry(p_wgT0, a_wgT0), _PairCarry(p_wuT0)])
    (dproj, d_lng, d_lnb, d_wsp, d_bsp3, d_cw), ((r_wuT0,), (a_wd0,)) = _call(
        _even_core_bwd(proj, dx1, even_v_ln_g, even_v_ln_b, wsp_b, bspb, cw_full, g_woe, tm),
        [_ChipSumCarry(p_wuT0, a_wuT0), _PairCarry(p_wd0)])
    small_names = ["norm_mix", "norm_ffn", "even_v_ln_g", "even_v_ln_b", "even_w_spatial", "even_b_spatial",
                   "even_conv_w", "attn_sink", "rel_bias", "final_norm"]
    wsp_at = small_names.index("even_w_spatial")
    small_parts = [jnp.concatenate([jnp.zeros_like(d_nmix1), d_nmix1]), jnp.concatenate([d_nffn0, d_nffn1]),
                   d_lng, d_lnb, jnp.sum(d_bsp3, axis=-1)[None], d_cw,
                   dsink[:, 0:N_HEADS], jnp.zeros_like(rel_bias), d_final, loss_part]
    (p_winT,), (r_wd0,) = _call(_wgrad(dproj, h0b, "wgrad_in"), _ChipSumCarry(p_wd0, a_wd0))
    (d_relb_t,), ((a_winT,), rest_slots) = _call(
        _rel_bias_grad(dbias, buckets), [_PairCarry(p_winT), _BroadcastCarry(small_parts)])
    d_relb = d_relb_t[:, 0:N_BUCKETS].T
    (dx0, _, d_nmix0), ((r_winT,), (wsp_slots,)) = _call(
        _proj_bwd_norm([dproj], x0, row(norm_mix[0]), dx1, g_winT, "in_proj_bwd", tm),
        [_ChipSumCarry(p_winT, a_winT), _BroadcastCarry([d_wsp[None].astype(BF16)])])
    small_slots = list(rest_slots[:wsp_at]) + [wsp_slots] + list(rest_slots[wsp_at:])

    grads = {}

    order = ["norm_mix", "norm_ffn", "even_w_in", "even_v_ln_g", "even_v_ln_b", "even_w_spatial", "even_b_spatial",
             "even_conv_w", "even_w_out", "attn_w_qkv", "attn_sink", "rel_bias", "attn_w_out", "ffn_w_gate",
             "ffn_w_up", "ffn_w_down", "final_norm"]
    ws = dict(norm_mix=norm_mix, norm_ffn=norm_ffn, even_w_in=even_w_in, even_v_ln_g=even_v_ln_g,
              even_v_ln_b=even_v_ln_b, even_w_spatial=even_w_spatial, even_b_spatial=even_b_spatial,
              even_conv_w=even_conv_w, even_w_out=even_w_out, attn_w_qkv=attn_w_qkv, attn_sink=attn_sink,
              rel_bias=rel_bias, attn_w_out=attn_w_out, ffn_w_gate=ffn_w_gate, ffn_w_up=ffn_w_up,
              ffn_w_down=ffn_w_down, final_norm=final_norm)
    ms = dict(norm_mix=m_norm_mix, norm_ffn=m_norm_ffn, even_w_in=m_even_w_in, even_v_ln_g=m_even_v_ln_g,
              even_v_ln_b=m_even_v_ln_b, even_w_spatial=m_even_w_spatial, even_b_spatial=m_even_b_spatial,
              even_conv_w=m_even_conv_w, even_w_out=m_even_w_out, attn_w_qkv=m_attn_w_qkv, attn_sink=m_attn_sink,
              rel_bias=m_rel_bias, attn_w_out=m_attn_w_out, ffn_w_gate=m_ffn_w_gate, ffn_w_up=m_ffn_w_up,
              ffn_w_down=m_ffn_w_down, final_norm=m_final_norm)
    vs = dict(norm_mix=v_norm_mix, norm_ffn=v_norm_ffn, even_w_in=v_even_w_in, even_v_ln_g=v_even_v_ln_g,
              even_v_ln_b=v_even_v_ln_b, even_w_spatial=v_even_w_spatial, even_b_spatial=v_even_b_spatial,
              even_conv_w=v_even_conv_w, even_w_out=v_even_w_out, attn_w_qkv=v_attn_w_qkv, attn_sink=v_attn_sink,
              rel_bias=v_rel_bias, attn_w_out=v_attn_w_out, ffn_w_gate=v_ffn_w_gate, ffn_w_up=v_ffn_w_up,
              ffn_w_down=v_ffn_w_down, final_norm=v_final_norm)
    big = dict(ffn_w_gate=([r_wgT0, r_wgT1], True), even_w_in=([r_winT], True), even_w_out=([r_woe], False),
               attn_w_qkv=([r_wqkvT], True), attn_w_out=([r_woa], False), ffn_w_up=([r_wuT0, r_wuT1], True),
               ffn_w_down=([r_wd0, r_wd1], False))
    delta, new_m, new_v = {}, {}, {}
    late_slots = None
    for n, (recvs, transposed) in big.items():
        lay = (lambda a: jnp.swapaxes(a, 1, 2)) if transposed else (lambda a: a)
        spec = _finish_weight(recvs, lay(ws[n]), lay(ms[n]), lay(vs[n]), "finish_" + n)
        if late_slots is None:
            outs, late_slots = _call(spec, _BroadcastCarry([d_nmix0, d_relb]))
        else:
            outs, _ = _call(spec)
        grads[n], delta[n], new_m[n], new_v[n] = [lay(o) for o in outs]
    shaped = lambda n, a: in_full(a) if n == "even_conv_w" else (a.reshape(1, dm) if n == "final_norm" else a)
    pick = lambda dct: [shaped(n, dct[n]) for n in small_names]
    lates = [(small_names.index("norm_mix"), late_slots[0]), (small_names.index("rel_bias"), late_slots[1])]
    results, loss11 = _adamw_small(pick(ws), pick(ms), pick(vs), small_slots[:-1], lates, small_slots[-1],
                                   "adamw_small")
    mine = lambda a: lax.dynamic_slice(a, (0, me * (bw // NDEV)), (3, bw // NDEV))[None]
    for n, res in zip(small_names, results):
        for dst, a in zip((grads, delta, new_m, new_v), res):
            dst[n] = mine(a) if n == "even_conv_w" else (a.reshape(dm) if n == "final_norm" else a)
    loss = loss11[0, 0]
    return (loss, dx0[None], *[grads[n] for n in order], *[delta[n] for n in order],
            *[new_m[n] for n in order], *[new_v[n] for n in order])
```

```python
import math

import jax
import jax.numpy as jnp
import numpy as np
from jax import lax
from jax.experimental import pallas as pl
from jax.experimental.pallas import tpu as pltpu

F32, BF16 = jnp.float32, jnp.bfloat16
NDEV = 8
EPS = 1e-6
CHUNK = 128
A_GROUPS = 4
N_HEADS, N_KV, HEAD_DIM = 16, 4, 64
N_BUCKETS, MAX_DISTANCE = 32, 128
NEG = -1e30
LOG2E = 1.4426950408889634
ADAM_LR, ADAM_B1, ADAM_B2, ADAM_EPS, ADAM_WD, ADAM_STEP = 0.001, 0.9, 0.999, 1e-08, 0.01, 10
VMEM_LIMIT = 56 * 1024 * 1024
MESH = pl.DeviceIdType.MESH
NT = (((1,), (1,)), ((), ()))
NN = (((1,), (0,)), ((), ()))
TN = (((0,), (0,)), ((), ()))
ANY = pl.BlockSpec(memory_space=pl.ANY)


def _cp(n_grid=1):
    return pltpu.CompilerParams(dimension_semantics=("arbitrary",) * n_grid, vmem_limit_bytes=VMEM_LIMIT)


def _dot(a, b, dims):
    return lax.dot_general(a, b, dims, preferred_element_type=F32)


def _my_index():
    return 4 * lax.axis_index("x") + 2 * lax.axis_index("y") + lax.axis_index("c")


def _peer(k):
    x, y, c = lax.axis_index("x"), lax.axis_index("y"), lax.axis_index("c")
    px = 1 - x if k & 4 else x
    py = 1 - y if k & 2 else y
    pc = 1 - c if k & 1 else c
    return (px, py, pc)


def _load_weight(gath_ref, wbuf, sems):
    rows = gath_ref.shape[1]
    cps = [pltpu.make_async_copy(gath_ref.at[d], wbuf.at[pl.ds(d * rows, rows), :], sems.at[d]) for d in range(NDEV)]
    for c in cps:
        c.start()
    for c in cps:
        c.wait()


class _GatherCarry:
    def __init__(self, pieces):
        self.inputs = list(pieces)
        self.n = len(pieces)
        self.out_shape = [jax.ShapeDtypeStruct((NDEV,) + p.shape, p.dtype) for p in pieces]
        self.scratch = [pltpu.SemaphoreType.DMA((7 * self.n,)), pltpu.SemaphoreType.DMA((7 * self.n,)),
                        pltpu.SemaphoreType.DMA((self.n,))]

    def _ctx(self):
        x, y, c = lax.axis_index("x"), lax.axis_index("y"), lax.axis_index("c")
        chips = [(1 - x, y), (x, 1 - y), (1 - x, 1 - y)]
        return (x, y, c), (x, y, 1 - c), chips, c

    def _copy(self, k, j, block, to, ins, outs, sems, src=None):
        send_sems, recv_sems, _ = sems
        slot = outs[j].at[4 * block[0] + 2 * block[1] + block[2]]
        return pltpu.make_async_remote_copy(
            src_ref=slot if src is None else src, dst_ref=slot, send_sem=send_sems.at[k * self.n + j],
            recv_sem=recv_sems.at[k * self.n + j], device_id=to, device_id_type=MESH)

    def start(self, ins, outs, sems):
        me, sibling, chips, c = self._ctx()
        for j in range(self.n):
            pltpu.make_async_copy(ins[j], outs[j].at[4 * me[0] + 2 * me[1] + me[2]], sems[2].at[j]).start()
            self._copy(0, j, me, sibling, ins, outs, sems, src=ins[j]).start()
            for q, chip in enumerate(chips):
                self._copy(1 + q, j, me, (*chip, c), ins, outs, sems, src=ins[j]).start()

    def mid(self, ins, outs, sems):
        me, sibling, chips, c = self._ctx()
        for q, chip in enumerate(chips):
            for j in range(self.n):
                self._copy(1 + q, j, (*chip, c), me, ins, outs, sems).wait_recv()
                self._copy(4 + q, j, (*chip, c), sibling, ins, outs, sems).start()

    def finish(self, ins, outs, sems):
        me, sibling, chips, c = self._ctx()
        for j in range(self.n):
            self._copy(0, j, sibling, me, ins, outs, sems).wait_recv()
            for q, chip in enumerate(chips):
                self._copy(4 + q, j, (*chip, 1 - c), me, ins, outs, sems).wait_recv()
        for j in range(self.n):
            self._copy(0, j, me, sibling, ins, outs, sems, src=ins[j]).wait_send()
            for q, chip in enumerate(chips):
                self._copy(1 + q, j, me, (*chip, c), ins, outs, sems, src=ins[j]).wait_send()
                self._copy(4 + q, j, (*chip, c), sibling, ins, outs, sems).wait_send()
            pltpu.make_async_copy(ins[j], outs[j].at[0], sems[2].at[j]).wait()


class _GradCarry:
    def __init__(self, pieces):
        self.inputs = list(pieces)
        self.n = len(pieces)
        self.rows = [p.shape[0] // NDEV for p in pieces]
        self.out_shape = [jax.ShapeDtypeStruct((NDEV, r, p.shape[1]), p.dtype) for p, r in zip(pieces, self.rows)]
        self.scratch = [pltpu.SemaphoreType.DMA((7 * self.n,)), pltpu.SemaphoreType.DMA((7 * self.n,)),
                        pltpu.SemaphoreType.DMA((self.n,))]

    def _copies(self, ins, outs, sems):
        me = _my_index()
        local, remote = [], []
        for j in range(self.n):
            r = self.rows[j]
            local.append(pltpu.make_async_copy(ins[j].at[pl.ds(pl.multiple_of(me * r, 16), r), :], outs[j].at[me],
                                               sems[2].at[j]))
            for k in range(1, NDEV):
                peer = _peer(k)
                pidx = 4 * peer[0] + 2 * peer[1] + peer[2]
                remote.append(pltpu.make_async_remote_copy(
                    src_ref=ins[j].at[pl.ds(pl.multiple_of(pidx * r, 16), r), :], dst_ref=outs[j].at[me],
                    send_sem=sems[0].at[(k - 1) * self.n + j], recv_sem=sems[1].at[(k - 1) * self.n + j],
                    device_id=peer, device_id_type=MESH))
        return local, remote

    def start(self, ins, outs, sems):
        local, remote = self._copies(ins, outs, sems)
        for cp in local + remote:
            cp.start()

    def mid(self, ins, outs, sems):
        pass

    def finish(self, ins, outs, sems):
        local, remote = self._copies(ins, outs, sems)
        for cp in remote + local:
            cp.wait()


class _BroadcastCarry:
    def __init__(self, parts):
        self.inputs = list(parts)
        self.n = len(self.inputs)
        self.out_shape = [jax.ShapeDtypeStruct((NDEV,) + p.shape, p.dtype) for p in self.inputs]
        self.scratch = [pltpu.SemaphoreType.DMA((7 * self.n,)), pltpu.SemaphoreType.DMA((7 * self.n,)),
                        pltpu.SemaphoreType.DMA((self.n,))]

    def _copies(self, ins, outs, sems):
        me = _my_index()
        cps = []
        for j in range(self.n):
            cps.append(pltpu.make_async_copy(ins[j], outs[j].at[me], sems[2].at[j]))
            cps += [pltpu.make_async_remote_copy(
                src_ref=ins[j], dst_ref=outs[j].at[me], send_sem=sems[0].at[(k - 1) * self.n + j],
                recv_sem=sems[1].at[(k - 1) * self.n + j], device_id=_peer(k), device_id_type=MESH)
                for k in range(1, NDEV)]
        return cps

    def start(self, ins, outs, sems):
        for cp in self._copies(ins, outs, sems):
            cp.start()

    def mid(self, ins, outs, sems):
        pass

    def finish(self, ins, outs, sems):
        for cp in self._copies(ins, outs, sems):
            cp.wait()


class _PairCarry:
    def __init__(self, piece):
        self.inputs = [piece]
        self.r = piece.shape[0] // NDEV
        self.out_shape = [jax.ShapeDtypeStruct((4, self.r, piece.shape[1]), piece.dtype)]
        self.scratch = [pltpu.SemaphoreType.DMA((4,)), pltpu.SemaphoreType.DMA((4,))]

    def _copies(self, ins, outs, sems):
        x, y, c = lax.axis_index("x"), lax.axis_index("y"), lax.axis_index("c")
        return [pltpu.make_async_remote_copy(
            src_ref=ins[0].at[pl.ds(pl.multiple_of((2 * q + 1 - c) * self.r, 16), self.r), :], dst_ref=outs[0].at[q],
            send_sem=sems[0].at[q], recv_sem=sems[1].at[q], device_id=(x, y, 1 - c), device_id_type=MESH)
            for q in range(4)]

    def start(self, ins, outs, sems):
        for cp in self._copies(ins, outs, sems):
            cp.start()

    def mid(self, ins, outs, sems):
        pass

    def finish(self, ins, outs, sems):
        for cp in self._copies(ins, outs, sems):
            cp.wait()


class _ChipSumCarry:
    def __init__(self, piece, landed):
        self.inputs = [piece, landed]
        self.r, dm = piece.shape[0] // NDEV, piece.shape[1]
        self.out_shape = [jax.ShapeDtypeStruct((4, self.r, dm), piece.dtype)]
        self.scratch = [pltpu.VMEM((4, self.r, dm), piece.dtype), pltpu.VMEM((8, self.r, dm), piece.dtype),
                        pltpu.SemaphoreType.DMA((8,)), pltpu.SemaphoreType.DMA((3,)), pltpu.SemaphoreType.DMA((3,)),
                        pltpu.SemaphoreType.DMA(())]

    def _copies(self, outs, scr):
        sums, _, _, send_sems, recv_sems, local_sem = scr
        x, y, c = lax.axis_index("x"), lax.axis_index("y"), lax.axis_index("c")
        mine = 2 * x + y
        local = pltpu.make_async_copy(sums.at[mine], outs[0].at[mine], local_sem)
        remote = []
        for k in range(1, 4):
            px = 1 - x if k & 2 else x
            py = 1 - y if k & 1 else y
            remote.append(pltpu.make_async_remote_copy(
                src_ref=sums.at[2 * px + py], dst_ref=outs[0].at[mine], send_sem=send_sems.at[k - 1],
                recv_sem=recv_sems.at[k - 1], device_id=(px, py, c), device_id_type=MESH))
        return local, remote

    def start(self, ins, outs, scr):
        sums, stage, stage_sems = scr[0], scr[1], scr[2]
        c = lax.axis_index("c")
        loads = []
        for q in range(4):
            loads.append((
                pltpu.make_async_copy(ins[0].at[pl.ds(pl.multiple_of((2 * q + c) * self.r, 16), self.r), :],
                                      stage.at[2 * q], stage_sems.at[2 * q]),
                pltpu.make_async_copy(ins[1].at[q], stage.at[2 * q + 1], stage_sems.at[2 * q + 1])))
        for a, b in loads:
            a.start()
            b.start()
        for q, (a, b) in enumerate(loads):
            a.wait()
            b.wait()
            sums[q] = (stage[2 * q].astype(F32) + stage[2 * q + 1].astype(F32)).astype(sums.dtype)
        local, remote = self._copies(outs, scr)
        for cp in [local] + remote:
            cp.start()

    def mid(self, ins, outs, scr):
        pass

    def finish(self, ins, outs, scr):
        local, remote = self._copies(outs, scr)
        for cp in remote + [local]:
            cp.wait()


def _call(spec, carry=None):
    body, grid = spec["body"], spec["grid"]
    in_specs, out_specs, out_shape = list(spec["in_specs"]), list(spec["out_specs"]), list(spec["out_shape"])
    scratch, args = list(spec.get("scratch", [])), list(spec["args"])
    if carry is None:
        out = pl.pallas_call(body, grid=grid, in_specs=in_specs, out_specs=tuple(out_specs),
                             out_shape=tuple(out_shape), scratch_shapes=scratch, compiler_params=_cp(len(grid)),
                             name=spec["name"])(*args)
        return tuple(out), ()
    carries = list(carry) if isinstance(carry, (list, tuple)) else [carry]
    n_in, n_out, n_s = len(in_specs), len(out_specs), len(scratch)
    steps = int(np.prod(grid))

    def split(refs, counts):
        parts, o = [], 0
        for cnt in counts:
            parts.append(refs[o:o + cnt])
            o += cnt
        return parts

    c_in = [len(cr.inputs) for cr in carries]
    c_out = [len(cr.out_shape) for cr in carries]
    c_scr = [len(cr.scratch) for cr in carries]

    def wrapped(*refs):
        ins, cins, outs, couts, scr, cscr = split(refs, [n_in, sum(c_in), n_out, sum(c_out), n_s, sum(c_scr)])
        per = list(zip(carries, split(cins, c_in), split(couts, c_out), split(cscr, c_scr)))
        step = pl.program_id(0)
        for ax in range(1, len(grid)):
            step = step * grid[ax] + pl.program_id(ax)

        @pl.when(step == 0)
        def _():
            for cr, ci, co, cs in per:
                cr.start(ci, co, cs)
        if steps >= 3:
            @pl.when(step == steps - 2)
            def _():
                for cr, ci, co, cs in per:
                    cr.mid(ci, co, cs)
        body(*ins, *outs, *scr)

        @pl.when(step == steps - 1)
        def _():
            for cr, ci, co, cs in per:
                if steps < 3:
                    cr.mid(ci, co, cs)
                cr.finish(ci, co, cs)

    out = pl.pallas_call(
        wrapped, grid=grid, in_specs=in_specs + [ANY] * sum(c_in), out_specs=tuple(out_specs + [ANY] * sum(c_out)),
        out_shape=tuple(out_shape + [s for cr in carries for s in cr.out_shape]),
        scratch_shapes=scratch + [s for cr in carries for s in cr.scratch],
        compiler_params=_cp(len(grid)), name=spec["name"])(*args, *[a for cr in carries for a in cr.inputs])
    c_res = [tuple(p) for p in split(out[n_out:], c_out)]
    return tuple(out[:n_out]), (c_res if isinstance(carry, (list, tuple)) else c_res[0])


def _rms_fwd(x, gain):
    r = lax.rsqrt(jnp.mean(x * x, axis=-1, keepdims=True) + EPS)
    return x * r * gain, r


def _rms_bwd(dh, x, r, gain):
    a = dh * gain
    dx = r * a - x * (r * r * r) * jnp.mean(a * x, axis=-1, keepdims=True)
    dgain = jnp.sum(dh * (x * r), axis=0, keepdims=True)
    return dx, dgain


def _gelu(x):
    return 0.5 * x * (1.0 + lax.erf(x * 0.7071067811865476))


def _gelu_grad(x):
    return 0.5 * (1.0 + lax.erf(x * 0.7071067811865476)) + x * jnp.exp(-0.5 * x * x) * 0.3989422804014327


def _sigmoid(x):
    return 1.0 / (1.0 + jnp.exp(-x))


def _adamw_math(w, g, m, v):
    nm = ADAM_B1 * m + (1.0 - ADAM_B1) * g
    nv = ADAM_B2 * v + (1.0 - ADAM_B2) * (g * g)
    m_hat = nm / (1.0 - ADAM_B1 ** ADAM_STEP)
    v_hat = nv / (1.0 - ADAM_B2 ** ADAM_STEP)
    return -ADAM_LR * (m_hat / (jnp.sqrt(v_hat) + ADAM_EPS) + ADAM_WD * w), nm, nv


def _tok(tm, w):
    return pl.BlockSpec((tm, w), lambda i: (i, 0))


def _full(shape):
    return pl.BlockSpec(shape, lambda *i: (0,) * len(shape))


def _norm_proj(x, gain, gath, out_dtype, name, tm):
    t, dm = x.shape
    n = gath.shape[1] * NDEV

    def body(x_ref, g_ref, gath_ref, proj_ref, hb_ref, wbuf, sems):
        @pl.when(pl.program_id(0) == 0)
        def _():
            _load_weight(gath_ref, wbuf, sems)
        h, _ = _rms_fwd(x_ref[...], g_ref[...])
        hb = h.astype(BF16)
        hb_ref[...] = hb
        proj_ref[...] = _dot(hb, wbuf[...], NT).astype(out_dtype)

    return dict(
        body=body, grid=(t // tm,), name=name, args=[x, gain, gath],
        out_shape=[jax.ShapeDtypeStruct((t, n), out_dtype), jax.ShapeDtypeStruct((t, dm), BF16)],
        in_specs=[_tok(tm, dm), _full((1, dm)), ANY], out_specs=[_tok(tm, n), _tok(tm, dm)],
        scratch=[pltpu.VMEM((n, dm), BF16), pltpu.SemaphoreType.DMA((NDEV,))])


def _proj_bwd_norm(dys, x, gain, dres, gath, name, tm):
    t, dm = x.shape
    n = gath.shape[1] * NDEV
    widths = [d.shape[1] for d in dys]
    assert sum(widths) == n
    nd = len(dys)

    def body(*refs):
        dy_refs = refs[:nd]
        x_ref, g_ref, dres_ref, gath_ref, dx_ref, dxb_ref, dgain_ref, wbuf, sems = refs[nd:]

        @pl.when(pl.program_id(0) == 0)
        def _():
            _load_weight(gath_ref, wbuf, sems)
            dgain_ref[...] = jnp.zeros_like(dgain_ref)
        xv, gain_v = x_ref[...], g_ref[...]
        _, r = _rms_fwd(xv, gain_v)
        dh, c0 = None, 0
        for dy_ref, wd in zip(dy_refs, widths):
            part = _dot(dy_ref[...], wbuf[c0:c0 + wd, :], NN)
            dh = part if dh is None else dh + part
            c0 += wd
        dx, dgain = _rms_bwd(dh, xv, r, gain_v)
        dx = dres_ref[...] + dx
        dx_ref[...] = dx
        dxb_ref[...] = dx.astype(BF16)
        dgain_ref[...] += dgain

    return dict(
        body=body, grid=(t // tm,), name=name, args=[*dys, x, gain, dres, gath],
        out_shape=[jax.ShapeDtypeStruct((t, dm), F32), jax.ShapeDtypeStruct((t, dm), BF16),
                   jax.ShapeDtypeStruct((1, dm), F32)],
        in_specs=[_tok(tm, wd) for wd in widths] + [_tok(tm, dm), _full((1, dm)), _tok(tm, dm), ANY],
        out_specs=[_tok(tm, dm), _tok(tm, dm), _full((1, dm))],
        scratch=[pltpu.VMEM((n, dm), BF16), pltpu.SemaphoreType.DMA((NDEV,))])


def _wgrad(a, b, name, tmm=256):
    parts = list(a) if isinstance(a, (list, tuple)) else [a]
    t = parts[0].shape[0]
    n = b.shape[1]
    tiles = [p.shape[1] // tmm for p in parts]
    first = [sum(tiles[:i]) for i in range(len(parts))]
    m = sum(tiles) * tmm

    def body(*refs):
        a_refs, b_ref, o_ref = refs[:len(parts)], refs[len(parts)], refs[len(parts) + 1]
        j = pl.program_id(0)
        for a_ref, j0, nt in zip(a_refs, first, tiles):
            if len(parts) == 1:
                o_ref[...] = _dot(a_ref[...], b_ref[...], TN).astype(BF16)
            else:
                @pl.when((j >= j0) & (j < j0 + nt))
                def _():
                    o_ref[...] = _dot(a_ref[...], b_ref[...], TN).astype(BF16)

    a_specs = [pl.BlockSpec((t, tmm), lambda j, j0=j0, nt=nt: (0, jnp.clip(j - j0, 0, nt - 1)))
               for j0, nt in zip(first, tiles)]
    return dict(
        body=body, grid=(sum(tiles),), name=name, args=[*parts, b], out_shape=[jax.ShapeDtypeStruct((m, n), BF16)],
        in_specs=a_specs + [pl.BlockSpec((t, n), lambda j: (0, 0))],
        out_specs=[pl.BlockSpec((tmm, n), lambda j: (j, 0))])


def _wgrad_pair(a1, a2, b, name, tmm=256):
    t, m = a1.shape
    n = b.shape[1]
    nt = m // tmm

    def body(a1_ref, a2_ref, b_ref, o1_ref, o2_ref):
        j = pl.program_id(0)

        @pl.when(j < nt)
        def _():
            o1_ref[...] = _dot(a1_ref[...], b_ref[...], TN).astype(BF16)

        @pl.when(j >= nt)
        def _():
            o2_ref[...] = _dot(a2_ref[...], b_ref[...], TN).astype(BF16)

    first = lambda j: jnp.minimum(j, nt - 1)
    second = lambda j: jnp.maximum(j - nt, 0)
    o = jax.ShapeDtypeStruct((m, n), BF16)
    return dict(
        body=body, grid=(2 * nt,), name=name, args=[a1, a2, b], out_shape=[o, o],
        in_specs=[pl.BlockSpec((t, tmm), lambda j: (0, first(j))), pl.BlockSpec((t, tmm), lambda j: (0, second(j))),
                  pl.BlockSpec((t, n), lambda j: (0, 0))],
        out_specs=[pl.BlockSpec((tmm, n), lambda j: (first(j), 0)), pl.BlockSpec((tmm, n), lambda j: (second(j), 0))])


def _halo_specs(tm, t, width, col_blocks):
    nb8 = tm // 8
    last = t // 8 - 1
    prev = [pl.BlockSpec((8, width), lambda i, cb=cb: (jnp.maximum(i * nb8 - 1, 0), cb)) for cb in col_blocks]
    nxt = [pl.BlockSpec((8, width), lambda i, cb=cb: (jnp.minimum((i + 1) * nb8, last), cb)) for cb in col_blocks]
    return prev, nxt


def _shift_rows(z, prev_row, next_row):
    tm = z.shape[0]
    row = lax.broadcasted_iota(jnp.int32, z.shape, 0)
    zm1 = jnp.where(row == 0, prev_row, pltpu.roll(z, 1, 0))
    zp1 = jnp.where(row == tm - 1, next_row, pltpu.roll(z, tm - 1, 0))
    return zm1, zp1


def _gating_fwd(proj, lng, lnb, wsp_ref, bsp_ref, aw):
    tm = proj.shape[0]
    a_u = _gelu(proj[:, 0:aw])
    gv = _gelu(proj[:, aw:2 * aw])
    mu = jnp.mean(gv, axis=-1, keepdims=True)
    xc = gv - mu
    rstd = lax.rsqrt(jnp.mean(xc * xc, axis=-1, keepdims=True) + EPS)
    vn = xc * rstd
    a_v = (vn * lng + lnb).astype(BF16)
    gd = aw // A_GROUPS
    rows = []
    for c in range(tm // CHUNK):
        cols = []
        for g in range(A_GROUPS):
            blk = a_v[c * CHUNK:(c + 1) * CHUNK, g * gd:(g + 1) * gd]
            cols.append(_dot(wsp_ref[g], blk, NN) + bsp_ref[g])
        rows.append(jnp.concatenate(cols, axis=1))
    mixed = jnp.concatenate(rows, axis=0)
    return a_u, vn, rstd, a_v, mixed


def _even_core_fwd(proj, x0, lng, lnb, wsp, bspb, cw, gath, tm):
    t, dm = x0.shape
    aw = lng.shape[1]
    bw = cw.shape[1]
    assert aw == bw and 2 * aw + 3 * bw == proj.shape[1]
    nt = t // tm
    prev, nxt = _halo_specs(tm, t, bw, [3, 4])

    def body(proj_ref, cp_ref, hp_ref, cn_ref, hn_ref, x0_ref, lng_ref, lnb_ref, wsp_ref, bsp_ref, cw_ref, gath_ref,
             x1_ref, y_ref, wbuf, sems):
        i = pl.program_id(0)

        @pl.when(i == 0)
        def _():
            _load_weight(gath_ref, wbuf, sems)
        proj_v = proj_ref[...]
        a_u, _, _, _, mixed = _gating_fwd(proj_v, lng_ref[...], lnb_ref[...], wsp_ref, bsp_ref, aw)
        a_out = a_u * mixed
        bb = proj_v[:, 2 * aw:2 * aw + bw]
        z = proj_v[:, 2 * aw + bw:2 * aw + 2 * bw] * proj_v[:, 2 * aw + 2 * bw:]
        zprev = jnp.where(i > 0, cp_ref[7:8, :] * hp_ref[7:8, :], 0.0)
        znext = jnp.where(i < nt - 1, cn_ref[0:1, :] * hn_ref[0:1, :], 0.0)
        zm1, zp1 = _shift_rows(z, zprev, znext)
        cwv = cw_ref[...]
        conv = zm1 * cwv[0:1, :] + z * cwv[1:2, :] + zp1 * cwv[2:3, :]
        y = jnp.concatenate([a_out, bb * conv], axis=1).astype(BF16)
        y_ref[...] = y
        x1_ref[...] = x0_ref[...] + _dot(y, wbuf[...], NN)

    return dict(
        body=body, grid=(nt,), name="even_core_fwd",
        args=[proj, proj, proj, proj, proj, x0, lng, lnb, wsp, bspb, cw, gath],
        out_shape=[jax.ShapeDtypeStruct((t, dm), F32), jax.ShapeDtypeStruct((t, aw + bw), BF16)],
        in_specs=[_tok(tm, proj.shape[1]), prev[0], prev[1], nxt[0], nxt[1], _tok(tm, dm), _full(lng.shape),
                  _full(lnb.shape), _full(wsp.shape), _full(bspb.shape), _full(cw.shape), ANY],
        out_specs=[_tok(tm, dm), _tok(tm, aw + bw)],
        scratch=[pltpu.VMEM((gath.shape[1] * NDEV, dm), BF16), pltpu.SemaphoreType.DMA((NDEV,))])


def _even_core_bwd(proj, dx1, lng, lnb, wsp, bspb, cw, gath, tm):
    t, dm = dx1.shape
    aw, bw = lng.shape[1], cw.shape[1]
    gd = aw // A_GROUPS
    nt = t // tm
    inw = proj.shape[1]
    prev, nxt = _halo_specs(tm, t, bw, [2, 3, 4])
    nb8 = tm // 8
    last8 = t // 8 - 1

    def body(proj_ref, bp_ref, cp_ref, hp_ref, bn_ref, cn_ref, hn_ref, dx_ref, dxp_ref, dxn_ref,
             lng_ref, lnb_ref, wsp_ref, bsp_ref, cw_ref, gath_ref,
             dproj_ref, dlng_ref, dlnb_ref, dwsp_ref, dbsp_ref, dcw_ref, wbuf, sems):
        i = pl.program_id(0)

        @pl.when(i == 0)
        def _():
            _load_weight(gath_ref, wbuf, sems)
            dlng_ref[...] = jnp.zeros_like(dlng_ref)
            dlnb_ref[...] = jnp.zeros_like(dlnb_ref)
            dwsp_ref[...] = jnp.zeros_like(dwsp_ref)
            dbsp_ref[...] = jnp.zeros_like(dbsp_ref)
            dcw_ref[...] = jnp.zeros_like(dcw_ref)
        proj_v = proj_ref[...]
        lng_v = lng_ref[...]
        a_u, vn, rstd, a_v, mixed = _gating_fwd(proj_v, lng_v, lnb_ref[...], wsp_ref, bsp_ref, aw)
        w = wbuf[...]
        dy = _dot(dx_ref[...].astype(BF16), w, NT)
        da_out, db_out = dy[:, 0:aw], dy[:, aw:]
        da_u = da_out * mixed
        dmixed = da_out * a_u
        dmb = dmixed.astype(BF16)
        rows = []
        for c in range(tm // CHUNK):
            cols = []
            for g in range(A_GROUPS):
                r0, c0 = c * CHUNK, g * gd
                dm_cg = dmb[r0:r0 + CHUNK, c0:c0 + gd]
                cols.append(_dot(wsp_ref[g], dm_cg, TN))
                dwsp_ref[g] += _dot(dm_cg, a_v[r0:r0 + CHUNK, c0:c0 + gd], NT)
                dbsp_ref[g] += dmixed[r0:r0 + CHUNK, c0:c0 + gd]
            rows.append(jnp.concatenate(cols, axis=1))
        dav = jnp.concatenate(rows, axis=0)
        dlng_ref[...] += jnp.sum(dav * vn, axis=0, keepdims=True)
        dlnb_ref[...] += jnp.sum(dav, axis=0, keepdims=True)
        dvn = dav * lng_v
        dgv = rstd * (dvn - jnp.mean(dvn, axis=-1, keepdims=True) - vn * jnp.mean(dvn * vn, axis=-1, keepdims=True))
        dv_pre = dgv * _gelu_grad(proj_v[:, aw:2 * aw])
        du_pre = da_u * _gelu_grad(proj_v[:, 0:aw])
        bb = proj_v[:, 2 * aw:2 * aw + bw]
        bc = proj_v[:, 2 * aw + bw:2 * aw + 2 * bw]
        bh = proj_v[:, 2 * aw + 2 * bw:]
        z = bc * bh
        zprev = jnp.where(i > 0, cp_ref[7:8, :] * hp_ref[7:8, :], 0.0)
        znext = jnp.where(i < nt - 1, cn_ref[0:1, :] * hn_ref[0:1, :], 0.0)
        zm1, zp1 = _shift_rows(z, zprev, znext)
        cwv = cw_ref[...]
        conv = zm1 * cwv[0:1, :] + z * cwv[1:2, :] + zp1 * cwv[2:3, :]
        dbb = db_out * conv
        dconv = db_out * bb
        dx_edge = jnp.concatenate([dxp_ref[...], dxn_ref[...]], axis=0).astype(BF16)
        dy_edge = _dot(dx_edge, w[aw:, :], NT)
        dcprev = jnp.where(i > 0, dy_edge[7:8, :] * bp_ref[7:8, :], 0.0)
        dcnext = jnp.where(i < nt - 1, dy_edge[8:9, :] * bn_ref[0:1, :], 0.0)
        dcm1, dcp1 = _shift_rows(dconv, dcprev, dcnext)
        dz = dcp1 * cwv[0:1, :] + dconv * cwv[1:2, :] + dcm1 * cwv[2:3, :]
        dcw_ref[0:1, :] += jnp.sum(dconv * zm1, axis=0, keepdims=True)
        dcw_ref[1:2, :] += jnp.sum(dconv * z, axis=0, keepdims=True)
        dcw_ref[2:3, :] += jnp.sum(dconv * zp1, axis=0, keepdims=True)
        dproj_ref[...] = jnp.concatenate([du_pre, dv_pre, dbb, dz * bh, dz * bc], axis=1).astype(BF16)

    row8 = lambda f: pl.BlockSpec((8, dm), f)
    return dict(
        body=body, grid=(nt,), name="even_core_bwd",
        args=[proj, proj, proj, proj, proj, proj, proj, dx1, dx1, dx1, lng, lnb, wsp, bspb, cw, gath],
        out_shape=[jax.ShapeDtypeStruct((t, inw), BF16), jax.ShapeDtypeStruct((1, aw), F32),
                   jax.ShapeDtypeStruct((1, aw), F32), jax.ShapeDtypeStruct(wsp.shape, F32),
                   jax.ShapeDtypeStruct((A_GROUPS, CHUNK, gd), F32), jax.ShapeDtypeStruct(cw.shape, F32)],
        in_specs=[_tok(tm, inw), prev[0], prev[1], prev[2], nxt[0], nxt[1], nxt[2], _tok(tm, dm),
                  row8(lambda i: (jnp.maximum(i * nb8 - 1, 0), 0)), row8(lambda i: (jnp.minimum((i + 1) * nb8, last8), 0)),
                  _full(lng.shape), _full(lnb.shape), _full(wsp.shape), _full(bspb.shape), _full(cw.shape), ANY],
        out_specs=[_tok(tm, inw), _full((1, aw)), _full((1, aw)), _full(wsp.shape),
                   _full((A_GROUPS, CHUNK, gd)), _full(cw.shape)],
        scratch=[pltpu.VMEM((gath.shape[1] * NDEV, dm), BF16), pltpu.SemaphoreType.DMA((NDEV,))])


def _ff_chunks(f, width=1024):
    return [(c0, min(c0 + width, f)) for c0 in range(0, f, width)]


def _ffn_up(x, gain, gath_g, gath_u, name, tm):
    t, dm = x.shape
    f = gath_g.shape[1] * NDEV

    def body(x_ref, g_ref, gg_ref, gu_ref, gate_ref, up_ref, act_ref, wg, wu, sems):
        @pl.when(pl.program_id(0) == 0)
        def _():
            _load_weight(gg_ref, wg, sems)
            _load_weight(gu_ref, wu, sems)
        h, _ = _rms_fwd(x_ref[...], g_ref[...])
        hb = h.astype(BF16)
        for c0, c1 in _ff_chunks(f):
            gate = _dot(hb, wg[c0:c1, :], NT)
            up = _dot(hb, wu[c0:c1, :], NT)
            gate_ref[:, c0:c1] = gate.astype(BF16)
            up_ref[:, c0:c1] = up.astype(BF16)
            act_ref[:, c0:c1] = (gate * _sigmoid(gate) * up).astype(BF16)

    o = jax.ShapeDtypeStruct((t, f), BF16)
    return dict(
        body=body, grid=(t // tm,), name=name, args=[x, gain, gath_g, gath_u], out_shape=[o, o, o],
        in_specs=[_tok(tm, dm), _full((1, dm)), ANY, ANY], out_specs=[_tok(tm, f)] * 3,
        scratch=[pltpu.VMEM((f, dm), BF16), pltpu.VMEM((f, dm), BF16), pltpu.SemaphoreType.DMA((NDEV,))])


def _ffn_down(x, act, gath_d, name, tm):
    t, dm = x.shape
    f = act.shape[1]

    def body(x_ref, a_ref, gd_ref, xo_ref, wd, sems):
        @pl.when(pl.program_id(0) == 0)
        def _():
            _load_weight(gd_ref, wd, sems)
        xo_ref[...] = x_ref[...] + _dot(a_ref[...], wd[...], NN)

    return dict(
        body=body, grid=(t // tm,), name=name, args=[x, act, gath_d], out_shape=[jax.ShapeDtypeStruct((t, dm), F32)],
        in_specs=[_tok(tm, dm), _tok(tm, f), ANY], out_specs=[_tok(tm, dm)],
        scratch=[pltpu.VMEM((f, dm), BF16), pltpu.SemaphoreType.DMA((NDEV,))])


def _ffn_down_loss(x, act, gath_d, target, gain, name, tm):
    t, dm = x.shape
    f = act.shape[1]
    steps = t // tm

    def body(x_ref, a_ref, gd_ref, t_ref, g_ref, loss_ref, dx_ref, dxb_ref, dgain_ref, wd, acc, sems):
        i = pl.program_id(0)

        @pl.when(i == 0)
        def _():
            _load_weight(gd_ref, wd, sems)
            acc[...] = jnp.zeros_like(acc)
            dgain_ref[...] = jnp.zeros_like(dgain_ref)
        xv = x_ref[...] + _dot(a_ref[...], wd[...], NN)
        gain_v = g_ref[...]
        y, r = _rms_fwd(xv, gain_v)
        e = y - t_ref[...]
        acc[...] += jnp.sum(e * e, axis=0, keepdims=True)
        dx, dgain = _rms_bwd(e * (1.0 / dm), xv, r, gain_v)
        dx_ref[...] = dx
        dxb_ref[...] = dx.astype(BF16)
        dgain_ref[...] += dgain

        @pl.when(i == steps - 1)
        def _():
            loss_ref[...] = jnp.sum(acc[...], axis=-1, keepdims=True) * (0.5 / dm)

    return dict(
        body=body, grid=(steps,), name=name, args=[x, act, gath_d, target, gain],
        out_shape=[jax.ShapeDtypeStruct((1, 1), F32), jax.ShapeDtypeStruct((t, dm), F32),
                   jax.ShapeDtypeStruct((t, dm), BF16), jax.ShapeDtypeStruct((1, dm), F32)],
        in_specs=[_tok(tm, dm), _tok(tm, f), ANY, _tok(tm, dm), _full((1, dm))],
        out_specs=[_full((1, 1)), _tok(tm, dm), _tok(tm, dm), _full((1, dm))],
        scratch=[pltpu.VMEM((f, dm), BF16), pltpu.VMEM((1, dm), F32), pltpu.SemaphoreType.DMA((NDEV,))])


def _ffn_bwd(dxo, x, gate, up, gain, gath_g, gath_u, gath_d, name, tm):
    t, dm = x.shape
    f = gate.shape[1]

    def body(dxo_ref, x_ref, gate_ref, up_ref, g_ref, gg_ref, gu_ref, gd_ref,
             dx_ref, dxb_ref, dg_ref, du_ref, hb_ref, dgain_ref, wg, wu, wd, sems):
        @pl.when(pl.program_id(0) == 0)
        def _():
            _load_weight(gg_ref, wg, sems)
            _load_weight(gu_ref, wu, sems)
            _load_weight(gd_ref, wd, sems)
            dgain_ref[...] = jnp.zeros_like(dgain_ref)
        xv, gain_v, dxo_v = x_ref[...], g_ref[...], dxo_ref[...]
        h, r = _rms_fwd(xv, gain_v)
        hb_ref[...] = h.astype(BF16)
        dxob = dxo_v.astype(BF16)
        dh = jnp.zeros_like(xv)
        for c0, c1 in _ff_chunks(f):
            gate_v = gate_ref[:, c0:c1].astype(F32)
            up_v = up_ref[:, c0:c1].astype(F32)
            s = _sigmoid(gate_v)
            silu = gate_v * s
            dact = _dot(dxob, wd[c0:c1, :], NT)
            dg = (dact * up_v * (s * (1.0 + gate_v * (1.0 - s)))).astype(BF16)
            du = (dact * silu).astype(BF16)
            dg_ref[:, c0:c1] = dg
            du_ref[:, c0:c1] = du
            dh = dh + _dot(dg, wg[c0:c1, :], NN) + _dot(du, wu[c0:c1, :], NN)
        dx, dgain = _rms_bwd(dh, xv, r, gain_v)
        dx = dxo_v + dx
        dx_ref[...] = dx
        dxb_ref[...] = dx.astype(BF16)
        dgain_ref[...] += dgain

    return dict(
        body=body, grid=(t // tm,), name=name, args=[dxo, x, gate, up, gain, gath_g, gath_u, gath_d],
        out_shape=[jax.ShapeDtypeStruct((t, dm), F32), jax.ShapeDtypeStruct((t, dm), BF16),
                   jax.ShapeDtypeStruct((t, f), BF16), jax.ShapeDtypeStruct((t, f), BF16),
                   jax.ShapeDtypeStruct((t, dm), BF16), jax.ShapeDtypeStruct((1, dm), F32)],
        in_specs=[_tok(tm, dm), _tok(tm, dm), _tok(tm, f), _tok(tm, f), _full((1, dm)), ANY, ANY, ANY],
        out_specs=[_tok(tm, dm), _tok(tm, dm), _tok(tm, f), _tok(tm, f), _tok(tm, dm), _full((1, dm))],
        scratch=[pltpu.VMEM((f, dm), BF16), pltpu.VMEM((f, dm), BF16), pltpu.VMEM((f, dm), BF16),
                 pltpu.SemaphoreType.DMA((NDEV,))])


def _t5_buckets(rel):
    nb = N_BUCKETS // 2
    ret = jnp.where(rel > 0, nb, 0)
    n = jnp.abs(rel)
    max_exact = nb // 2
    nf = jnp.maximum(n, 1).astype(jnp.float32)
    large = max_exact + (jnp.log(nf / max_exact) / math.log(MAX_DISTANCE / max_exact)
                         * (nb - max_exact)).astype(jnp.int32)
    large = jnp.minimum(large, nb - 1)
    return ret + jnp.where(n < max_exact, n, large)


def _bucket_table():
    qi = jnp.arange(CHUNK, dtype=jnp.int32)[:, None]
    kj = jnp.arange(3 * CHUNK, dtype=jnp.int32)[None, :]
    rel = kj - CHUNK - qi
    return jnp.where(jnp.abs(rel) <= CHUNK, _t5_buckets(rel), -1)


def _bias_table(rel_bias_t, buckets):
    nh = rel_bias_t.shape[0]

    def body(rb_ref, bk_ref, o_ref):
        bk = bk_ref[...]
        for h in range(nh):
            acc = jnp.where(bk < 0, NEG, 0.0).astype(F32)
            for b in range(N_BUCKETS):
                acc = jnp.where(bk == b, rb_ref[h, b] * LOG2E, acc)
            o_ref[h] = acc

    return dict(
        body=body, grid=(1,), name="bias_table", args=[rel_bias_t, buckets],
        out_shape=[jax.ShapeDtypeStruct((nh,) + buckets.shape, F32)],
        in_specs=[pl.BlockSpec(memory_space=pltpu.SMEM), _full(buckets.shape)],
        out_specs=[_full((nh,) + buckets.shape)])


def _rel_bias_grad(dbias, buckets):
    nh = dbias.shape[0]

    def body(db_ref, bk_ref, o_ref):
        bk = bk_ref[...]
        lane = lax.broadcasted_iota(jnp.int32, (1, 128), 1)
        for h in range(nh):
            d = db_ref[h]
            row = jnp.zeros((1, 128), F32)
            for b in range(N_BUCKETS):
                s = jnp.sum(jnp.sum(jnp.where(bk == b, d, 0.0), axis=1, keepdims=True), axis=0, keepdims=True)
                row = jnp.where(lane == b, s, row)
            o_ref[h:h + 1, :] = row

    return dict(
        body=body, grid=(1,), name="rel_bias_grad", args=[dbias, buckets],
        out_shape=[jax.ShapeDtypeStruct((nh, 128), F32)],
        in_specs=[_full(dbias.shape), _full(buckets.shape)], out_specs=[_full((nh, 128))])


def _half_masks():
    lane = lax.broadcasted_iota(jnp.int32, (CHUNK, 128), 1)
    return lane < HEAD_DIM, lane >= HEAD_DIM


def _kv_low(ref, starts, hk, lo):
    kt = (hk // 2) * 128
    out = []
    for jj in range(3):
        blk = ref[pl.ds(starts[jj], CHUNK), kt:kt + 128]
        if hk % 2 == 1:
            blk = pltpu.roll(blk, HEAD_DIM, 1)
        out.append(jnp.where(lo, blk, jnp.zeros_like(blk)))
    return out


def _stack_heads(tile_a, tile_b):
    return jnp.concatenate([tile_a, pltpu.roll(tile_a, HEAD_DIM, 1), tile_b, pltpu.roll(tile_b, HEAD_DIM, 1)], axis=0)


def _unstack_heads(o4):
    return (o4[0:CHUNK] + pltpu.roll(o4[CHUNK:2 * CHUNK], HEAD_DIM, 1),
            o4[2 * CHUNK:3 * CHUNK] + pltpu.roll(o4[3 * CHUNK:], HEAD_DIM, 1))


ATT_SLAB = 32


def _softmax_slab(s_scr, hk, g, r0, bias_ref, sink_ref, n, nblk):
    scale = HEAD_DIM ** -0.5 * LOG2E
    h = (N_HEADS // N_KV) * hk + g
    s = []
    for jj in range(3):
        sj = (s_scr[hk, jj, pl.ds(g * CHUNK + r0, ATT_SLAB), :] * scale
              + bias_ref[h, pl.ds(r0, ATT_SLAB), jj * CHUNK:(jj + 1) * CHUNK])
        if jj == 0:
            sj = jnp.where(n > 0, sj, NEG)
        if jj == 2:
            sj = jnp.where(n < nblk - 1, sj, NEG)
        s.append(sj)
    sink = sink_ref[h] * LOG2E
    m = jnp.maximum(jnp.max(jnp.maximum(jnp.maximum(s[0], s[1]), s[2]), axis=-1, keepdims=True), sink)
    e = [jnp.exp2(sj - m) for sj in s]
    es = jnp.exp2(sink - m)
    inv = 1.0 / (jnp.sum(e[0] + e[1] + e[2], axis=-1, keepdims=True) + es)
    return [ej * inv for ej in e], es * inv


def _key_block_starts(n, nblk):
    return [pl.multiple_of(jnp.clip(n - 1 + jj, 0, nblk - 1) * CHUNK, CHUNK) for jj in range(3)]


def _attn_fwd(qkv, x2, bias, sink, gath):
    t, dm = x2.shape
    nblk = t // CHUNK
    kvw = N_KV * HEAD_DIM
    kcb, vcb = dm // kvw, dm // kvw + 1
    slab = (N_KV, 3, 4 * CHUNK, CHUNK)

    def body(q_ref, k_ref, v_ref, x2_ref, bias_ref, sink_ref, gath_ref, x3_ref, att_ref, p_ref, ps_ref,
             wbuf, s_scr, sems):
        n = pl.program_id(0)

        @pl.when(n == 0)
        def _():
            _load_weight(gath_ref, wbuf, sems)
        lo, _ = _half_masks()
        lane_s = lax.broadcasted_iota(jnp.int32, (ATT_SLAB, 128), 1)
        starts = _key_block_starts(n, nblk)
        tiles = []
        for hk in range(N_KV):
            c0 = (2 * hk) * 128
            k_lo = _kv_low(k_ref, starts, hk, lo)
            v_lo = _kv_low(v_ref, starts, hk, lo)
            q4 = _stack_heads(q_ref[:, c0:c0 + 128], q_ref[:, c0 + 128:c0 + 256])
            for jj in range(3):
                s_scr[hk, jj] = _dot(q4, k_lo[jj], NT)
            for g in range(4):
                h = 4 * hk + g
                for r0 in range(0, CHUNK, ATT_SLAB):
                    p, ps = _softmax_slab(s_scr, hk, g, r0, bias_ref, sink_ref, n, nblk)
                    for jj in range(3):
                        p_ref[hk, jj, g * CHUNK + r0:g * CHUNK + r0 + ATT_SLAB, :] = p[jj].astype(BF16)
                    rest = jnp.zeros((ATT_SLAB, 128), F32) if h == 0 else ps_ref[r0:r0 + ATT_SLAB, :]
                    ps_ref[r0:r0 + ATT_SLAB, :] = jnp.where(lane_s == h, ps, rest)
            o4 = _dot(p_ref[hk, 0], v_lo[0], NN) + _dot(p_ref[hk, 1], v_lo[1], NN) + _dot(p_ref[hk, 2], v_lo[2], NN)
            tiles += list(_unstack_heads(o4))
        att = jnp.concatenate(tiles, axis=1).astype(BF16)
        att_ref[...] = att
        x3_ref[...] = x2_ref[...] + _dot(att, wbuf[...], NN)

    blk = pl.BlockSpec((CHUNK, dm), lambda n: (n, 0))
    return dict(
        body=body, grid=(nblk,), name="attn_fwd", args=[qkv, qkv, qkv, x2, bias, sink, gath],
        out_shape=[jax.ShapeDtypeStruct((t, dm), F32), jax.ShapeDtypeStruct((t, dm), BF16),
                   jax.ShapeDtypeStruct((nblk,) + slab, BF16), jax.ShapeDtypeStruct((t, 128), F32)],
        in_specs=[blk, pl.BlockSpec((t, kvw), lambda n: (0, kcb)), pl.BlockSpec((t, kvw), lambda n: (0, vcb)), blk,
                  _full(bias.shape), pl.BlockSpec(memory_space=pltpu.SMEM), ANY],
        out_specs=[blk, blk, pl.BlockSpec((None,) + slab, lambda n: (n, 0, 0, 0, 0)),
                   pl.BlockSpec((CHUNK, 128), lambda n: (n, 0))],
        scratch=[pltpu.VMEM((gath.shape[1] * NDEV, dm), BF16), pltpu.VMEM(slab, F32),
                 pltpu.SemaphoreType.DMA((NDEV,))])


def _attn_bwd(qkv, att, probs, sink_probs, dx3, bias_shape, gath):
    t, dm = dx3.shape
    nblk = t // CHUNK
    kvw = N_KV * HEAD_DIM
    kcb, vcb = dm // kvw, dm // kvw + 1
    scale = HEAD_DIM ** -0.5
    slab = (N_KV, 3, 4 * CHUNK, CHUNK)

    def body(q_ref, k_ref, v_ref, att_ref, p_ref, ps_ref, dx_ref, gath_ref,
             dq_ref, dkb_ref, dvb_ref, dbias_ref, dsink_ref,
             wbuf, dp_scr, ds_scr, prod_scr, dsum_scr, dk_ref, dv_ref, sems):
        n = pl.program_id(0)

        @pl.when(n == 0)
        def _():
            _load_weight(gath_ref, wbuf, sems)
            dk_ref[...] = jnp.zeros_like(dk_ref)
            dv_ref[...] = jnp.zeros_like(dv_ref)
            dbias_ref[...] = jnp.zeros_like(dbias_ref)
            dsink_ref[...] = jnp.zeros_like(dsink_ref)
        lo, hi = _half_masks()
        lane_s = lax.broadcasted_iota(jnp.int32, (ATT_SLAB, 128), 1)
        starts = _key_block_starts(n, nblk)
        dout = _dot(dx_ref[...].astype(BF16), wbuf[...], NT)
        prod_scr[...] = dout * att_ref[...].astype(F32)
        doutb = dout.astype(BF16)
        dq_tiles = []
        for hk in range(N_KV):
            kt = (hk // 2) * 128
            c0 = (2 * hk) * 128
            k_lo = _kv_low(k_ref, starts, hk, lo)
            v_lo = _kv_low(v_ref, starts, hk, lo)
            q4 = _stack_heads(q_ref[:, c0:c0 + 128], q_ref[:, c0 + 128:c0 + 256])
            do4 = _stack_heads(doutb[:, c0:c0 + 128], doutb[:, c0 + 128:c0 + 256])
            for jj in range(3):
                dp_scr[hk, jj] = _dot(do4, v_lo[jj], NT)
            for g in range(4):
                h = 4 * hk + g
                for r0 in range(0, CHUNK, ATT_SLAB):
                    rows = slice(g * CHUNK + r0, g * CHUNK + r0 + ATT_SLAB)
                    pt = prod_scr[r0:r0 + ATT_SLAB, c0 + (g // 2) * 128:c0 + (g // 2 + 1) * 128]
                    msk = lane_s < HEAD_DIM if g % 2 == 0 else lane_s >= HEAD_DIM
                    dsum = jnp.sum(jnp.where(msk, pt, 0.0), axis=-1, keepdims=True)
                    rest = jnp.zeros((ATT_SLAB, 128), F32) if h == 0 else dsum_scr[r0:r0 + ATT_SLAB, :]
                    dsum_scr[r0:r0 + ATT_SLAB, :] = jnp.where(lane_s == h, dsum, rest)
                    for jj in range(3):
                        ds = p_ref[hk, jj, rows, :].astype(F32) * (dp_scr[hk, jj, rows, :] - dsum)
                        dbias_ref[h, r0:r0 + ATT_SLAB, jj * CHUNK:(jj + 1) * CHUNK] += ds
                        ds_scr[hk, jj, rows, :] = ds.astype(BF16)
            dq4 = jnp.zeros((4 * CHUNK, 128), F32)
            for jj in range(3):
                ds4 = ds_scr[hk, jj]
                dq4 = dq4 + _dot(ds4, k_lo[jj], NN) * scale
                dkj = _dot(ds4, q4, TN) * scale
                dvj = _dot(p_ref[hk, jj], do4, TN)
                if hk % 2 == 1:
                    dkj, dvj = pltpu.roll(dkj, HEAD_DIM, 1), pltpu.roll(dvj, HEAD_DIM, 1)
                keep = lo if hk % 2 == 0 else hi
                dk_ref[pl.ds(starts[jj], CHUNK), kt:kt + 128] += jnp.where(keep, dkj, 0.0)
                dv_ref[pl.ds(starts[jj], CHUNK), kt:kt + 128] += jnp.where(keep, dvj, 0.0)
            dq_tiles += list(_unstack_heads(dq4))
        dq_ref[...] = jnp.concatenate(dq_tiles, axis=1).astype(BF16)
        dsink_ref[...] -= jnp.sum(ps_ref[...] * dsum_scr[...], axis=0, keepdims=True)

        @pl.when(n == nblk - 1)
        def _():
            dkb_ref[...] = dk_ref[...].astype(BF16)
            dvb_ref[...] = dv_ref[...].astype(BF16)

    blk = pl.BlockSpec((CHUNK, dm), lambda n: (n, 0))
    return dict(
        body=body, grid=(nblk,), name="attn_bwd", args=[qkv, qkv, qkv, att, probs, sink_probs, dx3, gath],
        out_shape=[jax.ShapeDtypeStruct((t, dm), BF16), jax.ShapeDtypeStruct((t, kvw), BF16),
                   jax.ShapeDtypeStruct((t, kvw), BF16), jax.ShapeDtypeStruct(bias_shape, F32),
                   jax.ShapeDtypeStruct((1, 128), F32)],
        in_specs=[blk, pl.BlockSpec((t, kvw), lambda n: (0, kcb)), pl.BlockSpec((t, kvw), lambda n: (0, vcb)),
                  blk, pl.BlockSpec((None,) + slab, lambda n: (n, 0, 0, 0, 0)),
                  pl.BlockSpec((CHUNK, 128), lambda n: (n, 0)), blk, ANY],
        out_specs=[blk, _full((t, kvw)), _full((t, kvw)), _full(bias_shape), _full((1, 128))],
        scratch=[pltpu.VMEM((gath.shape[1] * NDEV, dm), BF16), pltpu.VMEM(slab, F32), pltpu.VMEM(slab, BF16),
                 pltpu.VMEM((CHUNK, dm), F32), pltpu.VMEM((CHUNK, 128), F32),
                 pltpu.VMEM((t, kvw), F32), pltpu.VMEM((t, kvw), F32), pltpu.SemaphoreType.DMA((NDEV,))])


def _finish_weight(recvs, w, m, v, name):
    nl, r, dm = w.shape
    assert nl == len(recvs) and all(rc.shape[1:] == (r, dm) for rc in recvs)
    td = dm // 2
    wspec = pl.BlockSpec((None, r, td), lambda l, j: (l, 0, j))

    def body(*refs):
        r_refs = refs[:nl]
        w_ref, m_ref, v_ref, g_ref, d_ref, nm_ref, nv_ref = refs[nl:]
        layer = pl.program_id(0)
        for li in range(nl):
            @pl.when(layer == li)
            def _():
                g = r_refs[li][0].astype(F32)
                for d in range(1, recvs[li].shape[0]):
                    g = g + r_refs[li][d].astype(F32)
                delta, nm, nv = _adamw_math(w_ref[...], g, m_ref[...], v_ref[...])
                g_ref[...] = g
                d_ref[...] = delta
                nm_ref[...] = nm
                nv_ref[...] = nv

    o = jax.ShapeDtypeStruct(w.shape, F32)
    return dict(
        body=body, grid=(nl, 2), name=name, args=[*recvs, w, m, v], out_shape=[o, o, o, o],
        in_specs=[pl.BlockSpec((rc.shape[0], r, td), lambda l, j: (0, 0, j)) for rc in recvs] + [wspec] * 3,
        out_specs=[wspec] * 4)


def _adamw_small(ws, ms, vs, slots, lates, loss_slots, name):
    n = len(ws)
    nl = len(lates)

    def total(ref):
        acc = ref[0].astype(F32)
        for d in range(1, NDEV):
            acc = acc + ref[d].astype(F32)
        return acc

    def body(*refs):
        ins, outs = refs[:4 * n + nl + 1], refs[4 * n + nl + 1:]
        for i in range(n):
            w_ref, m_ref, v_ref, s_ref = ins[4 * i:4 * i + 4]
            g_ref, d_ref, nm_ref, nv_ref = outs[4 * i:4 * i + 4]
            g_ref[...] = total(s_ref)
            for k, (at, late) in enumerate(lates):
                if at == i:
                    g_ref[0:late.shape[1], :] = total(ins[4 * n + k])
            d_ref[...], nm_ref[...], nv_ref[...] = _adamw_math(w_ref[...], g_ref[...], m_ref[...], v_ref[...])
        outs[4 * n][...] = total(ins[4 * n + nl])

    args, out_shape = [], []
    for w, m, v, s in zip(ws, ms, vs, slots):
        args += [w, m, v, s]
        out_shape += [jax.ShapeDtypeStruct(w.shape, F32)] * 4
    args += [late for _, late in lates] + [loss_slots]
    out_shape.append(jax.ShapeDtypeStruct((1, 1), F32))
    out = pl.pallas_call(
        body, grid=(1,), out_shape=tuple(out_shape), in_specs=[_full(a.shape) for a in args],
        out_specs=tuple(_full(o.shape) for o in out_shape), compiler_params=_cp(), name=name)(*args)
    return [tuple(out[4 * i:4 * i + 4]) for i in range(n)], out[4 * n]


def kernel(x, norm_mix, norm_ffn, even_w_in, even_v_ln_g, even_v_ln_b, even_w_spatial, even_b_spatial, even_conv_w, even_w_out, attn_w_qkv, attn_sink, rel_bias, attn_w_out, ffn_w_gate, ffn_w_up, ffn_w_down, final_norm, loss_target, m_norm_mix, m_norm_ffn, m_even_w_in, m_even_v_ln_g, m_even_v_ln_b, m_even_w_spatial, m_even_b_spatial, m_even_conv_w, m_even_w_out, m_attn_w_qkv, m_attn_sink, m_rel_bias, m_attn_w_out, m_ffn_w_gate, m_ffn_w_up, m_ffn_w_down, m_final_norm, v_norm_mix, v_norm_ffn, v_even_w_in, v_even_v_ln_g, v_even_v_ln_b, v_even_w_spatial, v_even_b_spatial, v_even_conv_w, v_even_w_out, v_attn_w_qkv, v_attn_sink, v_rel_bias, v_attn_w_out, v_ffn_w_gate, v_ffn_w_up, v_ffn_w_down, v_final_norm):
    t, dm = x.shape[1], x.shape[2]
    aw = even_v_ln_g.shape[1]
    bw = even_conv_w.shape[2] * NDEV
    gd = aw // A_GROUPS
    tm = min(512, t // 2)
    tmf = min(256, t // 2)
    me = _my_index()
    row = lambda a: a.reshape(1, -1)

    colT = lambda w: w.T.astype(BF16)
    sh = dict(winT=colT(even_w_in[0]), wqkvT=colT(attn_w_qkv[0]), wgT0=colT(ffn_w_gate[0]), wuT0=colT(ffn_w_up[0]),
              wgT1=colT(ffn_w_gate[1]), wuT1=colT(ffn_w_up[1]), woe=even_w_out[0].astype(BF16),
              woa=attn_w_out[0].astype(BF16), wd0=ffn_w_down[0].astype(BF16), wd1=ffn_w_down[1].astype(BF16))
    gather = lambda names: _GatherCarry([sh[n] for n in names])

    in_full = lambda a: lax.dynamic_update_slice(jnp.zeros((3, bw), F32), a[0], (0, me * (bw // NDEV)))

    x0 = x[0]
    wsp_b = even_w_spatial[0].astype(BF16)
    bspb = jnp.broadcast_to(even_b_spatial[0][:, :, None], (A_GROUPS, CHUNK, gd))
    buckets = _bucket_table()
    sink = attn_sink[0]

    (bias,), ((g_winT,), (cw_slots,)) = _call(
        _bias_table(rel_bias.T, buckets), [gather(["winT"]), _BroadcastCarry([in_full(even_conv_w)])])
    cw_full = jnp.sum(cw_slots, axis=0)
    (proj, h0b), (g_woe, g_wgT0) = _call(_norm_proj(x0, row(norm_mix[0]), g_winT, F32, "in_proj", tm),
                                         gather(["woe", "wgT0"]))
    (x1, yb), (g_wuT0,) = _call(_even_core_fwd(proj, x0, even_v_ln_g, even_v_ln_b, wsp_b, bspb, cw_full, g_woe, tm),
                                gather(["wuT0"]))
    (gate0, up0, act0), (g_wd0,) = _call(_ffn_up(x1, row(norm_ffn[0]), g_wgT0, g_wuT0, "ffn_up0", tmf), gather(["wd0"]))
    (x2,), (g_wqkvT,) = _call(_ffn_down(x1, act0, g_wd0, "ffn_down0", tm), gather(["wqkvT"]))
    (qkv, h2b), (g_woa,) = _call(_norm_proj(x2, row(norm_mix[1]), g_wqkvT, BF16, "qkv_proj", tm), gather(["woa"]))
    (x3, attb, probs, sink_probs), (g_wgT1, g_wuT1, g_wd1) = _call(
        _attn_fwd(qkv, x2, bias, sink, g_woa), gather(["wgT1", "wuT1", "wd1"]))
    (gate1, up1, act1), _ = _call(_ffn_up(x3, row(norm_ffn[1]), g_wgT1, g_wuT1, "ffn_up1", tmf))
    (loss_part, dx4, dx4b, d_final), _ = _call(
        _ffn_down_loss(x3, act1, g_wd1, loss_target[0], row(final_norm), "ffn_down1_loss", tm))

    (dx3, dx3b, dg1, du1, h3b, d_nffn1), _ = _call(
        _ffn_bwd(dx4, x3, gate1, up1, row(norm_ffn[1]), g_wgT1, g_wuT1, g_wd1, "ffn_bwd1", tmf))
    (p_wgT1, p_wuT1), _ = _call(_wgrad_pair(dg1, du1, h3b, "wgrad_gate_up1"))
    (p_wd1,), ((a_wgT1,), (a_wuT1,)) = _call(
        _wgrad(act1, dx4b, "wgrad_down1"), [_PairCarry(p_wgT1), _PairCarry(p_wuT1)])
    (dq, dk, dv, dbias, dsink), ((r_wgT1,), (a_wd1,)) = _call(
        _attn_bwd(qkv, attb, probs, sink_probs, dx3, bias.shape, g_woa),
        [_ChipSumCarry(p_wgT1, a_wgT1), _PairCarry(p_wd1)])
    (p_woa,), _ = _call(_wgrad(attb, dx3b, "wgrad_attn_out"))
    (dx2, dx2b, d_nmix1), (r_wuT1,) = _call(
        _proj_bwd_norm([dq, dk, dv], x2, row(norm_mix[1]), dx3, g_wqkvT, "qkv_bwd", tm),
        _ChipSumCarry(p_wuT1, a_wuT1))
    (p_wqkvT,), _ = _call(_wgrad([dq, dk, dv], h2b, "wgrad_qkv"))
    (dx1, dx1b, dg0, du0, h1b, d_nffn0), ((r_wd1,), (r_woa, r_wqkvT)) = _call(
        _ffn_bwd(dx2, x1, gate0, up0, row(norm_ffn[0]), g_wgT0, g_wuT0, g_wd0, "ffn_bwd0", tmf),
        [_ChipSumCarry(p_wd1, a_wd1), _GradCarry([p_woa, p_wqkvT])])
    (p_woe,), _ = _call(_wgrad(yb, dx1b, "wgrad_even_out"))
    (p_wgT0,), _ = _call(_wgrad(dg0, h1b, "wgrad_gate0"))
    (p_wuT0,), ((a_wgT0,), (r_woe,)) = _call(
        _wgrad(du0, h1b, "wgrad_up0"), [_PairCarry(p_wgT0), _GradCarry([p_woe])])
    (p_wd0,), ((r_wgT0,), (a_wuT0,)) = _call(
        _wgrad(act0, dx2b, "wgrad_down0"), [_ChipSumCarry(p_wgT0, a_wgT0), _PairCarry(p_wuT0)])
    (dproj, d_lng, d_lnb, d_wsp, d_bsp3, d_cw), ((r_wuT0,), (a_wd0,)) = _call(
        _even_core_bwd(proj, dx1, even_v_ln_g, even_v_ln_b, wsp_b, bspb, cw_full, g_woe, tm),
        [_ChipSumCarry(p_wuT0, a_wuT0), _PairCarry(p_wd0)])
    small_names = ["norm_mix", "norm_ffn", "even_v_ln_g", "even_v_ln_b", "even_w_spatial", "even_b_spatial",
                   "even_conv_w", "attn_sink", "rel_bias", "final_norm"]
    wsp_at = small_names.index("even_w_spatial")
    small_parts = [jnp.concatenate([jnp.zeros_like(d_nmix1), d_nmix1]), jnp.concatenate([d_nffn0, d_nffn1]),
                   d_lng, d_lnb, jnp.sum(d_bsp3, axis=-1)[None], d_cw,
                   dsink[:, 0:N_HEADS], jnp.zeros_like(rel_bias), d_final, loss_part]
    (p_winT,), (r_wd0,) = _call(_wgrad(dproj, h0b, "wgrad_in"), _ChipSumCarry(p_wd0, a_wd0))
    (d_relb_t,), ((a_winT,), rest_slots) = _call(
        _rel_bias_grad(dbias, buckets), [_PairCarry(p_winT), _BroadcastCarry(small_parts)])
    d_relb = d_relb_t[:, 0:N_BUCKETS].T
    (dx0, _, d_nmix0), ((r_winT,), (wsp_slots,)) = _call(
        _proj_bwd_norm([dproj], x0, row(norm_mix[0]), dx1, g_winT, "in_proj_bwd", tm),
        [_ChipSumCarry(p_winT, a_winT), _BroadcastCarry([d_wsp[None].astype(BF16)])])
    small_slots = list(rest_slots[:wsp_at]) + [wsp_slots] + list(rest_slots[wsp_at:])

    grads = {}

    order = ["norm_mix", "norm_ffn", "even_w_in", "even_v_ln_g", "even_v_ln_b", "even_w_spatial", "even_b_spatial",
             "even_conv_w", "even_w_out", "attn_w_qkv", "attn_sink", "rel_bias", "attn_w_out", "ffn_w_gate",
             "ffn_w_up", "ffn_w_down", "final_norm"]
    ws = dict(norm_mix=norm_mix, norm_ffn=norm_ffn, even_w_in=even_w_in, even_v_ln_g=even_v_ln_g,
              even_v_ln_b=even_v_ln_b, even_w_spatial=even_w_spatial, even_b_spatial=even_b_spatial,
              even_conv_w=even_conv_w, even_w_out=even_w_out, attn_w_qkv=attn_w_qkv, attn_sink=attn_sink,
              rel_bias=rel_bias, attn_w_out=attn_w_out, ffn_w_gate=ffn_w_gate, ffn_w_up=ffn_w_up,
              ffn_w_down=ffn_w_down, final_norm=final_norm)
    ms = dict(norm_mix=m_norm_mix, norm_ffn=m_norm_ffn, even_w_in=m_even_w_in, even_v_ln_g=m_even_v_ln_g,
              even_v_ln_b=m_even_v_ln_b, even_w_spatial=m_even_w_spatial, even_b_spatial=m_even_b_spatial,
              even_conv_w=m_even_conv_w, even_w_out=m_even_w_out, attn_w_qkv=m_attn_w_qkv, attn_sink=m_attn_sink,
              rel_bias=m_rel_bias, attn_w_out=m_attn_w_out, ffn_w_gate=m_ffn_w_gate, ffn_w_up=m_ffn_w_up,
              ffn_w_down=m_ffn_w_down, final_norm=m_final_norm)
    vs = dict(norm_mix=v_norm_mix, norm_ffn=v_norm_ffn, even_w_in=v_even_w_in, even_v_ln_g=v_even_v_ln_g,
              even_v_ln_b=v_even_v_ln_b, even_w_spatial=v_even_w_spatial, even_b_spatial=v_even_b_spatial,
              even_conv_w=v_even_conv_w, even_w_out=v_even_w_out, attn_w_qkv=v_attn_w_qkv, attn_sink=v_attn_sink,
              rel_bias=v_rel_bias, attn_w_out=v_attn_w_out, ffn_w_gate=v_ffn_w_gate, ffn_w_up=v_ffn_w_up,
              ffn_w_down=v_ffn_w_down, final_norm=v_final_norm)
    big = dict(ffn_w_gate=([r_wgT0, r_wgT1], True), even_w_in=([r_winT], True), even_w_out=([r_woe], False),
               attn_w_qkv=([r_wqkvT], True), attn_w_out=([r_woa], False), ffn_w_up=([r_wuT0, r_wuT1], True),
               ffn_w_down=([r_wd0, r_wd1], False))
    delta, new_m, new_v = {}, {}, {}
    late_slots = None
    for n, (recvs, transposed) in big.items():
        lay = (lambda a: jnp.swapaxes(a, 1, 2)) if transposed else (lambda a: a)
        spec = _finish_weight(recvs, lay(ws[n]), lay(ms[n]), lay(vs[n]), "finish_" + n)
        if late_slots is None:
            outs, late_slots = _call(spec, _BroadcastCarry([d_nmix0, d_relb]))
        else:
            outs, _ = _call(spec)
        grads[n], delta[n], new_m[n], new_v[n] = [lay(o) for o in outs]
    shaped = lambda n, a: in_full(a) if n == "even_conv_w" else (a.reshape(1, dm) if n == "final_norm" else a)
    pick = lambda dct: [shaped(n, dct[n]) for n in small_names]
    lates = [(small_names.index("norm_mix"), late_slots[0]), (small_names.index("rel_bias"), late_slots[1])]
    results, loss11 = _adamw_small(pick(ws), pick(ms), pick(vs), small_slots[:-1], lates, small_slots[-1],
                                   "adamw_small")
    mine = lambda a: lax.dynamic_slice(a, (0, me * (bw // NDEV)), (3, bw // NDEV))[None]
    for n, res in zip(small_names, results):
        for dst, a in zip((grads, delta, new_m, new_v), res):
            dst[n] = mine(a) if n == "even_conv_w" else (a.reshape(dm) if n == "final_norm" else a)
    loss = loss11[0, 0]
    return (loss, dx0[None], *[grads[n] for n in order], *[delta[n] for n in order],
            *[new_m[n] for n in order], *[new_v[n] for n in order])
```

```python
import math

import jax
import jax.numpy as jnp
import numpy as np
from jax import lax
from jax.experimental import pallas as pl
from jax.experimental.pallas import tpu as pltpu

F32, BF16 = jnp.float32, jnp.bfloat16
NDEV = 8
EPS = 1e-6
CHUNK = 128
A_GROUPS = 4
N_HEADS, N_KV, HEAD_DIM = 16, 4, 64
N_BUCKETS, MAX_DISTANCE = 32, 128
NEG = -1e30
LOG2E = 1.4426950408889634
ADAM_LR, ADAM_B1, ADAM_B2, ADAM_EPS, ADAM_WD, ADAM_STEP = 0.001, 0.9, 0.999, 1e-08, 0.01, 10
VMEM_LIMIT = 56 * 1024 * 1024
MESH = pl.DeviceIdType.MESH
NT = (((1,), (1,)), ((), ()))
NN = (((1,), (0,)), ((), ()))
TN = (((0,), (0,)), ((), ()))
ANY = pl.BlockSpec(memory_space=pl.ANY)


def _cp(n_grid=1):
    return pltpu.CompilerParams(dimension_semantics=("arbitrary",) * n_grid, vmem_limit_bytes=VMEM_LIMIT)


def _dot(a, b, dims):
    return lax.dot_general(a, b, dims, preferred_element_type=F32)


def _my_index():
    return 4 * lax.axis_index("x") + 2 * lax.axis_index("y") + lax.axis_index("c")


def _peer(k):
    x, y, c = lax.axis_index("x"), lax.axis_index("y"), lax.axis_index("c")
    px = 1 - x if k & 4 else x
    py = 1 - y if k & 2 else y
    pc = 1 - c if k & 1 else c
    return (px, py, pc)


def _load_weight(gath_ref, wbuf, sems):
    rows = gath_ref.shape[1]
    cps = [pltpu.make_async_copy(gath_ref.at[d], wbuf.at[pl.ds(d * rows, rows), :], sems.at[d]) for d in range(NDEV)]
    for c in cps:
        c.start()
    for c in cps:
        c.wait()


class _GatherCarry:
    def __init__(self, pieces):
        self.inputs = list(pieces)
        self.n = len(pieces)
        self.out_shape = [jax.ShapeDtypeStruct((NDEV,) + p.shape, p.dtype) for p in pieces]
        self.scratch = [pltpu.SemaphoreType.DMA((7 * self.n,)), pltpu.SemaphoreType.DMA((7 * self.n,)),
                        pltpu.SemaphoreType.DMA((self.n,))]

    def _ctx(self):
        x, y, c = lax.axis_index("x"), lax.axis_index("y"), lax.axis_index("c")
        chips = [(1 - x, y), (x, 1 - y), (1 - x, 1 - y)]
        return (x, y, c), (x, y, 1 - c), chips, c

    def _copy(self, k, j, block, to, ins, outs, sems, src=None):
        send_sems, recv_sems, _ = sems
        slot = outs[j].at[4 * block[0] + 2 * block[1] + block[2]]
        return pltpu.make_async_remote_copy(
            src_ref=slot if src is None else src, dst_ref=slot, send_sem=send_sems.at[k * self.n + j],
            recv_sem=recv_sems.at[k * self.n + j], device_id=to, device_id_type=MESH)

    def start(self, ins, outs, sems):
        me, sibling, chips, c = self._ctx()
        for j in range(self.n):
            pltpu.make_async_copy(ins[j], outs[j].at[4 * me[0] + 2 * me[1] + me[2]], sems[2].at[j]).start()
            self._copy(0, j, me, sibling, ins, outs, sems, src=ins[j]).start()
            for q, chip in enumerate(chips):
                self._copy(1 + q, j, me, (*chip, c), ins, outs, sems, src=ins[j]).start()

    def mid(self, ins, outs, sems):
        me, sibling, chips, c = self._ctx()
        for q, chip in enumerate(chips):
            for j in range(self.n):
                self._copy(1 + q, j, (*chip, c), me, ins, outs, sems).wait_recv()
                self._copy(4 + q, j, (*chip, c), sibling, ins, outs, sems).start()

    def finish(self, ins, outs, sems):
        me, sibling, chips, c = self._ctx()
        for j in range(self.n):
            self._copy(0, j, sibling, me, ins, outs, sems).wait_recv()
            for q, chip in enumerate(chips):
                self._copy(4 + q, j, (*chip, 1 - c), me, ins, outs, sems).wait_recv()
        for j in range(self.n):
            self._copy(0, j, me, sibling, ins, outs, sems, src=ins[j]).wait_send()
            for q, chip in enumerate(chips):
                self._copy(1 + q, j, me, (*chip, c), ins, outs, sems, src=ins[j]).wait_send()
                self._copy(4 + q, j, (*chip, c), sibling, ins, outs, sems).wait_send()
            pltpu.make_async_copy(ins[j], outs[j].at[0], sems[2].at[j]).wait()


class _GradCarry:
    def __init__(self, pieces):
        self.inputs = list(pieces)
        self.n = len(pieces)
        self.rows = [p.shape[0] // NDEV for p in pieces]
        self.out_shape = [jax.ShapeDtypeStruct((NDEV, r, p.shape[1]), p.dtype) for p, r in zip(pieces, self.rows)]
        self.scratch = [pltpu.SemaphoreType.DMA((7 * self.n,)), pltpu.SemaphoreType.DMA((7 * self.n,)),
                        pltpu.SemaphoreType.DMA((self.n,))]

    def _copies(self, ins, outs, sems):
        me = _my_index()
        local, remote = [], []
        for j in range(self.n):
            r = self.rows[j]
            local.append(pltpu.make_async_copy(ins[j].at[pl.ds(pl.multiple_of(me * r, 16), r), :], outs[j].at[me],
                                               sems[2].at[j]))
            for k in range(1, NDEV):
                peer = _peer(k)
                pidx = 4 * peer[0] + 2 * peer[1] + peer[2]
                remote.append(pltpu.make_async_remote_copy(
                    src_ref=ins[j].at[pl.ds(pl.multiple_of(pidx * r, 16), r), :], dst_ref=outs[j].at[me],
                    send_sem=sems[0].at[(k - 1) * self.n + j], recv_sem=sems[1].at[(k - 1) * self.n + j],
                    device_id=peer, device_id_type=MESH))
        return local, remote

    def start(self, ins, outs, sems):
        local, remote = self._copies(ins, outs, sems)
        for cp in local + remote:
            cp.start()

    def mid(self, ins, outs, sems):
        pass

    def finish(self, ins, outs, sems):
        local, remote = self._copies(ins, outs, sems)
        for cp in remote + local:
            cp.wait()


class _BroadcastCarry:
    def __init__(self, parts):
        self.inputs = list(parts)
        self.n = len(self.inputs)
        self.out_shape = [jax.ShapeDtypeStruct((NDEV,) + p.shape, p.dtype) for p in self.inputs]
        self.scratch = [pltpu.SemaphoreType.DMA((7 * self.n,)), pltpu.SemaphoreType.DMA((7 * self.n,)),
                        pltpu.SemaphoreType.DMA((self.n,))]

    def _copies(self, ins, outs, sems):
        me = _my_index()
        cps = []
        for j in range(self.n):
            cps.append(pltpu.make_async_copy(ins[j], outs[j].at[me], sems[2].at[j]))
            cps += [pltpu.make_async_remote_copy(
                src_ref=ins[j], dst_ref=outs[j].at[me], send_sem=sems[0].at[(k - 1) * self.n + j],
                recv_sem=sems[1].at[(k - 1) * self.n + j], device_id=_peer(k), device_id_type=MESH)
                for k in range(1, NDEV)]
        return cps

    def start(self, ins, outs, sems):
        for cp in self._copies(ins, outs, sems):
            cp.start()

    def mid(self, ins, outs, sems):
        pass

    def finish(self, ins, outs, sems):
        for cp in self._copies(ins, outs, sems):
            cp.wait()


class _PairCarry:
    def __init__(self, piece):
        self.inputs = [piece]
        self.r = piece.shape[0] // NDEV
        self.out_shape = [jax.ShapeDtypeStruct((4, self.r, piece.shape[1]), piece.dtype)]
        self.scratch = [pltpu.SemaphoreType.DMA((4,)), pltpu.SemaphoreType.DMA((4,))]

    def _copies(self, ins, outs, sems):
        x, y, c = lax.axis_index("x"), lax.axis_index("y"), lax.axis_index("c")
        return [pltpu.make_async_remote_copy(
            src_ref=ins[0].at[pl.ds(pl.multiple_of((2 * q + 1 - c) * self.r, 16), self.r), :], dst_ref=outs[0].at[q],
            send_sem=sems[0].at[q], recv_sem=sems[1].at[q], device_id=(x, y, 1 - c), device_id_type=MESH)
            for q in range(4)]

    def start(self, ins, outs, sems):
        for cp in self._copies(ins, outs, sems):
            cp.start()

    def mid(self, ins, outs, sems):
        pass

    def finish(self, ins, outs, sems):
        for cp in self._copies(ins, outs, sems):
            cp.wait()


class _ChipSumCarry:
    def __init__(self, piece, landed):
        self.inputs = [piece, landed]
        self.r, dm = piece.shape[0] // NDEV, piece.shape[1]
        self.out_shape = [jax.ShapeDtypeStruct((4, self.r, dm), piece.dtype)]
        self.scratch = [pltpu.VMEM((4, self.r, dm), piece.dtype), pltpu.VMEM((8, self.r, dm), piece.dtype),
                        pltpu.SemaphoreType.DMA((8,)), pltpu.SemaphoreType.DMA((3,)), pltpu.SemaphoreType.DMA((3,)),
                        pltpu.SemaphoreType.DMA(())]

    def _copies(self, outs, scr):
        sums, _, _, send_sems, recv_sems, local_sem = scr
        x, y, c = lax.axis_index("x"), lax.axis_index("y"), lax.axis_index("c")
        mine = 2 * x + y
        local = pltpu.make_async_copy(sums.at[mine], outs[0].at[mine], local_sem)
        remote = []
        for k in range(1, 4):
            px = 1 - x if k & 2 else x
            py = 1 - y if k & 1 else y
            remote.append(pltpu.make_async_remote_copy(
                src_ref=sums.at[2 * px + py], dst_ref=outs[0].at[mine], send_sem=send_sems.at[k - 1],
                recv_sem=recv_sems.at[k - 1], device_id=(px, py, c), device_id_type=MESH))
        return local, remote

    def start(self, ins, outs, scr):
        sums, stage, stage_sems = scr[0], scr[1], scr[2]
        c = lax.axis_index("c")
        loads = []
        for q in range(4):
            loads.append((
                pltpu.make_async_copy(ins[0].at[pl.ds(pl.multiple_of((2 * q + c) * self.r, 16), self.r), :],
                                      stage.at[2 * q], stage_sems.at[2 * q]),
                pltpu.make_async_copy(ins[1].at[q], stage.at[2 * q + 1], stage_sems.at[2 * q + 1])))
        for a, b in loads:
            a.start()
            b.start()
        for q, (a, b) in enumerate(loads):
            a.wait()
            b.wait()
            sums[q] = (stage[2 * q].astype(F32) + stage[2 * q + 1].astype(F32)).astype(sums.dtype)
        local, remote = self._copies(outs, scr)
        for cp in [local] + remote:
            cp.start()

    def mid(self, ins, outs, scr):
        pass

    def finish(self, ins, outs, scr):
        local, remote = self._copies(outs, scr)
        for cp in remote + [local]:
            cp.wait()


def _call(spec, carry=None):
    body, grid = spec["body"], spec["grid"]
    in_specs, out_specs, out_shape = list(spec["in_specs"]), list(spec["out_specs"]), list(spec["out_shape"])
    scratch, args = list(spec.get("scratch", [])), list(spec["args"])
    if carry is None:
        out = pl.pallas_call(body, grid=grid, in_specs=in_specs, out_specs=tuple(out_specs),
                             out_shape=tuple(out_shape), scratch_shapes=scratch, compiler_params=_cp(len(grid)),
                             name=spec["name"])(*args)
        return tuple(out), ()
    carries = list(carry) if isinstance(carry, (list, tuple)) else [carry]
    n_in, n_out, n_s = len(in_specs), len(out_specs), len(scratch)
    steps = int(np.prod(grid))

    def split(refs, counts):
        parts, o = [], 0
        for cnt in counts:
            parts.append(refs[o:o + cnt])
            o += cnt
        return parts

    c_in = [len(cr.inputs) for cr in carries]
    c_out = [len(cr.out_shape) for cr in carries]
    c_scr = [len(cr.scratch) for cr in carries]

    def wrapped(*refs):
        ins, cins, outs, couts, scr, cscr = split(refs, [n_in, sum(c_in), n_out, sum(c_out), n_s, sum(c_scr)])
        per = list(zip(carries, split(cins, c_in), split(couts, c_out), split(cscr, c_scr)))
        step = pl.program_id(0)
        for ax in range(1, len(grid)):
            step = step * grid[ax] + pl.program_id(ax)

        @pl.when(step == 0)
        def _():
            for cr, ci, co, cs in per:
                cr.start(ci, co, cs)
        if steps >= 3:
            @pl.when(step == steps - 2)
            def _():
                for cr, ci, co, cs in per:
                    cr.mid(ci, co, cs)
        body(*ins, *outs, *scr)

        @pl.when(step == steps - 1)
        def _():
            for cr, ci, co, cs in per:
                if steps < 3:
                    cr.mid(ci, co, cs)
                cr.finish(ci, co, cs)

    out = pl.pallas_call(
        wrapped, grid=grid, in_specs=in_specs + [ANY] * sum(c_in), out_specs=tuple(out_specs + [ANY] * sum(c_out)),
        out_shape=tuple(out_shape + [s for cr in carries for s in cr.out_shape]),
        scratch_shapes=scratch + [s for cr in carries for s in cr.scratch],
        compiler_params=_cp(len(grid)), name=spec["name"])(*args, *[a for cr in carries for a in cr.inputs])
    c_res = [tuple(p) for p in split(out[n_out:], c_out)]
    return tuple(out[:n_out]), (c_res if isinstance(carry, (list, tuple)) else c_res[0])


def _rms_fwd(x, gain):
    r = lax.rsqrt(jnp.mean(x * x, axis=-1, keepdims=True) + EPS)
    return x * r * gain, r


def _rms_bwd(dh, x, r, gain):
    a = dh * gain
    dx = r * a - x * (r * r * r) * jnp.mean(a * x, axis=-1, keepdims=True)
    dgain = jnp.sum(dh * (x * r), axis=0, keepdims=True)
    return dx, dgain


def _gelu(x):
    return 0.5 * x * (1.0 + lax.erf(x * 0.7071067811865476))


def _gelu_grad(x):
    return 0.5 * (1.0 + lax.erf(x * 0.7071067811865476)) + x * jnp.exp(-0.5 * x * x) * 0.3989422804014327


def _sigmoid(x):
    return 1.0 / (1.0 + jnp.exp(-x))


def _adamw_math(w, g, m, v):
    nm = ADAM_B1 * m + (1.0 - ADAM_B1) * g
    nv = ADAM_B2 * v + (1.0 - ADAM_B2) * (g * g)
    m_hat = nm / (1.0 - ADAM_B1 ** ADAM_STEP)
    v_hat = nv / (1.0 - ADAM_B2 ** ADAM_STEP)
    return -ADAM_LR * (m_hat / (jnp.sqrt(v_hat) + ADAM_EPS) + ADAM_WD * w), nm, nv


def _tok(tm, w):
    return pl.BlockSpec((tm, w), lambda i: (i, 0))


def _full(shape):
    return pl.BlockSpec(shape, lambda *i: (0,) * len(shape))


def _norm_proj(x, gain, gath, out_dtype, name, tm):
    t, dm = x.shape
    n = gath.shape[1] * NDEV

    def body(x_ref, g_ref, gath_ref, proj_ref, hb_ref, wbuf, sems):
        @pl.when(pl.program_id(0) == 0)
        def _():
            _load_weight(gath_ref, wbuf, sems)
        h, _ = _rms_fwd(x_ref[...], g_ref[...])
        hb = h.astype(BF16)
        hb_ref[...] = hb
        proj_ref[...] = _dot(hb, wbuf[...], NT).astype(out_dtype)

    return dict(
        body=body, grid=(t // tm,), name=name, args=[x, gain, gath],
        out_shape=[jax.ShapeDtypeStruct((t, n), out_dtype), jax.ShapeDtypeStruct((t, dm), BF16)],
        in_specs=[_tok(tm, dm), _full((1, dm)), ANY], out_specs=[_tok(tm, n), _tok(tm, dm)],
        scratch=[pltpu.VMEM((n, dm), BF16), pltpu.SemaphoreType.DMA((NDEV,))])


def _proj_bwd_norm(dys, x, gain, dres, gath, name, tm):
    t, dm = x.shape
    n = gath.shape[1] * NDEV
    widths = [d.shape[1] for d in dys]
    assert sum(widths) == n
    nd = len(dys)

    def body(*refs):
        dy_refs = refs[:nd]
        x_ref, g_ref, dres_ref, gath_ref, dx_ref, dxb_ref, dgain_ref, wbuf, sems = refs[nd:]

        @pl.when(pl.program_id(0) == 0)
        def _():
            _load_weight(gath_ref, wbuf, sems)
            dgain_ref[...] = jnp.zeros_like(dgain_ref)
        xv, gain_v = x_ref[...], g_ref[...]
        _, r = _rms_fwd(xv, gain_v)
        dh, c0 = None, 0
        for dy_ref, wd in zip(dy_refs, widths):
            part = _dot(dy_ref[...], wbuf[c0:c0 + wd, :], NN)
            dh = part if dh is None else dh + part
            c0 += wd
        dx, dgain = _rms_bwd(dh, xv, r, gain_v)
        dx = dres_ref[...] + dx
        dx_ref[...] = dx
        dxb_ref[...] = dx.astype(BF16)
        dgain_ref[...] += dgain

    return dict(
        body=body, grid=(t // tm,), name=name, args=[*dys, x, gain, dres, gath],
        out_shape=[jax.ShapeDtypeStruct((t, dm), F32), jax.ShapeDtypeStruct((t, dm), BF16),
                   jax.ShapeDtypeStruct((1, dm), F32)],
        in_specs=[_tok(tm, wd) for wd in widths] + [_tok(tm, dm), _full((1, dm)), _tok(tm, dm), ANY],
        out_specs=[_tok(tm, dm), _tok(tm, dm), _full((1, dm))],
        scratch=[pltpu.VMEM((n, dm), BF16), pltpu.SemaphoreType.DMA((NDEV,))])


def _wgrad(a, b, name, tmm=256):
    parts = list(a) if isinstance(a, (list, tuple)) else [a]
    t = parts[0].shape[0]
    n = b.shape[1]
    tiles = [p.shape[1] // tmm for p in parts]
    first = [sum(tiles[:i]) for i in range(len(parts))]
    m = sum(tiles) * tmm

    def body(*refs):
        a_refs, b_ref, o_ref = refs[:len(parts)], refs[len(parts)], refs[len(parts) + 1]
        j = pl.program_id(0)
        for a_ref, j0, nt in zip(a_refs, first, tiles):
            if len(parts) == 1:
                o_ref[...] = _dot(a_ref[...], b_ref[...], TN).astype(BF16)
            else:
                @pl.when((j >= j0) & (j < j0 + nt))
                def _():
                    o_ref[...] = _dot(a_ref[...], b_ref[...], TN).astype(BF16)

    a_specs = [pl.BlockSpec((t, tmm), lambda j, j0=j0, nt=nt: (0, jnp.clip(j - j0, 0, nt - 1)))
               for j0, nt in zip(first, tiles)]
    return dict(
        body=body, grid=(sum(tiles),), name=name, args=[*parts, b], out_shape=[jax.ShapeDtypeStruct((m, n), BF16)],
        in_specs=a_specs + [pl.BlockSpec((t, n), lambda j: (0, 0))],
        out_specs=[pl.BlockSpec((tmm, n), lambda j: (j, 0))])


def _wgrad_pair(a1, a2, b, name, tmm=256):
    t, m = a1.shape
    n = b.shape[1]
    nt = m // tmm

    def body(a1_ref, a2_ref, b_ref, o1_ref, o2_ref):
        j = pl.program_id(0)

        @pl.when(j < nt)
        def _():
            o1_ref[...] = _dot(a1_ref[...], b_ref[...], TN).astype(BF16)

        @pl.when(j >= nt)
        def _():
            o2_ref[...] = _dot(a2_ref[...], b_ref[...], TN).astype(BF16)

    first = lambda j: jnp.minimum(j, nt - 1)
    second = lambda j: jnp.maximum(j - nt, 0)
    o = jax.ShapeDtypeStruct((m, n), BF16)
    return dict(
        body=body, grid=(2 * nt,), name=name, args=[a1, a2, b], out_shape=[o, o],
        in_specs=[pl.BlockSpec((t, tmm), lambda j: (0, first(j))), pl.BlockSpec((t, tmm), lambda j: (0, second(j))),
                  pl.BlockSpec((t, n), lambda j: (0, 0))],
        out_specs=[pl.BlockSpec((tmm, n), lambda j: (first(j), 0)), pl.BlockSpec((tmm, n), lambda j: (second(j), 0))])


def _halo_specs(tm, t, width, col_blocks):
    nb8 = tm // 8
    last = t // 8 - 1
    prev = [pl.BlockSpec((8, width), lambda i, cb=cb: (jnp.maximum(i * nb8 - 1, 0), cb)) for cb in col_blocks]
    nxt = [pl.BlockSpec((8, width), lambda i, cb=cb: (jnp.minimum((i + 1) * nb8, last), cb)) for cb in col_blocks]
    return prev, nxt


def _shift_rows(z, prev_row, next_row):
    tm = z.shape[0]
    row = lax.broadcasted_iota(jnp.int32, z.shape, 0)
    zm1 = jnp.where(row == 0, prev_row, pltpu.roll(z, 1, 0))
    zp1 = jnp.where(row == tm - 1, next_row, pltpu.roll(z, tm - 1, 0))
    return zm1, zp1


def _gating_fwd(proj, lng, lnb, wsp_ref, bsp_ref, aw):
    tm = proj.shape[0]
    a_u = _gelu(proj[:, 0:aw])
    gv = _gelu(proj[:, aw:2 * aw])
    mu = jnp.mean(gv, axis=-1, keepdims=True)
    xc = gv - mu
    rstd = lax.rsqrt(jnp.mean(xc * xc, axis=-1, keepdims=True) + EPS)
    vn = xc * rstd
    a_v = (vn * lng + lnb).astype(BF16)
    gd = aw // A_GROUPS
    rows = []
    for c in range(tm // CHUNK):
        cols = []
        for g in range(A_GROUPS):
            blk = a_v[c * CHUNK:(c + 1) * CHUNK, g * gd:(g + 1) * gd]
            cols.append(_dot(wsp_ref[g], blk, NN) + bsp_ref[g])
        rows.append(jnp.concatenate(cols, axis=1))
    mixed = jnp.concatenate(rows, axis=0)
    return a_u, vn, rstd, a_v, mixed


def _even_core_fwd(proj, x0, lng, lnb, wsp, bspb, cw, gath, tm):
    t, dm = x0.shape
    aw = lng.shape[1]
    bw = cw.shape[1]
    assert aw == bw and 2 * aw + 3 * bw == proj.shape[1]
    nt = t // tm
    prev, nxt = _halo_specs(tm, t, bw, [3, 4])

    def body(proj_ref, cp_ref, hp_ref, cn_ref, hn_ref, x0_ref, lng_ref, lnb_ref, wsp_ref, bsp_ref, cw_ref, gath_ref,
             x1_ref, y_ref, wbuf, sems):
        i = pl.program_id(0)

        @pl.when(i == 0)
        def _():
            _load_weight(gath_ref, wbuf, sems)
        proj_v = proj_ref[...]
        a_u, _, _, _, mixed = _gating_fwd(proj_v, lng_ref[...], lnb_ref[...], wsp_ref, bsp_ref, aw)
        a_out = a_u * mixed
        bb = proj_v[:, 2 * aw:2 * aw + bw]
        z = proj_v[:, 2 * aw + bw:2 * aw + 2 * bw] * proj_v[:, 2 * aw + 2 * bw:]
        zprev = jnp.where(i > 0, cp_ref[7:8, :] * hp_ref[7:8, :], 0.0)
        znext = jnp.where(i < nt - 1, cn_ref[0:1, :] * hn_ref[0:1, :], 0.0)
        zm1, zp1 = _shift_rows(z, zprev, znext)
        cwv = cw_ref[...]
        conv = zm1 * cwv[0:1, :] + z * cwv[1:2, :] + zp1 * cwv[2:3, :]
        y = jnp.concatenate([a_out, bb * conv], axis=1).astype(BF16)
        y_ref[...] = y
        x1_ref[...] = x0_ref[...] + _dot(y, wbuf[...], NN)

    return dict(
        body=body, grid=(nt,), name="even_core_fwd",
        args=[proj, proj, proj, proj, proj, x0, lng, lnb, wsp, bspb, cw, gath],
        out_shape=[jax.ShapeDtypeStruct((t, dm), F32), jax.ShapeDtypeStruct((t, aw + bw), BF16)],
        in_specs=[_tok(tm, proj.shape[1]), prev[0], prev[1], nxt[0], nxt[1], _tok(tm, dm), _full(lng.shape),
                  _full(lnb.shape), _full(wsp.shape), _full(bspb.shape), _full(cw.shape), ANY],
        out_specs=[_tok(tm, dm), _tok(tm, aw + bw)],
        scratch=[pltpu.VMEM((gath.shape[1] * NDEV, dm), BF16), pltpu.SemaphoreType.DMA((NDEV,))])


def _even_core_bwd(proj, dx1, lng, lnb, wsp, bspb, cw, gath, tm):
    t, dm = dx1.shape
    aw, bw = lng.shape[1], cw.shape[1]
    gd = aw // A_GROUPS
    nt = t // tm
    inw = proj.shape[1]
    prev, nxt = _halo_specs(tm, t, bw, [2, 3, 4])
    nb8 = tm // 8
    last8 = t // 8 - 1

    def body(proj_ref, bp_ref, cp_ref, hp_ref, bn_ref, cn_ref, hn_ref, dx_ref, dxp_ref, dxn_ref,
             lng_ref, lnb_ref, wsp_ref, bsp_ref, cw_ref, gath_ref,
             dproj_ref, dlng_ref, dlnb_ref, dwsp_ref, dbsp_ref, dcw_ref, wbuf, sems):
        i = pl.program_id(0)

        @pl.when(i == 0)
        def _():
            _load_weight(gath_ref, wbuf, sems)
            dlng_ref[...] = jnp.zeros_like(dlng_ref)
            dlnb_ref[...] = jnp.zeros_like(dlnb_ref)
            dwsp_ref[...] = jnp.zeros_like(dwsp_ref)
            dbsp_ref[...] = jnp.zeros_like(dbsp_ref)
            dcw_ref[...] = jnp.zeros_like(dcw_ref)
        proj_v = proj_ref[...]
        lng_v = lng_ref[...]
        a_u, vn, rstd, a_v, mixed = _gating_fwd(proj_v, lng_v, lnb_ref[...], wsp_ref, bsp_ref, aw)
        w = wbuf[...]
        dy = _dot(dx_ref[...].astype(BF16), w, NT)
        da_out, db_out = dy[:, 0:aw], dy[:, aw:]
        da_u = da_out * mixed
        dmixed = da_out * a_u
        dmb = dmixed.astype(BF16)
        rows = []
        for c in range(tm // CHUNK):
            cols = []
            for g in range(A_GROUPS):
                r0, c0 = c * CHUNK, g * gd
                dm_cg = dmb[r0:r0 + CHUNK, c0:c0 + gd]
                cols.append(_dot(wsp_ref[g], dm_cg, TN))
                dwsp_ref[g] += _dot(dm_cg, a_v[r0:r0 + CHUNK, c0:c0 + gd], NT)
                dbsp_ref[g] += dmixed[r0:r0 + CHUNK, c0:c0 + gd]
            rows.append(jnp.concatenate(cols, axis=1))
        dav = jnp.concatenate(rows, axis=0)
        dlng_ref[...] += jnp.sum(dav * vn, axis=0, keepdims=True)
        dlnb_ref[...] += jnp.sum(dav, axis=0, keepdims=True)
        dvn = dav * lng_v
        dgv = rstd * (dvn - jnp.mean(dvn, axis=-1, keepdims=True) - vn * jnp.mean(dvn * vn, axis=-1, keepdims=True))
        dv_pre = dgv * _gelu_grad(proj_v[:, aw:2 * aw])
        du_pre = da_u * _gelu_grad(proj_v[:, 0:aw])
        bb = proj_v[:, 2 * aw:2 * aw + bw]
        bc = proj_v[:, 2 * aw + bw:2 * aw + 2 * bw]
        bh = proj_v[:, 2 * aw + 2 * bw:]
        z = bc * bh
        zprev = jnp.where(i > 0, cp_ref[7:8, :] * hp_ref[7:8, :], 0.0)
        znext = jnp.where(i < nt - 1, cn_ref[0:1, :] * hn_ref[0:1, :], 0.0)
        zm1, zp1 = _shift_rows(z, zprev, znext)
        cwv = cw_ref[...]
        conv = zm1 * cwv[0:1, :] + z * cwv[1:2, :] + zp1 * cwv[2:3, :]
        dbb = db_out * conv
        dconv = db_out * bb
        dx_edge = jnp.concatenate([dxp_ref[...], dxn_ref[...]], axis=0).astype(BF16)
        dy_edge = _dot(dx_edge, w[aw:, :], NT)
        dcprev = jnp.where(i > 0, dy_edge[7:8, :] * bp_ref[7:8, :], 0.0)
        dcnext = jnp.where(i < nt - 1, dy_edge[8:9, :] * bn_ref[0:1, :], 0.0)
        dcm1, dcp1 = _shift_rows(dconv, dcprev, dcnext)
        dz = dcp1 * cwv[0:1, :] + dconv * cwv[1:2, :] + dcm1 * cwv[2:3, :]
        dcw_ref[0:1, :] += jnp.sum(dconv * zm1, axis=0, keepdims=True)
        dcw_ref[1:2, :] += jnp.sum(dconv * z, axis=0, keepdims=True)
        dcw_ref[2:3, :] += jnp.sum(dconv * zp1, axis=0, keepdims=True)
        dproj_ref[...] = jnp.concatenate([du_pre, dv_pre, dbb, dz * bh, dz * bc], axis=1).astype(BF16)

    row8 = lambda f: pl.BlockSpec((8, dm), f)
    return dict(
        body=body, grid=(nt,), name="even_core_bwd",
        args=[proj, proj, proj, proj, proj, proj, proj, dx1, dx1, dx1, lng, lnb, wsp, bspb, cw, gath],
        out_shape=[jax.ShapeDtypeStruct((t, inw), BF16), jax.ShapeDtypeStruct((1, aw), F32),
                   jax.ShapeDtypeStruct((1, aw), F32), jax.ShapeDtypeStruct(wsp.shape, F32),
                   jax.ShapeDtypeStruct((A_GROUPS, CHUNK, gd), F32), jax.ShapeDtypeStruct(cw.shape, F32)],
        in_specs=[_tok(tm, inw), prev[0], prev[1], prev[2], nxt[0], nxt[1], nxt[2], _tok(tm, dm),
                  row8(lambda i: (jnp.maximum(i * nb8 - 1, 0), 0)), row8(lambda i: (jnp.minimum((i + 1) * nb8, last8), 0)),
                  _full(lng.shape), _full(lnb.shape), _full(wsp.shape), _full(bspb.shape), _full(cw.shape), ANY],
        out_specs=[_tok(tm, inw), _full((1, aw)), _full((1, aw)), _full(wsp.shape),
                   _full((A_GROUPS, CHUNK, gd)), _full(cw.shape)],
        scratch=[pltpu.VMEM((gath.shape[1] * NDEV, dm), BF16), pltpu.SemaphoreType.DMA((NDEV,))])


def _ff_chunks(f, width=1024):
    return [(c0, min(c0 + width, f)) for c0 in range(0, f, width)]


def _ffn_up(x, gain, gath_g, gath_u, name, tm):
    t, dm = x.shape
    f = gath_g.shape[1] * NDEV

    def body(x_ref, g_ref, gg_ref, gu_ref, gate_ref, up_ref, act_ref, wg, wu, sems):
        @pl.when(pl.program_id(0) == 0)
        def _():
            _load_weight(gg_ref, wg, sems)
            _load_weight(gu_ref, wu, sems)
        h, _ = _rms_fwd(x_ref[...], g_ref[...])
        hb = h.astype(BF16)
        for c0, c1 in _ff_chunks(f):
            gate = _dot(hb, wg[c0:c1, :], NT)
            up = _dot(hb, wu[c0:c1, :], NT)
            gate_ref[:, c0:c1] = gate.astype(BF16)
            up_ref[:, c0:c1] = up.astype(BF16)
            act_ref[:, c0:c1] = (gate * _sigmoid(gate) * up).astype(BF16)

    o = jax.ShapeDtypeStruct((t, f), BF16)
    return dict(
        body=body, grid=(t // tm,), name=name, args=[x, gain, gath_g, gath_u], out_shape=[o, o, o],
        in_specs=[_tok(tm, dm), _full((1, dm)), ANY, ANY], out_specs=[_tok(tm, f)] * 3,
        scratch=[pltpu.VMEM((f, dm), BF16), pltpu.VMEM((f, dm), BF16), pltpu.SemaphoreType.DMA((NDEV,))])


def _ffn_down(x, act, gath_d, name, tm):
    t, dm = x.shape
    f = act.shape[1]

    def body(x_ref, a_ref, gd_ref, xo_ref, wd, sems):
        @pl.when(pl.program_id(0) == 0)
        def _():
            _load_weight(gd_ref, wd, sems)
        xo_ref[...] = x_ref[...] + _dot(a_ref[...], wd[...], NN)

    return dict(
        body=body, grid=(t // tm,), name=name, args=[x, act, gath_d], out_shape=[jax.ShapeDtypeStruct((t, dm), F32)],
        in_specs=[_tok(tm, dm), _tok(tm, f), ANY], out_specs=[_tok(tm, dm)],
        scratch=[pltpu.VMEM((f, dm), BF16), pltpu.SemaphoreType.DMA((NDEV,))])


def _ffn_down_loss(x, act, gath_d, target, gain, name, tm):
    t, dm = x.shape
    f = act.shape[1]
    steps = t // tm

    def body(x_ref, a_ref, gd_ref, t_ref, g_ref, loss_ref, dx_ref, dxb_ref, dgain_ref, wd, acc, sems):
        i = pl.program_id(0)

        @pl.when(i == 0)
        def _():
            _load_weight(gd_ref, wd, sems)
            acc[...] = jnp.zeros_like(acc)
            dgain_ref[...] = jnp.zeros_like(dgain_ref)
        xv = x_ref[...] + _dot(a_ref[...], wd[...], NN)
        gain_v = g_ref[...]
        y, r = _rms_fwd(xv, gain_v)
        e = y - t_ref[...]
        acc[...] += jnp.sum(e * e, axis=0, keepdims=True)
        dx, dgain = _rms_bwd(e * (1.0 / dm), xv, r, gain_v)
        dx_ref[...] = dx
        dxb_ref[...] = dx.astype(BF16)
        dgain_ref[...] += dgain

        @pl.when(i == steps - 1)
        def _():
            loss_ref[...] = jnp.sum(acc[...], axis=-1, keepdims=True) * (0.5 / dm)

    return dict(
        body=body, grid=(steps,), name=name, args=[x, act, gath_d, target, gain],
        out_shape=[jax.ShapeDtypeStruct((1, 1), F32), jax.ShapeDtypeStruct((t, dm), F32),
                   jax.ShapeDtypeStruct((t, dm), BF16), jax.ShapeDtypeStruct((1, dm), F32)],
        in_specs=[_tok(tm, dm), _tok(tm, f), ANY, _tok(tm, dm), _full((1, dm))],
        out_specs=[_full((1, 1)), _tok(tm, dm), _tok(tm, dm), _full((1, dm))],
        scratch=[pltpu.VMEM((f, dm), BF16), pltpu.VMEM((1, dm), F32), pltpu.SemaphoreType.DMA((NDEV,))])


def _ffn_loss(x, gain, gath_g, gath_u, gath_d, target, final_gain, name, tm):
    t, dm = x.shape
    f = gath_g.shape[1] * NDEV
    steps = t // tm

    def body(x_ref, g_ref, gg_ref, gu_ref, gd_ref, t_ref, fg_ref,
             gate_ref, up_ref, act_ref, loss_ref, dx_ref, dxb_ref, dgain_ref, wg, wu, wd, acc, sems):
        i = pl.program_id(0)

        @pl.when(i == 0)
        def _():
            _load_weight(gg_ref, wg, sems)
            _load_weight(gu_ref, wu, sems)
            _load_weight(gd_ref, wd, sems)
            acc[...] = jnp.zeros_like(acc)
            dgain_ref[...] = jnp.zeros_like(dgain_ref)
        xv = x_ref[...]
        h, _ = _rms_fwd(xv, g_ref[...])
        hb = h.astype(BF16)
        out = xv
        for c0, c1 in _ff_chunks(f):
            gate = _dot(hb, wg[c0:c1, :], NT)
            up = _dot(hb, wu[c0:c1, :], NT)
            gate_ref[:, c0:c1] = gate.astype(BF16)
            up_ref[:, c0:c1] = up.astype(BF16)
            act = (gate * _sigmoid(gate) * up).astype(BF16)
            act_ref[:, c0:c1] = act
            out = out + _dot(act, wd[c0:c1, :], NN)
        fg = fg_ref[...]
        y, r = _rms_fwd(out, fg)
        e = y - t_ref[...]
        acc[...] += jnp.sum(e * e, axis=0, keepdims=True)
        dx, dgain = _rms_bwd(e * (1.0 / dm), out, r, fg)
        dx_ref[...] = dx
        dxb_ref[...] = dx.astype(BF16)
        dgain_ref[...] += dgain

        @pl.when(i == steps - 1)
        def _():
            loss_ref[...] = jnp.sum(acc[...], axis=-1, keepdims=True) * (0.5 / dm)

    o = jax.ShapeDtypeStruct((t, f), BF16)
    return dict(
        body=body, grid=(steps,), name=name, args=[x, gain, gath_g, gath_u, gath_d, target, final_gain],
        out_shape=[o, o, o, jax.ShapeDtypeStruct((1, 1), F32), jax.ShapeDtypeStruct((t, dm), F32),
                   jax.ShapeDtypeStruct((t, dm), BF16), jax.ShapeDtypeStruct((1, dm), F32)],
        in_specs=[_tok(tm, dm), _full((1, dm)), ANY, ANY, ANY, _tok(tm, dm), _full((1, dm))],
        out_specs=[_tok(tm, f)] * 3 + [_full((1, 1)), _tok(tm, dm), _tok(tm, dm), _full((1, dm))],
        scratch=[pltpu.VMEM((f, dm), BF16), pltpu.VMEM((f, dm), BF16), pltpu.VMEM((f, dm), BF16),
                 pltpu.VMEM((1, dm), F32), pltpu.SemaphoreType.DMA((NDEV,))])


def _ffn_bwd(dxo, x, gate, up, gain, gath_g, gath_u, gath_d, name, tm):
    t, dm = x.shape
    f = gate.shape[1]

    def body(dxo_ref, x_ref, gate_ref, up_ref, g_ref, gg_ref, gu_ref, gd_ref,
             dx_ref, dxb_ref, dg_ref, du_ref, hb_ref, dgain_ref, wg, wu, wd, sems):
        @pl.when(pl.program_id(0) == 0)
        def _():
            _load_weight(gg_ref, wg, sems)
            _load_weight(gu_ref, wu, sems)
            _load_weight(gd_ref, wd, sems)
            dgain_ref[...] = jnp.zeros_like(dgain_ref)
        xv, gain_v, dxo_v = x_ref[...], g_ref[...], dxo_ref[...]
        h, r = _rms_fwd(xv, gain_v)
        hb_ref[...] = h.astype(BF16)
        dxob = dxo_v.astype(BF16)
        dh = jnp.zeros_like(xv)
        for c0, c1 in _ff_chunks(f):
            gate_v = gate_ref[:, c0:c1].astype(F32)
            up_v = up_ref[:, c0:c1].astype(F32)
            s = _sigmoid(gate_v)
            silu = gate_v * s
            dact = _dot(dxob, wd[c0:c1, :], NT)
            dg = (dact * up_v * (s * (1.0 + gate_v * (1.0 - s)))).astype(BF16)
            du = (dact * silu).astype(BF16)
            dg_ref[:, c0:c1] = dg
            du_ref[:, c0:c1] = du
            dh = dh + _dot(dg, wg[c0:c1, :], NN) + _dot(du, wu[c0:c1, :], NN)
        dx, dgain = _rms_bwd(dh, xv, r, gain_v)
        dx = dxo_v + dx
        dx_ref[...] = dx
        dxb_ref[...] = dx.astype(BF16)
        dgain_ref[...] += dgain

    return dict(
        body=body, grid=(t // tm,), name=name, args=[dxo, x, gate, up, gain, gath_g, gath_u, gath_d],
        out_shape=[jax.ShapeDtypeStruct((t, dm), F32), jax.ShapeDtypeStruct((t, dm), BF16),
                   jax.ShapeDtypeStruct((t, f), BF16), jax.ShapeDtypeStruct((t, f), BF16),
                   jax.ShapeDtypeStruct((t, dm), BF16), jax.ShapeDtypeStruct((1, dm), F32)],
        in_specs=[_tok(tm, dm), _tok(tm, dm), _tok(tm, f), _tok(tm, f), _full((1, dm)), ANY, ANY, ANY],
        out_specs=[_tok(tm, dm), _tok(tm, dm), _tok(tm, f), _tok(tm, f), _tok(tm, dm), _full((1, dm))],
        scratch=[pltpu.VMEM((f, dm), BF16), pltpu.VMEM((f, dm), BF16), pltpu.VMEM((f, dm), BF16),
                 pltpu.SemaphoreType.DMA((NDEV,))])


def _t5_buckets(rel):
    nb = N_BUCKETS // 2
    ret = jnp.where(rel > 0, nb, 0)
    n = jnp.abs(rel)
    max_exact = nb // 2
    nf = jnp.maximum(n, 1).astype(jnp.float32)
    large = max_exact + (jnp.log(nf / max_exact) / math.log(MAX_DISTANCE / max_exact)
                         * (nb - max_exact)).astype(jnp.int32)
    large = jnp.minimum(large, nb - 1)
    return ret + jnp.where(n < max_exact, n, large)


def _bucket_table():
    qi = jnp.arange(CHUNK, dtype=jnp.int32)[:, None]
    kj = jnp.arange(3 * CHUNK, dtype=jnp.int32)[None, :]
    rel = kj - CHUNK - qi
    return jnp.where(jnp.abs(rel) <= CHUNK, _t5_buckets(rel), -1)


def _bias_table(rel_bias_t, buckets):
    nh = rel_bias_t.shape[0]

    def body(rb_ref, bk_ref, o_ref):
        bk = bk_ref[...]
        for h in range(nh):
            acc = jnp.where(bk < 0, NEG, 0.0).astype(F32)
            for b in range(N_BUCKETS):
                acc = jnp.where(bk == b, rb_ref[h, b] * LOG2E, acc)
            o_ref[h] = acc

    return dict(
        body=body, grid=(1,), name="bias_table", args=[rel_bias_t, buckets],
        out_shape=[jax.ShapeDtypeStruct((nh,) + buckets.shape, F32)],
        in_specs=[pl.BlockSpec(memory_space=pltpu.SMEM), _full(buckets.shape)],
        out_specs=[_full((nh,) + buckets.shape)])


def _rel_bias_grad(dbias, buckets):
    nh = dbias.shape[0]

    def body(db_ref, bk_ref, o_ref):
        bk = bk_ref[...]
        lane = lax.broadcasted_iota(jnp.int32, (1, 128), 1)
        for h in range(nh):
            d = db_ref[h]
            row = jnp.zeros((1, 128), F32)
            for b in range(N_BUCKETS):
                s = jnp.sum(jnp.sum(jnp.where(bk == b, d, 0.0), axis=1, keepdims=True), axis=0, keepdims=True)
                row = jnp.where(lane == b, s, row)
            o_ref[h:h + 1, :] = row

    return dict(
        body=body, grid=(1,), name="rel_bias_grad", args=[dbias, buckets],
        out_shape=[jax.ShapeDtypeStruct((nh, 128), F32)],
        in_specs=[_full(dbias.shape), _full(buckets.shape)], out_specs=[_full((nh, 128))])


def _half_masks():
    lane = lax.broadcasted_iota(jnp.int32, (CHUNK, 128), 1)
    return lane < HEAD_DIM, lane >= HEAD_DIM


def _kv_low(ref, starts, hk, lo):
    kt = (hk // 2) * 128
    out = []
    for jj in range(3):
        blk = ref[pl.ds(starts[jj], CHUNK), kt:kt + 128]
        if hk % 2 == 1:
            blk = pltpu.roll(blk, HEAD_DIM, 1)
        out.append(jnp.where(lo, blk, jnp.zeros_like(blk)))
    return out


def _stack_heads(tile_a, tile_b):
    return jnp.concatenate([tile_a, pltpu.roll(tile_a, HEAD_DIM, 1), tile_b, pltpu.roll(tile_b, HEAD_DIM, 1)], axis=0)


def _unstack_heads(o4):
    return (o4[0:CHUNK] + pltpu.roll(o4[CHUNK:2 * CHUNK], HEAD_DIM, 1),
            o4[2 * CHUNK:3 * CHUNK] + pltpu.roll(o4[3 * CHUNK:], HEAD_DIM, 1))


ATT_SLAB = 32


def _softmax_slab(s_scr, hk, g, r0, bias_ref, sink_ref, n, nblk):
    scale = HEAD_DIM ** -0.5 * LOG2E
    h = (N_HEADS // N_KV) * hk + g
    s = []
    for jj in range(3):
        sj = (s_scr[hk, jj, pl.ds(g * CHUNK + r0, ATT_SLAB), :] * scale
              + bias_ref[h, pl.ds(r0, ATT_SLAB), jj * CHUNK:(jj + 1) * CHUNK])
        if jj == 0:
            sj = jnp.where(n > 0, sj, NEG)
        if jj == 2:
            sj = jnp.where(n < nblk - 1, sj, NEG)
        s.append(sj)
    sink = sink_ref[h] * LOG2E
    m = jnp.maximum(jnp.max(jnp.maximum(jnp.maximum(s[0], s[1]), s[2]), axis=-1, keepdims=True), sink)
    e = [jnp.exp2(sj - m) for sj in s]
    es = jnp.exp2(sink - m)
    inv = 1.0 / (jnp.sum(e[0] + e[1] + e[2], axis=-1, keepdims=True) + es)
    return [ej * inv for ej in e], es * inv


def _key_block_starts(n, nblk):
    return [pl.multiple_of(jnp.clip(n - 1 + jj, 0, nblk - 1) * CHUNK, CHUNK) for jj in range(3)]


def _attn_fwd(qkv, x2, bias, sink, gath):
    t, dm = x2.shape
    nblk = t // CHUNK
    kvw = N_KV * HEAD_DIM
    kcb, vcb = dm // kvw, dm // kvw + 1
    slab = (N_KV, 3, 4 * CHUNK, CHUNK)

    def body(q_ref, k_ref, v_ref, x2_ref, bias_ref, sink_ref, gath_ref, x3_ref, att_ref, p_ref, ps_ref,
             wbuf, s_scr, sems):
        n = pl.program_id(0)

        @pl.when(n == 0)
        def _():
            _load_weight(gath_ref, wbuf, sems)
        lo, _ = _half_masks()
        lane_s = lax.broadcasted_iota(jnp.int32, (ATT_SLAB, 128), 1)
        starts = _key_block_starts(n, nblk)
        tiles = []
        for hk in range(N_KV):
            c0 = (2 * hk) * 128
            k_lo = _kv_low(k_ref, starts, hk, lo)
            v_lo = _kv_low(v_ref, starts, hk, lo)
            q4 = _stack_heads(q_ref[:, c0:c0 + 128], q_ref[:, c0 + 128:c0 + 256])
            for jj in range(3):
                s_scr[hk, jj] = _dot(q4, k_lo[jj], NT)
            for g in range(4):
                h = 4 * hk + g
                for r0 in range(0, CHUNK, ATT_SLAB):
                    p, ps = _softmax_slab(s_scr, hk, g, r0, bias_ref, sink_ref, n, nblk)
                    for jj in range(3):
                        p_ref[hk, jj, g * CHUNK + r0:g * CHUNK + r0 + ATT_SLAB, :] = p[jj].astype(BF16)
                    rest = jnp.zeros((ATT_SLAB, 128), F32) if h == 0 else ps_ref[r0:r0 + ATT_SLAB, :]
                    ps_ref[r0:r0 + ATT_SLAB, :] = jnp.where(lane_s == h, ps, rest)
            o4 = _dot(p_ref[hk, 0], v_lo[0], NN) + _dot(p_ref[hk, 1], v_lo[1], NN) + _dot(p_ref[hk, 2], v_lo[2], NN)
            tiles += list(_unstack_heads(o4))
        att = jnp.concatenate(tiles, axis=1).astype(BF16)
        att_ref[...] = att
        x3_ref[...] = x2_ref[...] + _dot(att, wbuf[...], NN)

    blk = pl.BlockSpec((CHUNK, dm), lambda n: (n, 0))
    return dict(
        body=body, grid=(nblk,), name="attn_fwd", args=[qkv, qkv, qkv, x2, bias, sink, gath],
        out_shape=[jax.ShapeDtypeStruct((t, dm), F32), jax.ShapeDtypeStruct((t, dm), BF16),
                   jax.ShapeDtypeStruct((nblk,) + slab, BF16), jax.ShapeDtypeStruct((t, 128), F32)],
        in_specs=[blk, pl.BlockSpec((t, kvw), lambda n: (0, kcb)), pl.BlockSpec((t, kvw), lambda n: (0, vcb)), blk,
                  _full(bias.shape), pl.BlockSpec(memory_space=pltpu.SMEM), ANY],
        out_specs=[blk, blk, pl.BlockSpec((None,) + slab, lambda n: (n, 0, 0, 0, 0)),
                   pl.BlockSpec((CHUNK, 128), lambda n: (n, 0))],
        scratch=[pltpu.VMEM((gath.shape[1] * NDEV, dm), BF16), pltpu.VMEM(slab, F32),
                 pltpu.SemaphoreType.DMA((NDEV,))])


def _attn_bwd(qkv, att, probs, sink_probs, dx3, bias_shape, gath):
    t, dm = dx3.shape
    nblk = t // CHUNK
    kvw = N_KV * HEAD_DIM
    kcb, vcb = dm // kvw, dm // kvw + 1
    scale = HEAD_DIM ** -0.5
    slab = (N_KV, 3, 4 * CHUNK, CHUNK)

    def body(q_ref, k_ref, v_ref, att_ref, p_ref, ps_ref, dx_ref, gath_ref,
             dq_ref, dkb_ref, dvb_ref, dbias_ref, dsink_ref,
             wbuf, dp_scr, ds_scr, prod_scr, dsum_scr, dk_ref, dv_ref, sems):
        n = pl.program_id(0)

        @pl.when(n == 0)
        def _():
            _load_weight(gath_ref, wbuf, sems)
            dk_ref[...] = jnp.zeros_like(dk_ref)
            dv_ref[...] = jnp.zeros_like(dv_ref)
            dbias_ref[...] = jnp.zeros_like(dbias_ref)
            dsink_ref[...] = jnp.zeros_like(dsink_ref)
        lo, hi = _half_masks()
        lane_s = lax.broadcasted_iota(jnp.int32, (ATT_SLAB, 128), 1)
        starts = _key_block_starts(n, nblk)
        dout = _dot(dx_ref[...].astype(BF16), wbuf[...], NT)
        prod_scr[...] = dout * att_ref[...].astype(F32)
        doutb = dout.astype(BF16)
        dq_tiles = []
        for hk in range(N_KV):
            kt = (hk // 2) * 128
            c0 = (2 * hk) * 128
            k_lo = _kv_low(k_ref, starts, hk, lo)
            v_lo = _kv_low(v_ref, starts, hk, lo)
            q4 = _stack_heads(q_ref[:, c0:c0 + 128], q_ref[:, c0 + 128:c0 + 256])
            do4 = _stack_heads(doutb[:, c0:c0 + 128], doutb[:, c0 + 128:c0 + 256])
            for jj in range(3):
                dp_scr[hk, jj] = _dot(do4, v_lo[jj], NT)
            for g in range(4):
                h = 4 * hk + g
                for r0 in range(0, CHUNK, ATT_SLAB):
                    rows = slice(g * CHUNK + r0, g * CHUNK + r0 + ATT_SLAB)
                    pt = prod_scr[r0:r0 + ATT_SLAB, c0 + (g // 2) * 128:c0 + (g // 2 + 1) * 128]
                    msk = lane_s < HEAD_DIM if g % 2 == 0 else lane_s >= HEAD_DIM
                    dsum = jnp.sum(jnp.where(msk, pt, 0.0), axis=-1, keepdims=True)
                    rest = jnp.zeros((ATT_SLAB, 128), F32) if h == 0 else dsum_scr[r0:r0 + ATT_SLAB, :]
                    dsum_scr[r0:r0 + ATT_SLAB, :] = jnp.where(lane_s == h, dsum, rest)
                    for jj in range(3):
                        ds = p_ref[hk, jj, rows, :].astype(F32) * (dp_scr[hk, jj, rows, :] - dsum)
                        dbias_ref[h, r0:r0 + ATT_SLAB, jj * CHUNK:(jj + 1) * CHUNK] += ds
                        ds_scr[hk, jj, rows, :] = ds.astype(BF16)
            dq4 = jnp.zeros((4 * CHUNK, 128), F32)
            for jj in range(3):
                ds4 = ds_scr[hk, jj]
                dq4 = dq4 + _dot(ds4, k_lo[jj], NN) * scale
                dkj = _dot(ds4, q4, TN) * scale
                dvj = _dot(p_ref[hk, jj], do4, TN)
                if hk % 2 == 1:
                    dkj, dvj = pltpu.roll(dkj, HEAD_DIM, 1), pltpu.roll(dvj, HEAD_DIM, 1)
                keep = lo if hk % 2 == 0 else hi
                dk_ref[pl.ds(starts[jj], CHUNK), kt:kt + 128] += jnp.where(keep, dkj, 0.0)
                dv_ref[pl.ds(starts[jj], CHUNK), kt:kt + 128] += jnp.where(keep, dvj, 0.0)
            dq_tiles += list(_unstack_heads(dq4))
        dq_ref[...] = jnp.concatenate(dq_tiles, axis=1).astype(BF16)
        dsink_ref[...] -= jnp.sum(ps_ref[...] * dsum_scr[...], axis=0, keepdims=True)

        @pl.when(n == nblk - 1)
        def _():
            dkb_ref[...] = dk_ref[...].astype(BF16)
            dvb_ref[...] = dv_ref[...].astype(BF16)

    blk = pl.BlockSpec((CHUNK, dm), lambda n: (n, 0))
    return dict(
        body=body, grid=(nblk,), name="attn_bwd", args=[qkv, qkv, qkv, att, probs, sink_probs, dx3, gath],
        out_shape=[jax.ShapeDtypeStruct((t, dm), BF16), jax.ShapeDtypeStruct((t, kvw), BF16),
                   jax.ShapeDtypeStruct((t, kvw), BF16), jax.ShapeDtypeStruct(bias_shape, F32),
                   jax.ShapeDtypeStruct((1, 128), F32)],
        in_specs=[blk, pl.BlockSpec((t, kvw), lambda n: (0, kcb)), pl.BlockSpec((t, kvw), lambda n: (0, vcb)),
                  blk, pl.BlockSpec((None,) + slab, lambda n: (n, 0, 0, 0, 0)),
                  pl.BlockSpec((CHUNK, 128), lambda n: (n, 0)), blk, ANY],
        out_specs=[blk, _full((t, kvw)), _full((t, kvw)), _full(bias_shape), _full((1, 128))],
        scratch=[pltpu.VMEM((gath.shape[1] * NDEV, dm), BF16), pltpu.VMEM(slab, F32), pltpu.VMEM(slab, BF16),
                 pltpu.VMEM((CHUNK, dm), F32), pltpu.VMEM((CHUNK, 128), F32),
                 pltpu.VMEM((t, kvw), F32), pltpu.VMEM((t, kvw), F32), pltpu.SemaphoreType.DMA((NDEV,))])


def _finish_weight(recvs, w, m, v, name):
    nl, r, dm = w.shape
    assert nl == len(recvs) and all(rc.shape[1:] == (r, dm) for rc in recvs)
    td = dm // 2
    wspec = pl.BlockSpec((None, r, td), lambda l, j: (l, 0, j))

    def body(*refs):
        r_refs = refs[:nl]
        w_ref, m_ref, v_ref, g_ref, d_ref, nm_ref, nv_ref = refs[nl:]
        layer = pl.program_id(0)
        for li in range(nl):
            @pl.when(layer == li)
            def _():
                g = r_refs[li][0].astype(F32)
                for d in range(1, recvs[li].shape[0]):
                    g = g + r_refs[li][d].astype(F32)
                delta, nm, nv = _adamw_math(w_ref[...], g, m_ref[...], v_ref[...])
                g_ref[...] = g
                d_ref[...] = delta
                nm_ref[...] = nm
                nv_ref[...] = nv

    o = jax.ShapeDtypeStruct(w.shape, F32)
    return dict(
        body=body, grid=(nl, 2), name=name, args=[*recvs, w, m, v], out_shape=[o, o, o, o],
        in_specs=[pl.BlockSpec((rc.shape[0], r, td), lambda l, j: (0, 0, j)) for rc in recvs] + [wspec] * 3,
        out_specs=[wspec] * 4)


def _adamw_small(ws, ms, vs, slots, lates, loss_slots, name):
    n = len(ws)
    nl = len(lates)

    def total(ref):
        acc = ref[0].astype(F32)
        for d in range(1, NDEV):
            acc = acc + ref[d].astype(F32)
        return acc

    def body(*refs):
        ins, outs = refs[:4 * n + nl + 1], refs[4 * n + nl + 1:]
        for i in range(n):
            w_ref, m_ref, v_ref, s_ref = ins[4 * i:4 * i + 4]
            g_ref, d_ref, nm_ref, nv_ref = outs[4 * i:4 * i + 4]
            g_ref[...] = total(s_ref)
            for k, (at, late) in enumerate(lates):
                if at == i:
                    g_ref[0:late.shape[1], :] = total(ins[4 * n + k])
            d_ref[...], nm_ref[...], nv_ref[...] = _adamw_math(w_ref[...], g_ref[...], m_ref[...], v_ref[...])
        outs[4 * n][...] = total(ins[4 * n + nl])

    args, out_shape = [], []
    for w, m, v, s in zip(ws, ms, vs, slots):
        args += [w, m, v, s]
        out_shape += [jax.ShapeDtypeStruct(w.shape, F32)] * 4
    args += [late for _, late in lates] + [loss_slots]
    out_shape.append(jax.ShapeDtypeStruct((1, 1), F32))
    out = pl.pallas_call(
        body, grid=(1,), out_shape=tuple(out_shape), in_specs=[_full(a.shape) for a in args],
        out_specs=tuple(_full(o.shape) for o in out_shape), compiler_params=_cp(), name=name)(*args)
    return [tuple(out[4 * i:4 * i + 4]) for i in range(n)], out[4 * n]


def kernel(x, norm_mix, norm_ffn, even_w_in, even_v_ln_g, even_v_ln_b, even_w_spatial, even_b_spatial, even_conv_w, even_w_out, attn_w_qkv, attn_sink, rel_bias, attn_w_out, ffn_w_gate, ffn_w_up, ffn_w_down, final_norm, loss_target, m_norm_mix, m_norm_ffn, m_even_w_in, m_even_v_ln_g, m_even_v_ln_b, m_even_w_spatial, m_even_b_spatial, m_even_conv_w, m_even_w_out, m_attn_w_qkv, m_attn_sink, m_rel_bias, m_attn_w_out, m_ffn_w_gate, m_ffn_w_up, m_ffn_w_down, m_final_norm, v_norm_mix, v_norm_ffn, v_even_w_in, v_even_v_ln_g, v_even_v_ln_b, v_even_w_spatial, v_even_b_spatial, v_even_conv_w, v_even_w_out, v_attn_w_qkv, v_attn_sink, v_rel_bias, v_attn_w_out, v_ffn_w_gate, v_ffn_w_up, v_ffn_w_down, v_final_norm):
    t, dm = x.shape[1], x.shape[2]
    aw = even_v_ln_g.shape[1]
    bw = even_conv_w.shape[2] * NDEV
    gd = aw // A_GROUPS
    tm = min(512, t // 2)
    tmf = min(256, t // 2)
    me = _my_index()
    row = lambda a: a.reshape(1, -1)

    colT = lambda w: w.T.astype(BF16)
    sh = dict(winT=colT(even_w_in[0]), wqkvT=colT(attn_w_qkv[0]), wgT0=colT(ffn_w_gate[0]), wuT0=colT(ffn_w_up[0]),
              wgT1=colT(ffn_w_gate[1]), wuT1=colT(ffn_w_up[1]), woe=even_w_out[0].astype(BF16),
              woa=attn_w_out[0].astype(BF16), wd0=ffn_w_down[0].astype(BF16), wd1=ffn_w_down[1].astype(BF16))
    gather = lambda names: _GatherCarry([sh[n] for n in names])

    in_full = lambda a: lax.dynamic_update_slice(jnp.zeros((3, bw), F32), a[0], (0, me * (bw // NDEV)))

    x0 = x[0]
    wsp_b = even_w_spatial[0].astype(BF16)
    bspb = jnp.broadcast_to(even_b_spatial[0][:, :, None], (A_GROUPS, CHUNK, gd))
    buckets = _bucket_table()
    sink = attn_sink[0]

    (bias,), ((g_winT,), (cw_slots,)) = _call(
        _bias_table(rel_bias.T, buckets), [gather(["winT"]), _BroadcastCarry([in_full(even_conv_w)])])
    cw_full = jnp.sum(cw_slots, axis=0)
    (proj, h0b), (g_woe, g_wgT0) = _call(_norm_proj(x0, row(norm_mix[0]), g_winT, F32, "in_proj", tm),
                                         gather(["woe", "wgT0"]))
    (x1, yb), (g_wuT0,) = _call(_even_core_fwd(proj, x0, even_v_ln_g, even_v_ln_b, wsp_b, bspb, cw_full, g_woe, tm),
                                gather(["wuT0"]))
    (gate0, up0, act0), (g_wd0,) = _call(_ffn_up(x1, row(norm_ffn[0]), g_wgT0, g_wuT0, "ffn_up0", tmf), gather(["wd0"]))
    (x2,), (g_wqkvT,) = _call(_ffn_down(x1, act0, g_wd0, "ffn_down0", tm), gather(["wqkvT"]))
    (qkv, h2b), (g_woa,) = _call(_norm_proj(x2, row(norm_mix[1]), g_wqkvT, BF16, "qkv_proj", tm), gather(["woa"]))
    (x3, attb, probs, sink_probs), (g_wgT1, g_wuT1, g_wd1) = _call(
        _attn_fwd(qkv, x2, bias, sink, g_woa), gather(["wgT1", "wuT1", "wd1"]))
    (gate1, up1, act1, loss_part, dx4, dx4b, d_final), _ = _call(
        _ffn_loss(x3, row(norm_ffn[1]), g_wgT1, g_wuT1, g_wd1, loss_target[0], row(final_norm), "ffn1_loss", tmf))

    (dx3, dx3b, dg1, du1, h3b, d_nffn1), _ = _call(
        _ffn_bwd(dx4, x3, gate1, up1, row(norm_ffn[1]), g_wgT1, g_wuT1, g_wd1, "ffn_bwd1", tmf))
    (p_wgT1, p_wuT1), _ = _call(_wgrad_pair(dg1, du1, h3b, "wgrad_gate_up1"))
    (p_wd1,), ((a_wgT1,), (a_wuT1,)) = _call(
        _wgrad(act1, dx4b, "wgrad_down1"), [_PairCarry(p_wgT1), _PairCarry(p_wuT1)])
    (dq, dk, dv, dbias, dsink), ((r_wgT1,), (a_wd1,)) = _call(
        _attn_bwd(qkv, attb, probs, sink_probs, dx3, bias.shape, g_woa),
        [_ChipSumCarry(p_wgT1, a_wgT1), _PairCarry(p_wd1)])
    (p_woa,), _ = _call(_wgrad(attb, dx3b, "wgrad_attn_out"))
    (dx2, dx2b, d_nmix1), (r_wuT1,) = _call(
        _proj_bwd_norm([dq, dk, dv], x2, row(norm_mix[1]), dx3, g_wqkvT, "qkv_bwd", tm),
        _ChipSumCarry(p_wuT1, a_wuT1))
    (p_wqkvT,), _ = _call(_wgrad([dq, dk, dv], h2b, "wgrad_qkv"))
    (dx1, dx1b, dg0, du0, h1b, d_nffn0), ((r_wd1,), (r_woa, r_wqkvT)) = _call(
        _ffn_bwd(dx2, x1, gate0, up0, row(norm_ffn[0]), g_wgT0, g_wuT0, g_wd0, "ffn_bwd0", tmf),
        [_ChipSumCarry(p_wd1, a_wd1), _GradCarry([p_woa, p_wqkvT])])
    (p_woe,), _ = _call(_wgrad(yb, dx1b, "wgrad_even_out"))
    (p_wgT0,), _ = _call(_wgrad(dg0, h1b, "wgrad_gate0"))
    (p_wuT0,), ((a_wgT0,), (r_woe,)) = _call(
        _wgrad(du0, h1b, "wgrad_up0"), [_PairCarry(p_wgT0), _GradCarry([p_woe])])
    (p_wd0,), ((r_wgT0,), (a_wuT0,)) = _call(
        _wgrad(act0, dx2b, "wgrad_down0"), [_ChipSumCarry(p_wgT0, a_wgT0), _PairCarry(p_wuT0)])
    (dproj, d_lng, d_lnb, d_wsp, d_bsp3, d_cw), ((r_wuT0,), (a_wd0,)) = _call(
        _even_core_bwd(proj, dx1, even_v_ln_g, even_v_ln_b, wsp_b, bspb, cw_full, g_woe, tm),
        [_ChipSumCarry(p_wuT0, a_wuT0), _PairCarry(p_wd0)])
    small_names = ["norm_mix", "norm_ffn", "even_v_ln_g", "even_v_ln_b", "even_w_spatial", "even_b_spatial",
                   "even_conv_w", "attn_sink", "rel_bias", "final_norm"]
    wsp_at = small_names.index("even_w_spatial")
    small_parts = [jnp.concatenate([jnp.zeros_like(d_nmix1), d_nmix1]), jnp.concatenate([d_nffn0, d_nffn1]),
                   d_lng, d_lnb, jnp.sum(d_bsp3, axis=-1)[None], d_cw,
                   dsink[:, 0:N_HEADS], jnp.zeros_like(rel_bias), d_final, loss_part]
    (p_winT,), (r_wd0,) = _call(_wgrad(dproj, h0b, "wgrad_in"), _ChipSumCarry(p_wd0, a_wd0))
    (d_relb_t,), ((a_winT,), rest_slots) = _call(
        _rel_bias_grad(dbias, buckets), [_PairCarry(p_winT), _BroadcastCarry(small_parts)])
    d_relb = d_relb_t[:, 0:N_BUCKETS].T
    (dx0, _, d_nmix0), ((r_winT,), (wsp_slots,)) = _call(
        _proj_bwd_norm([dproj], x0, row(norm_mix[0]), dx1, g_winT, "in_proj_bwd", tm),
        [_ChipSumCarry(p_winT, a_winT), _BroadcastCarry([d_wsp[None].astype(BF16)])])
    small_slots = list(rest_slots[:wsp_at]) + [wsp_slots] + list(rest_slots[wsp_at:])

    grads = {}

    order = ["norm_mix", "norm_ffn", "even_w_in", "even_v_ln_g", "even_v_ln_b", "even_w_spatial", "even_b_spatial",
             "even_conv_w", "even_w_out", "attn_w_qkv", "attn_sink", "rel_bias", "attn_w_out", "ffn_w_gate",
             "ffn_w_up", "ffn_w_down", "final_norm"]
    ws = dict(norm_mix=norm_mix, norm_ffn=norm_ffn, even_w_in=even_w_in, even_v_ln_g=even_v_ln_g,
              even_v_ln_b=even_v_ln_b, even_w_spatial=even_w_spatial, even_b_spatial=even_b_spatial,
              even_conv_w=even_conv_w, even_w_out=even_w_out, attn_w_qkv=attn_w_qkv, attn_sink=attn_sink,
              rel_bias=rel_bias, attn_w_out=attn_w_out, ffn_w_gate=ffn_w_gate, ffn_w_up=ffn_w_up,
              ffn_w_down=ffn_w_down, final_norm=final_norm)
    ms = dict(norm_mix=m_norm_mix, norm_ffn=m_norm_ffn, even_w_in=m_even_w_in, even_v_ln_g=m_even_v_ln_g,
              even_v_ln_b=m_even_v_ln_b, even_w_spatial=m_even_w_spatial, even_b_spatial=m_even_b_spatial,
              even_conv_w=m_even_conv_w, even_w_out=m_even_w_out, attn_w_qkv=m_attn_w_qkv, attn_sink=m_attn_sink,
              rel_bias=m_rel_bias, attn_w_out=m_attn_w_out, ffn_w_gate=m_ffn_w_gate, ffn_w_up=m_ffn_w_up,
              ffn_w_down=m_ffn_w_down, final_norm=m_final_norm)
    vs = dict(norm_mix=v_norm_mix, norm_ffn=v_norm_ffn, even_w_in=v_even_w_in, even_v_ln_g=v_even_v_ln_g,
              even_v_ln_b=v_even_v_ln_b, even_w_spatial=v_even_w_spatial, even_b_spatial=v_even_b_spatial,
              even_conv_w=v_even_conv_w, even_w_out=v_even_w_out, attn_w_qkv=v_attn_w_qkv, attn_sink=v_attn_sink,
              rel_bias=v_rel_bias, attn_w_out=v_attn_w_out, ffn_w_gate=v_ffn_w_gate, ffn_w_up=v_ffn_w_up,
              ffn_w_down=v_ffn_w_down, final_norm=v_final_norm)
    big = dict(ffn_w_gate=([r_wgT0, r_wgT1], True), even_w_in=([r_winT], True), even_w_out=([r_woe], False),
               attn_w_qkv=([r_wqkvT], True), attn_w_out=([r_woa], False), ffn_w_up=([r_wuT0, r_wuT1], True),
               ffn_w_down=([r_wd0, r_wd1], False))
    delta, new_m, new_v = {}, {}, {}
    late_slots = None
    for n, (recvs, transposed) in big.items():
        lay = (lambda a: jnp.swapaxes(a, 1, 2)) if transposed else (lambda a: a)
        spec = _finish_weight(recvs, lay(ws[n]), lay(ms[n]), lay(vs[n]), "finish_" + n)
        if late_slots is None:
            outs, late_slots = _call(spec, _BroadcastCarry([d_nmix0, d_relb]))
        else:
            outs, _ = _call(spec)
        grads[n], delta[n], new_m[n], new_v[n] = [lay(o) for o in outs]
    shaped = lambda n, a: in_full(a) if n == "even_conv_w" else (a.reshape(1, dm) if n == "final_norm" else a)
    pick = lambda dct: [shaped(n, dct[n]) for n in small_names]
    lates = [(small_names.index("norm_mix"), late_slots[0]), (small_names.index("rel_bias"), late_slots[1])]
    results, loss11 = _adamw_small(pick(ws), pick(ms), pick(vs), small_slots[:-1], lates, small_slots[-1],
                                   "adamw_small")
    mine = lambda a: lax.dynamic_slice(a, (0, me * (bw // NDEV)), (3, bw // NDEV))[None]
    for n, res in zip(small_names, results):
        for dst, a in zip((grads, delta, new_m, new_v), res):
            dst[n] = mine(a) if n == "even_conv_w" else (a.reshape(dm) if n == "final_norm" else a)
    loss = loss11[0, 0]
    return (loss, dx0[None], *[grads[n] for n in order], *[delta[n] for n in order],
            *[new_m[n] for n in order], *[new_v[n] for n in order])
```
